```python
import jax, jax.numpy as jnp
from jax import lax
import numpy as np

D_MODEL = 1024
BATCH = 16
SEQ = 2048
DEPTH = 2

CHUNK = 64
Q_BLOCK = 128
NORM_EPS = 1e-6
MASK_VALUE = -1e30
TINY = 1e-30

MIX_WIDTH = D_MODEL
HGRN_HEADS = 4
HGRN_WIDTH = MIX_WIDTH // 4
HGRN_KEY_DIM = HGRN_WIDTH // HGRN_HEADS
HGRN_VAL_DIM = HGRN_WIDTH // HGRN_HEADS
POOL_WINDOWS = (2, 4, 8, 16)
POOL_GROUPS = len(POOL_WINDOWS)
POOL_WIDTH = MIX_WIDTH // 4
POOL_GROUP_DIM = POOL_WIDTH // POOL_GROUPS
FOX_HEADS = 8
FOX_WIDTH = MIX_WIDTH // 2
FOX_HEAD_DIM = FOX_WIDTH // FOX_HEADS

IN_WIDTHS = (HGRN_WIDTH, HGRN_WIDTH, HGRN_WIDTH, HGRN_WIDTH,
             POOL_WIDTH, POOL_WIDTH,
             FOX_WIDTH, FOX_WIDTH, FOX_WIDTH, FOX_WIDTH, FOX_HEADS)
IN_WIDTH = int(sum(IN_WIDTHS))
IN_SPLIT_POINTS = tuple(int(v) for v in np.cumsum(IN_WIDTHS)[:-1])

kernel_name = "hybrid_hgrn2_pool_fox_stream_encoder"


def _rmsnorm(x, g):
    xf = x.astype(jnp.float32)
    y = xf * lax.rsqrt(jnp.mean(xf * xf, axis=-1, keepdims=True) + NORM_EPS)
    return (y * g.astype(jnp.float32)).astype(x.dtype)


def _hgrn2_chunkwise(q, k, v, log_f):
    B, T, H, dk = q.shape
    dv = v.shape[-1]
    n = T // CHUNK

    def to_chunks(a):
        return a.astype(jnp.float32).reshape(B, n, CHUNK, H, a.shape[-1]).transpose(1, 0, 3, 2, 4)

    qc, kc, vc, gc = to_chunks(q), to_chunks(k), to_chunks(v), to_chunks(log_f)
    pos = jnp.arange(CHUNK)
    causal = (pos[:, None] >= pos[None, :])[:, :, None]
    causal_f = causal.astype(jnp.float32)

    def step(S, inp):
        qi, ki, vi, gi = inp
        b = jnp.cumsum(gi, axis=2)
        diff = b[:, :, :, None, :] - b[:, :, None, :, :]
        decay = jnp.exp(jnp.where(causal, diff, 0.0)) * causal_f
        attn = jnp.einsum('bhtd,bhsd,bhtsd->bhts', qi, ki, decay)
        o = (jnp.einsum('bhts,bhsv->bhtv', attn, vi)
             + jnp.einsum('bhtd,bhdv->bhtv', qi * jnp.exp(b), S))
        b_last = b[:, :, -1, :]
        S = (jnp.exp(b_last)[..., None] * S
             + jnp.einsum('bhsd,bhsv->bhdv', ki * jnp.exp(b_last[:, :, None, :] - b), vi))
        return S, o

    S0 = jnp.zeros((B, H, dk, dv), jnp.float32)
    _, o = lax.scan(step, S0, (qc, kc, vc, gc))
    return o.transpose(1, 0, 3, 2, 4).reshape(B, T, H, dv)


def _multiscale_pool(u):
    B, T, _ = u.shape
    uf = u.astype(jnp.float32).reshape(B, T, POOL_GROUPS, POOL_GROUP_DIM)
    cs = jnp.pad(jnp.cumsum(uf, axis=1), ((0, 0), (1, 0), (0, 0), (0, 0)))
    t = jnp.arange(T, dtype=jnp.float32)
    outs = []
    for g, w in enumerate(POOL_WINDOWS):
        c = cs[:, :, g]
        upper = c[:, 1:]
        lower = jnp.pad(c[:, :T + 1 - w], ((0, 0), (w - 1, 0), (0, 0)))
        count = jnp.minimum(t + 1.0, float(w))[None, :, None]
        outs.append((upper - lower) / count - uf[:, :, g])
    return jnp.stack(outs, axis=2)


def _forgetting_attention(q, k, v, log_f):
    B, T, H, D = q.shape
    c = jnp.cumsum(log_f, axis=1).transpose(0, 2, 1)
    scale = D ** -0.5
    outs = []
    for i in range(T // Q_BLOCK):
        q0, q1 = i * Q_BLOCK, (i + 1) * Q_BLOCK
        s = jnp.einsum('bqhd,bkhd->bhqk', q[:, q0:q1], k[:, :q1]).astype(jnp.float32) * scale
        mask = (q0 + jnp.arange(Q_BLOCK))[:, None] >= jnp.arange(q1)[None, :]
        bias = jnp.where(mask, c[:, :, q0:q1, None] - c[:, :, None, :q1], 0.0)
        p = jax.nn.softmax(jnp.where(mask, s + bias, MASK_VALUE), axis=-1)
        outs.append(jnp.einsum('bhqk,bkhd->bqhd', p.astype(v.dtype), v[:, :q1]))
    return jnp.concatenate(outs, axis=1)


def _fwd_setup_inputs(seed: int = 0) -> dict:
    key = jax.random.key(seed)
    ks = jax.random.split(key, 11)
    f32 = jnp.float32
    x = jax.random.normal(ks[0], (BATCH, SEQ, D_MODEL), f32)
    lower_bounds = jax.random.normal(ks[1], (DEPTH, HGRN_WIDTH), f32)
    pre_norm_g = 1.0 + 0.05 * jax.random.normal(ks[2], (DEPTH, D_MODEL), f32)
    w_in = jax.random.normal(ks[3], (DEPTH, D_MODEL, IN_WIDTH), f32) * D_MODEL ** -0.5
    hgrn_norm_g = 1.0 + 0.05 * jax.random.normal(ks[4], (DEPTH, HGRN_WIDTH), f32)
    fox_f_bias = jax.random.uniform(ks[5], (DEPTH, FOX_HEADS), f32, minval=1.0, maxval=4.0)
    pool_w = jax.random.normal(ks[6], (DEPTH, POOL_GROUPS, POOL_GROUP_DIM, POOL_GROUP_DIM), f32) * POOL_GROUP_DIM ** -0.5
    pool_scale = jax.random.uniform(ks[7], (DEPTH, POOL_WIDTH), f32, minval=0.5, maxval=1.5)
    w_out = jax.random.normal(ks[8], (DEPTH, MIX_WIDTH, D_MODEL), f32) * MIX_WIDTH ** -0.5
    post_norm_g = 1.0 + 0.05 * jax.random.normal(ks[9], (DEPTH, D_MODEL), f32)
    return {"x": x, "lower_bounds": lower_bounds, "pre_norm_g": pre_norm_g, "w_in": w_in,
            "hgrn_norm_g": hgrn_norm_g, "fox_f_bias": fox_f_bias, "pool_w": pool_w,
            "pool_scale": pool_scale, "w_out": w_out, "post_norm_g": post_norm_g}


def _fwd_reference(x, lower_bounds, pre_norm_g, w_in, hgrn_norm_g, fox_f_bias, pool_w, pool_scale, w_out, post_norm_g):
    B, T, _ = x.shape
    p = jax.nn.softmax(lower_bounds.astype(jnp.float32), axis=0)
    lbs = jnp.cumsum(p, axis=0) - p[0]
    for l in range(DEPTH):
        h = _rmsnorm(x, pre_norm_g[l])
        proj = jnp.einsum('btd,de->bte', h, w_in[l])
        q_a, f_a, i_a, g_a, u_b, g_b, q_c, k_c, v_c, g_c, f_c = jnp.split(proj, IN_SPLIT_POINTS, axis=-1)

        lb = lbs[l]
        z = f_a.astype(jnp.float32)
        f_gate = lb + (1.0 - lb) * jax.nn.sigmoid(z)
        log_f_a = jnp.log(jnp.maximum(f_gate, TINY))
        k_a = (1.0 - lb) * jax.nn.sigmoid(-z)
        hshape = (B, T, HGRN_HEADS, HGRN_KEY_DIM)
        o_a = _hgrn2_chunkwise(jax.nn.silu(q_a).reshape(hshape), k_a.reshape(hshape),
                               i_a.reshape(B, T, HGRN_HEADS, HGRN_VAL_DIM), log_f_a.reshape(hshape))
        o_a = _rmsnorm(o_a, hgrn_norm_g[l].reshape(HGRN_HEADS, HGRN_VAL_DIM)).reshape(B, T, HGRN_WIDTH)
        o_a = o_a.astype(x.dtype) * jax.nn.silu(g_a)

        pooled = _multiscale_pool(u_b)
        o_b = jnp.einsum('btgc,gcd->btgd', pooled, pool_w[l].astype(jnp.float32)).reshape(B, T, POOL_WIDTH)
        o_b = (o_b * pool_scale[l].astype(jnp.float32)).astype(x.dtype) * jax.nn.silu(g_b)

        log_f_c = jax.nn.log_sigmoid((f_c + fox_f_bias[l]).astype(jnp.float32))
        fshape = (B, T, FOX_HEADS, FOX_HEAD_DIM)
        o_c = _forgetting_attention(q_c.reshape(fshape), k_c.reshape(fshape), v_c.reshape(fshape), log_f_c)
        o_c = o_c.reshape(B, T, FOX_WIDTH).astype(x.dtype) * jax.nn.silu(g_c)

        mixed = jnp.concatenate([o_a, o_b, o_c], axis=-1)
        y = jnp.einsum('bte,ed->btd', mixed, w_out[l])
        x = x + _rmsnorm(y, post_norm_g[l])
    return x


import jax as _jax
import jax.numpy as _jnp

TWIN_FORMAT = 'train_step'
FWD_PARAMS = ['x', 'lower_bounds', 'pre_norm_g', 'w_in', 'hgrn_norm_g', 'fox_f_bias', 'pool_w', 'pool_scale', 'w_out', 'post_norm_g']
TWIN_WEIGHTS = ['lower_bounds', 'pre_norm_g', 'w_in', 'hgrn_norm_g', 'fox_f_bias', 'pool_w', 'pool_scale', 'w_out', 'post_norm_g']
TWIN_DIFF_INPUT = 'x'
TWIN_INPUTS = ['x', 'lower_bounds', 'pre_norm_g', 'w_in', 'hgrn_norm_g', 'fox_f_bias', 'pool_w', 'pool_scale', 'w_out', 'post_norm_g', 'loss_target', 'm_lower_bounds', 'm_pre_norm_g', 'm_w_in', 'm_hgrn_norm_g', 'm_fox_f_bias', 'm_pool_w', 'm_pool_scale', 'm_w_out', 'm_post_norm_g', 'v_lower_bounds', 'v_pre_norm_g', 'v_w_in', 'v_hgrn_norm_g', 'v_fox_f_bias', 'v_pool_w', 'v_pool_scale', 'v_w_out', 'v_post_norm_g']
TWIN_OUTPUTS = ['loss', 'grad_x', 'grad_lower_bounds', 'grad_pre_norm_g', 'grad_w_in', 'grad_hgrn_norm_g', 'grad_fox_f_bias', 'grad_pool_w', 'grad_pool_scale', 'grad_w_out', 'grad_post_norm_g', 'delta_lower_bounds', 'delta_pre_norm_g', 'delta_w_in', 'delta_hgrn_norm_g', 'delta_fox_f_bias', 'delta_pool_w', 'delta_pool_scale', 'delta_w_out', 'delta_post_norm_g', 'new_m_lower_bounds', 'new_m_pre_norm_g', 'new_m_w_in', 'new_m_hgrn_norm_g', 'new_m_fox_f_bias', 'new_m_pool_w', 'new_m_pool_scale', 'new_m_w_out', 'new_m_post_norm_g', 'new_v_lower_bounds', 'new_v_pre_norm_g', 'new_v_w_in', 'new_v_hgrn_norm_g', 'new_v_fox_f_bias', 'new_v_pool_w', 'new_v_pool_scale', 'new_v_w_out', 'new_v_post_norm_g']
TWIN_LEAF_KINDS = {'loss': 'loss', 'grad_x': 'grad_x', 'grad_lower_bounds': 'grad_w', 'grad_pre_norm_g': 'grad_w', 'grad_w_in': 'grad_w', 'grad_hgrn_norm_g': 'grad_w', 'grad_fox_f_bias': 'grad_w', 'grad_pool_w': 'grad_w', 'grad_pool_scale': 'grad_w', 'grad_w_out': 'grad_w', 'grad_post_norm_g': 'grad_w', 'delta_lower_bounds': 'delta_w', 'delta_pre_norm_g': 'delta_w', 'delta_w_in': 'delta_w', 'delta_hgrn_norm_g': 'delta_w', 'delta_fox_f_bias': 'delta_w', 'delta_pool_w': 'delta_w', 'delta_pool_scale': 'delta_w', 'delta_w_out': 'delta_w', 'delta_post_norm_g': 'delta_w', 'new_m_lower_bounds': 'new_m', 'new_m_pre_norm_g': 'new_m', 'new_m_w_in': 'new_m', 'new_m_hgrn_norm_g': 'new_m', 'new_m_fox_f_bias': 'new_m', 'new_m_pool_w': 'new_m', 'new_m_pool_scale': 'new_m', 'new_m_w_out': 'new_m', 'new_m_post_norm_g': 'new_m', 'new_v_lower_bounds': 'new_v', 'new_v_pre_norm_g': 'new_v', 'new_v_w_in': 'new_v', 'new_v_hgrn_norm_g': 'new_v', 'new_v_fox_f_bias': 'new_v', 'new_v_pool_w': 'new_v', 'new_v_pool_scale': 'new_v', 'new_v_w_out': 'new_v', 'new_v_post_norm_g': 'new_v'}


def _forward(args):
    return _fwd_reference(*[args[k] for k in FWD_PARAMS])


def _output_shape():
    out = _jax.eval_shape(lambda: _forward(_fwd_setup_inputs(0)))
    return out.shape, out.dtype

N_MICROBATCH = 1
ADAM_LR = 0.001
ADAM_B1 = 0.9
ADAM_B2 = 0.999
ADAM_EPS = 1e-08
ADAM_WD = 0.01
ADAM_STEP = 10
PER_EXAMPLE_BATCH_AXIS = {'x': 0, 'loss_target': 0}
SHARED_INPUTS = []
_WEIGHT_DTYPES = {'lower_bounds': _jnp.float32, 'pre_norm_g': _jnp.float32, 'w_in': _jnp.float32, 'hgrn_norm_g': _jnp.float32, 'fox_f_bias': _jnp.float32, 'pool_w': _jnp.float32, 'pool_scale': _jnp.float32, 'w_out': _jnp.float32, 'post_norm_g': _jnp.float32}
MOMENT_SCALE = {'lower_bounds': 5.851648e-02, 'pre_norm_g': 6.189854e-01, 'w_in': 3.307038e-01, 'hgrn_norm_g': 6.931251e-01, 'fox_f_bias': 1.718163e+00, 'pool_w': 5.811076e-01, 'pool_scale': 6.300396e-01, 'w_out': 4.699250e-01, 'post_norm_g': 3.198637e+01}


def _to_microbatches(a, axis):
    t = _jnp.moveaxis(a, axis, 0)
    t = t.reshape((N_MICROBATCH, t.shape[0] // N_MICROBATCH) + t.shape[1:])
    return _jnp.moveaxis(t, 1, axis + 1)


def setup_inputs(seed: int = 0) -> dict:
    inp = _fwd_setup_inputs(seed)
    key = _jax.random.fold_in(_jax.random.key(seed), 7919)
    shape, _ = _output_shape()
    out = dict(inp)
    out["loss_target"] = _jax.random.normal(_jax.random.fold_in(key, 0), shape, _jnp.float32)
    for i, name in enumerate(TWIN_WEIGHTS):
        w = inp[name].astype(_jnp.float32)
        if MOMENT_SCALE is None:
            s = _jnp.sqrt(_jnp.mean(_jnp.square(w)) + 1e-30)
        else:
            s = MOMENT_SCALE[name]
        km, kv = _jax.random.split(_jax.random.fold_in(key, i + 1))
        out[name] = w
        out["m_" + name] = s * _jax.random.normal(km, w.shape, _jnp.float32)
        out["v_" + name] = (s * s) * _jax.random.uniform(kv, w.shape, _jnp.float32, 0.5, 1.5)
    if N_MICROBATCH > 1:
        for name, axis in PER_EXAMPLE_BATCH_AXIS.items():
            out[name] = _to_microbatches(out[name], axis)
    return {'x': out['x'], 'lower_bounds': out['lower_bounds'], 'pre_norm_g': out['pre_norm_g'], 'w_in': out['w_in'], 'hgrn_norm_g': out['hgrn_norm_g'], 'fox_f_bias': out['fox_f_bias'], 'pool_w': out['pool_w'], 'pool_scale': out['pool_scale'], 'w_out': out['w_out'], 'post_norm_g': out['post_norm_g'], 'loss_target': out['loss_target'], 'm_lower_bounds': out['m_lower_bounds'], 'm_pre_norm_g': out['m_pre_norm_g'], 'm_w_in': out['m_w_in'], 'm_hgrn_norm_g': out['m_hgrn_norm_g'], 'm_fox_f_bias': out['m_fox_f_bias'], 'm_pool_w': out['m_pool_w'], 'm_pool_scale': out['m_pool_scale'], 'm_w_out': out['m_w_out'], 'm_post_norm_g': out['m_post_norm_g'], 'v_lower_bounds': out['v_lower_bounds'], 'v_pre_norm_g': out['v_pre_norm_g'], 'v_w_in': out['v_w_in'], 'v_hgrn_norm_g': out['v_hgrn_norm_g'], 'v_fox_f_bias': out['v_fox_f_bias'], 'v_pool_w': out['v_pool_w'], 'v_pool_scale': out['v_pool_scale'], 'v_w_out': out['v_w_out'], 'v_post_norm_g': out['v_post_norm_g']}


def _loss(weights, diff, rest, loss_target):
    with _jax.named_scope("forward"):
        args = {**rest, TWIN_DIFF_INPUT: diff, **{k: w.astype(_WEIGHT_DTYPES[k]) for k, w in weights.items()}}
        y = _forward(args)
    with _jax.named_scope("loss_head"):
        err = _jnp.square(y.astype(_jnp.float32) - loss_target)
        return 0.5 * _jnp.sum(_jnp.mean(err, axis=-1)) if err.ndim else 0.5 * err


def _adamw(w, g, m, v):
    m = ADAM_B1 * m + (1.0 - ADAM_B1) * g
    v = ADAM_B2 * v + (1.0 - ADAM_B2) * _jnp.square(g)
    m_hat = m / (1.0 - ADAM_B1 ** ADAM_STEP)
    v_hat = v / (1.0 - ADAM_B2 ** ADAM_STEP)
    delta = -ADAM_LR * (m_hat / (_jnp.sqrt(v_hat) + ADAM_EPS) + ADAM_WD * w)
    return delta, m, v


def reference(x, lower_bounds, pre_norm_g, w_in, hgrn_norm_g, fox_f_bias, pool_w, pool_scale, w_out, post_norm_g, loss_target, m_lower_bounds, m_pre_norm_g, m_w_in, m_hgrn_norm_g, m_fox_f_bias, m_pool_w, m_pool_scale, m_w_out, m_post_norm_g, v_lower_bounds, v_pre_norm_g, v_w_in, v_hgrn_norm_g, v_fox_f_bias, v_pool_w, v_pool_scale, v_w_out, v_post_norm_g):
    given = dict(x=x, lower_bounds=lower_bounds, pre_norm_g=pre_norm_g, w_in=w_in, hgrn_norm_g=hgrn_norm_g, fox_f_bias=fox_f_bias, pool_w=pool_w, pool_scale=pool_scale, w_out=w_out, post_norm_g=post_norm_g, loss_target=loss_target, m_lower_bounds=m_lower_bounds, m_pre_norm_g=m_pre_norm_g, m_w_in=m_w_in, m_hgrn_norm_g=m_hgrn_norm_g, m_fox_f_bias=m_fox_f_bias, m_pool_w=m_pool_w, m_pool_scale=m_pool_scale, m_w_out=m_w_out, m_post_norm_g=m_post_norm_g, v_lower_bounds=v_lower_bounds, v_pre_norm_g=v_pre_norm_g, v_w_in=v_w_in, v_hgrn_norm_g=v_hgrn_norm_g, v_fox_f_bias=v_fox_f_bias, v_pool_w=v_pool_w, v_pool_scale=v_pool_scale, v_w_out=v_w_out, v_post_norm_g=v_post_norm_g)
    weights = {n: given[n] for n in TWIN_WEIGHTS}
    shared = {n: given[n] for n in SHARED_INPUTS}
    per_example = {n: given[n] for n in ['x']}
    grad_fn = _jax.value_and_grad(_loss, argnums=(0, 1))

    def one_microbatch(ex, loss_target):
        ex = dict(ex)
        diff = ex.pop(TWIN_DIFF_INPUT)
        return grad_fn(weights, diff, {**shared, **ex}, loss_target)

    if N_MICROBATCH == 1:
        loss, (grad_w, grad_x) = one_microbatch(per_example, given["loss_target"])
    else:
        def body(carry, xs):
            loss_sum, grad_sum = carry
            l_k, (gw_k, gx_k) = one_microbatch(xs[0], xs[1])
            with _jax.named_scope("update"):
                return (loss_sum + l_k, _jax.tree.map(_jnp.add, grad_sum, gw_k)), gx_k

        init = (_jnp.zeros((), _jnp.float32), _jax.tree.map(_jnp.zeros_like, weights))
        (loss, grad_w), grad_x = _jax.lax.scan(body, init, (per_example, given["loss_target"]))
    with _jax.named_scope("update"):
        delta_w, new_m, new_v = {}, {}, {}
        for n in TWIN_WEIGHTS:
            delta_w[n], new_m[n], new_v[n] = _adamw(weights[n], grad_w[n], given["m_" + n], given["v_" + n])
    return (loss, grad_x, *[grad_w[n] for n in TWIN_WEIGHTS], *[delta_w[n] for n in TWIN_WEIGHTS],
            *[new_m[n] for n in TWIN_WEIGHTS], *[new_v[n] for n in TWIN_WEIGHTS])
```

```python
import functools

import numpy as np
import jax
import jax.numpy as jnp
from jax import lax
from jax.experimental import pallas as pl
from jax.experimental.pallas import tpu as pltpu

F32 = jnp.float32
BF16 = jnp.bfloat16
HI = lax.Precision.HIGHEST
MESH = pl.DeviceIdType.MESH

NORM_EPS = 1e-6
MASK_VALUE = -1e30
TINY = 1e-30
ADAM_LR, ADAM_B1, ADAM_B2, ADAM_EPS, ADAM_WD, ADAM_STEP = 0.001, 0.9, 0.999, 1e-08, 0.01, 10

D_MODEL = 1024
DEPTH = 2
N_CHIPS = 4
CHUNK = 64
LANES = 128
HGRN_W, POOL_W, FOX_W, FOX_HEADS = 256, 256, 512, 8
POOL_WINDOWS = (2, 4, 8, 16)
POOL_HALO = 16
IN_WIDTH = 3592
SHARD_W = IN_WIDTH // N_CHIPS
A_W, B_W, C_W, F_W = 1024, 512, 2048, 128
E_INT = A_W + B_W + C_W + F_W
B_BLK = A_W // 512
C_BLK0 = (A_W + B_W) // 512
F_BLK = (A_W + B_W + C_W) // 128


def _segments():
    segs = []
    for hp in range(2):
        for part in range(4):
            segs.append((part * 256 + hp * 128, 128))
    segs.append((1024, 256))
    segs.append((1280, 256))
    for hp in range(4):
        for part in range(4):
            segs.append((1536 + part * 512 + hp * 128, 128))
    segs.append((3584, 8))
    return segs


_SEGS = _segments()


def _to_internal(w):
    parts = [w[..., s:s + n] for s, n in _SEGS]
    parts.append(jnp.zeros(w.shape[:-1] + (E_INT - IN_WIDTH,), w.dtype))
    return jnp.concatenate(parts, axis=-1)


def _to_original(w):
    offs, o = [], 0
    for s, n in _SEGS:
        offs.append((s, o, n))
        o += n
    parts = [w[..., o:o + n] for s, o, n in sorted(offs)]
    return jnp.concatenate(parts, axis=-1)


def _cparams(sem=None, vmem_mb=48):
    kw = dict(vmem_limit_bytes=vmem_mb * 1024 * 1024)
    if sem is not None:
        kw["dimension_semantics"] = sem
    return pltpu.CompilerParams(**kw)


def _sig(x):
    return 1.0 / (1.0 + jnp.exp(-x))


def _silu(x):
    return x * _sig(x)


def _dsilu(x):
    s = _sig(x)
    return s * (1.0 + x * (1.0 - s))


def _rstd(x):
    return lax.rsqrt(jnp.mean(x * x, axis=-1, keepdims=True) + NORM_EPS)


def _dot(a, b, dims, **kw):
    return lax.dot_general(a, b, (dims, ((), ())), preferred_element_type=F32, **kw)


NN = ((1,), (0,))
NT = ((1,), (1,))
TN = ((0,), (0,))


def _iota(shape, dim):
    return lax.broadcasted_iota(jnp.int32, shape, dim)


def _lbs_fwd(lower_bounds):
    def body(a_ref, o_ref):
        a = a_ref[...]
        a0, a1 = a[0:1], a[1:2]
        m = jnp.maximum(a0, a1)
        e0, e1 = jnp.exp(a0 - m), jnp.exp(a1 - m)
        p0, p1 = e0 / (e0 + e1), e1 / (e0 + e1)
        o_ref[...] = jnp.concatenate([p0 - p0, (p0 + p1) - p0], axis=0)

    return pl.pallas_call(body, out_shape=jax.ShapeDtypeStruct(lower_bounds.shape, F32), name="lbs_fwd")(lower_bounds)


def _inproj_fwd(x2, g_row, w_int, name):
    n, d = x2.shape
    e = w_int.shape[1]
    tm = min(256, n)

    def body(x_ref, g_ref, w_ref, o_ref):
        x = x_ref[...]
        h = (x * _rstd(x) * g_ref[...]).astype(BF16)
        o_ref[...] = jnp.dot(h, w_ref[...], preferred_element_type=F32)

    return pl.pallas_call(
        body, grid=(n // tm,),
        in_specs=[pl.BlockSpec((tm, d), lambda i: (i, 0)), pl.BlockSpec((1, d), lambda i: (0, 0)),
                  pl.BlockSpec((d, e), lambda i: (0, 0))],
        out_specs=pl.BlockSpec((tm, e), lambda i: (i, 0)),
        out_shape=jax.ShapeDtypeStruct((n, e), F32),
        compiler_params=_cparams(("parallel",)), name=name)(x2, g_row, w_int)


def _chunk_cumsum_matrix():
    i, j = _iota((LANES, LANES), 0), _iota((LANES, LANES), 1)
    return ((i <= j) & ((i // CHUNK) == (j // CHUNK))).astype(F32)


def _hgrn_gates(a, lb):
    qa, z = a[:, 0:128], a[:, 128:256]
    sg, sgn = _sig(z), _sig(-z)
    fg = lb + (1.0 - lb) * sg
    lf = jnp.log(jnp.maximum(fg, TINY))
    kk = (1.0 - lb) * sgn
    return qa * _sig(qa), kk, lf, sg, sgn, fg


def _hgrn_fwd(proj3, lbs_row, gn_col, name):
    bsz, t, _ = proj3.shape
    nt = t // LANES

    def body(a_ref, lb_ref, gn_ref, og_ref, or_ref):
        lb = lb_ref[...]
        gn = gn_ref[...]
        umat = _chunk_cumsum_matrix()
        lane64 = _iota((1, LANES), 1) % CHUNK

        def tile(i, carry):
            r0 = pl.multiple_of(i * LANES, LANES)
            a = a_ref[pl.ds(r0, LANES), :]
            qq, kk, lf, _, _, _ = _hgrn_gates(a, lb)
            va, ga = a[:, 256:384], a[:, 384:512]
            q_t, k_t, v_t = qq.T, kk.T, va.T
            b_t = jnp.dot(lf.T, umat, precision=HI, preferred_element_type=F32)
            new_s, o_heads = [], []
            for h in range(2):
                s_h = carry[h]
                rs = slice(CHUNK * h, CHUNK * (h + 1))
                qh, kh, vh, bh = q_t[rs], k_t[rs], v_t[rs], b_t[rs]
                inter = []
                for c in range(2):
                    cs = slice(CHUNK * c, CHUNK * (c + 1))
                    b_ = bh[:, cs]
                    qt = (qh[:, cs] * jnp.exp(b_)).astype(BF16)
                    inter.append(_dot(s_h.astype(BF16), qt, TN))
                    bl = b_[:, CHUNK - 1:CHUNK]
                    kt = (kh[:, cs] * jnp.exp(bl - b_)).astype(BF16)
                    s_h = jnp.exp(bl) * s_h + _dot(kt, vh[:, cs].astype(BF16), NT)
                new_s.append(s_h)

                def offset(dlt, acc, qh=qh, kh=kh, vh=vh, bh=bh):
                    kr, br, vr = pltpu.roll(kh, dlt, 1), pltpu.roll(bh, dlt, 1), pltpu.roll(vh, dlt, 1)
                    e = jnp.exp(jnp.minimum(bh - br, 0.0))
                    att = jnp.sum(qh * kr * e, axis=0, keepdims=True)
                    att = jnp.where(lane64 >= dlt, att, 0.0)
                    return acc + att * vr

                o_heads.append(lax.fori_loop(0, CHUNK, offset, jnp.concatenate(inter, axis=1)))
            normed = []
            for h in range(2):
                o_h = o_heads[h]
                ms = jnp.mean(o_h * o_h, axis=0, keepdims=True)
                normed.append(o_h * lax.rsqrt(ms + NORM_EPS) * gn[CHUNK * h:CHUNK * (h + 1)])
            or_ref[pl.ds(r0, LANES), :] = jnp.concatenate(o_heads, axis=0).T
            og_ref[pl.ds(r0, LANES), :] = jnp.concatenate(normed, axis=0).T * _silu(ga)
            return tuple(new_s)

        zero = jnp.zeros((CHUNK, CHUNK), F32)
        lax.fori_loop(0, nt, tile, (zero, zero))

    out = jax.ShapeDtypeStruct((bsz, t, HGRN_W), F32)
    return pl.pallas_call(
        body, grid=(bsz, 2),
        in_specs=[pl.BlockSpec((None, t, 512), lambda b, p: (b, 0, p)),
                  pl.BlockSpec((1, 128), lambda b, p: (0, p)),
                  pl.BlockSpec((128, 1), lambda b, p: (p, 0))],
        out_specs=[pl.BlockSpec((None, t, 128), lambda b, p: (b, 0, p)),
                   pl.BlockSpec((None, t, 128), lambda b, p: (b, 0, p))],
        out_shape=[out, out],
        compiler_params=_cparams(("parallel", "parallel")), name=name)(proj3, lbs_row, gn_col)


def _hgrn_bwd(proj3, o_raw, dmixed, lbs_row, gn_row, name):
    bsz, t, _ = proj3.shape
    nt = t // LANES
    nchunk = t // CHUNK

    def body(a_ref, or_ref, do_ref, lb_ref, gn_ref, da_ref, dgn_ref, dlb_ref, s_sc):
        lb = lb_ref[...]
        gn = gn_ref[...]
        umat = _chunk_cumsum_matrix()
        lane = _iota((1, LANES), 1)
        lane64 = lane % CHUNK
        half = lane < CHUNK

        def t_layout(a):
            qq, kk, lf, sg, sgn, fg = _hgrn_gates(a, lb)
            b_t = jnp.dot(lf.T, umat, precision=HI, preferred_element_type=F32)
            return qq.T, kk.T, a[:, 256:384].T, b_t, (sg, sgn, fg)

        def fwd_tile(i, carry):
            r0 = pl.multiple_of(i * LANES, LANES)
            q_t, k_t, v_t, b_t, _ = t_layout(a_ref[pl.ds(r0, LANES), :])
            new_s = []
            for h in range(2):
                s_h = carry[h]
                rs = slice(CHUNK * h, CHUNK * (h + 1))
                for c in range(2):
                    cs = slice(CHUNK * c, CHUNK * (c + 1))
                    s_sc[h, 2 * i + c] = s_h
                    b_ = b_t[rs, cs]
                    bl = b_[:, CHUNK - 1:CHUNK]
                    kt = (k_t[rs, cs] * jnp.exp(bl - b_)).astype(BF16)
                    s_h = jnp.exp(bl) * s_h + _dot(kt, v_t[rs, cs].astype(BF16), NT)
                new_s.append(s_h)
            return tuple(new_s)

        zero = jnp.zeros((CHUNK, CHUNK), F32)
        lax.fori_loop(0, nt, fwd_tile, (zero, zero))

        def half_mean(v):
            m0 = jnp.sum(jnp.where(half, v, 0.0), axis=1, keepdims=True) * (1.0 / CHUNK)
            m1 = jnp.sum(jnp.where(half, 0.0, v), axis=1, keepdims=True) * (1.0 / CHUNK)
            return jnp.where(half, m0, m1)

        def bwd_tile(k, carry):
            ds0, ds1, dgn_acc, dlb_acc = carry
            i = nt - 1 - k
            r0 = pl.multiple_of(i * LANES, LANES)
            a = a_ref[pl.ds(r0, LANES), :]
            qa, z, ga = a[:, 0:128], a[:, 128:256], a[:, 384:512]
            q_t, k_t, v_t, b_t, (sg, sgn, fg) = t_layout(a)
            oraw = or_ref[pl.ds(r0, LANES), :]
            dout = do_ref[pl.ds(r0, LANES), :]
            r = lax.rsqrt(half_mean(oraw * oraw) + NORM_EPS)
            xn = oraw * r
            dga = dout * (xn * gn) * _dsilu(ga)
            don = dout * _silu(ga)
            dgn_acc = dgn_acc + jnp.sum(don * xn, axis=0, keepdims=True)
            dxn = don * gn
            do_t = (r * (dxn - xn * half_mean(dxn * xn))).T
            new_ds, dq_h, dk_h, dv_h, db_h = [], [], [], [], []
            for h in range(2):
                ds_h = (ds0, ds1)[h]
                rs = slice(CHUNK * h, CHUNK * (h + 1))
                qh, kh, vh, bh, doh = q_t[rs], k_t[rs], v_t[rs], b_t[rs], do_t[rs]
                dq_c, dk_c, dv_c, dbl_c = [None, None], [None, None], [None, None], [None, None]
                for c in (1, 0):
                    cs = slice(CHUNK * c, CHUNK * (c + 1))
                    s_n = s_sc[h, 2 * i + c]
                    b_ = bh[:, cs]
                    eb = jnp.exp(b_)
                    bl = b_[:, CHUNK - 1:CHUNK]
                    ek = jnp.exp(bl - b_)
                    ebl = jnp.exp(bl)
                    qt, kt = qh[:, cs] * eb, kh[:, cs] * ek
                    do_c = doh[:, cs].astype(BF16)
                    dsb = ds_h.astype(BF16)
                    dv_c[c] = _dot(dsb, kt.astype(BF16), TN)
                    dkt = _dot(dsb, vh[:, cs].astype(BF16), NN)
                    dqt = _dot(s_n.astype(BF16), do_c, NN)
                    dbl_c[c] = jnp.sum(ds_h * s_n, axis=1, keepdims=True) * ebl + jnp.sum(dkt * kt, axis=1, keepdims=True)
                    dq_c[c], dk_c[c] = dqt * eb, dkt * ek
                    ds_h = ebl * ds_h + _dot(qt.astype(BF16), do_c, NT)
                new_ds.append(ds_h)

                def offset(dlt, acc, qh=qh, kh=kh, vh=vh, bh=bh, doh=doh):
                    dq_i, dk_i, dv_i = acc
                    kr, br, vr = pltpu.roll(kh, dlt, 1), pltpu.roll(bh, dlt, 1), pltpu.roll(vh, dlt, 1)
                    e = jnp.where(lane64 >= dlt, jnp.exp(jnp.minimum(bh - br, 0.0)), 0.0)
                    qe = qh * e
                    att = jnp.sum(qe * kr, axis=0, keepdims=True)
                    datt = jnp.sum(doh * vr, axis=0, keepdims=True)
                    back = lax.rem(LANES - dlt, LANES)
                    return (dq_i + datt * (kr * e),
                            dk_i + pltpu.roll(datt * qe, back, 1),
                            dv_i + pltpu.roll(att * doh, back, 1))

                init = (jnp.concatenate(dq_c, axis=1), jnp.concatenate(dk_c, axis=1), jnp.concatenate(dv_c, axis=1))
                dqh, dkh, dvh = lax.fori_loop(0, CHUNK, offset, init)
                dbl = jnp.where(half, dbl_c[0], dbl_c[1])
                db_h.append(qh * dqh - kh * dkh + jnp.where(lane64 == CHUNK - 1, dbl, 0.0))
                dq_h.append(dqh)
                dk_h.append(dkh)
                dv_h.append(dvh)
            dqq = jnp.concatenate(dq_h, axis=0).T
            dkk = jnp.concatenate(dk_h, axis=0).T
            dvv = jnp.concatenate(dv_h, axis=0).T
            dlf = _dot(jnp.concatenate(db_h, axis=0), umat, NT, precision=HI).T
            dqa = dqq * _dsilu(qa)
            dfg = jnp.where(fg > TINY, dlf / fg, 0.0)
            dz = (dfg - dkk) * (1.0 - lb) * sg * sgn
            dlb_acc = dlb_acc + jnp.sum(dfg * (1.0 - sg) - dkk * sgn, axis=0, keepdims=True)
            da_ref[pl.ds(r0, LANES), :] = jnp.concatenate([dqa, dz, dvv, dga], axis=1)
            return new_ds[0], new_ds[1], dgn_acc, dlb_acc

        zrow = jnp.zeros((1, LANES), F32)
        _, _, dgn_acc, dlb_acc = lax.fori_loop(0, nt, bwd_tile, (zero, zero, zrow, zrow))
        dgn_ref[...] = jnp.broadcast_to(dgn_acc, (8, LANES))
        dlb_ref[...] = jnp.broadcast_to(dlb_acc, (8, LANES))

    rows = jax.ShapeDtypeStruct((bsz, 8, HGRN_W), F32)
    return pl.pallas_call(
        body, grid=(bsz, 2),
        in_specs=[pl.BlockSpec((None, t, 512), lambda b, p: (b, 0, p)),
                  pl.BlockSpec((None, t, 128), lambda b, p: (b, 0, p)),
                  pl.BlockSpec((None, t, 128), lambda b, p: (b, 0, p)),
                  pl.BlockSpec((1, 128), lambda b, p: (0, p)),
                  pl.BlockSpec((1, 128), lambda b, p: (0, p))],
        out_specs=[pl.BlockSpec((None, t, 512), lambda b, p: (b, 0, p)),
                   pl.BlockSpec((None, 8, 128), lambda b, p: (b, 0, p)),
                   pl.BlockSpec((None, 8, 128), lambda b, p: (b, 0, p))],
        out_shape=[jax.ShapeDtypeStruct((bsz, t, A_W), F32), rows, rows],
        scratch_shapes=[pltpu.VMEM((2, nchunk, CHUNK, CHUNK), F32)],
        compiler_params=_cparams(("parallel", "parallel")), name=name)(proj3, o_raw, dmixed, lbs_row, gn_row)


def _pool_tt(t):
    return min(256, t)


def _window_select(s2, s4, s8, s16, lane):
    return jnp.where(lane < 64, s2, jnp.where(lane < 128, s4, jnp.where(lane < 192, s8, s16)))


def _pool_counts(t0, tt):
    lane = _iota((tt, POOL_W), 1)
    tpos = (_iota((tt, POOL_W), 0) + t0 + 1).astype(F32)
    win = jnp.where(lane < 64, 2.0, jnp.where(lane < 128, 4.0, jnp.where(lane < 192, 8.0, 16.0)))
    return 1.0 / jnp.minimum(tpos, win), lane


def _pooled_tile(upad_ref, i, tt):
    r0 = pl.multiple_of(i * tt, 8)
    cat = upad_ref[pl.ds(r0, tt + POOL_HALO), :]
    s2 = cat + pltpu.roll(cat, 1, 0)
    s4 = s2 + pltpu.roll(s2, 2, 0)
    s8 = s4 + pltpu.roll(s4, 4, 0)
    s16 = s8 + pltpu.roll(s8, 8, 0)
    inv, lane = _pool_counts(i * tt, tt)
    sel = _window_select(s2[POOL_HALO:], s4[POOL_HALO:], s8[POOL_HALO:], s16[POOL_HALO:], lane)
    return sel * inv - cat[POOL_HALO:], inv, lane


def _pool_fwd(proj3, wbd, scale_row, name):
    bsz, t, _ = proj3.shape
    tt = _pool_tt(t)

    def body(p_ref, w_ref, sc_ref, o_ref, upad):
        upad[0:POOL_HALO, :] = jnp.zeros((POOL_HALO, POOL_W), F32)
        upad[POOL_HALO:, :] = p_ref[:, 0:POOL_W]
        w = w_ref[...]
        sc = sc_ref[...]

        def tile(i, c):
            pooled, _, _ = _pooled_tile(upad, i, tt)
            r0 = pl.multiple_of(i * tt, 8)
            g = p_ref[pl.ds(r0, tt), POOL_W:2 * POOL_W]
            pre = jnp.dot(pooled.astype(BF16), w, preferred_element_type=F32)
            o_ref[pl.ds(r0, tt), :] = pre * sc * _silu(g)
            return c

        lax.fori_loop(0, t // tt, tile, 0)

    return pl.pallas_call(
        body, grid=(bsz,),
        in_specs=[pl.BlockSpec((None, t, 512), lambda b: (b, 0, B_BLK)),
                  pl.BlockSpec((POOL_W, POOL_W), lambda b: (0, 0)),
                  pl.BlockSpec((1, POOL_W), lambda b: (0, 0))],
        out_specs=pl.BlockSpec((None, t, POOL_W), lambda b: (b, 0, 0)),
        out_shape=jax.ShapeDtypeStruct((bsz, t, POOL_W), F32),
        scratch_shapes=[pltpu.VMEM((t + POOL_HALO, POOL_W), F32)],
        compiler_params=_cparams(("parallel",)), name=name)(proj3, wbd, scale_row)


def _pool_bwd(proj3, dmixed, wbd, scale_row, name):
    bsz, t, _ = proj3.shape
    tt = _pool_tt(t)

    def body(p_ref, do_ref, w_ref, sc_ref, db_ref, dsc_ref, dw_ref, upad, epad):
        upad[0:POOL_HALO, :] = jnp.zeros((POOL_HALO, POOL_W), F32)
        upad[POOL_HALO:, :] = p_ref[:, 0:POOL_W]
        epad[t:, :] = jnp.zeros((POOL_HALO, POOL_W), F32)
        w = w_ref[...]
        sc = sc_ref[...]

        def tile(i, carry):
            dsc_acc, dw_acc = carry
            pooled, inv, _ = _pooled_tile(upad, i, tt)
            r0 = pl.multiple_of(i * tt, 8)
            g = p_ref[pl.ds(r0, tt), POOL_W:2 * POOL_W]
            dout = do_ref[pl.ds(r0, tt), :]
            pb = pooled.astype(BF16)
            pre = jnp.dot(pb, w, preferred_element_type=F32)
            t1 = dout * _silu(g)
            dsc_acc = dsc_acc + jnp.sum(t1 * pre, axis=0, keepdims=True)
            dpre = (t1 * sc).astype(BF16)
            db_ref[pl.ds(r0, tt), POOL_W:2 * POOL_W] = dout * pre * sc * _dsilu(g)
            dw_acc = dw_acc + _dot(pb, dpre, TN)
            dpooled = _dot(dpre, w, NT)
            epad[pl.ds(r0, tt), :] = dpooled * inv
            return dsc_acc, dw_acc

        dsc_acc, dw_acc = lax.fori_loop(0, t // tt, tile, (jnp.zeros((1, POOL_W), F32), jnp.zeros((POOL_W, POOL_W), F32)))
        dsc_ref[...] = jnp.broadcast_to(dsc_acc, (8, POOL_W))
        dw_ref[...] = dw_acc

        def tile2(i, c):
            r0 = pl.multiple_of(i * tt, 8)
            n = tt + POOL_HALO
            cat = epad[pl.ds(r0, n), :]
            s2 = cat + pltpu.roll(cat, n - 1, 0)
            s4 = s2 + pltpu.roll(s2, n - 2, 0)
            s8 = s4 + pltpu.roll(s4, n - 4, 0)
            s16 = s8 + pltpu.roll(s8, n - 8, 0)
            inv, lane = _pool_counts(i * tt, tt)
            sel = _window_select(s2[:tt], s4[:tt], s8[:tt], s16[:tt], lane)
            db_ref[pl.ds(r0, tt), 0:POOL_W] = sel - cat[:tt] / inv
            return c

        lax.fori_loop(0, t // tt, tile2, 0)

    return pl.pallas_call(
        body, grid=(bsz,),
        in_specs=[pl.BlockSpec((None, t, 512), lambda b: (b, 0, B_BLK)),
                  pl.BlockSpec((None, t, POOL_W), lambda b: (b, 0, 1)),
                  pl.BlockSpec((POOL_W, POOL_W), lambda b: (0, 0)),
                  pl.BlockSpec((1, POOL_W), lambda b: (0, 0))],
        out_specs=[pl.BlockSpec((None, t, 512), lambda b: (b, 0, 0)),
                   pl.BlockSpec((None, 8, POOL_W), lambda b: (b, 0, 0)),
                   pl.BlockSpec((None, POOL_W, POOL_W), lambda b: (b, 0, 0))],
        out_shape=[jax.ShapeDtypeStruct((bsz, t, B_W), F32), jax.ShapeDtypeStruct((bsz, 8, POOL_W), F32),
                   jax.ShapeDtypeStruct((bsz, POOL_W, POOL_W), F32)],
        scratch_shapes=[pltpu.VMEM((t + POOL_HALO, POOL_W), F32), pltpu.VMEM((t + POOL_HALO, POOL_W), F32)],
        compiler_params=_cparams(("parallel",)), name=name)(proj3, dmixed, wbd, scale_row)


def _head_select_rows(hp):
    r, c = _iota((8, LANES), 0), _iota((8, LANES), 1)
    return ((r < 2) & (c == 2 * hp + r)).astype(F32)


def _foxgate_fwd(proj3, bias_row, name):
    bsz, t, _ = proj3.shape
    nt = t // LANES

    def body(f_ref, b_ref, cn_ref, ct_ref):
        bias = b_ref[...]
        i, j = _iota((LANES, LANES), 0), _iota((LANES, LANES), 1)
        lower = (j <= i).astype(F32)
        spread = (_iota((LANES, FOX_W), 0) == _iota((LANES, FOX_W), 1) // 64).astype(F32)

        def tile(k, carry):
            r0 = pl.multiple_of(k * LANES, LANES)
            xg = f_ref[pl.ds(r0, LANES), :] + bias
            lf = jnp.minimum(xg, 0.0) - jnp.log(1.0 + jnp.exp(-jnp.abs(xg)))
            c = jnp.dot(lower, lf, precision=HI, preferred_element_type=F32) + carry
            cn_ref[pl.ds(r0, LANES), :] = jnp.dot(c, spread, precision=HI, preferred_element_type=F32)
            for hp in range(4):
                ct_ref[hp, :, pl.ds(r0, LANES)] = _dot(_head_select_rows(hp), c, NT, precision=HI)
            return c[LANES - 1:LANES, :]

        lax.fori_loop(0, nt, tile, jnp.zeros((1, LANES), F32))

    return pl.pallas_call(
        body, grid=(bsz,),
        in_specs=[pl.BlockSpec((None, t, 128), lambda b: (b, 0, F_BLK)), pl.BlockSpec((1, 128), lambda b: (0, 0))],
        out_specs=[pl.BlockSpec((None, t, FOX_W), lambda b: (b, 0, 0)),
                   pl.BlockSpec((None, 4, 8, t), lambda b: (b, 0, 0, 0))],
        out_shape=[jax.ShapeDtypeStruct((bsz, t, FOX_W), F32), jax.ShapeDtypeStruct((bsz, 4, 8, t), F32)],
        compiler_params=_cparams(("parallel",)), name=name)(proj3, bias_row)


def _foxgate_bwd(proj3, dc_nat, bias_row, name):
    bsz, t, _ = proj3.shape
    nt = t // LANES

    def body(f_ref, dc_ref, b_ref, df_ref, dbias_ref, run_sc):
        bias = b_ref[...]
        i, j = _iota((LANES, LANES), 0), _iota((LANES, LANES), 1)
        upper = (j >= i).astype(F32)
        valid = _iota((1, LANES), 1) < FOX_HEADS
        run_sc[...] = jnp.zeros((8, LANES), F32)
        dbias_ref[...] = jnp.zeros((8, LANES), F32)

        def tile(k, c):
            r0 = pl.multiple_of((nt - 1 - k) * LANES, LANES)
            dc = dc_ref[pl.ds(r0, LANES), :] + jnp.where(i == LANES - 1, run_sc[0:1, :], 0.0)
            dlf = jnp.dot(upper, dc, precision=HI, preferred_element_type=F32)
            xg = f_ref[pl.ds(r0, LANES), :] + bias
            df = jnp.where(valid, dlf * _sig(-xg), 0.0)
            df_ref[pl.ds(r0, LANES), :] = df
            run_sc[...] = dlf[0:8, :]
            dbias_ref[...] += jnp.sum(df, axis=0, keepdims=True)
            return c

        lax.fori_loop(0, nt, tile, 0)

    blk = pl.BlockSpec((None, t, 128), lambda b: (b, 0, 0))
    return pl.pallas_call(
        body, grid=(bsz,),
        in_specs=[pl.BlockSpec((None, t, 128), lambda b: (b, 0, F_BLK)), blk, pl.BlockSpec((1, 128), lambda b: (0, 0))],
        out_specs=[blk, pl.BlockSpec((None, 8, 128), lambda b: (b, 0, 0))],
        out_shape=[jax.ShapeDtypeStruct((bsz, t, F_W), F32), jax.ShapeDtypeStruct((bsz, 8, 128), F32)],
        scratch_shapes=[pltpu.VMEM((8, LANES), F32)],
        compiler_params=_cparams(("parallel",)), name=name)(proj3, dc_nat, bias_row)


def _fox_tile(t):
    return min(256, t)


def _fox_fwd(proj3, c_nat, c_t, name):
    bsz, t, _ = proj3.shape
    tq = _fox_tile(t)
    nq = t // tq

    def body(q_ref, kv_ref, cn_ref, ct_ref, og_ref, or_ref, lse_ref):
        i = pl.program_id(2)
        qblk = q_ref[...]
        rows = _iota((tq, tq), 0) + i * tq
        outs, lses = [], []
        for h in range(2):
            hs = slice(64 * h, 64 * (h + 1))
            qh = (qblk[:, hs] * 0.125).astype(BF16)
            cq = cn_ref[:, 64 * h:64 * h + 1]

            def kv_step(j, carry, h=h, qh=qh, cq=cq):
                m, l, acc = carry
                c0 = pl.multiple_of(j * tq, tq)
                kh = kv_ref[pl.ds(c0, tq), 128 + 64 * h:128 + 64 * (h + 1)].astype(BF16)
                vh = kv_ref[pl.ds(c0, tq), 256 + 64 * h:256 + 64 * (h + 1)].astype(BF16)
                ck = ct_ref[h:h + 1, pl.ds(c0, tq)]
                s = _dot(qh, kh, NT) + (cq - ck)
                s = jnp.where(rows >= _iota((tq, tq), 1) + j * tq, s, MASK_VALUE)
                m_new = jnp.maximum(m, jnp.max(s, axis=1, keepdims=True))
                alpha = jnp.exp(m - m_new)
                p = jnp.exp(s - m_new)
                l = alpha * l + jnp.sum(p, axis=1, keepdims=True)
                acc = alpha * acc + jnp.dot(p.astype(BF16), vh, preferred_element_type=F32)
                return m_new, l, acc

            init = (jnp.full((tq, 1), MASK_VALUE, F32), jnp.zeros((tq, 1), F32), jnp.zeros((tq, 64), F32))
            m, l, acc = lax.fori_loop(0, i + 1, kv_step, init)
            outs.append(acc / l)
            lses.append(jnp.broadcast_to(m + jnp.log(l), (tq, 64)))
        o = jnp.concatenate(outs, axis=1)
        or_ref[...] = o
        og_ref[...] = o * _silu(qblk[:, 384:512])
        lse_ref[...] = jnp.concatenate(lses, axis=1)

    out = jax.ShapeDtypeStruct((bsz, t, FOX_W), F32)
    blk = pl.BlockSpec((None, tq, 128), lambda b, p, i: (b, i, p))
    return pl.pallas_call(
        body, grid=(bsz, 4, nq),
        in_specs=[pl.BlockSpec((None, tq, 512), lambda b, p, i: (b, i, C_BLK0 + p)),
                  pl.BlockSpec((None, t, 512), lambda b, p, i: (b, 0, C_BLK0 + p)),
                  blk,
                  pl.BlockSpec((None, None, 8, t), lambda b, p, i: (b, p, 0, 0))],
        out_specs=[blk, blk, blk],
        out_shape=[out, out, out],
        compiler_params=_cparams(("parallel", "parallel", "arbitrary")), name=name)(proj3, proj3, c_nat, c_t)


def _fox_bwd(proj3, o_raw, dmixed, lse, c_nat, c_t, name):
    bsz, t, _ = proj3.shape
    tq = _fox_tile(t)
    nq = t // tq

    def body(a_ref, or_ref, do_ref, lse_ref, cn_ref, ct_ref, dc_out, dct_out, drow_out, dq_sc, do_sc, dl_sc):
        def prep(i, c):
            r0 = pl.multiple_of(i * tq, tq)
            g = a_ref[pl.ds(r0, tq), 384:512]
            dout = do_ref[pl.ds(r0, tq), :]
            o = or_ref[pl.ds(r0, tq), :]
            dc_out[pl.ds(r0, tq), 384:512] = dout * o * _dsilu(g)
            do = dout * _silu(g)
            do_sc[pl.ds(r0, tq), :] = do
            prod = do * o
            d0 = jnp.sum(prod[:, 0:64], axis=1, keepdims=True)
            d1 = jnp.sum(prod[:, 64:128], axis=1, keepdims=True)
            dl_sc[pl.ds(r0, tq), :] = jnp.concatenate([jnp.broadcast_to(d0, (tq, 64)), jnp.broadcast_to(d1, (tq, 64))], axis=1)
            dq_sc[pl.ds(r0, tq), :] = jnp.zeros((tq, 128), F32)
            drow_out[pl.ds(r0, tq), :] = jnp.zeros((tq, 128), F32)
            return c

        lax.fori_loop(0, nq, prep, 0)
        dct_out[...] = jnp.zeros((8, t), F32)

        def kv_tile(j, c):
            c0 = pl.multiple_of(j * tq, tq)
            cols = _iota((tq, tq), 1) + j * tq
            dks, dvs = [], []
            for h in range(2):
                kh = a_ref[pl.ds(c0, tq), 128 + 64 * h:128 + 64 * (h + 1)].astype(BF16)
                vh = a_ref[pl.ds(c0, tq), 256 + 64 * h:256 + 64 * (h + 1)].astype(BF16)
                ck = ct_ref[h:h + 1, pl.ds(c0, tq)]

                def q_step(i, carry, h=h, kh=kh, vh=vh, ck=ck):
                    dk, dv, dcol = carry
                    r0 = pl.multiple_of(i * tq, tq)
                    hs = slice(64 * h, 64 * (h + 1))
                    qh = (a_ref[pl.ds(r0, tq), hs] * 0.125).astype(BF16)
                    doh = do_sc[pl.ds(r0, tq), hs].astype(BF16)
                    lse_h = lse_ref[pl.ds(r0, tq), 64 * h:64 * h + 1]
                    dl_h = dl_sc[pl.ds(r0, tq), 64 * h:64 * h + 1]
                    cq = cn_ref[pl.ds(r0, tq), 64 * h:64 * h + 1]
                    s = _dot(qh, kh, NT) + (cq - ck)
                    p = jnp.where(_iota((tq, tq), 0) + i * tq >= cols, jnp.exp(s - lse_h), 0.0)
                    dv = dv + _dot(p.astype(BF16), doh, TN)
                    dp = _dot(doh, vh, NT)
                    ds = p * (dp - dl_h)
                    dsb = ds.astype(BF16)
                    dq_sc[pl.ds(r0, tq), hs] += jnp.dot(dsb, kh, preferred_element_type=F32) * 0.125
                    dk = dk + _dot(dsb, qh, TN)
                    dcol = dcol - jnp.sum(ds, axis=0, keepdims=True)
                    drow_out[pl.ds(r0, tq), hs] += jnp.broadcast_to(jnp.sum(ds, axis=1, keepdims=True), (tq, 64))
                    return dk, dv, dcol

                init = (jnp.zeros((tq, 64), F32), jnp.zeros((tq, 64), F32), jnp.zeros((1, tq), F32))
                dk, dv, dcol = lax.fori_loop(j, nq, q_step, init)
                dks.append(dk)
                dvs.append(dv)
                dct_out[h:h + 1, pl.ds(c0, tq)] = dcol
            dc_out[pl.ds(c0, tq), 128:256] = jnp.concatenate(dks, axis=1)
            dc_out[pl.ds(c0, tq), 256:384] = jnp.concatenate(dvs, axis=1)
            return c

        lax.fori_loop(0, nq, kv_tile, 0)
        dc_out[:, 0:128] = dq_sc[...]

    blk = pl.BlockSpec((None, t, 128), lambda b, p: (b, 0, p))
    return pl.pallas_call(
        body, grid=(bsz, 4),
        in_specs=[pl.BlockSpec((None, t, 512), lambda b, p: (b, 0, C_BLK0 + p)),
                  blk,
                  pl.BlockSpec((None, t, 128), lambda b, p: (b, 0, 4 + p)),
                  blk, blk,
                  pl.BlockSpec((None, None, 8, t), lambda b, p: (b, p, 0, 0))],
        out_specs=[pl.BlockSpec((None, t, 512), lambda b, p: (b, 0, p)),
                   pl.BlockSpec((None, None, 8, t), lambda b, p: (b, p, 0, 0)), blk],
        out_shape=[jax.ShapeDtypeStruct((bsz, t, C_W), F32), jax.ShapeDtypeStruct((bsz, 4, 8, t), F32),
                   jax.ShapeDtypeStruct((bsz, t, FOX_W), F32)],
        scratch_shapes=[pltpu.VMEM((t, 128), F32), pltpu.VMEM((t, 128), F32), pltpu.VMEM((t, 128), F32)],
        compiler_params=_cparams(("parallel", "parallel")), name=name)(proj3, o_raw, dmixed, lse, c_nat, c_t)


def _mix_tm(n):
    return min(512, n)


def _outproj_fwd(x2, oa, ob, oc, wo, g_row, name):
    n, d = x2.shape
    tm = _mix_tm(n)

    def body(x_ref, oa_ref, ob_ref, oc_ref, w_ref, g_ref, y_ref, xo_ref):
        y = (jnp.dot(oa_ref[...].astype(BF16), w_ref[0:256, :], preferred_element_type=F32)
             + jnp.dot(ob_ref[...].astype(BF16), w_ref[256:512, :], preferred_element_type=F32)
             + jnp.dot(oc_ref[...].astype(BF16), w_ref[512:1024, :], preferred_element_type=F32))
        y_ref[...] = y
        xo_ref[...] = x_ref[...] + y * _rstd(y) * g_ref[...]

    row = lambda w: pl.BlockSpec((tm, w), lambda i: (i, 0))
    out = jax.ShapeDtypeStruct((n, d), F32)
    return pl.pallas_call(
        body, grid=(n // tm,),
        in_specs=[row(d), row(256), row(256), row(512), pl.BlockSpec((d, d), lambda i: (0, 0)),
                  pl.BlockSpec((1, d), lambda i: (0, 0))],
        out_specs=[row(d), row(d)], out_shape=[out, out],
        compiler_params=_cparams(("parallel",)), name=name)(x2, oa, ob, oc, wo, g_row)


def _loss_head(x2, target2, name):
    n, d = x2.shape
    tm = _mix_tm(n)

    def body(x_ref, t_ref, dx_ref, l_ref):
        err = x_ref[...] - t_ref[...]
        dx_ref[...] = err * (1.0 / d)

        @pl.when(pl.program_id(0) == 0)
        def _():
            l_ref[...] = jnp.zeros((8, 128), F32)

        l_ref[...] += jnp.sum(err * err)

    row = pl.BlockSpec((tm, d), lambda i: (i, 0))
    return pl.pallas_call(
        body, grid=(n // tm,), in_specs=[row, row],
        out_specs=[row, pl.BlockSpec((8, 128), lambda i: (0, 0))],
        out_shape=[jax.ShapeDtypeStruct((n, d), F32), jax.ShapeDtypeStruct((8, 128), F32)],
        compiler_params=_cparams(("arbitrary",)), name=name)(x2, target2)


def _outproj_bwd(dxo, y, oa, ob, oc, wo, g_row, name):
    n, d = dxo.shape
    tm = _mix_tm(n)

    def body(dx_ref, y_ref, oa_ref, ob_ref, oc_ref, w_ref, g_ref, dm_ref, dw_ref, dg_ref):
        @pl.when(pl.program_id(0) == 0)
        def _():
            dw_ref[...] = jnp.zeros((d, d), F32)
            dg_ref[...] = jnp.zeros((8, d), F32)

        yv, dx = y_ref[...], dx_ref[...]
        r = _rstd(yv)
        yn = yv * r
        dg_ref[...] += jnp.sum(dx * yn, axis=0, keepdims=True)
        dyn = dx * g_ref[...]
        dy = (r * (dyn - yn * jnp.mean(dyn * yn, axis=-1, keepdims=True))).astype(BF16)
        dm_ref[...] = _dot(dy, w_ref[...], NT)
        dw_ref[0:256, :] += _dot(oa_ref[...].astype(BF16), dy, TN)
        dw_ref[256:512, :] += _dot(ob_ref[...].astype(BF16), dy, TN)
        dw_ref[512:1024, :] += _dot(oc_ref[...].astype(BF16), dy, TN)

    row = lambda w: pl.BlockSpec((tm, w), lambda i: (i, 0))
    fixed = lambda r, c: pl.BlockSpec((r, c), lambda i: (0, 0))
    return pl.pallas_call(
        body, grid=(n // tm,),
        in_specs=[row(d), row(d), row(256), row(256), row(512), fixed(d, d), fixed(1, d)],
        out_specs=[row(d), fixed(d, d), fixed(8, d)],
        out_shape=[jax.ShapeDtypeStruct((n, d), F32), jax.ShapeDtypeStruct((d, d), F32), jax.ShapeDtypeStruct((8, d), F32)],
        compiler_params=_cparams(("arbitrary",)), name=name)(dxo, y, oa, ob, oc, wo, g_row)


_PIECES = ((0, A_W), (A_W, B_W), (A_W + B_W, C_W), (A_W + B_W + C_W, F_W))


def _inproj_bwd_x(x2, dxo, g_row, w_int, pieces, name):
    n, d = x2.shape
    tm = min(256, n)

    def body(x_ref, dxo_ref, g_ref, w_ref, da_ref, db_ref, dc_ref, df_ref, dx_ref, dg_ref):
        @pl.when(pl.program_id(0) == 0)
        def _():
            dg_ref[...] = jnp.zeros((8, d), F32)

        dh = jnp.zeros((tm, d), F32)
        for ref, (o, w) in zip((da_ref, db_ref, dc_ref, df_ref), _PIECES):
            dh = dh + _dot(ref[...].astype(BF16), w_ref[:, o:o + w], NT)
        x = x_ref[...]
        r = _rstd(x)
        xn = x * r
        dg_ref[...] += jnp.sum(dh * xn, axis=0, keepdims=True)
        dxn = dh * g_ref[...]
        dx_ref[...] = dxo_ref[...] + r * (dxn - xn * jnp.mean(dxn * xn, axis=-1, keepdims=True))

    row = lambda w: pl.BlockSpec((tm, w), lambda i: (i, 0))
    fixed = lambda r, c: pl.BlockSpec((r, c), lambda i: (0, 0))
    return pl.pallas_call(
        body, grid=(n // tm,),
        in_specs=[row(d), row(d), fixed(1, d), fixed(d, E_INT)] + [row(w) for _, w in _PIECES],
        out_specs=[row(d), fixed(8, d)],
        out_shape=[jax.ShapeDtypeStruct((n, d), F32), jax.ShapeDtypeStruct((8, d), F32)],
        compiler_params=_cparams(("arbitrary",)), name=name)(x2, dxo, g_row, w_int, *pieces)


def _inproj_bwd_w(x2, g_row, piece, name):
    n, d = x2.shape
    w = piece.shape[1]
    tm = min(512, n)

    def body(x_ref, g_ref, dp_ref, dw_ref):
        @pl.when(pl.program_id(0) == 0)
        def _():
            dw_ref[...] = jnp.zeros((d, w), F32)

        x = x_ref[...]
        h = (x * _rstd(x) * g_ref[...]).astype(BF16)
        dw_ref[...] += _dot(h, dp_ref[...].astype(BF16), TN)

    return pl.pallas_call(
        body, grid=(n // tm,),
        in_specs=[pl.BlockSpec((tm, d), lambda i: (i, 0)), pl.BlockSpec((1, d), lambda i: (0, 0)),
                  pl.BlockSpec((tm, w), lambda i: (i, 0))],
        out_specs=pl.BlockSpec((d, w), lambda i: (0, 0)),
        out_shape=jax.ShapeDtypeStruct((d, w), F32),
        compiler_params=_cparams(("arbitrary",)), name=name)(x2, g_row, piece)


def _block_diag(pool_w_l):
    z = jnp.zeros((64, 64), pool_w_l.dtype)
    return jnp.concatenate(
        [jnp.concatenate([pool_w_l[g] if c == g else z for c in range(4)], axis=1) for g in range(4)], axis=0)


def _pad_lanes(v, width=128):
    return jnp.pad(v, ((0, 0),) * (v.ndim - 1) + ((0, width - v.shape[-1]),))


def _local_step(x, target, lower_bounds, pre_norm_g, w_in_int, hgrn_norm_g, fox_f_bias, pool_w, pool_scale,
                w_out_bf, post_norm_g):
    bsz, t, d = x.shape
    n = bsz * t
    lbs = _lbs_fwd(lower_bounds)
    saved = []
    xc = x.reshape(n, d)
    for l in range(DEPTH):
        proj = _inproj_fwd(xc, pre_norm_g[l:l + 1], w_in_int[l], f"inproj_fwd{l}").reshape(bsz, t, E_INT)
        wbd = _block_diag(pool_w[l]).astype(BF16)
        bias_row = _pad_lanes(fox_f_bias[l:l + 1])
        oa, oa_raw = _hgrn_fwd(proj, lbs[l:l + 1], hgrn_norm_g[l].reshape(HGRN_W, 1), f"hgrn_fwd{l}")
        ob = _pool_fwd(proj, wbd, pool_scale[l:l + 1], f"pool_fwd{l}")
        c_nat, c_t = _foxgate_fwd(proj, bias_row, f"foxgate_fwd{l}")
        oc, oc_raw, lse = _fox_fwd(proj, c_nat, c_t, f"fox_fwd{l}")
        y, xn = _outproj_fwd(xc, oa.reshape(n, -1), ob.reshape(n, -1), oc.reshape(n, -1), w_out_bf[l],
                             post_norm_g[l:l + 1], f"outproj_fwd{l}")
        saved.append((xc, proj, wbd, bias_row, oa, oa_raw, ob, oc, oc_raw, lse, c_nat, c_t, y))
        xc = xn
    dx, sq = _loss_head(xc, target.reshape(n, d), "loss_head")
    g = {k: [None] * DEPTH for k in ("pre", "w_in", "hgn", "bias", "pool_w", "pool_scale", "w_out", "post", "lbs")}
    for l in reversed(range(DEPTH)):
        xin, proj, wbd, bias_row, oa, oa_raw, ob, oc, oc_raw, lse, c_nat, c_t, y = saved[l]
        dmix, g["w_out"][l], dpost = _outproj_bwd(dx, y, oa.reshape(n, -1), ob.reshape(n, -1), oc.reshape(n, -1),
                                                  w_out_bf[l], post_norm_g[l:l + 1], f"outproj_bwd{l}")
        g["post"][l] = dpost[0]
        dmix3 = dmix.reshape(bsz, t, d)
        d_c, dct, drow = _fox_bwd(proj, oc_raw, dmix3, lse, c_nat, c_t, f"fox_bwd{l}")
        dc_nat = _pad_lanes(dct[:, :, 0:2, :].reshape(bsz, FOX_HEADS, t).transpose(0, 2, 1)
                            + drow.reshape(bsz, t, FOX_HEADS, 64)[..., 0])
        d_f, dbias = _foxgate_bwd(proj, dc_nat, bias_row, f"foxgate_bwd{l}")
        g["bias"][l] = jnp.sum(dbias[:, 0, :FOX_HEADS], axis=0)
        d_b, dscale, dwbd = _pool_bwd(proj, dmix3, wbd, pool_scale[l:l + 1], f"pool_bwd{l}")
        g["pool_scale"][l] = jnp.sum(dscale[:, 0], axis=0)
        dwbd = jnp.sum(dwbd, axis=0)
        g["pool_w"][l] = jnp.stack([dwbd[64 * k:64 * (k + 1), 64 * k:64 * (k + 1)] for k in range(4)])
        d_a, dgn, dlb = _hgrn_bwd(proj, oa_raw, dmix3, lbs[l:l + 1], hgrn_norm_g[l:l + 1], f"hgrn_bwd{l}")
        g["hgn"][l] = jnp.sum(dgn[:, 0], axis=0)
        g["lbs"][l] = jnp.sum(dlb[:, 0], axis=0)
        pieces = [p.reshape(n, -1) for p in (d_a, d_b, d_c, d_f)]
        g["w_in"][l] = jnp.concatenate(
            [_inproj_bwd_w(xin, pre_norm_g[l:l + 1], p, f"inproj_bwd_w{l}_{k}") for k, p in enumerate(pieces)], axis=1)
        dx, dpre = _inproj_bwd_x(xin, dx, pre_norm_g[l:l + 1], w_in_int[l], pieces, f"inproj_bwd_x{l}")
        g["pre"][l] = dpre[0]
    grads = {k: jnp.stack(v) for k, v in g.items()}
    return sq, dx.reshape(bsz, t, d), grads


def _place():
    return lax.axis_index("x"), lax.axis_index("y"), lax.axis_index("c")


def _other_chips(x, y):
    return [(1 - x, y), (x, 1 - y), (1 - x, 1 - y)]


_ANY = pl.BlockSpec(memory_space=pl.ANY)


def _gather_weights(w_in_sh, w_out_sh):
    def body(win_ref, wout_ref, ain_ref, aout_ref, send_sems, recv_sems, local_sems):
        x, y, c = _place()
        me = 2 * x + y
        mine = [pltpu.make_async_copy(win_ref, ain_ref.at[me], local_sems.at[0]),
                pltpu.make_async_copy(wout_ref, aout_ref.at[me], local_sems.at[1])]
        for cp in mine:
            cp.start()
        sends = []
        for k, (px, py) in enumerate(_other_chips(x, y)):
            for j, (src, dst) in enumerate(((win_ref, ain_ref), (wout_ref, aout_ref))):
                sends.append(pltpu.make_async_remote_copy(
                    src_ref=src, dst_ref=dst.at[me], send_sem=send_sems.at[2 * k + j], recv_sem=recv_sems.at[2 * k + j],
                    device_id=(px, py, c), device_id_type=MESH))
        for cp in sends:
            cp.start()
        for k, (px, py) in enumerate(_other_chips(x, y)):
            for j, (src, dst) in enumerate(((win_ref, ain_ref), (wout_ref, aout_ref))):
                pltpu.make_async_remote_copy(
                    src_ref=src, dst_ref=dst.at[2 * px + py], send_sem=send_sems.at[2 * k + j],
                    recv_sem=recv_sems.at[2 * k + j], device_id=(px, py, c), device_id_type=MESH).wait_recv()
        for cp in sends:
            cp.wait_send()
        for cp in mine:
            cp.wait()

    return pl.pallas_call(
        body, in_specs=[_ANY, _ANY], out_specs=[_ANY, _ANY],
        out_shape=[jax.ShapeDtypeStruct((N_CHIPS,) + w_in_sh.shape, w_in_sh.dtype),
                   jax.ShapeDtypeStruct((N_CHIPS,) + w_out_sh.shape, w_out_sh.dtype)],
        scratch_shapes=[pltpu.SemaphoreType.DMA((6,)), pltpu.SemaphoreType.DMA((6,)), pltpu.SemaphoreType.DMA((2,))],
        name="gather_weights")(w_in_sh, w_out_sh)


def _swap_with_sibling(parts, name):
    k = len(parts)

    def body(*refs):
        src, dst = refs[:k], refs[k:2 * k]
        send_sems, recv_sems = refs[2 * k:]
        x, y, c = _place()
        cps = [pltpu.make_async_remote_copy(src_ref=src[j], dst_ref=dst[j], send_sem=send_sems.at[j], recv_sem=recv_sems.at[j],
                                            device_id=(x, y, 1 - c), device_id_type=MESH) for j in range(k)]
        for cp in cps:
            cp.start()
        for cp in cps:
            cp.wait()

    return pl.pallas_call(
        body, in_specs=[_ANY] * k, out_specs=[_ANY] * k,
        out_shape=[jax.ShapeDtypeStruct(p.shape, p.dtype) for p in parts],
        scratch_shapes=[pltpu.SemaphoreType.DMA((k,)), pltpu.SemaphoreType.DMA((k,))], name=name)(*parts)


def _scatter_to_chips(parts, name):
    k = len(parts)

    def body(*refs):
        src, dst = refs[:k], refs[k:2 * k]
        send_sems, recv_sems = refs[2 * k:]
        x, y, c = _place()
        me = 2 * x + y
        cps = []
        for rel, (px, py) in enumerate(_other_chips(x, y)):
            for j in range(k):
                cps.append(pltpu.make_async_remote_copy(
                    src_ref=src[j].at[2 * px + py], dst_ref=dst[j].at[rel], send_sem=send_sems.at[rel * k + j],
                    recv_sem=recv_sems.at[rel * k + j], device_id=(px, py, c), device_id_type=MESH))
        for cp in cps:
            cp.start()
        for cp in cps:
            cp.wait()
        del me

    return pl.pallas_call(
        body, in_specs=[_ANY] * k, out_specs=[_ANY] * k,
        out_shape=[jax.ShapeDtypeStruct((3,) + p.shape[1:], p.dtype) for p in parts],
        scratch_shapes=[pltpu.SemaphoreType.DMA((3 * k,)), pltpu.SemaphoreType.DMA((3 * k,))], name=name)(*parts)


def _add_n(parts, name):
    r, c = parts[0].shape
    tr = 256 if r % 256 == 0 else r

    def body(*refs):
        acc = refs[0][...]
        for ref in refs[1:-1]:
            acc = acc + ref[...]
        refs[-1][...] = acc

    blk = pl.BlockSpec((tr, c), lambda i: (i, 0))
    return pl.pallas_call(
        body, grid=(r // tr,), in_specs=[blk] * len(parts), out_specs=blk,
        out_shape=jax.ShapeDtypeStruct((r, c), F32), compiler_params=_cparams(("parallel",)), name=name)(*parts)


def _all_reduce_small(packet):
    r, w = packet.shape

    def body(p_ref, o_ref, buf, send_sems, recv_sems):
        x, y, c = _place()
        me = 4 * x + 2 * y + c
        buf[me] = p_ref[...]
        peers = []
        for k in range(1, 8):
            fx, fy, fc = (k >> 2) & 1, (k >> 1) & 1, k & 1
            peers.append((x ^ fx, y ^ fy, c ^ fc))
        cps = [pltpu.make_async_remote_copy(src_ref=p_ref, dst_ref=buf.at[me], send_sem=send_sems.at[k], recv_sem=recv_sems.at[k],
                                            device_id=peer, device_id_type=MESH) for k, peer in enumerate(peers)]
        for cp in cps:
            cp.start()
        for k, (px, py, pc) in enumerate(peers):
            pltpu.make_async_remote_copy(src_ref=p_ref, dst_ref=buf.at[4 * px + 2 * py + pc], send_sem=send_sems.at[k],
                                         recv_sem=recv_sems.at[k], device_id=(px, py, pc), device_id_type=MESH).wait_recv()
        for cp in cps:
            cp.wait_send()
        acc = buf[0]
        for k in range(1, 8):
            acc = acc + buf[k]
        o_ref[...] = acc

    vm = pl.BlockSpec(memory_space=pltpu.VMEM)
    return pl.pallas_call(
        body, in_specs=[vm], out_specs=vm, out_shape=jax.ShapeDtypeStruct((r, w), F32),
        scratch_shapes=[pltpu.VMEM((8, r, w), F32), pltpu.SemaphoreType.DMA((7,)), pltpu.SemaphoreType.DMA((7,))],
        name="all_reduce_small")(packet)


def _adamw_math(w, g, m, v):
    m = ADAM_B1 * m + (1.0 - ADAM_B1) * g
    v = ADAM_B2 * v + (1.0 - ADAM_B2) * (g * g)
    m_hat = m / (1.0 - ADAM_B1 ** ADAM_STEP)
    v_hat = v / (1.0 - ADAM_B2 ** ADAM_STEP)
    return -ADAM_LR * (m_hat / (jnp.sqrt(v_hat) + ADAM_EPS) + ADAM_WD * w), m, v


def _adamw(w, g, m, v, name):
    nl, r, c = w.shape
    tr = 256 if r % 256 == 0 else r

    def body(w_ref, g_ref, m_ref, v_ref, d_ref, mo_ref, vo_ref):
        d_ref[...], mo_ref[...], vo_ref[...] = _adamw_math(w_ref[...], g_ref[...], m_ref[...], v_ref[...])

    blk = pl.BlockSpec((None, tr, c), lambda l, i: (l, i, 0))
    out = jax.ShapeDtypeStruct(w.shape, F32)
    return pl.pallas_call(
        body, grid=(nl, r // tr), in_specs=[blk] * 4, out_specs=[blk] * 3, out_shape=[out] * 3,
        compiler_params=_cparams(("parallel", "parallel")), name=name)(w, g, m, v)


def _small_update(gsum, lower_bounds, wpack, mpack, vpack):
    r, w = gsum.shape
    lb_rows = DEPTH * HGRN_W // 128

    def body(g_ref, a_ref, w_ref, m_ref, v_ref, go_ref, d_ref, mo_ref, vo_ref):
        a = a_ref[...]
        a0, a1 = a[0:1], a[1:2]
        mx = jnp.maximum(a0, a1)
        e0, e1 = jnp.exp(a0 - mx), jnp.exp(a1 - mx)
        p0, p1 = e0 / (e0 + e1), e1 / (e0 + e1)
        g = g_ref[...]
        half = lb_rows // 2
        dl0 = jnp.concatenate([g[k:k + 1] for k in range(half)], axis=1)
        dl1 = jnp.concatenate([g[half + k:half + k + 1] for k in range(half)], axis=1)
        dp0 = (dl0 + dl1) - (dl0 + dl1)
        dp1 = dl1
        inner = p0 * dp0 + p1 * dp1
        da0, da1 = p0 * (dp0 - inner), p1 * (dp1 - inner)
        rows = [da0[:, 128 * k:128 * (k + 1)] for k in range(half)] + [da1[:, 128 * k:128 * (k + 1)] for k in range(half)]
        gfull = jnp.concatenate(rows + [g[lb_rows:]], axis=0)
        go_ref[...] = gfull
        d_ref[...], mo_ref[...], vo_ref[...] = _adamw_math(w_ref[...], gfull, m_ref[...], v_ref[...])

    vm = pl.BlockSpec(memory_space=pltpu.VMEM)
    out = jax.ShapeDtypeStruct((r, w), F32)
    return pl.pallas_call(body, in_specs=[vm] * 5, out_specs=[vm] * 4, out_shape=[out] * 4, name="small_update")(
        gsum, lower_bounds, wpack, mpack, vpack)


_SMALL = ("lower_bounds", "pre_norm_g", "hgrn_norm_g", "fox_f_bias", "pool_w", "pool_scale", "post_norm_g")


def _pack(parts):
    rows = []
    for k in _SMALL:
        f = parts[k].reshape(-1)
        pad = (-f.shape[0]) % (8 * 128)
        rows.append(jnp.pad(f, (0, pad)).reshape(-1, 128))
    rows.append(jnp.zeros((8, 128), F32))
    return jnp.concatenate(rows, axis=0)


def _unpack(pack, like):
    out, r = {}, 0
    for k in _SMALL:
        size = int(np.prod(like[k].shape))
        nr = -(-size // (8 * 128)) * 8
        out[k] = pack[r:r + nr].reshape(-1)[:size].reshape(like[k].shape)
        r += nr
    return out, r


def kernel(x, lower_bounds, pre_norm_g, w_in, hgrn_norm_g, fox_f_bias, pool_w, pool_scale, w_out, post_norm_g, loss_target, m_lower_bounds, m_pre_norm_g, m_w_in, m_hgrn_norm_g, m_fox_f_bias, m_pool_w, m_pool_scale, m_w_out, m_post_norm_g, v_lower_bounds, v_pre_norm_g, v_w_in, v_hgrn_norm_g, v_fox_f_bias, v_pool_w, v_pool_scale, v_w_out, v_post_norm_g):
    cx, cy, cc = _place()
    chip = 2 * cx + cy

    ain, aout = _gather_weights(w_in.astype(BF16), w_out.astype(BF16))
    w_in_full = jnp.concatenate([ain[q] for q in range(N_CHIPS)], axis=-1)
    w_in_int = _to_internal(w_in_full)
    w_out_full = jnp.concatenate([aout[q] for q in range(N_CHIPS)], axis=1)

    sq, grad_x, g = _local_step(x, loss_target, lower_bounds, pre_norm_g, w_in_int, hgrn_norm_g, fox_f_bias, pool_w,
                                pool_scale, w_out_full, post_norm_g)

    gin = _to_original(g["w_in"])
    gin_blocks = jnp.stack([gin[:, :, SHARD_W * q:SHARD_W * (q + 1)] for q in range(N_CHIPS)])
    gout_blocks = g["w_out"].reshape(DEPTH, N_CHIPS, 256, D_MODEL).transpose(1, 0, 2, 3)
    take = lambda a, l: lax.dynamic_index_in_dim(a, l, axis=1, keepdims=False)
    mine_in, mine_out = take(gin_blocks, cc), take(gout_blocks, cc)
    sib_in, sib_out = _swap_with_sibling([take(gin_blocks, 1 - cc), take(gout_blocks, 1 - cc)], "grad_swap1")
    rin, rout = 4 * 1024, 4 * 256
    sum_in = _add_n([mine_in.reshape(rin, SHARD_W), sib_in.reshape(rin, SHARD_W)], "grad_add1_in").reshape(4, 1024, SHARD_W)
    sum_out = _add_n([mine_out.reshape(rout, D_MODEL), sib_out.reshape(rout, D_MODEL)], "grad_add1_out").reshape(4, 256, D_MODEL)
    got_in, got_out = _scatter_to_chips([sum_in, sum_out], "grad_scatter")
    own = lambda a: lax.dynamic_index_in_dim(a, chip, axis=0, keepdims=False)
    half_in = _add_n([own(sum_in)] + [got_in[k] for k in range(3)], "grad_add2_in")
    half_out = _add_n([own(sum_out)] + [got_out[k] for k in range(3)], "grad_add2_out")
    oth_in, oth_out = _swap_with_sibling([half_in, half_out], "grad_swap2")
    first = cc == 0
    grad_w_in = jnp.stack([jnp.where(first, half_in, oth_in), jnp.where(first, oth_in, half_in)])
    grad_w_out = jnp.stack([jnp.where(first, half_out, oth_out), jnp.where(first, oth_out, half_out)])

    small = {"lower_bounds": g["lbs"], "pre_norm_g": g["pre"], "hgrn_norm_g": g["hgn"], "fox_f_bias": g["bias"],
             "pool_w": g["pool_w"], "pool_scale": g["pool_scale"], "post_norm_g": g["post"]}
    packet = _pack(small)
    nrows = packet.shape[0]
    packet = packet.at[nrows - 1].set(sq[0])
    gsum = _all_reduce_small(packet)
    loss = gsum[nrows - 1, 0] * (0.5 / D_MODEL)

    weights = {"lower_bounds": lower_bounds, "pre_norm_g": pre_norm_g, "hgrn_norm_g": hgrn_norm_g,
               "fox_f_bias": fox_f_bias, "pool_w": pool_w, "pool_scale": pool_scale, "post_norm_g": post_norm_g}
    moments_m = {"lower_bounds": m_lower_bounds, "pre_norm_g": m_pre_norm_g, "hgrn_norm_g": m_hgrn_norm_g,
                 "fox_f_bias": m_fox_f_bias, "pool_w": m_pool_w, "pool_scale": m_pool_scale, "post_norm_g": m_post_norm_g}
    moments_v = {"lower_bounds": v_lower_bounds, "pre_norm_g": v_pre_norm_g, "hgrn_norm_g": v_hgrn_norm_g,
                 "fox_f_bias": v_fox_f_bias, "pool_w": v_pool_w, "pool_scale": v_pool_scale, "post_norm_g": v_post_norm_g}
    gp, dp, mp, vp = _small_update(gsum, lower_bounds, _pack(weights), _pack(moments_m), _pack(moments_v))
    gs, _ = _unpack(gp, weights)
    ds, _ = _unpack(dp, weights)
    ms, _ = _unpack(mp, weights)
    vs, _ = _unpack(vp, weights)

    d_in, m_in, v_in = _adamw(w_in, grad_w_in, m_w_in, v_w_in, "adamw_w_in")
    d_out, m_out, v_out = _adamw(w_out, grad_w_out, m_w_out, v_w_out, "adamw_w_out")

    def ordered(s, big_in, big_out):
        return (s["lower_bounds"], s["pre_norm_g"], big_in, s["hgrn_norm_g"], s["fox_f_bias"], s["pool_w"],
                s["pool_scale"], big_out, s["post_norm_g"])

    return (loss, grad_x, *ordered(gs, grad_w_in, grad_w_out), *ordered(ds, d_in, d_out),
            *ordered(ms, m_in, m_out), *ordered(vs, v_in, v_out))
```

```python
import functools

import numpy as np
import jax
import jax.numpy as jnp
from jax import lax
from jax.experimental import pallas as pl
from jax.experimental.pallas import tpu as pltpu

F32 = jnp.float32
BF16 = jnp.bfloat16
HI = lax.Precision.HIGHEST
MESH = pl.DeviceIdType.MESH

NORM_EPS = 1e-6
MASK_VALUE = -1e30
TINY = 1e-30
ADAM_LR, ADAM_B1, ADAM_B2, ADAM_EPS, ADAM_WD, ADAM_STEP = 0.001, 0.9, 0.999, 1e-08, 0.01, 10

D_MODEL = 1024
DEPTH = 2
N_CHIPS = 4
CHUNK = 64
LANES = 128
HGRN_W, POOL_W, FOX_W, FOX_HEADS = 256, 256, 512, 8
POOL_WINDOWS = (2, 4, 8, 16)
POOL_HALO = 16
IN_WIDTH = 3592
SHARD_W = IN_WIDTH // N_CHIPS
A_W, B_W, C_W, F_W = 1024, 512, 2048, 128
E_INT = A_W + B_W + C_W + F_W
B_BLK = A_W // 512
C_BLK0 = (A_W + B_W) // 512
F_BLK = (A_W + B_W + C_W) // 128


def _segments():
    segs = []
    for hp in range(2):
        for part in range(4):
            segs.append((part * 256 + hp * 128, 128))
    segs.append((1024, 256))
    segs.append((1280, 256))
    for hp in range(4):
        for part in range(4):
            segs.append((1536 + part * 512 + hp * 128, 128))
    segs.append((3584, 8))
    return segs


_SEGS = _segments()


def _to_internal(w):
    parts = [w[..., s:s + n] for s, n in _SEGS]
    parts.append(jnp.zeros(w.shape[:-1] + (E_INT - IN_WIDTH,), w.dtype))
    return jnp.concatenate(parts, axis=-1)


def _to_original(w):
    offs, o = [], 0
    for s, n in _SEGS:
        offs.append((s, o, n))
        o += n
    parts = [w[..., o:o + n] for s, o, n in sorted(offs)]
    return jnp.concatenate(parts, axis=-1)


def _cparams(sem=None, vmem_mb=48):
    kw = dict(vmem_limit_bytes=vmem_mb * 1024 * 1024)
    if sem is not None:
        kw["dimension_semantics"] = sem
    return pltpu.CompilerParams(**kw)


def _sig(x):
    return 1.0 / (1.0 + jnp.exp(-x))


def _silu(x):
    return x * _sig(x)


def _dsilu(x):
    s = _sig(x)
    return s * (1.0 + x * (1.0 - s))


def _rstd(x):
    return lax.rsqrt(jnp.mean(x * x, axis=-1, keepdims=True) + NORM_EPS)


def _dot(a, b, dims, **kw):
    return lax.dot_general(a, b, (dims, ((), ())), preferred_element_type=F32, **kw)


NN = ((1,), (0,))
NT = ((1,), (1,))
TN = ((0,), (0,))


def _iota(shape, dim):
    return lax.broadcasted_iota(jnp.int32, shape, dim)


def _lbs_fwd(lower_bounds):
    def body(a_ref, o_ref):
        a = a_ref[...]
        a0, a1 = a[0:1], a[1:2]
        m = jnp.maximum(a0, a1)
        e0, e1 = jnp.exp(a0 - m), jnp.exp(a1 - m)
        p0, p1 = e0 / (e0 + e1), e1 / (e0 + e1)
        o_ref[...] = jnp.concatenate([p0 - p0, (p0 + p1) - p0], axis=0)

    return pl.pallas_call(body, out_shape=jax.ShapeDtypeStruct(lower_bounds.shape, F32), name="lbs_fwd")(lower_bounds)


def _inproj_fwd(x2, g_row, w_int, name):
    n, d = x2.shape
    e = w_int.shape[1]
    tm = min(256, n)

    def body(x_ref, g_ref, w_ref, o_ref):
        x = x_ref[...]
        h = (x * _rstd(x) * g_ref[...]).astype(BF16)
        o_ref[...] = jnp.dot(h, w_ref[...], preferred_element_type=F32)

    return pl.pallas_call(
        body, grid=(n // tm,),
        in_specs=[pl.BlockSpec((tm, d), lambda i: (i, 0)), pl.BlockSpec((1, d), lambda i: (0, 0)),
                  pl.BlockSpec((d, e), lambda i: (0, 0))],
        out_specs=pl.BlockSpec((tm, e), lambda i: (i, 0)),
        out_shape=jax.ShapeDtypeStruct((n, e), F32),
        compiler_params=_cparams(("parallel",)), name=name)(x2, g_row, w_int)


def _chunk_cumsum_matrix():
    i, j = _iota((LANES, LANES), 0), _iota((LANES, LANES), 1)
    return ((i <= j) & ((i // CHUNK) == (j // CHUNK))).astype(F32)


def _hgrn_gates(a, lb):
    qa, z = a[:, 0:128], a[:, 128:256]
    sg, sgn = _sig(z), _sig(-z)
    fg = lb + (1.0 - lb) * sg
    lf = jnp.log(jnp.maximum(fg, TINY))
    kk = (1.0 - lb) * sgn
    return qa * _sig(qa), kk, lf, sg, sgn, fg


def _hgrn_fwd(proj3, lbs_row, gn_col, name):
    bsz, t, _ = proj3.shape
    nt = t // LANES

    def body(a_ref, lb_ref, gn_ref, og_ref, or_ref):
        lb = lb_ref[...]
        gn = gn_ref[...]
        umat = _chunk_cumsum_matrix()
        lane64 = _iota((1, LANES), 1) % CHUNK

        def tile(i, carry):
            r0 = pl.multiple_of(i * LANES, LANES)
            a = a_ref[pl.ds(r0, LANES), :]
            qq, kk, lf, _, _, _ = _hgrn_gates(a, lb)
            va, ga = a[:, 256:384], a[:, 384:512]
            q_t, k_t, v_t = qq.T, kk.T, va.T
            b_t = jnp.dot(lf.T, umat, precision=HI, preferred_element_type=F32)
            new_s, o_heads = [], []
            for h in range(2):
                s_h = carry[h]
                rs = slice(CHUNK * h, CHUNK * (h + 1))
                qh, kh, vh, bh = q_t[rs], k_t[rs], v_t[rs], b_t[rs]
                inter = []
                for c in range(2):
                    cs = slice(CHUNK * c, CHUNK * (c + 1))
                    b_ = bh[:, cs]
                    qt = (qh[:, cs] * jnp.exp(b_)).astype(BF16)
                    inter.append(_dot(s_h.astype(BF16), qt, TN))
                    bl = b_[:, CHUNK - 1:CHUNK]
                    kt = (kh[:, cs] * jnp.exp(bl - b_)).astype(BF16)
                    s_h = jnp.exp(bl) * s_h + _dot(kt, vh[:, cs].astype(BF16), NT)
                new_s.append(s_h)

                acc = jnp.concatenate(inter, axis=1) + jnp.sum(qh * kh, axis=0, keepdims=True) * vh
                for dlt in range(1, CHUNK):
                    kr, br, vr = pltpu.roll(kh, dlt, 1), pltpu.roll(bh, dlt, 1), pltpu.roll(vh, dlt, 1)
                    e = jnp.exp(jnp.minimum(bh - br, 0.0))
                    att = jnp.sum(qh * kr * e, axis=0, keepdims=True)
                    acc = acc + jnp.where(lane64 >= dlt, att, 0.0) * vr
                o_heads.append(acc)
            normed = []
            for h in range(2):
                o_h = o_heads[h]
                ms = jnp.mean(o_h * o_h, axis=0, keepdims=True)
                normed.append(o_h * lax.rsqrt(ms + NORM_EPS) * gn[CHUNK * h:CHUNK * (h + 1)])
            or_ref[pl.ds(r0, LANES), :] = jnp.concatenate(o_heads, axis=0).T
            og_ref[pl.ds(r0, LANES), :] = jnp.concatenate(normed, axis=0).T * _silu(ga)
            return tuple(new_s)

        zero = jnp.zeros((CHUNK, CHUNK), F32)
        lax.fori_loop(0, nt, tile, (zero, zero))

    out = jax.ShapeDtypeStruct((bsz, t, HGRN_W), F32)
    return pl.pallas_call(
        body, grid=(bsz, 2),
        in_specs=[pl.BlockSpec((None, t, 512), lambda b, p: (b, 0, p)),
                  pl.BlockSpec((1, 128), lambda b, p: (0, p)),
                  pl.BlockSpec((128, 1), lambda b, p: (p, 0))],
        out_specs=[pl.BlockSpec((None, t, 128), lambda b, p: (b, 0, p)),
                   pl.BlockSpec((None, t, 128), lambda b, p: (b, 0, p))],
        out_shape=[out, out],
        compiler_params=_cparams(("parallel", "parallel")), name=name)(proj3, lbs_row, gn_col)


def _hgrn_bwd(proj3, o_raw, dmixed, lbs_row, gn_row, name):
    bsz, t, _ = proj3.shape
    nt = t // LANES
    nchunk = t // CHUNK

    def body(a_ref, or_ref, do_ref, lb_ref, gn_ref, da_ref, dgn_ref, dlb_ref, s_sc):
        lb = lb_ref[...]
        gn = gn_ref[...]
        umat = _chunk_cumsum_matrix()
        lane = _iota((1, LANES), 1)
        lane64 = lane % CHUNK
        half = lane < CHUNK

        def t_layout(a):
            qq, kk, lf, sg, sgn, fg = _hgrn_gates(a, lb)
            b_t = jnp.dot(lf.T, umat, precision=HI, preferred_element_type=F32)
            return qq.T, kk.T, a[:, 256:384].T, b_t, (sg, sgn, fg)

        def fwd_tile(i, carry):
            r0 = pl.multiple_of(i * LANES, LANES)
            q_t, k_t, v_t, b_t, _ = t_layout(a_ref[pl.ds(r0, LANES), :])
            new_s = []
            for h in range(2):
                s_h = carry[h]
                rs = slice(CHUNK * h, CHUNK * (h + 1))
                for c in range(2):
                    cs = slice(CHUNK * c, CHUNK * (c + 1))
                    s_sc[h, 2 * i + c] = s_h
                    b_ = b_t[rs, cs]
                    bl = b_[:, CHUNK - 1:CHUNK]
                    kt = (k_t[rs, cs] * jnp.exp(bl - b_)).astype(BF16)
                    s_h = jnp.exp(bl) * s_h + _dot(kt, v_t[rs, cs].astype(BF16), NT)
                new_s.append(s_h)
            return tuple(new_s)

        zero = jnp.zeros((CHUNK, CHUNK), F32)
        lax.fori_loop(0, nt, fwd_tile, (zero, zero))

        def half_mean(v):
            m0 = jnp.sum(jnp.where(half, v, 0.0), axis=1, keepdims=True) * (1.0 / CHUNK)
            m1 = jnp.sum(jnp.where(half, 0.0, v), axis=1, keepdims=True) * (1.0 / CHUNK)
            return jnp.where(half, m0, m1)

        def bwd_tile(k, carry):
            ds0, ds1, dgn_acc, dlb_acc = carry
            i = nt - 1 - k
            r0 = pl.multiple_of(i * LANES, LANES)
            a = a_ref[pl.ds(r0, LANES), :]
            qa, z, ga = a[:, 0:128], a[:, 128:256], a[:, 384:512]
            q_t, k_t, v_t, b_t, (sg, sgn, fg) = t_layout(a)
            oraw = or_ref[pl.ds(r0, LANES), :]
            dout = do_ref[pl.ds(r0, LANES), :]
            r = lax.rsqrt(half_mean(oraw * oraw) + NORM_EPS)
            xn = oraw * r
            dga = dout * (xn * gn) * _dsilu(ga)
            don = dout * _silu(ga)
            dgn_acc = dgn_acc + jnp.sum(don * xn, axis=0, keepdims=True)
            dxn = don * gn
            do_t = (r * (dxn - xn * half_mean(dxn * xn))).T
            new_ds, dq_h, dk_h, dv_h, db_h = [], [], [], [], []
            for h in range(2):
                ds_h = (ds0, ds1)[h]
                rs = slice(CHUNK * h, CHUNK * (h + 1))
                qh, kh, vh, bh, doh = q_t[rs], k_t[rs], v_t[rs], b_t[rs], do_t[rs]
                dq_c, dk_c, dv_c, dbl_c = [None, None], [None, None], [None, None], [None, None]
                for c in (1, 0):
                    cs = slice(CHUNK * c, CHUNK * (c + 1))
                    s_n = s_sc[h, 2 * i + c]
                    b_ = bh[:, cs]
                    eb = jnp.exp(b_)
                    bl = b_[:, CHUNK - 1:CHUNK]
                    ek = jnp.exp(bl - b_)
                    ebl = jnp.exp(bl)
                    qt, kt = qh[:, cs] * eb, kh[:, cs] * ek
                    do_c = doh[:, cs].astype(BF16)
                    dsb = ds_h.astype(BF16)
                    dv_c[c] = _dot(dsb, kt.astype(BF16), TN)
                    dkt = _dot(dsb, vh[:, cs].astype(BF16), NN)
                    dqt = _dot(s_n.astype(BF16), do_c, NN)
                    dbl_c[c] = jnp.sum(ds_h * s_n, axis=1, keepdims=True) * ebl + jnp.sum(dkt * kt, axis=1, keepdims=True)
                    dq_c[c], dk_c[c] = dqt * eb, dkt * ek
                    ds_h = ebl * ds_h + _dot(qt.astype(BF16), do_c, NT)
                new_ds.append(ds_h)

                att0 = jnp.sum(qh * kh, axis=0, keepdims=True)
                datt0 = jnp.sum(doh * vh, axis=0, keepdims=True)
                dqh = jnp.concatenate(dq_c, axis=1) + datt0 * kh
                dkh = jnp.concatenate(dk_c, axis=1) + datt0 * qh
                dvh = jnp.concatenate(dv_c, axis=1) + att0 * doh
                for dlt in range(1, CHUNK):
                    kr, br, vr = pltpu.roll(kh, dlt, 1), pltpu.roll(bh, dlt, 1), pltpu.roll(vh, dlt, 1)
                    e = jnp.where(lane64 >= dlt, jnp.exp(jnp.minimum(bh - br, 0.0)), 0.0)
                    qe = qh * e
                    att = jnp.sum(qe * kr, axis=0, keepdims=True)
                    datt = jnp.sum(doh * vr, axis=0, keepdims=True)
                    dqh = dqh + datt * (kr * e)
                    dkh = dkh + pltpu.roll(datt * qe, LANES - dlt, 1)
                    dvh = dvh + pltpu.roll(att * doh, LANES - dlt, 1)
                dbl = jnp.where(half, dbl_c[0], dbl_c[1])
                db_h.append(qh * dqh - kh * dkh + jnp.where(lane64 == CHUNK - 1, dbl, 0.0))
                dq_h.append(dqh)
                dk_h.append(dkh)
                dv_h.append(dvh)
            dqq = jnp.concatenate(dq_h, axis=0).T
            dkk = jnp.concatenate(dk_h, axis=0).T
            dvv = jnp.concatenate(dv_h, axis=0).T
            dlf = _dot(jnp.concatenate(db_h, axis=0), umat, NT, precision=HI).T
            dqa = dqq * _dsilu(qa)
            dfg = jnp.where(fg > TINY, dlf / fg, 0.0)
            dz = (dfg - dkk) * (1.0 - lb) * sg * sgn
            dlb_acc = dlb_acc + jnp.sum(dfg * (1.0 - sg) - dkk * sgn, axis=0, keepdims=True)
            da_ref[pl.ds(r0, LANES), :] = jnp.concatenate([dqa, dz, dvv, dga], axis=1)
            return new_ds[0], new_ds[1], dgn_acc, dlb_acc

        zrow = jnp.zeros((1, LANES), F32)
        _, _, dgn_acc, dlb_acc = lax.fori_loop(0, nt, bwd_tile, (zero, zero, zrow, zrow))
        dgn_ref[...] = jnp.broadcast_to(dgn_acc, (8, LANES))
        dlb_ref[...] = jnp.broadcast_to(dlb_acc, (8, LANES))

    rows = jax.ShapeDtypeStruct((bsz, 8, HGRN_W), F32)
    return pl.pallas_call(
        body, grid=(bsz, 2),
        in_specs=[pl.BlockSpec((None, t, 512), lambda b, p: (b, 0, p)),
                  pl.BlockSpec((None, t, 128), lambda b, p: (b, 0, p)),
                  pl.BlockSpec((None, t, 128), lambda b, p: (b, 0, p)),
                  pl.BlockSpec((1, 128), lambda b, p: (0, p)),
                  pl.BlockSpec((1, 128), lambda b, p: (0, p))],
        out_specs=[pl.BlockSpec((None, t, 512), lambda b, p: (b, 0, p)),
                   pl.BlockSpec((None, 8, 128), lambda b, p: (b, 0, p)),
                   pl.BlockSpec((None, 8, 128), lambda b, p: (b, 0, p))],
        out_shape=[jax.ShapeDtypeStruct((bsz, t, A_W), F32), rows, rows],
        scratch_shapes=[pltpu.VMEM((2, nchunk, CHUNK, CHUNK), F32)],
        compiler_params=_cparams(("parallel", "parallel")), name=name)(proj3, o_raw, dmixed, lbs_row, gn_row)


def _pool_tt(t):
    return min(256, t)


def _window_select(s2, s4, s8, s16, lane):
    return jnp.where(lane < 64, s2, jnp.where(lane < 128, s4, jnp.where(lane < 192, s8, s16)))


def _pool_counts(t0, tt):
    lane = _iota((tt, POOL_W), 1)
    tpos = (_iota((tt, POOL_W), 0) + t0 + 1).astype(F32)
    win = jnp.where(lane < 64, 2.0, jnp.where(lane < 128, 4.0, jnp.where(lane < 192, 8.0, 16.0)))
    return 1.0 / jnp.minimum(tpos, win), lane


def _pooled_tile(upad_ref, i, tt):
    r0 = pl.multiple_of(i * tt, 8)
    cat = upad_ref[pl.ds(r0, tt + POOL_HALO), :]
    s2 = cat + pltpu.roll(cat, 1, 0)
    s4 = s2 + pltpu.roll(s2, 2, 0)
    s8 = s4 + pltpu.roll(s4, 4, 0)
    s16 = s8 + pltpu.roll(s8, 8, 0)
    inv, lane = _pool_counts(i * tt, tt)
    sel = _window_select(s2[POOL_HALO:], s4[POOL_HALO:], s8[POOL_HALO:], s16[POOL_HALO:], lane)
    return sel * inv - cat[POOL_HALO:], inv, lane


def _pool_fwd(proj3, wbd, scale_row, name):
    bsz, t, _ = proj3.shape
    tt = _pool_tt(t)

    def body(p_ref, w_ref, sc_ref, o_ref, upad):
        upad[0:POOL_HALO, :] = jnp.zeros((POOL_HALO, POOL_W), F32)
        upad[POOL_HALO:, :] = p_ref[:, 0:POOL_W]
        w = w_ref[...]
        sc = sc_ref[...]

        def tile(i, c):
            pooled, _, _ = _pooled_tile(upad, i, tt)
            r0 = pl.multiple_of(i * tt, 8)
            g = p_ref[pl.ds(r0, tt), POOL_W:2 * POOL_W]
            pre = jnp.dot(pooled.astype(BF16), w, preferred_element_type=F32)
            o_ref[pl.ds(r0, tt), :] = pre * sc * _silu(g)
            return c

        lax.fori_loop(0, t // tt, tile, 0)

    return pl.pallas_call(
        body, grid=(bsz,),
        in_specs=[pl.BlockSpec((None, t, 512), lambda b: (b, 0, B_BLK)),
                  pl.BlockSpec((POOL_W, POOL_W), lambda b: (0, 0)),
                  pl.BlockSpec((1, POOL_W), lambda b: (0, 0))],
        out_specs=pl.BlockSpec((None, t, POOL_W), lambda b: (b, 0, 0)),
        out_shape=jax.ShapeDtypeStruct((bsz, t, POOL_W), F32),
        scratch_shapes=[pltpu.VMEM((t + POOL_HALO, POOL_W), F32)],
        compiler_params=_cparams(("parallel",)), name=name)(proj3, wbd, scale_row)


def _pool_bwd(proj3, dmixed, wbd, scale_row, name):
    bsz, t, _ = proj3.shape
    tt = _pool_tt(t)

    def body(p_ref, do_ref, w_ref, sc_ref, db_ref, dsc_ref, dw_ref, upad, epad):
        upad[0:POOL_HALO, :] = jnp.zeros((POOL_HALO, POOL_W), F32)
        upad[POOL_HALO:, :] = p_ref[:, 0:POOL_W]
        epad[t:, :] = jnp.zeros((POOL_HALO, POOL_W), F32)
        w = w_ref[...]
        sc = sc_ref[...]

        def tile(i, carry):
            dsc_acc, dw_acc = carry
            pooled, inv, _ = _pooled_tile(upad, i, tt)
            r0 = pl.multiple_of(i * tt, 8)
            g = p_ref[pl.ds(r0, tt), POOL_W:2 * POOL_W]
            dout = do_ref[pl.ds(r0, tt), :]
            pb = pooled.astype(BF16)
            pre = jnp.dot(pb, w, preferred_element_type=F32)
            t1 = dout * _silu(g)
            dsc_acc = dsc_acc + jnp.sum(t1 * pre, axis=0, keepdims=True)
            dpre = (t1 * sc).astype(BF16)
            db_ref[pl.ds(r0, tt), POOL_W:2 * POOL_W] = dout * pre * sc * _dsilu(g)
            dw_acc = dw_acc + _dot(pb, dpre, TN)
            dpooled = _dot(dpre, w, NT)
            epad[pl.ds(r0, tt), :] = dpooled * inv
            return dsc_acc, dw_acc

        dsc_acc, dw_acc = lax.fori_loop(0, t // tt, tile, (jnp.zeros((1, POOL_W), F32), jnp.zeros((POOL_W, POOL_W), F32)))
        dsc_ref[...] = jnp.broadcast_to(dsc_acc, (8, POOL_W))
        dw_ref[...] = dw_acc

        def tile2(i, c):
            r0 = pl.multiple_of(i * tt, 8)
            n = tt + POOL_HALO
            cat = epad[pl.ds(r0, n), :]
            s2 = cat + pltpu.roll(cat, n - 1, 0)
            s4 = s2 + pltpu.roll(s2, n - 2, 0)
            s8 = s4 + pltpu.roll(s4, n - 4, 0)
            s16 = s8 + pltpu.roll(s8, n - 8, 0)
            inv, lane = _pool_counts(i * tt, tt)
            sel = _window_select(s2[:tt], s4[:tt], s8[:tt], s16[:tt], lane)
            db_ref[pl.ds(r0, tt), 0:POOL_W] = sel - cat[:tt] / inv
            return c

        lax.fori_loop(0, t // tt, tile2, 0)

    return pl.pallas_call(
        body, grid=(bsz,),
        in_specs=[pl.BlockSpec((None, t, 512), lambda b: (b, 0, B_BLK)),
                  pl.BlockSpec((None, t, POOL_W), lambda b: (b, 0, 1)),
                  pl.BlockSpec((POOL_W, POOL_W), lambda b: (0, 0)),
                  pl.BlockSpec((1, POOL_W), lambda b: (0, 0))],
        out_specs=[pl.BlockSpec((None, t, 512), lambda b: (b, 0, 0)),
                   pl.BlockSpec((None, 8, POOL_W), lambda b: (b, 0, 0)),
                   pl.BlockSpec((None, POOL_W, POOL_W), lambda b: (b, 0, 0))],
        out_shape=[jax.ShapeDtypeStruct((bsz, t, B_W), F32), jax.ShapeDtypeStruct((bsz, 8, POOL_W), F32),
                   jax.ShapeDtypeStruct((bsz, POOL_W, POOL_W), F32)],
        scratch_shapes=[pltpu.VMEM((t + POOL_HALO, POOL_W), F32), pltpu.VMEM((t + POOL_HALO, POOL_W), F32)],
        compiler_params=_cparams(("parallel",)), name=name)(proj3, dmixed, wbd, scale_row)


def _head_select_rows(hp):
    r, c = _iota((8, LANES), 0), _iota((8, LANES), 1)
    return ((r < 2) & (c == 2 * hp + r)).astype(F32)


def _foxgate_fwd(proj3, bias_row, name):
    bsz, t, _ = proj3.shape
    nt = t // LANES

    def body(f_ref, b_ref, cn_ref, ct_ref):
        bias = b_ref[...]
        i, j = _iota((LANES, LANES), 0), _iota((LANES, LANES), 1)
        lower = (j <= i).astype(F32)
        spread = (_iota((LANES, FOX_W), 0) == _iota((LANES, FOX_W), 1) // 64).astype(F32)

        def tile(k, carry):
            r0 = pl.multiple_of(k * LANES, LANES)
            xg = f_ref[pl.ds(r0, LANES), :] + bias
            lf = jnp.minimum(xg, 0.0) - jnp.log(1.0 + jnp.exp(-jnp.abs(xg)))
            c = jnp.dot(lower, lf, precision=HI, preferred_element_type=F32) + carry
            cn_ref[pl.ds(r0, LANES), :] = jnp.dot(c, spread, precision=HI, preferred_element_type=F32)
            for hp in range(4):
                ct_ref[hp, :, pl.ds(r0, LANES)] = _dot(_head_select_rows(hp), c, NT, precision=HI)
            return c[LANES - 1:LANES, :]

        lax.fori_loop(0, nt, tile, jnp.zeros((1, LANES), F32))

    return pl.pallas_call(
        body, grid=(bsz,),
        in_specs=[pl.BlockSpec((None, t, 128), lambda b: (b, 0, F_BLK)), pl.BlockSpec((1, 128), lambda b: (0, 0))],
        out_specs=[pl.BlockSpec((None, t, FOX_W), lambda b: (b, 0, 0)),
                   pl.BlockSpec((None, 4, 8, t), lambda b: (b, 0, 0, 0))],
        out_shape=[jax.ShapeDtypeStruct((bsz, t, FOX_W), F32), jax.ShapeDtypeStruct((bsz, 4, 8, t), F32)],
        compiler_params=_cparams(("parallel",)), name=name)(proj3, bias_row)


def _foxgate_bwd(proj3, dc_nat, bias_row, name):
    bsz, t, _ = proj3.shape
    nt = t // LANES

    def body(f_ref, dc_ref, b_ref, df_ref, dbias_ref, run_sc):
        bias = b_ref[...]
        i, j = _iota((LANES, LANES), 0), _iota((LANES, LANES), 1)
        upper = (j >= i).astype(F32)
        valid = _iota((1, LANES), 1) < FOX_HEADS
        run_sc[...] = jnp.zeros((8, LANES), F32)
        dbias_ref[...] = jnp.zeros((8, LANES), F32)

        def tile(k, c):
            r0 = pl.multiple_of((nt - 1 - k) * LANES, LANES)
            dc = dc_ref[pl.ds(r0, LANES), :] + jnp.where(i == LANES - 1, run_sc[0:1, :], 0.0)
            dlf = jnp.dot(upper, dc, precision=HI, preferred_element_type=F32)
            xg = f_ref[pl.ds(r0, LANES), :] + bias
            df = jnp.where(valid, dlf * _sig(-xg), 0.0)
            df_ref[pl.ds(r0, LANES), :] = df
            run_sc[...] = dlf[0:8, :]
            dbias_ref[...] += jnp.sum(df, axis=0, keepdims=True)
            return c

        lax.fori_loop(0, nt, tile, 0)

    blk = pl.BlockSpec((None, t, 128), lambda b: (b, 0, 0))
    return pl.pallas_call(
        body, grid=(bsz,),
        in_specs=[pl.BlockSpec((None, t, 128), lambda b: (b, 0, F_BLK)), blk, pl.BlockSpec((1, 128), lambda b: (0, 0))],
        out_specs=[blk, pl.BlockSpec((None, 8, 128), lambda b: (b, 0, 0))],
        out_shape=[jax.ShapeDtypeStruct((bsz, t, F_W), F32), jax.ShapeDtypeStruct((bsz, 8, 128), F32)],
        scratch_shapes=[pltpu.VMEM((8, LANES), F32)],
        compiler_params=_cparams(("parallel",)), name=name)(proj3, dc_nat, bias_row)


def _fox_tile(t):
    return min(256, t)


def _fox_fwd(proj3, c_nat, c_t, name):
    bsz, t, _ = proj3.shape
    tq = _fox_tile(t)
    nq = t // tq

    def body(q_ref, kv_ref, cn_ref, ct_ref, og_ref, or_ref, lse_ref):
        i = pl.program_id(2)
        qblk = q_ref[...]
        rows = _iota((tq, tq), 0) + i * tq
        outs, lses = [], []
        for h in range(2):
            hs = slice(64 * h, 64 * (h + 1))
            qh = (qblk[:, hs] * 0.125).astype(BF16)
            cq = cn_ref[:, 64 * h:64 * h + 1]

            def kv_step(j, carry, h=h, qh=qh, cq=cq):
                m, l, acc = carry
                c0 = pl.multiple_of(j * tq, tq)
                kh = kv_ref[pl.ds(c0, tq), 128 + 64 * h:128 + 64 * (h + 1)].astype(BF16)
                vh = kv_ref[pl.ds(c0, tq), 256 + 64 * h:256 + 64 * (h + 1)].astype(BF16)
                ck = ct_ref[h:h + 1, pl.ds(c0, tq)]
                s = _dot(qh, kh, NT) + (cq - ck)
                s = jnp.where(rows >= _iota((tq, tq), 1) + j * tq, s, MASK_VALUE)
                m_new = jnp.maximum(m, jnp.max(s, axis=1, keepdims=True))
                alpha = jnp.exp(m - m_new)
                p = jnp.exp(s - m_new)
                l = alpha * l + jnp.sum(p, axis=1, keepdims=True)
                acc = alpha * acc + jnp.dot(p.astype(BF16), vh, preferred_element_type=F32)
                return m_new, l, acc

            init = (jnp.full((tq, 1), MASK_VALUE, F32), jnp.zeros((tq, 1), F32), jnp.zeros((tq, 64), F32))
            m, l, acc = lax.fori_loop(0, i + 1, kv_step, init)
            outs.append(acc / l)
            lses.append(jnp.broadcast_to(m + jnp.log(l), (tq, 64)))
        o = jnp.concatenate(outs, axis=1)
        or_ref[...] = o
        og_ref[...] = o * _silu(qblk[:, 384:512])
        lse_ref[...] = jnp.concatenate(lses, axis=1)

    out = jax.ShapeDtypeStruct((bsz, t, FOX_W), F32)
    blk = pl.BlockSpec((None, tq, 128), lambda b, p, i: (b, i, p))
    return pl.pallas_call(
        body, grid=(bsz, 4, nq),
        in_specs=[pl.BlockSpec((None, tq, 512), lambda b, p, i: (b, i, C_BLK0 + p)),
                  pl.BlockSpec((None, t, 512), lambda b, p, i: (b, 0, C_BLK0 + p)),
                  blk,
                  pl.BlockSpec((None, None, 8, t), lambda b, p, i: (b, p, 0, 0))],
        out_specs=[blk, blk, blk],
        out_shape=[out, out, out],
        compiler_params=_cparams(("parallel", "parallel", "arbitrary")), name=name)(proj3, proj3, c_nat, c_t)


def _fox_bwd(proj3, o_raw, dmixed, lse, c_nat, c_t, name):
    bsz, t, _ = proj3.shape
    tq = _fox_tile(t)
    nq = t // tq

    def body(a_ref, or_ref, do_ref, lse_ref, cn_ref, ct_ref, dc_out, dct_out, drow_out, dq_sc, do_sc, dl_sc):
        def prep(i, c):
            r0 = pl.multiple_of(i * tq, tq)
            g = a_ref[pl.ds(r0, tq), 384:512]
            dout = do_ref[pl.ds(r0, tq), :]
            o = or_ref[pl.ds(r0, tq), :]
            dc_out[pl.ds(r0, tq), 384:512] = dout * o * _dsilu(g)
            do = dout * _silu(g)
            do_sc[pl.ds(r0, tq), :] = do
            prod = do * o
            d0 = jnp.sum(prod[:, 0:64], axis=1, keepdims=True)
            d1 = jnp.sum(prod[:, 64:128], axis=1, keepdims=True)
            dl_sc[pl.ds(r0, tq), :] = jnp.concatenate([jnp.broadcast_to(d0, (tq, 64)), jnp.broadcast_to(d1, (tq, 64))], axis=1)
            dq_sc[pl.ds(r0, tq), :] = jnp.zeros((tq, 128), F32)
            drow_out[pl.ds(r0, tq), :] = jnp.zeros((tq, 128), F32)
            return c

        lax.fori_loop(0, nq, prep, 0)
        dct_out[...] = jnp.zeros((8, t), F32)

        def kv_tile(j, c):
            c0 = pl.multiple_of(j * tq, tq)
            cols = _iota((tq, tq), 1) + j * tq
            dks, dvs = [], []
            for h in range(2):
                kh = a_ref[pl.ds(c0, tq), 128 + 64 * h:128 + 64 * (h + 1)].astype(BF16)
                vh = a_ref[pl.ds(c0, tq), 256 + 64 * h:256 + 64 * (h + 1)].astype(BF16)
                ck = ct_ref[h:h + 1, pl.ds(c0, tq)]

                def q_step(i, carry, h=h, kh=kh, vh=vh, ck=ck):
                    dk, dv, dcol = carry
                    r0 = pl.multiple_of(i * tq, tq)
                    hs = slice(64 * h, 64 * (h + 1))
                    qh = (a_ref[pl.ds(r0, tq), hs] * 0.125).astype(BF16)
                    doh = do_sc[pl.ds(r0, tq), hs].astype(BF16)
                    lse_h = lse_ref[pl.ds(r0, tq), 64 * h:64 * h + 1]
                    dl_h = dl_sc[pl.ds(r0, tq), 64 * h:64 * h + 1]
                    cq = cn_ref[pl.ds(r0, tq), 64 * h:64 * h + 1]
                    s = _dot(qh, kh, NT) + (cq - ck)
                    p = jnp.where(_iota((tq, tq), 0) + i * tq >= cols, jnp.exp(s - lse_h), 0.0)
                    dv = dv + _dot(p.astype(BF16), doh, TN)
                    dp = _dot(doh, vh, NT)
                    ds = p * (dp - dl_h)
                    dsb = ds.astype(BF16)
                    dq_sc[pl.ds(r0, tq), hs] += jnp.dot(dsb, kh, preferred_element_type=F32) * 0.125
                    dk = dk + _dot(dsb, qh, TN)
                    dcol = dcol - jnp.sum(ds, axis=0, keepdims=True)
                    drow_out[pl.ds(r0, tq), hs] += jnp.broadcast_to(jnp.sum(ds, axis=1, keepdims=True), (tq, 64))
                    return dk, dv, dcol

                init = (jnp.zeros((tq, 64), F32), jnp.zeros((tq, 64), F32), jnp.zeros((1, tq), F32))
                dk, dv, dcol = lax.fori_loop(j, nq, q_step, init)
                dks.append(dk)
                dvs.append(dv)
                dct_out[h:h + 1, pl.ds(c0, tq)] = dcol
            dc_out[pl.ds(c0, tq), 128:256] = jnp.concatenate(dks, axis=1)
            dc_out[pl.ds(c0, tq), 256:384] = jnp.concatenate(dvs, axis=1)
            return c

        lax.fori_loop(0, nq, kv_tile, 0)
        dc_out[:, 0:128] = dq_sc[...]

    blk = pl.BlockSpec((None, t, 128), lambda b, p: (b, 0, p))
    return pl.pallas_call(
        body, grid=(bsz, 4),
        in_specs=[pl.BlockSpec((None, t, 512), lambda b, p: (b, 0, C_BLK0 + p)),
                  blk,
                  pl.BlockSpec((None, t, 128), lambda b, p: (b, 0, 4 + p)),
                  blk, blk,
                  pl.BlockSpec((None, None, 8, t), lambda b, p: (b, p, 0, 0))],
        out_specs=[pl.BlockSpec((None, t, 512), lambda b, p: (b, 0, p)),
                   pl.BlockSpec((None, None, 8, t), lambda b, p: (b, p, 0, 0)), blk],
        out_shape=[jax.ShapeDtypeStruct((bsz, t, C_W), F32), jax.ShapeDtypeStruct((bsz, 4, 8, t), F32),
                   jax.ShapeDtypeStruct((bsz, t, FOX_W), F32)],
        scratch_shapes=[pltpu.VMEM((t, 128), F32), pltpu.VMEM((t, 128), F32), pltpu.VMEM((t, 128), F32)],
        compiler_params=_cparams(("parallel", "parallel")), name=name)(proj3, o_raw, dmixed, lse, c_nat, c_t)


def _mix_tm(n):
    return min(512, n)


def _outproj_fwd(x2, oa, ob, oc, wo, g_row, name):
    n, d = x2.shape
    tm = _mix_tm(n)

    def body(x_ref, oa_ref, ob_ref, oc_ref, w_ref, g_ref, y_ref, xo_ref):
        y = (jnp.dot(oa_ref[...].astype(BF16), w_ref[0:256, :], preferred_element_type=F32)
             + jnp.dot(ob_ref[...].astype(BF16), w_ref[256:512, :], preferred_element_type=F32)
             + jnp.dot(oc_ref[...].astype(BF16), w_ref[512:1024, :], preferred_element_type=F32))
        y_ref[...] = y
        xo_ref[...] = x_ref[...] + y * _rstd(y) * g_ref[...]

    row = lambda w: pl.BlockSpec((tm, w), lambda i: (i, 0))
    out = jax.ShapeDtypeStruct((n, d), F32)
    return pl.pallas_call(
        body, grid=(n // tm,),
        in_specs=[row(d), row(256), row(256), row(512), pl.BlockSpec((d, d), lambda i: (0, 0)),
                  pl.BlockSpec((1, d), lambda i: (0, 0))],
        out_specs=[row(d), row(d)], out_shape=[out, out],
        compiler_params=_cparams(("parallel",)), name=name)(x2, oa, ob, oc, wo, g_row)


def _loss_head(x2, target2, name):
    n, d = x2.shape
    tm = _mix_tm(n)

    def body(x_ref, t_ref, dx_ref, l_ref):
        err = x_ref[...] - t_ref[...]
        dx_ref[...] = err * (1.0 / d)

        @pl.when(pl.program_id(0) == 0)
        def _():
            l_ref[...] = jnp.zeros((8, 128), F32)

        l_ref[...] += jnp.sum(err * err)

    row = pl.BlockSpec((tm, d), lambda i: (i, 0))
    return pl.pallas_call(
        body, grid=(n // tm,), in_specs=[row, row],
        out_specs=[row, pl.BlockSpec((8, 128), lambda i: (0, 0))],
        out_shape=[jax.ShapeDtypeStruct((n, d), F32), jax.ShapeDtypeStruct((8, 128), F32)],
        compiler_params=_cparams(("arbitrary",)), name=name)(x2, target2)


def _outproj_bwd(dxo, y, oa, ob, oc, wo, g_row, name):
    n, d = dxo.shape
    tm = _mix_tm(n)

    def body(dx_ref, y_ref, oa_ref, ob_ref, oc_ref, w_ref, g_ref, dm_ref, dw_ref, dg_ref):
        @pl.when(pl.program_id(0) == 0)
        def _():
            dw_ref[...] = jnp.zeros((d, d), F32)
            dg_ref[...] = jnp.zeros((8, d), F32)

        yv, dx = y_ref[...], dx_ref[...]
        r = _rstd(yv)
        yn = yv * r
        dg_ref[...] += jnp.sum(dx * yn, axis=0, keepdims=True)
        dyn = dx * g_ref[...]
        dy = (r * (dyn - yn * jnp.mean(dyn * yn, axis=-1, keepdims=True))).astype(BF16)
        dm_ref[...] = _dot(dy, w_ref[...], NT)
        dw_ref[0:256, :] += _dot(oa_ref[...].astype(BF16), dy, TN)
        dw_ref[256:512, :] += _dot(ob_ref[...].astype(BF16), dy, TN)
        dw_ref[512:1024, :] += _dot(oc_ref[...].astype(BF16), dy, TN)

    row = lambda w: pl.BlockSpec((tm, w), lambda i: (i, 0))
    fixed = lambda r, c: pl.BlockSpec((r, c), lambda i: (0, 0))
    return pl.pallas_call(
        body, grid=(n // tm,),
        in_specs=[row(d), row(d), row(256), row(256), row(512), fixed(d, d), fixed(1, d)],
        out_specs=[row(d), fixed(d, d), fixed(8, d)],
        out_shape=[jax.ShapeDtypeStruct((n, d), F32), jax.ShapeDtypeStruct((d, d), F32), jax.ShapeDtypeStruct((8, d), F32)],
        compiler_params=_cparams(("arbitrary",)), name=name)(dxo, y, oa, ob, oc, wo, g_row)


_PIECES = ((0, A_W), (A_W, B_W), (A_W + B_W, C_W), (A_W + B_W + C_W, F_W))


def _inproj_bwd_x(x2, dxo, g_row, w_int, pieces, name):
    n, d = x2.shape
    tm = min(256, n)

    def body(x_ref, dxo_ref, g_ref, w_ref, da_ref, db_ref, dc_ref, df_ref, dx_ref, dg_ref):
        @pl.when(pl.program_id(0) == 0)
        def _():
            dg_ref[...] = jnp.zeros((8, d), F32)

        dh = jnp.zeros((tm, d), F32)
        for ref, (o, w) in zip((da_ref, db_ref, dc_ref, df_ref), _PIECES):
            dh = dh + _dot(ref[...].astype(BF16), w_ref[:, o:o + w], NT)
        x = x_ref[...]
        r = _rstd(x)
        xn = x * r
        dg_ref[...] += jnp.sum(dh * xn, axis=0, keepdims=True)
        dxn = dh * g_ref[...]
        dx_ref[...] = dxo_ref[...] + r * (dxn - xn * jnp.mean(dxn * xn, axis=-1, keepdims=True))

    row = lambda w: pl.BlockSpec((tm, w), lambda i: (i, 0))
    fixed = lambda r, c: pl.BlockSpec((r, c), lambda i: (0, 0))
    return pl.pallas_call(
        body, grid=(n // tm,),
        in_specs=[row(d), row(d), fixed(1, d), fixed(d, E_INT)] + [row(w) for _, w in _PIECES],
        out_specs=[row(d), fixed(8, d)],
        out_shape=[jax.ShapeDtypeStruct((n, d), F32), jax.ShapeDtypeStruct((8, d), F32)],
        compiler_params=_cparams(("arbitrary",)), name=name)(x2, dxo, g_row, w_int, *pieces)


def _inproj_bwd_w(x2, g_row, piece, name):
    n, d = x2.shape
    w = piece.shape[1]
    tm = min(512, n)

    def body(x_ref, g_ref, dp_ref, dw_ref):
        @pl.when(pl.program_id(0) == 0)
        def _():
            dw_ref[...] = jnp.zeros((d, w), F32)

        x = x_ref[...]
        h = (x * _rstd(x) * g_ref[...]).astype(BF16)
        dw_ref[...] += _dot(h, dp_ref[...].astype(BF16), TN)

    return pl.pallas_call(
        body, grid=(n // tm,),
        in_specs=[pl.BlockSpec((tm, d), lambda i: (i, 0)), pl.BlockSpec((1, d), lambda i: (0, 0)),
                  pl.BlockSpec((tm, w), lambda i: (i, 0))],
        out_specs=pl.BlockSpec((d, w), lambda i: (0, 0)),
        out_shape=jax.ShapeDtypeStruct((d, w), F32),
        compiler_params=_cparams(("arbitrary",)), name=name)(x2, g_row, piece)


def _block_diag(pool_w_l):
    z = jnp.zeros((64, 64), pool_w_l.dtype)
    return jnp.concatenate(
        [jnp.concatenate([pool_w_l[g] if c == g else z for c in range(4)], axis=1) for g in range(4)], axis=0)


def _pad_lanes(v, width=128):
    return jnp.pad(v, ((0, 0),) * (v.ndim - 1) + ((0, width - v.shape[-1]),))


def _local_step(x, target, lower_bounds, pre_norm_g, w_in_int, hgrn_norm_g, fox_f_bias, pool_w, pool_scale,
                w_out_bf, post_norm_g):
    bsz, t, d = x.shape
    n = bsz * t
    lbs = _lbs_fwd(lower_bounds)
    saved = []
    xc = x.reshape(n, d)
    for l in range(DEPTH):
        proj = _inproj_fwd(xc, pre_norm_g[l:l + 1], w_in_int[l], f"inproj_fwd{l}").reshape(bsz, t, E_INT)
        wbd = _block_diag(pool_w[l]).astype(BF16)
        bias_row = _pad_lanes(fox_f_bias[l:l + 1])
        oa, oa_raw = _hgrn_fwd(proj, lbs[l:l + 1], hgrn_norm_g[l].reshape(HGRN_W, 1), f"hgrn_fwd{l}")
        ob = _pool_fwd(proj, wbd, pool_scale[l:l + 1], f"pool_fwd{l}")
        c_nat, c_t = _foxgate_fwd(proj, bias_row, f"foxgate_fwd{l}")
        oc, oc_raw, lse = _fox_fwd(proj, c_nat, c_t, f"fox_fwd{l}")
        y, xn = _outproj_fwd(xc, oa.reshape(n, -1), ob.reshape(n, -1), oc.reshape(n, -1), w_out_bf[l],
                             post_norm_g[l:l + 1], f"outproj_fwd{l}")
        saved.append((xc, proj, wbd, bias_row, oa, oa_raw, ob, oc, oc_raw, lse, c_nat, c_t, y))
        xc = xn
    dx, sq = _loss_head(xc, target.reshape(n, d), "loss_head")
    g = {k: [None] * DEPTH for k in ("pre", "w_in", "hgn", "bias", "pool_w", "pool_scale", "w_out", "post", "lbs")}
    for l in reversed(range(DEPTH)):
        xin, proj, wbd, bias_row, oa, oa_raw, ob, oc, oc_raw, lse, c_nat, c_t, y = saved[l]
        dmix, g["w_out"][l], dpost = _outproj_bwd(dx, y, oa.reshape(n, -1), ob.reshape(n, -1), oc.reshape(n, -1),
                                                  w_out_bf[l], post_norm_g[l:l + 1], f"outproj_bwd{l}")
        g["post"][l] = dpost[0]
        dmix3 = dmix.reshape(bsz, t, d)
        d_c, dct, drow = _fox_bwd(proj, oc_raw, dmix3, lse, c_nat, c_t, f"fox_bwd{l}")
        dc_nat = _pad_lanes(dct[:, :, 0:2, :].reshape(bsz, FOX_HEADS, t).transpose(0, 2, 1)
                            + drow.reshape(bsz, t, FOX_HEADS, 64)[..., 0])
        d_f, dbias = _foxgate_bwd(proj, dc_nat, bias_row, f"foxgate_bwd{l}")
        g["bias"][l] = jnp.sum(dbias[:, 0, :FOX_HEADS], axis=0)
        d_b, dscale, dwbd = _pool_bwd(proj, dmix3, wbd, pool_scale[l:l + 1], f"pool_bwd{l}")
        g["pool_scale"][l] = jnp.sum(dscale[:, 0], axis=0)
        dwbd = jnp.sum(dwbd, axis=0)
        g["pool_w"][l] = jnp.stack([dwbd[64 * k:64 * (k + 1), 64 * k:64 * (k + 1)] for k in range(4)])
        d_a, dgn, dlb = _hgrn_bwd(proj, oa_raw, dmix3, lbs[l:l + 1], hgrn_norm_g[l:l + 1], f"hgrn_bwd{l}")
        g["hgn"][l] = jnp.sum(dgn[:, 0], axis=0)
        g["lbs"][l] = jnp.sum(dlb[:, 0], axis=0)
        pieces = [p.reshape(n, -1) for p in (d_a, d_b, d_c, d_f)]
        g["w_in"][l] = jnp.concatenate(
            [_inproj_bwd_w(xin, pre_norm_g[l:l + 1], p, f"inproj_bwd_w{l}_{k}") for k, p in enumerate(pieces)], axis=1)
        dx, dpre = _inproj_bwd_x(xin, dx, pre_norm_g[l:l + 1], w_in_int[l], pieces, f"inproj_bwd_x{l}")
        g["pre"][l] = dpre[0]
    grads = {k: jnp.stack(v) for k, v in g.items()}
    return sq, dx.reshape(bsz, t, d), grads


def _place():
    return lax.axis_index("x"), lax.axis_index("y"), lax.axis_index("c")


def _other_chips(x, y):
    return [(1 - x, y), (x, 1 - y), (1 - x, 1 - y)]


_ANY = pl.BlockSpec(memory_space=pl.ANY)


def _gather_weights(w_in_sh, w_out_sh):
    def body(win_ref, wout_ref, ain_ref, aout_ref, send_sems, recv_sems, local_sems):
        x, y, c = _place()
        me = 2 * x + y
        mine = [pltpu.make_async_copy(win_ref, ain_ref.at[me], local_sems.at[0]),
                pltpu.make_async_copy(wout_ref, aout_ref.at[me], local_sems.at[1])]
        for cp in mine:
            cp.start()
        sends = []
        for k, (px, py) in enumerate(_other_chips(x, y)):
            for j, (src, dst) in enumerate(((win_ref, ain_ref), (wout_ref, aout_ref))):
                sends.append(pltpu.make_async_remote_copy(
                    src_ref=src, dst_ref=dst.at[me], send_sem=send_sems.at[2 * k + j], recv_sem=recv_sems.at[2 * k + j],
                    device_id=(px, py, c), device_id_type=MESH))
        for cp in sends:
            cp.start()
        for k, (px, py) in enumerate(_other_chips(x, y)):
            for j, (src, dst) in enumerate(((win_ref, ain_ref), (wout_ref, aout_ref))):
                pltpu.make_async_remote_copy(
                    src_ref=src, dst_ref=dst.at[2 * px + py], send_sem=send_sems.at[2 * k + j],
                    recv_sem=recv_sems.at[2 * k + j], device_id=(px, py, c), device_id_type=MESH).wait_recv()
        for cp in sends:
            cp.wait_send()
        for cp in mine:
            cp.wait()

    return pl.pallas_call(
        body, in_specs=[_ANY, _ANY], out_specs=[_ANY, _ANY],
        out_shape=[jax.ShapeDtypeStruct((N_CHIPS,) + w_in_sh.shape, w_in_sh.dtype),
                   jax.ShapeDtypeStruct((N_CHIPS,) + w_out_sh.shape, w_out_sh.dtype)],
        scratch_shapes=[pltpu.SemaphoreType.DMA((6,)), pltpu.SemaphoreType.DMA((6,)), pltpu.SemaphoreType.DMA((2,))],
        name="gather_weights")(w_in_sh, w_out_sh)


def _swap_with_sibling(parts, name):
    k = len(parts)

    def body(*refs):
        src, dst = refs[:k], refs[k:2 * k]
        send_sems, recv_sems = refs[2 * k:]
        x, y, c = _place()
        cps = [pltpu.make_async_remote_copy(src_ref=src[j], dst_ref=dst[j], send_sem=send_sems.at[j], recv_sem=recv_sems.at[j],
                                            device_id=(x, y, 1 - c), device_id_type=MESH) for j in range(k)]
        for cp in cps:
            cp.start()
        for cp in cps:
            cp.wait()

    return pl.pallas_call(
        body, in_specs=[_ANY] * k, out_specs=[_ANY] * k,
        out_shape=[jax.ShapeDtypeStruct(p.shape, p.dtype) for p in parts],
        scratch_shapes=[pltpu.SemaphoreType.DMA((k,)), pltpu.SemaphoreType.DMA((k,))], name=name)(*parts)


def _scatter_to_chips(parts, name):
    k = len(parts)

    def body(*refs):
        src, dst = refs[:k], refs[k:2 * k]
        send_sems, recv_sems = refs[2 * k:]
        x, y, c = _place()
        me = 2 * x + y
        cps = []
        for rel, (px, py) in enumerate(_other_chips(x, y)):
            for j in range(k):
                cps.append(pltpu.make_async_remote_copy(
                    src_ref=src[j].at[2 * px + py], dst_ref=dst[j].at[rel], send_sem=send_sems.at[rel * k + j],
                    recv_sem=recv_sems.at[rel * k + j], device_id=(px, py, c), device_id_type=MESH))
        for cp in cps:
            cp.start()
        for cp in cps:
            cp.wait()
        del me

    return pl.pallas_call(
        body, in_specs=[_ANY] * k, out_specs=[_ANY] * k,
        out_shape=[jax.ShapeDtypeStruct((3,) + p.shape[1:], p.dtype) for p in parts],
        scratch_shapes=[pltpu.SemaphoreType.DMA((3 * k,)), pltpu.SemaphoreType.DMA((3 * k,))], name=name)(*parts)


def _add_n(parts, name):
    r, c = parts[0].shape
    tr = 256 if r % 256 == 0 else r

    def body(*refs):
        acc = refs[0][...]
        for ref in refs[1:-1]:
            acc = acc + ref[...]
        refs[-1][...] = acc

    blk = pl.BlockSpec((tr, c), lambda i: (i, 0))
    return pl.pallas_call(
        body, grid=(r // tr,), in_specs=[blk] * len(parts), out_specs=blk,
        out_shape=jax.ShapeDtypeStruct((r, c), F32), compiler_params=_cparams(("parallel",)), name=name)(*parts)


def _all_reduce_small(packet):
    r, w = packet.shape

    def body(p_ref, o_ref, buf, send_sems, recv_sems):
        x, y, c = _place()
        me = 4 * x + 2 * y + c
        buf[me] = p_ref[...]
        peers = []
        for k in range(1, 8):
            fx, fy, fc = (k >> 2) & 1, (k >> 1) & 1, k & 1
            peers.append((x ^ fx, y ^ fy, c ^ fc))
        cps = [pltpu.make_async_remote_copy(src_ref=p_ref, dst_ref=buf.at[me], send_sem=send_sems.at[k], recv_sem=recv_sems.at[k],
                                            device_id=peer, device_id_type=MESH) for k, peer in enumerate(peers)]
        for cp in cps:
            cp.start()
        for k, (px, py, pc) in enumerate(peers):
            pltpu.make_async_remote_copy(src_ref=p_ref, dst_ref=buf.at[4 * px + 2 * py + pc], send_sem=send_sems.at[k],
                                         recv_sem=recv_sems.at[k], device_id=(px, py, pc), device_id_type=MESH).wait_recv()
        for cp in cps:
            cp.wait_send()
        acc = buf[0]
        for k in range(1, 8):
            acc = acc + buf[k]
        o_ref[...] = acc

    vm = pl.BlockSpec(memory_space=pltpu.VMEM)
    return pl.pallas_call(
        body, in_specs=[vm], out_specs=vm, out_shape=jax.ShapeDtypeStruct((r, w), F32),
        scratch_shapes=[pltpu.VMEM((8, r, w), F32), pltpu.SemaphoreType.DMA((7,)), pltpu.SemaphoreType.DMA((7,))],
        name="all_reduce_small")(packet)


def _adamw_math(w, g, m, v):
    m = ADAM_B1 * m + (1.0 - ADAM_B1) * g
    v = ADAM_B2 * v + (1.0 - ADAM_B2) * (g * g)
    m_hat = m / (1.0 - ADAM_B1 ** ADAM_STEP)
    v_hat = v / (1.0 - ADAM_B2 ** ADAM_STEP)
    return -ADAM_LR * (m_hat / (jnp.sqrt(v_hat) + ADAM_EPS) + ADAM_WD * w), m, v


def _adamw(w, g, m, v, name):
    nl, r, c = w.shape
    tr = 256 if r % 256 == 0 else r

    def body(w_ref, g_ref, m_ref, v_ref, d_ref, mo_ref, vo_ref):
        d_ref[...], mo_ref[...], vo_ref[...] = _adamw_math(w_ref[...], g_ref[...], m_ref[...], v_ref[...])

    blk = pl.BlockSpec((None, tr, c), lambda l, i: (l, i, 0))
    out = jax.ShapeDtypeStruct(w.shape, F32)
    return pl.pallas_call(
        body, grid=(nl, r // tr), in_specs=[blk] * 4, out_specs=[blk] * 3, out_shape=[out] * 3,
        compiler_params=_cparams(("parallel", "parallel")), name=name)(w, g, m, v)


def _small_update(gsum, lower_bounds, wpack, mpack, vpack):
    r, w = gsum.shape
    lb_rows = DEPTH * HGRN_W // 128

    def body(g_ref, a_ref, w_ref, m_ref, v_ref, go_ref, d_ref, mo_ref, vo_ref):
        a = a_ref[...]
        a0, a1 = a[0:1], a[1:2]
        mx = jnp.maximum(a0, a1)
        e0, e1 = jnp.exp(a0 - mx), jnp.exp(a1 - mx)
        p0, p1 = e0 / (e0 + e1), e1 / (e0 + e1)
        g = g_ref[...]
        half = lb_rows // 2
        dl0 = jnp.concatenate([g[k:k + 1] for k in range(half)], axis=1)
        dl1 = jnp.concatenate([g[half + k:half + k + 1] for k in range(half)], axis=1)
        dp0 = (dl0 + dl1) - (dl0 + dl1)
        dp1 = dl1
        inner = p0 * dp0 + p1 * dp1
        da0, da1 = p0 * (dp0 - inner), p1 * (dp1 - inner)
        rows = [da0[:, 128 * k:128 * (k + 1)] for k in range(half)] + [da1[:, 128 * k:128 * (k + 1)] for k in range(half)]
        gfull = jnp.concatenate(rows + [g[lb_rows:]], axis=0)
        go_ref[...] = gfull
        d_ref[...], mo_ref[...], vo_ref[...] = _adamw_math(w_ref[...], gfull, m_ref[...], v_ref[...])

    vm = pl.BlockSpec(memory_space=pltpu.VMEM)
    out = jax.ShapeDtypeStruct((r, w), F32)
    return pl.pallas_call(body, in_specs=[vm] * 5, out_specs=[vm] * 4, out_shape=[out] * 4, name="small_update")(
        gsum, lower_bounds, wpack, mpack, vpack)


_SMALL = ("lower_bounds", "pre_norm_g", "hgrn_norm_g", "fox_f_bias", "pool_w", "pool_scale", "post_norm_g")


def _pack(parts):
    rows = []
    for k in _SMALL:
        f = parts[k].reshape(-1)
        pad = (-f.shape[0]) % (8 * 128)
        rows.append(jnp.pad(f, (0, pad)).reshape(-1, 128))
    rows.append(jnp.zeros((8, 128), F32))
    return jnp.concatenate(rows, axis=0)


def _unpack(pack, like):
    out, r = {}, 0
    for k in _SMALL:
        size = int(np.prod(like[k].shape))
        nr = -(-size // (8 * 128)) * 8
        out[k] = pack[r:r + nr].reshape(-1)[:size].reshape(like[k].shape)
        r += nr
    return out, r


def kernel(x, lower_bounds, pre_norm_g, w_in, hgrn_norm_g, fox_f_bias, pool_w, pool_scale, w_out, post_norm_g, loss_target, m_lower_bounds, m_pre_norm_g, m_w_in, m_hgrn_norm_g, m_fox_f_bias, m_pool_w, m_pool_scale, m_w_out, m_post_norm_g, v_lower_bounds, v_pre_norm_g, v_w_in, v_hgrn_norm_g, v_fox_f_bias, v_pool_w, v_pool_scale, v_w_out, v_post_norm_g):
    cx, cy, cc = _place()
    chip = 2 * cx + cy

    ain, aout = _gather_weights(w_in.astype(BF16), w_out.astype(BF16))
    w_in_full = jnp.concatenate([ain[q] for q in range(N_CHIPS)], axis=-1)
    w_in_int = _to_internal(w_in_full)
    w_out_full = jnp.concatenate([aout[q] for q in range(N_CHIPS)], axis=1)

    sq, grad_x, g = _local_step(x, loss_target, lower_bounds, pre_norm_g, w_in_int, hgrn_norm_g, fox_f_bias, pool_w,
                                pool_scale, w_out_full, post_norm_g)

    gin = _to_original(g["w_in"])
    gin_blocks = jnp.stack([gin[:, :, SHARD_W * q:SHARD_W * (q + 1)] for q in range(N_CHIPS)])
    gout_blocks = g["w_out"].reshape(DEPTH, N_CHIPS, 256, D_MODEL).transpose(1, 0, 2, 3)
    take = lambda a, l: lax.dynamic_index_in_dim(a, l, axis=1, keepdims=False)
    mine_in, mine_out = take(gin_blocks, cc), take(gout_blocks, cc)
    sib_in, sib_out = _swap_with_sibling([take(gin_blocks, 1 - cc), take(gout_blocks, 1 - cc)], "grad_swap1")
    rin, rout = 4 * 1024, 4 * 256
    sum_in = _add_n([mine_in.reshape(rin, SHARD_W), sib_in.reshape(rin, SHARD_W)], "grad_add1_in").reshape(4, 1024, SHARD_W)
    sum_out = _add_n([mine_out.reshape(rout, D_MODEL), sib_out.reshape(rout, D_MODEL)], "grad_add1_out").reshape(4, 256, D_MODEL)
    got_in, got_out = _scatter_to_chips([sum_in, sum_out], "grad_scatter")
    own = lambda a: lax.dynamic_index_in_dim(a, chip, axis=0, keepdims=False)
    half_in = _add_n([own(sum_in)] + [got_in[k] for k in range(3)], "grad_add2_in")
    half_out = _add_n([own(sum_out)] + [got_out[k] for k in range(3)], "grad_add2_out")
    oth_in, oth_out = _swap_with_sibling([half_in, half_out], "grad_swap2")
    first = cc == 0
    grad_w_in = jnp.stack([jnp.where(first, half_in, oth_in), jnp.where(first, oth_in, half_in)])
    grad_w_out = jnp.stack([jnp.where(first, half_out, oth_out), jnp.where(first, oth_out, half_out)])

    small = {"lower_bounds": g["lbs"], "pre_norm_g": g["pre"], "hgrn_norm_g": g["hgn"], "fox_f_bias": g["bias"],
             "pool_w": g["pool_w"], "pool_scale": g["pool_scale"], "post_norm_g": g["post"]}
    packet = _pack(small)
    nrows = packet.shape[0]
    packet = packet.at[nrows - 1].set(sq[0])
    gsum = _all_reduce_small(packet)
    loss = gsum[nrows - 1, 0] * (0.5 / D_MODEL)

    weights = {"lower_bounds": lower_bounds, "pre_norm_g": pre_norm_g, "hgrn_norm_g": hgrn_norm_g,
               "fox_f_bias": fox_f_bias, "pool_w": pool_w, "pool_scale": pool_scale, "post_norm_g": post_norm_g}
    moments_m = {"lower_bounds": m_lower_bounds, "pre_norm_g": m_pre_norm_g, "hgrn_norm_g": m_hgrn_norm_g,
                 "fox_f_bias": m_fox_f_bias, "pool_w": m_pool_w, "pool_scale": m_pool_scale, "post_norm_g": m_post_norm_g}
    moments_v = {"lower_bounds": v_lower_bounds, "pre_norm_g": v_pre_norm_g, "hgrn_norm_g": v_hgrn_norm_g,
                 "fox_f_bias": v_fox_f_bias, "pool_w": v_pool_w, "pool_scale": v_pool_scale, "post_norm_g": v_post_norm_g}
    gp, dp, mp, vp = _small_update(gsum, lower_bounds, _pack(weights), _pack(moments_m), _pack(moments_v))
    gs, _ = _unpack(gp, weights)
    ds, _ = _unpack(dp, weights)
    ms, _ = _unpack(mp, weights)
    vs, _ = _unpack(vp, weights)

    d_in, m_in, v_in = _adamw(w_in, grad_w_in, m_w_in, v_w_in, "adamw_w_in")
    d_out, m_out, v_out = _adamw(w_out, grad_w_out, m_w_out, v_w_out, "adamw_w_out")

    def ordered(s, big_in, big_out):
        return (s["lower_bounds"], s["pre_norm_g"], big_in, s["hgrn_norm_g"], s["fox_f_bias"], s["pool_w"],
                s["pool_scale"], big_out, s["post_norm_g"])

    return (loss, grad_x, *ordered(gs, grad_w_in, grad_w_out), *ordered(ds, d_in, d_out),
            *ordered(ms, m_in, m_out), *ordered(vs, v_in, v_out))
```

```python
import functools

import numpy as np
import jax
import jax.numpy as jnp
from jax import lax
from jax.experimental import pallas as pl
from jax.experimental.pallas import tpu as pltpu

F32 = jnp.float32
BF16 = jnp.bfloat16
HI = lax.Precision.HIGHEST
MESH = pl.DeviceIdType.MESH

NORM_EPS = 1e-6
MASK_VALUE = -1e30
TINY = 1e-30
ADAM_LR, ADAM_B1, ADAM_B2, ADAM_EPS, ADAM_WD, ADAM_STEP = 0.001, 0.9, 0.999, 1e-08, 0.01, 10

D_MODEL = 1024
DEPTH = 2
N_CHIPS = 4
CHUNK = 64
LANES = 128
HGRN_W, POOL_W, FOX_W, FOX_HEADS = 256, 256, 512, 8
POOL_WINDOWS = (2, 4, 8, 16)
POOL_HALO = 16
IN_WIDTH = 3592
SHARD_W = IN_WIDTH // N_CHIPS
A_W, B_W, C_W, F_W = 1024, 512, 2048, 128
E_INT = A_W + B_W + C_W + F_W
B_BLK = A_W // 512
C_BLK0 = (A_W + B_W) // 512
F_BLK = (A_W + B_W + C_W) // 128


def _segments():
    segs = []
    for hp in range(2):
        for part in range(4):
            segs.append((part * 256 + hp * 128, 128))
    segs.append((1024, 256))
    segs.append((1280, 256))
    for hp in range(4):
        for part in range(4):
            segs.append((1536 + part * 512 + hp * 128, 128))
    segs.append((3584, 8))
    return segs


_SEGS = _segments()


def _to_internal(w):
    parts = [w[..., s:s + n] for s, n in _SEGS]
    parts.append(jnp.zeros(w.shape[:-1] + (E_INT - IN_WIDTH,), w.dtype))
    return jnp.concatenate(parts, axis=-1)


def _to_original(w):
    offs, o = [], 0
    for s, n in _SEGS:
        offs.append((s, o, n))
        o += n
    parts = [w[..., o:o + n] for s, o, n in sorted(offs)]
    return jnp.concatenate(parts, axis=-1)


def _cparams(sem=None, vmem_mb=48):
    kw = dict(vmem_limit_bytes=vmem_mb * 1024 * 1024)
    if sem is not None:
        kw["dimension_semantics"] = sem
    return pltpu.CompilerParams(**kw)


def _sig(x):
    return 1.0 / (1.0 + jnp.exp(-x))


def _silu(x):
    return x * _sig(x)


def _dsilu(x):
    s = _sig(x)
    return s * (1.0 + x * (1.0 - s))


def _rstd(x):
    return lax.rsqrt(jnp.mean(x * x, axis=-1, keepdims=True) + NORM_EPS)


def _dot(a, b, dims, **kw):
    return lax.dot_general(a, b, (dims, ((), ())), preferred_element_type=F32, **kw)


NN = ((1,), (0,))
NT = ((1,), (1,))
TN = ((0,), (0,))


def _iota(shape, dim):
    return lax.broadcasted_iota(jnp.int32, shape, dim)


def _lbs_fwd(lower_bounds):
    def body(a_ref, o_ref):
        a = a_ref[...]
        a0, a1 = a[0:1], a[1:2]
        m = jnp.maximum(a0, a1)
        e0, e1 = jnp.exp(a0 - m), jnp.exp(a1 - m)
        p0, p1 = e0 / (e0 + e1), e1 / (e0 + e1)
        o_ref[...] = jnp.concatenate([p0 - p0, (p0 + p1) - p0], axis=0)

    return pl.pallas_call(body, out_shape=jax.ShapeDtypeStruct(lower_bounds.shape, F32), name="lbs_fwd")(lower_bounds)


def _inproj_fwd(x2, g_row, w_int, name):
    n, d = x2.shape
    e = w_int.shape[1]
    tm = min(256, n)

    def body(x_ref, g_ref, w_ref, o_ref):
        x = x_ref[...]
        h = (x * _rstd(x) * g_ref[...]).astype(BF16)
        o_ref[...] = jnp.dot(h, w_ref[...], preferred_element_type=F32)

    return pl.pallas_call(
        body, grid=(n // tm,),
        in_specs=[pl.BlockSpec((tm, d), lambda i: (i, 0)), pl.BlockSpec((1, d), lambda i: (0, 0)),
                  pl.BlockSpec((d, e), lambda i: (0, 0))],
        out_specs=pl.BlockSpec((tm, e), lambda i: (i, 0)),
        out_shape=jax.ShapeDtypeStruct((n, e), F32),
        compiler_params=_cparams(("parallel",)), name=name)(x2, g_row, w_int)


def _chunk_cumsum_matrix():
    i, j = _iota((LANES, LANES), 0), _iota((LANES, LANES), 1)
    return ((i <= j) & ((i // CHUNK) == (j // CHUNK))).astype(F32)


def _hgrn_gates(a, lb):
    qa, z = a[:, 0:128], a[:, 128:256]
    sg, sgn = _sig(z), _sig(-z)
    fg = lb + (1.0 - lb) * sg
    lf = jnp.log(jnp.maximum(fg, TINY))
    kk = (1.0 - lb) * sgn
    return qa * _sig(qa), kk, lf, sg, sgn, fg


def _hgrn_fwd(proj3, lbs_row, gn_col, name):
    bsz, t, _ = proj3.shape
    nt = t // LANES

    def body(a_ref, lb_ref, gn_ref, og_ref, or_ref):
        lb = lb_ref[...]
        gn = gn_ref[...]
        umat = _chunk_cumsum_matrix()
        lane64 = _iota((1, LANES), 1) % CHUNK

        def tile(i, carry):
            r0 = pl.multiple_of(i * LANES, LANES)
            a = a_ref[pl.ds(r0, LANES), :]
            qq, kk, lf, _, _, _ = _hgrn_gates(a, lb)
            va, ga = a[:, 256:384], a[:, 384:512]
            q_t, k_t, v_t = qq.T, kk.T, va.T
            b_t = jnp.dot(lf.T, umat, precision=HI, preferred_element_type=F32)
            new_s, o_heads = [], []
            for h in range(2):
                s_h = carry[h]
                rs = slice(CHUNK * h, CHUNK * (h + 1))
                qh, kh, vh, bh = q_t[rs], k_t[rs], v_t[rs], b_t[rs]
                inter = []
                for c in range(2):
                    cs = slice(CHUNK * c, CHUNK * (c + 1))
                    b_ = bh[:, cs]
                    qt = (qh[:, cs] * jnp.exp(b_)).astype(BF16)
                    inter.append(_dot(s_h.astype(BF16), qt, TN))
                    bl = b_[:, CHUNK - 1:CHUNK]
                    kt = (kh[:, cs] * jnp.exp(bl - b_)).astype(BF16)
                    s_h = jnp.exp(bl) * s_h + _dot(kt, vh[:, cs].astype(BF16), NT)
                new_s.append(s_h)

                acc = jnp.concatenate(inter, axis=1) + jnp.sum(qh * kh, axis=0, keepdims=True) * vh
                for dlt in range(1, CHUNK):
                    kr, br, vr = pltpu.roll(kh, dlt, 1), pltpu.roll(bh, dlt, 1), pltpu.roll(vh, dlt, 1)
                    e = jnp.exp(jnp.minimum(bh - br, 0.0))
                    att = jnp.sum(qh * kr * e, axis=0, keepdims=True)
                    acc = acc + jnp.where(lane64 >= dlt, att, 0.0) * vr
                o_heads.append(acc)
            normed = []
            for h in range(2):
                o_h = o_heads[h]
                ms = jnp.mean(o_h * o_h, axis=0, keepdims=True)
                normed.append(o_h * lax.rsqrt(ms + NORM_EPS) * gn[CHUNK * h:CHUNK * (h + 1)])
            or_ref[pl.ds(r0, LANES), :] = jnp.concatenate(o_heads, axis=0).T
            og_ref[pl.ds(r0, LANES), :] = jnp.concatenate(normed, axis=0).T * _silu(ga)
            return tuple(new_s)

        zero = jnp.zeros((CHUNK, CHUNK), F32)
        lax.fori_loop(0, nt, tile, (zero, zero))

    out = jax.ShapeDtypeStruct((bsz, t, HGRN_W), F32)
    return pl.pallas_call(
        body, grid=(bsz, 2),
        in_specs=[pl.BlockSpec((None, t, 512), lambda b, p: (b, 0, p)),
                  pl.BlockSpec((1, 128), lambda b, p: (0, p)),
                  pl.BlockSpec((128, 1), lambda b, p: (p, 0))],
        out_specs=[pl.BlockSpec((None, t, 128), lambda b, p: (b, 0, p)),
                   pl.BlockSpec((None, t, 128), lambda b, p: (b, 0, p))],
        out_shape=[out, out],
        compiler_params=_cparams(("parallel", "parallel")), name=name)(proj3, lbs_row, gn_col)


def _hgrn_bwd(proj3, o_raw, dmixed, lbs_row, gn_row, name):
    bsz, t, _ = proj3.shape
    nt = t // LANES
    nchunk = t // CHUNK

    def body(a_ref, or_ref, do_ref, lb_ref, gn_ref, da_ref, dgn_ref, dlb_ref, s_sc):
        lb = lb_ref[...]
        gn = gn_ref[...]
        umat = _chunk_cumsum_matrix()
        lane = _iota((1, LANES), 1)
        lane64 = lane % CHUNK
        half = lane < CHUNK

        def t_layout(a):
            qq, kk, lf, sg, sgn, fg = _hgrn_gates(a, lb)
            b_t = jnp.dot(lf.T, umat, precision=HI, preferred_element_type=F32)
            return qq.T, kk.T, a[:, 256:384].T, b_t, (sg, sgn, fg)

        def fwd_tile(i, carry):
            r0 = pl.multiple_of(i * LANES, LANES)
            q_t, k_t, v_t, b_t, _ = t_layout(a_ref[pl.ds(r0, LANES), :])
            new_s = []
            for h in range(2):
                s_h = carry[h]
                rs = slice(CHUNK * h, CHUNK * (h + 1))
                for c in range(2):
                    cs = slice(CHUNK * c, CHUNK * (c + 1))
                    s_sc[h, 2 * i + c] = s_h
                    b_ = b_t[rs, cs]
                    bl = b_[:, CHUNK - 1:CHUNK]
                    kt = (k_t[rs, cs] * jnp.exp(bl - b_)).astype(BF16)
                    s_h = jnp.exp(bl) * s_h + _dot(kt, v_t[rs, cs].astype(BF16), NT)
                new_s.append(s_h)
            return tuple(new_s)

        zero = jnp.zeros((CHUNK, CHUNK), F32)
        lax.fori_loop(0, nt, fwd_tile, (zero, zero))

        def half_mean(v):
            m0 = jnp.sum(jnp.where(half, v, 0.0), axis=1, keepdims=True) * (1.0 / CHUNK)
            m1 = jnp.sum(jnp.where(half, 0.0, v), axis=1, keepdims=True) * (1.0 / CHUNK)
            return jnp.where(half, m0, m1)

        def bwd_tile(k, carry):
            ds0, ds1, dgn_acc, dlb_acc = carry
            i = nt - 1 - k
            r0 = pl.multiple_of(i * LANES, LANES)
            a = a_ref[pl.ds(r0, LANES), :]
            qa, z, ga = a[:, 0:128], a[:, 128:256], a[:, 384:512]
            q_t, k_t, v_t, b_t, (sg, sgn, fg) = t_layout(a)
            oraw = or_ref[pl.ds(r0, LANES), :]
            dout = do_ref[pl.ds(r0, LANES), :]
            r = lax.rsqrt(half_mean(oraw * oraw) + NORM_EPS)
            xn = oraw * r
            dga = dout * (xn * gn) * _dsilu(ga)
            don = dout * _silu(ga)
            dgn_acc = dgn_acc + jnp.sum(don * xn, axis=0, keepdims=True)
            dxn = don * gn
            do_t = (r * (dxn - xn * half_mean(dxn * xn))).T
            new_ds, dq_h, dk_h, dv_h, db_h = [], [], [], [], []
            for h in range(2):
                ds_h = (ds0, ds1)[h]
                rs = slice(CHUNK * h, CHUNK * (h + 1))
                qh, kh, vh, bh, doh = q_t[rs], k_t[rs], v_t[rs], b_t[rs], do_t[rs]
                dq_c, dk_c, dv_c, dbl_c = [None, None], [None, None], [None, None], [None, None]
                for c in (1, 0):
                    cs = slice(CHUNK * c, CHUNK * (c + 1))
                    s_n = s_sc[h, 2 * i + c]
                    b_ = bh[:, cs]
                    eb = jnp.exp(b_)
                    bl = b_[:, CHUNK - 1:CHUNK]
                    ek = jnp.exp(bl - b_)
                    ebl = jnp.exp(bl)
                    qt, kt = qh[:, cs] * eb, kh[:, cs] * ek
                    do_c = doh[:, cs].astype(BF16)
                    dsb = ds_h.astype(BF16)
                    dv_c[c] = _dot(dsb, kt.astype(BF16), TN)
                    dkt = _dot(dsb, vh[:, cs].astype(BF16), NN)
                    dqt = _dot(s_n.astype(BF16), do_c, NN)
                    dbl_c[c] = jnp.sum(ds_h * s_n, axis=1, keepdims=True) * ebl + jnp.sum(dkt * kt, axis=1, keepdims=True)
                    dq_c[c], dk_c[c] = dqt * eb, dkt * ek
                    ds_h = ebl * ds_h + _dot(qt.astype(BF16), do_c, NT)
                new_ds.append(ds_h)

                att0 = jnp.sum(qh * kh, axis=0, keepdims=True)
                datt0 = jnp.sum(doh * vh, axis=0, keepdims=True)
                dqh = jnp.concatenate(dq_c, axis=1) + datt0 * kh
                dkh = jnp.concatenate(dk_c, axis=1) + datt0 * qh
                dvh = jnp.concatenate(dv_c, axis=1) + att0 * doh
                for dlt in range(1, CHUNK):
                    kr, br, vr = pltpu.roll(kh, dlt, 1), pltpu.roll(bh, dlt, 1), pltpu.roll(vh, dlt, 1)
                    e = jnp.where(lane64 >= dlt, jnp.exp(jnp.minimum(bh - br, 0.0)), 0.0)
                    qe = qh * e
                    att = jnp.sum(qe * kr, axis=0, keepdims=True)
                    datt = jnp.sum(doh * vr, axis=0, keepdims=True)
                    dqh = dqh + datt * (kr * e)
                    dkh = dkh + pltpu.roll(datt * qe, LANES - dlt, 1)
                    dvh = dvh + pltpu.roll(att * doh, LANES - dlt, 1)
                dbl = jnp.where(half, dbl_c[0], dbl_c[1])
                db_h.append(qh * dqh - kh * dkh + jnp.where(lane64 == CHUNK - 1, dbl, 0.0))
                dq_h.append(dqh)
                dk_h.append(dkh)
                dv_h.append(dvh)
            dqq = jnp.concatenate(dq_h, axis=0).T
            dkk = jnp.concatenate(dk_h, axis=0).T
            dvv = jnp.concatenate(dv_h, axis=0).T
            dlf = _dot(jnp.concatenate(db_h, axis=0), umat, NT, precision=HI).T
            dqa = dqq * _dsilu(qa)
            dfg = jnp.where(fg > TINY, dlf / fg, 0.0)
            dz = (dfg - dkk) * (1.0 - lb) * sg * sgn
            dlb_acc = dlb_acc + jnp.sum(dfg * (1.0 - sg) - dkk * sgn, axis=0, keepdims=True)
            da_ref[pl.ds(r0, LANES), :] = jnp.concatenate([dqa, dz, dvv, dga], axis=1)
            return new_ds[0], new_ds[1], dgn_acc, dlb_acc

        zrow = jnp.zeros((1, LANES), F32)
        _, _, dgn_acc, dlb_acc = lax.fori_loop(0, nt, bwd_tile, (zero, zero, zrow, zrow))
        dgn_ref[...] = jnp.broadcast_to(dgn_acc, (8, LANES))
        dlb_ref[...] = jnp.broadcast_to(dlb_acc, (8, LANES))

    rows = jax.ShapeDtypeStruct((bsz, 8, HGRN_W), F32)
    return pl.pallas_call(
        body, grid=(bsz, 2),
        in_specs=[pl.BlockSpec((None, t, 512), lambda b, p: (b, 0, p)),
                  pl.BlockSpec((None, t, 128), lambda b, p: (b, 0, p)),
                  pl.BlockSpec((None, t, 128), lambda b, p: (b, 0, p)),
                  pl.BlockSpec((1, 128), lambda b, p: (0, p)),
                  pl.BlockSpec((1, 128), lambda b, p: (0, p))],
        out_specs=[pl.BlockSpec((None, t, 512), lambda b, p: (b, 0, p)),
                   pl.BlockSpec((None, 8, 128), lambda b, p: (b, 0, p)),
                   pl.BlockSpec((None, 8, 128), lambda b, p: (b, 0, p))],
        out_shape=[jax.ShapeDtypeStruct((bsz, t, A_W), F32), rows, rows],
        scratch_shapes=[pltpu.VMEM((2, nchunk, CHUNK, CHUNK), F32)],
        compiler_params=_cparams(("parallel", "parallel")), name=name)(proj3, o_raw, dmixed, lbs_row, gn_row)


N_LEVELS = 6


def _hgrn_tables():
    t = np.arange(LANES)
    j = np.arange(LANES)[None, :]
    same_chunk = (t[:, None] // CHUNK) == (j // CHUNK)
    w = np.zeros((2 + N_LEVELS, LANES, LANES), np.float32)
    w[0] = same_chunk & (j <= t[:, None])
    w[1] = same_chunk & (j > t[:, None])
    maskf = np.zeros((N_LEVELS, LANES, LANES), np.float32)
    rightf = np.zeros((N_LEVELS, LANES, LANES), np.float32)
    for li in range(N_LEVELS):
        m = (CHUNK // 2) >> li
        start = t - (t % (2 * m))
        right = (t % (2 * m)) >= m
        first = np.where(right, start + m, t + 1)
        last = np.where(right, t, start + m - 1)
        w[2 + li] = (j >= first[:, None]) & (j <= last[:, None])
        maskf[li] = (t[:, None] // (2 * m)) == (j // (2 * m))
        rightf[li] = right[:, None]
    return jnp.asarray(w.reshape(-1, LANES), BF16), jnp.asarray(maskf), jnp.asarray(rightf)


def _split(x, n):
    parts = []
    for _ in range(n - 1):
        p = x.astype(BF16)
        parts.append(p)
        x = x - p.astype(F32)
    parts.append(x.astype(BF16))
    return parts


def _exact_dot(w, parts):
    acc = jnp.dot(w, parts[0], preferred_element_type=F32)
    for p in parts[1:]:
        acc = acc + jnp.dot(w, p, preferred_element_type=F32)
    return acc


def _head_sums(v, ones_blk, n=2):
    parts = _split(v, n)
    acc = jnp.dot(parts[0], ones_blk, preferred_element_type=F32)
    for p in parts[1:]:
        acc = acc + jnp.dot(p, ones_blk, preferred_element_type=F32)
    return acc


def _hgrn_consts():
    r, c = _iota((LANES, LANES), 0), _iota((LANES, LANES), 1)
    eye = r == c
    ones_blk = ((r // CHUNK) == (c // CHUNK)).astype(BF16)
    return eye, ones_blk, jnp.ones((CHUNK, LANES), BF16)


def _hgrn_levels(qq, kk, parts, w_ref, mk_ref, rt_ref, d_att=None):
    att = [jnp.zeros((LANES, LANES), F32)] * 2
    dq = dk = jnp.zeros((LANES, LANES), F32)
    for li in range(N_LEVELS):
        e = jnp.exp(_exact_dot(w_ref[(2 + li) * LANES:(3 + li) * LANES, :], parts))
        rt = rt_ref[li]
        mk = mk_ref[li]
        qef, kef = e * rt, e * (1.0 - rt)
        qe, ke = (qq * qef).astype(BF16), (kk * kef).astype(BF16)
        dqs, dks = [], []
        for h in range(2):
            hs = slice(CHUNK * h, CHUNK * (h + 1))
            att[h] = att[h] + _dot(qe[:, hs], ke[:, hs], NT) * mk
            if d_att is not None:
                dam = (d_att[h] * mk).astype(BF16)
                dqs.append(jnp.dot(dam, ke[:, hs], preferred_element_type=F32))
                dks.append(_dot(dam, qe[:, hs], TN))
        if d_att is not None:
            dq = dq + jnp.concatenate(dqs, axis=1) * qef
            dk = dk + jnp.concatenate(dks, axis=1) * kef
    return att, dq, dk


def _hgrn_fwd(proj3, lbs_row, gn_row, name):
    bsz, t, _ = proj3.shape
    nt = t // LANES
    w_all, maskf, rightf = _hgrn_tables()

    def body(a_ref, lb_ref, gn_ref, w_ref, mk_ref, rt_ref, og_ref, or_ref):
        lb = lb_ref[...]
        gn = gn_ref[...]
        eye, ones_blk, ones_h = _hgrn_consts()

        def tile(i, carry):
            r0 = pl.multiple_of(i * LANES, LANES)
            a = a_ref[pl.ds(r0, LANES), :]
            qq, kk, lf, _, _, _ = _hgrn_gates(a, lb)
            va, ga = a[:, 256:384], a[:, 384:512]
            parts = _split(lf, 3)
            eb = jnp.exp(_exact_dot(w_ref[0:LANES, :], parts))
            ee = jnp.exp(_exact_dot(w_ref[LANES:2 * LANES, :], parts))
            vb = va.astype(BF16)
            att, _, _ = _hgrn_levels(qq, kk, parts, w_ref, mk_ref, rt_ref)
            qk = _split(qq * kk, 2)
            qeb, keb = (qq * eb).astype(BF16), (kk * ee).astype(BF16)
            new_s, o_heads = [], []
            for h in range(2):
                hs = slice(CHUNK * h, CHUNK * (h + 1))
                diag = _exact_dot_r(qk, hs, ones_h)
                a_h = att[h] + jnp.where(eye, diag, 0.0)
                o_h = jnp.dot(a_h.astype(BF16), vb[:, hs], preferred_element_type=F32)
                st = carry[h]
                chunks = []
                for c in range(2):
                    rc = slice(CHUNK * c, CHUNK * (c + 1))
                    chunks.append(o_h[rc] + _dot(qeb[rc, hs], st.astype(BF16), NT))
                    ebl = eb[CHUNK * (c + 1) - 1:CHUNK * (c + 1), hs]
                    st = st * ebl + _dot(vb[rc, hs], keb[rc, hs], TN)
                new_s.append(st)
                o_heads.append(jnp.concatenate(chunks, axis=0))
            o = jnp.concatenate(o_heads, axis=1)
            ms = _head_sums(o * o, ones_blk) * (1.0 / CHUNK)
            or_ref[pl.ds(r0, LANES), :] = o
            og_ref[pl.ds(r0, LANES), :] = o * lax.rsqrt(ms + NORM_EPS) * gn * _silu(ga)
            return tuple(new_s)

        zero = jnp.zeros((CHUNK, CHUNK), F32)
        lax.fori_loop(0, nt, tile, (zero, zero))

    out = jax.ShapeDtypeStruct((bsz, t, HGRN_W), F32)
    row = pl.BlockSpec((1, 128), lambda b, p: (0, p))
    return pl.pallas_call(
        body, grid=(bsz, 2),
        in_specs=[pl.BlockSpec((None, t, 512), lambda b, p: (b, 0, p)), row, row,
                  pl.BlockSpec(w_all.shape, lambda b, p: (0, 0)),
                  pl.BlockSpec(maskf.shape, lambda b, p: (0, 0, 0)),
                  pl.BlockSpec(rightf.shape, lambda b, p: (0, 0, 0))],
        out_specs=[pl.BlockSpec((None, t, 128), lambda b, p: (b, 0, p)),
                   pl.BlockSpec((None, t, 128), lambda b, p: (b, 0, p))],
        out_shape=[out, out],
        compiler_params=_cparams(("parallel", "parallel")), name=name)(proj3, lbs_row, gn_row, w_all, maskf, rightf)


def _exact_dot_r(parts, hs, ones_h):
    acc = jnp.dot(parts[0][:, hs], ones_h, preferred_element_type=F32)
    for p in parts[1:]:
        acc = acc + jnp.dot(p[:, hs], ones_h, preferred_element_type=F32)
    return acc


def _hgrn_bwd(proj3, o_raw, dmixed, lbs_row, gn_row, name):
    bsz, t, _ = proj3.shape
    nt = t // LANES
    nchunk = t // CHUNK
    w_all, maskf, rightf = _hgrn_tables()

    def body(a_ref, or_ref, do_ref, lb_ref, gn_ref, w_ref, mk_ref, rt_ref, da_ref, dgn_ref, dlb_ref, s_sc):
        lb = lb_ref[...]
        gn = gn_ref[...]
        eye, ones_blk, ones_h = _hgrn_consts()
        r_i, c_i = _iota((LANES, LANES), 0), _iota((LANES, LANES), 1)
        suffix = ((c_i >= r_i) & ((r_i // CHUNK) == (c_i // CHUNK))).astype(BF16)
        row64 = _iota((LANES, CHUNK), 0)
        ones_t = jnp.ones((LANES, CHUNK), BF16)

        def fwd_tile(i, carry):
            r0 = pl.multiple_of(i * LANES, LANES)
            a = a_ref[pl.ds(r0, LANES), :]
            _, kk, lf, _, _, _ = _hgrn_gates(a, lb)
            parts = _split(lf, 3)
            eb = jnp.exp(_exact_dot(w_ref[0:LANES, :], parts))
            ee = jnp.exp(_exact_dot(w_ref[LANES:2 * LANES, :], parts))
            vb, keb = a[:, 256:384].astype(BF16), (kk * ee).astype(BF16)
            new_s = []
            for h in range(2):
                hs = slice(CHUNK * h, CHUNK * (h + 1))
                st = carry[h]
                for c in range(2):
                    rc = slice(CHUNK * c, CHUNK * (c + 1))
                    s_sc[h, 2 * i + c] = st
                    st = st * eb[CHUNK * (c + 1) - 1:CHUNK * (c + 1), hs] + _dot(vb[rc, hs], keb[rc, hs], TN)
                new_s.append(st)
            return tuple(new_s)

        zero = jnp.zeros((CHUNK, CHUNK), F32)
        lax.fori_loop(0, nt, fwd_tile, (zero, zero))

        def bwd_tile(k, carry):
            dst0, dst1, dgn_acc, dlb_acc = carry
            i = nt - 1 - k
            r0 = pl.multiple_of(i * LANES, LANES)
            a = a_ref[pl.ds(r0, LANES), :]
            qa, ga = a[:, 0:128], a[:, 384:512]
            qq, kk, lf, sg, sgn, fg = _hgrn_gates(a, lb)
            parts = _split(lf, 3)
            eb = jnp.exp(_exact_dot(w_ref[0:LANES, :], parts))
            ee = jnp.exp(_exact_dot(w_ref[LANES:2 * LANES, :], parts))
            vb = a[:, 256:384].astype(BF16)
            oraw = or_ref[pl.ds(r0, LANES), :]
            dout = do_ref[pl.ds(r0, LANES), :]
            r = lax.rsqrt(_head_sums(oraw * oraw, ones_blk) * (1.0 / CHUNK) + NORM_EPS)
            xn = oraw * r
            dga = dout * (xn * gn) * _dsilu(ga)
            don = dout * _silu(ga)
            dgn_acc = dgn_acc + jnp.sum(don * xn, axis=0, keepdims=True)
            dxn = don * gn
            do = r * (dxn - xn * (_head_sums(dxn * xn, ones_blk) * (1.0 / CHUNK)))
            dob = do.astype(BF16)
            d_att = [_dot(dob[:, CHUNK * h:CHUNK * (h + 1)], vb[:, CHUNK * h:CHUNK * (h + 1)], NT) for h in range(2)]
            att, dq, dk = _hgrn_levels(qq, kk, parts, w_ref, mk_ref, rt_ref, d_att)
            qk = _split(qq * kk, 2)
            qe_f, ke_f = qq * eb, kk * ee
            qeb, keb = qe_f.astype(BF16), ke_f.astype(BF16)
            new_ds, dq_h, dk_h, dv_h, dbl_h = [], [], [], [], []
            for h in range(2):
                hs = slice(CHUNK * h, CHUNK * (h + 1))
                a_h = att[h] + jnp.where(eye, _exact_dot_r(qk, hs, ones_h), 0.0)
                dv = _dot(a_h.astype(BF16), dob[:, hs], TN)
                ddiag = _exact_dot_r(_split(jnp.where(eye, d_att[h], 0.0), 2), slice(None), ones_t)
                dq_i = dq[:, hs] + ddiag * kk[:, hs]
                dk_i = dk[:, hs] + ddiag * qq[:, hs]
                dst = (dst0, dst1)[h]
                dq_c, dk_c, dv_c, dbl_c = [None, None], [None, None], [None, None], [None, None]
                for c in (1, 0):
                    rc = slice(CHUNK * c, CHUNK * (c + 1))
                    st_n = s_sc[h, 2 * i + c]
                    ebl = eb[CHUNK * (c + 1) - 1:CHUNK * (c + 1), hs]
                    dstb = dst.astype(BF16)
                    dv_c[c] = _dot(keb[rc, hs], dstb, NT)
                    dke = jnp.dot(vb[rc, hs], dstb, preferred_element_type=F32)
                    dqe = jnp.dot(dob[rc, hs], st_n.astype(BF16), preferred_element_type=F32)
                    dbl_c[c] = (jnp.sum(dst * st_n, axis=0, keepdims=True) * ebl
                                + jnp.sum(dke * ke_f[rc, hs], axis=0, keepdims=True))
                    dq_c[c], dk_c[c] = dqe * eb[rc, hs], dke * ee[rc, hs]
                    dst = dst * ebl + _dot(dob[rc, hs], qeb[rc, hs], TN)
                new_ds.append(dst)
                dq_h.append(dq_i + jnp.concatenate(dq_c, axis=0))
                dk_h.append(dk_i + jnp.concatenate(dk_c, axis=0))
                dv_h.append(dv + jnp.concatenate(dv_c, axis=0))
                dbl_h.append(jnp.where(row64 == CHUNK - 1, dbl_c[0], 0.0) + jnp.where(row64 == LANES - 1, dbl_c[1], 0.0))
            dqq = jnp.concatenate(dq_h, axis=1)
            dkk = jnp.concatenate(dk_h, axis=1)
            dvv = jnp.concatenate(dv_h, axis=1)
            db = qq * dqq - kk * dkk + jnp.concatenate(dbl_h, axis=1)
            dlf = _exact_dot(suffix, _split(db, 3))
            dqa = dqq * _dsilu(qa)
            dfg = jnp.where(fg > TINY, dlf / fg, 0.0)
            dz = (dfg - dkk) * (1.0 - lb) * sg * sgn
            dlb_acc = dlb_acc + jnp.sum(dfg * (1.0 - sg) - dkk * sgn, axis=0, keepdims=True)
            da_ref[pl.ds(r0, LANES), :] = jnp.concatenate([dqa, dz, dvv, dga], axis=1)
            return new_ds[0], new_ds[1], dgn_acc, dlb_acc

        zrow = jnp.zeros((1, LANES), F32)
        _, _, dgn_acc, dlb_acc = lax.fori_loop(0, nt, bwd_tile, (zero, zero, zrow, zrow))
        dgn_ref[...] = jnp.broadcast_to(dgn_acc, (8, LANES))
        dlb_ref[...] = jnp.broadcast_to(dlb_acc, (8, LANES))

    rows = jax.ShapeDtypeStruct((bsz, 8, HGRN_W), F32)
    row = pl.BlockSpec((1, 128), lambda b, p: (0, p))
    blk = pl.BlockSpec((None, t, 128), lambda b, p: (b, 0, p))
    return pl.pallas_call(
        body, grid=(bsz, 2),
        in_specs=[pl.BlockSpec((None, t, 512), lambda b, p: (b, 0, p)), blk, blk, row, row,
                  pl.BlockSpec(w_all.shape, lambda b, p: (0, 0)),
                  pl.BlockSpec(maskf.shape, lambda b, p: (0, 0, 0)),
                  pl.BlockSpec(rightf.shape, lambda b, p: (0, 0, 0))],
        out_specs=[pl.BlockSpec((None, t, 512), lambda b, p: (b, 0, p)),
                   pl.BlockSpec((None, 8, 128), lambda b, p: (b, 0, p)),
                   pl.BlockSpec((None, 8, 128), lambda b, p: (b, 0, p))],
        out_shape=[jax.ShapeDtypeStruct((bsz, t, A_W), F32), rows, rows],
        scratch_shapes=[pltpu.VMEM((2, nchunk, CHUNK, CHUNK), F32)],
        compiler_params=_cparams(("parallel", "parallel")), name=name)(
            proj3, o_raw, dmixed, lbs_row, gn_row, w_all, maskf, rightf)


def _pool_tt(t):
    return min(256, t)


def _window_select(s2, s4, s8, s16, lane):
    return jnp.where(lane < 64, s2, jnp.where(lane < 128, s4, jnp.where(lane < 192, s8, s16)))


def _pool_counts(t0, tt):
    lane = _iota((tt, POOL_W), 1)
    tpos = (_iota((tt, POOL_W), 0) + t0 + 1).astype(F32)
    win = jnp.where(lane < 64, 2.0, jnp.where(lane < 128, 4.0, jnp.where(lane < 192, 8.0, 16.0)))
    return 1.0 / jnp.minimum(tpos, win), lane


def _pooled_tile(upad_ref, i, tt):
    r0 = pl.multiple_of(i * tt, 8)
    cat = upad_ref[pl.ds(r0, tt + POOL_HALO), :]
    s2 = cat + pltpu.roll(cat, 1, 0)
    s4 = s2 + pltpu.roll(s2, 2, 0)
    s8 = s4 + pltpu.roll(s4, 4, 0)
    s16 = s8 + pltpu.roll(s8, 8, 0)
    inv, lane = _pool_counts(i * tt, tt)
    sel = _window_select(s2[POOL_HALO:], s4[POOL_HALO:], s8[POOL_HALO:], s16[POOL_HALO:], lane)
    return sel * inv - cat[POOL_HALO:], inv, lane


def _pool_fwd(proj3, wbd, scale_row, name):
    bsz, t, _ = proj3.shape
    tt = _pool_tt(t)

    def body(p_ref, w_ref, sc_ref, o_ref, upad):
        upad[0:POOL_HALO, :] = jnp.zeros((POOL_HALO, POOL_W), F32)
        upad[POOL_HALO:, :] = p_ref[:, 0:POOL_W]
        w = w_ref[...]
        sc = sc_ref[...]

        def tile(i, c):
            pooled, _, _ = _pooled_tile(upad, i, tt)
            r0 = pl.multiple_of(i * tt, 8)
            g = p_ref[pl.ds(r0, tt), POOL_W:2 * POOL_W]
            pre = jnp.dot(pooled.astype(BF16), w, preferred_element_type=F32)
            o_ref[pl.ds(r0, tt), :] = pre * sc * _silu(g)
            return c

        lax.fori_loop(0, t // tt, tile, 0)

    return pl.pallas_call(
        body, grid=(bsz,),
        in_specs=[pl.BlockSpec((None, t, 512), lambda b: (b, 0, B_BLK)),
                  pl.BlockSpec((POOL_W, POOL_W), lambda b: (0, 0)),
                  pl.BlockSpec((1, POOL_W), lambda b: (0, 0))],
        out_specs=pl.BlockSpec((None, t, POOL_W), lambda b: (b, 0, 0)),
        out_shape=jax.ShapeDtypeStruct((bsz, t, POOL_W), F32),
        scratch_shapes=[pltpu.VMEM((t + POOL_HALO, POOL_W), F32)],
        compiler_params=_cparams(("parallel",)), name=name)(proj3, wbd, scale_row)


def _pool_bwd(proj3, dmixed, wbd, scale_row, name):
    bsz, t, _ = proj3.shape
    tt = _pool_tt(t)

    def body(p_ref, do_ref, w_ref, sc_ref, db_ref, dsc_ref, dw_ref, upad, epad):
        upad[0:POOL_HALO, :] = jnp.zeros((POOL_HALO, POOL_W), F32)
        upad[POOL_HALO:, :] = p_ref[:, 0:POOL_W]
        epad[t:, :] = jnp.zeros((POOL_HALO, POOL_W), F32)
        w = w_ref[...]
        sc = sc_ref[...]

        def tile(i, carry):
            dsc_acc, dw_acc = carry
            pooled, inv, _ = _pooled_tile(upad, i, tt)
            r0 = pl.multiple_of(i * tt, 8)
            g = p_ref[pl.ds(r0, tt), POOL_W:2 * POOL_W]
            dout = do_ref[pl.ds(r0, tt), :]
            pb = pooled.astype(BF16)
            pre = jnp.dot(pb, w, preferred_element_type=F32)
            t1 = dout * _silu(g)
            dsc_acc = dsc_acc + jnp.sum(t1 * pre, axis=0, keepdims=True)
            dpre = (t1 * sc).astype(BF16)
            db_ref[pl.ds(r0, tt), POOL_W:2 * POOL_W] = dout * pre * sc * _dsilu(g)
            dw_acc = dw_acc + _dot(pb, dpre, TN)
            dpooled = _dot(dpre, w, NT)
            epad[pl.ds(r0, tt), :] = dpooled * inv
            return dsc_acc, dw_acc

        dsc_acc, dw_acc = lax.fori_loop(0, t // tt, tile, (jnp.zeros((1, POOL_W), F32), jnp.zeros((POOL_W, POOL_W), F32)))
        dsc_ref[...] = jnp.broadcast_to(dsc_acc, (8, POOL_W))
        dw_ref[...] = dw_acc

        def tile2(i, c):
            r0 = pl.multiple_of(i * tt, 8)
            n = tt + POOL_HALO
            cat = epad[pl.ds(r0, n), :]
            s2 = cat + pltpu.roll(cat, n - 1, 0)
            s4 = s2 + pltpu.roll(s2, n - 2, 0)
            s8 = s4 + pltpu.roll(s4, n - 4, 0)
            s16 = s8 + pltpu.roll(s8, n - 8, 0)
            inv, lane = _pool_counts(i * tt, tt)
            sel = _window_select(s2[:tt], s4[:tt], s8[:tt], s16[:tt], lane)
            db_ref[pl.ds(r0, tt), 0:POOL_W] = sel - cat[:tt] / inv
            return c

        lax.fori_loop(0, t // tt, tile2, 0)

    return pl.pallas_call(
        body, grid=(bsz,),
        in_specs=[pl.BlockSpec((None, t, 512), lambda b: (b, 0, B_BLK)),
                  pl.BlockSpec((None, t, POOL_W), lambda b: (b, 0, 1)),
                  pl.BlockSpec((POOL_W, POOL_W), lambda b: (0, 0)),
                  pl.BlockSpec((1, POOL_W), lambda b: (0, 0))],
        out_specs=[pl.BlockSpec((None, t, 512), lambda b: (b, 0, 0)),
                   pl.BlockSpec((None, 8, POOL_W), lambda b: (b, 0, 0)),
                   pl.BlockSpec((None, POOL_W, POOL_W), lambda b: (b, 0, 0))],
        out_shape=[jax.ShapeDtypeStruct((bsz, t, B_W), F32), jax.ShapeDtypeStruct((bsz, 8, POOL_W), F32),
                   jax.ShapeDtypeStruct((bsz, POOL_W, POOL_W), F32)],
        scratch_shapes=[pltpu.VMEM((t + POOL_HALO, POOL_W), F32), pltpu.VMEM((t + POOL_HALO, POOL_W), F32)],
        compiler_params=_cparams(("parallel",)), name=name)(proj3, dmixed, wbd, scale_row)


def _head_select_rows(hp):
    r, c = _iota((8, LANES), 0), _iota((8, LANES), 1)
    return ((r < 2) & (c == 2 * hp + r)).astype(F32)


def _foxgate_fwd(proj3, bias_row, name):
    bsz, t, _ = proj3.shape
    nt = t // LANES

    def body(f_ref, b_ref, cn_ref, ct_ref):
        bias = b_ref[...]
        i, j = _iota((LANES, LANES), 0), _iota((LANES, LANES), 1)
        lower = (j <= i).astype(F32)
        spread = (_iota((LANES, FOX_W), 0) == _iota((LANES, FOX_W), 1) // 64).astype(F32)

        def tile(k, carry):
            r0 = pl.multiple_of(k * LANES, LANES)
            xg = f_ref[pl.ds(r0, LANES), :] + bias
            lf = jnp.minimum(xg, 0.0) - jnp.log(1.0 + jnp.exp(-jnp.abs(xg)))
            c = jnp.dot(lower, lf, precision=HI, preferred_element_type=F32) + carry
            cn_ref[pl.ds(r0, LANES), :] = jnp.dot(c, spread, precision=HI, preferred_element_type=F32)
            for hp in range(4):
                ct_ref[hp, :, pl.ds(r0, LANES)] = _dot(_head_select_rows(hp), c, NT, precision=HI)
            return c[LANES - 1:LANES, :]

        lax.fori_loop(0, nt, tile, jnp.zeros((1, LANES), F32))

    return pl.pallas_call(
        body, grid=(bsz,),
        in_specs=[pl.BlockSpec((None, t, 128), lambda b: (b, 0, F_BLK)), pl.BlockSpec((1, 128), lambda b: (0, 0))],
        out_specs=[pl.BlockSpec((None, t, FOX_W), lambda b: (b, 0, 0)),
                   pl.BlockSpec((None, 4, 8, t), lambda b: (b, 0, 0, 0))],
        out_shape=[jax.ShapeDtypeStruct((bsz, t, FOX_W), F32), jax.ShapeDtypeStruct((bsz, 4, 8, t), F32)],
        compiler_params=_cparams(("parallel",)), name=name)(proj3, bias_row)


def _foxgate_bwd(proj3, dc_nat, bias_row, name):
    bsz, t, _ = proj3.shape
    nt = t // LANES

    def body(f_ref, dc_ref, b_ref, df_ref, dbias_ref, run_sc):
        bias = b_ref[...]
        i, j = _iota((LANES, LANES), 0), _iota((LANES, LANES), 1)
        upper = (j >= i).astype(F32)
        valid = _iota((1, LANES), 1) < FOX_HEADS
        run_sc[...] = jnp.zeros((8, LANES), F32)
        dbias_ref[...] = jnp.zeros((8, LANES), F32)

        def tile(k, c):
            r0 = pl.multiple_of((nt - 1 - k) * LANES, LANES)
            dc = dc_ref[pl.ds(r0, LANES), :] + jnp.where(i == LANES - 1, run_sc[0:1, :], 0.0)
            dlf = jnp.dot(upper, dc, precision=HI, preferred_element_type=F32)
            xg = f_ref[pl.ds(r0, LANES), :] + bias
            df = jnp.where(valid, dlf * _sig(-xg), 0.0)
            df_ref[pl.ds(r0, LANES), :] = df
            run_sc[...] = dlf[0:8, :]
            dbias_ref[...] += jnp.sum(df, axis=0, keepdims=True)
            return c

        lax.fori_loop(0, nt, tile, 0)

    blk = pl.BlockSpec((None, t, 128), lambda b: (b, 0, 0))
    return pl.pallas_call(
        body, grid=(bsz,),
        in_specs=[pl.BlockSpec((None, t, 128), lambda b: (b, 0, F_BLK)), blk, pl.BlockSpec((1, 128), lambda b: (0, 0))],
        out_specs=[blk, pl.BlockSpec((None, 8, 128), lambda b: (b, 0, 0))],
        out_shape=[jax.ShapeDtypeStruct((bsz, t, F_W), F32), jax.ShapeDtypeStruct((bsz, 8, 128), F32)],
        scratch_shapes=[pltpu.VMEM((8, LANES), F32)],
        compiler_params=_cparams(("parallel",)), name=name)(proj3, dc_nat, bias_row)


def _fox_tile(t):
    return min(256, t)


def _fox_fwd(proj3, c_nat, c_t, name):
    bsz, t, _ = proj3.shape
    tq = _fox_tile(t)
    nq = t // tq

    def body(q_ref, kv_ref, cn_ref, ct_ref, og_ref, or_ref, lse_ref):
        i = pl.program_id(2)
        qblk = q_ref[...]
        rows = _iota((tq, tq), 0) + i * tq
        outs, lses = [], []
        for h in range(2):
            hs = slice(64 * h, 64 * (h + 1))
            qh = (qblk[:, hs] * 0.125).astype(BF16)
            cq = cn_ref[:, 64 * h:64 * h + 1]

            def kv_step(j, carry, h=h, qh=qh, cq=cq):
                m, l, acc = carry
                c0 = pl.multiple_of(j * tq, tq)
                kh = kv_ref[pl.ds(c0, tq), 128 + 64 * h:128 + 64 * (h + 1)].astype(BF16)
                vh = kv_ref[pl.ds(c0, tq), 256 + 64 * h:256 + 64 * (h + 1)].astype(BF16)
                ck = ct_ref[h:h + 1, pl.ds(c0, tq)]
                s = _dot(qh, kh, NT) + (cq - ck)
                s = jnp.where(rows >= _iota((tq, tq), 1) + j * tq, s, MASK_VALUE)
                m_new = jnp.maximum(m, jnp.max(s, axis=1, keepdims=True))
                alpha = jnp.exp(m - m_new)
                p = jnp.exp(s - m_new)
                l = alpha * l + jnp.sum(p, axis=1, keepdims=True)
                acc = alpha * acc + jnp.dot(p.astype(BF16), vh, preferred_element_type=F32)
                return m_new, l, acc

            init = (jnp.full((tq, 1), MASK_VALUE, F32), jnp.zeros((tq, 1), F32), jnp.zeros((tq, 64), F32))
            m, l, acc = lax.fori_loop(0, i + 1, kv_step, init)
            outs.append(acc / l)
            lses.append(jnp.broadcast_to(m + jnp.log(l), (tq, 64)))
        o = jnp.concatenate(outs, axis=1)
        or_ref[...] = o
        og_ref[...] = o * _silu(qblk[:, 384:512])
        lse_ref[...] = jnp.concatenate(lses, axis=1)

    out = jax.ShapeDtypeStruct((bsz, t, FOX_W), F32)
    blk = pl.BlockSpec((None, tq, 128), lambda b, p, i: (b, i, p))
    return pl.pallas_call(
        body, grid=(bsz, 4, nq),
        in_specs=[pl.BlockSpec((None, tq, 512), lambda b, p, i: (b, i, C_BLK0 + p)),
                  pl.BlockSpec((None, t, 512), lambda b, p, i: (b, 0, C_BLK0 + p)),
                  blk,
                  pl.BlockSpec((None, None, 8, t), lambda b, p, i: (b, p, 0, 0))],
        out_specs=[blk, blk, blk],
        out_shape=[out, out, out],
        compiler_params=_cparams(("parallel", "parallel", "arbitrary")), name=name)(proj3, proj3, c_nat, c_t)


def _fox_bwd(proj3, o_raw, dmixed, lse, c_nat, c_t, name):
    bsz, t, _ = proj3.shape
    tq = _fox_tile(t)
    nq = t // tq

    def body(a_ref, or_ref, do_ref, lse_ref, cn_ref, ct_ref, dc_out, dct_out, drow_out, dq_sc, do_sc, dl_sc):
        def prep(i, c):
            r0 = pl.multiple_of(i * tq, tq)
            g = a_ref[pl.ds(r0, tq), 384:512]
            dout = do_ref[pl.ds(r0, tq), :]
            o = or_ref[pl.ds(r0, tq), :]
            dc_out[pl.ds(r0, tq), 384:512] = dout * o * _dsilu(g)
            do = dout * _silu(g)
            do_sc[pl.ds(r0, tq), :] = do
            prod = do * o
            d0 = jnp.sum(prod[:, 0:64], axis=1, keepdims=True)
            d1 = jnp.sum(prod[:, 64:128], axis=1, keepdims=True)
            dl_sc[pl.ds(r0, tq), :] = jnp.concatenate([jnp.broadcast_to(d0, (tq, 64)), jnp.broadcast_to(d1, (tq, 64))], axis=1)
            dq_sc[pl.ds(r0, tq), :] = jnp.zeros((tq, 128), F32)
            drow_out[pl.ds(r0, tq), :] = jnp.zeros((tq, 128), F32)
            return c

        lax.fori_loop(0, nq, prep, 0)
        dct_out[...] = jnp.zeros((8, t), F32)

        def kv_tile(j, c):
            c0 = pl.multiple_of(j * tq, tq)
            cols = _iota((tq, tq), 1) + j * tq
            dks, dvs = [], []
            for h in range(2):
                kh = a_ref[pl.ds(c0, tq), 128 + 64 * h:128 + 64 * (h + 1)].astype(BF16)
                vh = a_ref[pl.ds(c0, tq), 256 + 64 * h:256 + 64 * (h + 1)].astype(BF16)
                ck = ct_ref[h:h + 1, pl.ds(c0, tq)]

                def q_step(i, carry, h=h, kh=kh, vh=vh, ck=ck):
                    dk, dv, dcol = carry
                    r0 = pl.multiple_of(i * tq, tq)
                    hs = slice(64 * h, 64 * (h + 1))
                    qh = (a_ref[pl.ds(r0, tq), hs] * 0.125).astype(BF16)
                    doh = do_sc[pl.ds(r0, tq), hs].astype(BF16)
                    lse_h = lse_ref[pl.ds(r0, tq), 64 * h:64 * h + 1]
                    dl_h = dl_sc[pl.ds(r0, tq), 64 * h:64 * h + 1]
                    cq = cn_ref[pl.ds(r0, tq), 64 * h:64 * h + 1]
                    s = _dot(qh, kh, NT) + (cq - ck)
                    p = jnp.where(_iota((tq, tq), 0) + i * tq >= cols, jnp.exp(s - lse_h), 0.0)
                    dv = dv + _dot(p.astype(BF16), doh, TN)
                    dp = _dot(doh, vh, NT)
                    ds = p * (dp - dl_h)
                    dsb = ds.astype(BF16)
                    dq_sc[pl.ds(r0, tq), hs] += jnp.dot(dsb, kh, preferred_element_type=F32) * 0.125
                    dk = dk + _dot(dsb, qh, TN)
                    dcol = dcol - jnp.sum(ds, axis=0, keepdims=True)
                    drow_out[pl.ds(r0, tq), hs] += jnp.broadcast_to(jnp.sum(ds, axis=1, keepdims=True), (tq, 64))
                    return dk, dv, dcol

                init = (jnp.zeros((tq, 64), F32), jnp.zeros((tq, 64), F32), jnp.zeros((1, tq), F32))
                dk, dv, dcol = lax.fori_loop(j, nq, q_step, init)
                dks.append(dk)
                dvs.append(dv)
                dct_out[h:h + 1, pl.ds(c0, tq)] = dcol
            dc_out[pl.ds(c0, tq), 128:256] = jnp.concatenate(dks, axis=1)
            dc_out[pl.ds(c0, tq), 256:384] = jnp.concatenate(dvs, axis=1)
            return c

        lax.fori_loop(0, nq, kv_tile, 0)
        dc_out[:, 0:128] = dq_sc[...]

    blk = pl.BlockSpec((None, t, 128), lambda b, p: (b, 0, p))
    return pl.pallas_call(
        body, grid=(bsz, 4),
        in_specs=[pl.BlockSpec((None, t, 512), lambda b, p: (b, 0, C_BLK0 + p)),
                  blk,
                  pl.BlockSpec((None, t, 128), lambda b, p: (b, 0, 4 + p)),
                  blk, blk,
                  pl.BlockSpec((None, None, 8, t), lambda b, p: (b, p, 0, 0))],
        out_specs=[pl.BlockSpec((None, t, 512), lambda b, p: (b, 0, p)),
                   pl.BlockSpec((None, None, 8, t), lambda b, p: (b, p, 0, 0)), blk],
        out_shape=[jax.ShapeDtypeStruct((bsz, t, C_W), F32), jax.ShapeDtypeStruct((bsz, 4, 8, t), F32),
                   jax.ShapeDtypeStruct((bsz, t, FOX_W), F32)],
        scratch_shapes=[pltpu.VMEM((t, 128), F32), pltpu.VMEM((t, 128), F32), pltpu.VMEM((t, 128), F32)],
        compiler_params=_cparams(("parallel", "parallel")), name=name)(proj3, o_raw, dmixed, lse, c_nat, c_t)


def _mix_tm(n):
    return min(512, n)


def _outproj_fwd(x2, oa, ob, oc, wo, g_row, name):
    n, d = x2.shape
    tm = _mix_tm(n)

    def body(x_ref, oa_ref, ob_ref, oc_ref, w_ref, g_ref, y_ref, xo_ref):
        y = (jnp.dot(oa_ref[...].astype(BF16), w_ref[0:256, :], preferred_element_type=F32)
             + jnp.dot(ob_ref[...].astype(BF16), w_ref[256:512, :], preferred_element_type=F32)
             + jnp.dot(oc_ref[...].astype(BF16), w_ref[512:1024, :], preferred_element_type=F32))
        y_ref[...] = y
        xo_ref[...] = x_ref[...] + y * _rstd(y) * g_ref[...]

    row = lambda w: pl.BlockSpec((tm, w), lambda i: (i, 0))
    out = jax.ShapeDtypeStruct((n, d), F32)
    return pl.pallas_call(
        body, grid=(n // tm,),
        in_specs=[row(d), row(256), row(256), row(512), pl.BlockSpec((d, d), lambda i: (0, 0)),
                  pl.BlockSpec((1, d), lambda i: (0, 0))],
        out_specs=[row(d), row(d)], out_shape=[out, out],
        compiler_params=_cparams(("parallel",)), name=name)(x2, oa, ob, oc, wo, g_row)


def _loss_head(x2, target2, name):
    n, d = x2.shape
    tm = _mix_tm(n)

    def body(x_ref, t_ref, dx_ref, l_ref):
        err = x_ref[...] - t_ref[...]
        dx_ref[...] = err * (1.0 / d)

        @pl.when(pl.program_id(0) == 0)
        def _():
            l_ref[...] = jnp.zeros((8, 128), F32)

        l_ref[...] += jnp.sum(err * err)

    row = pl.BlockSpec((tm, d), lambda i: (i, 0))
    return pl.pallas_call(
        body, grid=(n // tm,), in_specs=[row, row],
        out_specs=[row, pl.BlockSpec((8, 128), lambda i: (0, 0))],
        out_shape=[jax.ShapeDtypeStruct((n, d), F32), jax.ShapeDtypeStruct((8, 128), F32)],
        compiler_params=_cparams(("arbitrary",)), name=name)(x2, target2)


def _outproj_bwd(dxo, y, oa, ob, oc, wo, g_row, name):
    n, d = dxo.shape
    tm = _mix_tm(n)

    def body(dx_ref, y_ref, oa_ref, ob_ref, oc_ref, w_ref, g_ref, dm_ref, dw_ref, dg_ref):
        @pl.when(pl.program_id(0) == 0)
        def _():
            dw_ref[...] = jnp.zeros((d, d), F32)
            dg_ref[...] = jnp.zeros((8, d), F32)

        yv, dx = y_ref[...], dx_ref[...]
        r = _rstd(yv)
        yn = yv * r
        dg_ref[...] += jnp.sum(dx * yn, axis=0, keepdims=True)
        dyn = dx * g_ref[...]
        dy = (r * (dyn - yn * jnp.mean(dyn * yn, axis=-1, keepdims=True))).astype(BF16)
        dm_ref[...] = _dot(dy, w_ref[...], NT)
        dw_ref[0:256, :] += _dot(oa_ref[...].astype(BF16), dy, TN)
        dw_ref[256:512, :] += _dot(ob_ref[...].astype(BF16), dy, TN)
        dw_ref[512:1024, :] += _dot(oc_ref[...].astype(BF16), dy, TN)

    row = lambda w: pl.BlockSpec((tm, w), lambda i: (i, 0))
    fixed = lambda r, c: pl.BlockSpec((r, c), lambda i: (0, 0))
    return pl.pallas_call(
        body, grid=(n // tm,),
        in_specs=[row(d), row(d), row(256), row(256), row(512), fixed(d, d), fixed(1, d)],
        out_specs=[row(d), fixed(d, d), fixed(8, d)],
        out_shape=[jax.ShapeDtypeStruct((n, d), F32), jax.ShapeDtypeStruct((d, d), F32), jax.ShapeDtypeStruct((8, d), F32)],
        compiler_params=_cparams(("arbitrary",)), name=name)(dxo, y, oa, ob, oc, wo, g_row)


_PIECES = ((0, A_W), (A_W, B_W), (A_W + B_W, C_W), (A_W + B_W + C_W, F_W))


def _inproj_bwd_x(x2, dxo, g_row, w_int, pieces, name):
    n, d = x2.shape
    tm = min(256, n)

    def body(x_ref, dxo_ref, g_ref, w_ref, da_ref, db_ref, dc_ref, df_ref, dx_ref, dg_ref):
        @pl.when(pl.program_id(0) == 0)
        def _():
            dg_ref[...] = jnp.zeros((8, d), F32)

        dh = jnp.zeros((tm, d), F32)
        for ref, (o, w) in zip((da_ref, db_ref, dc_ref, df_ref), _PIECES):
            dh = dh + _dot(ref[...].astype(BF16), w_ref[:, o:o + w], NT)
        x = x_ref[...]
        r = _rstd(x)
        xn = x * r
        dg_ref[...] += jnp.sum(dh * xn, axis=0, keepdims=True)
        dxn = dh * g_ref[...]
        dx_ref[...] = dxo_ref[...] + r * (dxn - xn * jnp.mean(dxn * xn, axis=-1, keepdims=True))

    row = lambda w: pl.BlockSpec((tm, w), lambda i: (i, 0))
    fixed = lambda r, c: pl.BlockSpec((r, c), lambda i: (0, 0))
    return pl.pallas_call(
        body, grid=(n // tm,),
        in_specs=[row(d), row(d), fixed(1, d), fixed(d, E_INT)] + [row(w) for _, w in _PIECES],
        out_specs=[row(d), fixed(8, d)],
        out_shape=[jax.ShapeDtypeStruct((n, d), F32), jax.ShapeDtypeStruct((8, d), F32)],
        compiler_params=_cparams(("arbitrary",)), name=name)(x2, dxo, g_row, w_int, *pieces)


def _inproj_bwd_w(x2, g_row, piece, name):
    n, d = x2.shape
    w = piece.shape[1]
    tm = min(512, n)

    def body(x_ref, g_ref, dp_ref, dw_ref):
        @pl.when(pl.program_id(0) == 0)
        def _():
            dw_ref[...] = jnp.zeros((d, w), F32)

        x = x_ref[...]
        h = (x * _rstd(x) * g_ref[...]).astype(BF16)
        dw_ref[...] += _dot(h, dp_ref[...].astype(BF16), TN)

    return pl.pallas_call(
        body, grid=(n // tm,),
        in_specs=[pl.BlockSpec((tm, d), lambda i: (i, 0)), pl.BlockSpec((1, d), lambda i: (0, 0)),
                  pl.BlockSpec((tm, w), lambda i: (i, 0))],
        out_specs=pl.BlockSpec((d, w), lambda i: (0, 0)),
        out_shape=jax.ShapeDtypeStruct((d, w), F32),
        compiler_params=_cparams(("arbitrary",)), name=name)(x2, g_row, piece)


def _block_diag(pool_w_l):
    z = jnp.zeros((64, 64), pool_w_l.dtype)
    return jnp.concatenate(
        [jnp.concatenate([pool_w_l[g] if c == g else z for c in range(4)], axis=1) for g in range(4)], axis=0)


def _pad_lanes(v, width=128):
    return jnp.pad(v, ((0, 0),) * (v.ndim - 1) + ((0, width - v.shape[-1]),))


def _local_step(x, target, lower_bounds, pre_norm_g, w_in_int, hgrn_norm_g, fox_f_bias, pool_w, pool_scale,
                w_out_bf, post_norm_g):
    bsz, t, d = x.shape
    n = bsz * t
    lbs = _lbs_fwd(lower_bounds)
    saved = []
    xc = x.reshape(n, d)
    for l in range(DEPTH):
        proj = _inproj_fwd(xc, pre_norm_g[l:l + 1], w_in_int[l], f"inproj_fwd{l}").reshape(bsz, t, E_INT)
        wbd = _block_diag(pool_w[l]).astype(BF16)
        bias_row = _pad_lanes(fox_f_bias[l:l + 1])
        oa, oa_raw = _hgrn_fwd(proj, lbs[l:l + 1], hgrn_norm_g[l:l + 1], f"hgrn_fwd{l}")
        ob = _pool_fwd(proj, wbd, pool_scale[l:l + 1], f"pool_fwd{l}")
        c_nat, c_t = _foxgate_fwd(proj, bias_row, f"foxgate_fwd{l}")
        oc, oc_raw, lse = _fox_fwd(proj, c_nat, c_t, f"fox_fwd{l}")
        y, xn = _outproj_fwd(xc, oa.reshape(n, -1), ob.reshape(n, -1), oc.reshape(n, -1), w_out_bf[l],
                             post_norm_g[l:l + 1], f"outproj_fwd{l}")
        saved.append((xc, proj, wbd, bias_row, oa, oa_raw, ob, oc, oc_raw, lse, c_nat, c_t, y))
        xc = xn
    dx, sq = _loss_head(xc, target.reshape(n, d), "loss_head")
    g = {k: [None] * DEPTH for k in ("pre", "w_in", "hgn", "bias", "pool_w", "pool_scale", "w_out", "post", "lbs")}
    for l in reversed(range(DEPTH)):
        xin, proj, wbd, bias_row, oa, oa_raw, ob, oc, oc_raw, lse, c_nat, c_t, y = saved[l]
        dmix, g["w_out"][l], dpost = _outproj_bwd(dx, y, oa.reshape(n, -1), ob.reshape(n, -1), oc.reshape(n, -1),
                                                  w_out_bf[l], post_norm_g[l:l + 1], f"outproj_bwd{l}")
        g["post"][l] = dpost[0]
        dmix3 = dmix.reshape(bsz, t, d)
        d_c, dct, drow = _fox_bwd(proj, oc_raw, dmix3, lse, c_nat, c_t, f"fox_bwd{l}")
        dc_nat = _pad_lanes(dct[:, :, 0:2, :].reshape(bsz, FOX_HEADS, t).transpose(0, 2, 1)
                            + drow.reshape(bsz, t, FOX_HEADS, 64)[..., 0])
        d_f, dbias = _foxgate_bwd(proj, dc_nat, bias_row, f"foxgate_bwd{l}")
        g["bias"][l] = jnp.sum(dbias[:, 0, :FOX_HEADS], axis=0)
        d_b, dscale, dwbd = _pool_bwd(proj, dmix3, wbd, pool_scale[l:l + 1], f"pool_bwd{l}")
        g["pool_scale"][l] = jnp.sum(dscale[:, 0], axis=0)
        dwbd = jnp.sum(dwbd, axis=0)
        g["pool_w"][l] = jnp.stack([dwbd[64 * k:64 * (k + 1), 64 * k:64 * (k + 1)] for k in range(4)])
        d_a, dgn, dlb = _hgrn_bwd(proj, oa_raw, dmix3, lbs[l:l + 1], hgrn_norm_g[l:l + 1], f"hgrn_bwd{l}")
        g["hgn"][l] = jnp.sum(dgn[:, 0], axis=0)
        g["lbs"][l] = jnp.sum(dlb[:, 0], axis=0)
        pieces = [p.reshape(n, -1) for p in (d_a, d_b, d_c, d_f)]
        g["w_in"][l] = jnp.concatenate(
            [_inproj_bwd_w(xin, pre_norm_g[l:l + 1], p, f"inproj_bwd_w{l}_{k}") for k, p in enumerate(pieces)], axis=1)
        dx, dpre = _inproj_bwd_x(xin, dx, pre_norm_g[l:l + 1], w_in_int[l], pieces, f"inproj_bwd_x{l}")
        g["pre"][l] = dpre[0]
    grads = {k: jnp.stack(v) for k, v in g.items()}
    return sq, dx.reshape(bsz, t, d), grads


def _place():
    return lax.axis_index("x"), lax.axis_index("y"), lax.axis_index("c")


def _other_chips(x, y):
    return [(1 - x, y), (x, 1 - y), (1 - x, 1 - y)]


_ANY = pl.BlockSpec(memory_space=pl.ANY)


def _gather_weights(w_in_sh, w_out_sh):
    def body(win_ref, wout_ref, ain_ref, aout_ref, send_sems, recv_sems, local_sems):
        x, y, c = _place()
        me = 2 * x + y
        mine = [pltpu.make_async_copy(win_ref, ain_ref.at[me], local_sems.at[0]),
                pltpu.make_async_copy(wout_ref, aout_ref.at[me], local_sems.at[1])]
        for cp in mine:
            cp.start()
        sends = []
        for k, (px, py) in enumerate(_other_chips(x, y)):
            for j, (src, dst) in enumerate(((win_ref, ain_ref), (wout_ref, aout_ref))):
                sends.append(pltpu.make_async_remote_copy(
                    src_ref=src, dst_ref=dst.at[me], send_sem=send_sems.at[2 * k + j], recv_sem=recv_sems.at[2 * k + j],
                    device_id=(px, py, c), device_id_type=MESH))
        for cp in sends:
            cp.start()
        for k, (px, py) in enumerate(_other_chips(x, y)):
            for j, (src, dst) in enumerate(((win_ref, ain_ref), (wout_ref, aout_ref))):
                pltpu.make_async_remote_copy(
                    src_ref=src, dst_ref=dst.at[2 * px + py], send_sem=send_sems.at[2 * k + j],
                    recv_sem=recv_sems.at[2 * k + j], device_id=(px, py, c), device_id_type=MESH).wait_recv()
        for cp in sends:
            cp.wait_send()
        for cp in mine:
            cp.wait()

    return pl.pallas_call(
        body, in_specs=[_ANY, _ANY], out_specs=[_ANY, _ANY],
        out_shape=[jax.ShapeDtypeStruct((N_CHIPS,) + w_in_sh.shape, w_in_sh.dtype),
                   jax.ShapeDtypeStruct((N_CHIPS,) + w_out_sh.shape, w_out_sh.dtype)],
        scratch_shapes=[pltpu.SemaphoreType.DMA((6,)), pltpu.SemaphoreType.DMA((6,)), pltpu.SemaphoreType.DMA((2,))],
        name="gather_weights")(w_in_sh, w_out_sh)


def _swap_with_sibling(parts, name):
    k = len(parts)

    def body(*refs):
        src, dst = refs[:k], refs[k:2 * k]
        send_sems, recv_sems = refs[2 * k:]
        x, y, c = _place()
        cps = [pltpu.make_async_remote_copy(src_ref=src[j], dst_ref=dst[j], send_sem=send_sems.at[j], recv_sem=recv_sems.at[j],
                                            device_id=(x, y, 1 - c), device_id_type=MESH) for j in range(k)]
        for cp in cps:
            cp.start()
        for cp in cps:
            cp.wait()

    return pl.pallas_call(
        body, in_specs=[_ANY] * k, out_specs=[_ANY] * k,
        out_shape=[jax.ShapeDtypeStruct(p.shape, p.dtype) for p in parts],
        scratch_shapes=[pltpu.SemaphoreType.DMA((k,)), pltpu.SemaphoreType.DMA((k,))], name=name)(*parts)


def _scatter_to_chips(parts, name):
    k = len(parts)

    def body(*refs):
        src, dst = refs[:k], refs[k:2 * k]
        send_sems, recv_sems = refs[2 * k:]
        x, y, c = _place()
        me = 2 * x + y
        cps = []
        for rel, (px, py) in enumerate(_other_chips(x, y)):
            for j in range(k):
                cps.append(pltpu.make_async_remote_copy(
                    src_ref=src[j].at[2 * px + py], dst_ref=dst[j].at[rel], send_sem=send_sems.at[rel * k + j],
                    recv_sem=recv_sems.at[rel * k + j], device_id=(px, py, c), device_id_type=MESH))
        for cp in cps:
            cp.start()
        for cp in cps:
            cp.wait()
        del me

    return pl.pallas_call(
        body, in_specs=[_ANY] * k, out_specs=[_ANY] * k,
        out_shape=[jax.ShapeDtypeStruct((3,) + p.shape[1:], p.dtype) for p in parts],
        scratch_shapes=[pltpu.SemaphoreType.DMA((3 * k,)), pltpu.SemaphoreType.DMA((3 * k,))], name=name)(*parts)


def _add_n(parts, name):
    r, c = parts[0].shape
    tr = 256 if r % 256 == 0 else r

    def body(*refs):
        acc = refs[0][...]
        for ref in refs[1:-1]:
            acc = acc + ref[...]
        refs[-1][...] = acc

    blk = pl.BlockSpec((tr, c), lambda i: (i, 0))
    return pl.pallas_call(
        body, grid=(r // tr,), in_specs=[blk] * len(parts), out_specs=blk,
        out_shape=jax.ShapeDtypeStruct((r, c), F32), compiler_params=_cparams(("parallel",)), name=name)(*parts)


def _all_reduce_small(packet):
    r, w = packet.shape

    def body(p_ref, o_ref, buf, send_sems, recv_sems):
        x, y, c = _place()
        me = 4 * x + 2 * y + c
        buf[me] = p_ref[...]
        peers = []
        for k in range(1, 8):
            fx, fy, fc = (k >> 2) & 1, (k >> 1) & 1, k & 1
            peers.append((x ^ fx, y ^ fy, c ^ fc))
        cps = [pltpu.make_async_remote_copy(src_ref=p_ref, dst_ref=buf.at[me], send_sem=send_sems.at[k], recv_sem=recv_sems.at[k],
                                            device_id=peer, device_id_type=MESH) for k, peer in enumerate(peers)]
        for cp in cps:
            cp.start()
        for k, (px, py, pc) in enumerate(peers):
            pltpu.make_async_remote_copy(src_ref=p_ref, dst_ref=buf.at[4 * px + 2 * py + pc], send_sem=send_sems.at[k],
                                         recv_sem=recv_sems.at[k], device_id=(px, py, pc), device_id_type=MESH).wait_recv()
        for cp in cps:
            cp.wait_send()
        acc = buf[0]
        for k in range(1, 8):
            acc = acc + buf[k]
        o_ref[...] = acc

    vm = pl.BlockSpec(memory_space=pltpu.VMEM)
    return pl.pallas_call(
        body, in_specs=[vm], out_specs=vm, out_shape=jax.ShapeDtypeStruct((r, w), F32),
        scratch_shapes=[pltpu.VMEM((8, r, w), F32), pltpu.SemaphoreType.DMA((7,)), pltpu.SemaphoreType.DMA((7,))],
        name="all_reduce_small")(packet)


def _adamw_math(w, g, m, v):
    m = ADAM_B1 * m + (1.0 - ADAM_B1) * g
    v = ADAM_B2 * v + (1.0 - ADAM_B2) * (g * g)
    m_hat = m / (1.0 - ADAM_B1 ** ADAM_STEP)
    v_hat = v / (1.0 - ADAM_B2 ** ADAM_STEP)
    return -ADAM_LR * (m_hat / (jnp.sqrt(v_hat) + ADAM_EPS) + ADAM_WD * w), m, v


def _adamw(w, g, m, v, name):
    nl, r, c = w.shape
    tr = 256 if r % 256 == 0 else r

    def body(w_ref, g_ref, m_ref, v_ref, d_ref, mo_ref, vo_ref):
        d_ref[...], mo_ref[...], vo_ref[...] = _adamw_math(w_ref[...], g_ref[...], m_ref[...], v_ref[...])

    blk = pl.BlockSpec((None, tr, c), lambda l, i: (l, i, 0))
    out = jax.ShapeDtypeStruct(w.shape, F32)
    return pl.pallas_call(
        body, grid=(nl, r // tr), in_specs=[blk] * 4, out_specs=[blk] * 3, out_shape=[out] * 3,
        compiler_params=_cparams(("parallel", "parallel")), name=name)(w, g, m, v)


def _small_update(gsum, lower_bounds, wpack, mpack, vpack):
    r, w = gsum.shape
    lb_rows = DEPTH * HGRN_W // 128

    def body(g_ref, a_ref, w_ref, m_ref, v_ref, go_ref, d_ref, mo_ref, vo_ref):
        a = a_ref[...]
        a0, a1 = a[0:1], a[1:2]
        mx = jnp.maximum(a0, a1)
        e0, e1 = jnp.exp(a0 - mx), jnp.exp(a1 - mx)
        p0, p1 = e0 / (e0 + e1), e1 / (e0 + e1)
        g = g_ref[...]
        half = lb_rows // 2
        dl0 = jnp.concatenate([g[k:k + 1] for k in range(half)], axis=1)
        dl1 = jnp.concatenate([g[half + k:half + k + 1] for k in range(half)], axis=1)
        dp0 = (dl0 + dl1) - (dl0 + dl1)
        dp1 = dl1
        inner = p0 * dp0 + p1 * dp1
        da0, da1 = p0 * (dp0 - inner), p1 * (dp1 - inner)
        rows = [da0[:, 128 * k:128 * (k + 1)] for k in range(half)] + [da1[:, 128 * k:128 * (k + 1)] for k in range(half)]
        gfull = jnp.concatenate(rows + [g[lb_rows:]], axis=0)
        go_ref[...] = gfull
        d_ref[...], mo_ref[...], vo_ref[...] = _adamw_math(w_ref[...], gfull, m_ref[...], v_ref[...])

    vm = pl.BlockSpec(memory_space=pltpu.VMEM)
    out = jax.ShapeDtypeStruct((r, w), F32)
    return pl.pallas_call(body, in_specs=[vm] * 5, out_specs=[vm] * 4, out_shape=[out] * 4, name="small_update")(
        gsum, lower_bounds, wpack, mpack, vpack)


_SMALL = ("lower_bounds", "pre_norm_g", "hgrn_norm_g", "fox_f_bias", "pool_w", "pool_scale", "post_norm_g")


def _pack(parts):
    rows = []
    for k in _SMALL:
        f = parts[k].reshape(-1)
        pad = (-f.shape[0]) % (8 * 128)
        rows.append(jnp.pad(f, (0, pad)).reshape(-1, 128))
    rows.append(jnp.zeros((8, 128), F32))
    return jnp.concatenate(rows, axis=0)


def _unpack(pack, like):
    out, r = {}, 0
    for k in _SMALL:
        size = int(np.prod(like[k].shape))
        nr = -(-size // (8 * 128)) * 8
        out[k] = pack[r:r + nr].reshape(-1)[:size].reshape(like[k].shape)
        r += nr
    return out, r


def kernel(x, lower_bounds, pre_norm_g, w_in, hgrn_norm_g, fox_f_bias, pool_w, pool_scale, w_out, post_norm_g, loss_target, m_lower_bounds, m_pre_norm_g, m_w_in, m_hgrn_norm_g, m_fox_f_bias, m_pool_w, m_pool_scale, m_w_out, m_post_norm_g, v_lower_bounds, v_pre_norm_g, v_w_in, v_hgrn_norm_g, v_fox_f_bias, v_pool_w, v_pool_scale, v_w_out, v_post_norm_g):
    cx, cy, cc = _place()
    chip = 2 * cx + cy

    ain, aout = _gather_weights(w_in.astype(BF16), w_out.astype(BF16))
    w_in_full = jnp.concatenate([ain[q] for q in range(N_CHIPS)], axis=-1)
    w_in_int = _to_internal(w_in_full)
    w_out_full = jnp.concatenate([aout[q] for q in range(N_CHIPS)], axis=1)

    sq, grad_x, g = _local_step(x, loss_target, lower_bounds, pre_norm_g, w_in_int, hgrn_norm_g, fox_f_bias, pool_w,
                                pool_scale, w_out_full, post_norm_g)

    gin = _to_original(g["w_in"])
    gin_blocks = jnp.stack([gin[:, :, SHARD_W * q:SHARD_W * (q + 1)] for q in range(N_CHIPS)])
    gout_blocks = g["w_out"].reshape(DEPTH, N_CHIPS, 256, D_MODEL).transpose(1, 0, 2, 3)
    take = lambda a, l: lax.dynamic_index_in_dim(a, l, axis=1, keepdims=False)
    mine_in, mine_out = take(gin_blocks, cc), take(gout_blocks, cc)
    sib_in, sib_out = _swap_with_sibling([take(gin_blocks, 1 - cc), take(gout_blocks, 1 - cc)], "grad_swap1")
    rin, rout = 4 * 1024, 4 * 256
    sum_in = _add_n([mine_in.reshape(rin, SHARD_W), sib_in.reshape(rin, SHARD_W)], "grad_add1_in").reshape(4, 1024, SHARD_W)
    sum_out = _add_n([mine_out.reshape(rout, D_MODEL), sib_out.reshape(rout, D_MODEL)], "grad_add1_out").reshape(4, 256, D_MODEL)
    got_in, got_out = _scatter_to_chips([sum_in, sum_out], "grad_scatter")
    own = lambda a: lax.dynamic_index_in_dim(a, chip, axis=0, keepdims=False)
    half_in = _add_n([own(sum_in)] + [got_in[k] for k in range(3)], "grad_add2_in")
    half_out = _add_n([own(sum_out)] + [got_out[k] for k in range(3)], "grad_add2_out")
    oth_in, oth_out = _swap_with_sibling([half_in, half_out], "grad_swap2")
    first = cc == 0
    grad_w_in = jnp.stack([jnp.where(first, half_in, oth_in), jnp.where(first, oth_in, half_in)])
    grad_w_out = jnp.stack([jnp.where(first, half_out, oth_out), jnp.where(first, oth_out, half_out)])

    small = {"lower_bounds": g["lbs"], "pre_norm_g": g["pre"], "hgrn_norm_g": g["hgn"], "fox_f_bias": g["bias"],
             "pool_w": g["pool_w"], "pool_scale": g["pool_scale"], "post_norm_g": g["post"]}
    packet = _pack(small)
    nrows = packet.shape[0]
    packet = packet.at[nrows - 1].set(sq[0])
    gsum = _all_reduce_small(packet)
    loss = gsum[nrows - 1, 0] * (0.5 / D_MODEL)

    weights = {"lower_bounds": lower_bounds, "pre_norm_g": pre_norm_g, "hgrn_norm_g": hgrn_norm_g,
               "fox_f_bias": fox_f_bias, "pool_w": pool_w, "pool_scale": pool_scale, "post_norm_g": post_norm_g}
    moments_m = {"lower_bounds": m_lower_bounds, "pre_norm_g": m_pre_norm_g, "hgrn_norm_g": m_hgrn_norm_g,
                 "fox_f_bias": m_fox_f_bias, "pool_w": m_pool_w, "pool_scale": m_pool_scale, "post_norm_g": m_post_norm_g}
    moments_v = {"lower_bounds": v_lower_bounds, "pre_norm_g": v_pre_norm_g, "hgrn_norm_g": v_hgrn_norm_g,
                 "fox_f_bias": v_fox_f_bias, "pool_w": v_pool_w, "pool_scale": v_pool_scale, "post_norm_g": v_post_norm_g}
    gp, dp, mp, vp = _small_update(gsum, lower_bounds, _pack(weights), _pack(moments_m), _pack(moments_v))
    gs, _ = _unpack(gp, weights)
    ds, _ = _unpack(dp, weights)
    ms, _ = _unpack(mp, weights)
    vs, _ = _unpack(vp, weights)

    d_in, m_in, v_in = _adamw(w_in, grad_w_in, m_w_in, v_w_in, "adamw_w_in")
    d_out, m_out, v_out = _adamw(w_out, grad_w_out, m_w_out, v_w_out, "adamw_w_out")

    def ordered(s, big_in, big_out):
        return (s["lower_bounds"], s["pre_norm_g"], big_in, s["hgrn_norm_g"], s["fox_f_bias"], s["pool_w"],
                s["pool_scale"], big_out, s["post_norm_g"])

    return (loss, grad_x, *ordered(gs, grad_w_in, grad_w_out), *ordered(ds, d_in, d_out),
            *ordered(ms, m_in, m_out), *ordered(vs, v_in, v_out))
```

```python
import functools

import numpy as np
import jax
import jax.numpy as jnp
from jax import lax
from jax.experimental import pallas as pl
from jax.experimental.pallas import tpu as pltpu

F32 = jnp.float32
BF16 = jnp.bfloat16
HI = lax.Precision.HIGHEST
MESH = pl.DeviceIdType.MESH

NORM_EPS = 1e-6
MASK_VALUE = -1e30
TINY = 1e-30
ADAM_LR, ADAM_B1, ADAM_B2, ADAM_EPS, ADAM_WD, ADAM_STEP = 0.001, 0.9, 0.999, 1e-08, 0.01, 10

D_MODEL = 1024
DEPTH = 2
N_CHIPS = 4
CHUNK = 64
LANES = 128
HGRN_W, POOL_W, FOX_W, FOX_HEADS = 256, 256, 512, 8
POOL_WINDOWS = (2, 4, 8, 16)
POOL_HALO = 16
IN_WIDTH = 3592
SHARD_W = IN_WIDTH // N_CHIPS
A_W, B_W, C_W, F_W = 1024, 512, 2048, 128
E_INT = A_W + B_W + C_W + F_W
B_BLK = A_W // 512
C_BLK0 = (A_W + B_W) // 512
F_BLK = (A_W + B_W + C_W) // 128


def _segments():
    segs = []
    for hp in range(2):
        for part in range(4):
            segs.append((part * 256 + hp * 128, 128))
    segs.append((1024, 256))
    segs.append((1280, 256))
    for hp in range(4):
        for part in range(4):
            segs.append((1536 + part * 512 + hp * 128, 128))
    segs.append((3584, 8))
    return segs


_SEGS = _segments()


def _to_internal(w):
    parts = [w[..., s:s + n] for s, n in _SEGS]
    parts.append(jnp.zeros(w.shape[:-1] + (E_INT - IN_WIDTH,), w.dtype))
    return jnp.concatenate(parts, axis=-1)


def _to_original(w):
    offs, o = [], 0
    for s, n in _SEGS:
        offs.append((s, o, n))
        o += n
    parts = [w[..., o:o + n] for s, o, n in sorted(offs)]
    return jnp.concatenate(parts, axis=-1)


def _cparams(sem=None, vmem_mb=48):
    kw = dict(vmem_limit_bytes=vmem_mb * 1024 * 1024)
    if sem is not None:
        kw["dimension_semantics"] = sem
    return pltpu.CompilerParams(**kw)


def _sig(x):
    return 1.0 / (1.0 + jnp.exp(-x))


def _silu(x):
    return x * _sig(x)


def _dsilu(x):
    s = _sig(x)
    return s * (1.0 + x * (1.0 - s))


def _rstd(x):
    return lax.rsqrt(jnp.mean(x * x, axis=-1, keepdims=True) + NORM_EPS)


def _dot(a, b, dims, **kw):
    return lax.dot_general(a, b, (dims, ((), ())), preferred_element_type=F32, **kw)


NN = ((1,), (0,))
NT = ((1,), (1,))
TN = ((0,), (0,))


def _iota(shape, dim):
    return lax.broadcasted_iota(jnp.int32, shape, dim)


def _lbs_fwd(lower_bounds):
    def body(a_ref, o_ref):
        a = a_ref[...]
        a0, a1 = a[0:1], a[1:2]
        m = jnp.maximum(a0, a1)
        e0, e1 = jnp.exp(a0 - m), jnp.exp(a1 - m)
        p0, p1 = e0 / (e0 + e1), e1 / (e0 + e1)
        o_ref[...] = jnp.concatenate([p0 - p0, (p0 + p1) - p0], axis=0)

    return pl.pallas_call(body, out_shape=jax.ShapeDtypeStruct(lower_bounds.shape, F32), name="lbs_fwd")(lower_bounds)


def _inproj_fwd(x2, g_row, w_int, name):
    n, d = x2.shape
    e = w_int.shape[1]
    tm = min(256, n)

    def body(x_ref, g_ref, w_ref, o_ref):
        x = x_ref[...]
        h = (x * _rstd(x) * g_ref[...]).astype(BF16)
        o_ref[...] = jnp.dot(h, w_ref[...], preferred_element_type=F32)

    return pl.pallas_call(
        body, grid=(n // tm,),
        in_specs=[pl.BlockSpec((tm, d), lambda i: (i, 0)), pl.BlockSpec((1, d), lambda i: (0, 0)),
                  pl.BlockSpec((d, e), lambda i: (0, 0))],
        out_specs=pl.BlockSpec((tm, e), lambda i: (i, 0)),
        out_shape=jax.ShapeDtypeStruct((n, e), F32),
        compiler_params=_cparams(("parallel",)), name=name)(x2, g_row, w_int)


def _chunk_cumsum_matrix():
    i, j = _iota((LANES, LANES), 0), _iota((LANES, LANES), 1)
    return ((i <= j) & ((i // CHUNK) == (j // CHUNK))).astype(F32)


def _hgrn_gates(a, lb):
    qa, z = a[:, 0:128], a[:, 128:256]
    sg, sgn = _sig(z), _sig(-z)
    fg = lb + (1.0 - lb) * sg
    lf = jnp.log(jnp.maximum(fg, TINY))
    kk = (1.0 - lb) * sgn
    return qa * _sig(qa), kk, lf, sg, sgn, fg


def _hgrn_fwd(proj3, lbs_row, gn_col, name):
    bsz, t, _ = proj3.shape
    nt = t // LANES

    def body(a_ref, lb_ref, gn_ref, og_ref, or_ref):
        lb = lb_ref[...]
        gn = gn_ref[...]
        umat = _chunk_cumsum_matrix()
        lane64 = _iota((1, LANES), 1) % CHUNK

        def tile(i, carry):
            r0 = pl.multiple_of(i * LANES, LANES)
            a = a_ref[pl.ds(r0, LANES), :]
            qq, kk, lf, _, _, _ = _hgrn_gates(a, lb)
            va, ga = a[:, 256:384], a[:, 384:512]
            q_t, k_t, v_t = qq.T, kk.T, va.T
            b_t = jnp.dot(lf.T, umat, precision=HI, preferred_element_type=F32)
            new_s, o_heads = [], []
            for h in range(2):
                s_h = carry[h]
                rs = slice(CHUNK * h, CHUNK * (h + 1))
                qh, kh, vh, bh = q_t[rs], k_t[rs], v_t[rs], b_t[rs]
                inter = []
                for c in range(2):
                    cs = slice(CHUNK * c, CHUNK * (c + 1))
                    b_ = bh[:, cs]
                    qt = (qh[:, cs] * jnp.exp(b_)).astype(BF16)
                    inter.append(_dot(s_h.astype(BF16), qt, TN))
                    bl = b_[:, CHUNK - 1:CHUNK]
                    kt = (kh[:, cs] * jnp.exp(bl - b_)).astype(BF16)
                    s_h = jnp.exp(bl) * s_h + _dot(kt, vh[:, cs].astype(BF16), NT)
                new_s.append(s_h)

                acc = jnp.concatenate(inter, axis=1) + jnp.sum(qh * kh, axis=0, keepdims=True) * vh
                for dlt in range(1, CHUNK):
                    kr, br, vr = pltpu.roll(kh, dlt, 1), pltpu.roll(bh, dlt, 1), pltpu.roll(vh, dlt, 1)
                    e = jnp.exp(jnp.minimum(bh - br, 0.0))
                    att = jnp.sum(qh * kr * e, axis=0, keepdims=True)
                    acc = acc + jnp.where(lane64 >= dlt, att, 0.0) * vr
                o_heads.append(acc)
            normed = []
            for h in range(2):
                o_h = o_heads[h]
                ms = jnp.mean(o_h * o_h, axis=0, keepdims=True)
                normed.append(o_h * lax.rsqrt(ms + NORM_EPS) * gn[CHUNK * h:CHUNK * (h + 1)])
            or_ref[pl.ds(r0, LANES), :] = jnp.concatenate(o_heads, axis=0).T
            og_ref[pl.ds(r0, LANES), :] = jnp.concatenate(normed, axis=0).T * _silu(ga)
            return tuple(new_s)

        zero = jnp.zeros((CHUNK, CHUNK), F32)
        lax.fori_loop(0, nt, tile, (zero, zero))

    out = jax.ShapeDtypeStruct((bsz, t, HGRN_W), F32)
    return pl.pallas_call(
        body, grid=(bsz, 2),
        in_specs=[pl.BlockSpec((None, t, 512), lambda b, p: (b, 0, p)),
                  pl.BlockSpec((1, 128), lambda b, p: (0, p)),
                  pl.BlockSpec((128, 1), lambda b, p: (p, 0))],
        out_specs=[pl.BlockSpec((None, t, 128), lambda b, p: (b, 0, p)),
                   pl.BlockSpec((None, t, 128), lambda b, p: (b, 0, p))],
        out_shape=[out, out],
        compiler_params=_cparams(("parallel", "parallel")), name=name)(proj3, lbs_row, gn_col)


def _hgrn_bwd(proj3, o_raw, dmixed, lbs_row, gn_row, name):
    bsz, t, _ = proj3.shape
    nt = t // LANES
    nchunk = t // CHUNK

    def body(a_ref, or_ref, do_ref, lb_ref, gn_ref, da_ref, dgn_ref, dlb_ref, s_sc):
        lb = lb_ref[...]
        gn = gn_ref[...]
        umat = _chunk_cumsum_matrix()
        lane = _iota((1, LANES), 1)
        lane64 = lane % CHUNK
        half = lane < CHUNK

        def t_layout(a):
            qq, kk, lf, sg, sgn, fg = _hgrn_gates(a, lb)
            b_t = jnp.dot(lf.T, umat, precision=HI, preferred_element_type=F32)
            return qq.T, kk.T, a[:, 256:384].T, b_t, (sg, sgn, fg)

        def fwd_tile(i, carry):
            r0 = pl.multiple_of(i * LANES, LANES)
            q_t, k_t, v_t, b_t, _ = t_layout(a_ref[pl.ds(r0, LANES), :])
            new_s = []
            for h in range(2):
                s_h = carry[h]
                rs = slice(CHUNK * h, CHUNK * (h + 1))
                for c in range(2):
                    cs = slice(CHUNK * c, CHUNK * (c + 1))
                    s_sc[h, 2 * i + c] = s_h
                    b_ = b_t[rs, cs]
                    bl = b_[:, CHUNK - 1:CHUNK]
                    kt = (k_t[rs, cs] * jnp.exp(bl - b_)).astype(BF16)
                    s_h = jnp.exp(bl) * s_h + _dot(kt, v_t[rs, cs].astype(BF16), NT)
                new_s.append(s_h)
            return tuple(new_s)

        zero = jnp.zeros((CHUNK, CHUNK), F32)
        lax.fori_loop(0, nt, fwd_tile, (zero, zero))

        def half_mean(v):
            m0 = jnp.sum(jnp.where(half, v, 0.0), axis=1, keepdims=True) * (1.0 / CHUNK)
            m1 = jnp.sum(jnp.where(half, 0.0, v), axis=1, keepdims=True) * (1.0 / CHUNK)
            return jnp.where(half, m0, m1)

        def bwd_tile(k, carry):
            ds0, ds1, dgn_acc, dlb_acc = carry
            i = nt - 1 - k
            r0 = pl.multiple_of(i * LANES, LANES)
            a = a_ref[pl.ds(r0, LANES), :]
            qa, z, ga = a[:, 0:128], a[:, 128:256], a[:, 384:512]
            q_t, k_t, v_t, b_t, (sg, sgn, fg) = t_layout(a)
            oraw = or_ref[pl.ds(r0, LANES), :]
            dout = do_ref[pl.ds(r0, LANES), :]
            r = lax.rsqrt(half_mean(oraw * oraw) + NORM_EPS)
            xn = oraw * r
            dga = dout * (xn * gn) * _dsilu(ga)
            don = dout * _silu(ga)
            dgn_acc = dgn_acc + jnp.sum(don * xn, axis=0, keepdims=True)
            dxn = don * gn
            do_t = (r * (dxn - xn * half_mean(dxn * xn))).T
            new_ds, dq_h, dk_h, dv_h, db_h = [], [], [], [], []
            for h in range(2):
                ds_h = (ds0, ds1)[h]
                rs = slice(CHUNK * h, CHUNK * (h + 1))
                qh, kh, vh, bh, doh = q_t[rs], k_t[rs], v_t[rs], b_t[rs], do_t[rs]
                dq_c, dk_c, dv_c, dbl_c = [None, None], [None, None], [None, None], [None, None]
                for c in (1, 0):
                    cs = slice(CHUNK * c, CHUNK * (c + 1))
                    s_n = s_sc[h, 2 * i + c]
                    b_ = bh[:, cs]
                    eb = jnp.exp(b_)
                    bl = b_[:, CHUNK - 1:CHUNK]
                    ek = jnp.exp(bl - b_)
                    ebl = jnp.exp(bl)
                    qt, kt = qh[:, cs] * eb, kh[:, cs] * ek
                    do_c = doh[:, cs].astype(BF16)
                    dsb = ds_h.astype(BF16)
                    dv_c[c] = _dot(dsb, kt.astype(BF16), TN)
                    dkt = _dot(dsb, vh[:, cs].astype(BF16), NN)
                    dqt = _dot(s_n.astype(BF16), do_c, NN)
                    dbl_c[c] = jnp.sum(ds_h * s_n, axis=1, keepdims=True) * ebl + jnp.sum(dkt * kt, axis=1, keepdims=True)
                    dq_c[c], dk_c[c] = dqt * eb, dkt * ek
                    ds_h = ebl * ds_h + _dot(qt.astype(BF16), do_c, NT)
                new_ds.append(ds_h)

                att0 = jnp.sum(qh * kh, axis=0, keepdims=True)
                datt0 = jnp.sum(doh * vh, axis=0, keepdims=True)
                dqh = jnp.concatenate(dq_c, axis=1) + datt0 * kh
                dkh = jnp.concatenate(dk_c, axis=1) + datt0 * qh
                dvh = jnp.concatenate(dv_c, axis=1) + att0 * doh
                for dlt in range(1, CHUNK):
                    kr, br, vr = pltpu.roll(kh, dlt, 1), pltpu.roll(bh, dlt, 1), pltpu.roll(vh, dlt, 1)
                    e = jnp.where(lane64 >= dlt, jnp.exp(jnp.minimum(bh - br, 0.0)), 0.0)
                    qe = qh * e
                    att = jnp.sum(qe * kr, axis=0, keepdims=True)
                    datt = jnp.sum(doh * vr, axis=0, keepdims=True)
                    dqh = dqh + datt * (kr * e)
                    dkh = dkh + pltpu.roll(datt * qe, LANES - dlt, 1)
                    dvh = dvh + pltpu.roll(att * doh, LANES - dlt, 1)
                dbl = jnp.where(half, dbl_c[0], dbl_c[1])
                db_h.append(qh * dqh - kh * dkh + jnp.where(lane64 == CHUNK - 1, dbl, 0.0))
                dq_h.append(dqh)
                dk_h.append(dkh)
                dv_h.append(dvh)
            dqq = jnp.concatenate(dq_h, axis=0).T
            dkk = jnp.concatenate(dk_h, axis=0).T
            dvv = jnp.concatenate(dv_h, axis=0).T
            dlf = _dot(jnp.concatenate(db_h, axis=0), umat, NT, precision=HI).T
            dqa = dqq * _dsilu(qa)
            dfg = jnp.where(fg > TINY, dlf / fg, 0.0)
            dz = (dfg - dkk) * (1.0 - lb) * sg * sgn
            dlb_acc = dlb_acc + jnp.sum(dfg * (1.0 - sg) - dkk * sgn, axis=0, keepdims=True)
            da_ref[pl.ds(r0, LANES), :] = jnp.concatenate([dqa, dz, dvv, dga], axis=1)
            return new_ds[0], new_ds[1], dgn_acc, dlb_acc

        zrow = jnp.zeros((1, LANES), F32)
        _, _, dgn_acc, dlb_acc = lax.fori_loop(0, nt, bwd_tile, (zero, zero, zrow, zrow))
        dgn_ref[...] = jnp.broadcast_to(dgn_acc, (8, LANES))
        dlb_ref[...] = jnp.broadcast_to(dlb_acc, (8, LANES))

    rows = jax.ShapeDtypeStruct((bsz, 8, HGRN_W), F32)
    return pl.pallas_call(
        body, grid=(bsz, 2),
        in_specs=[pl.BlockSpec((None, t, 512), lambda b, p: (b, 0, p)),
                  pl.BlockSpec((None, t, 128), lambda b, p: (b, 0, p)),
                  pl.BlockSpec((None, t, 128), lambda b, p: (b, 0, p)),
                  pl.BlockSpec((1, 128), lambda b, p: (0, p)),
                  pl.BlockSpec((1, 128), lambda b, p: (0, p))],
        out_specs=[pl.BlockSpec((None, t, 512), lambda b, p: (b, 0, p)),
                   pl.BlockSpec((None, 8, 128), lambda b, p: (b, 0, p)),
                   pl.BlockSpec((None, 8, 128), lambda b, p: (b, 0, p))],
        out_shape=[jax.ShapeDtypeStruct((bsz, t, A_W), F32), rows, rows],
        scratch_shapes=[pltpu.VMEM((2, nchunk, CHUNK, CHUNK), F32)],
        compiler_params=_cparams(("parallel", "parallel")), name=name)(proj3, o_raw, dmixed, lbs_row, gn_row)


N_LEVELS = 6


def _hgrn_tables():
    t = np.arange(LANES)
    j = np.arange(LANES)[None, :]
    same_chunk = (t[:, None] // CHUNK) == (j // CHUNK)
    w = np.zeros((2 + N_LEVELS, LANES, LANES), np.float32)
    w[0] = same_chunk & (j <= t[:, None])
    w[1] = same_chunk & (j > t[:, None])
    maskf = np.zeros((N_LEVELS, LANES, LANES), np.float32)
    rightf = np.zeros((N_LEVELS, LANES, LANES), np.float32)
    for li in range(N_LEVELS):
        m = (CHUNK // 2) >> li
        start = t - (t % (2 * m))
        right = (t % (2 * m)) >= m
        first = np.where(right, start + m, t + 1)
        last = np.where(right, t, start + m - 1)
        w[2 + li] = (j >= first[:, None]) & (j <= last[:, None])
        maskf[li] = (t[:, None] // (2 * m)) == (j // (2 * m))
        rightf[li] = right[:, None]
    return jnp.asarray(w.reshape(-1, LANES), BF16), jnp.asarray(maskf), jnp.asarray(rightf)


def _split(x, n):
    parts = []
    for _ in range(n - 1):
        p = x.astype(BF16)
        parts.append(p)
        x = x - p.astype(F32)
    parts.append(x.astype(BF16))
    return parts


def _exact_dot(w, parts):
    acc = jnp.dot(w, parts[0], preferred_element_type=F32)
    for p in parts[1:]:
        acc = acc + jnp.dot(w, p, preferred_element_type=F32)
    return acc


def _head_sums(v, ones_blk, n=2):
    parts = _split(v, n)
    acc = jnp.dot(parts[0], ones_blk, preferred_element_type=F32)
    for p in parts[1:]:
        acc = acc + jnp.dot(p, ones_blk, preferred_element_type=F32)
    return acc


def _hgrn_consts():
    r, c = _iota((LANES, LANES), 0), _iota((LANES, LANES), 1)
    eye = r == c
    ones_blk = ((r // CHUNK) == (c // CHUNK)).astype(BF16)
    return eye, ones_blk, jnp.ones((CHUNK, LANES), BF16)


def _hgrn_levels(qq, kk, parts, w_ref, mk_ref, rt_ref, d_att=None):
    att = [jnp.zeros((LANES, LANES), F32)] * 2
    dq = dk = db = jnp.zeros((LANES, LANES), F32)
    for li in range(N_LEVELS):
        e = jnp.exp(_exact_dot(w_ref[(2 + li) * LANES:(3 + li) * LANES, :], parts))
        rt = rt_ref[li]
        mk = mk_ref[li]
        qef, kef = e * rt, e * (1.0 - rt)
        qe, ke = (qq * qef).astype(BF16), (kk * kef).astype(BF16)
        dqs, dks = [], []
        for h in range(2):
            hs = slice(CHUNK * h, CHUNK * (h + 1))
            att[h] = att[h] + _dot(qe[:, hs], ke[:, hs], NT) * mk
            if d_att is not None:
                dam = (d_att[h] * mk).astype(BF16)
                dqs.append(jnp.dot(dam, ke[:, hs], preferred_element_type=F32))
                dks.append(_dot(dam, qe[:, hs], TN))
        if d_att is not None:
            dqe, dke = jnp.concatenate(dqs, axis=1), jnp.concatenate(dks, axis=1)
            dq = dq + dqe * qef
            dk = dk + dke * kef
            db = db + (dqe * qe.astype(F32) - dke * ke.astype(F32))
    return att, dq, dk, db


def _hgrn_fwd(proj3, lbs_row, gn_row, name):
    bsz, t, _ = proj3.shape
    nt = t // LANES
    w_all, maskf, rightf = _hgrn_tables()

    def body(a_ref, lb_ref, gn_ref, w_ref, mk_ref, rt_ref, og_ref, or_ref):
        lb = lb_ref[...]
        gn = gn_ref[...]
        eye, ones_blk, ones_h = _hgrn_consts()

        def tile(i, carry):
            r0 = pl.multiple_of(i * LANES, LANES)
            a = a_ref[pl.ds(r0, LANES), :]
            qq, kk, lf, _, _, _ = _hgrn_gates(a, lb)
            va, ga = a[:, 256:384], a[:, 384:512]
            parts = _split(lf, 3)
            eb = jnp.exp(_exact_dot(w_ref[0:LANES, :], parts))
            ee = jnp.exp(_exact_dot(w_ref[LANES:2 * LANES, :], parts))
            vb = va.astype(BF16)
            att, _, _, _ = _hgrn_levels(qq, kk, parts, w_ref, mk_ref, rt_ref)
            qk = _split(qq * kk, 2)
            qeb, keb = (qq * eb).astype(BF16), (kk * ee).astype(BF16)
            new_s, o_heads = [], []
            for h in range(2):
                hs = slice(CHUNK * h, CHUNK * (h + 1))
                diag = _exact_dot_r(qk, hs, ones_h)
                a_h = att[h] + jnp.where(eye, diag, 0.0)
                o_h = jnp.dot(a_h.astype(BF16), vb[:, hs], preferred_element_type=F32)
                st = carry[h]
                chunks = []
                for c in range(2):
                    rc = slice(CHUNK * c, CHUNK * (c + 1))
                    chunks.append(o_h[rc] + _dot(qeb[rc, hs], st.astype(BF16), NT))
                    ebl = eb[CHUNK * (c + 1) - 1:CHUNK * (c + 1), hs]
                    st = st * ebl + _dot(vb[rc, hs], keb[rc, hs], TN)
                new_s.append(st)
                o_heads.append(jnp.concatenate(chunks, axis=0))
            o = jnp.concatenate(o_heads, axis=1)
            ms = _head_sums(o * o, ones_blk) * (1.0 / CHUNK)
            or_ref[pl.ds(r0, LANES), :] = o
            og_ref[pl.ds(r0, LANES), :] = o * lax.rsqrt(ms + NORM_EPS) * gn * _silu(ga)
            return tuple(new_s)

        zero = jnp.zeros((CHUNK, CHUNK), F32)
        lax.fori_loop(0, nt, tile, (zero, zero))

    out = jax.ShapeDtypeStruct((bsz, t, HGRN_W), F32)
    row = pl.BlockSpec((1, 128), lambda b, p: (0, p))
    return pl.pallas_call(
        body, grid=(bsz, 2),
        in_specs=[pl.BlockSpec((None, t, 512), lambda b, p: (b, 0, p)), row, row,
                  pl.BlockSpec(w_all.shape, lambda b, p: (0, 0)),
                  pl.BlockSpec(maskf.shape, lambda b, p: (0, 0, 0)),
                  pl.BlockSpec(rightf.shape, lambda b, p: (0, 0, 0))],
        out_specs=[pl.BlockSpec((None, t, 128), lambda b, p: (b, 0, p)),
                   pl.BlockSpec((None, t, 128), lambda b, p: (b, 0, p))],
        out_shape=[out, out],
        compiler_params=_cparams(("parallel", "parallel")), name=name)(proj3, lbs_row, gn_row, w_all, maskf, rightf)


def _exact_dot_r(parts, hs, ones_h):
    acc = jnp.dot(parts[0][:, hs], ones_h, preferred_element_type=F32)
    for p in parts[1:]:
        acc = acc + jnp.dot(p[:, hs], ones_h, preferred_element_type=F32)
    return acc


def _hgrn_bwd(proj3, o_raw, dmixed, lbs_row, gn_row, name):
    bsz, t, _ = proj3.shape
    nt = t // LANES
    nchunk = t // CHUNK
    w_all, maskf, rightf = _hgrn_tables()

    def body(a_ref, or_ref, do_ref, lb_ref, gn_ref, w_ref, mk_ref, rt_ref, da_ref, dgn_ref, dlb_ref, s_sc):
        lb = lb_ref[...]
        gn = gn_ref[...]
        eye, ones_blk, ones_h = _hgrn_consts()
        r_i, c_i = _iota((LANES, LANES), 0), _iota((LANES, LANES), 1)
        suffix = ((c_i >= r_i) & ((r_i // CHUNK) == (c_i // CHUNK))).astype(BF16)
        row64 = _iota((LANES, CHUNK), 0)
        ones_t = jnp.ones((LANES, CHUNK), BF16)

        def fwd_tile(i, carry):
            r0 = pl.multiple_of(i * LANES, LANES)
            a = a_ref[pl.ds(r0, LANES), :]
            _, kk, lf, _, _, _ = _hgrn_gates(a, lb)
            parts = _split(lf, 3)
            eb = jnp.exp(_exact_dot(w_ref[0:LANES, :], parts))
            ee = jnp.exp(_exact_dot(w_ref[LANES:2 * LANES, :], parts))
            vb, keb = a[:, 256:384].astype(BF16), (kk * ee).astype(BF16)
            new_s = []
            for h in range(2):
                hs = slice(CHUNK * h, CHUNK * (h + 1))
                st = carry[h]
                for c in range(2):
                    rc = slice(CHUNK * c, CHUNK * (c + 1))
                    s_sc[h, 2 * i + c] = st
                    st = st * eb[CHUNK * (c + 1) - 1:CHUNK * (c + 1), hs] + _dot(vb[rc, hs], keb[rc, hs], TN)
                new_s.append(st)
            return tuple(new_s)

        zero = jnp.zeros((CHUNK, CHUNK), F32)
        lax.fori_loop(0, nt, fwd_tile, (zero, zero))

        def bwd_tile(k, carry):
            dst0, dst1, dgn_acc, dlb_acc = carry
            i = nt - 1 - k
            r0 = pl.multiple_of(i * LANES, LANES)
            a = a_ref[pl.ds(r0, LANES), :]
            qa, ga = a[:, 0:128], a[:, 384:512]
            qq, kk, lf, sg, sgn, fg = _hgrn_gates(a, lb)
            parts = _split(lf, 3)
            eb = jnp.exp(_exact_dot(w_ref[0:LANES, :], parts))
            ee = jnp.exp(_exact_dot(w_ref[LANES:2 * LANES, :], parts))
            vb = a[:, 256:384].astype(BF16)
            oraw = or_ref[pl.ds(r0, LANES), :]
            dout = do_ref[pl.ds(r0, LANES), :]
            r = lax.rsqrt(_head_sums(oraw * oraw, ones_blk) * (1.0 / CHUNK) + NORM_EPS)
            xn = oraw * r
            dga = dout * (xn * gn) * _dsilu(ga)
            don = dout * _silu(ga)
            dgn_acc = dgn_acc + jnp.sum(don * xn, axis=0, keepdims=True)
            dxn = don * gn
            do = r * (dxn - xn * (_head_sums(dxn * xn, ones_blk) * (1.0 / CHUNK)))
            dob = do.astype(BF16)
            d_att = [_dot(dob[:, CHUNK * h:CHUNK * (h + 1)], vb[:, CHUNK * h:CHUNK * (h + 1)], NT) for h in range(2)]
            att, dq, dk, db_lv = _hgrn_levels(qq, kk, parts, w_ref, mk_ref, rt_ref, d_att)
            qk = _split(qq * kk, 2)
            qe_f, ke_f = qq * eb, kk * ee
            qeb, keb = qe_f.astype(BF16), ke_f.astype(BF16)
            new_ds, dq_h, dk_h, dv_h, dbl_h = [], [], [], [], []
            for h in range(2):
                hs = slice(CHUNK * h, CHUNK * (h + 1))
                a_h = att[h] + jnp.where(eye, _exact_dot_r(qk, hs, ones_h), 0.0)
                dv = _dot(a_h.astype(BF16), dob[:, hs], TN)
                ddiag = _exact_dot_r(_split(jnp.where(eye, d_att[h], 0.0), 2), slice(None), ones_t)
                dq_i = dq[:, hs] + ddiag * kk[:, hs]
                dk_i = dk[:, hs] + ddiag * qq[:, hs]
                dst = (dst0, dst1)[h]
                dq_c, dk_c, dv_c, dbl_c = [None, None], [None, None], [None, None], [None, None]
                for c in (1, 0):
                    rc = slice(CHUNK * c, CHUNK * (c + 1))
                    st_n = s_sc[h, 2 * i + c]
                    ebl = eb[CHUNK * (c + 1) - 1:CHUNK * (c + 1), hs]
                    dstb = dst.astype(BF16)
                    dv_c[c] = _dot(keb[rc, hs], dstb, NT)
                    dke = jnp.dot(vb[rc, hs], dstb, preferred_element_type=F32)
                    dqe = jnp.dot(dob[rc, hs], st_n.astype(BF16), preferred_element_type=F32)
                    dbl_c[c] = (jnp.sum(dst * st_n, axis=0, keepdims=True) * ebl
                                + jnp.sum(dke * ke_f[rc, hs], axis=0, keepdims=True))
                    dq_c[c], dk_c[c] = dqe * eb[rc, hs], dke * ee[rc, hs]
                    dst = dst * ebl + _dot(dob[rc, hs], qeb[rc, hs], TN)
                new_ds.append(dst)
                dq_x, dk_x = jnp.concatenate(dq_c, axis=0), jnp.concatenate(dk_c, axis=0)
                dq_h.append(dq_i + dq_x)
                dk_h.append(dk_i + dk_x)
                dv_h.append(dv + jnp.concatenate(dv_c, axis=0))
                dbl_h.append(qq[:, hs] * dq_x - kk[:, hs] * dk_x
                             + jnp.where(row64 == CHUNK - 1, dbl_c[0], 0.0) + jnp.where(row64 == LANES - 1, dbl_c[1], 0.0))
            dqq = jnp.concatenate(dq_h, axis=1)
            dkk = jnp.concatenate(dk_h, axis=1)
            dvv = jnp.concatenate(dv_h, axis=1)
            db = db_lv + jnp.concatenate(dbl_h, axis=1)
            dlf = _exact_dot(suffix, _split(db, 3))
            dqa = dqq * _dsilu(qa)
            dfg = jnp.where(fg > TINY, dlf / fg, 0.0)
            dz = (dfg - dkk) * (1.0 - lb) * sg * sgn
            dlb_acc = dlb_acc + jnp.sum(dfg * (1.0 - sg) - dkk * sgn, axis=0, keepdims=True)
            da_ref[pl.ds(r0, LANES), :] = jnp.concatenate([dqa, dz, dvv, dga], axis=1)
            return new_ds[0], new_ds[1], dgn_acc, dlb_acc

        zrow = jnp.zeros((1, LANES), F32)
        _, _, dgn_acc, dlb_acc = lax.fori_loop(0, nt, bwd_tile, (zero, zero, zrow, zrow))
        dgn_ref[...] = jnp.broadcast_to(dgn_acc, (8, LANES))
        dlb_ref[...] = jnp.broadcast_to(dlb_acc, (8, LANES))

    rows = jax.ShapeDtypeStruct((bsz, 8, HGRN_W), F32)
    row = pl.BlockSpec((1, 128), lambda b, p: (0, p))
    blk = pl.BlockSpec((None, t, 128), lambda b, p: (b, 0, p))
    return pl.pallas_call(
        body, grid=(bsz, 2),
        in_specs=[pl.BlockSpec((None, t, 512), lambda b, p: (b, 0, p)), blk, blk, row, row,
                  pl.BlockSpec(w_all.shape, lambda b, p: (0, 0)),
                  pl.BlockSpec(maskf.shape, lambda b, p: (0, 0, 0)),
                  pl.BlockSpec(rightf.shape, lambda b, p: (0, 0, 0))],
        out_specs=[pl.BlockSpec((None, t, 512), lambda b, p: (b, 0, p)),
                   pl.BlockSpec((None, 8, 128), lambda b, p: (b, 0, p)),
                   pl.BlockSpec((None, 8, 128), lambda b, p: (b, 0, p))],
        out_shape=[jax.ShapeDtypeStruct((bsz, t, A_W), F32), rows, rows],
        scratch_shapes=[pltpu.VMEM((2, nchunk, CHUNK, CHUNK), F32)],
        compiler_params=_cparams(("parallel", "parallel")), name=name)(
            proj3, o_raw, dmixed, lbs_row, gn_row, w_all, maskf, rightf)


def _pool_tt(t):
    return min(256, t)


def _window_select(s2, s4, s8, s16, lane):
    return jnp.where(lane < 64, s2, jnp.where(lane < 128, s4, jnp.where(lane < 192, s8, s16)))


def _pool_counts(t0, tt):
    lane = _iota((tt, POOL_W), 1)
    tpos = (_iota((tt, POOL_W), 0) + t0 + 1).astype(F32)
    win = jnp.where(lane < 64, 2.0, jnp.where(lane < 128, 4.0, jnp.where(lane < 192, 8.0, 16.0)))
    return 1.0 / jnp.minimum(tpos, win), lane


def _pooled_tile(upad_ref, i, tt):
    r0 = pl.multiple_of(i * tt, 8)
    cat = upad_ref[pl.ds(r0, tt + POOL_HALO), :]
    s2 = cat + pltpu.roll(cat, 1, 0)
    s4 = s2 + pltpu.roll(s2, 2, 0)
    s8 = s4 + pltpu.roll(s4, 4, 0)
    s16 = s8 + pltpu.roll(s8, 8, 0)
    inv, lane = _pool_counts(i * tt, tt)
    sel = _window_select(s2[POOL_HALO:], s4[POOL_HALO:], s8[POOL_HALO:], s16[POOL_HALO:], lane)
    return sel * inv - cat[POOL_HALO:], inv, lane


def _pool_fwd(proj3, wbd, scale_row, name):
    bsz, t, _ = proj3.shape
    tt = _pool_tt(t)

    def body(p_ref, w_ref, sc_ref, o_ref, upad):
        upad[0:POOL_HALO, :] = jnp.zeros((POOL_HALO, POOL_W), F32)
        upad[POOL_HALO:, :] = p_ref[:, 0:POOL_W]
        w = w_ref[...]
        sc = sc_ref[...]

        def tile(i, c):
            pooled, _, _ = _pooled_tile(upad, i, tt)
            r0 = pl.multiple_of(i * tt, 8)
            g = p_ref[pl.ds(r0, tt), POOL_W:2 * POOL_W]
            pre = jnp.dot(pooled.astype(BF16), w, preferred_element_type=F32)
            o_ref[pl.ds(r0, tt), :] = pre * sc * _silu(g)
            return c

        lax.fori_loop(0, t // tt, tile, 0)

    return pl.pallas_call(
        body, grid=(bsz,),
        in_specs=[pl.BlockSpec((None, t, 512), lambda b: (b, 0, B_BLK)),
                  pl.BlockSpec((POOL_W, POOL_W), lambda b: (0, 0)),
                  pl.BlockSpec((1, POOL_W), lambda b: (0, 0))],
        out_specs=pl.BlockSpec((None, t, POOL_W), lambda b: (b, 0, 0)),
        out_shape=jax.ShapeDtypeStruct((bsz, t, POOL_W), F32),
        scratch_shapes=[pltpu.VMEM((t + POOL_HALO, POOL_W), F32)],
        compiler_params=_cparams(("parallel",)), name=name)(proj3, wbd, scale_row)


def _pool_bwd(proj3, dmixed, wbd, scale_row, name):
    bsz, t, _ = proj3.shape
    tt = _pool_tt(t)

    def body(p_ref, do_ref, w_ref, sc_ref, db_ref, dsc_ref, dw_ref, upad, epad):
        upad[0:POOL_HALO, :] = jnp.zeros((POOL_HALO, POOL_W), F32)
        upad[POOL_HALO:, :] = p_ref[:, 0:POOL_W]
        epad[t:, :] = jnp.zeros((POOL_HALO, POOL_W), F32)
        w = w_ref[...]
        sc = sc_ref[...]

        def tile(i, carry):
            dsc_acc, dw_acc = carry
            pooled, inv, _ = _pooled_tile(upad, i, tt)
            r0 = pl.multiple_of(i * tt, 8)
            g = p_ref[pl.ds(r0, tt), POOL_W:2 * POOL_W]
            dout = do_ref[pl.ds(r0, tt), :]
            pb = pooled.astype(BF16)
            pre = jnp.dot(pb, w, preferred_element_type=F32)
            t1 = dout * _silu(g)
            dsc_acc = dsc_acc + jnp.sum(t1 * pre, axis=0, keepdims=True)
            dpre = (t1 * sc).astype(BF16)
            db_ref[pl.ds(r0, tt), POOL_W:2 * POOL_W] = dout * pre * sc * _dsilu(g)
            dw_acc = dw_acc + _dot(pb, dpre, TN)
            dpooled = _dot(dpre, w, NT)
            epad[pl.ds(r0, tt), :] = dpooled * inv
            return dsc_acc, dw_acc

        dsc_acc, dw_acc = lax.fori_loop(0, t // tt, tile, (jnp.zeros((1, POOL_W), F32), jnp.zeros((POOL_W, POOL_W), F32)))
        dsc_ref[...] = jnp.broadcast_to(dsc_acc, (8, POOL_W))
        dw_ref[...] = dw_acc

        def tile2(i, c):
            r0 = pl.multiple_of(i * tt, 8)
            n = tt + POOL_HALO
            cat = epad[pl.ds(r0, n), :]
            s2 = cat + pltpu.roll(cat, n - 1, 0)
            s4 = s2 + pltpu.roll(s2, n - 2, 0)
            s8 = s4 + pltpu.roll(s4, n - 4, 0)
            s16 = s8 + pltpu.roll(s8, n - 8, 0)
            inv, lane = _pool_counts(i * tt, tt)
            sel = _window_select(s2[:tt], s4[:tt], s8[:tt], s16[:tt], lane)
            db_ref[pl.ds(r0, tt), 0:POOL_W] = sel - cat[:tt] / inv
            return c

        lax.fori_loop(0, t // tt, tile2, 0)

    return pl.pallas_call(
        body, grid=(bsz,),
        in_specs=[pl.BlockSpec((None, t, 512), lambda b: (b, 0, B_BLK)),
                  pl.BlockSpec((None, t, POOL_W), lambda b: (b, 0, 1)),
                  pl.BlockSpec((POOL_W, POOL_W), lambda b: (0, 0)),
                  pl.BlockSpec((1, POOL_W), lambda b: (0, 0))],
        out_specs=[pl.BlockSpec((None, t, 512), lambda b: (b, 0, 0)),
                   pl.BlockSpec((None, 8, POOL_W), lambda b: (b, 0, 0)),
                   pl.BlockSpec((None, POOL_W, POOL_W), lambda b: (b, 0, 0))],
        out_shape=[jax.ShapeDtypeStruct((bsz, t, B_W), F32), jax.ShapeDtypeStruct((bsz, 8, POOL_W), F32),
                   jax.ShapeDtypeStruct((bsz, POOL_W, POOL_W), F32)],
        scratch_shapes=[pltpu.VMEM((t + POOL_HALO, POOL_W), F32), pltpu.VMEM((t + POOL_HALO, POOL_W), F32)],
        compiler_params=_cparams(("parallel",)), name=name)(proj3, dmixed, wbd, scale_row)


def _head_select_rows(hp):
    r, c = _iota((8, LANES), 0), _iota((8, LANES), 1)
    return ((r < 2) & (c == 2 * hp + r)).astype(F32)


def _foxgate_fwd(proj3, bias_row, name):
    bsz, t, _ = proj3.shape
    nt = t // LANES

    def body(f_ref, b_ref, cn_ref, ct_ref):
        bias = b_ref[...]
        i, j = _iota((LANES, LANES), 0), _iota((LANES, LANES), 1)
        lower = (j <= i).astype(F32)
        spread = (_iota((LANES, FOX_W), 0) == _iota((LANES, FOX_W), 1) // 64).astype(F32)

        def tile(k, carry):
            r0 = pl.multiple_of(k * LANES, LANES)
            xg = f_ref[pl.ds(r0, LANES), :] + bias
            lf = jnp.minimum(xg, 0.0) - jnp.log(1.0 + jnp.exp(-jnp.abs(xg)))
            c = jnp.dot(lower, lf, precision=HI, preferred_element_type=F32) + carry
            cn_ref[pl.ds(r0, LANES), :] = jnp.dot(c, spread, precision=HI, preferred_element_type=F32)
            for hp in range(4):
                ct_ref[hp, :, pl.ds(r0, LANES)] = _dot(_head_select_rows(hp), c, NT, precision=HI)
            return c[LANES - 1:LANES, :]

        lax.fori_loop(0, nt, tile, jnp.zeros((1, LANES), F32))

    return pl.pallas_call(
        body, grid=(bsz,),
        in_specs=[pl.BlockSpec((None, t, 128), lambda b: (b, 0, F_BLK)), pl.BlockSpec((1, 128), lambda b: (0, 0))],
        out_specs=[pl.BlockSpec((None, t, FOX_W), lambda b: (b, 0, 0)),
                   pl.BlockSpec((None, 4, 8, t), lambda b: (b, 0, 0, 0))],
        out_shape=[jax.ShapeDtypeStruct((bsz, t, FOX_W), F32), jax.ShapeDtypeStruct((bsz, 4, 8, t), F32)],
        compiler_params=_cparams(("parallel",)), name=name)(proj3, bias_row)


def _foxgate_bwd(proj3, dc_nat, bias_row, name):
    bsz, t, _ = proj3.shape
    nt = t // LANES

    def body(f_ref, dc_ref, b_ref, df_ref, dbias_ref, run_sc):
        bias = b_ref[...]
        i, j = _iota((LANES, LANES), 0), _iota((LANES, LANES), 1)
        upper = (j >= i).astype(F32)
        valid = _iota((1, LANES), 1) < FOX_HEADS
        run_sc[...] = jnp.zeros((8, LANES), F32)
        dbias_ref[...] = jnp.zeros((8, LANES), F32)

        def tile(k, c):
            r0 = pl.multiple_of((nt - 1 - k) * LANES, LANES)
            dc = dc_ref[pl.ds(r0, LANES), :] + jnp.where(i == LANES - 1, run_sc[0:1, :], 0.0)
            dlf = jnp.dot(upper, dc, precision=HI, preferred_element_type=F32)
            xg = f_ref[pl.ds(r0, LANES), :] + bias
            df = jnp.where(valid, dlf * _sig(-xg), 0.0)
            df_ref[pl.ds(r0, LANES), :] = df
            run_sc[...] = dlf[0:8, :]
            dbias_ref[...] += jnp.sum(df, axis=0, keepdims=True)
            return c

        lax.fori_loop(0, nt, tile, 0)

    blk = pl.BlockSpec((None, t, 128), lambda b: (b, 0, 0))
    return pl.pallas_call(
        body, grid=(bsz,),
        in_specs=[pl.BlockSpec((None, t, 128), lambda b: (b, 0, F_BLK)), blk, pl.BlockSpec((1, 128), lambda b: (0, 0))],
        out_specs=[blk, pl.BlockSpec((None, 8, 128), lambda b: (b, 0, 0))],
        out_shape=[jax.ShapeDtypeStruct((bsz, t, F_W), F32), jax.ShapeDtypeStruct((bsz, 8, 128), F32)],
        scratch_shapes=[pltpu.VMEM((8, LANES), F32)],
        compiler_params=_cparams(("parallel",)), name=name)(proj3, dc_nat, bias_row)


def _fox_tile(t):
    return min(256, t)


def _fox_fwd(proj3, c_nat, c_t, name):
    bsz, t, _ = proj3.shape
    tq = _fox_tile(t)
    nq = t // tq

    def body(q_ref, kv_ref, cn_ref, ct_ref, og_ref, or_ref, lse_ref):
        i = pl.program_id(2)
        qblk = q_ref[...]
        qs = [(qblk[:, 64 * h:64 * (h + 1)] * 0.125).astype(BF16) for h in range(2)]
        cqs = [cn_ref[:, 64 * h:64 * h + 1] for h in range(2)]
        ones = jnp.ones((tq, 64), BF16)
        causal = _iota((tq, tq), 0) >= _iota((tq, tq), 1)

        def kv_step(j, carry, diagonal):
            c0 = pl.multiple_of(j * tq, tq)
            new = []
            for h in range(2):
                m, acc = carry[2 * h], carry[2 * h + 1]
                kh = kv_ref[pl.ds(c0, tq), 128 + 64 * h:128 + 64 * (h + 1)].astype(BF16)
                vh = jnp.concatenate([kv_ref[pl.ds(c0, tq), 256 + 64 * h:256 + 64 * (h + 1)].astype(BF16), ones], axis=1)
                s = _dot(qs[h], kh, NT) + (cqs[h] - ct_ref[h:h + 1, pl.ds(c0, tq)])
                if diagonal:
                    s = jnp.where(causal, s, MASK_VALUE)
                m_new = jnp.maximum(m, jnp.max(s, axis=1, keepdims=True))
                p = jnp.exp(s - m_new).astype(BF16)
                new += [m_new, jnp.exp(m - m_new) * acc + jnp.dot(p, vh, preferred_element_type=F32)]
            return tuple(new)

        init = (jnp.full((tq, 1), MASK_VALUE, F32), jnp.zeros((tq, 128), F32)) * 2
        carry = lax.fori_loop(0, i, functools.partial(kv_step, diagonal=False), init)
        carry = kv_step(i, carry, True)
        outs, lses = [], []
        for h in range(2):
            m, acc = carry[2 * h], carry[2 * h + 1]
            outs.append(acc[:, 0:64] / acc[:, 64:128])
            lses.append(m + jnp.log(acc[:, 64:128]))
        o = jnp.concatenate(outs, axis=1)
        or_ref[...] = o
        og_ref[...] = o * _silu(qblk[:, 384:512])
        lse_ref[...] = jnp.concatenate(lses, axis=1)

    out = jax.ShapeDtypeStruct((bsz, t, FOX_W), F32)
    blk = pl.BlockSpec((None, tq, 128), lambda b, p, i: (b, i, p))
    return pl.pallas_call(
        body, grid=(bsz, 4, nq),
        in_specs=[pl.BlockSpec((None, tq, 512), lambda b, p, i: (b, i, C_BLK0 + p)),
                  pl.BlockSpec((None, t, 512), lambda b, p, i: (b, 0, C_BLK0 + p)),
                  blk,
                  pl.BlockSpec((None, None, 8, t), lambda b, p, i: (b, p, 0, 0))],
        out_specs=[blk, blk, blk],
        out_shape=[out, out, out],
        compiler_params=_cparams(("parallel", "parallel", "arbitrary")), name=name)(proj3, proj3, c_nat, c_t)


def _fox_bwd(proj3, o_raw, dmixed, lse, c_nat, c_t, name):
    bsz, t, _ = proj3.shape
    tq = _fox_tile(t)
    nq = t // tq

    def body(a_ref, or_ref, do_ref, lse_ref, cn_ref, ct_ref, dc_out, dct_out, drow_out, dq_sc, do_sc, dl_sc):
        def prep(i, c):
            r0 = pl.multiple_of(i * tq, tq)
            g = a_ref[pl.ds(r0, tq), 384:512]
            dout = do_ref[pl.ds(r0, tq), :]
            o = or_ref[pl.ds(r0, tq), :]
            dc_out[pl.ds(r0, tq), 384:512] = dout * o * _dsilu(g)
            do = dout * _silu(g)
            do_sc[pl.ds(r0, tq), :] = do
            prod = do * o
            d0 = jnp.sum(prod[:, 0:64], axis=1, keepdims=True)
            d1 = jnp.sum(prod[:, 64:128], axis=1, keepdims=True)
            dl_sc[pl.ds(r0, tq), :] = jnp.concatenate([jnp.broadcast_to(d0, (tq, 64)), jnp.broadcast_to(d1, (tq, 64))], axis=1)
            dq_sc[pl.ds(r0, tq), :] = jnp.zeros((tq, 128), F32)
            drow_out[pl.ds(r0, tq), :] = jnp.zeros((tq, 128), F32)
            return c

        lax.fori_loop(0, nq, prep, 0)
        dct_out[...] = jnp.zeros((8, t), F32)

        causal = _iota((tq, tq), 0) >= _iota((tq, tq), 1)

        def kv_tile(j, c):
            c0 = pl.multiple_of(j * tq, tq)
            ks = [a_ref[pl.ds(c0, tq), 128 + 64 * h:128 + 64 * (h + 1)].astype(BF16) for h in range(2)]
            vs = [a_ref[pl.ds(c0, tq), 256 + 64 * h:256 + 64 * (h + 1)].astype(BF16) for h in range(2)]
            cks = [ct_ref[h:h + 1, pl.ds(c0, tq)] for h in range(2)]

            def q_step(i, carry, diagonal):
                r0 = pl.multiple_of(i * tq, tq)
                new = []
                for h in range(2):
                    dk, dv, dcol = carry[3 * h:3 * h + 3]
                    hs = slice(64 * h, 64 * (h + 1))
                    qh = (a_ref[pl.ds(r0, tq), hs] * 0.125).astype(BF16)
                    doh = do_sc[pl.ds(r0, tq), hs].astype(BF16)
                    lse_h = lse_ref[pl.ds(r0, tq), 64 * h:64 * h + 1]
                    dl_h = dl_sc[pl.ds(r0, tq), 64 * h:64 * h + 1]
                    cq = cn_ref[pl.ds(r0, tq), 64 * h:64 * h + 1]
                    p = jnp.exp(_dot(qh, ks[h], NT) + (cq - cks[h]) - lse_h)
                    if diagonal:
                        p = jnp.where(causal, p, 0.0)
                    dv = dv + _dot(p.astype(BF16), doh, TN)
                    ds = p * (_dot(doh, vs[h], NT) - dl_h)
                    dsb = ds.astype(BF16)
                    dq_sc[pl.ds(r0, tq), hs] += jnp.dot(dsb, ks[h], preferred_element_type=F32) * 0.125
                    dk = dk + _dot(dsb, qh, TN)
                    dcol = dcol - jnp.sum(ds, axis=0, keepdims=True)
                    drow_out[pl.ds(r0, tq), hs] += jnp.broadcast_to(jnp.sum(ds, axis=1, keepdims=True), (tq, 64))
                    new += [dk, dv, dcol]
                return tuple(new)

            init = (jnp.zeros((tq, 64), F32), jnp.zeros((tq, 64), F32), jnp.zeros((1, tq), F32)) * 2
            carry = q_step(j, init, True)
            carry = lax.fori_loop(j + 1, nq, functools.partial(q_step, diagonal=False), carry)
            for h in range(2):
                dct_out[h:h + 1, pl.ds(c0, tq)] = carry[3 * h + 2]
            dc_out[pl.ds(c0, tq), 128:256] = jnp.concatenate([carry[0], carry[3]], axis=1)
            dc_out[pl.ds(c0, tq), 256:384] = jnp.concatenate([carry[1], carry[4]], axis=1)
            return c

        lax.fori_loop(0, nq, kv_tile, 0)
        dc_out[:, 0:128] = dq_sc[...]

    blk = pl.BlockSpec((None, t, 128), lambda b, p: (b, 0, p))
    return pl.pallas_call(
        body, grid=(bsz, 4),
        in_specs=[pl.BlockSpec((None, t, 512), lambda b, p: (b, 0, C_BLK0 + p)),
                  blk,
                  pl.BlockSpec((None, t, 128), lambda b, p: (b, 0, 4 + p)),
                  blk, blk,
                  pl.BlockSpec((None, None, 8, t), lambda b, p: (b, p, 0, 0))],
        out_specs=[pl.BlockSpec((None, t, 512), lambda b, p: (b, 0, p)),
                   pl.BlockSpec((None, None, 8, t), lambda b, p: (b, p, 0, 0)), blk],
        out_shape=[jax.ShapeDtypeStruct((bsz, t, C_W), F32), jax.ShapeDtypeStruct((bsz, 4, 8, t), F32),
                   jax.ShapeDtypeStruct((bsz, t, FOX_W), F32)],
        scratch_shapes=[pltpu.VMEM((t, 128), F32), pltpu.VMEM((t, 128), F32), pltpu.VMEM((t, 128), F32)],
        compiler_params=_cparams(("parallel", "parallel")), name=name)(proj3, o_raw, dmixed, lse, c_nat, c_t)


def _mix_tm(n):
    return min(512, n)


def _outproj_fwd(x2, oa, ob, oc, wo, g_row, name):
    n, d = x2.shape
    tm = _mix_tm(n)

    def body(x_ref, oa_ref, ob_ref, oc_ref, w_ref, g_ref, y_ref, xo_ref):
        y = (jnp.dot(oa_ref[...].astype(BF16), w_ref[0:256, :], preferred_element_type=F32)
             + jnp.dot(ob_ref[...].astype(BF16), w_ref[256:512, :], preferred_element_type=F32)
             + jnp.dot(oc_ref[...].astype(BF16), w_ref[512:1024, :], preferred_element_type=F32))
        y_ref[...] = y
        xo_ref[...] = x_ref[...] + y * _rstd(y) * g_ref[...]

    row = lambda w: pl.BlockSpec((tm, w), lambda i: (i, 0))
    out = jax.ShapeDtypeStruct((n, d), F32)
    return pl.pallas_call(
        body, grid=(n // tm,),
        in_specs=[row(d), row(256), row(256), row(512), pl.BlockSpec((d, d), lambda i: (0, 0)),
                  pl.BlockSpec((1, d), lambda i: (0, 0))],
        out_specs=[row(d), row(d)], out_shape=[out, out],
        compiler_params=_cparams(("parallel",)), name=name)(x2, oa, ob, oc, wo, g_row)


def _loss_head(x2, target2, name):
    n, d = x2.shape
    tm = _mix_tm(n)

    def body(x_ref, t_ref, dx_ref, l_ref):
        err = x_ref[...] - t_ref[...]
        dx_ref[...] = err * (1.0 / d)

        @pl.when(pl.program_id(0) == 0)
        def _():
            l_ref[...] = jnp.zeros((8, 128), F32)

        l_ref[...] += jnp.sum(err * err)

    row = pl.BlockSpec((tm, d), lambda i: (i, 0))
    return pl.pallas_call(
        body, grid=(n // tm,), in_specs=[row, row],
        out_specs=[row, pl.BlockSpec((8, 128), lambda i: (0, 0))],
        out_shape=[jax.ShapeDtypeStruct((n, d), F32), jax.ShapeDtypeStruct((8, 128), F32)],
        compiler_params=_cparams(("arbitrary",)), name=name)(x2, target2)


def _outproj_bwd(dxo, y, oa, ob, oc, wo, g_row, name):
    n, d = dxo.shape
    tm = _mix_tm(n)

    def body(dx_ref, y_ref, oa_ref, ob_ref, oc_ref, w_ref, g_ref, dm_ref, dw_ref, dg_ref):
        @pl.when(pl.program_id(0) == 0)
        def _():
            dw_ref[...] = jnp.zeros((d, d), F32)
            dg_ref[...] = jnp.zeros((8, d), F32)

        yv, dx = y_ref[...], dx_ref[...]
        r = _rstd(yv)
        yn = yv * r
        dg_ref[...] += jnp.sum(dx * yn, axis=0, keepdims=True)
        dyn = dx * g_ref[...]
        dy = (r * (dyn - yn * jnp.mean(dyn * yn, axis=-1, keepdims=True))).astype(BF16)
        dm_ref[...] = _dot(dy, w_ref[...], NT)
        dw_ref[0:256, :] += _dot(oa_ref[...].astype(BF16), dy, TN)
        dw_ref[256:512, :] += _dot(ob_ref[...].astype(BF16), dy, TN)
        dw_ref[512:1024, :] += _dot(oc_ref[...].astype(BF16), dy, TN)

    row = lambda w: pl.BlockSpec((tm, w), lambda i: (i, 0))
    fixed = lambda r, c: pl.BlockSpec((r, c), lambda i: (0, 0))
    return pl.pallas_call(
        body, grid=(n // tm,),
        in_specs=[row(d), row(d), row(256), row(256), row(512), fixed(d, d), fixed(1, d)],
        out_specs=[row(d), fixed(d, d), fixed(8, d)],
        out_shape=[jax.ShapeDtypeStruct((n, d), F32), jax.ShapeDtypeStruct((d, d), F32), jax.ShapeDtypeStruct((8, d), F32)],
        compiler_params=_cparams(("arbitrary",)), name=name)(dxo, y, oa, ob, oc, wo, g_row)


_PIECES = ((0, A_W), (A_W, B_W), (A_W + B_W, C_W), (A_W + B_W + C_W, F_W))


def _inproj_bwd_x(x2, dxo, g_row, w_int, pieces, name):
    n, d = x2.shape
    tm = min(256, n)

    def body(x_ref, dxo_ref, g_ref, w_ref, da_ref, db_ref, dc_ref, df_ref, dx_ref, dg_ref):
        @pl.when(pl.program_id(0) == 0)
        def _():
            dg_ref[...] = jnp.zeros((8, d), F32)

        dh = jnp.zeros((tm, d), F32)
        for ref, (o, w) in zip((da_ref, db_ref, dc_ref, df_ref), _PIECES):
            dh = dh + _dot(ref[...].astype(BF16), w_ref[:, o:o + w], NT)
        x = x_ref[...]
        r = _rstd(x)
        xn = x * r
        dg_ref[...] += jnp.sum(dh * xn, axis=0, keepdims=True)
        dxn = dh * g_ref[...]
        dx_ref[...] = dxo_ref[...] + r * (dxn - xn * jnp.mean(dxn * xn, axis=-1, keepdims=True))

    row = lambda w: pl.BlockSpec((tm, w), lambda i: (i, 0))
    fixed = lambda r, c: pl.BlockSpec((r, c), lambda i: (0, 0))
    return pl.pallas_call(
        body, grid=(n // tm,),
        in_specs=[row(d), row(d), fixed(1, d), fixed(d, E_INT)] + [row(w) for _, w in _PIECES],
        out_specs=[row(d), fixed(8, d)],
        out_shape=[jax.ShapeDtypeStruct((n, d), F32), jax.ShapeDtypeStruct((8, d), F32)],
        compiler_params=_cparams(("arbitrary",)), name=name)(x2, dxo, g_row, w_int, *pieces)


def _inproj_bwd_w(x2, g_row, piece, name):
    n, d = x2.shape
    w = piece.shape[1]
    tm = min(512, n)

    def body(x_ref, g_ref, dp_ref, dw_ref):
        @pl.when(pl.program_id(0) == 0)
        def _():
            dw_ref[...] = jnp.zeros((d, w), F32)

        x = x_ref[...]
        h = (x * _rstd(x) * g_ref[...]).astype(BF16)
        dw_ref[...] += _dot(h, dp_ref[...].astype(BF16), TN)

    return pl.pallas_call(
        body, grid=(n // tm,),
        in_specs=[pl.BlockSpec((tm, d), lambda i: (i, 0)), pl.BlockSpec((1, d), lambda i: (0, 0)),
                  pl.BlockSpec((tm, w), lambda i: (i, 0))],
        out_specs=pl.BlockSpec((d, w), lambda i: (0, 0)),
        out_shape=jax.ShapeDtypeStruct((d, w), F32),
        compiler_params=_cparams(("arbitrary",)), name=name)(x2, g_row, piece)


def _block_diag(pool_w_l):
    z = jnp.zeros((64, 64), pool_w_l.dtype)
    return jnp.concatenate(
        [jnp.concatenate([pool_w_l[g] if c == g else z for c in range(4)], axis=1) for g in range(4)], axis=0)


def _pad_lanes(v, width=128):
    return jnp.pad(v, ((0, 0),) * (v.ndim - 1) + ((0, width - v.shape[-1]),))


def _local_step(x, target, lower_bounds, pre_norm_g, w_in_int, hgrn_norm_g, fox_f_bias, pool_w, pool_scale,
                w_out_bf, post_norm_g):
    bsz, t, d = x.shape
    n = bsz * t
    lbs = _lbs_fwd(lower_bounds)
    saved = []
    xc = x.reshape(n, d)
    for l in range(DEPTH):
        proj = _inproj_fwd(xc, pre_norm_g[l:l + 1], w_in_int[l], f"inproj_fwd{l}").reshape(bsz, t, E_INT)
        wbd = _block_diag(pool_w[l]).astype(BF16)
        bias_row = _pad_lanes(fox_f_bias[l:l + 1])
        oa, oa_raw = _hgrn_fwd(proj, lbs[l:l + 1], hgrn_norm_g[l:l + 1], f"hgrn_fwd{l}")
        ob = _pool_fwd(proj, wbd, pool_scale[l:l + 1], f"pool_fwd{l}")
        c_nat, c_t = _foxgate_fwd(proj, bias_row, f"foxgate_fwd{l}")
        oc, oc_raw, lse = _fox_fwd(proj, c_nat, c_t, f"fox_fwd{l}")
        y, xn = _outproj_fwd(xc, oa.reshape(n, -1), ob.reshape(n, -1), oc.reshape(n, -1), w_out_bf[l],
                             post_norm_g[l:l + 1], f"outproj_fwd{l}")
        saved.append((xc, proj, wbd, bias_row, oa, oa_raw, ob, oc, oc_raw, lse, c_nat, c_t, y))
        xc = xn
    dx, sq = _loss_head(xc, target.reshape(n, d), "loss_head")
    g = {k: [None] * DEPTH for k in ("pre", "w_in", "hgn", "bias", "pool_w", "pool_scale", "w_out", "post", "lbs")}
    for l in reversed(range(DEPTH)):
        xin, proj, wbd, bias_row, oa, oa_raw, ob, oc, oc_raw, lse, c_nat, c_t, y = saved[l]
        dmix, g["w_out"][l], dpost = _outproj_bwd(dx, y, oa.reshape(n, -1), ob.reshape(n, -1), oc.reshape(n, -1),
                                                  w_out_bf[l], post_norm_g[l:l + 1], f"outproj_bwd{l}")
        g["post"][l] = dpost[0]
        dmix3 = dmix.reshape(bsz, t, d)
        d_c, dct, drow = _fox_bwd(proj, oc_raw, dmix3, lse, c_nat, c_t, f"fox_bwd{l}")
        dc_nat = _pad_lanes(dct[:, :, 0:2, :].reshape(bsz, FOX_HEADS, t).transpose(0, 2, 1)
                            + drow.reshape(bsz, t, FOX_HEADS, 64)[..., 0])
        d_f, dbias = _foxgate_bwd(proj, dc_nat, bias_row, f"foxgate_bwd{l}")
        g["bias"][l] = jnp.sum(dbias[:, 0, :FOX_HEADS], axis=0)
        d_b, dscale, dwbd = _pool_bwd(proj, dmix3, wbd, pool_scale[l:l + 1], f"pool_bwd{l}")
        g["pool_scale"][l] = jnp.sum(dscale[:, 0], axis=0)
        dwbd = jnp.sum(dwbd, axis=0)
        g["pool_w"][l] = jnp.stack([dwbd[64 * k:64 * (k + 1), 64 * k:64 * (k + 1)] for k in range(4)])
        d_a, dgn, dlb = _hgrn_bwd(proj, oa_raw, dmix3, lbs[l:l + 1], hgrn_norm_g[l:l + 1], f"hgrn_bwd{l}")
        g["hgn"][l] = jnp.sum(dgn[:, 0], axis=0)
        g["lbs"][l] = jnp.sum(dlb[:, 0], axis=0)
        pieces = [p.reshape(n, -1) for p in (d_a, d_b, d_c, d_f)]
        g["w_in"][l] = jnp.concatenate(
            [_inproj_bwd_w(xin, pre_norm_g[l:l + 1], p, f"inproj_bwd_w{l}_{k}") for k, p in enumerate(pieces)], axis=1)
        dx, dpre = _inproj_bwd_x(xin, dx, pre_norm_g[l:l + 1], w_in_int[l], pieces, f"inproj_bwd_x{l}")
        g["pre"][l] = dpre[0]
    grads = {k: jnp.stack(v) for k, v in g.items()}
    return sq, dx.reshape(bsz, t, d), grads


def _place():
    return lax.axis_index("x"), lax.axis_index("y"), lax.axis_index("c")


def _other_chips(x, y):
    return [(1 - x, y), (x, 1 - y), (1 - x, 1 - y)]


_ANY = pl.BlockSpec(memory_space=pl.ANY)


def _gather_weights(w_in_sh, w_out_sh):
    def body(win_ref, wout_ref, ain_ref, aout_ref, send_sems, recv_sems, local_sems):
        x, y, c = _place()
        me = 2 * x + y
        mine = [pltpu.make_async_copy(win_ref, ain_ref.at[me], local_sems.at[0]),
                pltpu.make_async_copy(wout_ref, aout_ref.at[me], local_sems.at[1])]
        for cp in mine:
            cp.start()
        sends = []
        for k, (px, py) in enumerate(_other_chips(x, y)):
            for j, (src, dst) in enumerate(((win_ref, ain_ref), (wout_ref, aout_ref))):
                sends.append(pltpu.make_async_remote_copy(
                    src_ref=src, dst_ref=dst.at[me], send_sem=send_sems.at[2 * k + j], recv_sem=recv_sems.at[2 * k + j],
                    device_id=(px, py, c), device_id_type=MESH))
        for cp in sends:
            cp.start()
        for k, (px, py) in enumerate(_other_chips(x, y)):
            for j, (src, dst) in enumerate(((win_ref, ain_ref), (wout_ref, aout_ref))):
                pltpu.make_async_remote_copy(
                    src_ref=src, dst_ref=dst.at[2 * px + py], send_sem=send_sems.at[2 * k + j],
                    recv_sem=recv_sems.at[2 * k + j], device_id=(px, py, c), device_id_type=MESH).wait_recv()
        for cp in sends:
            cp.wait_send()
        for cp in mine:
            cp.wait()

    return pl.pallas_call(
        body, in_specs=[_ANY, _ANY], out_specs=[_ANY, _ANY],
        out_shape=[jax.ShapeDtypeStruct((N_CHIPS,) + w_in_sh.shape, w_in_sh.dtype),
                   jax.ShapeDtypeStruct((N_CHIPS,) + w_out_sh.shape, w_out_sh.dtype)],
        scratch_shapes=[pltpu.SemaphoreType.DMA((6,)), pltpu.SemaphoreType.DMA((6,)), pltpu.SemaphoreType.DMA((2,))],
        name="gather_weights")(w_in_sh, w_out_sh)


def _swap_with_sibling(parts, name):
    k = len(parts)

    def body(*refs):
        src, dst = refs[:k], refs[k:2 * k]
        send_sems, recv_sems = refs[2 * k:]
        x, y, c = _place()
        cps = [pltpu.make_async_remote_copy(src_ref=src[j], dst_ref=dst[j], send_sem=send_sems.at[j], recv_sem=recv_sems.at[j],
                                            device_id=(x, y, 1 - c), device_id_type=MESH) for j in range(k)]
        for cp in cps:
            cp.start()
        for cp in cps:
            cp.wait()

    return pl.pallas_call(
        body, in_specs=[_ANY] * k, out_specs=[_ANY] * k,
        out_shape=[jax.ShapeDtypeStruct(p.shape, p.dtype) for p in parts],
        scratch_shapes=[pltpu.SemaphoreType.DMA((k,)), pltpu.SemaphoreType.DMA((k,))], name=name)(*parts)


def _scatter_to_chips(parts, name):
    k = len(parts)

    def body(*refs):
        src, dst = refs[:k], refs[k:2 * k]
        send_sems, recv_sems = refs[2 * k:]
        x, y, c = _place()
        me = 2 * x + y
        cps = []
        for rel, (px, py) in enumerate(_other_chips(x, y)):
            for j in range(k):
                cps.append(pltpu.make_async_remote_copy(
                    src_ref=src[j].at[2 * px + py], dst_ref=dst[j].at[rel], send_sem=send_sems.at[rel * k + j],
                    recv_sem=recv_sems.at[rel * k + j], device_id=(px, py, c), device_id_type=MESH))
        for cp in cps:
            cp.start()
        for cp in cps:
            cp.wait()
        del me

    return pl.pallas_call(
        body, in_specs=[_ANY] * k, out_specs=[_ANY] * k,
        out_shape=[jax.ShapeDtypeStruct((3,) + p.shape[1:], p.dtype) for p in parts],
        scratch_shapes=[pltpu.SemaphoreType.DMA((3 * k,)), pltpu.SemaphoreType.DMA((3 * k,))], name=name)(*parts)


def _add_n(parts, name):
    r, c = parts[0].shape
    tr = 256 if r % 256 == 0 else r

    def body(*refs):
        acc = refs[0][...]
        for ref in refs[1:-1]:
            acc = acc + ref[...]
        refs[-1][...] = acc

    blk = pl.BlockSpec((tr, c), lambda i: (i, 0))
    return pl.pallas_call(
        body, grid=(r // tr,), in_specs=[blk] * len(parts), out_specs=blk,
        out_shape=jax.ShapeDtypeStruct((r, c), F32), compiler_params=_cparams(("parallel",)), name=name)(*parts)


def _all_reduce_small(packet):
    r, w = packet.shape

    def body(p_ref, o_ref, buf, send_sems, recv_sems):
        x, y, c = _place()
        me = 4 * x + 2 * y + c
        buf[me] = p_ref[...]
        peers = []
        for k in range(1, 8):
            fx, fy, fc = (k >> 2) & 1, (k >> 1) & 1, k & 1
            peers.append((x ^ fx, y ^ fy, c ^ fc))
        cps = [pltpu.make_async_remote_copy(src_ref=p_ref, dst_ref=buf.at[me], send_sem=send_sems.at[k], recv_sem=recv_sems.at[k],
                                            device_id=peer, device_id_type=MESH) for k, peer in enumerate(peers)]
        for cp in cps:
            cp.start()
        for k, (px, py, pc) in enumerate(peers):
            pltpu.make_async_remote_copy(src_ref=p_ref, dst_ref=buf.at[4 * px + 2 * py + pc], send_sem=send_sems.at[k],
                                         recv_sem=recv_sems.at[k], device_id=(px, py, pc), device_id_type=MESH).wait_recv()
        for cp in cps:
            cp.wait_send()
        acc = buf[0]
        for k in range(1, 8):
            acc = acc + buf[k]
        o_ref[...] = acc

    vm = pl.BlockSpec(memory_space=pltpu.VMEM)
    return pl.pallas_call(
        body, in_specs=[vm], out_specs=vm, out_shape=jax.ShapeDtypeStruct((r, w), F32),
        scratch_shapes=[pltpu.VMEM((8, r, w), F32), pltpu.SemaphoreType.DMA((7,)), pltpu.SemaphoreType.DMA((7,))],
        name="all_reduce_small")(packet)


def _adamw_math(w, g, m, v):
    m = ADAM_B1 * m + (1.0 - ADAM_B1) * g
    v = ADAM_B2 * v + (1.0 - ADAM_B2) * (g * g)
    m_hat = m / (1.0 - ADAM_B1 ** ADAM_STEP)
    v_hat = v / (1.0 - ADAM_B2 ** ADAM_STEP)
    return -ADAM_LR * (m_hat / (jnp.sqrt(v_hat) + ADAM_EPS) + ADAM_WD * w), m, v


def _adamw(w, g, m, v, name):
    nl, r, c = w.shape
    tr = 256 if r % 256 == 0 else r

    def body(w_ref, g_ref, m_ref, v_ref, d_ref, mo_ref, vo_ref):
        d_ref[...], mo_ref[...], vo_ref[...] = _adamw_math(w_ref[...], g_ref[...], m_ref[...], v_ref[...])

    blk = pl.BlockSpec((None, tr, c), lambda l, i: (l, i, 0))
    out = jax.ShapeDtypeStruct(w.shape, F32)
    return pl.pallas_call(
        body, grid=(nl, r // tr), in_specs=[blk] * 4, out_specs=[blk] * 3, out_shape=[out] * 3,
        compiler_params=_cparams(("parallel", "parallel")), name=name)(w, g, m, v)


def _small_update(gsum, lower_bounds, wpack, mpack, vpack):
    r, w = gsum.shape
    lb_rows = DEPTH * HGRN_W // 128

    def body(g_ref, a_ref, w_ref, m_ref, v_ref, go_ref, d_ref, mo_ref, vo_ref):
        a = a_ref[...]
        a0, a1 = a[0:1], a[1:2]
        mx = jnp.maximum(a0, a1)
        e0, e1 = jnp.exp(a0 - mx), jnp.exp(a1 - mx)
        p0, p1 = e0 / (e0 + e1), e1 / (e0 + e1)
        g = g_ref[...]
        half = lb_rows // 2
        dl0 = jnp.concatenate([g[k:k + 1] for k in range(half)], axis=1)
        dl1 = jnp.concatenate([g[half + k:half + k + 1] for k in range(half)], axis=1)
        dp0 = (dl0 + dl1) - (dl0 + dl1)
        dp1 = dl1
        inner = p0 * dp0 + p1 * dp1
        da0, da1 = p0 * (dp0 - inner), p1 * (dp1 - inner)
        rows = [da0[:, 128 * k:128 * (k + 1)] for k in range(half)] + [da1[:, 128 * k:128 * (k + 1)] for k in range(half)]
        gfull = jnp.concatenate(rows + [g[lb_rows:]], axis=0)
        go_ref[...] = gfull
        d_ref[...], mo_ref[...], vo_ref[...] = _adamw_math(w_ref[...], gfull, m_ref[...], v_ref[...])

    vm = pl.BlockSpec(memory_space=pltpu.VMEM)
    out = jax.ShapeDtypeStruct((r, w), F32)
    return pl.pallas_call(body, in_specs=[vm] * 5, out_specs=[vm] * 4, out_shape=[out] * 4, name="small_update")(
        gsum, lower_bounds, wpack, mpack, vpack)


_SMALL = ("lower_bounds", "pre_norm_g", "hgrn_norm_g", "fox_f_bias", "pool_w", "pool_scale", "post_norm_g")


def _pack(parts):
    rows = []
    for k in _SMALL:
        f = parts[k].reshape(-1)
        pad = (-f.shape[0]) % (8 * 128)
        rows.append(jnp.pad(f, (0, pad)).reshape(-1, 128))
    rows.append(jnp.zeros((8, 128), F32))
    return jnp.concatenate(rows, axis=0)


def _unpack(pack, like):
    out, r = {}, 0
    for k in _SMALL:
        size = int(np.prod(like[k].shape))
        nr = -(-size // (8 * 128)) * 8
        out[k] = pack[r:r + nr].reshape(-1)[:size].reshape(like[k].shape)
        r += nr
    return out, r


def kernel(x, lower_bounds, pre_norm_g, w_in, hgrn_norm_g, fox_f_bias, pool_w, pool_scale, w_out, post_norm_g, loss_target, m_lower_bounds, m_pre_norm_g, m_w_in, m_hgrn_norm_g, m_fox_f_bias, m_pool_w, m_pool_scale, m_w_out, m_post_norm_g, v_lower_bounds, v_pre_norm_g, v_w_in, v_hgrn_norm_g, v_fox_f_bias, v_pool_w, v_pool_scale, v_w_out, v_post_norm_g):
    cx, cy, cc = _place()
    chip = 2 * cx + cy

    ain, aout = _gather_weights(w_in.astype(BF16), w_out.astype(BF16))
    w_in_full = jnp.concatenate([ain[q] for q in range(N_CHIPS)], axis=-1)
    w_in_int = _to_internal(w_in_full)
    w_out_full = jnp.concatenate([aout[q] for q in range(N_CHIPS)], axis=1)

    sq, grad_x, g = _local_step(x, loss_target, lower_bounds, pre_norm_g, w_in_int, hgrn_norm_g, fox_f_bias, pool_w,
                                pool_scale, w_out_full, post_norm_g)

    gin = _to_original(g["w_in"])
    gin_blocks = jnp.stack([gin[:, :, SHARD_W * q:SHARD_W * (q + 1)] for q in range(N_CHIPS)])
    gout_blocks = g["w_out"].reshape(DEPTH, N_CHIPS, 256, D_MODEL).transpose(1, 0, 2, 3)
    take = lambda a, l: lax.dynamic_index_in_dim(a, l, axis=1, keepdims=False)
    mine_in, mine_out = take(gin_blocks, cc), take(gout_blocks, cc)
    sib_in, sib_out = _swap_with_sibling([take(gin_blocks, 1 - cc), take(gout_blocks, 1 - cc)], "grad_swap1")
    rin, rout = 4 * 1024, 4 * 256
    sum_in = _add_n([mine_in.reshape(rin, SHARD_W), sib_in.reshape(rin, SHARD_W)], "grad_add1_in").reshape(4, 1024, SHARD_W)
    sum_out = _add_n([mine_out.reshape(rout, D_MODEL), sib_out.reshape(rout, D_MODEL)], "grad_add1_out").reshape(4, 256, D_MODEL)
    got_in, got_out = _scatter_to_chips([sum_in, sum_out], "grad_scatter")
    own = lambda a: lax.dynamic_index_in_dim(a, chip, axis=0, keepdims=False)
    half_in = _add_n([own(sum_in)] + [got_in[k] for k in range(3)], "grad_add2_in")
    half_out = _add_n([own(sum_out)] + [got_out[k] for k in range(3)], "grad_add2_out")
    oth_in, oth_out = _swap_with_sibling([half_in, half_out], "grad_swap2")
    first = cc == 0
    grad_w_in = jnp.stack([jnp.where(first, half_in, oth_in), jnp.where(first, oth_in, half_in)])
    grad_w_out = jnp.stack([jnp.where(first, half_out, oth_out), jnp.where(first, oth_out, half_out)])

    small = {"lower_bounds": g["lbs"], "pre_norm_g": g["pre"], "hgrn_norm_g": g["hgn"], "fox_f_bias": g["bias"],
             "pool_w": g["pool_w"], "pool_scale": g["pool_scale"], "post_norm_g": g["post"]}
    packet = _pack(small)
    nrows = packet.shape[0]
    packet = packet.at[nrows - 1].set(sq[0])
    gsum = _all_reduce_small(packet)
    loss = gsum[nrows - 1, 0] * (0.5 / D_MODEL)

    weights = {"lower_bounds": lower_bounds, "pre_norm_g": pre_norm_g, "hgrn_norm_g": hgrn_norm_g,
               "fox_f_bias": fox_f_bias, "pool_w": pool_w, "pool_scale": pool_scale, "post_norm_g": post_norm_g}
    moments_m = {"lower_bounds": m_lower_bounds, "pre_norm_g": m_pre_norm_g, "hgrn_norm_g": m_hgrn_norm_g,
                 "fox_f_bias": m_fox_f_bias, "pool_w": m_pool_w, "pool_scale": m_pool_scale, "post_norm_g": m_post_norm_g}
    moments_v = {"lower_bounds": v_lower_bounds, "pre_norm_g": v_pre_norm_g, "hgrn_norm_g": v_hgrn_norm_g,
                 "fox_f_bias": v_fox_f_bias, "pool_w": v_pool_w, "pool_scale": v_pool_scale, "post_norm_g": v_post_norm_g}
    gp, dp, mp, vp = _small_update(gsum, lower_bounds, _pack(weights), _pack(moments_m), _pack(moments_v))
    gs, _ = _unpack(gp, weights)
    ds, _ = _unpack(dp, weights)
    ms, _ = _unpack(mp, weights)
    vs, _ = _unpack(vp, weights)

    d_in, m_in, v_in = _adamw(w_in, grad_w_in, m_w_in, v_w_in, "adamw_w_in")
    d_out, m_out, v_out = _adamw(w_out, grad_w_out, m_w_out, v_w_out, "adamw_w_out")

    def ordered(s, big_in, big_out):
        return (s["lower_bounds"], s["pre_norm_g"], big_in, s["hgrn_norm_g"], s["fox_f_bias"], s["pool_w"],
                s["pool_scale"], big_out, s["post_norm_g"])

    return (loss, grad_x, *ordered(gs, grad_w_in, grad_w_out), *ordered(ds, d_in, d_out),
            *ordered(ms, m_in, m_out), *ordered(vs, v_in, v_out))
```

```python
import functools

import numpy as np
import jax
import jax.numpy as jnp
from jax import lax
from jax.experimental import pallas as pl
from jax.experimental.pallas import tpu as pltpu

F32 = jnp.float32
BF16 = jnp.bfloat16
HI = lax.Precision.HIGHEST
MESH = pl.DeviceIdType.MESH

NORM_EPS = 1e-6
MASK_VALUE = -1e30
TINY = 1e-30
ADAM_LR, ADAM_B1, ADAM_B2, ADAM_EPS, ADAM_WD, ADAM_STEP = 0.001, 0.9, 0.999, 1e-08, 0.01, 10

D_MODEL = 1024
DEPTH = 2
N_CHIPS = 4
CHUNK = 64
LANES = 128
HGRN_W, POOL_W, FOX_W, FOX_HEADS = 256, 256, 512, 8
POOL_WINDOWS = (2, 4, 8, 16)
POOL_HALO = 16
IN_WIDTH = 3592
SHARD_W = IN_WIDTH // N_CHIPS
A_W, B_W, C_W, F_W = 1024, 512, 2048, 128
E_INT = A_W + B_W + C_W + F_W
B_BLK = A_W // 512
C_BLK0 = (A_W + B_W) // 512
F_BLK = (A_W + B_W + C_W) // 128


def _segments():
    segs = []
    for hp in range(2):
        for part in range(4):
            segs.append((part * 256 + hp * 128, 128))
    segs.append((1024, 256))
    segs.append((1280, 256))
    for hp in range(4):
        for part in range(4):
            segs.append((1536 + part * 512 + hp * 128, 128))
    segs.append((3584, 8))
    return segs


_SEGS = _segments()


def _to_internal(w):
    parts = [w[..., s:s + n] for s, n in _SEGS]
    parts.append(jnp.zeros(w.shape[:-1] + (E_INT - IN_WIDTH,), w.dtype))
    return jnp.concatenate(parts, axis=-1)


def _to_original(w):
    offs, o = [], 0
    for s, n in _SEGS:
        offs.append((s, o, n))
        o += n
    parts = [w[..., o:o + n] for s, o, n in sorted(offs)]
    return jnp.concatenate(parts, axis=-1)


def _cparams(sem=None, vmem_mb=48):
    kw = dict(vmem_limit_bytes=vmem_mb * 1024 * 1024)
    if sem is not None:
        kw["dimension_semantics"] = sem
    return pltpu.CompilerParams(**kw)


def _sig(x):
    return 1.0 / (1.0 + jnp.exp(-x))


def _silu(x):
    return x * _sig(x)


def _dsilu(x):
    s = _sig(x)
    return s * (1.0 + x * (1.0 - s))


def _rstd(x):
    return lax.rsqrt(jnp.mean(x * x, axis=-1, keepdims=True) + NORM_EPS)


def _dot(a, b, dims, **kw):
    return lax.dot_general(a, b, (dims, ((), ())), preferred_element_type=F32, **kw)


NN = ((1,), (0,))
NT = ((1,), (1,))
TN = ((0,), (0,))


def _iota(shape, dim):
    return lax.broadcasted_iota(jnp.int32, shape, dim)


def _lbs_fwd(lower_bounds):
    def body(a_ref, o_ref):
        a = a_ref[...]
        a0, a1 = a[0:1], a[1:2]
        m = jnp.maximum(a0, a1)
        e0, e1 = jnp.exp(a0 - m), jnp.exp(a1 - m)
        p0, p1 = e0 / (e0 + e1), e1 / (e0 + e1)
        o_ref[...] = jnp.concatenate([p0 - p0, (p0 + p1) - p0], axis=0)

    return pl.pallas_call(body, out_shape=jax.ShapeDtypeStruct(lower_bounds.shape, F32), name="lbs_fwd")(lower_bounds)


def _inproj_fwd(x2, g_row, w_int, name):
    n, d = x2.shape
    e = w_int.shape[1]
    tm = min(256, n)

    def body(x_ref, g_ref, w_ref, o_ref):
        x = x_ref[...]
        h = (x * _rstd(x) * g_ref[...]).astype(BF16)
        o_ref[...] = jnp.dot(h, w_ref[...], preferred_element_type=F32)

    return pl.pallas_call(
        body, grid=(n // tm,),
        in_specs=[pl.BlockSpec((tm, d), lambda i: (i, 0)), pl.BlockSpec((1, d), lambda i: (0, 0)),
                  pl.BlockSpec((d, e), lambda i: (0, 0))],
        out_specs=pl.BlockSpec((tm, e), lambda i: (i, 0)),
        out_shape=jax.ShapeDtypeStruct((n, e), F32),
        compiler_params=_cparams(("parallel",)), name=name)(x2, g_row, w_int)


def _chunk_cumsum_matrix():
    i, j = _iota((LANES, LANES), 0), _iota((LANES, LANES), 1)
    return ((i <= j) & ((i // CHUNK) == (j // CHUNK))).astype(F32)


def _hgrn_gates(a, lb):
    qa, z = a[:, 0:128], a[:, 128:256]
    sg, sgn = _sig(z), _sig(-z)
    fg = lb + (1.0 - lb) * sg
    lf = jnp.log(jnp.maximum(fg, TINY))
    kk = (1.0 - lb) * sgn
    return qa * _sig(qa), kk, lf, sg, sgn, fg


def _hgrn_fwd(proj3, lbs_row, gn_col, name):
    bsz, t, _ = proj3.shape
    nt = t // LANES

    def body(a_ref, lb_ref, gn_ref, og_ref, or_ref):
        lb = lb_ref[...]
        gn = gn_ref[...]
        umat = _chunk_cumsum_matrix()
        lane64 = _iota((1, LANES), 1) % CHUNK

        def tile(i, carry):
            r0 = pl.multiple_of(i * LANES, LANES)
            a = a_ref[pl.ds(r0, LANES), :]
            qq, kk, lf, _, _, _ = _hgrn_gates(a, lb)
            va, ga = a[:, 256:384], a[:, 384:512]
            q_t, k_t, v_t = qq.T, kk.T, va.T
            b_t = jnp.dot(lf.T, umat, precision=HI, preferred_element_type=F32)
            new_s, o_heads = [], []
            for h in range(2):
                s_h = carry[h]
                rs = slice(CHUNK * h, CHUNK * (h + 1))
                qh, kh, vh, bh = q_t[rs], k_t[rs], v_t[rs], b_t[rs]
                inter = []
                for c in range(2):
                    cs = slice(CHUNK * c, CHUNK * (c + 1))
                    b_ = bh[:, cs]
                    qt = (qh[:, cs] * jnp.exp(b_)).astype(BF16)
                    inter.append(_dot(s_h.astype(BF16), qt, TN))
                    bl = b_[:, CHUNK - 1:CHUNK]
                    kt = (kh[:, cs] * jnp.exp(bl - b_)).astype(BF16)
                    s_h = jnp.exp(bl) * s_h + _dot(kt, vh[:, cs].astype(BF16), NT)
                new_s.append(s_h)

                acc = jnp.concatenate(inter, axis=1) + jnp.sum(qh * kh, axis=0, keepdims=True) * vh
                for dlt in range(1, CHUNK):
                    kr, br, vr = pltpu.roll(kh, dlt, 1), pltpu.roll(bh, dlt, 1), pltpu.roll(vh, dlt, 1)
                    e = jnp.exp(jnp.minimum(bh - br, 0.0))
                    att = jnp.sum(qh * kr * e, axis=0, keepdims=True)
                    acc = acc + jnp.where(lane64 >= dlt, att, 0.0) * vr
                o_heads.append(acc)
            normed = []
            for h in range(2):
                o_h = o_heads[h]
                ms = jnp.mean(o_h * o_h, axis=0, keepdims=True)
                normed.append(o_h * lax.rsqrt(ms + NORM_EPS) * gn[CHUNK * h:CHUNK * (h + 1)])
            or_ref[pl.ds(r0, LANES), :] = jnp.concatenate(o_heads, axis=0).T
            og_ref[pl.ds(r0, LANES), :] = jnp.concatenate(normed, axis=0).T * _silu(ga)
            return tuple(new_s)

        zero = jnp.zeros((CHUNK, CHUNK), F32)
        lax.fori_loop(0, nt, tile, (zero, zero))

    out = jax.ShapeDtypeStruct((bsz, t, HGRN_W), F32)
    return pl.pallas_call(
        body, grid=(bsz, 2),
        in_specs=[pl.BlockSpec((None, t, 512), lambda b, p: (b, 0, p)),
                  pl.BlockSpec((1, 128), lambda b, p: (0, p)),
                  pl.BlockSpec((128, 1), lambda b, p: (p, 0))],
        out_specs=[pl.BlockSpec((None, t, 128), lambda b, p: (b, 0, p)),
                   pl.BlockSpec((None, t, 128), lambda b, p: (b, 0, p))],
        out_shape=[out, out],
        compiler_params=_cparams(("parallel", "parallel")), name=name)(proj3, lbs_row, gn_col)


def _hgrn_bwd(proj3, o_raw, dmixed, lbs_row, gn_row, name):
    bsz, t, _ = proj3.shape
    nt = t // LANES
    nchunk = t // CHUNK

    def body(a_ref, or_ref, do_ref, lb_ref, gn_ref, da_ref, dgn_ref, dlb_ref, s_sc):
        lb = lb_ref[...]
        gn = gn_ref[...]
        umat = _chunk_cumsum_matrix()
        lane = _iota((1, LANES), 1)
        lane64 = lane % CHUNK
        half = lane < CHUNK

        def t_layout(a):
            qq, kk, lf, sg, sgn, fg = _hgrn_gates(a, lb)
            b_t = jnp.dot(lf.T, umat, precision=HI, preferred_element_type=F32)
            return qq.T, kk.T, a[:, 256:384].T, b_t, (sg, sgn, fg)

        def fwd_tile(i, carry):
            r0 = pl.multiple_of(i * LANES, LANES)
            q_t, k_t, v_t, b_t, _ = t_layout(a_ref[pl.ds(r0, LANES), :])
            new_s = []
            for h in range(2):
                s_h = carry[h]
                rs = slice(CHUNK * h, CHUNK * (h + 1))
                for c in range(2):
                    cs = slice(CHUNK * c, CHUNK * (c + 1))
                    s_sc[h, 2 * i + c] = s_h
                    b_ = b_t[rs, cs]
                    bl = b_[:, CHUNK - 1:CHUNK]
                    kt = (k_t[rs, cs] * jnp.exp(bl - b_)).astype(BF16)
                    s_h = jnp.exp(bl) * s_h + _dot(kt, v_t[rs, cs].astype(BF16), NT)
                new_s.append(s_h)
            return tuple(new_s)

        zero = jnp.zeros((CHUNK, CHUNK), F32)
        lax.fori_loop(0, nt, fwd_tile, (zero, zero))

        def half_mean(v):
            m0 = jnp.sum(jnp.where(half, v, 0.0), axis=1, keepdims=True) * (1.0 / CHUNK)
            m1 = jnp.sum(jnp.where(half, 0.0, v), axis=1, keepdims=True) * (1.0 / CHUNK)
            return jnp.where(half, m0, m1)

        def bwd_tile(k, carry):
            ds0, ds1, dgn_acc, dlb_acc = carry
            i = nt - 1 - k
            r0 = pl.multiple_of(i * LANES, LANES)
            a = a_ref[pl.ds(r0, LANES), :]
            qa, z, ga = a[:, 0:128], a[:, 128:256], a[:, 384:512]
            q_t, k_t, v_t, b_t, (sg, sgn, fg) = t_layout(a)
            oraw = or_ref[pl.ds(r0, LANES), :]
            dout = do_ref[pl.ds(r0, LANES), :]
            r = lax.rsqrt(half_mean(oraw * oraw) + NORM_EPS)
            xn = oraw * r
            dga = dout * (xn * gn) * _dsilu(ga)
            don = dout * _silu(ga)
            dgn_acc = dgn_acc + jnp.sum(don * xn, axis=0, keepdims=True)
            dxn = don * gn
            do_t = (r * (dxn - xn * half_mean(dxn * xn))).T
            new_ds, dq_h, dk_h, dv_h, db_h = [], [], [], [], []
            for h in range(2):
                ds_h = (ds0, ds1)[h]
                rs = slice(CHUNK * h, CHUNK * (h + 1))
                qh, kh, vh, bh, doh = q_t[rs], k_t[rs], v_t[rs], b_t[rs], do_t[rs]
                dq_c, dk_c, dv_c, dbl_c = [None, None], [None, None], [None, None], [None, None]
                for c in (1, 0):
                    cs = slice(CHUNK * c, CHUNK * (c + 1))
                    s_n = s_sc[h, 2 * i + c]
                    b_ = bh[:, cs]
                    eb = jnp.exp(b_)
                    bl = b_[:, CHUNK - 1:CHUNK]
                    ek = jnp.exp(bl - b_)
                    ebl = jnp.exp(bl)
                    qt, kt = qh[:, cs] * eb, kh[:, cs] * ek
                    do_c = doh[:, cs].astype(BF16)
                    dsb = ds_h.astype(BF16)
                    dv_c[c] = _dot(dsb, kt.astype(BF16), TN)
                    dkt = _dot(dsb, vh[:, cs].astype(BF16), NN)
                    dqt = _dot(s_n.astype(BF16), do_c, NN)
                    dbl_c[c] = jnp.sum(ds_h * s_n, axis=1, keepdims=True) * ebl + jnp.sum(dkt * kt, axis=1, keepdims=True)
                    dq_c[c], dk_c[c] = dqt * eb, dkt * ek
                    ds_h = ebl * ds_h + _dot(qt.astype(BF16), do_c, NT)
                new_ds.append(ds_h)

                att0 = jnp.sum(qh * kh, axis=0, keepdims=True)
                datt0 = jnp.sum(doh * vh, axis=0, keepdims=True)
                dqh = jnp.concatenate(dq_c, axis=1) + datt0 * kh
                dkh = jnp.concatenate(dk_c, axis=1) + datt0 * qh
                dvh = jnp.concatenate(dv_c, axis=1) + att0 * doh
                for dlt in range(1, CHUNK):
                    kr, br, vr = pltpu.roll(kh, dlt, 1), pltpu.roll(bh, dlt, 1), pltpu.roll(vh, dlt, 1)
                    e = jnp.where(lane64 >= dlt, jnp.exp(jnp.minimum(bh - br, 0.0)), 0.0)
                    qe = qh * e
                    att = jnp.sum(qe * kr, axis=0, keepdims=True)
                    datt = jnp.sum(doh * vr, axis=0, keepdims=True)
                    dqh = dqh + datt * (kr * e)
                    dkh = dkh + pltpu.roll(datt * qe, LANES - dlt, 1)
                    dvh = dvh + pltpu.roll(att * doh, LANES - dlt, 1)
                dbl = jnp.where(half, dbl_c[0], dbl_c[1])
                db_h.append(qh * dqh - kh * dkh + jnp.where(lane64 == CHUNK - 1, dbl, 0.0))
                dq_h.append(dqh)
                dk_h.append(dkh)
                dv_h.append(dvh)
            dqq = jnp.concatenate(dq_h, axis=0).T
            dkk = jnp.concatenate(dk_h, axis=0).T
            dvv = jnp.concatenate(dv_h, axis=0).T
            dlf = _dot(jnp.concatenate(db_h, axis=0), umat, NT, precision=HI).T
            dqa = dqq * _dsilu(qa)
            dfg = jnp.where(fg > TINY, dlf / fg, 0.0)
            dz = (dfg - dkk) * (1.0 - lb) * sg * sgn
            dlb_acc = dlb_acc + jnp.sum(dfg * (1.0 - sg) - dkk * sgn, axis=0, keepdims=True)
            da_ref[pl.ds(r0, LANES), :] = jnp.concatenate([dqa, dz, dvv, dga], axis=1)
            return new_ds[0], new_ds[1], dgn_acc, dlb_acc

        zrow = jnp.zeros((1, LANES), F32)
        _, _, dgn_acc, dlb_acc = lax.fori_loop(0, nt, bwd_tile, (zero, zero, zrow, zrow))
        dgn_ref[...] = jnp.broadcast_to(dgn_acc, (8, LANES))
        dlb_ref[...] = jnp.broadcast_to(dlb_acc, (8, LANES))

    rows = jax.ShapeDtypeStruct((bsz, 8, HGRN_W), F32)
    return pl.pallas_call(
        body, grid=(bsz, 2),
        in_specs=[pl.BlockSpec((None, t, 512), lambda b, p: (b, 0, p)),
                  pl.BlockSpec((None, t, 128), lambda b, p: (b, 0, p)),
                  pl.BlockSpec((None, t, 128), lambda b, p: (b, 0, p)),
                  pl.BlockSpec((1, 128), lambda b, p: (0, p)),
                  pl.BlockSpec((1, 128), lambda b, p: (0, p))],
        out_specs=[pl.BlockSpec((None, t, 512), lambda b, p: (b, 0, p)),
                   pl.BlockSpec((None, 8, 128), lambda b, p: (b, 0, p)),
                   pl.BlockSpec((None, 8, 128), lambda b, p: (b, 0, p))],
        out_shape=[jax.ShapeDtypeStruct((bsz, t, A_W), F32), rows, rows],
        scratch_shapes=[pltpu.VMEM((2, nchunk, CHUNK, CHUNK), F32)],
        compiler_params=_cparams(("parallel", "parallel")), name=name)(proj3, o_raw, dmixed, lbs_row, gn_row)


N_LEVELS = 6


def _hgrn_tables():
    t = np.arange(LANES)
    j = np.arange(LANES)[None, :]
    same_chunk = (t[:, None] // CHUNK) == (j // CHUNK)
    w = np.zeros((2 + N_LEVELS, LANES, LANES), np.float32)
    w[0] = same_chunk & (j <= t[:, None])
    w[1] = same_chunk & (j > t[:, None])
    maskf = np.zeros((N_LEVELS, LANES, LANES), np.float32)
    rightf = np.zeros((N_LEVELS, LANES, LANES), np.float32)
    for li in range(N_LEVELS):
        m = (CHUNK // 2) >> li
        start = t - (t % (2 * m))
        right = (t % (2 * m)) >= m
        first = np.where(right, start + m, t + 1)
        last = np.where(right, t, start + m - 1)
        w[2 + li] = (j >= first[:, None]) & (j <= last[:, None])
        maskf[li] = (t[:, None] // (2 * m)) == (j // (2 * m))
        rightf[li] = right[:, None]
    return jnp.asarray(w.reshape(-1, LANES), BF16), jnp.asarray(maskf), jnp.asarray(rightf)


def _split(x, n):
    parts = []
    for _ in range(n - 1):
        p = x.astype(BF16)
        parts.append(p)
        x = x - p.astype(F32)
    parts.append(x.astype(BF16))
    return parts


def _exact_dot(w, parts):
    acc = jnp.dot(w, parts[0], preferred_element_type=F32)
    for p in parts[1:]:
        acc = acc + jnp.dot(w, p, preferred_element_type=F32)
    return acc


def _head_sums(v, ones_blk, n=2):
    parts = _split(v, n)
    acc = jnp.dot(parts[0], ones_blk, preferred_element_type=F32)
    for p in parts[1:]:
        acc = acc + jnp.dot(p, ones_blk, preferred_element_type=F32)
    return acc


def _hgrn_consts():
    r, c = _iota((LANES, LANES), 0), _iota((LANES, LANES), 1)
    eye = r == c
    ones_blk = ((r // CHUNK) == (c // CHUNK)).astype(BF16)
    return eye, ones_blk, jnp.ones((CHUNK, LANES), BF16)


def _hgrn_levels(qq, kk, parts, w_ref, mk_ref, rt_ref, d_att=None):
    att = [jnp.zeros((LANES, LANES), F32)] * 2
    dq = dk = db = jnp.zeros((LANES, LANES), F32)
    for li in range(N_LEVELS):
        e = jnp.exp(_exact_dot(w_ref[(2 + li) * LANES:(3 + li) * LANES, :], parts))
        rt = rt_ref[li]
        mk = mk_ref[li]
        qef, kef = e * rt, e * (1.0 - rt)
        qe, ke = (qq * qef).astype(BF16), (kk * kef).astype(BF16)
        dqs, dks = [], []
        for h in range(2):
            hs = slice(CHUNK * h, CHUNK * (h + 1))
            att[h] = att[h] + _dot(qe[:, hs], ke[:, hs], NT) * mk
            if d_att is not None:
                dam = (d_att[h] * mk).astype(BF16)
                dqs.append(jnp.dot(dam, ke[:, hs], preferred_element_type=F32))
                dks.append(_dot(dam, qe[:, hs], TN))
        if d_att is not None:
            dqe, dke = jnp.concatenate(dqs, axis=1), jnp.concatenate(dks, axis=1)
            dq = dq + dqe * qef
            dk = dk + dke * kef
            db = db + (dqe * qe.astype(F32) - dke * ke.astype(F32))
    return att, dq, dk, db


def _hgrn_fwd(proj3, lbs_row, gn_row, name):
    bsz, t, _ = proj3.shape
    nt = t // LANES
    w_all, maskf, rightf = _hgrn_tables()

    def body(a_ref, lb_ref, gn_ref, w_ref, mk_ref, rt_ref, og_ref, or_ref):
        lb = lb_ref[...]
        gn = gn_ref[...]
        eye, ones_blk, ones_h = _hgrn_consts()

        def tile(i, carry):
            r0 = pl.multiple_of(i * LANES, LANES)
            a = a_ref[pl.ds(r0, LANES), :]
            qq, kk, lf, _, _, _ = _hgrn_gates(a, lb)
            va, ga = a[:, 256:384], a[:, 384:512]
            parts = _split(lf, 3)
            eb = jnp.exp(_exact_dot(w_ref[0:LANES, :], parts))
            ee = jnp.exp(_exact_dot(w_ref[LANES:2 * LANES, :], parts))
            vb = va.astype(BF16)
            att, _, _, _ = _hgrn_levels(qq, kk, parts, w_ref, mk_ref, rt_ref)
            qk = _split(qq * kk, 2)
            qeb, keb = (qq * eb).astype(BF16), (kk * ee).astype(BF16)
            new_s, o_heads = [], []
            for h in range(2):
                hs = slice(CHUNK * h, CHUNK * (h + 1))
                diag = _exact_dot_r(qk, hs, ones_h)
                a_h = att[h] + jnp.where(eye, diag, 0.0)
                o_h = jnp.dot(a_h.astype(BF16), vb[:, hs], preferred_element_type=F32)
                st = carry[h]
                chunks = []
                for c in range(2):
                    rc = slice(CHUNK * c, CHUNK * (c + 1))
                    chunks.append(o_h[rc] + _dot(qeb[rc, hs], st.astype(BF16), NT))
                    ebl = eb[CHUNK * (c + 1) - 1:CHUNK * (c + 1), hs]
                    st = st * ebl + _dot(vb[rc, hs], keb[rc, hs], TN)
                new_s.append(st)
                o_heads.append(jnp.concatenate(chunks, axis=0))
            o = jnp.concatenate(o_heads, axis=1)
            ms = _head_sums(o * o, ones_blk) * (1.0 / CHUNK)
            or_ref[pl.ds(r0, LANES), :] = o
            og_ref[pl.ds(r0, LANES), :] = o * lax.rsqrt(ms + NORM_EPS) * gn * _silu(ga)
            return tuple(new_s)

        zero = jnp.zeros((CHUNK, CHUNK), F32)
        lax.fori_loop(0, nt, tile, (zero, zero))

    out = jax.ShapeDtypeStruct((bsz, t, HGRN_W), F32)
    row = pl.BlockSpec((1, 128), lambda b, p: (0, p))
    return pl.pallas_call(
        body, grid=(bsz, 2),
        in_specs=[pl.BlockSpec((None, t, 512), lambda b, p: (b, 0, p)), row, row,
                  pl.BlockSpec(w_all.shape, lambda b, p: (0, 0)),
                  pl.BlockSpec(maskf.shape, lambda b, p: (0, 0, 0)),
                  pl.BlockSpec(rightf.shape, lambda b, p: (0, 0, 0))],
        out_specs=[pl.BlockSpec((None, t, 128), lambda b, p: (b, 0, p)),
                   pl.BlockSpec((None, t, 128), lambda b, p: (b, 0, p))],
        out_shape=[out, out],
        compiler_params=_cparams(("parallel", "parallel")), name=name)(proj3, lbs_row, gn_row, w_all, maskf, rightf)


def _exact_dot_r(parts, hs, ones_h):
    acc = jnp.dot(parts[0][:, hs], ones_h, preferred_element_type=F32)
    for p in parts[1:]:
        acc = acc + jnp.dot(p[:, hs], ones_h, preferred_element_type=F32)
    return acc


def _hgrn_bwd(proj3, o_raw, dmixed, lbs_row, gn_row, name):
    bsz, t, _ = proj3.shape
    nt = t // LANES
    nchunk = t // CHUNK
    w_all, maskf, rightf = _hgrn_tables()

    def body(a_ref, or_ref, do_ref, lb_ref, gn_ref, w_ref, mk_ref, rt_ref, da_ref, dgn_ref, dlb_ref, s_sc):
        lb = lb_ref[...]
        gn = gn_ref[...]
        eye, ones_blk, ones_h = _hgrn_consts()
        r_i, c_i = _iota((LANES, LANES), 0), _iota((LANES, LANES), 1)
        suffix = ((c_i >= r_i) & ((r_i // CHUNK) == (c_i // CHUNK))).astype(BF16)
        row64 = _iota((LANES, CHUNK), 0)
        ones_t = jnp.ones((LANES, CHUNK), BF16)

        def fwd_tile(i, carry):
            r0 = pl.multiple_of(i * LANES, LANES)
            a = a_ref[pl.ds(r0, LANES), :]
            _, kk, lf, _, _, _ = _hgrn_gates(a, lb)
            parts = _split(lf, 3)
            eb = jnp.exp(_exact_dot(w_ref[0:LANES, :], parts))
            ee = jnp.exp(_exact_dot(w_ref[LANES:2 * LANES, :], parts))
            vb, keb = a[:, 256:384].astype(BF16), (kk * ee).astype(BF16)
            new_s = []
            for h in range(2):
                hs = slice(CHUNK * h, CHUNK * (h + 1))
                st = carry[h]
                for c in range(2):
                    rc = slice(CHUNK * c, CHUNK * (c + 1))
                    s_sc[h, 2 * i + c] = st
                    st = st * eb[CHUNK * (c + 1) - 1:CHUNK * (c + 1), hs] + _dot(vb[rc, hs], keb[rc, hs], TN)
                new_s.append(st)
            return tuple(new_s)

        zero = jnp.zeros((CHUNK, CHUNK), F32)
        lax.fori_loop(0, nt, fwd_tile, (zero, zero))

        def bwd_tile(k, carry):
            dst0, dst1, dgn_acc, dlb_acc = carry
            i = nt - 1 - k
            r0 = pl.multiple_of(i * LANES, LANES)
            a = a_ref[pl.ds(r0, LANES), :]
            qa, ga = a[:, 0:128], a[:, 384:512]
            qq, kk, lf, sg, sgn, fg = _hgrn_gates(a, lb)
            parts = _split(lf, 3)
            eb = jnp.exp(_exact_dot(w_ref[0:LANES, :], parts))
            ee = jnp.exp(_exact_dot(w_ref[LANES:2 * LANES, :], parts))
            vb = a[:, 256:384].astype(BF16)
            oraw = or_ref[pl.ds(r0, LANES), :]
            dout = do_ref[pl.ds(r0, LANES), :]
            r = lax.rsqrt(_head_sums(oraw * oraw, ones_blk) * (1.0 / CHUNK) + NORM_EPS)
            xn = oraw * r
            dga = dout * (xn * gn) * _dsilu(ga)
            don = dout * _silu(ga)
            dgn_acc = dgn_acc + jnp.sum(don * xn, axis=0, keepdims=True)
            dxn = don * gn
            do = r * (dxn - xn * (_head_sums(dxn * xn, ones_blk) * (1.0 / CHUNK)))
            dob = do.astype(BF16)
            d_att = [_dot(dob[:, CHUNK * h:CHUNK * (h + 1)], vb[:, CHUNK * h:CHUNK * (h + 1)], NT) for h in range(2)]
            att, dq, dk, db_lv = _hgrn_levels(qq, kk, parts, w_ref, mk_ref, rt_ref, d_att)
            qk = _split(qq * kk, 2)
            qe_f, ke_f = qq * eb, kk * ee
            qeb, keb = qe_f.astype(BF16), ke_f.astype(BF16)
            new_ds, dq_h, dk_h, dv_h, dbl_h = [], [], [], [], []
            for h in range(2):
                hs = slice(CHUNK * h, CHUNK * (h + 1))
                a_h = att[h] + jnp.where(eye, _exact_dot_r(qk, hs, ones_h), 0.0)
                dv = _dot(a_h.astype(BF16), dob[:, hs], TN)
                ddiag = _exact_dot_r(_split(jnp.where(eye, d_att[h], 0.0), 2), slice(None), ones_t)
                dq_i = dq[:, hs] + ddiag * kk[:, hs]
                dk_i = dk[:, hs] + ddiag * qq[:, hs]
                dst = (dst0, dst1)[h]
                dq_c, dk_c, dv_c, dbl_c = [None, None], [None, None], [None, None], [None, None]
                for c in (1, 0):
                    rc = slice(CHUNK * c, CHUNK * (c + 1))
                    st_n = s_sc[h, 2 * i + c]
                    ebl = eb[CHUNK * (c + 1) - 1:CHUNK * (c + 1), hs]
                    dstb = dst.astype(BF16)
                    dv_c[c] = _dot(keb[rc, hs], dstb, NT)
                    dke = jnp.dot(vb[rc, hs], dstb, preferred_element_type=F32)
                    dqe = jnp.dot(dob[rc, hs], st_n.astype(BF16), preferred_element_type=F32)
                    dbl_c[c] = (jnp.sum(dst * st_n, axis=0, keepdims=True) * ebl
                                + jnp.sum(dke * ke_f[rc, hs], axis=0, keepdims=True))
                    dq_c[c], dk_c[c] = dqe * eb[rc, hs], dke * ee[rc, hs]
                    dst = dst * ebl + _dot(dob[rc, hs], qeb[rc, hs], TN)
                new_ds.append(dst)
                dq_x, dk_x = jnp.concatenate(dq_c, axis=0), jnp.concatenate(dk_c, axis=0)
                dq_h.append(dq_i + dq_x)
                dk_h.append(dk_i + dk_x)
                dv_h.append(dv + jnp.concatenate(dv_c, axis=0))
                dbl_h.append(qq[:, hs] * dq_x - kk[:, hs] * dk_x
                             + jnp.where(row64 == CHUNK - 1, dbl_c[0], 0.0) + jnp.where(row64 == LANES - 1, dbl_c[1], 0.0))
            dqq = jnp.concatenate(dq_h, axis=1)
            dkk = jnp.concatenate(dk_h, axis=1)
            dvv = jnp.concatenate(dv_h, axis=1)
            db = db_lv + jnp.concatenate(dbl_h, axis=1)
            dlf = _exact_dot(suffix, _split(db, 3))
            dqa = dqq * _dsilu(qa)
            dfg = jnp.where(fg > TINY, dlf / fg, 0.0)
            dz = (dfg - dkk) * (1.0 - lb) * sg * sgn
            dlb_acc = dlb_acc + jnp.sum(dfg * (1.0 - sg) - dkk * sgn, axis=0, keepdims=True)
            da_ref[pl.ds(r0, LANES), :] = jnp.concatenate([dqa, dz, dvv, dga], axis=1)
            return new_ds[0], new_ds[1], dgn_acc, dlb_acc

        zrow = jnp.zeros((1, LANES), F32)
        _, _, dgn_acc, dlb_acc = lax.fori_loop(0, nt, bwd_tile, (zero, zero, zrow, zrow))
        dgn_ref[...] = jnp.broadcast_to(dgn_acc, (8, LANES))
        dlb_ref[...] = jnp.broadcast_to(dlb_acc, (8, LANES))

    rows = jax.ShapeDtypeStruct((bsz, 8, HGRN_W), F32)
    row = pl.BlockSpec((1, 128), lambda b, p: (0, p))
    blk = pl.BlockSpec((None, t, 128), lambda b, p: (b, 0, p))
    return pl.pallas_call(
        body, grid=(bsz, 2),
        in_specs=[pl.BlockSpec((None, t, 512), lambda b, p: (b, 0, p)), blk, blk, row, row,
                  pl.BlockSpec(w_all.shape, lambda b, p: (0, 0)),
                  pl.BlockSpec(maskf.shape, lambda b, p: (0, 0, 0)),
                  pl.BlockSpec(rightf.shape, lambda b, p: (0, 0, 0))],
        out_specs=[pl.BlockSpec((None, t, 512), lambda b, p: (b, 0, p)),
                   pl.BlockSpec((None, 8, 128), lambda b, p: (b, 0, p)),
                   pl.BlockSpec((None, 8, 128), lambda b, p: (b, 0, p))],
        out_shape=[jax.ShapeDtypeStruct((bsz, t, A_W), F32), rows, rows],
        scratch_shapes=[pltpu.VMEM((2, nchunk, CHUNK, CHUNK), F32)],
        compiler_params=_cparams(("parallel", "parallel")), name=name)(
            proj3, o_raw, dmixed, lbs_row, gn_row, w_all, maskf, rightf)


def _pool_tt(t):
    return min(256, t)


def _window_select(s2, s4, s8, s16, lane):
    return jnp.where(lane < 64, s2, jnp.where(lane < 128, s4, jnp.where(lane < 192, s8, s16)))


def _pool_counts(t0, tt):
    lane = _iota((tt, POOL_W), 1)
    tpos = (_iota((tt, POOL_W), 0) + t0 + 1).astype(F32)
    win = jnp.where(lane < 64, 2.0, jnp.where(lane < 128, 4.0, jnp.where(lane < 192, 8.0, 16.0)))
    return 1.0 / jnp.minimum(tpos, win), lane


def _pooled_tile(upad_ref, i, tt):
    r0 = pl.multiple_of(i * tt, 8)
    cat = upad_ref[pl.ds(r0, tt + POOL_HALO), :]
    s2 = cat + pltpu.roll(cat, 1, 0)
    s4 = s2 + pltpu.roll(s2, 2, 0)
    s8 = s4 + pltpu.roll(s4, 4, 0)
    s16 = s8 + pltpu.roll(s8, 8, 0)
    inv, lane = _pool_counts(i * tt, tt)
    sel = _window_select(s2[POOL_HALO:], s4[POOL_HALO:], s8[POOL_HALO:], s16[POOL_HALO:], lane)
    return sel * inv - cat[POOL_HALO:], inv, lane


def _pool_fwd(proj3, wbd, scale_row, name):
    bsz, t, _ = proj3.shape
    tt = _pool_tt(t)

    def body(p_ref, w_ref, sc_ref, o_ref, upad):
        upad[0:POOL_HALO, :] = jnp.zeros((POOL_HALO, POOL_W), F32)
        upad[POOL_HALO:, :] = p_ref[:, 0:POOL_W]
        w = w_ref[...]
        sc = sc_ref[...]

        def tile(i, c):
            pooled, _, _ = _pooled_tile(upad, i, tt)
            r0 = pl.multiple_of(i * tt, 8)
            g = p_ref[pl.ds(r0, tt), POOL_W:2 * POOL_W]
            pre = jnp.dot(pooled.astype(BF16), w, preferred_element_type=F32)
            o_ref[pl.ds(r0, tt), :] = pre * sc * _silu(g)
            return c

        lax.fori_loop(0, t // tt, tile, 0)

    return pl.pallas_call(
        body, grid=(bsz,),
        in_specs=[pl.BlockSpec((None, t, 512), lambda b: (b, 0, B_BLK)),
                  pl.BlockSpec((POOL_W, POOL_W), lambda b: (0, 0)),
                  pl.BlockSpec((1, POOL_W), lambda b: (0, 0))],
        out_specs=pl.BlockSpec((None, t, POOL_W), lambda b: (b, 0, 0)),
        out_shape=jax.ShapeDtypeStruct((bsz, t, POOL_W), F32),
        scratch_shapes=[pltpu.VMEM((t + POOL_HALO, POOL_W), F32)],
        compiler_params=_cparams(("parallel",)), name=name)(proj3, wbd, scale_row)


def _pool_bwd(proj3, dmixed, wbd, scale_row, name):
    bsz, t, _ = proj3.shape
    tt = _pool_tt(t)

    def body(p_ref, do_ref, w_ref, sc_ref, db_ref, dsc_ref, dw_ref, upad, epad):
        upad[0:POOL_HALO, :] = jnp.zeros((POOL_HALO, POOL_W), F32)
        upad[POOL_HALO:, :] = p_ref[:, 0:POOL_W]
        epad[t:, :] = jnp.zeros((POOL_HALO, POOL_W), F32)
        w = w_ref[...]
        sc = sc_ref[...]

        def tile(i, carry):
            dsc_acc, dw_acc = carry
            pooled, inv, _ = _pooled_tile(upad, i, tt)
            r0 = pl.multiple_of(i * tt, 8)
            g = p_ref[pl.ds(r0, tt), POOL_W:2 * POOL_W]
            dout = do_ref[pl.ds(r0, tt), :]
            pb = pooled.astype(BF16)
            pre = jnp.dot(pb, w, preferred_element_type=F32)
            t1 = dout * _silu(g)
            dsc_acc = dsc_acc + jnp.sum(t1 * pre, axis=0, keepdims=True)
            dpre = (t1 * sc).astype(BF16)
            db_ref[pl.ds(r0, tt), POOL_W:2 * POOL_W] = dout * pre * sc * _dsilu(g)
            dw_acc = dw_acc + _dot(pb, dpre, TN)
            dpooled = _dot(dpre, w, NT)
            epad[pl.ds(r0, tt), :] = dpooled * inv
            return dsc_acc, dw_acc

        dsc_acc, dw_acc = lax.fori_loop(0, t // tt, tile, (jnp.zeros((1, POOL_W), F32), jnp.zeros((POOL_W, POOL_W), F32)))
        dsc_ref[...] = jnp.broadcast_to(dsc_acc, (8, POOL_W))
        dw_ref[...] = dw_acc

        def tile2(i, c):
            r0 = pl.multiple_of(i * tt, 8)
            n = tt + POOL_HALO
            cat = epad[pl.ds(r0, n), :]
            s2 = cat + pltpu.roll(cat, n - 1, 0)
            s4 = s2 + pltpu.roll(s2, n - 2, 0)
            s8 = s4 + pltpu.roll(s4, n - 4, 0)
            s16 = s8 + pltpu.roll(s8, n - 8, 0)
            inv, lane = _pool_counts(i * tt, tt)
            sel = _window_select(s2[:tt], s4[:tt], s8[:tt], s16[:tt], lane)
            db_ref[pl.ds(r0, tt), 0:POOL_W] = sel - cat[:tt] / inv
            return c

        lax.fori_loop(0, t // tt, tile2, 0)

    return pl.pallas_call(
        body, grid=(bsz,),
        in_specs=[pl.BlockSpec((None, t, 512), lambda b: (b, 0, B_BLK)),
                  pl.BlockSpec((None, t, POOL_W), lambda b: (b, 0, 1)),
                  pl.BlockSpec((POOL_W, POOL_W), lambda b: (0, 0)),
                  pl.BlockSpec((1, POOL_W), lambda b: (0, 0))],
        out_specs=[pl.BlockSpec((None, t, 512), lambda b: (b, 0, 0)),
                   pl.BlockSpec((None, 8, POOL_W), lambda b: (b, 0, 0)),
                   pl.BlockSpec((None, POOL_W, POOL_W), lambda b: (b, 0, 0))],
        out_shape=[jax.ShapeDtypeStruct((bsz, t, B_W), F32), jax.ShapeDtypeStruct((bsz, 8, POOL_W), F32),
                   jax.ShapeDtypeStruct((bsz, POOL_W, POOL_W), F32)],
        scratch_shapes=[pltpu.VMEM((t + POOL_HALO, POOL_W), F32), pltpu.VMEM((t + POOL_HALO, POOL_W), F32)],
        compiler_params=_cparams(("parallel",)), name=name)(proj3, dmixed, wbd, scale_row)


def _head_select_rows(hp):
    r, c = _iota((8, LANES), 0), _iota((8, LANES), 1)
    return ((r < 2) & (c == 2 * hp + r)).astype(F32)


def _foxgate_fwd(proj3, bias_row, name):
    bsz, t, _ = proj3.shape
    nt = t // LANES

    def body(f_ref, b_ref, cn_ref, ct_ref):
        bias = b_ref[...]
        i, j = _iota((LANES, LANES), 0), _iota((LANES, LANES), 1)
        lower = (j <= i).astype(F32)
        spread = (_iota((LANES, FOX_W), 0) == _iota((LANES, FOX_W), 1) // 64).astype(F32)

        def tile(k, carry):
            r0 = pl.multiple_of(k * LANES, LANES)
            xg = f_ref[pl.ds(r0, LANES), :] + bias
            lf = jnp.minimum(xg, 0.0) - jnp.log(1.0 + jnp.exp(-jnp.abs(xg)))
            c = jnp.dot(lower, lf, precision=HI, preferred_element_type=F32) + carry
            cn_ref[pl.ds(r0, LANES), :] = jnp.dot(c, spread, precision=HI, preferred_element_type=F32)
            for hp in range(4):
                ct_ref[hp, :, pl.ds(r0, LANES)] = _dot(_head_select_rows(hp), c, NT, precision=HI)
            return c[LANES - 1:LANES, :]

        lax.fori_loop(0, nt, tile, jnp.zeros((1, LANES), F32))

    return pl.pallas_call(
        body, grid=(bsz,),
        in_specs=[pl.BlockSpec((None, t, 128), lambda b: (b, 0, F_BLK)), pl.BlockSpec((1, 128), lambda b: (0, 0))],
        out_specs=[pl.BlockSpec((None, t, FOX_W), lambda b: (b, 0, 0)),
                   pl.BlockSpec((None, 4, 8, t), lambda b: (b, 0, 0, 0))],
        out_shape=[jax.ShapeDtypeStruct((bsz, t, FOX_W), F32), jax.ShapeDtypeStruct((bsz, 4, 8, t), F32)],
        compiler_params=_cparams(("parallel",)), name=name)(proj3, bias_row)


def _foxgate_bwd(proj3, dc_nat, bias_row, name):
    bsz, t, _ = proj3.shape
    nt = t // LANES

    def body(f_ref, dc_ref, b_ref, df_ref, dbias_ref, run_sc):
        bias = b_ref[...]
        i, j = _iota((LANES, LANES), 0), _iota((LANES, LANES), 1)
        upper = (j >= i).astype(F32)
        valid = _iota((1, LANES), 1) < FOX_HEADS
        run_sc[...] = jnp.zeros((8, LANES), F32)
        dbias_ref[...] = jnp.zeros((8, LANES), F32)

        def tile(k, c):
            r0 = pl.multiple_of((nt - 1 - k) * LANES, LANES)
            dc = dc_ref[pl.ds(r0, LANES), :] + jnp.where(i == LANES - 1, run_sc[0:1, :], 0.0)
            dlf = jnp.dot(upper, dc, precision=HI, preferred_element_type=F32)
            xg = f_ref[pl.ds(r0, LANES), :] + bias
            df = jnp.where(valid, dlf * _sig(-xg), 0.0)
            df_ref[pl.ds(r0, LANES), :] = df
            run_sc[...] = dlf[0:8, :]
            dbias_ref[...] += jnp.sum(df, axis=0, keepdims=True)
            return c

        lax.fori_loop(0, nt, tile, 0)

    blk = pl.BlockSpec((None, t, 128), lambda b: (b, 0, 0))
    return pl.pallas_call(
        body, grid=(bsz,),
        in_specs=[pl.BlockSpec((None, t, 128), lambda b: (b, 0, F_BLK)), blk, pl.BlockSpec((1, 128), lambda b: (0, 0))],
        out_specs=[blk, pl.BlockSpec((None, 8, 128), lambda b: (b, 0, 0))],
        out_shape=[jax.ShapeDtypeStruct((bsz, t, F_W), F32), jax.ShapeDtypeStruct((bsz, 8, 128), F32)],
        scratch_shapes=[pltpu.VMEM((8, LANES), F32)],
        compiler_params=_cparams(("parallel",)), name=name)(proj3, dc_nat, bias_row)


def _fox_tile(t):
    return min(256, t)


def _fox_fwd(proj3, c_nat, c_t, name):
    bsz, t, _ = proj3.shape
    tq = _fox_tile(t)
    tk = min(2 * tq, t)
    nq = t // tq

    def body(q_ref, kv_ref, cn_ref, ct_ref, og_ref, or_ref, lse_ref):
        i = pl.program_id(2)
        qblk = q_ref[...]
        first = _iota((1, 128), 1) < 64
        qv = qblk[:, 0:128] * 0.125
        qm = [jnp.where(first, qv, 0.0).astype(BF16), jnp.where(first, 0.0, qv).astype(BF16)]
        cqs = [cn_ref[:, 0:1], cn_ref[:, 64:65]]
        rows = _iota((tq, tk), 0) + i * tq

        def kv_step(j, carry, masked):
            c0 = pl.multiple_of(j * tk, tk)
            kb = kv_ref[pl.ds(c0, tk), 128:256].astype(BF16)
            vblk = kv_ref[pl.ds(c0, tk), 256:384]
            vx = [jnp.where(first, vblk, 1.0).astype(BF16), jnp.where(first, 1.0, vblk).astype(BF16)]
            new = []
            for h in range(2):
                m, acc = carry[2 * h], carry[2 * h + 1]
                s = _dot(qm[h], kb, NT) + (cqs[h] - ct_ref[h:h + 1, pl.ds(c0, tk)])
                if masked:
                    s = jnp.where(rows >= _iota((tq, tk), 1) + j * tk, s, MASK_VALUE)
                m_new = jnp.maximum(m, jnp.max(s, axis=1, keepdims=True))
                p = jnp.exp(s - m_new).astype(BF16)
                new += [m_new, jnp.exp(m - m_new) * acc + jnp.dot(p, vx[h], preferred_element_type=F32)]
            return tuple(new)

        init = (jnp.full((tq, 1), MASK_VALUE, F32), jnp.zeros((tq, 128), F32)) * 2
        n_full = (i * tq) // tk
        carry = lax.fori_loop(0, n_full, functools.partial(kv_step, masked=False), init)
        m0, acc0, m1, acc1 = kv_step(n_full, carry, True)
        l0, l1 = pltpu.roll(acc0, 64, 1), pltpu.roll(acc1, 64, 1)
        o = jnp.where(first, acc0 / l0, acc1 / l1)
        or_ref[...] = o
        og_ref[...] = o * _silu(qblk[:, 384:512])
        lse_ref[...] = jnp.where(first, m0 + jnp.log(l0), m1 + jnp.log(l1))

    out = jax.ShapeDtypeStruct((bsz, t, FOX_W), F32)
    blk = pl.BlockSpec((None, tq, 128), lambda b, p, i: (b, i, p))
    return pl.pallas_call(
        body, grid=(bsz, 4, nq),
        in_specs=[pl.BlockSpec((None, tq, 512), lambda b, p, i: (b, i, C_BLK0 + p)),
                  pl.BlockSpec((None, t, 512), lambda b, p, i: (b, 0, C_BLK0 + p)),
                  blk,
                  pl.BlockSpec((None, None, 8, t), lambda b, p, i: (b, p, 0, 0))],
        out_specs=[blk, blk, blk],
        out_shape=[out, out, out],
        compiler_params=_cparams(("parallel", "parallel", "arbitrary")), name=name)(proj3, proj3, c_nat, c_t)


def _fox_bwd(proj3, o_raw, dmixed, lse, c_nat, c_t, name):
    bsz, t, _ = proj3.shape
    tq = _fox_tile(t)
    nq = t // tq

    def body(a_ref, or_ref, do_ref, lse_ref, cn_ref, ct_ref, dc_out, dct_out, drow_out, dq_sc, do_sc, dl_sc):
        def prep(i, c):
            r0 = pl.multiple_of(i * tq, tq)
            g = a_ref[pl.ds(r0, tq), 384:512]
            dout = do_ref[pl.ds(r0, tq), :]
            o = or_ref[pl.ds(r0, tq), :]
            dc_out[pl.ds(r0, tq), 384:512] = dout * o * _dsilu(g)
            do = dout * _silu(g)
            do_sc[pl.ds(r0, tq), :] = do
            prod = do * o
            d0 = jnp.sum(prod[:, 0:64], axis=1, keepdims=True)
            d1 = jnp.sum(prod[:, 64:128], axis=1, keepdims=True)
            dl_sc[pl.ds(r0, tq), :] = jnp.concatenate([jnp.broadcast_to(d0, (tq, 64)), jnp.broadcast_to(d1, (tq, 64))], axis=1)
            dq_sc[pl.ds(r0, tq), :] = jnp.zeros((tq, 128), F32)
            drow_out[pl.ds(r0, tq), :] = jnp.zeros((tq, 128), F32)
            return c

        lax.fori_loop(0, nq, prep, 0)
        dct_out[...] = jnp.zeros((8, t), F32)

        causal = _iota((tq, tq), 0) >= _iota((tq, tq), 1)

        first = _iota((1, 128), 1) < 64

        def heads(v):
            return [jnp.where(first, v, 0.0).astype(BF16), jnp.where(first, 0.0, v).astype(BF16)]

        def kv_tile(j, c):
            c0 = pl.multiple_of(j * tq, tq)
            kb = a_ref[pl.ds(c0, tq), 128:256].astype(BF16)
            vb = a_ref[pl.ds(c0, tq), 256:384].astype(BF16)
            cks = [ct_ref[h:h + 1, pl.ds(c0, tq)] for h in range(2)]

            def q_step(i, carry, diagonal):
                dk, dv, dcol0, dcol1 = carry
                r0 = pl.multiple_of(i * tq, tq)
                qv = a_ref[pl.ds(r0, tq), 0:128] * 0.125
                do = do_sc[pl.ds(r0, tq), :]
                qb, dob = qv.astype(BF16), do.astype(BF16)
                qm, dom = heads(qv), heads(do)
                full, dcols, rsums = [], [], []
                for h in range(2):
                    lse_h = lse_ref[pl.ds(r0, tq), 64 * h:64 * h + 1]
                    dl_h = dl_sc[pl.ds(r0, tq), 64 * h:64 * h + 1]
                    cq = cn_ref[pl.ds(r0, tq), 64 * h:64 * h + 1]
                    p = jnp.exp(_dot(qm[h], kb, NT) + (cq - cks[h]) - lse_h)
                    if diagonal:
                        p = jnp.where(causal, p, 0.0)
                    ds = p * (_dot(dom[h], vb, NT) - dl_h)
                    dsb = ds.astype(BF16)
                    full.append((_dot(p.astype(BF16), dob, TN), _dot(dsb, qb, TN),
                                 jnp.dot(dsb, kb, preferred_element_type=F32)))
                    dcols.append(jnp.sum(ds, axis=0, keepdims=True))
                    rsums.append(jnp.broadcast_to(jnp.sum(ds, axis=1, keepdims=True), (tq, 128)))
                dq_sc[pl.ds(r0, tq), :] += jnp.where(first, full[0][2], full[1][2]) * 0.125
                drow_out[pl.ds(r0, tq), :] += jnp.where(first, rsums[0], rsums[1])
                return (dk + jnp.where(first, full[0][1], full[1][1]), dv + jnp.where(first, full[0][0], full[1][0]),
                        dcol0 - dcols[0], dcol1 - dcols[1])

            init = (jnp.zeros((tq, 128), F32), jnp.zeros((tq, 128), F32), jnp.zeros((1, tq), F32), jnp.zeros((1, tq), F32))
            carry = q_step(j, init, True)
            dk, dv, dcol0, dcol1 = lax.fori_loop(j + 1, nq, functools.partial(q_step, diagonal=False), carry)
            dct_out[0:1, pl.ds(c0, tq)] = dcol0
            dct_out[1:2, pl.ds(c0, tq)] = dcol1
            dc_out[pl.ds(c0, tq), 128:256] = dk
            dc_out[pl.ds(c0, tq), 256:384] = dv
            return c

        lax.fori_loop(0, nq, kv_tile, 0)
        dc_out[:, 0:128] = dq_sc[...]

    blk = pl.BlockSpec((None, t, 128), lambda b, p: (b, 0, p))
    return pl.pallas_call(
        body, grid=(bsz, 4),
        in_specs=[pl.BlockSpec((None, t, 512), lambda b, p: (b, 0, C_BLK0 + p)),
                  blk,
                  pl.BlockSpec((None, t, 128), lambda b, p: (b, 0, 4 + p)),
                  blk, blk,
                  pl.BlockSpec((None, None, 8, t), lambda b, p: (b, p, 0, 0))],
        out_specs=[pl.BlockSpec((None, t, 512), lambda b, p: (b, 0, p)),
                   pl.BlockSpec((None, None, 8, t), lambda b, p: (b, p, 0, 0)), blk],
        out_shape=[jax.ShapeDtypeStruct((bsz, t, C_W), F32), jax.ShapeDtypeStruct((bsz, 4, 8, t), F32),
                   jax.ShapeDtypeStruct((bsz, t, FOX_W), F32)],
        scratch_shapes=[pltpu.VMEM((t, 128), F32), pltpu.VMEM((t, 128), F32), pltpu.VMEM((t, 128), F32)],
        compiler_params=_cparams(("parallel", "parallel")), name=name)(proj3, o_raw, dmixed, lse, c_nat, c_t)


def _mix_tm(n):
    return min(512, n)


def _outproj_fwd(x2, oa, ob, oc, wo, g_row, name):
    n, d = x2.shape
    tm = _mix_tm(n)

    def body(x_ref, oa_ref, ob_ref, oc_ref, w_ref, g_ref, y_ref, xo_ref):
        y = (jnp.dot(oa_ref[...].astype(BF16), w_ref[0:256, :], preferred_element_type=F32)
             + jnp.dot(ob_ref[...].astype(BF16), w_ref[256:512, :], preferred_element_type=F32)
             + jnp.dot(oc_ref[...].astype(BF16), w_ref[512:1024, :], preferred_element_type=F32))
        y_ref[...] = y
        xo_ref[...] = x_ref[...] + y * _rstd(y) * g_ref[...]

    row = lambda w: pl.BlockSpec((tm, w), lambda i: (i, 0))
    out = jax.ShapeDtypeStruct((n, d), F32)
    return pl.pallas_call(
        body, grid=(n // tm,),
        in_specs=[row(d), row(256), row(256), row(512), pl.BlockSpec((d, d), lambda i: (0, 0)),
                  pl.BlockSpec((1, d), lambda i: (0, 0))],
        out_specs=[row(d), row(d)], out_shape=[out, out],
        compiler_params=_cparams(("parallel",)), name=name)(x2, oa, ob, oc, wo, g_row)


def _loss_head(x2, target2, name):
    n, d = x2.shape
    tm = _mix_tm(n)

    def body(x_ref, t_ref, dx_ref, l_ref):
        err = x_ref[...] - t_ref[...]
        dx_ref[...] = err * (1.0 / d)

        @pl.when(pl.program_id(0) == 0)
        def _():
            l_ref[...] = jnp.zeros((8, 128), F32)

        l_ref[...] += jnp.sum(err * err)

    row = pl.BlockSpec((tm, d), lambda i: (i, 0))
    return pl.pallas_call(
        body, grid=(n // tm,), in_specs=[row, row],
        out_specs=[row, pl.BlockSpec((8, 128), lambda i: (0, 0))],
        out_shape=[jax.ShapeDtypeStruct((n, d), F32), jax.ShapeDtypeStruct((8, 128), F32)],
        compiler_params=_cparams(("arbitrary",)), name=name)(x2, target2)


def _outproj_bwd(dxo, y, oa, ob, oc, wo, g_row, name):
    n, d = dxo.shape
    tm = _mix_tm(n)

    def body(dx_ref, y_ref, oa_ref, ob_ref, oc_ref, w_ref, g_ref, dm_ref, dw_ref, dg_ref):
        @pl.when(pl.program_id(0) == 0)
        def _():
            dw_ref[...] = jnp.zeros((d, d), F32)
            dg_ref[...] = jnp.zeros((8, d), F32)

        yv, dx = y_ref[...], dx_ref[...]
        r = _rstd(yv)
        yn = yv * r
        dg_ref[...] += jnp.sum(dx * yn, axis=0, keepdims=True)
        dyn = dx * g_ref[...]
        dy = (r * (dyn - yn * jnp.mean(dyn * yn, axis=-1, keepdims=True))).astype(BF16)
        dm_ref[...] = _dot(dy, w_ref[...], NT)
        dw_ref[0:256, :] += _dot(oa_ref[...].astype(BF16), dy, TN)
        dw_ref[256:512, :] += _dot(ob_ref[...].astype(BF16), dy, TN)
        dw_ref[512:1024, :] += _dot(oc_ref[...].astype(BF16), dy, TN)

    row = lambda w: pl.BlockSpec((tm, w), lambda i: (i, 0))
    fixed = lambda r, c: pl.BlockSpec((r, c), lambda i: (0, 0))
    return pl.pallas_call(
        body, grid=(n // tm,),
        in_specs=[row(d), row(d), row(256), row(256), row(512), fixed(d, d), fixed(1, d)],
        out_specs=[row(d), fixed(d, d), fixed(8, d)],
        out_shape=[jax.ShapeDtypeStruct((n, d), F32), jax.ShapeDtypeStruct((d, d), F32), jax.ShapeDtypeStruct((8, d), F32)],
        compiler_params=_cparams(("arbitrary",)), name=name)(dxo, y, oa, ob, oc, wo, g_row)


_PIECES = ((0, A_W), (A_W, B_W), (A_W + B_W, C_W), (A_W + B_W + C_W, F_W))


def _inproj_bwd_x(x2, dxo, g_row, w_int, pieces, name):
    n, d = x2.shape
    tm = min(256, n)

    def body(x_ref, dxo_ref, g_ref, w_ref, da_ref, db_ref, dc_ref, df_ref, dx_ref, dg_ref):
        @pl.when(pl.program_id(0) == 0)
        def _():
            dg_ref[...] = jnp.zeros((8, d), F32)

        dh = jnp.zeros((tm, d), F32)
        for ref, (o, w) in zip((da_ref, db_ref, dc_ref, df_ref), _PIECES):
            dh = dh + _dot(ref[...].astype(BF16), w_ref[:, o:o + w], NT)
        x = x_ref[...]
        r = _rstd(x)
        xn = x * r
        dg_ref[...] += jnp.sum(dh * xn, axis=0, keepdims=True)
        dxn = dh * g_ref[...]
        dx_ref[...] = dxo_ref[...] + r * (dxn - xn * jnp.mean(dxn * xn, axis=-1, keepdims=True))

    row = lambda w: pl.BlockSpec((tm, w), lambda i: (i, 0))
    fixed = lambda r, c: pl.BlockSpec((r, c), lambda i: (0, 0))
    return pl.pallas_call(
        body, grid=(n // tm,),
        in_specs=[row(d), row(d), fixed(1, d), fixed(d, E_INT)] + [row(w) for _, w in _PIECES],
        out_specs=[row(d), fixed(8, d)],
        out_shape=[jax.ShapeDtypeStruct((n, d), F32), jax.ShapeDtypeStruct((8, d), F32)],
        compiler_params=_cparams(("arbitrary",)), name=name)(x2, dxo, g_row, w_int, *pieces)


def _inproj_bwd_w(x2, g_row, piece, name):
    n, d = x2.shape
    w = piece.shape[1]
    tm = min(512, n)

    def body(x_ref, g_ref, dp_ref, dw_ref):
        @pl.when(pl.program_id(0) == 0)
        def _():
            dw_ref[...] = jnp.zeros((d, w), F32)

        x = x_ref[...]
        h = (x * _rstd(x) * g_ref[...]).astype(BF16)
        dw_ref[...] += _dot(h, dp_ref[...].astype(BF16), TN)

    return pl.pallas_call(
        body, grid=(n // tm,),
        in_specs=[pl.BlockSpec((tm, d), lambda i: (i, 0)), pl.BlockSpec((1, d), lambda i: (0, 0)),
                  pl.BlockSpec((tm, w), lambda i: (i, 0))],
        out_specs=pl.BlockSpec((d, w), lambda i: (0, 0)),
        out_shape=jax.ShapeDtypeStruct((d, w), F32),
        compiler_params=_cparams(("arbitrary",)), name=name)(x2, g_row, piece)


def _block_diag(pool_w_l):
    z = jnp.zeros((64, 64), pool_w_l.dtype)
    return jnp.concatenate(
        [jnp.concatenate([pool_w_l[g] if c == g else z for c in range(4)], axis=1) for g in range(4)], axis=0)


def _pad_lanes(v, width=128):
    return jnp.pad(v, ((0, 0),) * (v.ndim - 1) + ((0, width - v.shape[-1]),))


def _local_step(x, target, lower_bounds, pre_norm_g, w_in_int, hgrn_norm_g, fox_f_bias, pool_w, pool_scale,
                w_out_bf, post_norm_g):
    bsz, t, d = x.shape
    n = bsz * t
    lbs = _lbs_fwd(lower_bounds)
    saved = []
    xc = x.reshape(n, d)
    for l in range(DEPTH):
        proj = _inproj_fwd(xc, pre_norm_g[l:l + 1], w_in_int[l], f"inproj_fwd{l}").reshape(bsz, t, E_INT)
        wbd = _block_diag(pool_w[l]).astype(BF16)
        bias_row = _pad_lanes(fox_f_bias[l:l + 1])
        oa, oa_raw = _hgrn_fwd(proj, lbs[l:l + 1], hgrn_norm_g[l:l + 1], f"hgrn_fwd{l}")
        ob = _pool_fwd(proj, wbd, pool_scale[l:l + 1], f"pool_fwd{l}")
        c_nat, c_t = _foxgate_fwd(proj, bias_row, f"foxgate_fwd{l}")
        oc, oc_raw, lse = _fox_fwd(proj, c_nat, c_t, f"fox_fwd{l}")
        y, xn = _outproj_fwd(xc, oa.reshape(n, -1), ob.reshape(n, -1), oc.reshape(n, -1), w_out_bf[l],
                             post_norm_g[l:l + 1], f"outproj_fwd{l}")
        saved.append((xc, proj, wbd, bias_row, oa, oa_raw, ob, oc, oc_raw, lse, c_nat, c_t, y))
        xc = xn
    dx, sq = _loss_head(xc, target.reshape(n, d), "loss_head")
    g = {k: [None] * DEPTH for k in ("pre", "w_in", "hgn", "bias", "pool_w", "pool_scale", "w_out", "post", "lbs")}
    for l in reversed(range(DEPTH)):
        xin, proj, wbd, bias_row, oa, oa_raw, ob, oc, oc_raw, lse, c_nat, c_t, y = saved[l]
        dmix, g["w_out"][l], dpost = _outproj_bwd(dx, y, oa.reshape(n, -1), ob.reshape(n, -1), oc.reshape(n, -1),
                                                  w_out_bf[l], post_norm_g[l:l + 1], f"outproj_bwd{l}")
        g["post"][l] = dpost[0]
        dmix3 = dmix.reshape(bsz, t, d)
        d_c, dct, drow = _fox_bwd(proj, oc_raw, dmix3, lse, c_nat, c_t, f"fox_bwd{l}")
        dc_nat = _pad_lanes(dct[:, :, 0:2, :].reshape(bsz, FOX_HEADS, t).transpose(0, 2, 1)
                            + drow.reshape(bsz, t, FOX_HEADS, 64)[..., 0])
        d_f, dbias = _foxgate_bwd(proj, dc_nat, bias_row, f"foxgate_bwd{l}")
        g["bias"][l] = jnp.sum(dbias[:, 0, :FOX_HEADS], axis=0)
        d_b, dscale, dwbd = _pool_bwd(proj, dmix3, wbd, pool_scale[l:l + 1], f"pool_bwd{l}")
        g["pool_scale"][l] = jnp.sum(dscale[:, 0], axis=0)
        dwbd = jnp.sum(dwbd, axis=0)
        g["pool_w"][l] = jnp.stack([dwbd[64 * k:64 * (k + 1), 64 * k:64 * (k + 1)] for k in range(4)])
        d_a, dgn, dlb = _hgrn_bwd(proj, oa_raw, dmix3, lbs[l:l + 1], hgrn_norm_g[l:l + 1], f"hgrn_bwd{l}")
        g["hgn"][l] = jnp.sum(dgn[:, 0], axis=0)
        g["lbs"][l] = jnp.sum(dlb[:, 0], axis=0)
        pieces = [p.reshape(n, -1) for p in (d_a, d_b, d_c, d_f)]
        g["w_in"][l] = jnp.concatenate(
            [_inproj_bwd_w(xin, pre_norm_g[l:l + 1], p, f"inproj_bwd_w{l}_{k}") for k, p in enumerate(pieces)], axis=1)
        dx, dpre = _inproj_bwd_x(xin, dx, pre_norm_g[l:l + 1], w_in_int[l], pieces, f"inproj_bwd_x{l}")
        g["pre"][l] = dpre[0]
    grads = {k: jnp.stack(v) for k, v in g.items()}
    return sq, dx.reshape(bsz, t, d), grads


def _place():
    return lax.axis_index("x"), lax.axis_index("y"), lax.axis_index("c")


def _other_chips(x, y):
    return [(1 - x, y), (x, 1 - y), (1 - x, 1 - y)]


_ANY = pl.BlockSpec(memory_space=pl.ANY)


def _gather_weights(w_in_sh, w_out_sh):
    def body(win_ref, wout_ref, ain_ref, aout_ref, send_sems, recv_sems, local_sems):
        x, y, c = _place()
        me = 2 * x + y
        mine = [pltpu.make_async_copy(win_ref, ain_ref.at[me], local_sems.at[0]),
                pltpu.make_async_copy(wout_ref, aout_ref.at[me], local_sems.at[1])]
        for cp in mine:
            cp.start()
        sends = []
        for k, (px, py) in enumerate(_other_chips(x, y)):
            for j, (src, dst) in enumerate(((win_ref, ain_ref), (wout_ref, aout_ref))):
                sends.append(pltpu.make_async_remote_copy(
                    src_ref=src, dst_ref=dst.at[me], send_sem=send_sems.at[2 * k + j], recv_sem=recv_sems.at[2 * k + j],
                    device_id=(px, py, c), device_id_type=MESH))
        for cp in sends:
            cp.start()
        for k, (px, py) in enumerate(_other_chips(x, y)):
            for j, (src, dst) in enumerate(((win_ref, ain_ref), (wout_ref, aout_ref))):
                pltpu.make_async_remote_copy(
                    src_ref=src, dst_ref=dst.at[2 * px + py], send_sem=send_sems.at[2 * k + j],
                    recv_sem=recv_sems.at[2 * k + j], device_id=(px, py, c), device_id_type=MESH).wait_recv()
        for cp in sends:
            cp.wait_send()
        for cp in mine:
            cp.wait()

    return pl.pallas_call(
        body, in_specs=[_ANY, _ANY], out_specs=[_ANY, _ANY],
        out_shape=[jax.ShapeDtypeStruct((N_CHIPS,) + w_in_sh.shape, w_in_sh.dtype),
                   jax.ShapeDtypeStruct((N_CHIPS,) + w_out_sh.shape, w_out_sh.dtype)],
        scratch_shapes=[pltpu.SemaphoreType.DMA((6,)), pltpu.SemaphoreType.DMA((6,)), pltpu.SemaphoreType.DMA((2,))],
        name="gather_weights")(w_in_sh, w_out_sh)


def _swap_with_sibling(parts, name):
    k = len(parts)

    def body(*refs):
        src, dst = refs[:k], refs[k:2 * k]
        send_sems, recv_sems = refs[2 * k:]
        x, y, c = _place()
        cps = [pltpu.make_async_remote_copy(src_ref=src[j], dst_ref=dst[j], send_sem=send_sems.at[j], recv_sem=recv_sems.at[j],
                                            device_id=(x, y, 1 - c), device_id_type=MESH) for j in range(k)]
        for cp in cps:
            cp.start()
        for cp in cps:
            cp.wait()

    return pl.pallas_call(
        body, in_specs=[_ANY] * k, out_specs=[_ANY] * k,
        out_shape=[jax.ShapeDtypeStruct(p.shape, p.dtype) for p in parts],
        scratch_shapes=[pltpu.SemaphoreType.DMA((k,)), pltpu.SemaphoreType.DMA((k,))], name=name)(*parts)


def _scatter_to_chips(parts, name):
    k = len(parts)

    def body(*refs):
        src, dst = refs[:k], refs[k:2 * k]
        send_sems, recv_sems = refs[2 * k:]
        x, y, c = _place()
        me = 2 * x + y
        cps = []
        for rel, (px, py) in enumerate(_other_chips(x, y)):
            for j in range(k):
                cps.append(pltpu.make_async_remote_copy(
                    src_ref=src[j].at[2 * px + py], dst_ref=dst[j].at[rel], send_sem=send_sems.at[rel * k + j],
                    recv_sem=recv_sems.at[rel * k + j], device_id=(px, py, c), device_id_type=MESH))
        for cp in cps:
            cp.start()
        for cp in cps:
            cp.wait()
        del me

    return pl.pallas_call(
        body, in_specs=[_ANY] * k, out_specs=[_ANY] * k,
        out_shape=[jax.ShapeDtypeStruct((3,) + p.shape[1:], p.dtype) for p in parts],
        scratch_shapes=[pltpu.SemaphoreType.DMA((3 * k,)), pltpu.SemaphoreType.DMA((3 * k,))], name=name)(*parts)


def _add_n(parts, name):
    r, c = parts[0].shape
    tr = 256 if r % 256 == 0 else r

    def body(*refs):
        acc = refs[0][...]
        for ref in refs[1:-1]:
            acc = acc + ref[...]
        refs[-1][...] = acc

    blk = pl.BlockSpec((tr, c), lambda i: (i, 0))
    return pl.pallas_call(
        body, grid=(r // tr,), in_specs=[blk] * len(parts), out_specs=blk,
        out_shape=jax.ShapeDtypeStruct((r, c), F32), compiler_params=_cparams(("parallel",)), name=name)(*parts)


def _all_reduce_small(packet):
    r, w = packet.shape

    def body(p_ref, o_ref, buf, send_sems, recv_sems):
        x, y, c = _place()
        me = 4 * x + 2 * y + c
        buf[me] = p_ref[...]
        peers = []
        for k in range(1, 8):
            fx, fy, fc = (k >> 2) & 1, (k >> 1) & 1, k & 1
            peers.append((x ^ fx, y ^ fy, c ^ fc))
        cps = [pltpu.make_async_remote_copy(src_ref=p_ref, dst_ref=buf.at[me], send_sem=send_sems.at[k], recv_sem=recv_sems.at[k],
                                            device_id=peer, device_id_type=MESH) for k, peer in enumerate(peers)]
        for cp in cps:
            cp.start()
        for k, (px, py, pc) in enumerate(peers):
            pltpu.make_async_remote_copy(src_ref=p_ref, dst_ref=buf.at[4 * px + 2 * py + pc], send_sem=send_sems.at[k],
                                         recv_sem=recv_sems.at[k], device_id=(px, py, pc), device_id_type=MESH).wait_recv()
        for cp in cps:
            cp.wait_send()
        acc = buf[0]
        for k in range(1, 8):
            acc = acc + buf[k]
        o_ref[...] = acc

    vm = pl.BlockSpec(memory_space=pltpu.VMEM)
    return pl.pallas_call(
        body, in_specs=[vm], out_specs=vm, out_shape=jax.ShapeDtypeStruct((r, w), F32),
        scratch_shapes=[pltpu.VMEM((8, r, w), F32), pltpu.SemaphoreType.DMA((7,)), pltpu.SemaphoreType.DMA((7,))],
        name="all_reduce_small")(packet)


def _adamw_math(w, g, m, v):
    m = ADAM_B1 * m + (1.0 - ADAM_B1) * g
    v = ADAM_B2 * v + (1.0 - ADAM_B2) * (g * g)
    m_hat = m / (1.0 - ADAM_B1 ** ADAM_STEP)
    v_hat = v / (1.0 - ADAM_B2 ** ADAM_STEP)
    return -ADAM_LR * (m_hat / (jnp.sqrt(v_hat) + ADAM_EPS) + ADAM_WD * w), m, v


def _adamw(w, g, m, v, name):
    nl, r, c = w.shape
    tr = 256 if r % 256 == 0 else r

    def body(w_ref, g_ref, m_ref, v_ref, d_ref, mo_ref, vo_ref):
        d_ref[...], mo_ref[...], vo_ref[...] = _adamw_math(w_ref[...], g_ref[...], m_ref[...], v_ref[...])

    blk = pl.BlockSpec((None, tr, c), lambda l, i: (l, i, 0))
    out = jax.ShapeDtypeStruct(w.shape, F32)
    return pl.pallas_call(
        body, grid=(nl, r // tr), in_specs=[blk] * 4, out_specs=[blk] * 3, out_shape=[out] * 3,
        compiler_params=_cparams(("parallel", "parallel")), name=name)(w, g, m, v)


def _small_update(gsum, lower_bounds, wpack, mpack, vpack):
    r, w = gsum.shape
    lb_rows = DEPTH * HGRN_W // 128

    def body(g_ref, a_ref, w_ref, m_ref, v_ref, go_ref, d_ref, mo_ref, vo_ref):
        a = a_ref[...]
        a0, a1 = a[0:1], a[1:2]
        mx = jnp.maximum(a0, a1)
        e0, e1 = jnp.exp(a0 - mx), jnp.exp(a1 - mx)
        p0, p1 = e0 / (e0 + e1), e1 / (e0 + e1)
        g = g_ref[...]
        half = lb_rows // 2
        dl0 = jnp.concatenate([g[k:k + 1] for k in range(half)], axis=1)
        dl1 = jnp.concatenate([g[half + k:half + k + 1] for k in range(half)], axis=1)
        dp0 = (dl0 + dl1) - (dl0 + dl1)
        dp1 = dl1
        inner = p0 * dp0 + p1 * dp1
        da0, da1 = p0 * (dp0 - inner), p1 * (dp1 - inner)
        rows = [da0[:, 128 * k:128 * (k + 1)] for k in range(half)] + [da1[:, 128 * k:128 * (k + 1)] for k in range(half)]
        gfull = jnp.concatenate(rows + [g[lb_rows:]], axis=0)
        go_ref[...] = gfull
        d_ref[...], mo_ref[...], vo_ref[...] = _adamw_math(w_ref[...], gfull, m_ref[...], v_ref[...])

    vm = pl.BlockSpec(memory_space=pltpu.VMEM)
    out = jax.ShapeDtypeStruct((r, w), F32)
    return pl.pallas_call(body, in_specs=[vm] * 5, out_specs=[vm] * 4, out_shape=[out] * 4, name="small_update")(
        gsum, lower_bounds, wpack, mpack, vpack)


_SMALL = ("lower_bounds", "pre_norm_g", "hgrn_norm_g", "fox_f_bias", "pool_w", "pool_scale", "post_norm_g")


def _pack(parts):
    rows = []
    for k in _SMALL:
        f = parts[k].reshape(-1)
        pad = (-f.shape[0]) % (8 * 128)
        rows.append(jnp.pad(f, (0, pad)).reshape(-1, 128))
    rows.append(jnp.zeros((8, 128), F32))
    return jnp.concatenate(rows, axis=0)


def _unpack(pack, like):
    out, r = {}, 0
    for k in _SMALL:
        size = int(np.prod(like[k].shape))
        nr = -(-size // (8 * 128)) * 8
        out[k] = pack[r:r + nr].reshape(-1)[:size].reshape(like[k].shape)
        r += nr
    return out, r


def kernel(x, lower_bounds, pre_norm_g, w_in, hgrn_norm_g, fox_f_bias, pool_w, pool_scale, w_out, post_norm_g, loss_target, m_lower_bounds, m_pre_norm_g, m_w_in, m_hgrn_norm_g, m_fox_f_bias, m_pool_w, m_pool_scale, m_w_out, m_post_norm_g, v_lower_bounds, v_pre_norm_g, v_w_in, v_hgrn_norm_g, v_fox_f_bias, v_pool_w, v_pool_scale, v_w_out, v_post_norm_g):
    cx, cy, cc = _place()
    chip = 2 * cx + cy

    ain, aout = _gather_weights(w_in.astype(BF16), w_out.astype(BF16))
    w_in_full = jnp.concatenate([ain[q] for q in range(N_CHIPS)], axis=-1)
    w_in_int = _to_internal(w_in_full)
    w_out_full = jnp.concatenate([aout[q] for q in range(N_CHIPS)], axis=1)

    sq, grad_x, g = _local_step(x, loss_target, lower_bounds, pre_norm_g, w_in_int, hgrn_norm_g, fox_f_bias, pool_w,
                                pool_scale, w_out_full, post_norm_g)

    gin = _to_original(g["w_in"])
    gin_blocks = jnp.stack([gin[:, :, SHARD_W * q:SHARD_W * (q + 1)] for q in range(N_CHIPS)])
    gout_blocks = g["w_out"].reshape(DEPTH, N_CHIPS, 256, D_MODEL).transpose(1, 0, 2, 3)
    take = lambda a, l: lax.dynamic_index_in_dim(a, l, axis=1, keepdims=False)
    mine_in, mine_out = take(gin_blocks, cc), take(gout_blocks, cc)
    sib_in, sib_out = _swap_with_sibling([take(gin_blocks, 1 - cc), take(gout_blocks, 1 - cc)], "grad_swap1")
    rin, rout = 4 * 1024, 4 * 256
    sum_in = _add_n([mine_in.reshape(rin, SHARD_W), sib_in.reshape(rin, SHARD_W)], "grad_add1_in").reshape(4, 1024, SHARD_W)
    sum_out = _add_n([mine_out.reshape(rout, D_MODEL), sib_out.reshape(rout, D_MODEL)], "grad_add1_out").reshape(4, 256, D_MODEL)
    got_in, got_out = _scatter_to_chips([sum_in, sum_out], "grad_scatter")
    own = lambda a: lax.dynamic_index_in_dim(a, chip, axis=0, keepdims=False)
    half_in = _add_n([own(sum_in)] + [got_in[k] for k in range(3)], "grad_add2_in")
    half_out = _add_n([own(sum_out)] + [got_out[k] for k in range(3)], "grad_add2_out")
    oth_in, oth_out = _swap_with_sibling([half_in, half_out], "grad_swap2")
    first = cc == 0
    grad_w_in = jnp.stack([jnp.where(first, half_in, oth_in), jnp.where(first, oth_in, half_in)])
    grad_w_out = jnp.stack([jnp.where(first, half_out, oth_out), jnp.where(first, oth_out, half_out)])

    small = {"lower_bounds": g["lbs"], "pre_norm_g": g["pre"], "hgrn_norm_g": g["hgn"], "fox_f_bias": g["bias"],
             "pool_w": g["pool_w"], "pool_scale": g["pool_scale"], "post_norm_g": g["post"]}
    packet = _pack(small)
    nrows = packet.shape[0]
    packet = packet.at[nrows - 1].set(sq[0])
    gsum = _all_reduce_small(packet)
    loss = gsum[nrows - 1, 0] * (0.5 / D_MODEL)

    weights = {"lower_bounds": lower_bounds, "pre_norm_g": pre_norm_g, "hgrn_norm_g": hgrn_norm_g,
               "fox_f_bias": fox_f_bias, "pool_w": pool_w, "pool_scale": pool_scale, "post_norm_g": post_norm_g}
    moments_m = {"lower_bounds": m_lower_bounds, "pre_norm_g": m_pre_norm_g, "hgrn_norm_g": m_hgrn_norm_g,
                 "fox_f_bias": m_fox_f_bias, "pool_w": m_pool_w, "pool_scale": m_pool_scale, "post_norm_g": m_post_norm_g}
    moments_v = {"lower_bounds": v_lower_bounds, "pre_norm_g": v_pre_norm_g, "hgrn_norm_g": v_hgrn_norm_g,
                 "fox_f_bias": v_fox_f_bias, "pool_w": v_pool_w, "pool_scale": v_pool_scale, "post_norm_g": v_post_norm_g}
    gp, dp, mp, vp = _small_update(gsum, lower_bounds, _pack(weights), _pack(moments_m), _pack(moments_v))
    gs, _ = _unpack(gp, weights)
    ds, _ = _unpack(dp, weights)
    ms, _ = _unpack(mp, weights)
    vs, _ = _unpack(vp, weights)

    d_in, m_in, v_in = _adamw(w_in, grad_w_in, m_w_in, v_w_in, "adamw_w_in")
    d_out, m_out, v_out = _adamw(w_out, grad_w_out, m_w_out, v_w_out, "adamw_w_out")

    def ordered(s, big_in, big_out):
        return (s["lower_bounds"], s["pre_norm_g"], big_in, s["hgrn_norm_g"], s["fox_f_bias"], s["pool_w"],
                s["pool_scale"], big_out, s["post_norm_g"])

    return (loss, grad_x, *ordered(gs, grad_w_in, grad_w_out), *ordered(ds, d_in, d_out),
            *ordered(ms, m_in, m_out), *ordered(vs, v_in, v_out))
```

```python
import functools

import numpy as np
import jax
import jax.numpy as jnp
from jax import lax
from jax.experimental import pallas as pl
from jax.experimental.pallas import tpu as pltpu

F32 = jnp.float32
BF16 = jnp.bfloat16
HI = lax.Precision.HIGHEST
MESH = pl.DeviceIdType.MESH

NORM_EPS = 1e-6
MASK_VALUE = -1e30
TINY = 1e-30
ADAM_LR, ADAM_B1, ADAM_B2, ADAM_EPS, ADAM_WD, ADAM_STEP = 0.001, 0.9, 0.999, 1e-08, 0.01, 10

D_MODEL = 1024
DEPTH = 2
N_CHIPS = 4
CHUNK = 64
LANES = 128
HGRN_W, POOL_W, FOX_W, FOX_HEADS = 256, 256, 512, 8
POOL_WINDOWS = (2, 4, 8, 16)
POOL_HALO = 16
IN_WIDTH = 3592
SHARD_W = IN_WIDTH // N_CHIPS
A_W, B_W, C_W, F_W = 1024, 512, 2048, 128
E_INT = A_W + B_W + C_W + F_W
B_BLK = A_W // 512
C_BLK0 = (A_W + B_W) // 512
F_BLK = (A_W + B_W + C_W) // 128


def _segments():
    segs = []
    for hp in range(2):
        for part in range(4):
            segs.append((part * 256 + hp * 128, 128))
    segs.append((1024, 256))
    segs.append((1280, 256))
    for hp in range(4):
        for part in range(4):
            segs.append((1536 + part * 512 + hp * 128, 128))
    segs.append((3584, 8))
    return segs


_SEGS = _segments()


def _to_internal(w):
    parts = [w[..., s:s + n] for s, n in _SEGS]
    parts.append(jnp.zeros(w.shape[:-1] + (E_INT - IN_WIDTH,), w.dtype))
    return jnp.concatenate(parts, axis=-1)


def _to_original(w):
    offs, o = [], 0
    for s, n in _SEGS:
        offs.append((s, o, n))
        o += n
    parts = [w[..., o:o + n] for s, o, n in sorted(offs)]
    return jnp.concatenate(parts, axis=-1)


def _cparams(sem=None, vmem_mb=48):
    kw = dict(vmem_limit_bytes=vmem_mb * 1024 * 1024)
    if sem is not None:
        kw["dimension_semantics"] = sem
    return pltpu.CompilerParams(**kw)


def _sig(x):
    return 1.0 / (1.0 + jnp.exp(-x))


def _silu(x):
    return x * _sig(x)


def _dsilu(x):
    s = _sig(x)
    return s * (1.0 + x * (1.0 - s))


def _rstd(x):
    return lax.rsqrt(jnp.mean(x * x, axis=-1, keepdims=True) + NORM_EPS)


def _dot(a, b, dims, **kw):
    return lax.dot_general(a, b, (dims, ((), ())), preferred_element_type=F32, **kw)


NN = ((1,), (0,))
NT = ((1,), (1,))
TN = ((0,), (0,))


def _iota(shape, dim):
    return lax.broadcasted_iota(jnp.int32, shape, dim)


def _lbs_fwd(lower_bounds):
    def body(a_ref, o_ref):
        a = a_ref[...]
        a0, a1 = a[0:1], a[1:2]
        m = jnp.maximum(a0, a1)
        e0, e1 = jnp.exp(a0 - m), jnp.exp(a1 - m)
        p0, p1 = e0 / (e0 + e1), e1 / (e0 + e1)
        o_ref[...] = jnp.concatenate([p0 - p0, (p0 + p1) - p0], axis=0)

    return pl.pallas_call(body, out_shape=jax.ShapeDtypeStruct(lower_bounds.shape, F32), name="lbs_fwd")(lower_bounds)


def _inproj_fwd(x2, g_row, w_int, name):
    n, d = x2.shape
    e = w_int.shape[1]
    tm = min(256, n)

    def body(x_ref, g_ref, w_ref, o_ref):
        x = x_ref[...]
        h = (x * _rstd(x) * g_ref[...]).astype(BF16)
        o_ref[...] = jnp.dot(h, w_ref[...], preferred_element_type=F32)

    return pl.pallas_call(
        body, grid=(n // tm,),
        in_specs=[pl.BlockSpec((tm, d), lambda i: (i, 0)), pl.BlockSpec((1, d), lambda i: (0, 0)),
                  pl.BlockSpec((d, e), lambda i: (0, 0))],
        out_specs=pl.BlockSpec((tm, e), lambda i: (i, 0)),
        out_shape=jax.ShapeDtypeStruct((n, e), F32),
        compiler_params=_cparams(("parallel",)), name=name)(x2, g_row, w_int)


def _chunk_cumsum_matrix():
    i, j = _iota((LANES, LANES), 0), _iota((LANES, LANES), 1)
    return ((i <= j) & ((i // CHUNK) == (j // CHUNK))).astype(F32)


def _hgrn_gates(a, lb):
    qa, z = a[:, 0:128], a[:, 128:256]
    sg, sgn = _sig(z), _sig(-z)
    fg = lb + (1.0 - lb) * sg
    lf = jnp.log(jnp.maximum(fg, TINY))
    kk = (1.0 - lb) * sgn
    return qa * _sig(qa), kk, lf, sg, sgn, fg


def _hgrn_fwd(proj3, lbs_row, gn_col, name):
    bsz, t, _ = proj3.shape
    nt = t // LANES

    def body(a_ref, lb_ref, gn_ref, og_ref, or_ref):
        lb = lb_ref[...]
        gn = gn_ref[...]
        umat = _chunk_cumsum_matrix()
        lane64 = _iota((1, LANES), 1) % CHUNK

        def tile(i, carry):
            r0 = pl.multiple_of(i * LANES, LANES)
            a = a_ref[pl.ds(r0, LANES), :]
            qq, kk, lf, _, _, _ = _hgrn_gates(a, lb)
            va, ga = a[:, 256:384], a[:, 384:512]
            q_t, k_t, v_t = qq.T, kk.T, va.T
            b_t = jnp.dot(lf.T, umat, precision=HI, preferred_element_type=F32)
            new_s, o_heads = [], []
            for h in range(2):
                s_h = carry[h]
                rs = slice(CHUNK * h, CHUNK * (h + 1))
                qh, kh, vh, bh = q_t[rs], k_t[rs], v_t[rs], b_t[rs]
                inter = []
                for c in range(2):
                    cs = slice(CHUNK * c, CHUNK * (c + 1))
                    b_ = bh[:, cs]
                    qt = (qh[:, cs] * jnp.exp(b_)).astype(BF16)
                    inter.append(_dot(s_h.astype(BF16), qt, TN))
                    bl = b_[:, CHUNK - 1:CHUNK]
                    kt = (kh[:, cs] * jnp.exp(bl - b_)).astype(BF16)
                    s_h = jnp.exp(bl) * s_h + _dot(kt, vh[:, cs].astype(BF16), NT)
                new_s.append(s_h)

                acc = jnp.concatenate(inter, axis=1) + jnp.sum(qh * kh, axis=0, keepdims=True) * vh
                for dlt in range(1, CHUNK):
                    kr, br, vr = pltpu.roll(kh, dlt, 1), pltpu.roll(bh, dlt, 1), pltpu.roll(vh, dlt, 1)
                    e = jnp.exp(jnp.minimum(bh - br, 0.0))
                    att = jnp.sum(qh * kr * e, axis=0, keepdims=True)
                    acc = acc + jnp.where(lane64 >= dlt, att, 0.0) * vr
                o_heads.append(acc)
            normed = []
            for h in range(2):
                o_h = o_heads[h]
                ms = jnp.mean(o_h * o_h, axis=0, keepdims=True)
                normed.append(o_h * lax.rsqrt(ms + NORM_EPS) * gn[CHUNK * h:CHUNK * (h + 1)])
            or_ref[pl.ds(r0, LANES), :] = jnp.concatenate(o_heads, axis=0).T
            og_ref[pl.ds(r0, LANES), :] = jnp.concatenate(normed, axis=0).T * _silu(ga)
            return tuple(new_s)

        zero = jnp.zeros((CHUNK, CHUNK), F32)
        lax.fori_loop(0, nt, tile, (zero, zero))

    out = jax.ShapeDtypeStruct((bsz, t, HGRN_W), F32)
    return pl.pallas_call(
        body, grid=(bsz, 2),
        in_specs=[pl.BlockSpec((None, t, 512), lambda b, p: (b, 0, p)),
                  pl.BlockSpec((1, 128), lambda b, p: (0, p)),
                  pl.BlockSpec((128, 1), lambda b, p: (p, 0))],
        out_specs=[pl.BlockSpec((None, t, 128), lambda b, p: (b, 0, p)),
                   pl.BlockSpec((None, t, 128), lambda b, p: (b, 0, p))],
        out_shape=[out, out],
        compiler_params=_cparams(("parallel", "parallel")), name=name)(proj3, lbs_row, gn_col)


def _hgrn_bwd(proj3, o_raw, dmixed, lbs_row, gn_row, name):
    bsz, t, _ = proj3.shape
    nt = t // LANES
    nchunk = t // CHUNK

    def body(a_ref, or_ref, do_ref, lb_ref, gn_ref, da_ref, dgn_ref, dlb_ref, s_sc):
        lb = lb_ref[...]
        gn = gn_ref[...]
        umat = _chunk_cumsum_matrix()
        lane = _iota((1, LANES), 1)
        lane64 = lane % CHUNK
        half = lane < CHUNK

        def t_layout(a):
            qq, kk, lf, sg, sgn, fg = _hgrn_gates(a, lb)
            b_t = jnp.dot(lf.T, umat, precision=HI, preferred_element_type=F32)
            return qq.T, kk.T, a[:, 256:384].T, b_t, (sg, sgn, fg)

        def fwd_tile(i, carry):
            r0 = pl.multiple_of(i * LANES, LANES)
            q_t, k_t, v_t, b_t, _ = t_layout(a_ref[pl.ds(r0, LANES), :])
            new_s = []
            for h in range(2):
                s_h = carry[h]
                rs = slice(CHUNK * h, CHUNK * (h + 1))
                for c in range(2):
                    cs = slice(CHUNK * c, CHUNK * (c + 1))
                    s_sc[h, 2 * i + c] = s_h
                    b_ = b_t[rs, cs]
                    bl = b_[:, CHUNK - 1:CHUNK]
                    kt = (k_t[rs, cs] * jnp.exp(bl - b_)).astype(BF16)
                    s_h = jnp.exp(bl) * s_h + _dot(kt, v_t[rs, cs].astype(BF16), NT)
                new_s.append(s_h)
            return tuple(new_s)

        zero = jnp.zeros((CHUNK, CHUNK), F32)
        lax.fori_loop(0, nt, fwd_tile, (zero, zero))

        def half_mean(v):
            m0 = jnp.sum(jnp.where(half, v, 0.0), axis=1, keepdims=True) * (1.0 / CHUNK)
            m1 = jnp.sum(jnp.where(half, 0.0, v), axis=1, keepdims=True) * (1.0 / CHUNK)
            return jnp.where(half, m0, m1)

        def bwd_tile(k, carry):
            ds0, ds1, dgn_acc, dlb_acc = carry
            i = nt - 1 - k
            r0 = pl.multiple_of(i * LANES, LANES)
            a = a_ref[pl.ds(r0, LANES), :]
            qa, z, ga = a[:, 0:128], a[:, 128:256], a[:, 384:512]
            q_t, k_t, v_t, b_t, (sg, sgn, fg) = t_layout(a)
            oraw = or_ref[pl.ds(r0, LANES), :]
            dout = do_ref[pl.ds(r0, LANES), :]
            r = lax.rsqrt(half_mean(oraw * oraw) + NORM_EPS)
            xn = oraw * r
            dga = dout * (xn * gn) * _dsilu(ga)
            don = dout * _silu(ga)
            dgn_acc = dgn_acc + jnp.sum(don * xn, axis=0, keepdims=True)
            dxn = don * gn
            do_t = (r * (dxn - xn * half_mean(dxn * xn))).T
            new_ds, dq_h, dk_h, dv_h, db_h = [], [], [], [], []
            for h in range(2):
                ds_h = (ds0, ds1)[h]
                rs = slice(CHUNK * h, CHUNK * (h + 1))
                qh, kh, vh, bh, doh = q_t[rs], k_t[rs], v_t[rs], b_t[rs], do_t[rs]
                dq_c, dk_c, dv_c, dbl_c = [None, None], [None, None], [None, None], [None, None]
                for c in (1, 0):
                    cs = slice(CHUNK * c, CHUNK * (c + 1))
                    s_n = s_sc[h, 2 * i + c]
                    b_ = bh[:, cs]
                    eb = jnp.exp(b_)
                    bl = b_[:, CHUNK - 1:CHUNK]
                    ek = jnp.exp(bl - b_)
                    ebl = jnp.exp(bl)
                    qt, kt = qh[:, cs] * eb, kh[:, cs] * ek
                    do_c = doh[:, cs].astype(BF16)
                    dsb = ds_h.astype(BF16)
                    dv_c[c] = _dot(dsb, kt.astype(BF16), TN)
                    dkt = _dot(dsb, vh[:, cs].astype(BF16), NN)
                    dqt = _dot(s_n.astype(BF16), do_c, NN)
                    dbl_c[c] = jnp.sum(ds_h * s_n, axis=1, keepdims=True) * ebl + jnp.sum(dkt * kt, axis=1, keepdims=True)
                    dq_c[c], dk_c[c] = dqt * eb, dkt * ek
                    ds_h = ebl * ds_h + _dot(qt.astype(BF16), do_c, NT)
                new_ds.append(ds_h)

                att0 = jnp.sum(qh * kh, axis=0, keepdims=True)
                datt0 = jnp.sum(doh * vh, axis=0, keepdims=True)
                dqh = jnp.concatenate(dq_c, axis=1) + datt0 * kh
                dkh = jnp.concatenate(dk_c, axis=1) + datt0 * qh
                dvh = jnp.concatenate(dv_c, axis=1) + att0 * doh
                for dlt in range(1, CHUNK):
                    kr, br, vr = pltpu.roll(kh, dlt, 1), pltpu.roll(bh, dlt, 1), pltpu.roll(vh, dlt, 1)
                    e = jnp.where(lane64 >= dlt, jnp.exp(jnp.minimum(bh - br, 0.0)), 0.0)
                    qe = qh * e
                    att = jnp.sum(qe * kr, axis=0, keepdims=True)
                    datt = jnp.sum(doh * vr, axis=0, keepdims=True)
                    dqh = dqh + datt * (kr * e)
                    dkh = dkh + pltpu.roll(datt * qe, LANES - dlt, 1)
                    dvh = dvh + pltpu.roll(att * doh, LANES - dlt, 1)
                dbl = jnp.where(half, dbl_c[0], dbl_c[1])
                db_h.append(qh * dqh - kh * dkh + jnp.where(lane64 == CHUNK - 1, dbl, 0.0))
                dq_h.append(dqh)
                dk_h.append(dkh)
                dv_h.append(dvh)
            dqq = jnp.concatenate(dq_h, axis=0).T
            dkk = jnp.concatenate(dk_h, axis=0).T
            dvv = jnp.concatenate(dv_h, axis=0).T
            dlf = _dot(jnp.concatenate(db_h, axis=0), umat, NT, precision=HI).T
            dqa = dqq * _dsilu(qa)
            dfg = jnp.where(fg > TINY, dlf / fg, 0.0)
            dz = (dfg - dkk) * (1.0 - lb) * sg * sgn
            dlb_acc = dlb_acc + jnp.sum(dfg * (1.0 - sg) - dkk * sgn, axis=0, keepdims=True)
            da_ref[pl.ds(r0, LANES), :] = jnp.concatenate([dqa, dz, dvv, dga], axis=1)
            return new_ds[0], new_ds[1], dgn_acc, dlb_acc

        zrow = jnp.zeros((1, LANES), F32)
        _, _, dgn_acc, dlb_acc = lax.fori_loop(0, nt, bwd_tile, (zero, zero, zrow, zrow))
        dgn_ref[...] = jnp.broadcast_to(dgn_acc, (8, LANES))
        dlb_ref[...] = jnp.broadcast_to(dlb_acc, (8, LANES))

    rows = jax.ShapeDtypeStruct((bsz, 8, HGRN_W), F32)
    return pl.pallas_call(
        body, grid=(bsz, 2),
        in_specs=[pl.BlockSpec((None, t, 512), lambda b, p: (b, 0, p)),
                  pl.BlockSpec((None, t, 128), lambda b, p: (b, 0, p)),
                  pl.BlockSpec((None, t, 128), lambda b, p: (b, 0, p)),
                  pl.BlockSpec((1, 128), lambda b, p: (0, p)),
                  pl.BlockSpec((1, 128), lambda b, p: (0, p))],
        out_specs=[pl.BlockSpec((None, t, 512), lambda b, p: (b, 0, p)),
                   pl.BlockSpec((None, 8, 128), lambda b, p: (b, 0, p)),
                   pl.BlockSpec((None, 8, 128), lambda b, p: (b, 0, p))],
        out_shape=[jax.ShapeDtypeStruct((bsz, t, A_W), F32), rows, rows],
        scratch_shapes=[pltpu.VMEM((2, nchunk, CHUNK, CHUNK), F32)],
        compiler_params=_cparams(("parallel", "parallel")), name=name)(proj3, o_raw, dmixed, lbs_row, gn_row)


N_LEVELS = 6


def _hgrn_tables():
    t = np.arange(LANES)
    j = np.arange(LANES)[None, :]
    same_chunk = (t[:, None] // CHUNK) == (j // CHUNK)
    w = np.zeros((2 + N_LEVELS, LANES, LANES), np.float32)
    w[0] = same_chunk & (j <= t[:, None])
    w[1] = same_chunk & (j > t[:, None])
    maskf = np.zeros((N_LEVELS, LANES, LANES), np.float32)
    rightf = np.zeros((N_LEVELS, LANES, LANES), np.float32)
    for li in range(N_LEVELS):
        m = (CHUNK // 2) >> li
        start = t - (t % (2 * m))
        right = (t % (2 * m)) >= m
        first = np.where(right, start + m, t + 1)
        last = np.where(right, t, start + m - 1)
        w[2 + li] = (j >= first[:, None]) & (j <= last[:, None])
        maskf[li] = (t[:, None] // (2 * m)) == (j // (2 * m))
        rightf[li] = right[:, None]
    return jnp.asarray(w.reshape(-1, LANES), BF16), jnp.asarray(maskf), jnp.asarray(rightf)


def _split(x, n):
    parts = []
    for _ in range(n - 1):
        p = x.astype(BF16)
        parts.append(p)
        x = x - p.astype(F32)
    parts.append(x.astype(BF16))
    return parts


def _exact_dot(w, parts):
    acc = jnp.dot(w, parts[0], preferred_element_type=F32)
    for p in parts[1:]:
        acc = acc + jnp.dot(w, p, preferred_element_type=F32)
    return acc


def _head_sums(v, ones_blk, n=2):
    parts = _split(v, n)
    acc = jnp.dot(parts[0], ones_blk, preferred_element_type=F32)
    for p in parts[1:]:
        acc = acc + jnp.dot(p, ones_blk, preferred_element_type=F32)
    return acc


def _hgrn_consts():
    r, c = _iota((LANES, LANES), 0), _iota((LANES, LANES), 1)
    eye = r == c
    ones_blk = ((r // CHUNK) == (c // CHUNK)).astype(BF16)
    return eye, ones_blk, jnp.ones((CHUNK, LANES), BF16)


def _hgrn_levels(qq, kk, parts, w_ref, mk_ref, rt_ref, d_att=None):
    att = [jnp.zeros((LANES, LANES), F32)] * 2
    dq = dk = db = jnp.zeros((LANES, LANES), F32)
    for li in range(N_LEVELS):
        e = jnp.exp(_exact_dot(w_ref[(2 + li) * LANES:(3 + li) * LANES, :], parts))
        rt = rt_ref[li]
        mk = mk_ref[li]
        qef, kef = e * rt, e * (1.0 - rt)
        qe, ke = (qq * qef).astype(BF16), (kk * kef).astype(BF16)
        dqs, dks = [], []
        for h in range(2):
            hs = slice(CHUNK * h, CHUNK * (h + 1))
            att[h] = att[h] + _dot(qe[:, hs], ke[:, hs], NT) * mk
            if d_att is not None:
                dam = (d_att[h] * mk).astype(BF16)
                dqs.append(jnp.dot(dam, ke[:, hs], preferred_element_type=F32))
                dks.append(_dot(dam, qe[:, hs], TN))
        if d_att is not None:
            dqe, dke = jnp.concatenate(dqs, axis=1), jnp.concatenate(dks, axis=1)
            dq = dq + dqe * qef
            dk = dk + dke * kef
            db = db + (dqe * qe.astype(F32) - dke * ke.astype(F32))
    return att, dq, dk, db


def _hgrn_fwd(proj3, lbs_row, gn_row, name):
    bsz, t, _ = proj3.shape
    nt = t // LANES
    w_all, maskf, rightf = _hgrn_tables()

    def body(a_ref, lb_ref, gn_ref, w_ref, mk_ref, rt_ref, og_ref, or_ref):
        lb = lb_ref[...]
        gn = gn_ref[...]
        eye, ones_blk, ones_h = _hgrn_consts()

        def tile(i, carry):
            r0 = pl.multiple_of(i * LANES, LANES)
            a = a_ref[pl.ds(r0, LANES), :]
            qq, kk, lf, _, _, _ = _hgrn_gates(a, lb)
            va, ga = a[:, 256:384], a[:, 384:512]
            parts = _split(lf, 3)
            eb = jnp.exp(_exact_dot(w_ref[0:LANES, :], parts))
            ee = jnp.exp(_exact_dot(w_ref[LANES:2 * LANES, :], parts))
            vb = va.astype(BF16)
            att, _, _, _ = _hgrn_levels(qq, kk, parts, w_ref, mk_ref, rt_ref)
            qk = _split(qq * kk, 2)
            qeb, keb = (qq * eb).astype(BF16), (kk * ee).astype(BF16)
            new_s, o_heads = [], []
            for h in range(2):
                hs = slice(CHUNK * h, CHUNK * (h + 1))
                diag = _exact_dot_r(qk, hs, ones_h)
                a_h = att[h] + jnp.where(eye, diag, 0.0)
                o_h = jnp.dot(a_h.astype(BF16), vb[:, hs], preferred_element_type=F32)
                st = carry[h]
                chunks = []
                for c in range(2):
                    rc = slice(CHUNK * c, CHUNK * (c + 1))
                    chunks.append(o_h[rc] + _dot(qeb[rc, hs], st.astype(BF16), NT))
                    ebl = eb[CHUNK * (c + 1) - 1:CHUNK * (c + 1), hs]
                    st = st * ebl + _dot(vb[rc, hs], keb[rc, hs], TN)
                new_s.append(st)
                o_heads.append(jnp.concatenate(chunks, axis=0))
            o = jnp.concatenate(o_heads, axis=1)
            ms = _head_sums(o * o, ones_blk) * (1.0 / CHUNK)
            or_ref[pl.ds(r0, LANES), :] = o
            og_ref[pl.ds(r0, LANES), :] = o * lax.rsqrt(ms + NORM_EPS) * gn * _silu(ga)
            return tuple(new_s)

        zero = jnp.zeros((CHUNK, CHUNK), F32)
        lax.fori_loop(0, nt, tile, (zero, zero))

    out = jax.ShapeDtypeStruct((bsz, t, HGRN_W), F32)
    row = pl.BlockSpec((1, 128), lambda b, p: (0, p))
    return pl.pallas_call(
        body, grid=(bsz, 2),
        in_specs=[pl.BlockSpec((None, t, 512), lambda b, p: (b, 0, p)), row, row,
                  pl.BlockSpec(w_all.shape, lambda b, p: (0, 0)),
                  pl.BlockSpec(maskf.shape, lambda b, p: (0, 0, 0)),
                  pl.BlockSpec(rightf.shape, lambda b, p: (0, 0, 0))],
        out_specs=[pl.BlockSpec((None, t, 128), lambda b, p: (b, 0, p)),
                   pl.BlockSpec((None, t, 128), lambda b, p: (b, 0, p))],
        out_shape=[out, out],
        compiler_params=_cparams(("parallel", "parallel")), name=name)(proj3, lbs_row, gn_row, w_all, maskf, rightf)


def _exact_dot_r(parts, hs, ones_h):
    acc = jnp.dot(parts[0][:, hs], ones_h, preferred_element_type=F32)
    for p in parts[1:]:
        acc = acc + jnp.dot(p[:, hs], ones_h, preferred_element_type=F32)
    return acc


def _hgrn_bwd(proj3, o_raw, dmixed, lbs_row, gn_row, name):
    bsz, t, _ = proj3.shape
    nt = t // LANES
    nchunk = t // CHUNK
    w_all, maskf, rightf = _hgrn_tables()

    def body(a_ref, or_ref, do_ref, lb_ref, gn_ref, w_ref, mk_ref, rt_ref, da_ref, dgn_ref, dlb_ref, s_sc):
        lb = lb_ref[...]
        gn = gn_ref[...]
        eye, ones_blk, ones_h = _hgrn_consts()
        r_i, c_i = _iota((LANES, LANES), 0), _iota((LANES, LANES), 1)
        suffix = ((c_i >= r_i) & ((r_i // CHUNK) == (c_i // CHUNK))).astype(BF16)
        row64 = _iota((LANES, CHUNK), 0)
        ones_t = jnp.ones((LANES, CHUNK), BF16)

        def fwd_tile(i, carry):
            r0 = pl.multiple_of(i * LANES, LANES)
            a = a_ref[pl.ds(r0, LANES), :]
            _, kk, lf, _, _, _ = _hgrn_gates(a, lb)
            parts = _split(lf, 3)
            eb = jnp.exp(_exact_dot(w_ref[0:LANES, :], parts))
            ee = jnp.exp(_exact_dot(w_ref[LANES:2 * LANES, :], parts))
            vb, keb = a[:, 256:384].astype(BF16), (kk * ee).astype(BF16)
            new_s = []
            for h in range(2):
                hs = slice(CHUNK * h, CHUNK * (h + 1))
                st = carry[h]
                for c in range(2):
                    rc = slice(CHUNK * c, CHUNK * (c + 1))
                    s_sc[h, 2 * i + c] = st
                    st = st * eb[CHUNK * (c + 1) - 1:CHUNK * (c + 1), hs] + _dot(vb[rc, hs], keb[rc, hs], TN)
                new_s.append(st)
            return tuple(new_s)

        zero = jnp.zeros((CHUNK, CHUNK), F32)
        lax.fori_loop(0, nt, fwd_tile, (zero, zero))

        def bwd_tile(k, carry):
            dst0, dst1, dgn_acc, dlb_acc = carry
            i = nt - 1 - k
            r0 = pl.multiple_of(i * LANES, LANES)
            a = a_ref[pl.ds(r0, LANES), :]
            qa, ga = a[:, 0:128], a[:, 384:512]
            qq, kk, lf, sg, sgn, fg = _hgrn_gates(a, lb)
            parts = _split(lf, 3)
            eb = jnp.exp(_exact_dot(w_ref[0:LANES, :], parts))
            ee = jnp.exp(_exact_dot(w_ref[LANES:2 * LANES, :], parts))
            vb = a[:, 256:384].astype(BF16)
            oraw = or_ref[pl.ds(r0, LANES), :]
            dout = do_ref[pl.ds(r0, LANES), :]
            r = lax.rsqrt(_head_sums(oraw * oraw, ones_blk) * (1.0 / CHUNK) + NORM_EPS)
            xn = oraw * r
            dga = dout * (xn * gn) * _dsilu(ga)
            don = dout * _silu(ga)
            dgn_acc = dgn_acc + jnp.sum(don * xn, axis=0, keepdims=True)
            dxn = don * gn
            do = r * (dxn - xn * (_head_sums(dxn * xn, ones_blk) * (1.0 / CHUNK)))
            dob = do.astype(BF16)
            d_att = [_dot(dob[:, CHUNK * h:CHUNK * (h + 1)], vb[:, CHUNK * h:CHUNK * (h + 1)], NT) for h in range(2)]
            att, dq, dk, db_lv = _hgrn_levels(qq, kk, parts, w_ref, mk_ref, rt_ref, d_att)
            qk = _split(qq * kk, 2)
            qe_f, ke_f = qq * eb, kk * ee
            qeb, keb = qe_f.astype(BF16), ke_f.astype(BF16)
            new_ds, dq_h, dk_h, dv_h, dbl_h = [], [], [], [], []
            for h in range(2):
                hs = slice(CHUNK * h, CHUNK * (h + 1))
                a_h = att[h] + jnp.where(eye, _exact_dot_r(qk, hs, ones_h), 0.0)
                dv = _dot(a_h.astype(BF16), dob[:, hs], TN)
                ddiag = _exact_dot_r(_split(jnp.where(eye, d_att[h], 0.0), 2), slice(None), ones_t)
                dq_i = dq[:, hs] + ddiag * kk[:, hs]
                dk_i = dk[:, hs] + ddiag * qq[:, hs]
                dst = (dst0, dst1)[h]
                dq_c, dk_c, dv_c, dbl_c = [None, None], [None, None], [None, None], [None, None]
                for c in (1, 0):
                    rc = slice(CHUNK * c, CHUNK * (c + 1))
                    st_n = s_sc[h, 2 * i + c]
                    ebl = eb[CHUNK * (c + 1) - 1:CHUNK * (c + 1), hs]
                    dstb = dst.astype(BF16)
                    dv_c[c] = _dot(keb[rc, hs], dstb, NT)
                    dke = jnp.dot(vb[rc, hs], dstb, preferred_element_type=F32)
                    dqe = jnp.dot(dob[rc, hs], st_n.astype(BF16), preferred_element_type=F32)
                    dbl_c[c] = (jnp.sum(dst * st_n, axis=0, keepdims=True) * ebl
                                + jnp.sum(dke * ke_f[rc, hs], axis=0, keepdims=True))
                    dq_c[c], dk_c[c] = dqe * eb[rc, hs], dke * ee[rc, hs]
                    dst = dst * ebl + _dot(dob[rc, hs], qeb[rc, hs], TN)
                new_ds.append(dst)
                dq_x, dk_x = jnp.concatenate(dq_c, axis=0), jnp.concatenate(dk_c, axis=0)
                dq_h.append(dq_i + dq_x)
                dk_h.append(dk_i + dk_x)
                dv_h.append(dv + jnp.concatenate(dv_c, axis=0))
                dbl_h.append(qq[:, hs] * dq_x - kk[:, hs] * dk_x
                             + jnp.where(row64 == CHUNK - 1, dbl_c[0], 0.0) + jnp.where(row64 == LANES - 1, dbl_c[1], 0.0))
            dqq = jnp.concatenate(dq_h, axis=1)
            dkk = jnp.concatenate(dk_h, axis=1)
            dvv = jnp.concatenate(dv_h, axis=1)
            db = db_lv + jnp.concatenate(dbl_h, axis=1)
            dlf = _exact_dot(suffix, _split(db, 3))
            dqa = dqq * _dsilu(qa)
            dfg = jnp.where(fg > TINY, dlf / fg, 0.0)
            dz = (dfg - dkk) * (1.0 - lb) * sg * sgn
            dlb_acc = dlb_acc + jnp.sum(dfg * (1.0 - sg) - dkk * sgn, axis=0, keepdims=True)
            da_ref[pl.ds(r0, LANES), :] = jnp.concatenate([dqa, dz, dvv, dga], axis=1)
            return new_ds[0], new_ds[1], dgn_acc, dlb_acc

        zrow = jnp.zeros((1, LANES), F32)
        _, _, dgn_acc, dlb_acc = lax.fori_loop(0, nt, bwd_tile, (zero, zero, zrow, zrow))
        dgn_ref[...] = jnp.broadcast_to(dgn_acc, (8, LANES))
        dlb_ref[...] = jnp.broadcast_to(dlb_acc, (8, LANES))

    rows = jax.ShapeDtypeStruct((bsz, 8, HGRN_W), F32)
    row = pl.BlockSpec((1, 128), lambda b, p: (0, p))
    blk = pl.BlockSpec((None, t, 128), lambda b, p: (b, 0, p))
    return pl.pallas_call(
        body, grid=(bsz, 2),
        in_specs=[pl.BlockSpec((None, t, 512), lambda b, p: (b, 0, p)), blk, blk, row, row,
                  pl.BlockSpec(w_all.shape, lambda b, p: (0, 0)),
                  pl.BlockSpec(maskf.shape, lambda b, p: (0, 0, 0)),
                  pl.BlockSpec(rightf.shape, lambda b, p: (0, 0, 0))],
        out_specs=[pl.BlockSpec((None, t, 512), lambda b, p: (b, 0, p)),
                   pl.BlockSpec((None, 8, 128), lambda b, p: (b, 0, p)),
                   pl.BlockSpec((None, 8, 128), lambda b, p: (b, 0, p))],
        out_shape=[jax.ShapeDtypeStruct((bsz, t, A_W), F32), rows, rows],
        scratch_shapes=[pltpu.VMEM((2, nchunk, CHUNK, CHUNK), F32)],
        compiler_params=_cparams(("parallel", "parallel")), name=name)(
            proj3, o_raw, dmixed, lbs_row, gn_row, w_all, maskf, rightf)


def _pool_tt(t):
    return min(256, t)


def _window_select(s2, s4, s8, s16, lane):
    return jnp.where(lane < 64, s2, jnp.where(lane < 128, s4, jnp.where(lane < 192, s8, s16)))


def _pool_counts(t0, tt):
    lane = _iota((tt, POOL_W), 1)
    tpos = (_iota((tt, POOL_W), 0) + t0 + 1).astype(F32)
    win = jnp.where(lane < 64, 2.0, jnp.where(lane < 128, 4.0, jnp.where(lane < 192, 8.0, 16.0)))
    return 1.0 / jnp.minimum(tpos, win), lane


def _pooled_tile(upad_ref, i, tt):
    r0 = pl.multiple_of(i * tt, 8)
    cat = upad_ref[pl.ds(r0, tt + POOL_HALO), :]
    s2 = cat + pltpu.roll(cat, 1, 0)
    s4 = s2 + pltpu.roll(s2, 2, 0)
    s8 = s4 + pltpu.roll(s4, 4, 0)
    s16 = s8 + pltpu.roll(s8, 8, 0)
    inv, lane = _pool_counts(i * tt, tt)
    sel = _window_select(s2[POOL_HALO:], s4[POOL_HALO:], s8[POOL_HALO:], s16[POOL_HALO:], lane)
    return sel * inv - cat[POOL_HALO:], inv, lane


def _pool_fwd(proj3, wbd, scale_row, name):
    bsz, t, _ = proj3.shape
    tt = _pool_tt(t)

    def body(p_ref, w_ref, sc_ref, o_ref, upad):
        upad[0:POOL_HALO, :] = jnp.zeros((POOL_HALO, POOL_W), F32)
        upad[POOL_HALO:, :] = p_ref[:, 0:POOL_W]
        w = w_ref[...]
        sc = sc_ref[...]

        def tile(i, c):
            pooled, _, _ = _pooled_tile(upad, i, tt)
            r0 = pl.multiple_of(i * tt, 8)
            g = p_ref[pl.ds(r0, tt), POOL_W:2 * POOL_W]
            pre = jnp.dot(pooled.astype(BF16), w, preferred_element_type=F32)
            o_ref[pl.ds(r0, tt), :] = pre * sc * _silu(g)
            return c

        lax.fori_loop(0, t // tt, tile, 0)

    return pl.pallas_call(
        body, grid=(bsz,),
        in_specs=[pl.BlockSpec((None, t, 512), lambda b: (b, 0, B_BLK)),
                  pl.BlockSpec((POOL_W, POOL_W), lambda b: (0, 0)),
                  pl.BlockSpec((1, POOL_W), lambda b: (0, 0))],
        out_specs=pl.BlockSpec((None, t, POOL_W), lambda b: (b, 0, 0)),
        out_shape=jax.ShapeDtypeStruct((bsz, t, POOL_W), F32),
        scratch_shapes=[pltpu.VMEM((t + POOL_HALO, POOL_W), F32)],
        compiler_params=_cparams(("parallel",)), name=name)(proj3, wbd, scale_row)


def _pool_bwd(proj3, dmixed, wbd, scale_row, name):
    bsz, t, _ = proj3.shape
    tt = _pool_tt(t)

    def body(p_ref, do_ref, w_ref, sc_ref, db_ref, dsc_ref, dw_ref, upad, epad):
        upad[0:POOL_HALO, :] = jnp.zeros((POOL_HALO, POOL_W), F32)
        upad[POOL_HALO:, :] = p_ref[:, 0:POOL_W]
        epad[t:, :] = jnp.zeros((POOL_HALO, POOL_W), F32)
        w = w_ref[...]
        sc = sc_ref[...]

        def tile(i, carry):
            dsc_acc, dw_acc = carry
            pooled, inv, _ = _pooled_tile(upad, i, tt)
            r0 = pl.multiple_of(i * tt, 8)
            g = p_ref[pl.ds(r0, tt), POOL_W:2 * POOL_W]
            dout = do_ref[pl.ds(r0, tt), :]
            pb = pooled.astype(BF16)
            pre = jnp.dot(pb, w, preferred_element_type=F32)
            t1 = dout * _silu(g)
            dsc_acc = dsc_acc + jnp.sum(t1 * pre, axis=0, keepdims=True)
            dpre = (t1 * sc).astype(BF16)
            db_ref[pl.ds(r0, tt), POOL_W:2 * POOL_W] = dout * pre * sc * _dsilu(g)
            dw_acc = dw_acc + _dot(pb, dpre, TN)
            dpooled = _dot(dpre, w, NT)
            epad[pl.ds(r0, tt), :] = dpooled * inv
            return dsc_acc, dw_acc

        dsc_acc, dw_acc = lax.fori_loop(0, t // tt, tile, (jnp.zeros((1, POOL_W), F32), jnp.zeros((POOL_W, POOL_W), F32)))
        dsc_ref[...] = jnp.broadcast_to(dsc_acc, (8, POOL_W))
        dw_ref[...] = dw_acc

        def tile2(i, c):
            r0 = pl.multiple_of(i * tt, 8)
            n = tt + POOL_HALO
            cat = epad[pl.ds(r0, n), :]
            s2 = cat + pltpu.roll(cat, n - 1, 0)
            s4 = s2 + pltpu.roll(s2, n - 2, 0)
            s8 = s4 + pltpu.roll(s4, n - 4, 0)
            s16 = s8 + pltpu.roll(s8, n - 8, 0)
            inv, lane = _pool_counts(i * tt, tt)
            sel = _window_select(s2[:tt], s4[:tt], s8[:tt], s16[:tt], lane)
            db_ref[pl.ds(r0, tt), 0:POOL_W] = sel - cat[:tt] / inv
            return c

        lax.fori_loop(0, t // tt, tile2, 0)

    return pl.pallas_call(
        body, grid=(bsz,),
        in_specs=[pl.BlockSpec((None, t, 512), lambda b: (b, 0, B_BLK)),
                  pl.BlockSpec((None, t, POOL_W), lambda b: (b, 0, 1)),
                  pl.BlockSpec((POOL_W, POOL_W), lambda b: (0, 0)),
                  pl.BlockSpec((1, POOL_W), lambda b: (0, 0))],
        out_specs=[pl.BlockSpec((None, t, 512), lambda b: (b, 0, 0)),
                   pl.BlockSpec((None, 8, POOL_W), lambda b: (b, 0, 0)),
                   pl.BlockSpec((None, POOL_W, POOL_W), lambda b: (b, 0, 0))],
        out_shape=[jax.ShapeDtypeStruct((bsz, t, B_W), F32), jax.ShapeDtypeStruct((bsz, 8, POOL_W), F32),
                   jax.ShapeDtypeStruct((bsz, POOL_W, POOL_W), F32)],
        scratch_shapes=[pltpu.VMEM((t + POOL_HALO, POOL_W), F32), pltpu.VMEM((t + POOL_HALO, POOL_W), F32)],
        compiler_params=_cparams(("parallel",)), name=name)(proj3, dmixed, wbd, scale_row)


def _head_select_rows(hp):
    r, c = _iota((8, LANES), 0), _iota((8, LANES), 1)
    return ((r < 2) & (c == 2 * hp + r)).astype(F32)


def _foxgate_fwd(proj3, bias_row, name):
    bsz, t, _ = proj3.shape
    nt = t // LANES

    def body(f_ref, b_ref, cn_ref, ct_ref):
        bias = b_ref[...]
        i, j = _iota((LANES, LANES), 0), _iota((LANES, LANES), 1)
        lower = (j <= i).astype(F32)
        spread = (_iota((LANES, FOX_W), 0) == _iota((LANES, FOX_W), 1) // 64).astype(F32)

        def tile(k, carry):
            r0 = pl.multiple_of(k * LANES, LANES)
            xg = f_ref[pl.ds(r0, LANES), :] + bias
            lf = jnp.minimum(xg, 0.0) - jnp.log(1.0 + jnp.exp(-jnp.abs(xg)))
            c = jnp.dot(lower, lf, precision=HI, preferred_element_type=F32) + carry
            cn_ref[pl.ds(r0, LANES), :] = jnp.dot(c, spread, precision=HI, preferred_element_type=F32)
            for hp in range(4):
                ct_ref[hp, :, pl.ds(r0, LANES)] = _dot(_head_select_rows(hp), c, NT, precision=HI)
            return c[LANES - 1:LANES, :]

        lax.fori_loop(0, nt, tile, jnp.zeros((1, LANES), F32))

    return pl.pallas_call(
        body, grid=(bsz,),
        in_specs=[pl.BlockSpec((None, t, 128), lambda b: (b, 0, F_BLK)), pl.BlockSpec((1, 128), lambda b: (0, 0))],
        out_specs=[pl.BlockSpec((None, t, FOX_W), lambda b: (b, 0, 0)),
                   pl.BlockSpec((None, 4, 8, t), lambda b: (b, 0, 0, 0))],
        out_shape=[jax.ShapeDtypeStruct((bsz, t, FOX_W), F32), jax.ShapeDtypeStruct((bsz, 4, 8, t), F32)],
        compiler_params=_cparams(("parallel",)), name=name)(proj3, bias_row)


def _foxgate_bwd(proj3, dc_nat, bias_row, name):
    bsz, t, _ = proj3.shape
    nt = t // LANES

    def body(f_ref, dc_ref, b_ref, df_ref, dbias_ref, run_sc):
        bias = b_ref[...]
        i, j = _iota((LANES, LANES), 0), _iota((LANES, LANES), 1)
        upper = (j >= i).astype(F32)
        valid = _iota((1, LANES), 1) < FOX_HEADS
        run_sc[...] = jnp.zeros((8, LANES), F32)
        dbias_ref[...] = jnp.zeros((8, LANES), F32)

        def tile(k, c):
            r0 = pl.multiple_of((nt - 1 - k) * LANES, LANES)
            dc = dc_ref[pl.ds(r0, LANES), :] + jnp.where(i == LANES - 1, run_sc[0:1, :], 0.0)
            dlf = jnp.dot(upper, dc, precision=HI, preferred_element_type=F32)
            xg = f_ref[pl.ds(r0, LANES), :] + bias
            df = jnp.where(valid, dlf * _sig(-xg), 0.0)
            df_ref[pl.ds(r0, LANES), :] = df
            run_sc[...] = dlf[0:8, :]
            dbias_ref[...] += jnp.sum(df, axis=0, keepdims=True)
            return c

        lax.fori_loop(0, nt, tile, 0)

    blk = pl.BlockSpec((None, t, 128), lambda b: (b, 0, 0))
    return pl.pallas_call(
        body, grid=(bsz,),
        in_specs=[pl.BlockSpec((None, t, 128), lambda b: (b, 0, F_BLK)), blk, pl.BlockSpec((1, 128), lambda b: (0, 0))],
        out_specs=[blk, pl.BlockSpec((None, 8, 128), lambda b: (b, 0, 0))],
        out_shape=[jax.ShapeDtypeStruct((bsz, t, F_W), F32), jax.ShapeDtypeStruct((bsz, 8, 128), F32)],
        scratch_shapes=[pltpu.VMEM((8, LANES), F32)],
        compiler_params=_cparams(("parallel",)), name=name)(proj3, dc_nat, bias_row)


def _fox_tile(t):
    return min(256, t)


def _fox_fwd(proj3, c_nat, c_t, name):
    bsz, t, _ = proj3.shape
    tq = _fox_tile(t)
    tk = min(2 * tq, t)
    nq = t // tq

    def body(q_ref, kv_ref, cn_ref, ct_ref, og_ref, or_ref, lse_ref):
        i = pl.program_id(2)
        qblk = q_ref[...]
        first = _iota((1, 128), 1) < 64
        qv = qblk[:, 0:128] * 0.125
        qm = [jnp.where(first, qv, 0.0).astype(BF16), jnp.where(first, 0.0, qv).astype(BF16)]
        cqs = [cn_ref[:, 0:1], cn_ref[:, 64:65]]
        rows = _iota((tq, tk), 0) + i * tq

        def kv_step(j, carry, masked):
            c0 = pl.multiple_of(j * tk, tk)
            kb = kv_ref[pl.ds(c0, tk), 128:256].astype(BF16)
            vblk = kv_ref[pl.ds(c0, tk), 256:384]
            vx = [jnp.where(first, vblk, 1.0).astype(BF16), jnp.where(first, 1.0, vblk).astype(BF16)]
            new = []
            for h in range(2):
                m, acc = carry[2 * h], carry[2 * h + 1]
                s = _dot(qm[h], kb, NT) + (cqs[h] - ct_ref[h:h + 1, pl.ds(c0, tk)])
                if masked:
                    s = jnp.where(rows >= _iota((tq, tk), 1) + j * tk, s, MASK_VALUE)
                m_new = jnp.maximum(m, jnp.max(s, axis=1, keepdims=True))
                p = jnp.exp(s - m_new).astype(BF16)
                new += [m_new, jnp.exp(m - m_new) * acc + jnp.dot(p, vx[h], preferred_element_type=F32)]
            return tuple(new)

        init = (jnp.full((tq, 1), MASK_VALUE, F32), jnp.zeros((tq, 128), F32)) * 2
        n_full = (i * tq) // tk
        carry = lax.fori_loop(0, n_full, functools.partial(kv_step, masked=False), init)
        m0, acc0, m1, acc1 = kv_step(n_full, carry, True)
        l0, l1 = pltpu.roll(acc0, 64, 1), pltpu.roll(acc1, 64, 1)
        o = jnp.where(first, acc0 / l0, acc1 / l1)
        or_ref[...] = o
        og_ref[...] = o * _silu(qblk[:, 384:512])
        lse_ref[...] = jnp.where(first, m0 + jnp.log(l0), m1 + jnp.log(l1))

    out = jax.ShapeDtypeStruct((bsz, t, FOX_W), F32)
    blk = pl.BlockSpec((None, tq, 128), lambda b, p, i: (b, i, p))
    return pl.pallas_call(
        body, grid=(bsz, 4, nq),
        in_specs=[pl.BlockSpec((None, tq, 512), lambda b, p, i: (b, i, C_BLK0 + p)),
                  pl.BlockSpec((None, t, 512), lambda b, p, i: (b, 0, C_BLK0 + p)),
                  blk,
                  pl.BlockSpec((None, None, 8, t), lambda b, p, i: (b, p, 0, 0))],
        out_specs=[blk, blk, blk],
        out_shape=[out, out, out],
        compiler_params=_cparams(("parallel", "parallel", "arbitrary")), name=name)(proj3, proj3, c_nat, c_t)


def _fox_bwd(proj3, o_raw, dmixed, lse, c_nat, c_t, name):
    bsz, t, _ = proj3.shape
    tq = _fox_tile(t)
    nq = t // tq

    def body(a_ref, or_ref, do_ref, lse_ref, cn_ref, ct_ref, dc_out, dct_out, drow_out, dq_sc, do_sc, dl_sc):
        def prep(i, c):
            r0 = pl.multiple_of(i * tq, tq)
            g = a_ref[pl.ds(r0, tq), 384:512]
            dout = do_ref[pl.ds(r0, tq), :]
            o = or_ref[pl.ds(r0, tq), :]
            dc_out[pl.ds(r0, tq), 384:512] = dout * o * _dsilu(g)
            do = dout * _silu(g)
            do_sc[pl.ds(r0, tq), :] = do
            prod = do * o
            d0 = jnp.sum(prod[:, 0:64], axis=1, keepdims=True)
            d1 = jnp.sum(prod[:, 64:128], axis=1, keepdims=True)
            dl_sc[pl.ds(r0, tq), :] = jnp.concatenate([jnp.broadcast_to(d0, (tq, 64)), jnp.broadcast_to(d1, (tq, 64))], axis=1)
            dq_sc[pl.ds(r0, tq), :] = jnp.zeros((tq, 128), F32)
            drow_out[pl.ds(r0, tq), :] = jnp.zeros((tq, 128), F32)
            return c

        lax.fori_loop(0, nq, prep, 0)
        dct_out[...] = jnp.zeros((8, t), F32)

        causal = _iota((tq, tq), 0) >= _iota((tq, tq), 1)

        first = _iota((1, 128), 1) < 64

        def heads(v):
            return [jnp.where(first, v, 0.0).astype(BF16), jnp.where(first, 0.0, v).astype(BF16)]

        def kv_tile(j, c):
            c0 = pl.multiple_of(j * tq, tq)
            kb = a_ref[pl.ds(c0, tq), 128:256].astype(BF16)
            vb = a_ref[pl.ds(c0, tq), 256:384].astype(BF16)
            cks = [ct_ref[h:h + 1, pl.ds(c0, tq)] for h in range(2)]

            def q_step(i, carry, diagonal):
                dk, dv, dcol0, dcol1 = carry
                r0 = pl.multiple_of(i * tq, tq)
                qv = a_ref[pl.ds(r0, tq), 0:128] * 0.125
                do = do_sc[pl.ds(r0, tq), :]
                qb, dob = qv.astype(BF16), do.astype(BF16)
                qm, dom = heads(qv), heads(do)
                full, dcols, rsums = [], [], []
                for h in range(2):
                    lse_h = lse_ref[pl.ds(r0, tq), 64 * h:64 * h + 1]
                    dl_h = dl_sc[pl.ds(r0, tq), 64 * h:64 * h + 1]
                    cq = cn_ref[pl.ds(r0, tq), 64 * h:64 * h + 1]
                    p = jnp.exp(_dot(qm[h], kb, NT) + (cq - cks[h]) - lse_h)
                    if diagonal:
                        p = jnp.where(causal, p, 0.0)
                    ds = p * (_dot(dom[h], vb, NT) - dl_h)
                    dsb = ds.astype(BF16)
                    full.append((_dot(p.astype(BF16), dob, TN), _dot(dsb, qb, TN),
                                 jnp.dot(dsb, kb, preferred_element_type=F32)))
                    dcols.append(jnp.sum(ds, axis=0, keepdims=True))
                    rsums.append(jnp.broadcast_to(jnp.sum(ds, axis=1, keepdims=True), (tq, 128)))
                dq_sc[pl.ds(r0, tq), :] += jnp.where(first, full[0][2], full[1][2]) * 0.125
                drow_out[pl.ds(r0, tq), :] += jnp.where(first, rsums[0], rsums[1])
                return (dk + jnp.where(first, full[0][1], full[1][1]), dv + jnp.where(first, full[0][0], full[1][0]),
                        dcol0 - dcols[0], dcol1 - dcols[1])

            init = (jnp.zeros((tq, 128), F32), jnp.zeros((tq, 128), F32), jnp.zeros((1, tq), F32), jnp.zeros((1, tq), F32))
            carry = q_step(j, init, True)
            dk, dv, dcol0, dcol1 = lax.fori_loop(j + 1, nq, functools.partial(q_step, diagonal=False), carry)
            dct_out[0:1, pl.ds(c0, tq)] = dcol0
            dct_out[1:2, pl.ds(c0, tq)] = dcol1
            dc_out[pl.ds(c0, tq), 128:256] = dk
            dc_out[pl.ds(c0, tq), 256:384] = dv
            return c

        lax.fori_loop(0, nq, kv_tile, 0)
        dc_out[:, 0:128] = dq_sc[...]

    blk = pl.BlockSpec((None, t, 128), lambda b, p: (b, 0, p))
    return pl.pallas_call(
        body, grid=(bsz, 4),
        in_specs=[pl.BlockSpec((None, t, 512), lambda b, p: (b, 0, C_BLK0 + p)),
                  blk,
                  pl.BlockSpec((None, t, 128), lambda b, p: (b, 0, 4 + p)),
                  blk, blk,
                  pl.BlockSpec((None, None, 8, t), lambda b, p: (b, p, 0, 0))],
        out_specs=[pl.BlockSpec((None, t, 512), lambda b, p: (b, 0, p)),
                   pl.BlockSpec((None, None, 8, t), lambda b, p: (b, p, 0, 0)), blk],
        out_shape=[jax.ShapeDtypeStruct((bsz, t, C_W), F32), jax.ShapeDtypeStruct((bsz, 4, 8, t), F32),
                   jax.ShapeDtypeStruct((bsz, t, FOX_W), F32)],
        scratch_shapes=[pltpu.VMEM((t, 128), F32), pltpu.VMEM((t, 128), F32), pltpu.VMEM((t, 128), F32)],
        compiler_params=_cparams(("parallel", "parallel")), name=name)(proj3, o_raw, dmixed, lse, c_nat, c_t)


def _mix_tm(n):
    return min(512, n)


def _outproj_fwd(x2, oa, ob, oc, wo, g_row, name):
    n, d = x2.shape
    tm = _mix_tm(n)

    def body(x_ref, oa_ref, ob_ref, oc_ref, w_ref, g_ref, y_ref, xo_ref):
        y = (jnp.dot(oa_ref[...].astype(BF16), w_ref[0:256, :], preferred_element_type=F32)
             + jnp.dot(ob_ref[...].astype(BF16), w_ref[256:512, :], preferred_element_type=F32)
             + jnp.dot(oc_ref[...].astype(BF16), w_ref[512:1024, :], preferred_element_type=F32))
        y_ref[...] = y
        xo_ref[...] = x_ref[...] + y * _rstd(y) * g_ref[...]

    row = lambda w: pl.BlockSpec((tm, w), lambda i: (i, 0))
    out = jax.ShapeDtypeStruct((n, d), F32)
    return pl.pallas_call(
        body, grid=(n // tm,),
        in_specs=[row(d), row(256), row(256), row(512), pl.BlockSpec((d, d), lambda i: (0, 0)),
                  pl.BlockSpec((1, d), lambda i: (0, 0))],
        out_specs=[row(d), row(d)], out_shape=[out, out],
        compiler_params=_cparams(("parallel",)), name=name)(x2, oa, ob, oc, wo, g_row)


def _loss_head(x2, target2, name):
    n, d = x2.shape
    tm = _mix_tm(n)

    def body(x_ref, t_ref, dx_ref, l_ref):
        err = x_ref[...] - t_ref[...]
        dx_ref[...] = err * (1.0 / d)

        @pl.when(pl.program_id(0) == 0)
        def _():
            l_ref[...] = jnp.zeros((8, 128), F32)

        l_ref[...] += jnp.sum(err * err)

    row = pl.BlockSpec((tm, d), lambda i: (i, 0))
    return pl.pallas_call(
        body, grid=(n // tm,), in_specs=[row, row],
        out_specs=[row, pl.BlockSpec((8, 128), lambda i: (0, 0))],
        out_shape=[jax.ShapeDtypeStruct((n, d), F32), jax.ShapeDtypeStruct((8, 128), F32)],
        compiler_params=_cparams(("arbitrary",)), name=name)(x2, target2)


def _outproj_bwd(dxo, y, oa, ob, oc, wo, g_row, name):
    n, d = dxo.shape
    tm = _mix_tm(n)

    def body(dx_ref, y_ref, oa_ref, ob_ref, oc_ref, w_ref, g_ref, dm_ref, dw_ref, dg_ref):
        @pl.when(pl.program_id(0) == 0)
        def _():
            dw_ref[...] = jnp.zeros((d, d), F32)
            dg_ref[...] = jnp.zeros((8, d), F32)

        yv, dx = y_ref[...], dx_ref[...]
        r = _rstd(yv)
        yn = yv * r
        dg_ref[...] += jnp.sum(dx * yn, axis=0, keepdims=True)
        dyn = dx * g_ref[...]
        dy = (r * (dyn - yn * jnp.mean(dyn * yn, axis=-1, keepdims=True))).astype(BF16)
        dm_ref[...] = _dot(dy, w_ref[...], NT)
        dw_ref[0:256, :] += _dot(oa_ref[...].astype(BF16), dy, TN)
        dw_ref[256:512, :] += _dot(ob_ref[...].astype(BF16), dy, TN)
        dw_ref[512:1024, :] += _dot(oc_ref[...].astype(BF16), dy, TN)

    row = lambda w: pl.BlockSpec((tm, w), lambda i: (i, 0))
    fixed = lambda r, c: pl.BlockSpec((r, c), lambda i: (0, 0))
    return pl.pallas_call(
        body, grid=(n // tm,),
        in_specs=[row(d), row(d), row(256), row(256), row(512), fixed(d, d), fixed(1, d)],
        out_specs=[row(d), fixed(d, d), fixed(8, d)],
        out_shape=[jax.ShapeDtypeStruct((n, d), F32), jax.ShapeDtypeStruct((d, d), F32), jax.ShapeDtypeStruct((8, d), F32)],
        compiler_params=_cparams(("arbitrary",)), name=name)(dxo, y, oa, ob, oc, wo, g_row)


_PIECES = ((0, A_W), (A_W, B_W), (A_W + B_W, C_W), (A_W + B_W + C_W, F_W))


def _inproj_bwd_x(x2, dxo, g_row, w_int, pieces, name):
    n, d = x2.shape
    tm = min(256, n)

    def body(x_ref, dxo_ref, g_ref, w_ref, da_ref, db_ref, dc_ref, df_ref, dx_ref, dg_ref):
        @pl.when(pl.program_id(0) == 0)
        def _():
            dg_ref[...] = jnp.zeros((8, d), F32)

        dh = jnp.zeros((tm, d), F32)
        for ref, (o, w) in zip((da_ref, db_ref, dc_ref, df_ref), _PIECES):
            dh = dh + _dot(ref[...].astype(BF16), w_ref[:, o:o + w], NT)
        x = x_ref[...]
        r = _rstd(x)
        xn = x * r
        dg_ref[...] += jnp.sum(dh * xn, axis=0, keepdims=True)
        dxn = dh * g_ref[...]
        dx_ref[...] = dxo_ref[...] + r * (dxn - xn * jnp.mean(dxn * xn, axis=-1, keepdims=True))

    row = lambda w: pl.BlockSpec((tm, w), lambda i: (i, 0))
    fixed = lambda r, c: pl.BlockSpec((r, c), lambda i: (0, 0))
    return pl.pallas_call(
        body, grid=(n // tm,),
        in_specs=[row(d), row(d), fixed(1, d), fixed(d, E_INT)] + [row(w) for _, w in _PIECES],
        out_specs=[row(d), fixed(8, d)],
        out_shape=[jax.ShapeDtypeStruct((n, d), F32), jax.ShapeDtypeStruct((8, d), F32)],
        compiler_params=_cparams(("arbitrary",)), name=name)(x2, dxo, g_row, w_int, *pieces)


def _inproj_bwd_w(x2, g_row, piece, name):
    n, d = x2.shape
    w = piece.shape[1]
    tm = min(512, n)

    def body(x_ref, g_ref, dp_ref, dw_ref):
        @pl.when(pl.program_id(0) == 0)
        def _():
            dw_ref[...] = jnp.zeros((d, w), F32)

        x = x_ref[...]
        h = (x * _rstd(x) * g_ref[...]).astype(BF16)
        dw_ref[...] += _dot(h, dp_ref[...].astype(BF16), TN)

    return pl.pallas_call(
        body, grid=(n // tm,),
        in_specs=[pl.BlockSpec((tm, d), lambda i: (i, 0)), pl.BlockSpec((1, d), lambda i: (0, 0)),
                  pl.BlockSpec((tm, w), lambda i: (i, 0))],
        out_specs=pl.BlockSpec((d, w), lambda i: (0, 0)),
        out_shape=jax.ShapeDtypeStruct((d, w), F32),
        compiler_params=_cparams(("arbitrary",)), name=name)(x2, g_row, piece)


def _block_diag(pool_w_l):
    z = jnp.zeros((64, 64), pool_w_l.dtype)
    return jnp.concatenate(
        [jnp.concatenate([pool_w_l[g] if c == g else z for c in range(4)], axis=1) for g in range(4)], axis=0)


def _pad_lanes(v, width=128):
    return jnp.pad(v, ((0, 0),) * (v.ndim - 1) + ((0, width - v.shape[-1]),))


def _local_step(x, target, lower_bounds, pre_norm_g, w_in_int, hgrn_norm_g, fox_f_bias, pool_w, pool_scale,
                w_out_bf, post_norm_g):
    bsz, t, d = x.shape
    n = bsz * t
    lbs = _lbs_fwd(lower_bounds)
    saved = []
    xc = x.reshape(n, d)
    for l in range(DEPTH):
        proj = _inproj_fwd(xc, pre_norm_g[l:l + 1], w_in_int[l], f"inproj_fwd{l}").reshape(bsz, t, E_INT)
        wbd = _block_diag(pool_w[l]).astype(BF16)
        bias_row = _pad_lanes(fox_f_bias[l:l + 1])
        oa, oa_raw = _hgrn_fwd(proj, lbs[l:l + 1], hgrn_norm_g[l:l + 1], f"hgrn_fwd{l}")
        ob = _pool_fwd(proj, wbd, pool_scale[l:l + 1], f"pool_fwd{l}")
        c_nat, c_t = _foxgate_fwd(proj, bias_row, f"foxgate_fwd{l}")
        oc, oc_raw, lse = _fox_fwd(proj, c_nat, c_t, f"fox_fwd{l}")
        y, xn = _outproj_fwd(xc, oa.reshape(n, -1), ob.reshape(n, -1), oc.reshape(n, -1), w_out_bf[l],
                             post_norm_g[l:l + 1], f"outproj_fwd{l}")
        saved.append((xc, proj, wbd, bias_row, oa, oa_raw, ob, oc, oc_raw, lse, c_nat, c_t, y))
        xc = xn
    dx, sq = _loss_head(xc, target.reshape(n, d), "loss_head")
    g = {k: [None] * DEPTH for k in ("pre", "w_in", "hgn", "bias", "pool_w", "pool_scale", "w_out", "post", "lbs")}
    for l in reversed(range(DEPTH)):
        xin, proj, wbd, bias_row, oa, oa_raw, ob, oc, oc_raw, lse, c_nat, c_t, y = saved[l]
        dmix, g["w_out"][l], dpost = _outproj_bwd(dx, y, oa.reshape(n, -1), ob.reshape(n, -1), oc.reshape(n, -1),
                                                  w_out_bf[l], post_norm_g[l:l + 1], f"outproj_bwd{l}")
        g["post"][l] = dpost[0]
        dmix3 = dmix.reshape(bsz, t, d)
        d_c, dct, drow = _fox_bwd(proj, oc_raw, dmix3, lse, c_nat, c_t, f"fox_bwd{l}")
        dc_nat = _pad_lanes(dct[:, :, 0:2, :].reshape(bsz, FOX_HEADS, t).transpose(0, 2, 1)
                            + drow.reshape(bsz, t, FOX_HEADS, 64)[..., 0])
        d_f, dbias = _foxgate_bwd(proj, dc_nat, bias_row, f"foxgate_bwd{l}")
        g["bias"][l] = jnp.sum(dbias[:, 0, :FOX_HEADS], axis=0)
        d_b, dscale, dwbd = _pool_bwd(proj, dmix3, wbd, pool_scale[l:l + 1], f"pool_bwd{l}")
        g["pool_scale"][l] = jnp.sum(dscale[:, 0], axis=0)
        dwbd = jnp.sum(dwbd, axis=0)
        g["pool_w"][l] = jnp.stack([dwbd[64 * k:64 * (k + 1), 64 * k:64 * (k + 1)] for k in range(4)])
        d_a, dgn, dlb = _hgrn_bwd(proj, oa_raw, dmix3, lbs[l:l + 1], hgrn_norm_g[l:l + 1], f"hgrn_bwd{l}")
        g["hgn"][l] = jnp.sum(dgn[:, 0], axis=0)
        g["lbs"][l] = jnp.sum(dlb[:, 0], axis=0)
        pieces = [p.reshape(n, -1) for p in (d_a, d_b, d_c, d_f)]
        g["w_in"][l] = jnp.concatenate(
            [_inproj_bwd_w(xin, pre_norm_g[l:l + 1], p, f"inproj_bwd_w{l}_{k}") for k, p in enumerate(pieces)], axis=1)
        dx, dpre = _inproj_bwd_x(xin, dx, pre_norm_g[l:l + 1], w_in_int[l], pieces, f"inproj_bwd_x{l}")
        g["pre"][l] = dpre[0]
    grads = {k: jnp.stack(v) for k, v in g.items()}
    return sq, dx.reshape(bsz, t, d), grads


def _place():
    return lax.axis_index("x"), lax.axis_index("y"), lax.axis_index("c")


def _other_chips(x, y):
    return [(1 - x, y), (x, 1 - y), (1 - x, 1 - y)]


_ANY = pl.BlockSpec(memory_space=pl.ANY)


def _gather_weights(w_in_sh, w_out_sh):
    def body(win_ref, wout_ref, ain_ref, aout_ref, ici_send, ici_recv, d2d_send, d2d_recv, local_sems):
        x, y, c = _place()
        me = 2 * x + y
        pairs = ((win_ref, ain_ref), (wout_ref, aout_ref))
        mine = [pltpu.make_async_copy(src, dst.at[me], local_sems.at[j]) for j, (src, dst) in enumerate(pairs)]
        for cp in mine:
            cp.start()
        chips = _other_chips(x, y)
        sends = [pltpu.make_async_remote_copy(
            src_ref=src.at[c], dst_ref=dst.at[me, c], send_sem=ici_send.at[2 * k + j], recv_sem=ici_recv.at[2 * k + j],
            device_id=(px, py, c), device_id_type=MESH) for k, (px, py) in enumerate(chips) for j, (src, dst) in enumerate(pairs)]
        for cp in sends:
            cp.start()
        passed = [pltpu.make_async_remote_copy(
            src_ref=dst.at[2 * px + py, c], dst_ref=dst.at[2 * px + py, c], send_sem=d2d_send.at[2 * k + j],
            recv_sem=d2d_recv.at[2 * k + j], device_id=(x, y, 1 - c), device_id_type=MESH)
            for k, (px, py) in enumerate(chips) for j, (src, dst) in enumerate(pairs)]
        for n, (k, j) in enumerate((k, j) for k in range(3) for j in range(2)):
            px, py = chips[k]
            src, dst = pairs[j]
            pltpu.make_async_remote_copy(
                src_ref=src.at[c], dst_ref=dst.at[2 * px + py, c], send_sem=ici_send.at[n], recv_sem=ici_recv.at[n],
                device_id=(px, py, c), device_id_type=MESH).wait_recv()
            passed[n].start()
        for n, (k, j) in enumerate((k, j) for k in range(3) for j in range(2)):
            px, py = chips[k]
            src, dst = pairs[j]
            pltpu.make_async_remote_copy(
                src_ref=dst.at[2 * px + py, 1 - c], dst_ref=dst.at[2 * px + py, 1 - c], send_sem=d2d_send.at[n],
                recv_sem=d2d_recv.at[n], device_id=(x, y, 1 - c), device_id_type=MESH).wait_recv()
        for cp in sends + passed:
            cp.wait_send()
        for cp in mine:
            cp.wait()

    sems = pltpu.SemaphoreType.DMA((6,))
    return pl.pallas_call(
        body, in_specs=[_ANY, _ANY], out_specs=[_ANY, _ANY],
        out_shape=[jax.ShapeDtypeStruct((N_CHIPS,) + w_in_sh.shape, w_in_sh.dtype),
                   jax.ShapeDtypeStruct((N_CHIPS,) + w_out_sh.shape, w_out_sh.dtype)],
        scratch_shapes=[sems, sems, sems, sems, pltpu.SemaphoreType.DMA((2,))],
        name="gather_weights")(w_in_sh, w_out_sh)


def _swap_with_sibling(parts, name):
    k = len(parts)

    def body(*refs):
        src, dst = refs[:k], refs[k:2 * k]
        send_sems, recv_sems = refs[2 * k:]
        x, y, c = _place()
        cps = [pltpu.make_async_remote_copy(src_ref=src[j], dst_ref=dst[j], send_sem=send_sems.at[j], recv_sem=recv_sems.at[j],
                                            device_id=(x, y, 1 - c), device_id_type=MESH) for j in range(k)]
        for cp in cps:
            cp.start()
        for cp in cps:
            cp.wait()

    return pl.pallas_call(
        body, in_specs=[_ANY] * k, out_specs=[_ANY] * k,
        out_shape=[jax.ShapeDtypeStruct(p.shape, p.dtype) for p in parts],
        scratch_shapes=[pltpu.SemaphoreType.DMA((k,)), pltpu.SemaphoreType.DMA((k,))], name=name)(*parts)


def _scatter_to_chips(parts, name):
    k = len(parts)

    def body(*refs):
        src, dst = refs[:k], refs[k:2 * k]
        send_sems, recv_sems = refs[2 * k:]
        x, y, c = _place()
        me = 2 * x + y
        cps = []
        for rel, (px, py) in enumerate(_other_chips(x, y)):
            for j in range(k):
                cps.append(pltpu.make_async_remote_copy(
                    src_ref=src[j].at[2 * px + py], dst_ref=dst[j].at[rel], send_sem=send_sems.at[rel * k + j],
                    recv_sem=recv_sems.at[rel * k + j], device_id=(px, py, c), device_id_type=MESH))
        for cp in cps:
            cp.start()
        for cp in cps:
            cp.wait()
        del me

    return pl.pallas_call(
        body, in_specs=[_ANY] * k, out_specs=[_ANY] * k,
        out_shape=[jax.ShapeDtypeStruct((3,) + p.shape[1:], p.dtype) for p in parts],
        scratch_shapes=[pltpu.SemaphoreType.DMA((3 * k,)), pltpu.SemaphoreType.DMA((3 * k,))], name=name)(*parts)


def _add_n(parts, name, with_bf16=False):
    r, c = parts[0].shape
    tr = 256 if r % 256 == 0 else r
    n = len(parts)

    def body(*refs):
        acc = refs[0][...].astype(F32)
        for ref in refs[1:n]:
            acc = acc + ref[...].astype(F32)
        refs[n][...] = acc
        if with_bf16:
            refs[n + 1][...] = acc.astype(BF16)

    blk = pl.BlockSpec((tr, c), lambda i: (i, 0))
    outs = [jax.ShapeDtypeStruct((r, c), F32)] + ([jax.ShapeDtypeStruct((r, c), BF16)] if with_bf16 else [])
    res = pl.pallas_call(
        body, grid=(r // tr,), in_specs=[blk] * n, out_specs=[blk] * len(outs),
        out_shape=outs, compiler_params=_cparams(("parallel",)), name=name)(*parts)
    return res if with_bf16 else res[0]


def _all_reduce_small(packet):
    r, w = packet.shape

    def body(p_ref, o_ref, buf, send_sems, recv_sems):
        x, y, c = _place()
        me = 4 * x + 2 * y + c
        buf[me] = p_ref[...]
        peers = []
        for k in range(1, 8):
            fx, fy, fc = (k >> 2) & 1, (k >> 1) & 1, k & 1
            peers.append((x ^ fx, y ^ fy, c ^ fc))
        cps = [pltpu.make_async_remote_copy(src_ref=p_ref, dst_ref=buf.at[me], send_sem=send_sems.at[k], recv_sem=recv_sems.at[k],
                                            device_id=peer, device_id_type=MESH) for k, peer in enumerate(peers)]
        for cp in cps:
            cp.start()
        for k, (px, py, pc) in enumerate(peers):
            pltpu.make_async_remote_copy(src_ref=p_ref, dst_ref=buf.at[4 * px + 2 * py + pc], send_sem=send_sems.at[k],
                                         recv_sem=recv_sems.at[k], device_id=(px, py, pc), device_id_type=MESH).wait_recv()
        for cp in cps:
            cp.wait_send()
        acc = buf[0]
        for k in range(1, 8):
            acc = acc + buf[k]
        o_ref[...] = acc

    vm = pl.BlockSpec(memory_space=pltpu.VMEM)
    return pl.pallas_call(
        body, in_specs=[vm], out_specs=vm, out_shape=jax.ShapeDtypeStruct((r, w), F32),
        scratch_shapes=[pltpu.VMEM((8, r, w), F32), pltpu.SemaphoreType.DMA((7,)), pltpu.SemaphoreType.DMA((7,))],
        name="all_reduce_small")(packet)


def _adamw_math(w, g, m, v):
    m = ADAM_B1 * m + (1.0 - ADAM_B1) * g
    v = ADAM_B2 * v + (1.0 - ADAM_B2) * (g * g)
    m_hat = m / (1.0 - ADAM_B1 ** ADAM_STEP)
    v_hat = v / (1.0 - ADAM_B2 ** ADAM_STEP)
    return -ADAM_LR * (m_hat / (jnp.sqrt(v_hat) + ADAM_EPS) + ADAM_WD * w), m, v


def _adamw(w, g, m, v, name):
    nl, r, c = w.shape
    tr = 256 if r % 256 == 0 else r

    def body(w_ref, g_ref, m_ref, v_ref, d_ref, mo_ref, vo_ref):
        d_ref[...], mo_ref[...], vo_ref[...] = _adamw_math(w_ref[...], g_ref[...], m_ref[...], v_ref[...])

    blk = pl.BlockSpec((None, tr, c), lambda l, i: (l, i, 0))
    out = jax.ShapeDtypeStruct(w.shape, F32)
    return pl.pallas_call(
        body, grid=(nl, r // tr), in_specs=[blk] * 4, out_specs=[blk] * 3, out_shape=[out] * 3,
        compiler_params=_cparams(("parallel", "parallel")), name=name)(w, g, m, v)


def _small_update(gsum, lower_bounds, wpack, mpack, vpack):
    r, w = gsum.shape
    lb_rows = DEPTH * HGRN_W // 128

    def body(g_ref, a_ref, w_ref, m_ref, v_ref, go_ref, d_ref, mo_ref, vo_ref):
        a = a_ref[...]
        a0, a1 = a[0:1], a[1:2]
        mx = jnp.maximum(a0, a1)
        e0, e1 = jnp.exp(a0 - mx), jnp.exp(a1 - mx)
        p0, p1 = e0 / (e0 + e1), e1 / (e0 + e1)
        g = g_ref[...]
        half = lb_rows // 2
        dl0 = jnp.concatenate([g[k:k + 1] for k in range(half)], axis=1)
        dl1 = jnp.concatenate([g[half + k:half + k + 1] for k in range(half)], axis=1)
        dp0 = (dl0 + dl1) - (dl0 + dl1)
        dp1 = dl1
        inner = p0 * dp0 + p1 * dp1
        da0, da1 = p0 * (dp0 - inner), p1 * (dp1 - inner)
        rows = [da0[:, 128 * k:128 * (k + 1)] for k in range(half)] + [da1[:, 128 * k:128 * (k + 1)] for k in range(half)]
        gfull = jnp.concatenate(rows + [g[lb_rows:]], axis=0)
        go_ref[...] = gfull
        d_ref[...], mo_ref[...], vo_ref[...] = _adamw_math(w_ref[...], gfull, m_ref[...], v_ref[...])

    vm = pl.BlockSpec(memory_space=pltpu.VMEM)
    out = jax.ShapeDtypeStruct((r, w), F32)
    return pl.pallas_call(body, in_specs=[vm] * 5, out_specs=[vm] * 4, out_shape=[out] * 4, name="small_update")(
        gsum, lower_bounds, wpack, mpack, vpack)


_SMALL = ("lower_bounds", "pre_norm_g", "hgrn_norm_g", "fox_f_bias", "pool_w", "pool_scale", "post_norm_g")


def _pack(parts):
    rows = []
    for k in _SMALL:
        f = parts[k].reshape(-1)
        pad = (-f.shape[0]) % (8 * 128)
        rows.append(jnp.pad(f, (0, pad)).reshape(-1, 128))
    rows.append(jnp.zeros((8, 128), F32))
    return jnp.concatenate(rows, axis=0)


def _unpack(pack, like):
    out, r = {}, 0
    for k in _SMALL:
        size = int(np.prod(like[k].shape))
        nr = -(-size // (8 * 128)) * 8
        out[k] = pack[r:r + nr].reshape(-1)[:size].reshape(like[k].shape)
        r += nr
    return out, r


def kernel(x, lower_bounds, pre_norm_g, w_in, hgrn_norm_g, fox_f_bias, pool_w, pool_scale, w_out, post_norm_g, loss_target, m_lower_bounds, m_pre_norm_g, m_w_in, m_hgrn_norm_g, m_fox_f_bias, m_pool_w, m_pool_scale, m_w_out, m_post_norm_g, v_lower_bounds, v_pre_norm_g, v_w_in, v_hgrn_norm_g, v_fox_f_bias, v_pool_w, v_pool_scale, v_w_out, v_post_norm_g):
    cx, cy, cc = _place()
    chip = 2 * cx + cy

    ain, aout = _gather_weights(w_in.astype(BF16), w_out.astype(BF16))
    w_in_full = jnp.concatenate([ain[q] for q in range(N_CHIPS)], axis=-1)
    w_in_int = _to_internal(w_in_full)
    w_out_full = jnp.concatenate([aout[q] for q in range(N_CHIPS)], axis=1)

    sq, grad_x, g = _local_step(x, loss_target, lower_bounds, pre_norm_g, w_in_int, hgrn_norm_g, fox_f_bias, pool_w,
                                pool_scale, w_out_full, post_norm_g)

    gin = _to_original(g["w_in"])
    gin_blocks = jnp.stack([gin[:, :, SHARD_W * q:SHARD_W * (q + 1)] for q in range(N_CHIPS)])
    gout_blocks = g["w_out"].reshape(DEPTH, N_CHIPS, 256, D_MODEL).transpose(1, 0, 2, 3)
    take = lambda a, l: lax.dynamic_index_in_dim(a, l, axis=1, keepdims=False)
    mine_in, mine_out = take(gin_blocks, cc), take(gout_blocks, cc)
    sib_in, sib_out = _swap_with_sibling([take(gin_blocks, 1 - cc), take(gout_blocks, 1 - cc)], "grad_swap1")
    rin, rout = 4 * 1024, 4 * 256
    sum_in, send_in = _add_n([mine_in.reshape(rin, SHARD_W), sib_in.reshape(rin, SHARD_W)], "grad_add1_in", True)
    sum_out, send_out = _add_n([mine_out.reshape(rout, D_MODEL), sib_out.reshape(rout, D_MODEL)], "grad_add1_out", True)
    sum_in, sum_out = sum_in.reshape(4, 1024, SHARD_W), sum_out.reshape(4, 256, D_MODEL)
    got_in, got_out = _scatter_to_chips([send_in.reshape(4, 1024, SHARD_W), send_out.reshape(4, 256, D_MODEL)], "grad_scatter")
    own = lambda a: lax.dynamic_index_in_dim(a, chip, axis=0, keepdims=False)
    half_in = _add_n([own(sum_in)] + [got_in[k] for k in range(3)], "grad_add2_in")
    half_out = _add_n([own(sum_out)] + [got_out[k] for k in range(3)], "grad_add2_out")
    oth_in, oth_out = _swap_with_sibling([half_in, half_out], "grad_swap2")
    first = cc == 0
    grad_w_in = jnp.stack([jnp.where(first, half_in, oth_in), jnp.where(first, oth_in, half_in)])
    grad_w_out = jnp.stack([jnp.where(first, half_out, oth_out), jnp.where(first, oth_out, half_out)])

    small = {"lower_bounds": g["lbs"], "pre_norm_g": g["pre"], "hgrn_norm_g": g["hgn"], "fox_f_bias": g["bias"],
             "pool_w": g["pool_w"], "pool_scale": g["pool_scale"], "post_norm_g": g["post"]}
    packet = _pack(small)
    nrows = packet.shape[0]
    packet = packet.at[nrows - 1].set(sq[0])
    gsum = _all_reduce_small(packet)
    loss = gsum[nrows - 1, 0] * (0.5 / D_MODEL)

    weights = {"lower_bounds": lower_bounds, "pre_norm_g": pre_norm_g, "hgrn_norm_g": hgrn_norm_g,
               "fox_f_bias": fox_f_bias, "pool_w": pool_w, "pool_scale": pool_scale, "post_norm_g": post_norm_g}
    moments_m = {"lower_bounds": m_lower_bounds, "pre_norm_g": m_pre_norm_g, "hgrn_norm_g": m_hgrn_norm_g,
                 "fox_f_bias": m_fox_f_bias, "pool_w": m_pool_w, "pool_scale": m_pool_scale, "post_norm_g": m_post_norm_g}
    moments_v = {"lower_bounds": v_lower_bounds, "pre_norm_g": v_pre_norm_g, "hgrn_norm_g": v_hgrn_norm_g,
                 "fox_f_bias": v_fox_f_bias, "pool_w": v_pool_w, "pool_scale": v_pool_scale, "post_norm_g": v_post_norm_g}
    gp, dp, mp, vp = _small_update(gsum, lower_bounds, _pack(weights), _pack(moments_m), _pack(moments_v))
    gs, _ = _unpack(gp, weights)
    ds, _ = _unpack(dp, weights)
    ms, _ = _unpack(mp, weights)
    vs, _ = _unpack(vp, weights)

    d_in, m_in, v_in = _adamw(w_in, grad_w_in, m_w_in, v_w_in, "adamw_w_in")
    d_out, m_out, v_out = _adamw(w_out, grad_w_out, m_w_out, v_w_out, "adamw_w_out")

    def ordered(s, big_in, big_out):
        return (s["lower_bounds"], s["pre_norm_g"], big_in, s["hgrn_norm_g"], s["fox_f_bias"], s["pool_w"],
                s["pool_scale"], big_out, s["post_norm_g"])

    return (loss, grad_x, *ordered(gs, grad_w_in, grad_w_out), *ordered(ds, d_in, d_out),
            *ordered(ms, m_in, m_out), *ordered(vs, v_in, v_out))
```

```python
import functools

import numpy as np
import jax
import jax.numpy as jnp
from jax import lax
from jax.experimental import pallas as pl
from jax.experimental.pallas import tpu as pltpu

F32 = jnp.float32
BF16 = jnp.bfloat16
HI = lax.Precision.HIGHEST
MESH = pl.DeviceIdType.MESH

NORM_EPS = 1e-6
MASK_VALUE = -1e30
TINY = 1e-30
ADAM_LR, ADAM_B1, ADAM_B2, ADAM_EPS, ADAM_WD, ADAM_STEP = 0.001, 0.9, 0.999, 1e-08, 0.01, 10

D_MODEL = 1024
DEPTH = 2
N_CHIPS = 4
CHUNK = 64
LANES = 128
HGRN_W, POOL_W, FOX_W, FOX_HEADS = 256, 256, 512, 8
POOL_WINDOWS = (2, 4, 8, 16)
POOL_HALO = 16
IN_WIDTH = 3592
SHARD_W = IN_WIDTH // N_CHIPS
A_W, B_W, C_W, F_W = 1024, 512, 2048, 128
E_INT = A_W + B_W + C_W + F_W
B_BLK = A_W // 512
C_BLK0 = (A_W + B_W) // 512
F_BLK = (A_W + B_W + C_W) // 128


def _segments():
    segs = []
    for hp in range(2):
        for part in range(4):
            segs.append((part * 256 + hp * 128, 128))
    segs.append((1024, 256))
    segs.append((1280, 256))
    for hp in range(4):
        for part in range(4):
            segs.append((1536 + part * 512 + hp * 128, 128))
    segs.append((3584, 8))
    return segs


_SEGS = _segments()


def _to_internal(w):
    parts = [w[..., s:s + n] for s, n in _SEGS]
    parts.append(jnp.zeros(w.shape[:-1] + (E_INT - IN_WIDTH,), w.dtype))
    return jnp.concatenate(parts, axis=-1)


def _to_original(w):
    offs, o = [], 0
    for s, n in _SEGS:
        offs.append((s, o, n))
        o += n
    parts = [w[..., o:o + n] for s, o, n in sorted(offs)]
    return jnp.concatenate(parts, axis=-1)


def _cparams(sem=None, vmem_mb=48):
    kw = dict(vmem_limit_bytes=vmem_mb * 1024 * 1024)
    if sem is not None:
        kw["dimension_semantics"] = sem
    return pltpu.CompilerParams(**kw)


def _sig(x):
    return 1.0 / (1.0 + jnp.exp(-x))


def _silu(x):
    return x * _sig(x)


def _dsilu(x):
    s = _sig(x)
    return s * (1.0 + x * (1.0 - s))


def _rstd(x):
    return lax.rsqrt(jnp.mean(x * x, axis=-1, keepdims=True) + NORM_EPS)


def _dot(a, b, dims, **kw):
    return lax.dot_general(a, b, (dims, ((), ())), preferred_element_type=F32, **kw)


NN = ((1,), (0,))
NT = ((1,), (1,))
TN = ((0,), (0,))


def _iota(shape, dim):
    return lax.broadcasted_iota(jnp.int32, shape, dim)


def _lbs_fwd(lower_bounds):
    def body(a_ref, o_ref):
        a = a_ref[...]
        a0, a1 = a[0:1], a[1:2]
        m = jnp.maximum(a0, a1)
        e0, e1 = jnp.exp(a0 - m), jnp.exp(a1 - m)
        p0, p1 = e0 / (e0 + e1), e1 / (e0 + e1)
        o_ref[...] = jnp.concatenate([p0 - p0, (p0 + p1) - p0], axis=0)

    return pl.pallas_call(body, out_shape=jax.ShapeDtypeStruct(lower_bounds.shape, F32), name="lbs_fwd")(lower_bounds)


def _inproj_fwd(x2, g_row, w_int, name):
    n, d = x2.shape
    e = w_int.shape[1]
    tm = min(256, n)

    def body(x_ref, g_ref, w_ref, o_ref):
        x = x_ref[...]
        h = (x * _rstd(x) * g_ref[...]).astype(BF16)
        o_ref[...] = jnp.dot(h, w_ref[...], preferred_element_type=F32)

    return pl.pallas_call(
        body, grid=(n // tm,),
        in_specs=[pl.BlockSpec((tm, d), lambda i: (i, 0)), pl.BlockSpec((1, d), lambda i: (0, 0)),
                  pl.BlockSpec((d, e), lambda i: (0, 0))],
        out_specs=pl.BlockSpec((tm, e), lambda i: (i, 0)),
        out_shape=jax.ShapeDtypeStruct((n, e), F32),
        compiler_params=_cparams(("parallel",)), name=name)(x2, g_row, w_int)


def _chunk_cumsum_matrix():
    i, j = _iota((LANES, LANES), 0), _iota((LANES, LANES), 1)
    return ((i <= j) & ((i // CHUNK) == (j // CHUNK))).astype(F32)


def _hgrn_gates(a, lb):
    qa, z = a[:, 0:128], a[:, 128:256]
    sg, sgn = _sig(z), _sig(-z)
    fg = lb + (1.0 - lb) * sg
    lf = jnp.log(jnp.maximum(fg, TINY))
    kk = (1.0 - lb) * sgn
    return qa * _sig(qa), kk, lf, sg, sgn, fg


def _hgrn_fwd(proj3, lbs_row, gn_col, name):
    bsz, t, _ = proj3.shape
    nt = t // LANES

    def body(a_ref, lb_ref, gn_ref, og_ref, or_ref):
        lb = lb_ref[...]
        gn = gn_ref[...]
        umat = _chunk_cumsum_matrix()
        lane64 = _iota((1, LANES), 1) % CHUNK

        def tile(i, carry):
            r0 = pl.multiple_of(i * LANES, LANES)
            a = a_ref[pl.ds(r0, LANES), :]
            qq, kk, lf, _, _, _ = _hgrn_gates(a, lb)
            va, ga = a[:, 256:384], a[:, 384:512]
            q_t, k_t, v_t = qq.T, kk.T, va.T
            b_t = jnp.dot(lf.T, umat, precision=HI, preferred_element_type=F32)
            new_s, o_heads = [], []
            for h in range(2):
                s_h = carry[h]
                rs = slice(CHUNK * h, CHUNK * (h + 1))
                qh, kh, vh, bh = q_t[rs], k_t[rs], v_t[rs], b_t[rs]
                inter = []
                for c in range(2):
                    cs = slice(CHUNK * c, CHUNK * (c + 1))
                    b_ = bh[:, cs]
                    qt = (qh[:, cs] * jnp.exp(b_)).astype(BF16)
                    inter.append(_dot(s_h.astype(BF16), qt, TN))
                    bl = b_[:, CHUNK - 1:CHUNK]
                    kt = (kh[:, cs] * jnp.exp(bl - b_)).astype(BF16)
                    s_h = jnp.exp(bl) * s_h + _dot(kt, vh[:, cs].astype(BF16), NT)
                new_s.append(s_h)

                acc = jnp.concatenate(inter, axis=1) + jnp.sum(qh * kh, axis=0, keepdims=True) * vh
                for dlt in range(1, CHUNK):
                    kr, br, vr = pltpu.roll(kh, dlt, 1), pltpu.roll(bh, dlt, 1), pltpu.roll(vh, dlt, 1)
                    e = jnp.exp(jnp.minimum(bh - br, 0.0))
                    att = jnp.sum(qh * kr * e, axis=0, keepdims=True)
                    acc = acc + jnp.where(lane64 >= dlt, att, 0.0) * vr
                o_heads.append(acc)
            normed = []
            for h in range(2):
                o_h = o_heads[h]
                ms = jnp.mean(o_h * o_h, axis=0, keepdims=True)
                normed.append(o_h * lax.rsqrt(ms + NORM_EPS) * gn[CHUNK * h:CHUNK * (h + 1)])
            or_ref[pl.ds(r0, LANES), :] = jnp.concatenate(o_heads, axis=0).T
            og_ref[pl.ds(r0, LANES), :] = jnp.concatenate(normed, axis=0).T * _silu(ga)
            return tuple(new_s)

        zero = jnp.zeros((CHUNK, CHUNK), F32)
        lax.fori_loop(0, nt, tile, (zero, zero))

    out = jax.ShapeDtypeStruct((bsz, t, HGRN_W), F32)
    return pl.pallas_call(
        body, grid=(bsz, 2),
        in_specs=[pl.BlockSpec((None, t, 512), lambda b, p: (b, 0, p)),
                  pl.BlockSpec((1, 128), lambda b, p: (0, p)),
                  pl.BlockSpec((128, 1), lambda b, p: (p, 0))],
        out_specs=[pl.BlockSpec((None, t, 128), lambda b, p: (b, 0, p)),
                   pl.BlockSpec((None, t, 128), lambda b, p: (b, 0, p))],
        out_shape=[out, out],
        compiler_params=_cparams(("parallel", "parallel")), name=name)(proj3, lbs_row, gn_col)


def _hgrn_bwd(proj3, o_raw, dmixed, lbs_row, gn_row, name):
    bsz, t, _ = proj3.shape
    nt = t // LANES
    nchunk = t // CHUNK

    def body(a_ref, or_ref, do_ref, lb_ref, gn_ref, da_ref, dgn_ref, dlb_ref, s_sc):
        lb = lb_ref[...]
        gn = gn_ref[...]
        umat = _chunk_cumsum_matrix()
        lane = _iota((1, LANES), 1)
        lane64 = lane % CHUNK
        half = lane < CHUNK

        def t_layout(a):
            qq, kk, lf, sg, sgn, fg = _hgrn_gates(a, lb)
            b_t = jnp.dot(lf.T, umat, precision=HI, preferred_element_type=F32)
            return qq.T, kk.T, a[:, 256:384].T, b_t, (sg, sgn, fg)

        def fwd_tile(i, carry):
            r0 = pl.multiple_of(i * LANES, LANES)
            q_t, k_t, v_t, b_t, _ = t_layout(a_ref[pl.ds(r0, LANES), :])
            new_s = []
            for h in range(2):
                s_h = carry[h]
                rs = slice(CHUNK * h, CHUNK * (h + 1))
                for c in range(2):
                    cs = slice(CHUNK * c, CHUNK * (c + 1))
                    s_sc[h, 2 * i + c] = s_h
                    b_ = b_t[rs, cs]
                    bl = b_[:, CHUNK - 1:CHUNK]
                    kt = (k_t[rs, cs] * jnp.exp(bl - b_)).astype(BF16)
                    s_h = jnp.exp(bl) * s_h + _dot(kt, v_t[rs, cs].astype(BF16), NT)
                new_s.append(s_h)
            return tuple(new_s)

        zero = jnp.zeros((CHUNK, CHUNK), F32)
        lax.fori_loop(0, nt, fwd_tile, (zero, zero))

        def half_mean(v):
            m0 = jnp.sum(jnp.where(half, v, 0.0), axis=1, keepdims=True) * (1.0 / CHUNK)
            m1 = jnp.sum(jnp.where(half, 0.0, v), axis=1, keepdims=True) * (1.0 / CHUNK)
            return jnp.where(half, m0, m1)

        def bwd_tile(k, carry):
            ds0, ds1, dgn_acc, dlb_acc = carry
            i = nt - 1 - k
            r0 = pl.multiple_of(i * LANES, LANES)
            a = a_ref[pl.ds(r0, LANES), :]
            qa, z, ga = a[:, 0:128], a[:, 128:256], a[:, 384:512]
            q_t, k_t, v_t, b_t, (sg, sgn, fg) = t_layout(a)
            oraw = or_ref[pl.ds(r0, LANES), :]
            dout = do_ref[pl.ds(r0, LANES), :]
            r = lax.rsqrt(half_mean(oraw * oraw) + NORM_EPS)
            xn = oraw * r
            dga = dout * (xn * gn) * _dsilu(ga)
            don = dout * _silu(ga)
            dgn_acc = dgn_acc + jnp.sum(don * xn, axis=0, keepdims=True)
            dxn = don * gn
            do_t = (r * (dxn - xn * half_mean(dxn * xn))).T
            new_ds, dq_h, dk_h, dv_h, db_h = [], [], [], [], []
            for h in range(2):
                ds_h = (ds0, ds1)[h]
                rs = slice(CHUNK * h, CHUNK * (h + 1))
                qh, kh, vh, bh, doh = q_t[rs], k_t[rs], v_t[rs], b_t[rs], do_t[rs]
                dq_c, dk_c, dv_c, dbl_c = [None, None], [None, None], [None, None], [None, None]
                for c in (1, 0):
                    cs = slice(CHUNK * c, CHUNK * (c + 1))
                    s_n = s_sc[h, 2 * i + c]
                    b_ = bh[:, cs]
                    eb = jnp.exp(b_)
                    bl = b_[:, CHUNK - 1:CHUNK]
                    ek = jnp.exp(bl - b_)
                    ebl = jnp.exp(bl)
                    qt, kt = qh[:, cs] * eb, kh[:, cs] * ek
                    do_c = doh[:, cs].astype(BF16)
                    dsb = ds_h.astype(BF16)
                    dv_c[c] = _dot(dsb, kt.astype(BF16), TN)
                    dkt = _dot(dsb, vh[:, cs].astype(BF16), NN)
                    dqt = _dot(s_n.astype(BF16), do_c, NN)
                    dbl_c[c] = jnp.sum(ds_h * s_n, axis=1, keepdims=True) * ebl + jnp.sum(dkt * kt, axis=1, keepdims=True)
                    dq_c[c], dk_c[c] = dqt * eb, dkt * ek
                    ds_h = ebl * ds_h + _dot(qt.astype(BF16), do_c, NT)
                new_ds.append(ds_h)

                att0 = jnp.sum(qh * kh, axis=0, keepdims=True)
                datt0 = jnp.sum(doh * vh, axis=0, keepdims=True)
                dqh = jnp.concatenate(dq_c, axis=1) + datt0 * kh
                dkh = jnp.concatenate(dk_c, axis=1) + datt0 * qh
                dvh = jnp.concatenate(dv_c, axis=1) + att0 * doh
                for dlt in range(1, CHUNK):
                    kr, br, vr = pltpu.roll(kh, dlt, 1), pltpu.roll(bh, dlt, 1), pltpu.roll(vh, dlt, 1)
                    e = jnp.where(lane64 >= dlt, jnp.exp(jnp.minimum(bh - br, 0.0)), 0.0)
                    qe = qh * e
                    att = jnp.sum(qe * kr, axis=0, keepdims=True)
                    datt = jnp.sum(doh * vr, axis=0, keepdims=True)
                    dqh = dqh + datt * (kr * e)
                    dkh = dkh + pltpu.roll(datt * qe, LANES - dlt, 1)
                    dvh = dvh + pltpu.roll(att * doh, LANES - dlt, 1)
                dbl = jnp.where(half, dbl_c[0], dbl_c[1])
                db_h.append(qh * dqh - kh * dkh + jnp.where(lane64 == CHUNK - 1, dbl, 0.0))
                dq_h.append(dqh)
                dk_h.append(dkh)
                dv_h.append(dvh)
            dqq = jnp.concatenate(dq_h, axis=0).T
            dkk = jnp.concatenate(dk_h, axis=0).T
            dvv = jnp.concatenate(dv_h, axis=0).T
            dlf = _dot(jnp.concatenate(db_h, axis=0), umat, NT, precision=HI).T
            dqa = dqq * _dsilu(qa)
            dfg = jnp.where(fg > TINY, dlf / fg, 0.0)
            dz = (dfg - dkk) * (1.0 - lb) * sg * sgn
            dlb_acc = dlb_acc + jnp.sum(dfg * (1.0 - sg) - dkk * sgn, axis=0, keepdims=True)
            da_ref[pl.ds(r0, LANES), :] = jnp.concatenate([dqa, dz, dvv, dga], axis=1)
            return new_ds[0], new_ds[1], dgn_acc, dlb_acc

        zrow = jnp.zeros((1, LANES), F32)
        _, _, dgn_acc, dlb_acc = lax.fori_loop(0, nt, bwd_tile, (zero, zero, zrow, zrow))
        dgn_ref[...] = jnp.broadcast_to(dgn_acc, (8, LANES))
        dlb_ref[...] = jnp.broadcast_to(dlb_acc, (8, LANES))

    rows = jax.ShapeDtypeStruct((bsz, 8, HGRN_W), F32)
    return pl.pallas_call(
        body, grid=(bsz, 2),
        in_specs=[pl.BlockSpec((None, t, 512), lambda b, p: (b, 0, p)),
                  pl.BlockSpec((None, t, 128), lambda b, p: (b, 0, p)),
                  pl.BlockSpec((None, t, 128), lambda b, p: (b, 0, p)),
                  pl.BlockSpec((1, 128), lambda b, p: (0, p)),
                  pl.BlockSpec((1, 128), lambda b, p: (0, p))],
        out_specs=[pl.BlockSpec((None, t, 512), lambda b, p: (b, 0, p)),
                   pl.BlockSpec((None, 8, 128), lambda b, p: (b, 0, p)),
                   pl.BlockSpec((None, 8, 128), lambda b, p: (b, 0, p))],
        out_shape=[jax.ShapeDtypeStruct((bsz, t, A_W), F32), rows, rows],
        scratch_shapes=[pltpu.VMEM((2, nchunk, CHUNK, CHUNK), F32)],
        compiler_params=_cparams(("parallel", "parallel")), name=name)(proj3, o_raw, dmixed, lbs_row, gn_row)


N_LEVELS = 6


def _hgrn_tables():
    t = np.arange(LANES)
    j = np.arange(LANES)[None, :]
    same_chunk = (t[:, None] // CHUNK) == (j // CHUNK)
    w = np.zeros((2 + N_LEVELS, LANES, LANES), np.float32)
    w[0] = same_chunk & (j <= t[:, None])
    w[1] = same_chunk & (j > t[:, None])
    maskf = np.zeros((N_LEVELS, LANES, LANES), np.float32)
    rightf = np.zeros((N_LEVELS, LANES, LANES), np.float32)
    for li in range(N_LEVELS):
        m = (CHUNK // 2) >> li
        start = t - (t % (2 * m))
        right = (t % (2 * m)) >= m
        first = np.where(right, start + m, t + 1)
        last = np.where(right, t, start + m - 1)
        w[2 + li] = (j >= first[:, None]) & (j <= last[:, None])
        maskf[li] = (t[:, None] // (2 * m)) == (j // (2 * m))
        rightf[li] = right[:, None]
    return jnp.asarray(w.reshape(-1, LANES), BF16), jnp.asarray(maskf), jnp.asarray(rightf)


def _split(x, n):
    parts = []
    for _ in range(n - 1):
        p = x.astype(BF16)
        parts.append(p)
        x = x - p.astype(F32)
    parts.append(x.astype(BF16))
    return parts


def _exact_dot(w, parts):
    acc = jnp.dot(w, parts[0], preferred_element_type=F32)
    for p in parts[1:]:
        acc = acc + jnp.dot(w, p, preferred_element_type=F32)
    return acc


def _head_sums(v, ones_blk, n=2):
    parts = _split(v, n)
    acc = jnp.dot(parts[0], ones_blk, preferred_element_type=F32)
    for p in parts[1:]:
        acc = acc + jnp.dot(p, ones_blk, preferred_element_type=F32)
    return acc


def _hgrn_consts():
    r, c = _iota((LANES, LANES), 0), _iota((LANES, LANES), 1)
    eye = r == c
    ones_blk = ((r // CHUNK) == (c // CHUNK)).astype(BF16)
    return eye, ones_blk, jnp.ones((CHUNK, LANES), BF16)


def _hgrn_levels(qq, kk, zall, mk_ref, rt_ref, d_att=None):
    att = [jnp.zeros((LANES, LANES), F32)] * 2
    dq = dk = db = jnp.zeros((LANES, LANES), F32)
    for li in range(N_LEVELS):
        e = jnp.exp(zall[(2 + li) * LANES:(3 + li) * LANES])
        rt = rt_ref[li]
        mk = mk_ref[li]
        qef, kef = e * rt, e * (1.0 - rt)
        qe, ke = (qq * qef).astype(BF16), (kk * kef).astype(BF16)
        dqs, dks = [], []
        for h in range(2):
            hs = slice(CHUNK * h, CHUNK * (h + 1))
            att[h] = att[h] + _dot(qe[:, hs], ke[:, hs], NT) * mk
            if d_att is not None:
                dam = (d_att[h] * mk).astype(BF16)
                dqs.append(jnp.dot(dam, ke[:, hs], preferred_element_type=F32))
                dks.append(_dot(dam, qe[:, hs], TN))
        if d_att is not None:
            dqe, dke = jnp.concatenate(dqs, axis=1), jnp.concatenate(dks, axis=1)
            dq = dq + dqe * qef
            dk = dk + dke * kef
            db = db + (dqe * qe.astype(F32) - dke * ke.astype(F32))
    return att, dq, dk, db


def _hgrn_fwd(proj3, lbs_row, gn_row, name):
    bsz, t, _ = proj3.shape
    nt = t // LANES
    w_all, maskf, rightf = _hgrn_tables()

    def body(a_ref, lb_ref, gn_ref, w_ref, mk_ref, rt_ref, og_ref, or_ref):
        lb = lb_ref[...]
        gn = gn_ref[...]
        eye, ones_blk, ones_h = _hgrn_consts()

        def tile(i, carry):
            r0 = pl.multiple_of(i * LANES, LANES)
            a = a_ref[pl.ds(r0, LANES), :]
            qq, kk, lf, _, _, _ = _hgrn_gates(a, lb)
            va, ga = a[:, 256:384], a[:, 384:512]
            parts = _split(lf, 3)
            zall = _exact_dot(w_ref[...], parts)
            eb, ee = jnp.exp(zall[0:LANES]), jnp.exp(zall[LANES:2 * LANES])
            vb = va.astype(BF16)
            att, _, _, _ = _hgrn_levels(qq, kk, zall, mk_ref, rt_ref)
            qk = _split(qq * kk, 2)
            qeb, keb = (qq * eb).astype(BF16), (kk * ee).astype(BF16)
            new_s, o_heads = [], []
            for h in range(2):
                hs = slice(CHUNK * h, CHUNK * (h + 1))
                diag = _exact_dot_r(qk, hs, ones_h)
                a_h = att[h] + jnp.where(eye, diag, 0.0)
                o_h = jnp.dot(a_h.astype(BF16), vb[:, hs], preferred_element_type=F32)
                st = carry[h]
                chunks = []
                for c in range(2):
                    rc = slice(CHUNK * c, CHUNK * (c + 1))
                    chunks.append(o_h[rc] + _dot(qeb[rc, hs], st.astype(BF16), NT))
                    ebl = eb[CHUNK * (c + 1) - 1:CHUNK * (c + 1), hs]
                    st = st * ebl + _dot(vb[rc, hs], keb[rc, hs], TN)
                new_s.append(st)
                o_heads.append(jnp.concatenate(chunks, axis=0))
            o = jnp.concatenate(o_heads, axis=1)
            ms = _head_sums(o * o, ones_blk) * (1.0 / CHUNK)
            or_ref[pl.ds(r0, LANES), :] = o
            og_ref[pl.ds(r0, LANES), :] = o * lax.rsqrt(ms + NORM_EPS) * gn * _silu(ga)
            return tuple(new_s)

        zero = jnp.zeros((CHUNK, CHUNK), F32)
        lax.fori_loop(0, nt, tile, (zero, zero))

    out = jax.ShapeDtypeStruct((bsz, t, HGRN_W), F32)
    row = pl.BlockSpec((1, 128), lambda b, p: (0, p))
    return pl.pallas_call(
        body, grid=(bsz, 2),
        in_specs=[pl.BlockSpec((None, t, 512), lambda b, p: (b, 0, p)), row, row,
                  pl.BlockSpec(w_all.shape, lambda b, p: (0, 0)),
                  pl.BlockSpec(maskf.shape, lambda b, p: (0, 0, 0)),
                  pl.BlockSpec(rightf.shape, lambda b, p: (0, 0, 0))],
        out_specs=[pl.BlockSpec((None, t, 128), lambda b, p: (b, 0, p)),
                   pl.BlockSpec((None, t, 128), lambda b, p: (b, 0, p))],
        out_shape=[out, out],
        compiler_params=_cparams(("parallel", "parallel")), name=name)(proj3, lbs_row, gn_row, w_all, maskf, rightf)


def _exact_dot_r(parts, hs, ones_h):
    acc = jnp.dot(parts[0][:, hs], ones_h, preferred_element_type=F32)
    for p in parts[1:]:
        acc = acc + jnp.dot(p[:, hs], ones_h, preferred_element_type=F32)
    return acc


def _hgrn_bwd(proj3, o_raw, dmixed, lbs_row, gn_row, name):
    bsz, t, _ = proj3.shape
    nt = t // LANES
    nchunk = t // CHUNK
    w_all, maskf, rightf = _hgrn_tables()

    def body(a_ref, or_ref, do_ref, lb_ref, gn_ref, w_ref, mk_ref, rt_ref, da_ref, dgn_ref, dlb_ref, s_sc):
        lb = lb_ref[...]
        gn = gn_ref[...]
        eye, ones_blk, ones_h = _hgrn_consts()
        r_i, c_i = _iota((LANES, LANES), 0), _iota((LANES, LANES), 1)
        suffix = ((c_i >= r_i) & ((r_i // CHUNK) == (c_i // CHUNK))).astype(BF16)
        row64 = _iota((LANES, CHUNK), 0)
        ones_t = jnp.ones((LANES, CHUNK), BF16)

        def fwd_tile(i, carry):
            r0 = pl.multiple_of(i * LANES, LANES)
            a = a_ref[pl.ds(r0, LANES), :]
            _, kk, lf, _, _, _ = _hgrn_gates(a, lb)
            parts = _split(lf, 3)
            zbe = _exact_dot(w_ref[0:2 * LANES, :], parts)
            eb, ee = jnp.exp(zbe[0:LANES]), jnp.exp(zbe[LANES:2 * LANES])
            vb, keb = a[:, 256:384].astype(BF16), (kk * ee).astype(BF16)
            new_s = []
            for h in range(2):
                hs = slice(CHUNK * h, CHUNK * (h + 1))
                st = carry[h]
                for c in range(2):
                    rc = slice(CHUNK * c, CHUNK * (c + 1))
                    s_sc[h, 2 * i + c] = st
                    st = st * eb[CHUNK * (c + 1) - 1:CHUNK * (c + 1), hs] + _dot(vb[rc, hs], keb[rc, hs], TN)
                new_s.append(st)
            return tuple(new_s)

        zero = jnp.zeros((CHUNK, CHUNK), F32)
        lax.fori_loop(0, nt, fwd_tile, (zero, zero))

        def bwd_tile(k, carry):
            dst0, dst1, dgn_acc, dlb_acc = carry
            i = nt - 1 - k
            r0 = pl.multiple_of(i * LANES, LANES)
            a = a_ref[pl.ds(r0, LANES), :]
            qa, ga = a[:, 0:128], a[:, 384:512]
            qq, kk, lf, sg, sgn, fg = _hgrn_gates(a, lb)
            parts = _split(lf, 3)
            zall = _exact_dot(w_ref[...], parts)
            eb, ee = jnp.exp(zall[0:LANES]), jnp.exp(zall[LANES:2 * LANES])
            vb = a[:, 256:384].astype(BF16)
            oraw = or_ref[pl.ds(r0, LANES), :]
            dout = do_ref[pl.ds(r0, LANES), :]
            r = lax.rsqrt(_head_sums(oraw * oraw, ones_blk) * (1.0 / CHUNK) + NORM_EPS)
            xn = oraw * r
            dga = dout * (xn * gn) * _dsilu(ga)
            don = dout * _silu(ga)
            dgn_acc = dgn_acc + jnp.sum(don * xn, axis=0, keepdims=True)
            dxn = don * gn
            do = r * (dxn - xn * (_head_sums(dxn * xn, ones_blk) * (1.0 / CHUNK)))
            dob = do.astype(BF16)
            d_att = [_dot(dob[:, CHUNK * h:CHUNK * (h + 1)], vb[:, CHUNK * h:CHUNK * (h + 1)], NT) for h in range(2)]
            att, dq, dk, db_lv = _hgrn_levels(qq, kk, zall, mk_ref, rt_ref, d_att)
            qk = _split(qq * kk, 2)
            qe_f, ke_f = qq * eb, kk * ee
            qeb, keb = qe_f.astype(BF16), ke_f.astype(BF16)
            new_ds, dq_h, dk_h, dv_h, dbl_h = [], [], [], [], []
            for h in range(2):
                hs = slice(CHUNK * h, CHUNK * (h + 1))
                a_h = att[h] + jnp.where(eye, _exact_dot_r(qk, hs, ones_h), 0.0)
                dv = _dot(a_h.astype(BF16), dob[:, hs], TN)
                ddiag = _exact_dot_r(_split(jnp.where(eye, d_att[h], 0.0), 2), slice(None), ones_t)
                dq_i = dq[:, hs] + ddiag * kk[:, hs]
                dk_i = dk[:, hs] + ddiag * qq[:, hs]
                dst = (dst0, dst1)[h]
                dq_c, dk_c, dv_c, dbl_c = [None, None], [None, None], [None, None], [None, None]
                for c in (1, 0):
                    rc = slice(CHUNK * c, CHUNK * (c + 1))
                    st_n = s_sc[h, 2 * i + c]
                    ebl = eb[CHUNK * (c + 1) - 1:CHUNK * (c + 1), hs]
                    dstb = dst.astype(BF16)
                    dv_c[c] = _dot(keb[rc, hs], dstb, NT)
                    dke = jnp.dot(vb[rc, hs], dstb, preferred_element_type=F32)
                    dqe = jnp.dot(dob[rc, hs], st_n.astype(BF16), preferred_element_type=F32)
                    dbl_c[c] = (jnp.sum(dst * st_n, axis=0, keepdims=True) * ebl
                                + jnp.sum(dke * ke_f[rc, hs], axis=0, keepdims=True))
                    dq_c[c], dk_c[c] = dqe * eb[rc, hs], dke * ee[rc, hs]
                    dst = dst * ebl + _dot(dob[rc, hs], qeb[rc, hs], TN)
                new_ds.append(dst)
                dq_x, dk_x = jnp.concatenate(dq_c, axis=0), jnp.concatenate(dk_c, axis=0)
                dq_h.append(dq_i + dq_x)
                dk_h.append(dk_i + dk_x)
                dv_h.append(dv + jnp.concatenate(dv_c, axis=0))
                dbl_h.append(qq[:, hs] * dq_x - kk[:, hs] * dk_x
                             + jnp.where(row64 == CHUNK - 1, dbl_c[0], 0.0) + jnp.where(row64 == LANES - 1, dbl_c[1], 0.0))
            dqq = jnp.concatenate(dq_h, axis=1)
            dkk = jnp.concatenate(dk_h, axis=1)
            dvv = jnp.concatenate(dv_h, axis=1)
            db = db_lv + jnp.concatenate(dbl_h, axis=1)
            dlf = _exact_dot(suffix, _split(db, 3))
            dqa = dqq * _dsilu(qa)
            dfg = jnp.where(fg > TINY, dlf / fg, 0.0)
            dz = (dfg - dkk) * (1.0 - lb) * sg * sgn
            dlb_acc = dlb_acc + jnp.sum(dfg * (1.0 - sg) - dkk * sgn, axis=0, keepdims=True)
            da_ref[pl.ds(r0, LANES), :] = jnp.concatenate([dqa, dz, dvv, dga], axis=1)
            return new_ds[0], new_ds[1], dgn_acc, dlb_acc

        zrow = jnp.zeros((1, LANES), F32)
        _, _, dgn_acc, dlb_acc = lax.fori_loop(0, nt, bwd_tile, (zero, zero, zrow, zrow))
        dgn_ref[...] = jnp.broadcast_to(dgn_acc, (8, LANES))
        dlb_ref[...] = jnp.broadcast_to(dlb_acc, (8, LANES))

    rows = jax.ShapeDtypeStruct((bsz, 8, HGRN_W), F32)
    row = pl.BlockSpec((1, 128), lambda b, p: (0, p))
    blk = pl.BlockSpec((None, t, 128), lambda b, p: (b, 0, p))
    return pl.pallas_call(
        body, grid=(bsz, 2),
        in_specs=[pl.BlockSpec((None, t, 512), lambda b, p: (b, 0, p)), blk, blk, row, row,
                  pl.BlockSpec(w_all.shape, lambda b, p: (0, 0)),
                  pl.BlockSpec(maskf.shape, lambda b, p: (0, 0, 0)),
                  pl.BlockSpec(rightf.shape, lambda b, p: (0, 0, 0))],
        out_specs=[pl.BlockSpec((None, t, 512), lambda b, p: (b, 0, p)),
                   pl.BlockSpec((None, 8, 128), lambda b, p: (b, 0, p)),
                   pl.BlockSpec((None, 8, 128), lambda b, p: (b, 0, p))],
        out_shape=[jax.ShapeDtypeStruct((bsz, t, A_W), F32), rows, rows],
        scratch_shapes=[pltpu.VMEM((2, nchunk, CHUNK, CHUNK), F32)],
        compiler_params=_cparams(("parallel", "parallel")), name=name)(
            proj3, o_raw, dmixed, lbs_row, gn_row, w_all, maskf, rightf)


def _pool_tt(t):
    return min(256, t)


def _window_select(s2, s4, s8, s16, lane):
    return jnp.where(lane < 64, s2, jnp.where(lane < 128, s4, jnp.where(lane < 192, s8, s16)))


def _pool_counts(t0, tt):
    lane = _iota((tt, POOL_W), 1)
    tpos = (_iota((tt, POOL_W), 0) + t0 + 1).astype(F32)
    win = jnp.where(lane < 64, 2.0, jnp.where(lane < 128, 4.0, jnp.where(lane < 192, 8.0, 16.0)))
    return 1.0 / jnp.minimum(tpos, win), lane


def _pooled_tile(upad_ref, i, tt):
    r0 = pl.multiple_of(i * tt, 8)
    cat = upad_ref[pl.ds(r0, tt + POOL_HALO), :]
    s2 = cat + pltpu.roll(cat, 1, 0)
    s4 = s2 + pltpu.roll(s2, 2, 0)
    s8 = s4 + pltpu.roll(s4, 4, 0)
    s16 = s8 + pltpu.roll(s8, 8, 0)
    inv, lane = _pool_counts(i * tt, tt)
    sel = _window_select(s2[POOL_HALO:], s4[POOL_HALO:], s8[POOL_HALO:], s16[POOL_HALO:], lane)
    return sel * inv - cat[POOL_HALO:], inv, lane


def _pool_fwd(proj3, wbd, scale_row, name):
    bsz, t, _ = proj3.shape
    tt = _pool_tt(t)

    def body(p_ref, w_ref, sc_ref, o_ref, upad):
        upad[0:POOL_HALO, :] = jnp.zeros((POOL_HALO, POOL_W), F32)
        upad[POOL_HALO:, :] = p_ref[:, 0:POOL_W]
        w = w_ref[...]
        sc = sc_ref[...]

        def tile(i, c):
            pooled, _, _ = _pooled_tile(upad, i, tt)
            r0 = pl.multiple_of(i * tt, 8)
            g = p_ref[pl.ds(r0, tt), POOL_W:2 * POOL_W]
            pre = jnp.dot(pooled.astype(BF16), w, preferred_element_type=F32)
            o_ref[pl.ds(r0, tt), :] = pre * sc * _silu(g)
            return c

        lax.fori_loop(0, t // tt, tile, 0)

    return pl.pallas_call(
        body, grid=(bsz,),
        in_specs=[pl.BlockSpec((None, t, 512), lambda b: (b, 0, B_BLK)),
                  pl.BlockSpec((POOL_W, POOL_W), lambda b: (0, 0)),
                  pl.BlockSpec((1, POOL_W), lambda b: (0, 0))],
        out_specs=pl.BlockSpec((None, t, POOL_W), lambda b: (b, 0, 0)),
        out_shape=jax.ShapeDtypeStruct((bsz, t, POOL_W), F32),
        scratch_shapes=[pltpu.VMEM((t + POOL_HALO, POOL_W), F32)],
        compiler_params=_cparams(("parallel",)), name=name)(proj3, wbd, scale_row)


def _pool_bwd(proj3, dmixed, wbd, scale_row, name):
    bsz, t, _ = proj3.shape
    tt = _pool_tt(t)

    def body(p_ref, do_ref, w_ref, sc_ref, db_ref, dsc_ref, dw_ref, upad, epad):
        upad[0:POOL_HALO, :] = jnp.zeros((POOL_HALO, POOL_W), F32)
        upad[POOL_HALO:, :] = p_ref[:, 0:POOL_W]
        epad[t:, :] = jnp.zeros((POOL_HALO, POOL_W), F32)
        w = w_ref[...]
        sc = sc_ref[...]

        def tile(i, carry):
            dsc_acc, dw_acc = carry
            pooled, inv, _ = _pooled_tile(upad, i, tt)
            r0 = pl.multiple_of(i * tt, 8)
            g = p_ref[pl.ds(r0, tt), POOL_W:2 * POOL_W]
            dout = do_ref[pl.ds(r0, tt), :]
            pb = pooled.astype(BF16)
            pre = jnp.dot(pb, w, preferred_element_type=F32)
            t1 = dout * _silu(g)
            dsc_acc = dsc_acc + jnp.sum(t1 * pre, axis=0, keepdims=True)
            dpre = (t1 * sc).astype(BF16)
            db_ref[pl.ds(r0, tt), POOL_W:2 * POOL_W] = dout * pre * sc * _dsilu(g)
            dw_acc = dw_acc + _dot(pb, dpre, TN)
            dpooled = _dot(dpre, w, NT)
            epad[pl.ds(r0, tt), :] = dpooled * inv
            return dsc_acc, dw_acc

        dsc_acc, dw_acc = lax.fori_loop(0, t // tt, tile, (jnp.zeros((1, POOL_W), F32), jnp.zeros((POOL_W, POOL_W), F32)))
        dsc_ref[...] = jnp.broadcast_to(dsc_acc, (8, POOL_W))
        dw_ref[...] = dw_acc

        def tile2(i, c):
            r0 = pl.multiple_of(i * tt, 8)
            n = tt + POOL_HALO
            cat = epad[pl.ds(r0, n), :]
            s2 = cat + pltpu.roll(cat, n - 1, 0)
            s4 = s2 + pltpu.roll(s2, n - 2, 0)
            s8 = s4 + pltpu.roll(s4, n - 4, 0)
            s16 = s8 + pltpu.roll(s8, n - 8, 0)
            inv, lane = _pool_counts(i * tt, tt)
            sel = _window_select(s2[:tt], s4[:tt], s8[:tt], s16[:tt], lane)
            db_ref[pl.ds(r0, tt), 0:POOL_W] = sel - cat[:tt] / inv
            return c

        lax.fori_loop(0, t // tt, tile2, 0)

    return pl.pallas_call(
        body, grid=(bsz,),
        in_specs=[pl.BlockSpec((None, t, 512), lambda b: (b, 0, B_BLK)),
                  pl.BlockSpec((None, t, POOL_W), lambda b: (b, 0, 1)),
                  pl.BlockSpec((POOL_W, POOL_W), lambda b: (0, 0)),
                  pl.BlockSpec((1, POOL_W), lambda b: (0, 0))],
        out_specs=[pl.BlockSpec((None, t, 512), lambda b: (b, 0, 0)),
                   pl.BlockSpec((None, 8, POOL_W), lambda b: (b, 0, 0)),
                   pl.BlockSpec((None, POOL_W, POOL_W), lambda b: (b, 0, 0))],
        out_shape=[jax.ShapeDtypeStruct((bsz, t, B_W), F32), jax.ShapeDtypeStruct((bsz, 8, POOL_W), F32),
                   jax.ShapeDtypeStruct((bsz, POOL_W, POOL_W), F32)],
        scratch_shapes=[pltpu.VMEM((t + POOL_HALO, POOL_W), F32), pltpu.VMEM((t + POOL_HALO, POOL_W), F32)],
        compiler_params=_cparams(("parallel",)), name=name)(proj3, dmixed, wbd, scale_row)


def _head_select_rows(hp):
    r, c = _iota((8, LANES), 0), _iota((8, LANES), 1)
    return ((r < 2) & (c == 2 * hp + r)).astype(F32)


def _foxgate_fwd(proj3, bias_row, name):
    bsz, t, _ = proj3.shape
    nt = t // LANES

    def body(f_ref, b_ref, cn_ref, ct_ref):
        bias = b_ref[...]
        i, j = _iota((LANES, LANES), 0), _iota((LANES, LANES), 1)
        lower = (j <= i).astype(F32)
        spread = (_iota((LANES, FOX_W), 0) == _iota((LANES, FOX_W), 1) // 64).astype(F32)

        def tile(k, carry):
            r0 = pl.multiple_of(k * LANES, LANES)
            xg = f_ref[pl.ds(r0, LANES), :] + bias
            lf = jnp.minimum(xg, 0.0) - jnp.log(1.0 + jnp.exp(-jnp.abs(xg)))
            c = jnp.dot(lower, lf, precision=HI, preferred_element_type=F32) + carry
            cn_ref[pl.ds(r0, LANES), :] = jnp.dot(c, spread, precision=HI, preferred_element_type=F32)
            for hp in range(4):
                ct_ref[hp, :, pl.ds(r0, LANES)] = _dot(_head_select_rows(hp), c, NT, precision=HI)
            return c[LANES - 1:LANES, :]

        lax.fori_loop(0, nt, tile, jnp.zeros((1, LANES), F32))

    return pl.pallas_call(
        body, grid=(bsz,),
        in_specs=[pl.BlockSpec((None, t, 128), lambda b: (b, 0, F_BLK)), pl.BlockSpec((1, 128), lambda b: (0, 0))],
        out_specs=[pl.BlockSpec((None, t, FOX_W), lambda b: (b, 0, 0)),
                   pl.BlockSpec((None, 4, 8, t), lambda b: (b, 0, 0, 0))],
        out_shape=[jax.ShapeDtypeStruct((bsz, t, FOX_W), F32), jax.ShapeDtypeStruct((bsz, 4, 8, t), F32)],
        compiler_params=_cparams(("parallel",)), name=name)(proj3, bias_row)


def _foxgate_bwd(proj3, dc_nat, bias_row, name):
    bsz, t, _ = proj3.shape
    nt = t // LANES

    def body(f_ref, dc_ref, b_ref, df_ref, dbias_ref, run_sc):
        bias = b_ref[...]
        i, j = _iota((LANES, LANES), 0), _iota((LANES, LANES), 1)
        upper = (j >= i).astype(F32)
        valid = _iota((1, LANES), 1) < FOX_HEADS
        run_sc[...] = jnp.zeros((8, LANES), F32)
        dbias_ref[...] = jnp.zeros((8, LANES), F32)

        def tile(k, c):
            r0 = pl.multiple_of((nt - 1 - k) * LANES, LANES)
            dc = dc_ref[pl.ds(r0, LANES), :] + jnp.where(i == LANES - 1, run_sc[0:1, :], 0.0)
            dlf = jnp.dot(upper, dc, precision=HI, preferred_element_type=F32)
            xg = f_ref[pl.ds(r0, LANES), :] + bias
            df = jnp.where(valid, dlf * _sig(-xg), 0.0)
            df_ref[pl.ds(r0, LANES), :] = df
            run_sc[...] = dlf[0:8, :]
            dbias_ref[...] += jnp.sum(df, axis=0, keepdims=True)
            return c

        lax.fori_loop(0, nt, tile, 0)

    blk = pl.BlockSpec((None, t, 128), lambda b: (b, 0, 0))
    return pl.pallas_call(
        body, grid=(bsz,),
        in_specs=[pl.BlockSpec((None, t, 128), lambda b: (b, 0, F_BLK)), blk, pl.BlockSpec((1, 128), lambda b: (0, 0))],
        out_specs=[blk, pl.BlockSpec((None, 8, 128), lambda b: (b, 0, 0))],
        out_shape=[jax.ShapeDtypeStruct((bsz, t, F_W), F32), jax.ShapeDtypeStruct((bsz, 8, 128), F32)],
        scratch_shapes=[pltpu.VMEM((8, LANES), F32)],
        compiler_params=_cparams(("parallel",)), name=name)(proj3, dc_nat, bias_row)


def _fox_tile(t):
    return min(256, t)


def _fox_fwd(proj3, c_nat, c_t, name):
    bsz, t, _ = proj3.shape
    tq = _fox_tile(t)
    tk = min(2 * tq, t)
    nq = t // tq

    def body(q_ref, kv_ref, cn_ref, ct_ref, og_ref, or_ref, lse_ref):
        i = pl.program_id(2)
        qblk = q_ref[...]
        first = _iota((1, 128), 1) < 64
        qv = qblk[:, 0:128] * 0.125
        qm = [jnp.where(first, qv, 0.0).astype(BF16), jnp.where(first, 0.0, qv).astype(BF16)]
        cqs = [cn_ref[:, 0:1], cn_ref[:, 64:65]]
        rows = _iota((tq, tk), 0) + i * tq

        def kv_step(j, carry, masked):
            c0 = pl.multiple_of(j * tk, tk)
            kb = kv_ref[pl.ds(c0, tk), 128:256].astype(BF16)
            vblk = kv_ref[pl.ds(c0, tk), 256:384]
            vx = [jnp.where(first, vblk, 1.0).astype(BF16), jnp.where(first, 1.0, vblk).astype(BF16)]
            new = []
            for h in range(2):
                m, acc = carry[2 * h], carry[2 * h + 1]
                s = _dot(qm[h], kb, NT) + (cqs[h] - ct_ref[h:h + 1, pl.ds(c0, tk)])
                if masked:
                    s = jnp.where(rows >= _iota((tq, tk), 1) + j * tk, s, MASK_VALUE)
                m_new = jnp.maximum(m, jnp.max(s, axis=1, keepdims=True))
                p = jnp.exp(s - m_new).astype(BF16)
                new += [m_new, jnp.exp(m - m_new) * acc + jnp.dot(p, vx[h], preferred_element_type=F32)]
            return tuple(new)

        init = (jnp.full((tq, 1), MASK_VALUE, F32), jnp.zeros((tq, 128), F32)) * 2
        n_full = (i * tq) // tk
        carry = lax.fori_loop(0, n_full, functools.partial(kv_step, masked=False), init)
        m0, acc0, m1, acc1 = kv_step(n_full, carry, True)
        l0, l1 = pltpu.roll(acc0, 64, 1), pltpu.roll(acc1, 64, 1)
        o = jnp.where(first, acc0 / l0, acc1 / l1)
        or_ref[...] = o
        og_ref[...] = o * _silu(qblk[:, 384:512])
        lse_ref[...] = jnp.where(first, m0 + jnp.log(l0), m1 + jnp.log(l1))

    out = jax.ShapeDtypeStruct((bsz, t, FOX_W), F32)
    blk = pl.BlockSpec((None, tq, 128), lambda b, p, i: (b, i, p))
    return pl.pallas_call(
        body, grid=(bsz, 4, nq),
        in_specs=[pl.BlockSpec((None, tq, 512), lambda b, p, i: (b, i, C_BLK0 + p)),
                  pl.BlockSpec((None, t, 512), lambda b, p, i: (b, 0, C_BLK0 + p)),
                  blk,
                  pl.BlockSpec((None, None, 8, t), lambda b, p, i: (b, p, 0, 0))],
        out_specs=[blk, blk, blk],
        out_shape=[out, out, out],
        compiler_params=_cparams(("parallel", "parallel", "arbitrary")), name=name)(proj3, proj3, c_nat, c_t)


def _fox_bwd(proj3, o_raw, dmixed, lse, c_nat, c_t, name):
    bsz, t, _ = proj3.shape
    tq = _fox_tile(t)
    nq = t // tq
    tk = min(2 * tq, t)
    ratio = tk // tq

    def body(a_ref, or_ref, do_ref, lse_ref, cn_ref, ct_ref, dc_out, dct_out, drow_out, dq_sc, do_sc, dl_sc):
        def prep(i, c):
            r0 = pl.multiple_of(i * tq, tq)
            g = a_ref[pl.ds(r0, tq), 384:512]
            dout = do_ref[pl.ds(r0, tq), :]
            o = or_ref[pl.ds(r0, tq), :]
            dc_out[pl.ds(r0, tq), 384:512] = dout * o * _dsilu(g)
            do = dout * _silu(g)
            do_sc[pl.ds(r0, tq), :] = do
            prod = do * o
            d0 = jnp.sum(prod[:, 0:64], axis=1, keepdims=True)
            d1 = jnp.sum(prod[:, 64:128], axis=1, keepdims=True)
            dl_sc[pl.ds(r0, tq), :] = jnp.concatenate([jnp.broadcast_to(d0, (tq, 64)), jnp.broadcast_to(d1, (tq, 64))], axis=1)
            dq_sc[pl.ds(r0, tq), :] = jnp.zeros((tq, 128), F32)
            drow_out[pl.ds(r0, tq), :] = jnp.zeros((tq, 128), F32)
            return c

        lax.fori_loop(0, nq, prep, 0)
        dct_out[...] = jnp.zeros((8, t), F32)

        first = _iota((1, 128), 1) < 64

        def heads(v):
            return [jnp.where(first, v, 0.0).astype(BF16), jnp.where(first, 0.0, v).astype(BF16)]

        def kv_tile(j, c):
            c0 = pl.multiple_of(j * tk, tk)
            kb = a_ref[pl.ds(c0, tk), 128:256].astype(BF16)
            vb = a_ref[pl.ds(c0, tk), 256:384].astype(BF16)
            cks = [ct_ref[h:h + 1, pl.ds(c0, tk)] for h in range(2)]

            def q_step(i, carry, diagonal):
                dk, dv, dcol0, dcol1 = carry
                r0 = pl.multiple_of(i * tq, tq)
                causal = _iota((tq, tk), 0) + i * tq >= _iota((tq, tk), 1) + j * tk
                qv = a_ref[pl.ds(r0, tq), 0:128] * 0.125
                do = do_sc[pl.ds(r0, tq), :]
                qb, dob = qv.astype(BF16), do.astype(BF16)
                qm, dom = heads(qv), heads(do)
                full, dcols, rsums = [], [], []
                for h in range(2):
                    lse_h = lse_ref[pl.ds(r0, tq), 64 * h:64 * h + 1]
                    dl_h = dl_sc[pl.ds(r0, tq), 64 * h:64 * h + 1]
                    cq = cn_ref[pl.ds(r0, tq), 64 * h:64 * h + 1]
                    p = jnp.exp(_dot(qm[h], kb, NT) + (cq - cks[h]) - lse_h)
                    if diagonal:
                        p = jnp.where(causal, p, 0.0)
                    ds = p * (_dot(dom[h], vb, NT) - dl_h)
                    dsb = ds.astype(BF16)
                    full.append((_dot(p.astype(BF16), dob, TN), _dot(dsb, qb, TN),
                                 jnp.dot(dsb, kb, preferred_element_type=F32)))
                    dcols.append(jnp.sum(ds, axis=0, keepdims=True))
                    rsums.append(jnp.broadcast_to(jnp.sum(ds, axis=1, keepdims=True), (tq, 128)))
                dq_sc[pl.ds(r0, tq), :] += jnp.where(first, full[0][2], full[1][2]) * 0.125
                drow_out[pl.ds(r0, tq), :] += jnp.where(first, rsums[0], rsums[1])
                return (dk + jnp.where(first, full[0][1], full[1][1]), dv + jnp.where(first, full[0][0], full[1][0]),
                        dcol0 - dcols[0], dcol1 - dcols[1])

            carry = (jnp.zeros((tk, 128), F32), jnp.zeros((tk, 128), F32), jnp.zeros((1, tk), F32), jnp.zeros((1, tk), F32))
            for r in range(ratio):
                carry = q_step(ratio * j + r, carry, True)
            dk, dv, dcol0, dcol1 = lax.fori_loop(ratio * (j + 1), nq, functools.partial(q_step, diagonal=False), carry)
            dct_out[0:1, pl.ds(c0, tk)] = dcol0
            dct_out[1:2, pl.ds(c0, tk)] = dcol1
            dc_out[pl.ds(c0, tk), 128:256] = dk
            dc_out[pl.ds(c0, tk), 256:384] = dv
            return c

        lax.fori_loop(0, t // tk, kv_tile, 0)
        dc_out[:, 0:128] = dq_sc[...]

    blk = pl.BlockSpec((None, t, 128), lambda b, p: (b, 0, p))
    return pl.pallas_call(
        body, grid=(bsz, 4),
        in_specs=[pl.BlockSpec((None, t, 512), lambda b, p: (b, 0, C_BLK0 + p)),
                  blk,
                  pl.BlockSpec((None, t, 128), lambda b, p: (b, 0, 4 + p)),
                  blk, blk,
                  pl.BlockSpec((None, None, 8, t), lambda b, p: (b, p, 0, 0))],
        out_specs=[pl.BlockSpec((None, t, 512), lambda b, p: (b, 0, p)),
                   pl.BlockSpec((None, None, 8, t), lambda b, p: (b, p, 0, 0)), blk],
        out_shape=[jax.ShapeDtypeStruct((bsz, t, C_W), F32), jax.ShapeDtypeStruct((bsz, 4, 8, t), F32),
                   jax.ShapeDtypeStruct((bsz, t, FOX_W), F32)],
        scratch_shapes=[pltpu.VMEM((t, 128), F32), pltpu.VMEM((t, 128), F32), pltpu.VMEM((t, 128), F32)],
        compiler_params=_cparams(("parallel", "parallel")), name=name)(proj3, o_raw, dmixed, lse, c_nat, c_t)


def _mix_tm(n):
    return min(512, n)


def _outproj_fwd(x2, oa, ob, oc, wo, g_row, name):
    n, d = x2.shape
    tm = _mix_tm(n)

    def body(x_ref, oa_ref, ob_ref, oc_ref, w_ref, g_ref, y_ref, xo_ref):
        y = (jnp.dot(oa_ref[...].astype(BF16), w_ref[0:256, :], preferred_element_type=F32)
             + jnp.dot(ob_ref[...].astype(BF16), w_ref[256:512, :], preferred_element_type=F32)
             + jnp.dot(oc_ref[...].astype(BF16), w_ref[512:1024, :], preferred_element_type=F32))
        y_ref[...] = y
        xo_ref[...] = x_ref[...] + y * _rstd(y) * g_ref[...]

    row = lambda w: pl.BlockSpec((tm, w), lambda i: (i, 0))
    out = jax.ShapeDtypeStruct((n, d), F32)
    return pl.pallas_call(
        body, grid=(n // tm,),
        in_specs=[row(d), row(256), row(256), row(512), pl.BlockSpec((d, d), lambda i: (0, 0)),
                  pl.BlockSpec((1, d), lambda i: (0, 0))],
        out_specs=[row(d), row(d)], out_shape=[out, out],
        compiler_params=_cparams(("parallel",)), name=name)(x2, oa, ob, oc, wo, g_row)


def _loss_head(x2, target2, name):
    n, d = x2.shape
    tm = _mix_tm(n)

    def body(x_ref, t_ref, dx_ref, l_ref):
        err = x_ref[...] - t_ref[...]
        dx_ref[...] = err * (1.0 / d)

        @pl.when(pl.program_id(0) == 0)
        def _():
            l_ref[...] = jnp.zeros((8, 128), F32)

        l_ref[...] += jnp.sum(err * err)

    row = pl.BlockSpec((tm, d), lambda i: (i, 0))
    return pl.pallas_call(
        body, grid=(n // tm,), in_specs=[row, row],
        out_specs=[row, pl.BlockSpec((8, 128), lambda i: (0, 0))],
        out_shape=[jax.ShapeDtypeStruct((n, d), F32), jax.ShapeDtypeStruct((8, 128), F32)],
        compiler_params=_cparams(("arbitrary",)), name=name)(x2, target2)


def _outproj_bwd(dxo, y, oa, ob, oc, wo, g_row, name):
    n, d = dxo.shape
    tm = _mix_tm(n)

    def body(dx_ref, y_ref, oa_ref, ob_ref, oc_ref, w_ref, g_ref, dm_ref, dw_ref, dg_ref):
        @pl.when(pl.program_id(0) == 0)
        def _():
            dw_ref[...] = jnp.zeros((d, d), F32)
            dg_ref[...] = jnp.zeros((8, d), F32)

        yv, dx = y_ref[...], dx_ref[...]
        r = _rstd(yv)
        yn = yv * r
        dg_ref[...] += jnp.sum(dx * yn, axis=0, keepdims=True)
        dyn = dx * g_ref[...]
        dy = (r * (dyn - yn * jnp.mean(dyn * yn, axis=-1, keepdims=True))).astype(BF16)
        dm_ref[...] = _dot(dy, w_ref[...], NT)
        dw_ref[0:256, :] += _dot(oa_ref[...].astype(BF16), dy, TN)
        dw_ref[256:512, :] += _dot(ob_ref[...].astype(BF16), dy, TN)
        dw_ref[512:1024, :] += _dot(oc_ref[...].astype(BF16), dy, TN)

    row = lambda w: pl.BlockSpec((tm, w), lambda i: (i, 0))
    fixed = lambda r, c: pl.BlockSpec((r, c), lambda i: (0, 0))
    return pl.pallas_call(
        body, grid=(n // tm,),
        in_specs=[row(d), row(d), row(256), row(256), row(512), fixed(d, d), fixed(1, d)],
        out_specs=[row(d), fixed(d, d), fixed(8, d)],
        out_shape=[jax.ShapeDtypeStruct((n, d), F32), jax.ShapeDtypeStruct((d, d), F32), jax.ShapeDtypeStruct((8, d), F32)],
        compiler_params=_cparams(("arbitrary",)), name=name)(dxo, y, oa, ob, oc, wo, g_row)


_PIECES = ((0, A_W), (A_W, B_W), (A_W + B_W, C_W), (A_W + B_W + C_W, F_W))


def _inproj_bwd_x(x2, dxo, g_row, w_int, pieces, name):
    n, d = x2.shape
    tm = min(256, n)

    def body(x_ref, dxo_ref, g_ref, w_ref, da_ref, db_ref, dc_ref, df_ref, dx_ref, dg_ref):
        @pl.when(pl.program_id(0) == 0)
        def _():
            dg_ref[...] = jnp.zeros((8, d), F32)

        dh = jnp.zeros((tm, d), F32)
        for ref, (o, w) in zip((da_ref, db_ref, dc_ref, df_ref), _PIECES):
            dh = dh + _dot(ref[...].astype(BF16), w_ref[:, o:o + w], NT)
        x = x_ref[...]
        r = _rstd(x)
        xn = x * r
        dg_ref[...] += jnp.sum(dh * xn, axis=0, keepdims=True)
        dxn = dh * g_ref[...]
        dx_ref[...] = dxo_ref[...] + r * (dxn - xn * jnp.mean(dxn * xn, axis=-1, keepdims=True))

    row = lambda w: pl.BlockSpec((tm, w), lambda i: (i, 0))
    fixed = lambda r, c: pl.BlockSpec((r, c), lambda i: (0, 0))
    return pl.pallas_call(
        body, grid=(n // tm,),
        in_specs=[row(d), row(d), fixed(1, d), fixed(d, E_INT)] + [row(w) for _, w in _PIECES],
        out_specs=[row(d), fixed(8, d)],
        out_shape=[jax.ShapeDtypeStruct((n, d), F32), jax.ShapeDtypeStruct((8, d), F32)],
        compiler_params=_cparams(("arbitrary",)), name=name)(x2, dxo, g_row, w_int, *pieces)


def _inproj_bwd_w(x2, g_row, piece, name):
    n, d = x2.shape
    w = piece.shape[1]
    tm = min(512, n)

    def body(x_ref, g_ref, dp_ref, dw_ref):
        @pl.when(pl.program_id(0) == 0)
        def _():
            dw_ref[...] = jnp.zeros((d, w), F32)

        x = x_ref[...]
        h = (x * _rstd(x) * g_ref[...]).astype(BF16)
        dw_ref[...] += _dot(h, dp_ref[...].astype(BF16), TN)

    return pl.pallas_call(
        body, grid=(n // tm,),
        in_specs=[pl.BlockSpec((tm, d), lambda i: (i, 0)), pl.BlockSpec((1, d), lambda i: (0, 0)),
                  pl.BlockSpec((tm, w), lambda i: (i, 0))],
        out_specs=pl.BlockSpec((d, w), lambda i: (0, 0)),
        out_shape=jax.ShapeDtypeStruct((d, w), F32),
        compiler_params=_cparams(("arbitrary",)), name=name)(x2, g_row, piece)


def _block_diag(pool_w_l):
    z = jnp.zeros((64, 64), pool_w_l.dtype)
    return jnp.concatenate(
        [jnp.concatenate([pool_w_l[g] if c == g else z for c in range(4)], axis=1) for g in range(4)], axis=0)


def _pad_lanes(v, width=128):
    return jnp.pad(v, ((0, 0),) * (v.ndim - 1) + ((0, width - v.shape[-1]),))


def _local_step(x, target, lower_bounds, pre_norm_g, w_in_int, hgrn_norm_g, fox_f_bias, pool_w, pool_scale,
                w_out_bf, post_norm_g):
    bsz, t, d = x.shape
    n = bsz * t
    lbs = _lbs_fwd(lower_bounds)
    saved = []
    xc = x.reshape(n, d)
    for l in range(DEPTH):
        proj = _inproj_fwd(xc, pre_norm_g[l:l + 1], w_in_int[l], f"inproj_fwd{l}").reshape(bsz, t, E_INT)
        wbd = _block_diag(pool_w[l]).astype(BF16)
        bias_row = _pad_lanes(fox_f_bias[l:l + 1])
        oa, oa_raw = _hgrn_fwd(proj, lbs[l:l + 1], hgrn_norm_g[l:l + 1], f"hgrn_fwd{l}")
        ob = _pool_fwd(proj, wbd, pool_scale[l:l + 1], f"pool_fwd{l}")
        c_nat, c_t = _foxgate_fwd(proj, bias_row, f"foxgate_fwd{l}")
        oc, oc_raw, lse = _fox_fwd(proj, c_nat, c_t, f"fox_fwd{l}")
        y, xn = _outproj_fwd(xc, oa.reshape(n, -1), ob.reshape(n, -1), oc.reshape(n, -1), w_out_bf[l],
                             post_norm_g[l:l + 1], f"outproj_fwd{l}")
        saved.append((xc, proj, wbd, bias_row, oa, oa_raw, ob, oc, oc_raw, lse, c_nat, c_t, y))
        xc = xn
    dx, sq = _loss_head(xc, target.reshape(n, d), "loss_head")
    g = {k: [None] * DEPTH for k in ("pre", "w_in", "hgn", "bias", "pool_w", "pool_scale", "w_out", "post", "lbs")}
    for l in reversed(range(DEPTH)):
        xin, proj, wbd, bias_row, oa, oa_raw, ob, oc, oc_raw, lse, c_nat, c_t, y = saved[l]
        dmix, g["w_out"][l], dpost = _outproj_bwd(dx, y, oa.reshape(n, -1), ob.reshape(n, -1), oc.reshape(n, -1),
                                                  w_out_bf[l], post_norm_g[l:l + 1], f"outproj_bwd{l}")
        g["post"][l] = dpost[0]
        dmix3 = dmix.reshape(bsz, t, d)
        d_c, dct, drow = _fox_bwd(proj, oc_raw, dmix3, lse, c_nat, c_t, f"fox_bwd{l}")
        dc_nat = _pad_lanes(dct[:, :, 0:2, :].reshape(bsz, FOX_HEADS, t).transpose(0, 2, 1)
                            + drow.reshape(bsz, t, FOX_HEADS, 64)[..., 0])
        d_f, dbias = _foxgate_bwd(proj, dc_nat, bias_row, f"foxgate_bwd{l}")
        g["bias"][l] = jnp.sum(dbias[:, 0, :FOX_HEADS], axis=0)
        d_b, dscale, dwbd = _pool_bwd(proj, dmix3, wbd, pool_scale[l:l + 1], f"pool_bwd{l}")
        g["pool_scale"][l] = jnp.sum(dscale[:, 0], axis=0)
        dwbd = jnp.sum(dwbd, axis=0)
        g["pool_w"][l] = jnp.stack([dwbd[64 * k:64 * (k + 1), 64 * k:64 * (k + 1)] for k in range(4)])
        d_a, dgn, dlb = _hgrn_bwd(proj, oa_raw, dmix3, lbs[l:l + 1], hgrn_norm_g[l:l + 1], f"hgrn_bwd{l}")
        g["hgn"][l] = jnp.sum(dgn[:, 0], axis=0)
        g["lbs"][l] = jnp.sum(dlb[:, 0], axis=0)
        pieces = [p.reshape(n, -1) for p in (d_a, d_b, d_c, d_f)]
        g["w_in"][l] = jnp.concatenate(
            [_inproj_bwd_w(xin, pre_norm_g[l:l + 1], p, f"inproj_bwd_w{l}_{k}") for k, p in enumerate(pieces)], axis=1)
        dx, dpre = _inproj_bwd_x(xin, dx, pre_norm_g[l:l + 1], w_in_int[l], pieces, f"inproj_bwd_x{l}")
        g["pre"][l] = dpre[0]
    grads = {k: jnp.stack(v) for k, v in g.items()}
    return sq, dx.reshape(bsz, t, d), grads


def _place():
    return lax.axis_index("x"), lax.axis_index("y"), lax.axis_index("c")


def _other_chips(x, y):
    return [(1 - x, y), (x, 1 - y), (1 - x, 1 - y)]


_ANY = pl.BlockSpec(memory_space=pl.ANY)


def _gather_weights(w_in_sh, w_out_sh):
    def body(win_ref, wout_ref, ain_ref, aout_ref, ici_send, ici_recv, d2d_send, d2d_recv, local_sems):
        x, y, c = _place()
        me = 2 * x + y
        pairs = ((win_ref, ain_ref), (wout_ref, aout_ref))
        mine = [pltpu.make_async_copy(src, dst.at[me], local_sems.at[j]) for j, (src, dst) in enumerate(pairs)]
        for cp in mine:
            cp.start()
        chips = _other_chips(x, y)
        sends = [pltpu.make_async_remote_copy(
            src_ref=src.at[c], dst_ref=dst.at[me, c], send_sem=ici_send.at[2 * k + j], recv_sem=ici_recv.at[2 * k + j],
            device_id=(px, py, c), device_id_type=MESH) for k, (px, py) in enumerate(chips) for j, (src, dst) in enumerate(pairs)]
        for cp in sends:
            cp.start()
        passed = [pltpu.make_async_remote_copy(
            src_ref=dst.at[2 * px + py, c], dst_ref=dst.at[2 * px + py, c], send_sem=d2d_send.at[2 * k + j],
            recv_sem=d2d_recv.at[2 * k + j], device_id=(x, y, 1 - c), device_id_type=MESH)
            for k, (px, py) in enumerate(chips) for j, (src, dst) in enumerate(pairs)]
        for n, (k, j) in enumerate((k, j) for k in range(3) for j in range(2)):
            px, py = chips[k]
            src, dst = pairs[j]
            pltpu.make_async_remote_copy(
                src_ref=src.at[c], dst_ref=dst.at[2 * px + py, c], send_sem=ici_send.at[n], recv_sem=ici_recv.at[n],
                device_id=(px, py, c), device_id_type=MESH).wait_recv()
            passed[n].start()
        for n, (k, j) in enumerate((k, j) for k in range(3) for j in range(2)):
            px, py = chips[k]
            src, dst = pairs[j]
            pltpu.make_async_remote_copy(
                src_ref=dst.at[2 * px + py, 1 - c], dst_ref=dst.at[2 * px + py, 1 - c], send_sem=d2d_send.at[n],
                recv_sem=d2d_recv.at[n], device_id=(x, y, 1 - c), device_id_type=MESH).wait_recv()
        for cp in sends + passed:
            cp.wait_send()
        for cp in mine:
            cp.wait()

    sems = pltpu.SemaphoreType.DMA((6,))
    return pl.pallas_call(
        body, in_specs=[_ANY, _ANY], out_specs=[_ANY, _ANY],
        out_shape=[jax.ShapeDtypeStruct((N_CHIPS,) + w_in_sh.shape, w_in_sh.dtype),
                   jax.ShapeDtypeStruct((N_CHIPS,) + w_out_sh.shape, w_out_sh.dtype)],
        scratch_shapes=[sems, sems, sems, sems, pltpu.SemaphoreType.DMA((2,))],
        name="gather_weights")(w_in_sh, w_out_sh)


def _swap_with_sibling(parts, name):
    k = len(parts)

    def body(*refs):
        src, dst = refs[:k], refs[k:2 * k]
        send_sems, recv_sems = refs[2 * k:]
        x, y, c = _place()
        cps = [pltpu.make_async_remote_copy(src_ref=src[j], dst_ref=dst[j], send_sem=send_sems.at[j], recv_sem=recv_sems.at[j],
                                            device_id=(x, y, 1 - c), device_id_type=MESH) for j in range(k)]
        for cp in cps:
            cp.start()
        for cp in cps:
            cp.wait()

    return pl.pallas_call(
        body, in_specs=[_ANY] * k, out_specs=[_ANY] * k,
        out_shape=[jax.ShapeDtypeStruct(p.shape, p.dtype) for p in parts],
        scratch_shapes=[pltpu.SemaphoreType.DMA((k,)), pltpu.SemaphoreType.DMA((k,))], name=name)(*parts)


def _scatter_to_chips(parts, name):
    k = len(parts)

    def body(*refs):
        src, dst = refs[:k], refs[k:2 * k]
        send_sems, recv_sems = refs[2 * k:]
        x, y, c = _place()
        me = 2 * x + y
        cps = []
        for rel, (px, py) in enumerate(_other_chips(x, y)):
            for j in range(k):
                cps.append(pltpu.make_async_remote_copy(
                    src_ref=src[j].at[2 * px + py], dst_ref=dst[j].at[rel], send_sem=send_sems.at[rel * k + j],
                    recv_sem=recv_sems.at[rel * k + j], device_id=(px, py, c), device_id_type=MESH))
        for cp in cps:
            cp.start()
        for cp in cps:
            cp.wait()
        del me

    return pl.pallas_call(
        body, in_specs=[_ANY] * k, out_specs=[_ANY] * k,
        out_shape=[jax.ShapeDtypeStruct((3,) + p.shape[1:], p.dtype) for p in parts],
        scratch_shapes=[pltpu.SemaphoreType.DMA((3 * k,)), pltpu.SemaphoreType.DMA((3 * k,))], name=name)(*parts)


def _add_n(parts, name, with_bf16=False):
    r, c = parts[0].shape
    tr = 256 if r % 256 == 0 else r
    n = len(parts)

    def body(*refs):
        acc = refs[0][...].astype(F32)
        for ref in refs[1:n]:
            acc = acc + ref[...].astype(F32)
        refs[n][...] = acc
        if with_bf16:
            refs[n + 1][...] = acc.astype(BF16)

    blk = pl.BlockSpec((tr, c), lambda i: (i, 0))
    outs = [jax.ShapeDtypeStruct((r, c), F32)] + ([jax.ShapeDtypeStruct((r, c), BF16)] if with_bf16 else [])
    res = pl.pallas_call(
        body, grid=(r // tr,), in_specs=[blk] * n, out_specs=[blk] * len(outs),
        out_shape=outs, compiler_params=_cparams(("parallel",)), name=name)(*parts)
    return res if with_bf16 else res[0]


def _all_reduce_small(packet):
    r, w = packet.shape

    def body(p_ref, o_ref, buf, send_sems, recv_sems):
        x, y, c = _place()
        me = 4 * x + 2 * y + c
        buf[me] = p_ref[...]
        peers = []
        for k in range(1, 8):
            fx, fy, fc = (k >> 2) & 1, (k >> 1) & 1, k & 1
            peers.append((x ^ fx, y ^ fy, c ^ fc))
        cps = [pltpu.make_async_remote_copy(src_ref=p_ref, dst_ref=buf.at[me], send_sem=send_sems.at[k], recv_sem=recv_sems.at[k],
                                            device_id=peer, device_id_type=MESH) for k, peer in enumerate(peers)]
        for cp in cps:
            cp.start()
        for k, (px, py, pc) in enumerate(peers):
            pltpu.make_async_remote_copy(src_ref=p_ref, dst_ref=buf.at[4 * px + 2 * py + pc], send_sem=send_sems.at[k],
                                         recv_sem=recv_sems.at[k], device_id=(px, py, pc), device_id_type=MESH).wait_recv()
        for cp in cps:
            cp.wait_send()
        acc = buf[0]
        for k in range(1, 8):
            acc = acc + buf[k]
        o_ref[...] = acc

    vm = pl.BlockSpec(memory_space=pltpu.VMEM)
    return pl.pallas_call(
        body, in_specs=[vm], out_specs=vm, out_shape=jax.ShapeDtypeStruct((r, w), F32),
        scratch_shapes=[pltpu.VMEM((8, r, w), F32), pltpu.SemaphoreType.DMA((7,)), pltpu.SemaphoreType.DMA((7,))],
        name="all_reduce_small")(packet)


def _adamw_math(w, g, m, v):
    m = ADAM_B1 * m + (1.0 - ADAM_B1) * g
    v = ADAM_B2 * v + (1.0 - ADAM_B2) * (g * g)
    m_hat = m / (1.0 - ADAM_B1 ** ADAM_STEP)
    v_hat = v / (1.0 - ADAM_B2 ** ADAM_STEP)
    return -ADAM_LR * (m_hat / (jnp.sqrt(v_hat) + ADAM_EPS) + ADAM_WD * w), m, v


def _adamw(w, g, m, v, name):
    nl, r, c = w.shape
    tr = 256 if r % 256 == 0 else r

    def body(w_ref, g_ref, m_ref, v_ref, d_ref, mo_ref, vo_ref):
        d_ref[...], mo_ref[...], vo_ref[...] = _adamw_math(w_ref[...], g_ref[...], m_ref[...], v_ref[...])

    blk = pl.BlockSpec((None, tr, c), lambda l, i: (l, i, 0))
    out = jax.ShapeDtypeStruct(w.shape, F32)
    return pl.pallas_call(
        body, grid=(nl, r // tr), in_specs=[blk] * 4, out_specs=[blk] * 3, out_shape=[out] * 3,
        compiler_params=_cparams(("parallel", "parallel")), name=name)(w, g, m, v)


def _small_update(gsum, lower_bounds, wpack, mpack, vpack):
    r, w = gsum.shape
    lb_rows = DEPTH * HGRN_W // 128

    def body(g_ref, a_ref, w_ref, m_ref, v_ref, go_ref, d_ref, mo_ref, vo_ref):
        a = a_ref[...]
        a0, a1 = a[0:1], a[1:2]
        mx = jnp.maximum(a0, a1)
        e0, e1 = jnp.exp(a0 - mx), jnp.exp(a1 - mx)
        p0, p1 = e0 / (e0 + e1), e1 / (e0 + e1)
        g = g_ref[...]
        half = lb_rows // 2
        dl0 = jnp.concatenate([g[k:k + 1] for k in range(half)], axis=1)
        dl1 = jnp.concatenate([g[half + k:half + k + 1] for k in range(half)], axis=1)
        dp0 = (dl0 + dl1) - (dl0 + dl1)
        dp1 = dl1
        inner = p0 * dp0 + p1 * dp1
        da0, da1 = p0 * (dp0 - inner), p1 * (dp1 - inner)
        rows = [da0[:, 128 * k:128 * (k + 1)] for k in range(half)] + [da1[:, 128 * k:128 * (k + 1)] for k in range(half)]
        gfull = jnp.concatenate(rows + [g[lb_rows:]], axis=0)
        go_ref[...] = gfull
        d_ref[...], mo_ref[...], vo_ref[...] = _adamw_math(w_ref[...], gfull, m_ref[...], v_ref[...])

    vm = pl.BlockSpec(memory_space=pltpu.VMEM)
    out = jax.ShapeDtypeStruct((r, w), F32)
    return pl.pallas_call(body, in_specs=[vm] * 5, out_specs=[vm] * 4, out_shape=[out] * 4, name="small_update")(
        gsum, lower_bounds, wpack, mpack, vpack)


_SMALL = ("lower_bounds", "pre_norm_g", "hgrn_norm_g", "fox_f_bias", "pool_w", "pool_scale", "post_norm_g")


def _pack(parts):
    rows = []
    for k in _SMALL:
        f = parts[k].reshape(-1)
        pad = (-f.shape[0]) % (8 * 128)
        rows.append(jnp.pad(f, (0, pad)).reshape(-1, 128))
    rows.append(jnp.zeros((8, 128), F32))
    return jnp.concatenate(rows, axis=0)


def _unpack(pack, like):
    out, r = {}, 0
    for k in _SMALL:
        size = int(np.prod(like[k].shape))
        nr = -(-size // (8 * 128)) * 8
        out[k] = pack[r:r + nr].reshape(-1)[:size].reshape(like[k].shape)
        r += nr
    return out, r


def kernel(x, lower_bounds, pre_norm_g, w_in, hgrn_norm_g, fox_f_bias, pool_w, pool_scale, w_out, post_norm_g, loss_target, m_lower_bounds, m_pre_norm_g, m_w_in, m_hgrn_norm_g, m_fox_f_bias, m_pool_w, m_pool_scale, m_w_out, m_post_norm_g, v_lower_bounds, v_pre_norm_g, v_w_in, v_hgrn_norm_g, v_fox_f_bias, v_pool_w, v_pool_scale, v_w_out, v_post_norm_g):
    cx, cy, cc = _place()
    chip = 2 * cx + cy

    ain, aout = _gather_weights(w_in.astype(BF16), w_out.astype(BF16))
    w_in_full = jnp.concatenate([ain[q] for q in range(N_CHIPS)], axis=-1)
    w_in_int = _to_internal(w_in_full)
    w_out_full = jnp.concatenate([aout[q] for q in range(N_CHIPS)], axis=1)

    sq, grad_x, g = _local_step(x, loss_target, lower_bounds, pre_norm_g, w_in_int, hgrn_norm_g, fox_f_bias, pool_w,
                                pool_scale, w_out_full, post_norm_g)

    gin = _to_original(g["w_in"])
    gin_blocks = jnp.stack([gin[:, :, SHARD_W * q:SHARD_W * (q + 1)] for q in range(N_CHIPS)])
    gout_blocks = g["w_out"].reshape(DEPTH, N_CHIPS, 256, D_MODEL).transpose(1, 0, 2, 3)
    take = lambda a, l: lax.dynamic_index_in_dim(a, l, axis=1, keepdims=False)
    mine_in, mine_out = take(gin_blocks, cc), take(gout_blocks, cc)
    sib_in, sib_out = _swap_with_sibling([take(gin_blocks, 1 - cc), take(gout_blocks, 1 - cc)], "grad_swap1")
    rin, rout = 4 * 1024, 4 * 256
    sum_in, send_in = _add_n([mine_in.reshape(rin, SHARD_W), sib_in.reshape(rin, SHARD_W)], "grad_add1_in", True)
    sum_out, send_out = _add_n([mine_out.reshape(rout, D_MODEL), sib_out.reshape(rout, D_MODEL)], "grad_add1_out", True)
    sum_in, sum_out = sum_in.reshape(4, 1024, SHARD_W), sum_out.reshape(4, 256, D_MODEL)
    got_in, got_out = _scatter_to_chips([send_in.reshape(4, 1024, SHARD_W), send_out.reshape(4, 256, D_MODEL)], "grad_scatter")
    own = lambda a: lax.dynamic_index_in_dim(a, chip, axis=0, keepdims=False)
    half_in = _add_n([own(sum_in)] + [got_in[k] for k in range(3)], "grad_add2_in")
    half_out = _add_n([own(sum_out)] + [got_out[k] for k in range(3)], "grad_add2_out")
    oth_in, oth_out = _swap_with_sibling([half_in, half_out], "grad_swap2")
    first = cc == 0
    grad_w_in = jnp.stack([jnp.where(first, half_in, oth_in), jnp.where(first, oth_in, half_in)])
    grad_w_out = jnp.stack([jnp.where(first, half_out, oth_out), jnp.where(first, oth_out, half_out)])

    small = {"lower_bounds": g["lbs"], "pre_norm_g": g["pre"], "hgrn_norm_g": g["hgn"], "fox_f_bias": g["bias"],
             "pool_w": g["pool_w"], "pool_scale": g["pool_scale"], "post_norm_g": g["post"]}
    packet = _pack(small)
    nrows = packet.shape[0]
    packet = packet.at[nrows - 1].set(sq[0])
    gsum = _all_reduce_small(packet)
    loss = gsum[nrows - 1, 0] * (0.5 / D_MODEL)

    weights = {"lower_bounds": lower_bounds, "pre_norm_g": pre_norm_g, "hgrn_norm_g": hgrn_norm_g,
               "fox_f_bias": fox_f_bias, "pool_w": pool_w, "pool_scale": pool_scale, "post_norm_g": post_norm_g}
    moments_m = {"lower_bounds": m_lower_bounds, "pre_norm_g": m_pre_norm_g, "hgrn_norm_g": m_hgrn_norm_g,
                 "fox_f_bias": m_fox_f_bias, "pool_w": m_pool_w, "pool_scale": m_pool_scale, "post_norm_g": m_post_norm_g}
    moments_v = {"lower_bounds": v_lower_bounds, "pre_norm_g": v_pre_norm_g, "hgrn_norm_g": v_hgrn_norm_g,
                 "fox_f_bias": v_fox_f_bias, "pool_w": v_pool_w, "pool_scale": v_pool_scale, "post_norm_g": v_post_norm_g}
    gp, dp, mp, vp = _small_update(gsum, lower_bounds, _pack(weights), _pack(moments_m), _pack(moments_v))
    gs, _ = _unpack(gp, weights)
    ds, _ = _unpack(dp, weights)
    ms, _ = _unpack(mp, weights)
    vs, _ = _unpack(vp, weights)

    d_in, m_in, v_in = _adamw(w_in, grad_w_in, m_w_in, v_w_in, "adamw_w_in")
    d_out, m_out, v_out = _adamw(w_out, grad_w_out, m_w_out, v_w_out, "adamw_w_out")

    def ordered(s, big_in, big_out):
        return (s["lower_bounds"], s["pre_norm_g"], big_in, s["hgrn_norm_g"], s["fox_f_bias"], s["pool_w"],
                s["pool_scale"], big_out, s["post_norm_g"])

    return (loss, grad_x, *ordered(gs, grad_w_in, grad_w_out), *ordered(ds, d_in, d_out),
            *ordered(ms, m_in, m_out), *ordered(vs, v_in, v_out))
```

```python
import functools

import numpy as np
import jax
import jax.numpy as jnp
from jax import lax
from jax.experimental import pallas as pl
from jax.experimental.pallas import tpu as pltpu

F32 = jnp.float32
BF16 = jnp.bfloat16
HI = lax.Precision.HIGHEST
MESH = pl.DeviceIdType.MESH

NORM_EPS = 1e-6
MASK_VALUE = -1e30
TINY = 1e-30
ADAM_LR, ADAM_B1, ADAM_B2, ADAM_EPS, ADAM_WD, ADAM_STEP = 0.001, 0.9, 0.999, 1e-08, 0.01, 10

D_MODEL = 1024
DEPTH = 2
N_CHIPS = 4
CHUNK = 64
LANES = 128
HGRN_W, POOL_W, FOX_W, FOX_HEADS = 256, 256, 512, 8
POOL_WINDOWS = (2, 4, 8, 16)
POOL_HALO = 16
IN_WIDTH = 3592
SHARD_W = IN_WIDTH // N_CHIPS
A_W, B_W, C_W, F_W = 1024, 512, 2048, 128
E_INT = A_W + B_W + C_W + F_W
B_BLK = A_W // 512
C_BLK0 = (A_W + B_W) // 512
F_BLK = (A_W + B_W + C_W) // 128


def _segments():
    segs = []
    for hp in range(2):
        for part in range(4):
            segs.append((part * 256 + hp * 128, 128))
    segs.append((1024, 256))
    segs.append((1280, 256))
    for hp in range(4):
        for part in range(4):
            segs.append((1536 + part * 512 + hp * 128, 128))
    segs.append((3584, 8))
    return segs


_SEGS = _segments()


def _to_internal(w):
    parts = [w[..., s:s + n] for s, n in _SEGS]
    parts.append(jnp.zeros(w.shape[:-1] + (E_INT - IN_WIDTH,), w.dtype))
    return jnp.concatenate(parts, axis=-1)


def _to_original(w):
    offs, o = [], 0
    for s, n in _SEGS:
        offs.append((s, o, n))
        o += n
    parts = [w[..., o:o + n] for s, o, n in sorted(offs)]
    return jnp.concatenate(parts, axis=-1)


def _cparams(sem=None, vmem_mb=48):
    kw = dict(vmem_limit_bytes=vmem_mb * 1024 * 1024)
    if sem is not None:
        kw["dimension_semantics"] = sem
    return pltpu.CompilerParams(**kw)


def _sig(x):
    return 1.0 / (1.0 + jnp.exp(-x))


def _silu(x):
    return x * _sig(x)


def _dsilu(x):
    s = _sig(x)
    return s * (1.0 + x * (1.0 - s))


def _rstd(x):
    return lax.rsqrt(jnp.mean(x * x, axis=-1, keepdims=True) + NORM_EPS)


def _dot(a, b, dims, **kw):
    return lax.dot_general(a, b, (dims, ((), ())), preferred_element_type=F32, **kw)


NN = ((1,), (0,))
NT = ((1,), (1,))
TN = ((0,), (0,))


def _iota(shape, dim):
    return lax.broadcasted_iota(jnp.int32, shape, dim)


def _lbs_fwd(lower_bounds):
    def body(a_ref, o_ref):
        a = a_ref[...]
        a0, a1 = a[0:1], a[1:2]
        m = jnp.maximum(a0, a1)
        e0, e1 = jnp.exp(a0 - m), jnp.exp(a1 - m)
        p0, p1 = e0 / (e0 + e1), e1 / (e0 + e1)
        o_ref[...] = jnp.concatenate([p0 - p0, (p0 + p1) - p0], axis=0)

    return pl.pallas_call(body, out_shape=jax.ShapeDtypeStruct(lower_bounds.shape, F32), name="lbs_fwd")(lower_bounds)


def _inproj_fwd(x2, g_row, w_int, name):
    n, d = x2.shape
    e = w_int.shape[1]
    tm = min(256, n)

    def body(x_ref, g_ref, w_ref, o_ref):
        x = x_ref[...]
        h = (x * _rstd(x) * g_ref[...]).astype(BF16)
        o_ref[...] = jnp.dot(h, w_ref[...], preferred_element_type=F32)

    return pl.pallas_call(
        body, grid=(n // tm,),
        in_specs=[pl.BlockSpec((tm, d), lambda i: (i, 0)), pl.BlockSpec((1, d), lambda i: (0, 0)),
                  pl.BlockSpec((d, e), lambda i: (0, 0))],
        out_specs=pl.BlockSpec((tm, e), lambda i: (i, 0)),
        out_shape=jax.ShapeDtypeStruct((n, e), F32),
        compiler_params=_cparams(("parallel",)), name=name)(x2, g_row, w_int)


def _chunk_cumsum_matrix():
    i, j = _iota((LANES, LANES), 0), _iota((LANES, LANES), 1)
    return ((i <= j) & ((i // CHUNK) == (j // CHUNK))).astype(F32)


def _hgrn_gates(a, lb):
    qa, z = a[:, 0:128], a[:, 128:256]
    sg, sgn = _sig(z), _sig(-z)
    fg = lb + (1.0 - lb) * sg
    lf = jnp.log(jnp.maximum(fg, TINY))
    kk = (1.0 - lb) * sgn
    return qa * _sig(qa), kk, lf, sg, sgn, fg


def _hgrn_fwd(proj3, lbs_row, gn_col, name):
    bsz, t, _ = proj3.shape
    nt = t // LANES

    def body(a_ref, lb_ref, gn_ref, og_ref, or_ref):
        lb = lb_ref[...]
        gn = gn_ref[...]
        umat = _chunk_cumsum_matrix()
        lane64 = _iota((1, LANES), 1) % CHUNK

        def tile(i, carry):
            r0 = pl.multiple_of(i * LANES, LANES)
            a = a_ref[pl.ds(r0, LANES), :]
            qq, kk, lf, _, _, _ = _hgrn_gates(a, lb)
            va, ga = a[:, 256:384], a[:, 384:512]
            q_t, k_t, v_t = qq.T, kk.T, va.T
            b_t = jnp.dot(lf.T, umat, precision=HI, preferred_element_type=F32)
            new_s, o_heads = [], []
            for h in range(2):
                s_h = carry[h]
                rs = slice(CHUNK * h, CHUNK * (h + 1))
                qh, kh, vh, bh = q_t[rs], k_t[rs], v_t[rs], b_t[rs]
                inter = []
                for c in range(2):
                    cs = slice(CHUNK * c, CHUNK * (c + 1))
                    b_ = bh[:, cs]
                    qt = (qh[:, cs] * jnp.exp(b_)).astype(BF16)
                    inter.append(_dot(s_h.astype(BF16), qt, TN))
                    bl = b_[:, CHUNK - 1:CHUNK]
                    kt = (kh[:, cs] * jnp.exp(bl - b_)).astype(BF16)
                    s_h = jnp.exp(bl) * s_h + _dot(kt, vh[:, cs].astype(BF16), NT)
                new_s.append(s_h)

                acc = jnp.concatenate(inter, axis=1) + jnp.sum(qh * kh, axis=0, keepdims=True) * vh
                for dlt in range(1, CHUNK):
                    kr, br, vr = pltpu.roll(kh, dlt, 1), pltpu.roll(bh, dlt, 1), pltpu.roll(vh, dlt, 1)
                    e = jnp.exp(jnp.minimum(bh - br, 0.0))
                    att = jnp.sum(qh * kr * e, axis=0, keepdims=True)
                    acc = acc + jnp.where(lane64 >= dlt, att, 0.0) * vr
                o_heads.append(acc)
            normed = []
            for h in range(2):
                o_h = o_heads[h]
                ms = jnp.mean(o_h * o_h, axis=0, keepdims=True)
                normed.append(o_h * lax.rsqrt(ms + NORM_EPS) * gn[CHUNK * h:CHUNK * (h + 1)])
            or_ref[pl.ds(r0, LANES), :] = jnp.concatenate(o_heads, axis=0).T
            og_ref[pl.ds(r0, LANES), :] = jnp.concatenate(normed, axis=0).T * _silu(ga)
            return tuple(new_s)

        zero = jnp.zeros((CHUNK, CHUNK), F32)
        lax.fori_loop(0, nt, tile, (zero, zero))

    out = jax.ShapeDtypeStruct((bsz, t, HGRN_W), F32)
    return pl.pallas_call(
        body, grid=(bsz, 2),
        in_specs=[pl.BlockSpec((None, t, 512), lambda b, p: (b, 0, p)),
                  pl.BlockSpec((1, 128), lambda b, p: (0, p)),
                  pl.BlockSpec((128, 1), lambda b, p: (p, 0))],
        out_specs=[pl.BlockSpec((None, t, 128), lambda b, p: (b, 0, p)),
                   pl.BlockSpec((None, t, 128), lambda b, p: (b, 0, p))],
        out_shape=[out, out],
        compiler_params=_cparams(("parallel", "parallel")), name=name)(proj3, lbs_row, gn_col)


def _hgrn_bwd(proj3, o_raw, dmixed, lbs_row, gn_row, name):
    bsz, t, _ = proj3.shape
    nt = t // LANES
    nchunk = t // CHUNK

    def body(a_ref, or_ref, do_ref, lb_ref, gn_ref, da_ref, dgn_ref, dlb_ref, s_sc):
        lb = lb_ref[...]
        gn = gn_ref[...]
        umat = _chunk_cumsum_matrix()
        lane = _iota((1, LANES), 1)
        lane64 = lane % CHUNK
        half = lane < CHUNK

        def t_layout(a):
            qq, kk, lf, sg, sgn, fg = _hgrn_gates(a, lb)
            b_t = jnp.dot(lf.T, umat, precision=HI, preferred_element_type=F32)
            return qq.T, kk.T, a[:, 256:384].T, b_t, (sg, sgn, fg)

        def fwd_tile(i, carry):
            r0 = pl.multiple_of(i * LANES, LANES)
            q_t, k_t, v_t, b_t, _ = t_layout(a_ref[pl.ds(r0, LANES), :])
            new_s = []
            for h in range(2):
                s_h = carry[h]
                rs = slice(CHUNK * h, CHUNK * (h + 1))
                for c in range(2):
                    cs = slice(CHUNK * c, CHUNK * (c + 1))
                    s_sc[h, 2 * i + c] = s_h
                    b_ = b_t[rs, cs]
                    bl = b_[:, CHUNK - 1:CHUNK]
                    kt = (k_t[rs, cs] * jnp.exp(bl - b_)).astype(BF16)
                    s_h = jnp.exp(bl) * s_h + _dot(kt, v_t[rs, cs].astype(BF16), NT)
                new_s.append(s_h)
            return tuple(new_s)

        zero = jnp.zeros((CHUNK, CHUNK), F32)
        lax.fori_loop(0, nt, fwd_tile, (zero, zero))

        def half_mean(v):
            m0 = jnp.sum(jnp.where(half, v, 0.0), axis=1, keepdims=True) * (1.0 / CHUNK)
            m1 = jnp.sum(jnp.where(half, 0.0, v), axis=1, keepdims=True) * (1.0 / CHUNK)
            return jnp.where(half, m0, m1)

        def bwd_tile(k, carry):
            ds0, ds1, dgn_acc, dlb_acc = carry
            i = nt - 1 - k
            r0 = pl.multiple_of(i * LANES, LANES)
            a = a_ref[pl.ds(r0, LANES), :]
            qa, z, ga = a[:, 0:128], a[:, 128:256], a[:, 384:512]
            q_t, k_t, v_t, b_t, (sg, sgn, fg) = t_layout(a)
            oraw = or_ref[pl.ds(r0, LANES), :]
            dout = do_ref[pl.ds(r0, LANES), :]
            r = lax.rsqrt(half_mean(oraw * oraw) + NORM_EPS)
            xn = oraw * r
            dga = dout * (xn * gn) * _dsilu(ga)
            don = dout * _silu(ga)
            dgn_acc = dgn_acc + jnp.sum(don * xn, axis=0, keepdims=True)
            dxn = don * gn
            do_t = (r * (dxn - xn * half_mean(dxn * xn))).T
            new_ds, dq_h, dk_h, dv_h, db_h = [], [], [], [], []
            for h in range(2):
                ds_h = (ds0, ds1)[h]
                rs = slice(CHUNK * h, CHUNK * (h + 1))
                qh, kh, vh, bh, doh = q_t[rs], k_t[rs], v_t[rs], b_t[rs], do_t[rs]
                dq_c, dk_c, dv_c, dbl_c = [None, None], [None, None], [None, None], [None, None]
                for c in (1, 0):
                    cs = slice(CHUNK * c, CHUNK * (c + 1))
                    s_n = s_sc[h, 2 * i + c]
                    b_ = bh[:, cs]
                    eb = jnp.exp(b_)
                    bl = b_[:, CHUNK - 1:CHUNK]
                    ek = jnp.exp(bl - b_)
                    ebl = jnp.exp(bl)
                    qt, kt = qh[:, cs] * eb, kh[:, cs] * ek
                    do_c = doh[:, cs].astype(BF16)
                    dsb = ds_h.astype(BF16)
                    dv_c[c] = _dot(dsb, kt.astype(BF16), TN)
                    dkt = _dot(dsb, vh[:, cs].astype(BF16), NN)
                    dqt = _dot(s_n.astype(BF16), do_c, NN)
                    dbl_c[c] = jnp.sum(ds_h * s_n, axis=1, keepdims=True) * ebl + jnp.sum(dkt * kt, axis=1, keepdims=True)
                    dq_c[c], dk_c[c] = dqt * eb, dkt * ek
                    ds_h = ebl * ds_h + _dot(qt.astype(BF16), do_c, NT)
                new_ds.append(ds_h)

                att0 = jnp.sum(qh * kh, axis=0, keepdims=True)
                datt0 = jnp.sum(doh * vh, axis=0, keepdims=True)
                dqh = jnp.concatenate(dq_c, axis=1) + datt0 * kh
                dkh = jnp.concatenate(dk_c, axis=1) + datt0 * qh
                dvh = jnp.concatenate(dv_c, axis=1) + att0 * doh
                for dlt in range(1, CHUNK):
                    kr, br, vr = pltpu.roll(kh, dlt, 1), pltpu.roll(bh, dlt, 1), pltpu.roll(vh, dlt, 1)
                    e = jnp.where(lane64 >= dlt, jnp.exp(jnp.minimum(bh - br, 0.0)), 0.0)
                    qe = qh * e
                    att = jnp.sum(qe * kr, axis=0, keepdims=True)
                    datt = jnp.sum(doh * vr, axis=0, keepdims=True)
                    dqh = dqh + datt * (kr * e)
                    dkh = dkh + pltpu.roll(datt * qe, LANES - dlt, 1)
                    dvh = dvh + pltpu.roll(att * doh, LANES - dlt, 1)
                dbl = jnp.where(half, dbl_c[0], dbl_c[1])
                db_h.append(qh * dqh - kh * dkh + jnp.where(lane64 == CHUNK - 1, dbl, 0.0))
                dq_h.append(dqh)
                dk_h.append(dkh)
                dv_h.append(dvh)
            dqq = jnp.concatenate(dq_h, axis=0).T
            dkk = jnp.concatenate(dk_h, axis=0).T
            dvv = jnp.concatenate(dv_h, axis=0).T
            dlf = _dot(jnp.concatenate(db_h, axis=0), umat, NT, precision=HI).T
            dqa = dqq * _dsilu(qa)
            dfg = jnp.where(fg > TINY, dlf / fg, 0.0)
            dz = (dfg - dkk) * (1.0 - lb) * sg * sgn
            dlb_acc = dlb_acc + jnp.sum(dfg * (1.0 - sg) - dkk * sgn, axis=0, keepdims=True)
            da_ref[pl.ds(r0, LANES), :] = jnp.concatenate([dqa, dz, dvv, dga], axis=1)
            return new_ds[0], new_ds[1], dgn_acc, dlb_acc

        zrow = jnp.zeros((1, LANES), F32)
        _, _, dgn_acc, dlb_acc = lax.fori_loop(0, nt, bwd_tile, (zero, zero, zrow, zrow))
        dgn_ref[...] = jnp.broadcast_to(dgn_acc, (8, LANES))
        dlb_ref[...] = jnp.broadcast_to(dlb_acc, (8, LANES))

    rows = jax.ShapeDtypeStruct((bsz, 8, HGRN_W), F32)
    return pl.pallas_call(
        body, grid=(bsz, 2),
        in_specs=[pl.BlockSpec((None, t, 512), lambda b, p: (b, 0, p)),
                  pl.BlockSpec((None, t, 128), lambda b, p: (b, 0, p)),
                  pl.BlockSpec((None, t, 128), lambda b, p: (b, 0, p)),
                  pl.BlockSpec((1, 128), lambda b, p: (0, p)),
                  pl.BlockSpec((1, 128), lambda b, p: (0, p))],
        out_specs=[pl.BlockSpec((None, t, 512), lambda b, p: (b, 0, p)),
                   pl.BlockSpec((None, 8, 128), lambda b, p: (b, 0, p)),
                   pl.BlockSpec((None, 8, 128), lambda b, p: (b, 0, p))],
        out_shape=[jax.ShapeDtypeStruct((bsz, t, A_W), F32), rows, rows],
        scratch_shapes=[pltpu.VMEM((2, nchunk, CHUNK, CHUNK), F32)],
        compiler_params=_cparams(("parallel", "parallel")), name=name)(proj3, o_raw, dmixed, lbs_row, gn_row)


N_LEVELS = 6


def _hgrn_tables():
    t = np.arange(LANES)
    j = np.arange(LANES)[None, :]
    same_chunk = (t[:, None] // CHUNK) == (j // CHUNK)
    w = np.zeros((2 + N_LEVELS, LANES, LANES), np.float32)
    w[0] = same_chunk & (j <= t[:, None])
    w[1] = same_chunk & (j > t[:, None])
    maskf = np.zeros((N_LEVELS, LANES, LANES), np.float32)
    rightf = np.zeros((N_LEVELS, LANES, LANES), np.float32)
    for li in range(N_LEVELS):
        m = (CHUNK // 2) >> li
        start = t - (t % (2 * m))
        right = (t % (2 * m)) >= m
        first = np.where(right, start + m, t + 1)
        last = np.where(right, t, start + m - 1)
        w[2 + li] = (j >= first[:, None]) & (j <= last[:, None])
        maskf[li] = (t[:, None] // (2 * m)) == (j // (2 * m))
        rightf[li] = right[:, None]
    return jnp.asarray(w.reshape(-1, LANES), BF16), jnp.asarray(maskf), jnp.asarray(rightf)


def _split(x, n):
    parts = []
    for _ in range(n - 1):
        p = x.astype(BF16)
        parts.append(p)
        x = x - p.astype(F32)
    parts.append(x.astype(BF16))
    return parts


def _exact_dot(w, parts):
    acc = jnp.dot(w, parts[0], preferred_element_type=F32)
    for p in parts[1:]:
        acc = acc + jnp.dot(w, p, preferred_element_type=F32)
    return acc


def _head_sums(v, ones_blk, n=2):
    parts = _split(v, n)
    acc = jnp.dot(parts[0], ones_blk, preferred_element_type=F32)
    for p in parts[1:]:
        acc = acc + jnp.dot(p, ones_blk, preferred_element_type=F32)
    return acc


def _hgrn_consts():
    r, c = _iota((LANES, LANES), 0), _iota((LANES, LANES), 1)
    eye = r == c
    ones_blk = ((r // CHUNK) == (c // CHUNK)).astype(BF16)
    return eye, ones_blk, jnp.ones((CHUNK, LANES), BF16)


def _hgrn_levels(qq, kk, zall, mk_ref, rt_ref, d_att=None):
    att = [jnp.zeros((LANES, LANES), F32)] * 2
    dq = dk = db = jnp.zeros((LANES, LANES), F32)
    for li in range(N_LEVELS):
        e = jnp.exp(zall[(2 + li) * LANES:(3 + li) * LANES])
        rt = rt_ref[li]
        mk = mk_ref[li]
        qef, kef = e * rt, e * (1.0 - rt)
        qe, ke = (qq * qef).astype(BF16), (kk * kef).astype(BF16)
        dqs, dks = [], []
        for h in range(2):
            hs = slice(CHUNK * h, CHUNK * (h + 1))
            att[h] = att[h] + _dot(qe[:, hs], ke[:, hs], NT) * mk
            if d_att is not None:
                dam = (d_att[h] * mk).astype(BF16)
                dqs.append(jnp.dot(dam, ke[:, hs], preferred_element_type=F32))
                dks.append(_dot(dam, qe[:, hs], TN))
        if d_att is not None:
            dqe, dke = jnp.concatenate(dqs, axis=1), jnp.concatenate(dks, axis=1)
            dq = dq + dqe * qef
            dk = dk + dke * kef
            db = db + (dqe * qe.astype(F32) - dke * ke.astype(F32))
    return att, dq, dk, db


def _hgrn_fwd(proj3, lbs_row, gn_row, name):
    bsz, t, _ = proj3.shape
    nt = t // LANES
    w_all, maskf, rightf = _hgrn_tables()

    def body(a_ref, lb_ref, gn_ref, w_ref, mk_ref, rt_ref, og_ref, or_ref, st_ref):
        lb = lb_ref[...]
        gn = gn_ref[...]
        eye, ones_blk, ones_h = _hgrn_consts()

        def tile(i, carry):
            r0 = pl.multiple_of(i * LANES, LANES)
            a = a_ref[pl.ds(r0, LANES), :]
            qq, kk, lf, _, _, _ = _hgrn_gates(a, lb)
            va, ga = a[:, 256:384], a[:, 384:512]
            parts = _split(lf, 3)
            zall = _exact_dot(w_ref[...], parts)
            eb, ee = jnp.exp(zall[0:LANES]), jnp.exp(zall[LANES:2 * LANES])
            vb = va.astype(BF16)
            att, _, _, _ = _hgrn_levels(qq, kk, zall, mk_ref, rt_ref)
            qk = _split(qq * kk, 2)
            qeb, keb = (qq * eb).astype(BF16), (kk * ee).astype(BF16)
            new_s, o_heads = [], []
            for h in range(2):
                hs = slice(CHUNK * h, CHUNK * (h + 1))
                diag = _exact_dot_r(qk, hs, ones_h)
                a_h = att[h] + jnp.where(eye, diag, 0.0)
                o_h = jnp.dot(a_h.astype(BF16), vb[:, hs], preferred_element_type=F32)
                st = carry[h]
                chunks = []
                for c in range(2):
                    rc = slice(CHUNK * c, CHUNK * (c + 1))
                    st_ref[h, 2 * i + c] = st
                    chunks.append(o_h[rc] + _dot(qeb[rc, hs], st.astype(BF16), NT))
                    ebl = eb[CHUNK * (c + 1) - 1:CHUNK * (c + 1), hs]
                    st = st * ebl + _dot(vb[rc, hs], keb[rc, hs], TN)
                new_s.append(st)
                o_heads.append(jnp.concatenate(chunks, axis=0))
            o = jnp.concatenate(o_heads, axis=1)
            ms = _head_sums(o * o, ones_blk) * (1.0 / CHUNK)
            or_ref[pl.ds(r0, LANES), :] = o
            og_ref[pl.ds(r0, LANES), :] = o * lax.rsqrt(ms + NORM_EPS) * gn * _silu(ga)
            return tuple(new_s)

        zero = jnp.zeros((CHUNK, CHUNK), F32)
        lax.fori_loop(0, nt, tile, (zero, zero))

    out = jax.ShapeDtypeStruct((bsz, t, HGRN_W), F32)
    row = pl.BlockSpec((1, 128), lambda b, p: (0, p))
    return pl.pallas_call(
        body, grid=(bsz, 2),
        in_specs=[pl.BlockSpec((None, t, 512), lambda b, p: (b, 0, p)), row, row,
                  pl.BlockSpec(w_all.shape, lambda b, p: (0, 0)),
                  pl.BlockSpec(maskf.shape, lambda b, p: (0, 0, 0)),
                  pl.BlockSpec(rightf.shape, lambda b, p: (0, 0, 0))],
        out_specs=[pl.BlockSpec((None, t, 128), lambda b, p: (b, 0, p)),
                   pl.BlockSpec((None, t, 128), lambda b, p: (b, 0, p)),
                   pl.BlockSpec((None, 2, t // CHUNK, CHUNK, CHUNK), lambda b, p: (b, p, 0, 0, 0))],
        out_shape=[out, out, jax.ShapeDtypeStruct((bsz, 4, t // CHUNK, CHUNK, CHUNK), F32)],
        compiler_params=_cparams(("parallel", "parallel")), name=name)(proj3, lbs_row, gn_row, w_all, maskf, rightf)


def _exact_dot_r(parts, hs, ones_h):
    acc = jnp.dot(parts[0][:, hs], ones_h, preferred_element_type=F32)
    for p in parts[1:]:
        acc = acc + jnp.dot(p[:, hs], ones_h, preferred_element_type=F32)
    return acc


def _hgrn_bwd(proj3, o_raw, dmixed, states, lbs_row, gn_row, name):
    bsz, t, _ = proj3.shape
    nt = t // LANES
    nchunk = t // CHUNK
    w_all, maskf, rightf = _hgrn_tables()

    def body(a_ref, or_ref, do_ref, s_sc, lb_ref, gn_ref, w_ref, mk_ref, rt_ref, da_ref, dgn_ref, dlb_ref):
        lb = lb_ref[...]
        gn = gn_ref[...]
        eye, ones_blk, ones_h = _hgrn_consts()
        r_i, c_i = _iota((LANES, LANES), 0), _iota((LANES, LANES), 1)
        suffix = ((c_i >= r_i) & ((r_i // CHUNK) == (c_i // CHUNK))).astype(BF16)
        row64 = _iota((LANES, CHUNK), 0)
        ones_t = jnp.ones((LANES, CHUNK), BF16)
        zero = jnp.zeros((CHUNK, CHUNK), F32)

        def bwd_tile(k, carry):
            dst0, dst1, dgn_acc, dlb_acc = carry
            i = nt - 1 - k
            r0 = pl.multiple_of(i * LANES, LANES)
            a = a_ref[pl.ds(r0, LANES), :]
            qa, ga = a[:, 0:128], a[:, 384:512]
            qq, kk, lf, sg, sgn, fg = _hgrn_gates(a, lb)
            parts = _split(lf, 3)
            zall = _exact_dot(w_ref[...], parts)
            eb, ee = jnp.exp(zall[0:LANES]), jnp.exp(zall[LANES:2 * LANES])
            vb = a[:, 256:384].astype(BF16)
            oraw = or_ref[pl.ds(r0, LANES), :]
            dout = do_ref[pl.ds(r0, LANES), :]
            r = lax.rsqrt(_head_sums(oraw * oraw, ones_blk) * (1.0 / CHUNK) + NORM_EPS)
            xn = oraw * r
            dga = dout * (xn * gn) * _dsilu(ga)
            don = dout * _silu(ga)
            dgn_acc = dgn_acc + jnp.sum(don * xn, axis=0, keepdims=True)
            dxn = don * gn
            do = r * (dxn - xn * (_head_sums(dxn * xn, ones_blk) * (1.0 / CHUNK)))
            dob = do.astype(BF16)
            d_att = [_dot(dob[:, CHUNK * h:CHUNK * (h + 1)], vb[:, CHUNK * h:CHUNK * (h + 1)], NT) for h in range(2)]
            att, dq, dk, db_lv = _hgrn_levels(qq, kk, zall, mk_ref, rt_ref, d_att)
            qk = _split(qq * kk, 2)
            qe_f, ke_f = qq * eb, kk * ee
            qeb, keb = qe_f.astype(BF16), ke_f.astype(BF16)
            new_ds, dq_h, dk_h, dv_h, dbl_h = [], [], [], [], []
            for h in range(2):
                hs = slice(CHUNK * h, CHUNK * (h + 1))
                a_h = att[h] + jnp.where(eye, _exact_dot_r(qk, hs, ones_h), 0.0)
                dv = _dot(a_h.astype(BF16), dob[:, hs], TN)
                ddiag = _exact_dot_r(_split(jnp.where(eye, d_att[h], 0.0), 2), slice(None), ones_t)
                dq_i = dq[:, hs] + ddiag * kk[:, hs]
                dk_i = dk[:, hs] + ddiag * qq[:, hs]
                dst = (dst0, dst1)[h]
                dq_c, dk_c, dv_c, dbl_c = [None, None], [None, None], [None, None], [None, None]
                for c in (1, 0):
                    rc = slice(CHUNK * c, CHUNK * (c + 1))
                    st_n = s_sc[h, 2 * i + c]
                    ebl = eb[CHUNK * (c + 1) - 1:CHUNK * (c + 1), hs]
                    dstb = dst.astype(BF16)
                    dv_c[c] = _dot(keb[rc, hs], dstb, NT)
                    dke = jnp.dot(vb[rc, hs], dstb, preferred_element_type=F32)
                    dqe = jnp.dot(dob[rc, hs], st_n.astype(BF16), preferred_element_type=F32)
                    dbl_c[c] = (jnp.sum(dst * st_n, axis=0, keepdims=True) * ebl
                                + jnp.sum(dke * ke_f[rc, hs], axis=0, keepdims=True))
                    dq_c[c], dk_c[c] = dqe * eb[rc, hs], dke * ee[rc, hs]
                    dst = dst * ebl + _dot(dob[rc, hs], qeb[rc, hs], TN)
                new_ds.append(dst)
                dq_x, dk_x = jnp.concatenate(dq_c, axis=0), jnp.concatenate(dk_c, axis=0)
                dq_h.append(dq_i + dq_x)
                dk_h.append(dk_i + dk_x)
                dv_h.append(dv + jnp.concatenate(dv_c, axis=0))
                dbl_h.append(qq[:, hs] * dq_x - kk[:, hs] * dk_x
                             + jnp.where(row64 == CHUNK - 1, dbl_c[0], 0.0) + jnp.where(row64 == LANES - 1, dbl_c[1], 0.0))
            dqq = jnp.concatenate(dq_h, axis=1)
            dkk = jnp.concatenate(dk_h, axis=1)
            dvv = jnp.concatenate(dv_h, axis=1)
            db = db_lv + jnp.concatenate(dbl_h, axis=1)
            dlf = _exact_dot(suffix, _split(db, 3))
            dqa = dqq * _dsilu(qa)
            dfg = jnp.where(fg > TINY, dlf / fg, 0.0)
            dz = (dfg - dkk) * (1.0 - lb) * sg * sgn
            dlb_acc = dlb_acc + jnp.sum(dfg * (1.0 - sg) - dkk * sgn, axis=0, keepdims=True)
            da_ref[pl.ds(r0, LANES), :] = jnp.concatenate([dqa, dz, dvv, dga], axis=1)
            return new_ds[0], new_ds[1], dgn_acc, dlb_acc

        zrow = jnp.zeros((1, LANES), F32)
        _, _, dgn_acc, dlb_acc = lax.fori_loop(0, nt, bwd_tile, (zero, zero, zrow, zrow))
        dgn_ref[...] = jnp.broadcast_to(dgn_acc, (8, LANES))
        dlb_ref[...] = jnp.broadcast_to(dlb_acc, (8, LANES))

    rows = jax.ShapeDtypeStruct((bsz, 8, HGRN_W), F32)
    row = pl.BlockSpec((1, 128), lambda b, p: (0, p))
    blk = pl.BlockSpec((None, t, 128), lambda b, p: (b, 0, p))
    return pl.pallas_call(
        body, grid=(bsz, 2),
        in_specs=[pl.BlockSpec((None, t, 512), lambda b, p: (b, 0, p)), blk, blk,
                  pl.BlockSpec((None, 2, nchunk, CHUNK, CHUNK), lambda b, p: (b, p, 0, 0, 0)), row, row,
                  pl.BlockSpec(w_all.shape, lambda b, p: (0, 0)),
                  pl.BlockSpec(maskf.shape, lambda b, p: (0, 0, 0)),
                  pl.BlockSpec(rightf.shape, lambda b, p: (0, 0, 0))],
        out_specs=[pl.BlockSpec((None, t, 512), lambda b, p: (b, 0, p)),
                   pl.BlockSpec((None, 8, 128), lambda b, p: (b, 0, p)),
                   pl.BlockSpec((None, 8, 128), lambda b, p: (b, 0, p))],
        out_shape=[jax.ShapeDtypeStruct((bsz, t, A_W), F32), rows, rows],
        compiler_params=_cparams(("parallel", "parallel")), name=name)(
            proj3, o_raw, dmixed, states, lbs_row, gn_row, w_all, maskf, rightf)


def _pool_tt(t):
    return min(256, t)


def _window_select(s2, s4, s8, s16, lane):
    return jnp.where(lane < 64, s2, jnp.where(lane < 128, s4, jnp.where(lane < 192, s8, s16)))


def _pool_counts(t0, tt):
    lane = _iota((tt, POOL_W), 1)
    tpos = (_iota((tt, POOL_W), 0) + t0 + 1).astype(F32)
    win = jnp.where(lane < 64, 2.0, jnp.where(lane < 128, 4.0, jnp.where(lane < 192, 8.0, 16.0)))
    return 1.0 / jnp.minimum(tpos, win), lane


def _pooled_tile(upad_ref, i, tt):
    r0 = pl.multiple_of(i * tt, 8)
    cat = upad_ref[pl.ds(r0, tt + POOL_HALO), :]
    s2 = cat + pltpu.roll(cat, 1, 0)
    s4 = s2 + pltpu.roll(s2, 2, 0)
    s8 = s4 + pltpu.roll(s4, 4, 0)
    s16 = s8 + pltpu.roll(s8, 8, 0)
    inv, lane = _pool_counts(i * tt, tt)
    sel = _window_select(s2[POOL_HALO:], s4[POOL_HALO:], s8[POOL_HALO:], s16[POOL_HALO:], lane)
    return sel * inv - cat[POOL_HALO:], inv, lane


def _pool_fwd(proj3, wbd, scale_row, name):
    bsz, t, _ = proj3.shape
    tt = _pool_tt(t)

    def body(p_ref, w_ref, sc_ref, o_ref, upad):
        upad[0:POOL_HALO, :] = jnp.zeros((POOL_HALO, POOL_W), F32)
        upad[POOL_HALO:, :] = p_ref[:, 0:POOL_W]
        w = w_ref[...]
        sc = sc_ref[...]

        def tile(i, c):
            pooled, _, _ = _pooled_tile(upad, i, tt)
            r0 = pl.multiple_of(i * tt, 8)
            g = p_ref[pl.ds(r0, tt), POOL_W:2 * POOL_W]
            pre = jnp.dot(pooled.astype(BF16), w, preferred_element_type=F32)
            o_ref[pl.ds(r0, tt), :] = pre * sc * _silu(g)
            return c

        lax.fori_loop(0, t // tt, tile, 0)

    return pl.pallas_call(
        body, grid=(bsz,),
        in_specs=[pl.BlockSpec((None, t, 512), lambda b: (b, 0, B_BLK)),
                  pl.BlockSpec((POOL_W, POOL_W), lambda b: (0, 0)),
                  pl.BlockSpec((1, POOL_W), lambda b: (0, 0))],
        out_specs=pl.BlockSpec((None, t, POOL_W), lambda b: (b, 0, 0)),
        out_shape=jax.ShapeDtypeStruct((bsz, t, POOL_W), F32),
        scratch_shapes=[pltpu.VMEM((t + POOL_HALO, POOL_W), F32)],
        compiler_params=_cparams(("parallel",)), name=name)(proj3, wbd, scale_row)


def _pool_bwd(proj3, dmixed, wbd, scale_row, name):
    bsz, t, _ = proj3.shape
    tt = _pool_tt(t)

    def body(p_ref, do_ref, w_ref, sc_ref, db_ref, dsc_ref, dw_ref, upad, epad):
        upad[0:POOL_HALO, :] = jnp.zeros((POOL_HALO, POOL_W), F32)
        upad[POOL_HALO:, :] = p_ref[:, 0:POOL_W]
        epad[t:, :] = jnp.zeros((POOL_HALO, POOL_W), F32)
        w = w_ref[...]
        sc = sc_ref[...]

        def tile(i, carry):
            dsc_acc, dw_acc = carry
            pooled, inv, _ = _pooled_tile(upad, i, tt)
            r0 = pl.multiple_of(i * tt, 8)
            g = p_ref[pl.ds(r0, tt), POOL_W:2 * POOL_W]
            dout = do_ref[pl.ds(r0, tt), :]
            pb = pooled.astype(BF16)
            pre = jnp.dot(pb, w, preferred_element_type=F32)
            t1 = dout * _silu(g)
            dsc_acc = dsc_acc + jnp.sum(t1 * pre, axis=0, keepdims=True)
            dpre = (t1 * sc).astype(BF16)
            db_ref[pl.ds(r0, tt), POOL_W:2 * POOL_W] = dout * pre * sc * _dsilu(g)
            dw_acc = dw_acc + _dot(pb, dpre, TN)
            dpooled = _dot(dpre, w, NT)
            epad[pl.ds(r0, tt), :] = dpooled * inv
            return dsc_acc, dw_acc

        dsc_acc, dw_acc = lax.fori_loop(0, t // tt, tile, (jnp.zeros((1, POOL_W), F32), jnp.zeros((POOL_W, POOL_W), F32)))
        dsc_ref[...] = jnp.broadcast_to(dsc_acc, (8, POOL_W))
        dw_ref[...] = dw_acc

        def tile2(i, c):
            r0 = pl.multiple_of(i * tt, 8)
            n = tt + POOL_HALO
            cat = epad[pl.ds(r0, n), :]
            s2 = cat + pltpu.roll(cat, n - 1, 0)
            s4 = s2 + pltpu.roll(s2, n - 2, 0)
            s8 = s4 + pltpu.roll(s4, n - 4, 0)
            s16 = s8 + pltpu.roll(s8, n - 8, 0)
            inv, lane = _pool_counts(i * tt, tt)
            sel = _window_select(s2[:tt], s4[:tt], s8[:tt], s16[:tt], lane)
            db_ref[pl.ds(r0, tt), 0:POOL_W] = sel - cat[:tt] / inv
            return c

        lax.fori_loop(0, t // tt, tile2, 0)

    return pl.pallas_call(
        body, grid=(bsz,),
        in_specs=[pl.BlockSpec((None, t, 512), lambda b: (b, 0, B_BLK)),
                  pl.BlockSpec((None, t, POOL_W), lambda b: (b, 0, 1)),
                  pl.BlockSpec((POOL_W, POOL_W), lambda b: (0, 0)),
                  pl.BlockSpec((1, POOL_W), lambda b: (0, 0))],
        out_specs=[pl.BlockSpec((None, t, 512), lambda b: (b, 0, 0)),
                   pl.BlockSpec((None, 8, POOL_W), lambda b: (b, 0, 0)),
                   pl.BlockSpec((None, POOL_W, POOL_W), lambda b: (b, 0, 0))],
        out_shape=[jax.ShapeDtypeStruct((bsz, t, B_W), F32), jax.ShapeDtypeStruct((bsz, 8, POOL_W), F32),
                   jax.ShapeDtypeStruct((bsz, POOL_W, POOL_W), F32)],
        scratch_shapes=[pltpu.VMEM((t + POOL_HALO, POOL_W), F32), pltpu.VMEM((t + POOL_HALO, POOL_W), F32)],
        compiler_params=_cparams(("parallel",)), name=name)(proj3, dmixed, wbd, scale_row)


def _head_select_rows(hp):
    r, c = _iota((8, LANES), 0), _iota((8, LANES), 1)
    return ((r < 2) & (c == 2 * hp + r)).astype(F32)


def _foxgate_fwd(proj3, bias_row, name):
    bsz, t, _ = proj3.shape
    nt = t // LANES

    def body(f_ref, b_ref, cn_ref, ct_ref):
        bias = b_ref[...]
        i, j = _iota((LANES, LANES), 0), _iota((LANES, LANES), 1)
        lower = (j <= i).astype(F32)
        spread = (_iota((LANES, FOX_W), 0) == _iota((LANES, FOX_W), 1) // 64).astype(F32)

        def tile(k, carry):
            r0 = pl.multiple_of(k * LANES, LANES)
            xg = f_ref[pl.ds(r0, LANES), :] + bias
            lf = jnp.minimum(xg, 0.0) - jnp.log(1.0 + jnp.exp(-jnp.abs(xg)))
            c = jnp.dot(lower, lf, precision=HI, preferred_element_type=F32) + carry
            cn_ref[pl.ds(r0, LANES), :] = jnp.dot(c, spread, precision=HI, preferred_element_type=F32)
            for hp in range(4):
                ct_ref[hp, :, pl.ds(r0, LANES)] = _dot(_head_select_rows(hp), c, NT, precision=HI)
            return c[LANES - 1:LANES, :]

        lax.fori_loop(0, nt, tile, jnp.zeros((1, LANES), F32))

    return pl.pallas_call(
        body, grid=(bsz,),
        in_specs=[pl.BlockSpec((None, t, 128), lambda b: (b, 0, F_BLK)), pl.BlockSpec((1, 128), lambda b: (0, 0))],
        out_specs=[pl.BlockSpec((None, t, FOX_W), lambda b: (b, 0, 0)),
                   pl.BlockSpec((None, 4, 8, t), lambda b: (b, 0, 0, 0))],
        out_shape=[jax.ShapeDtypeStruct((bsz, t, FOX_W), F32), jax.ShapeDtypeStruct((bsz, 4, 8, t), F32)],
        compiler_params=_cparams(("parallel",)), name=name)(proj3, bias_row)


def _foxgate_bwd(proj3, dc_nat, bias_row, name):
    bsz, t, _ = proj3.shape
    nt = t // LANES

    def body(f_ref, dc_ref, b_ref, df_ref, dbias_ref, run_sc):
        bias = b_ref[...]
        i, j = _iota((LANES, LANES), 0), _iota((LANES, LANES), 1)
        upper = (j >= i).astype(F32)
        valid = _iota((1, LANES), 1) < FOX_HEADS
        run_sc[...] = jnp.zeros((8, LANES), F32)
        dbias_ref[...] = jnp.zeros((8, LANES), F32)

        def tile(k, c):
            r0 = pl.multiple_of((nt - 1 - k) * LANES, LANES)
            dc = dc_ref[pl.ds(r0, LANES), :] + jnp.where(i == LANES - 1, run_sc[0:1, :], 0.0)
            dlf = jnp.dot(upper, dc, precision=HI, preferred_element_type=F32)
            xg = f_ref[pl.ds(r0, LANES), :] + bias
            df = jnp.where(valid, dlf * _sig(-xg), 0.0)
            df_ref[pl.ds(r0, LANES), :] = df
            run_sc[...] = dlf[0:8, :]
            dbias_ref[...] += jnp.sum(df, axis=0, keepdims=True)
            return c

        lax.fori_loop(0, nt, tile, 0)

    blk = pl.BlockSpec((None, t, 128), lambda b: (b, 0, 0))
    return pl.pallas_call(
        body, grid=(bsz,),
        in_specs=[pl.BlockSpec((None, t, 128), lambda b: (b, 0, F_BLK)), blk, pl.BlockSpec((1, 128), lambda b: (0, 0))],
        out_specs=[blk, pl.BlockSpec((None, 8, 128), lambda b: (b, 0, 0))],
        out_shape=[jax.ShapeDtypeStruct((bsz, t, F_W), F32), jax.ShapeDtypeStruct((bsz, 8, 128), F32)],
        scratch_shapes=[pltpu.VMEM((8, LANES), F32)],
        compiler_params=_cparams(("parallel",)), name=name)(proj3, dc_nat, bias_row)


def _fox_tile(t):
    return min(256, t)


def _fox_fwd(proj3, c_nat, c_t, name):
    bsz, t, _ = proj3.shape
    tq = _fox_tile(t)
    tk = min(2 * tq, t)
    nq = t // tq

    def body(q_ref, kv_ref, cn_ref, ct_ref, og_ref, or_ref, lse_ref):
        i = pl.program_id(2)
        qblk = q_ref[...]
        first = _iota((1, 128), 1) < 64
        qv = qblk[:, 0:128] * 0.125
        qm = [jnp.where(first, qv, 0.0).astype(BF16), jnp.where(first, 0.0, qv).astype(BF16)]
        cqs = [cn_ref[:, 0:1], cn_ref[:, 64:65]]
        rows = _iota((tq, tk), 0) + i * tq

        def kv_step(j, carry, masked):
            c0 = pl.multiple_of(j * tk, tk)
            kb = kv_ref[pl.ds(c0, tk), 128:256].astype(BF16)
            vblk = kv_ref[pl.ds(c0, tk), 256:384]
            vx = [jnp.where(first, vblk, 1.0).astype(BF16), jnp.where(first, 1.0, vblk).astype(BF16)]
            new = []
            for h in range(2):
                m, acc = carry[2 * h], carry[2 * h + 1]
                s = _dot(qm[h], kb, NT) + (cqs[h] - ct_ref[h:h + 1, pl.ds(c0, tk)])
                if masked:
                    s = jnp.where(rows >= _iota((tq, tk), 1) + j * tk, s, MASK_VALUE)
                m_new = jnp.maximum(m, jnp.max(s, axis=1, keepdims=True))
                p = jnp.exp(s - m_new).astype(BF16)
                new += [m_new, jnp.exp(m - m_new) * acc + jnp.dot(p, vx[h], preferred_element_type=F32)]
            return tuple(new)

        init = (jnp.full((tq, 1), MASK_VALUE, F32), jnp.zeros((tq, 128), F32)) * 2
        n_full = (i * tq) // tk
        carry = lax.fori_loop(0, n_full, functools.partial(kv_step, masked=False), init)
        m0, acc0, m1, acc1 = kv_step(n_full, carry, True)
        l0, l1 = pltpu.roll(acc0, 64, 1), pltpu.roll(acc1, 64, 1)
        o = jnp.where(first, acc0 / l0, acc1 / l1)
        or_ref[...] = o
        og_ref[...] = o * _silu(qblk[:, 384:512])
        lse_ref[...] = jnp.where(first, m0 + jnp.log(l0), m1 + jnp.log(l1))

    out = jax.ShapeDtypeStruct((bsz, t, FOX_W), F32)
    blk = pl.BlockSpec((None, tq, 128), lambda b, p, i: (b, i, p))
    return pl.pallas_call(
        body, grid=(bsz, 4, nq),
        in_specs=[pl.BlockSpec((None, tq, 512), lambda b, p, i: (b, i, C_BLK0 + p)),
                  pl.BlockSpec((None, t, 512), lambda b, p, i: (b, 0, C_BLK0 + p)),
                  blk,
                  pl.BlockSpec((None, None, 8, t), lambda b, p, i: (b, p, 0, 0))],
        out_specs=[blk, blk, blk],
        out_shape=[out, out, out],
        compiler_params=_cparams(("parallel", "parallel", "arbitrary")), name=name)(proj3, proj3, c_nat, c_t)


def _fox_bwd(proj3, o_raw, dmixed, lse, c_nat, c_t, name):
    bsz, t, _ = proj3.shape
    tq = _fox_tile(t)
    nq = t // tq
    tk = min(2 * tq, t)
    ratio = tk // tq

    def body(a_ref, or_ref, do_ref, lse_ref, cn_ref, ct_ref, dc_out, dct_out, drow_out, dq_sc, do_sc, dl_sc):
        def prep(i, c):
            r0 = pl.multiple_of(i * tq, tq)
            g = a_ref[pl.ds(r0, tq), 384:512]
            dout = do_ref[pl.ds(r0, tq), :]
            o = or_ref[pl.ds(r0, tq), :]
            dc_out[pl.ds(r0, tq), 384:512] = dout * o * _dsilu(g)
            do = dout * _silu(g)
            do_sc[pl.ds(r0, tq), :] = do
            prod = do * o
            d0 = jnp.sum(prod[:, 0:64], axis=1, keepdims=True)
            d1 = jnp.sum(prod[:, 64:128], axis=1, keepdims=True)
            dl_sc[pl.ds(r0, tq), :] = jnp.concatenate([jnp.broadcast_to(d0, (tq, 64)), jnp.broadcast_to(d1, (tq, 64))], axis=1)
            dq_sc[pl.ds(r0, tq), :] = jnp.zeros((tq, 128), F32)
            drow_out[pl.ds(r0, tq), :] = jnp.zeros((tq, 128), F32)
            return c

        lax.fori_loop(0, nq, prep, 0)
        dct_out[...] = jnp.zeros((8, t), F32)

        first = _iota((1, 128), 1) < 64

        def heads(v):
            return [jnp.where(first, v, 0.0).astype(BF16), jnp.where(first, 0.0, v).astype(BF16)]

        def kv_tile(j, c):
            c0 = pl.multiple_of(j * tk, tk)
            kb = a_ref[pl.ds(c0, tk), 128:256].astype(BF16)
            vb = a_ref[pl.ds(c0, tk), 256:384].astype(BF16)
            cks = [ct_ref[h:h + 1, pl.ds(c0, tk)] for h in range(2)]

            def q_step(i, carry, diagonal):
                dk, dv, dcol0, dcol1 = carry
                r0 = pl.multiple_of(i * tq, tq)
                causal = _iota((tq, tk), 0) + i * tq >= _iota((tq, tk), 1) + j * tk
                qv = a_ref[pl.ds(r0, tq), 0:128] * 0.125
                do = do_sc[pl.ds(r0, tq), :]
                qb, dob = qv.astype(BF16), do.astype(BF16)
                qm, dom = heads(qv), heads(do)
                full, dcols, rsums = [], [], []
                for h in range(2):
                    lse_h = lse_ref[pl.ds(r0, tq), 64 * h:64 * h + 1]
                    dl_h = dl_sc[pl.ds(r0, tq), 64 * h:64 * h + 1]
                    cq = cn_ref[pl.ds(r0, tq), 64 * h:64 * h + 1]
                    p = jnp.exp(_dot(qm[h], kb, NT) + (cq - cks[h]) - lse_h)
                    if diagonal:
                        p = jnp.where(causal, p, 0.0)
                    ds = p * (_dot(dom[h], vb, NT) - dl_h)
                    dsb = ds.astype(BF16)
                    full.append((_dot(p.astype(BF16), dob, TN), _dot(dsb, qb, TN),
                                 jnp.dot(dsb, kb, preferred_element_type=F32)))
                    dcols.append(jnp.sum(ds, axis=0, keepdims=True))
                    rsums.append(jnp.broadcast_to(jnp.sum(ds, axis=1, keepdims=True), (tq, 128)))
                dq_sc[pl.ds(r0, tq), :] += jnp.where(first, full[0][2], full[1][2]) * 0.125
                drow_out[pl.ds(r0, tq), :] += jnp.where(first, rsums[0], rsums[1])
                return (dk + jnp.where(first, full[0][1], full[1][1]), dv + jnp.where(first, full[0][0], full[1][0]),
                        dcol0 - dcols[0], dcol1 - dcols[1])

            carry = (jnp.zeros((tk, 128), F32), jnp.zeros((tk, 128), F32), jnp.zeros((1, tk), F32), jnp.zeros((1, tk), F32))
            for r in range(ratio):
                carry = q_step(ratio * j + r, carry, True)
            dk, dv, dcol0, dcol1 = lax.fori_loop(ratio * (j + 1), nq, functools.partial(q_step, diagonal=False), carry)
            dct_out[0:1, pl.ds(c0, tk)] = dcol0
            dct_out[1:2, pl.ds(c0, tk)] = dcol1
            dc_out[pl.ds(c0, tk), 128:256] = dk
            dc_out[pl.ds(c0, tk), 256:384] = dv
            return c

        lax.fori_loop(0, t // tk, kv_tile, 0)
        dc_out[:, 0:128] = dq_sc[...]

    blk = pl.BlockSpec((None, t, 128), lambda b, p: (b, 0, p))
    return pl.pallas_call(
        body, grid=(bsz, 4),
        in_specs=[pl.BlockSpec((None, t, 512), lambda b, p: (b, 0, C_BLK0 + p)),
                  blk,
                  pl.BlockSpec((None, t, 128), lambda b, p: (b, 0, 4 + p)),
                  blk, blk,
                  pl.BlockSpec((None, None, 8, t), lambda b, p: (b, p, 0, 0))],
        out_specs=[pl.BlockSpec((None, t, 512), lambda b, p: (b, 0, p)),
                   pl.BlockSpec((None, None, 8, t), lambda b, p: (b, p, 0, 0)), blk],
        out_shape=[jax.ShapeDtypeStruct((bsz, t, C_W), F32), jax.ShapeDtypeStruct((bsz, 4, 8, t), F32),
                   jax.ShapeDtypeStruct((bsz, t, FOX_W), F32)],
        scratch_shapes=[pltpu.VMEM((t, 128), F32), pltpu.VMEM((t, 128), F32), pltpu.VMEM((t, 128), F32)],
        compiler_params=_cparams(("parallel", "parallel")), name=name)(proj3, o_raw, dmixed, lse, c_nat, c_t)


def _mix_tm(n):
    return min(512, n)


def _outproj_fwd(x2, oa, ob, oc, wo, g_row, name):
    n, d = x2.shape
    tm = _mix_tm(n)

    def body(x_ref, oa_ref, ob_ref, oc_ref, w_ref, g_ref, y_ref, xo_ref):
        y = (jnp.dot(oa_ref[...].astype(BF16), w_ref[0:256, :], preferred_element_type=F32)
             + jnp.dot(ob_ref[...].astype(BF16), w_ref[256:512, :], preferred_element_type=F32)
             + jnp.dot(oc_ref[...].astype(BF16), w_ref[512:1024, :], preferred_element_type=F32))
        y_ref[...] = y
        xo_ref[...] = x_ref[...] + y * _rstd(y) * g_ref[...]

    row = lambda w: pl.BlockSpec((tm, w), lambda i: (i, 0))
    out = jax.ShapeDtypeStruct((n, d), F32)
    return pl.pallas_call(
        body, grid=(n // tm,),
        in_specs=[row(d), row(256), row(256), row(512), pl.BlockSpec((d, d), lambda i: (0, 0)),
                  pl.BlockSpec((1, d), lambda i: (0, 0))],
        out_specs=[row(d), row(d)], out_shape=[out, out],
        compiler_params=_cparams(("parallel",)), name=name)(x2, oa, ob, oc, wo, g_row)


def _loss_head(x2, target2, name):
    n, d = x2.shape
    tm = _mix_tm(n)

    def body(x_ref, t_ref, dx_ref, l_ref):
        err = x_ref[...] - t_ref[...]
        dx_ref[...] = err * (1.0 / d)

        @pl.when(pl.program_id(0) == 0)
        def _():
            l_ref[...] = jnp.zeros((8, 128), F32)

        l_ref[...] += jnp.sum(err * err)

    row = pl.BlockSpec((tm, d), lambda i: (i, 0))
    return pl.pallas_call(
        body, grid=(n // tm,), in_specs=[row, row],
        out_specs=[row, pl.BlockSpec((8, 128), lambda i: (0, 0))],
        out_shape=[jax.ShapeDtypeStruct((n, d), F32), jax.ShapeDtypeStruct((8, 128), F32)],
        compiler_params=_cparams(("arbitrary",)), name=name)(x2, target2)


def _outproj_bwd(dxo, y, oa, ob, oc, wo, g_row, name):
    n, d = dxo.shape
    tm = _mix_tm(n)

    def body(dx_ref, y_ref, oa_ref, ob_ref, oc_ref, w_ref, g_ref, dm_ref, dw_ref, dg_ref):
        @pl.when(pl.program_id(0) == 0)
        def _():
            dw_ref[...] = jnp.zeros((d, d), F32)
            dg_ref[...] = jnp.zeros((8, d), F32)

        yv, dx = y_ref[...], dx_ref[...]
        r = _rstd(yv)
        yn = yv * r
        dg_ref[...] += jnp.sum(dx * yn, axis=0, keepdims=True)
        dyn = dx * g_ref[...]
        dy = (r * (dyn - yn * jnp.mean(dyn * yn, axis=-1, keepdims=True))).astype(BF16)
        dm_ref[...] = _dot(dy, w_ref[...], NT)
        dw_ref[0:256, :] += _dot(oa_ref[...].astype(BF16), dy, TN)
        dw_ref[256:512, :] += _dot(ob_ref[...].astype(BF16), dy, TN)
        dw_ref[512:1024, :] += _dot(oc_ref[...].astype(BF16), dy, TN)

    row = lambda w: pl.BlockSpec((tm, w), lambda i: (i, 0))
    fixed = lambda r, c: pl.BlockSpec((r, c), lambda i: (0, 0))
    return pl.pallas_call(
        body, grid=(n // tm,),
        in_specs=[row(d), row(d), row(256), row(256), row(512), fixed(d, d), fixed(1, d)],
        out_specs=[row(d), fixed(d, d), fixed(8, d)],
        out_shape=[jax.ShapeDtypeStruct((n, d), F32), jax.ShapeDtypeStruct((d, d), F32), jax.ShapeDtypeStruct((8, d), F32)],
        compiler_params=_cparams(("arbitrary",)), name=name)(dxo, y, oa, ob, oc, wo, g_row)


_PIECES = ((0, A_W), (A_W, B_W), (A_W + B_W, C_W), (A_W + B_W + C_W, F_W))


def _inproj_bwd_x(x2, dxo, g_row, w_int, pieces, name):
    n, d = x2.shape
    tm = min(256, n)

    def body(x_ref, dxo_ref, g_ref, w_ref, da_ref, db_ref, dc_ref, df_ref, dx_ref, dg_ref):
        @pl.when(pl.program_id(0) == 0)
        def _():
            dg_ref[...] = jnp.zeros((8, d), F32)

        dh = jnp.zeros((tm, d), F32)
        for ref, (o, w) in zip((da_ref, db_ref, dc_ref, df_ref), _PIECES):
            dh = dh + _dot(ref[...].astype(BF16), w_ref[:, o:o + w], NT)
        x = x_ref[...]
        r = _rstd(x)
        xn = x * r
        dg_ref[...] += jnp.sum(dh * xn, axis=0, keepdims=True)
        dxn = dh * g_ref[...]
        dx_ref[...] = dxo_ref[...] + r * (dxn - xn * jnp.mean(dxn * xn, axis=-1, keepdims=True))

    row = lambda w: pl.BlockSpec((tm, w), lambda i: (i, 0))
    fixed = lambda r, c: pl.BlockSpec((r, c), lambda i: (0, 0))
    return pl.pallas_call(
        body, grid=(n // tm,),
        in_specs=[row(d), row(d), fixed(1, d), fixed(d, E_INT)] + [row(w) for _, w in _PIECES],
        out_specs=[row(d), fixed(8, d)],
        out_shape=[jax.ShapeDtypeStruct((n, d), F32), jax.ShapeDtypeStruct((8, d), F32)],
        compiler_params=_cparams(("arbitrary",)), name=name)(x2, dxo, g_row, w_int, *pieces)


def _inproj_bwd_w(x2, g_row, pieces, name):
    n, d = x2.shape
    tm = min(256, n)

    def body(x_ref, g_ref, da_ref, db_ref, dc_ref, df_ref, dw_ref):
        @pl.when(pl.program_id(0) == 0)
        def _():
            dw_ref[...] = jnp.zeros((d, E_INT), F32)

        x = x_ref[...]
        h = (x * _rstd(x) * g_ref[...]).astype(BF16)
        for ref, (o, w) in zip((da_ref, db_ref, dc_ref, df_ref), _PIECES):
            dw_ref[:, o:o + w] += _dot(h, ref[...].astype(BF16), TN)

    row = lambda w: pl.BlockSpec((tm, w), lambda i: (i, 0))
    return pl.pallas_call(
        body, grid=(n // tm,),
        in_specs=[row(d), pl.BlockSpec((1, d), lambda i: (0, 0))] + [row(w) for _, w in _PIECES],
        out_specs=pl.BlockSpec((d, E_INT), lambda i: (0, 0)),
        out_shape=jax.ShapeDtypeStruct((d, E_INT), F32),
        compiler_params=_cparams(("arbitrary",), vmem_mb=56), name=name)(x2, g_row, *pieces)


def _block_diag(pool_w_l):
    z = jnp.zeros((64, 64), pool_w_l.dtype)
    return jnp.concatenate(
        [jnp.concatenate([pool_w_l[g] if c == g else z for c in range(4)], axis=1) for g in range(4)], axis=0)


def _pad_lanes(v, width=128):
    return jnp.pad(v, ((0, 0),) * (v.ndim - 1) + ((0, width - v.shape[-1]),))


def _local_step(x, target, lower_bounds, pre_norm_g, w_in_int, hgrn_norm_g, fox_f_bias, pool_w, pool_scale,
                w_out_bf, post_norm_g):
    bsz, t, d = x.shape
    n = bsz * t
    lbs = _lbs_fwd(lower_bounds)
    saved = []
    xc = x.reshape(n, d)
    for l in range(DEPTH):
        proj = _inproj_fwd(xc, pre_norm_g[l:l + 1], w_in_int[l], f"inproj_fwd{l}").reshape(bsz, t, E_INT)
        wbd = _block_diag(pool_w[l]).astype(BF16)
        bias_row = _pad_lanes(fox_f_bias[l:l + 1])
        oa, oa_raw, states = _hgrn_fwd(proj, lbs[l:l + 1], hgrn_norm_g[l:l + 1], f"hgrn_fwd{l}")
        ob = _pool_fwd(proj, wbd, pool_scale[l:l + 1], f"pool_fwd{l}")
        c_nat, c_t = _foxgate_fwd(proj, bias_row, f"foxgate_fwd{l}")
        oc, oc_raw, lse = _fox_fwd(proj, c_nat, c_t, f"fox_fwd{l}")
        y, xn = _outproj_fwd(xc, oa.reshape(n, -1), ob.reshape(n, -1), oc.reshape(n, -1), w_out_bf[l],
                             post_norm_g[l:l + 1], f"outproj_fwd{l}")
        saved.append((xc, proj, wbd, bias_row, oa, oa_raw, states, ob, oc, oc_raw, lse, c_nat, c_t, y))
        xc = xn
    dx, sq = _loss_head(xc, target.reshape(n, d), "loss_head")
    g = {k: [None] * DEPTH for k in ("pre", "w_in", "hgn", "bias", "pool_w", "pool_scale", "w_out", "post", "lbs")}
    for l in reversed(range(DEPTH)):
        xin, proj, wbd, bias_row, oa, oa_raw, states, ob, oc, oc_raw, lse, c_nat, c_t, y = saved[l]
        dmix, g["w_out"][l], dpost = _outproj_bwd(dx, y, oa.reshape(n, -1), ob.reshape(n, -1), oc.reshape(n, -1),
                                                  w_out_bf[l], post_norm_g[l:l + 1], f"outproj_bwd{l}")
        g["post"][l] = dpost[0]
        dmix3 = dmix.reshape(bsz, t, d)
        d_c, dct, drow = _fox_bwd(proj, oc_raw, dmix3, lse, c_nat, c_t, f"fox_bwd{l}")
        dc_nat = _pad_lanes(dct[:, :, 0:2, :].reshape(bsz, FOX_HEADS, t).transpose(0, 2, 1)
                            + drow.reshape(bsz, t, FOX_HEADS, 64)[..., 0])
        d_f, dbias = _foxgate_bwd(proj, dc_nat, bias_row, f"foxgate_bwd{l}")
        g["bias"][l] = jnp.sum(dbias[:, 0, :FOX_HEADS], axis=0)
        d_b, dscale, dwbd = _pool_bwd(proj, dmix3, wbd, pool_scale[l:l + 1], f"pool_bwd{l}")
        g["pool_scale"][l] = jnp.sum(dscale[:, 0], axis=0)
        dwbd = jnp.sum(dwbd, axis=0)
        g["pool_w"][l] = jnp.stack([dwbd[64 * k:64 * (k + 1), 64 * k:64 * (k + 1)] for k in range(4)])
        d_a, dgn, dlb = _hgrn_bwd(proj, oa_raw, dmix3, states, lbs[l:l + 1], hgrn_norm_g[l:l + 1], f"hgrn_bwd{l}")
        g["hgn"][l] = jnp.sum(dgn[:, 0], axis=0)
        g["lbs"][l] = jnp.sum(dlb[:, 0], axis=0)
        pieces = [p.reshape(n, -1) for p in (d_a, d_b, d_c, d_f)]
        g["w_in"][l] = _inproj_bwd_w(xin, pre_norm_g[l:l + 1], pieces, f"inproj_bwd_w{l}")
        dx, dpre = _inproj_bwd_x(xin, dx, pre_norm_g[l:l + 1], w_in_int[l], pieces, f"inproj_bwd_x{l}")
        g["pre"][l] = dpre[0]
    grads = {k: jnp.stack(v) for k, v in g.items()}
    return sq, dx.reshape(bsz, t, d), grads


def _place():
    return lax.axis_index("x"), lax.axis_index("y"), lax.axis_index("c")


def _other_chips(x, y):
    return [(1 - x, y), (x, 1 - y), (1 - x, 1 - y)]


_ANY = pl.BlockSpec(memory_space=pl.ANY)


def _gather_weights(w_in_sh, w_out_sh):
    def body(win_ref, wout_ref, ain_ref, aout_ref, ici_send, ici_recv, d2d_send, d2d_recv, local_sems):
        x, y, c = _place()
        me = 2 * x + y
        pairs = ((win_ref, ain_ref), (wout_ref, aout_ref))
        mine = [pltpu.make_async_copy(src, dst.at[me], local_sems.at[j]) for j, (src, dst) in enumerate(pairs)]
        for cp in mine:
            cp.start()
        chips = _other_chips(x, y)
        sends = [pltpu.make_async_remote_copy(
            src_ref=src.at[c], dst_ref=dst.at[me, c], send_sem=ici_send.at[2 * k + j], recv_sem=ici_recv.at[2 * k + j],
            device_id=(px, py, c), device_id_type=MESH) for k, (px, py) in enumerate(chips) for j, (src, dst) in enumerate(pairs)]
        for cp in sends:
            cp.start()
        passed = [pltpu.make_async_remote_copy(
            src_ref=dst.at[2 * px + py, c], dst_ref=dst.at[2 * px + py, c], send_sem=d2d_send.at[2 * k + j],
            recv_sem=d2d_recv.at[2 * k + j], device_id=(x, y, 1 - c), device_id_type=MESH)
            for k, (px, py) in enumerate(chips) for j, (src, dst) in enumerate(pairs)]
        for n, (k, j) in enumerate((k, j) for k in range(3) for j in range(2)):
            px, py = chips[k]
            src, dst = pairs[j]
            pltpu.make_async_remote_copy(
                src_ref=src.at[c], dst_ref=dst.at[2 * px + py, c], send_sem=ici_send.at[n], recv_sem=ici_recv.at[n],
                device_id=(px, py, c), device_id_type=MESH).wait_recv()
            passed[n].start()
        for n, (k, j) in enumerate((k, j) for k in range(3) for j in range(2)):
            px, py = chips[k]
            src, dst = pairs[j]
            pltpu.make_async_remote_copy(
                src_ref=dst.at[2 * px + py, 1 - c], dst_ref=dst.at[2 * px + py, 1 - c], send_sem=d2d_send.at[n],
                recv_sem=d2d_recv.at[n], device_id=(x, y, 1 - c), device_id_type=MESH).wait_recv()
        for cp in sends + passed:
            cp.wait_send()
        for cp in mine:
            cp.wait()

    sems = pltpu.SemaphoreType.DMA((6,))
    return pl.pallas_call(
        body, in_specs=[_ANY, _ANY], out_specs=[_ANY, _ANY],
        out_shape=[jax.ShapeDtypeStruct((N_CHIPS,) + w_in_sh.shape, w_in_sh.dtype),
                   jax.ShapeDtypeStruct((N_CHIPS,) + w_out_sh.shape, w_out_sh.dtype)],
        scratch_shapes=[sems, sems, sems, sems, pltpu.SemaphoreType.DMA((2,))],
        name="gather_weights")(w_in_sh, w_out_sh)


def _swap_with_sibling(parts, name):
    k = len(parts)

    def body(*refs):
        src, dst = refs[:k], refs[k:2 * k]
        send_sems, recv_sems = refs[2 * k:]
        x, y, c = _place()
        cps = [pltpu.make_async_remote_copy(src_ref=src[j], dst_ref=dst[j], send_sem=send_sems.at[j], recv_sem=recv_sems.at[j],
                                            device_id=(x, y, 1 - c), device_id_type=MESH) for j in range(k)]
        for cp in cps:
            cp.start()
        for cp in cps:
            cp.wait()

    return pl.pallas_call(
        body, in_specs=[_ANY] * k, out_specs=[_ANY] * k,
        out_shape=[jax.ShapeDtypeStruct(p.shape, p.dtype) for p in parts],
        scratch_shapes=[pltpu.SemaphoreType.DMA((k,)), pltpu.SemaphoreType.DMA((k,))], name=name)(*parts)


def _scatter_to_chips(parts, name):
    k = len(parts)

    def body(*refs):
        src, dst = refs[:k], refs[k:2 * k]
        send_sems, recv_sems = refs[2 * k:]
        x, y, c = _place()
        me = 2 * x + y
        cps = []
        for rel, (px, py) in enumerate(_other_chips(x, y)):
            for j in range(k):
                cps.append(pltpu.make_async_remote_copy(
                    src_ref=src[j].at[2 * px + py], dst_ref=dst[j].at[rel], send_sem=send_sems.at[rel * k + j],
                    recv_sem=recv_sems.at[rel * k + j], device_id=(px, py, c), device_id_type=MESH))
        for cp in cps:
            cp.start()
        for cp in cps:
            cp.wait()
        del me

    return pl.pallas_call(
        body, in_specs=[_ANY] * k, out_specs=[_ANY] * k,
        out_shape=[jax.ShapeDtypeStruct((3,) + p.shape[1:], p.dtype) for p in parts],
        scratch_shapes=[pltpu.SemaphoreType.DMA((3 * k,)), pltpu.SemaphoreType.DMA((3 * k,))], name=name)(*parts)


def _add_n(parts, name, with_bf16=False):
    r, c = parts[0].shape
    tr = 256 if r % 256 == 0 else r
    n = len(parts)

    def body(*refs):
        acc = refs[0][...].astype(F32)
        for ref in refs[1:n]:
            acc = acc + ref[...].astype(F32)
        refs[n][...] = acc
        if with_bf16:
            refs[n + 1][...] = acc.astype(BF16)

    blk = pl.BlockSpec((tr, c), lambda i: (i, 0))
    outs = [jax.ShapeDtypeStruct((r, c), F32)] + ([jax.ShapeDtypeStruct((r, c), BF16)] if with_bf16 else [])
    res = pl.pallas_call(
        body, grid=(r // tr,), in_specs=[blk] * n, out_specs=[blk] * len(outs),
        out_shape=outs, compiler_params=_cparams(("parallel",)), name=name)(*parts)
    return res if with_bf16 else res[0]


def _all_reduce_small(packet):
    r, w = packet.shape

    def body(p_ref, o_ref, buf, send_sems, recv_sems):
        x, y, c = _place()
        me = 4 * x + 2 * y + c
        buf[me] = p_ref[...]
        peers = []
        for k in range(1, 8):
            fx, fy, fc = (k >> 2) & 1, (k >> 1) & 1, k & 1
            peers.append((x ^ fx, y ^ fy, c ^ fc))
        cps = [pltpu.make_async_remote_copy(src_ref=p_ref, dst_ref=buf.at[me], send_sem=send_sems.at[k], recv_sem=recv_sems.at[k],
                                            device_id=peer, device_id_type=MESH) for k, peer in enumerate(peers)]
        for cp in cps:
            cp.start()
        for k, (px, py, pc) in enumerate(peers):
            pltpu.make_async_remote_copy(src_ref=p_ref, dst_ref=buf.at[4 * px + 2 * py + pc], send_sem=send_sems.at[k],
                                         recv_sem=recv_sems.at[k], device_id=(px, py, pc), device_id_type=MESH).wait_recv()
        for cp in cps:
            cp.wait_send()
        acc = buf[0]
        for k in range(1, 8):
            acc = acc + buf[k]
        o_ref[...] = acc

    vm = pl.BlockSpec(memory_space=pltpu.VMEM)
    return pl.pallas_call(
        body, in_specs=[vm], out_specs=vm, out_shape=jax.ShapeDtypeStruct((r, w), F32),
        scratch_shapes=[pltpu.VMEM((8, r, w), F32), pltpu.SemaphoreType.DMA((7,)), pltpu.SemaphoreType.DMA((7,))],
        name="all_reduce_small")(packet)


def _adamw_math(w, g, m, v):
    m = ADAM_B1 * m + (1.0 - ADAM_B1) * g
    v = ADAM_B2 * v + (1.0 - ADAM_B2) * (g * g)
    m_hat = m / (1.0 - ADAM_B1 ** ADAM_STEP)
    v_hat = v / (1.0 - ADAM_B2 ** ADAM_STEP)
    return -ADAM_LR * (m_hat / (jnp.sqrt(v_hat) + ADAM_EPS) + ADAM_WD * w), m, v


def _adamw(w, g, m, v, name):
    nl, r, c = w.shape
    tr = 256 if r % 256 == 0 else r

    def body(w_ref, g_ref, m_ref, v_ref, d_ref, mo_ref, vo_ref):
        d_ref[...], mo_ref[...], vo_ref[...] = _adamw_math(w_ref[...], g_ref[...], m_ref[...], v_ref[...])

    blk = pl.BlockSpec((None, tr, c), lambda l, i: (l, i, 0))
    out = jax.ShapeDtypeStruct(w.shape, F32)
    return pl.pallas_call(
        body, grid=(nl, r // tr), in_specs=[blk] * 4, out_specs=[blk] * 3, out_shape=[out] * 3,
        compiler_params=_cparams(("parallel", "parallel")), name=name)(w, g, m, v)


def _small_update(gsum, lower_bounds, wpack, mpack, vpack):
    r, w = gsum.shape
    lb_rows = DEPTH * HGRN_W // 128

    def body(g_ref, a_ref, w_ref, m_ref, v_ref, go_ref, d_ref, mo_ref, vo_ref):
        a = a_ref[...]
        a0, a1 = a[0:1], a[1:2]
        mx = jnp.maximum(a0, a1)
        e0, e1 = jnp.exp(a0 - mx), jnp.exp(a1 - mx)
        p0, p1 = e0 / (e0 + e1), e1 / (e0 + e1)
        g = g_ref[...]
        half = lb_rows // 2
        dl0 = jnp.concatenate([g[k:k + 1] for k in range(half)], axis=1)
        dl1 = jnp.concatenate([g[half + k:half + k + 1] for k in range(half)], axis=1)
        dp0 = (dl0 + dl1) - (dl0 + dl1)
        dp1 = dl1
        inner = p0 * dp0 + p1 * dp1
        da0, da1 = p0 * (dp0 - inner), p1 * (dp1 - inner)
        rows = [da0[:, 128 * k:128 * (k + 1)] for k in range(half)] + [da1[:, 128 * k:128 * (k + 1)] for k in range(half)]
        gfull = jnp.concatenate(rows + [g[lb_rows:]], axis=0)
        go_ref[...] = gfull
        d_ref[...], mo_ref[...], vo_ref[...] = _adamw_math(w_ref[...], gfull, m_ref[...], v_ref[...])

    vm = pl.BlockSpec(memory_space=pltpu.VMEM)
    out = jax.ShapeDtypeStruct((r, w), F32)
    return pl.pallas_call(body, in_specs=[vm] * 5, out_specs=[vm] * 4, out_shape=[out] * 4, name="small_update")(
        gsum, lower_bounds, wpack, mpack, vpack)


_SMALL = ("lower_bounds", "pre_norm_g", "hgrn_norm_g", "fox_f_bias", "pool_w", "pool_scale", "post_norm_g")


def _pack(parts):
    rows = []
    for k in _SMALL:
        f = parts[k].reshape(-1)
        pad = (-f.shape[0]) % (8 * 128)
        rows.append(jnp.pad(f, (0, pad)).reshape(-1, 128))
    rows.append(jnp.zeros((8, 128), F32))
    return jnp.concatenate(rows, axis=0)


def _unpack(pack, like):
    out, r = {}, 0
    for k in _SMALL:
        size = int(np.prod(like[k].shape))
        nr = -(-size // (8 * 128)) * 8
        out[k] = pack[r:r + nr].reshape(-1)[:size].reshape(like[k].shape)
        r += nr
    return out, r


def kernel(x, lower_bounds, pre_norm_g, w_in, hgrn_norm_g, fox_f_bias, pool_w, pool_scale, w_out, post_norm_g, loss_target, m_lower_bounds, m_pre_norm_g, m_w_in, m_hgrn_norm_g, m_fox_f_bias, m_pool_w, m_pool_scale, m_w_out, m_post_norm_g, v_lower_bounds, v_pre_norm_g, v_w_in, v_hgrn_norm_g, v_fox_f_bias, v_pool_w, v_pool_scale, v_w_out, v_post_norm_g):
    cx, cy, cc = _place()
    chip = 2 * cx + cy

    ain, aout = _gather_weights(w_in.astype(BF16), w_out.astype(BF16))
    w_in_full = jnp.concatenate([ain[q] for q in range(N_CHIPS)], axis=-1)
    w_in_int = _to_internal(w_in_full)
    w_out_full = jnp.concatenate([aout[q] for q in range(N_CHIPS)], axis=1)

    sq, grad_x, g = _local_step(x, loss_target, lower_bounds, pre_norm_g, w_in_int, hgrn_norm_g, fox_f_bias, pool_w,
                                pool_scale, w_out_full, post_norm_g)

    gin = _to_original(g["w_in"])
    gin_blocks = jnp.stack([gin[:, :, SHARD_W * q:SHARD_W * (q + 1)] for q in range(N_CHIPS)])
    gout_blocks = g["w_out"].reshape(DEPTH, N_CHIPS, 256, D_MODEL).transpose(1, 0, 2, 3)
    take = lambda a, l: lax.dynamic_index_in_dim(a, l, axis=1, keepdims=False)
    mine_in, mine_out = take(gin_blocks, cc), take(gout_blocks, cc)
    sib_in, sib_out = _swap_with_sibling([take(gin_blocks, 1 - cc), take(gout_blocks, 1 - cc)], "grad_swap1")
    rin, rout = 4 * 1024, 4 * 256
    sum_in, send_in = _add_n([mine_in.reshape(rin, SHARD_W), sib_in.reshape(rin, SHARD_W)], "grad_add1_in", True)
    sum_out, send_out = _add_n([mine_out.reshape(rout, D_MODEL), sib_out.reshape(rout, D_MODEL)], "grad_add1_out", True)
    sum_in, sum_out = sum_in.reshape(4, 1024, SHARD_W), sum_out.reshape(4, 256, D_MODEL)
    got_in, got_out = _scatter_to_chips([send_in.reshape(4, 1024, SHARD_W), send_out.reshape(4, 256, D_MODEL)], "grad_scatter")
    own = lambda a: lax.dynamic_index_in_dim(a, chip, axis=0, keepdims=False)
    half_in = _add_n([own(sum_in)] + [got_in[k] for k in range(3)], "grad_add2_in")
    half_out = _add_n([own(sum_out)] + [got_out[k] for k in range(3)], "grad_add2_out")
    oth_in, oth_out = _swap_with_sibling([half_in, half_out], "grad_swap2")
    first = cc == 0
    grad_w_in = jnp.stack([jnp.where(first, half_in, oth_in), jnp.where(first, oth_in, half_in)])
    grad_w_out = jnp.stack([jnp.where(first, half_out, oth_out), jnp.where(first, oth_out, half_out)])

    small = {"lower_bounds": g["lbs"], "pre_norm_g": g["pre"], "hgrn_norm_g": g["hgn"], "fox_f_bias": g["bias"],
             "pool_w": g["pool_w"], "pool_scale": g["pool_scale"], "post_norm_g": g["post"]}
    packet = _pack(small)
    nrows = packet.shape[0]
    packet = packet.at[nrows - 1].set(sq[0])
    gsum = _all_reduce_small(packet)
    loss = gsum[nrows - 1, 0] * (0.5 / D_MODEL)

    weights = {"lower_bounds": lower_bounds, "pre_norm_g": pre_norm_g, "hgrn_norm_g": hgrn_norm_g,
               "fox_f_bias": fox_f_bias, "pool_w": pool_w, "pool_scale": pool_scale, "post_norm_g": post_norm_g}
    moments_m = {"lower_bounds": m_lower_bounds, "pre_norm_g": m_pre_norm_g, "hgrn_norm_g": m_hgrn_norm_g,
                 "fox_f_bias": m_fox_f_bias, "pool_w": m_pool_w, "pool_scale": m_pool_scale, "post_norm_g": m_post_norm_g}
    moments_v = {"lower_bounds": v_lower_bounds, "pre_norm_g": v_pre_norm_g, "hgrn_norm_g": v_hgrn_norm_g,
                 "fox_f_bias": v_fox_f_bias, "pool_w": v_pool_w, "pool_scale": v_pool_scale, "post_norm_g": v_post_norm_g}
    gp, dp, mp, vp = _small_update(gsum, lower_bounds, _pack(weights), _pack(moments_m), _pack(moments_v))
    gs, _ = _unpack(gp, weights)
    ds, _ = _unpack(dp, weights)
    ms, _ = _unpack(mp, weights)
    vs, _ = _unpack(vp, weights)

    d_in, m_in, v_in = _adamw(w_in, grad_w_in, m_w_in, v_w_in, "adamw_w_in")
    d_out, m_out, v_out = _adamw(w_out, grad_w_out, m_w_out, v_w_out, "adamw_w_out")

    def ordered(s, big_in, big_out):
        return (s["lower_bounds"], s["pre_norm_g"], big_in, s["hgrn_norm_g"], s["fox_f_bias"], s["pool_w"],
                s["pool_scale"], big_out, s["post_norm_g"])

    return (loss, grad_x, *ordered(gs, grad_w_in, grad_w_out), *ordered(ds, d_in, d_out),
            *ordered(ms, m_in, m_out), *ordered(vs, v_in, v_out))
```

```python
import functools

import numpy as np
import jax
import jax.numpy as jnp
from jax import lax
from jax.experimental import pallas as pl
from jax.experimental.pallas import tpu as pltpu

F32 = jnp.float32
BF16 = jnp.bfloat16
HI = lax.Precision.HIGHEST
MESH = pl.DeviceIdType.MESH

NORM_EPS = 1e-6
MASK_VALUE = -1e30
TINY = 1e-30
ADAM_LR, ADAM_B1, ADAM_B2, ADAM_EPS, ADAM_WD, ADAM_STEP = 0.001, 0.9, 0.999, 1e-08, 0.01, 10

D_MODEL = 1024
DEPTH = 2
N_CHIPS = 4
CHUNK = 64
LANES = 128
HGRN_W, POOL_W, FOX_W, FOX_HEADS = 256, 256, 512, 8
POOL_WINDOWS = (2, 4, 8, 16)
POOL_HALO = 16
IN_WIDTH = 3592
SHARD_W = IN_WIDTH // N_CHIPS
A_W, B_W, C_W, F_W = 1024, 512, 2048, 128
E_INT = A_W + B_W + C_W + F_W
B_BLK = A_W // 512
C_BLK0 = (A_W + B_W) // 512
F_BLK = (A_W + B_W + C_W) // 128


def _segments():
    segs = []
    for hp in range(2):
        for part in range(4):
            segs.append((part * 256 + hp * 128, 128))
    segs.append((1024, 256))
    segs.append((1280, 256))
    for hp in range(4):
        for part in range(4):
            segs.append((1536 + part * 512 + hp * 128, 128))
    segs.append((3584, 8))
    return segs


_SEGS = _segments()


def _to_internal(w):
    parts = [w[..., s:s + n] for s, n in _SEGS]
    parts.append(jnp.zeros(w.shape[:-1] + (E_INT - IN_WIDTH,), w.dtype))
    return jnp.concatenate(parts, axis=-1)


def _to_original(w):
    offs, o = [], 0
    for s, n in _SEGS:
        offs.append((s, o, n))
        o += n
    parts = [w[..., o:o + n] for s, o, n in sorted(offs)]
    return jnp.concatenate(parts, axis=-1)


def _internal_from_shards(shards):
    parts = []
    for s, n in _SEGS:
        while n > 0:
            q, r = divmod(s, SHARD_W)
            take = min(n, SHARD_W - r)
            parts.append(shards[q][..., r:r + take])
            s, n = s + take, n - take
    parts.append(jnp.zeros(shards[0].shape[:-1] + (E_INT - IN_WIDTH,), shards[0].dtype))
    return jnp.concatenate(parts, axis=-1)


def _shards_from_internal(w):
    offs, o = [], 0
    for s, n in _SEGS:
        offs.append((s, o, n))
        o += n
    blocks = []
    for q in range(N_CHIPS):
        lo, hi = SHARD_W * q, SHARD_W * (q + 1)
        parts = [w[..., o + max(lo, s) - s:o + min(hi, s + n) - s] for s, o, n in sorted(offs) if s < hi and s + n > lo]
        blocks.append(jnp.concatenate(parts, axis=-1))
    return jnp.stack(blocks)


def _cparams(sem=None, vmem_mb=48):
    kw = dict(vmem_limit_bytes=vmem_mb * 1024 * 1024)
    if sem is not None:
        kw["dimension_semantics"] = sem
    return pltpu.CompilerParams(**kw)


def _sig(x):
    return 1.0 / (1.0 + jnp.exp(-x))


def _silu(x):
    return x * _sig(x)


def _dsilu(x):
    s = _sig(x)
    return s * (1.0 + x * (1.0 - s))


def _rstd(x):
    return lax.rsqrt(jnp.mean(x * x, axis=-1, keepdims=True) + NORM_EPS)


def _dot(a, b, dims, **kw):
    return lax.dot_general(a, b, (dims, ((), ())), preferred_element_type=F32, **kw)


NN = ((1,), (0,))
NT = ((1,), (1,))
TN = ((0,), (0,))


def _iota(shape, dim):
    return lax.broadcasted_iota(jnp.int32, shape, dim)


def _lbs_fwd(lower_bounds):
    def body(a_ref, o_ref):
        a = a_ref[...]
        a0, a1 = a[0:1], a[1:2]
        m = jnp.maximum(a0, a1)
        e0, e1 = jnp.exp(a0 - m), jnp.exp(a1 - m)
        p0, p1 = e0 / (e0 + e1), e1 / (e0 + e1)
        o_ref[...] = jnp.concatenate([p0 - p0, (p0 + p1) - p0], axis=0)

    return pl.pallas_call(body, out_shape=jax.ShapeDtypeStruct(lower_bounds.shape, F32), name="lbs_fwd")(lower_bounds)


def _inproj_fwd(x2, g_row, w_int, name):
    n, d = x2.shape
    e = w_int.shape[1]
    tm = min(256, n)

    def body(x_ref, g_ref, w_ref, o_ref):
        x = x_ref[...]
        h = (x * _rstd(x) * g_ref[...]).astype(BF16)
        o_ref[...] = jnp.dot(h, w_ref[...], preferred_element_type=F32)

    return pl.pallas_call(
        body, grid=(n // tm,),
        in_specs=[pl.BlockSpec((tm, d), lambda i: (i, 0)), pl.BlockSpec((1, d), lambda i: (0, 0)),
                  pl.BlockSpec((d, e), lambda i: (0, 0))],
        out_specs=pl.BlockSpec((tm, e), lambda i: (i, 0)),
        out_shape=jax.ShapeDtypeStruct((n, e), F32),
        compiler_params=_cparams(("parallel",)), name=name)(x2, g_row, w_int)


def _chunk_cumsum_matrix():
    i, j = _iota((LANES, LANES), 0), _iota((LANES, LANES), 1)
    return ((i <= j) & ((i // CHUNK) == (j // CHUNK))).astype(F32)


def _hgrn_gates(a, lb):
    qa, z = a[:, 0:128], a[:, 128:256]
    sg, sgn = _sig(z), _sig(-z)
    fg = lb + (1.0 - lb) * sg
    lf = jnp.log(jnp.maximum(fg, TINY))
    kk = (1.0 - lb) * sgn
    return qa * _sig(qa), kk, lf, sg, sgn, fg


def _hgrn_fwd(proj3, lbs_row, gn_col, name):
    bsz, t, _ = proj3.shape
    nt = t // LANES

    def body(a_ref, lb_ref, gn_ref, og_ref, or_ref):
        lb = lb_ref[...]
        gn = gn_ref[...]
        umat = _chunk_cumsum_matrix()
        lane64 = _iota((1, LANES), 1) % CHUNK

        def tile(i, carry):
            r0 = pl.multiple_of(i * LANES, LANES)
            a = a_ref[pl.ds(r0, LANES), :]
            qq, kk, lf, _, _, _ = _hgrn_gates(a, lb)
            va, ga = a[:, 256:384], a[:, 384:512]
            q_t, k_t, v_t = qq.T, kk.T, va.T
            b_t = jnp.dot(lf.T, umat, precision=HI, preferred_element_type=F32)
            new_s, o_heads = [], []
            for h in range(2):
                s_h = carry[h]
                rs = slice(CHUNK * h, CHUNK * (h + 1))
                qh, kh, vh, bh = q_t[rs], k_t[rs], v_t[rs], b_t[rs]
                inter = []
                for c in range(2):
                    cs = slice(CHUNK * c, CHUNK * (c + 1))
                    b_ = bh[:, cs]
                    qt = (qh[:, cs] * jnp.exp(b_)).astype(BF16)
                    inter.append(_dot(s_h.astype(BF16), qt, TN))
                    bl = b_[:, CHUNK - 1:CHUNK]
                    kt = (kh[:, cs] * jnp.exp(bl - b_)).astype(BF16)
                    s_h = jnp.exp(bl) * s_h + _dot(kt, vh[:, cs].astype(BF16), NT)
                new_s.append(s_h)

                acc = jnp.concatenate(inter, axis=1) + jnp.sum(qh * kh, axis=0, keepdims=True) * vh
                for dlt in range(1, CHUNK):
                    kr, br, vr = pltpu.roll(kh, dlt, 1), pltpu.roll(bh, dlt, 1), pltpu.roll(vh, dlt, 1)
                    e = jnp.exp(jnp.minimum(bh - br, 0.0))
                    att = jnp.sum(qh * kr * e, axis=0, keepdims=True)
                    acc = acc + jnp.where(lane64 >= dlt, att, 0.0) * vr
                o_heads.append(acc)
            normed = []
            for h in range(2):
                o_h = o_heads[h]
                ms = jnp.mean(o_h * o_h, axis=0, keepdims=True)
                normed.append(o_h * lax.rsqrt(ms + NORM_EPS) * gn[CHUNK * h:CHUNK * (h + 1)])
            or_ref[pl.ds(r0, LANES), :] = jnp.concatenate(o_heads, axis=0).T
            og_ref[pl.ds(r0, LANES), :] = jnp.concatenate(normed, axis=0).T * _silu(ga)
            return tuple(new_s)

        zero = jnp.zeros((CHUNK, CHUNK), F32)
        lax.fori_loop(0, nt, tile, (zero, zero))

    out = jax.ShapeDtypeStruct((bsz, t, HGRN_W), F32)
    return pl.pallas_call(
        body, grid=(bsz, 2),
        in_specs=[pl.BlockSpec((None, t, 512), lambda b, p: (b, 0, p)),
                  pl.BlockSpec((1, 128), lambda b, p: (0, p)),
                  pl.BlockSpec((128, 1), lambda b, p: (p, 0))],
        out_specs=[pl.BlockSpec((None, t, 128), lambda b, p: (b, 0, p)),
                   pl.BlockSpec((None, t, 128), lambda b, p: (b, 0, p))],
        out_shape=[out, out],
        compiler_params=_cparams(("parallel", "parallel")), name=name)(proj3, lbs_row, gn_col)


def _hgrn_bwd(proj3, o_raw, dmixed, lbs_row, gn_row, name):
    bsz, t, _ = proj3.shape
    nt = t // LANES
    nchunk = t // CHUNK

    def body(a_ref, or_ref, do_ref, lb_ref, gn_ref, da_ref, dgn_ref, dlb_ref, s_sc):
        lb = lb_ref[...]
        gn = gn_ref[...]
        umat = _chunk_cumsum_matrix()
        lane = _iota((1, LANES), 1)
        lane64 = lane % CHUNK
        half = lane < CHUNK

        def t_layout(a):
            qq, kk, lf, sg, sgn, fg = _hgrn_gates(a, lb)
            b_t = jnp.dot(lf.T, umat, precision=HI, preferred_element_type=F32)
            return qq.T, kk.T, a[:, 256:384].T, b_t, (sg, sgn, fg)

        def fwd_tile(i, carry):
            r0 = pl.multiple_of(i * LANES, LANES)
            q_t, k_t, v_t, b_t, _ = t_layout(a_ref[pl.ds(r0, LANES), :])
            new_s = []
            for h in range(2):
                s_h = carry[h]
                rs = slice(CHUNK * h, CHUNK * (h + 1))
                for c in range(2):
                    cs = slice(CHUNK * c, CHUNK * (c + 1))
                    s_sc[h, 2 * i + c] = s_h
                    b_ = b_t[rs, cs]
                    bl = b_[:, CHUNK - 1:CHUNK]
                    kt = (k_t[rs, cs] * jnp.exp(bl - b_)).astype(BF16)
                    s_h = jnp.exp(bl) * s_h + _dot(kt, v_t[rs, cs].astype(BF16), NT)
                new_s.append(s_h)
            return tuple(new_s)

        zero = jnp.zeros((CHUNK, CHUNK), F32)
        lax.fori_loop(0, nt, fwd_tile, (zero, zero))

        def half_mean(v):
            m0 = jnp.sum(jnp.where(half, v, 0.0), axis=1, keepdims=True) * (1.0 / CHUNK)
            m1 = jnp.sum(jnp.where(half, 0.0, v), axis=1, keepdims=True) * (1.0 / CHUNK)
            return jnp.where(half, m0, m1)

        def bwd_tile(k, carry):
            ds0, ds1, dgn_acc, dlb_acc = carry
            i = nt - 1 - k
            r0 = pl.multiple_of(i * LANES, LANES)
            a = a_ref[pl.ds(r0, LANES), :]
            qa, z, ga = a[:, 0:128], a[:, 128:256], a[:, 384:512]
            q_t, k_t, v_t, b_t, (sg, sgn, fg) = t_layout(a)
            oraw = or_ref[pl.ds(r0, LANES), :]
            dout = do_ref[pl.ds(r0, LANES), :]
            r = lax.rsqrt(half_mean(oraw * oraw) + NORM_EPS)
            xn = oraw * r
            dga = dout * (xn * gn) * _dsilu(ga)
            don = dout * _silu(ga)
            dgn_acc = dgn_acc + jnp.sum(don * xn, axis=0, keepdims=True)
            dxn = don * gn
            do_t = (r * (dxn - xn * half_mean(dxn * xn))).T
            new_ds, dq_h, dk_h, dv_h, db_h = [], [], [], [], []
            for h in range(2):
                ds_h = (ds0, ds1)[h]
                rs = slice(CHUNK * h, CHUNK * (h + 1))
                qh, kh, vh, bh, doh = q_t[rs], k_t[rs], v_t[rs], b_t[rs], do_t[rs]
                dq_c, dk_c, dv_c, dbl_c = [None, None], [None, None], [None, None], [None, None]
                for c in (1, 0):
                    cs = slice(CHUNK * c, CHUNK * (c + 1))
                    s_n = s_sc[h, 2 * i + c]
                    b_ = bh[:, cs]
                    eb = jnp.exp(b_)
                    bl = b_[:, CHUNK - 1:CHUNK]
                    ek = jnp.exp(bl - b_)
                    ebl = jnp.exp(bl)
                    qt, kt = qh[:, cs] * eb, kh[:, cs] * ek
                    do_c = doh[:, cs].astype(BF16)
                    dsb = ds_h.astype(BF16)
                    dv_c[c] = _dot(dsb, kt.astype(BF16), TN)
                    dkt = _dot(dsb, vh[:, cs].astype(BF16), NN)
                    dqt = _dot(s_n.astype(BF16), do_c, NN)
                    dbl_c[c] = jnp.sum(ds_h * s_n, axis=1, keepdims=True) * ebl + jnp.sum(dkt * kt, axis=1, keepdims=True)
                    dq_c[c], dk_c[c] = dqt * eb, dkt * ek
                    ds_h = ebl * ds_h + _dot(qt.astype(BF16), do_c, NT)
                new_ds.append(ds_h)

                att0 = jnp.sum(qh * kh, axis=0, keepdims=True)
                datt0 = jnp.sum(doh * vh, axis=0, keepdims=True)
                dqh = jnp.concatenate(dq_c, axis=1) + datt0 * kh
                dkh = jnp.concatenate(dk_c, axis=1) + datt0 * qh
                dvh = jnp.concatenate(dv_c, axis=1) + att0 * doh
                for dlt in range(1, CHUNK):
                    kr, br, vr = pltpu.roll(kh, dlt, 1), pltpu.roll(bh, dlt, 1), pltpu.roll(vh, dlt, 1)
                    e = jnp.where(lane64 >= dlt, jnp.exp(jnp.minimum(bh - br, 0.0)), 0.0)
                    qe = qh * e
                    att = jnp.sum(qe * kr, axis=0, keepdims=True)
                    datt = jnp.sum(doh * vr, axis=0, keepdims=True)
                    dqh = dqh + datt * (kr * e)
                    dkh = dkh + pltpu.roll(datt * qe, LANES - dlt, 1)
                    dvh = dvh + pltpu.roll(att * doh, LANES - dlt, 1)
                dbl = jnp.where(half, dbl_c[0], dbl_c[1])
                db_h.append(qh * dqh - kh * dkh + jnp.where(lane64 == CHUNK - 1, dbl, 0.0))
                dq_h.append(dqh)
                dk_h.append(dkh)
                dv_h.append(dvh)
            dqq = jnp.concatenate(dq_h, axis=0).T
            dkk = jnp.concatenate(dk_h, axis=0).T
            dvv = jnp.concatenate(dv_h, axis=0).T
            dlf = _dot(jnp.concatenate(db_h, axis=0), umat, NT, precision=HI).T
            dqa = dqq * _dsilu(qa)
            dfg = jnp.where(fg > TINY, dlf / fg, 0.0)
            dz = (dfg - dkk) * (1.0 - lb) * sg * sgn
            dlb_acc = dlb_acc + jnp.sum(dfg * (1.0 - sg) - dkk * sgn, axis=0, keepdims=True)
            da_ref[pl.ds(r0, LANES), :] = jnp.concatenate([dqa, dz, dvv, dga], axis=1)
            return new_ds[0], new_ds[1], dgn_acc, dlb_acc

        zrow = jnp.zeros((1, LANES), F32)
        _, _, dgn_acc, dlb_acc = lax.fori_loop(0, nt, bwd_tile, (zero, zero, zrow, zrow))
        dgn_ref[...] = jnp.broadcast_to(dgn_acc, (8, LANES))
        dlb_ref[...] = jnp.broadcast_to(dlb_acc, (8, LANES))

    rows = jax.ShapeDtypeStruct((bsz, 8, HGRN_W), F32)
    return pl.pallas_call(
        body, grid=(bsz, 2),
        in_specs=[pl.BlockSpec((None, t, 512), lambda b, p: (b, 0, p)),
                  pl.BlockSpec((None, t, 128), lambda b, p: (b, 0, p)),
                  pl.BlockSpec((None, t, 128), lambda b, p: (b, 0, p)),
                  pl.BlockSpec((1, 128), lambda b, p: (0, p)),
                  pl.BlockSpec((1, 128), lambda b, p: (0, p))],
        out_specs=[pl.BlockSpec((None, t, 512), lambda b, p: (b, 0, p)),
                   pl.BlockSpec((None, 8, 128), lambda b, p: (b, 0, p)),
                   pl.BlockSpec((None, 8, 128), lambda b, p: (b, 0, p))],
        out_shape=[jax.ShapeDtypeStruct((bsz, t, A_W), F32), rows, rows],
        scratch_shapes=[pltpu.VMEM((2, nchunk, CHUNK, CHUNK), F32)],
        compiler_params=_cparams(("parallel", "parallel")), name=name)(proj3, o_raw, dmixed, lbs_row, gn_row)


N_LEVELS = 6


def _hgrn_tables():
    t = np.arange(LANES)
    j = np.arange(LANES)[None, :]
    same_chunk = (t[:, None] // CHUNK) == (j // CHUNK)
    w = np.zeros((2 + N_LEVELS, LANES, LANES), np.float32)
    w[0] = same_chunk & (j <= t[:, None])
    w[1] = same_chunk & (j > t[:, None])
    maskf = np.zeros((N_LEVELS, LANES, LANES), np.float32)
    rightf = np.zeros((N_LEVELS, LANES, LANES), np.float32)
    for li in range(N_LEVELS):
        m = (CHUNK // 2) >> li
        start = t - (t % (2 * m))
        right = (t % (2 * m)) >= m
        first = np.where(right, start + m, t + 1)
        last = np.where(right, t, start + m - 1)
        w[2 + li] = (j >= first[:, None]) & (j <= last[:, None])
        maskf[li] = (t[:, None] // (2 * m)) == (j // (2 * m))
        rightf[li] = right[:, None]
    return jnp.asarray(w.reshape(-1, LANES), BF16), jnp.asarray(maskf), jnp.asarray(rightf)


def _split(x, n):
    parts = []
    for _ in range(n - 1):
        p = x.astype(BF16)
        parts.append(p)
        x = x - p.astype(F32)
    parts.append(x.astype(BF16))
    return parts


def _exact_dot(w, parts):
    acc = jnp.dot(w, parts[0], preferred_element_type=F32)
    for p in parts[1:]:
        acc = acc + jnp.dot(w, p, preferred_element_type=F32)
    return acc


def _head_sums(v, ones_blk, n=2):
    parts = _split(v, n)
    acc = jnp.dot(parts[0], ones_blk, preferred_element_type=F32)
    for p in parts[1:]:
        acc = acc + jnp.dot(p, ones_blk, preferred_element_type=F32)
    return acc


def _hgrn_consts():
    r, c = _iota((LANES, LANES), 0), _iota((LANES, LANES), 1)
    eye = r == c
    ones_blk = ((r // CHUNK) == (c // CHUNK)).astype(BF16)
    return eye, ones_blk, jnp.ones((CHUNK, LANES), BF16)


def _hgrn_levels(qq, kk, zall, mk_ref, rt_ref, d_att=None):
    att = [jnp.zeros((LANES, LANES), F32)] * 2
    dq = dk = db = jnp.zeros((LANES, LANES), F32)
    for li in range(N_LEVELS):
        e = jnp.exp(zall[(2 + li) * LANES:(3 + li) * LANES])
        rt = rt_ref[li]
        mk = mk_ref[li]
        qef, kef = e * rt, e * (1.0 - rt)
        qe, ke = (qq * qef).astype(BF16), (kk * kef).astype(BF16)
        dqs, dks = [], []
        for h in range(2):
            hs = slice(CHUNK * h, CHUNK * (h + 1))
            att[h] = att[h] + _dot(qe[:, hs], ke[:, hs], NT) * mk
            if d_att is not None:
                dam = (d_att[h] * mk).astype(BF16)
                dqs.append(jnp.dot(dam, ke[:, hs], preferred_element_type=F32))
                dks.append(_dot(dam, qe[:, hs], TN))
        if d_att is not None:
            dqe, dke = jnp.concatenate(dqs, axis=1), jnp.concatenate(dks, axis=1)
            dq = dq + dqe * qef
            dk = dk + dke * kef
            db = db + (dqe * qe.astype(F32) - dke * ke.astype(F32))
    return att, dq, dk, db


def _hgrn_fwd(proj3, lbs_row, gn_row, name):
    bsz, t, _ = proj3.shape
    nt = t // LANES
    w_all, maskf, rightf = _hgrn_tables()

    def body(a_ref, lb_ref, gn_ref, w_ref, mk_ref, rt_ref, og_ref, or_ref, st_ref):
        lb = lb_ref[...]
        gn = gn_ref[...]
        eye, ones_blk, ones_h = _hgrn_consts()

        def tile(i, carry):
            r0 = pl.multiple_of(i * LANES, LANES)
            a = a_ref[pl.ds(r0, LANES), :]
            qq, kk, lf, _, _, _ = _hgrn_gates(a, lb)
            va, ga = a[:, 256:384], a[:, 384:512]
            parts = _split(lf, 3)
            zall = _exact_dot(w_ref[...], parts)
            eb, ee = jnp.exp(zall[0:LANES]), jnp.exp(zall[LANES:2 * LANES])
            vb = va.astype(BF16)
            att, _, _, _ = _hgrn_levels(qq, kk, zall, mk_ref, rt_ref)
            qk = _split(qq * kk, 2)
            qeb, keb = (qq * eb).astype(BF16), (kk * ee).astype(BF16)
            new_s, o_heads = [], []
            for h in range(2):
                hs = slice(CHUNK * h, CHUNK * (h + 1))
                diag = _exact_dot_r(qk, hs, ones_h)
                a_h = att[h] + jnp.where(eye, diag, 0.0)
                o_h = jnp.dot(a_h.astype(BF16), vb[:, hs], preferred_element_type=F32)
                st = carry[h]
                chunks = []
                for c in range(2):
                    rc = slice(CHUNK * c, CHUNK * (c + 1))
                    st_ref[h, 2 * i + c] = st
                    chunks.append(o_h[rc] + _dot(qeb[rc, hs], st.astype(BF16), NT))
                    ebl = eb[CHUNK * (c + 1) - 1:CHUNK * (c + 1), hs]
                    st = st * ebl + _dot(vb[rc, hs], keb[rc, hs], TN)
                new_s.append(st)
                o_heads.append(jnp.concatenate(chunks, axis=0))
            o = jnp.concatenate(o_heads, axis=1)
            ms = _head_sums(o * o, ones_blk) * (1.0 / CHUNK)
            or_ref[pl.ds(r0, LANES), :] = o
            og_ref[pl.ds(r0, LANES), :] = o * lax.rsqrt(ms + NORM_EPS) * gn * _silu(ga)
            return tuple(new_s)

        zero = jnp.zeros((CHUNK, CHUNK), F32)
        lax.fori_loop(0, nt, tile, (zero, zero))

    out = jax.ShapeDtypeStruct((bsz, t, HGRN_W), F32)
    row = pl.BlockSpec((1, 128), lambda b, p: (0, p))
    return pl.pallas_call(
        body, grid=(bsz, 2),
        in_specs=[pl.BlockSpec((None, t, 512), lambda b, p: (b, 0, p)), row, row,
                  pl.BlockSpec(w_all.shape, lambda b, p: (0, 0)),
                  pl.BlockSpec(maskf.shape, lambda b, p: (0, 0, 0)),
                  pl.BlockSpec(rightf.shape, lambda b, p: (0, 0, 0))],
        out_specs=[pl.BlockSpec((None, t, 128), lambda b, p: (b, 0, p)),
                   pl.BlockSpec((None, t, 128), lambda b, p: (b, 0, p)),
                   pl.BlockSpec((None, 2, t // CHUNK, CHUNK, CHUNK), lambda b, p: (b, p, 0, 0, 0))],
        out_shape=[out, out, jax.ShapeDtypeStruct((bsz, 4, t // CHUNK, CHUNK, CHUNK), F32)],
        compiler_params=_cparams(("parallel", "parallel")), name=name)(proj3, lbs_row, gn_row, w_all, maskf, rightf)


def _exact_dot_r(parts, hs, ones_h):
    acc = jnp.dot(parts[0][:, hs], ones_h, preferred_element_type=F32)
    for p in parts[1:]:
        acc = acc + jnp.dot(p[:, hs], ones_h, preferred_element_type=F32)
    return acc


def _hgrn_bwd(proj3, o_raw, dmixed, states, lbs_row, gn_row, name):
    bsz, t, _ = proj3.shape
    nt = t // LANES
    nchunk = t // CHUNK
    w_all, maskf, rightf = _hgrn_tables()

    def body(a_ref, or_ref, do_ref, s_sc, lb_ref, gn_ref, w_ref, mk_ref, rt_ref, da_ref, dgn_ref, dlb_ref):
        lb = lb_ref[...]
        gn = gn_ref[...]
        eye, ones_blk, ones_h = _hgrn_consts()
        r_i, c_i = _iota((LANES, LANES), 0), _iota((LANES, LANES), 1)
        suffix = ((c_i >= r_i) & ((r_i // CHUNK) == (c_i // CHUNK))).astype(BF16)
        row64 = _iota((LANES, CHUNK), 0)
        ones_t = jnp.ones((LANES, CHUNK), BF16)
        zero = jnp.zeros((CHUNK, CHUNK), F32)

        def bwd_tile(k, carry):
            dst0, dst1, dgn_acc, dlb_acc = carry
            i = nt - 1 - k
            r0 = pl.multiple_of(i * LANES, LANES)
            a = a_ref[pl.ds(r0, LANES), :]
            qa, ga = a[:, 0:128], a[:, 384:512]
            qq, kk, lf, sg, sgn, fg = _hgrn_gates(a, lb)
            parts = _split(lf, 3)
            zall = _exact_dot(w_ref[...], parts)
            eb, ee = jnp.exp(zall[0:LANES]), jnp.exp(zall[LANES:2 * LANES])
            vb = a[:, 256:384].astype(BF16)
            oraw = or_ref[pl.ds(r0, LANES), :]
            dout = do_ref[pl.ds(r0, LANES), :]
            r = lax.rsqrt(_head_sums(oraw * oraw, ones_blk) * (1.0 / CHUNK) + NORM_EPS)
            xn = oraw * r
            dga = dout * (xn * gn) * _dsilu(ga)
            don = dout * _silu(ga)
            dgn_acc = dgn_acc + jnp.sum(don * xn, axis=0, keepdims=True)
            dxn = don * gn
            do = r * (dxn - xn * (_head_sums(dxn * xn, ones_blk) * (1.0 / CHUNK)))
            dob = do.astype(BF16)
            d_att = [_dot(dob[:, CHUNK * h:CHUNK * (h + 1)], vb[:, CHUNK * h:CHUNK * (h + 1)], NT) for h in range(2)]
            att, dq, dk, db_lv = _hgrn_levels(qq, kk, zall, mk_ref, rt_ref, d_att)
            qk = _split(qq * kk, 2)
            qe_f, ke_f = qq * eb, kk * ee
            qeb, keb = qe_f.astype(BF16), ke_f.astype(BF16)
            new_ds, dq_h, dk_h, dv_h, dbl_h = [], [], [], [], []
            for h in range(2):
                hs = slice(CHUNK * h, CHUNK * (h + 1))
                a_h = att[h] + jnp.where(eye, _exact_dot_r(qk, hs, ones_h), 0.0)
                dv = _dot(a_h.astype(BF16), dob[:, hs], TN)
                ddiag = _exact_dot_r(_split(jnp.where(eye, d_att[h], 0.0), 2), slice(None), ones_t)
                dq_i = dq[:, hs] + ddiag * kk[:, hs]
                dk_i = dk[:, hs] + ddiag * qq[:, hs]
                dst = (dst0, dst1)[h]
                dq_c, dk_c, dv_c, dbl_c = [None, None], [None, None], [None, None], [None, None]
                for c in (1, 0):
                    rc = slice(CHUNK * c, CHUNK * (c + 1))
                    st_n = s_sc[h, 2 * i + c]
                    ebl = eb[CHUNK * (c + 1) - 1:CHUNK * (c + 1), hs]
                    dstb = dst.astype(BF16)
                    dv_c[c] = _dot(keb[rc, hs], dstb, NT)
                    dke = jnp.dot(vb[rc, hs], dstb, preferred_element_type=F32)
                    dqe = jnp.dot(dob[rc, hs], st_n.astype(BF16), preferred_element_type=F32)
                    dbl_c[c] = (jnp.sum(dst * st_n, axis=0, keepdims=True) * ebl
                                + jnp.sum(dke * ke_f[rc, hs], axis=0, keepdims=True))
                    dq_c[c], dk_c[c] = dqe * eb[rc, hs], dke * ee[rc, hs]
                    dst = dst * ebl + _dot(dob[rc, hs], qeb[rc, hs], TN)
                new_ds.append(dst)
                dq_x, dk_x = jnp.concatenate(dq_c, axis=0), jnp.concatenate(dk_c, axis=0)
                dq_h.append(dq_i + dq_x)
                dk_h.append(dk_i + dk_x)
                dv_h.append(dv + jnp.concatenate(dv_c, axis=0))
                dbl_h.append(qq[:, hs] * dq_x - kk[:, hs] * dk_x
                             + jnp.where(row64 == CHUNK - 1, dbl_c[0], 0.0) + jnp.where(row64 == LANES - 1, dbl_c[1], 0.0))
            dqq = jnp.concatenate(dq_h, axis=1)
            dkk = jnp.concatenate(dk_h, axis=1)
            dvv = jnp.concatenate(dv_h, axis=1)
            db = db_lv + jnp.concatenate(dbl_h, axis=1)
            dlf = _exact_dot(suffix, _split(db, 3))
            dqa = dqq * _dsilu(qa)
            dfg = jnp.where(fg > TINY, dlf / fg, 0.0)
            dz = (dfg - dkk) * (1.0 - lb) * sg * sgn
            dlb_acc = dlb_acc + jnp.sum(dfg * (1.0 - sg) - dkk * sgn, axis=0, keepdims=True)
            da_ref[pl.ds(r0, LANES), :] = jnp.concatenate([dqa, dz, dvv, dga], axis=1)
            return new_ds[0], new_ds[1], dgn_acc, dlb_acc

        zrow = jnp.zeros((1, LANES), F32)
        _, _, dgn_acc, dlb_acc = lax.fori_loop(0, nt, bwd_tile, (zero, zero, zrow, zrow))
        dgn_ref[...] = jnp.broadcast_to(dgn_acc, (8, LANES))
        dlb_ref[...] = jnp.broadcast_to(dlb_acc, (8, LANES))

    rows = jax.ShapeDtypeStruct((bsz, 8, HGRN_W), F32)
    row = pl.BlockSpec((1, 128), lambda b, p: (0, p))
    blk = pl.BlockSpec((None, t, 128), lambda b, p: (b, 0, p))
    return pl.pallas_call(
        body, grid=(bsz, 2),
        in_specs=[pl.BlockSpec((None, t, 512), lambda b, p: (b, 0, p)), blk, blk,
                  pl.BlockSpec((None, 2, nchunk, CHUNK, CHUNK), lambda b, p: (b, p, 0, 0, 0)), row, row,
                  pl.BlockSpec(w_all.shape, lambda b, p: (0, 0)),
                  pl.BlockSpec(maskf.shape, lambda b, p: (0, 0, 0)),
                  pl.BlockSpec(rightf.shape, lambda b, p: (0, 0, 0))],
        out_specs=[pl.BlockSpec((None, t, 512), lambda b, p: (b, 0, p)),
                   pl.BlockSpec((None, 8, 128), lambda b, p: (b, 0, p)),
                   pl.BlockSpec((None, 8, 128), lambda b, p: (b, 0, p))],
        out_shape=[jax.ShapeDtypeStruct((bsz, t, A_W), F32), rows, rows],
        compiler_params=_cparams(("parallel", "parallel")), name=name)(
            proj3, o_raw, dmixed, states, lbs_row, gn_row, w_all, maskf, rightf)


def _pool_tt(t):
    return min(256, t)


def _window_select(s2, s4, s8, s16, lane):
    return jnp.where(lane < 64, s2, jnp.where(lane < 128, s4, jnp.where(lane < 192, s8, s16)))


def _pool_counts(t0, tt):
    lane = _iota((tt, POOL_W), 1)
    tpos = (_iota((tt, POOL_W), 0) + t0 + 1).astype(F32)
    win = jnp.where(lane < 64, 2.0, jnp.where(lane < 128, 4.0, jnp.where(lane < 192, 8.0, 16.0)))
    return 1.0 / jnp.minimum(tpos, win), lane


def _pooled_tile(upad_ref, i, tt):
    r0 = pl.multiple_of(i * tt, 8)
    cat = upad_ref[pl.ds(r0, tt + POOL_HALO), :]
    s2 = cat + pltpu.roll(cat, 1, 0)
    s4 = s2 + pltpu.roll(s2, 2, 0)
    s8 = s4 + pltpu.roll(s4, 4, 0)
    s16 = s8 + pltpu.roll(s8, 8, 0)
    inv, lane = _pool_counts(i * tt, tt)
    sel = _window_select(s2[POOL_HALO:], s4[POOL_HALO:], s8[POOL_HALO:], s16[POOL_HALO:], lane)
    return sel * inv - cat[POOL_HALO:], inv, lane


def _pool_fwd(proj3, wbd, scale_row, name):
    bsz, t, _ = proj3.shape
    tt = _pool_tt(t)

    def body(p_ref, w_ref, sc_ref, o_ref, upad):
        upad[0:POOL_HALO, :] = jnp.zeros((POOL_HALO, POOL_W), F32)
        upad[POOL_HALO:, :] = p_ref[:, 0:POOL_W]
        w = w_ref[...]
        sc = sc_ref[...]

        def tile(i, c):
            pooled, _, _ = _pooled_tile(upad, i, tt)
            r0 = pl.multiple_of(i * tt, 8)
            g = p_ref[pl.ds(r0, tt), POOL_W:2 * POOL_W]
            pre = jnp.dot(pooled.astype(BF16), w, preferred_element_type=F32)
            o_ref[pl.ds(r0, tt), :] = pre * sc * _silu(g)
            return c

        lax.fori_loop(0, t // tt, tile, 0)

    return pl.pallas_call(
        body, grid=(bsz,),
        in_specs=[pl.BlockSpec((None, t, 512), lambda b: (b, 0, B_BLK)),
                  pl.BlockSpec((POOL_W, POOL_W), lambda b: (0, 0)),
                  pl.BlockSpec((1, POOL_W), lambda b: (0, 0))],
        out_specs=pl.BlockSpec((None, t, POOL_W), lambda b: (b, 0, 0)),
        out_shape=jax.ShapeDtypeStruct((bsz, t, POOL_W), F32),
        scratch_shapes=[pltpu.VMEM((t + POOL_HALO, POOL_W), F32)],
        compiler_params=_cparams(("parallel",)), name=name)(proj3, wbd, scale_row)


def _pool_bwd(proj3, dmixed, wbd, scale_row, name):
    bsz, t, _ = proj3.shape
    tt = _pool_tt(t)

    def body(p_ref, do_ref, w_ref, sc_ref, db_ref, dsc_ref, dw_ref, upad, epad):
        upad[0:POOL_HALO, :] = jnp.zeros((POOL_HALO, POOL_W), F32)
        upad[POOL_HALO:, :] = p_ref[:, 0:POOL_W]
        epad[t:, :] = jnp.zeros((POOL_HALO, POOL_W), F32)
        w = w_ref[...]
        sc = sc_ref[...]

        def tile(i, carry):
            dsc_acc, dw_acc = carry
            pooled, inv, _ = _pooled_tile(upad, i, tt)
            r0 = pl.multiple_of(i * tt, 8)
            g = p_ref[pl.ds(r0, tt), POOL_W:2 * POOL_W]
            dout = do_ref[pl.ds(r0, tt), :]
            pb = pooled.astype(BF16)
            pre = jnp.dot(pb, w, preferred_element_type=F32)
            t1 = dout * _silu(g)
            dsc_acc = dsc_acc + jnp.sum(t1 * pre, axis=0, keepdims=True)
            dpre = (t1 * sc).astype(BF16)
            db_ref[pl.ds(r0, tt), POOL_W:2 * POOL_W] = dout * pre * sc * _dsilu(g)
            dw_acc = dw_acc + _dot(pb, dpre, TN)
            dpooled = _dot(dpre, w, NT)
            epad[pl.ds(r0, tt), :] = dpooled * inv
            return dsc_acc, dw_acc

        dsc_acc, dw_acc = lax.fori_loop(0, t // tt, tile, (jnp.zeros((1, POOL_W), F32), jnp.zeros((POOL_W, POOL_W), F32)))
        dsc_ref[...] = jnp.broadcast_to(dsc_acc, (8, POOL_W))
        dw_ref[...] = dw_acc

        def tile2(i, c):
            r0 = pl.multiple_of(i * tt, 8)
            n = tt + POOL_HALO
            cat = epad[pl.ds(r0, n), :]
            s2 = cat + pltpu.roll(cat, n - 1, 0)
            s4 = s2 + pltpu.roll(s2, n - 2, 0)
            s8 = s4 + pltpu.roll(s4, n - 4, 0)
            s16 = s8 + pltpu.roll(s8, n - 8, 0)
            inv, lane = _pool_counts(i * tt, tt)
            sel = _window_select(s2[:tt], s4[:tt], s8[:tt], s16[:tt], lane)
            db_ref[pl.ds(r0, tt), 0:POOL_W] = sel - cat[:tt] / inv
            return c

        lax.fori_loop(0, t // tt, tile2, 0)

    return pl.pallas_call(
        body, grid=(bsz,),
        in_specs=[pl.BlockSpec((None, t, 512), lambda b: (b, 0, B_BLK)),
                  pl.BlockSpec((None, t, POOL_W), lambda b: (b, 0, 1)),
                  pl.BlockSpec((POOL_W, POOL_W), lambda b: (0, 0)),
                  pl.BlockSpec((1, POOL_W), lambda b: (0, 0))],
        out_specs=[pl.BlockSpec((None, t, 512), lambda b: (b, 0, 0)),
                   pl.BlockSpec((None, 8, POOL_W), lambda b: (b, 0, 0)),
                   pl.BlockSpec((None, POOL_W, POOL_W), lambda b: (b, 0, 0))],
        out_shape=[jax.ShapeDtypeStruct((bsz, t, B_W), F32), jax.ShapeDtypeStruct((bsz, 8, POOL_W), F32),
                   jax.ShapeDtypeStruct((bsz, POOL_W, POOL_W), F32)],
        scratch_shapes=[pltpu.VMEM((t + POOL_HALO, POOL_W), F32), pltpu.VMEM((t + POOL_HALO, POOL_W), F32)],
        compiler_params=_cparams(("parallel",)), name=name)(proj3, dmixed, wbd, scale_row)


def _head_select_rows(hp):
    r, c = _iota((8, LANES), 0), _iota((8, LANES), 1)
    return ((r < 2) & (c == 2 * hp + r)).astype(F32)


def _foxgate_fwd(proj3, bias_row, name):
    bsz, t, _ = proj3.shape
    nt = t // LANES

    def body(f_ref, b_ref, cn_ref, ct_ref):
        bias = b_ref[...]
        i, j = _iota((LANES, LANES), 0), _iota((LANES, LANES), 1)
        lower = (j <= i).astype(F32)
        spread = (_iota((LANES, FOX_W), 0) == _iota((LANES, FOX_W), 1) // 64).astype(F32)

        def tile(k, carry):
            r0 = pl.multiple_of(k * LANES, LANES)
            xg = f_ref[pl.ds(r0, LANES), :] + bias
            lf = jnp.minimum(xg, 0.0) - jnp.log(1.0 + jnp.exp(-jnp.abs(xg)))
            c = jnp.dot(lower, lf, precision=HI, preferred_element_type=F32) + carry
            cn_ref[pl.ds(r0, LANES), :] = jnp.dot(c, spread, precision=HI, preferred_element_type=F32)
            for hp in range(4):
                ct_ref[hp, :, pl.ds(r0, LANES)] = _dot(_head_select_rows(hp), c, NT, precision=HI)
            return c[LANES - 1:LANES, :]

        lax.fori_loop(0, nt, tile, jnp.zeros((1, LANES), F32))

    return pl.pallas_call(
        body, grid=(bsz,),
        in_specs=[pl.BlockSpec((None, t, 128), lambda b: (b, 0, F_BLK)), pl.BlockSpec((1, 128), lambda b: (0, 0))],
        out_specs=[pl.BlockSpec((None, t, FOX_W), lambda b: (b, 0, 0)),
                   pl.BlockSpec((None, 4, 8, t), lambda b: (b, 0, 0, 0))],
        out_shape=[jax.ShapeDtypeStruct((bsz, t, FOX_W), F32), jax.ShapeDtypeStruct((bsz, 4, 8, t), F32)],
        compiler_params=_cparams(("parallel",)), name=name)(proj3, bias_row)


def _foxgate_bwd(proj3, dc_nat, bias_row, name):
    bsz, t, _ = proj3.shape
    nt = t // LANES

    def body(f_ref, dc_ref, b_ref, df_ref, dbias_ref, run_sc):
        bias = b_ref[...]
        i, j = _iota((LANES, LANES), 0), _iota((LANES, LANES), 1)
        upper = (j >= i).astype(F32)
        valid = _iota((1, LANES), 1) < FOX_HEADS
        run_sc[...] = jnp.zeros((8, LANES), F32)
        dbias_ref[...] = jnp.zeros((8, LANES), F32)

        def tile(k, c):
            r0 = pl.multiple_of((nt - 1 - k) * LANES, LANES)
            dc = dc_ref[pl.ds(r0, LANES), :] + jnp.where(i == LANES - 1, run_sc[0:1, :], 0.0)
            dlf = jnp.dot(upper, dc, precision=HI, preferred_element_type=F32)
            xg = f_ref[pl.ds(r0, LANES), :] + bias
            df = jnp.where(valid, dlf * _sig(-xg), 0.0)
            df_ref[pl.ds(r0, LANES), :] = df
            run_sc[...] = dlf[0:8, :]
            dbias_ref[...] += jnp.sum(df, axis=0, keepdims=True)
            return c

        lax.fori_loop(0, nt, tile, 0)

    blk = pl.BlockSpec((None, t, 128), lambda b: (b, 0, 0))
    return pl.pallas_call(
        body, grid=(bsz,),
        in_specs=[pl.BlockSpec((None, t, 128), lambda b: (b, 0, F_BLK)), blk, pl.BlockSpec((1, 128), lambda b: (0, 0))],
        out_specs=[blk, pl.BlockSpec((None, 8, 128), lambda b: (b, 0, 0))],
        out_shape=[jax.ShapeDtypeStruct((bsz, t, F_W), F32), jax.ShapeDtypeStruct((bsz, 8, 128), F32)],
        scratch_shapes=[pltpu.VMEM((8, LANES), F32)],
        compiler_params=_cparams(("parallel",)), name=name)(proj3, dc_nat, bias_row)


def _fox_tile(t):
    return min(256, t)


def _fox_fwd(proj3, c_nat, c_t, name):
    bsz, t, _ = proj3.shape
    tq = _fox_tile(t)
    tk = min(2 * tq, t)
    nq = t // tq

    def body(q_ref, kv_ref, cn_ref, ct_ref, og_ref, or_ref, lse_ref):
        i = pl.program_id(2)
        qblk = q_ref[...]
        first = _iota((1, 128), 1) < 64
        qv = qblk[:, 0:128] * 0.125
        qm = [jnp.where(first, qv, 0.0).astype(BF16), jnp.where(first, 0.0, qv).astype(BF16)]
        cqs = [cn_ref[:, 0:1], cn_ref[:, 64:65]]
        rows = _iota((tq, tk), 0) + i * tq

        def kv_step(j, carry, masked):
            c0 = pl.multiple_of(j * tk, tk)
            kb = kv_ref[pl.ds(c0, tk), 128:256].astype(BF16)
            vblk = kv_ref[pl.ds(c0, tk), 256:384]
            vx = [jnp.where(first, vblk, 1.0).astype(BF16), jnp.where(first, 1.0, vblk).astype(BF16)]
            new = []
            for h in range(2):
                m, acc = carry[2 * h], carry[2 * h + 1]
                s = _dot(qm[h], kb, NT) + (cqs[h] - ct_ref[h:h + 1, pl.ds(c0, tk)])
                if masked:
                    s = jnp.where(rows >= _iota((tq, tk), 1) + j * tk, s, MASK_VALUE)
                m_new = jnp.maximum(m, jnp.max(s, axis=1, keepdims=True))
                p = jnp.exp(s - m_new).astype(BF16)
                new += [m_new, jnp.exp(m - m_new) * acc + jnp.dot(p, vx[h], preferred_element_type=F32)]
            return tuple(new)

        init = (jnp.full((tq, 1), MASK_VALUE, F32), jnp.zeros((tq, 128), F32)) * 2
        n_full = (i * tq) // tk
        carry = lax.fori_loop(0, n_full, functools.partial(kv_step, masked=False), init)
        m0, acc0, m1, acc1 = kv_step(n_full, carry, True)
        l0, l1 = pltpu.roll(acc0, 64, 1), pltpu.roll(acc1, 64, 1)
        o = jnp.where(first, acc0 / l0, acc1 / l1)
        or_ref[...] = o
        og_ref[...] = o * _silu(qblk[:, 384:512])
        lse_ref[...] = jnp.where(first, m0 + jnp.log(l0), m1 + jnp.log(l1))

    out = jax.ShapeDtypeStruct((bsz, t, FOX_W), F32)
    blk = pl.BlockSpec((None, tq, 128), lambda b, p, i: (b, i, p))
    return pl.pallas_call(
        body, grid=(bsz, 4, nq),
        in_specs=[pl.BlockSpec((None, tq, 512), lambda b, p, i: (b, i, C_BLK0 + p)),
                  pl.BlockSpec((None, t, 512), lambda b, p, i: (b, 0, C_BLK0 + p)),
                  blk,
                  pl.BlockSpec((None, None, 8, t), lambda b, p, i: (b, p, 0, 0))],
        out_specs=[blk, blk, blk],
        out_shape=[out, out, out],
        compiler_params=_cparams(("parallel", "parallel", "arbitrary")), name=name)(proj3, proj3, c_nat, c_t)


def _fox_bwd(proj3, o_raw, dmixed, lse, c_nat, c_t, name):
    bsz, t, _ = proj3.shape
    tq = _fox_tile(t)
    nq = t // tq
    tk = min(2 * tq, t)
    ratio = tk // tq

    def body(a_ref, or_ref, do_ref, lse_ref, cn_ref, ct_ref, dc_out, dct_out, drow_out, dq_sc, do_sc, dl_sc):
        def prep(i, c):
            r0 = pl.multiple_of(i * tq, tq)
            g = a_ref[pl.ds(r0, tq), 384:512]
            dout = do_ref[pl.ds(r0, tq), :]
            o = or_ref[pl.ds(r0, tq), :]
            dc_out[pl.ds(r0, tq), 384:512] = dout * o * _dsilu(g)
            do = dout * _silu(g)
            do_sc[pl.ds(r0, tq), :] = do
            prod = do * o
            d0 = jnp.sum(prod[:, 0:64], axis=1, keepdims=True)
            d1 = jnp.sum(prod[:, 64:128], axis=1, keepdims=True)
            dl_sc[pl.ds(r0, tq), :] = jnp.concatenate([jnp.broadcast_to(d0, (tq, 64)), jnp.broadcast_to(d1, (tq, 64))], axis=1)
            dq_sc[pl.ds(r0, tq), :] = jnp.zeros((tq, 128), F32)
            drow_out[pl.ds(r0, tq), :] = jnp.zeros((tq, 128), F32)
            return c

        lax.fori_loop(0, nq, prep, 0)
        dct_out[...] = jnp.zeros((8, t), F32)

        first = _iota((1, 128), 1) < 64

        def heads(v):
            return [jnp.where(first, v, 0.0).astype(BF16), jnp.where(first, 0.0, v).astype(BF16)]

        def kv_tile(j, c):
            c0 = pl.multiple_of(j * tk, tk)
            kb = a_ref[pl.ds(c0, tk), 128:256].astype(BF16)
            vb = a_ref[pl.ds(c0, tk), 256:384].astype(BF16)
            cks = [ct_ref[h:h + 1, pl.ds(c0, tk)] for h in range(2)]

            def q_step(i, carry, diagonal):
                dk, dv, dcol0, dcol1 = carry
                r0 = pl.multiple_of(i * tq, tq)
                causal = _iota((tq, tk), 0) + i * tq >= _iota((tq, tk), 1) + j * tk
                qv = a_ref[pl.ds(r0, tq), 0:128] * 0.125
                do = do_sc[pl.ds(r0, tq), :]
                qb, dob = qv.astype(BF16), do.astype(BF16)
                qm, dom = heads(qv), heads(do)
                full, dcols, rsums = [], [], []
                for h in range(2):
                    lse_h = lse_ref[pl.ds(r0, tq), 64 * h:64 * h + 1]
                    dl_h = dl_sc[pl.ds(r0, tq), 64 * h:64 * h + 1]
                    cq = cn_ref[pl.ds(r0, tq), 64 * h:64 * h + 1]
                    p = jnp.exp(_dot(qm[h], kb, NT) + (cq - cks[h]) - lse_h)
                    if diagonal:
                        p = jnp.where(causal, p, 0.0)
                    ds = p * (_dot(dom[h], vb, NT) - dl_h)
                    dsb = ds.astype(BF16)
                    full.append((_dot(p.astype(BF16), dob, TN), _dot(dsb, qb, TN),
                                 jnp.dot(dsb, kb, preferred_element_type=F32)))
                    dcols.append(jnp.sum(ds, axis=0, keepdims=True))
                    rsums.append(jnp.broadcast_to(jnp.sum(ds, axis=1, keepdims=True), (tq, 128)))
                dq_sc[pl.ds(r0, tq), :] += jnp.where(first, full[0][2], full[1][2]) * 0.125
                drow_out[pl.ds(r0, tq), :] += jnp.where(first, rsums[0], rsums[1])
                return (dk + jnp.where(first, full[0][1], full[1][1]), dv + jnp.where(first, full[0][0], full[1][0]),
                        dcol0 - dcols[0], dcol1 - dcols[1])

            carry = (jnp.zeros((tk, 128), F32), jnp.zeros((tk, 128), F32), jnp.zeros((1, tk), F32), jnp.zeros((1, tk), F32))
            for r in range(ratio):
                carry = q_step(ratio * j + r, carry, True)
            dk, dv, dcol0, dcol1 = lax.fori_loop(ratio * (j + 1), nq, functools.partial(q_step, diagonal=False), carry)
            dct_out[0:1, pl.ds(c0, tk)] = dcol0
            dct_out[1:2, pl.ds(c0, tk)] = dcol1
            dc_out[pl.ds(c0, tk), 128:256] = dk
            dc_out[pl.ds(c0, tk), 256:384] = dv
            return c

        lax.fori_loop(0, t // tk, kv_tile, 0)
        dc_out[:, 0:128] = dq_sc[...]

    blk = pl.BlockSpec((None, t, 128), lambda b, p: (b, 0, p))
    return pl.pallas_call(
        body, grid=(bsz, 4),
        in_specs=[pl.BlockSpec((None, t, 512), lambda b, p: (b, 0, C_BLK0 + p)),
                  blk,
                  pl.BlockSpec((None, t, 128), lambda b, p: (b, 0, 4 + p)),
                  blk, blk,
                  pl.BlockSpec((None, None, 8, t), lambda b, p: (b, p, 0, 0))],
        out_specs=[pl.BlockSpec((None, t, 512), lambda b, p: (b, 0, p)),
                   pl.BlockSpec((None, None, 8, t), lambda b, p: (b, p, 0, 0)), blk],
        out_shape=[jax.ShapeDtypeStruct((bsz, t, C_W), F32), jax.ShapeDtypeStruct((bsz, 4, 8, t), F32),
                   jax.ShapeDtypeStruct((bsz, t, FOX_W), F32)],
        scratch_shapes=[pltpu.VMEM((t, 128), F32), pltpu.VMEM((t, 128), F32), pltpu.VMEM((t, 128), F32)],
        compiler_params=_cparams(("parallel", "parallel")), name=name)(proj3, o_raw, dmixed, lse, c_nat, c_t)


def _mix_tm(n):
    return min(512, n)


def _outproj_fwd(x2, oa, ob, oc, wo, g_row, name):
    n, d = x2.shape
    tm = _mix_tm(n)

    def body(x_ref, oa_ref, ob_ref, oc_ref, w_ref, g_ref, y_ref, xo_ref):
        y = (jnp.dot(oa_ref[...].astype(BF16), w_ref[0:256, :], preferred_element_type=F32)
             + jnp.dot(ob_ref[...].astype(BF16), w_ref[256:512, :], preferred_element_type=F32)
             + jnp.dot(oc_ref[...].astype(BF16), w_ref[512:1024, :], preferred_element_type=F32))
        y_ref[...] = y
        xo_ref[...] = x_ref[...] + y * _rstd(y) * g_ref[...]

    row = lambda w: pl.BlockSpec((tm, w), lambda i: (i, 0))
    out = jax.ShapeDtypeStruct((n, d), F32)
    return pl.pallas_call(
        body, grid=(n // tm,),
        in_specs=[row(d), row(256), row(256), row(512), pl.BlockSpec((d, d), lambda i: (0, 0)),
                  pl.BlockSpec((1, d), lambda i: (0, 0))],
        out_specs=[row(d), row(d)], out_shape=[out, out],
        compiler_params=_cparams(("parallel",)), name=name)(x2, oa, ob, oc, wo, g_row)


def _loss_head(x2, target2, name):
    n, d = x2.shape
    tm = _mix_tm(n)

    def body(x_ref, t_ref, dx_ref, l_ref):
        err = x_ref[...] - t_ref[...]
        dx_ref[...] = err * (1.0 / d)

        @pl.when(pl.program_id(0) == 0)
        def _():
            l_ref[...] = jnp.zeros((8, 128), F32)

        l_ref[...] += jnp.sum(err * err)

    row = pl.BlockSpec((tm, d), lambda i: (i, 0))
    return pl.pallas_call(
        body, grid=(n // tm,), in_specs=[row, row],
        out_specs=[row, pl.BlockSpec((8, 128), lambda i: (0, 0))],
        out_shape=[jax.ShapeDtypeStruct((n, d), F32), jax.ShapeDtypeStruct((8, 128), F32)],
        compiler_params=_cparams(("arbitrary",)), name=name)(x2, target2)


def _outproj_bwd(dxo, y, oa, ob, oc, wo, g_row, name):
    n, d = dxo.shape
    tm = _mix_tm(n)

    def body(dx_ref, y_ref, oa_ref, ob_ref, oc_ref, w_ref, g_ref, dm_ref, dw_ref, dg_ref):
        @pl.when(pl.program_id(0) == 0)
        def _():
            dw_ref[...] = jnp.zeros((d, d), F32)
            dg_ref[...] = jnp.zeros((8, d), F32)

        yv, dx = y_ref[...], dx_ref[...]
        r = _rstd(yv)
        yn = yv * r
        dg_ref[...] += jnp.sum(dx * yn, axis=0, keepdims=True)
        dyn = dx * g_ref[...]
        dy = (r * (dyn - yn * jnp.mean(dyn * yn, axis=-1, keepdims=True))).astype(BF16)
        dm_ref[...] = _dot(dy, w_ref[...], NT)
        dw_ref[0:256, :] += _dot(oa_ref[...].astype(BF16), dy, TN)
        dw_ref[256:512, :] += _dot(ob_ref[...].astype(BF16), dy, TN)
        dw_ref[512:1024, :] += _dot(oc_ref[...].astype(BF16), dy, TN)

    row = lambda w: pl.BlockSpec((tm, w), lambda i: (i, 0))
    fixed = lambda r, c: pl.BlockSpec((r, c), lambda i: (0, 0))
    return pl.pallas_call(
        body, grid=(n // tm,),
        in_specs=[row(d), row(d), row(256), row(256), row(512), fixed(d, d), fixed(1, d)],
        out_specs=[row(d), fixed(d, d), fixed(8, d)],
        out_shape=[jax.ShapeDtypeStruct((n, d), F32), jax.ShapeDtypeStruct((d, d), F32), jax.ShapeDtypeStruct((8, d), F32)],
        compiler_params=_cparams(("arbitrary",)), name=name)(dxo, y, oa, ob, oc, wo, g_row)


_PIECES = ((0, A_W), (A_W, B_W), (A_W + B_W, C_W), (A_W + B_W + C_W, F_W))


def _inproj_bwd_x(x2, dxo, g_row, w_int, pieces, name):
    n, d = x2.shape
    tm = min(256, n)

    def body(x_ref, dxo_ref, g_ref, w_ref, da_ref, db_ref, dc_ref, df_ref, dx_ref, dg_ref):
        @pl.when(pl.program_id(0) == 0)
        def _():
            dg_ref[...] = jnp.zeros((8, d), F32)

        dh = jnp.zeros((tm, d), F32)
        for ref, (o, w) in zip((da_ref, db_ref, dc_ref, df_ref), _PIECES):
            dh = dh + _dot(ref[...].astype(BF16), w_ref[:, o:o + w], NT)
        x = x_ref[...]
        r = _rstd(x)
        xn = x * r
        dg_ref[...] += jnp.sum(dh * xn, axis=0, keepdims=True)
        dxn = dh * g_ref[...]
        dx_ref[...] = dxo_ref[...] + r * (dxn - xn * jnp.mean(dxn * xn, axis=-1, keepdims=True))

    row = lambda w: pl.BlockSpec((tm, w), lambda i: (i, 0))
    fixed = lambda r, c: pl.BlockSpec((r, c), lambda i: (0, 0))
    return pl.pallas_call(
        body, grid=(n // tm,),
        in_specs=[row(d), row(d), fixed(1, d), fixed(d, E_INT)] + [row(w) for _, w in _PIECES],
        out_specs=[row(d), fixed(8, d)],
        out_shape=[jax.ShapeDtypeStruct((n, d), F32), jax.ShapeDtypeStruct((8, d), F32)],
        compiler_params=_cparams(("arbitrary",)), name=name)(x2, dxo, g_row, w_int, *pieces)


def _inproj_bwd_w(x2, g_row, pieces, name):
    n, d = x2.shape
    tm = min(256, n)

    def body(x_ref, g_ref, da_ref, db_ref, dc_ref, df_ref, dw_ref):
        @pl.when(pl.program_id(0) == 0)
        def _():
            dw_ref[...] = jnp.zeros((d, E_INT), F32)

        x = x_ref[...]
        h = (x * _rstd(x) * g_ref[...]).astype(BF16)
        for ref, (o, w) in zip((da_ref, db_ref, dc_ref, df_ref), _PIECES):
            dw_ref[:, o:o + w] += _dot(h, ref[...].astype(BF16), TN)

    row = lambda w: pl.BlockSpec((tm, w), lambda i: (i, 0))
    return pl.pallas_call(
        body, grid=(n // tm,),
        in_specs=[row(d), pl.BlockSpec((1, d), lambda i: (0, 0))] + [row(w) for _, w in _PIECES],
        out_specs=pl.BlockSpec((d, E_INT), lambda i: (0, 0)),
        out_shape=jax.ShapeDtypeStruct((d, E_INT), F32),
        compiler_params=_cparams(("arbitrary",), vmem_mb=56), name=name)(x2, g_row, *pieces)


def _block_diag(pool_w_l):
    z = jnp.zeros((64, 64), pool_w_l.dtype)
    return jnp.concatenate(
        [jnp.concatenate([pool_w_l[g] if c == g else z for c in range(4)], axis=1) for g in range(4)], axis=0)


def _pad_lanes(v, width=128):
    return jnp.pad(v, ((0, 0),) * (v.ndim - 1) + ((0, width - v.shape[-1]),))


def _local_step(x, target, lower_bounds, pre_norm_g, w_in_int, hgrn_norm_g, fox_f_bias, pool_w, pool_scale,
                w_out_bf, post_norm_g):
    bsz, t, d = x.shape
    n = bsz * t
    lbs = _lbs_fwd(lower_bounds)
    saved = []
    xc = x.reshape(n, d)
    for l in range(DEPTH):
        proj = _inproj_fwd(xc, pre_norm_g[l:l + 1], w_in_int[l], f"inproj_fwd{l}").reshape(bsz, t, E_INT)
        wbd = _block_diag(pool_w[l]).astype(BF16)
        bias_row = _pad_lanes(fox_f_bias[l:l + 1])
        oa, oa_raw, states = _hgrn_fwd(proj, lbs[l:l + 1], hgrn_norm_g[l:l + 1], f"hgrn_fwd{l}")
        ob = _pool_fwd(proj, wbd, pool_scale[l:l + 1], f"pool_fwd{l}")
        c_nat, c_t = _foxgate_fwd(proj, bias_row, f"foxgate_fwd{l}")
        oc, oc_raw, lse = _fox_fwd(proj, c_nat, c_t, f"fox_fwd{l}")
        y, xn = _outproj_fwd(xc, oa.reshape(n, -1), ob.reshape(n, -1), oc.reshape(n, -1), w_out_bf[l],
                             post_norm_g[l:l + 1], f"outproj_fwd{l}")
        saved.append((xc, proj, wbd, bias_row, oa, oa_raw, states, ob, oc, oc_raw, lse, c_nat, c_t, y))
        xc = xn
    dx, sq = _loss_head(xc, target.reshape(n, d), "loss_head")
    g = {k: [None] * DEPTH for k in ("pre", "w_in", "hgn", "bias", "pool_w", "pool_scale", "w_out", "post", "lbs")}
    for l in reversed(range(DEPTH)):
        xin, proj, wbd, bias_row, oa, oa_raw, states, ob, oc, oc_raw, lse, c_nat, c_t, y = saved[l]
        dmix, g["w_out"][l], dpost = _outproj_bwd(dx, y, oa.reshape(n, -1), ob.reshape(n, -1), oc.reshape(n, -1),
                                                  w_out_bf[l], post_norm_g[l:l + 1], f"outproj_bwd{l}")
        g["post"][l] = dpost[0]
        dmix3 = dmix.reshape(bsz, t, d)
        d_c, dct, drow = _fox_bwd(proj, oc_raw, dmix3, lse, c_nat, c_t, f"fox_bwd{l}")
        dc_nat = _pad_lanes(dct[:, :, 0:2, :].reshape(bsz, FOX_HEADS, t).transpose(0, 2, 1)
                            + drow.reshape(bsz, t, FOX_HEADS, 64)[..., 0])
        d_f, dbias = _foxgate_bwd(proj, dc_nat, bias_row, f"foxgate_bwd{l}")
        g["bias"][l] = jnp.sum(dbias[:, 0, :FOX_HEADS], axis=0)
        d_b, dscale, dwbd = _pool_bwd(proj, dmix3, wbd, pool_scale[l:l + 1], f"pool_bwd{l}")
        g["pool_scale"][l] = jnp.sum(dscale[:, 0], axis=0)
        dwbd = jnp.sum(dwbd, axis=0)
        g["pool_w"][l] = jnp.stack([dwbd[64 * k:64 * (k + 1), 64 * k:64 * (k + 1)] for k in range(4)])
        d_a, dgn, dlb = _hgrn_bwd(proj, oa_raw, dmix3, states, lbs[l:l + 1], hgrn_norm_g[l:l + 1], f"hgrn_bwd{l}")
        g["hgn"][l] = jnp.sum(dgn[:, 0], axis=0)
        g["lbs"][l] = jnp.sum(dlb[:, 0], axis=0)
        pieces = [p.reshape(n, -1) for p in (d_a, d_b, d_c, d_f)]
        g["w_in"][l] = _inproj_bwd_w(xin, pre_norm_g[l:l + 1], pieces, f"inproj_bwd_w{l}")
        dx, dpre = _inproj_bwd_x(xin, dx, pre_norm_g[l:l + 1], w_in_int[l], pieces, f"inproj_bwd_x{l}")
        g["pre"][l] = dpre[0]
    grads = {k: jnp.stack(v) for k, v in g.items()}
    return sq, dx.reshape(bsz, t, d), grads


def _place():
    return lax.axis_index("x"), lax.axis_index("y"), lax.axis_index("c")


def _other_chips(x, y):
    return [(1 - x, y), (x, 1 - y), (1 - x, 1 - y)]


_ANY = pl.BlockSpec(memory_space=pl.ANY)


def _gather_weights(w_in_sh, w_out_sh):
    def body(win_ref, wout_ref, ain_ref, aout_ref, ici_send, ici_recv, d2d_send, d2d_recv, local_sems):
        x, y, c = _place()
        me = 2 * x + y
        pairs = ((win_ref, ain_ref), (wout_ref, aout_ref))
        mine = [pltpu.make_async_copy(src, dst.at[me], local_sems.at[j]) for j, (src, dst) in enumerate(pairs)]
        for cp in mine:
            cp.start()
        chips = _other_chips(x, y)
        sends = [pltpu.make_async_remote_copy(
            src_ref=src.at[c], dst_ref=dst.at[me, c], send_sem=ici_send.at[2 * k + j], recv_sem=ici_recv.at[2 * k + j],
            device_id=(px, py, c), device_id_type=MESH) for k, (px, py) in enumerate(chips) for j, (src, dst) in enumerate(pairs)]
        for cp in sends:
            cp.start()
        passed = [pltpu.make_async_remote_copy(
            src_ref=dst.at[2 * px + py, c], dst_ref=dst.at[2 * px + py, c], send_sem=d2d_send.at[2 * k + j],
            recv_sem=d2d_recv.at[2 * k + j], device_id=(x, y, 1 - c), device_id_type=MESH)
            for k, (px, py) in enumerate(chips) for j, (src, dst) in enumerate(pairs)]
        for n, (k, j) in enumerate((k, j) for k in range(3) for j in range(2)):
            px, py = chips[k]
            src, dst = pairs[j]
            pltpu.make_async_remote_copy(
                src_ref=src.at[c], dst_ref=dst.at[2 * px + py, c], send_sem=ici_send.at[n], recv_sem=ici_recv.at[n],
                device_id=(px, py, c), device_id_type=MESH).wait_recv()
            passed[n].start()
        for n, (k, j) in enumerate((k, j) for k in range(3) for j in range(2)):
            px, py = chips[k]
            src, dst = pairs[j]
            pltpu.make_async_remote_copy(
                src_ref=dst.at[2 * px + py, 1 - c], dst_ref=dst.at[2 * px + py, 1 - c], send_sem=d2d_send.at[n],
                recv_sem=d2d_recv.at[n], device_id=(x, y, 1 - c), device_id_type=MESH).wait_recv()
        for cp in sends + passed:
            cp.wait_send()
        for cp in mine:
            cp.wait()

    sems = pltpu.SemaphoreType.DMA((6,))
    return pl.pallas_call(
        body, in_specs=[_ANY, _ANY], out_specs=[_ANY, _ANY],
        out_shape=[jax.ShapeDtypeStruct((N_CHIPS,) + w_in_sh.shape, w_in_sh.dtype),
                   jax.ShapeDtypeStruct((N_CHIPS,) + w_out_sh.shape, w_out_sh.dtype)],
        scratch_shapes=[sems, sems, sems, sems, pltpu.SemaphoreType.DMA((2,))],
        name="gather_weights")(w_in_sh, w_out_sh)


def _swap_with_sibling(parts, name):
    k = len(parts)

    def body(*refs):
        src, dst = refs[:k], refs[k:2 * k]
        send_sems, recv_sems = refs[2 * k:]
        x, y, c = _place()
        cps = [pltpu.make_async_remote_copy(src_ref=src[j], dst_ref=dst[j], send_sem=send_sems.at[j], recv_sem=recv_sems.at[j],
                                            device_id=(x, y, 1 - c), device_id_type=MESH) for j in range(k)]
        for cp in cps:
            cp.start()
        for cp in cps:
            cp.wait()

    return pl.pallas_call(
        body, in_specs=[_ANY] * k, out_specs=[_ANY] * k,
        out_shape=[jax.ShapeDtypeStruct(p.shape, p.dtype) for p in parts],
        scratch_shapes=[pltpu.SemaphoreType.DMA((k,)), pltpu.SemaphoreType.DMA((k,))], name=name)(*parts)


def _scatter_to_chips(parts, name):
    k = len(parts)

    def body(*refs):
        src, dst = refs[:k], refs[k:2 * k]
        send_sems, recv_sems = refs[2 * k:]
        x, y, c = _place()
        me = 2 * x + y
        cps = []
        for rel, (px, py) in enumerate(_other_chips(x, y)):
            for j in range(k):
                cps.append(pltpu.make_async_remote_copy(
                    src_ref=src[j].at[2 * px + py], dst_ref=dst[j].at[rel], send_sem=send_sems.at[rel * k + j],
                    recv_sem=recv_sems.at[rel * k + j], device_id=(px, py, c), device_id_type=MESH))
        for cp in cps:
            cp.start()
        for cp in cps:
            cp.wait()
        del me

    return pl.pallas_call(
        body, in_specs=[_ANY] * k, out_specs=[_ANY] * k,
        out_shape=[jax.ShapeDtypeStruct((3,) + p.shape[1:], p.dtype) for p in parts],
        scratch_shapes=[pltpu.SemaphoreType.DMA((3 * k,)), pltpu.SemaphoreType.DMA((3 * k,))], name=name)(*parts)


def _add_n(parts, name, with_bf16=False):
    r, c = parts[0].shape
    tr = 256 if r % 256 == 0 else r
    n = len(parts)

    def body(*refs):
        acc = refs[0][...].astype(F32)
        for ref in refs[1:n]:
            acc = acc + ref[...].astype(F32)
        refs[n][...] = acc
        if with_bf16:
            refs[n + 1][...] = acc.astype(BF16)

    blk = pl.BlockSpec((tr, c), lambda i: (i, 0))
    outs = [jax.ShapeDtypeStruct((r, c), F32)] + ([jax.ShapeDtypeStruct((r, c), BF16)] if with_bf16 else [])
    res = pl.pallas_call(
        body, grid=(r // tr,), in_specs=[blk] * n, out_specs=[blk] * len(outs),
        out_shape=outs, compiler_params=_cparams(("parallel",)), name=name)(*parts)
    return res if with_bf16 else res[0]


def _all_reduce_small(packet):
    r, w = packet.shape

    def body(p_ref, o_ref, buf, send_sems, recv_sems):
        x, y, c = _place()
        me = 4 * x + 2 * y + c
        buf[me] = p_ref[...]
        peers = []
        for k in range(1, 8):
            fx, fy, fc = (k >> 2) & 1, (k >> 1) & 1, k & 1
            peers.append((x ^ fx, y ^ fy, c ^ fc))
        cps = [pltpu.make_async_remote_copy(src_ref=p_ref, dst_ref=buf.at[me], send_sem=send_sems.at[k], recv_sem=recv_sems.at[k],
                                            device_id=peer, device_id_type=MESH) for k, peer in enumerate(peers)]
        for cp in cps:
            cp.start()
        for k, (px, py, pc) in enumerate(peers):
            pltpu.make_async_remote_copy(src_ref=p_ref, dst_ref=buf.at[4 * px + 2 * py + pc], send_sem=send_sems.at[k],
                                         recv_sem=recv_sems.at[k], device_id=(px, py, pc), device_id_type=MESH).wait_recv()
        for cp in cps:
            cp.wait_send()
        acc = buf[0]
        for k in range(1, 8):
            acc = acc + buf[k]
        o_ref[...] = acc

    vm = pl.BlockSpec(memory_space=pltpu.VMEM)
    return pl.pallas_call(
        body, in_specs=[vm], out_specs=vm, out_shape=jax.ShapeDtypeStruct((r, w), F32),
        scratch_shapes=[pltpu.VMEM((8, r, w), F32), pltpu.SemaphoreType.DMA((7,)), pltpu.SemaphoreType.DMA((7,))],
        name="all_reduce_small")(packet)


def _adamw_math(w, g, m, v):
    m = ADAM_B1 * m + (1.0 - ADAM_B1) * g
    v = ADAM_B2 * v + (1.0 - ADAM_B2) * (g * g)
    m_hat = m / (1.0 - ADAM_B1 ** ADAM_STEP)
    v_hat = v / (1.0 - ADAM_B2 ** ADAM_STEP)
    return -ADAM_LR * (m_hat / (jnp.sqrt(v_hat) + ADAM_EPS) + ADAM_WD * w), m, v


def _adamw(w, g, m, v, name):
    nl, r, c = w.shape
    tr = 256 if r % 256 == 0 else r

    def body(w_ref, g_ref, m_ref, v_ref, d_ref, mo_ref, vo_ref):
        d_ref[...], mo_ref[...], vo_ref[...] = _adamw_math(w_ref[...], g_ref[...], m_ref[...], v_ref[...])

    blk = pl.BlockSpec((None, tr, c), lambda l, i: (l, i, 0))
    out = jax.ShapeDtypeStruct(w.shape, F32)
    return pl.pallas_call(
        body, grid=(nl, r // tr), in_specs=[blk] * 4, out_specs=[blk] * 3, out_shape=[out] * 3,
        compiler_params=_cparams(("parallel", "parallel")), name=name)(w, g, m, v)


def _small_update(gsum, lower_bounds, wpack, mpack, vpack):
    r, w = gsum.shape
    lb_rows = DEPTH * HGRN_W // 128

    def body(g_ref, a_ref, w_ref, m_ref, v_ref, go_ref, d_ref, mo_ref, vo_ref):
        a = a_ref[...]
        a0, a1 = a[0:1], a[1:2]
        mx = jnp.maximum(a0, a1)
        e0, e1 = jnp.exp(a0 - mx), jnp.exp(a1 - mx)
        p0, p1 = e0 / (e0 + e1), e1 / (e0 + e1)
        g = g_ref[...]
        half = lb_rows // 2
        dl0 = jnp.concatenate([g[k:k + 1] for k in range(half)], axis=1)
        dl1 = jnp.concatenate([g[half + k:half + k + 1] for k in range(half)], axis=1)
        dp0 = (dl0 + dl1) - (dl0 + dl1)
        dp1 = dl1
        inner = p0 * dp0 + p1 * dp1
        da0, da1 = p0 * (dp0 - inner), p1 * (dp1 - inner)
        rows = [da0[:, 128 * k:128 * (k + 1)] for k in range(half)] + [da1[:, 128 * k:128 * (k + 1)] for k in range(half)]
        gfull = jnp.concatenate(rows + [g[lb_rows:]], axis=0)
        go_ref[...] = gfull
        d_ref[...], mo_ref[...], vo_ref[...] = _adamw_math(w_ref[...], gfull, m_ref[...], v_ref[...])

    vm = pl.BlockSpec(memory_space=pltpu.VMEM)
    out = jax.ShapeDtypeStruct((r, w), F32)
    return pl.pallas_call(body, in_specs=[vm] * 5, out_specs=[vm] * 4, out_shape=[out] * 4, name="small_update")(
        gsum, lower_bounds, wpack, mpack, vpack)


_SMALL = ("lower_bounds", "pre_norm_g", "hgrn_norm_g", "fox_f_bias", "pool_w", "pool_scale", "post_norm_g")


def _pack(parts):
    rows = []
    for k in _SMALL:
        f = parts[k].reshape(-1)
        pad = (-f.shape[0]) % (8 * 128)
        rows.append(jnp.pad(f, (0, pad)).reshape(-1, 128))
    rows.append(jnp.zeros((8, 128), F32))
    return jnp.concatenate(rows, axis=0)


def _unpack(pack, like):
    out, r = {}, 0
    for k in _SMALL:
        size = int(np.prod(like[k].shape))
        nr = -(-size // (8 * 128)) * 8
        out[k] = pack[r:r + nr].reshape(-1)[:size].reshape(like[k].shape)
        r += nr
    return out, r


def kernel(x, lower_bounds, pre_norm_g, w_in, hgrn_norm_g, fox_f_bias, pool_w, pool_scale, w_out, post_norm_g, loss_target, m_lower_bounds, m_pre_norm_g, m_w_in, m_hgrn_norm_g, m_fox_f_bias, m_pool_w, m_pool_scale, m_w_out, m_post_norm_g, v_lower_bounds, v_pre_norm_g, v_w_in, v_hgrn_norm_g, v_fox_f_bias, v_pool_w, v_pool_scale, v_w_out, v_post_norm_g):
    cx, cy, cc = _place()
    chip = 2 * cx + cy

    ain, aout = _gather_weights(w_in.astype(BF16), w_out.astype(BF16))
    w_in_int = _internal_from_shards([ain[q] for q in range(N_CHIPS)])
    w_out_full = jnp.concatenate([aout[q] for q in range(N_CHIPS)], axis=1)

    sq, grad_x, g = _local_step(x, loss_target, lower_bounds, pre_norm_g, w_in_int, hgrn_norm_g, fox_f_bias, pool_w,
                                pool_scale, w_out_full, post_norm_g)

    layer = lambda a, l: lax.dynamic_index_in_dim(a, l, axis=0, keepdims=False)
    blocks_in = lambda l: _shards_from_internal(layer(g["w_in"], l))
    blocks_out = lambda l: layer(g["w_out"], l).reshape(N_CHIPS, 256, D_MODEL)
    mine_in, mine_out = blocks_in(cc), blocks_out(cc)
    sib_in, sib_out = _swap_with_sibling([blocks_in(1 - cc), blocks_out(1 - cc)], "grad_swap1")
    rin, rout = 4 * 1024, 4 * 256
    sum_in, send_in = _add_n([mine_in.reshape(rin, SHARD_W), sib_in.reshape(rin, SHARD_W)], "grad_add1_in", True)
    sum_out, send_out = _add_n([mine_out.reshape(rout, D_MODEL), sib_out.reshape(rout, D_MODEL)], "grad_add1_out", True)
    sum_in, sum_out = sum_in.reshape(4, 1024, SHARD_W), sum_out.reshape(4, 256, D_MODEL)
    got_in, got_out = _scatter_to_chips([send_in.reshape(4, 1024, SHARD_W), send_out.reshape(4, 256, D_MODEL)], "grad_scatter")
    own = lambda a: lax.dynamic_index_in_dim(a, chip, axis=0, keepdims=False)
    half_in = _add_n([own(sum_in)] + [got_in[k] for k in range(3)], "grad_add2_in")
    half_out = _add_n([own(sum_out)] + [got_out[k] for k in range(3)], "grad_add2_out")
    oth_in, oth_out = _swap_with_sibling([half_in, half_out], "grad_swap2")
    first = cc == 0
    grad_w_in = jnp.stack([jnp.where(first, half_in, oth_in), jnp.where(first, oth_in, half_in)])
    grad_w_out = jnp.stack([jnp.where(first, half_out, oth_out), jnp.where(first, oth_out, half_out)])

    small = {"lower_bounds": g["lbs"], "pre_norm_g": g["pre"], "hgrn_norm_g": g["hgn"], "fox_f_bias": g["bias"],
             "pool_w": g["pool_w"], "pool_scale": g["pool_scale"], "post_norm_g": g["post"]}
    packet = _pack(small)
    nrows = packet.shape[0]
    packet = packet.at[nrows - 1].set(sq[0])
    gsum = _all_reduce_small(packet)
    loss = gsum[nrows - 1, 0] * (0.5 / D_MODEL)

    weights = {"lower_bounds": lower_bounds, "pre_norm_g": pre_norm_g, "hgrn_norm_g": hgrn_norm_g,
               "fox_f_bias": fox_f_bias, "pool_w": pool_w, "pool_scale": pool_scale, "post_norm_g": post_norm_g}
    moments_m = {"lower_bounds": m_lower_bounds, "pre_norm_g": m_pre_norm_g, "hgrn_norm_g": m_hgrn_norm_g,
                 "fox_f_bias": m_fox_f_bias, "pool_w": m_pool_w, "pool_scale": m_pool_scale, "post_norm_g": m_post_norm_g}
    moments_v = {"lower_bounds": v_lower_bounds, "pre_norm_g": v_pre_norm_g, "hgrn_norm_g": v_hgrn_norm_g,
                 "fox_f_bias": v_fox_f_bias, "pool_w": v_pool_w, "pool_scale": v_pool_scale, "post_norm_g": v_post_norm_g}
    gp, dp, mp, vp = _small_update(gsum, lower_bounds, _pack(weights), _pack(moments_m), _pack(moments_v))
    gs, _ = _unpack(gp, weights)
    ds, _ = _unpack(dp, weights)
    ms, _ = _unpack(mp, weights)
    vs, _ = _unpack(vp, weights)

    d_in, m_in, v_in = _adamw(w_in, grad_w_in, m_w_in, v_w_in, "adamw_w_in")
    d_out, m_out, v_out = _adamw(w_out, grad_w_out, m_w_out, v_w_out, "adamw_w_out")

    def ordered(s, big_in, big_out):
        return (s["lower_bounds"], s["pre_norm_g"], big_in, s["hgrn_norm_g"], s["fox_f_bias"], s["pool_w"],
                s["pool_scale"], big_out, s["post_norm_g"])

    return (loss, grad_x, *ordered(gs, grad_w_in, grad_w_out), *ordered(ds, d_in, d_out),
            *ordered(ms, m_in, m_out), *ordered(vs, v_in, v_out))
```

```python
import functools

import numpy as np
import jax
import jax.numpy as jnp
from jax import lax
from jax.experimental import pallas as pl
from jax.experimental.pallas import tpu as pltpu
from jax.experimental.pallas import tpu_sc as plsc

F32 = jnp.float32
BF16 = jnp.bfloat16
HI = lax.Precision.HIGHEST
MESH = pl.DeviceIdType.MESH

NORM_EPS = 1e-6
MASK_VALUE = -1e30
TINY = 1e-30
ADAM_LR, ADAM_B1, ADAM_B2, ADAM_EPS, ADAM_WD, ADAM_STEP = 0.001, 0.9, 0.999, 1e-08, 0.01, 10

D_MODEL = 1024
DEPTH = 2
N_CHIPS = 4
CHUNK = 64
LANES = 128
HGRN_W, POOL_W, FOX_W, FOX_HEADS = 256, 256, 512, 8
POOL_WINDOWS = (2, 4, 8, 16)
POOL_HALO = 16
IN_WIDTH = 3592
SHARD_W = IN_WIDTH // N_CHIPS
A_W, B_W, C_W, F_W = 1024, 512, 2048, 128
E_INT = A_W + B_W + C_W + F_W
B_BLK = A_W // 512
C_BLK0 = (A_W + B_W) // 512
F_BLK = (A_W + B_W + C_W) // 128


def _segments():
    segs = []
    for hp in range(2):
        for part in range(4):
            segs.append((part * 256 + hp * 128, 128))
    segs.append((1024, 256))
    segs.append((1280, 256))
    for hp in range(4):
        for part in range(4):
            segs.append((1536 + part * 512 + hp * 128, 128))
    segs.append((3584, 8))
    return segs


_SEGS = _segments()


def _to_internal(w):
    parts = [w[..., s:s + n] for s, n in _SEGS]
    parts.append(jnp.zeros(w.shape[:-1] + (E_INT - IN_WIDTH,), w.dtype))
    return jnp.concatenate(parts, axis=-1)


def _to_original(w):
    offs, o = [], 0
    for s, n in _SEGS:
        offs.append((s, o, n))
        o += n
    parts = [w[..., o:o + n] for s, o, n in sorted(offs)]
    return jnp.concatenate(parts, axis=-1)


def _internal_from_shards(shards):
    parts = []
    for s, n in _SEGS:
        while n > 0:
            q, r = divmod(s, SHARD_W)
            take = min(n, SHARD_W - r)
            parts.append(shards[q][..., r:r + take])
            s, n = s + take, n - take
    parts.append(jnp.zeros(shards[0].shape[:-1] + (E_INT - IN_WIDTH,), shards[0].dtype))
    return jnp.concatenate(parts, axis=-1)


def _shards_from_internal(w):
    offs, o = [], 0
    for s, n in _SEGS:
        offs.append((s, o, n))
        o += n
    blocks = []
    for q in range(N_CHIPS):
        lo, hi = SHARD_W * q, SHARD_W * (q + 1)
        parts = [w[..., o + max(lo, s) - s:o + min(hi, s + n) - s] for s, o, n in sorted(offs) if s < hi and s + n > lo]
        blocks.append(jnp.concatenate(parts, axis=-1))
    return jnp.stack(blocks)


def _cparams(sem=None, vmem_mb=48):
    kw = dict(vmem_limit_bytes=vmem_mb * 1024 * 1024)
    if sem is not None:
        kw["dimension_semantics"] = sem
    return pltpu.CompilerParams(**kw)


def _sig(x):
    return 1.0 / (1.0 + jnp.exp(-x))


def _silu(x):
    return x * _sig(x)


def _dsilu(x):
    s = _sig(x)
    return s * (1.0 + x * (1.0 - s))


def _rstd(x):
    return lax.rsqrt(jnp.mean(x * x, axis=-1, keepdims=True) + NORM_EPS)


def _dot(a, b, dims, **kw):
    return lax.dot_general(a, b, (dims, ((), ())), preferred_element_type=F32, **kw)


NN = ((1,), (0,))
NT = ((1,), (1,))
TN = ((0,), (0,))


def _iota(shape, dim):
    return lax.broadcasted_iota(jnp.int32, shape, dim)


def _lbs_fwd(lower_bounds):
    def body(a_ref, o_ref):
        a = a_ref[...]
        a0, a1 = a[0:1], a[1:2]
        m = jnp.maximum(a0, a1)
        e0, e1 = jnp.exp(a0 - m), jnp.exp(a1 - m)
        p0, p1 = e0 / (e0 + e1), e1 / (e0 + e1)
        o_ref[...] = jnp.concatenate([p0 - p0, (p0 + p1) - p0], axis=0)

    return pl.pallas_call(body, out_shape=jax.ShapeDtypeStruct(lower_bounds.shape, F32), name="lbs_fwd")(lower_bounds)


def _inproj_fwd(x2, g_row, w_int, name):
    n, d = x2.shape
    e = w_int.shape[1]
    tm = min(256, n)

    def body(x_ref, g_ref, w_ref, o_ref):
        x = x_ref[...]
        h = (x * _rstd(x) * g_ref[...]).astype(BF16)
        o_ref[...] = jnp.dot(h, w_ref[...], preferred_element_type=F32)

    return pl.pallas_call(
        body, grid=(n // tm,),
        in_specs=[pl.BlockSpec((tm, d), lambda i: (i, 0)), pl.BlockSpec((1, d), lambda i: (0, 0)),
                  pl.BlockSpec((d, e), lambda i: (0, 0))],
        out_specs=pl.BlockSpec((tm, e), lambda i: (i, 0)),
        out_shape=jax.ShapeDtypeStruct((n, e), F32),
        compiler_params=_cparams(("parallel",)), name=name)(x2, g_row, w_int)


def _chunk_cumsum_matrix():
    i, j = _iota((LANES, LANES), 0), _iota((LANES, LANES), 1)
    return ((i <= j) & ((i // CHUNK) == (j // CHUNK))).astype(F32)


def _hgrn_gates(a, lb):
    qa, z = a[:, 0:128], a[:, 128:256]
    sg, sgn = _sig(z), _sig(-z)
    fg = lb + (1.0 - lb) * sg
    lf = jnp.log(jnp.maximum(fg, TINY))
    kk = (1.0 - lb) * sgn
    return qa * _sig(qa), kk, lf, sg, sgn, fg


def _hgrn_fwd(proj3, lbs_row, gn_col, name):
    bsz, t, _ = proj3.shape
    nt = t // LANES

    def body(a_ref, lb_ref, gn_ref, og_ref, or_ref):
        lb = lb_ref[...]
        gn = gn_ref[...]
        umat = _chunk_cumsum_matrix()
        lane64 = _iota((1, LANES), 1) % CHUNK

        def tile(i, carry):
            r0 = pl.multiple_of(i * LANES, LANES)
            a = a_ref[pl.ds(r0, LANES), :]
            qq, kk, lf, _, _, _ = _hgrn_gates(a, lb)
            va, ga = a[:, 256:384], a[:, 384:512]
            q_t, k_t, v_t = qq.T, kk.T, va.T
            b_t = jnp.dot(lf.T, umat, precision=HI, preferred_element_type=F32)
            new_s, o_heads = [], []
            for h in range(2):
                s_h = carry[h]
                rs = slice(CHUNK * h, CHUNK * (h + 1))
                qh, kh, vh, bh = q_t[rs], k_t[rs], v_t[rs], b_t[rs]
                inter = []
                for c in range(2):
                    cs = slice(CHUNK * c, CHUNK * (c + 1))
                    b_ = bh[:, cs]
                    qt = (qh[:, cs] * jnp.exp(b_)).astype(BF16)
                    inter.append(_dot(s_h.astype(BF16), qt, TN))
                    bl = b_[:, CHUNK - 1:CHUNK]
                    kt = (kh[:, cs] * jnp.exp(bl - b_)).astype(BF16)
                    s_h = jnp.exp(bl) * s_h + _dot(kt, vh[:, cs].astype(BF16), NT)
                new_s.append(s_h)

                acc = jnp.concatenate(inter, axis=1) + jnp.sum(qh * kh, axis=0, keepdims=True) * vh
                for dlt in range(1, CHUNK):
                    kr, br, vr = pltpu.roll(kh, dlt, 1), pltpu.roll(bh, dlt, 1), pltpu.roll(vh, dlt, 1)
                    e = jnp.exp(jnp.minimum(bh - br, 0.0))
                    att = jnp.sum(qh * kr * e, axis=0, keepdims=True)
                    acc = acc + jnp.where(lane64 >= dlt, att, 0.0) * vr
                o_heads.append(acc)
            normed = []
            for h in range(2):
                o_h = o_heads[h]
                ms = jnp.mean(o_h * o_h, axis=0, keepdims=True)
                normed.append(o_h * lax.rsqrt(ms + NORM_EPS) * gn[CHUNK * h:CHUNK * (h + 1)])
            or_ref[pl.ds(r0, LANES), :] = jnp.concatenate(o_heads, axis=0).T
            og_ref[pl.ds(r0, LANES), :] = jnp.concatenate(normed, axis=0).T * _silu(ga)
            return tuple(new_s)

        zero = jnp.zeros((CHUNK, CHUNK), F32)
        lax.fori_loop(0, nt, tile, (zero, zero))

    out = jax.ShapeDtypeStruct((bsz, t, HGRN_W), F32)
    return pl.pallas_call(
        body, grid=(bsz, 2),
        in_specs=[pl.BlockSpec((None, t, 512), lambda b, p: (b, 0, p)),
                  pl.BlockSpec((1, 128), lambda b, p: (0, p)),
                  pl.BlockSpec((128, 1), lambda b, p: (p, 0))],
        out_specs=[pl.BlockSpec((None, t, 128), lambda b, p: (b, 0, p)),
                   pl.BlockSpec((None, t, 128), lambda b, p: (b, 0, p))],
        out_shape=[out, out],
        compiler_params=_cparams(("parallel", "parallel")), name=name)(proj3, lbs_row, gn_col)


def _hgrn_bwd(proj3, o_raw, dmixed, lbs_row, gn_row, name):
    bsz, t, _ = proj3.shape
    nt = t // LANES
    nchunk = t // CHUNK

    def body(a_ref, or_ref, do_ref, lb_ref, gn_ref, da_ref, dgn_ref, dlb_ref, s_sc):
        lb = lb_ref[...]
        gn = gn_ref[...]
        umat = _chunk_cumsum_matrix()
        lane = _iota((1, LANES), 1)
        lane64 = lane % CHUNK
        half = lane < CHUNK

        def t_layout(a):
            qq, kk, lf, sg, sgn, fg = _hgrn_gates(a, lb)
            b_t = jnp.dot(lf.T, umat, precision=HI, preferred_element_type=F32)
            return qq.T, kk.T, a[:, 256:384].T, b_t, (sg, sgn, fg)

        def fwd_tile(i, carry):
            r0 = pl.multiple_of(i * LANES, LANES)
            q_t, k_t, v_t, b_t, _ = t_layout(a_ref[pl.ds(r0, LANES), :])
            new_s = []
            for h in range(2):
                s_h = carry[h]
                rs = slice(CHUNK * h, CHUNK * (h + 1))
                for c in range(2):
                    cs = slice(CHUNK * c, CHUNK * (c + 1))
                    s_sc[h, 2 * i + c] = s_h
                    b_ = b_t[rs, cs]
                    bl = b_[:, CHUNK - 1:CHUNK]
                    kt = (k_t[rs, cs] * jnp.exp(bl - b_)).astype(BF16)
                    s_h = jnp.exp(bl) * s_h + _dot(kt, v_t[rs, cs].astype(BF16), NT)
                new_s.append(s_h)
            return tuple(new_s)

        zero = jnp.zeros((CHUNK, CHUNK), F32)
        lax.fori_loop(0, nt, fwd_tile, (zero, zero))

        def half_mean(v):
            m0 = jnp.sum(jnp.where(half, v, 0.0), axis=1, keepdims=True) * (1.0 / CHUNK)
            m1 = jnp.sum(jnp.where(half, 0.0, v), axis=1, keepdims=True) * (1.0 / CHUNK)
            return jnp.where(half, m0, m1)

        def bwd_tile(k, carry):
            ds0, ds1, dgn_acc, dlb_acc = carry
            i = nt - 1 - k
            r0 = pl.multiple_of(i * LANES, LANES)
            a = a_ref[pl.ds(r0, LANES), :]
            qa, z, ga = a[:, 0:128], a[:, 128:256], a[:, 384:512]
            q_t, k_t, v_t, b_t, (sg, sgn, fg) = t_layout(a)
            oraw = or_ref[pl.ds(r0, LANES), :]
            dout = do_ref[pl.ds(r0, LANES), :]
            r = lax.rsqrt(half_mean(oraw * oraw) + NORM_EPS)
            xn = oraw * r
            dga = dout * (xn * gn) * _dsilu(ga)
            don = dout * _silu(ga)
            dgn_acc = dgn_acc + jnp.sum(don * xn, axis=0, keepdims=True)
            dxn = don * gn
            do_t = (r * (dxn - xn * half_mean(dxn * xn))).T
            new_ds, dq_h, dk_h, dv_h, db_h = [], [], [], [], []
            for h in range(2):
                ds_h = (ds0, ds1)[h]
                rs = slice(CHUNK * h, CHUNK * (h + 1))
                qh, kh, vh, bh, doh = q_t[rs], k_t[rs], v_t[rs], b_t[rs], do_t[rs]
                dq_c, dk_c, dv_c, dbl_c = [None, None], [None, None], [None, None], [None, None]
                for c in (1, 0):
                    cs = slice(CHUNK * c, CHUNK * (c + 1))
                    s_n = s_sc[h, 2 * i + c]
                    b_ = bh[:, cs]
                    eb = jnp.exp(b_)
                    bl = b_[:, CHUNK - 1:CHUNK]
                    ek = jnp.exp(bl - b_)
                    ebl = jnp.exp(bl)
                    qt, kt = qh[:, cs] * eb, kh[:, cs] * ek
                    do_c = doh[:, cs].astype(BF16)
                    dsb = ds_h.astype(BF16)
                    dv_c[c] = _dot(dsb, kt.astype(BF16), TN)
                    dkt = _dot(dsb, vh[:, cs].astype(BF16), NN)
                    dqt = _dot(s_n.astype(BF16), do_c, NN)
                    dbl_c[c] = jnp.sum(ds_h * s_n, axis=1, keepdims=True) * ebl + jnp.sum(dkt * kt, axis=1, keepdims=True)
                    dq_c[c], dk_c[c] = dqt * eb, dkt * ek
                    ds_h = ebl * ds_h + _dot(qt.astype(BF16), do_c, NT)
                new_ds.append(ds_h)

                att0 = jnp.sum(qh * kh, axis=0, keepdims=True)
                datt0 = jnp.sum(doh * vh, axis=0, keepdims=True)
                dqh = jnp.concatenate(dq_c, axis=1) + datt0 * kh
                dkh = jnp.concatenate(dk_c, axis=1) + datt0 * qh
                dvh = jnp.concatenate(dv_c, axis=1) + att0 * doh
                for dlt in range(1, CHUNK):
                    kr, br, vr = pltpu.roll(kh, dlt, 1), pltpu.roll(bh, dlt, 1), pltpu.roll(vh, dlt, 1)
                    e = jnp.where(lane64 >= dlt, jnp.exp(jnp.minimum(bh - br, 0.0)), 0.0)
                    qe = qh * e
                    att = jnp.sum(qe * kr, axis=0, keepdims=True)
                    datt = jnp.sum(doh * vr, axis=0, keepdims=True)
                    dqh = dqh + datt * (kr * e)
                    dkh = dkh + pltpu.roll(datt * qe, LANES - dlt, 1)
                    dvh = dvh + pltpu.roll(att * doh, LANES - dlt, 1)
                dbl = jnp.where(half, dbl_c[0], dbl_c[1])
                db_h.append(qh * dqh - kh * dkh + jnp.where(lane64 == CHUNK - 1, dbl, 0.0))
                dq_h.append(dqh)
                dk_h.append(dkh)
                dv_h.append(dvh)
            dqq = jnp.concatenate(dq_h, axis=0).T
            dkk = jnp.concatenate(dk_h, axis=0).T
            dvv = jnp.concatenate(dv_h, axis=0).T
            dlf = _dot(jnp.concatenate(db_h, axis=0), umat, NT, precision=HI).T
            dqa = dqq * _dsilu(qa)
            dfg = jnp.where(fg > TINY, dlf / fg, 0.0)
            dz = (dfg - dkk) * (1.0 - lb) * sg * sgn
            dlb_acc = dlb_acc + jnp.sum(dfg * (1.0 - sg) - dkk * sgn, axis=0, keepdims=True)
            da_ref[pl.ds(r0, LANES), :] = jnp.concatenate([dqa, dz, dvv, dga], axis=1)
            return new_ds[0], new_ds[1], dgn_acc, dlb_acc

        zrow = jnp.zeros((1, LANES), F32)
        _, _, dgn_acc, dlb_acc = lax.fori_loop(0, nt, bwd_tile, (zero, zero, zrow, zrow))
        dgn_ref[...] = jnp.broadcast_to(dgn_acc, (8, LANES))
        dlb_ref[...] = jnp.broadcast_to(dlb_acc, (8, LANES))

    rows = jax.ShapeDtypeStruct((bsz, 8, HGRN_W), F32)
    return pl.pallas_call(
        body, grid=(bsz, 2),
        in_specs=[pl.BlockSpec((None, t, 512), lambda b, p: (b, 0, p)),
                  pl.BlockSpec((None, t, 128), lambda b, p: (b, 0, p)),
                  pl.BlockSpec((None, t, 128), lambda b, p: (b, 0, p)),
                  pl.BlockSpec((1, 128), lambda b, p: (0, p)),
                  pl.BlockSpec((1, 128), lambda b, p: (0, p))],
        out_specs=[pl.BlockSpec((None, t, 512), lambda b, p: (b, 0, p)),
                   pl.BlockSpec((None, 8, 128), lambda b, p: (b, 0, p)),
                   pl.BlockSpec((None, 8, 128), lambda b, p: (b, 0, p))],
        out_shape=[jax.ShapeDtypeStruct((bsz, t, A_W), F32), rows, rows],
        scratch_shapes=[pltpu.VMEM((2, nchunk, CHUNK, CHUNK), F32)],
        compiler_params=_cparams(("parallel", "parallel")), name=name)(proj3, o_raw, dmixed, lbs_row, gn_row)


N_LEVELS = 6


def _hgrn_tables():
    t = np.arange(LANES)
    j = np.arange(LANES)[None, :]
    same_chunk = (t[:, None] // CHUNK) == (j // CHUNK)
    w = np.zeros((2 + N_LEVELS, LANES, LANES), np.float32)
    w[0] = same_chunk & (j <= t[:, None])
    w[1] = same_chunk & (j > t[:, None])
    maskf = np.zeros((N_LEVELS, LANES, LANES), np.float32)
    rightf = np.zeros((N_LEVELS, LANES, LANES), np.float32)
    for li in range(N_LEVELS):
        m = (CHUNK // 2) >> li
        start = t - (t % (2 * m))
        right = (t % (2 * m)) >= m
        first = np.where(right, start + m, t + 1)
        last = np.where(right, t, start + m - 1)
        w[2 + li] = (j >= first[:, None]) & (j <= last[:, None])
        maskf[li] = (t[:, None] // (2 * m)) == (j // (2 * m))
        rightf[li] = right[:, None]
    return jnp.asarray(w.reshape(-1, LANES), BF16), jnp.asarray(maskf), jnp.asarray(rightf)


def _split(x, n):
    parts = []
    for _ in range(n - 1):
        p = x.astype(BF16)
        parts.append(p)
        x = x - p.astype(F32)
    parts.append(x.astype(BF16))
    return parts


def _exact_dot(w, parts):
    acc = jnp.dot(w, parts[0], preferred_element_type=F32)
    for p in parts[1:]:
        acc = acc + jnp.dot(w, p, preferred_element_type=F32)
    return acc


def _head_sums(v, ones_blk, n=2):
    parts = _split(v, n)
    acc = jnp.dot(parts[0], ones_blk, preferred_element_type=F32)
    for p in parts[1:]:
        acc = acc + jnp.dot(p, ones_blk, preferred_element_type=F32)
    return acc


def _hgrn_consts():
    r, c = _iota((LANES, LANES), 0), _iota((LANES, LANES), 1)
    eye = r == c
    ones_blk = ((r // CHUNK) == (c // CHUNK)).astype(BF16)
    return eye, ones_blk, jnp.ones((CHUNK, LANES), BF16)


def _hgrn_levels(qq, kk, zall, mk_ref, rt_ref, d_att=None):
    att = [jnp.zeros((LANES, LANES), F32)] * 2
    dq = dk = db = jnp.zeros((LANES, LANES), F32)
    for li in range(N_LEVELS):
        e = jnp.exp(zall[(2 + li) * LANES:(3 + li) * LANES])
        rt = rt_ref[li]
        mk = mk_ref[li]
        qef, kef = e * rt, e * (1.0 - rt)
        qe, ke = (qq * qef).astype(BF16), (kk * kef).astype(BF16)
        dqs, dks = [], []
        for h in range(2):
            hs = slice(CHUNK * h, CHUNK * (h + 1))
            att[h] = att[h] + _dot(qe[:, hs], ke[:, hs], NT) * mk
            if d_att is not None:
                dam = (d_att[h] * mk).astype(BF16)
                dqs.append(jnp.dot(dam, ke[:, hs], preferred_element_type=F32))
                dks.append(_dot(dam, qe[:, hs], TN))
        if d_att is not None:
            dqe, dke = jnp.concatenate(dqs, axis=1), jnp.concatenate(dks, axis=1)
            dq = dq + dqe * qef
            dk = dk + dke * kef
            db = db + (dqe * qe.astype(F32) - dke * ke.astype(F32))
    return att, dq, dk, db


def _hgrn_fwd(proj3, lbs_row, gn_row, name):
    bsz, t, _ = proj3.shape
    nt = t // LANES
    w_all, maskf, rightf = _hgrn_tables()

    def body(a_ref, lb_ref, gn_ref, w_ref, mk_ref, rt_ref, og_ref, or_ref, st_ref):
        lb = lb_ref[...]
        gn = gn_ref[...]
        eye, ones_blk, ones_h = _hgrn_consts()

        def tile(i, carry):
            r0 = pl.multiple_of(i * LANES, LANES)
            a = a_ref[pl.ds(r0, LANES), :]
            qq, kk, lf, _, _, _ = _hgrn_gates(a, lb)
            va, ga = a[:, 256:384], a[:, 384:512]
            parts = _split(lf, 3)
            zall = _exact_dot(w_ref[...], parts)
            eb, ee = jnp.exp(zall[0:LANES]), jnp.exp(zall[LANES:2 * LANES])
            vb = va.astype(BF16)
            att, _, _, _ = _hgrn_levels(qq, kk, zall, mk_ref, rt_ref)
            qk = _split(qq * kk, 2)
            qeb, keb = (qq * eb).astype(BF16), (kk * ee).astype(BF16)
            new_s, o_heads = [], []
            for h in range(2):
                hs = slice(CHUNK * h, CHUNK * (h + 1))
                diag = _exact_dot_r(qk, hs, ones_h)
                a_h = att[h] + jnp.where(eye, diag, 0.0)
                o_h = jnp.dot(a_h.astype(BF16), vb[:, hs], preferred_element_type=F32)
                st = carry[h]
                chunks = []
                for c in range(2):
                    rc = slice(CHUNK * c, CHUNK * (c + 1))
                    st_ref[h, 2 * i + c] = st
                    chunks.append(o_h[rc] + _dot(qeb[rc, hs], st.astype(BF16), NT))
                    ebl = eb[CHUNK * (c + 1) - 1:CHUNK * (c + 1), hs]
                    st = st * ebl + _dot(vb[rc, hs], keb[rc, hs], TN)
                new_s.append(st)
                o_heads.append(jnp.concatenate(chunks, axis=0))
            o = jnp.concatenate(o_heads, axis=1)
            ms = _head_sums(o * o, ones_blk) * (1.0 / CHUNK)
            or_ref[pl.ds(r0, LANES), :] = o
            og_ref[pl.ds(r0, LANES), :] = o * lax.rsqrt(ms + NORM_EPS) * gn * _silu(ga)
            return tuple(new_s)

        zero = jnp.zeros((CHUNK, CHUNK), F32)
        lax.fori_loop(0, nt, tile, (zero, zero))

    out = jax.ShapeDtypeStruct((bsz, t, HGRN_W), F32)
    row = pl.BlockSpec((1, 128), lambda b, p: (0, p))
    return pl.pallas_call(
        body, grid=(bsz, 2),
        in_specs=[pl.BlockSpec((None, t, 512), lambda b, p: (b, 0, p)), row, row,
                  pl.BlockSpec(w_all.shape, lambda b, p: (0, 0)),
                  pl.BlockSpec(maskf.shape, lambda b, p: (0, 0, 0)),
                  pl.BlockSpec(rightf.shape, lambda b, p: (0, 0, 0))],
        out_specs=[pl.BlockSpec((None, t, 128), lambda b, p: (b, 0, p)),
                   pl.BlockSpec((None, t, 128), lambda b, p: (b, 0, p)),
                   pl.BlockSpec((None, 2, t // CHUNK, CHUNK, CHUNK), lambda b, p: (b, p, 0, 0, 0))],
        out_shape=[out, out, jax.ShapeDtypeStruct((bsz, 4, t // CHUNK, CHUNK, CHUNK), F32)],
        compiler_params=_cparams(("parallel", "parallel")), name=name)(proj3, lbs_row, gn_row, w_all, maskf, rightf)


def _exact_dot_r(parts, hs, ones_h):
    acc = jnp.dot(parts[0][:, hs], ones_h, preferred_element_type=F32)
    for p in parts[1:]:
        acc = acc + jnp.dot(p[:, hs], ones_h, preferred_element_type=F32)
    return acc


def _hgrn_bwd(proj3, o_raw, dmixed, states, lbs_row, gn_row, name):
    bsz, t, _ = proj3.shape
    nt = t // LANES
    nchunk = t // CHUNK
    w_all, maskf, rightf = _hgrn_tables()

    def body(a_ref, or_ref, do_ref, s_sc, lb_ref, gn_ref, w_ref, mk_ref, rt_ref, da_ref, dgn_ref, dlb_ref):
        lb = lb_ref[...]
        gn = gn_ref[...]
        eye, ones_blk, ones_h = _hgrn_consts()
        r_i, c_i = _iota((LANES, LANES), 0), _iota((LANES, LANES), 1)
        suffix = ((c_i >= r_i) & ((r_i // CHUNK) == (c_i // CHUNK))).astype(BF16)
        row64 = _iota((LANES, CHUNK), 0)
        ones_t = jnp.ones((LANES, CHUNK), BF16)
        zero = jnp.zeros((CHUNK, CHUNK), F32)

        def bwd_tile(k, carry):
            dst0, dst1, dgn_acc, dlb_acc = carry
            i = nt - 1 - k
            r0 = pl.multiple_of(i * LANES, LANES)
            a = a_ref[pl.ds(r0, LANES), :]
            qa, ga = a[:, 0:128], a[:, 384:512]
            qq, kk, lf, sg, sgn, fg = _hgrn_gates(a, lb)
            parts = _split(lf, 3)
            zall = _exact_dot(w_ref[...], parts)
            eb, ee = jnp.exp(zall[0:LANES]), jnp.exp(zall[LANES:2 * LANES])
            vb = a[:, 256:384].astype(BF16)
            oraw = or_ref[pl.ds(r0, LANES), :]
            dout = do_ref[pl.ds(r0, LANES), :]
            r = lax.rsqrt(_head_sums(oraw * oraw, ones_blk) * (1.0 / CHUNK) + NORM_EPS)
            xn = oraw * r
            dga = dout * (xn * gn) * _dsilu(ga)
            don = dout * _silu(ga)
            dgn_acc = dgn_acc + jnp.sum(don * xn, axis=0, keepdims=True)
            dxn = don * gn
            do = r * (dxn - xn * (_head_sums(dxn * xn, ones_blk) * (1.0 / CHUNK)))
            dob = do.astype(BF16)
            d_att = [_dot(dob[:, CHUNK * h:CHUNK * (h + 1)], vb[:, CHUNK * h:CHUNK * (h + 1)], NT) for h in range(2)]
            att, dq, dk, db_lv = _hgrn_levels(qq, kk, zall, mk_ref, rt_ref, d_att)
            qk = _split(qq * kk, 2)
            qe_f, ke_f = qq * eb, kk * ee
            qeb, keb = qe_f.astype(BF16), ke_f.astype(BF16)
            new_ds, dq_h, dk_h, dv_h, dbl_h = [], [], [], [], []
            for h in range(2):
                hs = slice(CHUNK * h, CHUNK * (h + 1))
                a_h = att[h] + jnp.where(eye, _exact_dot_r(qk, hs, ones_h), 0.0)
                dv = _dot(a_h.astype(BF16), dob[:, hs], TN)
                ddiag = _exact_dot_r(_split(jnp.where(eye, d_att[h], 0.0), 2), slice(None), ones_t)
                dq_i = dq[:, hs] + ddiag * kk[:, hs]
                dk_i = dk[:, hs] + ddiag * qq[:, hs]
                dst = (dst0, dst1)[h]
                dq_c, dk_c, dv_c, dbl_c = [None, None], [None, None], [None, None], [None, None]
                for c in (1, 0):
                    rc = slice(CHUNK * c, CHUNK * (c + 1))
                    st_n = s_sc[h, 2 * i + c]
                    ebl = eb[CHUNK * (c + 1) - 1:CHUNK * (c + 1), hs]
                    dstb = dst.astype(BF16)
                    dv_c[c] = _dot(keb[rc, hs], dstb, NT)
                    dke = jnp.dot(vb[rc, hs], dstb, preferred_element_type=F32)
                    dqe = jnp.dot(dob[rc, hs], st_n.astype(BF16), preferred_element_type=F32)
                    dbl_c[c] = (jnp.sum(dst * st_n, axis=0, keepdims=True) * ebl
                                + jnp.sum(dke * ke_f[rc, hs], axis=0, keepdims=True))
                    dq_c[c], dk_c[c] = dqe * eb[rc, hs], dke * ee[rc, hs]
                    dst = dst * ebl + _dot(dob[rc, hs], qeb[rc, hs], TN)
                new_ds.append(dst)
                dq_x, dk_x = jnp.concatenate(dq_c, axis=0), jnp.concatenate(dk_c, axis=0)
                dq_h.append(dq_i + dq_x)
                dk_h.append(dk_i + dk_x)
                dv_h.append(dv + jnp.concatenate(dv_c, axis=0))
                dbl_h.append(qq[:, hs] * dq_x - kk[:, hs] * dk_x
                             + jnp.where(row64 == CHUNK - 1, dbl_c[0], 0.0) + jnp.where(row64 == LANES - 1, dbl_c[1], 0.0))
            dqq = jnp.concatenate(dq_h, axis=1)
            dkk = jnp.concatenate(dk_h, axis=1)
            dvv = jnp.concatenate(dv_h, axis=1)
            db = db_lv + jnp.concatenate(dbl_h, axis=1)
            dlf = _exact_dot(suffix, _split(db, 3))
            dqa = dqq * _dsilu(qa)
            dfg = jnp.where(fg > TINY, dlf / fg, 0.0)
            dz = (dfg - dkk) * (1.0 - lb) * sg * sgn
            dlb_acc = dlb_acc + jnp.sum(dfg * (1.0 - sg) - dkk * sgn, axis=0, keepdims=True)
            da_ref[pl.ds(r0, LANES), :] = jnp.concatenate([dqa, dz, dvv, dga], axis=1)
            return new_ds[0], new_ds[1], dgn_acc, dlb_acc

        zrow = jnp.zeros((1, LANES), F32)
        _, _, dgn_acc, dlb_acc = lax.fori_loop(0, nt, bwd_tile, (zero, zero, zrow, zrow))
        dgn_ref[...] = jnp.broadcast_to(dgn_acc, (8, LANES))
        dlb_ref[...] = jnp.broadcast_to(dlb_acc, (8, LANES))

    rows = jax.ShapeDtypeStruct((bsz, 8, HGRN_W), F32)
    row = pl.BlockSpec((1, 128), lambda b, p: (0, p))
    blk = pl.BlockSpec((None, t, 128), lambda b, p: (b, 0, p))
    return pl.pallas_call(
        body, grid=(bsz, 2),
        in_specs=[pl.BlockSpec((None, t, 512), lambda b, p: (b, 0, p)), blk, blk,
                  pl.BlockSpec((None, 2, nchunk, CHUNK, CHUNK), lambda b, p: (b, p, 0, 0, 0)), row, row,
                  pl.BlockSpec(w_all.shape, lambda b, p: (0, 0)),
                  pl.BlockSpec(maskf.shape, lambda b, p: (0, 0, 0)),
                  pl.BlockSpec(rightf.shape, lambda b, p: (0, 0, 0))],
        out_specs=[pl.BlockSpec((None, t, 512), lambda b, p: (b, 0, p)),
                   pl.BlockSpec((None, 8, 128), lambda b, p: (b, 0, p)),
                   pl.BlockSpec((None, 8, 128), lambda b, p: (b, 0, p))],
        out_shape=[jax.ShapeDtypeStruct((bsz, t, A_W), F32), rows, rows],
        compiler_params=_cparams(("parallel", "parallel")), name=name)(
            proj3, o_raw, dmixed, states, lbs_row, gn_row, w_all, maskf, rightf)


def _pool_tt(t):
    return min(256, t)


def _window_select(s2, s4, s8, s16, lane):
    return jnp.where(lane < 64, s2, jnp.where(lane < 128, s4, jnp.where(lane < 192, s8, s16)))


def _pool_counts(t0, tt):
    lane = _iota((tt, POOL_W), 1)
    tpos = (_iota((tt, POOL_W), 0) + t0 + 1).astype(F32)
    win = jnp.where(lane < 64, 2.0, jnp.where(lane < 128, 4.0, jnp.where(lane < 192, 8.0, 16.0)))
    return 1.0 / jnp.minimum(tpos, win), lane


def _pooled_tile(upad_ref, i, tt):
    r0 = pl.multiple_of(i * tt, 8)
    cat = upad_ref[pl.ds(r0, tt + POOL_HALO), :]
    s2 = cat + pltpu.roll(cat, 1, 0)
    s4 = s2 + pltpu.roll(s2, 2, 0)
    s8 = s4 + pltpu.roll(s4, 4, 0)
    s16 = s8 + pltpu.roll(s8, 8, 0)
    inv, lane = _pool_counts(i * tt, tt)
    sel = _window_select(s2[POOL_HALO:], s4[POOL_HALO:], s8[POOL_HALO:], s16[POOL_HALO:], lane)
    return sel * inv - cat[POOL_HALO:], inv, lane


def _pool_fwd(proj3, wbd, scale_row, name):
    bsz, t, _ = proj3.shape
    tt = _pool_tt(t)

    def body(p_ref, w_ref, sc_ref, o_ref, upad):
        upad[0:POOL_HALO, :] = jnp.zeros((POOL_HALO, POOL_W), F32)
        upad[POOL_HALO:, :] = p_ref[:, 0:POOL_W]
        w = w_ref[...]
        sc = sc_ref[...]

        def tile(i, c):
            pooled, _, _ = _pooled_tile(upad, i, tt)
            r0 = pl.multiple_of(i * tt, 8)
            g = p_ref[pl.ds(r0, tt), POOL_W:2 * POOL_W]
            pre = jnp.dot(pooled.astype(BF16), w, preferred_element_type=F32)
            o_ref[pl.ds(r0, tt), :] = pre * sc * _silu(g)
            return c

        lax.fori_loop(0, t // tt, tile, 0)

    return pl.pallas_call(
        body, grid=(bsz,),
        in_specs=[pl.BlockSpec((None, t, 512), lambda b: (b, 0, B_BLK)),
                  pl.BlockSpec((POOL_W, POOL_W), lambda b: (0, 0)),
                  pl.BlockSpec((1, POOL_W), lambda b: (0, 0))],
        out_specs=pl.BlockSpec((None, t, POOL_W), lambda b: (b, 0, 0)),
        out_shape=jax.ShapeDtypeStruct((bsz, t, POOL_W), F32),
        scratch_shapes=[pltpu.VMEM((t + POOL_HALO, POOL_W), F32)],
        compiler_params=_cparams(("parallel",)), name=name)(proj3, wbd, scale_row)


def _pool_bwd(proj3, dmixed, wbd, scale_row, name):
    bsz, t, _ = proj3.shape
    tt = _pool_tt(t)

    def body(p_ref, do_ref, w_ref, sc_ref, db_ref, dsc_ref, dw_ref, upad, epad):
        upad[0:POOL_HALO, :] = jnp.zeros((POOL_HALO, POOL_W), F32)
        upad[POOL_HALO:, :] = p_ref[:, 0:POOL_W]
        epad[t:, :] = jnp.zeros((POOL_HALO, POOL_W), F32)
        w = w_ref[...]
        sc = sc_ref[...]

        def tile(i, carry):
            dsc_acc, dw_acc = carry
            pooled, inv, _ = _pooled_tile(upad, i, tt)
            r0 = pl.multiple_of(i * tt, 8)
            g = p_ref[pl.ds(r0, tt), POOL_W:2 * POOL_W]
            dout = do_ref[pl.ds(r0, tt), :]
            pb = pooled.astype(BF16)
            pre = jnp.dot(pb, w, preferred_element_type=F32)
            t1 = dout * _silu(g)
            dsc_acc = dsc_acc + jnp.sum(t1 * pre, axis=0, keepdims=True)
            dpre = (t1 * sc).astype(BF16)
            db_ref[pl.ds(r0, tt), POOL_W:2 * POOL_W] = dout * pre * sc * _dsilu(g)
            dw_acc = dw_acc + _dot(pb, dpre, TN)
            dpooled = _dot(dpre, w, NT)
            epad[pl.ds(r0, tt), :] = dpooled * inv
            return dsc_acc, dw_acc

        dsc_acc, dw_acc = lax.fori_loop(0, t // tt, tile, (jnp.zeros((1, POOL_W), F32), jnp.zeros((POOL_W, POOL_W), F32)))
        dsc_ref[...] = jnp.broadcast_to(dsc_acc, (8, POOL_W))
        dw_ref[...] = dw_acc

        def tile2(i, c):
            r0 = pl.multiple_of(i * tt, 8)
            n = tt + POOL_HALO
            cat = epad[pl.ds(r0, n), :]
            s2 = cat + pltpu.roll(cat, n - 1, 0)
            s4 = s2 + pltpu.roll(s2, n - 2, 0)
            s8 = s4 + pltpu.roll(s4, n - 4, 0)
            s16 = s8 + pltpu.roll(s8, n - 8, 0)
            inv, lane = _pool_counts(i * tt, tt)
            sel = _window_select(s2[:tt], s4[:tt], s8[:tt], s16[:tt], lane)
            db_ref[pl.ds(r0, tt), 0:POOL_W] = sel - cat[:tt] / inv
            return c

        lax.fori_loop(0, t // tt, tile2, 0)

    return pl.pallas_call(
        body, grid=(bsz,),
        in_specs=[pl.BlockSpec((None, t, 512), lambda b: (b, 0, B_BLK)),
                  pl.BlockSpec((None, t, POOL_W), lambda b: (b, 0, 1)),
                  pl.BlockSpec((POOL_W, POOL_W), lambda b: (0, 0)),
                  pl.BlockSpec((1, POOL_W), lambda b: (0, 0))],
        out_specs=[pl.BlockSpec((None, t, 512), lambda b: (b, 0, 0)),
                   pl.BlockSpec((None, 8, POOL_W), lambda b: (b, 0, 0)),
                   pl.BlockSpec((None, POOL_W, POOL_W), lambda b: (b, 0, 0))],
        out_shape=[jax.ShapeDtypeStruct((bsz, t, B_W), F32), jax.ShapeDtypeStruct((bsz, 8, POOL_W), F32),
                   jax.ShapeDtypeStruct((bsz, POOL_W, POOL_W), F32)],
        scratch_shapes=[pltpu.VMEM((t + POOL_HALO, POOL_W), F32), pltpu.VMEM((t + POOL_HALO, POOL_W), F32)],
        compiler_params=_cparams(("parallel",)), name=name)(proj3, dmixed, wbd, scale_row)


def _head_select_rows(hp):
    r, c = _iota((8, LANES), 0), _iota((8, LANES), 1)
    return ((r < 2) & (c == 2 * hp + r)).astype(F32)


def _foxgate_fwd(proj3, bias_row, name):
    bsz, t, _ = proj3.shape
    nt = t // LANES

    def body(f_ref, b_ref, cn_ref, ct_ref):
        bias = b_ref[...]
        i, j = _iota((LANES, LANES), 0), _iota((LANES, LANES), 1)
        lower = (j <= i).astype(F32)
        spread = (_iota((LANES, FOX_W), 0) == _iota((LANES, FOX_W), 1) // 64).astype(F32)

        def tile(k, carry):
            r0 = pl.multiple_of(k * LANES, LANES)
            xg = f_ref[pl.ds(r0, LANES), :] + bias
            lf = jnp.minimum(xg, 0.0) - jnp.log(1.0 + jnp.exp(-jnp.abs(xg)))
            c = jnp.dot(lower, lf, precision=HI, preferred_element_type=F32) + carry
            cn_ref[pl.ds(r0, LANES), :] = jnp.dot(c, spread, precision=HI, preferred_element_type=F32)
            for hp in range(4):
                ct_ref[hp, :, pl.ds(r0, LANES)] = _dot(_head_select_rows(hp), c, NT, precision=HI)
            return c[LANES - 1:LANES, :]

        lax.fori_loop(0, nt, tile, jnp.zeros((1, LANES), F32))

    return pl.pallas_call(
        body, grid=(bsz,),
        in_specs=[pl.BlockSpec((None, t, 128), lambda b: (b, 0, F_BLK)), pl.BlockSpec((1, 128), lambda b: (0, 0))],
        out_specs=[pl.BlockSpec((None, t, FOX_W), lambda b: (b, 0, 0)),
                   pl.BlockSpec((None, 4, 8, t), lambda b: (b, 0, 0, 0))],
        out_shape=[jax.ShapeDtypeStruct((bsz, t, FOX_W), F32), jax.ShapeDtypeStruct((bsz, 4, 8, t), F32)],
        compiler_params=_cparams(("parallel",)), name=name)(proj3, bias_row)


def _foxgate_bwd(proj3, dc_nat, bias_row, name):
    bsz, t, _ = proj3.shape
    nt = t // LANES

    def body(f_ref, dc_ref, b_ref, df_ref, dbias_ref, run_sc):
        bias = b_ref[...]
        i, j = _iota((LANES, LANES), 0), _iota((LANES, LANES), 1)
        upper = (j >= i).astype(F32)
        valid = _iota((1, LANES), 1) < FOX_HEADS
        run_sc[...] = jnp.zeros((8, LANES), F32)
        dbias_ref[...] = jnp.zeros((8, LANES), F32)

        def tile(k, c):
            r0 = pl.multiple_of((nt - 1 - k) * LANES, LANES)
            dc = dc_ref[pl.ds(r0, LANES), :] + jnp.where(i == LANES - 1, run_sc[0:1, :], 0.0)
            dlf = jnp.dot(upper, dc, precision=HI, preferred_element_type=F32)
            xg = f_ref[pl.ds(r0, LANES), :] + bias
            df = jnp.where(valid, dlf * _sig(-xg), 0.0)
            df_ref[pl.ds(r0, LANES), :] = df
            run_sc[...] = dlf[0:8, :]
            dbias_ref[...] += jnp.sum(df, axis=0, keepdims=True)
            return c

        lax.fori_loop(0, nt, tile, 0)

    blk = pl.BlockSpec((None, t, 128), lambda b: (b, 0, 0))
    return pl.pallas_call(
        body, grid=(bsz,),
        in_specs=[pl.BlockSpec((None, t, 128), lambda b: (b, 0, F_BLK)), blk, pl.BlockSpec((1, 128), lambda b: (0, 0))],
        out_specs=[blk, pl.BlockSpec((None, 8, 128), lambda b: (b, 0, 0))],
        out_shape=[jax.ShapeDtypeStruct((bsz, t, F_W), F32), jax.ShapeDtypeStruct((bsz, 8, 128), F32)],
        scratch_shapes=[pltpu.VMEM((8, LANES), F32)],
        compiler_params=_cparams(("parallel",)), name=name)(proj3, dc_nat, bias_row)


def _fox_tile(t):
    return min(256, t)


def _fox_fwd(proj3, c_nat, c_t, name):
    bsz, t, _ = proj3.shape
    tq = _fox_tile(t)
    tk = min(2 * tq, t)
    nq = t // tq

    def body(q_ref, kv_ref, cn_ref, ct_ref, og_ref, or_ref, lse_ref):
        i = pl.program_id(2)
        qblk = q_ref[...]
        first = _iota((1, 128), 1) < 64
        qv = qblk[:, 0:128] * 0.125
        qm = [jnp.where(first, qv, 0.0).astype(BF16), jnp.where(first, 0.0, qv).astype(BF16)]
        cqs = [cn_ref[:, 0:1], cn_ref[:, 64:65]]
        rows = _iota((tq, tk), 0) + i * tq

        def kv_step(j, carry, masked):
            c0 = pl.multiple_of(j * tk, tk)
            kb = kv_ref[pl.ds(c0, tk), 128:256].astype(BF16)
            vblk = kv_ref[pl.ds(c0, tk), 256:384]
            vx = [jnp.where(first, vblk, 1.0).astype(BF16), jnp.where(first, 1.0, vblk).astype(BF16)]
            new = []
            for h in range(2):
                m, acc = carry[2 * h], carry[2 * h + 1]
                s = _dot(qm[h], kb, NT) + (cqs[h] - ct_ref[h:h + 1, pl.ds(c0, tk)])
                if masked:
                    s = jnp.where(rows >= _iota((tq, tk), 1) + j * tk, s, MASK_VALUE)
                m_new = jnp.maximum(m, jnp.max(s, axis=1, keepdims=True))
                p = jnp.exp(s - m_new).astype(BF16)
                new += [m_new, jnp.exp(m - m_new) * acc + jnp.dot(p, vx[h], preferred_element_type=F32)]
            return tuple(new)

        init = (jnp.full((tq, 1), MASK_VALUE, F32), jnp.zeros((tq, 128), F32)) * 2
        n_full = (i * tq) // tk
        carry = lax.fori_loop(0, n_full, functools.partial(kv_step, masked=False), init)
        m0, acc0, m1, acc1 = kv_step(n_full, carry, True)
        l0, l1 = pltpu.roll(acc0, 64, 1), pltpu.roll(acc1, 64, 1)
        o = jnp.where(first, acc0 / l0, acc1 / l1)
        or_ref[...] = o
        og_ref[...] = o * _silu(qblk[:, 384:512])
        lse_ref[...] = jnp.where(first, m0 + jnp.log(l0), m1 + jnp.log(l1))

    out = jax.ShapeDtypeStruct((bsz, t, FOX_W), F32)
    blk = pl.BlockSpec((None, tq, 128), lambda b, p, i: (b, i, p))
    return pl.pallas_call(
        body, grid=(bsz, 4, nq),
        in_specs=[pl.BlockSpec((None, tq, 512), lambda b, p, i: (b, i, C_BLK0 + p)),
                  pl.BlockSpec((None, t, 512), lambda b, p, i: (b, 0, C_BLK0 + p)),
                  blk,
                  pl.BlockSpec((None, None, 8, t), lambda b, p, i: (b, p, 0, 0))],
        out_specs=[blk, blk, blk],
        out_shape=[out, out, out],
        compiler_params=_cparams(("parallel", "parallel", "arbitrary")), name=name)(proj3, proj3, c_nat, c_t)


def _fox_bwd(proj3, o_raw, dmixed, lse, c_nat, c_t, name):
    bsz, t, _ = proj3.shape
    tq = _fox_tile(t)
    nq = t // tq
    tk = min(2 * tq, t)
    ratio = tk // tq

    def body(a_ref, or_ref, do_ref, lse_ref, cn_ref, ct_ref, dc_out, dct_out, drow_out, dq_sc, do_sc, dl_sc):
        def prep(i, c):
            r0 = pl.multiple_of(i * tq, tq)
            g = a_ref[pl.ds(r0, tq), 384:512]
            dout = do_ref[pl.ds(r0, tq), :]
            o = or_ref[pl.ds(r0, tq), :]
            dc_out[pl.ds(r0, tq), 384:512] = dout * o * _dsilu(g)
            do = dout * _silu(g)
            do_sc[pl.ds(r0, tq), :] = do
            prod = do * o
            d0 = jnp.sum(prod[:, 0:64], axis=1, keepdims=True)
            d1 = jnp.sum(prod[:, 64:128], axis=1, keepdims=True)
            dl_sc[pl.ds(r0, tq), :] = jnp.concatenate([jnp.broadcast_to(d0, (tq, 64)), jnp.broadcast_to(d1, (tq, 64))], axis=1)
            dq_sc[pl.ds(r0, tq), :] = jnp.zeros((tq, 128), F32)
            drow_out[pl.ds(r0, tq), :] = jnp.zeros((tq, 128), F32)
            return c

        lax.fori_loop(0, nq, prep, 0)
        dct_out[...] = jnp.zeros((8, t), F32)

        first = _iota((1, 128), 1) < 64

        def heads(v):
            return [jnp.where(first, v, 0.0).astype(BF16), jnp.where(first, 0.0, v).astype(BF16)]

        def kv_tile(j, c):
            c0 = pl.multiple_of(j * tk, tk)
            kb = a_ref[pl.ds(c0, tk), 128:256].astype(BF16)
            vb = a_ref[pl.ds(c0, tk), 256:384].astype(BF16)
            cks = [ct_ref[h:h + 1, pl.ds(c0, tk)] for h in range(2)]

            def q_step(i, carry, diagonal):
                dk, dv, dcol0, dcol1 = carry
                r0 = pl.multiple_of(i * tq, tq)
                causal = _iota((tq, tk), 0) + i * tq >= _iota((tq, tk), 1) + j * tk
                qv = a_ref[pl.ds(r0, tq), 0:128] * 0.125
                do = do_sc[pl.ds(r0, tq), :]
                qb, dob = qv.astype(BF16), do.astype(BF16)
                qm, dom = heads(qv), heads(do)
                full, dcols, rsums = [], [], []
                for h in range(2):
                    lse_h = lse_ref[pl.ds(r0, tq), 64 * h:64 * h + 1]
                    dl_h = dl_sc[pl.ds(r0, tq), 64 * h:64 * h + 1]
                    cq = cn_ref[pl.ds(r0, tq), 64 * h:64 * h + 1]
                    p = jnp.exp(_dot(qm[h], kb, NT) + (cq - cks[h]) - lse_h)
                    if diagonal:
                        p = jnp.where(causal, p, 0.0)
                    ds = p * (_dot(dom[h], vb, NT) - dl_h)
                    dsb = ds.astype(BF16)
                    full.append((_dot(p.astype(BF16), dob, TN), _dot(dsb, qb, TN),
                                 jnp.dot(dsb, kb, preferred_element_type=F32)))
                    dcols.append(jnp.sum(ds, axis=0, keepdims=True))
                    rsums.append(jnp.broadcast_to(jnp.sum(ds, axis=1, keepdims=True), (tq, 128)))
                dq_sc[pl.ds(r0, tq), :] += jnp.where(first, full[0][2], full[1][2]) * 0.125
                drow_out[pl.ds(r0, tq), :] += jnp.where(first, rsums[0], rsums[1])
                return (dk + jnp.where(first, full[0][1], full[1][1]), dv + jnp.where(first, full[0][0], full[1][0]),
                        dcol0 - dcols[0], dcol1 - dcols[1])

            carry = (jnp.zeros((tk, 128), F32), jnp.zeros((tk, 128), F32), jnp.zeros((1, tk), F32), jnp.zeros((1, tk), F32))
            for r in range(ratio):
                carry = q_step(ratio * j + r, carry, True)
            dk, dv, dcol0, dcol1 = lax.fori_loop(ratio * (j + 1), nq, functools.partial(q_step, diagonal=False), carry)
            dct_out[0:1, pl.ds(c0, tk)] = dcol0
            dct_out[1:2, pl.ds(c0, tk)] = dcol1
            dc_out[pl.ds(c0, tk), 128:256] = dk
            dc_out[pl.ds(c0, tk), 256:384] = dv
            return c

        lax.fori_loop(0, t // tk, kv_tile, 0)
        dc_out[:, 0:128] = dq_sc[...]

    blk = pl.BlockSpec((None, t, 128), lambda b, p: (b, 0, p))
    return pl.pallas_call(
        body, grid=(bsz, 4),
        in_specs=[pl.BlockSpec((None, t, 512), lambda b, p: (b, 0, C_BLK0 + p)),
                  blk,
                  pl.BlockSpec((None, t, 128), lambda b, p: (b, 0, 4 + p)),
                  blk, blk,
                  pl.BlockSpec((None, None, 8, t), lambda b, p: (b, p, 0, 0))],
        out_specs=[pl.BlockSpec((None, t, 512), lambda b, p: (b, 0, p)),
                   pl.BlockSpec((None, None, 8, t), lambda b, p: (b, p, 0, 0)), blk],
        out_shape=[jax.ShapeDtypeStruct((bsz, t, C_W), F32), jax.ShapeDtypeStruct((bsz, 4, 8, t), F32),
                   jax.ShapeDtypeStruct((bsz, t, FOX_W), F32)],
        scratch_shapes=[pltpu.VMEM((t, 128), F32), pltpu.VMEM((t, 128), F32), pltpu.VMEM((t, 128), F32)],
        compiler_params=_cparams(("parallel", "parallel")), name=name)(proj3, o_raw, dmixed, lse, c_nat, c_t)


def _mix_tm(n):
    return min(512, n)


def _outproj_fwd(x2, oa, ob, oc, wo, g_row, name):
    n, d = x2.shape
    tm = _mix_tm(n)

    def body(x_ref, oa_ref, ob_ref, oc_ref, w_ref, g_ref, y_ref, xo_ref):
        y = (jnp.dot(oa_ref[...].astype(BF16), w_ref[0:256, :], preferred_element_type=F32)
             + jnp.dot(ob_ref[...].astype(BF16), w_ref[256:512, :], preferred_element_type=F32)
             + jnp.dot(oc_ref[...].astype(BF16), w_ref[512:1024, :], preferred_element_type=F32))
        y_ref[...] = y
        xo_ref[...] = x_ref[...] + y * _rstd(y) * g_ref[...]

    row = lambda w: pl.BlockSpec((tm, w), lambda i: (i, 0))
    out = jax.ShapeDtypeStruct((n, d), F32)
    return pl.pallas_call(
        body, grid=(n // tm,),
        in_specs=[row(d), row(256), row(256), row(512), pl.BlockSpec((d, d), lambda i: (0, 0)),
                  pl.BlockSpec((1, d), lambda i: (0, 0))],
        out_specs=[row(d), row(d)], out_shape=[out, out],
        compiler_params=_cparams(("parallel",)), name=name)(x2, oa, ob, oc, wo, g_row)


def _loss_head(x2, target2, name):
    n, d = x2.shape
    tm = _mix_tm(n)

    def body(x_ref, t_ref, dx_ref, l_ref):
        err = x_ref[...] - t_ref[...]
        dx_ref[...] = err * (1.0 / d)

        @pl.when(pl.program_id(0) == 0)
        def _():
            l_ref[...] = jnp.zeros((8, 128), F32)

        l_ref[...] += jnp.sum(err * err)

    row = pl.BlockSpec((tm, d), lambda i: (i, 0))
    return pl.pallas_call(
        body, grid=(n // tm,), in_specs=[row, row],
        out_specs=[row, pl.BlockSpec((8, 128), lambda i: (0, 0))],
        out_shape=[jax.ShapeDtypeStruct((n, d), F32), jax.ShapeDtypeStruct((8, 128), F32)],
        compiler_params=_cparams(("arbitrary",)), name=name)(x2, target2)


def _outproj_bwd(dxo, y, oa, ob, oc, wo, g_row, name):
    n, d = dxo.shape
    tm = _mix_tm(n)

    def body(dx_ref, y_ref, oa_ref, ob_ref, oc_ref, w_ref, g_ref, dm_ref, dw_ref, dg_ref):
        @pl.when(pl.program_id(0) == 0)
        def _():
            dw_ref[...] = jnp.zeros((d, d), F32)
            dg_ref[...] = jnp.zeros((8, d), F32)

        yv, dx = y_ref[...], dx_ref[...]
        r = _rstd(yv)
        yn = yv * r
        dg_ref[...] += jnp.sum(dx * yn, axis=0, keepdims=True)
        dyn = dx * g_ref[...]
        dy = (r * (dyn - yn * jnp.mean(dyn * yn, axis=-1, keepdims=True))).astype(BF16)
        dm_ref[...] = _dot(dy, w_ref[...], NT)
        dw_ref[0:256, :] += _dot(oa_ref[...].astype(BF16), dy, TN)
        dw_ref[256:512, :] += _dot(ob_ref[...].astype(BF16), dy, TN)
        dw_ref[512:1024, :] += _dot(oc_ref[...].astype(BF16), dy, TN)

    row = lambda w: pl.BlockSpec((tm, w), lambda i: (i, 0))
    fixed = lambda r, c: pl.BlockSpec((r, c), lambda i: (0, 0))
    return pl.pallas_call(
        body, grid=(n // tm,),
        in_specs=[row(d), row(d), row(256), row(256), row(512), fixed(d, d), fixed(1, d)],
        out_specs=[row(d), fixed(d, d), fixed(8, d)],
        out_shape=[jax.ShapeDtypeStruct((n, d), F32), jax.ShapeDtypeStruct((d, d), F32), jax.ShapeDtypeStruct((8, d), F32)],
        compiler_params=_cparams(("arbitrary",)), name=name)(dxo, y, oa, ob, oc, wo, g_row)


_PIECES = ((0, A_W), (A_W, B_W), (A_W + B_W, C_W), (A_W + B_W + C_W, F_W))


def _inproj_bwd_x(x2, dxo, g_row, w_int, pieces, name):
    n, d = x2.shape
    tm = min(256, n)

    def body(x_ref, dxo_ref, g_ref, w_ref, da_ref, db_ref, dc_ref, df_ref, dx_ref, dg_ref):
        @pl.when(pl.program_id(0) == 0)
        def _():
            dg_ref[...] = jnp.zeros((8, d), F32)

        dh = jnp.zeros((tm, d), F32)
        for ref, (o, w) in zip((da_ref, db_ref, dc_ref, df_ref), _PIECES):
            dh = dh + _dot(ref[...].astype(BF16), w_ref[:, o:o + w], NT)
        x = x_ref[...]
        r = _rstd(x)
        xn = x * r
        dg_ref[...] += jnp.sum(dh * xn, axis=0, keepdims=True)
        dxn = dh * g_ref[...]
        dx_ref[...] = dxo_ref[...] + r * (dxn - xn * jnp.mean(dxn * xn, axis=-1, keepdims=True))

    row = lambda w: pl.BlockSpec((tm, w), lambda i: (i, 0))
    fixed = lambda r, c: pl.BlockSpec((r, c), lambda i: (0, 0))
    return pl.pallas_call(
        body, grid=(n // tm,),
        in_specs=[row(d), row(d), fixed(1, d), fixed(d, E_INT)] + [row(w) for _, w in _PIECES],
        out_specs=[row(d), fixed(8, d)],
        out_shape=[jax.ShapeDtypeStruct((n, d), F32), jax.ShapeDtypeStruct((8, d), F32)],
        compiler_params=_cparams(("arbitrary",)), name=name)(x2, dxo, g_row, w_int, *pieces)


def _inproj_bwd_w(x2, g_row, pieces, name):
    n, d = x2.shape
    tm = min(256, n)

    def body(x_ref, g_ref, da_ref, db_ref, dc_ref, df_ref, dw_ref):
        @pl.when(pl.program_id(0) == 0)
        def _():
            dw_ref[...] = jnp.zeros((d, E_INT), F32)

        x = x_ref[...]
        h = (x * _rstd(x) * g_ref[...]).astype(BF16)
        for ref, (o, w) in zip((da_ref, db_ref, dc_ref, df_ref), _PIECES):
            dw_ref[:, o:o + w] += _dot(h, ref[...].astype(BF16), TN)

    row = lambda w: pl.BlockSpec((tm, w), lambda i: (i, 0))
    return pl.pallas_call(
        body, grid=(n // tm,),
        in_specs=[row(d), pl.BlockSpec((1, d), lambda i: (0, 0))] + [row(w) for _, w in _PIECES],
        out_specs=pl.BlockSpec((d, E_INT), lambda i: (0, 0)),
        out_shape=jax.ShapeDtypeStruct((d, E_INT), F32),
        compiler_params=_cparams(("arbitrary",), vmem_mb=56), name=name)(x2, g_row, *pieces)


def _block_diag(pool_w_l):
    z = jnp.zeros((64, 64), pool_w_l.dtype)
    return jnp.concatenate(
        [jnp.concatenate([pool_w_l[g] if c == g else z for c in range(4)], axis=1) for g in range(4)], axis=0)


def _pad_lanes(v, width=128):
    return jnp.pad(v, ((0, 0),) * (v.ndim - 1) + ((0, width - v.shape[-1]),))


def _local_step(x, target, lower_bounds, pre_norm_g, w_in_int, hgrn_norm_g, fox_f_bias, pool_w, pool_scale,
                w_out_bf, post_norm_g):
    bsz, t, d = x.shape
    n = bsz * t
    lbs = _lbs_fwd(lower_bounds)
    saved = []
    xc = x.reshape(n, d)
    for l in range(DEPTH):
        proj = _inproj_fwd(xc, pre_norm_g[l:l + 1], w_in_int[l], f"inproj_fwd{l}").reshape(bsz, t, E_INT)
        wbd = _block_diag(pool_w[l]).astype(BF16)
        bias_row = _pad_lanes(fox_f_bias[l:l + 1])
        oa, oa_raw, states = _hgrn_fwd(proj, lbs[l:l + 1], hgrn_norm_g[l:l + 1], f"hgrn_fwd{l}")
        ob = _pool_fwd(proj, wbd, pool_scale[l:l + 1], f"pool_fwd{l}")
        c_nat, c_t = _foxgate_fwd(proj, bias_row, f"foxgate_fwd{l}")
        oc, oc_raw, lse = _fox_fwd(proj, c_nat, c_t, f"fox_fwd{l}")
        y, xn = _outproj_fwd(xc, oa.reshape(n, -1), ob.reshape(n, -1), oc.reshape(n, -1), w_out_bf[l],
                             post_norm_g[l:l + 1], f"outproj_fwd{l}")
        saved.append((xc, proj, wbd, bias_row, oa, oa_raw, states, ob, oc, oc_raw, lse, c_nat, c_t, y))
        xc = xn
    dx, sq = _loss_head(xc, target.reshape(n, d), "loss_head")
    g = {k: [None] * DEPTH for k in ("pre", "w_in", "hgn", "bias", "pool_w", "pool_scale", "w_out", "post", "lbs")}
    for l in reversed(range(DEPTH)):
        xin, proj, wbd, bias_row, oa, oa_raw, states, ob, oc, oc_raw, lse, c_nat, c_t, y = saved[l]
        dmix, g["w_out"][l], dpost = _outproj_bwd(dx, y, oa.reshape(n, -1), ob.reshape(n, -1), oc.reshape(n, -1),
                                                  w_out_bf[l], post_norm_g[l:l + 1], f"outproj_bwd{l}")
        g["post"][l] = dpost[0]
        dmix3 = dmix.reshape(bsz, t, d)
        d_c, dct, drow = _fox_bwd(proj, oc_raw, dmix3, lse, c_nat, c_t, f"fox_bwd{l}")
        dc_nat = _pad_lanes(dct[:, :, 0:2, :].reshape(bsz, FOX_HEADS, t).transpose(0, 2, 1)
                            + drow.reshape(bsz, t, FOX_HEADS, 64)[..., 0])
        d_f, dbias = _foxgate_bwd(proj, dc_nat, bias_row, f"foxgate_bwd{l}")
        g["bias"][l] = jnp.sum(dbias[:, 0, :FOX_HEADS], axis=0)
        d_b, dscale, dwbd = _pool_bwd(proj, dmix3, wbd, pool_scale[l:l + 1], f"pool_bwd{l}")
        g["pool_scale"][l] = jnp.sum(dscale[:, 0], axis=0)
        dwbd = jnp.sum(dwbd, axis=0)
        g["pool_w"][l] = jnp.stack([dwbd[64 * k:64 * (k + 1), 64 * k:64 * (k + 1)] for k in range(4)])
        d_a, dgn, dlb = _hgrn_bwd(proj, oa_raw, dmix3, states, lbs[l:l + 1], hgrn_norm_g[l:l + 1], f"hgrn_bwd{l}")
        g["hgn"][l] = jnp.sum(dgn[:, 0], axis=0)
        g["lbs"][l] = jnp.sum(dlb[:, 0], axis=0)
        pieces = [p.reshape(n, -1) for p in (d_a, d_b, d_c, d_f)]
        g["w_in"][l] = _inproj_bwd_w(xin, pre_norm_g[l:l + 1], pieces, f"inproj_bwd_w{l}")
        dx, dpre = _inproj_bwd_x(xin, dx, pre_norm_g[l:l + 1], w_in_int[l], pieces, f"inproj_bwd_x{l}")
        g["pre"][l] = dpre[0]
    grads = {k: jnp.stack(v) for k, v in g.items()}
    return sq, dx.reshape(bsz, t, d), grads


def _place():
    return lax.axis_index("x"), lax.axis_index("y"), lax.axis_index("c")


def _other_chips(x, y):
    return [(1 - x, y), (x, 1 - y), (1 - x, 1 - y)]


_ANY = pl.BlockSpec(memory_space=pl.ANY)


def _gather_body(handshake):
    def body(win_ref, wout_ref, ain_ref, aout_ref, ici_send, ici_recv, d2d_send, d2d_recv, local_sems):
        x, y, c = _place()
        if handshake:
            barrier = pltpu.get_barrier_semaphore()
            for peer in [(px, py, c) for px, py in _other_chips(x, y)] + [(x, y, 1 - c)]:
                pl.semaphore_signal(barrier, inc=1, device_id=peer, device_id_type=MESH)
            pl.semaphore_wait(barrier, 4)
        me = 2 * x + y
        pairs = ((win_ref, ain_ref), (wout_ref, aout_ref))
        mine = [pltpu.make_async_copy(src, dst.at[me], local_sems.at[j]) for j, (src, dst) in enumerate(pairs)]
        for cp in mine:
            cp.start()
        chips = _other_chips(x, y)
        sends = [pltpu.make_async_remote_copy(
            src_ref=src.at[c], dst_ref=dst.at[me, c], send_sem=ici_send.at[2 * k + j], recv_sem=ici_recv.at[2 * k + j],
            device_id=(px, py, c), device_id_type=MESH) for k, (px, py) in enumerate(chips) for j, (src, dst) in enumerate(pairs)]
        for cp in sends:
            cp.start()
        passed = [pltpu.make_async_remote_copy(
            src_ref=dst.at[2 * px + py, c], dst_ref=dst.at[2 * px + py, c], send_sem=d2d_send.at[2 * k + j],
            recv_sem=d2d_recv.at[2 * k + j], device_id=(x, y, 1 - c), device_id_type=MESH)
            for k, (px, py) in enumerate(chips) for j, (src, dst) in enumerate(pairs)]
        for n, (k, j) in enumerate((k, j) for k in range(3) for j in range(2)):
            px, py = chips[k]
            src, dst = pairs[j]
            pltpu.make_async_remote_copy(
                src_ref=src.at[c], dst_ref=dst.at[2 * px + py, c], send_sem=ici_send.at[n], recv_sem=ici_recv.at[n],
                device_id=(px, py, c), device_id_type=MESH).wait_recv()
            passed[n].start()
        for n, (k, j) in enumerate((k, j) for k in range(3) for j in range(2)):
            px, py = chips[k]
            src, dst = pairs[j]
            pltpu.make_async_remote_copy(
                src_ref=dst.at[2 * px + py, 1 - c], dst_ref=dst.at[2 * px + py, 1 - c], send_sem=d2d_send.at[n],
                recv_sem=d2d_recv.at[n], device_id=(x, y, 1 - c), device_id_type=MESH).wait_recv()
        for cp in sends + passed:
            cp.wait_send()
        for cp in mine:
            cp.wait()

    return body


_GATHER_SEMS = [pltpu.SemaphoreType.DMA((6,))] * 4 + [pltpu.SemaphoreType.DMA((2,))]


def _gather_weights(w_in_sh, w_out_sh):
    return pl.pallas_call(
        _gather_body(False), in_specs=[_ANY, _ANY], out_specs=[_ANY, _ANY],
        out_shape=[jax.ShapeDtypeStruct((N_CHIPS,) + w_in_sh.shape, w_in_sh.dtype),
                   jax.ShapeDtypeStruct((N_CHIPS,) + w_out_sh.shape, w_out_sh.dtype)],
        scratch_shapes=_GATHER_SEMS, name="gather_weights")(w_in_sh, w_out_sh)


def _gather_weights_beside(w_in_sh, w_out_sh):
    hbm = pltpu.MemorySpace.HBM
    win_ref, wout_ref = jax.new_ref(w_in_sh, memory_space=hbm), jax.new_ref(w_out_sh, memory_space=hbm)
    ain_ref = jax.empty_ref(jax.ShapeDtypeStruct((N_CHIPS,) + w_in_sh.shape, w_in_sh.dtype), memory_space=hbm)
    aout_ref = jax.empty_ref(jax.ShapeDtypeStruct((N_CHIPS,) + w_out_sh.shape, w_out_sh.dtype), memory_space=hbm)
    body = _gather_body(True)

    @pl.kernel(mesh=plsc.ScalarSubcoreMesh(axis_name="sequencer", num_cores=1), name="gather_weights_beside",
               scratch_types=_GATHER_SEMS, compiler_params=pltpu.CompilerParams(collective_id=1))
    def launch(ici_send, ici_recv, d2d_send, d2d_recv, local_sems):
        body(win_ref, wout_ref, ain_ref, aout_ref, ici_send, ici_recv, d2d_send, d2d_recv, local_sems)

    launch()
    return ain_ref[...], aout_ref[...]


def _swap_with_sibling(parts, name):
    k = len(parts)

    def body(*refs):
        src, dst = refs[:k], refs[k:2 * k]
        send_sems, recv_sems = refs[2 * k:]
        x, y, c = _place()
        cps = [pltpu.make_async_remote_copy(src_ref=src[j], dst_ref=dst[j], send_sem=send_sems.at[j], recv_sem=recv_sems.at[j],
                                            device_id=(x, y, 1 - c), device_id_type=MESH) for j in range(k)]
        for cp in cps:
            cp.start()
        for cp in cps:
            cp.wait()

    return pl.pallas_call(
        body, in_specs=[_ANY] * k, out_specs=[_ANY] * k,
        out_shape=[jax.ShapeDtypeStruct(p.shape, p.dtype) for p in parts],
        scratch_shapes=[pltpu.SemaphoreType.DMA((k,)), pltpu.SemaphoreType.DMA((k,))], name=name)(*parts)


def _scatter_to_chips(parts, name):
    k = len(parts)

    def body(*refs):
        src, dst = refs[:k], refs[k:2 * k]
        send_sems, recv_sems = refs[2 * k:]
        x, y, c = _place()
        me = 2 * x + y
        cps = []
        for rel, (px, py) in enumerate(_other_chips(x, y)):
            for j in range(k):
                cps.append(pltpu.make_async_remote_copy(
                    src_ref=src[j].at[2 * px + py], dst_ref=dst[j].at[rel], send_sem=send_sems.at[rel * k + j],
                    recv_sem=recv_sems.at[rel * k + j], device_id=(px, py, c), device_id_type=MESH))
        for cp in cps:
            cp.start()
        for cp in cps:
            cp.wait()
        del me

    return pl.pallas_call(
        body, in_specs=[_ANY] * k, out_specs=[_ANY] * k,
        out_shape=[jax.ShapeDtypeStruct((3,) + p.shape[1:], p.dtype) for p in parts],
        scratch_shapes=[pltpu.SemaphoreType.DMA((3 * k,)), pltpu.SemaphoreType.DMA((3 * k,))], name=name)(*parts)


def _add_n(parts, name, with_bf16=False):
    r, c = parts[0].shape
    tr = 256 if r % 256 == 0 else r
    n = len(parts)

    def body(*refs):
        acc = refs[0][...].astype(F32)
        for ref in refs[1:n]:
            acc = acc + ref[...].astype(F32)
        refs[n][...] = acc
        if with_bf16:
            refs[n + 1][...] = acc.astype(BF16)

    blk = pl.BlockSpec((tr, c), lambda i: (i, 0))
    outs = [jax.ShapeDtypeStruct((r, c), F32)] + ([jax.ShapeDtypeStruct((r, c), BF16)] if with_bf16 else [])
    res = pl.pallas_call(
        body, grid=(r // tr,), in_specs=[blk] * n, out_specs=[blk] * len(outs),
        out_shape=outs, compiler_params=_cparams(("parallel",)), name=name)(*parts)
    return res if with_bf16 else res[0]


def _all_reduce_small(packet):
    r, w = packet.shape

    def body(p_ref, o_ref, buf, send_sems, recv_sems):
        x, y, c = _place()
        me = 4 * x + 2 * y + c
        buf[me] = p_ref[...]
        peers = []
        for k in range(1, 8):
            fx, fy, fc = (k >> 2) & 1, (k >> 1) & 1, k & 1
            peers.append((x ^ fx, y ^ fy, c ^ fc))
        cps = [pltpu.make_async_remote_copy(src_ref=p_ref, dst_ref=buf.at[me], send_sem=send_sems.at[k], recv_sem=recv_sems.at[k],
                                            device_id=peer, device_id_type=MESH) for k, peer in enumerate(peers)]
        for cp in cps:
            cp.start()
        for k, (px, py, pc) in enumerate(peers):
            pltpu.make_async_remote_copy(src_ref=p_ref, dst_ref=buf.at[4 * px + 2 * py + pc], send_sem=send_sems.at[k],
                                         recv_sem=recv_sems.at[k], device_id=(px, py, pc), device_id_type=MESH).wait_recv()
        for cp in cps:
            cp.wait_send()
        acc = buf[0]
        for k in range(1, 8):
            acc = acc + buf[k]
        o_ref[...] = acc

    vm = pl.BlockSpec(memory_space=pltpu.VMEM)
    return pl.pallas_call(
        body, in_specs=[vm], out_specs=vm, out_shape=jax.ShapeDtypeStruct((r, w), F32),
        scratch_shapes=[pltpu.VMEM((8, r, w), F32), pltpu.SemaphoreType.DMA((7,)), pltpu.SemaphoreType.DMA((7,))],
        name="all_reduce_small")(packet)


def _adamw_math(w, g, m, v):
    m = ADAM_B1 * m + (1.0 - ADAM_B1) * g
    v = ADAM_B2 * v + (1.0 - ADAM_B2) * (g * g)
    m_hat = m / (1.0 - ADAM_B1 ** ADAM_STEP)
    v_hat = v / (1.0 - ADAM_B2 ** ADAM_STEP)
    return -ADAM_LR * (m_hat / (jnp.sqrt(v_hat) + ADAM_EPS) + ADAM_WD * w), m, v


def _adamw(w, g, m, v, name):
    nl, r, c = w.shape
    tr = 256 if r % 256 == 0 else r

    def body(w_ref, g_ref, m_ref, v_ref, d_ref, mo_ref, vo_ref):
        d_ref[...], mo_ref[...], vo_ref[...] = _adamw_math(w_ref[...], g_ref[...], m_ref[...], v_ref[...])

    blk = pl.BlockSpec((None, tr, c), lambda l, i: (l, i, 0))
    out = jax.ShapeDtypeStruct(w.shape, F32)
    return pl.pallas_call(
        body, grid=(nl, r // tr), in_specs=[blk] * 4, out_specs=[blk] * 3, out_shape=[out] * 3,
        compiler_params=_cparams(("parallel", "parallel")), name=name)(w, g, m, v)


def _small_update(gsum, lower_bounds, wpack, mpack, vpack):
    r, w = gsum.shape
    lb_rows = DEPTH * HGRN_W // 128

    def body(g_ref, a_ref, w_ref, m_ref, v_ref, go_ref, d_ref, mo_ref, vo_ref):
        a = a_ref[...]
        a0, a1 = a[0:1], a[1:2]
        mx = jnp.maximum(a0, a1)
        e0, e1 = jnp.exp(a0 - mx), jnp.exp(a1 - mx)
        p0, p1 = e0 / (e0 + e1), e1 / (e0 + e1)
        g = g_ref[...]
        half = lb_rows // 2
        dl0 = jnp.concatenate([g[k:k + 1] for k in range(half)], axis=1)
        dl1 = jnp.concatenate([g[half + k:half + k + 1] for k in range(half)], axis=1)
        dp0 = (dl0 + dl1) - (dl0 + dl1)
        dp1 = dl1
        inner = p0 * dp0 + p1 * dp1
        da0, da1 = p0 * (dp0 - inner), p1 * (dp1 - inner)
        rows = [da0[:, 128 * k:128 * (k + 1)] for k in range(half)] + [da1[:, 128 * k:128 * (k + 1)] for k in range(half)]
        gfull = jnp.concatenate(rows + [g[lb_rows:]], axis=0)
        go_ref[...] = gfull
        d_ref[...], mo_ref[...], vo_ref[...] = _adamw_math(w_ref[...], gfull, m_ref[...], v_ref[...])

    vm = pl.BlockSpec(memory_space=pltpu.VMEM)
    out = jax.ShapeDtypeStruct((r, w), F32)
    return pl.pallas_call(body, in_specs=[vm] * 5, out_specs=[vm] * 4, out_shape=[out] * 4, name="small_update")(
        gsum, lower_bounds, wpack, mpack, vpack)


_SMALL = ("lower_bounds", "pre_norm_g", "hgrn_norm_g", "fox_f_bias", "pool_w", "pool_scale", "post_norm_g")


def _pack(parts):
    rows = []
    for k in _SMALL:
        f = parts[k].reshape(-1)
        pad = (-f.shape[0]) % (8 * 128)
        rows.append(jnp.pad(f, (0, pad)).reshape(-1, 128))
    rows.append(jnp.zeros((8, 128), F32))
    return jnp.concatenate(rows, axis=0)


def _unpack(pack, like):
    out, r = {}, 0
    for k in _SMALL:
        size = int(np.prod(like[k].shape))
        nr = -(-size // (8 * 128)) * 8
        out[k] = pack[r:r + nr].reshape(-1)[:size].reshape(like[k].shape)
        r += nr
    return out, r


def kernel(x, lower_bounds, pre_norm_g, w_in, hgrn_norm_g, fox_f_bias, pool_w, pool_scale, w_out, post_norm_g, loss_target, m_lower_bounds, m_pre_norm_g, m_w_in, m_hgrn_norm_g, m_fox_f_bias, m_pool_w, m_pool_scale, m_w_out, m_post_norm_g, v_lower_bounds, v_pre_norm_g, v_w_in, v_hgrn_norm_g, v_fox_f_bias, v_pool_w, v_pool_scale, v_w_out, v_post_norm_g):
    cx, cy, cc = _place()
    chip = 2 * cx + cy

    halves = lambda w, l: w[l].reshape(2, w.shape[1] // 2, w.shape[2]).astype(BF16)
    gathered = [_gather_weights(halves(w_in, 0), halves(w_out, 0)), _gather_weights_beside(halves(w_in, 1), halves(w_out, 1))]
    w_in_int = [_internal_from_shards([ain[q].reshape(D_MODEL, SHARD_W) for q in range(N_CHIPS)]) for ain, _ in gathered]
    w_out_full = [aout.reshape(D_MODEL, D_MODEL) for _, aout in gathered]

    sq, grad_x, g = _local_step(x, loss_target, lower_bounds, pre_norm_g, w_in_int, hgrn_norm_g, fox_f_bias, pool_w,
                                pool_scale, w_out_full, post_norm_g)

    layer = lambda a, l: lax.dynamic_index_in_dim(a, l, axis=0, keepdims=False)
    blocks_in = lambda l: _shards_from_internal(layer(g["w_in"], l))
    blocks_out = lambda l: layer(g["w_out"], l).reshape(N_CHIPS, 256, D_MODEL)
    mine_in, mine_out = blocks_in(cc), blocks_out(cc)
    sib_in, sib_out = _swap_with_sibling([blocks_in(1 - cc), blocks_out(1 - cc)], "grad_swap1")
    rin, rout = 4 * 1024, 4 * 256
    sum_in, send_in = _add_n([mine_in.reshape(rin, SHARD_W), sib_in.reshape(rin, SHARD_W)], "grad_add1_in", True)
    sum_out, send_out = _add_n([mine_out.reshape(rout, D_MODEL), sib_out.reshape(rout, D_MODEL)], "grad_add1_out", True)
    sum_in, sum_out = sum_in.reshape(4, 1024, SHARD_W), sum_out.reshape(4, 256, D_MODEL)
    got_in, got_out = _scatter_to_chips([send_in.reshape(4, 1024, SHARD_W), send_out.reshape(4, 256, D_MODEL)], "grad_scatter")
    own = lambda a: lax.dynamic_index_in_dim(a, chip, axis=0, keepdims=False)
    half_in = _add_n([own(sum_in)] + [got_in[k] for k in range(3)], "grad_add2_in")
    half_out = _add_n([own(sum_out)] + [got_out[k] for k in range(3)], "grad_add2_out")
    oth_in, oth_out = _swap_with_sibling([half_in, half_out], "grad_swap2")
    first = cc == 0
    grad_w_in = jnp.stack([jnp.where(first, half_in, oth_in), jnp.where(first, oth_in, half_in)])
    grad_w_out = jnp.stack([jnp.where(first, half_out, oth_out), jnp.where(first, oth_out, half_out)])

    small = {"lower_bounds": g["lbs"], "pre_norm_g": g["pre"], "hgrn_norm_g": g["hgn"], "fox_f_bias": g["bias"],
             "pool_w": g["pool_w"], "pool_scale": g["pool_scale"], "post_norm_g": g["post"]}
    packet = _pack(small)
    nrows = packet.shape[0]
    packet = packet.at[nrows - 1].set(sq[0])
    gsum = _all_reduce_small(packet)
    loss = gsum[nrows - 1, 0] * (0.5 / D_MODEL)

    weights = {"lower_bounds": lower_bounds, "pre_norm_g": pre_norm_g, "hgrn_norm_g": hgrn_norm_g,
               "fox_f_bias": fox_f_bias, "pool_w": pool_w, "pool_scale": pool_scale, "post_norm_g": post_norm_g}
    moments_m = {"lower_bounds": m_lower_bounds, "pre_norm_g": m_pre_norm_g, "hgrn_norm_g": m_hgrn_norm_g,
                 "fox_f_bias": m_fox_f_bias, "pool_w": m_pool_w, "pool_scale": m_pool_scale, "post_norm_g": m_post_norm_g}
    moments_v = {"lower_bounds": v_lower_bounds, "pre_norm_g": v_pre_norm_g, "hgrn_norm_g": v_hgrn_norm_g,
                 "fox_f_bias": v_fox_f_bias, "pool_w": v_pool_w, "pool_scale": v_pool_scale, "post_norm_g": v_post_norm_g}
    gp, dp, mp, vp = _small_update(gsum, lower_bounds, _pack(weights), _pack(moments_m), _pack(moments_v))
    gs, _ = _unpack(gp, weights)
    ds, _ = _unpack(dp, weights)
    ms, _ = _unpack(mp, weights)
    vs, _ = _unpack(vp, weights)

    d_in, m_in, v_in = _adamw(w_in, grad_w_in, m_w_in, v_w_in, "adamw_w_in")
    d_out, m_out, v_out = _adamw(w_out, grad_w_out, m_w_out, v_w_out, "adamw_w_out")

    def ordered(s, big_in, big_out):
        return (s["lower_bounds"], s["pre_norm_g"], big_in, s["hgrn_norm_g"], s["fox_f_bias"], s["pool_w"],
                s["pool_scale"], big_out, s["post_norm_g"])

    return (loss, grad_x, *ordered(gs, grad_w_in, grad_w_out), *ordered(ds, d_in, d_out),
            *ordered(ms, m_in, m_out), *ordered(vs, v_in, v_out))
```

```python
import functools

import numpy as np
import jax
import jax.numpy as jnp
from jax import lax
from jax.experimental import pallas as pl
from jax.experimental.pallas import tpu as pltpu
from jax.experimental.pallas import tpu_sc as plsc

F32 = jnp.float32
BF16 = jnp.bfloat16
HI = lax.Precision.HIGHEST
MESH = pl.DeviceIdType.MESH

NORM_EPS = 1e-6
MASK_VALUE = -1e30
TINY = 1e-30
ADAM_LR, ADAM_B1, ADAM_B2, ADAM_EPS, ADAM_WD, ADAM_STEP = 0.001, 0.9, 0.999, 1e-08, 0.01, 10

D_MODEL = 1024
DEPTH = 2
N_CHIPS = 4
CHUNK = 64
LANES = 128
HGRN_W, POOL_W, FOX_W, FOX_HEADS = 256, 256, 512, 8
POOL_WINDOWS = (2, 4, 8, 16)
POOL_HALO = 16
IN_WIDTH = 3592
SHARD_W = IN_WIDTH // N_CHIPS
A_W, B_W, C_W, F_W = 1024, 512, 2048, 128
E_INT = A_W + B_W + C_W + F_W
B_BLK = A_W // 512
C_BLK0 = (A_W + B_W) // 512
F_BLK = (A_W + B_W + C_W) // 128


def _segments():
    segs = []
    for hp in range(2):
        for part in range(4):
            segs.append((part * 256 + hp * 128, 128))
    segs.append((1024, 256))
    segs.append((1280, 256))
    for hp in range(4):
        for part in range(4):
            segs.append((1536 + part * 512 + hp * 128, 128))
    segs.append((3584, 8))
    return segs


_SEGS = _segments()


def _to_internal(w):
    parts = [w[..., s:s + n] for s, n in _SEGS]
    parts.append(jnp.zeros(w.shape[:-1] + (E_INT - IN_WIDTH,), w.dtype))
    return jnp.concatenate(parts, axis=-1)


def _to_original(w):
    offs, o = [], 0
    for s, n in _SEGS:
        offs.append((s, o, n))
        o += n
    parts = [w[..., o:o + n] for s, o, n in sorted(offs)]
    return jnp.concatenate(parts, axis=-1)


def _internal_from_shards(shards):
    parts = []
    for s, n in _SEGS:
        while n > 0:
            q, r = divmod(s, SHARD_W)
            take = min(n, SHARD_W - r)
            parts.append(shards[q][..., r:r + take])
            s, n = s + take, n - take
    parts.append(jnp.zeros(shards[0].shape[:-1] + (E_INT - IN_WIDTH,), shards[0].dtype))
    return jnp.concatenate(parts, axis=-1)


def _shards_from_internal(w):
    offs, o = [], 0
    for s, n in _SEGS:
        offs.append((s, o, n))
        o += n
    blocks = []
    for q in range(N_CHIPS):
        lo, hi = SHARD_W * q, SHARD_W * (q + 1)
        parts = [w[..., o + max(lo, s) - s:o + min(hi, s + n) - s] for s, o, n in sorted(offs) if s < hi and s + n > lo]
        blocks.append(jnp.concatenate(parts, axis=-1))
    return jnp.stack(blocks)


def _cparams(sem=None, vmem_mb=48):
    kw = dict(vmem_limit_bytes=vmem_mb * 1024 * 1024)
    if sem is not None:
        kw["dimension_semantics"] = sem
    return pltpu.CompilerParams(**kw)


def _sig(x):
    return 1.0 / (1.0 + jnp.exp(-x))


def _silu(x):
    return x * _sig(x)


def _dsilu(x):
    s = _sig(x)
    return s * (1.0 + x * (1.0 - s))


def _rstd(x):
    return lax.rsqrt(jnp.mean(x * x, axis=-1, keepdims=True) + NORM_EPS)


def _dot(a, b, dims, **kw):
    return lax.dot_general(a, b, (dims, ((), ())), preferred_element_type=F32, **kw)


NN = ((1,), (0,))
NT = ((1,), (1,))
TN = ((0,), (0,))


def _iota(shape, dim):
    return lax.broadcasted_iota(jnp.int32, shape, dim)


def _lbs_fwd(lower_bounds):
    def body(a_ref, o_ref):
        a = a_ref[...]
        a0, a1 = a[0:1], a[1:2]
        m = jnp.maximum(a0, a1)
        e0, e1 = jnp.exp(a0 - m), jnp.exp(a1 - m)
        p0, p1 = e0 / (e0 + e1), e1 / (e0 + e1)
        o_ref[...] = jnp.concatenate([p0 - p0, (p0 + p1) - p0], axis=0)

    return pl.pallas_call(body, out_shape=jax.ShapeDtypeStruct(lower_bounds.shape, F32), name="lbs_fwd")(lower_bounds)


def _inproj_fwd(x2, g_row, w_int, name):
    n, d = x2.shape
    e = w_int.shape[1]
    tm = min(256, n)

    def body(x_ref, g_ref, w_ref, o_ref):
        x = x_ref[...]
        h = (x * _rstd(x) * g_ref[...]).astype(BF16)
        o_ref[...] = jnp.dot(h, w_ref[...], preferred_element_type=F32)

    return pl.pallas_call(
        body, grid=(n // tm,),
        in_specs=[pl.BlockSpec((tm, d), lambda i: (i, 0)), pl.BlockSpec((1, d), lambda i: (0, 0)),
                  pl.BlockSpec((d, e), lambda i: (0, 0))],
        out_specs=pl.BlockSpec((tm, e), lambda i: (i, 0)),
        out_shape=jax.ShapeDtypeStruct((n, e), F32),
        compiler_params=_cparams(("parallel",)), name=name)(x2, g_row, w_int)


def _chunk_cumsum_matrix():
    i, j = _iota((LANES, LANES), 0), _iota((LANES, LANES), 1)
    return ((i <= j) & ((i // CHUNK) == (j // CHUNK))).astype(F32)


def _hgrn_gates(a, lb):
    qa, z = a[:, 0:128], a[:, 128:256]
    sg, sgn = _sig(z), _sig(-z)
    fg = lb + (1.0 - lb) * sg
    lf = jnp.log(jnp.maximum(fg, TINY))
    kk = (1.0 - lb) * sgn
    return qa * _sig(qa), kk, lf, sg, sgn, fg


def _hgrn_fwd(proj3, lbs_row, gn_col, name):
    bsz, t, _ = proj3.shape
    nt = t // LANES

    def body(a_ref, lb_ref, gn_ref, og_ref, or_ref):
        lb = lb_ref[...]
        gn = gn_ref[...]
        umat = _chunk_cumsum_matrix()
        lane64 = _iota((1, LANES), 1) % CHUNK

        def tile(i, carry):
            r0 = pl.multiple_of(i * LANES, LANES)
            a = a_ref[pl.ds(r0, LANES), :]
            qq, kk, lf, _, _, _ = _hgrn_gates(a, lb)
            va, ga = a[:, 256:384], a[:, 384:512]
            q_t, k_t, v_t = qq.T, kk.T, va.T
            b_t = jnp.dot(lf.T, umat, precision=HI, preferred_element_type=F32)
            new_s, o_heads = [], []
            for h in range(2):
                s_h = carry[h]
                rs = slice(CHUNK * h, CHUNK * (h + 1))
                qh, kh, vh, bh = q_t[rs], k_t[rs], v_t[rs], b_t[rs]
                inter = []
                for c in range(2):
                    cs = slice(CHUNK * c, CHUNK * (c + 1))
                    b_ = bh[:, cs]
                    qt = (qh[:, cs] * jnp.exp(b_)).astype(BF16)
                    inter.append(_dot(s_h.astype(BF16), qt, TN))
                    bl = b_[:, CHUNK - 1:CHUNK]
                    kt = (kh[:, cs] * jnp.exp(bl - b_)).astype(BF16)
                    s_h = jnp.exp(bl) * s_h + _dot(kt, vh[:, cs].astype(BF16), NT)
                new_s.append(s_h)

                acc = jnp.concatenate(inter, axis=1) + jnp.sum(qh * kh, axis=0, keepdims=True) * vh
                for dlt in range(1, CHUNK):
                    kr, br, vr = pltpu.roll(kh, dlt, 1), pltpu.roll(bh, dlt, 1), pltpu.roll(vh, dlt, 1)
                    e = jnp.exp(jnp.minimum(bh - br, 0.0))
                    att = jnp.sum(qh * kr * e, axis=0, keepdims=True)
                    acc = acc + jnp.where(lane64 >= dlt, att, 0.0) * vr
                o_heads.append(acc)
            normed = []
            for h in range(2):
                o_h = o_heads[h]
                ms = jnp.mean(o_h * o_h, axis=0, keepdims=True)
                normed.append(o_h * lax.rsqrt(ms + NORM_EPS) * gn[CHUNK * h:CHUNK * (h + 1)])
            or_ref[pl.ds(r0, LANES), :] = jnp.concatenate(o_heads, axis=0).T
            og_ref[pl.ds(r0, LANES), :] = jnp.concatenate(normed, axis=0).T * _silu(ga)
            return tuple(new_s)

        zero = jnp.zeros((CHUNK, CHUNK), F32)
        lax.fori_loop(0, nt, tile, (zero, zero))

    out = jax.ShapeDtypeStruct((bsz, t, HGRN_W), F32)
    return pl.pallas_call(
        body, grid=(bsz, 2),
        in_specs=[pl.BlockSpec((None, t, 512), lambda b, p: (b, 0, p)),
                  pl.BlockSpec((1, 128), lambda b, p: (0, p)),
                  pl.BlockSpec((128, 1), lambda b, p: (p, 0))],
        out_specs=[pl.BlockSpec((None, t, 128), lambda b, p: (b, 0, p)),
                   pl.BlockSpec((None, t, 128), lambda b, p: (b, 0, p))],
        out_shape=[out, out],
        compiler_params=_cparams(("parallel", "parallel")), name=name)(proj3, lbs_row, gn_col)


def _hgrn_bwd(proj3, o_raw, dmixed, lbs_row, gn_row, name):
    bsz, t, _ = proj3.shape
    nt = t // LANES
    nchunk = t // CHUNK

    def body(a_ref, or_ref, do_ref, lb_ref, gn_ref, da_ref, dgn_ref, dlb_ref, s_sc):
        lb = lb_ref[...]
        gn = gn_ref[...]
        umat = _chunk_cumsum_matrix()
        lane = _iota((1, LANES), 1)
        lane64 = lane % CHUNK
        half = lane < CHUNK

        def t_layout(a):
            qq, kk, lf, sg, sgn, fg = _hgrn_gates(a, lb)
            b_t = jnp.dot(lf.T, umat, precision=HI, preferred_element_type=F32)
            return qq.T, kk.T, a[:, 256:384].T, b_t, (sg, sgn, fg)

        def fwd_tile(i, carry):
            r0 = pl.multiple_of(i * LANES, LANES)
            q_t, k_t, v_t, b_t, _ = t_layout(a_ref[pl.ds(r0, LANES), :])
            new_s = []
            for h in range(2):
                s_h = carry[h]
                rs = slice(CHUNK * h, CHUNK * (h + 1))
                for c in range(2):
                    cs = slice(CHUNK * c, CHUNK * (c + 1))
                    s_sc[h, 2 * i + c] = s_h
                    b_ = b_t[rs, cs]
                    bl = b_[:, CHUNK - 1:CHUNK]
                    kt = (k_t[rs, cs] * jnp.exp(bl - b_)).astype(BF16)
                    s_h = jnp.exp(bl) * s_h + _dot(kt, v_t[rs, cs].astype(BF16), NT)
                new_s.append(s_h)
            return tuple(new_s)

        zero = jnp.zeros((CHUNK, CHUNK), F32)
        lax.fori_loop(0, nt, fwd_tile, (zero, zero))

        def half_mean(v):
            m0 = jnp.sum(jnp.where(half, v, 0.0), axis=1, keepdims=True) * (1.0 / CHUNK)
            m1 = jnp.sum(jnp.where(half, 0.0, v), axis=1, keepdims=True) * (1.0 / CHUNK)
            return jnp.where(half, m0, m1)

        def bwd_tile(k, carry):
            ds0, ds1, dgn_acc, dlb_acc = carry
            i = nt - 1 - k
            r0 = pl.multiple_of(i * LANES, LANES)
            a = a_ref[pl.ds(r0, LANES), :]
            qa, z, ga = a[:, 0:128], a[:, 128:256], a[:, 384:512]
            q_t, k_t, v_t, b_t, (sg, sgn, fg) = t_layout(a)
            oraw = or_ref[pl.ds(r0, LANES), :]
            dout = do_ref[pl.ds(r0, LANES), :]
            r = lax.rsqrt(half_mean(oraw * oraw) + NORM_EPS)
            xn = oraw * r
            dga = dout * (xn * gn) * _dsilu(ga)
            don = dout * _silu(ga)
            dgn_acc = dgn_acc + jnp.sum(don * xn, axis=0, keepdims=True)
            dxn = don * gn
            do_t = (r * (dxn - xn * half_mean(dxn * xn))).T
            new_ds, dq_h, dk_h, dv_h, db_h = [], [], [], [], []
            for h in range(2):
                ds_h = (ds0, ds1)[h]
                rs = slice(CHUNK * h, CHUNK * (h + 1))
                qh, kh, vh, bh, doh = q_t[rs], k_t[rs], v_t[rs], b_t[rs], do_t[rs]
                dq_c, dk_c, dv_c, dbl_c = [None, None], [None, None], [None, None], [None, None]
                for c in (1, 0):
                    cs = slice(CHUNK * c, CHUNK * (c + 1))
                    s_n = s_sc[h, 2 * i + c]
                    b_ = bh[:, cs]
                    eb = jnp.exp(b_)
                    bl = b_[:, CHUNK - 1:CHUNK]
                    ek = jnp.exp(bl - b_)
                    ebl = jnp.exp(bl)
                    qt, kt = qh[:, cs] * eb, kh[:, cs] * ek
                    do_c = doh[:, cs].astype(BF16)
                    dsb = ds_h.astype(BF16)
                    dv_c[c] = _dot(dsb, kt.astype(BF16), TN)
                    dkt = _dot(dsb, vh[:, cs].astype(BF16), NN)
                    dqt = _dot(s_n.astype(BF16), do_c, NN)
                    dbl_c[c] = jnp.sum(ds_h * s_n, axis=1, keepdims=True) * ebl + jnp.sum(dkt * kt, axis=1, keepdims=True)
                    dq_c[c], dk_c[c] = dqt * eb, dkt * ek
                    ds_h = ebl * ds_h + _dot(qt.astype(BF16), do_c, NT)
                new_ds.append(ds_h)

                att0 = jnp.sum(qh * kh, axis=0, keepdims=True)
                datt0 = jnp.sum(doh * vh, axis=0, keepdims=True)
                dqh = jnp.concatenate(dq_c, axis=1) + datt0 * kh
                dkh = jnp.concatenate(dk_c, axis=1) + datt0 * qh
                dvh = jnp.concatenate(dv_c, axis=1) + att0 * doh
                for dlt in range(1, CHUNK):
                    kr, br, vr = pltpu.roll(kh, dlt, 1), pltpu.roll(bh, dlt, 1), pltpu.roll(vh, dlt, 1)
                    e = jnp.where(lane64 >= dlt, jnp.exp(jnp.minimum(bh - br, 0.0)), 0.0)
                    qe = qh * e
                    att = jnp.sum(qe * kr, axis=0, keepdims=True)
                    datt = jnp.sum(doh * vr, axis=0, keepdims=True)
                    dqh = dqh + datt * (kr * e)
                    dkh = dkh + pltpu.roll(datt * qe, LANES - dlt, 1)
                    dvh = dvh + pltpu.roll(att * doh, LANES - dlt, 1)
                dbl = jnp.where(half, dbl_c[0], dbl_c[1])
                db_h.append(qh * dqh - kh * dkh + jnp.where(lane64 == CHUNK - 1, dbl, 0.0))
                dq_h.append(dqh)
                dk_h.append(dkh)
                dv_h.append(dvh)
            dqq = jnp.concatenate(dq_h, axis=0).T
            dkk = jnp.concatenate(dk_h, axis=0).T
            dvv = jnp.concatenate(dv_h, axis=0).T
            dlf = _dot(jnp.concatenate(db_h, axis=0), umat, NT, precision=HI).T
            dqa = dqq * _dsilu(qa)
            dfg = jnp.where(fg > TINY, dlf / fg, 0.0)
            dz = (dfg - dkk) * (1.0 - lb) * sg * sgn
            dlb_acc = dlb_acc + jnp.sum(dfg * (1.0 - sg) - dkk * sgn, axis=0, keepdims=True)
            da_ref[pl.ds(r0, LANES), :] = jnp.concatenate([dqa, dz, dvv, dga], axis=1)
            return new_ds[0], new_ds[1], dgn_acc, dlb_acc

        zrow = jnp.zeros((1, LANES), F32)
        _, _, dgn_acc, dlb_acc = lax.fori_loop(0, nt, bwd_tile, (zero, zero, zrow, zrow))
        dgn_ref[...] = jnp.broadcast_to(dgn_acc, (8, LANES))
        dlb_ref[...] = jnp.broadcast_to(dlb_acc, (8, LANES))

    rows = jax.ShapeDtypeStruct((bsz, 8, HGRN_W), F32)
    return pl.pallas_call(
        body, grid=(bsz, 2),
        in_specs=[pl.BlockSpec((None, t, 512), lambda b, p: (b, 0, p)),
                  pl.BlockSpec((None, t, 128), lambda b, p: (b, 0, p)),
                  pl.BlockSpec((None, t, 128), lambda b, p: (b, 0, p)),
                  pl.BlockSpec((1, 128), lambda b, p: (0, p)),
                  pl.BlockSpec((1, 128), lambda b, p: (0, p))],
        out_specs=[pl.BlockSpec((None, t, 512), lambda b, p: (b, 0, p)),
                   pl.BlockSpec((None, 8, 128), lambda b, p: (b, 0, p)),
                   pl.BlockSpec((None, 8, 128), lambda b, p: (b, 0, p))],
        out_shape=[jax.ShapeDtypeStruct((bsz, t, A_W), F32), rows, rows],
        scratch_shapes=[pltpu.VMEM((2, nchunk, CHUNK, CHUNK), F32)],
        compiler_params=_cparams(("parallel", "parallel")), name=name)(proj3, o_raw, dmixed, lbs_row, gn_row)


N_LEVELS = 6


def _hgrn_tables():
    t = np.arange(LANES)
    j = np.arange(LANES)[None, :]
    same_chunk = (t[:, None] // CHUNK) == (j // CHUNK)
    w = np.zeros((2 + N_LEVELS, LANES, LANES), np.float32)
    w[0] = same_chunk & (j <= t[:, None])
    w[1] = same_chunk & (j > t[:, None])
    maskf = np.zeros((N_LEVELS, LANES, LANES), np.float32)
    rightf = np.zeros((N_LEVELS, LANES, LANES), np.float32)
    for li in range(N_LEVELS):
        m = (CHUNK // 2) >> li
        start = t - (t % (2 * m))
        right = (t % (2 * m)) >= m
        first = np.where(right, start + m, t + 1)
        last = np.where(right, t, start + m - 1)
        w[2 + li] = (j >= first[:, None]) & (j <= last[:, None])
        maskf[li] = (t[:, None] // (2 * m)) == (j // (2 * m))
        rightf[li] = right[:, None]
    return jnp.asarray(w.reshape(-1, LANES), BF16), jnp.asarray(maskf), jnp.asarray(rightf)


def _split(x, n):
    parts = []
    for _ in range(n - 1):
        p = x.astype(BF16)
        parts.append(p)
        x = x - p.astype(F32)
    parts.append(x.astype(BF16))
    return parts


def _exact_dot(w, parts):
    acc = jnp.dot(w, parts[0], preferred_element_type=F32)
    for p in parts[1:]:
        acc = acc + jnp.dot(w, p, preferred_element_type=F32)
    return acc


def _head_sums(v, ones_blk, n=2):
    parts = _split(v, n)
    acc = jnp.dot(parts[0], ones_blk, preferred_element_type=F32)
    for p in parts[1:]:
        acc = acc + jnp.dot(p, ones_blk, preferred_element_type=F32)
    return acc


def _hgrn_consts():
    r, c = _iota((LANES, LANES), 0), _iota((LANES, LANES), 1)
    eye = r == c
    ones_blk = ((r // CHUNK) == (c // CHUNK)).astype(BF16)
    return eye, ones_blk, jnp.ones((CHUNK, LANES), BF16)


def _hgrn_levels(qq, kk, zall, mk_ref, rt_ref, d_att=None):
    att = [jnp.zeros((LANES, LANES), F32)] * 2
    dq = dk = db = jnp.zeros((LANES, LANES), F32)
    for li in range(N_LEVELS):
        e = jnp.exp(zall[(2 + li) * LANES:(3 + li) * LANES])
        rt = rt_ref[li]
        mk = mk_ref[li]
        qef, kef = e * rt, e * (1.0 - rt)
        qe, ke = (qq * qef).astype(BF16), (kk * kef).astype(BF16)
        dqs, dks = [], []
        for h in range(2):
            hs = slice(CHUNK * h, CHUNK * (h + 1))
            att[h] = att[h] + _dot(qe[:, hs], ke[:, hs], NT) * mk
            if d_att is not None:
                dam = (d_att[h] * mk).astype(BF16)
                dqs.append(jnp.dot(dam, ke[:, hs], preferred_element_type=F32))
                dks.append(_dot(dam, qe[:, hs], TN))
        if d_att is not None:
            dqe, dke = jnp.concatenate(dqs, axis=1), jnp.concatenate(dks, axis=1)
            dq = dq + dqe * qef
            dk = dk + dke * kef
            db = db + (dqe * qe.astype(F32) - dke * ke.astype(F32))
    return att, dq, dk, db


def _hgrn_fwd(proj3, lbs_row, gn_row, name):
    bsz, t, _ = proj3.shape
    nt = t // LANES
    w_all, maskf, rightf = _hgrn_tables()

    def body(a_ref, lb_ref, gn_ref, w_ref, mk_ref, rt_ref, og_ref, or_ref, st_ref):
        lb = lb_ref[...]
        gn = gn_ref[...]
        eye, ones_blk, ones_h = _hgrn_consts()

        def tile(i, carry):
            r0 = pl.multiple_of(i * LANES, LANES)
            a = a_ref[pl.ds(r0, LANES), :]
            qq, kk, lf, _, _, _ = _hgrn_gates(a, lb)
            va, ga = a[:, 256:384], a[:, 384:512]
            parts = _split(lf, 3)
            zall = _exact_dot(w_ref[...], parts)
            eb, ee = jnp.exp(zall[0:LANES]), jnp.exp(zall[LANES:2 * LANES])
            vb = va.astype(BF16)
            att, _, _, _ = _hgrn_levels(qq, kk, zall, mk_ref, rt_ref)
            qk = _split(qq * kk, 2)
            qeb, keb = (qq * eb).astype(BF16), (kk * ee).astype(BF16)
            new_s, o_heads = [], []
            for h in range(2):
                hs = slice(CHUNK * h, CHUNK * (h + 1))
                diag = _exact_dot_r(qk, hs, ones_h)
                a_h = att[h] + jnp.where(eye, diag, 0.0)
                o_h = jnp.dot(a_h.astype(BF16), vb[:, hs], preferred_element_type=F32)
                st = carry[h]
                chunks = []
                for c in range(2):
                    rc = slice(CHUNK * c, CHUNK * (c + 1))
                    st_ref[h, 2 * i + c] = st
                    chunks.append(o_h[rc] + _dot(qeb[rc, hs], st.astype(BF16), NT))
                    ebl = eb[CHUNK * (c + 1) - 1:CHUNK * (c + 1), hs]
                    st = st * ebl + _dot(vb[rc, hs], keb[rc, hs], TN)
                new_s.append(st)
                o_heads.append(jnp.concatenate(chunks, axis=0))
            o = jnp.concatenate(o_heads, axis=1)
            ms = _head_sums(o * o, ones_blk) * (1.0 / CHUNK)
            or_ref[pl.ds(r0, LANES), :] = o
            og_ref[pl.ds(r0, LANES), :] = o * lax.rsqrt(ms + NORM_EPS) * gn * _silu(ga)
            return tuple(new_s)

        zero = jnp.zeros((CHUNK, CHUNK), F32)
        lax.fori_loop(0, nt, tile, (zero, zero))

    out = jax.ShapeDtypeStruct((bsz, t, HGRN_W), F32)
    row = pl.BlockSpec((1, 128), lambda b, p: (0, p))
    return pl.pallas_call(
        body, grid=(bsz, 2),
        in_specs=[pl.BlockSpec((None, t, 512), lambda b, p: (b, 0, p)), row, row,
                  pl.BlockSpec(w_all.shape, lambda b, p: (0, 0)),
                  pl.BlockSpec(maskf.shape, lambda b, p: (0, 0, 0)),
                  pl.BlockSpec(rightf.shape, lambda b, p: (0, 0, 0))],
        out_specs=[pl.BlockSpec((None, t, 128), lambda b, p: (b, 0, p)),
                   pl.BlockSpec((None, t, 128), lambda b, p: (b, 0, p)),
                   pl.BlockSpec((None, 2, t // CHUNK, CHUNK, CHUNK), lambda b, p: (b, p, 0, 0, 0))],
        out_shape=[out, out, jax.ShapeDtypeStruct((bsz, 4, t // CHUNK, CHUNK, CHUNK), F32)],
        compiler_params=_cparams(("parallel", "parallel")), name=name)(proj3, lbs_row, gn_row, w_all, maskf, rightf)


def _exact_dot_r(parts, hs, ones_h):
    acc = jnp.dot(parts[0][:, hs], ones_h, preferred_element_type=F32)
    for p in parts[1:]:
        acc = acc + jnp.dot(p[:, hs], ones_h, preferred_element_type=F32)
    return acc


def _hgrn_bwd(proj3, o_raw, dmixed, states, lbs_row, gn_row, name):
    bsz, t, _ = proj3.shape
    nt = t // LANES
    nchunk = t // CHUNK
    w_all, maskf, rightf = _hgrn_tables()

    def body(a_ref, or_ref, do_ref, s_sc, lb_ref, gn_ref, w_ref, mk_ref, rt_ref, da_ref, dgn_ref, dlb_ref):
        lb = lb_ref[...]
        gn = gn_ref[...]
        eye, ones_blk, ones_h = _hgrn_consts()
        r_i, c_i = _iota((LANES, LANES), 0), _iota((LANES, LANES), 1)
        suffix = ((c_i >= r_i) & ((r_i // CHUNK) == (c_i // CHUNK))).astype(BF16)
        row64 = _iota((LANES, CHUNK), 0)
        ones_t = jnp.ones((LANES, CHUNK), BF16)
        zero = jnp.zeros((CHUNK, CHUNK), F32)

        def bwd_tile(k, carry):
            dst0, dst1, dgn_acc, dlb_acc = carry
            i = nt - 1 - k
            r0 = pl.multiple_of(i * LANES, LANES)
            a = a_ref[pl.ds(r0, LANES), :]
            qa, ga = a[:, 0:128], a[:, 384:512]
            qq, kk, lf, sg, sgn, fg = _hgrn_gates(a, lb)
            parts = _split(lf, 3)
            zall = _exact_dot(w_ref[...], parts)
            eb, ee = jnp.exp(zall[0:LANES]), jnp.exp(zall[LANES:2 * LANES])
            vb = a[:, 256:384].astype(BF16)
            oraw = or_ref[pl.ds(r0, LANES), :]
            dout = do_ref[pl.ds(r0, LANES), :]
            r = lax.rsqrt(_head_sums(oraw * oraw, ones_blk) * (1.0 / CHUNK) + NORM_EPS)
            xn = oraw * r
            dga = dout * (xn * gn) * _dsilu(ga)
            don = dout * _silu(ga)
            dgn_acc = dgn_acc + jnp.sum(don * xn, axis=0, keepdims=True)
            dxn = don * gn
            do = r * (dxn - xn * (_head_sums(dxn * xn, ones_blk) * (1.0 / CHUNK)))
            dob = do.astype(BF16)
            d_att = [_dot(dob[:, CHUNK * h:CHUNK * (h + 1)], vb[:, CHUNK * h:CHUNK * (h + 1)], NT) for h in range(2)]
            att, dq, dk, db_lv = _hgrn_levels(qq, kk, zall, mk_ref, rt_ref, d_att)
            qk = _split(qq * kk, 2)
            qe_f, ke_f = qq * eb, kk * ee
            qeb, keb = qe_f.astype(BF16), ke_f.astype(BF16)
            new_ds, dq_h, dk_h, dv_h, dbl_h = [], [], [], [], []
            for h in range(2):
                hs = slice(CHUNK * h, CHUNK * (h + 1))
                a_h = att[h] + jnp.where(eye, _exact_dot_r(qk, hs, ones_h), 0.0)
                dv = _dot(a_h.astype(BF16), dob[:, hs], TN)
                ddiag = _exact_dot_r(_split(jnp.where(eye, d_att[h], 0.0), 2), slice(None), ones_t)
                dq_i = dq[:, hs] + ddiag * kk[:, hs]
                dk_i = dk[:, hs] + ddiag * qq[:, hs]
                dst = (dst0, dst1)[h]
                dq_c, dk_c, dv_c, dbl_c = [None, None], [None, None], [None, None], [None, None]
                for c in (1, 0):
                    rc = slice(CHUNK * c, CHUNK * (c + 1))
                    st_n = s_sc[h, 2 * i + c]
                    ebl = eb[CHUNK * (c + 1) - 1:CHUNK * (c + 1), hs]
                    dstb = dst.astype(BF16)
                    dv_c[c] = _dot(keb[rc, hs], dstb, NT)
                    dke = jnp.dot(vb[rc, hs], dstb, preferred_element_type=F32)
                    dqe = jnp.dot(dob[rc, hs], st_n.astype(BF16), preferred_element_type=F32)
                    dbl_c[c] = (jnp.sum(dst * st_n, axis=0, keepdims=True) * ebl
                                + jnp.sum(dke * ke_f[rc, hs], axis=0, keepdims=True))
                    dq_c[c], dk_c[c] = dqe * eb[rc, hs], dke * ee[rc, hs]
                    dst = dst * ebl + _dot(dob[rc, hs], qeb[rc, hs], TN)
                new_ds.append(dst)
                dq_x, dk_x = jnp.concatenate(dq_c, axis=0), jnp.concatenate(dk_c, axis=0)
                dq_h.append(dq_i + dq_x)
                dk_h.append(dk_i + dk_x)
                dv_h.append(dv + jnp.concatenate(dv_c, axis=0))
                dbl_h.append(qq[:, hs] * dq_x - kk[:, hs] * dk_x
                             + jnp.where(row64 == CHUNK - 1, dbl_c[0], 0.0) + jnp.where(row64 == LANES - 1, dbl_c[1], 0.0))
            dqq = jnp.concatenate(dq_h, axis=1)
            dkk = jnp.concatenate(dk_h, axis=1)
            dvv = jnp.concatenate(dv_h, axis=1)
            db = db_lv + jnp.concatenate(dbl_h, axis=1)
            dlf = _exact_dot(suffix, _split(db, 3))
            dqa = dqq * _dsilu(qa)
            dfg = jnp.where(fg > TINY, dlf / fg, 0.0)
            dz = (dfg - dkk) * (1.0 - lb) * sg * sgn
            dlb_acc = dlb_acc + jnp.sum(dfg * (1.0 - sg) - dkk * sgn, axis=0, keepdims=True)
            da_ref[pl.ds(r0, LANES), :] = jnp.concatenate([dqa, dz, dvv, dga], axis=1)
            return new_ds[0], new_ds[1], dgn_acc, dlb_acc

        zrow = jnp.zeros((1, LANES), F32)
        _, _, dgn_acc, dlb_acc = lax.fori_loop(0, nt, bwd_tile, (zero, zero, zrow, zrow))
        dgn_ref[...] = jnp.broadcast_to(dgn_acc, (8, LANES))
        dlb_ref[...] = jnp.broadcast_to(dlb_acc, (8, LANES))

    rows = jax.ShapeDtypeStruct((bsz, 8, HGRN_W), F32)
    row = pl.BlockSpec((1, 128), lambda b, p: (0, p))
    blk = pl.BlockSpec((None, t, 128), lambda b, p: (b, 0, p))
    return pl.pallas_call(
        body, grid=(bsz, 2),
        in_specs=[pl.BlockSpec((None, t, 512), lambda b, p: (b, 0, p)), blk, blk,
                  pl.BlockSpec((None, 2, nchunk, CHUNK, CHUNK), lambda b, p: (b, p, 0, 0, 0)), row, row,
                  pl.BlockSpec(w_all.shape, lambda b, p: (0, 0)),
                  pl.BlockSpec(maskf.shape, lambda b, p: (0, 0, 0)),
                  pl.BlockSpec(rightf.shape, lambda b, p: (0, 0, 0))],
        out_specs=[pl.BlockSpec((None, t, 512), lambda b, p: (b, 0, p)),
                   pl.BlockSpec((None, 8, 128), lambda b, p: (b, 0, p)),
                   pl.BlockSpec((None, 8, 128), lambda b, p: (b, 0, p))],
        out_shape=[jax.ShapeDtypeStruct((bsz, t, A_W), F32), rows, rows],
        compiler_params=_cparams(("parallel", "parallel")), name=name)(
            proj3, o_raw, dmixed, states, lbs_row, gn_row, w_all, maskf, rightf)


def _pool_tt(t):
    return min(256, t)


def _window_select(s2, s4, s8, s16, lane):
    return jnp.where(lane < 64, s2, jnp.where(lane < 128, s4, jnp.where(lane < 192, s8, s16)))


def _pool_counts(t0, tt):
    lane = _iota((tt, POOL_W), 1)
    tpos = (_iota((tt, POOL_W), 0) + t0 + 1).astype(F32)
    win = jnp.where(lane < 64, 2.0, jnp.where(lane < 128, 4.0, jnp.where(lane < 192, 8.0, 16.0)))
    return 1.0 / jnp.minimum(tpos, win), lane


def _pooled_tile(upad_ref, i, tt):
    r0 = pl.multiple_of(i * tt, 8)
    cat = upad_ref[pl.ds(r0, tt + POOL_HALO), :]
    s2 = cat + pltpu.roll(cat, 1, 0)
    s4 = s2 + pltpu.roll(s2, 2, 0)
    s8 = s4 + pltpu.roll(s4, 4, 0)
    s16 = s8 + pltpu.roll(s8, 8, 0)
    inv, lane = _pool_counts(i * tt, tt)
    sel = _window_select(s2[POOL_HALO:], s4[POOL_HALO:], s8[POOL_HALO:], s16[POOL_HALO:], lane)
    return sel * inv - cat[POOL_HALO:], inv, lane


def _pool_fwd(proj3, wbd, scale_row, name):
    bsz, t, _ = proj3.shape
    tt = _pool_tt(t)

    def body(p_ref, w_ref, sc_ref, o_ref, upad):
        upad[0:POOL_HALO, :] = jnp.zeros((POOL_HALO, POOL_W), F32)
        upad[POOL_HALO:, :] = p_ref[:, 0:POOL_W]
        w = w_ref[...]
        sc = sc_ref[...]

        def tile(i, c):
            pooled, _, _ = _pooled_tile(upad, i, tt)
            r0 = pl.multiple_of(i * tt, 8)
            g = p_ref[pl.ds(r0, tt), POOL_W:2 * POOL_W]
            pre = jnp.dot(pooled.astype(BF16), w, preferred_element_type=F32)
            o_ref[pl.ds(r0, tt), :] = pre * sc * _silu(g)
            return c

        lax.fori_loop(0, t // tt, tile, 0)

    return pl.pallas_call(
        body, grid=(bsz,),
        in_specs=[pl.BlockSpec((None, t, 512), lambda b: (b, 0, B_BLK)),
                  pl.BlockSpec((POOL_W, POOL_W), lambda b: (0, 0)),
                  pl.BlockSpec((1, POOL_W), lambda b: (0, 0))],
        out_specs=pl.BlockSpec((None, t, POOL_W), lambda b: (b, 0, 0)),
        out_shape=jax.ShapeDtypeStruct((bsz, t, POOL_W), F32),
        scratch_shapes=[pltpu.VMEM((t + POOL_HALO, POOL_W), F32)],
        compiler_params=_cparams(("parallel",)), name=name)(proj3, wbd, scale_row)


def _pool_bwd(proj3, dmixed, wbd, scale_row, name):
    bsz, t, _ = proj3.shape
    tt = _pool_tt(t)

    def body(p_ref, do_ref, w_ref, sc_ref, db_ref, dsc_ref, dw_ref, upad, epad):
        upad[0:POOL_HALO, :] = jnp.zeros((POOL_HALO, POOL_W), F32)
        upad[POOL_HALO:, :] = p_ref[:, 0:POOL_W]
        epad[t:, :] = jnp.zeros((POOL_HALO, POOL_W), F32)
        w = w_ref[...]
        sc = sc_ref[...]

        def tile(i, carry):
            dsc_acc, dw_acc = carry
            pooled, inv, _ = _pooled_tile(upad, i, tt)
            r0 = pl.multiple_of(i * tt, 8)
            g = p_ref[pl.ds(r0, tt), POOL_W:2 * POOL_W]
            dout = do_ref[pl.ds(r0, tt), :]
            pb = pooled.astype(BF16)
            pre = jnp.dot(pb, w, preferred_element_type=F32)
            t1 = dout * _silu(g)
            dsc_acc = dsc_acc + jnp.sum(t1 * pre, axis=0, keepdims=True)
            dpre = (t1 * sc).astype(BF16)
            db_ref[pl.ds(r0, tt), POOL_W:2 * POOL_W] = dout * pre * sc * _dsilu(g)
            dw_acc = dw_acc + _dot(pb, dpre, TN)
            dpooled = _dot(dpre, w, NT)
            epad[pl.ds(r0, tt), :] = dpooled * inv
            return dsc_acc, dw_acc

        dsc_acc, dw_acc = lax.fori_loop(0, t // tt, tile, (jnp.zeros((1, POOL_W), F32), jnp.zeros((POOL_W, POOL_W), F32)))
        dsc_ref[...] = jnp.broadcast_to(dsc_acc, (8, POOL_W))
        dw_ref[...] = dw_acc

        def tile2(i, c):
            r0 = pl.multiple_of(i * tt, 8)
            n = tt + POOL_HALO
            cat = epad[pl.ds(r0, n), :]
            s2 = cat + pltpu.roll(cat, n - 1, 0)
            s4 = s2 + pltpu.roll(s2, n - 2, 0)
            s8 = s4 + pltpu.roll(s4, n - 4, 0)
            s16 = s8 + pltpu.roll(s8, n - 8, 0)
            inv, lane = _pool_counts(i * tt, tt)
            sel = _window_select(s2[:tt], s4[:tt], s8[:tt], s16[:tt], lane)
            db_ref[pl.ds(r0, tt), 0:POOL_W] = sel - cat[:tt] / inv
            return c

        lax.fori_loop(0, t // tt, tile2, 0)

    return pl.pallas_call(
        body, grid=(bsz,),
        in_specs=[pl.BlockSpec((None, t, 512), lambda b: (b, 0, B_BLK)),
                  pl.BlockSpec((None, t, POOL_W), lambda b: (b, 0, 1)),
                  pl.BlockSpec((POOL_W, POOL_W), lambda b: (0, 0)),
                  pl.BlockSpec((1, POOL_W), lambda b: (0, 0))],
        out_specs=[pl.BlockSpec((None, t, 512), lambda b: (b, 0, 0)),
                   pl.BlockSpec((None, 8, POOL_W), lambda b: (b, 0, 0)),
                   pl.BlockSpec((None, POOL_W, POOL_W), lambda b: (b, 0, 0))],
        out_shape=[jax.ShapeDtypeStruct((bsz, t, B_W), F32), jax.ShapeDtypeStruct((bsz, 8, POOL_W), F32),
                   jax.ShapeDtypeStruct((bsz, POOL_W, POOL_W), F32)],
        scratch_shapes=[pltpu.VMEM((t + POOL_HALO, POOL_W), F32), pltpu.VMEM((t + POOL_HALO, POOL_W), F32)],
        compiler_params=_cparams(("parallel",)), name=name)(proj3, dmixed, wbd, scale_row)


def _head_select_rows(hp):
    r, c = _iota((8, LANES), 0), _iota((8, LANES), 1)
    return ((r < 2) & (c == 2 * hp + r)).astype(F32)


def _foxgate_fwd(proj3, bias_row, name):
    bsz, t, _ = proj3.shape
    nt = t // LANES

    def body(f_ref, b_ref, cn_ref, ct_ref):
        bias = b_ref[...]
        i, j = _iota((LANES, LANES), 0), _iota((LANES, LANES), 1)
        lower = (j <= i).astype(F32)
        spread = (_iota((LANES, FOX_W), 0) == _iota((LANES, FOX_W), 1) // 64).astype(F32)

        def tile(k, carry):
            r0 = pl.multiple_of(k * LANES, LANES)
            xg = f_ref[pl.ds(r0, LANES), :] + bias
            lf = jnp.minimum(xg, 0.0) - jnp.log(1.0 + jnp.exp(-jnp.abs(xg)))
            c = jnp.dot(lower, lf, precision=HI, preferred_element_type=F32) + carry
            cn_ref[pl.ds(r0, LANES), :] = jnp.dot(c, spread, precision=HI, preferred_element_type=F32)
            for hp in range(4):
                ct_ref[hp, :, pl.ds(r0, LANES)] = _dot(_head_select_rows(hp), c, NT, precision=HI)
            return c[LANES - 1:LANES, :]

        lax.fori_loop(0, nt, tile, jnp.zeros((1, LANES), F32))

    return pl.pallas_call(
        body, grid=(bsz,),
        in_specs=[pl.BlockSpec((None, t, 128), lambda b: (b, 0, F_BLK)), pl.BlockSpec((1, 128), lambda b: (0, 0))],
        out_specs=[pl.BlockSpec((None, t, FOX_W), lambda b: (b, 0, 0)),
                   pl.BlockSpec((None, 4, 8, t), lambda b: (b, 0, 0, 0))],
        out_shape=[jax.ShapeDtypeStruct((bsz, t, FOX_W), F32), jax.ShapeDtypeStruct((bsz, 4, 8, t), F32)],
        compiler_params=_cparams(("parallel",)), name=name)(proj3, bias_row)


def _foxgate_bwd(proj3, dc_nat, bias_row, name):
    bsz, t, _ = proj3.shape
    nt = t // LANES

    def body(f_ref, dc_ref, b_ref, df_ref, dbias_ref, run_sc):
        bias = b_ref[...]
        i, j = _iota((LANES, LANES), 0), _iota((LANES, LANES), 1)
        upper = (j >= i).astype(F32)
        valid = _iota((1, LANES), 1) < FOX_HEADS
        run_sc[...] = jnp.zeros((8, LANES), F32)
        dbias_ref[...] = jnp.zeros((8, LANES), F32)

        def tile(k, c):
            r0 = pl.multiple_of((nt - 1 - k) * LANES, LANES)
            dc = dc_ref[pl.ds(r0, LANES), :] + jnp.where(i == LANES - 1, run_sc[0:1, :], 0.0)
            dlf = jnp.dot(upper, dc, precision=HI, preferred_element_type=F32)
            xg = f_ref[pl.ds(r0, LANES), :] + bias
            df = jnp.where(valid, dlf * _sig(-xg), 0.0)
            df_ref[pl.ds(r0, LANES), :] = df
            run_sc[...] = dlf[0:8, :]
            dbias_ref[...] += jnp.sum(df, axis=0, keepdims=True)
            return c

        lax.fori_loop(0, nt, tile, 0)

    blk = pl.BlockSpec((None, t, 128), lambda b: (b, 0, 0))
    return pl.pallas_call(
        body, grid=(bsz,),
        in_specs=[pl.BlockSpec((None, t, 128), lambda b: (b, 0, F_BLK)), blk, pl.BlockSpec((1, 128), lambda b: (0, 0))],
        out_specs=[blk, pl.BlockSpec((None, 8, 128), lambda b: (b, 0, 0))],
        out_shape=[jax.ShapeDtypeStruct((bsz, t, F_W), F32), jax.ShapeDtypeStruct((bsz, 8, 128), F32)],
        scratch_shapes=[pltpu.VMEM((8, LANES), F32)],
        compiler_params=_cparams(("parallel",)), name=name)(proj3, dc_nat, bias_row)


def _fox_tile(t):
    return min(256, t)


def _fox_fwd(proj3, c_nat, c_t, name):
    bsz, t, _ = proj3.shape
    tq = _fox_tile(t)
    tk = min(2 * tq, t)
    nq = t // tq

    def body(q_ref, kv_ref, cn_ref, ct_ref, og_ref, or_ref, lse_ref):
        i = pl.program_id(2)
        qblk = q_ref[...]
        first = _iota((1, 128), 1) < 64
        qv = qblk[:, 0:128] * 0.125
        qm = [jnp.where(first, qv, 0.0).astype(BF16), jnp.where(first, 0.0, qv).astype(BF16)]
        cqs = [cn_ref[:, 0:1], cn_ref[:, 64:65]]
        rows = _iota((tq, tk), 0) + i * tq

        def kv_step(j, carry, masked):
            c0 = pl.multiple_of(j * tk, tk)
            kb = kv_ref[pl.ds(c0, tk), 128:256].astype(BF16)
            vblk = kv_ref[pl.ds(c0, tk), 256:384]
            vx = [jnp.where(first, vblk, 1.0).astype(BF16), jnp.where(first, 1.0, vblk).astype(BF16)]
            new = []
            for h in range(2):
                m, acc = carry[2 * h], carry[2 * h + 1]
                s = _dot(qm[h], kb, NT) + (cqs[h] - ct_ref[h:h + 1, pl.ds(c0, tk)])
                if masked:
                    s = jnp.where(rows >= _iota((tq, tk), 1) + j * tk, s, MASK_VALUE)
                m_new = jnp.maximum(m, jnp.max(s, axis=1, keepdims=True))
                p = jnp.exp(s - m_new).astype(BF16)
                new += [m_new, jnp.exp(m - m_new) * acc + jnp.dot(p, vx[h], preferred_element_type=F32)]
            return tuple(new)

        init = (jnp.full((tq, 1), MASK_VALUE, F32), jnp.zeros((tq, 128), F32)) * 2
        n_full = (i * tq) // tk
        carry = lax.fori_loop(0, n_full, functools.partial(kv_step, masked=False), init)
        m0, acc0, m1, acc1 = kv_step(n_full, carry, True)
        l0, l1 = pltpu.roll(acc0, 64, 1), pltpu.roll(acc1, 64, 1)
        o = jnp.where(first, acc0 / l0, acc1 / l1)
        or_ref[...] = o
        og_ref[...] = o * _silu(qblk[:, 384:512])
        lse_ref[...] = jnp.where(first, m0 + jnp.log(l0), m1 + jnp.log(l1))

    out = jax.ShapeDtypeStruct((bsz, t, FOX_W), F32)
    blk = pl.BlockSpec((None, tq, 128), lambda b, p, i: (b, i, p))
    return pl.pallas_call(
        body, grid=(bsz, 4, nq),
        in_specs=[pl.BlockSpec((None, tq, 512), lambda b, p, i: (b, i, C_BLK0 + p)),
                  pl.BlockSpec((None, t, 512), lambda b, p, i: (b, 0, C_BLK0 + p)),
                  blk,
                  pl.BlockSpec((None, None, 8, t), lambda b, p, i: (b, p, 0, 0))],
        out_specs=[blk, blk, blk],
        out_shape=[out, out, out],
        compiler_params=_cparams(("parallel", "parallel", "arbitrary")), name=name)(proj3, proj3, c_nat, c_t)


def _fox_bwd(proj3, o_raw, dmixed, lse, c_nat, c_t, name):
    bsz, t, _ = proj3.shape
    tq = _fox_tile(t)
    nq = t // tq
    tk = min(2 * tq, t)
    ratio = tk // tq

    def body(a_ref, or_ref, do_ref, lse_ref, cn_ref, ct_ref, dc_out, dct_out, drow_out, dq_sc, do_sc, dl_sc):
        def prep(i, c):
            r0 = pl.multiple_of(i * tq, tq)
            g = a_ref[pl.ds(r0, tq), 384:512]
            dout = do_ref[pl.ds(r0, tq), :]
            o = or_ref[pl.ds(r0, tq), :]
            dc_out[pl.ds(r0, tq), 384:512] = dout * o * _dsilu(g)
            do = dout * _silu(g)
            do_sc[pl.ds(r0, tq), :] = do
            prod = do * o
            d0 = jnp.sum(prod[:, 0:64], axis=1, keepdims=True)
            d1 = jnp.sum(prod[:, 64:128], axis=1, keepdims=True)
            dl_sc[pl.ds(r0, tq), :] = jnp.concatenate([jnp.broadcast_to(d0, (tq, 64)), jnp.broadcast_to(d1, (tq, 64))], axis=1)
            dq_sc[pl.ds(r0, tq), :] = jnp.zeros((tq, 128), F32)
            drow_out[pl.ds(r0, tq), :] = jnp.zeros((tq, 128), F32)
            return c

        lax.fori_loop(0, nq, prep, 0)
        dct_out[...] = jnp.zeros((8, t), F32)

        first = _iota((1, 128), 1) < 64

        def heads(v):
            return [jnp.where(first, v, 0.0).astype(BF16), jnp.where(first, 0.0, v).astype(BF16)]

        def kv_tile(j, c):
            c0 = pl.multiple_of(j * tk, tk)
            kb = a_ref[pl.ds(c0, tk), 128:256].astype(BF16)
            vb = a_ref[pl.ds(c0, tk), 256:384].astype(BF16)
            cks = [ct_ref[h:h + 1, pl.ds(c0, tk)] for h in range(2)]

            def q_step(i, carry, diagonal):
                dk, dv, dcol0, dcol1 = carry
                r0 = pl.multiple_of(i * tq, tq)
                causal = _iota((tq, tk), 0) + i * tq >= _iota((tq, tk), 1) + j * tk
                qv = a_ref[pl.ds(r0, tq), 0:128] * 0.125
                do = do_sc[pl.ds(r0, tq), :]
                qb, dob = qv.astype(BF16), do.astype(BF16)
                qm, dom = heads(qv), heads(do)
                full, dcols, rsums = [], [], []
                for h in range(2):
                    lse_h = lse_ref[pl.ds(r0, tq), 64 * h:64 * h + 1]
                    dl_h = dl_sc[pl.ds(r0, tq), 64 * h:64 * h + 1]
                    cq = cn_ref[pl.ds(r0, tq), 64 * h:64 * h + 1]
                    p = jnp.exp(_dot(qm[h], kb, NT) + (cq - cks[h]) - lse_h)
                    if diagonal:
                        p = jnp.where(causal, p, 0.0)
                    ds = p * (_dot(dom[h], vb, NT) - dl_h)
                    dsb = ds.astype(BF16)
                    full.append((_dot(p.astype(BF16), dob, TN), _dot(dsb, qb, TN),
                                 jnp.dot(dsb, kb, preferred_element_type=F32)))
                    dcols.append(jnp.sum(ds, axis=0, keepdims=True))
                    rsums.append(jnp.broadcast_to(jnp.sum(ds, axis=1, keepdims=True), (tq, 128)))
                dq_sc[pl.ds(r0, tq), :] += jnp.where(first, full[0][2], full[1][2]) * 0.125
                drow_out[pl.ds(r0, tq), :] += jnp.where(first, rsums[0], rsums[1])
                return (dk + jnp.where(first, full[0][1], full[1][1]), dv + jnp.where(first, full[0][0], full[1][0]),
                        dcol0 - dcols[0], dcol1 - dcols[1])

            carry = (jnp.zeros((tk, 128), F32), jnp.zeros((tk, 128), F32), jnp.zeros((1, tk), F32), jnp.zeros((1, tk), F32))
            for r in range(ratio):
                carry = q_step(ratio * j + r, carry, True)
            dk, dv, dcol0, dcol1 = lax.fori_loop(ratio * (j + 1), nq, functools.partial(q_step, diagonal=False), carry)
            dct_out[0:1, pl.ds(c0, tk)] = dcol0
            dct_out[1:2, pl.ds(c0, tk)] = dcol1
            dc_out[pl.ds(c0, tk), 128:256] = dk
            dc_out[pl.ds(c0, tk), 256:384] = dv
            return c

        lax.fori_loop(0, t // tk, kv_tile, 0)
        dc_out[:, 0:128] = dq_sc[...]

    blk = pl.BlockSpec((None, t, 128), lambda b, p: (b, 0, p))
    return pl.pallas_call(
        body, grid=(bsz, 4),
        in_specs=[pl.BlockSpec((None, t, 512), lambda b, p: (b, 0, C_BLK0 + p)),
                  blk,
                  pl.BlockSpec((None, t, 128), lambda b, p: (b, 0, 4 + p)),
                  blk, blk,
                  pl.BlockSpec((None, None, 8, t), lambda b, p: (b, p, 0, 0))],
        out_specs=[pl.BlockSpec((None, t, 512), lambda b, p: (b, 0, p)),
                   pl.BlockSpec((None, None, 8, t), lambda b, p: (b, p, 0, 0)), blk],
        out_shape=[jax.ShapeDtypeStruct((bsz, t, C_W), F32), jax.ShapeDtypeStruct((bsz, 4, 8, t), F32),
                   jax.ShapeDtypeStruct((bsz, t, FOX_W), F32)],
        scratch_shapes=[pltpu.VMEM((t, 128), F32), pltpu.VMEM((t, 128), F32), pltpu.VMEM((t, 128), F32)],
        compiler_params=_cparams(("parallel", "parallel")), name=name)(proj3, o_raw, dmixed, lse, c_nat, c_t)


def _mix_tm(n):
    return min(512, n)


def _outproj_fwd(x2, oa, ob, oc, wo, g_row, name):
    n, d = x2.shape
    tm = _mix_tm(n)

    def body(x_ref, oa_ref, ob_ref, oc_ref, w_ref, g_ref, y_ref, xo_ref):
        y = (jnp.dot(oa_ref[...].astype(BF16), w_ref[0:256, :], preferred_element_type=F32)
             + jnp.dot(ob_ref[...].astype(BF16), w_ref[256:512, :], preferred_element_type=F32)
             + jnp.dot(oc_ref[...].astype(BF16), w_ref[512:1024, :], preferred_element_type=F32))
        y_ref[...] = y
        xo_ref[...] = x_ref[...] + y * _rstd(y) * g_ref[...]

    row = lambda w: pl.BlockSpec((tm, w), lambda i: (i, 0))
    out = jax.ShapeDtypeStruct((n, d), F32)
    return pl.pallas_call(
        body, grid=(n // tm,),
        in_specs=[row(d), row(256), row(256), row(512), pl.BlockSpec((d, d), lambda i: (0, 0)),
                  pl.BlockSpec((1, d), lambda i: (0, 0))],
        out_specs=[row(d), row(d)], out_shape=[out, out],
        compiler_params=_cparams(("parallel",)), name=name)(x2, oa, ob, oc, wo, g_row)


def _loss_head(x2, target2, name):
    n, d = x2.shape
    tm = _mix_tm(n)

    def body(x_ref, t_ref, dx_ref, l_ref):
        err = x_ref[...] - t_ref[...]
        dx_ref[...] = err * (1.0 / d)

        @pl.when(pl.program_id(0) == 0)
        def _():
            l_ref[...] = jnp.zeros((8, 128), F32)

        l_ref[...] += jnp.sum(err * err)

    row = pl.BlockSpec((tm, d), lambda i: (i, 0))
    return pl.pallas_call(
        body, grid=(n // tm,), in_specs=[row, row],
        out_specs=[row, pl.BlockSpec((8, 128), lambda i: (0, 0))],
        out_shape=[jax.ShapeDtypeStruct((n, d), F32), jax.ShapeDtypeStruct((8, 128), F32)],
        compiler_params=_cparams(("arbitrary",)), name=name)(x2, target2)


def _outproj_bwd(dxo, y, oa, ob, oc, wo, g_row, name):
    n, d = dxo.shape
    tm = _mix_tm(n)

    def body(dx_ref, y_ref, oa_ref, ob_ref, oc_ref, w_ref, g_ref, dm_ref, dw_ref, dg_ref):
        @pl.when(pl.program_id(0) == 0)
        def _():
            dw_ref[...] = jnp.zeros((d, d), F32)
            dg_ref[...] = jnp.zeros((8, d), F32)

        yv, dx = y_ref[...], dx_ref[...]
        r = _rstd(yv)
        yn = yv * r
        dg_ref[...] += jnp.sum(dx * yn, axis=0, keepdims=True)
        dyn = dx * g_ref[...]
        dy = (r * (dyn - yn * jnp.mean(dyn * yn, axis=-1, keepdims=True))).astype(BF16)
        dm_ref[...] = _dot(dy, w_ref[...], NT)
        dw_ref[0:256, :] += _dot(oa_ref[...].astype(BF16), dy, TN)
        dw_ref[256:512, :] += _dot(ob_ref[...].astype(BF16), dy, TN)
        dw_ref[512:1024, :] += _dot(oc_ref[...].astype(BF16), dy, TN)

    row = lambda w: pl.BlockSpec((tm, w), lambda i: (i, 0))
    fixed = lambda r, c: pl.BlockSpec((r, c), lambda i: (0, 0))
    return pl.pallas_call(
        body, grid=(n // tm,),
        in_specs=[row(d), row(d), row(256), row(256), row(512), fixed(d, d), fixed(1, d)],
        out_specs=[row(d), fixed(d, d), fixed(8, d)],
        out_shape=[jax.ShapeDtypeStruct((n, d), F32), jax.ShapeDtypeStruct((d, d), F32), jax.ShapeDtypeStruct((8, d), F32)],
        compiler_params=_cparams(("arbitrary",)), name=name)(dxo, y, oa, ob, oc, wo, g_row)


_PIECES = ((0, A_W), (A_W, B_W), (A_W + B_W, C_W), (A_W + B_W + C_W, F_W))


def _inproj_bwd_x(x2, dxo, g_row, w_int, pieces, name):
    n, d = x2.shape
    tm = min(256, n)

    def body(x_ref, dxo_ref, g_ref, w_ref, da_ref, db_ref, dc_ref, df_ref, dx_ref, dg_ref):
        @pl.when(pl.program_id(0) == 0)
        def _():
            dg_ref[...] = jnp.zeros((8, d), F32)

        dh = jnp.zeros((tm, d), F32)
        for ref, (o, w) in zip((da_ref, db_ref, dc_ref, df_ref), _PIECES):
            dh = dh + _dot(ref[...].astype(BF16), w_ref[:, o:o + w], NT)
        x = x_ref[...]
        r = _rstd(x)
        xn = x * r
        dg_ref[...] += jnp.sum(dh * xn, axis=0, keepdims=True)
        dxn = dh * g_ref[...]
        dx_ref[...] = dxo_ref[...] + r * (dxn - xn * jnp.mean(dxn * xn, axis=-1, keepdims=True))

    row = lambda w: pl.BlockSpec((tm, w), lambda i: (i, 0))
    fixed = lambda r, c: pl.BlockSpec((r, c), lambda i: (0, 0))
    return pl.pallas_call(
        body, grid=(n // tm,),
        in_specs=[row(d), row(d), fixed(1, d), fixed(d, E_INT)] + [row(w) for _, w in _PIECES],
        out_specs=[row(d), fixed(8, d)],
        out_shape=[jax.ShapeDtypeStruct((n, d), F32), jax.ShapeDtypeStruct((8, d), F32)],
        compiler_params=_cparams(("arbitrary",)), name=name)(x2, dxo, g_row, w_int, *pieces)


def _inproj_bwd_w(x2, g_row, pieces, name):
    n, d = x2.shape
    tm = min(256, n)

    def body(x_ref, g_ref, da_ref, db_ref, dc_ref, df_ref, dw_ref):
        @pl.when(pl.program_id(0) == 0)
        def _():
            dw_ref[...] = jnp.zeros((d, E_INT), F32)

        x = x_ref[...]
        h = (x * _rstd(x) * g_ref[...]).astype(BF16)
        for ref, (o, w) in zip((da_ref, db_ref, dc_ref, df_ref), _PIECES):
            dw_ref[:, o:o + w] += _dot(h, ref[...].astype(BF16), TN)

    row = lambda w: pl.BlockSpec((tm, w), lambda i: (i, 0))
    return pl.pallas_call(
        body, grid=(n // tm,),
        in_specs=[row(d), pl.BlockSpec((1, d), lambda i: (0, 0))] + [row(w) for _, w in _PIECES],
        out_specs=pl.BlockSpec((d, E_INT), lambda i: (0, 0)),
        out_shape=jax.ShapeDtypeStruct((d, E_INT), F32),
        compiler_params=_cparams(("arbitrary",), vmem_mb=56), name=name)(x2, g_row, *pieces)


def _block_diag(pool_w_l):
    z = jnp.zeros((64, 64), pool_w_l.dtype)
    return jnp.concatenate(
        [jnp.concatenate([pool_w_l[g] if c == g else z for c in range(4)], axis=1) for g in range(4)], axis=0)


def _pad_lanes(v, width=128):
    return jnp.pad(v, ((0, 0),) * (v.ndim - 1) + ((0, width - v.shape[-1]),))


def _local_step(x, target, lower_bounds, pre_norm_g, w_in_int, hgrn_norm_g, fox_f_bias, pool_w, pool_scale,
                w_out_bf, post_norm_g, on_weight_grads):
    bsz, t, d = x.shape
    n = bsz * t
    lbs = _lbs_fwd(lower_bounds)
    saved = []
    xc = x.reshape(n, d)
    for l in range(DEPTH):
        proj = _inproj_fwd(xc, pre_norm_g[l:l + 1], w_in_int[l], f"inproj_fwd{l}").reshape(bsz, t, E_INT)
        wbd = _block_diag(pool_w[l]).astype(BF16)
        bias_row = _pad_lanes(fox_f_bias[l:l + 1])
        oa, oa_raw, states = _hgrn_fwd(proj, lbs[l:l + 1], hgrn_norm_g[l:l + 1], f"hgrn_fwd{l}")
        ob = _pool_fwd(proj, wbd, pool_scale[l:l + 1], f"pool_fwd{l}")
        c_nat, c_t = _foxgate_fwd(proj, bias_row, f"foxgate_fwd{l}")
        oc, oc_raw, lse = _fox_fwd(proj, c_nat, c_t, f"fox_fwd{l}")
        y, xn = _outproj_fwd(xc, oa.reshape(n, -1), ob.reshape(n, -1), oc.reshape(n, -1), w_out_bf[l],
                             post_norm_g[l:l + 1], f"outproj_fwd{l}")
        saved.append((xc, proj, wbd, bias_row, oa, oa_raw, states, ob, oc, oc_raw, lse, c_nat, c_t, y))
        xc = xn
    dx, sq = _loss_head(xc, target.reshape(n, d), "loss_head")
    g = {k: [None] * DEPTH for k in ("pre", "hgn", "bias", "pool_w", "pool_scale", "post", "lbs")}
    handed = [None] * DEPTH
    for l in reversed(range(DEPTH)):
        xin, proj, wbd, bias_row, oa, oa_raw, states, ob, oc, oc_raw, lse, c_nat, c_t, y = saved[l]
        dmix, d_w_out, dpost = _outproj_bwd(dx, y, oa.reshape(n, -1), ob.reshape(n, -1), oc.reshape(n, -1),
                                            w_out_bf[l], post_norm_g[l:l + 1], f"outproj_bwd{l}")
        g["post"][l] = dpost[0]
        dmix3 = dmix.reshape(bsz, t, d)
        d_c, dct, drow = _fox_bwd(proj, oc_raw, dmix3, lse, c_nat, c_t, f"fox_bwd{l}")
        dc_nat = _pad_lanes(dct[:, :, 0:2, :].reshape(bsz, FOX_HEADS, t).transpose(0, 2, 1)
                            + drow.reshape(bsz, t, FOX_HEADS, 64)[..., 0])
        d_f, dbias = _foxgate_bwd(proj, dc_nat, bias_row, f"foxgate_bwd{l}")
        g["bias"][l] = jnp.sum(dbias[:, 0, :FOX_HEADS], axis=0)
        d_b, dscale, dwbd = _pool_bwd(proj, dmix3, wbd, pool_scale[l:l + 1], f"pool_bwd{l}")
        g["pool_scale"][l] = jnp.sum(dscale[:, 0], axis=0)
        dwbd = jnp.sum(dwbd, axis=0)
        g["pool_w"][l] = jnp.stack([dwbd[64 * k:64 * (k + 1), 64 * k:64 * (k + 1)] for k in range(4)])
        d_a, dgn, dlb = _hgrn_bwd(proj, oa_raw, dmix3, states, lbs[l:l + 1], hgrn_norm_g[l:l + 1], f"hgrn_bwd{l}")
        g["hgn"][l] = jnp.sum(dgn[:, 0], axis=0)
        g["lbs"][l] = jnp.sum(dlb[:, 0], axis=0)
        pieces = [p.reshape(n, -1) for p in (d_a, d_b, d_c, d_f)]
        handed[l] = on_weight_grads(l, _inproj_bwd_w(xin, pre_norm_g[l:l + 1], pieces, f"inproj_bwd_w{l}"), d_w_out)
        dx, dpre = _inproj_bwd_x(xin, dx, pre_norm_g[l:l + 1], w_in_int[l], pieces, f"inproj_bwd_x{l}")
        g["pre"][l] = dpre[0]
    grads = {k: jnp.stack(v) for k, v in g.items()}
    return sq, dx.reshape(bsz, t, d), grads, handed


def _place():
    return lax.axis_index("x"), lax.axis_index("y"), lax.axis_index("c")


def _other_chips(x, y):
    return [(1 - x, y), (x, 1 - y), (1 - x, 1 - y)]


_ANY = pl.BlockSpec(memory_space=pl.ANY)


def _gather_body(handshake):
    def body(win_ref, wout_ref, ain_ref, aout_ref, ici_send, ici_recv, d2d_send, d2d_recv, local_sems):
        x, y, c = _place()
        if handshake:
            barrier = pltpu.get_barrier_semaphore()
            for peer in [(px, py, c) for px, py in _other_chips(x, y)] + [(x, y, 1 - c)]:
                pl.semaphore_signal(barrier, inc=1, device_id=peer, device_id_type=MESH)
            pl.semaphore_wait(barrier, 4)
        me = 2 * x + y
        pairs = ((win_ref, ain_ref), (wout_ref, aout_ref))
        mine = [pltpu.make_async_copy(src, dst.at[me], local_sems.at[j]) for j, (src, dst) in enumerate(pairs)]
        for cp in mine:
            cp.start()
        chips = _other_chips(x, y)
        sends = [pltpu.make_async_remote_copy(
            src_ref=src.at[c], dst_ref=dst.at[me, c], send_sem=ici_send.at[2 * k + j], recv_sem=ici_recv.at[2 * k + j],
            device_id=(px, py, c), device_id_type=MESH) for k, (px, py) in enumerate(chips) for j, (src, dst) in enumerate(pairs)]
        for cp in sends:
            cp.start()
        passed = [pltpu.make_async_remote_copy(
            src_ref=dst.at[2 * px + py, c], dst_ref=dst.at[2 * px + py, c], send_sem=d2d_send.at[2 * k + j],
            recv_sem=d2d_recv.at[2 * k + j], device_id=(x, y, 1 - c), device_id_type=MESH)
            for k, (px, py) in enumerate(chips) for j, (src, dst) in enumerate(pairs)]
        for n, (k, j) in enumerate((k, j) for k in range(3) for j in range(2)):
            px, py = chips[k]
            src, dst = pairs[j]
            pltpu.make_async_remote_copy(
                src_ref=src.at[c], dst_ref=dst.at[2 * px + py, c], send_sem=ici_send.at[n], recv_sem=ici_recv.at[n],
                device_id=(px, py, c), device_id_type=MESH).wait_recv()
            passed[n].start()
        for n, (k, j) in enumerate((k, j) for k in range(3) for j in range(2)):
            px, py = chips[k]
            src, dst = pairs[j]
            pltpu.make_async_remote_copy(
                src_ref=dst.at[2 * px + py, 1 - c], dst_ref=dst.at[2 * px + py, 1 - c], send_sem=d2d_send.at[n],
                recv_sem=d2d_recv.at[n], device_id=(x, y, 1 - c), device_id_type=MESH).wait_recv()
        for cp in sends + passed:
            cp.wait_send()
        for cp in mine:
            cp.wait()

    return body


_GATHER_SEMS = [pltpu.SemaphoreType.DMA((6,))] * 4 + [pltpu.SemaphoreType.DMA((2,))]


def _gather_weights(w_in_sh, w_out_sh):
    return pl.pallas_call(
        _gather_body(False), in_specs=[_ANY, _ANY], out_specs=[_ANY, _ANY],
        out_shape=[jax.ShapeDtypeStruct((N_CHIPS,) + w_in_sh.shape, w_in_sh.dtype),
                   jax.ShapeDtypeStruct((N_CHIPS,) + w_out_sh.shape, w_out_sh.dtype)],
        scratch_shapes=_GATHER_SEMS, name="gather_weights")(w_in_sh, w_out_sh)


def _gather_weights_beside(w_in_sh, w_out_sh):
    hbm = pltpu.MemorySpace.HBM
    win_ref, wout_ref = jax.new_ref(w_in_sh, memory_space=hbm), jax.new_ref(w_out_sh, memory_space=hbm)
    ain_ref = jax.empty_ref(jax.ShapeDtypeStruct((N_CHIPS,) + w_in_sh.shape, w_in_sh.dtype), memory_space=hbm)
    aout_ref = jax.empty_ref(jax.ShapeDtypeStruct((N_CHIPS,) + w_out_sh.shape, w_out_sh.dtype), memory_space=hbm)
    body = _gather_body(True)

    @pl.kernel(mesh=plsc.ScalarSubcoreMesh(axis_name="sequencer", num_cores=1), name="gather_weights_beside",
               scratch_types=_GATHER_SEMS, compiler_params=pltpu.CompilerParams(collective_id=1))
    def launch(ici_send, ici_recv, d2d_send, d2d_recv, local_sems):
        body(win_ref, wout_ref, ain_ref, aout_ref, ici_send, ici_recv, d2d_send, d2d_recv, local_sems)

    launch()
    return ain_ref[...], aout_ref[...]


def _swap_with_sibling(parts, name):
    k = len(parts)

    def body(*refs):
        src, dst = refs[:k], refs[k:2 * k]
        send_sems, recv_sems = refs[2 * k:]
        x, y, c = _place()
        cps = [pltpu.make_async_remote_copy(src_ref=src[j], dst_ref=dst[j], send_sem=send_sems.at[j], recv_sem=recv_sems.at[j],
                                            device_id=(x, y, 1 - c), device_id_type=MESH) for j in range(k)]
        for cp in cps:
            cp.start()
        for cp in cps:
            cp.wait()

    return pl.pallas_call(
        body, in_specs=[_ANY] * k, out_specs=[_ANY] * k,
        out_shape=[jax.ShapeDtypeStruct(p.shape, p.dtype) for p in parts],
        scratch_shapes=[pltpu.SemaphoreType.DMA((k,)), pltpu.SemaphoreType.DMA((k,))], name=name)(*parts)


N_PEERS = 7


def _grad_exchange_body(handshake):
    def body(pin_ref, pout_ref, lin_ref, lout_ref, send_sems, recv_sems):
        x, y, c = _place()
        if handshake:
            barrier = pltpu.get_barrier_semaphore()
            for k in range(1, N_PEERS + 1):
                peer = (x ^ ((k >> 2) & 1), y ^ ((k >> 1) & 1), c ^ (k & 1))
                pl.semaphore_signal(barrier, inc=1, device_id=peer, device_id_type=MESH)
            pl.semaphore_wait(barrier, N_PEERS)
        me = 2 * x + y
        pairs = ((pin_ref, lin_ref), (pout_ref, lout_ref))
        cps = []
        for k, (px, py) in enumerate(_other_chips(x, y)):
            for r in range(2):
                for j, (src, dst) in enumerate(pairs):
                    cps.append(pltpu.make_async_remote_copy(
                        src_ref=src.at[2 * px + py, r], dst_ref=dst.at[2 * k + c], send_sem=send_sems.at[2 * (2 * k + r) + j],
                        recv_sem=recv_sems.at[2 * (2 * k + c) + j], device_id=(px, py, r), device_id_type=MESH))
        for j, (src, dst) in enumerate(pairs):
            cps.append(pltpu.make_async_remote_copy(
                src_ref=src.at[me, 1 - c], dst_ref=dst.at[N_PEERS - 1], send_sem=send_sems.at[2 * (N_PEERS - 1) + j],
                recv_sem=recv_sems.at[2 * (N_PEERS - 1) + j], device_id=(x, y, 1 - c), device_id_type=MESH))
        for cp in cps:
            cp.start()
        for s in range(N_PEERS):
            for j, (src, dst) in enumerate(pairs):
                pltpu.make_async_remote_copy(
                    src_ref=src.at[0, 0], dst_ref=dst.at[s], send_sem=send_sems.at[2 * s + j], recv_sem=recv_sems.at[2 * s + j],
                    device_id=(x, y, 1 - c), device_id_type=MESH).wait_recv()
        for cp in cps:
            cp.wait_send()

    return body


_EXCHANGE_SEMS = [pltpu.SemaphoreType.DMA((2 * N_PEERS,))] * 2


def _landing(p):
    return jax.ShapeDtypeStruct((N_PEERS,) + p.shape[2:], p.dtype)


def _grad_exchange(pin, pout, name):
    return pl.pallas_call(
        _grad_exchange_body(False), in_specs=[_ANY, _ANY], out_specs=[_ANY, _ANY],
        out_shape=[_landing(pin), _landing(pout)], scratch_shapes=_EXCHANGE_SEMS, name=name)(pin, pout)


def _grad_exchange_beside(pin, pout, name):
    hbm = pltpu.MemorySpace.HBM
    pin_ref, pout_ref = jax.new_ref(pin, memory_space=hbm), jax.new_ref(pout, memory_space=hbm)
    lin_ref, lout_ref = jax.empty_ref(_landing(pin), memory_space=hbm), jax.empty_ref(_landing(pout), memory_space=hbm)
    body = _grad_exchange_body(True)

    @pl.kernel(mesh=plsc.ScalarSubcoreMesh(axis_name="sequencer", num_cores=1), name=name,
               scratch_types=_EXCHANGE_SEMS, compiler_params=pltpu.CompilerParams(collective_id=2))
    def launch(send_sems, recv_sems):
        body(pin_ref, pout_ref, lin_ref, lout_ref, send_sems, recv_sems)

    launch()
    return lin_ref[...], lout_ref[...]


def _add_n(parts, name, with_bf16=False):
    r, c = parts[0].shape
    tr = 256 if r % 256 == 0 else r
    n = len(parts)

    def body(*refs):
        acc = refs[0][...].astype(F32)
        for ref in refs[1:n]:
            acc = acc + ref[...].astype(F32)
        refs[n][...] = acc
        if with_bf16:
            refs[n + 1][...] = acc.astype(BF16)

    blk = pl.BlockSpec((tr, c), lambda i: (i, 0))
    outs = [jax.ShapeDtypeStruct((r, c), F32)] + ([jax.ShapeDtypeStruct((r, c), BF16)] if with_bf16 else [])
    res = pl.pallas_call(
        body, grid=(r // tr,), in_specs=[blk] * n, out_specs=[blk] * len(outs),
        out_shape=outs, compiler_params=_cparams(("parallel",)), name=name)(*parts)
    return res if with_bf16 else res[0]


def _all_reduce_small(packet):
    r, w = packet.shape

    def body(p_ref, o_ref, buf, send_sems, recv_sems):
        x, y, c = _place()
        me = 4 * x + 2 * y + c
        buf[me] = p_ref[...]
        peers = []
        for k in range(1, 8):
            fx, fy, fc = (k >> 2) & 1, (k >> 1) & 1, k & 1
            peers.append((x ^ fx, y ^ fy, c ^ fc))
        cps = [pltpu.make_async_remote_copy(src_ref=p_ref, dst_ref=buf.at[me], send_sem=send_sems.at[k], recv_sem=recv_sems.at[k],
                                            device_id=peer, device_id_type=MESH) for k, peer in enumerate(peers)]
        for cp in cps:
            cp.start()
        for k, (px, py, pc) in enumerate(peers):
            pltpu.make_async_remote_copy(src_ref=p_ref, dst_ref=buf.at[4 * px + 2 * py + pc], send_sem=send_sems.at[k],
                                         recv_sem=recv_sems.at[k], device_id=(px, py, pc), device_id_type=MESH).wait_recv()
        for cp in cps:
            cp.wait_send()
        acc = buf[0]
        for k in range(1, 8):
            acc = acc + buf[k]
        o_ref[...] = acc

    vm = pl.BlockSpec(memory_space=pltpu.VMEM)
    return pl.pallas_call(
        body, in_specs=[vm], out_specs=vm, out_shape=jax.ShapeDtypeStruct((r, w), F32),
        scratch_shapes=[pltpu.VMEM((8, r, w), F32), pltpu.SemaphoreType.DMA((7,)), pltpu.SemaphoreType.DMA((7,))],
        name="all_reduce_small")(packet)


def _adamw_math(w, g, m, v):
    m = ADAM_B1 * m + (1.0 - ADAM_B1) * g
    v = ADAM_B2 * v + (1.0 - ADAM_B2) * (g * g)
    m_hat = m / (1.0 - ADAM_B1 ** ADAM_STEP)
    v_hat = v / (1.0 - ADAM_B2 ** ADAM_STEP)
    return -ADAM_LR * (m_hat / (jnp.sqrt(v_hat) + ADAM_EPS) + ADAM_WD * w), m, v


def _adamw(w, g, m, v, name):
    nl, r, c = w.shape
    tr = 256 if r % 256 == 0 else r

    def body(w_ref, g_ref, m_ref, v_ref, d_ref, mo_ref, vo_ref):
        d_ref[...], mo_ref[...], vo_ref[...] = _adamw_math(w_ref[...], g_ref[...], m_ref[...], v_ref[...])

    blk = pl.BlockSpec((None, tr, c), lambda l, i: (l, i, 0))
    out = jax.ShapeDtypeStruct(w.shape, F32)
    return pl.pallas_call(
        body, grid=(nl, r // tr), in_specs=[blk] * 4, out_specs=[blk] * 3, out_shape=[out] * 3,
        compiler_params=_cparams(("parallel", "parallel")), name=name)(w, g, m, v)


def _small_update(gsum, lower_bounds, wpack, mpack, vpack):
    r, w = gsum.shape
    lb_rows = DEPTH * HGRN_W // 128

    def body(g_ref, a_ref, w_ref, m_ref, v_ref, go_ref, d_ref, mo_ref, vo_ref):
        a = a_ref[...]
        a0, a1 = a[0:1], a[1:2]
        mx = jnp.maximum(a0, a1)
        e0, e1 = jnp.exp(a0 - mx), jnp.exp(a1 - mx)
        p0, p1 = e0 / (e0 + e1), e1 / (e0 + e1)
        g = g_ref[...]
        half = lb_rows // 2
        dl0 = jnp.concatenate([g[k:k + 1] for k in range(half)], axis=1)
        dl1 = jnp.concatenate([g[half + k:half + k + 1] for k in range(half)], axis=1)
        dp0 = (dl0 + dl1) - (dl0 + dl1)
        dp1 = dl1
        inner = p0 * dp0 + p1 * dp1
        da0, da1 = p0 * (dp0 - inner), p1 * (dp1 - inner)
        rows = [da0[:, 128 * k:128 * (k + 1)] for k in range(half)] + [da1[:, 128 * k:128 * (k + 1)] for k in range(half)]
        gfull = jnp.concatenate(rows + [g[lb_rows:]], axis=0)
        go_ref[...] = gfull
        d_ref[...], mo_ref[...], vo_ref[...] = _adamw_math(w_ref[...], gfull, m_ref[...], v_ref[...])

    vm = pl.BlockSpec(memory_space=pltpu.VMEM)
    out = jax.ShapeDtypeStruct((r, w), F32)
    return pl.pallas_call(body, in_specs=[vm] * 5, out_specs=[vm] * 4, out_shape=[out] * 4, name="small_update")(
        gsum, lower_bounds, wpack, mpack, vpack)


_SMALL = ("lower_bounds", "pre_norm_g", "hgrn_norm_g", "fox_f_bias", "pool_w", "pool_scale", "post_norm_g")


def _pack(parts):
    rows = []
    for k in _SMALL:
        f = parts[k].reshape(-1)
        pad = (-f.shape[0]) % (8 * 128)
        rows.append(jnp.pad(f, (0, pad)).reshape(-1, 128))
    rows.append(jnp.zeros((8, 128), F32))
    return jnp.concatenate(rows, axis=0)


def _unpack(pack, like):
    out, r = {}, 0
    for k in _SMALL:
        size = int(np.prod(like[k].shape))
        nr = -(-size // (8 * 128)) * 8
        out[k] = pack[r:r + nr].reshape(-1)[:size].reshape(like[k].shape)
        r += nr
    return out, r


def kernel(x, lower_bounds, pre_norm_g, w_in, hgrn_norm_g, fox_f_bias, pool_w, pool_scale, w_out, post_norm_g, loss_target, m_lower_bounds, m_pre_norm_g, m_w_in, m_hgrn_norm_g, m_fox_f_bias, m_pool_w, m_pool_scale, m_w_out, m_post_norm_g, v_lower_bounds, v_pre_norm_g, v_w_in, v_hgrn_norm_g, v_fox_f_bias, v_pool_w, v_pool_scale, v_w_out, v_post_norm_g):
    cx, cy, cc = _place()
    chip = 2 * cx + cy

    halves = lambda w, l: w[l].reshape(2, w.shape[1] // 2, w.shape[2]).astype(BF16)
    first_layer = _gather_weights(halves(w_in, 0), halves(w_out, 0))
    first_layer, later = lax.optimization_barrier((first_layer, (halves(w_in, 1), halves(w_out, 1))))
    gathered = [first_layer, _gather_weights_beside(*later)]
    w_in_int = [_internal_from_shards([ain[q].reshape(D_MODEL, SHARD_W) for q in range(N_CHIPS)]) for ain, _ in gathered]
    w_out_full = [aout.reshape(D_MODEL, D_MODEL) for _, aout in gathered]

    def on_weight_grads(l, d_w_in, d_w_out):
        pin = _shards_from_internal(d_w_in).reshape(N_CHIPS, 2, D_MODEL // 2, SHARD_W)
        pout = d_w_out.reshape(N_CHIPS, 2, D_MODEL // (2 * N_CHIPS), D_MODEL)
        own = [lax.dynamic_index_in_dim(lax.dynamic_index_in_dim(p, chip, 0, False), cc, 0, False) for p in (pin, pout)]
        if l == 0:
            landed = _grad_exchange(pin.astype(BF16), pout.astype(BF16), f"grad_exchange{l}")
        else:
            landed = _grad_exchange_beside(pin.astype(BF16), pout.astype(BF16), f"grad_exchange{l}")
        return own, landed

    sq, grad_x, g, handed = _local_step(x, loss_target, lower_bounds, pre_norm_g, w_in_int, hgrn_norm_g, fox_f_bias,
                                        pool_w, pool_scale, w_out_full, post_norm_g, on_weight_grads)
    grad_x, handed = lax.optimization_barrier((grad_x, handed))
    first = cc == 0
    layers_in, layers_out = [], []
    for l in range(DEPTH):
        own, landed = handed[l]
        halves_l = [_add_n([o] + [t[s] for s in range(N_PEERS)], f"grad_sum{l}_{j}") for j, (o, t) in enumerate(zip(own, landed))]
        others = _swap_with_sibling(halves_l, f"grad_swap{l}")
        whole = [jnp.where(first, jnp.concatenate([h, o], axis=0), jnp.concatenate([o, h], axis=0))
                 for h, o in zip(halves_l, others)]
        layers_in.append(whole[0])
        layers_out.append(whole[1])
    grad_w_in, grad_w_out = jnp.stack(layers_in), jnp.stack(layers_out)

    small = {"lower_bounds": g["lbs"], "pre_norm_g": g["pre"], "hgrn_norm_g": g["hgn"], "fox_f_bias": g["bias"],
             "pool_w": g["pool_w"], "pool_scale": g["pool_scale"], "post_norm_g": g["post"]}
    packet = _pack(small)
    nrows = packet.shape[0]
    packet = packet.at[nrows - 1].set(sq[0])
    gsum = _all_reduce_small(packet)
    loss = gsum[nrows - 1, 0] * (0.5 / D_MODEL)

    weights = {"lower_bounds": lower_bounds, "pre_norm_g": pre_norm_g, "hgrn_norm_g": hgrn_norm_g,
               "fox_f_bias": fox_f_bias, "pool_w": pool_w, "pool_scale": pool_scale, "post_norm_g": post_norm_g}
    moments_m = {"lower_bounds": m_lower_bounds, "pre_norm_g": m_pre_norm_g, "hgrn_norm_g": m_hgrn_norm_g,
                 "fox_f_bias": m_fox_f_bias, "pool_w": m_pool_w, "pool_scale": m_pool_scale, "post_norm_g": m_post_norm_g}
    moments_v = {"lower_bounds": v_lower_bounds, "pre_norm_g": v_pre_norm_g, "hgrn_norm_g": v_hgrn_norm_g,
                 "fox_f_bias": v_fox_f_bias, "pool_w": v_pool_w, "pool_scale": v_pool_scale, "post_norm_g": v_post_norm_g}
    gp, dp, mp, vp = _small_update(gsum, lower_bounds, _pack(weights), _pack(moments_m), _pack(moments_v))
    gs, _ = _unpack(gp, weights)
    ds, _ = _unpack(dp, weights)
    ms, _ = _unpack(mp, weights)
    vs, _ = _unpack(vp, weights)

    d_in, m_in, v_in = _adamw(w_in, grad_w_in, m_w_in, v_w_in, "adamw_w_in")
    d_out, m_out, v_out = _adamw(w_out, grad_w_out, m_w_out, v_w_out, "adamw_w_out")

    def ordered(s, big_in, big_out):
        return (s["lower_bounds"], s["pre_norm_g"], big_in, s["hgrn_norm_g"], s["fox_f_bias"], s["pool_w"],
                s["pool_scale"], big_out, s["post_norm_g"])

    return (loss, grad_x, *ordered(gs, grad_w_in, grad_w_out), *ordered(ds, d_in, d_out),
            *ordered(ms, m_in, m_out), *ordered(vs, v_in, v_out))
```

```python
import functools

import numpy as np
import jax
import jax.numpy as jnp
from jax import lax
from jax.experimental import pallas as pl
from jax.experimental.pallas import tpu as pltpu
from jax.experimental.pallas import tpu_sc as plsc

F32 = jnp.float32
BF16 = jnp.bfloat16
HI = lax.Precision.HIGHEST
MESH = pl.DeviceIdType.MESH

NORM_EPS = 1e-6
MASK_VALUE = -1e30
TINY = 1e-30
ADAM_LR, ADAM_B1, ADAM_B2, ADAM_EPS, ADAM_WD, ADAM_STEP = 0.001, 0.9, 0.999, 1e-08, 0.01, 10

D_MODEL = 1024
DEPTH = 2
N_CHIPS = 4
CHUNK = 64
LANES = 128
HGRN_W, POOL_W, FOX_W, FOX_HEADS = 256, 256, 512, 8
POOL_WINDOWS = (2, 4, 8, 16)
POOL_HALO = 16
IN_WIDTH = 3592
SHARD_W = IN_WIDTH // N_CHIPS
A_W, B_W, C_W, F_W = 1024, 512, 2048, 128
E_INT = A_W + B_W + C_W + F_W
B_BLK = A_W // 512
C_BLK0 = (A_W + B_W) // 512
F_BLK = (A_W + B_W + C_W) // 128


def _segments():
    segs = []
    for hp in range(2):
        for part in range(4):
            segs.append((part * 256 + hp * 128, 128))
    segs.append((1024, 256))
    segs.append((1280, 256))
    for hp in range(4):
        for part in range(4):
            segs.append((1536 + part * 512 + hp * 128, 128))
    segs.append((3584, 8))
    return segs


_SEGS = _segments()


def _to_internal(w):
    parts = [w[..., s:s + n] for s, n in _SEGS]
    parts.append(jnp.zeros(w.shape[:-1] + (E_INT - IN_WIDTH,), w.dtype))
    return jnp.concatenate(parts, axis=-1)


def _to_original(w):
    offs, o = [], 0
    for s, n in _SEGS:
        offs.append((s, o, n))
        o += n
    parts = [w[..., o:o + n] for s, o, n in sorted(offs)]
    return jnp.concatenate(parts, axis=-1)


def _internal_from_shards(shards):
    parts = []
    for s, n in _SEGS:
        while n > 0:
            q, r = divmod(s, SHARD_W)
            take = min(n, SHARD_W - r)
            parts.append(shards[q][..., r:r + take])
            s, n = s + take, n - take
    parts.append(jnp.zeros(shards[0].shape[:-1] + (E_INT - IN_WIDTH,), shards[0].dtype))
    return jnp.concatenate(parts, axis=-1)


def _shards_from_internal(w):
    offs, o = [], 0
    for s, n in _SEGS:
        offs.append((s, o, n))
        o += n
    blocks = []
    for q in range(N_CHIPS):
        lo, hi = SHARD_W * q, SHARD_W * (q + 1)
        parts = [w[..., o + max(lo, s) - s:o + min(hi, s + n) - s] for s, o, n in sorted(offs) if s < hi and s + n > lo]
        blocks.append(jnp.concatenate(parts, axis=-1))
    return jnp.stack(blocks)


def _cparams(sem=None, vmem_mb=48):
    kw = dict(vmem_limit_bytes=vmem_mb * 1024 * 1024)
    if sem is not None:
        kw["dimension_semantics"] = sem
    return pltpu.CompilerParams(**kw)


def _sig(x):
    return 1.0 / (1.0 + jnp.exp(-x))


def _silu(x):
    return x * _sig(x)


def _dsilu(x):
    s = _sig(x)
    return s * (1.0 + x * (1.0 - s))


def _rstd(x):
    return lax.rsqrt(jnp.mean(x * x, axis=-1, keepdims=True) + NORM_EPS)


def _dot(a, b, dims, **kw):
    return lax.dot_general(a, b, (dims, ((), ())), preferred_element_type=F32, **kw)


NN = ((1,), (0,))
NT = ((1,), (1,))
TN = ((0,), (0,))


def _iota(shape, dim):
    return lax.broadcasted_iota(jnp.int32, shape, dim)


def _lbs_fwd(lower_bounds):
    def body(a_ref, o_ref):
        a = a_ref[...]
        a0, a1 = a[0:1], a[1:2]
        m = jnp.maximum(a0, a1)
        e0, e1 = jnp.exp(a0 - m), jnp.exp(a1 - m)
        p0, p1 = e0 / (e0 + e1), e1 / (e0 + e1)
        o_ref[...] = jnp.concatenate([p0 - p0, (p0 + p1) - p0], axis=0)

    return pl.pallas_call(body, out_shape=jax.ShapeDtypeStruct(lower_bounds.shape, F32), name="lbs_fwd")(lower_bounds)


def _inproj_fwd(x2, g_row, w_int, name):
    n, d = x2.shape
    e = w_int.shape[1]
    tm = min(256, n)

    def body(x_ref, g_ref, w_ref, o_ref):
        x = x_ref[...]
        h = (x * _rstd(x) * g_ref[...]).astype(BF16)
        o_ref[...] = jnp.dot(h, w_ref[...], preferred_element_type=F32)

    return pl.pallas_call(
        body, grid=(n // tm,),
        in_specs=[pl.BlockSpec((tm, d), lambda i: (i, 0)), pl.BlockSpec((1, d), lambda i: (0, 0)),
                  pl.BlockSpec((d, e), lambda i: (0, 0))],
        out_specs=pl.BlockSpec((tm, e), lambda i: (i, 0)),
        out_shape=jax.ShapeDtypeStruct((n, e), F32),
        compiler_params=_cparams(("parallel",)), name=name)(x2, g_row, w_int)


def _chunk_cumsum_matrix():
    i, j = _iota((LANES, LANES), 0), _iota((LANES, LANES), 1)
    return ((i <= j) & ((i // CHUNK) == (j // CHUNK))).astype(F32)


def _hgrn_gates(a, lb):
    qa, z = a[:, 0:128], a[:, 128:256]
    sg, sgn = _sig(z), _sig(-z)
    fg = lb + (1.0 - lb) * sg
    lf = jnp.log(jnp.maximum(fg, TINY))
    kk = (1.0 - lb) * sgn
    return qa * _sig(qa), kk, lf, sg, sgn, fg


def _hgrn_fwd(proj3, lbs_row, gn_col, name):
    bsz, t, _ = proj3.shape
    nt = t // LANES

    def body(a_ref, lb_ref, gn_ref, og_ref, or_ref):
        lb = lb_ref[...]
        gn = gn_ref[...]
        umat = _chunk_cumsum_matrix()
        lane64 = _iota((1, LANES), 1) % CHUNK

        def tile(i, carry):
            r0 = pl.multiple_of(i * LANES, LANES)
            a = a_ref[pl.ds(r0, LANES), :]
            qq, kk, lf, _, _, _ = _hgrn_gates(a, lb)
            va, ga = a[:, 256:384], a[:, 384:512]
            q_t, k_t, v_t = qq.T, kk.T, va.T
            b_t = jnp.dot(lf.T, umat, precision=HI, preferred_element_type=F32)
            new_s, o_heads = [], []
            for h in range(2):
                s_h = carry[h]
                rs = slice(CHUNK * h, CHUNK * (h + 1))
                qh, kh, vh, bh = q_t[rs], k_t[rs], v_t[rs], b_t[rs]
                inter = []
                for c in range(2):
                    cs = slice(CHUNK * c, CHUNK * (c + 1))
                    b_ = bh[:, cs]
                    qt = (qh[:, cs] * jnp.exp(b_)).astype(BF16)
                    inter.append(_dot(s_h.astype(BF16), qt, TN))
                    bl = b_[:, CHUNK - 1:CHUNK]
                    kt = (kh[:, cs] * jnp.exp(bl - b_)).astype(BF16)
                    s_h = jnp.exp(bl) * s_h + _dot(kt, vh[:, cs].astype(BF16), NT)
                new_s.append(s_h)

                acc = jnp.concatenate(inter, axis=1) + jnp.sum(qh * kh, axis=0, keepdims=True) * vh
                for dlt in range(1, CHUNK):
                    kr, br, vr = pltpu.roll(kh, dlt, 1), pltpu.roll(bh, dlt, 1), pltpu.roll(vh, dlt, 1)
                    e = jnp.exp(jnp.minimum(bh - br, 0.0))
                    att = jnp.sum(qh * kr * e, axis=0, keepdims=True)
                    acc = acc + jnp.where(lane64 >= dlt, att, 0.0) * vr
                o_heads.append(acc)
            normed = []
            for h in range(2):
                o_h = o_heads[h]
                ms = jnp.mean(o_h * o_h, axis=0, keepdims=True)
                normed.append(o_h * lax.rsqrt(ms + NORM_EPS) * gn[CHUNK * h:CHUNK * (h + 1)])
            or_ref[pl.ds(r0, LANES), :] = jnp.concatenate(o_heads, axis=0).T
            og_ref[pl.ds(r0, LANES), :] = jnp.concatenate(normed, axis=0).T * _silu(ga)
            return tuple(new_s)

        zero = jnp.zeros((CHUNK, CHUNK), F32)
        lax.fori_loop(0, nt, tile, (zero, zero))

    out = jax.ShapeDtypeStruct((bsz, t, HGRN_W), F32)
    return pl.pallas_call(
        body, grid=(bsz, 2),
        in_specs=[pl.BlockSpec((None, t, 512), lambda b, p: (b, 0, p)),
                  pl.BlockSpec((1, 128), lambda b, p: (0, p)),
                  pl.BlockSpec((128, 1), lambda b, p: (p, 0))],
        out_specs=[pl.BlockSpec((None, t, 128), lambda b, p: (b, 0, p)),
                   pl.BlockSpec((None, t, 128), lambda b, p: (b, 0, p))],
        out_shape=[out, out],
        compiler_params=_cparams(("parallel", "parallel")), name=name)(proj3, lbs_row, gn_col)


def _hgrn_bwd(proj3, o_raw, dmixed, lbs_row, gn_row, name):
    bsz, t, _ = proj3.shape
    nt = t // LANES
    nchunk = t // CHUNK

    def body(a_ref, or_ref, do_ref, lb_ref, gn_ref, da_ref, dgn_ref, dlb_ref, s_sc):
        lb = lb_ref[...]
        gn = gn_ref[...]
        umat = _chunk_cumsum_matrix()
        lane = _iota((1, LANES), 1)
        lane64 = lane % CHUNK
        half = lane < CHUNK

        def t_layout(a):
            qq, kk, lf, sg, sgn, fg = _hgrn_gates(a, lb)
            b_t = jnp.dot(lf.T, umat, precision=HI, preferred_element_type=F32)
            return qq.T, kk.T, a[:, 256:384].T, b_t, (sg, sgn, fg)

        def fwd_tile(i, carry):
            r0 = pl.multiple_of(i * LANES, LANES)
            q_t, k_t, v_t, b_t, _ = t_layout(a_ref[pl.ds(r0, LANES), :])
            new_s = []
            for h in range(2):
                s_h = carry[h]
                rs = slice(CHUNK * h, CHUNK * (h + 1))
                for c in range(2):
                    cs = slice(CHUNK * c, CHUNK * (c + 1))
                    s_sc[h, 2 * i + c] = s_h
                    b_ = b_t[rs, cs]
                    bl = b_[:, CHUNK - 1:CHUNK]
                    kt = (k_t[rs, cs] * jnp.exp(bl - b_)).astype(BF16)
                    s_h = jnp.exp(bl) * s_h + _dot(kt, v_t[rs, cs].astype(BF16), NT)
                new_s.append(s_h)
            return tuple(new_s)

        zero = jnp.zeros((CHUNK, CHUNK), F32)
        lax.fori_loop(0, nt, fwd_tile, (zero, zero))

        def half_mean(v):
            m0 = jnp.sum(jnp.where(half, v, 0.0), axis=1, keepdims=True) * (1.0 / CHUNK)
            m1 = jnp.sum(jnp.where(half, 0.0, v), axis=1, keepdims=True) * (1.0 / CHUNK)
            return jnp.where(half, m0, m1)

        def bwd_tile(k, carry):
            ds0, ds1, dgn_acc, dlb_acc = carry
            i = nt - 1 - k
            r0 = pl.multiple_of(i * LANES, LANES)
            a = a_ref[pl.ds(r0, LANES), :]
            qa, z, ga = a[:, 0:128], a[:, 128:256], a[:, 384:512]
            q_t, k_t, v_t, b_t, (sg, sgn, fg) = t_layout(a)
            oraw = or_ref[pl.ds(r0, LANES), :]
            dout = do_ref[pl.ds(r0, LANES), :]
            r = lax.rsqrt(half_mean(oraw * oraw) + NORM_EPS)
            xn = oraw * r
            dga = dout * (xn * gn) * _dsilu(ga)
            don = dout * _silu(ga)
            dgn_acc = dgn_acc + jnp.sum(don * xn, axis=0, keepdims=True)
            dxn = don * gn
            do_t = (r * (dxn - xn * half_mean(dxn * xn))).T
            new_ds, dq_h, dk_h, dv_h, db_h = [], [], [], [], []
            for h in range(2):
                ds_h = (ds0, ds1)[h]
                rs = slice(CHUNK * h, CHUNK * (h + 1))
                qh, kh, vh, bh, doh = q_t[rs], k_t[rs], v_t[rs], b_t[rs], do_t[rs]
                dq_c, dk_c, dv_c, dbl_c = [None, None], [None, None], [None, None], [None, None]
                for c in (1, 0):
                    cs = slice(CHUNK * c, CHUNK * (c + 1))
                    s_n = s_sc[h, 2 * i + c]
                    b_ = bh[:, cs]
                    eb = jnp.exp(b_)
                    bl = b_[:, CHUNK - 1:CHUNK]
                    ek = jnp.exp(bl - b_)
                    ebl = jnp.exp(bl)
                    qt, kt = qh[:, cs] * eb, kh[:, cs] * ek
                    do_c = doh[:, cs].astype(BF16)
                    dsb = ds_h.astype(BF16)
                    dv_c[c] = _dot(dsb, kt.astype(BF16), TN)
                    dkt = _dot(dsb, vh[:, cs].astype(BF16), NN)
                    dqt = _dot(s_n.astype(BF16), do_c, NN)
                    dbl_c[c] = jnp.sum(ds_h * s_n, axis=1, keepdims=True) * ebl + jnp.sum(dkt * kt, axis=1, keepdims=True)
                    dq_c[c], dk_c[c] = dqt * eb, dkt * ek
                    ds_h = ebl * ds_h + _dot(qt.astype(BF16), do_c, NT)
                new_ds.append(ds_h)

                att0 = jnp.sum(qh * kh, axis=0, keepdims=True)
                datt0 = jnp.sum(doh * vh, axis=0, keepdims=True)
                dqh = jnp.concatenate(dq_c, axis=1) + datt0 * kh
                dkh = jnp.concatenate(dk_c, axis=1) + datt0 * qh
                dvh = jnp.concatenate(dv_c, axis=1) + att0 * doh
                for dlt in range(1, CHUNK):
                    kr, br, vr = pltpu.roll(kh, dlt, 1), pltpu.roll(bh, dlt, 1), pltpu.roll(vh, dlt, 1)
                    e = jnp.where(lane64 >= dlt, jnp.exp(jnp.minimum(bh - br, 0.0)), 0.0)
                    qe = qh * e
                    att = jnp.sum(qe * kr, axis=0, keepdims=True)
                    datt = jnp.sum(doh * vr, axis=0, keepdims=True)
                    dqh = dqh + datt * (kr * e)
                    dkh = dkh + pltpu.roll(datt * qe, LANES - dlt, 1)
                    dvh = dvh + pltpu.roll(att * doh, LANES - dlt, 1)
                dbl = jnp.where(half, dbl_c[0], dbl_c[1])
                db_h.append(qh * dqh - kh * dkh + jnp.where(lane64 == CHUNK - 1, dbl, 0.0))
                dq_h.append(dqh)
                dk_h.append(dkh)
                dv_h.append(dvh)
            dqq = jnp.concatenate(dq_h, axis=0).T
            dkk = jnp.concatenate(dk_h, axis=0).T
            dvv = jnp.concatenate(dv_h, axis=0).T
            dlf = _dot(jnp.concatenate(db_h, axis=0), umat, NT, precision=HI).T
            dqa = dqq * _dsilu(qa)
            dfg = jnp.where(fg > TINY, dlf / fg, 0.0)
            dz = (dfg - dkk) * (1.0 - lb) * sg * sgn
            dlb_acc = dlb_acc + jnp.sum(dfg * (1.0 - sg) - dkk * sgn, axis=0, keepdims=True)
            da_ref[pl.ds(r0, LANES), :] = jnp.concatenate([dqa, dz, dvv, dga], axis=1)
            return new_ds[0], new_ds[1], dgn_acc, dlb_acc

        zrow = jnp.zeros((1, LANES), F32)
        _, _, dgn_acc, dlb_acc = lax.fori_loop(0, nt, bwd_tile, (zero, zero, zrow, zrow))
        dgn_ref[...] = jnp.broadcast_to(dgn_acc, (8, LANES))
        dlb_ref[...] = jnp.broadcast_to(dlb_acc, (8, LANES))

    rows = jax.ShapeDtypeStruct((bsz, 8, HGRN_W), F32)
    return pl.pallas_call(
        body, grid=(bsz, 2),
        in_specs=[pl.BlockSpec((None, t, 512), lambda b, p: (b, 0, p)),
                  pl.BlockSpec((None, t, 128), lambda b, p: (b, 0, p)),
                  pl.BlockSpec((None, t, 128), lambda b, p: (b, 0, p)),
                  pl.BlockSpec((1, 128), lambda b, p: (0, p)),
                  pl.BlockSpec((1, 128), lambda b, p: (0, p))],
        out_specs=[pl.BlockSpec((None, t, 512), lambda b, p: (b, 0, p)),
                   pl.BlockSpec((None, 8, 128), lambda b, p: (b, 0, p)),
                   pl.BlockSpec((None, 8, 128), lambda b, p: (b, 0, p))],
        out_shape=[jax.ShapeDtypeStruct((bsz, t, A_W), F32), rows, rows],
        scratch_shapes=[pltpu.VMEM((2, nchunk, CHUNK, CHUNK), F32)],
        compiler_params=_cparams(("parallel", "parallel")), name=name)(proj3, o_raw, dmixed, lbs_row, gn_row)


N_LEVELS = 6


def _hgrn_tables():
    t = np.arange(LANES)
    j = np.arange(LANES)[None, :]
    same_chunk = (t[:, None] // CHUNK) == (j // CHUNK)
    w = np.zeros((2 + N_LEVELS, LANES, LANES), np.float32)
    w[0] = same_chunk & (j <= t[:, None])
    w[1] = same_chunk & (j > t[:, None])
    maskf = np.zeros((N_LEVELS, LANES, LANES), np.float32)
    rightf = np.zeros((N_LEVELS, LANES, LANES), np.float32)
    for li in range(N_LEVELS):
        m = (CHUNK // 2) >> li
        start = t - (t % (2 * m))
        right = (t % (2 * m)) >= m
        first = np.where(right, start + m, t + 1)
        last = np.where(right, t, start + m - 1)
        w[2 + li] = (j >= first[:, None]) & (j <= last[:, None])
        maskf[li] = (t[:, None] // (2 * m)) == (j // (2 * m))
        rightf[li] = right[:, None]
    return jnp.asarray(w.reshape(-1, LANES), BF16), jnp.asarray(maskf), jnp.asarray(rightf)


def _split(x, n):
    parts = []
    for _ in range(n - 1):
        p = x.astype(BF16)
        parts.append(p)
        x = x - p.astype(F32)
    parts.append(x.astype(BF16))
    return parts


def _exact_dot(w, parts):
    acc = jnp.dot(w, parts[0], preferred_element_type=F32)
    for p in parts[1:]:
        acc = acc + jnp.dot(w, p, preferred_element_type=F32)
    return acc


def _head_sums(v, ones_blk, n=2):
    parts = _split(v, n)
    acc = jnp.dot(parts[0], ones_blk, preferred_element_type=F32)
    for p in parts[1:]:
        acc = acc + jnp.dot(p, ones_blk, preferred_element_type=F32)
    return acc


def _hgrn_consts():
    r, c = _iota((LANES, LANES), 0), _iota((LANES, LANES), 1)
    eye = r == c
    ones_blk = ((r // CHUNK) == (c // CHUNK)).astype(BF16)
    return eye, ones_blk, jnp.ones((CHUNK, LANES), BF16)


def _hgrn_levels(qq, kk, zall, mk_ref, rt_ref, d_att=None):
    att = [jnp.zeros((LANES, LANES), F32)] * 2
    dq = dk = db = jnp.zeros((LANES, LANES), F32)
    for li in range(N_LEVELS):
        e = jnp.exp(zall[(2 + li) * LANES:(3 + li) * LANES])
        rt = rt_ref[li]
        mk = mk_ref[li]
        qef, kef = e * rt, e * (1.0 - rt)
        qe, ke = (qq * qef).astype(BF16), (kk * kef).astype(BF16)
        dqs, dks = [], []
        for h in range(2):
            hs = slice(CHUNK * h, CHUNK * (h + 1))
            att[h] = att[h] + _dot(qe[:, hs], ke[:, hs], NT) * mk
            if d_att is not None:
                dam = (d_att[h] * mk).astype(BF16)
                dqs.append(jnp.dot(dam, ke[:, hs], preferred_element_type=F32))
                dks.append(_dot(dam, qe[:, hs], TN))
        if d_att is not None:
            dqe, dke = jnp.concatenate(dqs, axis=1), jnp.concatenate(dks, axis=1)
            dq = dq + dqe * qef
            dk = dk + dke * kef
            db = db + (dqe * qe.astype(F32) - dke * ke.astype(F32))
    return att, dq, dk, db


def _hgrn_fwd(proj3, lbs_row, gn_row, name):
    bsz, t, _ = proj3.shape
    nt = t // LANES
    w_all, maskf, rightf = _hgrn_tables()

    def body(a_ref, lb_ref, gn_ref, w_ref, mk_ref, rt_ref, og_ref, or_ref, st_ref):
        lb = lb_ref[...]
        gn = gn_ref[...]
        eye, ones_blk, ones_h = _hgrn_consts()

        def tile(i, carry):
            r0 = pl.multiple_of(i * LANES, LANES)
            a = a_ref[pl.ds(r0, LANES), :]
            qq, kk, lf, _, _, _ = _hgrn_gates(a, lb)
            va, ga = a[:, 256:384], a[:, 384:512]
            parts = _split(lf, 3)
            zall = _exact_dot(w_ref[...], parts)
            eb, ee = jnp.exp(zall[0:LANES]), jnp.exp(zall[LANES:2 * LANES])
            vb = va.astype(BF16)
            att, _, _, _ = _hgrn_levels(qq, kk, zall, mk_ref, rt_ref)
            qk = _split(qq * kk, 2)
            qeb, keb = (qq * eb).astype(BF16), (kk * ee).astype(BF16)
            new_s, o_heads = [], []
            for h in range(2):
                hs = slice(CHUNK * h, CHUNK * (h + 1))
                diag = _exact_dot_r(qk, hs, ones_h)
                a_h = att[h] + jnp.where(eye, diag, 0.0)
                o_h = jnp.dot(a_h.astype(BF16), vb[:, hs], preferred_element_type=F32)
                st = carry[h]
                chunks = []
                for c in range(2):
                    rc = slice(CHUNK * c, CHUNK * (c + 1))
                    st_ref[h, 2 * i + c] = st
                    chunks.append(o_h[rc] + _dot(qeb[rc, hs], st.astype(BF16), NT))
                    ebl = eb[CHUNK * (c + 1) - 1:CHUNK * (c + 1), hs]
                    st = st * ebl + _dot(vb[rc, hs], keb[rc, hs], TN)
                new_s.append(st)
                o_heads.append(jnp.concatenate(chunks, axis=0))
            o = jnp.concatenate(o_heads, axis=1)
            ms = _head_sums(o * o, ones_blk) * (1.0 / CHUNK)
            or_ref[pl.ds(r0, LANES), :] = o
            og_ref[pl.ds(r0, LANES), :] = o * lax.rsqrt(ms + NORM_EPS) * gn * _silu(ga)
            return tuple(new_s)

        zero = jnp.zeros((CHUNK, CHUNK), F32)
        lax.fori_loop(0, nt, tile, (zero, zero))

    out = jax.ShapeDtypeStruct((bsz, t, HGRN_W), F32)
    row = pl.BlockSpec((1, 128), lambda b, p: (0, p))
    return pl.pallas_call(
        body, grid=(bsz, 2),
        in_specs=[pl.BlockSpec((None, t, 512), lambda b, p: (b, 0, p)), row, row,
                  pl.BlockSpec(w_all.shape, lambda b, p: (0, 0)),
                  pl.BlockSpec(maskf.shape, lambda b, p: (0, 0, 0)),
                  pl.BlockSpec(rightf.shape, lambda b, p: (0, 0, 0))],
        out_specs=[pl.BlockSpec((None, t, 128), lambda b, p: (b, 0, p)),
                   pl.BlockSpec((None, t, 128), lambda b, p: (b, 0, p)),
                   pl.BlockSpec((None, 2, t // CHUNK, CHUNK, CHUNK), lambda b, p: (b, p, 0, 0, 0))],
        out_shape=[out, out, jax.ShapeDtypeStruct((bsz, 4, t // CHUNK, CHUNK, CHUNK), F32)],
        compiler_params=_cparams(("parallel", "parallel")), name=name)(proj3, lbs_row, gn_row, w_all, maskf, rightf)


def _exact_dot_r(parts, hs, ones_h):
    acc = jnp.dot(parts[0][:, hs], ones_h, preferred_element_type=F32)
    for p in parts[1:]:
        acc = acc + jnp.dot(p[:, hs], ones_h, preferred_element_type=F32)
    return acc


def _hgrn_bwd(proj3, o_raw, dmixed, states, lbs_row, gn_row, name):
    bsz, t, _ = proj3.shape
    nt = t // LANES
    nchunk = t // CHUNK
    w_all, maskf, rightf = _hgrn_tables()

    def body(a_ref, or_ref, do_ref, s_sc, lb_ref, gn_ref, w_ref, mk_ref, rt_ref, da_ref, dgn_ref, dlb_ref):
        lb = lb_ref[...]
        gn = gn_ref[...]
        eye, ones_blk, ones_h = _hgrn_consts()
        r_i, c_i = _iota((LANES, LANES), 0), _iota((LANES, LANES), 1)
        suffix = ((c_i >= r_i) & ((r_i // CHUNK) == (c_i // CHUNK))).astype(BF16)
        row64 = _iota((LANES, CHUNK), 0)
        ones_t = jnp.ones((LANES, CHUNK), BF16)
        zero = jnp.zeros((CHUNK, CHUNK), F32)

        def bwd_tile(k, carry):
            dst0, dst1, dgn_acc, dlb_acc = carry
            i = nt - 1 - k
            r0 = pl.multiple_of(i * LANES, LANES)
            a = a_ref[pl.ds(r0, LANES), :]
            qa, ga = a[:, 0:128], a[:, 384:512]
            qq, kk, lf, sg, sgn, fg = _hgrn_gates(a, lb)
            parts = _split(lf, 3)
            zall = _exact_dot(w_ref[...], parts)
            eb, ee = jnp.exp(zall[0:LANES]), jnp.exp(zall[LANES:2 * LANES])
            vb = a[:, 256:384].astype(BF16)
            oraw = or_ref[pl.ds(r0, LANES), :]
            dout = do_ref[pl.ds(r0, LANES), :]
            r = lax.rsqrt(_head_sums(oraw * oraw, ones_blk) * (1.0 / CHUNK) + NORM_EPS)
            xn = oraw * r
            dga = dout * (xn * gn) * _dsilu(ga)
            don = dout * _silu(ga)
            dgn_acc = dgn_acc + jnp.sum(don * xn, axis=0, keepdims=True)
            dxn = don * gn
            do = r * (dxn - xn * (_head_sums(dxn * xn, ones_blk) * (1.0 / CHUNK)))
            dob = do.astype(BF16)
            d_att = [_dot(dob[:, CHUNK * h:CHUNK * (h + 1)], vb[:, CHUNK * h:CHUNK * (h + 1)], NT) for h in range(2)]
            att, dq, dk, db_lv = _hgrn_levels(qq, kk, zall, mk_ref, rt_ref, d_att)
            qk = _split(qq * kk, 2)
            qe_f, ke_f = qq * eb, kk * ee
            qeb, keb = qe_f.astype(BF16), ke_f.astype(BF16)
            new_ds, dq_h, dk_h, dv_h, dbl_h = [], [], [], [], []
            for h in range(2):
                hs = slice(CHUNK * h, CHUNK * (h + 1))
                a_h = att[h] + jnp.where(eye, _exact_dot_r(qk, hs, ones_h), 0.0)
                dv = _dot(a_h.astype(BF16), dob[:, hs], TN)
                ddiag = _exact_dot_r(_split(jnp.where(eye, d_att[h], 0.0), 2), slice(None), ones_t)
                dq_i = dq[:, hs] + ddiag * kk[:, hs]
                dk_i = dk[:, hs] + ddiag * qq[:, hs]
                dst = (dst0, dst1)[h]
                dq_c, dk_c, dv_c, dbl_c = [None, None], [None, None], [None, None], [None, None]
                for c in (1, 0):
                    rc = slice(CHUNK * c, CHUNK * (c + 1))
                    st_n = s_sc[h, 2 * i + c]
                    ebl = eb[CHUNK * (c + 1) - 1:CHUNK * (c + 1), hs]
                    dstb = dst.astype(BF16)
                    dv_c[c] = _dot(keb[rc, hs], dstb, NT)
                    dke = jnp.dot(vb[rc, hs], dstb, preferred_element_type=F32)
                    dqe = jnp.dot(dob[rc, hs], st_n.astype(BF16), preferred_element_type=F32)
                    dbl_c[c] = (jnp.sum(dst * st_n, axis=0, keepdims=True) * ebl
                                + jnp.sum(dke * ke_f[rc, hs], axis=0, keepdims=True))
                    dq_c[c], dk_c[c] = dqe * eb[rc, hs], dke * ee[rc, hs]
                    dst = dst * ebl + _dot(dob[rc, hs], qeb[rc, hs], TN)
                new_ds.append(dst)
                dq_x, dk_x = jnp.concatenate(dq_c, axis=0), jnp.concatenate(dk_c, axis=0)
                dq_h.append(dq_i + dq_x)
                dk_h.append(dk_i + dk_x)
                dv_h.append(dv + jnp.concatenate(dv_c, axis=0))
                dbl_h.append(qq[:, hs] * dq_x - kk[:, hs] * dk_x
                             + jnp.where(row64 == CHUNK - 1, dbl_c[0], 0.0) + jnp.where(row64 == LANES - 1, dbl_c[1], 0.0))
            dqq = jnp.concatenate(dq_h, axis=1)
            dkk = jnp.concatenate(dk_h, axis=1)
            dvv = jnp.concatenate(dv_h, axis=1)
            db = db_lv + jnp.concatenate(dbl_h, axis=1)
            dlf = _exact_dot(suffix, _split(db, 3))
            dqa = dqq * _dsilu(qa)
            dfg = jnp.where(fg > TINY, dlf / fg, 0.0)
            dz = (dfg - dkk) * (1.0 - lb) * sg * sgn
            dlb_acc = dlb_acc + jnp.sum(dfg * (1.0 - sg) - dkk * sgn, axis=0, keepdims=True)
            da_ref[pl.ds(r0, LANES), :] = jnp.concatenate([dqa, dz, dvv, dga], axis=1)
            return new_ds[0], new_ds[1], dgn_acc, dlb_acc

        zrow = jnp.zeros((1, LANES), F32)
        _, _, dgn_acc, dlb_acc = lax.fori_loop(0, nt, bwd_tile, (zero, zero, zrow, zrow))
        dgn_ref[...] = jnp.broadcast_to(dgn_acc, (8, LANES))
        dlb_ref[...] = jnp.broadcast_to(dlb_acc, (8, LANES))

    rows = jax.ShapeDtypeStruct((bsz, 8, HGRN_W), F32)
    row = pl.BlockSpec((1, 128), lambda b, p: (0, p))
    blk = pl.BlockSpec((None, t, 128), lambda b, p: (b, 0, p))
    return pl.pallas_call(
        body, grid=(bsz, 2),
        in_specs=[pl.BlockSpec((None, t, 512), lambda b, p: (b, 0, p)), blk, blk,
                  pl.BlockSpec((None, 2, nchunk, CHUNK, CHUNK), lambda b, p: (b, p, 0, 0, 0)), row, row,
                  pl.BlockSpec(w_all.shape, lambda b, p: (0, 0)),
                  pl.BlockSpec(maskf.shape, lambda b, p: (0, 0, 0)),
                  pl.BlockSpec(rightf.shape, lambda b, p: (0, 0, 0))],
        out_specs=[pl.BlockSpec((None, t, 512), lambda b, p: (b, 0, p)),
                   pl.BlockSpec((None, 8, 128), lambda b, p: (b, 0, p)),
                   pl.BlockSpec((None, 8, 128), lambda b, p: (b, 0, p))],
        out_shape=[jax.ShapeDtypeStruct((bsz, t, A_W), F32), rows, rows],
        compiler_params=_cparams(("parallel", "parallel")), name=name)(
            proj3, o_raw, dmixed, states, lbs_row, gn_row, w_all, maskf, rightf)


def _pool_tt(t):
    return min(256, t)


def _window_select(s2, s4, s8, s16, lane):
    return jnp.where(lane < 64, s2, jnp.where(lane < 128, s4, jnp.where(lane < 192, s8, s16)))


def _pool_counts(t0, tt):
    lane = _iota((tt, POOL_W), 1)
    tpos = (_iota((tt, POOL_W), 0) + t0 + 1).astype(F32)
    win = jnp.where(lane < 64, 2.0, jnp.where(lane < 128, 4.0, jnp.where(lane < 192, 8.0, 16.0)))
    return 1.0 / jnp.minimum(tpos, win), lane


def _pooled_tile(upad_ref, i, tt):
    r0 = pl.multiple_of(i * tt, 8)
    cat = upad_ref[pl.ds(r0, tt + POOL_HALO), :]
    s2 = cat + pltpu.roll(cat, 1, 0)
    s4 = s2 + pltpu.roll(s2, 2, 0)
    s8 = s4 + pltpu.roll(s4, 4, 0)
    s16 = s8 + pltpu.roll(s8, 8, 0)
    inv, lane = _pool_counts(i * tt, tt)
    sel = _window_select(s2[POOL_HALO:], s4[POOL_HALO:], s8[POOL_HALO:], s16[POOL_HALO:], lane)
    return sel * inv - cat[POOL_HALO:], inv, lane


def _pool_fwd(proj3, wbd, scale_row, name):
    bsz, t, _ = proj3.shape
    tt = _pool_tt(t)

    def body(p_ref, w_ref, sc_ref, o_ref, upad):
        upad[0:POOL_HALO, :] = jnp.zeros((POOL_HALO, POOL_W), F32)
        upad[POOL_HALO:, :] = p_ref[:, 0:POOL_W]
        w = w_ref[...]
        sc = sc_ref[...]

        def tile(i, c):
            pooled, _, _ = _pooled_tile(upad, i, tt)
            r0 = pl.multiple_of(i * tt, 8)
            g = p_ref[pl.ds(r0, tt), POOL_W:2 * POOL_W]
            pre = jnp.dot(pooled.astype(BF16), w, preferred_element_type=F32)
            o_ref[pl.ds(r0, tt), :] = pre * sc * _silu(g)
            return c

        lax.fori_loop(0, t // tt, tile, 0)

    return pl.pallas_call(
        body, grid=(bsz,),
        in_specs=[pl.BlockSpec((None, t, 512), lambda b: (b, 0, B_BLK)),
                  pl.BlockSpec((POOL_W, POOL_W), lambda b: (0, 0)),
                  pl.BlockSpec((1, POOL_W), lambda b: (0, 0))],
        out_specs=pl.BlockSpec((None, t, POOL_W), lambda b: (b, 0, 0)),
        out_shape=jax.ShapeDtypeStruct((bsz, t, POOL_W), F32),
        scratch_shapes=[pltpu.VMEM((t + POOL_HALO, POOL_W), F32)],
        compiler_params=_cparams(("parallel",)), name=name)(proj3, wbd, scale_row)


def _pool_bwd(proj3, dmixed, wbd, scale_row, name):
    bsz, t, _ = proj3.shape
    tt = _pool_tt(t)

    def body(p_ref, do_ref, w_ref, sc_ref, db_ref, dsc_ref, dw_ref, upad, epad):
        upad[0:POOL_HALO, :] = jnp.zeros((POOL_HALO, POOL_W), F32)
        upad[POOL_HALO:, :] = p_ref[:, 0:POOL_W]
        epad[t:, :] = jnp.zeros((POOL_HALO, POOL_W), F32)
        w = w_ref[...]
        sc = sc_ref[...]

        def tile(i, carry):
            dsc_acc, dw_acc = carry
            pooled, inv, _ = _pooled_tile(upad, i, tt)
            r0 = pl.multiple_of(i * tt, 8)
            g = p_ref[pl.ds(r0, tt), POOL_W:2 * POOL_W]
            dout = do_ref[pl.ds(r0, tt), :]
            pb = pooled.astype(BF16)
            pre = jnp.dot(pb, w, preferred_element_type=F32)
            t1 = dout * _silu(g)
            dsc_acc = dsc_acc + jnp.sum(t1 * pre, axis=0, keepdims=True)
            dpre = (t1 * sc).astype(BF16)
            db_ref[pl.ds(r0, tt), POOL_W:2 * POOL_W] = dout * pre * sc * _dsilu(g)
            dw_acc = dw_acc + _dot(pb, dpre, TN)
            dpooled = _dot(dpre, w, NT)
            epad[pl.ds(r0, tt), :] = dpooled * inv
            return dsc_acc, dw_acc

        dsc_acc, dw_acc = lax.fori_loop(0, t // tt, tile, (jnp.zeros((1, POOL_W), F32), jnp.zeros((POOL_W, POOL_W), F32)))
        dsc_ref[...] = jnp.broadcast_to(dsc_acc, (8, POOL_W))
        dw_ref[...] = dw_acc

        def tile2(i, c):
            r0 = pl.multiple_of(i * tt, 8)
            n = tt + POOL_HALO
            cat = epad[pl.ds(r0, n), :]
            s2 = cat + pltpu.roll(cat, n - 1, 0)
            s4 = s2 + pltpu.roll(s2, n - 2, 0)
            s8 = s4 + pltpu.roll(s4, n - 4, 0)
            s16 = s8 + pltpu.roll(s8, n - 8, 0)
            inv, lane = _pool_counts(i * tt, tt)
            sel = _window_select(s2[:tt], s4[:tt], s8[:tt], s16[:tt], lane)
            db_ref[pl.ds(r0, tt), 0:POOL_W] = sel - cat[:tt] / inv
            return c

        lax.fori_loop(0, t // tt, tile2, 0)

    return pl.pallas_call(
        body, grid=(bsz,),
        in_specs=[pl.BlockSpec((None, t, 512), lambda b: (b, 0, B_BLK)),
                  pl.BlockSpec((None, t, POOL_W), lambda b: (b, 0, 1)),
                  pl.BlockSpec((POOL_W, POOL_W), lambda b: (0, 0)),
                  pl.BlockSpec((1, POOL_W), lambda b: (0, 0))],
        out_specs=[pl.BlockSpec((None, t, 512), lambda b: (b, 0, 0)),
                   pl.BlockSpec((None, 8, POOL_W), lambda b: (b, 0, 0)),
                   pl.BlockSpec((None, POOL_W, POOL_W), lambda b: (b, 0, 0))],
        out_shape=[jax.ShapeDtypeStruct((bsz, t, B_W), F32), jax.ShapeDtypeStruct((bsz, 8, POOL_W), F32),
                   jax.ShapeDtypeStruct((bsz, POOL_W, POOL_W), F32)],
        scratch_shapes=[pltpu.VMEM((t + POOL_HALO, POOL_W), F32), pltpu.VMEM((t + POOL_HALO, POOL_W), F32)],
        compiler_params=_cparams(("parallel",)), name=name)(proj3, dmixed, wbd, scale_row)


def _head_select_rows(hp):
    r, c = _iota((8, LANES), 0), _iota((8, LANES), 1)
    return ((r < 2) & (c == 2 * hp + r)).astype(F32)


def _foxgate_fwd(proj3, bias_row, name):
    bsz, t, _ = proj3.shape
    nt = t // LANES

    def body(f_ref, b_ref, cn_ref, ct_ref):
        bias = b_ref[...]
        i, j = _iota((LANES, LANES), 0), _iota((LANES, LANES), 1)
        lower = (j <= i).astype(F32)
        spread = (_iota((LANES, FOX_W), 0) == _iota((LANES, FOX_W), 1) // 64).astype(F32)

        def tile(k, carry):
            r0 = pl.multiple_of(k * LANES, LANES)
            xg = f_ref[pl.ds(r0, LANES), :] + bias
            lf = jnp.minimum(xg, 0.0) - jnp.log(1.0 + jnp.exp(-jnp.abs(xg)))
            c = jnp.dot(lower, lf, precision=HI, preferred_element_type=F32) + carry
            cn_ref[pl.ds(r0, LANES), :] = jnp.dot(c, spread, precision=HI, preferred_element_type=F32)
            for hp in range(4):
                ct_ref[hp, :, pl.ds(r0, LANES)] = _dot(_head_select_rows(hp), c, NT, precision=HI)
            return c[LANES - 1:LANES, :]

        lax.fori_loop(0, nt, tile, jnp.zeros((1, LANES), F32))

    return pl.pallas_call(
        body, grid=(bsz,),
        in_specs=[pl.BlockSpec((None, t, 128), lambda b: (b, 0, F_BLK)), pl.BlockSpec((1, 128), lambda b: (0, 0))],
        out_specs=[pl.BlockSpec((None, t, FOX_W), lambda b: (b, 0, 0)),
                   pl.BlockSpec((None, 4, 8, t), lambda b: (b, 0, 0, 0))],
        out_shape=[jax.ShapeDtypeStruct((bsz, t, FOX_W), F32), jax.ShapeDtypeStruct((bsz, 4, 8, t), F32)],
        compiler_params=_cparams(("parallel",)), name=name)(proj3, bias_row)


def _foxgate_bwd(proj3, dc_nat, bias_row, name):
    bsz, t, _ = proj3.shape
    nt = t // LANES

    def body(f_ref, dc_ref, b_ref, df_ref, dbias_ref, run_sc):
        bias = b_ref[...]
        i, j = _iota((LANES, LANES), 0), _iota((LANES, LANES), 1)
        upper = (j >= i).astype(F32)
        valid = _iota((1, LANES), 1) < FOX_HEADS
        run_sc[...] = jnp.zeros((8, LANES), F32)
        dbias_ref[...] = jnp.zeros((8, LANES), F32)

        def tile(k, c):
            r0 = pl.multiple_of((nt - 1 - k) * LANES, LANES)
            dc = dc_ref[pl.ds(r0, LANES), :] + jnp.where(i == LANES - 1, run_sc[0:1, :], 0.0)
            dlf = jnp.dot(upper, dc, precision=HI, preferred_element_type=F32)
            xg = f_ref[pl.ds(r0, LANES), :] + bias
            df = jnp.where(valid, dlf * _sig(-xg), 0.0)
            df_ref[pl.ds(r0, LANES), :] = df
            run_sc[...] = dlf[0:8, :]
            dbias_ref[...] += jnp.sum(df, axis=0, keepdims=True)
            return c

        lax.fori_loop(0, nt, tile, 0)

    blk = pl.BlockSpec((None, t, 128), lambda b: (b, 0, 0))
    return pl.pallas_call(
        body, grid=(bsz,),
        in_specs=[pl.BlockSpec((None, t, 128), lambda b: (b, 0, F_BLK)), blk, pl.BlockSpec((1, 128), lambda b: (0, 0))],
        out_specs=[blk, pl.BlockSpec((None, 8, 128), lambda b: (b, 0, 0))],
        out_shape=[jax.ShapeDtypeStruct((bsz, t, F_W), F32), jax.ShapeDtypeStruct((bsz, 8, 128), F32)],
        scratch_shapes=[pltpu.VMEM((8, LANES), F32)],
        compiler_params=_cparams(("parallel",)), name=name)(proj3, dc_nat, bias_row)


def _fox_tile(t):
    return min(256, t)


def _fox_fwd(proj3, c_nat, c_t, name):
    bsz, t, _ = proj3.shape
    tq = _fox_tile(t)
    tk = min(2 * tq, t)
    nq = t // tq

    def body(q_ref, kv_ref, cn_ref, ct_ref, og_ref, or_ref, lse_ref):
        i = pl.program_id(2)
        qblk = q_ref[...]
        first = _iota((1, 128), 1) < 64
        qv = qblk[:, 0:128] * 0.125
        qm = [jnp.where(first, qv, 0.0).astype(BF16), jnp.where(first, 0.0, qv).astype(BF16)]
        cqs = [cn_ref[:, 0:1], cn_ref[:, 64:65]]
        rows = _iota((tq, tk), 0) + i * tq

        def kv_step(j, carry, masked):
            c0 = pl.multiple_of(j * tk, tk)
            kb = kv_ref[pl.ds(c0, tk), 128:256].astype(BF16)
            vblk = kv_ref[pl.ds(c0, tk), 256:384]
            vx = [jnp.where(first, vblk, 1.0).astype(BF16), jnp.where(first, 1.0, vblk).astype(BF16)]
            new = []
            for h in range(2):
                m, acc = carry[2 * h], carry[2 * h + 1]
                s = _dot(qm[h], kb, NT) + (cqs[h] - ct_ref[h:h + 1, pl.ds(c0, tk)])
                if masked:
                    s = jnp.where(rows >= _iota((tq, tk), 1) + j * tk, s, MASK_VALUE)
                m_new = jnp.maximum(m, jnp.max(s, axis=1, keepdims=True))
                p = jnp.exp(s - m_new).astype(BF16)
                new += [m_new, jnp.exp(m - m_new) * acc + jnp.dot(p, vx[h], preferred_element_type=F32)]
            return tuple(new)

        init = (jnp.full((tq, 1), MASK_VALUE, F32), jnp.zeros((tq, 128), F32)) * 2
        n_full = (i * tq) // tk
        carry = lax.fori_loop(0, n_full, functools.partial(kv_step, masked=False), init)
        m0, acc0, m1, acc1 = kv_step(n_full, carry, True)
        l0, l1 = pltpu.roll(acc0, 64, 1), pltpu.roll(acc1, 64, 1)
        o = jnp.where(first, acc0 / l0, acc1 / l1)
        or_ref[...] = o
        og_ref[...] = o * _silu(qblk[:, 384:512])
        lse_ref[...] = jnp.where(first, m0 + jnp.log(l0), m1 + jnp.log(l1))

    out = jax.ShapeDtypeStruct((bsz, t, FOX_W), F32)
    blk = pl.BlockSpec((None, tq, 128), lambda b, p, i: (b, i, p))
    return pl.pallas_call(
        body, grid=(bsz, 4, nq),
        in_specs=[pl.BlockSpec((None, tq, 512), lambda b, p, i: (b, i, C_BLK0 + p)),
                  pl.BlockSpec((None, t, 512), lambda b, p, i: (b, 0, C_BLK0 + p)),
                  blk,
                  pl.BlockSpec((None, None, 8, t), lambda b, p, i: (b, p, 0, 0))],
        out_specs=[blk, blk, blk],
        out_shape=[out, out, out],
        compiler_params=_cparams(("parallel", "parallel", "arbitrary")), name=name)(proj3, proj3, c_nat, c_t)


def _fox_bwd(proj3, o_raw, dmixed, lse, c_nat, c_t, name):
    bsz, t, _ = proj3.shape
    tq = _fox_tile(t)
    nq = t // tq
    tk = min(2 * tq, t)
    ratio = tk // tq

    def body(a_ref, or_ref, do_ref, lse_ref, cn_ref, ct_ref, dc_out, dct_out, drow_out, dq_sc, do_sc, dl_sc):
        def prep(i, c):
            r0 = pl.multiple_of(i * tq, tq)
            g = a_ref[pl.ds(r0, tq), 384:512]
            dout = do_ref[pl.ds(r0, tq), :]
            o = or_ref[pl.ds(r0, tq), :]
            dc_out[pl.ds(r0, tq), 384:512] = dout * o * _dsilu(g)
            do = dout * _silu(g)
            do_sc[pl.ds(r0, tq), :] = do
            prod = do * o
            d0 = jnp.sum(prod[:, 0:64], axis=1, keepdims=True)
            d1 = jnp.sum(prod[:, 64:128], axis=1, keepdims=True)
            dl_sc[pl.ds(r0, tq), :] = jnp.concatenate([jnp.broadcast_to(d0, (tq, 64)), jnp.broadcast_to(d1, (tq, 64))], axis=1)
            dq_sc[pl.ds(r0, tq), :] = jnp.zeros((tq, 128), F32)
            drow_out[pl.ds(r0, tq), :] = jnp.zeros((tq, 128), F32)
            return c

        lax.fori_loop(0, nq, prep, 0)
        dct_out[...] = jnp.zeros((8, t), F32)

        first = _iota((1, 128), 1) < 64

        def heads(v):
            return [jnp.where(first, v, 0.0).astype(BF16), jnp.where(first, 0.0, v).astype(BF16)]

        def kv_tile(j, c):
            c0 = pl.multiple_of(j * tk, tk)
            kb = a_ref[pl.ds(c0, tk), 128:256].astype(BF16)
            vb = a_ref[pl.ds(c0, tk), 256:384].astype(BF16)
            cks = [ct_ref[h:h + 1, pl.ds(c0, tk)] for h in range(2)]

            def q_step(i, carry, diagonal):
                dk, dv, dcol0, dcol1 = carry
                r0 = pl.multiple_of(i * tq, tq)
                causal = _iota((tq, tk), 0) + i * tq >= _iota((tq, tk), 1) + j * tk
                qv = a_ref[pl.ds(r0, tq), 0:128] * 0.125
                do = do_sc[pl.ds(r0, tq), :]
                qb, dob = qv.astype(BF16), do.astype(BF16)
                qm, dom = heads(qv), heads(do)
                full, dcols, rsums = [], [], []
                for h in range(2):
                    lse_h = lse_ref[pl.ds(r0, tq), 64 * h:64 * h + 1]
                    dl_h = dl_sc[pl.ds(r0, tq), 64 * h:64 * h + 1]
                    cq = cn_ref[pl.ds(r0, tq), 64 * h:64 * h + 1]
                    p = jnp.exp(_dot(qm[h], kb, NT) + (cq - cks[h]) - lse_h)
                    if diagonal:
                        p = jnp.where(causal, p, 0.0)
                    ds = p * (_dot(dom[h], vb, NT) - dl_h)
                    dsb = ds.astype(BF16)
                    full.append((_dot(p.astype(BF16), dob, TN), _dot(dsb, qb, TN),
                                 jnp.dot(dsb, kb, preferred_element_type=F32)))
                    dcols.append(jnp.sum(ds, axis=0, keepdims=True))
                    rsums.append(jnp.broadcast_to(jnp.sum(ds, axis=1, keepdims=True), (tq, 128)))
                dq_sc[pl.ds(r0, tq), :] += jnp.where(first, full[0][2], full[1][2]) * 0.125
                drow_out[pl.ds(r0, tq), :] += jnp.where(first, rsums[0], rsums[1])
                return (dk + jnp.where(first, full[0][1], full[1][1]), dv + jnp.where(first, full[0][0], full[1][0]),
                        dcol0 - dcols[0], dcol1 - dcols[1])

            carry = (jnp.zeros((tk, 128), F32), jnp.zeros((tk, 128), F32), jnp.zeros((1, tk), F32), jnp.zeros((1, tk), F32))
            for r in range(ratio):
                carry = q_step(ratio * j + r, carry, True)
            dk, dv, dcol0, dcol1 = lax.fori_loop(ratio * (j + 1), nq, functools.partial(q_step, diagonal=False), carry)
            dct_out[0:1, pl.ds(c0, tk)] = dcol0
            dct_out[1:2, pl.ds(c0, tk)] = dcol1
            dc_out[pl.ds(c0, tk), 128:256] = dk
            dc_out[pl.ds(c0, tk), 256:384] = dv
            return c

        lax.fori_loop(0, t // tk, kv_tile, 0)
        dc_out[:, 0:128] = dq_sc[...]

    blk = pl.BlockSpec((None, t, 128), lambda b, p: (b, 0, p))
    return pl.pallas_call(
        body, grid=(bsz, 4),
        in_specs=[pl.BlockSpec((None, t, 512), lambda b, p: (b, 0, C_BLK0 + p)),
                  blk,
                  pl.BlockSpec((None, t, 128), lambda b, p: (b, 0, 4 + p)),
                  blk, blk,
                  pl.BlockSpec((None, None, 8, t), lambda b, p: (b, p, 0, 0))],
        out_specs=[pl.BlockSpec((None, t, 512), lambda b, p: (b, 0, p)),
                   pl.BlockSpec((None, None, 8, t), lambda b, p: (b, p, 0, 0)), blk],
        out_shape=[jax.ShapeDtypeStruct((bsz, t, C_W), F32), jax.ShapeDtypeStruct((bsz, 4, 8, t), F32),
                   jax.ShapeDtypeStruct((bsz, t, FOX_W), F32)],
        scratch_shapes=[pltpu.VMEM((t, 128), F32), pltpu.VMEM((t, 128), F32), pltpu.VMEM((t, 128), F32)],
        compiler_params=_cparams(("parallel", "parallel")), name=name)(proj3, o_raw, dmixed, lse, c_nat, c_t)


def _mix_tm(n):
    return min(512, n)


def _outproj_fwd(x2, oa, ob, oc, wo, g_row, name):
    n, d = x2.shape
    tm = _mix_tm(n)

    def body(x_ref, oa_ref, ob_ref, oc_ref, w_ref, g_ref, y_ref, xo_ref):
        y = (jnp.dot(oa_ref[...].astype(BF16), w_ref[0:256, :], preferred_element_type=F32)
             + jnp.dot(ob_ref[...].astype(BF16), w_ref[256:512, :], preferred_element_type=F32)
             + jnp.dot(oc_ref[...].astype(BF16), w_ref[512:1024, :], preferred_element_type=F32))
        y_ref[...] = y
        xo_ref[...] = x_ref[...] + y * _rstd(y) * g_ref[...]

    row = lambda w: pl.BlockSpec((tm, w), lambda i: (i, 0))
    out = jax.ShapeDtypeStruct((n, d), F32)
    return pl.pallas_call(
        body, grid=(n // tm,),
        in_specs=[row(d), row(256), row(256), row(512), pl.BlockSpec((d, d), lambda i: (0, 0)),
                  pl.BlockSpec((1, d), lambda i: (0, 0))],
        out_specs=[row(d), row(d)], out_shape=[out, out],
        compiler_params=_cparams(("parallel",)), name=name)(x2, oa, ob, oc, wo, g_row)


def _loss_head(x2, target2, name):
    n, d = x2.shape
    tm = _mix_tm(n)

    def body(x_ref, t_ref, dx_ref, l_ref):
        err = x_ref[...] - t_ref[...]
        dx_ref[...] = err * (1.0 / d)

        @pl.when(pl.program_id(0) == 0)
        def _():
            l_ref[...] = jnp.zeros((8, 128), F32)

        l_ref[...] += jnp.sum(err * err)

    row = pl.BlockSpec((tm, d), lambda i: (i, 0))
    return pl.pallas_call(
        body, grid=(n // tm,), in_specs=[row, row],
        out_specs=[row, pl.BlockSpec((8, 128), lambda i: (0, 0))],
        out_shape=[jax.ShapeDtypeStruct((n, d), F32), jax.ShapeDtypeStruct((8, 128), F32)],
        compiler_params=_cparams(("arbitrary",)), name=name)(x2, target2)


def _outproj_bwd(dxo, y, oa, ob, oc, wo, g_row, name):
    n, d = dxo.shape
    tm = _mix_tm(n)

    def body(dx_ref, y_ref, oa_ref, ob_ref, oc_ref, w_ref, g_ref, dm_ref, dw_ref, dg_ref):
        @pl.when(pl.program_id(0) == 0)
        def _():
            dw_ref[...] = jnp.zeros((d, d), F32)
            dg_ref[...] = jnp.zeros((8, d), F32)

        yv, dx = y_ref[...], dx_ref[...]
        r = _rstd(yv)
        yn = yv * r
        dg_ref[...] += jnp.sum(dx * yn, axis=0, keepdims=True)
        dyn = dx * g_ref[...]
        dy = (r * (dyn - yn * jnp.mean(dyn * yn, axis=-1, keepdims=True))).astype(BF16)
        dm_ref[...] = _dot(dy, w_ref[...], NT)
        dw_ref[0:256, :] += _dot(oa_ref[...].astype(BF16), dy, TN)
        dw_ref[256:512, :] += _dot(ob_ref[...].astype(BF16), dy, TN)
        dw_ref[512:1024, :] += _dot(oc_ref[...].astype(BF16), dy, TN)

    row = lambda w: pl.BlockSpec((tm, w), lambda i: (i, 0))
    fixed = lambda r, c: pl.BlockSpec((r, c), lambda i: (0, 0))
    return pl.pallas_call(
        body, grid=(n // tm,),
        in_specs=[row(d), row(d), row(256), row(256), row(512), fixed(d, d), fixed(1, d)],
        out_specs=[row(d), fixed(d, d), fixed(8, d)],
        out_shape=[jax.ShapeDtypeStruct((n, d), F32), jax.ShapeDtypeStruct((d, d), F32), jax.ShapeDtypeStruct((8, d), F32)],
        compiler_params=_cparams(("arbitrary",)), name=name)(dxo, y, oa, ob, oc, wo, g_row)


_PIECES = ((0, A_W), (A_W, B_W), (A_W + B_W, C_W), (A_W + B_W + C_W, F_W))


def _inproj_bwd_x(x2, dxo, g_row, w_int, pieces, name):
    n, d = x2.shape
    tm = min(256, n)

    def body(x_ref, dxo_ref, g_ref, w_ref, da_ref, db_ref, dc_ref, df_ref, dx_ref, dg_ref):
        @pl.when(pl.program_id(0) == 0)
        def _():
            dg_ref[...] = jnp.zeros((8, d), F32)

        dh = jnp.zeros((tm, d), F32)
        for ref, (o, w) in zip((da_ref, db_ref, dc_ref, df_ref), _PIECES):
            dh = dh + _dot(ref[...].astype(BF16), w_ref[:, o:o + w], NT)
        x = x_ref[...]
        r = _rstd(x)
        xn = x * r
        dg_ref[...] += jnp.sum(dh * xn, axis=0, keepdims=True)
        dxn = dh * g_ref[...]
        dx_ref[...] = dxo_ref[...] + r * (dxn - xn * jnp.mean(dxn * xn, axis=-1, keepdims=True))

    row = lambda w: pl.BlockSpec((tm, w), lambda i: (i, 0))
    fixed = lambda r, c: pl.BlockSpec((r, c), lambda i: (0, 0))
    return pl.pallas_call(
        body, grid=(n // tm,),
        in_specs=[row(d), row(d), fixed(1, d), fixed(d, E_INT)] + [row(w) for _, w in _PIECES],
        out_specs=[row(d), fixed(8, d)],
        out_shape=[jax.ShapeDtypeStruct((n, d), F32), jax.ShapeDtypeStruct((8, d), F32)],
        compiler_params=_cparams(("arbitrary",)), name=name)(x2, dxo, g_row, w_int, *pieces)


def _inproj_bwd_w(x2, g_row, pieces, name):
    n, d = x2.shape
    tm = min(256, n)

    def body(x_ref, g_ref, da_ref, db_ref, dc_ref, df_ref, dw_ref):
        @pl.when(pl.program_id(0) == 0)
        def _():
            dw_ref[...] = jnp.zeros((d, E_INT), F32)

        x = x_ref[...]
        h = (x * _rstd(x) * g_ref[...]).astype(BF16)
        for ref, (o, w) in zip((da_ref, db_ref, dc_ref, df_ref), _PIECES):
            dw_ref[:, o:o + w] += _dot(h, ref[...].astype(BF16), TN)

    row = lambda w: pl.BlockSpec((tm, w), lambda i: (i, 0))
    return pl.pallas_call(
        body, grid=(n // tm,),
        in_specs=[row(d), pl.BlockSpec((1, d), lambda i: (0, 0))] + [row(w) for _, w in _PIECES],
        out_specs=pl.BlockSpec((d, E_INT), lambda i: (0, 0)),
        out_shape=jax.ShapeDtypeStruct((d, E_INT), F32),
        compiler_params=_cparams(("arbitrary",), vmem_mb=56), name=name)(x2, g_row, *pieces)


def _block_diag(pool_w_l):
    z = jnp.zeros((64, 64), pool_w_l.dtype)
    return jnp.concatenate(
        [jnp.concatenate([pool_w_l[g] if c == g else z for c in range(4)], axis=1) for g in range(4)], axis=0)


def _pad_lanes(v, width=128):
    return jnp.pad(v, ((0, 0),) * (v.ndim - 1) + ((0, width - v.shape[-1]),))


def _local_step(x, target, lower_bounds, pre_norm_g, w_in_int, hgrn_norm_g, fox_f_bias, pool_w, pool_scale,
                w_out_bf, post_norm_g, on_weight_grads):
    bsz, t, d = x.shape
    n = bsz * t
    lbs = _lbs_fwd(lower_bounds)
    saved = []
    xc = x.reshape(n, d)
    for l in range(DEPTH):
        proj = _inproj_fwd(xc, pre_norm_g[l:l + 1], w_in_int[l], f"inproj_fwd{l}").reshape(bsz, t, E_INT)
        wbd = _block_diag(pool_w[l]).astype(BF16)
        bias_row = _pad_lanes(fox_f_bias[l:l + 1])
        oa, oa_raw, states = _hgrn_fwd(proj, lbs[l:l + 1], hgrn_norm_g[l:l + 1], f"hgrn_fwd{l}")
        ob = _pool_fwd(proj, wbd, pool_scale[l:l + 1], f"pool_fwd{l}")
        c_nat, c_t = _foxgate_fwd(proj, bias_row, f"foxgate_fwd{l}")
        oc, oc_raw, lse = _fox_fwd(proj, c_nat, c_t, f"fox_fwd{l}")
        y, xn = _outproj_fwd(xc, oa.reshape(n, -1), ob.reshape(n, -1), oc.reshape(n, -1), w_out_bf[l],
                             post_norm_g[l:l + 1], f"outproj_fwd{l}")
        saved.append((xc, proj, wbd, bias_row, oa, oa_raw, states, ob, oc, oc_raw, lse, c_nat, c_t, y))
        xc = xn
    dx, sq = _loss_head(xc, target.reshape(n, d), "loss_head")
    g = {k: [None] * DEPTH for k in ("pre", "hgn", "bias", "pool_w", "pool_scale", "post", "lbs")}
    handed = [None] * DEPTH
    for l in reversed(range(DEPTH)):
        xin, proj, wbd, bias_row, oa, oa_raw, states, ob, oc, oc_raw, lse, c_nat, c_t, y = saved[l]
        dmix, d_w_out, dpost = _outproj_bwd(dx, y, oa.reshape(n, -1), ob.reshape(n, -1), oc.reshape(n, -1),
                                            w_out_bf[l], post_norm_g[l:l + 1], f"outproj_bwd{l}")
        g["post"][l] = dpost[0]
        dmix3 = dmix.reshape(bsz, t, d)
        d_c, dct, drow = _fox_bwd(proj, oc_raw, dmix3, lse, c_nat, c_t, f"fox_bwd{l}")
        dc_nat = _pad_lanes(dct[:, :, 0:2, :].reshape(bsz, FOX_HEADS, t).transpose(0, 2, 1)
                            + drow.reshape(bsz, t, FOX_HEADS, 64)[..., 0])
        d_f, dbias = _foxgate_bwd(proj, dc_nat, bias_row, f"foxgate_bwd{l}")
        g["bias"][l] = jnp.sum(dbias[:, 0, :FOX_HEADS], axis=0)
        d_b, dscale, dwbd = _pool_bwd(proj, dmix3, wbd, pool_scale[l:l + 1], f"pool_bwd{l}")
        g["pool_scale"][l] = jnp.sum(dscale[:, 0], axis=0)
        dwbd = jnp.sum(dwbd, axis=0)
        g["pool_w"][l] = jnp.stack([dwbd[64 * k:64 * (k + 1), 64 * k:64 * (k + 1)] for k in range(4)])
        d_a, dgn, dlb = _hgrn_bwd(proj, oa_raw, dmix3, states, lbs[l:l + 1], hgrn_norm_g[l:l + 1], f"hgrn_bwd{l}")
        g["hgn"][l] = jnp.sum(dgn[:, 0], axis=0)
        g["lbs"][l] = jnp.sum(dlb[:, 0], axis=0)
        pieces = [p.reshape(n, -1) for p in (d_a, d_b, d_c, d_f)]
        handed[l] = on_weight_grads(l, _inproj_bwd_w(xin, pre_norm_g[l:l + 1], pieces, f"inproj_bwd_w{l}"), d_w_out)
        dx, dpre = _inproj_bwd_x(xin, dx, pre_norm_g[l:l + 1], w_in_int[l], pieces, f"inproj_bwd_x{l}")
        g["pre"][l] = dpre[0]
    grads = {k: jnp.stack(v) for k, v in g.items()}
    return sq, dx.reshape(bsz, t, d), grads, handed


def _place():
    return lax.axis_index("x"), lax.axis_index("y"), lax.axis_index("c")


def _other_chips(x, y):
    return [(1 - x, y), (x, 1 - y), (1 - x, 1 - y)]


_ANY = pl.BlockSpec(memory_space=pl.ANY)


def _gather_body(handshake, n_arrays):
    def body(*refs):
        srcs, dsts = refs[:n_arrays], refs[n_arrays:2 * n_arrays]
        ici_send, ici_recv, d2d_send, d2d_recv, local_sems = refs[2 * n_arrays:]
        x, y, c = _place()
        if handshake:
            barrier = pltpu.get_barrier_semaphore()
            for peer in [(px, py, c) for px, py in _other_chips(x, y)] + [(x, y, 1 - c)]:
                pl.semaphore_signal(barrier, inc=1, device_id=peer, device_id_type=MESH)
            pl.semaphore_wait(barrier, 4)
        me = 2 * x + y
        pairs = list(zip(srcs, dsts))
        order = [(k, j) for k in range(3) for j in range(n_arrays)]
        mine = [pltpu.make_async_copy(src, dst.at[me], local_sems.at[j]) for j, (src, dst) in enumerate(pairs)]
        for cp in mine:
            cp.start()
        chips = _other_chips(x, y)
        sends = [pltpu.make_async_remote_copy(
            src_ref=pairs[j][0].at[c], dst_ref=pairs[j][1].at[me, c], send_sem=ici_send.at[n], recv_sem=ici_recv.at[n],
            device_id=(chips[k][0], chips[k][1], c), device_id_type=MESH) for n, (k, j) in enumerate(order)]
        for cp in sends:
            cp.start()
        passed = [pltpu.make_async_remote_copy(
            src_ref=pairs[j][1].at[2 * chips[k][0] + chips[k][1], c], dst_ref=pairs[j][1].at[2 * chips[k][0] + chips[k][1], c],
            send_sem=d2d_send.at[n], recv_sem=d2d_recv.at[n], device_id=(x, y, 1 - c), device_id_type=MESH)
            for n, (k, j) in enumerate(order)]
        for n, (k, j) in enumerate(order):
            px, py = chips[k]
            src, dst = pairs[j]
            pltpu.make_async_remote_copy(
                src_ref=src.at[c], dst_ref=dst.at[2 * px + py, c], send_sem=ici_send.at[n], recv_sem=ici_recv.at[n],
                device_id=(px, py, c), device_id_type=MESH).wait_recv()
            passed[n].start()
        for n, (k, j) in enumerate(order):
            px, py = chips[k]
            src, dst = pairs[j]
            pltpu.make_async_remote_copy(
                src_ref=dst.at[2 * px + py, 1 - c], dst_ref=dst.at[2 * px + py, 1 - c], send_sem=d2d_send.at[n],
                recv_sem=d2d_recv.at[n], device_id=(x, y, 1 - c), device_id_type=MESH).wait_recv()
        for cp in sends + passed:
            cp.wait_send()
        for cp in mine:
            cp.wait()

    return body


def _gather_sems(n_arrays):
    return [pltpu.SemaphoreType.DMA((3 * n_arrays,))] * 4 + [pltpu.SemaphoreType.DMA((n_arrays,))]


def _gathered(a):
    return jax.ShapeDtypeStruct((N_CHIPS,) + a.shape, a.dtype)


def _gather_weights(arrays):
    n = len(arrays)
    return pl.pallas_call(
        _gather_body(False, n), in_specs=[_ANY] * n, out_specs=[_ANY] * n, out_shape=[_gathered(a) for a in arrays],
        scratch_shapes=_gather_sems(n), name="gather_weights")(*arrays)


def _gather_weights_beside(arrays):
    hbm = pltpu.MemorySpace.HBM
    n = len(arrays)
    srcs = [jax.new_ref(a, memory_space=hbm) for a in arrays]
    dsts = [jax.empty_ref(_gathered(a), memory_space=hbm) for a in arrays]
    body = _gather_body(True, n)

    @pl.kernel(mesh=plsc.ScalarSubcoreMesh(axis_name="sequencer", num_cores=1), name="gather_weights_beside",
               scratch_types=_gather_sems(n), compiler_params=pltpu.CompilerParams(collective_id=1))
    def launch(*sems):
        body(*srcs, *dsts, *sems)

    launch()
    return [d[...] for d in dsts]


def _swap_with_sibling(parts, name):
    k = len(parts)

    def body(*refs):
        src, dst = refs[:k], refs[k:2 * k]
        send_sems, recv_sems = refs[2 * k:]
        x, y, c = _place()
        cps = [pltpu.make_async_remote_copy(src_ref=src[j], dst_ref=dst[j], send_sem=send_sems.at[j], recv_sem=recv_sems.at[j],
                                            device_id=(x, y, 1 - c), device_id_type=MESH) for j in range(k)]
        for cp in cps:
            cp.start()
        for cp in cps:
            cp.wait()

    return pl.pallas_call(
        body, in_specs=[_ANY] * k, out_specs=[_ANY] * k,
        out_shape=[jax.ShapeDtypeStruct(p.shape, p.dtype) for p in parts],
        scratch_shapes=[pltpu.SemaphoreType.DMA((k,)), pltpu.SemaphoreType.DMA((k,))], name=name)(*parts)


N_PEERS = 7


def _grad_exchange_body():
    def body(pin_ref, pout_ref, lin_ref, lout_ref, send_sems, recv_sems):
        x, y, c = _place()
        barrier = pltpu.get_barrier_semaphore()
        for k in range(1, N_PEERS + 1):
            peer = (x ^ ((k >> 2) & 1), y ^ ((k >> 1) & 1), c ^ (k & 1))
            pl.semaphore_signal(barrier, inc=1, device_id=peer, device_id_type=MESH)
        pl.semaphore_wait(barrier, N_PEERS)
        me = 2 * x + y
        pairs = ((pin_ref, lin_ref), (pout_ref, lout_ref))
        cps = []
        for k, (px, py) in enumerate(_other_chips(x, y)):
            for r in range(2):
                for j, (src, dst) in enumerate(pairs):
                    cps.append(pltpu.make_async_remote_copy(
                        src_ref=src.at[2 * px + py, r], dst_ref=dst.at[2 * k + c], send_sem=send_sems.at[2 * (2 * k + r) + j],
                        recv_sem=recv_sems.at[2 * (2 * k + c) + j], device_id=(px, py, r), device_id_type=MESH))
        for j, (src, dst) in enumerate(pairs):
            cps.append(pltpu.make_async_remote_copy(
                src_ref=src.at[me, 1 - c], dst_ref=dst.at[N_PEERS - 1], send_sem=send_sems.at[2 * (N_PEERS - 1) + j],
                recv_sem=recv_sems.at[2 * (N_PEERS - 1) + j], device_id=(x, y, 1 - c), device_id_type=MESH))
        for cp in cps:
            cp.start()
        for s in range(N_PEERS):
            for j, (src, dst) in enumerate(pairs):
                pltpu.make_async_remote_copy(
                    src_ref=src.at[0, 0], dst_ref=dst.at[s], send_sem=send_sems.at[2 * s + j], recv_sem=recv_sems.at[2 * s + j],
                    device_id=(x, y, 1 - c), device_id_type=MESH).wait_recv()
        for cp in cps:
            cp.wait_send()

    return body


_EXCHANGE_SEMS = [pltpu.SemaphoreType.DMA((2 * N_PEERS,))] * 2


def _landing(p):
    return jax.ShapeDtypeStruct((N_PEERS,) + p.shape[2:], p.dtype)


def _grad_exchange_beside(pin, pout, name, collective_id):
    hbm = pltpu.MemorySpace.HBM
    pin_ref, pout_ref = jax.new_ref(pin, memory_space=hbm), jax.new_ref(pout, memory_space=hbm)
    lin_ref, lout_ref = jax.empty_ref(_landing(pin), memory_space=hbm), jax.empty_ref(_landing(pout), memory_space=hbm)
    body = _grad_exchange_body()

    @pl.kernel(mesh=plsc.ScalarSubcoreMesh(axis_name="sequencer", num_cores=1), name=name,
               scratch_types=_EXCHANGE_SEMS, compiler_params=pltpu.CompilerParams(collective_id=collective_id))
    def launch(send_sems, recv_sems):
        body(pin_ref, pout_ref, lin_ref, lout_ref, send_sems, recv_sems)

    launch()
    return lin_ref[...], lout_ref[...]


def _add_n(parts, name, with_bf16=False):
    r, c = parts[0].shape
    tr = 256 if r % 256 == 0 else r
    n = len(parts)

    def body(*refs):
        acc = refs[0][...].astype(F32)
        for ref in refs[1:n]:
            acc = acc + ref[...].astype(F32)
        refs[n][...] = acc
        if with_bf16:
            refs[n + 1][...] = acc.astype(BF16)

    blk = pl.BlockSpec((tr, c), lambda i: (i, 0))
    outs = [jax.ShapeDtypeStruct((r, c), F32)] + ([jax.ShapeDtypeStruct((r, c), BF16)] if with_bf16 else [])
    res = pl.pallas_call(
        body, grid=(r // tr,), in_specs=[blk] * n, out_specs=[blk] * len(outs),
        out_shape=outs, compiler_params=_cparams(("parallel",)), name=name)(*parts)
    return res if with_bf16 else res[0]


def _all_reduce_small(packet):
    r, w = packet.shape

    def body(p_ref, o_ref, buf, send_sems, recv_sems):
        x, y, c = _place()
        me = 4 * x + 2 * y + c
        buf[me] = p_ref[...]
        peers = []
        for k in range(1, 8):
            fx, fy, fc = (k >> 2) & 1, (k >> 1) & 1, k & 1
            peers.append((x ^ fx, y ^ fy, c ^ fc))
        cps = [pltpu.make_async_remote_copy(src_ref=p_ref, dst_ref=buf.at[me], send_sem=send_sems.at[k], recv_sem=recv_sems.at[k],
                                            device_id=peer, device_id_type=MESH) for k, peer in enumerate(peers)]
        for cp in cps:
            cp.start()
        for k, (px, py, pc) in enumerate(peers):
            pltpu.make_async_remote_copy(src_ref=p_ref, dst_ref=buf.at[4 * px + 2 * py + pc], send_sem=send_sems.at[k],
                                         recv_sem=recv_sems.at[k], device_id=(px, py, pc), device_id_type=MESH).wait_recv()
        for cp in cps:
            cp.wait_send()
        acc = buf[0]
        for k in range(1, 8):
            acc = acc + buf[k]
        o_ref[...] = acc

    vm = pl.BlockSpec(memory_space=pltpu.VMEM)
    return pl.pallas_call(
        body, in_specs=[vm], out_specs=vm, out_shape=jax.ShapeDtypeStruct((r, w), F32),
        scratch_shapes=[pltpu.VMEM((8, r, w), F32), pltpu.SemaphoreType.DMA((7,)), pltpu.SemaphoreType.DMA((7,))],
        name="all_reduce_small")(packet)


def _adamw_math(w, g, m, v):
    m = ADAM_B1 * m + (1.0 - ADAM_B1) * g
    v = ADAM_B2 * v + (1.0 - ADAM_B2) * (g * g)
    m_hat = m / (1.0 - ADAM_B1 ** ADAM_STEP)
    v_hat = v / (1.0 - ADAM_B2 ** ADAM_STEP)
    return -ADAM_LR * (m_hat / (jnp.sqrt(v_hat) + ADAM_EPS) + ADAM_WD * w), m, v


def _adamw(w, g, m, v, name):
    nl, r, c = w.shape
    tr = 256 if r % 256 == 0 else r

    def body(w_ref, g_ref, m_ref, v_ref, d_ref, mo_ref, vo_ref):
        d_ref[...], mo_ref[...], vo_ref[...] = _adamw_math(w_ref[...], g_ref[...], m_ref[...], v_ref[...])

    blk = pl.BlockSpec((None, tr, c), lambda l, i: (l, i, 0))
    out = jax.ShapeDtypeStruct(w.shape, F32)
    return pl.pallas_call(
        body, grid=(nl, r // tr), in_specs=[blk] * 4, out_specs=[blk] * 3, out_shape=[out] * 3,
        compiler_params=_cparams(("parallel", "parallel")), name=name)(w, g, m, v)


def _small_update(gsum, lower_bounds, wpack, mpack, vpack):
    r, w = gsum.shape
    lb_rows = DEPTH * HGRN_W // 128

    def body(g_ref, a_ref, w_ref, m_ref, v_ref, go_ref, d_ref, mo_ref, vo_ref):
        a = a_ref[...]
        a0, a1 = a[0:1], a[1:2]
        mx = jnp.maximum(a0, a1)
        e0, e1 = jnp.exp(a0 - mx), jnp.exp(a1 - mx)
        p0, p1 = e0 / (e0 + e1), e1 / (e0 + e1)
        g = g_ref[...]
        half = lb_rows // 2
        dl0 = jnp.concatenate([g[k:k + 1] for k in range(half)], axis=1)
        dl1 = jnp.concatenate([g[half + k:half + k + 1] for k in range(half)], axis=1)
        dp0 = (dl0 + dl1) - (dl0 + dl1)
        dp1 = dl1
        inner = p0 * dp0 + p1 * dp1
        da0, da1 = p0 * (dp0 - inner), p1 * (dp1 - inner)
        rows = [da0[:, 128 * k:128 * (k + 1)] for k in range(half)] + [da1[:, 128 * k:128 * (k + 1)] for k in range(half)]
        gfull = jnp.concatenate(rows + [g[lb_rows:]], axis=0)
        go_ref[...] = gfull
        d_ref[...], mo_ref[...], vo_ref[...] = _adamw_math(w_ref[...], gfull, m_ref[...], v_ref[...])

    vm = pl.BlockSpec(memory_space=pltpu.VMEM)
    out = jax.ShapeDtypeStruct((r, w), F32)
    return pl.pallas_call(body, in_specs=[vm] * 5, out_specs=[vm] * 4, out_shape=[out] * 4, name="small_update")(
        gsum, lower_bounds, wpack, mpack, vpack)


_SMALL = ("lower_bounds", "pre_norm_g", "hgrn_norm_g", "fox_f_bias", "pool_w", "pool_scale", "post_norm_g")


def _pack(parts):
    rows = []
    for k in _SMALL:
        f = parts[k].reshape(-1)
        pad = (-f.shape[0]) % (8 * 128)
        rows.append(jnp.pad(f, (0, pad)).reshape(-1, 128))
    rows.append(jnp.zeros((8, 128), F32))
    return jnp.concatenate(rows, axis=0)


def _unpack(pack, like):
    out, r = {}, 0
    for k in _SMALL:
        size = int(np.prod(like[k].shape))
        nr = -(-size // (8 * 128)) * 8
        out[k] = pack[r:r + nr].reshape(-1)[:size].reshape(like[k].shape)
        r += nr
    return out, r


def kernel(x, lower_bounds, pre_norm_g, w_in, hgrn_norm_g, fox_f_bias, pool_w, pool_scale, w_out, post_norm_g, loss_target, m_lower_bounds, m_pre_norm_g, m_w_in, m_hgrn_norm_g, m_fox_f_bias, m_pool_w, m_pool_scale, m_w_out, m_post_norm_g, v_lower_bounds, v_pre_norm_g, v_w_in, v_hgrn_norm_g, v_fox_f_bias, v_pool_w, v_pool_scale, v_w_out, v_post_norm_g):
    cx, cy, cc = _place()
    chip = 2 * cx + cy

    halves = lambda w, l: w[l].reshape(2, w.shape[1] // 2, w.shape[2]).astype(BF16)
    needed_first = _gather_weights([halves(w_in, 0)])
    needed_first, later = lax.optimization_barrier((needed_first, [halves(w_out, 0), halves(w_in, 1), halves(w_out, 1)]))
    later = _gather_weights_beside(later)
    w_in_int = [_internal_from_shards([a[q].reshape(D_MODEL, SHARD_W) for q in range(N_CHIPS)]) for a in (needed_first[0], later[1])]
    w_out_full = [a.reshape(D_MODEL, D_MODEL) for a in (later[0], later[2])]

    def on_weight_grads(l, d_w_in, d_w_out):
        pin = _shards_from_internal(d_w_in).reshape(N_CHIPS, 2, D_MODEL // 2, SHARD_W)
        pout = d_w_out.reshape(N_CHIPS, 2, D_MODEL // (2 * N_CHIPS), D_MODEL)
        own = [lax.dynamic_index_in_dim(lax.dynamic_index_in_dim(p, chip, 0, False), cc, 0, False) for p in (pin, pout)]
        return own, _grad_exchange_beside(pin.astype(BF16), pout.astype(BF16), f"grad_exchange{l}", 2 + l)

    sq, grad_x, g, handed = _local_step(x, loss_target, lower_bounds, pre_norm_g, w_in_int, hgrn_norm_g, fox_f_bias,
                                        pool_w, pool_scale, w_out_full, post_norm_g, on_weight_grads)
    first = cc == 0

    def finish(l, own, landed):
        halves_l = [_add_n([o] + [t[s] for s in range(N_PEERS)], f"grad_sum{l}_{j}") for j, (o, t) in enumerate(zip(own, landed))]
        others = _swap_with_sibling(halves_l, f"grad_swap{l}")
        g_in, g_out = [jnp.where(first, jnp.concatenate([h, o], axis=0), jnp.concatenate([o, h], axis=0))[None]
                       for h, o in zip(halves_l, others)]
        return (g_in, g_out, _adamw(w_in[l:l + 1], g_in, m_w_in[l:l + 1], v_w_in[l:l + 1], f"adamw_w_in{l}"),
                _adamw(w_out[l:l + 1], g_out, m_w_out[l:l + 1], v_w_out[l:l + 1], f"adamw_w_out{l}"))

    grad_x, last = lax.optimization_barrier((grad_x, handed[1]))
    done = [None, finish(1, *last)]

    small = {"lower_bounds": g["lbs"], "pre_norm_g": g["pre"], "hgrn_norm_g": g["hgn"], "fox_f_bias": g["bias"],
             "pool_w": g["pool_w"], "pool_scale": g["pool_scale"], "post_norm_g": g["post"]}
    packet = _pack(small)
    nrows = packet.shape[0]
    packet = packet.at[nrows - 1].set(sq[0])
    gsum = _all_reduce_small(packet)
    loss = gsum[nrows - 1, 0] * (0.5 / D_MODEL)

    weights = {"lower_bounds": lower_bounds, "pre_norm_g": pre_norm_g, "hgrn_norm_g": hgrn_norm_g,
               "fox_f_bias": fox_f_bias, "pool_w": pool_w, "pool_scale": pool_scale, "post_norm_g": post_norm_g}
    moments_m = {"lower_bounds": m_lower_bounds, "pre_norm_g": m_pre_norm_g, "hgrn_norm_g": m_hgrn_norm_g,
                 "fox_f_bias": m_fox_f_bias, "pool_w": m_pool_w, "pool_scale": m_pool_scale, "post_norm_g": m_post_norm_g}
    moments_v = {"lower_bounds": v_lower_bounds, "pre_norm_g": v_pre_norm_g, "hgrn_norm_g": v_hgrn_norm_g,
                 "fox_f_bias": v_fox_f_bias, "pool_w": v_pool_w, "pool_scale": v_pool_scale, "post_norm_g": v_post_norm_g}
    gp, dp, mp, vp = _small_update(gsum, lower_bounds, _pack(weights), _pack(moments_m), _pack(moments_v))
    gs, _ = _unpack(gp, weights)
    ds, _ = _unpack(dp, weights)
    ms, _ = _unpack(mp, weights)
    vs, _ = _unpack(vp, weights)

    first_layer, _ = lax.optimization_barrier((handed[0], (done[1], gp, dp, mp, vp)))
    done[0] = finish(0, *first_layer)
    both = lambda pick: jnp.concatenate([pick(done[l]) for l in range(DEPTH)], axis=0)
    grad_w_in, grad_w_out = both(lambda r: r[0]), both(lambda r: r[1])
    d_in, m_in, v_in = [both(lambda r, k=k: r[2][k]) for k in range(3)]
    d_out, m_out, v_out = [both(lambda r, k=k: r[3][k]) for k in range(3)]

    def ordered(s, big_in, big_out):
        return (s["lower_bounds"], s["pre_norm_g"], big_in, s["hgrn_norm_g"], s["fox_f_bias"], s["pool_w"],
                s["pool_scale"], big_out, s["post_norm_g"])

    return (loss, grad_x, *ordered(gs, grad_w_in, grad_w_out), *ordered(ds, d_in, d_out),
            *ordered(ms, m_in, m_out), *ordered(vs, v_in, v_out))
```

```python
import functools

import numpy as np
import jax
import jax.numpy as jnp
from jax import lax
from jax.experimental import pallas as pl
from jax.experimental.pallas import tpu as pltpu
from jax.experimental.pallas import tpu_sc as plsc

F32 = jnp.float32
BF16 = jnp.bfloat16
HI = lax.Precision.HIGHEST
MESH = pl.DeviceIdType.MESH

NORM_EPS = 1e-6
MASK_VALUE = -1e30
TINY = 1e-30
ADAM_LR, ADAM_B1, ADAM_B2, ADAM_EPS, ADAM_WD, ADAM_STEP = 0.001, 0.9, 0.999, 1e-08, 0.01, 10

D_MODEL = 1024
DEPTH = 2
N_CHIPS = 4
CHUNK = 64
LANES = 128
HGRN_W, POOL_W, FOX_W, FOX_HEADS = 256, 256, 512, 8
POOL_WINDOWS = (2, 4, 8, 16)
POOL_HALO = 16
IN_WIDTH = 3592
SHARD_W = IN_WIDTH // N_CHIPS
A_W, B_W, C_W, F_W = 1024, 512, 2048, 128
E_INT = A_W + B_W + C_W + F_W
B_BLK = A_W // 512
C_BLK0 = (A_W + B_W) // 512
F_BLK = (A_W + B_W + C_W) // 128


def _segments():
    segs = []
    for hp in range(2):
        for part in range(4):
            segs.append((part * 256 + hp * 128, 128))
    segs.append((1024, 256))
    segs.append((1280, 256))
    for hp in range(4):
        for part in range(4):
            segs.append((1536 + part * 512 + hp * 128, 128))
    segs.append((3584, 8))
    return segs


_SEGS = _segments()


def _to_internal(w):
    parts = [w[..., s:s + n] for s, n in _SEGS]
    parts.append(jnp.zeros(w.shape[:-1] + (E_INT - IN_WIDTH,), w.dtype))
    return jnp.concatenate(parts, axis=-1)


def _to_original(w):
    offs, o = [], 0
    for s, n in _SEGS:
        offs.append((s, o, n))
        o += n
    parts = [w[..., o:o + n] for s, o, n in sorted(offs)]
    return jnp.concatenate(parts, axis=-1)


def _internal_from_shards(shards):
    parts = []
    for s, n in _SEGS:
        while n > 0:
            q, r = divmod(s, SHARD_W)
            take = min(n, SHARD_W - r)
            parts.append(shards[q][..., r:r + take])
            s, n = s + take, n - take
    parts.append(jnp.zeros(shards[0].shape[:-1] + (E_INT - IN_WIDTH,), shards[0].dtype))
    return jnp.concatenate(parts, axis=-1)


def _shards_from_internal(w):
    offs, o = [], 0
    for s, n in _SEGS:
        offs.append((s, o, n))
        o += n
    blocks = []
    for q in range(N_CHIPS):
        lo, hi = SHARD_W * q, SHARD_W * (q + 1)
        parts = [w[..., o + max(lo, s) - s:o + min(hi, s + n) - s] for s, o, n in sorted(offs) if s < hi and s + n > lo]
        blocks.append(jnp.concatenate(parts, axis=-1))
    return jnp.stack(blocks)


def _cparams(sem=None, vmem_mb=48):
    kw = dict(vmem_limit_bytes=vmem_mb * 1024 * 1024)
    if sem is not None:
        kw["dimension_semantics"] = sem
    return pltpu.CompilerParams(**kw)


def _sig(x):
    return 1.0 / (1.0 + jnp.exp(-x))


def _silu(x):
    return x * _sig(x)


def _dsilu(x):
    s = _sig(x)
    return s * (1.0 + x * (1.0 - s))


def _rstd(x):
    return lax.rsqrt(jnp.mean(x * x, axis=-1, keepdims=True) + NORM_EPS)


def _dot(a, b, dims, **kw):
    return lax.dot_general(a, b, (dims, ((), ())), preferred_element_type=F32, **kw)


NN = ((1,), (0,))
NT = ((1,), (1,))
TN = ((0,), (0,))


def _iota(shape, dim):
    return lax.broadcasted_iota(jnp.int32, shape, dim)


def _lbs_fwd(lower_bounds):
    def body(a_ref, o_ref):
        a = a_ref[...]
        a0, a1 = a[0:1], a[1:2]
        m = jnp.maximum(a0, a1)
        e0, e1 = jnp.exp(a0 - m), jnp.exp(a1 - m)
        p0, p1 = e0 / (e0 + e1), e1 / (e0 + e1)
        o_ref[...] = jnp.concatenate([p0 - p0, (p0 + p1) - p0], axis=0)

    return pl.pallas_call(body, out_shape=jax.ShapeDtypeStruct(lower_bounds.shape, F32), name="lbs_fwd")(lower_bounds)


def _inproj_fwd(x2, g_row, w_int, name):
    n, d = x2.shape
    e = w_int.shape[1]
    tm = min(256, n)

    def body(x_ref, g_ref, w_ref, o_ref):
        x = x_ref[...]
        h = (x * _rstd(x) * g_ref[...]).astype(BF16)
        o_ref[...] = jnp.dot(h, w_ref[...], preferred_element_type=F32)

    return pl.pallas_call(
        body, grid=(n // tm,),
        in_specs=[pl.BlockSpec((tm, d), lambda i: (i, 0)), pl.BlockSpec((1, d), lambda i: (0, 0)),
                  pl.BlockSpec((d, e), lambda i: (0, 0))],
        out_specs=pl.BlockSpec((tm, e), lambda i: (i, 0)),
        out_shape=jax.ShapeDtypeStruct((n, e), F32),
        compiler_params=_cparams(("parallel",)), name=name)(x2, g_row, w_int)


def _chunk_cumsum_matrix():
    i, j = _iota((LANES, LANES), 0), _iota((LANES, LANES), 1)
    return ((i <= j) & ((i // CHUNK) == (j // CHUNK))).astype(F32)


def _hgrn_gates(a, lb):
    qa, z = a[:, 0:128], a[:, 128:256]
    sg, sgn = _sig(z), _sig(-z)
    fg = lb + (1.0 - lb) * sg
    lf = jnp.log(jnp.maximum(fg, TINY))
    kk = (1.0 - lb) * sgn
    return qa * _sig(qa), kk, lf, sg, sgn, fg


def _hgrn_fwd(proj3, lbs_row, gn_col, name):
    bsz, t, _ = proj3.shape
    nt = t // LANES

    def body(a_ref, lb_ref, gn_ref, og_ref, or_ref):
        lb = lb_ref[...]
        gn = gn_ref[...]
        umat = _chunk_cumsum_matrix()
        lane64 = _iota((1, LANES), 1) % CHUNK

        def tile(i, carry):
            r0 = pl.multiple_of(i * LANES, LANES)
            a = a_ref[pl.ds(r0, LANES), :]
            qq, kk, lf, _, _, _ = _hgrn_gates(a, lb)
            va, ga = a[:, 256:384], a[:, 384:512]
            q_t, k_t, v_t = qq.T, kk.T, va.T
            b_t = jnp.dot(lf.T, umat, precision=HI, preferred_element_type=F32)
            new_s, o_heads = [], []
            for h in range(2):
                s_h = carry[h]
                rs = slice(CHUNK * h, CHUNK * (h + 1))
                qh, kh, vh, bh = q_t[rs], k_t[rs], v_t[rs], b_t[rs]
                inter = []
                for c in range(2):
                    cs = slice(CHUNK * c, CHUNK * (c + 1))
                    b_ = bh[:, cs]
                    qt = (qh[:, cs] * jnp.exp(b_)).astype(BF16)
                    inter.append(_dot(s_h.astype(BF16), qt, TN))
                    bl = b_[:, CHUNK - 1:CHUNK]
                    kt = (kh[:, cs] * jnp.exp(bl - b_)).astype(BF16)
                    s_h = jnp.exp(bl) * s_h + _dot(kt, vh[:, cs].astype(BF16), NT)
                new_s.append(s_h)

                acc = jnp.concatenate(inter, axis=1) + jnp.sum(qh * kh, axis=0, keepdims=True) * vh
                for dlt in range(1, CHUNK):
                    kr, br, vr = pltpu.roll(kh, dlt, 1), pltpu.roll(bh, dlt, 1), pltpu.roll(vh, dlt, 1)
                    e = jnp.exp(jnp.minimum(bh - br, 0.0))
                    att = jnp.sum(qh * kr * e, axis=0, keepdims=True)
                    acc = acc + jnp.where(lane64 >= dlt, att, 0.0) * vr
                o_heads.append(acc)
            normed = []
            for h in range(2):
                o_h = o_heads[h]
                ms = jnp.mean(o_h * o_h, axis=0, keepdims=True)
                normed.append(o_h * lax.rsqrt(ms + NORM_EPS) * gn[CHUNK * h:CHUNK * (h + 1)])
            or_ref[pl.ds(r0, LANES), :] = jnp.concatenate(o_heads, axis=0).T
            og_ref[pl.ds(r0, LANES), :] = jnp.concatenate(normed, axis=0).T * _silu(ga)
            return tuple(new_s)

        zero = jnp.zeros((CHUNK, CHUNK), F32)
        lax.fori_loop(0, nt, tile, (zero, zero))

    out = jax.ShapeDtypeStruct((bsz, t, HGRN_W), F32)
    return pl.pallas_call(
        body, grid=(bsz, 2),
        in_specs=[pl.BlockSpec((None, t, 512), lambda b, p: (b, 0, p)),
                  pl.BlockSpec((1, 128), lambda b, p: (0, p)),
                  pl.BlockSpec((128, 1), lambda b, p: (p, 0))],
        out_specs=[pl.BlockSpec((None, t, 128), lambda b, p: (b, 0, p)),
                   pl.BlockSpec((None, t, 128), lambda b, p: (b, 0, p))],
        out_shape=[out, out],
        compiler_params=_cparams(("parallel", "parallel")), name=name)(proj3, lbs_row, gn_col)


def _hgrn_bwd(proj3, o_raw, dmixed, lbs_row, gn_row, name):
    bsz, t, _ = proj3.shape
    nt = t // LANES
    nchunk = t // CHUNK

    def body(a_ref, or_ref, do_ref, lb_ref, gn_ref, da_ref, dgn_ref, dlb_ref, s_sc):
        lb = lb_ref[...]
        gn = gn_ref[...]
        umat = _chunk_cumsum_matrix()
        lane = _iota((1, LANES), 1)
        lane64 = lane % CHUNK
        half = lane < CHUNK

        def t_layout(a):
            qq, kk, lf, sg, sgn, fg = _hgrn_gates(a, lb)
            b_t = jnp.dot(lf.T, umat, precision=HI, preferred_element_type=F32)
            return qq.T, kk.T, a[:, 256:384].T, b_t, (sg, sgn, fg)

        def fwd_tile(i, carry):
            r0 = pl.multiple_of(i * LANES, LANES)
            q_t, k_t, v_t, b_t, _ = t_layout(a_ref[pl.ds(r0, LANES), :])
            new_s = []
            for h in range(2):
                s_h = carry[h]
                rs = slice(CHUNK * h, CHUNK * (h + 1))
                for c in range(2):
                    cs = slice(CHUNK * c, CHUNK * (c + 1))
                    s_sc[h, 2 * i + c] = s_h
                    b_ = b_t[rs, cs]
                    bl = b_[:, CHUNK - 1:CHUNK]
                    kt = (k_t[rs, cs] * jnp.exp(bl - b_)).astype(BF16)
                    s_h = jnp.exp(bl) * s_h + _dot(kt, v_t[rs, cs].astype(BF16), NT)
                new_s.append(s_h)
            return tuple(new_s)

        zero = jnp.zeros((CHUNK, CHUNK), F32)
        lax.fori_loop(0, nt, fwd_tile, (zero, zero))

        def half_mean(v):
            m0 = jnp.sum(jnp.where(half, v, 0.0), axis=1, keepdims=True) * (1.0 / CHUNK)
            m1 = jnp.sum(jnp.where(half, 0.0, v), axis=1, keepdims=True) * (1.0 / CHUNK)
            return jnp.where(half, m0, m1)

        def bwd_tile(k, carry):
            ds0, ds1, dgn_acc, dlb_acc = carry
            i = nt - 1 - k
            r0 = pl.multiple_of(i * LANES, LANES)
            a = a_ref[pl.ds(r0, LANES), :]
            qa, z, ga = a[:, 0:128], a[:, 128:256], a[:, 384:512]
            q_t, k_t, v_t, b_t, (sg, sgn, fg) = t_layout(a)
            oraw = or_ref[pl.ds(r0, LANES), :]
            dout = do_ref[pl.ds(r0, LANES), :]
            r = lax.rsqrt(half_mean(oraw * oraw) + NORM_EPS)
            xn = oraw * r
            dga = dout * (xn * gn) * _dsilu(ga)
            don = dout * _silu(ga)
            dgn_acc = dgn_acc + jnp.sum(don * xn, axis=0, keepdims=True)
            dxn = don * gn
            do_t = (r * (dxn - xn * half_mean(dxn * xn))).T
            new_ds, dq_h, dk_h, dv_h, db_h = [], [], [], [], []
            for h in range(2):
                ds_h = (ds0, ds1)[h]
                rs = slice(CHUNK * h, CHUNK * (h + 1))
                qh, kh, vh, bh, doh = q_t[rs], k_t[rs], v_t[rs], b_t[rs], do_t[rs]
                dq_c, dk_c, dv_c, dbl_c = [None, None], [None, None], [None, None], [None, None]
                for c in (1, 0):
                    cs = slice(CHUNK * c, CHUNK * (c + 1))
                    s_n = s_sc[h, 2 * i + c]
                    b_ = bh[:, cs]
                    eb = jnp.exp(b_)
                    bl = b_[:, CHUNK - 1:CHUNK]
                    ek = jnp.exp(bl - b_)
                    ebl = jnp.exp(bl)
                    qt, kt = qh[:, cs] * eb, kh[:, cs] * ek
                    do_c = doh[:, cs].astype(BF16)
                    dsb = ds_h.astype(BF16)
                    dv_c[c] = _dot(dsb, kt.astype(BF16), TN)
                    dkt = _dot(dsb, vh[:, cs].astype(BF16), NN)
                    dqt = _dot(s_n.astype(BF16), do_c, NN)
                    dbl_c[c] = jnp.sum(ds_h * s_n, axis=1, keepdims=True) * ebl + jnp.sum(dkt * kt, axis=1, keepdims=True)
                    dq_c[c], dk_c[c] = dqt * eb, dkt * ek
                    ds_h = ebl * ds_h + _dot(qt.astype(BF16), do_c, NT)
                new_ds.append(ds_h)

                att0 = jnp.sum(qh * kh, axis=0, keepdims=True)
                datt0 = jnp.sum(doh * vh, axis=0, keepdims=True)
                dqh = jnp.concatenate(dq_c, axis=1) + datt0 * kh
                dkh = jnp.concatenate(dk_c, axis=1) + datt0 * qh
                dvh = jnp.concatenate(dv_c, axis=1) + att0 * doh
                for dlt in range(1, CHUNK):
                    kr, br, vr = pltpu.roll(kh, dlt, 1), pltpu.roll(bh, dlt, 1), pltpu.roll(vh, dlt, 1)
                    e = jnp.where(lane64 >= dlt, jnp.exp(jnp.minimum(bh - br, 0.0)), 0.0)
                    qe = qh * e
                    att = jnp.sum(qe * kr, axis=0, keepdims=True)
                    datt = jnp.sum(doh * vr, axis=0, keepdims=True)
                    dqh = dqh + datt * (kr * e)
                    dkh = dkh + pltpu.roll(datt * qe, LANES - dlt, 1)
                    dvh = dvh + pltpu.roll(att * doh, LANES - dlt, 1)
                dbl = jnp.where(half, dbl_c[0], dbl_c[1])
                db_h.append(qh * dqh - kh * dkh + jnp.where(lane64 == CHUNK - 1, dbl, 0.0))
                dq_h.append(dqh)
                dk_h.append(dkh)
                dv_h.append(dvh)
            dqq = jnp.concatenate(dq_h, axis=0).T
            dkk = jnp.concatenate(dk_h, axis=0).T
            dvv = jnp.concatenate(dv_h, axis=0).T
            dlf = _dot(jnp.concatenate(db_h, axis=0), umat, NT, precision=HI).T
            dqa = dqq * _dsilu(qa)
            dfg = jnp.where(fg > TINY, dlf / fg, 0.0)
            dz = (dfg - dkk) * (1.0 - lb) * sg * sgn
            dlb_acc = dlb_acc + jnp.sum(dfg * (1.0 - sg) - dkk * sgn, axis=0, keepdims=True)
            da_ref[pl.ds(r0, LANES), :] = jnp.concatenate([dqa, dz, dvv, dga], axis=1)
            return new_ds[0], new_ds[1], dgn_acc, dlb_acc

        zrow = jnp.zeros((1, LANES), F32)
        _, _, dgn_acc, dlb_acc = lax.fori_loop(0, nt, bwd_tile, (zero, zero, zrow, zrow))
        dgn_ref[...] = jnp.broadcast_to(dgn_acc, (8, LANES))
        dlb_ref[...] = jnp.broadcast_to(dlb_acc, (8, LANES))

    rows = jax.ShapeDtypeStruct((bsz, 8, HGRN_W), F32)
    return pl.pallas_call(
        body, grid=(bsz, 2),
        in_specs=[pl.BlockSpec((None, t, 512), lambda b, p: (b, 0, p)),
                  pl.BlockSpec((None, t, 128), lambda b, p: (b, 0, p)),
                  pl.BlockSpec((None, t, 128), lambda b, p: (b, 0, p)),
                  pl.BlockSpec((1, 128), lambda b, p: (0, p)),
                  pl.BlockSpec((1, 128), lambda b, p: (0, p))],
        out_specs=[pl.BlockSpec((None, t, 512), lambda b, p: (b, 0, p)),
                   pl.BlockSpec((None, 8, 128), lambda b, p: (b, 0, p)),
                   pl.BlockSpec((None, 8, 128), lambda b, p: (b, 0, p))],
        out_shape=[jax.ShapeDtypeStruct((bsz, t, A_W), F32), rows, rows],
        scratch_shapes=[pltpu.VMEM((2, nchunk, CHUNK, CHUNK), F32)],
        compiler_params=_cparams(("parallel", "parallel")), name=name)(proj3, o_raw, dmixed, lbs_row, gn_row)


N_LEVELS = 6


def _hgrn_tables():
    t = np.arange(LANES)
    j = np.arange(LANES)[None, :]
    same_chunk = (t[:, None] // CHUNK) == (j // CHUNK)
    w = np.zeros((2 + N_LEVELS, LANES, LANES), np.float32)
    w[0] = same_chunk & (j <= t[:, None])
    w[1] = same_chunk & (j > t[:, None])
    maskf = np.zeros((N_LEVELS, LANES, LANES), np.float32)
    rightf = np.zeros((N_LEVELS, LANES, LANES), np.float32)
    for li in range(N_LEVELS):
        m = (CHUNK // 2) >> li
        start = t - (t % (2 * m))
        right = (t % (2 * m)) >= m
        first = np.where(right, start + m, t + 1)
        last = np.where(right, t, start + m - 1)
        w[2 + li] = (j >= first[:, None]) & (j <= last[:, None])
        maskf[li] = (t[:, None] // (2 * m)) == (j // (2 * m))
        rightf[li] = right[:, None]
    return jnp.asarray(w.reshape(-1, LANES), BF16), jnp.asarray(maskf), jnp.asarray(rightf)


def _split(x, n):
    parts = []
    for _ in range(n - 1):
        p = x.astype(BF16)
        parts.append(p)
        x = x - p.astype(F32)
    parts.append(x.astype(BF16))
    return parts


def _exact_dot(w, parts):
    acc = jnp.dot(w, parts[0], preferred_element_type=F32)
    for p in parts[1:]:
        acc = acc + jnp.dot(w, p, preferred_element_type=F32)
    return acc


def _head_sums(v, ones_blk, n=2):
    parts = _split(v, n)
    acc = jnp.dot(parts[0], ones_blk, preferred_element_type=F32)
    for p in parts[1:]:
        acc = acc + jnp.dot(p, ones_blk, preferred_element_type=F32)
    return acc


def _hgrn_consts():
    r, c = _iota((LANES, LANES), 0), _iota((LANES, LANES), 1)
    eye = r == c
    ones_blk = ((r // CHUNK) == (c // CHUNK)).astype(BF16)
    return eye, ones_blk, jnp.ones((CHUNK, LANES), BF16)


def _hgrn_levels(qq, kk, zall, mk_ref, rt_ref, d_att=None):
    att = [jnp.zeros((LANES, LANES), F32)] * 2
    dq = dk = db = jnp.zeros((LANES, LANES), F32)
    for li in range(N_LEVELS):
        e = jnp.exp(zall[(2 + li) * LANES:(3 + li) * LANES])
        rt = rt_ref[li]
        mk = mk_ref[li]
        qef, kef = e * rt, e * (1.0 - rt)
        qe, ke = (qq * qef).astype(BF16), (kk * kef).astype(BF16)
        dqs, dks = [], []
        for h in range(2):
            hs = slice(CHUNK * h, CHUNK * (h + 1))
            att[h] = att[h] + _dot(qe[:, hs], ke[:, hs], NT) * mk
            if d_att is not None:
                dam = (d_att[h] * mk).astype(BF16)
                dqs.append(jnp.dot(dam, ke[:, hs], preferred_element_type=F32))
                dks.append(_dot(dam, qe[:, hs], TN))
        if d_att is not None:
            dqe, dke = jnp.concatenate(dqs, axis=1), jnp.concatenate(dks, axis=1)
            dq = dq + dqe * qef
            dk = dk + dke * kef
            db = db + (dqe * qe.astype(F32) - dke * ke.astype(F32))
    return att, dq, dk, db


def _hgrn_fwd(proj3, lbs_row, gn_row, name):
    bsz, t, _ = proj3.shape
    nt = t // LANES
    w_all, maskf, rightf = _hgrn_tables()

    def body(a_ref, lb_ref, gn_ref, w_ref, mk_ref, rt_ref, og_ref, or_ref, st_ref):
        lb = lb_ref[...]
        gn = gn_ref[...]
        eye, ones_blk, ones_h = _hgrn_consts()

        def tile(i, carry):
            r0 = pl.multiple_of(i * LANES, LANES)
            a = a_ref[pl.ds(r0, LANES), :]
            qq, kk, lf, _, _, _ = _hgrn_gates(a, lb)
            va, ga = a[:, 256:384], a[:, 384:512]
            parts = _split(lf, 3)
            zall = _exact_dot(w_ref[...], parts)
            eb, ee = jnp.exp(zall[0:LANES]), jnp.exp(zall[LANES:2 * LANES])
            vb = va.astype(BF16)
            att, _, _, _ = _hgrn_levels(qq, kk, zall, mk_ref, rt_ref)
            qk = _split(qq * kk, 2)
            qeb, keb = (qq * eb).astype(BF16), (kk * ee).astype(BF16)
            new_s, o_heads = [], []
            for h in range(2):
                hs = slice(CHUNK * h, CHUNK * (h + 1))
                diag = _exact_dot_r(qk, hs, ones_h)
                a_h = att[h] + jnp.where(eye, diag, 0.0)
                o_h = jnp.dot(a_h.astype(BF16), vb[:, hs], preferred_element_type=F32)
                st = carry[h]
                chunks = []
                for c in range(2):
                    rc = slice(CHUNK * c, CHUNK * (c + 1))
                    st_ref[h, 2 * i + c] = st
                    chunks.append(o_h[rc] + _dot(qeb[rc, hs], st.astype(BF16), NT))
                    ebl = eb[CHUNK * (c + 1) - 1:CHUNK * (c + 1), hs]
                    st = st * ebl + _dot(vb[rc, hs], keb[rc, hs], TN)
                new_s.append(st)
                o_heads.append(jnp.concatenate(chunks, axis=0))
            o = jnp.concatenate(o_heads, axis=1)
            ms = _head_sums(o * o, ones_blk) * (1.0 / CHUNK)
            or_ref[pl.ds(r0, LANES), :] = o
            og_ref[pl.ds(r0, LANES), :] = o * lax.rsqrt(ms + NORM_EPS) * gn * _silu(ga)
            return tuple(new_s)

        zero = jnp.zeros((CHUNK, CHUNK), F32)
        lax.fori_loop(0, nt // 2, lambda i, carry: tile(2 * i + 1, tile(2 * i, carry)), (zero, zero))

    out = jax.ShapeDtypeStruct((bsz, t, HGRN_W), F32)
    row = pl.BlockSpec((1, 128), lambda b, p: (0, p))
    return pl.pallas_call(
        body, grid=(bsz, 2),
        in_specs=[pl.BlockSpec((None, t, 512), lambda b, p: (b, 0, p)), row, row,
                  pl.BlockSpec(w_all.shape, lambda b, p: (0, 0)),
                  pl.BlockSpec(maskf.shape, lambda b, p: (0, 0, 0)),
                  pl.BlockSpec(rightf.shape, lambda b, p: (0, 0, 0))],
        out_specs=[pl.BlockSpec((None, t, 128), lambda b, p: (b, 0, p)),
                   pl.BlockSpec((None, t, 128), lambda b, p: (b, 0, p)),
                   pl.BlockSpec((None, 2, t // CHUNK, CHUNK, CHUNK), lambda b, p: (b, p, 0, 0, 0))],
        out_shape=[out, out, jax.ShapeDtypeStruct((bsz, 4, t // CHUNK, CHUNK, CHUNK), F32)],
        compiler_params=_cparams(("parallel", "parallel")), name=name)(proj3, lbs_row, gn_row, w_all, maskf, rightf)


def _exact_dot_r(parts, hs, ones_h):
    acc = jnp.dot(parts[0][:, hs], ones_h, preferred_element_type=F32)
    for p in parts[1:]:
        acc = acc + jnp.dot(p[:, hs], ones_h, preferred_element_type=F32)
    return acc


def _hgrn_bwd(proj3, o_raw, dmixed, states, lbs_row, gn_row, name):
    bsz, t, _ = proj3.shape
    nt = t // LANES
    nchunk = t // CHUNK
    w_all, maskf, rightf = _hgrn_tables()

    def body(a_ref, or_ref, do_ref, s_sc, lb_ref, gn_ref, w_ref, mk_ref, rt_ref, da_ref, dgn_ref, dlb_ref):
        lb = lb_ref[...]
        gn = gn_ref[...]
        eye, ones_blk, ones_h = _hgrn_consts()
        r_i, c_i = _iota((LANES, LANES), 0), _iota((LANES, LANES), 1)
        suffix = ((c_i >= r_i) & ((r_i // CHUNK) == (c_i // CHUNK))).astype(BF16)
        row64 = _iota((LANES, CHUNK), 0)
        ones_t = jnp.ones((LANES, CHUNK), BF16)
        zero = jnp.zeros((CHUNK, CHUNK), F32)

        def bwd_tile(k, carry):
            dst0, dst1, dgn_acc, dlb_acc = carry
            i = nt - 1 - k
            r0 = pl.multiple_of(i * LANES, LANES)
            a = a_ref[pl.ds(r0, LANES), :]
            qa, ga = a[:, 0:128], a[:, 384:512]
            qq, kk, lf, sg, sgn, fg = _hgrn_gates(a, lb)
            parts = _split(lf, 3)
            zall = _exact_dot(w_ref[...], parts)
            eb, ee = jnp.exp(zall[0:LANES]), jnp.exp(zall[LANES:2 * LANES])
            vb = a[:, 256:384].astype(BF16)
            oraw = or_ref[pl.ds(r0, LANES), :]
            dout = do_ref[pl.ds(r0, LANES), :]
            r = lax.rsqrt(_head_sums(oraw * oraw, ones_blk) * (1.0 / CHUNK) + NORM_EPS)
            xn = oraw * r
            dga = dout * (xn * gn) * _dsilu(ga)
            don = dout * _silu(ga)
            dgn_acc = dgn_acc + jnp.sum(don * xn, axis=0, keepdims=True)
            dxn = don * gn
            do = r * (dxn - xn * (_head_sums(dxn * xn, ones_blk) * (1.0 / CHUNK)))
            dob = do.astype(BF16)
            d_att = [_dot(dob[:, CHUNK * h:CHUNK * (h + 1)], vb[:, CHUNK * h:CHUNK * (h + 1)], NT) for h in range(2)]
            att, dq, dk, db_lv = _hgrn_levels(qq, kk, zall, mk_ref, rt_ref, d_att)
            qk = _split(qq * kk, 2)
            qe_f, ke_f = qq * eb, kk * ee
            qeb, keb = qe_f.astype(BF16), ke_f.astype(BF16)
            new_ds, dq_h, dk_h, dv_h, dbl_h = [], [], [], [], []
            for h in range(2):
                hs = slice(CHUNK * h, CHUNK * (h + 1))
                a_h = att[h] + jnp.where(eye, _exact_dot_r(qk, hs, ones_h), 0.0)
                dv = _dot(a_h.astype(BF16), dob[:, hs], TN)
                ddiag = _exact_dot_r(_split(jnp.where(eye, d_att[h], 0.0), 2), slice(None), ones_t)
                dq_i = dq[:, hs] + ddiag * kk[:, hs]
                dk_i = dk[:, hs] + ddiag * qq[:, hs]
                dst = (dst0, dst1)[h]
                dq_c, dk_c, dv_c, dbl_c = [None, None], [None, None], [None, None], [None, None]
                for c in (1, 0):
                    rc = slice(CHUNK * c, CHUNK * (c + 1))
                    st_n = s_sc[h, 2 * i + c]
                    ebl = eb[CHUNK * (c + 1) - 1:CHUNK * (c + 1), hs]
                    dstb = dst.astype(BF16)
                    dv_c[c] = _dot(keb[rc, hs], dstb, NT)
                    dke = jnp.dot(vb[rc, hs], dstb, preferred_element_type=F32)
                    dqe = jnp.dot(dob[rc, hs], st_n.astype(BF16), preferred_element_type=F32)
                    dbl_c[c] = (jnp.sum(dst * st_n, axis=0, keepdims=True) * ebl
                                + jnp.sum(dke * ke_f[rc, hs], axis=0, keepdims=True))
                    dq_c[c], dk_c[c] = dqe * eb[rc, hs], dke * ee[rc, hs]
                    dst = dst * ebl + _dot(dob[rc, hs], qeb[rc, hs], TN)
                new_ds.append(dst)
                dq_x, dk_x = jnp.concatenate(dq_c, axis=0), jnp.concatenate(dk_c, axis=0)
                dq_h.append(dq_i + dq_x)
                dk_h.append(dk_i + dk_x)
                dv_h.append(dv + jnp.concatenate(dv_c, axis=0))
                dbl_h.append(qq[:, hs] * dq_x - kk[:, hs] * dk_x
                             + jnp.where(row64 == CHUNK - 1, dbl_c[0], 0.0) + jnp.where(row64 == LANES - 1, dbl_c[1], 0.0))
            dqq = jnp.concatenate(dq_h, axis=1)
            dkk = jnp.concatenate(dk_h, axis=1)
            dvv = jnp.concatenate(dv_h, axis=1)
            db = db_lv + jnp.concatenate(dbl_h, axis=1)
            dlf = _exact_dot(suffix, _split(db, 3))
            dqa = dqq * _dsilu(qa)
            dfg = jnp.where(fg > TINY, dlf / fg, 0.0)
            dz = (dfg - dkk) * (1.0 - lb) * sg * sgn
            dlb_acc = dlb_acc + jnp.sum(dfg * (1.0 - sg) - dkk * sgn, axis=0, keepdims=True)
            da_ref[pl.ds(r0, LANES), :] = jnp.concatenate([dqa, dz, dvv, dga], axis=1)
            return new_ds[0], new_ds[1], dgn_acc, dlb_acc

        zrow = jnp.zeros((1, LANES), F32)
        _, _, dgn_acc, dlb_acc = lax.fori_loop(
            0, nt // 2, lambda k, carry: bwd_tile(2 * k + 1, bwd_tile(2 * k, carry)), (zero, zero, zrow, zrow))
        dgn_ref[...] = jnp.broadcast_to(dgn_acc, (8, LANES))
        dlb_ref[...] = jnp.broadcast_to(dlb_acc, (8, LANES))

    rows = jax.ShapeDtypeStruct((bsz, 8, HGRN_W), F32)
    row = pl.BlockSpec((1, 128), lambda b, p: (0, p))
    blk = pl.BlockSpec((None, t, 128), lambda b, p: (b, 0, p))
    return pl.pallas_call(
        body, grid=(bsz, 2),
        in_specs=[pl.BlockSpec((None, t, 512), lambda b, p: (b, 0, p)), blk, blk,
                  pl.BlockSpec((None, 2, nchunk, CHUNK, CHUNK), lambda b, p: (b, p, 0, 0, 0)), row, row,
                  pl.BlockSpec(w_all.shape, lambda b, p: (0, 0)),
                  pl.BlockSpec(maskf.shape, lambda b, p: (0, 0, 0)),
                  pl.BlockSpec(rightf.shape, lambda b, p: (0, 0, 0))],
        out_specs=[pl.BlockSpec((None, t, 512), lambda b, p: (b, 0, p)),
                   pl.BlockSpec((None, 8, 128), lambda b, p: (b, 0, p)),
                   pl.BlockSpec((None, 8, 128), lambda b, p: (b, 0, p))],
        out_shape=[jax.ShapeDtypeStruct((bsz, t, A_W), F32), rows, rows],
        compiler_params=_cparams(("parallel", "parallel")), name=name)(
            proj3, o_raw, dmixed, states, lbs_row, gn_row, w_all, maskf, rightf)


def _pool_tt(t):
    return min(256, t)


def _window_select(s2, s4, s8, s16, lane):
    return jnp.where(lane < 64, s2, jnp.where(lane < 128, s4, jnp.where(lane < 192, s8, s16)))


def _pool_counts(t0, tt):
    lane = _iota((tt, POOL_W), 1)
    tpos = (_iota((tt, POOL_W), 0) + t0 + 1).astype(F32)
    win = jnp.where(lane < 64, 2.0, jnp.where(lane < 128, 4.0, jnp.where(lane < 192, 8.0, 16.0)))
    return 1.0 / jnp.minimum(tpos, win), lane


def _pooled_tile(upad_ref, i, tt):
    r0 = pl.multiple_of(i * tt, 8)
    cat = upad_ref[pl.ds(r0, tt + POOL_HALO), :]
    s2 = cat + pltpu.roll(cat, 1, 0)
    s4 = s2 + pltpu.roll(s2, 2, 0)
    s8 = s4 + pltpu.roll(s4, 4, 0)
    s16 = s8 + pltpu.roll(s8, 8, 0)
    inv, lane = _pool_counts(i * tt, tt)
    sel = _window_select(s2[POOL_HALO:], s4[POOL_HALO:], s8[POOL_HALO:], s16[POOL_HALO:], lane)
    return sel * inv - cat[POOL_HALO:], inv, lane


def _pool_fwd(proj3, wbd, scale_row, name):
    bsz, t, _ = proj3.shape
    tt = _pool_tt(t)

    def body(p_ref, w_ref, sc_ref, o_ref, upad):
        upad[0:POOL_HALO, :] = jnp.zeros((POOL_HALO, POOL_W), F32)
        upad[POOL_HALO:, :] = p_ref[:, 0:POOL_W]
        w = w_ref[...]
        sc = sc_ref[...]

        def tile(i, c):
            pooled, _, _ = _pooled_tile(upad, i, tt)
            r0 = pl.multiple_of(i * tt, 8)
            g = p_ref[pl.ds(r0, tt), POOL_W:2 * POOL_W]
            pre = jnp.dot(pooled.astype(BF16), w, preferred_element_type=F32)
            o_ref[pl.ds(r0, tt), :] = pre * sc * _silu(g)
            return c

        lax.fori_loop(0, t // tt, tile, 0)

    return pl.pallas_call(
        body, grid=(bsz,),
        in_specs=[pl.BlockSpec((None, t, 512), lambda b: (b, 0, B_BLK)),
                  pl.BlockSpec((POOL_W, POOL_W), lambda b: (0, 0)),
                  pl.BlockSpec((1, POOL_W), lambda b: (0, 0))],
        out_specs=pl.BlockSpec((None, t, POOL_W), lambda b: (b, 0, 0)),
        out_shape=jax.ShapeDtypeStruct((bsz, t, POOL_W), F32),
        scratch_shapes=[pltpu.VMEM((t + POOL_HALO, POOL_W), F32)],
        compiler_params=_cparams(("parallel",)), name=name)(proj3, wbd, scale_row)


def _pool_bwd(proj3, dmixed, wbd, scale_row, name):
    bsz, t, _ = proj3.shape
    tt = _pool_tt(t)

    def body(p_ref, do_ref, w_ref, sc_ref, db_ref, dsc_ref, dw_ref, upad, epad):
        upad[0:POOL_HALO, :] = jnp.zeros((POOL_HALO, POOL_W), F32)
        upad[POOL_HALO:, :] = p_ref[:, 0:POOL_W]
        epad[t:, :] = jnp.zeros((POOL_HALO, POOL_W), F32)
        w = w_ref[...]
        sc = sc_ref[...]

        def tile(i, carry):
            dsc_acc, dw_acc = carry
            pooled, inv, _ = _pooled_tile(upad, i, tt)
            r0 = pl.multiple_of(i * tt, 8)
            g = p_ref[pl.ds(r0, tt), POOL_W:2 * POOL_W]
            dout = do_ref[pl.ds(r0, tt), :]
            pb = pooled.astype(BF16)
            pre = jnp.dot(pb, w, preferred_element_type=F32)
            t1 = dout * _silu(g)
            dsc_acc = dsc_acc + jnp.sum(t1 * pre, axis=0, keepdims=True)
            dpre = (t1 * sc).astype(BF16)
            db_ref[pl.ds(r0, tt), POOL_W:2 * POOL_W] = dout * pre * sc * _dsilu(g)
            dw_acc = dw_acc + _dot(pb, dpre, TN)
            dpooled = _dot(dpre, w, NT)
            epad[pl.ds(r0, tt), :] = dpooled * inv
            return dsc_acc, dw_acc

        dsc_acc, dw_acc = lax.fori_loop(0, t // tt, tile, (jnp.zeros((1, POOL_W), F32), jnp.zeros((POOL_W, POOL_W), F32)))
        dsc_ref[...] = jnp.broadcast_to(dsc_acc, (8, POOL_W))
        dw_ref[...] = dw_acc

        def tile2(i, c):
            r0 = pl.multiple_of(i * tt, 8)
            n = tt + POOL_HALO
            cat = epad[pl.ds(r0, n), :]
            s2 = cat + pltpu.roll(cat, n - 1, 0)
            s4 = s2 + pltpu.roll(s2, n - 2, 0)
            s8 = s4 + pltpu.roll(s4, n - 4, 0)
            s16 = s8 + pltpu.roll(s8, n - 8, 0)
            inv, lane = _pool_counts(i * tt, tt)
            sel = _window_select(s2[:tt], s4[:tt], s8[:tt], s16[:tt], lane)
            db_ref[pl.ds(r0, tt), 0:POOL_W] = sel - cat[:tt] / inv
            return c

        lax.fori_loop(0, t // tt, tile2, 0)

    return pl.pallas_call(
        body, grid=(bsz,),
        in_specs=[pl.BlockSpec((None, t, 512), lambda b: (b, 0, B_BLK)),
                  pl.BlockSpec((None, t, POOL_W), lambda b: (b, 0, 1)),
                  pl.BlockSpec((POOL_W, POOL_W), lambda b: (0, 0)),
                  pl.BlockSpec((1, POOL_W), lambda b: (0, 0))],
        out_specs=[pl.BlockSpec((None, t, 512), lambda b: (b, 0, 0)),
                   pl.BlockSpec((None, 8, POOL_W), lambda b: (b, 0, 0)),
                   pl.BlockSpec((None, POOL_W, POOL_W), lambda b: (b, 0, 0))],
        out_shape=[jax.ShapeDtypeStruct((bsz, t, B_W), F32), jax.ShapeDtypeStruct((bsz, 8, POOL_W), F32),
                   jax.ShapeDtypeStruct((bsz, POOL_W, POOL_W), F32)],
        scratch_shapes=[pltpu.VMEM((t + POOL_HALO, POOL_W), F32), pltpu.VMEM((t + POOL_HALO, POOL_W), F32)],
        compiler_params=_cparams(("parallel",)), name=name)(proj3, dmixed, wbd, scale_row)


def _head_select_rows(hp):
    r, c = _iota((8, LANES), 0), _iota((8, LANES), 1)
    return ((r < 2) & (c == 2 * hp + r)).astype(F32)


def _foxgate_fwd(proj3, bias_row, name):
    bsz, t, _ = proj3.shape
    nt = t // LANES

    def body(f_ref, b_ref, cn_ref, ct_ref):
        bias = b_ref[...]
        i, j = _iota((LANES, LANES), 0), _iota((LANES, LANES), 1)
        lower = (j <= i).astype(F32)
        spread = (_iota((LANES, FOX_W), 0) == _iota((LANES, FOX_W), 1) // 64).astype(F32)

        def tile(k, carry):
            r0 = pl.multiple_of(k * LANES, LANES)
            xg = f_ref[pl.ds(r0, LANES), :] + bias
            lf = jnp.minimum(xg, 0.0) - jnp.log(1.0 + jnp.exp(-jnp.abs(xg)))
            c = jnp.dot(lower, lf, precision=HI, preferred_element_type=F32) + carry
            cn_ref[pl.ds(r0, LANES), :] = jnp.dot(c, spread, precision=HI, preferred_element_type=F32)
            for hp in range(4):
                ct_ref[hp, :, pl.ds(r0, LANES)] = _dot(_head_select_rows(hp), c, NT, precision=HI)
            return c[LANES - 1:LANES, :]

        lax.fori_loop(0, nt, tile, jnp.zeros((1, LANES), F32))

    return pl.pallas_call(
        body, grid=(bsz,),
        in_specs=[pl.BlockSpec((None, t, 128), lambda b: (b, 0, F_BLK)), pl.BlockSpec((1, 128), lambda b: (0, 0))],
        out_specs=[pl.BlockSpec((None, t, FOX_W), lambda b: (b, 0, 0)),
                   pl.BlockSpec((None, 4, 8, t), lambda b: (b, 0, 0, 0))],
        out_shape=[jax.ShapeDtypeStruct((bsz, t, FOX_W), F32), jax.ShapeDtypeStruct((bsz, 4, 8, t), F32)],
        compiler_params=_cparams(("parallel",)), name=name)(proj3, bias_row)


def _foxgate_bwd(proj3, dc_nat, bias_row, name):
    bsz, t, _ = proj3.shape
    nt = t // LANES

    def body(f_ref, dc_ref, b_ref, df_ref, dbias_ref, run_sc):
        bias = b_ref[...]
        i, j = _iota((LANES, LANES), 0), _iota((LANES, LANES), 1)
        upper = (j >= i).astype(F32)
        valid = _iota((1, LANES), 1) < FOX_HEADS
        run_sc[...] = jnp.zeros((8, LANES), F32)
        dbias_ref[...] = jnp.zeros((8, LANES), F32)

        def tile(k, c):
            r0 = pl.multiple_of((nt - 1 - k) * LANES, LANES)
            dc = dc_ref[pl.ds(r0, LANES), :] + jnp.where(i == LANES - 1, run_sc[0:1, :], 0.0)
            dlf = jnp.dot(upper, dc, precision=HI, preferred_element_type=F32)
            xg = f_ref[pl.ds(r0, LANES), :] + bias
            df = jnp.where(valid, dlf * _sig(-xg), 0.0)
            df_ref[pl.ds(r0, LANES), :] = df
            run_sc[...] = dlf[0:8, :]
            dbias_ref[...] += jnp.sum(df, axis=0, keepdims=True)
            return c

        lax.fori_loop(0, nt, tile, 0)

    blk = pl.BlockSpec((None, t, 128), lambda b: (b, 0, 0))
    return pl.pallas_call(
        body, grid=(bsz,),
        in_specs=[pl.BlockSpec((None, t, 128), lambda b: (b, 0, F_BLK)), blk, pl.BlockSpec((1, 128), lambda b: (0, 0))],
        out_specs=[blk, pl.BlockSpec((None, 8, 128), lambda b: (b, 0, 0))],
        out_shape=[jax.ShapeDtypeStruct((bsz, t, F_W), F32), jax.ShapeDtypeStruct((bsz, 8, 128), F32)],
        scratch_shapes=[pltpu.VMEM((8, LANES), F32)],
        compiler_params=_cparams(("parallel",)), name=name)(proj3, dc_nat, bias_row)


def _fox_tile(t):
    return min(256, t)


def _fox_fwd(proj3, c_nat, c_t, name):
    bsz, t, _ = proj3.shape
    tq = _fox_tile(t)
    tk = min(2 * tq, t)
    nq = t // tq

    def body(q_ref, kv_ref, cn_ref, ct_ref, og_ref, or_ref, lse_ref):
        i = pl.program_id(2)
        qblk = q_ref[...]
        first = _iota((1, 128), 1) < 64
        qv = qblk[:, 0:128] * 0.125
        qm = [jnp.where(first, qv, 0.0).astype(BF16), jnp.where(first, 0.0, qv).astype(BF16)]
        cqs = [cn_ref[:, 0:1], cn_ref[:, 64:65]]
        rows = _iota((tq, tk), 0) + i * tq

        def scores(j):
            c0 = pl.multiple_of(j * tk, tk)
            kb = kv_ref[pl.ds(c0, tk), 128:256].astype(BF16)
            return tuple(_dot(qm[h], kb, NT) + (cqs[h] - ct_ref[h:h + 1, pl.ds(c0, tk)]) for h in range(2))

        def absorb(j, state, s01, masked):
            c0 = pl.multiple_of(j * tk, tk)
            vblk = kv_ref[pl.ds(c0, tk), 256:384]
            vx = [jnp.where(first, vblk, 1.0).astype(BF16), jnp.where(first, 1.0, vblk).astype(BF16)]
            new = []
            for h in range(2):
                m, acc, s = state[2 * h], state[2 * h + 1], s01[h]
                if masked:
                    s = jnp.where(rows >= _iota((tq, tk), 1) + j * tk, s, MASK_VALUE)
                m_new = jnp.maximum(m, jnp.max(s, axis=1, keepdims=True))
                p = jnp.exp(s - m_new).astype(BF16)
                new += [m_new, jnp.exp(m - m_new) * acc + jnp.dot(p, vx[h], preferred_element_type=F32)]
            return tuple(new)

        def kv_step(j, carry):
            ahead = scores(j + 1)
            return absorb(j, carry[:4], carry[4:], False) + ahead

        init = (jnp.full((tq, 1), MASK_VALUE, F32), jnp.zeros((tq, 128), F32)) * 2
        n_full = (i * tq) // tk
        carry = lax.fori_loop(0, n_full, kv_step, init + scores(0))
        m0, acc0, m1, acc1 = absorb(n_full, carry[:4], carry[4:], True)
        l0, l1 = pltpu.roll(acc0, 64, 1), pltpu.roll(acc1, 64, 1)
        o = jnp.where(first, acc0 / l0, acc1 / l1)
        or_ref[...] = o
        og_ref[...] = o * _silu(qblk[:, 384:512])
        lse_ref[...] = jnp.where(first, m0 + jnp.log(l0), m1 + jnp.log(l1))

    out = jax.ShapeDtypeStruct((bsz, t, FOX_W), F32)
    blk = pl.BlockSpec((None, tq, 128), lambda b, p, i: (b, i, p))
    return pl.pallas_call(
        body, grid=(bsz, 4, nq),
        in_specs=[pl.BlockSpec((None, tq, 512), lambda b, p, i: (b, i, C_BLK0 + p)),
                  pl.BlockSpec((None, t, 512), lambda b, p, i: (b, 0, C_BLK0 + p)),
                  blk,
                  pl.BlockSpec((None, None, 8, t), lambda b, p, i: (b, p, 0, 0))],
        out_specs=[blk, blk, blk],
        out_shape=[out, out, out],
        compiler_params=_cparams(("parallel", "parallel", "arbitrary")), name=name)(proj3, proj3, c_nat, c_t)


def _fox_bwd(proj3, o_raw, dmixed, lse, c_nat, c_t, name):
    bsz, t, _ = proj3.shape
    tq = _fox_tile(t)
    nq = t // tq
    tk = min(2 * tq, t)
    ratio = tk // tq

    def body(a_ref, or_ref, do_ref, lse_ref, cn_ref, ct_ref, dc_out, dct_out, drow_out, dq_sc, do_sc, dl_sc):
        def prep(i, c):
            r0 = pl.multiple_of(i * tq, tq)
            g = a_ref[pl.ds(r0, tq), 384:512]
            dout = do_ref[pl.ds(r0, tq), :]
            o = or_ref[pl.ds(r0, tq), :]
            dc_out[pl.ds(r0, tq), 384:512] = dout * o * _dsilu(g)
            do = dout * _silu(g)
            do_sc[pl.ds(r0, tq), :] = do
            prod = do * o
            d0 = jnp.sum(prod[:, 0:64], axis=1, keepdims=True)
            d1 = jnp.sum(prod[:, 64:128], axis=1, keepdims=True)
            dl_sc[pl.ds(r0, tq), :] = jnp.concatenate([jnp.broadcast_to(d0, (tq, 64)), jnp.broadcast_to(d1, (tq, 64))], axis=1)
            dq_sc[pl.ds(r0, tq), :] = jnp.zeros((tq, 128), F32)
            drow_out[pl.ds(r0, tq), :] = jnp.zeros((tq, 128), F32)
            return c

        lax.fori_loop(0, nq, prep, 0)
        dct_out[...] = jnp.zeros((8, t), F32)

        first = _iota((1, 128), 1) < 64

        def heads(v):
            return [jnp.where(first, v, 0.0).astype(BF16), jnp.where(first, 0.0, v).astype(BF16)]

        def kv_tile(j, c):
            c0 = pl.multiple_of(j * tk, tk)
            kb = a_ref[pl.ds(c0, tk), 128:256].astype(BF16)
            vb = a_ref[pl.ds(c0, tk), 256:384].astype(BF16)
            cks = [ct_ref[h:h + 1, pl.ds(c0, tk)] for h in range(2)]

            def q_step(i, carry, diagonal):
                dk, dv, dcol0, dcol1 = carry
                r0 = pl.multiple_of(i * tq, tq)
                causal = _iota((tq, tk), 0) + i * tq >= _iota((tq, tk), 1) + j * tk
                qv = a_ref[pl.ds(r0, tq), 0:128] * 0.125
                do = do_sc[pl.ds(r0, tq), :]
                qb, dob = qv.astype(BF16), do.astype(BF16)
                qm, dom = heads(qv), heads(do)
                full, dcols, rsums = [], [], []
                for h in range(2):
                    lse_h = lse_ref[pl.ds(r0, tq), 64 * h:64 * h + 1]
                    dl_h = dl_sc[pl.ds(r0, tq), 64 * h:64 * h + 1]
                    cq = cn_ref[pl.ds(r0, tq), 64 * h:64 * h + 1]
                    p = jnp.exp(_dot(qm[h], kb, NT) + (cq - cks[h]) - lse_h)
                    if diagonal:
                        p = jnp.where(causal, p, 0.0)
                    ds = p * (_dot(dom[h], vb, NT) - dl_h)
                    dsb = ds.astype(BF16)
                    full.append((_dot(p.astype(BF16), dob, TN), _dot(dsb, qb, TN),
                                 jnp.dot(dsb, kb, preferred_element_type=F32)))
                    dcols.append(jnp.sum(ds, axis=0, keepdims=True))
                    rsums.append(jnp.broadcast_to(jnp.sum(ds, axis=1, keepdims=True), (tq, 128)))
                dq_sc[pl.ds(r0, tq), :] += jnp.where(first, full[0][2], full[1][2]) * 0.125
                drow_out[pl.ds(r0, tq), :] += jnp.where(first, rsums[0], rsums[1])
                return (dk + jnp.where(first, full[0][1], full[1][1]), dv + jnp.where(first, full[0][0], full[1][0]),
                        dcol0 - dcols[0], dcol1 - dcols[1])

            carry = (jnp.zeros((tk, 128), F32), jnp.zeros((tk, 128), F32), jnp.zeros((1, tk), F32), jnp.zeros((1, tk), F32))
            for r in range(ratio):
                carry = q_step(ratio * j + r, carry, True)
            dk, dv, dcol0, dcol1 = lax.fori_loop(ratio * (j + 1), nq, functools.partial(q_step, diagonal=False), carry)
            dct_out[0:1, pl.ds(c0, tk)] = dcol0
            dct_out[1:2, pl.ds(c0, tk)] = dcol1
            dc_out[pl.ds(c0, tk), 128:256] = dk
            dc_out[pl.ds(c0, tk), 256:384] = dv
            return c

        lax.fori_loop(0, t // tk, kv_tile, 0)
        dc_out[:, 0:128] = dq_sc[...]

    blk = pl.BlockSpec((None, t, 128), lambda b, p: (b, 0, p))
    return pl.pallas_call(
        body, grid=(bsz, 4),
        in_specs=[pl.BlockSpec((None, t, 512), lambda b, p: (b, 0, C_BLK0 + p)),
                  blk,
                  pl.BlockSpec((None, t, 128), lambda b, p: (b, 0, 4 + p)),
                  blk, blk,
                  pl.BlockSpec((None, None, 8, t), lambda b, p: (b, p, 0, 0))],
        out_specs=[pl.BlockSpec((None, t, 512), lambda b, p: (b, 0, p)),
                   pl.BlockSpec((None, None, 8, t), lambda b, p: (b, p, 0, 0)), blk],
        out_shape=[jax.ShapeDtypeStruct((bsz, t, C_W), F32), jax.ShapeDtypeStruct((bsz, 4, 8, t), F32),
                   jax.ShapeDtypeStruct((bsz, t, FOX_W), F32)],
        scratch_shapes=[pltpu.VMEM((t, 128), F32), pltpu.VMEM((t, 128), F32), pltpu.VMEM((t, 128), F32)],
        compiler_params=_cparams(("parallel", "parallel")), name=name)(proj3, o_raw, dmixed, lse, c_nat, c_t)


def _mix_tm(n):
    return min(512, n)


def _outproj_fwd(x2, oa, ob, oc, wo, g_row, name):
    n, d = x2.shape
    tm = _mix_tm(n)

    def body(x_ref, oa_ref, ob_ref, oc_ref, w_ref, g_ref, y_ref, xo_ref):
        y = (jnp.dot(oa_ref[...].astype(BF16), w_ref[0:256, :], preferred_element_type=F32)
             + jnp.dot(ob_ref[...].astype(BF16), w_ref[256:512, :], preferred_element_type=F32)
             + jnp.dot(oc_ref[...].astype(BF16), w_ref[512:1024, :], preferred_element_type=F32))
        y_ref[...] = y
        xo_ref[...] = x_ref[...] + y * _rstd(y) * g_ref[...]

    row = lambda w: pl.BlockSpec((tm, w), lambda i: (i, 0))
    out = jax.ShapeDtypeStruct((n, d), F32)
    return pl.pallas_call(
        body, grid=(n // tm,),
        in_specs=[row(d), row(256), row(256), row(512), pl.BlockSpec((d, d), lambda i: (0, 0)),
                  pl.BlockSpec((1, d), lambda i: (0, 0))],
        out_specs=[row(d), row(d)], out_shape=[out, out],
        compiler_params=_cparams(("parallel",)), name=name)(x2, oa, ob, oc, wo, g_row)


def _loss_head(x2, target2, name):
    n, d = x2.shape
    tm = _mix_tm(n)

    def body(x_ref, t_ref, dx_ref, l_ref):
        err = x_ref[...] - t_ref[...]
        dx_ref[...] = err * (1.0 / d)

        @pl.when(pl.program_id(0) == 0)
        def _():
            l_ref[...] = jnp.zeros((8, 128), F32)

        l_ref[...] += jnp.sum(err * err)

    row = pl.BlockSpec((tm, d), lambda i: (i, 0))
    return pl.pallas_call(
        body, grid=(n // tm,), in_specs=[row, row],
        out_specs=[row, pl.BlockSpec((8, 128), lambda i: (0, 0))],
        out_shape=[jax.ShapeDtypeStruct((n, d), F32), jax.ShapeDtypeStruct((8, 128), F32)],
        compiler_params=_cparams(("arbitrary",)), name=name)(x2, target2)


def _outproj_bwd(dxo, y, oa, ob, oc, wo, g_row, name):
    n, d = dxo.shape
    tm = _mix_tm(n)

    def body(dx_ref, y_ref, oa_ref, ob_ref, oc_ref, w_ref, g_ref, dm_ref, dw_ref, dg_ref):
        @pl.when(pl.program_id(0) == 0)
        def _():
            dw_ref[...] = jnp.zeros((d, d), F32)
            dg_ref[...] = jnp.zeros((8, d), F32)

        yv, dx = y_ref[...], dx_ref[...]
        r = _rstd(yv)
        yn = yv * r
        dg_ref[...] += jnp.sum(dx * yn, axis=0, keepdims=True)
        dyn = dx * g_ref[...]
        dy = (r * (dyn - yn * jnp.mean(dyn * yn, axis=-1, keepdims=True))).astype(BF16)
        dm_ref[...] = _dot(dy, w_ref[...], NT)
        dw_ref[0:256, :] += _dot(oa_ref[...].astype(BF16), dy, TN)
        dw_ref[256:512, :] += _dot(ob_ref[...].astype(BF16), dy, TN)
        dw_ref[512:1024, :] += _dot(oc_ref[...].astype(BF16), dy, TN)

    row = lambda w: pl.BlockSpec((tm, w), lambda i: (i, 0))
    fixed = lambda r, c: pl.BlockSpec((r, c), lambda i: (0, 0))
    return pl.pallas_call(
        body, grid=(n // tm,),
        in_specs=[row(d), row(d), row(256), row(256), row(512), fixed(d, d), fixed(1, d)],
        out_specs=[row(d), fixed(d, d), fixed(8, d)],
        out_shape=[jax.ShapeDtypeStruct((n, d), F32), jax.ShapeDtypeStruct((d, d), F32), jax.ShapeDtypeStruct((8, d), F32)],
        compiler_params=_cparams(("arbitrary",)), name=name)(dxo, y, oa, ob, oc, wo, g_row)


_PIECES = ((0, A_W), (A_W, B_W), (A_W + B_W, C_W), (A_W + B_W + C_W, F_W))


def _inproj_bwd_x(x2, dxo, g_row, w_int, pieces, name):
    n, d = x2.shape
    tm = min(256, n)

    def body(x_ref, dxo_ref, g_ref, w_ref, da_ref, db_ref, dc_ref, df_ref, dx_ref, dg_ref):
        @pl.when(pl.program_id(0) == 0)
        def _():
            dg_ref[...] = jnp.zeros((8, d), F32)

        dh = jnp.zeros((tm, d), F32)
        for ref, (o, w) in zip((da_ref, db_ref, dc_ref, df_ref), _PIECES):
            dh = dh + _dot(ref[...].astype(BF16), w_ref[:, o:o + w], NT)
        x = x_ref[...]
        r = _rstd(x)
        xn = x * r
        dg_ref[...] += jnp.sum(dh * xn, axis=0, keepdims=True)
        dxn = dh * g_ref[...]
        dx_ref[...] = dxo_ref[...] + r * (dxn - xn * jnp.mean(dxn * xn, axis=-1, keepdims=True))

    row = lambda w: pl.BlockSpec((tm, w), lambda i: (i, 0))
    fixed = lambda r, c: pl.BlockSpec((r, c), lambda i: (0, 0))
    return pl.pallas_call(
        body, grid=(n // tm,),
        in_specs=[row(d), row(d), fixed(1, d), fixed(d, E_INT)] + [row(w) for _, w in _PIECES],
        out_specs=[row(d), fixed(8, d)],
        out_shape=[jax.ShapeDtypeStruct((n, d), F32), jax.ShapeDtypeStruct((8, d), F32)],
        compiler_params=_cparams(("arbitrary",)), name=name)(x2, dxo, g_row, w_int, *pieces)


def _inproj_bwd_w(x2, g_row, pieces, name):
    n, d = x2.shape
    tm = min(256, n)

    def body(x_ref, g_ref, da_ref, db_ref, dc_ref, df_ref, dw_ref):
        @pl.when(pl.program_id(0) == 0)
        def _():
            dw_ref[...] = jnp.zeros((d, E_INT), F32)

        x = x_ref[...]
        h = (x * _rstd(x) * g_ref[...]).astype(BF16)
        for ref, (o, w) in zip((da_ref, db_ref, dc_ref, df_ref), _PIECES):
            dw_ref[:, o:o + w] += _dot(h, ref[...].astype(BF16), TN)

    row = lambda w: pl.BlockSpec((tm, w), lambda i: (i, 0))
    return pl.pallas_call(
        body, grid=(n // tm,),
        in_specs=[row(d), pl.BlockSpec((1, d), lambda i: (0, 0))] + [row(w) for _, w in _PIECES],
        out_specs=pl.BlockSpec((d, E_INT), lambda i: (0, 0)),
        out_shape=jax.ShapeDtypeStruct((d, E_INT), F32),
        compiler_params=_cparams(("arbitrary",), vmem_mb=56), name=name)(x2, g_row, *pieces)


def _block_diag(pool_w_l):
    z = jnp.zeros((64, 64), pool_w_l.dtype)
    return jnp.concatenate(
        [jnp.concatenate([pool_w_l[g] if c == g else z for c in range(4)], axis=1) for g in range(4)], axis=0)


def _pad_lanes(v, width=128):
    return jnp.pad(v, ((0, 0),) * (v.ndim - 1) + ((0, width - v.shape[-1]),))


def _local_step(x, target, lower_bounds, pre_norm_g, w_in_int, hgrn_norm_g, fox_f_bias, pool_w, pool_scale,
                w_out_bf, post_norm_g, on_weight_grads):
    bsz, t, d = x.shape
    n = bsz * t
    lbs = _lbs_fwd(lower_bounds)
    saved = []
    xc = x.reshape(n, d)
    for l in range(DEPTH):
        proj = _inproj_fwd(xc, pre_norm_g[l:l + 1], w_in_int[l], f"inproj_fwd{l}").reshape(bsz, t, E_INT)
        wbd = _block_diag(pool_w[l]).astype(BF16)
        bias_row = _pad_lanes(fox_f_bias[l:l + 1])
        oa, oa_raw, states = _hgrn_fwd(proj, lbs[l:l + 1], hgrn_norm_g[l:l + 1], f"hgrn_fwd{l}")
        ob = _pool_fwd(proj, wbd, pool_scale[l:l + 1], f"pool_fwd{l}")
        c_nat, c_t = _foxgate_fwd(proj, bias_row, f"foxgate_fwd{l}")
        oc, oc_raw, lse = _fox_fwd(proj, c_nat, c_t, f"fox_fwd{l}")
        y, xn = _outproj_fwd(xc, oa.reshape(n, -1), ob.reshape(n, -1), oc.reshape(n, -1), w_out_bf[l],
                             post_norm_g[l:l + 1], f"outproj_fwd{l}")
        saved.append((xc, proj, wbd, bias_row, oa, oa_raw, states, ob, oc, oc_raw, lse, c_nat, c_t, y))
        xc = xn
    dx, sq = _loss_head(xc, target.reshape(n, d), "loss_head")
    g = {k: [None] * DEPTH for k in ("pre", "hgn", "bias", "pool_w", "pool_scale", "post", "lbs")}
    handed = [None] * DEPTH
    for l in reversed(range(DEPTH)):
        xin, proj, wbd, bias_row, oa, oa_raw, states, ob, oc, oc_raw, lse, c_nat, c_t, y = saved[l]
        dmix, d_w_out, dpost = _outproj_bwd(dx, y, oa.reshape(n, -1), ob.reshape(n, -1), oc.reshape(n, -1),
                                            w_out_bf[l], post_norm_g[l:l + 1], f"outproj_bwd{l}")
        g["post"][l] = dpost[0]
        dmix3 = dmix.reshape(bsz, t, d)
        d_c, dct, drow = _fox_bwd(proj, oc_raw, dmix3, lse, c_nat, c_t, f"fox_bwd{l}")
        dc_nat = _pad_lanes(dct[:, :, 0:2, :].reshape(bsz, FOX_HEADS, t).transpose(0, 2, 1)
                            + drow.reshape(bsz, t, FOX_HEADS, 64)[..., 0])
        d_f, dbias = _foxgate_bwd(proj, dc_nat, bias_row, f"foxgate_bwd{l}")
        g["bias"][l] = jnp.sum(dbias[:, 0, :FOX_HEADS], axis=0)
        d_b, dscale, dwbd = _pool_bwd(proj, dmix3, wbd, pool_scale[l:l + 1], f"pool_bwd{l}")
        g["pool_scale"][l] = jnp.sum(dscale[:, 0], axis=0)
        dwbd = jnp.sum(dwbd, axis=0)
        g["pool_w"][l] = jnp.stack([dwbd[64 * k:64 * (k + 1), 64 * k:64 * (k + 1)] for k in range(4)])
        d_a, dgn, dlb = _hgrn_bwd(proj, oa_raw, dmix3, states, lbs[l:l + 1], hgrn_norm_g[l:l + 1], f"hgrn_bwd{l}")
        g["hgn"][l] = jnp.sum(dgn[:, 0], axis=0)
        g["lbs"][l] = jnp.sum(dlb[:, 0], axis=0)
        pieces = [p.reshape(n, -1) for p in (d_a, d_b, d_c, d_f)]
        handed[l] = on_weight_grads(l, _inproj_bwd_w(xin, pre_norm_g[l:l + 1], pieces, f"inproj_bwd_w{l}"), d_w_out)
        dx, dpre = _inproj_bwd_x(xin, dx, pre_norm_g[l:l + 1], w_in_int[l], pieces, f"inproj_bwd_x{l}")
        g["pre"][l] = dpre[0]
    grads = {k: jnp.stack(v) for k, v in g.items()}
    return sq, dx.reshape(bsz, t, d), grads, handed


def _place():
    return lax.axis_index("x"), lax.axis_index("y"), lax.axis_index("c")


def _other_chips(x, y):
    return [(1 - x, y), (x, 1 - y), (1 - x, 1 - y)]


_ANY = pl.BlockSpec(memory_space=pl.ANY)


def _gather_body(handshake, n_arrays):
    def body(*refs):
        srcs, dsts = refs[:n_arrays], refs[n_arrays:2 * n_arrays]
        ici_send, ici_recv, d2d_send, d2d_recv, local_sems = refs[2 * n_arrays:]
        x, y, c = _place()
        if handshake:
            barrier = pltpu.get_barrier_semaphore()
            for peer in [(px, py, c) for px, py in _other_chips(x, y)] + [(x, y, 1 - c)]:
                pl.semaphore_signal(barrier, inc=1, device_id=peer, device_id_type=MESH)
            pl.semaphore_wait(barrier, 4)
        me = 2 * x + y
        pairs = list(zip(srcs, dsts))
        order = [(k, j) for k in range(3) for j in range(n_arrays)]
        mine = [pltpu.make_async_copy(src, dst.at[me], local_sems.at[j]) for j, (src, dst) in enumerate(pairs)]
        for cp in mine:
            cp.start()
        chips = _other_chips(x, y)
        sends = [pltpu.make_async_remote_copy(
            src_ref=pairs[j][0].at[c], dst_ref=pairs[j][1].at[me, c], send_sem=ici_send.at[n], recv_sem=ici_recv.at[n],
            device_id=(chips[k][0], chips[k][1], c), device_id_type=MESH) for n, (k, j) in enumerate(order)]
        for cp in sends:
            cp.start()
        passed = [pltpu.make_async_remote_copy(
            src_ref=pairs[j][1].at[2 * chips[k][0] + chips[k][1], c], dst_ref=pairs[j][1].at[2 * chips[k][0] + chips[k][1], c],
            send_sem=d2d_send.at[n], recv_sem=d2d_recv.at[n], device_id=(x, y, 1 - c), device_id_type=MESH)
            for n, (k, j) in enumerate(order)]
        for n, (k, j) in enumerate(order):
            px, py = chips[k]
            src, dst = pairs[j]
            pltpu.make_async_remote_copy(
                src_ref=src.at[c], dst_ref=dst.at[2 * px + py, c], send_sem=ici_send.at[n], recv_sem=ici_recv.at[n],
                device_id=(px, py, c), device_id_type=MESH).wait_recv()
            passed[n].start()
        for n, (k, j) in enumerate(order):
            px, py = chips[k]
            src, dst = pairs[j]
            pltpu.make_async_remote_copy(
                src_ref=dst.at[2 * px + py, 1 - c], dst_ref=dst.at[2 * px + py, 1 - c], send_sem=d2d_send.at[n],
                recv_sem=d2d_recv.at[n], device_id=(x, y, 1 - c), device_id_type=MESH).wait_recv()
        for cp in sends + passed:
            cp.wait_send()
        for cp in mine:
            cp.wait()

    return body


def _gather_sems(n_arrays):
    return [pltpu.SemaphoreType.DMA((3 * n_arrays,))] * 4 + [pltpu.SemaphoreType.DMA((n_arrays,))]


def _gathered(a):
    return jax.ShapeDtypeStruct((N_CHIPS,) + a.shape, a.dtype)


def _gather_weights(arrays):
    n = len(arrays)
    return pl.pallas_call(
        _gather_body(False, n), in_specs=[_ANY] * n, out_specs=[_ANY] * n, out_shape=[_gathered(a) for a in arrays],
        scratch_shapes=_gather_sems(n), name="gather_weights")(*arrays)


def _gather_weights_beside(arrays):
    hbm = pltpu.MemorySpace.HBM
    n = len(arrays)
    srcs = [jax.new_ref(a, memory_space=hbm) for a in arrays]
    dsts = [jax.empty_ref(_gathered(a), memory_space=hbm) for a in arrays]
    body = _gather_body(True, n)

    @pl.kernel(mesh=plsc.ScalarSubcoreMesh(axis_name="sequencer", num_cores=1), name="gather_weights_beside",
               scratch_types=_gather_sems(n), compiler_params=pltpu.CompilerParams(collective_id=1))
    def launch(*sems):
        body(*srcs, *dsts, *sems)

    launch()
    return [d[...] for d in dsts]


def _swap_with_sibling(parts, name):
    k = len(parts)

    def body(*refs):
        src, dst = refs[:k], refs[k:2 * k]
        send_sems, recv_sems = refs[2 * k:]
        x, y, c = _place()
        cps = [pltpu.make_async_remote_copy(src_ref=src[j], dst_ref=dst[j], send_sem=send_sems.at[j], recv_sem=recv_sems.at[j],
                                            device_id=(x, y, 1 - c), device_id_type=MESH) for j in range(k)]
        for cp in cps:
            cp.start()
        for cp in cps:
            cp.wait()

    return pl.pallas_call(
        body, in_specs=[_ANY] * k, out_specs=[_ANY] * k,
        out_shape=[jax.ShapeDtypeStruct(p.shape, p.dtype) for p in parts],
        scratch_shapes=[pltpu.SemaphoreType.DMA((k,)), pltpu.SemaphoreType.DMA((k,))], name=name)(*parts)


N_PEERS = 7


def _grad_exchange_body():
    def body(pin_ref, pout_ref, lin_ref, lout_ref, send_sems, recv_sems):
        x, y, c = _place()
        barrier = pltpu.get_barrier_semaphore()
        for k in range(1, N_PEERS + 1):
            peer = (x ^ ((k >> 2) & 1), y ^ ((k >> 1) & 1), c ^ (k & 1))
            pl.semaphore_signal(barrier, inc=1, device_id=peer, device_id_type=MESH)
        pl.semaphore_wait(barrier, N_PEERS)
        me = 2 * x + y
        pairs = ((pin_ref, lin_ref), (pout_ref, lout_ref))
        cps = []
        for k, (px, py) in enumerate(_other_chips(x, y)):
            for r in range(2):
                for j, (src, dst) in enumerate(pairs):
                    cps.append(pltpu.make_async_remote_copy(
                        src_ref=src.at[2 * px + py, r], dst_ref=dst.at[2 * k + c], send_sem=send_sems.at[2 * (2 * k + r) + j],
                        recv_sem=recv_sems.at[2 * (2 * k + c) + j], device_id=(px, py, r), device_id_type=MESH))
        for j, (src, dst) in enumerate(pairs):
            cps.append(pltpu.make_async_remote_copy(
                src_ref=src.at[me, 1 - c], dst_ref=dst.at[N_PEERS - 1], send_sem=send_sems.at[2 * (N_PEERS - 1) + j],
                recv_sem=recv_sems.at[2 * (N_PEERS - 1) + j], device_id=(x, y, 1 - c), device_id_type=MESH))
        for cp in cps:
            cp.start()
        for s in range(N_PEERS):
            for j, (src, dst) in enumerate(pairs):
                pltpu.make_async_remote_copy(
                    src_ref=src.at[0, 0], dst_ref=dst.at[s], send_sem=send_sems.at[2 * s + j], recv_sem=recv_sems.at[2 * s + j],
                    device_id=(x, y, 1 - c), device_id_type=MESH).wait_recv()
        for cp in cps:
            cp.wait_send()

    return body


_EXCHANGE_SEMS = [pltpu.SemaphoreType.DMA((2 * N_PEERS,))] * 2


def _landing(p):
    return jax.ShapeDtypeStruct((N_PEERS,) + p.shape[2:], p.dtype)


def _grad_exchange_beside(pin, pout, name, collective_id):
    hbm = pltpu.MemorySpace.HBM
    pin_ref, pout_ref = jax.new_ref(pin, memory_space=hbm), jax.new_ref(pout, memory_space=hbm)
    lin_ref, lout_ref = jax.empty_ref(_landing(pin), memory_space=hbm), jax.empty_ref(_landing(pout), memory_space=hbm)
    body = _grad_exchange_body()

    @pl.kernel(mesh=plsc.ScalarSubcoreMesh(axis_name="sequencer", num_cores=1), name=name,
               scratch_types=_EXCHANGE_SEMS, compiler_params=pltpu.CompilerParams(collective_id=collective_id))
    def launch(send_sems, recv_sems):
        body(pin_ref, pout_ref, lin_ref, lout_ref, send_sems, recv_sems)

    launch()
    return lin_ref[...], lout_ref[...]


def _add_n(parts, name, with_bf16=False):
    r, c = parts[0].shape
    tr = 256 if r % 256 == 0 else r
    n = len(parts)

    def body(*refs):
        acc = refs[0][...].astype(F32)
        for ref in refs[1:n]:
            acc = acc + ref[...].astype(F32)
        refs[n][...] = acc
        if with_bf16:
            refs[n + 1][...] = acc.astype(BF16)

    blk = pl.BlockSpec((tr, c), lambda i: (i, 0))
    outs = [jax.ShapeDtypeStruct((r, c), F32)] + ([jax.ShapeDtypeStruct((r, c), BF16)] if with_bf16 else [])
    res = pl.pallas_call(
        body, grid=(r // tr,), in_specs=[blk] * n, out_specs=[blk] * len(outs),
        out_shape=outs, compiler_params=_cparams(("parallel",)), name=name)(*parts)
    return res if with_bf16 else res[0]


def _all_reduce_small(packet):
    r, w = packet.shape

    def body(p_ref, o_ref, buf, send_sems, recv_sems):
        x, y, c = _place()
        me = 4 * x + 2 * y + c
        buf[me] = p_ref[...]
        peers = []
        for k in range(1, 8):
            fx, fy, fc = (k >> 2) & 1, (k >> 1) & 1, k & 1
            peers.append((x ^ fx, y ^ fy, c ^ fc))
        cps = [pltpu.make_async_remote_copy(src_ref=p_ref, dst_ref=buf.at[me], send_sem=send_sems.at[k], recv_sem=recv_sems.at[k],
                                            device_id=peer, device_id_type=MESH) for k, peer in enumerate(peers)]
        for cp in cps:
            cp.start()
        for k, (px, py, pc) in enumerate(peers):
            pltpu.make_async_remote_copy(src_ref=p_ref, dst_ref=buf.at[4 * px + 2 * py + pc], send_sem=send_sems.at[k],
                                         recv_sem=recv_sems.at[k], device_id=(px, py, pc), device_id_type=MESH).wait_recv()
        for cp in cps:
            cp.wait_send()
        acc = buf[0]
        for k in range(1, 8):
            acc = acc + buf[k]
        o_ref[...] = acc

    vm = pl.BlockSpec(memory_space=pltpu.VMEM)
    return pl.pallas_call(
        body, in_specs=[vm], out_specs=vm, out_shape=jax.ShapeDtypeStruct((r, w), F32),
        scratch_shapes=[pltpu.VMEM((8, r, w), F32), pltpu.SemaphoreType.DMA((7,)), pltpu.SemaphoreType.DMA((7,))],
        name="all_reduce_small")(packet)


def _adamw_math(w, g, m, v):
    m = ADAM_B1 * m + (1.0 - ADAM_B1) * g
    v = ADAM_B2 * v + (1.0 - ADAM_B2) * (g * g)
    m_hat = m / (1.0 - ADAM_B1 ** ADAM_STEP)
    v_hat = v / (1.0 - ADAM_B2 ** ADAM_STEP)
    return -ADAM_LR * (m_hat / (jnp.sqrt(v_hat) + ADAM_EPS) + ADAM_WD * w), m, v


def _adamw(w, g, m, v, name):
    nl, r, c = w.shape
    tr = 256 if r % 256 == 0 else r

    def body(w_ref, g_ref, m_ref, v_ref, d_ref, mo_ref, vo_ref):
        d_ref[...], mo_ref[...], vo_ref[...] = _adamw_math(w_ref[...], g_ref[...], m_ref[...], v_ref[...])

    blk = pl.BlockSpec((None, tr, c), lambda l, i: (l, i, 0))
    out = jax.ShapeDtypeStruct(w.shape, F32)
    return pl.pallas_call(
        body, grid=(nl, r // tr), in_specs=[blk] * 4, out_specs=[blk] * 3, out_shape=[out] * 3,
        compiler_params=_cparams(("parallel", "parallel")), name=name)(w, g, m, v)


def _small_update(gsum, lower_bounds, wpack, mpack, vpack):
    r, w = gsum.shape
    lb_rows = DEPTH * HGRN_W // 128

    def body(g_ref, a_ref, w_ref, m_ref, v_ref, go_ref, d_ref, mo_ref, vo_ref):
        a = a_ref[...]
        a0, a1 = a[0:1], a[1:2]
        mx = jnp.maximum(a0, a1)
        e0, e1 = jnp.exp(a0 - mx), jnp.exp(a1 - mx)
        p0, p1 = e0 / (e0 + e1), e1 / (e0 + e1)
        g = g_ref[...]
        half = lb_rows // 2
        dl0 = jnp.concatenate([g[k:k + 1] for k in range(half)], axis=1)
        dl1 = jnp.concatenate([g[half + k:half + k + 1] for k in range(half)], axis=1)
        dp0 = (dl0 + dl1) - (dl0 + dl1)
        dp1 = dl1
        inner = p0 * dp0 + p1 * dp1
        da0, da1 = p0 * (dp0 - inner), p1 * (dp1 - inner)
        rows = [da0[:, 128 * k:128 * (k + 1)] for k in range(half)] + [da1[:, 128 * k:128 * (k + 1)] for k in range(half)]
        gfull = jnp.concatenate(rows + [g[lb_rows:]], axis=0)
        go_ref[...] = gfull
        d_ref[...], mo_ref[...], vo_ref[...] = _adamw_math(w_ref[...], gfull, m_ref[...], v_ref[...])

    vm = pl.BlockSpec(memory_space=pltpu.VMEM)
    out = jax.ShapeDtypeStruct((r, w), F32)
    return pl.pallas_call(body, in_specs=[vm] * 5, out_specs=[vm] * 4, out_shape=[out] * 4, name="small_update")(
        gsum, lower_bounds, wpack, mpack, vpack)


_SMALL = ("lower_bounds", "pre_norm_g", "hgrn_norm_g", "fox_f_bias", "pool_w", "pool_scale", "post_norm_g")


def _pack(parts):
    rows = []
    for k in _SMALL:
        f = parts[k].reshape(-1)
        pad = (-f.shape[0]) % (8 * 128)
        rows.append(jnp.pad(f, (0, pad)).reshape(-1, 128))
    rows.append(jnp.zeros((8, 128), F32))
    return jnp.concatenate(rows, axis=0)


def _unpack(pack, like):
    out, r = {}, 0
    for k in _SMALL:
        size = int(np.prod(like[k].shape))
        nr = -(-size // (8 * 128)) * 8
        out[k] = pack[r:r + nr].reshape(-1)[:size].reshape(like[k].shape)
        r += nr
    return out, r


def kernel(x, lower_bounds, pre_norm_g, w_in, hgrn_norm_g, fox_f_bias, pool_w, pool_scale, w_out, post_norm_g, loss_target, m_lower_bounds, m_pre_norm_g, m_w_in, m_hgrn_norm_g, m_fox_f_bias, m_pool_w, m_pool_scale, m_w_out, m_post_norm_g, v_lower_bounds, v_pre_norm_g, v_w_in, v_hgrn_norm_g, v_fox_f_bias, v_pool_w, v_pool_scale, v_w_out, v_post_norm_g):
    cx, cy, cc = _place()
    chip = 2 * cx + cy

    halves = lambda w, l: w[l].reshape(2, w.shape[1] // 2, w.shape[2]).astype(BF16)
    needed_first = _gather_weights([halves(w_in, 0)])
    needed_first, later = lax.optimization_barrier((needed_first, [halves(w_out, 0), halves(w_in, 1), halves(w_out, 1)]))
    later = _gather_weights_beside(later)
    w_in_int = [_internal_from_shards([a[q].reshape(D_MODEL, SHARD_W) for q in range(N_CHIPS)]) for a in (needed_first[0], later[1])]
    w_out_full = [a.reshape(D_MODEL, D_MODEL) for a in (later[0], later[2])]

    def on_weight_grads(l, d_w_in, d_w_out):
        pin = _shards_from_internal(d_w_in).reshape(N_CHIPS, 2, D_MODEL // 2, SHARD_W)
        pout = d_w_out.reshape(N_CHIPS, 2, D_MODEL // (2 * N_CHIPS), D_MODEL)
        own = [lax.dynamic_index_in_dim(lax.dynamic_index_in_dim(p, chip, 0, False), cc, 0, False) for p in (pin, pout)]
        return own, _grad_exchange_beside(pin.astype(BF16), pout.astype(BF16), f"grad_exchange{l}", 2 + l)

    sq, grad_x, g, handed = _local_step(x, loss_target, lower_bounds, pre_norm_g, w_in_int, hgrn_norm_g, fox_f_bias,
                                        pool_w, pool_scale, w_out_full, post_norm_g, on_weight_grads)
    first = cc == 0

    def finish(l, own, landed):
        halves_l = [_add_n([o] + [t[s] for s in range(N_PEERS)], f"grad_sum{l}_{j}") for j, (o, t) in enumerate(zip(own, landed))]
        others = _swap_with_sibling(halves_l, f"grad_swap{l}")
        g_in, g_out = [jnp.where(first, jnp.concatenate([h, o], axis=0), jnp.concatenate([o, h], axis=0))[None]
                       for h, o in zip(halves_l, others)]
        return (g_in, g_out, _adamw(w_in[l:l + 1], g_in, m_w_in[l:l + 1], v_w_in[l:l + 1], f"adamw_w_in{l}"),
                _adamw(w_out[l:l + 1], g_out, m_w_out[l:l + 1], v_w_out[l:l + 1], f"adamw_w_out{l}"))

    grad_x, last = lax.optimization_barrier((grad_x, handed[1]))
    done = [None, finish(1, *last)]

    small = {"lower_bounds": g["lbs"], "pre_norm_g": g["pre"], "hgrn_norm_g": g["hgn"], "fox_f_bias": g["bias"],
             "pool_w": g["pool_w"], "pool_scale": g["pool_scale"], "post_norm_g": g["post"]}
    packet = _pack(small)
    nrows = packet.shape[0]
    packet = packet.at[nrows - 1].set(sq[0])
    gsum = _all_reduce_small(packet)
    loss = gsum[nrows - 1, 0] * (0.5 / D_MODEL)

    weights = {"lower_bounds": lower_bounds, "pre_norm_g": pre_norm_g, "hgrn_norm_g": hgrn_norm_g,
               "fox_f_bias": fox_f_bias, "pool_w": pool_w, "pool_scale": pool_scale, "post_norm_g": post_norm_g}
    moments_m = {"lower_bounds": m_lower_bounds, "pre_norm_g": m_pre_norm_g, "hgrn_norm_g": m_hgrn_norm_g,
                 "fox_f_bias": m_fox_f_bias, "pool_w": m_pool_w, "pool_scale": m_pool_scale, "post_norm_g": m_post_norm_g}
    moments_v = {"lower_bounds": v_lower_bounds, "pre_norm_g": v_pre_norm_g, "hgrn_norm_g": v_hgrn_norm_g,
                 "fox_f_bias": v_fox_f_bias, "pool_w": v_pool_w, "pool_scale": v_pool_scale, "post_norm_g": v_post_norm_g}
    gp, dp, mp, vp = _small_update(gsum, lower_bounds, _pack(weights), _pack(moments_m), _pack(moments_v))
    gs, _ = _unpack(gp, weights)
    ds, _ = _unpack(dp, weights)
    ms, _ = _unpack(mp, weights)
    vs, _ = _unpack(vp, weights)

    first_layer, _ = lax.optimization_barrier((handed[0], (done[1], gp, dp, mp, vp)))
    done[0] = finish(0, *first_layer)
    both = lambda pick: jnp.concatenate([pick(done[l]) for l in range(DEPTH)], axis=0)
    grad_w_in, grad_w_out = both(lambda r: r[0]), both(lambda r: r[1])
    d_in, m_in, v_in = [both(lambda r, k=k: r[2][k]) for k in range(3)]
    d_out, m_out, v_out = [both(lambda r, k=k: r[3][k]) for k in range(3)]

    def ordered(s, big_in, big_out):
        return (s["lower_bounds"], s["pre_norm_g"], big_in, s["hgrn_norm_g"], s["fox_f_bias"], s["pool_w"],
                s["pool_scale"], big_out, s["post_norm_g"])

    return (loss, grad_x, *ordered(gs, grad_w_in, grad_w_out), *ordered(ds, d_in, d_out),
            *ordered(ms, m_in, m_out), *ordered(vs, v_in, v_out))
```

```python
import functools

import numpy as np
import jax
import jax.numpy as jnp
from jax import lax
from jax.experimental import pallas as pl
from jax.experimental.pallas import tpu as pltpu
from jax.experimental.pallas import tpu_sc as plsc

F32 = jnp.float32
BF16 = jnp.bfloat16
HI = lax.Precision.HIGHEST
MESH = pl.DeviceIdType.MESH

NORM_EPS = 1e-6
MASK_VALUE = -1e30
TINY = 1e-30
ADAM_LR, ADAM_B1, ADAM_B2, ADAM_EPS, ADAM_WD, ADAM_STEP = 0.001, 0.9, 0.999, 1e-08, 0.01, 10

D_MODEL = 1024
DEPTH = 2
N_CHIPS = 4
CHUNK = 64
LANES = 128
HGRN_W, POOL_W, FOX_W, FOX_HEADS = 256, 256, 512, 8
POOL_WINDOWS = (2, 4, 8, 16)
POOL_HALO = 16
IN_WIDTH = 3592
SHARD_W = IN_WIDTH // N_CHIPS
A_W, B_W, C_W, F_W = 1024, 512, 2048, 128
E_INT = A_W + B_W + C_W + F_W
B_BLK = A_W // 512
C_BLK0 = (A_W + B_W) // 512
F_BLK = (A_W + B_W + C_W) // 128


def _segments():
    segs = []
    for hp in range(2):
        for part in range(4):
            segs.append((part * 256 + hp * 128, 128))
    segs.append((1024, 256))
    segs.append((1280, 256))
    for hp in range(4):
        for part in range(4):
            segs.append((1536 + part * 512 + hp * 128, 128))
    segs.append((3584, 8))
    return segs


_SEGS = _segments()


def _to_internal(w):
    parts = [w[..., s:s + n] for s, n in _SEGS]
    parts.append(jnp.zeros(w.shape[:-1] + (E_INT - IN_WIDTH,), w.dtype))
    return jnp.concatenate(parts, axis=-1)


def _to_original(w):
    offs, o = [], 0
    for s, n in _SEGS:
        offs.append((s, o, n))
        o += n
    parts = [w[..., o:o + n] for s, o, n in sorted(offs)]
    return jnp.concatenate(parts, axis=-1)


def _internal_from_shards(shards):
    parts = []
    for s, n in _SEGS:
        while n > 0:
            q, r = divmod(s, SHARD_W)
            take = min(n, SHARD_W - r)
            parts.append(shards[q][..., r:r + take])
            s, n = s + take, n - take
    parts.append(jnp.zeros(shards[0].shape[:-1] + (E_INT - IN_WIDTH,), shards[0].dtype))
    return jnp.concatenate(parts, axis=-1)


def _shards_from_internal(w):
    offs, o = [], 0
    for s, n in _SEGS:
        offs.append((s, o, n))
        o += n
    blocks = []
    for q in range(N_CHIPS):
        lo, hi = SHARD_W * q, SHARD_W * (q + 1)
        parts = [w[..., o + max(lo, s) - s:o + min(hi, s + n) - s] for s, o, n in sorted(offs) if s < hi and s + n > lo]
        blocks.append(jnp.concatenate(parts, axis=-1))
    return jnp.stack(blocks)


def _cparams(sem=None, vmem_mb=48):
    kw = dict(vmem_limit_bytes=vmem_mb * 1024 * 1024)
    if sem is not None:
        kw["dimension_semantics"] = sem
    return pltpu.CompilerParams(**kw)


def _sig(x):
    return 1.0 / (1.0 + jnp.exp(-x))


def _silu(x):
    return x * _sig(x)


def _dsilu(x):
    s = _sig(x)
    return s * (1.0 + x * (1.0 - s))


def _rstd(x):
    return lax.rsqrt(jnp.mean(x * x, axis=-1, keepdims=True) + NORM_EPS)


def _dot(a, b, dims, **kw):
    return lax.dot_general(a, b, (dims, ((), ())), preferred_element_type=F32, **kw)


NN = ((1,), (0,))
NT = ((1,), (1,))
TN = ((0,), (0,))


def _iota(shape, dim):
    return lax.broadcasted_iota(jnp.int32, shape, dim)


def _lbs_fwd(lower_bounds):
    def body(a_ref, o_ref):
        a = a_ref[...]
        a0, a1 = a[0:1], a[1:2]
        m = jnp.maximum(a0, a1)
        e0, e1 = jnp.exp(a0 - m), jnp.exp(a1 - m)
        p0, p1 = e0 / (e0 + e1), e1 / (e0 + e1)
        o_ref[...] = jnp.concatenate([p0 - p0, (p0 + p1) - p0], axis=0)

    return pl.pallas_call(body, out_shape=jax.ShapeDtypeStruct(lower_bounds.shape, F32), name="lbs_fwd")(lower_bounds)


def _inproj_fwd(x2, g_row, w_int, name):
    n, d = x2.shape
    e = w_int.shape[1]
    tm = min(256, n)

    def body(x_ref, g_ref, w_ref, o_ref):
        x = x_ref[...]
        h = (x * _rstd(x) * g_ref[...]).astype(BF16)
        o_ref[...] = jnp.dot(h, w_ref[...], preferred_element_type=F32)

    return pl.pallas_call(
        body, grid=(n // tm,),
        in_specs=[pl.BlockSpec((tm, d), lambda i: (i, 0)), pl.BlockSpec((1, d), lambda i: (0, 0)),
                  pl.BlockSpec((d, e), lambda i: (0, 0))],
        out_specs=pl.BlockSpec((tm, e), lambda i: (i, 0)),
        out_shape=jax.ShapeDtypeStruct((n, e), F32),
        compiler_params=_cparams(("parallel",)), name=name)(x2, g_row, w_int)


def _chunk_cumsum_matrix():
    i, j = _iota((LANES, LANES), 0), _iota((LANES, LANES), 1)
    return ((i <= j) & ((i // CHUNK) == (j // CHUNK))).astype(F32)


def _hgrn_gates(a, lb):
    qa, z = a[:, 0:128], a[:, 128:256]
    sg, sgn = _sig(z), _sig(-z)
    fg = lb + (1.0 - lb) * sg
    lf = jnp.log(jnp.maximum(fg, TINY))
    kk = (1.0 - lb) * sgn
    return qa * _sig(qa), kk, lf, sg, sgn, fg


def _hgrn_fwd(proj3, lbs_row, gn_col, name):
    bsz, t, _ = proj3.shape
    nt = t // LANES

    def body(a_ref, lb_ref, gn_ref, og_ref, or_ref):
        lb = lb_ref[...]
        gn = gn_ref[...]
        umat = _chunk_cumsum_matrix()
        lane64 = _iota((1, LANES), 1) % CHUNK

        def tile(i, carry):
            r0 = pl.multiple_of(i * LANES, LANES)
            a = a_ref[pl.ds(r0, LANES), :]
            qq, kk, lf, _, _, _ = _hgrn_gates(a, lb)
            va, ga = a[:, 256:384], a[:, 384:512]
            q_t, k_t, v_t = qq.T, kk.T, va.T
            b_t = jnp.dot(lf.T, umat, precision=HI, preferred_element_type=F32)
            new_s, o_heads = [], []
            for h in range(2):
                s_h = carry[h]
                rs = slice(CHUNK * h, CHUNK * (h + 1))
                qh, kh, vh, bh = q_t[rs], k_t[rs], v_t[rs], b_t[rs]
                inter = []
                for c in range(2):
                    cs = slice(CHUNK * c, CHUNK * (c + 1))
                    b_ = bh[:, cs]
                    qt = (qh[:, cs] * jnp.exp(b_)).astype(BF16)
                    inter.append(_dot(s_h.astype(BF16), qt, TN))
                    bl = b_[:, CHUNK - 1:CHUNK]
                    kt = (kh[:, cs] * jnp.exp(bl - b_)).astype(BF16)
                    s_h = jnp.exp(bl) * s_h + _dot(kt, vh[:, cs].astype(BF16), NT)
                new_s.append(s_h)

                acc = jnp.concatenate(inter, axis=1) + jnp.sum(qh * kh, axis=0, keepdims=True) * vh
                for dlt in range(1, CHUNK):
                    kr, br, vr = pltpu.roll(kh, dlt, 1), pltpu.roll(bh, dlt, 1), pltpu.roll(vh, dlt, 1)
                    e = jnp.exp(jnp.minimum(bh - br, 0.0))
                    att = jnp.sum(qh * kr * e, axis=0, keepdims=True)
                    acc = acc + jnp.where(lane64 >= dlt, att, 0.0) * vr
                o_heads.append(acc)
            normed = []
            for h in range(2):
                o_h = o_heads[h]
                ms = jnp.mean(o_h * o_h, axis=0, keepdims=True)
                normed.append(o_h * lax.rsqrt(ms + NORM_EPS) * gn[CHUNK * h:CHUNK * (h + 1)])
            or_ref[pl.ds(r0, LANES), :] = jnp.concatenate(o_heads, axis=0).T
            og_ref[pl.ds(r0, LANES), :] = jnp.concatenate(normed, axis=0).T * _silu(ga)
            return tuple(new_s)

        zero = jnp.zeros((CHUNK, CHUNK), F32)
        lax.fori_loop(0, nt, tile, (zero, zero))

    out = jax.ShapeDtypeStruct((bsz, t, HGRN_W), F32)
    return pl.pallas_call(
        body, grid=(bsz, 2),
        in_specs=[pl.BlockSpec((None, t, 512), lambda b, p: (b, 0, p)),
                  pl.BlockSpec((1, 128), lambda b, p: (0, p)),
                  pl.BlockSpec((128, 1), lambda b, p: (p, 0))],
        out_specs=[pl.BlockSpec((None, t, 128), lambda b, p: (b, 0, p)),
                   pl.BlockSpec((None, t, 128), lambda b, p: (b, 0, p))],
        out_shape=[out, out],
        compiler_params=_cparams(("parallel", "parallel")), name=name)(proj3, lbs_row, gn_col)


def _hgrn_bwd(proj3, o_raw, dmixed, lbs_row, gn_row, name):
    bsz, t, _ = proj3.shape
    nt = t // LANES
    nchunk = t // CHUNK

    def body(a_ref, or_ref, do_ref, lb_ref, gn_ref, da_ref, dgn_ref, dlb_ref, s_sc):
        lb = lb_ref[...]
        gn = gn_ref[...]
        umat = _chunk_cumsum_matrix()
        lane = _iota((1, LANES), 1)
        lane64 = lane % CHUNK
        half = lane < CHUNK

        def t_layout(a):
            qq, kk, lf, sg, sgn, fg = _hgrn_gates(a, lb)
            b_t = jnp.dot(lf.T, umat, precision=HI, preferred_element_type=F32)
            return qq.T, kk.T, a[:, 256:384].T, b_t, (sg, sgn, fg)

        def fwd_tile(i, carry):
            r0 = pl.multiple_of(i * LANES, LANES)
            q_t, k_t, v_t, b_t, _ = t_layout(a_ref[pl.ds(r0, LANES), :])
            new_s = []
            for h in range(2):
                s_h = carry[h]
                rs = slice(CHUNK * h, CHUNK * (h + 1))
                for c in range(2):
                    cs = slice(CHUNK * c, CHUNK * (c + 1))
                    s_sc[h, 2 * i + c] = s_h
                    b_ = b_t[rs, cs]
                    bl = b_[:, CHUNK - 1:CHUNK]
                    kt = (k_t[rs, cs] * jnp.exp(bl - b_)).astype(BF16)
                    s_h = jnp.exp(bl) * s_h + _dot(kt, v_t[rs, cs].astype(BF16), NT)
                new_s.append(s_h)
            return tuple(new_s)

        zero = jnp.zeros((CHUNK, CHUNK), F32)
        lax.fori_loop(0, nt, fwd_tile, (zero, zero))

        def half_mean(v):
            m0 = jnp.sum(jnp.where(half, v, 0.0), axis=1, keepdims=True) * (1.0 / CHUNK)
            m1 = jnp.sum(jnp.where(half, 0.0, v), axis=1, keepdims=True) * (1.0 / CHUNK)
            return jnp.where(half, m0, m1)

        def bwd_tile(k, carry):
            ds0, ds1, dgn_acc, dlb_acc = carry
            i = nt - 1 - k
            r0 = pl.multiple_of(i * LANES, LANES)
            a = a_ref[pl.ds(r0, LANES), :]
            qa, z, ga = a[:, 0:128], a[:, 128:256], a[:, 384:512]
            q_t, k_t, v_t, b_t, (sg, sgn, fg) = t_layout(a)
            oraw = or_ref[pl.ds(r0, LANES), :]
            dout = do_ref[pl.ds(r0, LANES), :]
            r = lax.rsqrt(half_mean(oraw * oraw) + NORM_EPS)
            xn = oraw * r
            dga = dout * (xn * gn) * _dsilu(ga)
            don = dout * _silu(ga)
            dgn_acc = dgn_acc + jnp.sum(don * xn, axis=0, keepdims=True)
            dxn = don * gn
            do_t = (r * (dxn - xn * half_mean(dxn * xn))).T
            new_ds, dq_h, dk_h, dv_h, db_h = [], [], [], [], []
            for h in range(2):
                ds_h = (ds0, ds1)[h]
                rs = slice(CHUNK * h, CHUNK * (h + 1))
                qh, kh, vh, bh, doh = q_t[rs], k_t[rs], v_t[rs], b_t[rs], do_t[rs]
                dq_c, dk_c, dv_c, dbl_c = [None, None], [None, None], [None, None], [None, None]
                for c in (1, 0):
                    cs = slice(CHUNK * c, CHUNK * (c + 1))
                    s_n = s_sc[h, 2 * i + c]
                    b_ = bh[:, cs]
                    eb = jnp.exp(b_)
                    bl = b_[:, CHUNK - 1:CHUNK]
                    ek = jnp.exp(bl - b_)
                    ebl = jnp.exp(bl)
                    qt, kt = qh[:, cs] * eb, kh[:, cs] * ek
                    do_c = doh[:, cs].astype(BF16)
                    dsb = ds_h.astype(BF16)
                    dv_c[c] = _dot(dsb, kt.astype(BF16), TN)
                    dkt = _dot(dsb, vh[:, cs].astype(BF16), NN)
                    dqt = _dot(s_n.astype(BF16), do_c, NN)
                    dbl_c[c] = jnp.sum(ds_h * s_n, axis=1, keepdims=True) * ebl + jnp.sum(dkt * kt, axis=1, keepdims=True)
                    dq_c[c], dk_c[c] = dqt * eb, dkt * ek
                    ds_h = ebl * ds_h + _dot(qt.astype(BF16), do_c, NT)
                new_ds.append(ds_h)

                att0 = jnp.sum(qh * kh, axis=0, keepdims=True)
                datt0 = jnp.sum(doh * vh, axis=0, keepdims=True)
                dqh = jnp.concatenate(dq_c, axis=1) + datt0 * kh
                dkh = jnp.concatenate(dk_c, axis=1) + datt0 * qh
                dvh = jnp.concatenate(dv_c, axis=1) + att0 * doh
                for dlt in range(1, CHUNK):
                    kr, br, vr = pltpu.roll(kh, dlt, 1), pltpu.roll(bh, dlt, 1), pltpu.roll(vh, dlt, 1)
                    e = jnp.where(lane64 >= dlt, jnp.exp(jnp.minimum(bh - br, 0.0)), 0.0)
                    qe = qh * e
                    att = jnp.sum(qe * kr, axis=0, keepdims=True)
                    datt = jnp.sum(doh * vr, axis=0, keepdims=True)
                    dqh = dqh + datt * (kr * e)
                    dkh = dkh + pltpu.roll(datt * qe, LANES - dlt, 1)
                    dvh = dvh + pltpu.roll(att * doh, LANES - dlt, 1)
                dbl = jnp.where(half, dbl_c[0], dbl_c[1])
                db_h.append(qh * dqh - kh * dkh + jnp.where(lane64 == CHUNK - 1, dbl, 0.0))
                dq_h.append(dqh)
                dk_h.append(dkh)
                dv_h.append(dvh)
            dqq = jnp.concatenate(dq_h, axis=0).T
            dkk = jnp.concatenate(dk_h, axis=0).T
            dvv = jnp.concatenate(dv_h, axis=0).T
            dlf = _dot(jnp.concatenate(db_h, axis=0), umat, NT, precision=HI).T
            dqa = dqq * _dsilu(qa)
            dfg = jnp.where(fg > TINY, dlf / fg, 0.0)
            dz = (dfg - dkk) * (1.0 - lb) * sg * sgn
            dlb_acc = dlb_acc + jnp.sum(dfg * (1.0 - sg) - dkk * sgn, axis=0, keepdims=True)
            da_ref[pl.ds(r0, LANES), :] = jnp.concatenate([dqa, dz, dvv, dga], axis=1)
            return new_ds[0], new_ds[1], dgn_acc, dlb_acc

        zrow = jnp.zeros((1, LANES), F32)
        _, _, dgn_acc, dlb_acc = lax.fori_loop(0, nt, bwd_tile, (zero, zero, zrow, zrow))
        dgn_ref[...] = jnp.broadcast_to(dgn_acc, (8, LANES))
        dlb_ref[...] = jnp.broadcast_to(dlb_acc, (8, LANES))

    rows = jax.ShapeDtypeStruct((bsz, 8, HGRN_W), F32)
    return pl.pallas_call(
        body, grid=(bsz, 2),
        in_specs=[pl.BlockSpec((None, t, 512), lambda b, p: (b, 0, p)),
                  pl.BlockSpec((None, t, 128), lambda b, p: (b, 0, p)),
                  pl.BlockSpec((None, t, 128), lambda b, p: (b, 0, p)),
                  pl.BlockSpec((1, 128), lambda b, p: (0, p)),
                  pl.BlockSpec((1, 128), lambda b, p: (0, p))],
        out_specs=[pl.BlockSpec((None, t, 512), lambda b, p: (b, 0, p)),
                   pl.BlockSpec((None, 8, 128), lambda b, p: (b, 0, p)),
                   pl.BlockSpec((None, 8, 128), lambda b, p: (b, 0, p))],
        out_shape=[jax.ShapeDtypeStruct((bsz, t, A_W), F32), rows, rows],
        scratch_shapes=[pltpu.VMEM((2, nchunk, CHUNK, CHUNK), F32)],
        compiler_params=_cparams(("parallel", "parallel")), name=name)(proj3, o_raw, dmixed, lbs_row, gn_row)


N_LEVELS = 6


def _hgrn_tables():
    t = np.arange(LANES)
    j = np.arange(LANES)[None, :]
    same_chunk = (t[:, None] // CHUNK) == (j // CHUNK)
    w = np.zeros((2 + N_LEVELS, LANES, LANES), np.float32)
    w[0] = same_chunk & (j <= t[:, None])
    w[1] = same_chunk & (j > t[:, None])
    maskf = np.zeros((N_LEVELS, LANES, LANES), np.float32)
    rightf = np.zeros((N_LEVELS, LANES, LANES), np.float32)
    for li in range(N_LEVELS):
        m = (CHUNK // 2) >> li
        start = t - (t % (2 * m))
        right = (t % (2 * m)) >= m
        first = np.where(right, start + m, t + 1)
        last = np.where(right, t, start + m - 1)
        w[2 + li] = (j >= first[:, None]) & (j <= last[:, None])
        maskf[li] = (t[:, None] // (2 * m)) == (j // (2 * m))
        rightf[li] = right[:, None]
    return jnp.asarray(w.reshape(-1, LANES), BF16), jnp.asarray(maskf), jnp.asarray(rightf)


def _split(x, n):
    parts = []
    for _ in range(n - 1):
        p = x.astype(BF16)
        parts.append(p)
        x = x - p.astype(F32)
    parts.append(x.astype(BF16))
    return parts


def _exact_dot(w, parts):
    acc = jnp.dot(w, parts[0], preferred_element_type=F32)
    for p in parts[1:]:
        acc = acc + jnp.dot(w, p, preferred_element_type=F32)
    return acc


def _head_sums(v, ones_blk, n=2):
    parts = _split(v, n)
    acc = jnp.dot(parts[0], ones_blk, preferred_element_type=F32)
    for p in parts[1:]:
        acc = acc + jnp.dot(p, ones_blk, preferred_element_type=F32)
    return acc


def _hgrn_consts():
    r, c = _iota((LANES, LANES), 0), _iota((LANES, LANES), 1)
    eye = r == c
    ones_blk = ((r // CHUNK) == (c // CHUNK)).astype(BF16)
    return eye, ones_blk, jnp.ones((CHUNK, LANES), BF16)


def _hgrn_levels(qq, kk, zall, mk_ref, rt_ref, d_att=None):
    att = [jnp.zeros((LANES, LANES), F32)] * 2
    dq = dk = db = jnp.zeros((LANES, LANES), F32)
    for li in range(N_LEVELS):
        e = jnp.exp(zall[(2 + li) * LANES:(3 + li) * LANES])
        rt = rt_ref[li]
        mk = mk_ref[li]
        qef, kef = e * rt, e * (1.0 - rt)
        qe, ke = (qq * qef).astype(BF16), (kk * kef).astype(BF16)
        dqs, dks = [], []
        for h in range(2):
            hs = slice(CHUNK * h, CHUNK * (h + 1))
            att[h] = att[h] + _dot(qe[:, hs], ke[:, hs], NT) * mk
            if d_att is not None:
                dam = (d_att[h] * mk).astype(BF16)
                dqs.append(jnp.dot(dam, ke[:, hs], preferred_element_type=F32))
                dks.append(_dot(dam, qe[:, hs], TN))
        if d_att is not None:
            dqe, dke = jnp.concatenate(dqs, axis=1), jnp.concatenate(dks, axis=1)
            dq = dq + dqe * qef
            dk = dk + dke * kef
            db = db + (dqe * qe.astype(F32) - dke * ke.astype(F32))
    return att, dq, dk, db


def _hgrn_fwd(proj3, lbs_row, gn_row, name):
    bsz, t, _ = proj3.shape
    nt = t // LANES
    w_all, maskf, rightf = _hgrn_tables()

    def body(a_ref, lb_ref, gn_ref, w_ref, mk_ref, rt_ref, og_ref, or_ref, st_ref):
        lb = lb_ref[...]
        gn = gn_ref[...]
        eye, ones_blk, ones_h = _hgrn_consts()

        def tile(i, carry):
            r0 = pl.multiple_of(i * LANES, LANES)
            a = a_ref[pl.ds(r0, LANES), :]
            qq, kk, lf, _, _, _ = _hgrn_gates(a, lb)
            va, ga = a[:, 256:384], a[:, 384:512]
            parts = _split(lf, 3)
            zall = _exact_dot(w_ref[...], parts)
            eb, ee = jnp.exp(zall[0:LANES]), jnp.exp(zall[LANES:2 * LANES])
            vb = va.astype(BF16)
            att, _, _, _ = _hgrn_levels(qq, kk, zall, mk_ref, rt_ref)
            qk = _split(qq * kk, 2)
            qeb, keb = (qq * eb).astype(BF16), (kk * ee).astype(BF16)
            new_s, o_heads = [], []
            for h in range(2):
                hs = slice(CHUNK * h, CHUNK * (h + 1))
                diag = _exact_dot_r(qk, hs, ones_h)
                a_h = att[h] + jnp.where(eye, diag, 0.0)
                o_h = jnp.dot(a_h.astype(BF16), vb[:, hs], preferred_element_type=F32)
                st = carry[h]
                chunks = []
                for c in range(2):
                    rc = slice(CHUNK * c, CHUNK * (c + 1))
                    st_ref[h, 2 * i + c] = st
                    chunks.append(o_h[rc] + _dot(qeb[rc, hs], st.astype(BF16), NT))
                    ebl = eb[CHUNK * (c + 1) - 1:CHUNK * (c + 1), hs]
                    st = st * ebl + _dot(vb[rc, hs], keb[rc, hs], TN)
                new_s.append(st)
                o_heads.append(jnp.concatenate(chunks, axis=0))
            o = jnp.concatenate(o_heads, axis=1)
            ms = _head_sums(o * o, ones_blk) * (1.0 / CHUNK)
            or_ref[pl.ds(r0, LANES), :] = o
            og_ref[pl.ds(r0, LANES), :] = o * lax.rsqrt(ms + NORM_EPS) * gn * _silu(ga)
            return tuple(new_s)

        zero = jnp.zeros((CHUNK, CHUNK), F32)
        lax.fori_loop(0, nt // 2, lambda i, carry: tile(2 * i + 1, tile(2 * i, carry)), (zero, zero))

    out = jax.ShapeDtypeStruct((bsz, t, HGRN_W), F32)
    row = pl.BlockSpec((1, 128), lambda b, p: (0, p))
    return pl.pallas_call(
        body, grid=(bsz, 2),
        in_specs=[pl.BlockSpec((None, t, 512), lambda b, p: (b, 0, p)), row, row,
                  pl.BlockSpec(w_all.shape, lambda b, p: (0, 0)),
                  pl.BlockSpec(maskf.shape, lambda b, p: (0, 0, 0)),
                  pl.BlockSpec(rightf.shape, lambda b, p: (0, 0, 0))],
        out_specs=[pl.BlockSpec((None, t, 128), lambda b, p: (b, 0, p)),
                   pl.BlockSpec((None, t, 128), lambda b, p: (b, 0, p)),
                   pl.BlockSpec((None, 2, t // CHUNK, CHUNK, CHUNK), lambda b, p: (b, p, 0, 0, 0))],
        out_shape=[out, out, jax.ShapeDtypeStruct((bsz, 4, t // CHUNK, CHUNK, CHUNK), F32)],
        compiler_params=_cparams(("parallel", "parallel")), name=name)(proj3, lbs_row, gn_row, w_all, maskf, rightf)


def _exact_dot_r(parts, hs, ones_h):
    acc = jnp.dot(parts[0][:, hs], ones_h, preferred_element_type=F32)
    for p in parts[1:]:
        acc = acc + jnp.dot(p[:, hs], ones_h, preferred_element_type=F32)
    return acc


def _hgrn_bwd(proj3, o_raw, dmixed, states, lbs_row, gn_row, name):
    bsz, t, _ = proj3.shape
    nt = t // LANES
    nchunk = t // CHUNK
    w_all, maskf, rightf = _hgrn_tables()

    def body(a_ref, or_ref, do_ref, s_sc, lb_ref, gn_ref, w_ref, mk_ref, rt_ref, da_ref, dgn_ref, dlb_ref):
        lb = lb_ref[...]
        gn = gn_ref[...]
        eye, ones_blk, ones_h = _hgrn_consts()
        r_i, c_i = _iota((LANES, LANES), 0), _iota((LANES, LANES), 1)
        suffix = ((c_i >= r_i) & ((r_i // CHUNK) == (c_i // CHUNK))).astype(BF16)
        row64 = _iota((LANES, CHUNK), 0)
        ones_t = jnp.ones((LANES, CHUNK), BF16)
        zero = jnp.zeros((CHUNK, CHUNK), F32)

        def bwd_tile(k, carry):
            dst0, dst1, dgn_acc, dlb_acc = carry
            i = nt - 1 - k
            r0 = pl.multiple_of(i * LANES, LANES)
            a = a_ref[pl.ds(r0, LANES), :]
            qa, ga = a[:, 0:128], a[:, 384:512]
            qq, kk, lf, sg, sgn, fg = _hgrn_gates(a, lb)
            parts = _split(lf, 3)
            zall = _exact_dot(w_ref[...], parts)
            eb, ee = jnp.exp(zall[0:LANES]), jnp.exp(zall[LANES:2 * LANES])
            vb = a[:, 256:384].astype(BF16)
            oraw = or_ref[pl.ds(r0, LANES), :]
            dout = do_ref[pl.ds(r0, LANES), :]
            r = lax.rsqrt(_head_sums(oraw * oraw, ones_blk) * (1.0 / CHUNK) + NORM_EPS)
            xn = oraw * r
            dga = dout * (xn * gn) * _dsilu(ga)
            don = dout * _silu(ga)
            dgn_acc = dgn_acc + jnp.sum(don * xn, axis=0, keepdims=True)
            dxn = don * gn
            do = r * (dxn - xn * (_head_sums(dxn * xn, ones_blk) * (1.0 / CHUNK)))
            dob = do.astype(BF16)
            d_att = [_dot(dob[:, CHUNK * h:CHUNK * (h + 1)], vb[:, CHUNK * h:CHUNK * (h + 1)], NT) for h in range(2)]
            att, dq, dk, db_lv = _hgrn_levels(qq, kk, zall, mk_ref, rt_ref, d_att)
            qk = _split(qq * kk, 2)
            qe_f, ke_f = qq * eb, kk * ee
            qeb, keb = qe_f.astype(BF16), ke_f.astype(BF16)
            new_ds, dq_h, dk_h, dv_h, dbl_h = [], [], [], [], []
            for h in range(2):
                hs = slice(CHUNK * h, CHUNK * (h + 1))
                a_h = att[h] + jnp.where(eye, _exact_dot_r(qk, hs, ones_h), 0.0)
                dv = _dot(a_h.astype(BF16), dob[:, hs], TN)
                ddiag = _exact_dot_r(_split(jnp.where(eye, d_att[h], 0.0), 2), slice(None), ones_t)
                dq_i = dq[:, hs] + ddiag * kk[:, hs]
                dk_i = dk[:, hs] + ddiag * qq[:, hs]
                dst = (dst0, dst1)[h]
                dq_c, dk_c, dv_c, dbl_c = [None, None], [None, None], [None, None], [None, None]
                for c in (1, 0):
                    rc = slice(CHUNK * c, CHUNK * (c + 1))
                    st_n = s_sc[h, 2 * i + c]
                    ebl = eb[CHUNK * (c + 1) - 1:CHUNK * (c + 1), hs]
                    dstb = dst.astype(BF16)
                    dv_c[c] = _dot(keb[rc, hs], dstb, NT)
                    dke = jnp.dot(vb[rc, hs], dstb, preferred_element_type=F32)
                    dqe = jnp.dot(dob[rc, hs], st_n.astype(BF16), preferred_element_type=F32)
                    dbl_c[c] = (jnp.sum(dst * st_n, axis=0, keepdims=True) * ebl
                                + jnp.sum(dke * ke_f[rc, hs], axis=0, keepdims=True))
                    dq_c[c], dk_c[c] = dqe * eb[rc, hs], dke * ee[rc, hs]
                    dst = dst * ebl + _dot(dob[rc, hs], qeb[rc, hs], TN)
                new_ds.append(dst)
                dq_x, dk_x = jnp.concatenate(dq_c, axis=0), jnp.concatenate(dk_c, axis=0)
                dq_h.append(dq_i + dq_x)
                dk_h.append(dk_i + dk_x)
                dv_h.append(dv + jnp.concatenate(dv_c, axis=0))
                dbl_h.append(qq[:, hs] * dq_x - kk[:, hs] * dk_x
                             + jnp.where(row64 == CHUNK - 1, dbl_c[0], 0.0) + jnp.where(row64 == LANES - 1, dbl_c[1], 0.0))
            dqq = jnp.concatenate(dq_h, axis=1)
            dkk = jnp.concatenate(dk_h, axis=1)
            dvv = jnp.concatenate(dv_h, axis=1)
            db = db_lv + jnp.concatenate(dbl_h, axis=1)
            dlf = _exact_dot(suffix, _split(db, 3))
            dqa = dqq * _dsilu(qa)
            dfg = jnp.where(fg > TINY, dlf / fg, 0.0)
            dz = (dfg - dkk) * (1.0 - lb) * sg * sgn
            dlb_acc = dlb_acc + jnp.sum(dfg * (1.0 - sg) - dkk * sgn, axis=0, keepdims=True)
            da_ref[pl.ds(r0, LANES), :] = jnp.concatenate([dqa, dz, dvv, dga], axis=1)
            return new_ds[0], new_ds[1], dgn_acc, dlb_acc

        zrow = jnp.zeros((1, LANES), F32)
        _, _, dgn_acc, dlb_acc = lax.fori_loop(
            0, nt // 2, lambda k, carry: bwd_tile(2 * k + 1, bwd_tile(2 * k, carry)), (zero, zero, zrow, zrow))
        dgn_ref[...] = jnp.broadcast_to(dgn_acc, (8, LANES))
        dlb_ref[...] = jnp.broadcast_to(dlb_acc, (8, LANES))

    rows = jax.ShapeDtypeStruct((bsz, 8, HGRN_W), F32)
    row = pl.BlockSpec((1, 128), lambda b, p: (0, p))
    blk = pl.BlockSpec((None, t, 128), lambda b, p: (b, 0, p))
    return pl.pallas_call(
        body, grid=(bsz, 2),
        in_specs=[pl.BlockSpec((None, t, 512), lambda b, p: (b, 0, p)), blk, blk,
                  pl.BlockSpec((None, 2, nchunk, CHUNK, CHUNK), lambda b, p: (b, p, 0, 0, 0)), row, row,
                  pl.BlockSpec(w_all.shape, lambda b, p: (0, 0)),
                  pl.BlockSpec(maskf.shape, lambda b, p: (0, 0, 0)),
                  pl.BlockSpec(rightf.shape, lambda b, p: (0, 0, 0))],
        out_specs=[pl.BlockSpec((None, t, 512), lambda b, p: (b, 0, p)),
                   pl.BlockSpec((None, 8, 128), lambda b, p: (b, 0, p)),
                   pl.BlockSpec((None, 8, 128), lambda b, p: (b, 0, p))],
        out_shape=[jax.ShapeDtypeStruct((bsz, t, A_W), F32), rows, rows],
        compiler_params=_cparams(("parallel", "parallel")), name=name)(
            proj3, o_raw, dmixed, states, lbs_row, gn_row, w_all, maskf, rightf)


def _pool_tt(t):
    return min(256, t)


def _window_select(s2, s4, s8, s16, lane):
    return jnp.where(lane < 64, s2, jnp.where(lane < 128, s4, jnp.where(lane < 192, s8, s16)))


def _pool_counts(t0, tt):
    lane = _iota((tt, POOL_W), 1)
    tpos = (_iota((tt, POOL_W), 0) + t0 + 1).astype(F32)
    win = jnp.where(lane < 64, 2.0, jnp.where(lane < 128, 4.0, jnp.where(lane < 192, 8.0, 16.0)))
    return 1.0 / jnp.minimum(tpos, win), lane


def _pooled_tile(upad_ref, i, tt):
    r0 = pl.multiple_of(i * tt, 8)
    cat = upad_ref[pl.ds(r0, tt + POOL_HALO), :]
    s2 = cat + pltpu.roll(cat, 1, 0)
    s4 = s2 + pltpu.roll(s2, 2, 0)
    s8 = s4 + pltpu.roll(s4, 4, 0)
    s16 = s8 + pltpu.roll(s8, 8, 0)
    inv, lane = _pool_counts(i * tt, tt)
    sel = _window_select(s2[POOL_HALO:], s4[POOL_HALO:], s8[POOL_HALO:], s16[POOL_HALO:], lane)
    return sel * inv - cat[POOL_HALO:], inv, lane


def _pool_fwd(proj3, wbd, scale_row, name):
    bsz, t, _ = proj3.shape
    tt = _pool_tt(t)

    def body(p_ref, w_ref, sc_ref, o_ref, upad):
        upad[0:POOL_HALO, :] = jnp.zeros((POOL_HALO, POOL_W), F32)
        upad[POOL_HALO:, :] = p_ref[:, 0:POOL_W]
        w = w_ref[...]
        sc = sc_ref[...]

        def tile(i, c):
            pooled, _, _ = _pooled_tile(upad, i, tt)
            r0 = pl.multiple_of(i * tt, 8)
            g = p_ref[pl.ds(r0, tt), POOL_W:2 * POOL_W]
            pre = jnp.dot(pooled.astype(BF16), w, preferred_element_type=F32)
            o_ref[pl.ds(r0, tt), :] = pre * sc * _silu(g)
            return c

        lax.fori_loop(0, t // tt, tile, 0)

    return pl.pallas_call(
        body, grid=(bsz,),
        in_specs=[pl.BlockSpec((None, t, 512), lambda b: (b, 0, B_BLK)),
                  pl.BlockSpec((POOL_W, POOL_W), lambda b: (0, 0)),
                  pl.BlockSpec((1, POOL_W), lambda b: (0, 0))],
        out_specs=pl.BlockSpec((None, t, POOL_W), lambda b: (b, 0, 0)),
        out_shape=jax.ShapeDtypeStruct((bsz, t, POOL_W), F32),
        scratch_shapes=[pltpu.VMEM((t + POOL_HALO, POOL_W), F32)],
        compiler_params=_cparams(("parallel",)), name=name)(proj3, wbd, scale_row)


def _pool_bwd(proj3, dmixed, wbd, scale_row, name):
    bsz, t, _ = proj3.shape
    tt = _pool_tt(t)

    def body(p_ref, do_ref, w_ref, sc_ref, db_ref, dsc_ref, dw_ref, upad, epad):
        upad[0:POOL_HALO, :] = jnp.zeros((POOL_HALO, POOL_W), F32)
        upad[POOL_HALO:, :] = p_ref[:, 0:POOL_W]
        epad[t:, :] = jnp.zeros((POOL_HALO, POOL_W), F32)
        w = w_ref[...]
        sc = sc_ref[...]

        def tile(i, carry):
            dsc_acc, dw_acc = carry
            pooled, inv, _ = _pooled_tile(upad, i, tt)
            r0 = pl.multiple_of(i * tt, 8)
            g = p_ref[pl.ds(r0, tt), POOL_W:2 * POOL_W]
            dout = do_ref[pl.ds(r0, tt), :]
            pb = pooled.astype(BF16)
            pre = jnp.dot(pb, w, preferred_element_type=F32)
            t1 = dout * _silu(g)
            dsc_acc = dsc_acc + jnp.sum(t1 * pre, axis=0, keepdims=True)
            dpre = (t1 * sc).astype(BF16)
            db_ref[pl.ds(r0, tt), POOL_W:2 * POOL_W] = dout * pre * sc * _dsilu(g)
            dw_acc = dw_acc + _dot(pb, dpre, TN)
            dpooled = _dot(dpre, w, NT)
            epad[pl.ds(r0, tt), :] = dpooled * inv
            return dsc_acc, dw_acc

        dsc_acc, dw_acc = lax.fori_loop(0, t // tt, tile, (jnp.zeros((1, POOL_W), F32), jnp.zeros((POOL_W, POOL_W), F32)))
        dsc_ref[...] = jnp.broadcast_to(dsc_acc, (8, POOL_W))
        dw_ref[...] = dw_acc

        def tile2(i, c):
            r0 = pl.multiple_of(i * tt, 8)
            n = tt + POOL_HALO
            cat = epad[pl.ds(r0, n), :]
            s2 = cat + pltpu.roll(cat, n - 1, 0)
            s4 = s2 + pltpu.roll(s2, n - 2, 0)
            s8 = s4 + pltpu.roll(s4, n - 4, 0)
            s16 = s8 + pltpu.roll(s8, n - 8, 0)
            inv, lane = _pool_counts(i * tt, tt)
            sel = _window_select(s2[:tt], s4[:tt], s8[:tt], s16[:tt], lane)
            db_ref[pl.ds(r0, tt), 0:POOL_W] = sel - cat[:tt] / inv
            return c

        lax.fori_loop(0, t // tt, tile2, 0)

    return pl.pallas_call(
        body, grid=(bsz,),
        in_specs=[pl.BlockSpec((None, t, 512), lambda b: (b, 0, B_BLK)),
                  pl.BlockSpec((None, t, POOL_W), lambda b: (b, 0, 1)),
                  pl.BlockSpec((POOL_W, POOL_W), lambda b: (0, 0)),
                  pl.BlockSpec((1, POOL_W), lambda b: (0, 0))],
        out_specs=[pl.BlockSpec((None, t, 512), lambda b: (b, 0, 0)),
                   pl.BlockSpec((None, 8, POOL_W), lambda b: (b, 0, 0)),
                   pl.BlockSpec((None, POOL_W, POOL_W), lambda b: (b, 0, 0))],
        out_shape=[jax.ShapeDtypeStruct((bsz, t, B_W), F32), jax.ShapeDtypeStruct((bsz, 8, POOL_W), F32),
                   jax.ShapeDtypeStruct((bsz, POOL_W, POOL_W), F32)],
        scratch_shapes=[pltpu.VMEM((t + POOL_HALO, POOL_W), F32), pltpu.VMEM((t + POOL_HALO, POOL_W), F32)],
        compiler_params=_cparams(("parallel",)), name=name)(proj3, dmixed, wbd, scale_row)


def _head_select_rows(hp):
    r, c = _iota((8, LANES), 0), _iota((8, LANES), 1)
    return ((r < 2) & (c == 2 * hp + r)).astype(F32)


def _foxgate_fwd(proj3, bias_row, name):
    bsz, t, _ = proj3.shape
    nt = t // LANES

    def body(f_ref, b_ref, cn_ref, ct_ref):
        bias = b_ref[...]
        i, j = _iota((LANES, LANES), 0), _iota((LANES, LANES), 1)
        lower = (j <= i).astype(BF16)
        spread = (_iota((LANES, FOX_W), 0) == _iota((LANES, FOX_W), 1) // 64).astype(BF16)
        select = [_head_select_rows(hp).astype(BF16) for hp in range(4)]
        offset = jnp.zeros((1, LANES), F32)
        for k in range(nt):
            rows = slice(k * LANES, (k + 1) * LANES)
            xg = f_ref[rows, :] + bias
            lf = jnp.minimum(xg, 0.0) - jnp.log(1.0 + jnp.exp(-jnp.abs(xg)))
            c = _exact_dot(lower, _split(lf, 3)) + offset
            offset = c[LANES - 1:LANES, :]
            parts = _split(c, 3)
            cn_ref[rows, :] = _head_sums(c, spread, 3)
            for hp in range(4):
                acc = _dot(select[hp], parts[0], NT)
                for p in parts[1:]:
                    acc = acc + _dot(select[hp], p, NT)
                ct_ref[hp, :, rows] = acc

    return pl.pallas_call(
        body, grid=(bsz,),
        in_specs=[pl.BlockSpec((None, t, 128), lambda b: (b, 0, F_BLK)), pl.BlockSpec((1, 128), lambda b: (0, 0))],
        out_specs=[pl.BlockSpec((None, t, FOX_W), lambda b: (b, 0, 0)),
                   pl.BlockSpec((None, 4, 8, t), lambda b: (b, 0, 0, 0))],
        out_shape=[jax.ShapeDtypeStruct((bsz, t, FOX_W), F32), jax.ShapeDtypeStruct((bsz, 4, 8, t), F32)],
        compiler_params=_cparams(("parallel",)), name=name)(proj3, bias_row)


def _foxgate_bwd(proj3, dc_nat, bias_row, name):
    bsz, t, _ = proj3.shape
    nt = t // LANES

    def body(f_ref, dc_ref, b_ref, df_ref, dbias_ref, run_sc):
        bias = b_ref[...]
        i, j = _iota((LANES, LANES), 0), _iota((LANES, LANES), 1)
        upper = (j >= i).astype(F32)
        valid = _iota((1, LANES), 1) < FOX_HEADS
        run_sc[...] = jnp.zeros((8, LANES), F32)
        dbias_ref[...] = jnp.zeros((8, LANES), F32)

        def tile(k, c):
            r0 = pl.multiple_of((nt - 1 - k) * LANES, LANES)
            dc = dc_ref[pl.ds(r0, LANES), :] + jnp.where(i == LANES - 1, run_sc[0:1, :], 0.0)
            dlf = jnp.dot(upper, dc, precision=HI, preferred_element_type=F32)
            xg = f_ref[pl.ds(r0, LANES), :] + bias
            df = jnp.where(valid, dlf * _sig(-xg), 0.0)
            df_ref[pl.ds(r0, LANES), :] = df
            run_sc[...] = dlf[0:8, :]
            dbias_ref[...] += jnp.sum(df, axis=0, keepdims=True)
            return c

        lax.fori_loop(0, nt, tile, 0)

    blk = pl.BlockSpec((None, t, 128), lambda b: (b, 0, 0))
    return pl.pallas_call(
        body, grid=(bsz,),
        in_specs=[pl.BlockSpec((None, t, 128), lambda b: (b, 0, F_BLK)), blk, pl.BlockSpec((1, 128), lambda b: (0, 0))],
        out_specs=[blk, pl.BlockSpec((None, 8, 128), lambda b: (b, 0, 0))],
        out_shape=[jax.ShapeDtypeStruct((bsz, t, F_W), F32), jax.ShapeDtypeStruct((bsz, 8, 128), F32)],
        scratch_shapes=[pltpu.VMEM((8, LANES), F32)],
        compiler_params=_cparams(("parallel",)), name=name)(proj3, dc_nat, bias_row)


def _fox_tile(t):
    return min(256, t)


def _fox_fwd(proj3, c_nat, c_t, name):
    bsz, t, _ = proj3.shape
    tq = _fox_tile(t)
    tk = min(2 * tq, t)
    nq = t // tq

    def body(q_ref, kv_ref, cn_ref, ct_ref, og_ref, or_ref, lse_ref):
        i = pl.program_id(2)
        qblk = q_ref[...]
        first = _iota((1, 128), 1) < 64
        qv = qblk[:, 0:128] * 0.125
        qm = [jnp.where(first, qv, 0.0).astype(BF16), jnp.where(first, 0.0, qv).astype(BF16)]
        cqs = [cn_ref[:, 0:1], cn_ref[:, 64:65]]
        rows = _iota((tq, tk), 0) + i * tq

        def scores(j):
            c0 = pl.multiple_of(j * tk, tk)
            kb = kv_ref[pl.ds(c0, tk), 128:256].astype(BF16)
            return tuple(_dot(qm[h], kb, NT) + (cqs[h] - ct_ref[h:h + 1, pl.ds(c0, tk)]) for h in range(2))

        def absorb(j, state, s01, masked):
            c0 = pl.multiple_of(j * tk, tk)
            vblk = kv_ref[pl.ds(c0, tk), 256:384]
            vx = [jnp.where(first, vblk, 1.0).astype(BF16), jnp.where(first, 1.0, vblk).astype(BF16)]
            new = []
            for h in range(2):
                m, acc, s = state[2 * h], state[2 * h + 1], s01[h]
                if masked:
                    s = jnp.where(rows >= _iota((tq, tk), 1) + j * tk, s, MASK_VALUE)
                m_new = jnp.maximum(m, jnp.max(s, axis=1, keepdims=True))
                p = jnp.exp(s - m_new).astype(BF16)
                new += [m_new, jnp.exp(m - m_new) * acc + jnp.dot(p, vx[h], preferred_element_type=F32)]
            return tuple(new)

        def kv_step(j, carry):
            ahead = scores(j + 1)
            return absorb(j, carry[:4], carry[4:], False) + ahead

        init = (jnp.full((tq, 1), MASK_VALUE, F32), jnp.zeros((tq, 128), F32)) * 2
        n_full = (i * tq) // tk
        carry = lax.fori_loop(0, n_full, kv_step, init + scores(0))
        m0, acc0, m1, acc1 = absorb(n_full, carry[:4], carry[4:], True)
        l0, l1 = pltpu.roll(acc0, 64, 1), pltpu.roll(acc1, 64, 1)
        o = jnp.where(first, acc0 / l0, acc1 / l1)
        or_ref[...] = o
        og_ref[...] = o * _silu(qblk[:, 384:512])
        lse_ref[...] = jnp.where(first, m0 + jnp.log(l0), m1 + jnp.log(l1))

    out = jax.ShapeDtypeStruct((bsz, t, FOX_W), F32)
    blk = pl.BlockSpec((None, tq, 128), lambda b, p, i: (b, i, p))
    return pl.pallas_call(
        body, grid=(bsz, 4, nq),
        in_specs=[pl.BlockSpec((None, tq, 512), lambda b, p, i: (b, i, C_BLK0 + p)),
                  pl.BlockSpec((None, t, 512), lambda b, p, i: (b, 0, C_BLK0 + p)),
                  blk,
                  pl.BlockSpec((None, None, 8, t), lambda b, p, i: (b, p, 0, 0))],
        out_specs=[blk, blk, blk],
        out_shape=[out, out, out],
        compiler_params=_cparams(("parallel", "parallel", "arbitrary")), name=name)(proj3, proj3, c_nat, c_t)


def _fox_bwd(proj3, o_raw, dmixed, lse, c_nat, c_t, name):
    bsz, t, _ = proj3.shape
    tq = _fox_tile(t)
    nq = t // tq
    tk = min(2 * tq, t)
    ratio = tk // tq

    def body(a_ref, or_ref, do_ref, lse_ref, cn_ref, ct_ref, dc_out, dct_out, drow_out, dq_sc, do_sc, dl_sc):
        def prep(i, c):
            r0 = pl.multiple_of(i * tq, tq)
            g = a_ref[pl.ds(r0, tq), 384:512]
            dout = do_ref[pl.ds(r0, tq), :]
            o = or_ref[pl.ds(r0, tq), :]
            dc_out[pl.ds(r0, tq), 384:512] = dout * o * _dsilu(g)
            do = dout * _silu(g)
            do_sc[pl.ds(r0, tq), :] = do
            prod = do * o
            d0 = jnp.sum(prod[:, 0:64], axis=1, keepdims=True)
            d1 = jnp.sum(prod[:, 64:128], axis=1, keepdims=True)
            dl_sc[pl.ds(r0, tq), :] = jnp.concatenate([jnp.broadcast_to(d0, (tq, 64)), jnp.broadcast_to(d1, (tq, 64))], axis=1)
            dq_sc[pl.ds(r0, tq), :] = jnp.zeros((tq, 128), F32)
            drow_out[pl.ds(r0, tq), :] = jnp.zeros((tq, 128), F32)
            return c

        lax.fori_loop(0, nq, prep, 0)
        dct_out[...] = jnp.zeros((8, t), F32)

        first = _iota((1, 128), 1) < 64

        def heads(v):
            return [jnp.where(first, v, 0.0).astype(BF16), jnp.where(first, 0.0, v).astype(BF16)]

        def kv_tile(j, c):
            c0 = pl.multiple_of(j * tk, tk)
            kb = a_ref[pl.ds(c0, tk), 128:256].astype(BF16)
            vb = a_ref[pl.ds(c0, tk), 256:384].astype(BF16)
            cks = [ct_ref[h:h + 1, pl.ds(c0, tk)] for h in range(2)]

            def exponents(i):
                r0 = pl.multiple_of(i * tq, tq)
                qm = heads(a_ref[pl.ds(r0, tq), 0:128] * 0.125)
                return tuple(_dot(qm[h], kb, NT) + (cn_ref[pl.ds(r0, tq), 64 * h:64 * h + 1] - cks[h])
                             - lse_ref[pl.ds(r0, tq), 64 * h:64 * h + 1] for h in range(2))

            def q_step(i, carry, logp, diagonal):
                dk, dv, dcol0, dcol1 = carry
                r0 = pl.multiple_of(i * tq, tq)
                causal = _iota((tq, tk), 0) + i * tq >= _iota((tq, tk), 1) + j * tk
                do = do_sc[pl.ds(r0, tq), :]
                qb, dob = (a_ref[pl.ds(r0, tq), 0:128] * 0.125).astype(BF16), do.astype(BF16)
                dom = heads(do)
                full, dcols, rsums = [], [], []
                for h in range(2):
                    dl_h = dl_sc[pl.ds(r0, tq), 64 * h:64 * h + 1]
                    p = jnp.exp(logp[h])
                    if diagonal:
                        p = jnp.where(causal, p, 0.0)
                    ds = p * (_dot(dom[h], vb, NT) - dl_h)
                    dsb = ds.astype(BF16)
                    full.append((_dot(p.astype(BF16), dob, TN), _dot(dsb, qb, TN),
                                 jnp.dot(dsb, kb, preferred_element_type=F32)))
                    dcols.append(jnp.sum(ds, axis=0, keepdims=True))
                    rsums.append(jnp.broadcast_to(jnp.sum(ds, axis=1, keepdims=True), (tq, 128)))
                dq_sc[pl.ds(r0, tq), :] += jnp.where(first, full[0][2], full[1][2]) * 0.125
                drow_out[pl.ds(r0, tq), :] += jnp.where(first, rsums[0], rsums[1])
                return (dk + jnp.where(first, full[0][1], full[1][1]), dv + jnp.where(first, full[0][0], full[1][0]),
                        dcol0 - dcols[0], dcol1 - dcols[1])

            carry = (jnp.zeros((tk, 128), F32), jnp.zeros((tk, 128), F32), jnp.zeros((1, tk), F32), jnp.zeros((1, tk), F32))
            for r in range(ratio):
                carry = q_step(ratio * j + r, carry, exponents(ratio * j + r), True)
            below = ratio * (j + 1)

            def step(i, state):
                ahead = exponents(jnp.minimum(i + 1, nq - 1))
                return q_step(i, state[:4], state[4:], False) + ahead

            dk, dv, dcol0, dcol1, _, _ = lax.fori_loop(below, nq, step, carry + exponents(jnp.minimum(below, nq - 1)))
            dct_out[0:1, pl.ds(c0, tk)] = dcol0
            dct_out[1:2, pl.ds(c0, tk)] = dcol1
            dc_out[pl.ds(c0, tk), 128:256] = dk
            dc_out[pl.ds(c0, tk), 256:384] = dv
            return c

        lax.fori_loop(0, t // tk, kv_tile, 0)
        dc_out[:, 0:128] = dq_sc[...]

    blk = pl.BlockSpec((None, t, 128), lambda b, p: (b, 0, p))
    return pl.pallas_call(
        body, grid=(bsz, 4),
        in_specs=[pl.BlockSpec((None, t, 512), lambda b, p: (b, 0, C_BLK0 + p)),
                  blk,
                  pl.BlockSpec((None, t, 128), lambda b, p: (b, 0, 4 + p)),
                  blk, blk,
                  pl.BlockSpec((None, None, 8, t), lambda b, p: (b, p, 0, 0))],
        out_specs=[pl.BlockSpec((None, t, 512), lambda b, p: (b, 0, p)),
                   pl.BlockSpec((None, None, 8, t), lambda b, p: (b, p, 0, 0)), blk],
        out_shape=[jax.ShapeDtypeStruct((bsz, t, C_W), F32), jax.ShapeDtypeStruct((bsz, 4, 8, t), F32),
                   jax.ShapeDtypeStruct((bsz, t, FOX_W), F32)],
        scratch_shapes=[pltpu.VMEM((t, 128), F32), pltpu.VMEM((t, 128), F32), pltpu.VMEM((t, 128), F32)],
        compiler_params=_cparams(("parallel", "parallel")), name=name)(proj3, o_raw, dmixed, lse, c_nat, c_t)


def _mix_tm(n):
    return min(512, n)


def _outproj_fwd(x2, oa, ob, oc, wo, g_row, name):
    n, d = x2.shape
    tm = _mix_tm(n)

    def body(x_ref, oa_ref, ob_ref, oc_ref, w_ref, g_ref, y_ref, xo_ref):
        y = (jnp.dot(oa_ref[...].astype(BF16), w_ref[0:256, :], preferred_element_type=F32)
             + jnp.dot(ob_ref[...].astype(BF16), w_ref[256:512, :], preferred_element_type=F32)
             + jnp.dot(oc_ref[...].astype(BF16), w_ref[512:1024, :], preferred_element_type=F32))
        y_ref[...] = y
        xo_ref[...] = x_ref[...] + y * _rstd(y) * g_ref[...]

    row = lambda w: pl.BlockSpec((tm, w), lambda i: (i, 0))
    out = jax.ShapeDtypeStruct((n, d), F32)
    return pl.pallas_call(
        body, grid=(n // tm,),
        in_specs=[row(d), row(256), row(256), row(512), pl.BlockSpec((d, d), lambda i: (0, 0)),
                  pl.BlockSpec((1, d), lambda i: (0, 0))],
        out_specs=[row(d), row(d)], out_shape=[out, out],
        compiler_params=_cparams(("parallel",)), name=name)(x2, oa, ob, oc, wo, g_row)


def _outproj_fwd_loss(x2, oa, ob, oc, wo, g_row, target2, name):
    n, d = x2.shape
    tm = _mix_tm(n)

    def body(x_ref, oa_ref, ob_ref, oc_ref, w_ref, g_ref, t_ref, y_ref, dx_ref, l_ref):
        y = (jnp.dot(oa_ref[...].astype(BF16), w_ref[0:256, :], preferred_element_type=F32)
             + jnp.dot(ob_ref[...].astype(BF16), w_ref[256:512, :], preferred_element_type=F32)
             + jnp.dot(oc_ref[...].astype(BF16), w_ref[512:1024, :], preferred_element_type=F32))
        y_ref[...] = y
        err = (x_ref[...] + y * _rstd(y) * g_ref[...]) - t_ref[...]
        dx_ref[...] = err * (1.0 / d)

        @pl.when(pl.program_id(0) == 0)
        def _():
            l_ref[...] = jnp.zeros((8, 128), F32)

        l_ref[...] += jnp.sum(err * err)

    row = lambda w: pl.BlockSpec((tm, w), lambda i: (i, 0))
    out = jax.ShapeDtypeStruct((n, d), F32)
    return pl.pallas_call(
        body, grid=(n // tm,),
        in_specs=[row(d), row(256), row(256), row(512), pl.BlockSpec((d, d), lambda i: (0, 0)),
                  pl.BlockSpec((1, d), lambda i: (0, 0)), row(d)],
        out_specs=[row(d), row(d), pl.BlockSpec((8, 128), lambda i: (0, 0))],
        out_shape=[out, out, jax.ShapeDtypeStruct((8, 128), F32)],
        compiler_params=_cparams(("arbitrary",)), name=name)(x2, oa, ob, oc, wo, g_row, target2)


def _outproj_bwd(dxo, y, oa, ob, oc, wo, g_row, name):
    n, d = dxo.shape
    tm = _mix_tm(n)

    def body(dx_ref, y_ref, oa_ref, ob_ref, oc_ref, w_ref, g_ref, dm_ref, dw_ref, dg_ref):
        @pl.when(pl.program_id(0) == 0)
        def _():
            dw_ref[...] = jnp.zeros((d, d), F32)
            dg_ref[...] = jnp.zeros((8, d), F32)

        yv, dx = y_ref[...], dx_ref[...]
        r = _rstd(yv)
        yn = yv * r
        dg_ref[...] += jnp.sum(dx * yn, axis=0, keepdims=True)
        dyn = dx * g_ref[...]
        dy = (r * (dyn - yn * jnp.mean(dyn * yn, axis=-1, keepdims=True))).astype(BF16)
        dm_ref[...] = _dot(dy, w_ref[...], NT)
        dw_ref[0:256, :] += _dot(oa_ref[...].astype(BF16), dy, TN)
        dw_ref[256:512, :] += _dot(ob_ref[...].astype(BF16), dy, TN)
        dw_ref[512:1024, :] += _dot(oc_ref[...].astype(BF16), dy, TN)

    row = lambda w: pl.BlockSpec((tm, w), lambda i: (i, 0))
    fixed = lambda r, c: pl.BlockSpec((r, c), lambda i: (0, 0))
    return pl.pallas_call(
        body, grid=(n // tm,),
        in_specs=[row(d), row(d), row(256), row(256), row(512), fixed(d, d), fixed(1, d)],
        out_specs=[row(d), fixed(d, d), fixed(8, d)],
        out_shape=[jax.ShapeDtypeStruct((n, d), F32), jax.ShapeDtypeStruct((d, d), F32), jax.ShapeDtypeStruct((8, d), F32)],
        compiler_params=_cparams(("arbitrary",)), name=name)(dxo, y, oa, ob, oc, wo, g_row)


_PIECES = ((0, A_W), (A_W, B_W), (A_W + B_W, C_W), (A_W + B_W + C_W, F_W))


def _inproj_bwd_x(x2, dxo, g_row, w_int, pieces, name):
    n, d = x2.shape
    tm = min(256, n)

    def body(x_ref, dxo_ref, g_ref, w_ref, da_ref, db_ref, dc_ref, df_ref, dx_ref, dg_ref):
        @pl.when(pl.program_id(0) == 0)
        def _():
            dg_ref[...] = jnp.zeros((8, d), F32)

        dh = jnp.zeros((tm, d), F32)
        for ref, (o, w) in zip((da_ref, db_ref, dc_ref, df_ref), _PIECES):
            dh = dh + _dot(ref[...].astype(BF16), w_ref[:, o:o + w], NT)
        x = x_ref[...]
        r = _rstd(x)
        xn = x * r
        dg_ref[...] += jnp.sum(dh * xn, axis=0, keepdims=True)
        dxn = dh * g_ref[...]
        dx_ref[...] = dxo_ref[...] + r * (dxn - xn * jnp.mean(dxn * xn, axis=-1, keepdims=True))

    row = lambda w: pl.BlockSpec((tm, w), lambda i: (i, 0))
    fixed = lambda r, c: pl.BlockSpec((r, c), lambda i: (0, 0))
    return pl.pallas_call(
        body, grid=(n // tm,),
        in_specs=[row(d), row(d), fixed(1, d), fixed(d, E_INT)] + [row(w) for _, w in _PIECES],
        out_specs=[row(d), fixed(8, d)],
        out_shape=[jax.ShapeDtypeStruct((n, d), F32), jax.ShapeDtypeStruct((8, d), F32)],
        compiler_params=_cparams(("arbitrary",)), name=name)(x2, dxo, g_row, w_int, *pieces)


def _inproj_bwd_w(x2, g_row, pieces, name):
    n, d = x2.shape
    tm = min(256, n)

    def body(x_ref, g_ref, da_ref, db_ref, dc_ref, df_ref, dw_ref):
        @pl.when(pl.program_id(0) == 0)
        def _():
            dw_ref[...] = jnp.zeros((d, E_INT), F32)

        x = x_ref[...]
        h = (x * _rstd(x) * g_ref[...]).astype(BF16)
        for ref, (o, w) in zip((da_ref, db_ref, dc_ref, df_ref), _PIECES):
            dw_ref[:, o:o + w] += _dot(h, ref[...].astype(BF16), TN)

    row = lambda w: pl.BlockSpec((tm, w), lambda i: (i, 0))
    return pl.pallas_call(
        body, grid=(n // tm,),
        in_specs=[row(d), pl.BlockSpec((1, d), lambda i: (0, 0))] + [row(w) for _, w in _PIECES],
        out_specs=pl.BlockSpec((d, E_INT), lambda i: (0, 0)),
        out_shape=jax.ShapeDtypeStruct((d, E_INT), F32),
        compiler_params=_cparams(("arbitrary",), vmem_mb=56), name=name)(x2, g_row, *pieces)


def _block_diag(pool_w_l):
    z = jnp.zeros((64, 64), pool_w_l.dtype)
    return jnp.concatenate(
        [jnp.concatenate([pool_w_l[g] if c == g else z for c in range(4)], axis=1) for g in range(4)], axis=0)


def _pad_lanes(v, width=128):
    return jnp.pad(v, ((0, 0),) * (v.ndim - 1) + ((0, width - v.shape[-1]),))


def _local_step(x, target, lower_bounds, pre_norm_g, w_in_int, hgrn_norm_g, fox_f_bias, pool_w, pool_scale,
                w_out_bf, post_norm_g, on_weight_grads):
    bsz, t, d = x.shape
    n = bsz * t
    lbs = _lbs_fwd(lower_bounds)
    saved = []
    xc = x.reshape(n, d)
    for l in range(DEPTH):
        proj = _inproj_fwd(xc, pre_norm_g[l:l + 1], w_in_int[l], f"inproj_fwd{l}").reshape(bsz, t, E_INT)
        wbd = _block_diag(pool_w[l]).astype(BF16)
        bias_row = _pad_lanes(fox_f_bias[l:l + 1])
        oa, oa_raw, states = _hgrn_fwd(proj, lbs[l:l + 1], hgrn_norm_g[l:l + 1], f"hgrn_fwd{l}")
        ob = _pool_fwd(proj, wbd, pool_scale[l:l + 1], f"pool_fwd{l}")
        c_nat, c_t = _foxgate_fwd(proj, bias_row, f"foxgate_fwd{l}")
        oc, oc_raw, lse = _fox_fwd(proj, c_nat, c_t, f"fox_fwd{l}")
        mixed = (oa.reshape(n, -1), ob.reshape(n, -1), oc.reshape(n, -1))
        if l < DEPTH - 1:
            y, xn = _outproj_fwd(xc, *mixed, w_out_bf[l], post_norm_g[l:l + 1], f"outproj_fwd{l}")
        else:
            y, dx, sq = _outproj_fwd_loss(xc, *mixed, w_out_bf[l], post_norm_g[l:l + 1], target.reshape(n, d),
                                          f"outproj_fwd{l}")
        saved.append((xc, proj, wbd, bias_row, oa, oa_raw, states, ob, oc, oc_raw, lse, c_nat, c_t, y))
        xc = xn
    g = {k: [None] * DEPTH for k in ("pre", "hgn", "bias", "pool_w", "pool_scale", "post", "lbs")}
    handed = [None] * DEPTH
    for l in reversed(range(DEPTH)):
        xin, proj, wbd, bias_row, oa, oa_raw, states, ob, oc, oc_raw, lse, c_nat, c_t, y = saved[l]
        dmix, d_w_out, dpost = _outproj_bwd(dx, y, oa.reshape(n, -1), ob.reshape(n, -1), oc.reshape(n, -1),
                                            w_out_bf[l], post_norm_g[l:l + 1], f"outproj_bwd{l}")
        g["post"][l] = dpost[0]
        dmix3 = dmix.reshape(bsz, t, d)
        d_c, dct, drow = _fox_bwd(proj, oc_raw, dmix3, lse, c_nat, c_t, f"fox_bwd{l}")
        dc_nat = _pad_lanes(dct[:, :, 0:2, :].reshape(bsz, FOX_HEADS, t).transpose(0, 2, 1)
                            + drow.reshape(bsz, t, FOX_HEADS, 64)[..., 0])
        d_f, dbias = _foxgate_bwd(proj, dc_nat, bias_row, f"foxgate_bwd{l}")
        g["bias"][l] = jnp.sum(dbias[:, 0, :FOX_HEADS], axis=0)
        d_b, dscale, dwbd = _pool_bwd(proj, dmix3, wbd, pool_scale[l:l + 1], f"pool_bwd{l}")
        g["pool_scale"][l] = jnp.sum(dscale[:, 0], axis=0)
        dwbd = jnp.sum(dwbd, axis=0)
        g["pool_w"][l] = jnp.stack([dwbd[64 * k:64 * (k + 1), 64 * k:64 * (k + 1)] for k in range(4)])
        d_a, dgn, dlb = _hgrn_bwd(proj, oa_raw, dmix3, states, lbs[l:l + 1], hgrn_norm_g[l:l + 1], f"hgrn_bwd{l}")
        g["hgn"][l] = jnp.sum(dgn[:, 0], axis=0)
        g["lbs"][l] = jnp.sum(dlb[:, 0], axis=0)
        pieces = [p.reshape(n, -1) for p in (d_a, d_b, d_c, d_f)]
        handed[l] = on_weight_grads(l, _inproj_bwd_w(xin, pre_norm_g[l:l + 1], pieces, f"inproj_bwd_w{l}"), d_w_out)
        dx, dpre = _inproj_bwd_x(xin, dx, pre_norm_g[l:l + 1], w_in_int[l], pieces, f"inproj_bwd_x{l}")
        g["pre"][l] = dpre[0]
    grads = {k: jnp.stack(v) for k, v in g.items()}
    return sq, dx.reshape(bsz, t, d), grads, handed


def _place():
    return lax.axis_index("x"), lax.axis_index("y"), lax.axis_index("c")


def _other_chips(x, y):
    return [(1 - x, y), (x, 1 - y), (1 - x, 1 - y)]


_ANY = pl.BlockSpec(memory_space=pl.ANY)


def _gather_body(handshake, n_arrays):
    def body(*refs):
        srcs, dsts = refs[:n_arrays], refs[n_arrays:2 * n_arrays]
        ici_send, ici_recv, d2d_send, d2d_recv, local_sems = refs[2 * n_arrays:]
        x, y, c = _place()
        if handshake:
            barrier = pltpu.get_barrier_semaphore()
            for peer in [(px, py, c) for px, py in _other_chips(x, y)] + [(x, y, 1 - c)]:
                pl.semaphore_signal(barrier, inc=1, device_id=peer, device_id_type=MESH)
            pl.semaphore_wait(barrier, 4)
        me = 2 * x + y
        pairs = list(zip(srcs, dsts))
        order = [(k, j) for k in range(3) for j in range(n_arrays)]
        mine = [pltpu.make_async_copy(src, dst.at[me], local_sems.at[j]) for j, (src, dst) in enumerate(pairs)]
        for cp in mine:
            cp.start()
        chips = _other_chips(x, y)
        sends = [pltpu.make_async_remote_copy(
            src_ref=pairs[j][0].at[c], dst_ref=pairs[j][1].at[me, c], send_sem=ici_send.at[n], recv_sem=ici_recv.at[n],
            device_id=(chips[k][0], chips[k][1], c), device_id_type=MESH) for n, (k, j) in enumerate(order)]
        for cp in sends:
            cp.start()
        passed = [pltpu.make_async_remote_copy(
            src_ref=pairs[j][1].at[2 * chips[k][0] + chips[k][1], c], dst_ref=pairs[j][1].at[2 * chips[k][0] + chips[k][1], c],
            send_sem=d2d_send.at[n], recv_sem=d2d_recv.at[n], device_id=(x, y, 1 - c), device_id_type=MESH)
            for n, (k, j) in enumerate(order)]
        for n, (k, j) in enumerate(order):
            px, py = chips[k]
            src, dst = pairs[j]
            pltpu.make_async_remote_copy(
                src_ref=src.at[c], dst_ref=dst.at[2 * px + py, c], send_sem=ici_send.at[n], recv_sem=ici_recv.at[n],
                device_id=(px, py, c), device_id_type=MESH).wait_recv()
            passed[n].start()
        for n, (k, j) in enumerate(order):
            px, py = chips[k]
            src, dst = pairs[j]
            pltpu.make_async_remote_copy(
                src_ref=dst.at[2 * px + py, 1 - c], dst_ref=dst.at[2 * px + py, 1 - c], send_sem=d2d_send.at[n],
                recv_sem=d2d_recv.at[n], device_id=(x, y, 1 - c), device_id_type=MESH).wait_recv()
        for cp in sends + passed:
            cp.wait_send()
        for cp in mine:
            cp.wait()

    return body


def _gather_sems(n_arrays):
    return [pltpu.SemaphoreType.DMA((3 * n_arrays,))] * 4 + [pltpu.SemaphoreType.DMA((n_arrays,))]


def _gathered(a):
    return jax.ShapeDtypeStruct((N_CHIPS,) + a.shape, a.dtype)


def _gather_weights(arrays):
    n = len(arrays)
    return pl.pallas_call(
        _gather_body(False, n), in_specs=[_ANY] * n, out_specs=[_ANY] * n, out_shape=[_gathered(a) for a in arrays],
        scratch_shapes=_gather_sems(n), name="gather_weights")(*arrays)


def _gather_weights_beside(arrays):
    hbm = pltpu.MemorySpace.HBM
    n = len(arrays)
    srcs = [jax.new_ref(a, memory_space=hbm) for a in arrays]
    dsts = [jax.empty_ref(_gathered(a), memory_space=hbm) for a in arrays]
    body = _gather_body(True, n)

    @pl.kernel(mesh=plsc.ScalarSubcoreMesh(axis_name="sequencer", num_cores=1), name="gather_weights_beside",
               scratch_types=_gather_sems(n), compiler_params=pltpu.CompilerParams(collective_id=1))
    def launch(*sems):
        body(*srcs, *dsts, *sems)

    launch()
    return [d[...] for d in dsts]


def _swap_with_sibling(parts, name):
    k = len(parts)

    def body(*refs):
        src, dst = refs[:k], refs[k:2 * k]
        send_sems, recv_sems = refs[2 * k:]
        x, y, c = _place()
        cps = [pltpu.make_async_remote_copy(src_ref=src[j], dst_ref=dst[j], send_sem=send_sems.at[j], recv_sem=recv_sems.at[j],
                                            device_id=(x, y, 1 - c), device_id_type=MESH) for j in range(k)]
        for cp in cps:
            cp.start()
        for cp in cps:
            cp.wait()

    return pl.pallas_call(
        body, in_specs=[_ANY] * k, out_specs=[_ANY] * k,
        out_shape=[jax.ShapeDtypeStruct(p.shape, p.dtype) for p in parts],
        scratch_shapes=[pltpu.SemaphoreType.DMA((k,)), pltpu.SemaphoreType.DMA((k,))], name=name)(*parts)


N_PEERS = 7


def _grad_exchange_body():
    def body(pin_ref, pout_ref, lin_ref, lout_ref, send_sems, recv_sems):
        x, y, c = _place()
        barrier = pltpu.get_barrier_semaphore()
        for k in range(1, N_PEERS + 1):
            peer = (x ^ ((k >> 2) & 1), y ^ ((k >> 1) & 1), c ^ (k & 1))
            pl.semaphore_signal(barrier, inc=1, device_id=peer, device_id_type=MESH)
        pl.semaphore_wait(barrier, N_PEERS)
        me = 2 * x + y
        pairs = ((pin_ref, lin_ref), (pout_ref, lout_ref))
        cps = []
        for k, (px, py) in enumerate(_other_chips(x, y)):
            for r in range(2):
                for j, (src, dst) in enumerate(pairs):
                    cps.append(pltpu.make_async_remote_copy(
                        src_ref=src.at[2 * px + py, r], dst_ref=dst.at[2 * k + c], send_sem=send_sems.at[2 * (2 * k + r) + j],
                        recv_sem=recv_sems.at[2 * (2 * k + c) + j], device_id=(px, py, r), device_id_type=MESH))
        for j, (src, dst) in enumerate(pairs):
            cps.append(pltpu.make_async_remote_copy(
                src_ref=src.at[me, 1 - c], dst_ref=dst.at[N_PEERS - 1], send_sem=send_sems.at[2 * (N_PEERS - 1) + j],
                recv_sem=recv_sems.at[2 * (N_PEERS - 1) + j], device_id=(x, y, 1 - c), device_id_type=MESH))
        for cp in cps:
            cp.start()
        for s in range(N_PEERS):
            for j, (src, dst) in enumerate(pairs):
                pltpu.make_async_remote_copy(
                    src_ref=src.at[0, 0], dst_ref=dst.at[s], send_sem=send_sems.at[2 * s + j], recv_sem=recv_sems.at[2 * s + j],
                    device_id=(x, y, 1 - c), device_id_type=MESH).wait_recv()
        for cp in cps:
            cp.wait_send()

    return body


_EXCHANGE_SEMS = [pltpu.SemaphoreType.DMA((2 * N_PEERS,))] * 2


def _landing(p):
    return jax.ShapeDtypeStruct((N_PEERS,) + p.shape[2:], p.dtype)


def _grad_exchange_beside(pin, pout, name, collective_id):
    hbm = pltpu.MemorySpace.HBM
    pin_ref, pout_ref = jax.new_ref(pin, memory_space=hbm), jax.new_ref(pout, memory_space=hbm)
    lin_ref, lout_ref = jax.empty_ref(_landing(pin), memory_space=hbm), jax.empty_ref(_landing(pout), memory_space=hbm)
    body = _grad_exchange_body()

    @pl.kernel(mesh=plsc.ScalarSubcoreMesh(axis_name="sequencer", num_cores=1), name=name,
               scratch_types=_EXCHANGE_SEMS, compiler_params=pltpu.CompilerParams(collective_id=collective_id))
    def launch(send_sems, recv_sems):
        body(pin_ref, pout_ref, lin_ref, lout_ref, send_sems, recv_sems)

    launch()
    return lin_ref[...], lout_ref[...]


def _add_n(parts, name, with_bf16=False):
    r, c = parts[0].shape
    tr = 256 if r % 256 == 0 else r
    n = len(parts)

    def body(*refs):
        acc = refs[0][...].astype(F32)
        for ref in refs[1:n]:
            acc = acc + ref[...].astype(F32)
        refs[n][...] = acc
        if with_bf16:
            refs[n + 1][...] = acc.astype(BF16)

    blk = pl.BlockSpec((tr, c), lambda i: (i, 0))
    outs = [jax.ShapeDtypeStruct((r, c), F32)] + ([jax.ShapeDtypeStruct((r, c), BF16)] if with_bf16 else [])
    res = pl.pallas_call(
        body, grid=(r // tr,), in_specs=[blk] * n, out_specs=[blk] * len(outs),
        out_shape=outs, compiler_params=_cparams(("parallel",)), name=name)(*parts)
    return res if with_bf16 else res[0]


def _all_reduce_small(packet):
    r, w = packet.shape

    def body(p_ref, o_ref, buf, send_sems, recv_sems):
        x, y, c = _place()
        me = 4 * x + 2 * y + c
        buf[me] = p_ref[...]
        peers = []
        for k in range(1, 8):
            fx, fy, fc = (k >> 2) & 1, (k >> 1) & 1, k & 1
            peers.append((x ^ fx, y ^ fy, c ^ fc))
        cps = [pltpu.make_async_remote_copy(src_ref=p_ref, dst_ref=buf.at[me], send_sem=send_sems.at[k], recv_sem=recv_sems.at[k],
                                            device_id=peer, device_id_type=MESH) for k, peer in enumerate(peers)]
        for cp in cps:
            cp.start()
        for k, (px, py, pc) in enumerate(peers):
            pltpu.make_async_remote_copy(src_ref=p_ref, dst_ref=buf.at[4 * px + 2 * py + pc], send_sem=send_sems.at[k],
                                         recv_sem=recv_sems.at[k], device_id=(px, py, pc), device_id_type=MESH).wait_recv()
        for cp in cps:
            cp.wait_send()
        acc = buf[0]
        for k in range(1, 8):
            acc = acc + buf[k]
        o_ref[...] = acc

    vm = pl.BlockSpec(memory_space=pltpu.VMEM)
    return pl.pallas_call(
        body, in_specs=[vm], out_specs=vm, out_shape=jax.ShapeDtypeStruct((r, w), F32),
        scratch_shapes=[pltpu.VMEM((8, r, w), F32), pltpu.SemaphoreType.DMA((7,)), pltpu.SemaphoreType.DMA((7,))],
        name="all_reduce_small")(packet)


def _adamw_math(w, g, m, v):
    m = ADAM_B1 * m + (1.0 - ADAM_B1) * g
    v = ADAM_B2 * v + (1.0 - ADAM_B2) * (g * g)
    m_hat = m / (1.0 - ADAM_B1 ** ADAM_STEP)
    v_hat = v / (1.0 - ADAM_B2 ** ADAM_STEP)
    return -ADAM_LR * (m_hat / (jnp.sqrt(v_hat) + ADAM_EPS) + ADAM_WD * w), m, v


def _adamw(w, g, m, v, name):
    nl, r, c = w.shape
    tr = 256 if r % 256 == 0 else r

    def body(w_ref, g_ref, m_ref, v_ref, d_ref, mo_ref, vo_ref):
        d_ref[...], mo_ref[...], vo_ref[...] = _adamw_math(w_ref[...], g_ref[...], m_ref[...], v_ref[...])

    blk = pl.BlockSpec((None, tr, c), lambda l, i: (l, i, 0))
    out = jax.ShapeDtypeStruct(w.shape, F32)
    return pl.pallas_call(
        body, grid=(nl, r // tr), in_specs=[blk] * 4, out_specs=[blk] * 3, out_shape=[out] * 3,
        compiler_params=_cparams(("parallel", "parallel")), name=name)(w, g, m, v)


def _small_update(gsum, lower_bounds, wpack, mpack, vpack):
    r, w = gsum.shape
    lb_rows = DEPTH * HGRN_W // 128

    def body(g_ref, a_ref, w_ref, m_ref, v_ref, go_ref, d_ref, mo_ref, vo_ref):
        a = a_ref[...]
        a0, a1 = a[0:1], a[1:2]
        mx = jnp.maximum(a0, a1)
        e0, e1 = jnp.exp(a0 - mx), jnp.exp(a1 - mx)
        p0, p1 = e0 / (e0 + e1), e1 / (e0 + e1)
        g = g_ref[...]
        half = lb_rows // 2
        dl0 = jnp.concatenate([g[k:k + 1] for k in range(half)], axis=1)
        dl1 = jnp.concatenate([g[half + k:half + k + 1] for k in range(half)], axis=1)
        dp0 = (dl0 + dl1) - (dl0 + dl1)
        dp1 = dl1
        inner = p0 * dp0 + p1 * dp1
        da0, da1 = p0 * (dp0 - inner), p1 * (dp1 - inner)
        rows = [da0[:, 128 * k:128 * (k + 1)] for k in range(half)] + [da1[:, 128 * k:128 * (k + 1)] for k in range(half)]
        gfull = jnp.concatenate(rows + [g[lb_rows:]], axis=0)
        go_ref[...] = gfull
        d_ref[...], mo_ref[...], vo_ref[...] = _adamw_math(w_ref[...], gfull, m_ref[...], v_ref[...])

    vm = pl.BlockSpec(memory_space=pltpu.VMEM)
    out = jax.ShapeDtypeStruct((r, w), F32)
    return pl.pallas_call(body, in_specs=[vm] * 5, out_specs=[vm] * 4, out_shape=[out] * 4, name="small_update")(
        gsum, lower_bounds, wpack, mpack, vpack)


_SMALL = ("lower_bounds", "pre_norm_g", "hgrn_norm_g", "fox_f_bias", "pool_w", "pool_scale", "post_norm_g")


def _pack(parts):
    rows = []
    for k in _SMALL:
        f = parts[k].reshape(-1)
        pad = (-f.shape[0]) % (8 * 128)
        rows.append(jnp.pad(f, (0, pad)).reshape(-1, 128))
    rows.append(jnp.zeros((8, 128), F32))
    return jnp.concatenate(rows, axis=0)


def _unpack(pack, like):
    out, r = {}, 0
    for k in _SMALL:
        size = int(np.prod(like[k].shape))
        nr = -(-size // (8 * 128)) * 8
        out[k] = pack[r:r + nr].reshape(-1)[:size].reshape(like[k].shape)
        r += nr
    return out, r


def kernel(x, lower_bounds, pre_norm_g, w_in, hgrn_norm_g, fox_f_bias, pool_w, pool_scale, w_out, post_norm_g, loss_target, m_lower_bounds, m_pre_norm_g, m_w_in, m_hgrn_norm_g, m_fox_f_bias, m_pool_w, m_pool_scale, m_w_out, m_post_norm_g, v_lower_bounds, v_pre_norm_g, v_w_in, v_hgrn_norm_g, v_fox_f_bias, v_pool_w, v_pool_scale, v_w_out, v_post_norm_g):
    cx, cy, cc = _place()
    chip = 2 * cx + cy

    halves = lambda w, l: w[l].reshape(2, w.shape[1] // 2, w.shape[2]).astype(BF16)
    needed_first = _gather_weights([halves(w_in, 0)])
    needed_first, later = lax.optimization_barrier((needed_first, [halves(w_out, 0), halves(w_in, 1), halves(w_out, 1)]))
    later = _gather_weights_beside(later)
    w_in_int = [_internal_from_shards([a[q].reshape(D_MODEL, SHARD_W) for q in range(N_CHIPS)]) for a in (needed_first[0], later[1])]
    w_out_full = [a.reshape(D_MODEL, D_MODEL) for a in (later[0], later[2])]

    def on_weight_grads(l, d_w_in, d_w_out):
        pin = _shards_from_internal(d_w_in).reshape(N_CHIPS, 2, D_MODEL // 2, SHARD_W)
        pout = d_w_out.reshape(N_CHIPS, 2, D_MODEL // (2 * N_CHIPS), D_MODEL)
        own = [lax.dynamic_index_in_dim(lax.dynamic_index_in_dim(p, chip, 0, False), cc, 0, False) for p in (pin, pout)]
        return own, _grad_exchange_beside(pin.astype(BF16), pout.astype(BF16), f"grad_exchange{l}", 2 + l)

    sq, grad_x, g, handed = _local_step(x, loss_target, lower_bounds, pre_norm_g, w_in_int, hgrn_norm_g, fox_f_bias,
                                        pool_w, pool_scale, w_out_full, post_norm_g, on_weight_grads)
    first = cc == 0

    def finish(l, own, landed):
        halves_l = [_add_n([o] + [t[s] for s in range(N_PEERS)], f"grad_sum{l}_{j}") for j, (o, t) in enumerate(zip(own, landed))]
        others = _swap_with_sibling(halves_l, f"grad_swap{l}")
        g_in, g_out = [jnp.where(first, jnp.concatenate([h, o], axis=0), jnp.concatenate([o, h], axis=0))[None]
                       for h, o in zip(halves_l, others)]
        return (g_in, g_out, _adamw(w_in[l:l + 1], g_in, m_w_in[l:l + 1], v_w_in[l:l + 1], f"adamw_w_in{l}"),
                _adamw(w_out[l:l + 1], g_out, m_w_out[l:l + 1], v_w_out[l:l + 1], f"adamw_w_out{l}"))

    grad_x, last = lax.optimization_barrier((grad_x, handed[1]))
    done = [None, finish(1, *last)]

    small = {"lower_bounds": g["lbs"], "pre_norm_g": g["pre"], "hgrn_norm_g": g["hgn"], "fox_f_bias": g["bias"],
             "pool_w": g["pool_w"], "pool_scale": g["pool_scale"], "post_norm_g": g["post"]}
    packet = _pack(small)
    nrows = packet.shape[0]
    packet = packet.at[nrows - 1].set(sq[0])
    gsum = _all_reduce_small(packet)
    loss = gsum[nrows - 1, 0] * (0.5 / D_MODEL)

    weights = {"lower_bounds": lower_bounds, "pre_norm_g": pre_norm_g, "hgrn_norm_g": hgrn_norm_g,
               "fox_f_bias": fox_f_bias, "pool_w": pool_w, "pool_scale": pool_scale, "post_norm_g": post_norm_g}
    moments_m = {"lower_bounds": m_lower_bounds, "pre_norm_g": m_pre_norm_g, "hgrn_norm_g": m_hgrn_norm_g,
                 "fox_f_bias": m_fox_f_bias, "pool_w": m_pool_w, "pool_scale": m_pool_scale, "post_norm_g": m_post_norm_g}
    moments_v = {"lower_bounds": v_lower_bounds, "pre_norm_g": v_pre_norm_g, "hgrn_norm_g": v_hgrn_norm_g,
                 "fox_f_bias": v_fox_f_bias, "pool_w": v_pool_w, "pool_scale": v_pool_scale, "post_norm_g": v_post_norm_g}
    gp, dp, mp, vp = _small_update(gsum, lower_bounds, _pack(weights), _pack(moments_m), _pack(moments_v))
    gs, _ = _unpack(gp, weights)
    ds, _ = _unpack(dp, weights)
    ms, _ = _unpack(mp, weights)
    vs, _ = _unpack(vp, weights)

    first_layer, _ = lax.optimization_barrier((handed[0], (done[1], gp, dp, mp, vp)))
    done[0] = finish(0, *first_layer)
    both = lambda pick: jnp.concatenate([pick(done[l]) for l in range(DEPTH)], axis=0)
    grad_w_in, grad_w_out = both(lambda r: r[0]), both(lambda r: r[1])
    d_in, m_in, v_in = [both(lambda r, k=k: r[2][k]) for k in range(3)]
    d_out, m_out, v_out = [both(lambda r, k=k: r[3][k]) for k in range(3)]

    def ordered(s, big_in, big_out):
        return (s["lower_bounds"], s["pre_norm_g"], big_in, s["hgrn_norm_g"], s["fox_f_bias"], s["pool_w"],
                s["pool_scale"], big_out, s["post_norm_g"])

    return (loss, grad_x, *ordered(gs, grad_w_in, grad_w_out), *ordered(ds, d_in, d_out),
            *ordered(ms, m_in, m_out), *ordered(vs, v_in, v_out))
```

```python
import functools

import numpy as np
import jax
import jax.numpy as jnp
from jax import lax
from jax.experimental import pallas as pl
from jax.experimental.pallas import tpu as pltpu
from jax.experimental.pallas import tpu_sc as plsc

F32 = jnp.float32
BF16 = jnp.bfloat16
HI = lax.Precision.HIGHEST
MESH = pl.DeviceIdType.MESH

NORM_EPS = 1e-6
MASK_VALUE = -1e30
TINY = 1e-30
ADAM_LR, ADAM_B1, ADAM_B2, ADAM_EPS, ADAM_WD, ADAM_STEP = 0.001, 0.9, 0.999, 1e-08, 0.01, 10

D_MODEL = 1024
DEPTH = 2
N_CHIPS = 4
CHUNK = 64
LANES = 128
HGRN_W, POOL_W, FOX_W, FOX_HEADS = 256, 256, 512, 8
POOL_WINDOWS = (2, 4, 8, 16)
POOL_HALO = 16
IN_WIDTH = 3592
SHARD_W = IN_WIDTH // N_CHIPS
A_W, B_W, C_W, F_W = 1024, 512, 2048, 128
E_INT = A_W + B_W + C_W + F_W
B_BLK = A_W // 512
C_BLK0 = (A_W + B_W) // 512
F_BLK = (A_W + B_W + C_W) // 128


def _segments():
    segs = []
    for hp in range(2):
        for part in range(4):
            segs.append((part * 256 + hp * 128, 128))
    segs.append((1024, 256))
    segs.append((1280, 256))
    for hp in range(4):
        for part in range(4):
            segs.append((1536 + part * 512 + hp * 128, 128))
    segs.append((3584, 8))
    return segs


_SEGS = _segments()


def _to_internal(w):
    parts = [w[..., s:s + n] for s, n in _SEGS]
    parts.append(jnp.zeros(w.shape[:-1] + (E_INT - IN_WIDTH,), w.dtype))
    return jnp.concatenate(parts, axis=-1)


def _to_original(w):
    offs, o = [], 0
    for s, n in _SEGS:
        offs.append((s, o, n))
        o += n
    parts = [w[..., o:o + n] for s, o, n in sorted(offs)]
    return jnp.concatenate(parts, axis=-1)


def _internal_from_shards(shards):
    parts = []
    for s, n in _SEGS:
        while n > 0:
            q, r = divmod(s, SHARD_W)
            take = min(n, SHARD_W - r)
            parts.append(shards[q][..., r:r + take])
            s, n = s + take, n - take
    parts.append(jnp.zeros(shards[0].shape[:-1] + (E_INT - IN_WIDTH,), shards[0].dtype))
    return jnp.concatenate(parts, axis=-1)


def _shards_from_internal(w):
    offs, o = [], 0
    for s, n in _SEGS:
        offs.append((s, o, n))
        o += n
    blocks = []
    for q in range(N_CHIPS):
        lo, hi = SHARD_W * q, SHARD_W * (q + 1)
        parts = [w[..., o + max(lo, s) - s:o + min(hi, s + n) - s] for s, o, n in sorted(offs) if s < hi and s + n > lo]
        blocks.append(jnp.concatenate(parts, axis=-1))
    return jnp.stack(blocks)


def _cparams(sem=None, vmem_mb=48):
    kw = dict(vmem_limit_bytes=vmem_mb * 1024 * 1024)
    if sem is not None:
        kw["dimension_semantics"] = sem
    return pltpu.CompilerParams(**kw)


def _sig(x):
    return 1.0 / (1.0 + jnp.exp(-x))


def _silu(x):
    return x * _sig(x)


def _dsilu(x):
    s = _sig(x)
    return s * (1.0 + x * (1.0 - s))


def _rstd(x):
    return lax.rsqrt(jnp.mean(x * x, axis=-1, keepdims=True) + NORM_EPS)


def _dot(a, b, dims, **kw):
    return lax.dot_general(a, b, (dims, ((), ())), preferred_element_type=F32, **kw)


NN = ((1,), (0,))
NT = ((1,), (1,))
TN = ((0,), (0,))


def _iota(shape, dim):
    return lax.broadcasted_iota(jnp.int32, shape, dim)


def _lbs_fwd(lower_bounds):
    def body(a_ref, o_ref):
        a = a_ref[...]
        a0, a1 = a[0:1], a[1:2]
        m = jnp.maximum(a0, a1)
        e0, e1 = jnp.exp(a0 - m), jnp.exp(a1 - m)
        p0, p1 = e0 / (e0 + e1), e1 / (e0 + e1)
        o_ref[...] = jnp.concatenate([p0 - p0, (p0 + p1) - p0], axis=0)

    return pl.pallas_call(body, out_shape=jax.ShapeDtypeStruct(lower_bounds.shape, F32), name="lbs_fwd")(lower_bounds)


def _inproj_fwd(x2, g_row, w_int, name):
    n, d = x2.shape
    e = w_int.shape[1]
    tm = min(256, n)

    def body(x_ref, g_ref, w_ref, o_ref):
        x = x_ref[...]
        h = (x * _rstd(x) * g_ref[...]).astype(BF16)
        o_ref[...] = jnp.dot(h, w_ref[...], preferred_element_type=F32)

    return pl.pallas_call(
        body, grid=(n // tm,),
        in_specs=[pl.BlockSpec((tm, d), lambda i: (i, 0)), pl.BlockSpec((1, d), lambda i: (0, 0)),
                  pl.BlockSpec((d, e), lambda i: (0, 0))],
        out_specs=pl.BlockSpec((tm, e), lambda i: (i, 0)),
        out_shape=jax.ShapeDtypeStruct((n, e), F32),
        compiler_params=_cparams(("parallel",)), name=name)(x2, g_row, w_int)


def _chunk_cumsum_matrix():
    i, j = _iota((LANES, LANES), 0), _iota((LANES, LANES), 1)
    return ((i <= j) & ((i // CHUNK) == (j // CHUNK))).astype(F32)


def _hgrn_gates(a, lb):
    qa, z = a[:, 0:128], a[:, 128:256]
    sg, sgn = _sig(z), _sig(-z)
    fg = lb + (1.0 - lb) * sg
    lf = jnp.log(jnp.maximum(fg, TINY))
    kk = (1.0 - lb) * sgn
    return qa * _sig(qa), kk, lf, sg, sgn, fg


def _hgrn_fwd(proj3, lbs_row, gn_col, name):
    bsz, t, _ = proj3.shape
    nt = t // LANES

    def body(a_ref, lb_ref, gn_ref, og_ref, or_ref):
        lb = lb_ref[...]
        gn = gn_ref[...]
        umat = _chunk_cumsum_matrix()
        lane64 = _iota((1, LANES), 1) % CHUNK

        def tile(i, carry):
            r0 = pl.multiple_of(i * LANES, LANES)
            a = a_ref[pl.ds(r0, LANES), :]
            qq, kk, lf, _, _, _ = _hgrn_gates(a, lb)
            va, ga = a[:, 256:384], a[:, 384:512]
            q_t, k_t, v_t = qq.T, kk.T, va.T
            b_t = jnp.dot(lf.T, umat, precision=HI, preferred_element_type=F32)
            new_s, o_heads = [], []
            for h in range(2):
                s_h = carry[h]
                rs = slice(CHUNK * h, CHUNK * (h + 1))
                qh, kh, vh, bh = q_t[rs], k_t[rs], v_t[rs], b_t[rs]
                inter = []
                for c in range(2):
                    cs = slice(CHUNK * c, CHUNK * (c + 1))
                    b_ = bh[:, cs]
                    qt = (qh[:, cs] * jnp.exp(b_)).astype(BF16)
                    inter.append(_dot(s_h.astype(BF16), qt, TN))
                    bl = b_[:, CHUNK - 1:CHUNK]
                    kt = (kh[:, cs] * jnp.exp(bl - b_)).astype(BF16)
                    s_h = jnp.exp(bl) * s_h + _dot(kt, vh[:, cs].astype(BF16), NT)
                new_s.append(s_h)

                acc = jnp.concatenate(inter, axis=1) + jnp.sum(qh * kh, axis=0, keepdims=True) * vh
                for dlt in range(1, CHUNK):
                    kr, br, vr = pltpu.roll(kh, dlt, 1), pltpu.roll(bh, dlt, 1), pltpu.roll(vh, dlt, 1)
                    e = jnp.exp(jnp.minimum(bh - br, 0.0))
                    att = jnp.sum(qh * kr * e, axis=0, keepdims=True)
                    acc = acc + jnp.where(lane64 >= dlt, att, 0.0) * vr
                o_heads.append(acc)
            normed = []
            for h in range(2):
                o_h = o_heads[h]
                ms = jnp.mean(o_h * o_h, axis=0, keepdims=True)
                normed.append(o_h * lax.rsqrt(ms + NORM_EPS) * gn[CHUNK * h:CHUNK * (h + 1)])
            or_ref[pl.ds(r0, LANES), :] = jnp.concatenate(o_heads, axis=0).T
            og_ref[pl.ds(r0, LANES), :] = jnp.concatenate(normed, axis=0).T * _silu(ga)
            return tuple(new_s)

        zero = jnp.zeros((CHUNK, CHUNK), F32)
        lax.fori_loop(0, nt, tile, (zero, zero))

    out = jax.ShapeDtypeStruct((bsz, t, HGRN_W), F32)
    return pl.pallas_call(
        body, grid=(bsz, 2),
        in_specs=[pl.BlockSpec((None, t, 512), lambda b, p: (b, 0, p)),
                  pl.BlockSpec((1, 128), lambda b, p: (0, p)),
                  pl.BlockSpec((128, 1), lambda b, p: (p, 0))],
        out_specs=[pl.BlockSpec((None, t, 128), lambda b, p: (b, 0, p)),
                   pl.BlockSpec((None, t, 128), lambda b, p: (b, 0, p))],
        out_shape=[out, out],
        compiler_params=_cparams(("parallel", "parallel")), name=name)(proj3, lbs_row, gn_col)


def _hgrn_bwd(proj3, o_raw, dmixed, lbs_row, gn_row, name):
    bsz, t, _ = proj3.shape
    nt = t // LANES
    nchunk = t // CHUNK

    def body(a_ref, or_ref, do_ref, lb_ref, gn_ref, da_ref, dgn_ref, dlb_ref, s_sc):
        lb = lb_ref[...]
        gn = gn_ref[...]
        umat = _chunk_cumsum_matrix()
        lane = _iota((1, LANES), 1)
        lane64 = lane % CHUNK
        half = lane < CHUNK

        def t_layout(a):
            qq, kk, lf, sg, sgn, fg = _hgrn_gates(a, lb)
            b_t = jnp.dot(lf.T, umat, precision=HI, preferred_element_type=F32)
            return qq.T, kk.T, a[:, 256:384].T, b_t, (sg, sgn, fg)

        def fwd_tile(i, carry):
            r0 = pl.multiple_of(i * LANES, LANES)
            q_t, k_t, v_t, b_t, _ = t_layout(a_ref[pl.ds(r0, LANES), :])
            new_s = []
            for h in range(2):
                s_h = carry[h]
                rs = slice(CHUNK * h, CHUNK * (h + 1))
                for c in range(2):
                    cs = slice(CHUNK * c, CHUNK * (c + 1))
                    s_sc[h, 2 * i + c] = s_h
                    b_ = b_t[rs, cs]
                    bl = b_[:, CHUNK - 1:CHUNK]
                    kt = (k_t[rs, cs] * jnp.exp(bl - b_)).astype(BF16)
                    s_h = jnp.exp(bl) * s_h + _dot(kt, v_t[rs, cs].astype(BF16), NT)
                new_s.append(s_h)
            return tuple(new_s)

        zero = jnp.zeros((CHUNK, CHUNK), F32)
        lax.fori_loop(0, nt, fwd_tile, (zero, zero))

        def half_mean(v):
            m0 = jnp.sum(jnp.where(half, v, 0.0), axis=1, keepdims=True) * (1.0 / CHUNK)
            m1 = jnp.sum(jnp.where(half, 0.0, v), axis=1, keepdims=True) * (1.0 / CHUNK)
            return jnp.where(half, m0, m1)

        def bwd_tile(k, carry):
            ds0, ds1, dgn_acc, dlb_acc = carry
            i = nt - 1 - k
            r0 = pl.multiple_of(i * LANES, LANES)
            a = a_ref[pl.ds(r0, LANES), :]
            qa, z, ga = a[:, 0:128], a[:, 128:256], a[:, 384:512]
            q_t, k_t, v_t, b_t, (sg, sgn, fg) = t_layout(a)
            oraw = or_ref[pl.ds(r0, LANES), :]
            dout = do_ref[pl.ds(r0, LANES), :]
            r = lax.rsqrt(half_mean(oraw * oraw) + NORM_EPS)
            xn = oraw * r
            dga = dout * (xn * gn) * _dsilu(ga)
            don = dout * _silu(ga)
            dgn_acc = dgn_acc + jnp.sum(don * xn, axis=0, keepdims=True)
            dxn = don * gn
            do_t = (r * (dxn - xn * half_mean(dxn * xn))).T
            new_ds, dq_h, dk_h, dv_h, db_h = [], [], [], [], []
            for h in range(2):
                ds_h = (ds0, ds1)[h]
                rs = slice(CHUNK * h, CHUNK * (h + 1))
                qh, kh, vh, bh, doh = q_t[rs], k_t[rs], v_t[rs], b_t[rs], do_t[rs]
                dq_c, dk_c, dv_c, dbl_c = [None, None], [None, None], [None, None], [None, None]
                for c in (1, 0):
                    cs = slice(CHUNK * c, CHUNK * (c + 1))
                    s_n = s_sc[h, 2 * i + c]
                    b_ = bh[:, cs]
                    eb = jnp.exp(b_)
                    bl = b_[:, CHUNK - 1:CHUNK]
                    ek = jnp.exp(bl - b_)
                    ebl = jnp.exp(bl)
                    qt, kt = qh[:, cs] * eb, kh[:, cs] * ek
                    do_c = doh[:, cs].astype(BF16)
                    dsb = ds_h.astype(BF16)
                    dv_c[c] = _dot(dsb, kt.astype(BF16), TN)
                    dkt = _dot(dsb, vh[:, cs].astype(BF16), NN)
                    dqt = _dot(s_n.astype(BF16), do_c, NN)
                    dbl_c[c] = jnp.sum(ds_h * s_n, axis=1, keepdims=True) * ebl + jnp.sum(dkt * kt, axis=1, keepdims=True)
                    dq_c[c], dk_c[c] = dqt * eb, dkt * ek
                    ds_h = ebl * ds_h + _dot(qt.astype(BF16), do_c, NT)
                new_ds.append(ds_h)

                att0 = jnp.sum(qh * kh, axis=0, keepdims=True)
                datt0 = jnp.sum(doh * vh, axis=0, keepdims=True)
                dqh = jnp.concatenate(dq_c, axis=1) + datt0 * kh
                dkh = jnp.concatenate(dk_c, axis=1) + datt0 * qh
                dvh = jnp.concatenate(dv_c, axis=1) + att0 * doh
                for dlt in range(1, CHUNK):
                    kr, br, vr = pltpu.roll(kh, dlt, 1), pltpu.roll(bh, dlt, 1), pltpu.roll(vh, dlt, 1)
                    e = jnp.where(lane64 >= dlt, jnp.exp(jnp.minimum(bh - br, 0.0)), 0.0)
                    qe = qh * e
                    att = jnp.sum(qe * kr, axis=0, keepdims=True)
                    datt = jnp.sum(doh * vr, axis=0, keepdims=True)
                    dqh = dqh + datt * (kr * e)
                    dkh = dkh + pltpu.roll(datt * qe, LANES - dlt, 1)
                    dvh = dvh + pltpu.roll(att * doh, LANES - dlt, 1)
                dbl = jnp.where(half, dbl_c[0], dbl_c[1])
                db_h.append(qh * dqh - kh * dkh + jnp.where(lane64 == CHUNK - 1, dbl, 0.0))
                dq_h.append(dqh)
                dk_h.append(dkh)
                dv_h.append(dvh)
            dqq = jnp.concatenate(dq_h, axis=0).T
            dkk = jnp.concatenate(dk_h, axis=0).T
            dvv = jnp.concatenate(dv_h, axis=0).T
            dlf = _dot(jnp.concatenate(db_h, axis=0), umat, NT, precision=HI).T
            dqa = dqq * _dsilu(qa)
            dfg = jnp.where(fg > TINY, dlf / fg, 0.0)
            dz = (dfg - dkk) * (1.0 - lb) * sg * sgn
            dlb_acc = dlb_acc + jnp.sum(dfg * (1.0 - sg) - dkk * sgn, axis=0, keepdims=True)
            da_ref[pl.ds(r0, LANES), :] = jnp.concatenate([dqa, dz, dvv, dga], axis=1)
            return new_ds[0], new_ds[1], dgn_acc, dlb_acc

        zrow = jnp.zeros((1, LANES), F32)
        _, _, dgn_acc, dlb_acc = lax.fori_loop(0, nt, bwd_tile, (zero, zero, zrow, zrow))
        dgn_ref[...] = jnp.broadcast_to(dgn_acc, (8, LANES))
        dlb_ref[...] = jnp.broadcast_to(dlb_acc, (8, LANES))

    rows = jax.ShapeDtypeStruct((bsz, 8, HGRN_W), F32)
    return pl.pallas_call(
        body, grid=(bsz, 2),
        in_specs=[pl.BlockSpec((None, t, 512), lambda b, p: (b, 0, p)),
                  pl.BlockSpec((None, t, 128), lambda b, p: (b, 0, p)),
                  pl.BlockSpec((None, t, 128), lambda b, p: (b, 0, p)),
                  pl.BlockSpec((1, 128), lambda b, p: (0, p)),
                  pl.BlockSpec((1, 128), lambda b, p: (0, p))],
        out_specs=[pl.BlockSpec((None, t, 512), lambda b, p: (b, 0, p)),
                   pl.BlockSpec((None, 8, 128), lambda b, p: (b, 0, p)),
                   pl.BlockSpec((None, 8, 128), lambda b, p: (b, 0, p))],
        out_shape=[jax.ShapeDtypeStruct((bsz, t, A_W), F32), rows, rows],
        scratch_shapes=[pltpu.VMEM((2, nchunk, CHUNK, CHUNK), F32)],
        compiler_params=_cparams(("parallel", "parallel")), name=name)(proj3, o_raw, dmixed, lbs_row, gn_row)


N_LEVELS = 6


def _hgrn_tables():
    t = np.arange(LANES)
    j = np.arange(LANES)[None, :]
    same_chunk = (t[:, None] // CHUNK) == (j // CHUNK)
    w = np.zeros((2 + N_LEVELS, LANES, LANES), np.float32)
    w[0] = same_chunk & (j <= t[:, None])
    w[1] = same_chunk & (j > t[:, None])
    maskf = np.zeros((N_LEVELS, LANES, LANES), np.float32)
    rightf = np.zeros((N_LEVELS, LANES, LANES), np.float32)
    for li in range(N_LEVELS):
        m = (CHUNK // 2) >> li
        start = t - (t % (2 * m))
        right = (t % (2 * m)) >= m
        first = np.where(right, start + m, t + 1)
        last = np.where(right, t, start + m - 1)
        w[2 + li] = (j >= first[:, None]) & (j <= last[:, None])
        maskf[li] = (t[:, None] // (2 * m)) == (j // (2 * m))
        rightf[li] = right[:, None]
    return jnp.asarray(w.reshape(-1, LANES), BF16), jnp.asarray(maskf), jnp.asarray(rightf)


def _split(x, n):
    parts = []
    for _ in range(n - 1):
        p = x.astype(BF16)
        parts.append(p)
        x = x - p.astype(F32)
    parts.append(x.astype(BF16))
    return parts


def _exact_dot(w, parts):
    acc = jnp.dot(w, parts[0], preferred_element_type=F32)
    for p in parts[1:]:
        acc = acc + jnp.dot(w, p, preferred_element_type=F32)
    return acc


def _head_sums(v, ones_blk, n=2):
    parts = _split(v, n)
    acc = jnp.dot(parts[0], ones_blk, preferred_element_type=F32)
    for p in parts[1:]:
        acc = acc + jnp.dot(p, ones_blk, preferred_element_type=F32)
    return acc


def _hgrn_consts():
    r, c = _iota((LANES, LANES), 0), _iota((LANES, LANES), 1)
    eye = r == c
    ones_blk = ((r // CHUNK) == (c // CHUNK)).astype(BF16)
    return eye, ones_blk, jnp.ones((CHUNK, LANES), BF16)


def _hgrn_levels(qq, kk, zall, mk_ref, rt_ref, d_att=None):
    att = [jnp.zeros((LANES, LANES), F32)] * 2
    dq = dk = db = jnp.zeros((LANES, LANES), F32)
    for li in range(N_LEVELS):
        e = jnp.exp(zall[(2 + li) * LANES:(3 + li) * LANES])
        rt = rt_ref[li]
        mk = mk_ref[li]
        qef, kef = e * rt, e * (1.0 - rt)
        qe, ke = (qq * qef).astype(BF16), (kk * kef).astype(BF16)
        dqs, dks = [], []
        for h in range(2):
            hs = slice(CHUNK * h, CHUNK * (h + 1))
            att[h] = att[h] + _dot(qe[:, hs], ke[:, hs], NT) * mk
            if d_att is not None:
                dam = (d_att[h] * mk).astype(BF16)
                dqs.append(jnp.dot(dam, ke[:, hs], preferred_element_type=F32))
                dks.append(_dot(dam, qe[:, hs], TN))
        if d_att is not None:
            dqe, dke = jnp.concatenate(dqs, axis=1), jnp.concatenate(dks, axis=1)
            dq = dq + dqe * qef
            dk = dk + dke * kef
            db = db + (dqe * qe.astype(F32) - dke * ke.astype(F32))
    return att, dq, dk, db


def _hgrn_fwd(proj3, lbs_row, gn_row, name):
    bsz, t, _ = proj3.shape
    nt = t // LANES
    w_all, maskf, rightf = _hgrn_tables()

    def body(a_ref, lb_ref, gn_ref, w_ref, mk_ref, rt_ref, og_ref, or_ref, st_ref):
        lb = lb_ref[...]
        gn = gn_ref[...]
        eye, ones_blk, ones_h = _hgrn_consts()

        def tile(i, carry):
            r0 = pl.multiple_of(i * LANES, LANES)
            a = a_ref[pl.ds(r0, LANES), :]
            qq, kk, lf, _, _, _ = _hgrn_gates(a, lb)
            va, ga = a[:, 256:384], a[:, 384:512]
            parts = _split(lf, 3)
            zall = _exact_dot(w_ref[...], parts)
            eb, ee = jnp.exp(zall[0:LANES]), jnp.exp(zall[LANES:2 * LANES])
            vb = va.astype(BF16)
            att, _, _, _ = _hgrn_levels(qq, kk, zall, mk_ref, rt_ref)
            qk = _split(qq * kk, 2)
            qeb, keb = (qq * eb).astype(BF16), (kk * ee).astype(BF16)
            new_s, o_heads = [], []
            for h in range(2):
                hs = slice(CHUNK * h, CHUNK * (h + 1))
                diag = _exact_dot_r(qk, hs, ones_h)
                a_h = att[h] + jnp.where(eye, diag, 0.0)
                o_h = jnp.dot(a_h.astype(BF16), vb[:, hs], preferred_element_type=F32)
                st = carry[h]
                chunks = []
                for c in range(2):
                    rc = slice(CHUNK * c, CHUNK * (c + 1))
                    st_ref[h, 2 * i + c] = st
                    chunks.append(o_h[rc] + _dot(qeb[rc, hs], st.astype(BF16), NT))
                    ebl = eb[CHUNK * (c + 1) - 1:CHUNK * (c + 1), hs]
                    st = st * ebl + _dot(vb[rc, hs], keb[rc, hs], TN)
                new_s.append(st)
                o_heads.append(jnp.concatenate(chunks, axis=0))
            o = jnp.concatenate(o_heads, axis=1)
            ms = _head_sums(o * o, ones_blk) * (1.0 / CHUNK)
            or_ref[pl.ds(r0, LANES), :] = o
            og_ref[pl.ds(r0, LANES), :] = o * lax.rsqrt(ms + NORM_EPS) * gn * _silu(ga)
            return tuple(new_s)

        zero = jnp.zeros((CHUNK, CHUNK), F32)
        lax.fori_loop(0, nt // 2, lambda i, carry: tile(2 * i + 1, tile(2 * i, carry)), (zero, zero))

    out = jax.ShapeDtypeStruct((bsz, t, HGRN_W), F32)
    row = pl.BlockSpec((1, 128), lambda b, p: (0, p))
    return pl.pallas_call(
        body, grid=(bsz, 2),
        in_specs=[pl.BlockSpec((None, t, 512), lambda b, p: (b, 0, p)), row, row,
                  pl.BlockSpec(w_all.shape, lambda b, p: (0, 0)),
                  pl.BlockSpec(maskf.shape, lambda b, p: (0, 0, 0)),
                  pl.BlockSpec(rightf.shape, lambda b, p: (0, 0, 0))],
        out_specs=[pl.BlockSpec((None, t, 128), lambda b, p: (b, 0, p)),
                   pl.BlockSpec((None, t, 128), lambda b, p: (b, 0, p)),
                   pl.BlockSpec((None, 2, t // CHUNK, CHUNK, CHUNK), lambda b, p: (b, p, 0, 0, 0))],
        out_shape=[out, out, jax.ShapeDtypeStruct((bsz, 4, t // CHUNK, CHUNK, CHUNK), F32)],
        compiler_params=_cparams(("parallel", "parallel")), name=name)(proj3, lbs_row, gn_row, w_all, maskf, rightf)


def _exact_dot_r(parts, hs, ones_h):
    acc = jnp.dot(parts[0][:, hs], ones_h, preferred_element_type=F32)
    for p in parts[1:]:
        acc = acc + jnp.dot(p[:, hs], ones_h, preferred_element_type=F32)
    return acc


def _hgrn_bwd(proj3, o_raw, dmixed, states, lbs_row, gn_row, name):
    bsz, t, _ = proj3.shape
    nt = t // LANES
    nchunk = t // CHUNK
    w_all, maskf, rightf = _hgrn_tables()

    def body(a_ref, or_ref, do_ref, s_sc, lb_ref, gn_ref, w_ref, mk_ref, rt_ref, da_ref, dgn_ref, dlb_ref):
        lb = lb_ref[...]
        gn = gn_ref[...]
        eye, ones_blk, ones_h = _hgrn_consts()
        r_i, c_i = _iota((LANES, LANES), 0), _iota((LANES, LANES), 1)
        suffix = ((c_i >= r_i) & ((r_i // CHUNK) == (c_i // CHUNK))).astype(BF16)
        row64 = _iota((LANES, CHUNK), 0)
        ones_t = jnp.ones((LANES, CHUNK), BF16)
        zero = jnp.zeros((CHUNK, CHUNK), F32)

        def bwd_tile(k, carry):
            dst0, dst1, dgn_acc, dlb_acc = carry
            i = nt - 1 - k
            r0 = pl.multiple_of(i * LANES, LANES)
            a = a_ref[pl.ds(r0, LANES), :]
            qa, ga = a[:, 0:128], a[:, 384:512]
            qq, kk, lf, sg, sgn, fg = _hgrn_gates(a, lb)
            parts = _split(lf, 3)
            zall = _exact_dot(w_ref[...], parts)
            eb, ee = jnp.exp(zall[0:LANES]), jnp.exp(zall[LANES:2 * LANES])
            vb = a[:, 256:384].astype(BF16)
            oraw = or_ref[pl.ds(r0, LANES), :]
            dout = do_ref[pl.ds(r0, LANES), :]
            r = lax.rsqrt(_head_sums(oraw * oraw, ones_blk) * (1.0 / CHUNK) + NORM_EPS)
            xn = oraw * r
            dga = dout * (xn * gn) * _dsilu(ga)
            don = dout * _silu(ga)
            dgn_acc = dgn_acc + jnp.sum(don * xn, axis=0, keepdims=True)
            dxn = don * gn
            do = r * (dxn - xn * (_head_sums(dxn * xn, ones_blk) * (1.0 / CHUNK)))
            dob = do.astype(BF16)
            d_att = [_dot(dob[:, CHUNK * h:CHUNK * (h + 1)], vb[:, CHUNK * h:CHUNK * (h + 1)], NT) for h in range(2)]
            att, dq, dk, db_lv = _hgrn_levels(qq, kk, zall, mk_ref, rt_ref, d_att)
            qk = _split(qq * kk, 2)
            qe_f, ke_f = qq * eb, kk * ee
            qeb, keb = qe_f.astype(BF16), ke_f.astype(BF16)
            new_ds, dq_h, dk_h, dv_h, dbl_h = [], [], [], [], []
            for h in range(2):
                hs = slice(CHUNK * h, CHUNK * (h + 1))
                a_h = att[h] + jnp.where(eye, _exact_dot_r(qk, hs, ones_h), 0.0)
                dv = _dot(a_h.astype(BF16), dob[:, hs], TN)
                ddiag = _exact_dot_r(_split(jnp.where(eye, d_att[h], 0.0), 2), slice(None), ones_t)
                dq_i = dq[:, hs] + ddiag * kk[:, hs]
                dk_i = dk[:, hs] + ddiag * qq[:, hs]
                dst = (dst0, dst1)[h]
                dq_c, dk_c, dv_c, dbl_c = [None, None], [None, None], [None, None], [None, None]
                for c in (1, 0):
                    rc = slice(CHUNK * c, CHUNK * (c + 1))
                    st_n = s_sc[h, 2 * i + c]
                    ebl = eb[CHUNK * (c + 1) - 1:CHUNK * (c + 1), hs]
                    dstb = dst.astype(BF16)
                    dv_c[c] = _dot(keb[rc, hs], dstb, NT)
                    dke = jnp.dot(vb[rc, hs], dstb, preferred_element_type=F32)
                    dqe = jnp.dot(dob[rc, hs], st_n.astype(BF16), preferred_element_type=F32)
                    dbl_c[c] = (jnp.sum(dst * st_n, axis=0, keepdims=True) * ebl
                                + jnp.sum(dke * ke_f[rc, hs], axis=0, keepdims=True))
                    dq_c[c], dk_c[c] = dqe * eb[rc, hs], dke * ee[rc, hs]
                    dst = dst * ebl + _dot(dob[rc, hs], qeb[rc, hs], TN)
                new_ds.append(dst)
                dq_x, dk_x = jnp.concatenate(dq_c, axis=0), jnp.concatenate(dk_c, axis=0)
                dq_h.append(dq_i + dq_x)
                dk_h.append(dk_i + dk_x)
                dv_h.append(dv + jnp.concatenate(dv_c, axis=0))
                dbl_h.append(qq[:, hs] * dq_x - kk[:, hs] * dk_x
                             + jnp.where(row64 == CHUNK - 1, dbl_c[0], 0.0) + jnp.where(row64 == LANES - 1, dbl_c[1], 0.0))
            dqq = jnp.concatenate(dq_h, axis=1)
            dkk = jnp.concatenate(dk_h, axis=1)
            dvv = jnp.concatenate(dv_h, axis=1)
            db = db_lv + jnp.concatenate(dbl_h, axis=1)
            dlf = _exact_dot(suffix, _split(db, 3))
            dqa = dqq * _dsilu(qa)
            dfg = jnp.where(fg > TINY, dlf / fg, 0.0)
            dz = (dfg - dkk) * (1.0 - lb) * sg * sgn
            dlb_acc = dlb_acc + jnp.sum(dfg * (1.0 - sg) - dkk * sgn, axis=0, keepdims=True)
            da_ref[pl.ds(r0, LANES), :] = jnp.concatenate([dqa, dz, dvv, dga], axis=1)
            return new_ds[0], new_ds[1], dgn_acc, dlb_acc

        zrow = jnp.zeros((1, LANES), F32)
        _, _, dgn_acc, dlb_acc = lax.fori_loop(
            0, nt // 2, lambda k, carry: bwd_tile(2 * k + 1, bwd_tile(2 * k, carry)), (zero, zero, zrow, zrow))
        dgn_ref[...] = jnp.broadcast_to(dgn_acc, (8, LANES))
        dlb_ref[...] = jnp.broadcast_to(dlb_acc, (8, LANES))

    rows = jax.ShapeDtypeStruct((bsz, 8, HGRN_W), F32)
    row = pl.BlockSpec((1, 128), lambda b, p: (0, p))
    blk = pl.BlockSpec((None, t, 128), lambda b, p: (b, 0, p))
    return pl.pallas_call(
        body, grid=(bsz, 2),
        in_specs=[pl.BlockSpec((None, t, 512), lambda b, p: (b, 0, p)), blk, blk,
                  pl.BlockSpec((None, 2, nchunk, CHUNK, CHUNK), lambda b, p: (b, p, 0, 0, 0)), row, row,
                  pl.BlockSpec(w_all.shape, lambda b, p: (0, 0)),
                  pl.BlockSpec(maskf.shape, lambda b, p: (0, 0, 0)),
                  pl.BlockSpec(rightf.shape, lambda b, p: (0, 0, 0))],
        out_specs=[pl.BlockSpec((None, t, 512), lambda b, p: (b, 0, p)),
                   pl.BlockSpec((None, 8, 128), lambda b, p: (b, 0, p)),
                   pl.BlockSpec((None, 8, 128), lambda b, p: (b, 0, p))],
        out_shape=[jax.ShapeDtypeStruct((bsz, t, A_W), F32), rows, rows],
        compiler_params=_cparams(("parallel", "parallel")), name=name)(
            proj3, o_raw, dmixed, states, lbs_row, gn_row, w_all, maskf, rightf)


def _pool_tt(t):
    return min(256, t)


def _window_select(s2, s4, s8, s16, lane):
    return jnp.where(lane < 64, s2, jnp.where(lane < 128, s4, jnp.where(lane < 192, s8, s16)))


def _pool_counts(t0, tt):
    lane = _iota((tt, POOL_W), 1)
    tpos = (_iota((tt, POOL_W), 0) + t0 + 1).astype(F32)
    win = jnp.where(lane < 64, 2.0, jnp.where(lane < 128, 4.0, jnp.where(lane < 192, 8.0, 16.0)))
    return 1.0 / jnp.minimum(tpos, win), lane


def _pooled_tile(upad_ref, i, tt):
    r0 = pl.multiple_of(i * tt, 8)
    cat = upad_ref[pl.ds(r0, tt + POOL_HALO), :]
    s2 = cat + pltpu.roll(cat, 1, 0)
    s4 = s2 + pltpu.roll(s2, 2, 0)
    s8 = s4 + pltpu.roll(s4, 4, 0)
    s16 = s8 + pltpu.roll(s8, 8, 0)
    inv, lane = _pool_counts(i * tt, tt)
    sel = _window_select(s2[POOL_HALO:], s4[POOL_HALO:], s8[POOL_HALO:], s16[POOL_HALO:], lane)
    return sel * inv - cat[POOL_HALO:], inv, lane


def _pool_fwd(proj3, wbd, scale_row, name):
    bsz, t, _ = proj3.shape
    tt = _pool_tt(t)

    def body(p_ref, w_ref, sc_ref, o_ref, upad):
        upad[0:POOL_HALO, :] = jnp.zeros((POOL_HALO, POOL_W), F32)
        upad[POOL_HALO:, :] = p_ref[:, 0:POOL_W]
        w = w_ref[...]
        sc = sc_ref[...]

        def tile(i, c):
            pooled, _, _ = _pooled_tile(upad, i, tt)
            r0 = pl.multiple_of(i * tt, 8)
            g = p_ref[pl.ds(r0, tt), POOL_W:2 * POOL_W]
            pre = jnp.dot(pooled.astype(BF16), w, preferred_element_type=F32)
            o_ref[pl.ds(r0, tt), :] = pre * sc * _silu(g)
            return c

        lax.fori_loop(0, t // tt, tile, 0)

    return pl.pallas_call(
        body, grid=(bsz,),
        in_specs=[pl.BlockSpec((None, t, 512), lambda b: (b, 0, B_BLK)),
                  pl.BlockSpec((POOL_W, POOL_W), lambda b: (0, 0)),
                  pl.BlockSpec((1, POOL_W), lambda b: (0, 0))],
        out_specs=pl.BlockSpec((None, t, POOL_W), lambda b: (b, 0, 0)),
        out_shape=jax.ShapeDtypeStruct((bsz, t, POOL_W), F32),
        scratch_shapes=[pltpu.VMEM((t + POOL_HALO, POOL_W), F32)],
        compiler_params=_cparams(("parallel",)), name=name)(proj3, wbd, scale_row)


def _pool_bwd(proj3, dmixed, wbd, scale_row, name):
    bsz, t, _ = proj3.shape
    tt = _pool_tt(t)

    def body(p_ref, do_ref, w_ref, sc_ref, db_ref, dsc_ref, dw_ref, upad, epad):
        upad[0:POOL_HALO, :] = jnp.zeros((POOL_HALO, POOL_W), F32)
        upad[POOL_HALO:, :] = p_ref[:, 0:POOL_W]
        epad[t:, :] = jnp.zeros((POOL_HALO, POOL_W), F32)
        w = w_ref[...]
        sc = sc_ref[...]

        def tile(i, carry):
            dsc_acc, dw_acc = carry
            pooled, inv, _ = _pooled_tile(upad, i, tt)
            r0 = pl.multiple_of(i * tt, 8)
            g = p_ref[pl.ds(r0, tt), POOL_W:2 * POOL_W]
            dout = do_ref[pl.ds(r0, tt), :]
            pb = pooled.astype(BF16)
            pre = jnp.dot(pb, w, preferred_element_type=F32)
            t1 = dout * _silu(g)
            dsc_acc = dsc_acc + jnp.sum(t1 * pre, axis=0, keepdims=True)
            dpre = (t1 * sc).astype(BF16)
            db_ref[pl.ds(r0, tt), POOL_W:2 * POOL_W] = dout * pre * sc * _dsilu(g)
            dw_acc = dw_acc + _dot(pb, dpre, TN)
            dpooled = _dot(dpre, w, NT)
            epad[pl.ds(r0, tt), :] = dpooled * inv
            return dsc_acc, dw_acc

        dsc_acc, dw_acc = lax.fori_loop(0, t // tt, tile, (jnp.zeros((1, POOL_W), F32), jnp.zeros((POOL_W, POOL_W), F32)))
        dsc_ref[...] = jnp.broadcast_to(dsc_acc, (8, POOL_W))
        dw_ref[...] = dw_acc

        def tile2(i, c):
            r0 = pl.multiple_of(i * tt, 8)
            n = tt + POOL_HALO
            cat = epad[pl.ds(r0, n), :]
            s2 = cat + pltpu.roll(cat, n - 1, 0)
            s4 = s2 + pltpu.roll(s2, n - 2, 0)
            s8 = s4 + pltpu.roll(s4, n - 4, 0)
            s16 = s8 + pltpu.roll(s8, n - 8, 0)
            inv, lane = _pool_counts(i * tt, tt)
            sel = _window_select(s2[:tt], s4[:tt], s8[:tt], s16[:tt], lane)
            db_ref[pl.ds(r0, tt), 0:POOL_W] = sel - cat[:tt] / inv
            return c

        lax.fori_loop(0, t // tt, tile2, 0)

    return pl.pallas_call(
        body, grid=(bsz,),
        in_specs=[pl.BlockSpec((None, t, 512), lambda b: (b, 0, B_BLK)),
                  pl.BlockSpec((None, t, POOL_W), lambda b: (b, 0, 1)),
                  pl.BlockSpec((POOL_W, POOL_W), lambda b: (0, 0)),
                  pl.BlockSpec((1, POOL_W), lambda b: (0, 0))],
        out_specs=[pl.BlockSpec((None, t, 512), lambda b: (b, 0, 0)),
                   pl.BlockSpec((None, 8, POOL_W), lambda b: (b, 0, 0)),
                   pl.BlockSpec((None, POOL_W, POOL_W), lambda b: (b, 0, 0))],
        out_shape=[jax.ShapeDtypeStruct((bsz, t, B_W), F32), jax.ShapeDtypeStruct((bsz, 8, POOL_W), F32),
                   jax.ShapeDtypeStruct((bsz, POOL_W, POOL_W), F32)],
        scratch_shapes=[pltpu.VMEM((t + POOL_HALO, POOL_W), F32), pltpu.VMEM((t + POOL_HALO, POOL_W), F32)],
        compiler_params=_cparams(("parallel",)), name=name)(proj3, dmixed, wbd, scale_row)


def _head_select_rows(hp):
    r, c = _iota((8, LANES), 0), _iota((8, LANES), 1)
    return ((r < 2) & (c == 2 * hp + r)).astype(F32)


def _foxgate_fwd(proj3, bias_row, name):
    bsz, t, _ = proj3.shape
    nt = t // LANES

    def body(f_ref, b_ref, cn_ref, ct_ref):
        bias = b_ref[...]
        i, j = _iota((LANES, LANES), 0), _iota((LANES, LANES), 1)
        lower = (j <= i).astype(BF16)
        spread = (_iota((LANES, FOX_W), 0) == _iota((LANES, FOX_W), 1) // 64).astype(BF16)
        select = [_head_select_rows(hp).astype(BF16) for hp in range(4)]
        offset = jnp.zeros((1, LANES), F32)
        for k in range(nt):
            rows = slice(k * LANES, (k + 1) * LANES)
            xg = f_ref[rows, :] + bias
            lf = jnp.minimum(xg, 0.0) - jnp.log(1.0 + jnp.exp(-jnp.abs(xg)))
            c = _exact_dot(lower, _split(lf, 3)) + offset
            offset = c[LANES - 1:LANES, :]
            parts = _split(c, 3)
            cn_ref[rows, :] = _head_sums(c, spread, 3)
            for hp in range(4):
                acc = _dot(select[hp], parts[0], NT)
                for p in parts[1:]:
                    acc = acc + _dot(select[hp], p, NT)
                ct_ref[hp, :, rows] = acc

    return pl.pallas_call(
        body, grid=(bsz,),
        in_specs=[pl.BlockSpec((None, t, 128), lambda b: (b, 0, F_BLK)), pl.BlockSpec((1, 128), lambda b: (0, 0))],
        out_specs=[pl.BlockSpec((None, t, FOX_W), lambda b: (b, 0, 0)),
                   pl.BlockSpec((None, 4, 8, t), lambda b: (b, 0, 0, 0))],
        out_shape=[jax.ShapeDtypeStruct((bsz, t, FOX_W), F32), jax.ShapeDtypeStruct((bsz, 4, 8, t), F32)],
        compiler_params=_cparams(("parallel",)), name=name)(proj3, bias_row)


def _foxgate_bwd(proj3, dc_nat, bias_row, name):
    bsz, t, _ = proj3.shape
    nt = t // LANES

    def body(f_ref, dc_ref, b_ref, df_ref, dbias_ref, run_sc):
        bias = b_ref[...]
        i, j = _iota((LANES, LANES), 0), _iota((LANES, LANES), 1)
        upper = (j >= i).astype(F32)
        valid = _iota((1, LANES), 1) < FOX_HEADS
        run_sc[...] = jnp.zeros((8, LANES), F32)
        dbias_ref[...] = jnp.zeros((8, LANES), F32)

        def tile(k, c):
            r0 = pl.multiple_of((nt - 1 - k) * LANES, LANES)
            dc = dc_ref[pl.ds(r0, LANES), :] + jnp.where(i == LANES - 1, run_sc[0:1, :], 0.0)
            dlf = jnp.dot(upper, dc, precision=HI, preferred_element_type=F32)
            xg = f_ref[pl.ds(r0, LANES), :] + bias
            df = jnp.where(valid, dlf * _sig(-xg), 0.0)
            df_ref[pl.ds(r0, LANES), :] = df
            run_sc[...] = dlf[0:8, :]
            dbias_ref[...] += jnp.sum(df, axis=0, keepdims=True)
            return c

        lax.fori_loop(0, nt, tile, 0)

    blk = pl.BlockSpec((None, t, 128), lambda b: (b, 0, 0))
    return pl.pallas_call(
        body, grid=(bsz,),
        in_specs=[pl.BlockSpec((None, t, 128), lambda b: (b, 0, F_BLK)), blk, pl.BlockSpec((1, 128), lambda b: (0, 0))],
        out_specs=[blk, pl.BlockSpec((None, 8, 128), lambda b: (b, 0, 0))],
        out_shape=[jax.ShapeDtypeStruct((bsz, t, F_W), F32), jax.ShapeDtypeStruct((bsz, 8, 128), F32)],
        scratch_shapes=[pltpu.VMEM((8, LANES), F32)],
        compiler_params=_cparams(("parallel",)), name=name)(proj3, dc_nat, bias_row)


def _fox_tile(t):
    return min(256, t)


def _fox_fwd(proj3, c_nat, c_t, name):
    bsz, t, _ = proj3.shape
    tq = tk = min(2 * _fox_tile(t), t)
    nq = t // tq

    def body(q_ref, kv_ref, cn_ref, ct_ref, og_ref, or_ref, lse_ref):
        i = pl.program_id(2)
        qblk = q_ref[...]
        first = _iota((1, 128), 1) < 64
        qv = qblk[:, 0:128] * 0.125
        qm = [jnp.where(first, qv, 0.0).astype(BF16), jnp.where(first, 0.0, qv).astype(BF16)]
        cqs = [cn_ref[:, 0:1], cn_ref[:, 64:65]]
        rows = _iota((tq, tk), 0) + i * tq

        def scores(j):
            c0 = pl.multiple_of(j * tk, tk)
            kb = kv_ref[pl.ds(c0, tk), 128:256].astype(BF16)
            return tuple(_dot(qm[h], kb, NT) + (cqs[h] - ct_ref[h:h + 1, pl.ds(c0, tk)]) for h in range(2))

        def absorb(j, state, s01, masked):
            c0 = pl.multiple_of(j * tk, tk)
            vblk = kv_ref[pl.ds(c0, tk), 256:384]
            vx = [jnp.where(first, vblk, 1.0).astype(BF16), jnp.where(first, 1.0, vblk).astype(BF16)]
            new = []
            for h in range(2):
                m, acc, s = state[2 * h], state[2 * h + 1], s01[h]
                if masked:
                    s = jnp.where(rows >= _iota((tq, tk), 1) + j * tk, s, MASK_VALUE)
                m_new = jnp.maximum(m, jnp.max(s, axis=1, keepdims=True))
                p = jnp.exp(s - m_new).astype(BF16)
                new += [m_new, jnp.exp(m - m_new) * acc + jnp.dot(p, vx[h], preferred_element_type=F32)]
            return tuple(new)

        def kv_step(j, carry):
            ahead = scores(j + 1)
            return absorb(j, carry[:4], carry[4:], False) + ahead

        init = (jnp.full((tq, 1), MASK_VALUE, F32), jnp.zeros((tq, 128), F32)) * 2
        n_full = (i * tq) // tk
        carry = lax.fori_loop(0, n_full, kv_step, init + scores(0))
        m0, acc0, m1, acc1 = absorb(n_full, carry[:4], carry[4:], True)
        l0, l1 = pltpu.roll(acc0, 64, 1), pltpu.roll(acc1, 64, 1)
        o = jnp.where(first, acc0 / l0, acc1 / l1)
        or_ref[...] = o
        og_ref[...] = o * _silu(qblk[:, 384:512])
        lse_ref[...] = jnp.where(first, m0 + jnp.log(l0), m1 + jnp.log(l1))

    out = jax.ShapeDtypeStruct((bsz, t, FOX_W), F32)
    blk = pl.BlockSpec((None, tq, 128), lambda b, p, i: (b, i, p))
    return pl.pallas_call(
        body, grid=(bsz, 4, nq),
        in_specs=[pl.BlockSpec((None, tq, 512), lambda b, p, i: (b, i, C_BLK0 + p)),
                  pl.BlockSpec((None, t, 512), lambda b, p, i: (b, 0, C_BLK0 + p)),
                  blk,
                  pl.BlockSpec((None, None, 8, t), lambda b, p, i: (b, p, 0, 0))],
        out_specs=[blk, blk, blk],
        out_shape=[out, out, out],
        compiler_params=_cparams(("parallel", "parallel", "arbitrary")), name=name)(proj3, proj3, c_nat, c_t)


def _fox_bwd(proj3, o_raw, dmixed, lse, c_nat, c_t, name):
    bsz, t, _ = proj3.shape
    tq = _fox_tile(t)
    nq = t // tq
    tk = min(2 * tq, t)
    ratio = tk // tq

    def body(a_ref, or_ref, do_ref, lse_ref, cn_ref, ct_ref, dc_out, dct_out, drow_out, dq_sc, do_sc, dl_sc):
        def prep(i, c):
            r0 = pl.multiple_of(i * tq, tq)
            g = a_ref[pl.ds(r0, tq), 384:512]
            dout = do_ref[pl.ds(r0, tq), :]
            o = or_ref[pl.ds(r0, tq), :]
            dc_out[pl.ds(r0, tq), 384:512] = dout * o * _dsilu(g)
            do = dout * _silu(g)
            do_sc[pl.ds(r0, tq), :] = do
            prod = do * o
            d0 = jnp.sum(prod[:, 0:64], axis=1, keepdims=True)
            d1 = jnp.sum(prod[:, 64:128], axis=1, keepdims=True)
            dl_sc[pl.ds(r0, tq), :] = jnp.concatenate([jnp.broadcast_to(d0, (tq, 64)), jnp.broadcast_to(d1, (tq, 64))], axis=1)
            dq_sc[pl.ds(r0, tq), :] = jnp.zeros((tq, 128), F32)
            drow_out[pl.ds(r0, tq), :] = jnp.zeros((tq, 128), F32)
            return c

        lax.fori_loop(0, nq, prep, 0)
        dct_out[...] = jnp.zeros((8, t), F32)

        first = _iota((1, 128), 1) < 64

        def heads(v):
            return [jnp.where(first, v, 0.0).astype(BF16), jnp.where(first, 0.0, v).astype(BF16)]

        def kv_tile(j, c):
            c0 = pl.multiple_of(j * tk, tk)
            kb = a_ref[pl.ds(c0, tk), 128:256].astype(BF16)
            vb = a_ref[pl.ds(c0, tk), 256:384].astype(BF16)
            cks = [ct_ref[h:h + 1, pl.ds(c0, tk)] for h in range(2)]

            def q_step(i, carry, diagonal):
                dk, dv, dcol0, dcol1 = carry
                r0 = pl.multiple_of(i * tq, tq)
                causal = _iota((tq, tk), 0) + i * tq >= _iota((tq, tk), 1) + j * tk
                qv = a_ref[pl.ds(r0, tq), 0:128] * 0.125
                do = do_sc[pl.ds(r0, tq), :]
                qb, dob = qv.astype(BF16), do.astype(BF16)
                qm, dom = heads(qv), heads(do)
                full, dcols, rsums = [], [], []
                for h in range(2):
                    lse_h = lse_ref[pl.ds(r0, tq), 64 * h:64 * h + 1]
                    dl_h = dl_sc[pl.ds(r0, tq), 64 * h:64 * h + 1]
                    cq = cn_ref[pl.ds(r0, tq), 64 * h:64 * h + 1]
                    p = jnp.exp(_dot(qm[h], kb, NT) + (cq - cks[h]) - lse_h)
                    if diagonal:
                        p = jnp.where(causal, p, 0.0)
                    ds = p * (_dot(dom[h], vb, NT) - dl_h)
                    dsb = ds.astype(BF16)
                    full.append((_dot(p.astype(BF16), dob, TN), _dot(dsb, qb, TN),
                                 jnp.dot(dsb, kb, preferred_element_type=F32)))
                    dcols.append(jnp.sum(ds, axis=0, keepdims=True))
                    rsums.append(jnp.broadcast_to(jnp.sum(ds, axis=1, keepdims=True), (tq, 128)))
                dq_sc[pl.ds(r0, tq), :] += jnp.where(first, full[0][2], full[1][2]) * 0.125
                drow_out[pl.ds(r0, tq), :] += jnp.where(first, rsums[0], rsums[1])
                return (dk + jnp.where(first, full[0][1], full[1][1]), dv + jnp.where(first, full[0][0], full[1][0]),
                        dcol0 - dcols[0], dcol1 - dcols[1])

            carry = (jnp.zeros((tk, 128), F32), jnp.zeros((tk, 128), F32), jnp.zeros((1, tk), F32), jnp.zeros((1, tk), F32))
            for r in range(ratio):
                carry = q_step(ratio * j + r, carry, True)
            dk, dv, dcol0, dcol1 = lax.fori_loop(ratio * (j + 1), nq, functools.partial(q_step, diagonal=False), carry)
            dct_out[0:1, pl.ds(c0, tk)] = dcol0
            dct_out[1:2, pl.ds(c0, tk)] = dcol1
            dc_out[pl.ds(c0, tk), 128:256] = dk
            dc_out[pl.ds(c0, tk), 256:384] = dv
            return c

        lax.fori_loop(0, t // tk, kv_tile, 0)
        dc_out[:, 0:128] = dq_sc[...]

    blk = pl.BlockSpec((None, t, 128), lambda b, p: (b, 0, p))
    return pl.pallas_call(
        body, grid=(bsz, 4),
        in_specs=[pl.BlockSpec((None, t, 512), lambda b, p: (b, 0, C_BLK0 + p)),
                  blk,
                  pl.BlockSpec((None, t, 128), lambda b, p: (b, 0, 4 + p)),
                  blk, blk,
                  pl.BlockSpec((None, None, 8, t), lambda b, p: (b, p, 0, 0))],
        out_specs=[pl.BlockSpec((None, t, 512), lambda b, p: (b, 0, p)),
                   pl.BlockSpec((None, None, 8, t), lambda b, p: (b, p, 0, 0)), blk],
        out_shape=[jax.ShapeDtypeStruct((bsz, t, C_W), F32), jax.ShapeDtypeStruct((bsz, 4, 8, t), F32),
                   jax.ShapeDtypeStruct((bsz, t, FOX_W), F32)],
        scratch_shapes=[pltpu.VMEM((t, 128), F32), pltpu.VMEM((t, 128), F32), pltpu.VMEM((t, 128), F32)],
        compiler_params=_cparams(("parallel", "parallel")), name=name)(proj3, o_raw, dmixed, lse, c_nat, c_t)


def _mix_tm(n):
    return min(512, n)


def _outproj_fwd(x2, oa, ob, oc, wo, g_row, name):
    n, d = x2.shape
    tm = _mix_tm(n)

    def body(x_ref, oa_ref, ob_ref, oc_ref, w_ref, g_ref, y_ref, xo_ref):
        y = (jnp.dot(oa_ref[...].astype(BF16), w_ref[0:256, :], preferred_element_type=F32)
             + jnp.dot(ob_ref[...].astype(BF16), w_ref[256:512, :], preferred_element_type=F32)
             + jnp.dot(oc_ref[...].astype(BF16), w_ref[512:1024, :], preferred_element_type=F32))
        y_ref[...] = y
        xo_ref[...] = x_ref[...] + y * _rstd(y) * g_ref[...]

    row = lambda w: pl.BlockSpec((tm, w), lambda i: (i, 0))
    out = jax.ShapeDtypeStruct((n, d), F32)
    return pl.pallas_call(
        body, grid=(n // tm,),
        in_specs=[row(d), row(256), row(256), row(512), pl.BlockSpec((d, d), lambda i: (0, 0)),
                  pl.BlockSpec((1, d), lambda i: (0, 0))],
        out_specs=[row(d), row(d)], out_shape=[out, out],
        compiler_params=_cparams(("parallel",)), name=name)(x2, oa, ob, oc, wo, g_row)


def _outproj_fwd_loss(x2, oa, ob, oc, wo, g_row, target2, name):
    n, d = x2.shape
    tm = _mix_tm(n)

    def body(x_ref, oa_ref, ob_ref, oc_ref, w_ref, g_ref, t_ref, y_ref, dx_ref, l_ref):
        y = (jnp.dot(oa_ref[...].astype(BF16), w_ref[0:256, :], preferred_element_type=F32)
             + jnp.dot(ob_ref[...].astype(BF16), w_ref[256:512, :], preferred_element_type=F32)
             + jnp.dot(oc_ref[...].astype(BF16), w_ref[512:1024, :], preferred_element_type=F32))
        y_ref[...] = y
        err = (x_ref[...] + y * _rstd(y) * g_ref[...]) - t_ref[...]
        dx_ref[...] = err * (1.0 / d)

        @pl.when(pl.program_id(0) == 0)
        def _():
            l_ref[...] = jnp.zeros((8, 128), F32)

        l_ref[...] += jnp.sum(err * err)

    row = lambda w: pl.BlockSpec((tm, w), lambda i: (i, 0))
    out = jax.ShapeDtypeStruct((n, d), F32)
    return pl.pallas_call(
        body, grid=(n // tm,),
        in_specs=[row(d), row(256), row(256), row(512), pl.BlockSpec((d, d), lambda i: (0, 0)),
                  pl.BlockSpec((1, d), lambda i: (0, 0)), row(d)],
        out_specs=[row(d), row(d), pl.BlockSpec((8, 128), lambda i: (0, 0))],
        out_shape=[out, out, jax.ShapeDtypeStruct((8, 128), F32)],
        compiler_params=_cparams(("arbitrary",)), name=name)(x2, oa, ob, oc, wo, g_row, target2)


def _outproj_bwd(dxo, y, oa, ob, oc, wo, g_row, name):
    n, d = dxo.shape
    tm = _mix_tm(n)

    def body(dx_ref, y_ref, oa_ref, ob_ref, oc_ref, w_ref, g_ref, dm_ref, dw_ref, dg_ref):
        @pl.when(pl.program_id(0) == 0)
        def _():
            dw_ref[...] = jnp.zeros((d, d), F32)
            dg_ref[...] = jnp.zeros((8, d), F32)

        yv, dx = y_ref[...], dx_ref[...]
        r = _rstd(yv)
        yn = yv * r
        dg_ref[...] += jnp.sum(dx * yn, axis=0, keepdims=True)
        dyn = dx * g_ref[...]
        dy = (r * (dyn - yn * jnp.mean(dyn * yn, axis=-1, keepdims=True))).astype(BF16)
        dm_ref[...] = _dot(dy, w_ref[...], NT)
        dw_ref[0:256, :] += _dot(oa_ref[...].astype(BF16), dy, TN)
        dw_ref[256:512, :] += _dot(ob_ref[...].astype(BF16), dy, TN)
        dw_ref[512:1024, :] += _dot(oc_ref[...].astype(BF16), dy, TN)

    row = lambda w: pl.BlockSpec((tm, w), lambda i: (i, 0))
    fixed = lambda r, c: pl.BlockSpec((r, c), lambda i: (0, 0))
    return pl.pallas_call(
        body, grid=(n // tm,),
        in_specs=[row(d), row(d), row(256), row(256), row(512), fixed(d, d), fixed(1, d)],
        out_specs=[row(d), fixed(d, d), fixed(8, d)],
        out_shape=[jax.ShapeDtypeStruct((n, d), F32), jax.ShapeDtypeStruct((d, d), F32), jax.ShapeDtypeStruct((8, d), F32)],
        compiler_params=_cparams(("arbitrary",)), name=name)(dxo, y, oa, ob, oc, wo, g_row)


_PIECES = ((0, A_W), (A_W, B_W), (A_W + B_W, C_W), (A_W + B_W + C_W, F_W))


def _inproj_bwd_x(x2, dxo, g_row, w_int, pieces, name):
    n, d = x2.shape
    tm = min(256, n)

    def body(x_ref, dxo_ref, g_ref, w_ref, da_ref, db_ref, dc_ref, df_ref, dx_ref, dg_ref):
        @pl.when(pl.program_id(0) == 0)
        def _():
            dg_ref[...] = jnp.zeros((8, d), F32)

        dh = jnp.zeros((tm, d), F32)
        for ref, (o, w) in zip((da_ref, db_ref, dc_ref, df_ref), _PIECES):
            dh = dh + _dot(ref[...].astype(BF16), w_ref[:, o:o + w], NT)
        x = x_ref[...]
        r = _rstd(x)
        xn = x * r
        dg_ref[...] += jnp.sum(dh * xn, axis=0, keepdims=True)
        dxn = dh * g_ref[...]
        dx_ref[...] = dxo_ref[...] + r * (dxn - xn * jnp.mean(dxn * xn, axis=-1, keepdims=True))

    row = lambda w: pl.BlockSpec((tm, w), lambda i: (i, 0))
    fixed = lambda r, c: pl.BlockSpec((r, c), lambda i: (0, 0))
    return pl.pallas_call(
        body, grid=(n // tm,),
        in_specs=[row(d), row(d), fixed(1, d), fixed(d, E_INT)] + [row(w) for _, w in _PIECES],
        out_specs=[row(d), fixed(8, d)],
        out_shape=[jax.ShapeDtypeStruct((n, d), F32), jax.ShapeDtypeStruct((8, d), F32)],
        compiler_params=_cparams(("arbitrary",)), name=name)(x2, dxo, g_row, w_int, *pieces)


def _inproj_bwd_w(x2, g_row, pieces, name):
    n, d = x2.shape
    tm = min(256, n)

    def body(x_ref, g_ref, da_ref, db_ref, dc_ref, df_ref, dw_ref):
        @pl.when(pl.program_id(0) == 0)
        def _():
            dw_ref[...] = jnp.zeros((d, E_INT), F32)

        x = x_ref[...]
        h = (x * _rstd(x) * g_ref[...]).astype(BF16)
        for ref, (o, w) in zip((da_ref, db_ref, dc_ref, df_ref), _PIECES):
            dw_ref[:, o:o + w] += _dot(h, ref[...].astype(BF16), TN)

    row = lambda w: pl.BlockSpec((tm, w), lambda i: (i, 0))
    return pl.pallas_call(
        body, grid=(n // tm,),
        in_specs=[row(d), pl.BlockSpec((1, d), lambda i: (0, 0))] + [row(w) for _, w in _PIECES],
        out_specs=pl.BlockSpec((d, E_INT), lambda i: (0, 0)),
        out_shape=jax.ShapeDtypeStruct((d, E_INT), F32),
        compiler_params=_cparams(("arbitrary",), vmem_mb=56), name=name)(x2, g_row, *pieces)


def _block_diag(pool_w_l):
    z = jnp.zeros((64, 64), pool_w_l.dtype)
    return jnp.concatenate(
        [jnp.concatenate([pool_w_l[g] if c == g else z for c in range(4)], axis=1) for g in range(4)], axis=0)


def _pad_lanes(v, width=128):
    return jnp.pad(v, ((0, 0),) * (v.ndim - 1) + ((0, width - v.shape[-1]),))


def _local_step(x, target, lower_bounds, pre_norm_g, w_in_int, hgrn_norm_g, fox_f_bias, pool_w, pool_scale,
                w_out_bf, post_norm_g, on_weight_grads):
    bsz, t, d = x.shape
    n = bsz * t
    lbs = _lbs_fwd(lower_bounds)
    saved = []
    xc = x.reshape(n, d)
    for l in range(DEPTH):
        proj = _inproj_fwd(xc, pre_norm_g[l:l + 1], w_in_int[l], f"inproj_fwd{l}").reshape(bsz, t, E_INT)
        wbd = _block_diag(pool_w[l]).astype(BF16)
        bias_row = _pad_lanes(fox_f_bias[l:l + 1])
        oa, oa_raw, states = _hgrn_fwd(proj, lbs[l:l + 1], hgrn_norm_g[l:l + 1], f"hgrn_fwd{l}")
        ob = _pool_fwd(proj, wbd, pool_scale[l:l + 1], f"pool_fwd{l}")
        c_nat, c_t = _foxgate_fwd(proj, bias_row, f"foxgate_fwd{l}")
        oc, oc_raw, lse = _fox_fwd(proj, c_nat, c_t, f"fox_fwd{l}")
        mixed = (oa.reshape(n, -1), ob.reshape(n, -1), oc.reshape(n, -1))
        if l < DEPTH - 1:
            y, xn = _outproj_fwd(xc, *mixed, w_out_bf[l], post_norm_g[l:l + 1], f"outproj_fwd{l}")
        else:
            y, dx, sq = _outproj_fwd_loss(xc, *mixed, w_out_bf[l], post_norm_g[l:l + 1], target.reshape(n, d),
                                          f"outproj_fwd{l}")
        saved.append((xc, proj, wbd, bias_row, oa, oa_raw, states, ob, oc, oc_raw, lse, c_nat, c_t, y))
        xc = xn
    g = {k: [None] * DEPTH for k in ("pre", "hgn", "bias", "pool_w", "pool_scale", "post", "lbs")}
    handed = [None] * DEPTH
    for l in reversed(range(DEPTH)):
        xin, proj, wbd, bias_row, oa, oa_raw, states, ob, oc, oc_raw, lse, c_nat, c_t, y = saved[l]
        dmix, d_w_out, dpost = _outproj_bwd(dx, y, oa.reshape(n, -1), ob.reshape(n, -1), oc.reshape(n, -1),
                                            w_out_bf[l], post_norm_g[l:l + 1], f"outproj_bwd{l}")
        g["post"][l] = dpost[0]
        dmix3 = dmix.reshape(bsz, t, d)
        d_c, dct, drow = _fox_bwd(proj, oc_raw, dmix3, lse, c_nat, c_t, f"fox_bwd{l}")
        dc_nat = _pad_lanes(dct[:, :, 0:2, :].reshape(bsz, FOX_HEADS, t).transpose(0, 2, 1)
                            + drow.reshape(bsz, t, FOX_HEADS, 64)[..., 0])
        d_f, dbias = _foxgate_bwd(proj, dc_nat, bias_row, f"foxgate_bwd{l}")
        g["bias"][l] = jnp.sum(dbias[:, 0, :FOX_HEADS], axis=0)
        d_b, dscale, dwbd = _pool_bwd(proj, dmix3, wbd, pool_scale[l:l + 1], f"pool_bwd{l}")
        g["pool_scale"][l] = jnp.sum(dscale[:, 0], axis=0)
        dwbd = jnp.sum(dwbd, axis=0)
        g["pool_w"][l] = jnp.stack([dwbd[64 * k:64 * (k + 1), 64 * k:64 * (k + 1)] for k in range(4)])
        d_a, dgn, dlb = _hgrn_bwd(proj, oa_raw, dmix3, states, lbs[l:l + 1], hgrn_norm_g[l:l + 1], f"hgrn_bwd{l}")
        g["hgn"][l] = jnp.sum(dgn[:, 0], axis=0)
        g["lbs"][l] = jnp.sum(dlb[:, 0], axis=0)
        pieces = [p.reshape(n, -1) for p in (d_a, d_b, d_c, d_f)]
        handed[l] = on_weight_grads(l, _inproj_bwd_w(xin, pre_norm_g[l:l + 1], pieces, f"inproj_bwd_w{l}"), d_w_out)
        dx, dpre = _inproj_bwd_x(xin, dx, pre_norm_g[l:l + 1], w_in_int[l], pieces, f"inproj_bwd_x{l}")
        g["pre"][l] = dpre[0]
    grads = {k: jnp.stack(v) for k, v in g.items()}
    return sq, dx.reshape(bsz, t, d), grads, handed


def _place():
    return lax.axis_index("x"), lax.axis_index("y"), lax.axis_index("c")


def _other_chips(x, y):
    return [(1 - x, y), (x, 1 - y), (1 - x, 1 - y)]


_ANY = pl.BlockSpec(memory_space=pl.ANY)


def _gather_body(handshake, n_arrays):
    def body(*refs):
        srcs, dsts = refs[:n_arrays], refs[n_arrays:2 * n_arrays]
        ici_send, ici_recv, d2d_send, d2d_recv, local_sems = refs[2 * n_arrays:]
        x, y, c = _place()
        if handshake:
            barrier = pltpu.get_barrier_semaphore()
            for peer in [(px, py, c) for px, py in _other_chips(x, y)] + [(x, y, 1 - c)]:
                pl.semaphore_signal(barrier, inc=1, device_id=peer, device_id_type=MESH)
            pl.semaphore_wait(barrier, 4)
        me = 2 * x + y
        pairs = list(zip(srcs, dsts))
        order = [(k, j) for k in range(3) for j in range(n_arrays)]
        mine = [pltpu.make_async_copy(src, dst.at[me], local_sems.at[j]) for j, (src, dst) in enumerate(pairs)]
        for cp in mine:
            cp.start()
        chips = _other_chips(x, y)
        sends = [pltpu.make_async_remote_copy(
            src_ref=pairs[j][0].at[c], dst_ref=pairs[j][1].at[me, c], send_sem=ici_send.at[n], recv_sem=ici_recv.at[n],
            device_id=(chips[k][0], chips[k][1], c), device_id_type=MESH) for n, (k, j) in enumerate(order)]
        for cp in sends:
            cp.start()
        passed = [pltpu.make_async_remote_copy(
            src_ref=pairs[j][1].at[2 * chips[k][0] + chips[k][1], c], dst_ref=pairs[j][1].at[2 * chips[k][0] + chips[k][1], c],
            send_sem=d2d_send.at[n], recv_sem=d2d_recv.at[n], device_id=(x, y, 1 - c), device_id_type=MESH)
            for n, (k, j) in enumerate(order)]
        for n, (k, j) in enumerate(order):
            px, py = chips[k]
            src, dst = pairs[j]
            pltpu.make_async_remote_copy(
                src_ref=src.at[c], dst_ref=dst.at[2 * px + py, c], send_sem=ici_send.at[n], recv_sem=ici_recv.at[n],
                device_id=(px, py, c), device_id_type=MESH).wait_recv()
            passed[n].start()
        for n, (k, j) in enumerate(order):
            px, py = chips[k]
            src, dst = pairs[j]
            pltpu.make_async_remote_copy(
                src_ref=dst.at[2 * px + py, 1 - c], dst_ref=dst.at[2 * px + py, 1 - c], send_sem=d2d_send.at[n],
                recv_sem=d2d_recv.at[n], device_id=(x, y, 1 - c), device_id_type=MESH).wait_recv()
        for cp in sends + passed:
            cp.wait_send()
        for cp in mine:
            cp.wait()

    return body


def _gather_sems(n_arrays):
    return [pltpu.SemaphoreType.DMA((3 * n_arrays,))] * 4 + [pltpu.SemaphoreType.DMA((n_arrays,))]


def _gathered(a):
    return jax.ShapeDtypeStruct((N_CHIPS,) + a.shape, a.dtype)


def _gather_weights(arrays):
    n = len(arrays)
    return pl.pallas_call(
        _gather_body(False, n), in_specs=[_ANY] * n, out_specs=[_ANY] * n, out_shape=[_gathered(a) for a in arrays],
        scratch_shapes=_gather_sems(n), name="gather_weights")(*arrays)


def _gather_weights_beside(arrays):
    hbm = pltpu.MemorySpace.HBM
    n = len(arrays)
    srcs = [jax.new_ref(a, memory_space=hbm) for a in arrays]
    dsts = [jax.empty_ref(_gathered(a), memory_space=hbm) for a in arrays]
    body = _gather_body(True, n)

    @pl.kernel(mesh=plsc.ScalarSubcoreMesh(axis_name="sequencer", num_cores=1), name="gather_weights_beside",
               scratch_types=_gather_sems(n), compiler_params=pltpu.CompilerParams(collective_id=1))
    def launch(*sems):
        body(*srcs, *dsts, *sems)

    launch()
    return [d[...] for d in dsts]


def _swap_with_sibling(parts, name):
    k = len(parts)

    def body(*refs):
        src, dst = refs[:k], refs[k:2 * k]
        send_sems, recv_sems = refs[2 * k:]
        x, y, c = _place()
        cps = [pltpu.make_async_remote_copy(src_ref=src[j], dst_ref=dst[j], send_sem=send_sems.at[j], recv_sem=recv_sems.at[j],
                                            device_id=(x, y, 1 - c), device_id_type=MESH) for j in range(k)]
        for cp in cps:
            cp.start()
        for cp in cps:
            cp.wait()

    return pl.pallas_call(
        body, in_specs=[_ANY] * k, out_specs=[_ANY] * k,
        out_shape=[jax.ShapeDtypeStruct(p.shape, p.dtype) for p in parts],
        scratch_shapes=[pltpu.SemaphoreType.DMA((k,)), pltpu.SemaphoreType.DMA((k,))], name=name)(*parts)


N_PEERS = 7


def _grad_exchange_body():
    def body(pin_ref, pout_ref, lin_ref, lout_ref, send_sems, recv_sems):
        x, y, c = _place()
        barrier = pltpu.get_barrier_semaphore()
        for k in range(1, N_PEERS + 1):
            peer = (x ^ ((k >> 2) & 1), y ^ ((k >> 1) & 1), c ^ (k & 1))
            pl.semaphore_signal(barrier, inc=1, device_id=peer, device_id_type=MESH)
        pl.semaphore_wait(barrier, N_PEERS)
        me = 2 * x + y
        pairs = ((pin_ref, lin_ref), (pout_ref, lout_ref))
        cps = []
        for k, (px, py) in enumerate(_other_chips(x, y)):
            for r in range(2):
                for j, (src, dst) in enumerate(pairs):
                    cps.append(pltpu.make_async_remote_copy(
                        src_ref=src.at[2 * px + py, r], dst_ref=dst.at[2 * k + c], send_sem=send_sems.at[2 * (2 * k + r) + j],
                        recv_sem=recv_sems.at[2 * (2 * k + c) + j], device_id=(px, py, r), device_id_type=MESH))
        for j, (src, dst) in enumerate(pairs):
            cps.append(pltpu.make_async_remote_copy(
                src_ref=src.at[me, 1 - c], dst_ref=dst.at[N_PEERS - 1], send_sem=send_sems.at[2 * (N_PEERS - 1) + j],
                recv_sem=recv_sems.at[2 * (N_PEERS - 1) + j], device_id=(x, y, 1 - c), device_id_type=MESH))
        for cp in cps:
            cp.start()
        for s in range(N_PEERS):
            for j, (src, dst) in enumerate(pairs):
                pltpu.make_async_remote_copy(
                    src_ref=src.at[0, 0], dst_ref=dst.at[s], send_sem=send_sems.at[2 * s + j], recv_sem=recv_sems.at[2 * s + j],
                    device_id=(x, y, 1 - c), device_id_type=MESH).wait_recv()
        for cp in cps:
            cp.wait_send()

    return body


_EXCHANGE_SEMS = [pltpu.SemaphoreType.DMA((2 * N_PEERS,))] * 2


def _landing(p):
    return jax.ShapeDtypeStruct((N_PEERS,) + p.shape[2:], p.dtype)


def _grad_exchange_beside(pin, pout, name, collective_id):
    hbm = pltpu.MemorySpace.HBM
    pin_ref, pout_ref = jax.new_ref(pin, memory_space=hbm), jax.new_ref(pout, memory_space=hbm)
    lin_ref, lout_ref = jax.empty_ref(_landing(pin), memory_space=hbm), jax.empty_ref(_landing(pout), memory_space=hbm)
    body = _grad_exchange_body()

    @pl.kernel(mesh=plsc.ScalarSubcoreMesh(axis_name="sequencer", num_cores=1), name=name,
               scratch_types=_EXCHANGE_SEMS, compiler_params=pltpu.CompilerParams(collective_id=collective_id))
    def launch(send_sems, recv_sems):
        body(pin_ref, pout_ref, lin_ref, lout_ref, send_sems, recv_sems)

    launch()
    return lin_ref[...], lout_ref[...]


def _add_n(parts, name, with_bf16=False):
    r, c = parts[0].shape
    tr = 256 if r % 256 == 0 else r
    n = len(parts)

    def body(*refs):
        acc = refs[0][...].astype(F32)
        for ref in refs[1:n]:
            acc = acc + ref[...].astype(F32)
        refs[n][...] = acc
        if with_bf16:
            refs[n + 1][...] = acc.astype(BF16)

    blk = pl.BlockSpec((tr, c), lambda i: (i, 0))
    outs = [jax.ShapeDtypeStruct((r, c), F32)] + ([jax.ShapeDtypeStruct((r, c), BF16)] if with_bf16 else [])
    res = pl.pallas_call(
        body, grid=(r // tr,), in_specs=[blk] * n, out_specs=[blk] * len(outs),
        out_shape=outs, compiler_params=_cparams(("parallel",)), name=name)(*parts)
    return res if with_bf16 else res[0]


def _all_reduce_small(packet):
    r, w = packet.shape

    def body(p_ref, o_ref, buf, send_sems, recv_sems):
        x, y, c = _place()
        me = 4 * x + 2 * y + c
        buf[me] = p_ref[...]
        peers = []
        for k in range(1, 8):
            fx, fy, fc = (k >> 2) & 1, (k >> 1) & 1, k & 1
            peers.append((x ^ fx, y ^ fy, c ^ fc))
        cps = [pltpu.make_async_remote_copy(src_ref=p_ref, dst_ref=buf.at[me], send_sem=send_sems.at[k], recv_sem=recv_sems.at[k],
                                            device_id=peer, device_id_type=MESH) for k, peer in enumerate(peers)]
        for cp in cps:
            cp.start()
        for k, (px, py, pc) in enumerate(peers):
            pltpu.make_async_remote_copy(src_ref=p_ref, dst_ref=buf.at[4 * px + 2 * py + pc], send_sem=send_sems.at[k],
                                         recv_sem=recv_sems.at[k], device_id=(px, py, pc), device_id_type=MESH).wait_recv()
        for cp in cps:
            cp.wait_send()
        acc = buf[0]
        for k in range(1, 8):
            acc = acc + buf[k]
        o_ref[...] = acc

    vm = pl.BlockSpec(memory_space=pltpu.VMEM)
    return pl.pallas_call(
        body, in_specs=[vm], out_specs=vm, out_shape=jax.ShapeDtypeStruct((r, w), F32),
        scratch_shapes=[pltpu.VMEM((8, r, w), F32), pltpu.SemaphoreType.DMA((7,)), pltpu.SemaphoreType.DMA((7,))],
        name="all_reduce_small")(packet)


def _adamw_math(w, g, m, v):
    m = ADAM_B1 * m + (1.0 - ADAM_B1) * g
    v = ADAM_B2 * v + (1.0 - ADAM_B2) * (g * g)
    m_hat = m / (1.0 - ADAM_B1 ** ADAM_STEP)
    v_hat = v / (1.0 - ADAM_B2 ** ADAM_STEP)
    return -ADAM_LR * (m_hat / (jnp.sqrt(v_hat) + ADAM_EPS) + ADAM_WD * w), m, v


def _adamw(w, g, m, v, name):
    nl, r, c = w.shape
    tr = 256 if r % 256 == 0 else r

    def body(w_ref, g_ref, m_ref, v_ref, d_ref, mo_ref, vo_ref):
        d_ref[...], mo_ref[...], vo_ref[...] = _adamw_math(w_ref[...], g_ref[...], m_ref[...], v_ref[...])

    blk = pl.BlockSpec((None, tr, c), lambda l, i: (l, i, 0))
    out = jax.ShapeDtypeStruct(w.shape, F32)
    return pl.pallas_call(
        body, grid=(nl, r // tr), in_specs=[blk] * 4, out_specs=[blk] * 3, out_shape=[out] * 3,
        compiler_params=_cparams(("parallel", "parallel")), name=name)(w, g, m, v)


def _small_update(gsum, lower_bounds, wpack, mpack, vpack):
    r, w = gsum.shape
    lb_rows = DEPTH * HGRN_W // 128

    def body(g_ref, a_ref, w_ref, m_ref, v_ref, go_ref, d_ref, mo_ref, vo_ref):
        a = a_ref[...]
        a0, a1 = a[0:1], a[1:2]
        mx = jnp.maximum(a0, a1)
        e0, e1 = jnp.exp(a0 - mx), jnp.exp(a1 - mx)
        p0, p1 = e0 / (e0 + e1), e1 / (e0 + e1)
        g = g_ref[...]
        half = lb_rows // 2
        dl0 = jnp.concatenate([g[k:k + 1] for k in range(half)], axis=1)
        dl1 = jnp.concatenate([g[half + k:half + k + 1] for k in range(half)], axis=1)
        dp0 = (dl0 + dl1) - (dl0 + dl1)
        dp1 = dl1
        inner = p0 * dp0 + p1 * dp1
        da0, da1 = p0 * (dp0 - inner), p1 * (dp1 - inner)
        rows = [da0[:, 128 * k:128 * (k + 1)] for k in range(half)] + [da1[:, 128 * k:128 * (k + 1)] for k in range(half)]
        gfull = jnp.concatenate(rows + [g[lb_rows:]], axis=0)
        go_ref[...] = gfull
        d_ref[...], mo_ref[...], vo_ref[...] = _adamw_math(w_ref[...], gfull, m_ref[...], v_ref[...])

    vm = pl.BlockSpec(memory_space=pltpu.VMEM)
    out = jax.ShapeDtypeStruct((r, w), F32)
    return pl.pallas_call(body, in_specs=[vm] * 5, out_specs=[vm] * 4, out_shape=[out] * 4, name="small_update")(
        gsum, lower_bounds, wpack, mpack, vpack)


_SMALL = ("lower_bounds", "pre_norm_g", "hgrn_norm_g", "fox_f_bias", "pool_w", "pool_scale", "post_norm_g")


def _pack(parts):
    rows = []
    for k in _SMALL:
        f = parts[k].reshape(-1)
        pad = (-f.shape[0]) % (8 * 128)
        rows.append(jnp.pad(f, (0, pad)).reshape(-1, 128))
    rows.append(jnp.zeros((8, 128), F32))
    return jnp.concatenate(rows, axis=0)


def _unpack(pack, like):
    out, r = {}, 0
    for k in _SMALL:
        size = int(np.prod(like[k].shape))
        nr = -(-size // (8 * 128)) * 8
        out[k] = pack[r:r + nr].reshape(-1)[:size].reshape(like[k].shape)
        r += nr
    return out, r


def kernel(x, lower_bounds, pre_norm_g, w_in, hgrn_norm_g, fox_f_bias, pool_w, pool_scale, w_out, post_norm_g, loss_target, m_lower_bounds, m_pre_norm_g, m_w_in, m_hgrn_norm_g, m_fox_f_bias, m_pool_w, m_pool_scale, m_w_out, m_post_norm_g, v_lower_bounds, v_pre_norm_g, v_w_in, v_hgrn_norm_g, v_fox_f_bias, v_pool_w, v_pool_scale, v_w_out, v_post_norm_g):
    cx, cy, cc = _place()
    chip = 2 * cx + cy

    halves = lambda w, l: w[l].reshape(2, w.shape[1] // 2, w.shape[2]).astype(BF16)
    needed_first = _gather_weights([halves(w_in, 0)])
    needed_first, later = lax.optimization_barrier((needed_first, [halves(w_out, 0), halves(w_in, 1), halves(w_out, 1)]))
    later = _gather_weights_beside(later)
    w_in_int = [_internal_from_shards([a[q].reshape(D_MODEL, SHARD_W) for q in range(N_CHIPS)]) for a in (needed_first[0], later[1])]
    w_out_full = [a.reshape(D_MODEL, D_MODEL) for a in (later[0], later[2])]

    def on_weight_grads(l, d_w_in, d_w_out):
        pin = _shards_from_internal(d_w_in).reshape(N_CHIPS, 2, D_MODEL // 2, SHARD_W)
        pout = d_w_out.reshape(N_CHIPS, 2, D_MODEL // (2 * N_CHIPS), D_MODEL)
        own = [lax.dynamic_index_in_dim(lax.dynamic_index_in_dim(p, chip, 0, False), cc, 0, False) for p in (pin, pout)]
        return own, _grad_exchange_beside(pin.astype(BF16), pout.astype(BF16), f"grad_exchange{l}", 2 + l)

    sq, grad_x, g, handed = _local_step(x, loss_target, lower_bounds, pre_norm_g, w_in_int, hgrn_norm_g, fox_f_bias,
                                        pool_w, pool_scale, w_out_full, post_norm_g, on_weight_grads)
    first = cc == 0

    def finish(l, own, landed):
        halves_l = [_add_n([o] + [t[s] for s in range(N_PEERS)], f"grad_sum{l}_{j}") for j, (o, t) in enumerate(zip(own, landed))]
        others = _swap_with_sibling(halves_l, f"grad_swap{l}")
        g_in, g_out = [jnp.where(first, jnp.concatenate([h, o], axis=0), jnp.concatenate([o, h], axis=0))[None]
                       for h, o in zip(halves_l, others)]
        return (g_in, g_out, _adamw(w_in[l:l + 1], g_in, m_w_in[l:l + 1], v_w_in[l:l + 1], f"adamw_w_in{l}"),
                _adamw(w_out[l:l + 1], g_out, m_w_out[l:l + 1], v_w_out[l:l + 1], f"adamw_w_out{l}"))

    grad_x, last = lax.optimization_barrier((grad_x, handed[1]))
    done = [None, finish(1, *last)]

    small = {"lower_bounds": g["lbs"], "pre_norm_g": g["pre"], "hgrn_norm_g": g["hgn"], "fox_f_bias": g["bias"],
             "pool_w": g["pool_w"], "pool_scale": g["pool_scale"], "post_norm_g": g["post"]}
    packet = _pack(small)
    nrows = packet.shape[0]
    packet = packet.at[nrows - 1].set(sq[0])
    gsum = _all_reduce_small(packet)
    loss = gsum[nrows - 1, 0] * (0.5 / D_MODEL)

    weights = {"lower_bounds": lower_bounds, "pre_norm_g": pre_norm_g, "hgrn_norm_g": hgrn_norm_g,
               "fox_f_bias": fox_f_bias, "pool_w": pool_w, "pool_scale": pool_scale, "post_norm_g": post_norm_g}
    moments_m = {"lower_bounds": m_lower_bounds, "pre_norm_g": m_pre_norm_g, "hgrn_norm_g": m_hgrn_norm_g,
                 "fox_f_bias": m_fox_f_bias, "pool_w": m_pool_w, "pool_scale": m_pool_scale, "post_norm_g": m_post_norm_g}
    moments_v = {"lower_bounds": v_lower_bounds, "pre_norm_g": v_pre_norm_g, "hgrn_norm_g": v_hgrn_norm_g,
                 "fox_f_bias": v_fox_f_bias, "pool_w": v_pool_w, "pool_scale": v_pool_scale, "post_norm_g": v_post_norm_g}
    gp, dp, mp, vp = _small_update(gsum, lower_bounds, _pack(weights), _pack(moments_m), _pack(moments_v))
    gs, _ = _unpack(gp, weights)
    ds, _ = _unpack(dp, weights)
    ms, _ = _unpack(mp, weights)
    vs, _ = _unpack(vp, weights)

    first_layer, _ = lax.optimization_barrier((handed[0], (done[1], gp, dp, mp, vp)))
    done[0] = finish(0, *first_layer)
    both = lambda pick: jnp.concatenate([pick(done[l]) for l in range(DEPTH)], axis=0)
    grad_w_in, grad_w_out = both(lambda r: r[0]), both(lambda r: r[1])
    d_in, m_in, v_in = [both(lambda r, k=k: r[2][k]) for k in range(3)]
    d_out, m_out, v_out = [both(lambda r, k=k: r[3][k]) for k in range(3)]

    def ordered(s, big_in, big_out):
        return (s["lower_bounds"], s["pre_norm_g"], big_in, s["hgrn_norm_g"], s["fox_f_bias"], s["pool_w"],
                s["pool_scale"], big_out, s["post_norm_g"])

    return (loss, grad_x, *ordered(gs, grad_w_in, grad_w_out), *ordered(ds, d_in, d_out),
            *ordered(ms, m_in, m_out), *ordered(vs, v_in, v_out))
```

```python
import functools

import numpy as np
import jax
import jax.numpy as jnp
from jax import lax
from jax.experimental import pallas as pl
from jax.experimental.pallas import tpu as pltpu
from jax.experimental.pallas import tpu_sc as plsc

F32 = jnp.float32
BF16 = jnp.bfloat16
HI = lax.Precision.HIGHEST
MESH = pl.DeviceIdType.MESH

NORM_EPS = 1e-6
MASK_VALUE = -1e30
TINY = 1e-30
ADAM_LR, ADAM_B1, ADAM_B2, ADAM_EPS, ADAM_WD, ADAM_STEP = 0.001, 0.9, 0.999, 1e-08, 0.01, 10

D_MODEL = 1024
DEPTH = 2
N_CHIPS = 4
CHUNK = 64
LANES = 128
HGRN_W, POOL_W, FOX_W, FOX_HEADS = 256, 256, 512, 8
POOL_WINDOWS = (2, 4, 8, 16)
POOL_HALO = 16
IN_WIDTH = 3592
SHARD_W = IN_WIDTH // N_CHIPS
A_W, B_W, C_W, F_W = 1024, 512, 2048, 128
E_INT = A_W + B_W + C_W + F_W
B_BLK = A_W // 512
C_BLK0 = (A_W + B_W) // 512
F_BLK = (A_W + B_W + C_W) // 128


def _segments():
    segs = []
    for hp in range(2):
        for part in range(4):
            segs.append((part * 256 + hp * 128, 128))
    segs.append((1024, 256))
    segs.append((1280, 256))
    for hp in range(4):
        for part in range(4):
            segs.append((1536 + part * 512 + hp * 128, 128))
    segs.append((3584, 8))
    return segs


_SEGS = _segments()


def _to_internal(w):
    parts = [w[..., s:s + n] for s, n in _SEGS]
    parts.append(jnp.zeros(w.shape[:-1] + (E_INT - IN_WIDTH,), w.dtype))
    return jnp.concatenate(parts, axis=-1)


def _to_original(w):
    offs, o = [], 0
    for s, n in _SEGS:
        offs.append((s, o, n))
        o += n
    parts = [w[..., o:o + n] for s, o, n in sorted(offs)]
    return jnp.concatenate(parts, axis=-1)


def _internal_from_shards(shards):
    parts = []
    for s, n in _SEGS:
        while n > 0:
            q, r = divmod(s, SHARD_W)
            take = min(n, SHARD_W - r)
            parts.append(shards[q][..., r:r + take])
            s, n = s + take, n - take
    parts.append(jnp.zeros(shards[0].shape[:-1] + (E_INT - IN_WIDTH,), shards[0].dtype))
    return jnp.concatenate(parts, axis=-1)


def _shards_from_internal(w):
    offs, o = [], 0
    for s, n in _SEGS:
        offs.append((s, o, n))
        o += n
    blocks = []
    for q in range(N_CHIPS):
        lo, hi = SHARD_W * q, SHARD_W * (q + 1)
        parts = [w[..., o + max(lo, s) - s:o + min(hi, s + n) - s] for s, o, n in sorted(offs) if s < hi and s + n > lo]
        blocks.append(jnp.concatenate(parts, axis=-1))
    return jnp.stack(blocks)


def _cparams(sem=None, vmem_mb=48):
    kw = dict(vmem_limit_bytes=vmem_mb * 1024 * 1024)
    if sem is not None:
        kw["dimension_semantics"] = sem
    return pltpu.CompilerParams(**kw)


def _sig(x):
    return 1.0 / (1.0 + jnp.exp(-x))


def _silu(x):
    return x * _sig(x)


def _dsilu(x):
    s = _sig(x)
    return s * (1.0 + x * (1.0 - s))


def _rstd(x):
    return lax.rsqrt(jnp.mean(x * x, axis=-1, keepdims=True) + NORM_EPS)


def _dot(a, b, dims, **kw):
    return lax.dot_general(a, b, (dims, ((), ())), preferred_element_type=F32, **kw)


NN = ((1,), (0,))
NT = ((1,), (1,))
TN = ((0,), (0,))


def _iota(shape, dim):
    return lax.broadcasted_iota(jnp.int32, shape, dim)


def _lbs_fwd(lower_bounds):
    def body(a_ref, o_ref):
        a = a_ref[...]
        a0, a1 = a[0:1], a[1:2]
        m = jnp.maximum(a0, a1)
        e0, e1 = jnp.exp(a0 - m), jnp.exp(a1 - m)
        p0, p1 = e0 / (e0 + e1), e1 / (e0 + e1)
        o_ref[...] = jnp.concatenate([p0 - p0, (p0 + p1) - p0], axis=0)

    return pl.pallas_call(body, out_shape=jax.ShapeDtypeStruct(lower_bounds.shape, F32), name="lbs_fwd")(lower_bounds)


def _inproj_fwd(x2, g_row, w_int, name):
    n, d = x2.shape
    e = w_int.shape[1]
    tm = min(512, n)

    def body(x_ref, g_ref, w_ref, o_ref):
        x = x_ref[...]
        h = (x * _rstd(x) * g_ref[...]).astype(BF16)
        o_ref[...] = jnp.dot(h, w_ref[...], preferred_element_type=F32)

    return pl.pallas_call(
        body, grid=(n // tm,),
        in_specs=[pl.BlockSpec((tm, d), lambda i: (i, 0)), pl.BlockSpec((1, d), lambda i: (0, 0)),
                  pl.BlockSpec((d, e), lambda i: (0, 0))],
        out_specs=pl.BlockSpec((tm, e), lambda i: (i, 0)),
        out_shape=jax.ShapeDtypeStruct((n, e), F32),
        compiler_params=_cparams(("parallel",)), name=name)(x2, g_row, w_int)


def _chunk_cumsum_matrix():
    i, j = _iota((LANES, LANES), 0), _iota((LANES, LANES), 1)
    return ((i <= j) & ((i // CHUNK) == (j // CHUNK))).astype(F32)


def _hgrn_gates(a, lb):
    qa, z = a[:, 0:128], a[:, 128:256]
    sg, sgn = _sig(z), _sig(-z)
    fg = lb + (1.0 - lb) * sg
    lf = jnp.log(jnp.maximum(fg, TINY))
    kk = (1.0 - lb) * sgn
    return qa * _sig(qa), kk, lf, sg, sgn, fg


def _hgrn_fwd(proj3, lbs_row, gn_col, name):
    bsz, t, _ = proj3.shape
    nt = t // LANES

    def body(a_ref, lb_ref, gn_ref, og_ref, or_ref):
        lb = lb_ref[...]
        gn = gn_ref[...]
        umat = _chunk_cumsum_matrix()
        lane64 = _iota((1, LANES), 1) % CHUNK

        def tile(i, carry):
            r0 = pl.multiple_of(i * LANES, LANES)
            a = a_ref[pl.ds(r0, LANES), :]
            qq, kk, lf, _, _, _ = _hgrn_gates(a, lb)
            va, ga = a[:, 256:384], a[:, 384:512]
            q_t, k_t, v_t = qq.T, kk.T, va.T
            b_t = jnp.dot(lf.T, umat, precision=HI, preferred_element_type=F32)
            new_s, o_heads = [], []
            for h in range(2):
                s_h = carry[h]
                rs = slice(CHUNK * h, CHUNK * (h + 1))
                qh, kh, vh, bh = q_t[rs], k_t[rs], v_t[rs], b_t[rs]
                inter = []
                for c in range(2):
                    cs = slice(CHUNK * c, CHUNK * (c + 1))
                    b_ = bh[:, cs]
                    qt = (qh[:, cs] * jnp.exp(b_)).astype(BF16)
                    inter.append(_dot(s_h.astype(BF16), qt, TN))
                    bl = b_[:, CHUNK - 1:CHUNK]
                    kt = (kh[:, cs] * jnp.exp(bl - b_)).astype(BF16)
                    s_h = jnp.exp(bl) * s_h + _dot(kt, vh[:, cs].astype(BF16), NT)
                new_s.append(s_h)

                acc = jnp.concatenate(inter, axis=1) + jnp.sum(qh * kh, axis=0, keepdims=True) * vh
                for dlt in range(1, CHUNK):
                    kr, br, vr = pltpu.roll(kh, dlt, 1), pltpu.roll(bh, dlt, 1), pltpu.roll(vh, dlt, 1)
                    e = jnp.exp(jnp.minimum(bh - br, 0.0))
                    att = jnp.sum(qh * kr * e, axis=0, keepdims=True)
                    acc = acc + jnp.where(lane64 >= dlt, att, 0.0) * vr
                o_heads.append(acc)
            normed = []
            for h in range(2):
                o_h = o_heads[h]
                ms = jnp.mean(o_h * o_h, axis=0, keepdims=True)
                normed.append(o_h * lax.rsqrt(ms + NORM_EPS) * gn[CHUNK * h:CHUNK * (h + 1)])
            or_ref[pl.ds(r0, LANES), :] = jnp.concatenate(o_heads, axis=0).T
            og_ref[pl.ds(r0, LANES), :] = jnp.concatenate(normed, axis=0).T * _silu(ga)
            return tuple(new_s)

        zero = jnp.zeros((CHUNK, CHUNK), F32)
        lax.fori_loop(0, nt, tile, (zero, zero))

    out = jax.ShapeDtypeStruct((bsz, t, HGRN_W), F32)
    return pl.pallas_call(
        body, grid=(bsz, 2),
        in_specs=[pl.BlockSpec((None, t, 512), lambda b, p: (b, 0, p)),
                  pl.BlockSpec((1, 128), lambda b, p: (0, p)),
                  pl.BlockSpec((128, 1), lambda b, p: (p, 0))],
        out_specs=[pl.BlockSpec((None, t, 128), lambda b, p: (b, 0, p)),
                   pl.BlockSpec((None, t, 128), lambda b, p: (b, 0, p))],
        out_shape=[out, out],
        compiler_params=_cparams(("parallel", "parallel")), name=name)(proj3, lbs_row, gn_col)


def _hgrn_bwd(proj3, o_raw, dmixed, lbs_row, gn_row, name):
    bsz, t, _ = proj3.shape
    nt = t // LANES
    nchunk = t // CHUNK

    def body(a_ref, or_ref, do_ref, lb_ref, gn_ref, da_ref, dgn_ref, dlb_ref, s_sc):
        lb = lb_ref[...]
        gn = gn_ref[...]
        umat = _chunk_cumsum_matrix()
        lane = _iota((1, LANES), 1)
        lane64 = lane % CHUNK
        half = lane < CHUNK

        def t_layout(a):
            qq, kk, lf, sg, sgn, fg = _hgrn_gates(a, lb)
            b_t = jnp.dot(lf.T, umat, precision=HI, preferred_element_type=F32)
            return qq.T, kk.T, a[:, 256:384].T, b_t, (sg, sgn, fg)

        def fwd_tile(i, carry):
            r0 = pl.multiple_of(i * LANES, LANES)
            q_t, k_t, v_t, b_t, _ = t_layout(a_ref[pl.ds(r0, LANES), :])
            new_s = []
            for h in range(2):
                s_h = carry[h]
                rs = slice(CHUNK * h, CHUNK * (h + 1))
                for c in range(2):
                    cs = slice(CHUNK * c, CHUNK * (c + 1))
                    s_sc[h, 2 * i + c] = s_h
                    b_ = b_t[rs, cs]
                    bl = b_[:, CHUNK - 1:CHUNK]
                    kt = (k_t[rs, cs] * jnp.exp(bl - b_)).astype(BF16)
                    s_h = jnp.exp(bl) * s_h + _dot(kt, v_t[rs, cs].astype(BF16), NT)
                new_s.append(s_h)
            return tuple(new_s)

        zero = jnp.zeros((CHUNK, CHUNK), F32)
        lax.fori_loop(0, nt, fwd_tile, (zero, zero))

        def half_mean(v):
            m0 = jnp.sum(jnp.where(half, v, 0.0), axis=1, keepdims=True) * (1.0 / CHUNK)
            m1 = jnp.sum(jnp.where(half, 0.0, v), axis=1, keepdims=True) * (1.0 / CHUNK)
            return jnp.where(half, m0, m1)

        def bwd_tile(k, carry):
            ds0, ds1, dgn_acc, dlb_acc = carry
            i = nt - 1 - k
            r0 = pl.multiple_of(i * LANES, LANES)
            a = a_ref[pl.ds(r0, LANES), :]
            qa, z, ga = a[:, 0:128], a[:, 128:256], a[:, 384:512]
            q_t, k_t, v_t, b_t, (sg, sgn, fg) = t_layout(a)
            oraw = or_ref[pl.ds(r0, LANES), :]
            dout = do_ref[pl.ds(r0, LANES), :]
            r = lax.rsqrt(half_mean(oraw * oraw) + NORM_EPS)
            xn = oraw * r
            dga = dout * (xn * gn) * _dsilu(ga)
            don = dout * _silu(ga)
            dgn_acc = dgn_acc + jnp.sum(don * xn, axis=0, keepdims=True)
            dxn = don * gn
            do_t = (r * (dxn - xn * half_mean(dxn * xn))).T
            new_ds, dq_h, dk_h, dv_h, db_h = [], [], [], [], []
            for h in range(2):
                ds_h = (ds0, ds1)[h]
                rs = slice(CHUNK * h, CHUNK * (h + 1))
                qh, kh, vh, bh, doh = q_t[rs], k_t[rs], v_t[rs], b_t[rs], do_t[rs]
                dq_c, dk_c, dv_c, dbl_c = [None, None], [None, None], [None, None], [None, None]
                for c in (1, 0):
                    cs = slice(CHUNK * c, CHUNK * (c + 1))
                    s_n = s_sc[h, 2 * i + c]
                    b_ = bh[:, cs]
                    eb = jnp.exp(b_)
                    bl = b_[:, CHUNK - 1:CHUNK]
                    ek = jnp.exp(bl - b_)
                    ebl = jnp.exp(bl)
                    qt, kt = qh[:, cs] * eb, kh[:, cs] * ek
                    do_c = doh[:, cs].astype(BF16)
                    dsb = ds_h.astype(BF16)
                    dv_c[c] = _dot(dsb, kt.astype(BF16), TN)
                    dkt = _dot(dsb, vh[:, cs].astype(BF16), NN)
                    dqt = _dot(s_n.astype(BF16), do_c, NN)
                    dbl_c[c] = jnp.sum(ds_h * s_n, axis=1, keepdims=True) * ebl + jnp.sum(dkt * kt, axis=1, keepdims=True)
                    dq_c[c], dk_c[c] = dqt * eb, dkt * ek
                    ds_h = ebl * ds_h + _dot(qt.astype(BF16), do_c, NT)
                new_ds.append(ds_h)

                att0 = jnp.sum(qh * kh, axis=0, keepdims=True)
                datt0 = jnp.sum(doh * vh, axis=0, keepdims=True)
                dqh = jnp.concatenate(dq_c, axis=1) + datt0 * kh
                dkh = jnp.concatenate(dk_c, axis=1) + datt0 * qh
                dvh = jnp.concatenate(dv_c, axis=1) + att0 * doh
                for dlt in range(1, CHUNK):
                    kr, br, vr = pltpu.roll(kh, dlt, 1), pltpu.roll(bh, dlt, 1), pltpu.roll(vh, dlt, 1)
                    e = jnp.where(lane64 >= dlt, jnp.exp(jnp.minimum(bh - br, 0.0)), 0.0)
                    qe = qh * e
                    att = jnp.sum(qe * kr, axis=0, keepdims=True)
                    datt = jnp.sum(doh * vr, axis=0, keepdims=True)
                    dqh = dqh + datt * (kr * e)
                    dkh = dkh + pltpu.roll(datt * qe, LANES - dlt, 1)
                    dvh = dvh + pltpu.roll(att * doh, LANES - dlt, 1)
                dbl = jnp.where(half, dbl_c[0], dbl_c[1])
                db_h.append(qh * dqh - kh * dkh + jnp.where(lane64 == CHUNK - 1, dbl, 0.0))
                dq_h.append(dqh)
                dk_h.append(dkh)
                dv_h.append(dvh)
            dqq = jnp.concatenate(dq_h, axis=0).T
            dkk = jnp.concatenate(dk_h, axis=0).T
            dvv = jnp.concatenate(dv_h, axis=0).T
            dlf = _dot(jnp.concatenate(db_h, axis=0), umat, NT, precision=HI).T
            dqa = dqq * _dsilu(qa)
            dfg = jnp.where(fg > TINY, dlf / fg, 0.0)
            dz = (dfg - dkk) * (1.0 - lb) * sg * sgn
            dlb_acc = dlb_acc + jnp.sum(dfg * (1.0 - sg) - dkk * sgn, axis=0, keepdims=True)
            da_ref[pl.ds(r0, LANES), :] = jnp.concatenate([dqa, dz, dvv, dga], axis=1)
            return new_ds[0], new_ds[1], dgn_acc, dlb_acc

        zrow = jnp.zeros((1, LANES), F32)
        _, _, dgn_acc, dlb_acc = lax.fori_loop(0, nt, bwd_tile, (zero, zero, zrow, zrow))
        dgn_ref[...] = jnp.broadcast_to(dgn_acc, (8, LANES))
        dlb_ref[...] = jnp.broadcast_to(dlb_acc, (8, LANES))

    rows = jax.ShapeDtypeStruct((bsz, 8, HGRN_W), F32)
    return pl.pallas_call(
        body, grid=(bsz, 2),
        in_specs=[pl.BlockSpec((None, t, 512), lambda b, p: (b, 0, p)),
                  pl.BlockSpec((None, t, 128), lambda b, p: (b, 0, p)),
                  pl.BlockSpec((None, t, 128), lambda b, p: (b, 0, p)),
                  pl.BlockSpec((1, 128), lambda b, p: (0, p)),
                  pl.BlockSpec((1, 128), lambda b, p: (0, p))],
        out_specs=[pl.BlockSpec((None, t, 512), lambda b, p: (b, 0, p)),
                   pl.BlockSpec((None, 8, 128), lambda b, p: (b, 0, p)),
                   pl.BlockSpec((None, 8, 128), lambda b, p: (b, 0, p))],
        out_shape=[jax.ShapeDtypeStruct((bsz, t, A_W), F32), rows, rows],
        scratch_shapes=[pltpu.VMEM((2, nchunk, CHUNK, CHUNK), F32)],
        compiler_params=_cparams(("parallel", "parallel")), name=name)(proj3, o_raw, dmixed, lbs_row, gn_row)


N_LEVELS = 6


def _hgrn_tables():
    t = np.arange(LANES)
    j = np.arange(LANES)[None, :]
    same_chunk = (t[:, None] // CHUNK) == (j // CHUNK)
    w = np.zeros((2 + N_LEVELS, LANES, LANES), np.float32)
    w[0] = same_chunk & (j <= t[:, None])
    w[1] = same_chunk & (j > t[:, None])
    maskf = np.zeros((N_LEVELS, LANES, LANES), np.float32)
    rightf = np.zeros((N_LEVELS, LANES, LANES), np.float32)
    for li in range(N_LEVELS):
        m = (CHUNK // 2) >> li
        start = t - (t % (2 * m))
        right = (t % (2 * m)) >= m
        first = np.where(right, start + m, t + 1)
        last = np.where(right, t, start + m - 1)
        w[2 + li] = (j >= first[:, None]) & (j <= last[:, None])
        maskf[li] = (t[:, None] // (2 * m)) == (j // (2 * m))
        rightf[li] = right[:, None]
    return jnp.asarray(w.reshape(-1, LANES), BF16), jnp.asarray(maskf), jnp.asarray(rightf)


def _split(x, n):
    parts = []
    for _ in range(n - 1):
        p = x.astype(BF16)
        parts.append(p)
        x = x - p.astype(F32)
    parts.append(x.astype(BF16))
    return parts


def _exact_dot(w, parts):
    acc = jnp.dot(w, parts[0], preferred_element_type=F32)
    for p in parts[1:]:
        acc = acc + jnp.dot(w, p, preferred_element_type=F32)
    return acc


def _head_sums(v, ones_blk, n=2):
    parts = _split(v, n)
    acc = jnp.dot(parts[0], ones_blk, preferred_element_type=F32)
    for p in parts[1:]:
        acc = acc + jnp.dot(p, ones_blk, preferred_element_type=F32)
    return acc


def _hgrn_consts():
    r, c = _iota((LANES, LANES), 0), _iota((LANES, LANES), 1)
    eye = r == c
    ones_blk = ((r // CHUNK) == (c // CHUNK)).astype(BF16)
    return eye, ones_blk, jnp.ones((CHUNK, LANES), BF16)


def _hgrn_levels(qq, kk, zall, mk_ref, rt_ref, d_att=None):
    att = [jnp.zeros((LANES, LANES), F32)] * 2
    dq = dk = db = jnp.zeros((LANES, LANES), F32)
    for li in range(N_LEVELS):
        e = jnp.exp(zall[(2 + li) * LANES:(3 + li) * LANES])
        rt = rt_ref[li]
        mk = mk_ref[li]
        qef, kef = e * rt, e * (1.0 - rt)
        qe, ke = (qq * qef).astype(BF16), (kk * kef).astype(BF16)
        dqs, dks = [], []
        for h in range(2):
            hs = slice(CHUNK * h, CHUNK * (h + 1))
            att[h] = att[h] + _dot(qe[:, hs], ke[:, hs], NT) * mk
            if d_att is not None:
                dam = (d_att[h] * mk).astype(BF16)
                dqs.append(jnp.dot(dam, ke[:, hs], preferred_element_type=F32))
                dks.append(_dot(dam, qe[:, hs], TN))
        if d_att is not None:
            dqe, dke = jnp.concatenate(dqs, axis=1), jnp.concatenate(dks, axis=1)
            dq = dq + dqe * qef
            dk = dk + dke * kef
            db = db + (dqe * qe.astype(F32) - dke * ke.astype(F32))
    return att, dq, dk, db


def _hgrn_fwd(proj3, lbs_row, gn_row, name):
    bsz, t, _ = proj3.shape
    nt = t // LANES
    w_all, maskf, rightf = _hgrn_tables()

    def body(a_ref, lb_ref, gn_ref, w_ref, mk_ref, rt_ref, og_ref, or_ref, st_ref):
        lb = lb_ref[...]
        gn = gn_ref[...]
        eye, ones_blk, ones_h = _hgrn_consts()

        def tile(i, carry):
            r0 = pl.multiple_of(i * LANES, LANES)
            a = a_ref[pl.ds(r0, LANES), :]
            qq, kk, lf, _, _, _ = _hgrn_gates(a, lb)
            va, ga = a[:, 256:384], a[:, 384:512]
            parts = _split(lf, 3)
            zall = _exact_dot(w_ref[...], parts)
            eb, ee = jnp.exp(zall[0:LANES]), jnp.exp(zall[LANES:2 * LANES])
            vb = va.astype(BF16)
            att, _, _, _ = _hgrn_levels(qq, kk, zall, mk_ref, rt_ref)
            qk = _split(qq * kk, 2)
            qeb, keb = (qq * eb).astype(BF16), (kk * ee).astype(BF16)
            new_s, o_heads = [], []
            for h in range(2):
                hs = slice(CHUNK * h, CHUNK * (h + 1))
                diag = _exact_dot_r(qk, hs, ones_h)
                a_h = att[h] + jnp.where(eye, diag, 0.0)
                o_h = jnp.dot(a_h.astype(BF16), vb[:, hs], preferred_element_type=F32)
                st = carry[h]
                chunks = []
                for c in range(2):
                    rc = slice(CHUNK * c, CHUNK * (c + 1))
                    st_ref[h, 2 * i + c] = st
                    chunks.append(o_h[rc] + _dot(qeb[rc, hs], st.astype(BF16), NT))
                    ebl = eb[CHUNK * (c + 1) - 1:CHUNK * (c + 1), hs]
                    st = st * ebl + _dot(vb[rc, hs], keb[rc, hs], TN)
                new_s.append(st)
                o_heads.append(jnp.concatenate(chunks, axis=0))
            o = jnp.concatenate(o_heads, axis=1)
            ms = _head_sums(o * o, ones_blk) * (1.0 / CHUNK)
            or_ref[pl.ds(r0, LANES), :] = o
            og_ref[pl.ds(r0, LANES), :] = o * lax.rsqrt(ms + NORM_EPS) * gn * _silu(ga)
            return tuple(new_s)

        zero = jnp.zeros((CHUNK, CHUNK), F32)
        lax.fori_loop(0, nt // 2, lambda i, carry: tile(2 * i + 1, tile(2 * i, carry)), (zero, zero))

    out = jax.ShapeDtypeStruct((bsz, t, HGRN_W), F32)
    row = pl.BlockSpec((1, 128), lambda b, p: (0, p))
    return pl.pallas_call(
        body, grid=(bsz, 2),
        in_specs=[pl.BlockSpec((None, t, 512), lambda b, p: (b, 0, p)), row, row,
                  pl.BlockSpec(w_all.shape, lambda b, p: (0, 0)),
                  pl.BlockSpec(maskf.shape, lambda b, p: (0, 0, 0)),
                  pl.BlockSpec(rightf.shape, lambda b, p: (0, 0, 0))],
        out_specs=[pl.BlockSpec((None, t, 128), lambda b, p: (b, 0, p)),
                   pl.BlockSpec((None, t, 128), lambda b, p: (b, 0, p)),
                   pl.BlockSpec((None, 2, t // CHUNK, CHUNK, CHUNK), lambda b, p: (b, p, 0, 0, 0))],
        out_shape=[out, out, jax.ShapeDtypeStruct((bsz, 4, t // CHUNK, CHUNK, CHUNK), F32)],
        compiler_params=_cparams(("parallel", "parallel")), name=name)(proj3, lbs_row, gn_row, w_all, maskf, rightf)


def _exact_dot_r(parts, hs, ones_h):
    acc = jnp.dot(parts[0][:, hs], ones_h, preferred_element_type=F32)
    for p in parts[1:]:
        acc = acc + jnp.dot(p[:, hs], ones_h, preferred_element_type=F32)
    return acc


def _hgrn_bwd(proj3, o_raw, dmixed, states, lbs_row, gn_row, name):
    bsz, t, _ = proj3.shape
    nt = t // LANES
    nchunk = t // CHUNK
    w_all, maskf, rightf = _hgrn_tables()

    def body(a_ref, or_ref, do_ref, s_sc, lb_ref, gn_ref, w_ref, mk_ref, rt_ref, da_ref, dgn_ref, dlb_ref):
        lb = lb_ref[...]
        gn = gn_ref[...]
        eye, ones_blk, ones_h = _hgrn_consts()
        r_i, c_i = _iota((LANES, LANES), 0), _iota((LANES, LANES), 1)
        suffix = ((c_i >= r_i) & ((r_i // CHUNK) == (c_i // CHUNK))).astype(BF16)
        row64 = _iota((LANES, CHUNK), 0)
        ones_t = jnp.ones((LANES, CHUNK), BF16)
        zero = jnp.zeros((CHUNK, CHUNK), F32)

        def bwd_tile(k, carry):
            dst0, dst1, dgn_acc, dlb_acc = carry
            i = nt - 1 - k
            r0 = pl.multiple_of(i * LANES, LANES)
            a = a_ref[pl.ds(r0, LANES), :]
            qa, ga = a[:, 0:128], a[:, 384:512]
            qq, kk, lf, sg, sgn, fg = _hgrn_gates(a, lb)
            parts = _split(lf, 3)
            zall = _exact_dot(w_ref[...], parts)
            eb, ee = jnp.exp(zall[0:LANES]), jnp.exp(zall[LANES:2 * LANES])
            vb = a[:, 256:384].astype(BF16)
            oraw = or_ref[pl.ds(r0, LANES), :]
            dout = do_ref[pl.ds(r0, LANES), :]
            r = lax.rsqrt(_head_sums(oraw * oraw, ones_blk) * (1.0 / CHUNK) + NORM_EPS)
            xn = oraw * r
            dga = dout * (xn * gn) * _dsilu(ga)
            don = dout * _silu(ga)
            dgn_acc = dgn_acc + jnp.sum(don * xn, axis=0, keepdims=True)
            dxn = don * gn
            do = r * (dxn - xn * (_head_sums(dxn * xn, ones_blk) * (1.0 / CHUNK)))
            dob = do.astype(BF16)
            d_att = [_dot(dob[:, CHUNK * h:CHUNK * (h + 1)], vb[:, CHUNK * h:CHUNK * (h + 1)], NT) for h in range(2)]
            att, dq, dk, db_lv = _hgrn_levels(qq, kk, zall, mk_ref, rt_ref, d_att)
            qk = _split(qq * kk, 2)
            qe_f, ke_f = qq * eb, kk * ee
            qeb, keb = qe_f.astype(BF16), ke_f.astype(BF16)
            new_ds, dq_h, dk_h, dv_h, dbl_h = [], [], [], [], []
            for h in range(2):
                hs = slice(CHUNK * h, CHUNK * (h + 1))
                a_h = att[h] + jnp.where(eye, _exact_dot_r(qk, hs, ones_h), 0.0)
                dv = _dot(a_h.astype(BF16), dob[:, hs], TN)
                ddiag = _exact_dot_r(_split(jnp.where(eye, d_att[h], 0.0), 2), slice(None), ones_t)
                dq_i = dq[:, hs] + ddiag * kk[:, hs]
                dk_i = dk[:, hs] + ddiag * qq[:, hs]
                dst = (dst0, dst1)[h]
                dq_c, dk_c, dv_c, dbl_c = [None, None], [None, None], [None, None], [None, None]
                for c in (1, 0):
                    rc = slice(CHUNK * c, CHUNK * (c + 1))
                    st_n = s_sc[h, 2 * i + c]
                    ebl = eb[CHUNK * (c + 1) - 1:CHUNK * (c + 1), hs]
                    dstb = dst.astype(BF16)
                    dv_c[c] = _dot(keb[rc, hs], dstb, NT)
                    dke = jnp.dot(vb[rc, hs], dstb, preferred_element_type=F32)
                    dqe = jnp.dot(dob[rc, hs], st_n.astype(BF16), preferred_element_type=F32)
                    dbl_c[c] = (jnp.sum(dst * st_n, axis=0, keepdims=True) * ebl
                                + jnp.sum(dke * ke_f[rc, hs], axis=0, keepdims=True))
                    dq_c[c], dk_c[c] = dqe * eb[rc, hs], dke * ee[rc, hs]
                    dst = dst * ebl + _dot(dob[rc, hs], qeb[rc, hs], TN)
                new_ds.append(dst)
                dq_x, dk_x = jnp.concatenate(dq_c, axis=0), jnp.concatenate(dk_c, axis=0)
                dq_h.append(dq_i + dq_x)
                dk_h.append(dk_i + dk_x)
                dv_h.append(dv + jnp.concatenate(dv_c, axis=0))
                dbl_h.append(qq[:, hs] * dq_x - kk[:, hs] * dk_x
                             + jnp.where(row64 == CHUNK - 1, dbl_c[0], 0.0) + jnp.where(row64 == LANES - 1, dbl_c[1], 0.0))
            dqq = jnp.concatenate(dq_h, axis=1)
            dkk = jnp.concatenate(dk_h, axis=1)
            dvv = jnp.concatenate(dv_h, axis=1)
            db = db_lv + jnp.concatenate(dbl_h, axis=1)
            dlf = _exact_dot(suffix, _split(db, 3))
            dqa = dqq * _dsilu(qa)
            dfg = jnp.where(fg > TINY, dlf / fg, 0.0)
            dz = (dfg - dkk) * (1.0 - lb) * sg * sgn
            dlb_acc = dlb_acc + jnp.sum(dfg * (1.0 - sg) - dkk * sgn, axis=0, keepdims=True)
            da_ref[pl.ds(r0, LANES), :] = jnp.concatenate([dqa, dz, dvv, dga], axis=1)
            return new_ds[0], new_ds[1], dgn_acc, dlb_acc

        zrow = jnp.zeros((1, LANES), F32)
        _, _, dgn_acc, dlb_acc = lax.fori_loop(
            0, nt // 2, lambda k, carry: bwd_tile(2 * k + 1, bwd_tile(2 * k, carry)), (zero, zero, zrow, zrow))
        dgn_ref[...] = jnp.broadcast_to(dgn_acc, (8, LANES))
        dlb_ref[...] = jnp.broadcast_to(dlb_acc, (8, LANES))

    rows = jax.ShapeDtypeStruct((bsz, 8, HGRN_W), F32)
    row = pl.BlockSpec((1, 128), lambda b, p: (0, p))
    blk = pl.BlockSpec((None, t, 128), lambda b, p: (b, 0, p))
    return pl.pallas_call(
        body, grid=(bsz, 2),
        in_specs=[pl.BlockSpec((None, t, 512), lambda b, p: (b, 0, p)), blk, blk,
                  pl.BlockSpec((None, 2, nchunk, CHUNK, CHUNK), lambda b, p: (b, p, 0, 0, 0)), row, row,
                  pl.BlockSpec(w_all.shape, lambda b, p: (0, 0)),
                  pl.BlockSpec(maskf.shape, lambda b, p: (0, 0, 0)),
                  pl.BlockSpec(rightf.shape, lambda b, p: (0, 0, 0))],
        out_specs=[pl.BlockSpec((None, t, 512), lambda b, p: (b, 0, p)),
                   pl.BlockSpec((None, 8, 128), lambda b, p: (b, 0, p)),
                   pl.BlockSpec((None, 8, 128), lambda b, p: (b, 0, p))],
        out_shape=[jax.ShapeDtypeStruct((bsz, t, A_W), F32), rows, rows],
        compiler_params=_cparams(("parallel", "parallel")), name=name)(
            proj3, o_raw, dmixed, states, lbs_row, gn_row, w_all, maskf, rightf)


def _pool_tt(t):
    return min(256, t)


def _window_select(s2, s4, s8, s16, lane):
    return jnp.where(lane < 64, s2, jnp.where(lane < 128, s4, jnp.where(lane < 192, s8, s16)))


def _pool_counts(t0, tt):
    lane = _iota((tt, POOL_W), 1)
    tpos = (_iota((tt, POOL_W), 0) + t0 + 1).astype(F32)
    win = jnp.where(lane < 64, 2.0, jnp.where(lane < 128, 4.0, jnp.where(lane < 192, 8.0, 16.0)))
    return 1.0 / jnp.minimum(tpos, win), lane


def _pooled_tile(upad_ref, i, tt):
    r0 = pl.multiple_of(i * tt, 8)
    cat = upad_ref[pl.ds(r0, tt + POOL_HALO), :]
    s2 = cat + pltpu.roll(cat, 1, 0)
    s4 = s2 + pltpu.roll(s2, 2, 0)
    s8 = s4 + pltpu.roll(s4, 4, 0)
    s16 = s8 + pltpu.roll(s8, 8, 0)
    inv, lane = _pool_counts(i * tt, tt)
    sel = _window_select(s2[POOL_HALO:], s4[POOL_HALO:], s8[POOL_HALO:], s16[POOL_HALO:], lane)
    return sel * inv - cat[POOL_HALO:], inv, lane


def _pool_fwd(proj3, wbd, scale_row, name):
    bsz, t, _ = proj3.shape
    tt = _pool_tt(t)

    def body(p_ref, w_ref, sc_ref, o_ref, upad):
        upad[0:POOL_HALO, :] = jnp.zeros((POOL_HALO, POOL_W), F32)
        upad[POOL_HALO:, :] = p_ref[:, 0:POOL_W]
        w = w_ref[...]
        sc = sc_ref[...]

        def tile(i, c):
            pooled, _, _ = _pooled_tile(upad, i, tt)
            r0 = pl.multiple_of(i * tt, 8)
            g = p_ref[pl.ds(r0, tt), POOL_W:2 * POOL_W]
            pre = jnp.dot(pooled.astype(BF16), w, preferred_element_type=F32)
            o_ref[pl.ds(r0, tt), :] = pre * sc * _silu(g)
            return c

        lax.fori_loop(0, t // tt, tile, 0)

    return pl.pallas_call(
        body, grid=(bsz,),
        in_specs=[pl.BlockSpec((None, t, 512), lambda b: (b, 0, B_BLK)),
                  pl.BlockSpec((POOL_W, POOL_W), lambda b: (0, 0)),
                  pl.BlockSpec((1, POOL_W), lambda b: (0, 0))],
        out_specs=pl.BlockSpec((None, t, POOL_W), lambda b: (b, 0, 0)),
        out_shape=jax.ShapeDtypeStruct((bsz, t, POOL_W), F32),
        scratch_shapes=[pltpu.VMEM((t + POOL_HALO, POOL_W), F32)],
        compiler_params=_cparams(("parallel",)), name=name)(proj3, wbd, scale_row)


def _pool_bwd(proj3, dmixed, wbd, scale_row, name):
    bsz, t, _ = proj3.shape
    tt = _pool_tt(t)

    def body(p_ref, do_ref, w_ref, sc_ref, db_ref, dsc_ref, dw_ref, upad, epad):
        upad[0:POOL_HALO, :] = jnp.zeros((POOL_HALO, POOL_W), F32)
        upad[POOL_HALO:, :] = p_ref[:, 0:POOL_W]
        epad[t:, :] = jnp.zeros((POOL_HALO, POOL_W), F32)
        w = w_ref[...]
        sc = sc_ref[...]

        def tile(i, carry):
            dsc_acc, dw_acc = carry
            pooled, inv, _ = _pooled_tile(upad, i, tt)
            r0 = pl.multiple_of(i * tt, 8)
            g = p_ref[pl.ds(r0, tt), POOL_W:2 * POOL_W]
            dout = do_ref[pl.ds(r0, tt), :]
            pb = pooled.astype(BF16)
            pre = jnp.dot(pb, w, preferred_element_type=F32)
            t1 = dout * _silu(g)
            dsc_acc = dsc_acc + jnp.sum(t1 * pre, axis=0, keepdims=True)
            dpre = (t1 * sc).astype(BF16)
            db_ref[pl.ds(r0, tt), POOL_W:2 * POOL_W] = dout * pre * sc * _dsilu(g)
            dw_acc = dw_acc + _dot(pb, dpre, TN)
            dpooled = _dot(dpre, w, NT)
            epad[pl.ds(r0, tt), :] = dpooled * inv
            return dsc_acc, dw_acc

        dsc_acc, dw_acc = lax.fori_loop(0, t // tt, tile, (jnp.zeros((1, POOL_W), F32), jnp.zeros((POOL_W, POOL_W), F32)))
        dsc_ref[...] = jnp.broadcast_to(dsc_acc, (8, POOL_W))
        dw_ref[...] = dw_acc

        def tile2(i, c):
            r0 = pl.multiple_of(i * tt, 8)
            n = tt + POOL_HALO
            cat = epad[pl.ds(r0, n), :]
            s2 = cat + pltpu.roll(cat, n - 1, 0)
            s4 = s2 + pltpu.roll(s2, n - 2, 0)
            s8 = s4 + pltpu.roll(s4, n - 4, 0)
            s16 = s8 + pltpu.roll(s8, n - 8, 0)
            inv, lane = _pool_counts(i * tt, tt)
            sel = _window_select(s2[:tt], s4[:tt], s8[:tt], s16[:tt], lane)
            db_ref[pl.ds(r0, tt), 0:POOL_W] = sel - cat[:tt] / inv
            return c

        lax.fori_loop(0, t // tt, tile2, 0)

    return pl.pallas_call(
        body, grid=(bsz,),
        in_specs=[pl.BlockSpec((None, t, 512), lambda b: (b, 0, B_BLK)),
                  pl.BlockSpec((None, t, POOL_W), lambda b: (b, 0, 1)),
                  pl.BlockSpec((POOL_W, POOL_W), lambda b: (0, 0)),
                  pl.BlockSpec((1, POOL_W), lambda b: (0, 0))],
        out_specs=[pl.BlockSpec((None, t, 512), lambda b: (b, 0, 0)),
                   pl.BlockSpec((None, 8, POOL_W), lambda b: (b, 0, 0)),
                   pl.BlockSpec((None, POOL_W, POOL_W), lambda b: (b, 0, 0))],
        out_shape=[jax.ShapeDtypeStruct((bsz, t, B_W), F32), jax.ShapeDtypeStruct((bsz, 8, POOL_W), F32),
                   jax.ShapeDtypeStruct((bsz, POOL_W, POOL_W), F32)],
        scratch_shapes=[pltpu.VMEM((t + POOL_HALO, POOL_W), F32), pltpu.VMEM((t + POOL_HALO, POOL_W), F32)],
        compiler_params=_cparams(("parallel",)), name=name)(proj3, dmixed, wbd, scale_row)


def _head_select_rows(hp):
    r, c = _iota((8, LANES), 0), _iota((8, LANES), 1)
    return ((r < 2) & (c == 2 * hp + r)).astype(F32)


def _foxgate_fwd(proj3, bias_row, name):
    bsz, t, _ = proj3.shape
    nt = t // LANES

    def body(f_ref, b_ref, cn_ref, ct_ref):
        bias = b_ref[...]
        i, j = _iota((LANES, LANES), 0), _iota((LANES, LANES), 1)
        lower = (j <= i).astype(BF16)
        spread = (_iota((LANES, FOX_W), 0) == _iota((LANES, FOX_W), 1) // 64).astype(BF16)
        select = [_head_select_rows(hp).astype(BF16) for hp in range(4)]
        offset = jnp.zeros((1, LANES), F32)
        for k in range(nt):
            rows = slice(k * LANES, (k + 1) * LANES)
            xg = f_ref[rows, :] + bias
            lf = jnp.minimum(xg, 0.0) - jnp.log(1.0 + jnp.exp(-jnp.abs(xg)))
            c = _exact_dot(lower, _split(lf, 3)) + offset
            offset = c[LANES - 1:LANES, :]
            parts = _split(c, 3)
            cn_ref[rows, :] = _head_sums(c, spread, 3)
            for hp in range(4):
                acc = _dot(select[hp], parts[0], NT)
                for p in parts[1:]:
                    acc = acc + _dot(select[hp], p, NT)
                ct_ref[hp, :, rows] = acc

    return pl.pallas_call(
        body, grid=(bsz,),
        in_specs=[pl.BlockSpec((None, t, 128), lambda b: (b, 0, F_BLK)), pl.BlockSpec((1, 128), lambda b: (0, 0))],
        out_specs=[pl.BlockSpec((None, t, FOX_W), lambda b: (b, 0, 0)),
                   pl.BlockSpec((None, 4, 8, t), lambda b: (b, 0, 0, 0))],
        out_shape=[jax.ShapeDtypeStruct((bsz, t, FOX_W), F32), jax.ShapeDtypeStruct((bsz, 4, 8, t), F32)],
        compiler_params=_cparams(("parallel",)), name=name)(proj3, bias_row)


def _foxgate_bwd(proj3, dc_nat, bias_row, name):
    bsz, t, _ = proj3.shape
    nt = t // LANES

    def body(f_ref, dc_ref, b_ref, df_ref, dbias_ref, run_sc):
        bias = b_ref[...]
        i, j = _iota((LANES, LANES), 0), _iota((LANES, LANES), 1)
        upper = (j >= i).astype(F32)
        valid = _iota((1, LANES), 1) < FOX_HEADS
        run_sc[...] = jnp.zeros((8, LANES), F32)
        dbias_ref[...] = jnp.zeros((8, LANES), F32)

        def tile(k, c):
            r0 = pl.multiple_of((nt - 1 - k) * LANES, LANES)
            dc = dc_ref[pl.ds(r0, LANES), :] + jnp.where(i == LANES - 1, run_sc[0:1, :], 0.0)
            dlf = jnp.dot(upper, dc, precision=HI, preferred_element_type=F32)
            xg = f_ref[pl.ds(r0, LANES), :] + bias
            df = jnp.where(valid, dlf * _sig(-xg), 0.0)
            df_ref[pl.ds(r0, LANES), :] = df
            run_sc[...] = dlf[0:8, :]
            dbias_ref[...] += jnp.sum(df, axis=0, keepdims=True)
            return c

        lax.fori_loop(0, nt, tile, 0)

    blk = pl.BlockSpec((None, t, 128), lambda b: (b, 0, 0))
    return pl.pallas_call(
        body, grid=(bsz,),
        in_specs=[pl.BlockSpec((None, t, 128), lambda b: (b, 0, F_BLK)), blk, pl.BlockSpec((1, 128), lambda b: (0, 0))],
        out_specs=[blk, pl.BlockSpec((None, 8, 128), lambda b: (b, 0, 0))],
        out_shape=[jax.ShapeDtypeStruct((bsz, t, F_W), F32), jax.ShapeDtypeStruct((bsz, 8, 128), F32)],
        scratch_shapes=[pltpu.VMEM((8, LANES), F32)],
        compiler_params=_cparams(("parallel",)), name=name)(proj3, dc_nat, bias_row)


def _fox_tile(t):
    return min(256, t)


def _fox_fwd(proj3, c_nat, c_t, name):
    bsz, t, _ = proj3.shape
    tq = tk = min(2 * _fox_tile(t), t)
    nq = t // tq

    def body(q_ref, kv_ref, cn_ref, ct_ref, og_ref, or_ref, lse_ref):
        i = pl.program_id(2)
        qblk = q_ref[...]
        first = _iota((1, 128), 1) < 64
        qv = qblk[:, 0:128] * 0.125
        qm = [jnp.where(first, qv, 0.0).astype(BF16), jnp.where(first, 0.0, qv).astype(BF16)]
        cqs = [cn_ref[:, 0:1], cn_ref[:, 64:65]]
        rows = _iota((tq, tk), 0) + i * tq

        def scores(j):
            c0 = pl.multiple_of(j * tk, tk)
            kb = kv_ref[pl.ds(c0, tk), 128:256].astype(BF16)
            return tuple(_dot(qm[h], kb, NT) + (cqs[h] - ct_ref[h:h + 1, pl.ds(c0, tk)]) for h in range(2))

        def absorb(j, state, s01, masked):
            c0 = pl.multiple_of(j * tk, tk)
            vblk = kv_ref[pl.ds(c0, tk), 256:384]
            vx = [jnp.where(first, vblk, 1.0).astype(BF16), jnp.where(first, 1.0, vblk).astype(BF16)]
            new = []
            for h in range(2):
                m, acc, s = state[2 * h], state[2 * h + 1], s01[h]
                if masked:
                    s = jnp.where(rows >= _iota((tq, tk), 1) + j * tk, s, MASK_VALUE)
                m_new = jnp.maximum(m, jnp.max(s, axis=1, keepdims=True))
                p = jnp.exp(s - m_new).astype(BF16)
                new += [m_new, jnp.exp(m - m_new) * acc + jnp.dot(p, vx[h], preferred_element_type=F32)]
            return tuple(new)

        def kv_step(j, carry):
            ahead = scores(j + 1)
            return absorb(j, carry[:4], carry[4:], False) + ahead

        init = (jnp.full((tq, 1), MASK_VALUE, F32), jnp.zeros((tq, 128), F32)) * 2
        n_full = (i * tq) // tk
        carry = lax.fori_loop(0, n_full, kv_step, init + scores(0))
        m0, acc0, m1, acc1 = absorb(n_full, carry[:4], carry[4:], True)
        l0, l1 = pltpu.roll(acc0, 64, 1), pltpu.roll(acc1, 64, 1)
        o = jnp.where(first, acc0 / l0, acc1 / l1)
        or_ref[...] = o
        og_ref[...] = o * _silu(qblk[:, 384:512])
        lse_ref[...] = jnp.where(first, m0 + jnp.log(l0), m1 + jnp.log(l1))

    out = jax.ShapeDtypeStruct((bsz, t, FOX_W), F32)
    blk = pl.BlockSpec((None, tq, 128), lambda b, p, i: (b, i, p))
    return pl.pallas_call(
        body, grid=(bsz, 4, nq),
        in_specs=[pl.BlockSpec((None, tq, 512), lambda b, p, i: (b, i, C_BLK0 + p)),
                  pl.BlockSpec((None, t, 512), lambda b, p, i: (b, 0, C_BLK0 + p)),
                  blk,
                  pl.BlockSpec((None, None, 8, t), lambda b, p, i: (b, p, 0, 0))],
        out_specs=[blk, blk, blk],
        out_shape=[out, out, out],
        compiler_params=_cparams(("parallel", "parallel", "arbitrary")), name=name)(proj3, proj3, c_nat, c_t)


def _fox_bwd(proj3, o_raw, dmixed, lse, c_nat, c_t, name):
    bsz, t, _ = proj3.shape
    tq = tk = min(2 * _fox_tile(t), t)
    nq = t // tq
    ratio = tk // tq

    def body(a_ref, or_ref, do_ref, lse_ref, cn_ref, ct_ref, dc_out, dct_out, drow_out, dq_sc, do_sc, dl_sc):
        def prep(i, c):
            r0 = pl.multiple_of(i * tq, tq)
            g = a_ref[pl.ds(r0, tq), 384:512]
            dout = do_ref[pl.ds(r0, tq), :]
            o = or_ref[pl.ds(r0, tq), :]
            dc_out[pl.ds(r0, tq), 384:512] = dout * o * _dsilu(g)
            do = dout * _silu(g)
            do_sc[pl.ds(r0, tq), :] = do
            prod = do * o
            d0 = jnp.sum(prod[:, 0:64], axis=1, keepdims=True)
            d1 = jnp.sum(prod[:, 64:128], axis=1, keepdims=True)
            dl_sc[pl.ds(r0, tq), :] = jnp.concatenate([jnp.broadcast_to(d0, (tq, 64)), jnp.broadcast_to(d1, (tq, 64))], axis=1)
            dq_sc[pl.ds(r0, tq), :] = jnp.zeros((tq, 128), F32)
            drow_out[pl.ds(r0, tq), :] = jnp.zeros((tq, 128), F32)
            return c

        lax.fori_loop(0, nq, prep, 0)
        dct_out[...] = jnp.zeros((8, t), F32)

        first = _iota((1, 128), 1) < 64

        def heads(v):
            return [jnp.where(first, v, 0.0).astype(BF16), jnp.where(first, 0.0, v).astype(BF16)]

        def kv_tile(j, c):
            c0 = pl.multiple_of(j * tk, tk)
            kb = a_ref[pl.ds(c0, tk), 128:256].astype(BF16)
            vb = a_ref[pl.ds(c0, tk), 256:384].astype(BF16)
            cks = [ct_ref[h:h + 1, pl.ds(c0, tk)] for h in range(2)]

            def q_step(i, carry, diagonal):
                dk, dv, dcol0, dcol1 = carry
                r0 = pl.multiple_of(i * tq, tq)
                causal = _iota((tq, tk), 0) + i * tq >= _iota((tq, tk), 1) + j * tk
                qv = a_ref[pl.ds(r0, tq), 0:128] * 0.125
                do = do_sc[pl.ds(r0, tq), :]
                qb, dob = qv.astype(BF16), do.astype(BF16)
                qm, dom = heads(qv), heads(do)
                full, dcols, rsums = [], [], []
                for h in range(2):
                    lse_h = lse_ref[pl.ds(r0, tq), 64 * h:64 * h + 1]
                    dl_h = dl_sc[pl.ds(r0, tq), 64 * h:64 * h + 1]
                    cq = cn_ref[pl.ds(r0, tq), 64 * h:64 * h + 1]
                    p = jnp.exp(_dot(qm[h], kb, NT) + (cq - cks[h]) - lse_h)
                    if diagonal:
                        p = jnp.where(causal, p, 0.0)
                    ds = p * (_dot(dom[h], vb, NT) - dl_h)
                    dsb = ds.astype(BF16)
                    full.append((_dot(p.astype(BF16), dob, TN), _dot(dsb, qb, TN),
                                 jnp.dot(dsb, kb, preferred_element_type=F32)))
                    dcols.append(jnp.sum(ds, axis=0, keepdims=True))
                    rsums.append(jnp.broadcast_to(jnp.sum(ds, axis=1, keepdims=True), (tq, 128)))
                dq_sc[pl.ds(r0, tq), :] += jnp.where(first, full[0][2], full[1][2]) * 0.125
                drow_out[pl.ds(r0, tq), :] += jnp.where(first, rsums[0], rsums[1])
                return (dk + jnp.where(first, full[0][1], full[1][1]), dv + jnp.where(first, full[0][0], full[1][0]),
                        dcol0 - dcols[0], dcol1 - dcols[1])

            carry = (jnp.zeros((tk, 128), F32), jnp.zeros((tk, 128), F32), jnp.zeros((1, tk), F32), jnp.zeros((1, tk), F32))
            for r in range(ratio):
                carry = q_step(ratio * j + r, carry, True)
            dk, dv, dcol0, dcol1 = lax.fori_loop(ratio * (j + 1), nq, functools.partial(q_step, diagonal=False), carry)
            dct_out[0:1, pl.ds(c0, tk)] = dcol0
            dct_out[1:2, pl.ds(c0, tk)] = dcol1
            dc_out[pl.ds(c0, tk), 128:256] = dk
            dc_out[pl.ds(c0, tk), 256:384] = dv
            return c

        lax.fori_loop(0, t // tk, kv_tile, 0)
        dc_out[:, 0:128] = dq_sc[...]

    blk = pl.BlockSpec((None, t, 128), lambda b, p: (b, 0, p))
    return pl.pallas_call(
        body, grid=(bsz, 4),
        in_specs=[pl.BlockSpec((None, t, 512), lambda b, p: (b, 0, C_BLK0 + p)),
                  blk,
                  pl.BlockSpec((None, t, 128), lambda b, p: (b, 0, 4 + p)),
                  blk, blk,
                  pl.BlockSpec((None, None, 8, t), lambda b, p: (b, p, 0, 0))],
        out_specs=[pl.BlockSpec((None, t, 512), lambda b, p: (b, 0, p)),
                   pl.BlockSpec((None, None, 8, t), lambda b, p: (b, p, 0, 0)), blk],
        out_shape=[jax.ShapeDtypeStruct((bsz, t, C_W), F32), jax.ShapeDtypeStruct((bsz, 4, 8, t), F32),
                   jax.ShapeDtypeStruct((bsz, t, FOX_W), F32)],
        scratch_shapes=[pltpu.VMEM((t, 128), F32), pltpu.VMEM((t, 128), F32), pltpu.VMEM((t, 128), F32)],
        compiler_params=_cparams(("parallel", "parallel")), name=name)(proj3, o_raw, dmixed, lse, c_nat, c_t)


def _mix_tm(n):
    return min(512, n)


def _outproj_fwd(x2, oa, ob, oc, wo, g_row, name):
    n, d = x2.shape
    tm = _mix_tm(n)

    def body(x_ref, oa_ref, ob_ref, oc_ref, w_ref, g_ref, y_ref, xo_ref):
        y = (jnp.dot(oa_ref[...].astype(BF16), w_ref[0:256, :], preferred_element_type=F32)
             + jnp.dot(ob_ref[...].astype(BF16), w_ref[256:512, :], preferred_element_type=F32)
             + jnp.dot(oc_ref[...].astype(BF16), w_ref[512:1024, :], preferred_element_type=F32))
        y_ref[...] = y
        xo_ref[...] = x_ref[...] + y * _rstd(y) * g_ref[...]

    row = lambda w: pl.BlockSpec((tm, w), lambda i: (i, 0))
    out = jax.ShapeDtypeStruct((n, d), F32)
    return pl.pallas_call(
        body, grid=(n // tm,),
        in_specs=[row(d), row(256), row(256), row(512), pl.BlockSpec((d, d), lambda i: (0, 0)),
                  pl.BlockSpec((1, d), lambda i: (0, 0))],
        out_specs=[row(d), row(d)], out_shape=[out, out],
        compiler_params=_cparams(("parallel",)), name=name)(x2, oa, ob, oc, wo, g_row)


def _outproj_fwd_loss(x2, oa, ob, oc, wo, g_row, target2, name):
    n, d = x2.shape
    tm = _mix_tm(n)

    def body(x_ref, oa_ref, ob_ref, oc_ref, w_ref, g_ref, t_ref, y_ref, dx_ref, l_ref):
        y = (jnp.dot(oa_ref[...].astype(BF16), w_ref[0:256, :], preferred_element_type=F32)
             + jnp.dot(ob_ref[...].astype(BF16), w_ref[256:512, :], preferred_element_type=F32)
             + jnp.dot(oc_ref[...].astype(BF16), w_ref[512:1024, :], preferred_element_type=F32))
        y_ref[...] = y
        err = (x_ref[...] + y * _rstd(y) * g_ref[...]) - t_ref[...]
        dx_ref[...] = err * (1.0 / d)

        @pl.when(pl.program_id(0) == 0)
        def _():
            l_ref[...] = jnp.zeros((8, 128), F32)

        l_ref[...] += jnp.sum(err * err)

    row = lambda w: pl.BlockSpec((tm, w), lambda i: (i, 0))
    out = jax.ShapeDtypeStruct((n, d), F32)
    return pl.pallas_call(
        body, grid=(n // tm,),
        in_specs=[row(d), row(256), row(256), row(512), pl.BlockSpec((d, d), lambda i: (0, 0)),
                  pl.BlockSpec((1, d), lambda i: (0, 0)), row(d)],
        out_specs=[row(d), row(d), pl.BlockSpec((8, 128), lambda i: (0, 0))],
        out_shape=[out, out, jax.ShapeDtypeStruct((8, 128), F32)],
        compiler_params=_cparams(("arbitrary",)), name=name)(x2, oa, ob, oc, wo, g_row, target2)


def _outproj_bwd(dxo, y, oa, ob, oc, wo, g_row, name):
    n, d = dxo.shape
    tm = _mix_tm(n)

    def body(dx_ref, y_ref, oa_ref, ob_ref, oc_ref, w_ref, g_ref, dm_ref, dw_ref, dg_ref):
        @pl.when(pl.program_id(0) == 0)
        def _():
            dw_ref[...] = jnp.zeros((d, d), F32)
            dg_ref[...] = jnp.zeros((8, d), F32)

        yv, dx = y_ref[...], dx_ref[...]
        r = _rstd(yv)
        yn = yv * r
        dg_ref[...] += jnp.sum(dx * yn, axis=0, keepdims=True)
        dyn = dx * g_ref[...]
        dy = (r * (dyn - yn * jnp.mean(dyn * yn, axis=-1, keepdims=True))).astype(BF16)
        dm_ref[...] = _dot(dy, w_ref[...], NT)
        dw_ref[0:256, :] += _dot(oa_ref[...].astype(BF16), dy, TN)
        dw_ref[256:512, :] += _dot(ob_ref[...].astype(BF16), dy, TN)
        dw_ref[512:1024, :] += _dot(oc_ref[...].astype(BF16), dy, TN)

    row = lambda w: pl.BlockSpec((tm, w), lambda i: (i, 0))
    fixed = lambda r, c: pl.BlockSpec((r, c), lambda i: (0, 0))
    return pl.pallas_call(
        body, grid=(n // tm,),
        in_specs=[row(d), row(d), row(256), row(256), row(512), fixed(d, d), fixed(1, d)],
        out_specs=[row(d), fixed(d, d), fixed(8, d)],
        out_shape=[jax.ShapeDtypeStruct((n, d), F32), jax.ShapeDtypeStruct((d, d), F32), jax.ShapeDtypeStruct((8, d), F32)],
        compiler_params=_cparams(("arbitrary",)), name=name)(dxo, y, oa, ob, oc, wo, g_row)


_PIECES = ((0, A_W), (A_W, B_W), (A_W + B_W, C_W), (A_W + B_W + C_W, F_W))


def _inproj_bwd_x(x2, dxo, g_row, w_int, pieces, name):
    n, d = x2.shape
    tm = min(256, n)

    def body(x_ref, dxo_ref, g_ref, w_ref, da_ref, db_ref, dc_ref, df_ref, dx_ref, dg_ref):
        @pl.when(pl.program_id(0) == 0)
        def _():
            dg_ref[...] = jnp.zeros((8, d), F32)

        dh = jnp.zeros((tm, d), F32)
        for ref, (o, w) in zip((da_ref, db_ref, dc_ref, df_ref), _PIECES):
            dh = dh + _dot(ref[...].astype(BF16), w_ref[:, o:o + w], NT)
        x = x_ref[...]
        r = _rstd(x)
        xn = x * r
        dg_ref[...] += jnp.sum(dh * xn, axis=0, keepdims=True)
        dxn = dh * g_ref[...]
        dx_ref[...] = dxo_ref[...] + r * (dxn - xn * jnp.mean(dxn * xn, axis=-1, keepdims=True))

    row = lambda w: pl.BlockSpec((tm, w), lambda i: (i, 0))
    fixed = lambda r, c: pl.BlockSpec((r, c), lambda i: (0, 0))
    return pl.pallas_call(
        body, grid=(n // tm,),
        in_specs=[row(d), row(d), fixed(1, d), fixed(d, E_INT)] + [row(w) for _, w in _PIECES],
        out_specs=[row(d), fixed(8, d)],
        out_shape=[jax.ShapeDtypeStruct((n, d), F32), jax.ShapeDtypeStruct((8, d), F32)],
        compiler_params=_cparams(("arbitrary",)), name=name)(x2, dxo, g_row, w_int, *pieces)


def _inproj_bwd_w(x2, g_row, pieces, name):
    n, d = x2.shape
    tm = min(256, n)

    def body(x_ref, g_ref, da_ref, db_ref, dc_ref, df_ref, dw_ref):
        @pl.when(pl.program_id(0) == 0)
        def _():
            dw_ref[...] = jnp.zeros((d, E_INT), F32)

        x = x_ref[...]
        h = (x * _rstd(x) * g_ref[...]).astype(BF16)
        for ref, (o, w) in zip((da_ref, db_ref, dc_ref, df_ref), _PIECES):
            dw_ref[:, o:o + w] += _dot(h, ref[...].astype(BF16), TN)

    row = lambda w: pl.BlockSpec((tm, w), lambda i: (i, 0))
    return pl.pallas_call(
        body, grid=(n // tm,),
        in_specs=[row(d), pl.BlockSpec((1, d), lambda i: (0, 0))] + [row(w) for _, w in _PIECES],
        out_specs=pl.BlockSpec((d, E_INT), lambda i: (0, 0)),
        out_shape=jax.ShapeDtypeStruct((d, E_INT), F32),
        compiler_params=_cparams(("arbitrary",), vmem_mb=56), name=name)(x2, g_row, *pieces)


def _block_diag(pool_w_l):
    z = jnp.zeros((64, 64), pool_w_l.dtype)
    return jnp.concatenate(
        [jnp.concatenate([pool_w_l[g] if c == g else z for c in range(4)], axis=1) for g in range(4)], axis=0)


def _pad_lanes(v, width=128):
    return jnp.pad(v, ((0, 0),) * (v.ndim - 1) + ((0, width - v.shape[-1]),))


def _local_step(x, target, lower_bounds, pre_norm_g, w_in_int, hgrn_norm_g, fox_f_bias, pool_w, pool_scale,
                w_out_bf, post_norm_g, on_weight_grads):
    bsz, t, d = x.shape
    n = bsz * t
    lbs = _lbs_fwd(lower_bounds)
    saved = []
    xc = x.reshape(n, d)
    for l in range(DEPTH):
        proj = _inproj_fwd(xc, pre_norm_g[l:l + 1], w_in_int[l], f"inproj_fwd{l}").reshape(bsz, t, E_INT)
        wbd = _block_diag(pool_w[l]).astype(BF16)
        bias_row = _pad_lanes(fox_f_bias[l:l + 1])
        oa, oa_raw, states = _hgrn_fwd(proj, lbs[l:l + 1], hgrn_norm_g[l:l + 1], f"hgrn_fwd{l}")
        ob = _pool_fwd(proj, wbd, pool_scale[l:l + 1], f"pool_fwd{l}")
        c_nat, c_t = _foxgate_fwd(proj, bias_row, f"foxgate_fwd{l}")
        oc, oc_raw, lse = _fox_fwd(proj, c_nat, c_t, f"fox_fwd{l}")
        mixed = (oa.reshape(n, -1), ob.reshape(n, -1), oc.reshape(n, -1))
        if l < DEPTH - 1:
            y, xn = _outproj_fwd(xc, *mixed, w_out_bf[l], post_norm_g[l:l + 1], f"outproj_fwd{l}")
        else:
            y, dx, sq = _outproj_fwd_loss(xc, *mixed, w_out_bf[l], post_norm_g[l:l + 1], target.reshape(n, d),
                                          f"outproj_fwd{l}")
        saved.append((xc, proj, wbd, bias_row, oa, oa_raw, states, ob, oc, oc_raw, lse, c_nat, c_t, y))
        xc = xn
    g = {k: [None] * DEPTH for k in ("pre", "hgn", "bias", "pool_w", "pool_scale", "post", "lbs")}
    handed = [None] * DEPTH
    for l in reversed(range(DEPTH)):
        xin, proj, wbd, bias_row, oa, oa_raw, states, ob, oc, oc_raw, lse, c_nat, c_t, y = saved[l]
        dmix, d_w_out, dpost = _outproj_bwd(dx, y, oa.reshape(n, -1), ob.reshape(n, -1), oc.reshape(n, -1),
                                            w_out_bf[l], post_norm_g[l:l + 1], f"outproj_bwd{l}")
        g["post"][l] = dpost[0]
        dmix3 = dmix.reshape(bsz, t, d)
        d_c, dct, drow = _fox_bwd(proj, oc_raw, dmix3, lse, c_nat, c_t, f"fox_bwd{l}")
        dc_nat = _pad_lanes(dct[:, :, 0:2, :].reshape(bsz, FOX_HEADS, t).transpose(0, 2, 1)
                            + drow.reshape(bsz, t, FOX_HEADS, 64)[..., 0])
        d_f, dbias = _foxgate_bwd(proj, dc_nat, bias_row, f"foxgate_bwd{l}")
        g["bias"][l] = jnp.sum(dbias[:, 0, :FOX_HEADS], axis=0)
        d_b, dscale, dwbd = _pool_bwd(proj, dmix3, wbd, pool_scale[l:l + 1], f"pool_bwd{l}")
        g["pool_scale"][l] = jnp.sum(dscale[:, 0], axis=0)
        dwbd = jnp.sum(dwbd, axis=0)
        g["pool_w"][l] = jnp.stack([dwbd[64 * k:64 * (k + 1), 64 * k:64 * (k + 1)] for k in range(4)])
        d_a, dgn, dlb = _hgrn_bwd(proj, oa_raw, dmix3, states, lbs[l:l + 1], hgrn_norm_g[l:l + 1], f"hgrn_bwd{l}")
        g["hgn"][l] = jnp.sum(dgn[:, 0], axis=0)
        g["lbs"][l] = jnp.sum(dlb[:, 0], axis=0)
        pieces = [p.reshape(n, -1) for p in (d_a, d_b, d_c, d_f)]
        handed[l] = on_weight_grads(l, _inproj_bwd_w(xin, pre_norm_g[l:l + 1], pieces, f"inproj_bwd_w{l}"), d_w_out)
        dx, dpre = _inproj_bwd_x(xin, dx, pre_norm_g[l:l + 1], w_in_int[l], pieces, f"inproj_bwd_x{l}")
        g["pre"][l] = dpre[0]
    grads = {k: jnp.stack(v) for k, v in g.items()}
    return sq, dx.reshape(bsz, t, d), grads, handed


def _place():
    return lax.axis_index("x"), lax.axis_index("y"), lax.axis_index("c")


def _other_chips(x, y):
    return [(1 - x, y), (x, 1 - y), (1 - x, 1 - y)]


_ANY = pl.BlockSpec(memory_space=pl.ANY)


def _gather_body(handshake, n_arrays):
    def body(*refs):
        srcs, dsts = refs[:n_arrays], refs[n_arrays:2 * n_arrays]
        ici_send, ici_recv, d2d_send, d2d_recv, local_sems = refs[2 * n_arrays:]
        x, y, c = _place()
        if handshake:
            barrier = pltpu.get_barrier_semaphore()
            for peer in [(px, py, c) for px, py in _other_chips(x, y)] + [(x, y, 1 - c)]:
                pl.semaphore_signal(barrier, inc=1, device_id=peer, device_id_type=MESH)
            pl.semaphore_wait(barrier, 4)
        me = 2 * x + y
        pairs = list(zip(srcs, dsts))
        order = [(k, j) for k in range(3) for j in range(n_arrays)]
        mine = [pltpu.make_async_copy(src, dst.at[me], local_sems.at[j]) for j, (src, dst) in enumerate(pairs)]
        for cp in mine:
            cp.start()
        chips = _other_chips(x, y)
        sends = [pltpu.make_async_remote_copy(
            src_ref=pairs[j][0].at[c], dst_ref=pairs[j][1].at[me, c], send_sem=ici_send.at[n], recv_sem=ici_recv.at[n],
            device_id=(chips[k][0], chips[k][1], c), device_id_type=MESH) for n, (k, j) in enumerate(order)]
        for cp in sends:
            cp.start()
        passed = [pltpu.make_async_remote_copy(
            src_ref=pairs[j][1].at[2 * chips[k][0] + chips[k][1], c], dst_ref=pairs[j][1].at[2 * chips[k][0] + chips[k][1], c],
            send_sem=d2d_send.at[n], recv_sem=d2d_recv.at[n], device_id=(x, y, 1 - c), device_id_type=MESH)
            for n, (k, j) in enumerate(order)]
        for n, (k, j) in enumerate(order):
            px, py = chips[k]
            src, dst = pairs[j]
            pltpu.make_async_remote_copy(
                src_ref=src.at[c], dst_ref=dst.at[2 * px + py, c], send_sem=ici_send.at[n], recv_sem=ici_recv.at[n],
                device_id=(px, py, c), device_id_type=MESH).wait_recv()
            passed[n].start()
        for n, (k, j) in enumerate(order):
            px, py = chips[k]
            src, dst = pairs[j]
            pltpu.make_async_remote_copy(
                src_ref=dst.at[2 * px + py, 1 - c], dst_ref=dst.at[2 * px + py, 1 - c], send_sem=d2d_send.at[n],
                recv_sem=d2d_recv.at[n], device_id=(x, y, 1 - c), device_id_type=MESH).wait_recv()
        for cp in sends + passed:
            cp.wait_send()
        for cp in mine:
            cp.wait()

    return body


def _gather_sems(n_arrays):
    return [pltpu.SemaphoreType.DMA((3 * n_arrays,))] * 4 + [pltpu.SemaphoreType.DMA((n_arrays,))]


def _gathered(a):
    return jax.ShapeDtypeStruct((N_CHIPS,) + a.shape, a.dtype)


def _gather_weights(arrays):
    n = len(arrays)
    return pl.pallas_call(
        _gather_body(False, n), in_specs=[_ANY] * n, out_specs=[_ANY] * n, out_shape=[_gathered(a) for a in arrays],
        scratch_shapes=_gather_sems(n), name="gather_weights")(*arrays)


def _gather_weights_beside(arrays):
    hbm = pltpu.MemorySpace.HBM
    n = len(arrays)
    srcs = [jax.new_ref(a, memory_space=hbm) for a in arrays]
    dsts = [jax.empty_ref(_gathered(a), memory_space=hbm) for a in arrays]
    body = _gather_body(True, n)

    @pl.kernel(mesh=plsc.ScalarSubcoreMesh(axis_name="sequencer", num_cores=1), name="gather_weights_beside",
               scratch_types=_gather_sems(n), compiler_params=pltpu.CompilerParams(collective_id=1))
    def launch(*sems):
        body(*srcs, *dsts, *sems)

    launch()
    return [d[...] for d in dsts]


def _swap_with_sibling(parts, name):
    k = len(parts)

    def body(*refs):
        src, dst = refs[:k], refs[k:2 * k]
        send_sems, recv_sems = refs[2 * k:]
        x, y, c = _place()
        cps = [pltpu.make_async_remote_copy(src_ref=src[j], dst_ref=dst[j], send_sem=send_sems.at[j], recv_sem=recv_sems.at[j],
                                            device_id=(x, y, 1 - c), device_id_type=MESH) for j in range(k)]
        for cp in cps:
            cp.start()
        for cp in cps:
            cp.wait()

    return pl.pallas_call(
        body, in_specs=[_ANY] * k, out_specs=[_ANY] * k,
        out_shape=[jax.ShapeDtypeStruct(p.shape, p.dtype) for p in parts],
        scratch_shapes=[pltpu.SemaphoreType.DMA((k,)), pltpu.SemaphoreType.DMA((k,))], name=name)(*parts)


N_PEERS = 7


def _grad_exchange_body():
    def body(pin_ref, pout_ref, lin_ref, lout_ref, send_sems, recv_sems):
        x, y, c = _place()
        barrier = pltpu.get_barrier_semaphore()
        for k in range(1, N_PEERS + 1):
            peer = (x ^ ((k >> 2) & 1), y ^ ((k >> 1) & 1), c ^ (k & 1))
            pl.semaphore_signal(barrier, inc=1, device_id=peer, device_id_type=MESH)
        pl.semaphore_wait(barrier, N_PEERS)
        me = 2 * x + y
        pairs = ((pin_ref, lin_ref), (pout_ref, lout_ref))
        cps = []
        for k, (px, py) in enumerate(_other_chips(x, y)):
            for r in range(2):
                for j, (src, dst) in enumerate(pairs):
                    cps.append(pltpu.make_async_remote_copy(
                        src_ref=src.at[2 * px + py, r], dst_ref=dst.at[2 * k + c], send_sem=send_sems.at[2 * (2 * k + r) + j],
                        recv_sem=recv_sems.at[2 * (2 * k + c) + j], device_id=(px, py, r), device_id_type=MESH))
        for j, (src, dst) in enumerate(pairs):
            cps.append(pltpu.make_async_remote_copy(
                src_ref=src.at[me, 1 - c], dst_ref=dst.at[N_PEERS - 1], send_sem=send_sems.at[2 * (N_PEERS - 1) + j],
                recv_sem=recv_sems.at[2 * (N_PEERS - 1) + j], device_id=(x, y, 1 - c), device_id_type=MESH))
        for cp in cps:
            cp.start()
        for s in range(N_PEERS):
            for j, (src, dst) in enumerate(pairs):
                pltpu.make_async_remote_copy(
                    src_ref=src.at[0, 0], dst_ref=dst.at[s], send_sem=send_sems.at[2 * s + j], recv_sem=recv_sems.at[2 * s + j],
                    device_id=(x, y, 1 - c), device_id_type=MESH).wait_recv()
        for cp in cps:
            cp.wait_send()

    return body


_EXCHANGE_SEMS = [pltpu.SemaphoreType.DMA((2 * N_PEERS,))] * 2


def _landing(p):
    return jax.ShapeDtypeStruct((N_PEERS,) + p.shape[2:], p.dtype)


def _grad_exchange_beside(pin, pout, name, collective_id):
    hbm = pltpu.MemorySpace.HBM
    pin_ref, pout_ref = jax.new_ref(pin, memory_space=hbm), jax.new_ref(pout, memory_space=hbm)
    lin_ref, lout_ref = jax.empty_ref(_landing(pin), memory_space=hbm), jax.empty_ref(_landing(pout), memory_space=hbm)
    body = _grad_exchange_body()

    @pl.kernel(mesh=plsc.ScalarSubcoreMesh(axis_name="sequencer", num_cores=1), name=name,
               scratch_types=_EXCHANGE_SEMS, compiler_params=pltpu.CompilerParams(collective_id=collective_id))
    def launch(send_sems, recv_sems):
        body(pin_ref, pout_ref, lin_ref, lout_ref, send_sems, recv_sems)

    launch()
    return lin_ref[...], lout_ref[...]


def _add_n(parts, name, with_bf16=False):
    r, c = parts[0].shape
    tr = 256 if r % 256 == 0 else r
    n = len(parts)

    def body(*refs):
        acc = refs[0][...].astype(F32)
        for ref in refs[1:n]:
            acc = acc + ref[...].astype(F32)
        refs[n][...] = acc
        if with_bf16:
            refs[n + 1][...] = acc.astype(BF16)

    blk = pl.BlockSpec((tr, c), lambda i: (i, 0))
    outs = [jax.ShapeDtypeStruct((r, c), F32)] + ([jax.ShapeDtypeStruct((r, c), BF16)] if with_bf16 else [])
    res = pl.pallas_call(
        body, grid=(r // tr,), in_specs=[blk] * n, out_specs=[blk] * len(outs),
        out_shape=outs, compiler_params=_cparams(("parallel",)), name=name)(*parts)
    return res if with_bf16 else res[0]


def _all_reduce_small(packet):
    r, w = packet.shape

    def body(p_ref, o_ref, buf, send_sems, recv_sems):
        x, y, c = _place()
        me = 4 * x + 2 * y + c
        buf[me] = p_ref[...]
        peers = []
        for k in range(1, 8):
            fx, fy, fc = (k >> 2) & 1, (k >> 1) & 1, k & 1
            peers.append((x ^ fx, y ^ fy, c ^ fc))
        cps = [pltpu.make_async_remote_copy(src_ref=p_ref, dst_ref=buf.at[me], send_sem=send_sems.at[k], recv_sem=recv_sems.at[k],
                                            device_id=peer, device_id_type=MESH) for k, peer in enumerate(peers)]
        for cp in cps:
            cp.start()
        for k, (px, py, pc) in enumerate(peers):
            pltpu.make_async_remote_copy(src_ref=p_ref, dst_ref=buf.at[4 * px + 2 * py + pc], send_sem=send_sems.at[k],
                                         recv_sem=recv_sems.at[k], device_id=(px, py, pc), device_id_type=MESH).wait_recv()
        for cp in cps:
            cp.wait_send()
        acc = buf[0]
        for k in range(1, 8):
            acc = acc + buf[k]
        o_ref[...] = acc

    vm = pl.BlockSpec(memory_space=pltpu.VMEM)
    return pl.pallas_call(
        body, in_specs=[vm], out_specs=vm, out_shape=jax.ShapeDtypeStruct((r, w), F32),
        scratch_shapes=[pltpu.VMEM((8, r, w), F32), pltpu.SemaphoreType.DMA((7,)), pltpu.SemaphoreType.DMA((7,))],
        name="all_reduce_small")(packet)


def _adamw_math(w, g, m, v):
    m = ADAM_B1 * m + (1.0 - ADAM_B1) * g
    v = ADAM_B2 * v + (1.0 - ADAM_B2) * (g * g)
    m_hat = m / (1.0 - ADAM_B1 ** ADAM_STEP)
    v_hat = v / (1.0 - ADAM_B2 ** ADAM_STEP)
    return -ADAM_LR * (m_hat / (jnp.sqrt(v_hat) + ADAM_EPS) + ADAM_WD * w), m, v


def _adamw(w, g, m, v, name):
    nl, r, c = w.shape
    tr = 256 if r % 256 == 0 else r

    def body(w_ref, g_ref, m_ref, v_ref, d_ref, mo_ref, vo_ref):
        d_ref[...], mo_ref[...], vo_ref[...] = _adamw_math(w_ref[...], g_ref[...], m_ref[...], v_ref[...])

    blk = pl.BlockSpec((None, tr, c), lambda l, i: (l, i, 0))
    out = jax.ShapeDtypeStruct(w.shape, F32)
    return pl.pallas_call(
        body, grid=(nl, r // tr), in_specs=[blk] * 4, out_specs=[blk] * 3, out_shape=[out] * 3,
        compiler_params=_cparams(("parallel", "parallel")), name=name)(w, g, m, v)


def _small_update(gsum, lower_bounds, wpack, mpack, vpack):
    r, w = gsum.shape
    lb_rows = DEPTH * HGRN_W // 128

    def body(g_ref, a_ref, w_ref, m_ref, v_ref, go_ref, d_ref, mo_ref, vo_ref):
        a = a_ref[...]
        a0, a1 = a[0:1], a[1:2]
        mx = jnp.maximum(a0, a1)
        e0, e1 = jnp.exp(a0 - mx), jnp.exp(a1 - mx)
        p0, p1 = e0 / (e0 + e1), e1 / (e0 + e1)
        g = g_ref[...]
        half = lb_rows // 2
        dl0 = jnp.concatenate([g[k:k + 1] for k in range(half)], axis=1)
        dl1 = jnp.concatenate([g[half + k:half + k + 1] for k in range(half)], axis=1)
        dp0 = (dl0 + dl1) - (dl0 + dl1)
        dp1 = dl1
        inner = p0 * dp0 + p1 * dp1
        da0, da1 = p0 * (dp0 - inner), p1 * (dp1 - inner)
        rows = [da0[:, 128 * k:128 * (k + 1)] for k in range(half)] + [da1[:, 128 * k:128 * (k + 1)] for k in range(half)]
        gfull = jnp.concatenate(rows + [g[lb_rows:]], axis=0)
        go_ref[...] = gfull
        d_ref[...], mo_ref[...], vo_ref[...] = _adamw_math(w_ref[...], gfull, m_ref[...], v_ref[...])

    vm = pl.BlockSpec(memory_space=pltpu.VMEM)
    out = jax.ShapeDtypeStruct((r, w), F32)
    return pl.pallas_call(body, in_specs=[vm] * 5, out_specs=[vm] * 4, out_shape=[out] * 4, name="small_update")(
        gsum, lower_bounds, wpack, mpack, vpack)


_SMALL = ("lower_bounds", "pre_norm_g", "hgrn_norm_g", "fox_f_bias", "pool_w", "pool_scale", "post_norm_g")


def _pack(parts):
    rows = []
    for k in _SMALL:
        f = parts[k].reshape(-1)
        pad = (-f.shape[0]) % (8 * 128)
        rows.append(jnp.pad(f, (0, pad)).reshape(-1, 128))
    rows.append(jnp.zeros((8, 128), F32))
    return jnp.concatenate(rows, axis=0)


def _unpack(pack, like):
    out, r = {}, 0
    for k in _SMALL:
        size = int(np.prod(like[k].shape))
        nr = -(-size // (8 * 128)) * 8
        out[k] = pack[r:r + nr].reshape(-1)[:size].reshape(like[k].shape)
        r += nr
    return out, r


def kernel(x, lower_bounds, pre_norm_g, w_in, hgrn_norm_g, fox_f_bias, pool_w, pool_scale, w_out, post_norm_g, loss_target, m_lower_bounds, m_pre_norm_g, m_w_in, m_hgrn_norm_g, m_fox_f_bias, m_pool_w, m_pool_scale, m_w_out, m_post_norm_g, v_lower_bounds, v_pre_norm_g, v_w_in, v_hgrn_norm_g, v_fox_f_bias, v_pool_w, v_pool_scale, v_w_out, v_post_norm_g):
    cx, cy, cc = _place()
    chip = 2 * cx + cy

    halves = lambda w, l: w[l].reshape(2, w.shape[1] // 2, w.shape[2]).astype(BF16)
    needed_first = _gather_weights([halves(w_in, 0)])
    needed_first, later = lax.optimization_barrier((needed_first, [halves(w_out, 0), halves(w_in, 1), halves(w_out, 1)]))
    later = _gather_weights_beside(later)
    w_in_int = [_internal_from_shards([a[q].reshape(D_MODEL, SHARD_W) for q in range(N_CHIPS)]) for a in (needed_first[0], later[1])]
    w_out_full = [a.reshape(D_MODEL, D_MODEL) for a in (later[0], later[2])]

    def on_weight_grads(l, d_w_in, d_w_out):
        pin = _shards_from_internal(d_w_in).reshape(N_CHIPS, 2, D_MODEL // 2, SHARD_W)
        pout = d_w_out.reshape(N_CHIPS, 2, D_MODEL // (2 * N_CHIPS), D_MODEL)
        own = [lax.dynamic_index_in_dim(lax.dynamic_index_in_dim(p, chip, 0, False), cc, 0, False) for p in (pin, pout)]
        return own, _grad_exchange_beside(pin.astype(BF16), pout.astype(BF16), f"grad_exchange{l}", 2 + l)

    sq, grad_x, g, handed = _local_step(x, loss_target, lower_bounds, pre_norm_g, w_in_int, hgrn_norm_g, fox_f_bias,
                                        pool_w, pool_scale, w_out_full, post_norm_g, on_weight_grads)
    first = cc == 0

    def finish(l, own, landed):
        halves_l = [_add_n([o] + [t[s] for s in range(N_PEERS)], f"grad_sum{l}_{j}") for j, (o, t) in enumerate(zip(own, landed))]
        others = _swap_with_sibling(halves_l, f"grad_swap{l}")
        g_in, g_out = [jnp.where(first, jnp.concatenate([h, o], axis=0), jnp.concatenate([o, h], axis=0))[None]
                       for h, o in zip(halves_l, others)]
        return (g_in, g_out, _adamw(w_in[l:l + 1], g_in, m_w_in[l:l + 1], v_w_in[l:l + 1], f"adamw_w_in{l}"),
                _adamw(w_out[l:l + 1], g_out, m_w_out[l:l + 1], v_w_out[l:l + 1], f"adamw_w_out{l}"))

    grad_x, last = lax.optimization_barrier((grad_x, handed[1]))
    done = [None, finish(1, *last)]

    small = {"lower_bounds": g["lbs"], "pre_norm_g": g["pre"], "hgrn_norm_g": g["hgn"], "fox_f_bias": g["bias"],
             "pool_w": g["pool_w"], "pool_scale": g["pool_scale"], "post_norm_g": g["post"]}
    packet = _pack(small)
    nrows = packet.shape[0]
    packet = packet.at[nrows - 1].set(sq[0])
    gsum = _all_reduce_small(packet)
    loss = gsum[nrows - 1, 0] * (0.5 / D_MODEL)

    weights = {"lower_bounds": lower_bounds, "pre_norm_g": pre_norm_g, "hgrn_norm_g": hgrn_norm_g,
               "fox_f_bias": fox_f_bias, "pool_w": pool_w, "pool_scale": pool_scale, "post_norm_g": post_norm_g}
    moments_m = {"lower_bounds": m_lower_bounds, "pre_norm_g": m_pre_norm_g, "hgrn_norm_g": m_hgrn_norm_g,
                 "fox_f_bias": m_fox_f_bias, "pool_w": m_pool_w, "pool_scale": m_pool_scale, "post_norm_g": m_post_norm_g}
    moments_v = {"lower_bounds": v_lower_bounds, "pre_norm_g": v_pre_norm_g, "hgrn_norm_g": v_hgrn_norm_g,
                 "fox_f_bias": v_fox_f_bias, "pool_w": v_pool_w, "pool_scale": v_pool_scale, "post_norm_g": v_post_norm_g}
    gp, dp, mp, vp = _small_update(gsum, lower_bounds, _pack(weights), _pack(moments_m), _pack(moments_v))
    gs, _ = _unpack(gp, weights)
    ds, _ = _unpack(dp, weights)
    ms, _ = _unpack(mp, weights)
    vs, _ = _unpack(vp, weights)

    first_layer, _ = lax.optimization_barrier((handed[0], (done[1], gp, dp, mp, vp)))
    done[0] = finish(0, *first_layer)
    both = lambda pick: jnp.concatenate([pick(done[l]) for l in range(DEPTH)], axis=0)
    grad_w_in, grad_w_out = both(lambda r: r[0]), both(lambda r: r[1])
    d_in, m_in, v_in = [both(lambda r, k=k: r[2][k]) for k in range(3)]
    d_out, m_out, v_out = [both(lambda r, k=k: r[3][k]) for k in range(3)]

    def ordered(s, big_in, big_out):
        return (s["lower_bounds"], s["pre_norm_g"], big_in, s["hgrn_norm_g"], s["fox_f_bias"], s["pool_w"],
                s["pool_scale"], big_out, s["post_norm_g"])

    return (loss, grad_x, *ordered(gs, grad_w_in, grad_w_out), *ordered(ds, d_in, d_out),
            *ordered(ms, m_in, m_out), *ordered(vs, v_in, v_out))
```

```python
import functools

import numpy as np
import jax
import jax.numpy as jnp
from jax import lax
from jax.experimental import pallas as pl
from jax.experimental.pallas import tpu as pltpu
from jax.experimental.pallas import tpu_sc as plsc

F32 = jnp.float32
BF16 = jnp.bfloat16
HI = lax.Precision.HIGHEST
MESH = pl.DeviceIdType.MESH

NORM_EPS = 1e-6
MASK_VALUE = -1e30
TINY = 1e-30
ADAM_LR, ADAM_B1, ADAM_B2, ADAM_EPS, ADAM_WD, ADAM_STEP = 0.001, 0.9, 0.999, 1e-08, 0.01, 10

D_MODEL = 1024
DEPTH = 2
N_CHIPS = 4
CHUNK = 64
LANES = 128
HGRN_W, POOL_W, FOX_W, FOX_HEADS = 256, 256, 512, 8
POOL_WINDOWS = (2, 4, 8, 16)
POOL_HALO = 16
IN_WIDTH = 3592
SHARD_W = IN_WIDTH // N_CHIPS
A_W, B_W, C_W, F_W = 1024, 512, 2048, 128
E_INT = A_W + B_W + C_W + F_W
B_BLK = A_W // 512
C_BLK0 = (A_W + B_W) // 512
F_BLK = (A_W + B_W + C_W) // 128


def _segments():
    segs = []
    for hp in range(2):
        for part in range(4):
            segs.append((part * 256 + hp * 128, 128))
    segs.append((1024, 256))
    segs.append((1280, 256))
    for hp in range(4):
        for part in range(4):
            segs.append((1536 + part * 512 + hp * 128, 128))
    segs.append((3584, 8))
    return segs


_SEGS = _segments()


def _to_internal(w):
    parts = [w[..., s:s + n] for s, n in _SEGS]
    parts.append(jnp.zeros(w.shape[:-1] + (E_INT - IN_WIDTH,), w.dtype))
    return jnp.concatenate(parts, axis=-1)


def _to_original(w):
    offs, o = [], 0
    for s, n in _SEGS:
        offs.append((s, o, n))
        o += n
    parts = [w[..., o:o + n] for s, o, n in sorted(offs)]
    return jnp.concatenate(parts, axis=-1)


def _internal_from_shards(shards):
    parts = []
    for s, n in _SEGS:
        while n > 0:
            q, r = divmod(s, SHARD_W)
            take = min(n, SHARD_W - r)
            parts.append(shards[q][..., r:r + take])
            s, n = s + take, n - take
    parts.append(jnp.zeros(shards[0].shape[:-1] + (E_INT - IN_WIDTH,), shards[0].dtype))
    return jnp.concatenate(parts, axis=-1)


def _shards_from_internal(w):
    offs, o = [], 0
    for s, n in _SEGS:
        offs.append((s, o, n))
        o += n
    blocks = []
    for q in range(N_CHIPS):
        lo, hi = SHARD_W * q, SHARD_W * (q + 1)
        parts = [w[..., o + max(lo, s) - s:o + min(hi, s + n) - s] for s, o, n in sorted(offs) if s < hi and s + n > lo]
        blocks.append(jnp.concatenate(parts, axis=-1))
    return jnp.stack(blocks)


def _cparams(sem=None, vmem_mb=48):
    kw = dict(vmem_limit_bytes=vmem_mb * 1024 * 1024)
    if sem is not None:
        kw["dimension_semantics"] = sem
    return pltpu.CompilerParams(**kw)


def _sig(x):
    return 1.0 / (1.0 + jnp.exp(-x))


def _silu(x):
    return x * _sig(x)


def _dsilu(x):
    s = _sig(x)
    return s * (1.0 + x * (1.0 - s))


def _rstd(x):
    return lax.rsqrt(jnp.mean(x * x, axis=-1, keepdims=True) + NORM_EPS)


def _dot(a, b, dims, **kw):
    return lax.dot_general(a, b, (dims, ((), ())), preferred_element_type=F32, **kw)


NN = ((1,), (0,))
NT = ((1,), (1,))
TN = ((0,), (0,))


def _iota(shape, dim):
    return lax.broadcasted_iota(jnp.int32, shape, dim)


def _lbs_fwd(lower_bounds):
    def body(a_ref, o_ref):
        a = a_ref[...]
        a0, a1 = a[0:1], a[1:2]
        m = jnp.maximum(a0, a1)
        e0, e1 = jnp.exp(a0 - m), jnp.exp(a1 - m)
        p0, p1 = e0 / (e0 + e1), e1 / (e0 + e1)
        o_ref[...] = jnp.concatenate([p0 - p0, (p0 + p1) - p0], axis=0)

    return pl.pallas_call(body, out_shape=jax.ShapeDtypeStruct(lower_bounds.shape, F32), name="lbs_fwd")(lower_bounds)


def _inproj_fwd(x2, g_row, w_int, name):
    n, d = x2.shape
    e = w_int.shape[1]
    tm = min(512, n)

    def body(x_ref, g_ref, w_ref, o_ref):
        x = x_ref[...]
        h = (x * _rstd(x) * g_ref[...]).astype(BF16)
        o_ref[...] = jnp.dot(h, w_ref[...], preferred_element_type=F32)

    return pl.pallas_call(
        body, grid=(n // tm,),
        in_specs=[pl.BlockSpec((tm, d), lambda i: (i, 0)), pl.BlockSpec((1, d), lambda i: (0, 0)),
                  pl.BlockSpec((d, e), lambda i: (0, 0))],
        out_specs=pl.BlockSpec((tm, e), lambda i: (i, 0)),
        out_shape=jax.ShapeDtypeStruct((n, e), F32),
        compiler_params=_cparams(("parallel",)), name=name)(x2, g_row, w_int)


def _chunk_cumsum_matrix():
    i, j = _iota((LANES, LANES), 0), _iota((LANES, LANES), 1)
    return ((i <= j) & ((i // CHUNK) == (j // CHUNK))).astype(F32)


def _hgrn_gates(a, lb):
    qa, z = a[:, 0:128], a[:, 128:256]
    sg, sgn = _sig(z), _sig(-z)
    fg = lb + (1.0 - lb) * sg
    lf = jnp.log(jnp.maximum(fg, TINY))
    kk = (1.0 - lb) * sgn
    return qa * _sig(qa), kk, lf, sg, sgn, fg


def _hgrn_fwd(proj3, lbs_row, gn_col, name):
    bsz, t, _ = proj3.shape
    nt = t // LANES

    def body(a_ref, lb_ref, gn_ref, og_ref, or_ref):
        lb = lb_ref[...]
        gn = gn_ref[...]
        umat = _chunk_cumsum_matrix()
        lane64 = _iota((1, LANES), 1) % CHUNK

        def tile(i, carry):
            r0 = pl.multiple_of(i * LANES, LANES)
            a = a_ref[pl.ds(r0, LANES), :]
            qq, kk, lf, _, _, _ = _hgrn_gates(a, lb)
            va, ga = a[:, 256:384], a[:, 384:512]
            q_t, k_t, v_t = qq.T, kk.T, va.T
            b_t = jnp.dot(lf.T, umat, precision=HI, preferred_element_type=F32)
            new_s, o_heads = [], []
            for h in range(2):
                s_h = carry[h]
                rs = slice(CHUNK * h, CHUNK * (h + 1))
                qh, kh, vh, bh = q_t[rs], k_t[rs], v_t[rs], b_t[rs]
                inter = []
                for c in range(2):
                    cs = slice(CHUNK * c, CHUNK * (c + 1))
                    b_ = bh[:, cs]
                    qt = (qh[:, cs] * jnp.exp(b_)).astype(BF16)
                    inter.append(_dot(s_h.astype(BF16), qt, TN))
                    bl = b_[:, CHUNK - 1:CHUNK]
                    kt = (kh[:, cs] * jnp.exp(bl - b_)).astype(BF16)
                    s_h = jnp.exp(bl) * s_h + _dot(kt, vh[:, cs].astype(BF16), NT)
                new_s.append(s_h)

                acc = jnp.concatenate(inter, axis=1) + jnp.sum(qh * kh, axis=0, keepdims=True) * vh
                for dlt in range(1, CHUNK):
                    kr, br, vr = pltpu.roll(kh, dlt, 1), pltpu.roll(bh, dlt, 1), pltpu.roll(vh, dlt, 1)
                    e = jnp.exp(jnp.minimum(bh - br, 0.0))
                    att = jnp.sum(qh * kr * e, axis=0, keepdims=True)
                    acc = acc + jnp.where(lane64 >= dlt, att, 0.0) * vr
                o_heads.append(acc)
            normed = []
            for h in range(2):
                o_h = o_heads[h]
                ms = jnp.mean(o_h * o_h, axis=0, keepdims=True)
                normed.append(o_h * lax.rsqrt(ms + NORM_EPS) * gn[CHUNK * h:CHUNK * (h + 1)])
            or_ref[pl.ds(r0, LANES), :] = jnp.concatenate(o_heads, axis=0).T
            og_ref[pl.ds(r0, LANES), :] = jnp.concatenate(normed, axis=0).T * _silu(ga)
            return tuple(new_s)

        zero = jnp.zeros((CHUNK, CHUNK), F32)
        lax.fori_loop(0, nt, tile, (zero, zero))

    out = jax.ShapeDtypeStruct((bsz, t, HGRN_W), F32)
    return pl.pallas_call(
        body, grid=(bsz, 2),
        in_specs=[pl.BlockSpec((None, t, 512), lambda b, p: (b, 0, p)),
                  pl.BlockSpec((1, 128), lambda b, p: (0, p)),
                  pl.BlockSpec((128, 1), lambda b, p: (p, 0))],
        out_specs=[pl.BlockSpec((None, t, 128), lambda b, p: (b, 0, p)),
                   pl.BlockSpec((None, t, 128), lambda b, p: (b, 0, p))],
        out_shape=[out, out],
        compiler_params=_cparams(("parallel", "parallel")), name=name)(proj3, lbs_row, gn_col)


def _hgrn_bwd(proj3, o_raw, dmixed, lbs_row, gn_row, name):
    bsz, t, _ = proj3.shape
    nt = t // LANES
    nchunk = t // CHUNK

    def body(a_ref, or_ref, do_ref, lb_ref, gn_ref, da_ref, dgn_ref, dlb_ref, s_sc):
        lb = lb_ref[...]
        gn = gn_ref[...]
        umat = _chunk_cumsum_matrix()
        lane = _iota((1, LANES), 1)
        lane64 = lane % CHUNK
        half = lane < CHUNK

        def t_layout(a):
            qq, kk, lf, sg, sgn, fg = _hgrn_gates(a, lb)
            b_t = jnp.dot(lf.T, umat, precision=HI, preferred_element_type=F32)
            return qq.T, kk.T, a[:, 256:384].T, b_t, (sg, sgn, fg)

        def fwd_tile(i, carry):
            r0 = pl.multiple_of(i * LANES, LANES)
            q_t, k_t, v_t, b_t, _ = t_layout(a_ref[pl.ds(r0, LANES), :])
            new_s = []
            for h in range(2):
                s_h = carry[h]
                rs = slice(CHUNK * h, CHUNK * (h + 1))
                for c in range(2):
                    cs = slice(CHUNK * c, CHUNK * (c + 1))
                    s_sc[h, 2 * i + c] = s_h
                    b_ = b_t[rs, cs]
                    bl = b_[:, CHUNK - 1:CHUNK]
                    kt = (k_t[rs, cs] * jnp.exp(bl - b_)).astype(BF16)
                    s_h = jnp.exp(bl) * s_h + _dot(kt, v_t[rs, cs].astype(BF16), NT)
                new_s.append(s_h)
            return tuple(new_s)

        zero = jnp.zeros((CHUNK, CHUNK), F32)
        lax.fori_loop(0, nt, fwd_tile, (zero, zero))

        def half_mean(v):
            m0 = jnp.sum(jnp.where(half, v, 0.0), axis=1, keepdims=True) * (1.0 / CHUNK)
            m1 = jnp.sum(jnp.where(half, 0.0, v), axis=1, keepdims=True) * (1.0 / CHUNK)
            return jnp.where(half, m0, m1)

        def bwd_tile(k, carry):
            ds0, ds1, dgn_acc, dlb_acc = carry
            i = nt - 1 - k
            r0 = pl.multiple_of(i * LANES, LANES)
            a = a_ref[pl.ds(r0, LANES), :]
            qa, z, ga = a[:, 0:128], a[:, 128:256], a[:, 384:512]
            q_t, k_t, v_t, b_t, (sg, sgn, fg) = t_layout(a)
            oraw = or_ref[pl.ds(r0, LANES), :]
            dout = do_ref[pl.ds(r0, LANES), :]
            r = lax.rsqrt(half_mean(oraw * oraw) + NORM_EPS)
            xn = oraw * r
            dga = dout * (xn * gn) * _dsilu(ga)
            don = dout * _silu(ga)
            dgn_acc = dgn_acc + jnp.sum(don * xn, axis=0, keepdims=True)
            dxn = don * gn
            do_t = (r * (dxn - xn * half_mean(dxn * xn))).T
            new_ds, dq_h, dk_h, dv_h, db_h = [], [], [], [], []
            for h in range(2):
                ds_h = (ds0, ds1)[h]
                rs = slice(CHUNK * h, CHUNK * (h + 1))
                qh, kh, vh, bh, doh = q_t[rs], k_t[rs], v_t[rs], b_t[rs], do_t[rs]
                dq_c, dk_c, dv_c, dbl_c = [None, None], [None, None], [None, None], [None, None]
                for c in (1, 0):
                    cs = slice(CHUNK * c, CHUNK * (c + 1))
                    s_n = s_sc[h, 2 * i + c]
                    b_ = bh[:, cs]
                    eb = jnp.exp(b_)
                    bl = b_[:, CHUNK - 1:CHUNK]
                    ek = jnp.exp(bl - b_)
                    ebl = jnp.exp(bl)
                    qt, kt = qh[:, cs] * eb, kh[:, cs] * ek
                    do_c = doh[:, cs].astype(BF16)
                    dsb = ds_h.astype(BF16)
                    dv_c[c] = _dot(dsb, kt.astype(BF16), TN)
                    dkt = _dot(dsb, vh[:, cs].astype(BF16), NN)
                    dqt = _dot(s_n.astype(BF16), do_c, NN)
                    dbl_c[c] = jnp.sum(ds_h * s_n, axis=1, keepdims=True) * ebl + jnp.sum(dkt * kt, axis=1, keepdims=True)
                    dq_c[c], dk_c[c] = dqt * eb, dkt * ek
                    ds_h = ebl * ds_h + _dot(qt.astype(BF16), do_c, NT)
                new_ds.append(ds_h)

                att0 = jnp.sum(qh * kh, axis=0, keepdims=True)
                datt0 = jnp.sum(doh * vh, axis=0, keepdims=True)
                dqh = jnp.concatenate(dq_c, axis=1) + datt0 * kh
                dkh = jnp.concatenate(dk_c, axis=1) + datt0 * qh
                dvh = jnp.concatenate(dv_c, axis=1) + att0 * doh
                for dlt in range(1, CHUNK):
                    kr, br, vr = pltpu.roll(kh, dlt, 1), pltpu.roll(bh, dlt, 1), pltpu.roll(vh, dlt, 1)
                    e = jnp.where(lane64 >= dlt, jnp.exp(jnp.minimum(bh - br, 0.0)), 0.0)
                    qe = qh * e
                    att = jnp.sum(qe * kr, axis=0, keepdims=True)
                    datt = jnp.sum(doh * vr, axis=0, keepdims=True)
                    dqh = dqh + datt * (kr * e)
                    dkh = dkh + pltpu.roll(datt * qe, LANES - dlt, 1)
                    dvh = dvh + pltpu.roll(att * doh, LANES - dlt, 1)
                dbl = jnp.where(half, dbl_c[0], dbl_c[1])
                db_h.append(qh * dqh - kh * dkh + jnp.where(lane64 == CHUNK - 1, dbl, 0.0))
                dq_h.append(dqh)
                dk_h.append(dkh)
                dv_h.append(dvh)
            dqq = jnp.concatenate(dq_h, axis=0).T
            dkk = jnp.concatenate(dk_h, axis=0).T
            dvv = jnp.concatenate(dv_h, axis=0).T
            dlf = _dot(jnp.concatenate(db_h, axis=0), umat, NT, precision=HI).T
            dqa = dqq * _dsilu(qa)
            dfg = jnp.where(fg > TINY, dlf / fg, 0.0)
            dz = (dfg - dkk) * (1.0 - lb) * sg * sgn
            dlb_acc = dlb_acc + jnp.sum(dfg * (1.0 - sg) - dkk * sgn, axis=0, keepdims=True)
            da_ref[pl.ds(r0, LANES), :] = jnp.concatenate([dqa, dz, dvv, dga], axis=1)
            return new_ds[0], new_ds[1], dgn_acc, dlb_acc

        zrow = jnp.zeros((1, LANES), F32)
        _, _, dgn_acc, dlb_acc = lax.fori_loop(0, nt, bwd_tile, (zero, zero, zrow, zrow))
        dgn_ref[...] = jnp.broadcast_to(dgn_acc, (8, LANES))
        dlb_ref[...] = jnp.broadcast_to(dlb_acc, (8, LANES))

    rows = jax.ShapeDtypeStruct((bsz, 8, HGRN_W), F32)
    return pl.pallas_call(
        body, grid=(bsz, 2),
        in_specs=[pl.BlockSpec((None, t, 512), lambda b, p: (b, 0, p)),
                  pl.BlockSpec((None, t, 128), lambda b, p: (b, 0, p)),
                  pl.BlockSpec((None, t, 128), lambda b, p: (b, 0, p)),
                  pl.BlockSpec((1, 128), lambda b, p: (0, p)),
                  pl.BlockSpec((1, 128), lambda b, p: (0, p))],
        out_specs=[pl.BlockSpec((None, t, 512), lambda b, p: (b, 0, p)),
                   pl.BlockSpec((None, 8, 128), lambda b, p: (b, 0, p)),
                   pl.BlockSpec((None, 8, 128), lambda b, p: (b, 0, p))],
        out_shape=[jax.ShapeDtypeStruct((bsz, t, A_W), F32), rows, rows],
        scratch_shapes=[pltpu.VMEM((2, nchunk, CHUNK, CHUNK), F32)],
        compiler_params=_cparams(("parallel", "parallel")), name=name)(proj3, o_raw, dmixed, lbs_row, gn_row)


N_LEVELS = 6


def _hgrn_tables():
    t = np.arange(LANES)
    j = np.arange(LANES)[None, :]
    same_chunk = (t[:, None] // CHUNK) == (j // CHUNK)
    w = np.zeros((2 + N_LEVELS, LANES, LANES), np.float32)
    w[0] = same_chunk & (j <= t[:, None])
    w[1] = same_chunk & (j > t[:, None])
    maskf = np.zeros((N_LEVELS, LANES, LANES), np.float32)
    rightf = np.zeros((N_LEVELS, LANES, LANES), np.float32)
    for li in range(N_LEVELS):
        m = (CHUNK // 2) >> li
        start = t - (t % (2 * m))
        right = (t % (2 * m)) >= m
        first = np.where(right, start + m, t + 1)
        last = np.where(right, t, start + m - 1)
        w[2 + li] = (j >= first[:, None]) & (j <= last[:, None])
        maskf[li] = (t[:, None] // (2 * m)) == (j // (2 * m))
        rightf[li] = right[:, None]
    return jnp.asarray(w.reshape(-1, LANES), BF16), jnp.asarray(maskf), jnp.asarray(rightf)


def _split(x, n):
    parts = []
    for _ in range(n - 1):
        p = x.astype(BF16)
        parts.append(p)
        x = x - p.astype(F32)
    parts.append(x.astype(BF16))
    return parts


def _exact_dot(w, parts):
    acc = jnp.dot(w, parts[0], preferred_element_type=F32)
    for p in parts[1:]:
        acc = acc + jnp.dot(w, p, preferred_element_type=F32)
    return acc


def _head_sums(v, ones_blk, n=2):
    parts = _split(v, n)
    acc = jnp.dot(parts[0], ones_blk, preferred_element_type=F32)
    for p in parts[1:]:
        acc = acc + jnp.dot(p, ones_blk, preferred_element_type=F32)
    return acc


def _hgrn_consts():
    r, c = _iota((LANES, LANES), 0), _iota((LANES, LANES), 1)
    eye = r == c
    ones_blk = ((r // CHUNK) == (c // CHUNK)).astype(BF16)
    return eye, ones_blk, jnp.ones((CHUNK, LANES), BF16)


def _hgrn_levels(qq, kk, zall, mk_ref, rt_ref, d_att=None):
    att = [jnp.zeros((LANES, LANES), F32)] * 2
    dq = dk = db = jnp.zeros((LANES, LANES), F32)
    for li in range(N_LEVELS):
        e = jnp.exp(zall[(2 + li) * LANES:(3 + li) * LANES])
        rt = rt_ref[li]
        mk = mk_ref[li]
        qef, kef = e * rt, e * (1.0 - rt)
        qe, ke = (qq * qef).astype(BF16), (kk * kef).astype(BF16)
        dqs, dks = [], []
        for h in range(2):
            hs = slice(CHUNK * h, CHUNK * (h + 1))
            att[h] = att[h] + _dot(qe[:, hs], ke[:, hs], NT) * mk
            if d_att is not None:
                dam = (d_att[h] * mk).astype(BF16)
                dqs.append(jnp.dot(dam, ke[:, hs], preferred_element_type=F32))
                dks.append(_dot(dam, qe[:, hs], TN))
        if d_att is not None:
            dqe, dke = jnp.concatenate(dqs, axis=1), jnp.concatenate(dks, axis=1)
            dq = dq + dqe * qef
            dk = dk + dke * kef
            db = db + (dqe * qe.astype(F32) - dke * ke.astype(F32))
    return att, dq, dk, db


def _hgrn_fwd(proj3, lbs_row, gn_row, name):
    bsz, t, _ = proj3.shape
    nt = t // LANES
    w_all, maskf, rightf = _hgrn_tables()

    def body(a_ref, lb_ref, gn_ref, w_ref, mk_ref, rt_ref, og_ref, or_ref, st_ref):
        lb = lb_ref[...]
        gn = gn_ref[...]
        eye, ones_blk, ones_h = _hgrn_consts()

        def tile(i, carry):
            r0 = pl.multiple_of(i * LANES, LANES)
            a = a_ref[pl.ds(r0, LANES), :]
            qq, kk, lf, _, _, _ = _hgrn_gates(a, lb)
            va, ga = a[:, 256:384], a[:, 384:512]
            parts = _split(lf, 3)
            zall = _exact_dot(w_ref[...], parts)
            eb, ee = jnp.exp(zall[0:LANES]), jnp.exp(zall[LANES:2 * LANES])
            vb = va.astype(BF16)
            att, _, _, _ = _hgrn_levels(qq, kk, zall, mk_ref, rt_ref)
            qk = _split(qq * kk, 2)
            qeb, keb = (qq * eb).astype(BF16), (kk * ee).astype(BF16)
            new_s, o_heads = [], []
            for h in range(2):
                hs = slice(CHUNK * h, CHUNK * (h + 1))
                diag = _exact_dot_r(qk, hs, ones_h)
                a_h = att[h] + jnp.where(eye, diag, 0.0)
                o_h = jnp.dot(a_h.astype(BF16), vb[:, hs], preferred_element_type=F32)
                st = carry[h]
                chunks = []
                for c in range(2):
                    rc = slice(CHUNK * c, CHUNK * (c + 1))
                    st_ref[h, 2 * i + c] = st
                    chunks.append(o_h[rc] + _dot(qeb[rc, hs], st.astype(BF16), NT))
                    ebl = eb[CHUNK * (c + 1) - 1:CHUNK * (c + 1), hs]
                    st = st * ebl + _dot(vb[rc, hs], keb[rc, hs], TN)
                new_s.append(st)
                o_heads.append(jnp.concatenate(chunks, axis=0))
            o = jnp.concatenate(o_heads, axis=1)
            ms = _head_sums(o * o, ones_blk) * (1.0 / CHUNK)
            or_ref[pl.ds(r0, LANES), :] = o
            og_ref[pl.ds(r0, LANES), :] = o * lax.rsqrt(ms + NORM_EPS) * gn * _silu(ga)
            return tuple(new_s)

        zero = jnp.zeros((CHUNK, CHUNK), F32)
        lax.fori_loop(0, nt // 2, lambda i, carry: tile(2 * i + 1, tile(2 * i, carry)), (zero, zero))

    out = jax.ShapeDtypeStruct((bsz, t, HGRN_W), F32)
    row = pl.BlockSpec((1, 128), lambda b, p: (0, p))
    return pl.pallas_call(
        body, grid=(bsz, 2),
        in_specs=[pl.BlockSpec((None, t, 512), lambda b, p: (b, 0, p)), row, row,
                  pl.BlockSpec(w_all.shape, lambda b, p: (0, 0)),
                  pl.BlockSpec(maskf.shape, lambda b, p: (0, 0, 0)),
                  pl.BlockSpec(rightf.shape, lambda b, p: (0, 0, 0))],
        out_specs=[pl.BlockSpec((None, t, 128), lambda b, p: (b, 0, p)),
                   pl.BlockSpec((None, t, 128), lambda b, p: (b, 0, p)),
                   pl.BlockSpec((None, 2, t // CHUNK, CHUNK, CHUNK), lambda b, p: (b, p, 0, 0, 0))],
        out_shape=[out, out, jax.ShapeDtypeStruct((bsz, 4, t // CHUNK, CHUNK, CHUNK), F32)],
        compiler_params=_cparams(("parallel", "parallel")), name=name)(proj3, lbs_row, gn_row, w_all, maskf, rightf)


def _exact_dot_r(parts, hs, ones_h):
    acc = jnp.dot(parts[0][:, hs], ones_h, preferred_element_type=F32)
    for p in parts[1:]:
        acc = acc + jnp.dot(p[:, hs], ones_h, preferred_element_type=F32)
    return acc


def _hgrn_bwd(proj3, o_raw, dmixed, states, lbs_row, gn_row, name):
    bsz, t, _ = proj3.shape
    nt = t // LANES
    nchunk = t // CHUNK
    w_all, maskf, rightf = _hgrn_tables()

    def body(a_ref, or_ref, do_ref, s_sc, lb_ref, gn_ref, w_ref, mk_ref, rt_ref, da_ref, dgn_ref, dlb_ref):
        lb = lb_ref[...]
        gn = gn_ref[...]
        eye, ones_blk, ones_h = _hgrn_consts()
        r_i, c_i = _iota((LANES, LANES), 0), _iota((LANES, LANES), 1)
        suffix = ((c_i >= r_i) & ((r_i // CHUNK) == (c_i // CHUNK))).astype(BF16)
        row64 = _iota((LANES, CHUNK), 0)
        ones_t = jnp.ones((LANES, CHUNK), BF16)
        zero = jnp.zeros((CHUNK, CHUNK), F32)

        def bwd_tile(k, carry):
            dst0, dst1, dgn_acc, dlb_acc = carry
            i = nt - 1 - k
            r0 = pl.multiple_of(i * LANES, LANES)
            a = a_ref[pl.ds(r0, LANES), :]
            qa, ga = a[:, 0:128], a[:, 384:512]
            qq, kk, lf, sg, sgn, fg = _hgrn_gates(a, lb)
            parts = _split(lf, 3)
            zall = _exact_dot(w_ref[...], parts)
            eb, ee = jnp.exp(zall[0:LANES]), jnp.exp(zall[LANES:2 * LANES])
            vb = a[:, 256:384].astype(BF16)
            oraw = or_ref[pl.ds(r0, LANES), :]
            dout = do_ref[pl.ds(r0, LANES), :]
            r = lax.rsqrt(_head_sums(oraw * oraw, ones_blk) * (1.0 / CHUNK) + NORM_EPS)
            xn = oraw * r
            dga = dout * (xn * gn) * _dsilu(ga)
            don = dout * _silu(ga)
            dgn_acc = dgn_acc + jnp.sum(don * xn, axis=0, keepdims=True)
            dxn = don * gn
            do = r * (dxn - xn * (_head_sums(dxn * xn, ones_blk) * (1.0 / CHUNK)))
            dob = do.astype(BF16)
            d_att = [_dot(dob[:, CHUNK * h:CHUNK * (h + 1)], vb[:, CHUNK * h:CHUNK * (h + 1)], NT) for h in range(2)]
            att, dq, dk, db_lv = _hgrn_levels(qq, kk, zall, mk_ref, rt_ref, d_att)
            qk = _split(qq * kk, 2)
            qe_f, ke_f = qq * eb, kk * ee
            qeb, keb = qe_f.astype(BF16), ke_f.astype(BF16)
            new_ds, dq_h, dk_h, dv_h, dbl_h = [], [], [], [], []
            for h in range(2):
                hs = slice(CHUNK * h, CHUNK * (h + 1))
                a_h = att[h] + jnp.where(eye, _exact_dot_r(qk, hs, ones_h), 0.0)
                dv = _dot(a_h.astype(BF16), dob[:, hs], TN)
                ddiag = _exact_dot_r(_split(jnp.where(eye, d_att[h], 0.0), 2), slice(None), ones_t)
                dq_i = dq[:, hs] + ddiag * kk[:, hs]
                dk_i = dk[:, hs] + ddiag * qq[:, hs]
                dst = (dst0, dst1)[h]
                dq_c, dk_c, dv_c, dbl_c = [None, None], [None, None], [None, None], [None, None]
                for c in (1, 0):
                    rc = slice(CHUNK * c, CHUNK * (c + 1))
                    st_n = s_sc[h, 2 * i + c]
                    ebl = eb[CHUNK * (c + 1) - 1:CHUNK * (c + 1), hs]
                    dstb = dst.astype(BF16)
                    dv_c[c] = _dot(keb[rc, hs], dstb, NT)
                    dke = jnp.dot(vb[rc, hs], dstb, preferred_element_type=F32)
                    dqe = jnp.dot(dob[rc, hs], st_n.astype(BF16), preferred_element_type=F32)
                    dbl_c[c] = (jnp.sum(dst * st_n, axis=0, keepdims=True) * ebl
                                + jnp.sum(dke * ke_f[rc, hs], axis=0, keepdims=True))
                    dq_c[c], dk_c[c] = dqe * eb[rc, hs], dke * ee[rc, hs]
                    dst = dst * ebl + _dot(dob[rc, hs], qeb[rc, hs], TN)
                new_ds.append(dst)
                dq_x, dk_x = jnp.concatenate(dq_c, axis=0), jnp.concatenate(dk_c, axis=0)
                dq_h.append(dq_i + dq_x)
                dk_h.append(dk_i + dk_x)
                dv_h.append(dv + jnp.concatenate(dv_c, axis=0))
                dbl_h.append(qq[:, hs] * dq_x - kk[:, hs] * dk_x
                             + jnp.where(row64 == CHUNK - 1, dbl_c[0], 0.0) + jnp.where(row64 == LANES - 1, dbl_c[1], 0.0))
            dqq = jnp.concatenate(dq_h, axis=1)
            dkk = jnp.concatenate(dk_h, axis=1)
            dvv = jnp.concatenate(dv_h, axis=1)
            db = db_lv + jnp.concatenate(dbl_h, axis=1)
            dlf = _exact_dot(suffix, _split(db, 3))
            dqa = dqq * _dsilu(qa)
            dfg = jnp.where(fg > TINY, dlf / fg, 0.0)
            dz = (dfg - dkk) * (1.0 - lb) * sg * sgn
            dlb_acc = dlb_acc + jnp.sum(dfg * (1.0 - sg) - dkk * sgn, axis=0, keepdims=True)
            da_ref[pl.ds(r0, LANES), :] = jnp.concatenate([dqa, dz, dvv, dga], axis=1)
            return new_ds[0], new_ds[1], dgn_acc, dlb_acc

        zrow = jnp.zeros((1, LANES), F32)
        _, _, dgn_acc, dlb_acc = lax.fori_loop(
            0, nt // 2, lambda k, carry: bwd_tile(2 * k + 1, bwd_tile(2 * k, carry)), (zero, zero, zrow, zrow))
        dgn_ref[...] = jnp.broadcast_to(dgn_acc, (8, LANES))
        dlb_ref[...] = jnp.broadcast_to(dlb_acc, (8, LANES))

    rows = jax.ShapeDtypeStruct((bsz, 8, HGRN_W), F32)
    row = pl.BlockSpec((1, 128), lambda b, p: (0, p))
    blk = pl.BlockSpec((None, t, 128), lambda b, p: (b, 0, p))
    return pl.pallas_call(
        body, grid=(bsz, 2),
        in_specs=[pl.BlockSpec((None, t, 512), lambda b, p: (b, 0, p)), blk, blk,
                  pl.BlockSpec((None, 2, nchunk, CHUNK, CHUNK), lambda b, p: (b, p, 0, 0, 0)), row, row,
                  pl.BlockSpec(w_all.shape, lambda b, p: (0, 0)),
                  pl.BlockSpec(maskf.shape, lambda b, p: (0, 0, 0)),
                  pl.BlockSpec(rightf.shape, lambda b, p: (0, 0, 0))],
        out_specs=[pl.BlockSpec((None, t, 512), lambda b, p: (b, 0, p)),
                   pl.BlockSpec((None, 8, 128), lambda b, p: (b, 0, p)),
                   pl.BlockSpec((None, 8, 128), lambda b, p: (b, 0, p))],
        out_shape=[jax.ShapeDtypeStruct((bsz, t, A_W), F32), rows, rows],
        compiler_params=_cparams(("parallel", "parallel")), name=name)(
            proj3, o_raw, dmixed, states, lbs_row, gn_row, w_all, maskf, rightf)


def _pool_tt(t):
    return min(256, t)


def _window_select(s2, s4, s8, s16, lane):
    return jnp.where(lane < 64, s2, jnp.where(lane < 128, s4, jnp.where(lane < 192, s8, s16)))


def _pool_counts(t0, tt):
    lane = _iota((tt, POOL_W), 1)
    tpos = (_iota((tt, POOL_W), 0) + t0 + 1).astype(F32)
    win = jnp.where(lane < 64, 2.0, jnp.where(lane < 128, 4.0, jnp.where(lane < 192, 8.0, 16.0)))
    return 1.0 / jnp.minimum(tpos, win), lane


def _pooled_tile(upad_ref, i, tt):
    r0 = pl.multiple_of(i * tt, 8)
    cat = upad_ref[pl.ds(r0, tt + POOL_HALO), :]
    s2 = cat + pltpu.roll(cat, 1, 0)
    s4 = s2 + pltpu.roll(s2, 2, 0)
    s8 = s4 + pltpu.roll(s4, 4, 0)
    s16 = s8 + pltpu.roll(s8, 8, 0)
    inv, lane = _pool_counts(i * tt, tt)
    sel = _window_select(s2[POOL_HALO:], s4[POOL_HALO:], s8[POOL_HALO:], s16[POOL_HALO:], lane)
    return sel * inv - cat[POOL_HALO:], inv, lane


def _pool_fwd(proj3, wbd, scale_row, name):
    bsz, t, _ = proj3.shape
    tt = _pool_tt(t)

    def body(p_ref, w_ref, sc_ref, o_ref, upad):
        upad[0:POOL_HALO, :] = jnp.zeros((POOL_HALO, POOL_W), F32)
        upad[POOL_HALO:, :] = p_ref[:, 0:POOL_W]
        w = w_ref[...]
        sc = sc_ref[...]

        def tile(i, c):
            pooled, _, _ = _pooled_tile(upad, i, tt)
            r0 = pl.multiple_of(i * tt, 8)
            g = p_ref[pl.ds(r0, tt), POOL_W:2 * POOL_W]
            pre = jnp.dot(pooled.astype(BF16), w, preferred_element_type=F32)
            o_ref[pl.ds(r0, tt), :] = pre * sc * _silu(g)
            return c

        lax.fori_loop(0, t // tt, tile, 0)

    return pl.pallas_call(
        body, grid=(bsz,),
        in_specs=[pl.BlockSpec((None, t, 512), lambda b: (b, 0, B_BLK)),
                  pl.BlockSpec((POOL_W, POOL_W), lambda b: (0, 0)),
                  pl.BlockSpec((1, POOL_W), lambda b: (0, 0))],
        out_specs=pl.BlockSpec((None, t, POOL_W), lambda b: (b, 0, 0)),
        out_shape=jax.ShapeDtypeStruct((bsz, t, POOL_W), F32),
        scratch_shapes=[pltpu.VMEM((t + POOL_HALO, POOL_W), F32)],
        compiler_params=_cparams(("parallel",)), name=name)(proj3, wbd, scale_row)


def _pool_bwd(proj3, dmixed, wbd, scale_row, name):
    bsz, t, _ = proj3.shape
    tt = _pool_tt(t)

    def body(p_ref, do_ref, w_ref, sc_ref, db_ref, dsc_ref, dw_ref, upad, epad):
        upad[0:POOL_HALO, :] = jnp.zeros((POOL_HALO, POOL_W), F32)
        upad[POOL_HALO:, :] = p_ref[:, 0:POOL_W]
        epad[t:, :] = jnp.zeros((POOL_HALO, POOL_W), F32)
        w = w_ref[...]
        sc = sc_ref[...]

        def tile(i, carry):
            dsc_acc, dw_acc = carry
            pooled, inv, _ = _pooled_tile(upad, i, tt)
            r0 = pl.multiple_of(i * tt, 8)
            g = p_ref[pl.ds(r0, tt), POOL_W:2 * POOL_W]
            dout = do_ref[pl.ds(r0, tt), :]
            pb = pooled.astype(BF16)
            pre = jnp.dot(pb, w, preferred_element_type=F32)
            t1 = dout * _silu(g)
            dsc_acc = dsc_acc + jnp.sum(t1 * pre, axis=0, keepdims=True)
            dpre = (t1 * sc).astype(BF16)
            db_ref[pl.ds(r0, tt), POOL_W:2 * POOL_W] = dout * pre * sc * _dsilu(g)
            dw_acc = dw_acc + _dot(pb, dpre, TN)
            dpooled = _dot(dpre, w, NT)
            epad[pl.ds(r0, tt), :] = dpooled * inv
            return dsc_acc, dw_acc

        dsc_acc, dw_acc = lax.fori_loop(0, t // tt, tile, (jnp.zeros((1, POOL_W), F32), jnp.zeros((POOL_W, POOL_W), F32)))
        dsc_ref[...] = jnp.broadcast_to(dsc_acc, (8, POOL_W))
        dw_ref[...] = dw_acc

        def tile2(i, c):
            r0 = pl.multiple_of(i * tt, 8)
            n = tt + POOL_HALO
            cat = epad[pl.ds(r0, n), :]
            s2 = cat + pltpu.roll(cat, n - 1, 0)
            s4 = s2 + pltpu.roll(s2, n - 2, 0)
            s8 = s4 + pltpu.roll(s4, n - 4, 0)
            s16 = s8 + pltpu.roll(s8, n - 8, 0)
            inv, lane = _pool_counts(i * tt, tt)
            sel = _window_select(s2[:tt], s4[:tt], s8[:tt], s16[:tt], lane)
            db_ref[pl.ds(r0, tt), 0:POOL_W] = sel - cat[:tt] / inv
            return c

        lax.fori_loop(0, t // tt, tile2, 0)

    return pl.pallas_call(
        body, grid=(bsz,),
        in_specs=[pl.BlockSpec((None, t, 512), lambda b: (b, 0, B_BLK)),
                  pl.BlockSpec((None, t, POOL_W), lambda b: (b, 0, 1)),
                  pl.BlockSpec((POOL_W, POOL_W), lambda b: (0, 0)),
                  pl.BlockSpec((1, POOL_W), lambda b: (0, 0))],
        out_specs=[pl.BlockSpec((None, t, 512), lambda b: (b, 0, 0)),
                   pl.BlockSpec((None, 8, POOL_W), lambda b: (b, 0, 0)),
                   pl.BlockSpec((None, POOL_W, POOL_W), lambda b: (b, 0, 0))],
        out_shape=[jax.ShapeDtypeStruct((bsz, t, B_W), F32), jax.ShapeDtypeStruct((bsz, 8, POOL_W), F32),
                   jax.ShapeDtypeStruct((bsz, POOL_W, POOL_W), F32)],
        scratch_shapes=[pltpu.VMEM((t + POOL_HALO, POOL_W), F32), pltpu.VMEM((t + POOL_HALO, POOL_W), F32)],
        compiler_params=_cparams(("parallel",)), name=name)(proj3, dmixed, wbd, scale_row)


def _head_select_rows(hp):
    r, c = _iota((8, LANES), 0), _iota((8, LANES), 1)
    return ((r < 2) & (c == 2 * hp + r)).astype(F32)


def _foxgate_fwd(proj3, bias_row, name):
    bsz, t, _ = proj3.shape
    nt = t // LANES

    def body(f_ref, b_ref, cn_ref, ct_ref):
        bias = b_ref[...]
        i, j = _iota((LANES, LANES), 0), _iota((LANES, LANES), 1)
        lower = (j <= i).astype(BF16)
        spread = (_iota((LANES, FOX_W), 0) == _iota((LANES, FOX_W), 1) // 64).astype(BF16)
        select = [_head_select_rows(hp).astype(BF16) for hp in range(4)]
        offset = jnp.zeros((1, LANES), F32)
        for k in range(nt):
            rows = slice(k * LANES, (k + 1) * LANES)
            xg = f_ref[rows, :] + bias
            lf = jnp.minimum(xg, 0.0) - jnp.log(1.0 + jnp.exp(-jnp.abs(xg)))
            c = _exact_dot(lower, _split(lf, 3)) + offset
            offset = c[LANES - 1:LANES, :]
            parts = _split(c, 3)
            cn_ref[rows, :] = _head_sums(c, spread, 3)
            for hp in range(4):
                acc = _dot(select[hp], parts[0], NT)
                for p in parts[1:]:
                    acc = acc + _dot(select[hp], p, NT)
                ct_ref[hp, :, rows] = acc

    return pl.pallas_call(
        body, grid=(bsz,),
        in_specs=[pl.BlockSpec((None, t, 128), lambda b: (b, 0, F_BLK)), pl.BlockSpec((1, 128), lambda b: (0, 0))],
        out_specs=[pl.BlockSpec((None, t, FOX_W), lambda b: (b, 0, 0)),
                   pl.BlockSpec((None, 4, 8, t), lambda b: (b, 0, 0, 0))],
        out_shape=[jax.ShapeDtypeStruct((bsz, t, FOX_W), F32), jax.ShapeDtypeStruct((bsz, 4, 8, t), F32)],
        compiler_params=_cparams(("parallel",)), name=name)(proj3, bias_row)


def _foxgate_bwd(proj3, dc_nat, bias_row, name):
    bsz, t, _ = proj3.shape
    nt = t // LANES

    def body(f_ref, dc_ref, b_ref, df_ref, dbias_ref, run_sc):
        bias = b_ref[...]
        i, j = _iota((LANES, LANES), 0), _iota((LANES, LANES), 1)
        upper = (j >= i).astype(F32)
        valid = _iota((1, LANES), 1) < FOX_HEADS
        run_sc[...] = jnp.zeros((8, LANES), F32)
        dbias_ref[...] = jnp.zeros((8, LANES), F32)

        def tile(k, c):
            r0 = pl.multiple_of((nt - 1 - k) * LANES, LANES)
            dc = dc_ref[pl.ds(r0, LANES), :] + jnp.where(i == LANES - 1, run_sc[0:1, :], 0.0)
            dlf = jnp.dot(upper, dc, precision=HI, preferred_element_type=F32)
            xg = f_ref[pl.ds(r0, LANES), :] + bias
            df = jnp.where(valid, dlf * _sig(-xg), 0.0)
            df_ref[pl.ds(r0, LANES), :] = df
            run_sc[...] = dlf[0:8, :]
            dbias_ref[...] += jnp.sum(df, axis=0, keepdims=True)
            return c

        lax.fori_loop(0, nt, tile, 0)

    blk = pl.BlockSpec((None, t, 128), lambda b: (b, 0, 0))
    return pl.pallas_call(
        body, grid=(bsz,),
        in_specs=[pl.BlockSpec((None, t, 128), lambda b: (b, 0, F_BLK)), blk, pl.BlockSpec((1, 128), lambda b: (0, 0))],
        out_specs=[blk, pl.BlockSpec((None, 8, 128), lambda b: (b, 0, 0))],
        out_shape=[jax.ShapeDtypeStruct((bsz, t, F_W), F32), jax.ShapeDtypeStruct((bsz, 8, 128), F32)],
        scratch_shapes=[pltpu.VMEM((8, LANES), F32)],
        compiler_params=_cparams(("parallel",)), name=name)(proj3, dc_nat, bias_row)


def _fox_tile(t):
    return min(256, t)


def _fox_fwd(proj3, c_nat, c_t, name):
    bsz, t, _ = proj3.shape
    tq = tk = min(2 * _fox_tile(t), t)
    nq = t // tq

    def body(q_ref, kv_ref, cn_ref, ct_ref, og_ref, or_ref, lse_ref):
        i = pl.program_id(2)
        qblk = q_ref[...]
        first = _iota((1, 128), 1) < 64
        qv = qblk[:, 0:128] * 0.125
        qm = [jnp.where(first, qv, 0.0).astype(BF16), jnp.where(first, 0.0, qv).astype(BF16)]
        cqs = [cn_ref[:, 0:1], cn_ref[:, 64:65]]
        rows = _iota((tq, tk), 0) + i * tq

        def scores(j):
            c0 = pl.multiple_of(j * tk, tk)
            kb = kv_ref[pl.ds(c0, tk), 128:256].astype(BF16)
            return tuple(_dot(qm[h], kb, NT) + (cqs[h] - ct_ref[h:h + 1, pl.ds(c0, tk)]) for h in range(2))

        def absorb(j, state, s01, masked):
            c0 = pl.multiple_of(j * tk, tk)
            vblk = kv_ref[pl.ds(c0, tk), 256:384]
            vx = [jnp.where(first, vblk, 1.0).astype(BF16), jnp.where(first, 1.0, vblk).astype(BF16)]
            new = []
            for h in range(2):
                m, acc, s = state[2 * h], state[2 * h + 1], s01[h]
                if masked:
                    s = jnp.where(rows >= _iota((tq, tk), 1) + j * tk, s, MASK_VALUE)
                m_new = jnp.maximum(m, jnp.max(s, axis=1, keepdims=True))
                p = jnp.exp(s - m_new).astype(BF16)
                new += [m_new, jnp.exp(m - m_new) * acc + jnp.dot(p, vx[h], preferred_element_type=F32)]
            return tuple(new)

        def kv_step(j, carry):
            ahead = scores(j + 1)
            return absorb(j, carry[:4], carry[4:], False) + ahead

        init = (jnp.full((tq, 1), MASK_VALUE, F32), jnp.zeros((tq, 128), F32)) * 2
        n_full = (i * tq) // tk
        carry = lax.fori_loop(0, n_full, kv_step, init + scores(0))
        m0, acc0, m1, acc1 = absorb(n_full, carry[:4], carry[4:], True)
        l0, l1 = pltpu.roll(acc0, 64, 1), pltpu.roll(acc1, 64, 1)
        o = jnp.where(first, acc0 / l0, acc1 / l1)
        or_ref[...] = o
        og_ref[...] = o * _silu(qblk[:, 384:512])
        lse_ref[...] = jnp.where(first, m0 + jnp.log(l0), m1 + jnp.log(l1))

    out = jax.ShapeDtypeStruct((bsz, t, FOX_W), F32)
    blk = pl.BlockSpec((None, tq, 128), lambda b, p, i: (b, i, p))
    return pl.pallas_call(
        body, grid=(bsz, 4, nq),
        in_specs=[pl.BlockSpec((None, tq, 512), lambda b, p, i: (b, i, C_BLK0 + p)),
                  pl.BlockSpec((None, t, 512), lambda b, p, i: (b, 0, C_BLK0 + p)),
                  blk,
                  pl.BlockSpec((None, None, 8, t), lambda b, p, i: (b, p, 0, 0))],
        out_specs=[blk, blk, blk],
        out_shape=[out, out, out],
        compiler_params=_cparams(("parallel", "parallel", "arbitrary")), name=name)(proj3, proj3, c_nat, c_t)


def _fox_bwd(proj3, o_raw, dmixed, lse, c_nat, c_t, name):
    bsz, t, _ = proj3.shape
    tq = tk = min(2 * _fox_tile(t), t)
    nq = t // tq
    ratio = tk // tq

    def body(a_ref, or_ref, do_ref, lse_ref, cn_ref, ct_ref, dc_out, dct_out, drow_out, dq_sc, do_sc, dl_sc):
        def prep(i, c):
            r0 = pl.multiple_of(i * tq, tq)
            g = a_ref[pl.ds(r0, tq), 384:512]
            dout = do_ref[pl.ds(r0, tq), :]
            o = or_ref[pl.ds(r0, tq), :]
            dc_out[pl.ds(r0, tq), 384:512] = dout * o * _dsilu(g)
            do = dout * _silu(g)
            do_sc[pl.ds(r0, tq), :] = do
            prod = do * o
            d0 = jnp.sum(prod[:, 0:64], axis=1, keepdims=True)
            d1 = jnp.sum(prod[:, 64:128], axis=1, keepdims=True)
            dl_sc[pl.ds(r0, tq), :] = jnp.concatenate([jnp.broadcast_to(d0, (tq, 64)), jnp.broadcast_to(d1, (tq, 64))], axis=1)
            dq_sc[pl.ds(r0, tq), :] = jnp.zeros((tq, 128), F32)
            drow_out[pl.ds(r0, tq), :] = jnp.zeros((tq, 128), F32)
            return c

        lax.fori_loop(0, nq, prep, 0)
        dct_out[...] = jnp.zeros((8, t), F32)

        first = _iota((1, 128), 1) < 64

        def heads(v):
            return [jnp.where(first, v, 0.0).astype(BF16), jnp.where(first, 0.0, v).astype(BF16)]

        def kv_tile(j, c):
            c0 = pl.multiple_of(j * tk, tk)
            kb = a_ref[pl.ds(c0, tk), 128:256].astype(BF16)
            vb = a_ref[pl.ds(c0, tk), 256:384].astype(BF16)
            cks = [ct_ref[h:h + 1, pl.ds(c0, tk)] for h in range(2)]

            def q_step(i, carry, diagonal):
                dk, dv, dcol0, dcol1 = carry
                r0 = pl.multiple_of(i * tq, tq)
                causal = _iota((tq, tk), 0) + i * tq >= _iota((tq, tk), 1) + j * tk
                qv = a_ref[pl.ds(r0, tq), 0:128] * 0.125
                do = do_sc[pl.ds(r0, tq), :]
                qb, dob = qv.astype(BF16), do.astype(BF16)
                qm, dom = heads(qv), heads(do)
                full, dcols, rsums = [], [], []
                for h in range(2):
                    lse_h = lse_ref[pl.ds(r0, tq), 64 * h:64 * h + 1]
                    dl_h = dl_sc[pl.ds(r0, tq), 64 * h:64 * h + 1]
                    cq = cn_ref[pl.ds(r0, tq), 64 * h:64 * h + 1]
                    p = jnp.exp(_dot(qm[h], kb, NT) + (cq - cks[h]) - lse_h)
                    if diagonal:
                        p = jnp.where(causal, p, 0.0)
                    ds = p * (_dot(dom[h], vb, NT) - dl_h)
                    dsb = ds.astype(BF16)
                    full.append((_dot(p.astype(BF16), dob, TN), _dot(dsb, qb, TN),
                                 jnp.dot(dsb, kb, preferred_element_type=F32)))
                    dcols.append(jnp.sum(ds, axis=0, keepdims=True))
                    rsums.append(jnp.broadcast_to(jnp.sum(ds, axis=1, keepdims=True), (tq, 128)))
                dq_sc[pl.ds(r0, tq), :] += jnp.where(first, full[0][2], full[1][2]) * 0.125
                drow_out[pl.ds(r0, tq), :] += jnp.where(first, rsums[0], rsums[1])
                return (dk + jnp.where(first, full[0][1], full[1][1]), dv + jnp.where(first, full[0][0], full[1][0]),
                        dcol0 - dcols[0], dcol1 - dcols[1])

            carry = (jnp.zeros((tk, 128), F32), jnp.zeros((tk, 128), F32), jnp.zeros((1, tk), F32), jnp.zeros((1, tk), F32))
            for r in range(ratio):
                carry = q_step(ratio * j + r, carry, True)
            dk, dv, dcol0, dcol1 = lax.fori_loop(ratio * (j + 1), nq, functools.partial(q_step, diagonal=False), carry)
            dct_out[0:1, pl.ds(c0, tk)] = dcol0
            dct_out[1:2, pl.ds(c0, tk)] = dcol1
            dc_out[pl.ds(c0, tk), 128:256] = dk
            dc_out[pl.ds(c0, tk), 256:384] = dv
            return c

        lax.fori_loop(0, t // tk, kv_tile, 0)
        dc_out[:, 0:128] = dq_sc[...]

    blk = pl.BlockSpec((None, t, 128), lambda b, p: (b, 0, p))
    return pl.pallas_call(
        body, grid=(bsz, 4),
        in_specs=[pl.BlockSpec((None, t, 512), lambda b, p: (b, 0, C_BLK0 + p)),
                  blk,
                  pl.BlockSpec((None, t, 128), lambda b, p: (b, 0, 4 + p)),
                  blk, blk,
                  pl.BlockSpec((None, None, 8, t), lambda b, p: (b, p, 0, 0))],
        out_specs=[pl.BlockSpec((None, t, 512), lambda b, p: (b, 0, p)),
                   pl.BlockSpec((None, None, 8, t), lambda b, p: (b, p, 0, 0)), blk],
        out_shape=[jax.ShapeDtypeStruct((bsz, t, C_W), F32), jax.ShapeDtypeStruct((bsz, 4, 8, t), F32),
                   jax.ShapeDtypeStruct((bsz, t, FOX_W), F32)],
        scratch_shapes=[pltpu.VMEM((t, 128), F32), pltpu.VMEM((t, 128), F32), pltpu.VMEM((t, 128), F32)],
        compiler_params=_cparams(("parallel", "parallel")), name=name)(proj3, o_raw, dmixed, lse, c_nat, c_t)


def _mix_tm(n):
    return min(512, n)


def _outproj_fwd(x2, oa, ob, oc, wo, g_row, name):
    n, d = x2.shape
    tm = _mix_tm(n)

    def body(x_ref, oa_ref, ob_ref, oc_ref, w_ref, g_ref, y_ref, xo_ref):
        y = (jnp.dot(oa_ref[...].astype(BF16), w_ref[0:256, :], preferred_element_type=F32)
             + jnp.dot(ob_ref[...].astype(BF16), w_ref[256:512, :], preferred_element_type=F32)
             + jnp.dot(oc_ref[...].astype(BF16), w_ref[512:1024, :], preferred_element_type=F32))
        y_ref[...] = y
        xo_ref[...] = x_ref[...] + y * _rstd(y) * g_ref[...]

    row = lambda w: pl.BlockSpec((tm, w), lambda i: (i, 0))
    out = jax.ShapeDtypeStruct((n, d), F32)
    return pl.pallas_call(
        body, grid=(n // tm,),
        in_specs=[row(d), row(256), row(256), row(512), pl.BlockSpec((d, d), lambda i: (0, 0)),
                  pl.BlockSpec((1, d), lambda i: (0, 0))],
        out_specs=[row(d), row(d)], out_shape=[out, out],
        compiler_params=_cparams(("parallel",)), name=name)(x2, oa, ob, oc, wo, g_row)


def _outproj_fwd_loss(x2, oa, ob, oc, wo, g_row, target2, name):
    n, d = x2.shape
    tm = _mix_tm(n)

    def body(x_ref, oa_ref, ob_ref, oc_ref, w_ref, g_ref, t_ref, y_ref, dx_ref, l_ref):
        y = (jnp.dot(oa_ref[...].astype(BF16), w_ref[0:256, :], preferred_element_type=F32)
             + jnp.dot(ob_ref[...].astype(BF16), w_ref[256:512, :], preferred_element_type=F32)
             + jnp.dot(oc_ref[...].astype(BF16), w_ref[512:1024, :], preferred_element_type=F32))
        y_ref[...] = y
        err = (x_ref[...] + y * _rstd(y) * g_ref[...]) - t_ref[...]
        dx_ref[...] = err * (1.0 / d)

        @pl.when(pl.program_id(0) == 0)
        def _():
            l_ref[...] = jnp.zeros((8, 128), F32)

        l_ref[...] += jnp.sum(err * err)

    row = lambda w: pl.BlockSpec((tm, w), lambda i: (i, 0))
    out = jax.ShapeDtypeStruct((n, d), F32)
    return pl.pallas_call(
        body, grid=(n // tm,),
        in_specs=[row(d), row(256), row(256), row(512), pl.BlockSpec((d, d), lambda i: (0, 0)),
                  pl.BlockSpec((1, d), lambda i: (0, 0)), row(d)],
        out_specs=[row(d), row(d), pl.BlockSpec((8, 128), lambda i: (0, 0))],
        out_shape=[out, out, jax.ShapeDtypeStruct((8, 128), F32)],
        compiler_params=_cparams(("arbitrary",)), name=name)(x2, oa, ob, oc, wo, g_row, target2)


def _outproj_bwd(dxo, y, oa, ob, oc, wo, g_row, name):
    n, d = dxo.shape
    tm = _mix_tm(n)

    def body(dx_ref, y_ref, oa_ref, ob_ref, oc_ref, w_ref, g_ref, dm_ref, dw_ref, dg_ref):
        @pl.when(pl.program_id(0) == 0)
        def _():
            dw_ref[...] = jnp.zeros((d, d), F32)
            dg_ref[...] = jnp.zeros((8, d), F32)

        yv, dx = y_ref[...], dx_ref[...]
        r = _rstd(yv)
        yn = yv * r
        dg_ref[...] += jnp.sum(dx * yn, axis=0, keepdims=True)
        dyn = dx * g_ref[...]
        dy = (r * (dyn - yn * jnp.mean(dyn * yn, axis=-1, keepdims=True))).astype(BF16)
        dm_ref[...] = _dot(dy, w_ref[...], NT)
        dw_ref[0:256, :] += _dot(oa_ref[...].astype(BF16), dy, TN)
        dw_ref[256:512, :] += _dot(ob_ref[...].astype(BF16), dy, TN)
        dw_ref[512:1024, :] += _dot(oc_ref[...].astype(BF16), dy, TN)

    row = lambda w: pl.BlockSpec((tm, w), lambda i: (i, 0))
    fixed = lambda r, c: pl.BlockSpec((r, c), lambda i: (0, 0))
    return pl.pallas_call(
        body, grid=(n // tm,),
        in_specs=[row(d), row(d), row(256), row(256), row(512), fixed(d, d), fixed(1, d)],
        out_specs=[row(d), fixed(d, d), fixed(8, d)],
        out_shape=[jax.ShapeDtypeStruct((n, d), F32), jax.ShapeDtypeStruct((d, d), F32), jax.ShapeDtypeStruct((8, d), F32)],
        compiler_params=_cparams(("arbitrary",)), name=name)(dxo, y, oa, ob, oc, wo, g_row)


_PIECES = ((0, A_W), (A_W, B_W), (A_W + B_W, C_W), (A_W + B_W + C_W, F_W))


def _inproj_bwd_x(x2, dxo, g_row, w_int, pieces, name):
    n, d = x2.shape
    tm = min(256, n)

    def body(x_ref, dxo_ref, g_ref, w_ref, da_ref, db_ref, dc_ref, df_ref, dx_ref, dg_ref):
        @pl.when(pl.program_id(0) == 0)
        def _():
            dg_ref[...] = jnp.zeros((8, d), F32)

        dh = jnp.zeros((tm, d), F32)
        for ref, (o, w) in zip((da_ref, db_ref, dc_ref, df_ref), _PIECES):
            dh = dh + _dot(ref[...].astype(BF16), w_ref[:, o:o + w], NT)
        x = x_ref[...]
        r = _rstd(x)
        xn = x * r
        dg_ref[...] += jnp.sum(dh * xn, axis=0, keepdims=True)
        dxn = dh * g_ref[...]
        dx_ref[...] = dxo_ref[...] + r * (dxn - xn * jnp.mean(dxn * xn, axis=-1, keepdims=True))

    row = lambda w: pl.BlockSpec((tm, w), lambda i: (i, 0))
    fixed = lambda r, c: pl.BlockSpec((r, c), lambda i: (0, 0))
    return pl.pallas_call(
        body, grid=(n // tm,),
        in_specs=[row(d), row(d), fixed(1, d), fixed(d, E_INT)] + [row(w) for _, w in _PIECES],
        out_specs=[row(d), fixed(8, d)],
        out_shape=[jax.ShapeDtypeStruct((n, d), F32), jax.ShapeDtypeStruct((8, d), F32)],
        compiler_params=_cparams(("arbitrary",)), name=name)(x2, dxo, g_row, w_int, *pieces)


def _inproj_bwd_w(x2, g_row, pieces, name):
    n, d = x2.shape
    tm = min(256, n)

    def body(x_ref, g_ref, da_ref, db_ref, dc_ref, df_ref, dw_ref):
        @pl.when(pl.program_id(0) == 0)
        def _():
            dw_ref[...] = jnp.zeros((d, E_INT), F32)

        x = x_ref[...]
        h = (x * _rstd(x) * g_ref[...]).astype(BF16)
        for ref, (o, w) in zip((da_ref, db_ref, dc_ref, df_ref), _PIECES):
            dw_ref[:, o:o + w] += _dot(h, ref[...].astype(BF16), TN)

    row = lambda w: pl.BlockSpec((tm, w), lambda i: (i, 0))
    return pl.pallas_call(
        body, grid=(n // tm,),
        in_specs=[row(d), pl.BlockSpec((1, d), lambda i: (0, 0))] + [row(w) for _, w in _PIECES],
        out_specs=pl.BlockSpec((d, E_INT), lambda i: (0, 0)),
        out_shape=jax.ShapeDtypeStruct((d, E_INT), F32),
        compiler_params=_cparams(("arbitrary",), vmem_mb=56), name=name)(x2, g_row, *pieces)


def _block_diag(pool_w_l):
    z = jnp.zeros((64, 64), pool_w_l.dtype)
    return jnp.concatenate(
        [jnp.concatenate([pool_w_l[g] if c == g else z for c in range(4)], axis=1) for g in range(4)], axis=0)


def _pad_lanes(v, width=128):
    return jnp.pad(v, ((0, 0),) * (v.ndim - 1) + ((0, width - v.shape[-1]),))


def _local_step(x, target, lower_bounds, pre_norm_g, w_in_int, hgrn_norm_g, fox_f_bias, pool_w, pool_scale,
                w_out_bf, post_norm_g, on_weight_grads):
    bsz, t, d = x.shape
    n = bsz * t
    lbs = _lbs_fwd(lower_bounds)
    saved = []
    xc = x.reshape(n, d)
    for l in range(DEPTH):
        proj = _inproj_fwd(xc, pre_norm_g[l:l + 1], w_in_int[l], f"inproj_fwd{l}").reshape(bsz, t, E_INT)
        wbd = _block_diag(pool_w[l]).astype(BF16)
        bias_row = _pad_lanes(fox_f_bias[l:l + 1])
        oa, oa_raw, states = _hgrn_fwd(proj, lbs[l:l + 1], hgrn_norm_g[l:l + 1], f"hgrn_fwd{l}")
        ob = _pool_fwd(proj, wbd, pool_scale[l:l + 1], f"pool_fwd{l}")
        c_nat, c_t = _foxgate_fwd(proj, bias_row, f"foxgate_fwd{l}")
        oc, oc_raw, lse = _fox_fwd(proj, c_nat, c_t, f"fox_fwd{l}")
        mixed = (oa.reshape(n, -1), ob.reshape(n, -1), oc.reshape(n, -1))
        if l < DEPTH - 1:
            y, xn = _outproj_fwd(xc, *mixed, w_out_bf[l], post_norm_g[l:l + 1], f"outproj_fwd{l}")
        else:
            y, dx, sq = _outproj_fwd_loss(xc, *mixed, w_out_bf[l], post_norm_g[l:l + 1], target.reshape(n, d),
                                          f"outproj_fwd{l}")
        saved.append((xc, proj, wbd, bias_row, oa, oa_raw, states, ob, oc, oc_raw, lse, c_nat, c_t, y))
        xc = xn
    g = {k: [None] * DEPTH for k in ("pre", "hgn", "bias", "pool_w", "pool_scale", "post", "lbs")}
    handed = [None] * DEPTH
    for l in reversed(range(DEPTH)):
        xin, proj, wbd, bias_row, oa, oa_raw, states, ob, oc, oc_raw, lse, c_nat, c_t, y = saved[l]
        dmix, d_w_out, dpost = _outproj_bwd(dx, y, oa.reshape(n, -1), ob.reshape(n, -1), oc.reshape(n, -1),
                                            w_out_bf[l], post_norm_g[l:l + 1], f"outproj_bwd{l}")
        g["post"][l] = dpost[0]
        dmix3 = dmix.reshape(bsz, t, d)
        d_c, dct, drow = _fox_bwd(proj, oc_raw, dmix3, lse, c_nat, c_t, f"fox_bwd{l}")
        dc_nat = _pad_lanes(dct[:, :, 0:2, :].reshape(bsz, FOX_HEADS, t).transpose(0, 2, 1)
                            + drow.reshape(bsz, t, FOX_HEADS, 64)[..., 0])
        d_f, dbias = _foxgate_bwd(proj, dc_nat, bias_row, f"foxgate_bwd{l}")
        g["bias"][l] = jnp.sum(dbias[:, 0, :FOX_HEADS], axis=0)
        d_b, dscale, dwbd = _pool_bwd(proj, dmix3, wbd, pool_scale[l:l + 1], f"pool_bwd{l}")
        g["pool_scale"][l] = jnp.sum(dscale[:, 0], axis=0)
        dwbd = jnp.sum(dwbd, axis=0)
        g["pool_w"][l] = jnp.stack([dwbd[64 * k:64 * (k + 1), 64 * k:64 * (k + 1)] for k in range(4)])
        d_a, dgn, dlb = _hgrn_bwd(proj, oa_raw, dmix3, states, lbs[l:l + 1], hgrn_norm_g[l:l + 1], f"hgrn_bwd{l}")
        g["hgn"][l] = jnp.sum(dgn[:, 0], axis=0)
        g["lbs"][l] = jnp.sum(dlb[:, 0], axis=0)
        pieces = [p.reshape(n, -1) for p in (d_a, d_b, d_c, d_f)]
        handed[l] = on_weight_grads(l, _inproj_bwd_w(xin, pre_norm_g[l:l + 1], pieces, f"inproj_bwd_w{l}"), d_w_out)
        dx, dpre = _inproj_bwd_x(xin, dx, pre_norm_g[l:l + 1], w_in_int[l], pieces, f"inproj_bwd_x{l}")
        g["pre"][l] = dpre[0]
    grads = {k: jnp.stack(v) for k, v in g.items()}
    return sq, dx.reshape(bsz, t, d), grads, handed


def _place():
    return lax.axis_index("x"), lax.axis_index("y"), lax.axis_index("c")


def _other_chips(x, y):
    return [(1 - x, y), (x, 1 - y), (1 - x, 1 - y)]


_ANY = pl.BlockSpec(memory_space=pl.ANY)


def _gather_body(handshake, n_arrays):
    def body(*refs):
        srcs, dsts = refs[:n_arrays], refs[n_arrays:2 * n_arrays]
        ici_send, ici_recv, d2d_send, d2d_recv, local_sems = refs[2 * n_arrays:]
        x, y, c = _place()
        if handshake:
            barrier = pltpu.get_barrier_semaphore()
            for peer in [(px, py, c) for px, py in _other_chips(x, y)] + [(x, y, 1 - c)]:
                pl.semaphore_signal(barrier, inc=1, device_id=peer, device_id_type=MESH)
            pl.semaphore_wait(barrier, 4)
        me = 2 * x + y
        pairs = list(zip(srcs, dsts))
        order = [(k, j) for k in range(3) for j in range(n_arrays)]
        mine = [pltpu.make_async_copy(src, dst.at[me], local_sems.at[j]) for j, (src, dst) in enumerate(pairs)]
        for cp in mine:
            cp.start()
        chips = _other_chips(x, y)
        sends = [pltpu.make_async_remote_copy(
            src_ref=pairs[j][0].at[c], dst_ref=pairs[j][1].at[me, c], send_sem=ici_send.at[n], recv_sem=ici_recv.at[n],
            device_id=(chips[k][0], chips[k][1], c), device_id_type=MESH) for n, (k, j) in enumerate(order)]
        for cp in sends:
            cp.start()
        passed = [pltpu.make_async_remote_copy(
            src_ref=pairs[j][1].at[2 * chips[k][0] + chips[k][1], c], dst_ref=pairs[j][1].at[2 * chips[k][0] + chips[k][1], c],
            send_sem=d2d_send.at[n], recv_sem=d2d_recv.at[n], device_id=(x, y, 1 - c), device_id_type=MESH)
            for n, (k, j) in enumerate(order)]
        for n, (k, j) in enumerate(order):
            px, py = chips[k]
            src, dst = pairs[j]
            pltpu.make_async_remote_copy(
                src_ref=src.at[c], dst_ref=dst.at[2 * px + py, c], send_sem=ici_send.at[n], recv_sem=ici_recv.at[n],
                device_id=(px, py, c), device_id_type=MESH).wait_recv()
            passed[n].start()
        for n, (k, j) in enumerate(order):
            px, py = chips[k]
            src, dst = pairs[j]
            pltpu.make_async_remote_copy(
                src_ref=dst.at[2 * px + py, 1 - c], dst_ref=dst.at[2 * px + py, 1 - c], send_sem=d2d_send.at[n],
                recv_sem=d2d_recv.at[n], device_id=(x, y, 1 - c), device_id_type=MESH).wait_recv()
        for cp in sends + passed:
            cp.wait_send()
        for cp in mine:
            cp.wait()

    return body


def _gather_sems(n_arrays):
    return [pltpu.SemaphoreType.DMA((3 * n_arrays,))] * 4 + [pltpu.SemaphoreType.DMA((n_arrays,))]


def _gathered(a):
    return jax.ShapeDtypeStruct((N_CHIPS,) + a.shape, a.dtype)


def _gather_weights(arrays):
    n = len(arrays)
    return pl.pallas_call(
        _gather_body(False, n), in_specs=[_ANY] * n, out_specs=[_ANY] * n, out_shape=[_gathered(a) for a in arrays],
        scratch_shapes=_gather_sems(n), name="gather_weights")(*arrays)


def _gather_weights_beside(arrays):
    hbm = pltpu.MemorySpace.HBM
    n = len(arrays)
    srcs = [jax.new_ref(a, memory_space=hbm) for a in arrays]
    dsts = [jax.empty_ref(_gathered(a), memory_space=hbm) for a in arrays]
    body = _gather_body(True, n)

    @pl.kernel(mesh=plsc.ScalarSubcoreMesh(axis_name="sequencer", num_cores=1), name="gather_weights_beside",
               scratch_types=_gather_sems(n), compiler_params=pltpu.CompilerParams(collective_id=1))
    def launch(*sems):
        body(*srcs, *dsts, *sems)

    launch()
    return [d[...] for d in dsts]


def _swap_with_sibling(parts, name):
    k = len(parts)

    def body(*refs):
        src, dst = refs[:k], refs[k:2 * k]
        send_sems, recv_sems = refs[2 * k:]
        x, y, c = _place()
        cps = [pltpu.make_async_remote_copy(src_ref=src[j], dst_ref=dst[j], send_sem=send_sems.at[j], recv_sem=recv_sems.at[j],
                                            device_id=(x, y, 1 - c), device_id_type=MESH) for j in range(k)]
        for cp in cps:
            cp.start()
        for cp in cps:
            cp.wait()

    return pl.pallas_call(
        body, in_specs=[_ANY] * k, out_specs=[_ANY] * k,
        out_shape=[jax.ShapeDtypeStruct(p.shape, p.dtype) for p in parts],
        scratch_shapes=[pltpu.SemaphoreType.DMA((k,)), pltpu.SemaphoreType.DMA((k,))], name=name)(*parts)


N_PEERS = 7


def _grad_exchange_body():
    def body(pin_ref, pout_ref, lin_ref, lout_ref, send_sems, recv_sems):
        x, y, c = _place()
        barrier = pltpu.get_barrier_semaphore()
        for k in range(1, N_PEERS + 1):
            peer = (x ^ ((k >> 2) & 1), y ^ ((k >> 1) & 1), c ^ (k & 1))
            pl.semaphore_signal(barrier, inc=1, device_id=peer, device_id_type=MESH)
        pl.semaphore_wait(barrier, N_PEERS)
        me = 2 * x + y
        pairs = ((pin_ref, lin_ref), (pout_ref, lout_ref))
        cps = []
        for k, (px, py) in enumerate(_other_chips(x, y)):
            for r in range(2):
                for j, (src, dst) in enumerate(pairs):
                    cps.append(pltpu.make_async_remote_copy(
                        src_ref=src.at[2 * px + py, r], dst_ref=dst.at[2 * k + c], send_sem=send_sems.at[2 * (2 * k + r) + j],
                        recv_sem=recv_sems.at[2 * (2 * k + c) + j], device_id=(px, py, r), device_id_type=MESH))
        for j, (src, dst) in enumerate(pairs):
            cps.append(pltpu.make_async_remote_copy(
                src_ref=src.at[me, 1 - c], dst_ref=dst.at[N_PEERS - 1], send_sem=send_sems.at[2 * (N_PEERS - 1) + j],
                recv_sem=recv_sems.at[2 * (N_PEERS - 1) + j], device_id=(x, y, 1 - c), device_id_type=MESH))
        for cp in cps:
            cp.start()
        for s in range(N_PEERS):
            for j, (src, dst) in enumerate(pairs):
                pltpu.make_async_remote_copy(
                    src_ref=src.at[0, 0], dst_ref=dst.at[s], send_sem=send_sems.at[2 * s + j], recv_sem=recv_sems.at[2 * s + j],
                    device_id=(x, y, 1 - c), device_id_type=MESH).wait_recv()
        for cp in cps:
            cp.wait_send()

    return body


_EXCHANGE_SEMS = [pltpu.SemaphoreType.DMA((2 * N_PEERS,))] * 2


def _landing(p):
    return jax.ShapeDtypeStruct((N_PEERS,) + p.shape[2:], p.dtype)


def _grad_exchange_beside(pin, pout, name, collective_id):
    hbm = pltpu.MemorySpace.HBM
    pin_ref, pout_ref = jax.new_ref(pin, memory_space=hbm), jax.new_ref(pout, memory_space=hbm)
    lin_ref, lout_ref = jax.empty_ref(_landing(pin), memory_space=hbm), jax.empty_ref(_landing(pout), memory_space=hbm)
    body = _grad_exchange_body()

    @pl.kernel(mesh=plsc.ScalarSubcoreMesh(axis_name="sequencer", num_cores=1), name=name,
               scratch_types=_EXCHANGE_SEMS, compiler_params=pltpu.CompilerParams(collective_id=collective_id))
    def launch(send_sems, recv_sems):
        body(pin_ref, pout_ref, lin_ref, lout_ref, send_sems, recv_sems)

    launch()
    return lin_ref[...], lout_ref[...]


def _add_n(parts, name, with_bf16=False):
    r, c = parts[0].shape
    tr = 256 if r % 256 == 0 else r
    n = len(parts)

    def body(*refs):
        acc = refs[0][...].astype(F32)
        for ref in refs[1:n]:
            acc = acc + ref[...].astype(F32)
        refs[n][...] = acc
        if with_bf16:
            refs[n + 1][...] = acc.astype(BF16)

    blk = pl.BlockSpec((tr, c), lambda i: (i, 0))
    outs = [jax.ShapeDtypeStruct((r, c), F32)] + ([jax.ShapeDtypeStruct((r, c), BF16)] if with_bf16 else [])
    res = pl.pallas_call(
        body, grid=(r // tr,), in_specs=[blk] * n, out_specs=[blk] * len(outs),
        out_shape=outs, compiler_params=_cparams(("parallel",)), name=name)(*parts)
    return res if with_bf16 else res[0]


def _all_reduce_small(packet):
    r, w = packet.shape

    def body(p_ref, o_ref, buf, send_sems, recv_sems):
        x, y, c = _place()
        me = 4 * x + 2 * y + c
        buf[me] = p_ref[...]
        peers = []
        for k in range(1, 8):
            fx, fy, fc = (k >> 2) & 1, (k >> 1) & 1, k & 1
            peers.append((x ^ fx, y ^ fy, c ^ fc))
        cps = [pltpu.make_async_remote_copy(src_ref=p_ref, dst_ref=buf.at[me], send_sem=send_sems.at[k], recv_sem=recv_sems.at[k],
                                            device_id=peer, device_id_type=MESH) for k, peer in enumerate(peers)]
        for cp in cps:
            cp.start()
        for k, (px, py, pc) in enumerate(peers):
            pltpu.make_async_remote_copy(src_ref=p_ref, dst_ref=buf.at[4 * px + 2 * py + pc], send_sem=send_sems.at[k],
                                         recv_sem=recv_sems.at[k], device_id=(px, py, pc), device_id_type=MESH).wait_recv()
        for cp in cps:
            cp.wait_send()
        acc = buf[0]
        for k in range(1, 8):
            acc = acc + buf[k]
        o_ref[...] = acc

    vm = pl.BlockSpec(memory_space=pltpu.VMEM)
    return pl.pallas_call(
        body, in_specs=[vm], out_specs=vm, out_shape=jax.ShapeDtypeStruct((r, w), F32),
        scratch_shapes=[pltpu.VMEM((8, r, w), F32), pltpu.SemaphoreType.DMA((7,)), pltpu.SemaphoreType.DMA((7,))],
        name="all_reduce_small")(packet)


def _adamw_math(w, g, m, v):
    m = ADAM_B1 * m + (1.0 - ADAM_B1) * g
    v = ADAM_B2 * v + (1.0 - ADAM_B2) * (g * g)
    m_hat = m / (1.0 - ADAM_B1 ** ADAM_STEP)
    v_hat = v / (1.0 - ADAM_B2 ** ADAM_STEP)
    return -ADAM_LR * (m_hat / (jnp.sqrt(v_hat) + ADAM_EPS) + ADAM_WD * w), m, v


def _adamw(w, g_lower, g_upper, m, v, name):
    nl, r, c = w.shape
    tr = 128
    per_half = r // (2 * tr)

    def body(w_ref, lo_ref, up_ref, m_ref, v_ref, g_ref, d_ref, mo_ref, vo_ref):
        g = jnp.where(pl.program_id(1) == 0, lo_ref[...], up_ref[...])
        g_ref[...] = g
        d_ref[...], mo_ref[...], vo_ref[...] = _adamw_math(w_ref[...], g, m_ref[...], v_ref[...])

    blk = pl.BlockSpec((None, tr, c), lambda l, h, i: (l, h * per_half + i, 0))
    half = pl.BlockSpec((None, tr, c), lambda l, h, i: (l, i, 0))
    out = jax.ShapeDtypeStruct(w.shape, F32)
    return pl.pallas_call(
        body, grid=(nl, 2, per_half), in_specs=[blk, half, half, blk, blk], out_specs=[blk] * 4, out_shape=[out] * 4,
        compiler_params=_cparams(("parallel", "parallel", "parallel")), name=name)(w, g_lower, g_upper, m, v)


def _small_update(gsum, lower_bounds, wpack, mpack, vpack):
    r, w = gsum.shape
    lb_rows = DEPTH * HGRN_W // 128

    def body(g_ref, a_ref, w_ref, m_ref, v_ref, go_ref, d_ref, mo_ref, vo_ref):
        a = a_ref[...]
        a0, a1 = a[0:1], a[1:2]
        mx = jnp.maximum(a0, a1)
        e0, e1 = jnp.exp(a0 - mx), jnp.exp(a1 - mx)
        p0, p1 = e0 / (e0 + e1), e1 / (e0 + e1)
        g = g_ref[...]
        half = lb_rows // 2
        dl0 = jnp.concatenate([g[k:k + 1] for k in range(half)], axis=1)
        dl1 = jnp.concatenate([g[half + k:half + k + 1] for k in range(half)], axis=1)
        dp0 = (dl0 + dl1) - (dl0 + dl1)
        dp1 = dl1
        inner = p0 * dp0 + p1 * dp1
        da0, da1 = p0 * (dp0 - inner), p1 * (dp1 - inner)
        rows = [da0[:, 128 * k:128 * (k + 1)] for k in range(half)] + [da1[:, 128 * k:128 * (k + 1)] for k in range(half)]
        gfull = jnp.concatenate(rows + [g[lb_rows:]], axis=0)
        go_ref[...] = gfull
        d_ref[...], mo_ref[...], vo_ref[...] = _adamw_math(w_ref[...], gfull, m_ref[...], v_ref[...])

    vm = pl.BlockSpec(memory_space=pltpu.VMEM)
    out = jax.ShapeDtypeStruct((r, w), F32)
    return pl.pallas_call(body, in_specs=[vm] * 5, out_specs=[vm] * 4, out_shape=[out] * 4, name="small_update")(
        gsum, lower_bounds, wpack, mpack, vpack)


_SMALL = ("lower_bounds", "pre_norm_g", "hgrn_norm_g", "fox_f_bias", "pool_w", "pool_scale", "post_norm_g")


def _pack(parts):
    rows = []
    for k in _SMALL:
        f = parts[k].reshape(-1)
        pad = (-f.shape[0]) % (8 * 128)
        rows.append(jnp.pad(f, (0, pad)).reshape(-1, 128))
    rows.append(jnp.zeros((8, 128), F32))
    return jnp.concatenate(rows, axis=0)


def _unpack(pack, like):
    out, r = {}, 0
    for k in _SMALL:
        size = int(np.prod(like[k].shape))
        nr = -(-size // (8 * 128)) * 8
        out[k] = pack[r:r + nr].reshape(-1)[:size].reshape(like[k].shape)
        r += nr
    return out, r


def kernel(x, lower_bounds, pre_norm_g, w_in, hgrn_norm_g, fox_f_bias, pool_w, pool_scale, w_out, post_norm_g, loss_target, m_lower_bounds, m_pre_norm_g, m_w_in, m_hgrn_norm_g, m_fox_f_bias, m_pool_w, m_pool_scale, m_w_out, m_post_norm_g, v_lower_bounds, v_pre_norm_g, v_w_in, v_hgrn_norm_g, v_fox_f_bias, v_pool_w, v_pool_scale, v_w_out, v_post_norm_g):
    cx, cy, cc = _place()
    chip = 2 * cx + cy

    halves = lambda w, l: w[l].reshape(2, w.shape[1] // 2, w.shape[2]).astype(BF16)
    needed_first = _gather_weights([halves(w_in, 0)])
    needed_first, later = lax.optimization_barrier((needed_first, [halves(w_out, 0), halves(w_in, 1), halves(w_out, 1)]))
    later = _gather_weights_beside(later)
    w_in_int = [_internal_from_shards([a[q].reshape(D_MODEL, SHARD_W) for q in range(N_CHIPS)]) for a in (needed_first[0], later[1])]
    w_out_full = [a.reshape(D_MODEL, D_MODEL) for a in (later[0], later[2])]

    def on_weight_grads(l, d_w_in, d_w_out):
        pin = _shards_from_internal(d_w_in).reshape(N_CHIPS, 2, D_MODEL // 2, SHARD_W)
        pout = d_w_out.reshape(N_CHIPS, 2, D_MODEL // (2 * N_CHIPS), D_MODEL)
        own = [lax.dynamic_index_in_dim(lax.dynamic_index_in_dim(p, chip, 0, False), cc, 0, False) for p in (pin, pout)]
        return own, _grad_exchange_beside(pin.astype(BF16), pout.astype(BF16), f"grad_exchange{l}", 2 + l)

    sq, grad_x, g, handed = _local_step(x, loss_target, lower_bounds, pre_norm_g, w_in_int, hgrn_norm_g, fox_f_bias,
                                        pool_w, pool_scale, w_out_full, post_norm_g, on_weight_grads)
    first = cc == 0

    def finish(l, own, landed):
        mine = [_add_n([o] + [t[s] for s in range(N_PEERS)], f"grad_sum{l}_{j}") for j, (o, t) in enumerate(zip(own, landed))]
        theirs = _swap_with_sibling(mine, f"grad_swap{l}")
        return [(jnp.where(first, h, o), jnp.where(first, o, h)) for h, o in zip(mine, theirs)]

    grad_x, last = lax.optimization_barrier((grad_x, handed[1]))
    done = [None, finish(1, *last)]

    small = {"lower_bounds": g["lbs"], "pre_norm_g": g["pre"], "hgrn_norm_g": g["hgn"], "fox_f_bias": g["bias"],
             "pool_w": g["pool_w"], "pool_scale": g["pool_scale"], "post_norm_g": g["post"]}
    packet = _pack(small)
    nrows = packet.shape[0]
    packet = packet.at[nrows - 1].set(sq[0])
    gsum = _all_reduce_small(packet)
    loss = gsum[nrows - 1, 0] * (0.5 / D_MODEL)

    weights = {"lower_bounds": lower_bounds, "pre_norm_g": pre_norm_g, "hgrn_norm_g": hgrn_norm_g,
               "fox_f_bias": fox_f_bias, "pool_w": pool_w, "pool_scale": pool_scale, "post_norm_g": post_norm_g}
    moments_m = {"lower_bounds": m_lower_bounds, "pre_norm_g": m_pre_norm_g, "hgrn_norm_g": m_hgrn_norm_g,
                 "fox_f_bias": m_fox_f_bias, "pool_w": m_pool_w, "pool_scale": m_pool_scale, "post_norm_g": m_post_norm_g}
    moments_v = {"lower_bounds": v_lower_bounds, "pre_norm_g": v_pre_norm_g, "hgrn_norm_g": v_hgrn_norm_g,
                 "fox_f_bias": v_fox_f_bias, "pool_w": v_pool_w, "pool_scale": v_pool_scale, "post_norm_g": v_post_norm_g}
    gp, dp, mp, vp = _small_update(gsum, lower_bounds, _pack(weights), _pack(moments_m), _pack(moments_v))
    gs, _ = _unpack(gp, weights)
    ds, _ = _unpack(dp, weights)
    ms, _ = _unpack(mp, weights)
    vs, _ = _unpack(vp, weights)

    first_layer, _ = lax.optimization_barrier((handed[0], (done[1], gp, dp, mp, vp)))
    done[0] = finish(0, *first_layer)
    halves_of = lambda j, side: jnp.stack([done[l][j][side] for l in range(DEPTH)])
    grad_w_in, d_in, m_in, v_in = _adamw(w_in, halves_of(0, 0), halves_of(0, 1), m_w_in, v_w_in, "adamw_w_in")
    grad_w_out, d_out, m_out, v_out = _adamw(w_out, halves_of(1, 0), halves_of(1, 1), m_w_out, v_w_out, "adamw_w_out")

    def ordered(s, big_in, big_out):
        return (s["lower_bounds"], s["pre_norm_g"], big_in, s["hgrn_norm_g"], s["fox_f_bias"], s["pool_w"],
                s["pool_scale"], big_out, s["post_norm_g"])

    return (loss, grad_x, *ordered(gs, grad_w_in, grad_w_out), *ordered(ds, d_in, d_out),
            *ordered(ms, m_in, m_out), *ordered(vs, v_in, v_out))
```

```python
import functools

import numpy as np
import jax
import jax.numpy as jnp
from jax import lax
from jax.experimental import pallas as pl
from jax.experimental.pallas import tpu as pltpu
from jax.experimental.pallas import tpu_sc as plsc

F32 = jnp.float32
BF16 = jnp.bfloat16
HI = lax.Precision.HIGHEST
MESH = pl.DeviceIdType.MESH

NORM_EPS = 1e-6
MASK_VALUE = -1e30
TINY = 1e-30
ADAM_LR, ADAM_B1, ADAM_B2, ADAM_EPS, ADAM_WD, ADAM_STEP = 0.001, 0.9, 0.999, 1e-08, 0.01, 10

D_MODEL = 1024
DEPTH = 2
N_CHIPS = 4
CHUNK = 64
LANES = 128
HGRN_W, POOL_W, FOX_W, FOX_HEADS = 256, 256, 512, 8
POOL_WINDOWS = (2, 4, 8, 16)
POOL_HALO = 16
IN_WIDTH = 3592
SHARD_W = IN_WIDTH // N_CHIPS
A_W, B_W, C_W, F_W = 1024, 512, 2048, 128
E_INT = A_W + B_W + C_W + F_W
B_BLK = A_W // 512
C_BLK0 = (A_W + B_W) // 512
F_BLK = (A_W + B_W + C_W) // 128


def _segments():
    segs = []
    for hp in range(2):
        for part in range(4):
            segs.append((part * 256 + hp * 128, 128))
    segs.append((1024, 256))
    segs.append((1280, 256))
    for hp in range(4):
        for part in range(4):
            segs.append((1536 + part * 512 + hp * 128, 128))
    segs.append((3584, 8))
    return segs


_SEGS = _segments()


def _to_internal(w):
    parts = [w[..., s:s + n] for s, n in _SEGS]
    parts.append(jnp.zeros(w.shape[:-1] + (E_INT - IN_WIDTH,), w.dtype))
    return jnp.concatenate(parts, axis=-1)


def _to_original(w):
    offs, o = [], 0
    for s, n in _SEGS:
        offs.append((s, o, n))
        o += n
    parts = [w[..., o:o + n] for s, o, n in sorted(offs)]
    return jnp.concatenate(parts, axis=-1)


def _internal_from_shards(shards):
    parts = []
    for s, n in _SEGS:
        while n > 0:
            q, r = divmod(s, SHARD_W)
            take = min(n, SHARD_W - r)
            parts.append(shards[q][..., r:r + take])
            s, n = s + take, n - take
    parts.append(jnp.zeros(shards[0].shape[:-1] + (E_INT - IN_WIDTH,), shards[0].dtype))
    return jnp.concatenate(parts, axis=-1)


def _shards_from_internal(w):
    offs, o = [], 0
    for s, n in _SEGS:
        offs.append((s, o, n))
        o += n
    blocks = []
    for q in range(N_CHIPS):
        lo, hi = SHARD_W * q, SHARD_W * (q + 1)
        parts = [w[..., o + max(lo, s) - s:o + min(hi, s + n) - s] for s, o, n in sorted(offs) if s < hi and s + n > lo]
        blocks.append(jnp.concatenate(parts, axis=-1))
    return jnp.stack(blocks)


def _cparams(sem=None, vmem_mb=48):
    kw = dict(vmem_limit_bytes=vmem_mb * 1024 * 1024)
    if sem is not None:
        kw["dimension_semantics"] = sem
    return pltpu.CompilerParams(**kw)


def _sig(x):
    return 1.0 / (1.0 + jnp.exp(-x))


def _silu(x):
    return x * _sig(x)


def _dsilu(x):
    s = _sig(x)
    return s * (1.0 + x * (1.0 - s))


def _rstd(x):
    return lax.rsqrt(jnp.mean(x * x, axis=-1, keepdims=True) + NORM_EPS)


def _dot(a, b, dims, **kw):
    return lax.dot_general(a, b, (dims, ((), ())), preferred_element_type=F32, **kw)


NN = ((1,), (0,))
NT = ((1,), (1,))
TN = ((0,), (0,))


def _iota(shape, dim):
    return lax.broadcasted_iota(jnp.int32, shape, dim)


def _lbs_fwd(lower_bounds):
    def body(a_ref, o_ref):
        a = a_ref[...]
        a0, a1 = a[0:1], a[1:2]
        m = jnp.maximum(a0, a1)
        e0, e1 = jnp.exp(a0 - m), jnp.exp(a1 - m)
        p0, p1 = e0 / (e0 + e1), e1 / (e0 + e1)
        o_ref[...] = jnp.concatenate([p0 - p0, (p0 + p1) - p0], axis=0)

    return pl.pallas_call(body, out_shape=jax.ShapeDtypeStruct(lower_bounds.shape, F32), name="lbs_fwd")(lower_bounds)


def _inproj_fwd(x2, g_row, w_int, name):
    n, d = x2.shape
    e = w_int.shape[1]
    tm = min(512, n)

    def body(x_ref, g_ref, w_ref, o_ref):
        x = x_ref[...]
        h = (x * _rstd(x) * g_ref[...]).astype(BF16)
        o_ref[...] = jnp.dot(h, w_ref[...], preferred_element_type=F32)

    return pl.pallas_call(
        body, grid=(n // tm,),
        in_specs=[pl.BlockSpec((tm, d), lambda i: (i, 0)), pl.BlockSpec((1, d), lambda i: (0, 0)),
                  pl.BlockSpec((d, e), lambda i: (0, 0))],
        out_specs=pl.BlockSpec((tm, e), lambda i: (i, 0)),
        out_shape=jax.ShapeDtypeStruct((n, e), F32),
        compiler_params=_cparams(("parallel",)), name=name)(x2, g_row, w_int)


def _chunk_cumsum_matrix():
    i, j = _iota((LANES, LANES), 0), _iota((LANES, LANES), 1)
    return ((i <= j) & ((i // CHUNK) == (j // CHUNK))).astype(F32)


def _hgrn_gates(a, lb):
    qa, z = a[:, 0:128], a[:, 128:256]
    sg, sgn = _sig(z), _sig(-z)
    fg = lb + (1.0 - lb) * sg
    lf = jnp.log(jnp.maximum(fg, TINY))
    kk = (1.0 - lb) * sgn
    return qa * _sig(qa), kk, lf, sg, sgn, fg


def _hgrn_fwd(proj3, lbs_row, gn_col, name):
    bsz, t, _ = proj3.shape
    nt = t // LANES

    def body(a_ref, lb_ref, gn_ref, og_ref, or_ref):
        lb = lb_ref[...]
        gn = gn_ref[...]
        umat = _chunk_cumsum_matrix()
        lane64 = _iota((1, LANES), 1) % CHUNK

        def tile(i, carry):
            r0 = pl.multiple_of(i * LANES, LANES)
            a = a_ref[pl.ds(r0, LANES), :]
            qq, kk, lf, _, _, _ = _hgrn_gates(a, lb)
            va, ga = a[:, 256:384], a[:, 384:512]
            q_t, k_t, v_t = qq.T, kk.T, va.T
            b_t = jnp.dot(lf.T, umat, precision=HI, preferred_element_type=F32)
            new_s, o_heads = [], []
            for h in range(2):
                s_h = carry[h]
                rs = slice(CHUNK * h, CHUNK * (h + 1))
                qh, kh, vh, bh = q_t[rs], k_t[rs], v_t[rs], b_t[rs]
                inter = []
                for c in range(2):
                    cs = slice(CHUNK * c, CHUNK * (c + 1))
                    b_ = bh[:, cs]
                    qt = (qh[:, cs] * jnp.exp(b_)).astype(BF16)
                    inter.append(_dot(s_h.astype(BF16), qt, TN))
                    bl = b_[:, CHUNK - 1:CHUNK]
                    kt = (kh[:, cs] * jnp.exp(bl - b_)).astype(BF16)
                    s_h = jnp.exp(bl) * s_h + _dot(kt, vh[:, cs].astype(BF16), NT)
                new_s.append(s_h)

                acc = jnp.concatenate(inter, axis=1) + jnp.sum(qh * kh, axis=0, keepdims=True) * vh
                for dlt in range(1, CHUNK):
                    kr, br, vr = pltpu.roll(kh, dlt, 1), pltpu.roll(bh, dlt, 1), pltpu.roll(vh, dlt, 1)
                    e = jnp.exp(jnp.minimum(bh - br, 0.0))
                    att = jnp.sum(qh * kr * e, axis=0, keepdims=True)
                    acc = acc + jnp.where(lane64 >= dlt, att, 0.0) * vr
                o_heads.append(acc)
            normed = []
            for h in range(2):
                o_h = o_heads[h]
                ms = jnp.mean(o_h * o_h, axis=0, keepdims=True)
                normed.append(o_h * lax.rsqrt(ms + NORM_EPS) * gn[CHUNK * h:CHUNK * (h + 1)])
            or_ref[pl.ds(r0, LANES), :] = jnp.concatenate(o_heads, axis=0).T
            og_ref[pl.ds(r0, LANES), :] = jnp.concatenate(normed, axis=0).T * _silu(ga)
            return tuple(new_s)

        zero = jnp.zeros((CHUNK, CHUNK), F32)
        lax.fori_loop(0, nt, tile, (zero, zero))

    out = jax.ShapeDtypeStruct((bsz, t, HGRN_W), F32)
    return pl.pallas_call(
        body, grid=(bsz, 2),
        in_specs=[pl.BlockSpec((None, t, 512), lambda b, p: (b, 0, p)),
                  pl.BlockSpec((1, 128), lambda b, p: (0, p)),
                  pl.BlockSpec((128, 1), lambda b, p: (p, 0))],
        out_specs=[pl.BlockSpec((None, t, 128), lambda b, p: (b, 0, p)),
                   pl.BlockSpec((None, t, 128), lambda b, p: (b, 0, p))],
        out_shape=[out, out],
        compiler_params=_cparams(("parallel", "parallel")), name=name)(proj3, lbs_row, gn_col)


def _hgrn_bwd(proj3, o_raw, dmixed, lbs_row, gn_row, name):
    bsz, t, _ = proj3.shape
    nt = t // LANES
    nchunk = t // CHUNK

    def body(a_ref, or_ref, do_ref, lb_ref, gn_ref, da_ref, dgn_ref, dlb_ref, s_sc):
        lb = lb_ref[...]
        gn = gn_ref[...]
        umat = _chunk_cumsum_matrix()
        lane = _iota((1, LANES), 1)
        lane64 = lane % CHUNK
        half = lane < CHUNK

        def t_layout(a):
            qq, kk, lf, sg, sgn, fg = _hgrn_gates(a, lb)
            b_t = jnp.dot(lf.T, umat, precision=HI, preferred_element_type=F32)
            return qq.T, kk.T, a[:, 256:384].T, b_t, (sg, sgn, fg)

        def fwd_tile(i, carry):
            r0 = pl.multiple_of(i * LANES, LANES)
            q_t, k_t, v_t, b_t, _ = t_layout(a_ref[pl.ds(r0, LANES), :])
            new_s = []
            for h in range(2):
                s_h = carry[h]
                rs = slice(CHUNK * h, CHUNK * (h + 1))
                for c in range(2):
                    cs = slice(CHUNK * c, CHUNK * (c + 1))
                    s_sc[h, 2 * i + c] = s_h
                    b_ = b_t[rs, cs]
                    bl = b_[:, CHUNK - 1:CHUNK]
                    kt = (k_t[rs, cs] * jnp.exp(bl - b_)).astype(BF16)
                    s_h = jnp.exp(bl) * s_h + _dot(kt, v_t[rs, cs].astype(BF16), NT)
                new_s.append(s_h)
            return tuple(new_s)

        zero = jnp.zeros((CHUNK, CHUNK), F32)
        lax.fori_loop(0, nt, fwd_tile, (zero, zero))

        def half_mean(v):
            m0 = jnp.sum(jnp.where(half, v, 0.0), axis=1, keepdims=True) * (1.0 / CHUNK)
            m1 = jnp.sum(jnp.where(half, 0.0, v), axis=1, keepdims=True) * (1.0 / CHUNK)
            return jnp.where(half, m0, m1)

        def bwd_tile(k, carry):
            ds0, ds1, dgn_acc, dlb_acc = carry
            i = nt - 1 - k
            r0 = pl.multiple_of(i * LANES, LANES)
            a = a_ref[pl.ds(r0, LANES), :]
            qa, z, ga = a[:, 0:128], a[:, 128:256], a[:, 384:512]
            q_t, k_t, v_t, b_t, (sg, sgn, fg) = t_layout(a)
            oraw = or_ref[pl.ds(r0, LANES), :]
            dout = do_ref[pl.ds(r0, LANES), :]
            r = lax.rsqrt(half_mean(oraw * oraw) + NORM_EPS)
            xn = oraw * r
            dga = dout * (xn * gn) * _dsilu(ga)
            don = dout * _silu(ga)
            dgn_acc = dgn_acc + jnp.sum(don * xn, axis=0, keepdims=True)
            dxn = don * gn
            do_t = (r * (dxn - xn * half_mean(dxn * xn))).T
            new_ds, dq_h, dk_h, dv_h, db_h = [], [], [], [], []
            for h in range(2):
                ds_h = (ds0, ds1)[h]
                rs = slice(CHUNK * h, CHUNK * (h + 1))
                qh, kh, vh, bh, doh = q_t[rs], k_t[rs], v_t[rs], b_t[rs], do_t[rs]
                dq_c, dk_c, dv_c, dbl_c = [None, None], [None, None], [None, None], [None, None]
                for c in (1, 0):
                    cs = slice(CHUNK * c, CHUNK * (c + 1))
                    s_n = s_sc[h, 2 * i + c]
                    b_ = bh[:, cs]
                    eb = jnp.exp(b_)
                    bl = b_[:, CHUNK - 1:CHUNK]
                    ek = jnp.exp(bl - b_)
                    ebl = jnp.exp(bl)
                    qt, kt = qh[:, cs] * eb, kh[:, cs] * ek
                    do_c = doh[:, cs].astype(BF16)
                    dsb = ds_h.astype(BF16)
                    dv_c[c] = _dot(dsb, kt.astype(BF16), TN)
                    dkt = _dot(dsb, vh[:, cs].astype(BF16), NN)
                    dqt = _dot(s_n.astype(BF16), do_c, NN)
                    dbl_c[c] = jnp.sum(ds_h * s_n, axis=1, keepdims=True) * ebl + jnp.sum(dkt * kt, axis=1, keepdims=True)
                    dq_c[c], dk_c[c] = dqt * eb, dkt * ek
                    ds_h = ebl * ds_h + _dot(qt.astype(BF16), do_c, NT)
                new_ds.append(ds_h)

                att0 = jnp.sum(qh * kh, axis=0, keepdims=True)
                datt0 = jnp.sum(doh * vh, axis=0, keepdims=True)
                dqh = jnp.concatenate(dq_c, axis=1) + datt0 * kh
                dkh = jnp.concatenate(dk_c, axis=1) + datt0 * qh
                dvh = jnp.concatenate(dv_c, axis=1) + att0 * doh
                for dlt in range(1, CHUNK):
                    kr, br, vr = pltpu.roll(kh, dlt, 1), pltpu.roll(bh, dlt, 1), pltpu.roll(vh, dlt, 1)
                    e = jnp.where(lane64 >= dlt, jnp.exp(jnp.minimum(bh - br, 0.0)), 0.0)
                    qe = qh * e
                    att = jnp.sum(qe * kr, axis=0, keepdims=True)
                    datt = jnp.sum(doh * vr, axis=0, keepdims=True)
                    dqh = dqh + datt * (kr * e)
                    dkh = dkh + pltpu.roll(datt * qe, LANES - dlt, 1)
                    dvh = dvh + pltpu.roll(att * doh, LANES - dlt, 1)
                dbl = jnp.where(half, dbl_c[0], dbl_c[1])
                db_h.append(qh * dqh - kh * dkh + jnp.where(lane64 == CHUNK - 1, dbl, 0.0))
                dq_h.append(dqh)
                dk_h.append(dkh)
                dv_h.append(dvh)
            dqq = jnp.concatenate(dq_h, axis=0).T
            dkk = jnp.concatenate(dk_h, axis=0).T
            dvv = jnp.concatenate(dv_h, axis=0).T
            dlf = _dot(jnp.concatenate(db_h, axis=0), umat, NT, precision=HI).T
            dqa = dqq * _dsilu(qa)
            dfg = jnp.where(fg > TINY, dlf / fg, 0.0)
            dz = (dfg - dkk) * (1.0 - lb) * sg * sgn
            dlb_acc = dlb_acc + jnp.sum(dfg * (1.0 - sg) - dkk * sgn, axis=0, keepdims=True)
            da_ref[pl.ds(r0, LANES), :] = jnp.concatenate([dqa, dz, dvv, dga], axis=1)
            return new_ds[0], new_ds[1], dgn_acc, dlb_acc

        zrow = jnp.zeros((1, LANES), F32)
        _, _, dgn_acc, dlb_acc = lax.fori_loop(0, nt, bwd_tile, (zero, zero, zrow, zrow))
        dgn_ref[...] = jnp.broadcast_to(dgn_acc, (8, LANES))
        dlb_ref[...] = jnp.broadcast_to(dlb_acc, (8, LANES))

    rows = jax.ShapeDtypeStruct((bsz, 8, HGRN_W), F32)
    return pl.pallas_call(
        body, grid=(bsz, 2),
        in_specs=[pl.BlockSpec((None, t, 512), lambda b, p: (b, 0, p)),
                  pl.BlockSpec((None, t, 128), lambda b, p: (b, 0, p)),
                  pl.BlockSpec((None, t, 128), lambda b, p: (b, 0, p)),
                  pl.BlockSpec((1, 128), lambda b, p: (0, p)),
                  pl.BlockSpec((1, 128), lambda b, p: (0, p))],
        out_specs=[pl.BlockSpec((None, t, 512), lambda b, p: (b, 0, p)),
                   pl.BlockSpec((None, 8, 128), lambda b, p: (b, 0, p)),
                   pl.BlockSpec((None, 8, 128), lambda b, p: (b, 0, p))],
        out_shape=[jax.ShapeDtypeStruct((bsz, t, A_W), F32), rows, rows],
        scratch_shapes=[pltpu.VMEM((2, nchunk, CHUNK, CHUNK), F32)],
        compiler_params=_cparams(("parallel", "parallel")), name=name)(proj3, o_raw, dmixed, lbs_row, gn_row)


N_LEVELS = 6


def _hgrn_tables():
    t = np.arange(LANES)
    j = np.arange(LANES)[None, :]
    same_chunk = (t[:, None] // CHUNK) == (j // CHUNK)
    w = np.zeros((2 + N_LEVELS, LANES, LANES), np.float32)
    w[0] = same_chunk & (j <= t[:, None])
    w[1] = same_chunk & (j > t[:, None])
    maskf = np.zeros((N_LEVELS, LANES, LANES), np.float32)
    rightf = np.zeros((N_LEVELS, LANES, LANES), np.float32)
    for li in range(N_LEVELS):
        m = (CHUNK // 2) >> li
        start = t - (t % (2 * m))
        right = (t % (2 * m)) >= m
        first = np.where(right, start + m, t + 1)
        last = np.where(right, t, start + m - 1)
        w[2 + li] = (j >= first[:, None]) & (j <= last[:, None])
        maskf[li] = (t[:, None] // (2 * m)) == (j // (2 * m))
        rightf[li] = right[:, None]
    return jnp.asarray(w.reshape(-1, LANES), BF16), jnp.asarray(maskf), jnp.asarray(rightf)


def _split(x, n):
    parts = []
    for _ in range(n - 1):
        p = x.astype(BF16)
        parts.append(p)
        x = x - p.astype(F32)
    parts.append(x.astype(BF16))
    return parts


def _exact_dot(w, parts):
    acc = jnp.dot(w, parts[0], preferred_element_type=F32)
    for p in parts[1:]:
        acc = acc + jnp.dot(w, p, preferred_element_type=F32)
    return acc


def _head_sums(v, ones_blk, n=2):
    parts = _split(v, n)
    acc = jnp.dot(parts[0], ones_blk, preferred_element_type=F32)
    for p in parts[1:]:
        acc = acc + jnp.dot(p, ones_blk, preferred_element_type=F32)
    return acc


def _hgrn_consts():
    r, c = _iota((LANES, LANES), 0), _iota((LANES, LANES), 1)
    eye = r == c
    ones_blk = ((r // CHUNK) == (c // CHUNK)).astype(BF16)
    return eye, ones_blk, jnp.ones((CHUNK, LANES), BF16)


def _hgrn_levels(qq, kk, zall, mk_ref, rt_ref, d_att=None):
    att = [jnp.zeros((LANES, LANES), F32)] * 2
    dq = dk = db = jnp.zeros((LANES, LANES), F32)
    for li in range(N_LEVELS):
        e = jnp.exp(zall[(2 + li) * LANES:(3 + li) * LANES])
        rt = rt_ref[li]
        mk = mk_ref[li]
        qef, kef = e * rt, e * (1.0 - rt)
        qe, ke = (qq * qef).astype(BF16), (kk * kef).astype(BF16)
        dqs, dks = [], []
        for h in range(2):
            hs = slice(CHUNK * h, CHUNK * (h + 1))
            att[h] = att[h] + _dot(qe[:, hs], ke[:, hs], NT) * mk
            if d_att is not None:
                dam = (d_att[h] * mk).astype(BF16)
                dqs.append(jnp.dot(dam, ke[:, hs], preferred_element_type=F32))
                dks.append(_dot(dam, qe[:, hs], TN))
        if d_att is not None:
            dqe, dke = jnp.concatenate(dqs, axis=1), jnp.concatenate(dks, axis=1)
            dq = dq + dqe * qef
            dk = dk + dke * kef
            db = db + (dqe * qe.astype(F32) - dke * ke.astype(F32))
    return att, dq, dk, db


def _hgrn_fwd(proj3, lbs_row, gn_row, name):
    bsz, t, _ = proj3.shape
    nt = t // LANES
    w_all, maskf, rightf = _hgrn_tables()

    def body(a_ref, lb_ref, gn_ref, w_ref, mk_ref, rt_ref, og_ref, or_ref, st_ref):
        lb = lb_ref[...]
        gn = gn_ref[...]
        eye, ones_blk, ones_h = _hgrn_consts()

        def tile(i, carry):
            r0 = pl.multiple_of(i * LANES, LANES)
            a = a_ref[pl.ds(r0, LANES), :]
            qq, kk, lf, _, _, _ = _hgrn_gates(a, lb)
            va, ga = a[:, 256:384], a[:, 384:512]
            parts = _split(lf, 3)
            zall = _exact_dot(w_ref[...], parts)
            eb, ee = jnp.exp(zall[0:LANES]), jnp.exp(zall[LANES:2 * LANES])
            vb = va.astype(BF16)
            att, _, _, _ = _hgrn_levels(qq, kk, zall, mk_ref, rt_ref)
            qk = _split(qq * kk, 2)
            qeb, keb = (qq * eb).astype(BF16), (kk * ee).astype(BF16)
            new_s, o_heads = [], []
            for h in range(2):
                hs = slice(CHUNK * h, CHUNK * (h + 1))
                diag = _exact_dot_r(qk, hs, ones_h)
                a_h = att[h] + jnp.where(eye, diag, 0.0)
                o_h = jnp.dot(a_h.astype(BF16), vb[:, hs], preferred_element_type=F32)
                st = carry[h]
                chunks = []
                for c in range(2):
                    rc = slice(CHUNK * c, CHUNK * (c + 1))
                    st_ref[h, 2 * i + c] = st
                    chunks.append(o_h[rc] + _dot(qeb[rc, hs], st.astype(BF16), NT))
                    ebl = eb[CHUNK * (c + 1) - 1:CHUNK * (c + 1), hs]
                    st = st * ebl + _dot(vb[rc, hs], keb[rc, hs], TN)
                new_s.append(st)
                o_heads.append(jnp.concatenate(chunks, axis=0))
            o = jnp.concatenate(o_heads, axis=1)
            ms = _head_sums(o * o, ones_blk) * (1.0 / CHUNK)
            or_ref[pl.ds(r0, LANES), :] = o
            og_ref[pl.ds(r0, LANES), :] = o * lax.rsqrt(ms + NORM_EPS) * gn * _silu(ga)
            return tuple(new_s)

        zero = jnp.zeros((CHUNK, CHUNK), F32)
        per_step = 4 if nt % 4 == 0 else 2

        def step(i, carry):
            for k in range(per_step):
                carry = tile(per_step * i + k, carry)
            return carry

        lax.fori_loop(0, nt // per_step, step, (zero, zero))

    out = jax.ShapeDtypeStruct((bsz, t, HGRN_W), F32)
    row = pl.BlockSpec((1, 128), lambda b, p: (0, p))
    return pl.pallas_call(
        body, grid=(bsz, 2),
        in_specs=[pl.BlockSpec((None, t, 512), lambda b, p: (b, 0, p)), row, row,
                  pl.BlockSpec(w_all.shape, lambda b, p: (0, 0)),
                  pl.BlockSpec(maskf.shape, lambda b, p: (0, 0, 0)),
                  pl.BlockSpec(rightf.shape, lambda b, p: (0, 0, 0))],
        out_specs=[pl.BlockSpec((None, t, 128), lambda b, p: (b, 0, p)),
                   pl.BlockSpec((None, t, 128), lambda b, p: (b, 0, p)),
                   pl.BlockSpec((None, 2, t // CHUNK, CHUNK, CHUNK), lambda b, p: (b, p, 0, 0, 0))],
        out_shape=[out, out, jax.ShapeDtypeStruct((bsz, 4, t // CHUNK, CHUNK, CHUNK), F32)],
        compiler_params=_cparams(("parallel", "parallel")), name=name)(proj3, lbs_row, gn_row, w_all, maskf, rightf)


def _exact_dot_r(parts, hs, ones_h):
    acc = jnp.dot(parts[0][:, hs], ones_h, preferred_element_type=F32)
    for p in parts[1:]:
        acc = acc + jnp.dot(p[:, hs], ones_h, preferred_element_type=F32)
    return acc


def _hgrn_bwd(proj3, o_raw, dmixed, states, lbs_row, gn_row, name):
    bsz, t, _ = proj3.shape
    nt = t // LANES
    nchunk = t // CHUNK
    w_all, maskf, rightf = _hgrn_tables()

    def body(a_ref, or_ref, do_ref, s_sc, lb_ref, gn_ref, w_ref, mk_ref, rt_ref, da_ref, dgn_ref, dlb_ref):
        lb = lb_ref[...]
        gn = gn_ref[...]
        eye, ones_blk, ones_h = _hgrn_consts()
        r_i, c_i = _iota((LANES, LANES), 0), _iota((LANES, LANES), 1)
        suffix = ((c_i >= r_i) & ((r_i // CHUNK) == (c_i // CHUNK))).astype(BF16)
        row64 = _iota((LANES, CHUNK), 0)
        ones_t = jnp.ones((LANES, CHUNK), BF16)
        zero = jnp.zeros((CHUNK, CHUNK), F32)

        def bwd_tile(k, carry):
            dst0, dst1, dgn_acc, dlb_acc = carry
            i = nt - 1 - k
            r0 = pl.multiple_of(i * LANES, LANES)
            a = a_ref[pl.ds(r0, LANES), :]
            qa, ga = a[:, 0:128], a[:, 384:512]
            qq, kk, lf, sg, sgn, fg = _hgrn_gates(a, lb)
            parts = _split(lf, 3)
            zall = _exact_dot(w_ref[...], parts)
            eb, ee = jnp.exp(zall[0:LANES]), jnp.exp(zall[LANES:2 * LANES])
            vb = a[:, 256:384].astype(BF16)
            oraw = or_ref[pl.ds(r0, LANES), :]
            dout = do_ref[pl.ds(r0, LANES), :]
            r = lax.rsqrt(_head_sums(oraw * oraw, ones_blk) * (1.0 / CHUNK) + NORM_EPS)
            xn = oraw * r
            dga = dout * (xn * gn) * _dsilu(ga)
            don = dout * _silu(ga)
            dgn_acc = dgn_acc + jnp.sum(don * xn, axis=0, keepdims=True)
            dxn = don * gn
            do = r * (dxn - xn * (_head_sums(dxn * xn, ones_blk) * (1.0 / CHUNK)))
            dob = do.astype(BF16)
            d_att = [_dot(dob[:, CHUNK * h:CHUNK * (h + 1)], vb[:, CHUNK * h:CHUNK * (h + 1)], NT) for h in range(2)]
            att, dq, dk, db_lv = _hgrn_levels(qq, kk, zall, mk_ref, rt_ref, d_att)
            qk = _split(qq * kk, 2)
            qe_f, ke_f = qq * eb, kk * ee
            qeb, keb = qe_f.astype(BF16), ke_f.astype(BF16)
            new_ds, dq_h, dk_h, dv_h, dbl_h = [], [], [], [], []
            for h in range(2):
                hs = slice(CHUNK * h, CHUNK * (h + 1))
                a_h = att[h] + jnp.where(eye, _exact_dot_r(qk, hs, ones_h), 0.0)
                dv = _dot(a_h.astype(BF16), dob[:, hs], TN)
                ddiag = _exact_dot_r(_split(jnp.where(eye, d_att[h], 0.0), 2), slice(None), ones_t)
                dq_i = dq[:, hs] + ddiag * kk[:, hs]
                dk_i = dk[:, hs] + ddiag * qq[:, hs]
                dst = (dst0, dst1)[h]
                dq_c, dk_c, dv_c, dbl_c = [None, None], [None, None], [None, None], [None, None]
                for c in (1, 0):
                    rc = slice(CHUNK * c, CHUNK * (c + 1))
                    st_n = s_sc[h, 2 * i + c]
                    ebl = eb[CHUNK * (c + 1) - 1:CHUNK * (c + 1), hs]
                    dstb = dst.astype(BF16)
                    dv_c[c] = _dot(keb[rc, hs], dstb, NT)
                    dke = jnp.dot(vb[rc, hs], dstb, preferred_element_type=F32)
                    dqe = jnp.dot(dob[rc, hs], st_n.astype(BF16), preferred_element_type=F32)
                    dbl_c[c] = (jnp.sum(dst * st_n, axis=0, keepdims=True) * ebl
                                + jnp.sum(dke * ke_f[rc, hs], axis=0, keepdims=True))
                    dq_c[c], dk_c[c] = dqe * eb[rc, hs], dke * ee[rc, hs]
                    dst = dst * ebl + _dot(dob[rc, hs], qeb[rc, hs], TN)
                new_ds.append(dst)
                dq_x, dk_x = jnp.concatenate(dq_c, axis=0), jnp.concatenate(dk_c, axis=0)
                dq_h.append(dq_i + dq_x)
                dk_h.append(dk_i + dk_x)
                dv_h.append(dv + jnp.concatenate(dv_c, axis=0))
                dbl_h.append(qq[:, hs] * dq_x - kk[:, hs] * dk_x
                             + jnp.where(row64 == CHUNK - 1, dbl_c[0], 0.0) + jnp.where(row64 == LANES - 1, dbl_c[1], 0.0))
            dqq = jnp.concatenate(dq_h, axis=1)
            dkk = jnp.concatenate(dk_h, axis=1)
            dvv = jnp.concatenate(dv_h, axis=1)
            db = db_lv + jnp.concatenate(dbl_h, axis=1)
            dlf = _exact_dot(suffix, _split(db, 3))
            dqa = dqq * _dsilu(qa)
            dfg = jnp.where(fg > TINY, dlf / fg, 0.0)
            dz = (dfg - dkk) * (1.0 - lb) * sg * sgn
            dlb_acc = dlb_acc + jnp.sum(dfg * (1.0 - sg) - dkk * sgn, axis=0, keepdims=True)
            da_ref[pl.ds(r0, LANES), :] = jnp.concatenate([dqa, dz, dvv, dga], axis=1)
            return new_ds[0], new_ds[1], dgn_acc, dlb_acc

        zrow = jnp.zeros((1, LANES), F32)
        per_step = 4 if nt % 4 == 0 else 2

        def step(k, carry):
            for r in range(per_step):
                carry = bwd_tile(per_step * k + r, carry)
            return carry

        _, _, dgn_acc, dlb_acc = lax.fori_loop(0, nt // per_step, step, (zero, zero, zrow, zrow))
        dgn_ref[...] = jnp.broadcast_to(dgn_acc, (8, LANES))
        dlb_ref[...] = jnp.broadcast_to(dlb_acc, (8, LANES))

    rows = jax.ShapeDtypeStruct((bsz, 8, HGRN_W), F32)
    row = pl.BlockSpec((1, 128), lambda b, p: (0, p))
    blk = pl.BlockSpec((None, t, 128), lambda b, p: (b, 0, p))
    return pl.pallas_call(
        body, grid=(bsz, 2),
        in_specs=[pl.BlockSpec((None, t, 512), lambda b, p: (b, 0, p)), blk, blk,
                  pl.BlockSpec((None, 2, nchunk, CHUNK, CHUNK), lambda b, p: (b, p, 0, 0, 0)), row, row,
                  pl.BlockSpec(w_all.shape, lambda b, p: (0, 0)),
                  pl.BlockSpec(maskf.shape, lambda b, p: (0, 0, 0)),
                  pl.BlockSpec(rightf.shape, lambda b, p: (0, 0, 0))],
        out_specs=[pl.BlockSpec((None, t, 512), lambda b, p: (b, 0, p)),
                   pl.BlockSpec((None, 8, 128), lambda b, p: (b, 0, p)),
                   pl.BlockSpec((None, 8, 128), lambda b, p: (b, 0, p))],
        out_shape=[jax.ShapeDtypeStruct((bsz, t, A_W), F32), rows, rows],
        compiler_params=_cparams(("parallel", "parallel")), name=name)(
            proj3, o_raw, dmixed, states, lbs_row, gn_row, w_all, maskf, rightf)


def _pool_tt(t):
    return min(256, t)


def _window_select(s2, s4, s8, s16, lane):
    return jnp.where(lane < 64, s2, jnp.where(lane < 128, s4, jnp.where(lane < 192, s8, s16)))


def _pool_counts(t0, tt):
    lane = _iota((tt, POOL_W), 1)
    tpos = (_iota((tt, POOL_W), 0) + t0 + 1).astype(F32)
    win = jnp.where(lane < 64, 2.0, jnp.where(lane < 128, 4.0, jnp.where(lane < 192, 8.0, 16.0)))
    return 1.0 / jnp.minimum(tpos, win), lane


def _pooled_tile(upad_ref, i, tt):
    r0 = pl.multiple_of(i * tt, 8)
    cat = upad_ref[pl.ds(r0, tt + POOL_HALO), :]
    s2 = cat + pltpu.roll(cat, 1, 0)
    s4 = s2 + pltpu.roll(s2, 2, 0)
    s8 = s4 + pltpu.roll(s4, 4, 0)
    s16 = s8 + pltpu.roll(s8, 8, 0)
    inv, lane = _pool_counts(i * tt, tt)
    sel = _window_select(s2[POOL_HALO:], s4[POOL_HALO:], s8[POOL_HALO:], s16[POOL_HALO:], lane)
    return sel * inv - cat[POOL_HALO:], inv, lane


def _pool_fwd(proj3, wbd, scale_row, name):
    bsz, t, _ = proj3.shape
    tt = _pool_tt(t)

    def body(p_ref, w_ref, sc_ref, o_ref, upad):
        upad[0:POOL_HALO, :] = jnp.zeros((POOL_HALO, POOL_W), F32)
        upad[POOL_HALO:, :] = p_ref[:, 0:POOL_W]
        w = w_ref[...]
        sc = sc_ref[...]

        def tile(i, c):
            pooled, _, _ = _pooled_tile(upad, i, tt)
            r0 = pl.multiple_of(i * tt, 8)
            g = p_ref[pl.ds(r0, tt), POOL_W:2 * POOL_W]
            pre = jnp.dot(pooled.astype(BF16), w, preferred_element_type=F32)
            o_ref[pl.ds(r0, tt), :] = pre * sc * _silu(g)
            return c

        lax.fori_loop(0, t // tt, tile, 0)

    return pl.pallas_call(
        body, grid=(bsz,),
        in_specs=[pl.BlockSpec((None, t, 512), lambda b: (b, 0, B_BLK)),
                  pl.BlockSpec((POOL_W, POOL_W), lambda b: (0, 0)),
                  pl.BlockSpec((1, POOL_W), lambda b: (0, 0))],
        out_specs=pl.BlockSpec((None, t, POOL_W), lambda b: (b, 0, 0)),
        out_shape=jax.ShapeDtypeStruct((bsz, t, POOL_W), F32),
        scratch_shapes=[pltpu.VMEM((t + POOL_HALO, POOL_W), F32)],
        compiler_params=_cparams(("parallel",)), name=name)(proj3, wbd, scale_row)


def _pool_bwd(proj3, dmixed, wbd, scale_row, name):
    bsz, t, _ = proj3.shape
    tt = _pool_tt(t)

    def body(p_ref, do_ref, w_ref, sc_ref, db_ref, dsc_ref, dw_ref, upad, epad):
        upad[0:POOL_HALO, :] = jnp.zeros((POOL_HALO, POOL_W), F32)
        upad[POOL_HALO:, :] = p_ref[:, 0:POOL_W]
        epad[t:, :] = jnp.zeros((POOL_HALO, POOL_W), F32)
        w = w_ref[...]
        sc = sc_ref[...]

        def tile(i, carry):
            dsc_acc, dw_acc = carry
            pooled, inv, _ = _pooled_tile(upad, i, tt)
            r0 = pl.multiple_of(i * tt, 8)
            g = p_ref[pl.ds(r0, tt), POOL_W:2 * POOL_W]
            dout = do_ref[pl.ds(r0, tt), :]
            pb = pooled.astype(BF16)
            pre = jnp.dot(pb, w, preferred_element_type=F32)
            t1 = dout * _silu(g)
            dsc_acc = dsc_acc + jnp.sum(t1 * pre, axis=0, keepdims=True)
            dpre = (t1 * sc).astype(BF16)
            db_ref[pl.ds(r0, tt), POOL_W:2 * POOL_W] = dout * pre * sc * _dsilu(g)
            dw_acc = dw_acc + _dot(pb, dpre, TN)
            dpooled = _dot(dpre, w, NT)
            epad[pl.ds(r0, tt), :] = dpooled * inv
            return dsc_acc, dw_acc

        dsc_acc, dw_acc = lax.fori_loop(0, t // tt, tile, (jnp.zeros((1, POOL_W), F32), jnp.zeros((POOL_W, POOL_W), F32)))
        dsc_ref[...] = jnp.broadcast_to(dsc_acc, (8, POOL_W))
        dw_ref[...] = dw_acc

        def tile2(i, c):
            r0 = pl.multiple_of(i * tt, 8)
            n = tt + POOL_HALO
            cat = epad[pl.ds(r0, n), :]
            s2 = cat + pltpu.roll(cat, n - 1, 0)
            s4 = s2 + pltpu.roll(s2, n - 2, 0)
            s8 = s4 + pltpu.roll(s4, n - 4, 0)
            s16 = s8 + pltpu.roll(s8, n - 8, 0)
            inv, lane = _pool_counts(i * tt, tt)
            sel = _window_select(s2[:tt], s4[:tt], s8[:tt], s16[:tt], lane)
            db_ref[pl.ds(r0, tt), 0:POOL_W] = sel - cat[:tt] / inv
            return c

        lax.fori_loop(0, t // tt, tile2, 0)

    return pl.pallas_call(
        body, grid=(bsz,),
        in_specs=[pl.BlockSpec((None, t, 512), lambda b: (b, 0, B_BLK)),
                  pl.BlockSpec((None, t, POOL_W), lambda b: (b, 0, 1)),
                  pl.BlockSpec((POOL_W, POOL_W), lambda b: (0, 0)),
                  pl.BlockSpec((1, POOL_W), lambda b: (0, 0))],
        out_specs=[pl.BlockSpec((None, t, 512), lambda b: (b, 0, 0)),
                   pl.BlockSpec((None, 8, POOL_W), lambda b: (b, 0, 0)),
                   pl.BlockSpec((None, POOL_W, POOL_W), lambda b: (b, 0, 0))],
        out_shape=[jax.ShapeDtypeStruct((bsz, t, B_W), F32), jax.ShapeDtypeStruct((bsz, 8, POOL_W), F32),
                   jax.ShapeDtypeStruct((bsz, POOL_W, POOL_W), F32)],
        scratch_shapes=[pltpu.VMEM((t + POOL_HALO, POOL_W), F32), pltpu.VMEM((t + POOL_HALO, POOL_W), F32)],
        compiler_params=_cparams(("parallel",)), name=name)(proj3, dmixed, wbd, scale_row)


def _head_select_rows(hp):
    r, c = _iota((8, LANES), 0), _iota((8, LANES), 1)
    return ((r < 2) & (c == 2 * hp + r)).astype(F32)


def _foxgate_fwd(proj3, bias_row, name):
    bsz, t, _ = proj3.shape
    nt = t // LANES

    def body(f_ref, b_ref, cn_ref, ct_ref):
        bias = b_ref[...]
        i, j = _iota((LANES, LANES), 0), _iota((LANES, LANES), 1)
        lower = (j <= i).astype(BF16)
        spread = (_iota((LANES, FOX_W), 0) == _iota((LANES, FOX_W), 1) // 64).astype(BF16)
        select = [_head_select_rows(hp).astype(BF16) for hp in range(4)]
        offset = jnp.zeros((1, LANES), F32)
        for k in range(nt):
            rows = slice(k * LANES, (k + 1) * LANES)
            xg = f_ref[rows, :] + bias
            lf = jnp.minimum(xg, 0.0) - jnp.log(1.0 + jnp.exp(-jnp.abs(xg)))
            c = _exact_dot(lower, _split(lf, 3)) + offset
            offset = c[LANES - 1:LANES, :]
            parts = _split(c, 3)
            cn_ref[rows, :] = _head_sums(c, spread, 3)
            for hp in range(4):
                acc = _dot(select[hp], parts[0], NT)
                for p in parts[1:]:
                    acc = acc + _dot(select[hp], p, NT)
                ct_ref[hp, :, rows] = acc

    return pl.pallas_call(
        body, grid=(bsz,),
        in_specs=[pl.BlockSpec((None, t, 128), lambda b: (b, 0, F_BLK)), pl.BlockSpec((1, 128), lambda b: (0, 0))],
        out_specs=[pl.BlockSpec((None, t, FOX_W), lambda b: (b, 0, 0)),
                   pl.BlockSpec((None, 4, 8, t), lambda b: (b, 0, 0, 0))],
        out_shape=[jax.ShapeDtypeStruct((bsz, t, FOX_W), F32), jax.ShapeDtypeStruct((bsz, 4, 8, t), F32)],
        compiler_params=_cparams(("parallel",)), name=name)(proj3, bias_row)


def _foxgate_bwd(proj3, dc_nat, bias_row, name):
    bsz, t, _ = proj3.shape
    nt = t // LANES

    def body(f_ref, dc_ref, b_ref, df_ref, dbias_ref, run_sc):
        bias = b_ref[...]
        i, j = _iota((LANES, LANES), 0), _iota((LANES, LANES), 1)
        upper = (j >= i).astype(F32)
        valid = _iota((1, LANES), 1) < FOX_HEADS
        run_sc[...] = jnp.zeros((8, LANES), F32)
        dbias_ref[...] = jnp.zeros((8, LANES), F32)

        def tile(k, c):
            r0 = pl.multiple_of((nt - 1 - k) * LANES, LANES)
            dc = dc_ref[pl.ds(r0, LANES), :] + jnp.where(i == LANES - 1, run_sc[0:1, :], 0.0)
            dlf = jnp.dot(upper, dc, precision=HI, preferred_element_type=F32)
            xg = f_ref[pl.ds(r0, LANES), :] + bias
            df = jnp.where(valid, dlf * _sig(-xg), 0.0)
            df_ref[pl.ds(r0, LANES), :] = df
            run_sc[...] = dlf[0:8, :]
            dbias_ref[...] += jnp.sum(df, axis=0, keepdims=True)
            return c

        lax.fori_loop(0, nt, tile, 0)

    blk = pl.BlockSpec((None, t, 128), lambda b: (b, 0, 0))
    return pl.pallas_call(
        body, grid=(bsz,),
        in_specs=[pl.BlockSpec((None, t, 128), lambda b: (b, 0, F_BLK)), blk, pl.BlockSpec((1, 128), lambda b: (0, 0))],
        out_specs=[blk, pl.BlockSpec((None, 8, 128), lambda b: (b, 0, 0))],
        out_shape=[jax.ShapeDtypeStruct((bsz, t, F_W), F32), jax.ShapeDtypeStruct((bsz, 8, 128), F32)],
        scratch_shapes=[pltpu.VMEM((8, LANES), F32)],
        compiler_params=_cparams(("parallel",)), name=name)(proj3, dc_nat, bias_row)


def _fox_tile(t):
    return min(256, t)


def _fox_fwd(proj3, c_nat, c_t, name):
    bsz, t, _ = proj3.shape
    tq = tk = min(2 * _fox_tile(t), t)
    nq = t // tq

    def body(q_ref, kv_ref, cn_ref, ct_ref, og_ref, or_ref, lse_ref):
        i = pl.program_id(2)
        qblk = q_ref[...]
        first = _iota((1, 128), 1) < 64
        qv = qblk[:, 0:128] * 0.125
        qm = [jnp.where(first, qv, 0.0).astype(BF16), jnp.where(first, 0.0, qv).astype(BF16)]
        cqs = [cn_ref[:, 0:1], cn_ref[:, 64:65]]
        rows = _iota((tq, tk), 0) + i * tq

        def scores(j):
            c0 = pl.multiple_of(j * tk, tk)
            kb = kv_ref[pl.ds(c0, tk), 128:256].astype(BF16)
            return tuple(_dot(qm[h], kb, NT) + (cqs[h] - ct_ref[h:h + 1, pl.ds(c0, tk)]) for h in range(2))

        def absorb(j, state, s01, masked):
            c0 = pl.multiple_of(j * tk, tk)
            vblk = kv_ref[pl.ds(c0, tk), 256:384]
            vx = [jnp.where(first, vblk, 1.0).astype(BF16), jnp.where(first, 1.0, vblk).astype(BF16)]
            new = []
            for h in range(2):
                m, acc, s = state[2 * h], state[2 * h + 1], s01[h]
                if masked:
                    s = jnp.where(rows >= _iota((tq, tk), 1) + j * tk, s, MASK_VALUE)
                m_new = jnp.maximum(m, jnp.max(s, axis=1, keepdims=True))
                p = jnp.exp(s - m_new).astype(BF16)
                new += [m_new, jnp.exp(m - m_new) * acc + jnp.dot(p, vx[h], preferred_element_type=F32)]
            return tuple(new)

        init = (jnp.full((tq, 1), MASK_VALUE, F32), jnp.zeros((tq, 128), F32)) * 2
        n_full = (i * tq) // tk
        state = lax.fori_loop(0, n_full, lambda j, state: absorb(j, state, scores(j), False), init)
        m0, acc0, m1, acc1 = absorb(n_full, state, scores(n_full), True)
        l0, l1 = pltpu.roll(acc0, 64, 1), pltpu.roll(acc1, 64, 1)
        o = jnp.where(first, acc0 / l0, acc1 / l1)
        or_ref[...] = o
        og_ref[...] = o * _silu(qblk[:, 384:512])
        lse_ref[...] = jnp.where(first, m0 + jnp.log(l0), m1 + jnp.log(l1))

    out = jax.ShapeDtypeStruct((bsz, t, FOX_W), F32)
    blk = pl.BlockSpec((None, tq, 128), lambda b, p, i: (b, i, p))
    return pl.pallas_call(
        body, grid=(bsz, 4, nq),
        in_specs=[pl.BlockSpec((None, tq, 512), lambda b, p, i: (b, i, C_BLK0 + p)),
                  pl.BlockSpec((None, t, 512), lambda b, p, i: (b, 0, C_BLK0 + p)),
                  blk,
                  pl.BlockSpec((None, None, 8, t), lambda b, p, i: (b, p, 0, 0))],
        out_specs=[blk, blk, blk],
        out_shape=[out, out, out],
        compiler_params=_cparams(("parallel", "parallel", "arbitrary")), name=name)(proj3, proj3, c_nat, c_t)


def _fox_bwd(proj3, o_raw, dmixed, lse, c_nat, c_t, name):
    bsz, t, _ = proj3.shape
    tq = tk = min(2 * _fox_tile(t), t)
    nq = t // tq
    ratio = tk // tq

    def body(a_ref, or_ref, do_ref, lse_ref, cn_ref, ct_ref, dc_out, dct_out, drow_out, dq_sc, do_sc, dl_sc):
        def prep(i, c):
            r0 = pl.multiple_of(i * tq, tq)
            g = a_ref[pl.ds(r0, tq), 384:512]
            dout = do_ref[pl.ds(r0, tq), :]
            o = or_ref[pl.ds(r0, tq), :]
            dc_out[pl.ds(r0, tq), 384:512] = dout * o * _dsilu(g)
            do = dout * _silu(g)
            do_sc[pl.ds(r0, tq), :] = do
            prod = do * o
            d0 = jnp.sum(prod[:, 0:64], axis=1, keepdims=True)
            d1 = jnp.sum(prod[:, 64:128], axis=1, keepdims=True)
            dl_sc[pl.ds(r0, tq), :] = jnp.concatenate([jnp.broadcast_to(d0, (tq, 64)), jnp.broadcast_to(d1, (tq, 64))], axis=1)
            dq_sc[pl.ds(r0, tq), :] = jnp.zeros((tq, 128), F32)
            drow_out[pl.ds(r0, tq), :] = jnp.zeros((tq, 128), F32)
            return c

        lax.fori_loop(0, nq, prep, 0)
        dct_out[...] = jnp.zeros((8, t), F32)

        first = _iota((1, 128), 1) < 64

        def heads(v):
            return [jnp.where(first, v, 0.0).astype(BF16), jnp.where(first, 0.0, v).astype(BF16)]

        def kv_tile(j, c):
            c0 = pl.multiple_of(j * tk, tk)
            kb = a_ref[pl.ds(c0, tk), 128:256].astype(BF16)
            vb = a_ref[pl.ds(c0, tk), 256:384].astype(BF16)
            cks = [ct_ref[h:h + 1, pl.ds(c0, tk)] for h in range(2)]

            def q_step(i, carry, diagonal):
                dk, dv, dcol0, dcol1 = carry
                r0 = pl.multiple_of(i * tq, tq)
                causal = _iota((tq, tk), 0) + i * tq >= _iota((tq, tk), 1) + j * tk
                qv = a_ref[pl.ds(r0, tq), 0:128] * 0.125
                do = do_sc[pl.ds(r0, tq), :]
                qb, dob = qv.astype(BF16), do.astype(BF16)
                qm, dom = heads(qv), heads(do)
                full, dcols, rsums = [], [], []
                for h in range(2):
                    lse_h = lse_ref[pl.ds(r0, tq), 64 * h:64 * h + 1]
                    dl_h = dl_sc[pl.ds(r0, tq), 64 * h:64 * h + 1]
                    cq = cn_ref[pl.ds(r0, tq), 64 * h:64 * h + 1]
                    p = jnp.exp(_dot(qm[h], kb, NT) + (cq - cks[h]) - lse_h)
                    if diagonal:
                        p = jnp.where(causal, p, 0.0)
                    ds = p * (_dot(dom[h], vb, NT) - dl_h)
                    dsb = ds.astype(BF16)
                    full.append((_dot(p.astype(BF16), dob, TN), _dot(dsb, qb, TN),
                                 jnp.dot(dsb, kb, preferred_element_type=F32)))
                    dcols.append(jnp.sum(ds, axis=0, keepdims=True))
                    rsums.append(jnp.broadcast_to(jnp.sum(ds, axis=1, keepdims=True), (tq, 128)))
                dq_sc[pl.ds(r0, tq), :] += jnp.where(first, full[0][2], full[1][2]) * 0.125
                drow_out[pl.ds(r0, tq), :] += jnp.where(first, rsums[0], rsums[1])
                return (dk + jnp.where(first, full[0][1], full[1][1]), dv + jnp.where(first, full[0][0], full[1][0]),
                        dcol0 - dcols[0], dcol1 - dcols[1])

            carry = (jnp.zeros((tk, 128), F32), jnp.zeros((tk, 128), F32), jnp.zeros((1, tk), F32), jnp.zeros((1, tk), F32))
            for r in range(ratio):
                carry = q_step(ratio * j + r, carry, True)
            dk, dv, dcol0, dcol1 = lax.fori_loop(ratio * (j + 1), nq, functools.partial(q_step, diagonal=False), carry)
            dct_out[0:1, pl.ds(c0, tk)] = dcol0
            dct_out[1:2, pl.ds(c0, tk)] = dcol1
            dc_out[pl.ds(c0, tk), 128:256] = dk
            dc_out[pl.ds(c0, tk), 256:384] = dv
            return c

        lax.fori_loop(0, t // tk, kv_tile, 0)
        dc_out[:, 0:128] = dq_sc[...]

    blk = pl.BlockSpec((None, t, 128), lambda b, p: (b, 0, p))
    return pl.pallas_call(
        body, grid=(bsz, 4),
        in_specs=[pl.BlockSpec((None, t, 512), lambda b, p: (b, 0, C_BLK0 + p)),
                  blk,
                  pl.BlockSpec((None, t, 128), lambda b, p: (b, 0, 4 + p)),
                  blk, blk,
                  pl.BlockSpec((None, None, 8, t), lambda b, p: (b, p, 0, 0))],
        out_specs=[pl.BlockSpec((None, t, 512), lambda b, p: (b, 0, p)),
                   pl.BlockSpec((None, None, 8, t), lambda b, p: (b, p, 0, 0)), blk],
        out_shape=[jax.ShapeDtypeStruct((bsz, t, C_W), F32), jax.ShapeDtypeStruct((bsz, 4, 8, t), F32),
                   jax.ShapeDtypeStruct((bsz, t, FOX_W), F32)],
        scratch_shapes=[pltpu.VMEM((t, 128), F32), pltpu.VMEM((t, 128), F32), pltpu.VMEM((t, 128), F32)],
        compiler_params=_cparams(("parallel", "parallel")), name=name)(proj3, o_raw, dmixed, lse, c_nat, c_t)


def _mix_tm(n):
    return min(512, n)


def _outproj_fwd(x2, oa, ob, oc, wo, g_row, name):
    n, d = x2.shape
    tm = _mix_tm(n)

    def body(x_ref, oa_ref, ob_ref, oc_ref, w_ref, g_ref, y_ref, xo_ref):
        y = (jnp.dot(oa_ref[...].astype(BF16), w_ref[0:256, :], preferred_element_type=F32)
             + jnp.dot(ob_ref[...].astype(BF16), w_ref[256:512, :], preferred_element_type=F32)
             + jnp.dot(oc_ref[...].astype(BF16), w_ref[512:1024, :], preferred_element_type=F32))
        y_ref[...] = y
        xo_ref[...] = x_ref[...] + y * _rstd(y) * g_ref[...]

    row = lambda w: pl.BlockSpec((tm, w), lambda i: (i, 0))
    out = jax.ShapeDtypeStruct((n, d), F32)
    return pl.pallas_call(
        body, grid=(n // tm,),
        in_specs=[row(d), row(256), row(256), row(512), pl.BlockSpec((d, d), lambda i: (0, 0)),
                  pl.BlockSpec((1, d), lambda i: (0, 0))],
        out_specs=[row(d), row(d)], out_shape=[out, out],
        compiler_params=_cparams(("parallel",)), name=name)(x2, oa, ob, oc, wo, g_row)


def _outproj_fwd_loss(x2, oa, ob, oc, wo, g_row, target2, name):
    n, d = x2.shape
    tm = _mix_tm(n)

    def body(x_ref, oa_ref, ob_ref, oc_ref, w_ref, g_ref, t_ref, y_ref, dx_ref, l_ref):
        y = (jnp.dot(oa_ref[...].astype(BF16), w_ref[0:256, :], preferred_element_type=F32)
             + jnp.dot(ob_ref[...].astype(BF16), w_ref[256:512, :], preferred_element_type=F32)
             + jnp.dot(oc_ref[...].astype(BF16), w_ref[512:1024, :], preferred_element_type=F32))
        y_ref[...] = y
        err = (x_ref[...] + y * _rstd(y) * g_ref[...]) - t_ref[...]
        dx_ref[...] = err * (1.0 / d)

        @pl.when(pl.program_id(0) == 0)
        def _():
            l_ref[...] = jnp.zeros((8, 128), F32)

        l_ref[...] += jnp.sum(err * err)

    row = lambda w: pl.BlockSpec((tm, w), lambda i: (i, 0))
    out = jax.ShapeDtypeStruct((n, d), F32)
    return pl.pallas_call(
        body, grid=(n // tm,),
        in_specs=[row(d), row(256), row(256), row(512), pl.BlockSpec((d, d), lambda i: (0, 0)),
                  pl.BlockSpec((1, d), lambda i: (0, 0)), row(d)],
        out_specs=[row(d), row(d), pl.BlockSpec((8, 128), lambda i: (0, 0))],
        out_shape=[out, out, jax.ShapeDtypeStruct((8, 128), F32)],
        compiler_params=_cparams(("arbitrary",)), name=name)(x2, oa, ob, oc, wo, g_row, target2)


def _outproj_bwd(dxo, y, oa, ob, oc, wo, g_row, name):
    n, d = dxo.shape
    tm = _mix_tm(n)

    def body(dx_ref, y_ref, oa_ref, ob_ref, oc_ref, w_ref, g_ref, dm_ref, dw_ref, dg_ref):
        @pl.when(pl.program_id(0) == 0)
        def _():
            dw_ref[...] = jnp.zeros((d, d), F32)
            dg_ref[...] = jnp.zeros((8, d), F32)

        yv, dx = y_ref[...], dx_ref[...]
        r = _rstd(yv)
        yn = yv * r
        dg_ref[...] += jnp.sum(dx * yn, axis=0, keepdims=True)
        dyn = dx * g_ref[...]
        dy = (r * (dyn - yn * jnp.mean(dyn * yn, axis=-1, keepdims=True))).astype(BF16)
        dm_ref[...] = _dot(dy, w_ref[...], NT)
        dw_ref[0:256, :] += _dot(oa_ref[...].astype(BF16), dy, TN)
        dw_ref[256:512, :] += _dot(ob_ref[...].astype(BF16), dy, TN)
        dw_ref[512:1024, :] += _dot(oc_ref[...].astype(BF16), dy, TN)

    row = lambda w: pl.BlockSpec((tm, w), lambda i: (i, 0))
    fixed = lambda r, c: pl.BlockSpec((r, c), lambda i: (0, 0))
    return pl.pallas_call(
        body, grid=(n // tm,),
        in_specs=[row(d), row(d), row(256), row(256), row(512), fixed(d, d), fixed(1, d)],
        out_specs=[row(d), fixed(d, d), fixed(8, d)],
        out_shape=[jax.ShapeDtypeStruct((n, d), F32), jax.ShapeDtypeStruct((d, d), F32), jax.ShapeDtypeStruct((8, d), F32)],
        compiler_params=_cparams(("arbitrary",)), name=name)(dxo, y, oa, ob, oc, wo, g_row)


_PIECES = ((0, A_W), (A_W, B_W), (A_W + B_W, C_W), (A_W + B_W + C_W, F_W))


def _inproj_bwd_x(x2, dxo, g_row, w_int, pieces, name):
    n, d = x2.shape
    tm = min(256, n)

    def body(x_ref, dxo_ref, g_ref, w_ref, da_ref, db_ref, dc_ref, df_ref, dx_ref, dg_ref):
        @pl.when(pl.program_id(0) == 0)
        def _():
            dg_ref[...] = jnp.zeros((8, d), F32)

        dh = jnp.zeros((tm, d), F32)
        for ref, (o, w) in zip((da_ref, db_ref, dc_ref, df_ref), _PIECES):
            dh = dh + _dot(ref[...].astype(BF16), w_ref[:, o:o + w], NT)
        x = x_ref[...]
        r = _rstd(x)
        xn = x * r
        dg_ref[...] += jnp.sum(dh * xn, axis=0, keepdims=True)
        dxn = dh * g_ref[...]
        dx_ref[...] = dxo_ref[...] + r * (dxn - xn * jnp.mean(dxn * xn, axis=-1, keepdims=True))

    row = lambda w: pl.BlockSpec((tm, w), lambda i: (i, 0))
    fixed = lambda r, c: pl.BlockSpec((r, c), lambda i: (0, 0))
    return pl.pallas_call(
        body, grid=(n // tm,),
        in_specs=[row(d), row(d), fixed(1, d), fixed(d, E_INT)] + [row(w) for _, w in _PIECES],
        out_specs=[row(d), fixed(8, d)],
        out_shape=[jax.ShapeDtypeStruct((n, d), F32), jax.ShapeDtypeStruct((8, d), F32)],
        compiler_params=_cparams(("arbitrary",)), name=name)(x2, dxo, g_row, w_int, *pieces)


def _inproj_bwd_w(x2, g_row, pieces, name):
    n, d = x2.shape
    tm = min(256, n)

    def body(x_ref, g_ref, da_ref, db_ref, dc_ref, df_ref, dw_ref):
        @pl.when(pl.program_id(0) == 0)
        def _():
            dw_ref[...] = jnp.zeros((d, E_INT), F32)

        x = x_ref[...]
        h = (x * _rstd(x) * g_ref[...]).astype(BF16)
        for ref, (o, w) in zip((da_ref, db_ref, dc_ref, df_ref), _PIECES):
            dw_ref[:, o:o + w] += _dot(h, ref[...].astype(BF16), TN)

    row = lambda w: pl.BlockSpec((tm, w), lambda i: (i, 0))
    return pl.pallas_call(
        body, grid=(n // tm,),
        in_specs=[row(d), pl.BlockSpec((1, d), lambda i: (0, 0))] + [row(w) for _, w in _PIECES],
        out_specs=pl.BlockSpec((d, E_INT), lambda i: (0, 0)),
        out_shape=jax.ShapeDtypeStruct((d, E_INT), F32),
        compiler_params=_cparams(("arbitrary",), vmem_mb=56), name=name)(x2, g_row, *pieces)


def _block_diag(pool_w_l):
    z = jnp.zeros((64, 64), pool_w_l.dtype)
    return jnp.concatenate(
        [jnp.concatenate([pool_w_l[g] if c == g else z for c in range(4)], axis=1) for g in range(4)], axis=0)


def _pad_lanes(v, width=128):
    return jnp.pad(v, ((0, 0),) * (v.ndim - 1) + ((0, width - v.shape[-1]),))


def _local_step(x, target, lower_bounds, pre_norm_g, w_in_int, hgrn_norm_g, fox_f_bias, pool_w, pool_scale,
                w_out_bf, post_norm_g, on_weight_grads):
    bsz, t, d = x.shape
    n = bsz * t
    lbs = _lbs_fwd(lower_bounds)
    saved = []
    xc = x.reshape(n, d)
    for l in range(DEPTH):
        proj = _inproj_fwd(xc, pre_norm_g[l:l + 1], w_in_int[l], f"inproj_fwd{l}").reshape(bsz, t, E_INT)
        wbd = _block_diag(pool_w[l]).astype(BF16)
        bias_row = _pad_lanes(fox_f_bias[l:l + 1])
        oa, oa_raw, states = _hgrn_fwd(proj, lbs[l:l + 1], hgrn_norm_g[l:l + 1], f"hgrn_fwd{l}")
        ob = _pool_fwd(proj, wbd, pool_scale[l:l + 1], f"pool_fwd{l}")
        c_nat, c_t = _foxgate_fwd(proj, bias_row, f"foxgate_fwd{l}")
        oc, oc_raw, lse = _fox_fwd(proj, c_nat, c_t, f"fox_fwd{l}")
        mixed = (oa.reshape(n, -1), ob.reshape(n, -1), oc.reshape(n, -1))
        if l < DEPTH - 1:
            y, xn = _outproj_fwd(xc, *mixed, w_out_bf[l], post_norm_g[l:l + 1], f"outproj_fwd{l}")
        else:
            y, dx, sq = _outproj_fwd_loss(xc, *mixed, w_out_bf[l], post_norm_g[l:l + 1], target.reshape(n, d),
                                          f"outproj_fwd{l}")
        saved.append((xc, proj, wbd, bias_row, oa, oa_raw, states, ob, oc, oc_raw, lse, c_nat, c_t, y))
        xc = xn
    g = {k: [None] * DEPTH for k in ("pre", "hgn", "bias", "pool_w", "pool_scale", "post", "lbs")}
    handed = [None] * DEPTH
    for l in reversed(range(DEPTH)):
        xin, proj, wbd, bias_row, oa, oa_raw, states, ob, oc, oc_raw, lse, c_nat, c_t, y = saved[l]
        dmix, d_w_out, dpost = _outproj_bwd(dx, y, oa.reshape(n, -1), ob.reshape(n, -1), oc.reshape(n, -1),
                                            w_out_bf[l], post_norm_g[l:l + 1], f"outproj_bwd{l}")
        g["post"][l] = dpost[0]
        dmix3 = dmix.reshape(bsz, t, d)
        d_c, dct, drow = _fox_bwd(proj, oc_raw, dmix3, lse, c_nat, c_t, f"fox_bwd{l}")
        dc_nat = _pad_lanes(dct[:, :, 0:2, :].reshape(bsz, FOX_HEADS, t).transpose(0, 2, 1)
                            + drow.reshape(bsz, t, FOX_HEADS, 64)[..., 0])
        d_f, dbias = _foxgate_bwd(proj, dc_nat, bias_row, f"foxgate_bwd{l}")
        g["bias"][l] = jnp.sum(dbias[:, 0, :FOX_HEADS], axis=0)
        d_b, dscale, dwbd = _pool_bwd(proj, dmix3, wbd, pool_scale[l:l + 1], f"pool_bwd{l}")
        g["pool_scale"][l] = jnp.sum(dscale[:, 0], axis=0)
        dwbd = jnp.sum(dwbd, axis=0)
        g["pool_w"][l] = jnp.stack([dwbd[64 * k:64 * (k + 1), 64 * k:64 * (k + 1)] for k in range(4)])
        d_a, dgn, dlb = _hgrn_bwd(proj, oa_raw, dmix3, states, lbs[l:l + 1], hgrn_norm_g[l:l + 1], f"hgrn_bwd{l}")
        g["hgn"][l] = jnp.sum(dgn[:, 0], axis=0)
        g["lbs"][l] = jnp.sum(dlb[:, 0], axis=0)
        pieces = [p.reshape(n, -1) for p in (d_a, d_b, d_c, d_f)]
        handed[l] = on_weight_grads(l, _inproj_bwd_w(xin, pre_norm_g[l:l + 1], pieces, f"inproj_bwd_w{l}"), d_w_out)
        dx, dpre = _inproj_bwd_x(xin, dx, pre_norm_g[l:l + 1], w_in_int[l], pieces, f"inproj_bwd_x{l}")
        g["pre"][l] = dpre[0]
    grads = {k: jnp.stack(v) for k, v in g.items()}
    return sq, dx.reshape(bsz, t, d), grads, handed


def _place():
    return lax.axis_index("x"), lax.axis_index("y"), lax.axis_index("c")


def _other_chips(x, y):
    return [(1 - x, y), (x, 1 - y), (1 - x, 1 - y)]


_ANY = pl.BlockSpec(memory_space=pl.ANY)


def _gather_body(handshake, n_arrays):
    def body(*refs):
        srcs, dsts = refs[:n_arrays], refs[n_arrays:2 * n_arrays]
        ici_send, ici_recv, d2d_send, d2d_recv, local_sems = refs[2 * n_arrays:]
        x, y, c = _place()
        if handshake:
            barrier = pltpu.get_barrier_semaphore()
            for peer in [(px, py, c) for px, py in _other_chips(x, y)] + [(x, y, 1 - c)]:
                pl.semaphore_signal(barrier, inc=1, device_id=peer, device_id_type=MESH)
            pl.semaphore_wait(barrier, 4)
        me = 2 * x + y
        pairs = list(zip(srcs, dsts))
        order = [(k, j) for k in range(3) for j in range(n_arrays)]
        mine = [pltpu.make_async_copy(src, dst.at[me], local_sems.at[j]) for j, (src, dst) in enumerate(pairs)]
        for cp in mine:
            cp.start()
        chips = _other_chips(x, y)
        sends = [pltpu.make_async_remote_copy(
            src_ref=pairs[j][0].at[c], dst_ref=pairs[j][1].at[me, c], send_sem=ici_send.at[n], recv_sem=ici_recv.at[n],
            device_id=(chips[k][0], chips[k][1], c), device_id_type=MESH) for n, (k, j) in enumerate(order)]
        for cp in sends:
            cp.start()
        passed = [pltpu.make_async_remote_copy(
            src_ref=pairs[j][1].at[2 * chips[k][0] + chips[k][1], c], dst_ref=pairs[j][1].at[2 * chips[k][0] + chips[k][1], c],
            send_sem=d2d_send.at[n], recv_sem=d2d_recv.at[n], device_id=(x, y, 1 - c), device_id_type=MESH)
            for n, (k, j) in enumerate(order)]
        for n, (k, j) in enumerate(order):
            px, py = chips[k]
            src, dst = pairs[j]
            pltpu.make_async_remote_copy(
                src_ref=src.at[c], dst_ref=dst.at[2 * px + py, c], send_sem=ici_send.at[n], recv_sem=ici_recv.at[n],
                device_id=(px, py, c), device_id_type=MESH).wait_recv()
            passed[n].start()
        for n, (k, j) in enumerate(order):
            px, py = chips[k]
            src, dst = pairs[j]
            pltpu.make_async_remote_copy(
                src_ref=dst.at[2 * px + py, 1 - c], dst_ref=dst.at[2 * px + py, 1 - c], send_sem=d2d_send.at[n],
                recv_sem=d2d_recv.at[n], device_id=(x, y, 1 - c), device_id_type=MESH).wait_recv()
        for cp in sends + passed:
            cp.wait_send()
        for cp in mine:
            cp.wait()

    return body


def _gather_sems(n_arrays):
    return [pltpu.SemaphoreType.DMA((3 * n_arrays,))] * 4 + [pltpu.SemaphoreType.DMA((n_arrays,))]


def _gathered(a):
    return jax.ShapeDtypeStruct((N_CHIPS,) + a.shape, a.dtype)


def _gather_weights(arrays):
    n = len(arrays)
    return pl.pallas_call(
        _gather_body(False, n), in_specs=[_ANY] * n, out_specs=[_ANY] * n, out_shape=[_gathered(a) for a in arrays],
        scratch_shapes=_gather_sems(n), name="gather_weights")(*arrays)


def _gather_weights_beside(arrays):
    hbm = pltpu.MemorySpace.HBM
    n = len(arrays)
    srcs = [jax.new_ref(a, memory_space=hbm) for a in arrays]
    dsts = [jax.empty_ref(_gathered(a), memory_space=hbm) for a in arrays]
    body = _gather_body(True, n)

    @pl.kernel(mesh=plsc.ScalarSubcoreMesh(axis_name="sequencer", num_cores=1), name="gather_weights_beside",
               scratch_types=_gather_sems(n), compiler_params=pltpu.CompilerParams(collective_id=1))
    def launch(*sems):
        body(*srcs, *dsts, *sems)

    launch()
    return [d[...] for d in dsts]


def _swap_with_sibling(parts, name):
    k = len(parts)

    def body(*refs):
        src, dst = refs[:k], refs[k:2 * k]
        send_sems, recv_sems = refs[2 * k:]
        x, y, c = _place()
        cps = [pltpu.make_async_remote_copy(src_ref=src[j], dst_ref=dst[j], send_sem=send_sems.at[j], recv_sem=recv_sems.at[j],
                                            device_id=(x, y, 1 - c), device_id_type=MESH) for j in range(k)]
        for cp in cps:
            cp.start()
        for cp in cps:
            cp.wait()

    return pl.pallas_call(
        body, in_specs=[_ANY] * k, out_specs=[_ANY] * k,
        out_shape=[jax.ShapeDtypeStruct(p.shape, p.dtype) for p in parts],
        scratch_shapes=[pltpu.SemaphoreType.DMA((k,)), pltpu.SemaphoreType.DMA((k,))], name=name)(*parts)


N_PEERS = 7


def _grad_exchange_body():
    def body(pin_ref, pout_ref, lin_ref, lout_ref, send_sems, recv_sems):
        x, y, c = _place()
        barrier = pltpu.get_barrier_semaphore()
        for k in range(1, N_PEERS + 1):
            peer = (x ^ ((k >> 2) & 1), y ^ ((k >> 1) & 1), c ^ (k & 1))
            pl.semaphore_signal(barrier, inc=1, device_id=peer, device_id_type=MESH)
        pl.semaphore_wait(barrier, N_PEERS)
        me = 2 * x + y
        pairs = ((pin_ref, lin_ref), (pout_ref, lout_ref))
        cps = []
        for k, (px, py) in enumerate(_other_chips(x, y)):
            for r in range(2):
                for j, (src, dst) in enumerate(pairs):
                    cps.append(pltpu.make_async_remote_copy(
                        src_ref=src.at[2 * px + py, r], dst_ref=dst.at[2 * k + c], send_sem=send_sems.at[2 * (2 * k + r) + j],
                        recv_sem=recv_sems.at[2 * (2 * k + c) + j], device_id=(px, py, r), device_id_type=MESH))
        for j, (src, dst) in enumerate(pairs):
            cps.append(pltpu.make_async_remote_copy(
                src_ref=src.at[me, 1 - c], dst_ref=dst.at[N_PEERS - 1], send_sem=send_sems.at[2 * (N_PEERS - 1) + j],
                recv_sem=recv_sems.at[2 * (N_PEERS - 1) + j], device_id=(x, y, 1 - c), device_id_type=MESH))
        for cp in cps:
            cp.start()
        for s in range(N_PEERS):
            for j, (src, dst) in enumerate(pairs):
                pltpu.make_async_remote_copy(
                    src_ref=src.at[0, 0], dst_ref=dst.at[s], send_sem=send_sems.at[2 * s + j], recv_sem=recv_sems.at[2 * s + j],
                    device_id=(x, y, 1 - c), device_id_type=MESH).wait_recv()
        for cp in cps:
            cp.wait_send()

    return body


_EXCHANGE_SEMS = [pltpu.SemaphoreType.DMA((2 * N_PEERS,))] * 2


def _landing(p):
    return jax.ShapeDtypeStruct((N_PEERS,) + p.shape[2:], p.dtype)


def _grad_exchange_beside(pin, pout, name, collective_id):
    hbm = pltpu.MemorySpace.HBM
    pin_ref, pout_ref = jax.new_ref(pin, memory_space=hbm), jax.new_ref(pout, memory_space=hbm)
    lin_ref, lout_ref = jax.empty_ref(_landing(pin), memory_space=hbm), jax.empty_ref(_landing(pout), memory_space=hbm)
    body = _grad_exchange_body()

    @pl.kernel(mesh=plsc.ScalarSubcoreMesh(axis_name="sequencer", num_cores=1), name=name,
               scratch_types=_EXCHANGE_SEMS, compiler_params=pltpu.CompilerParams(collective_id=collective_id))
    def launch(send_sems, recv_sems):
        body(pin_ref, pout_ref, lin_ref, lout_ref, send_sems, recv_sems)

    launch()
    return lin_ref[...], lout_ref[...]


def _add_n(parts, name, with_bf16=False):
    r, c = parts[0].shape
    tr = 256 if r % 256 == 0 else r
    n = len(parts)

    def body(*refs):
        acc = refs[0][...].astype(F32)
        for ref in refs[1:n]:
            acc = acc + ref[...].astype(F32)
        refs[n][...] = acc
        if with_bf16:
            refs[n + 1][...] = acc.astype(BF16)

    blk = pl.BlockSpec((tr, c), lambda i: (i, 0))
    outs = [jax.ShapeDtypeStruct((r, c), F32)] + ([jax.ShapeDtypeStruct((r, c), BF16)] if with_bf16 else [])
    res = pl.pallas_call(
        body, grid=(r // tr,), in_specs=[blk] * n, out_specs=[blk] * len(outs),
        out_shape=outs, compiler_params=_cparams(("parallel",)), name=name)(*parts)
    return res if with_bf16 else res[0]


def _all_reduce_small(packet):
    r, w = packet.shape

    def body(p_ref, o_ref, buf, send_sems, recv_sems):
        x, y, c = _place()
        me = 4 * x + 2 * y + c
        buf[me] = p_ref[...]
        peers = []
        for k in range(1, 8):
            fx, fy, fc = (k >> 2) & 1, (k >> 1) & 1, k & 1
            peers.append((x ^ fx, y ^ fy, c ^ fc))
        cps = [pltpu.make_async_remote_copy(src_ref=p_ref, dst_ref=buf.at[me], send_sem=send_sems.at[k], recv_sem=recv_sems.at[k],
                                            device_id=peer, device_id_type=MESH) for k, peer in enumerate(peers)]
        for cp in cps:
            cp.start()
        for k, (px, py, pc) in enumerate(peers):
            pltpu.make_async_remote_copy(src_ref=p_ref, dst_ref=buf.at[4 * px + 2 * py + pc], send_sem=send_sems.at[k],
                                         recv_sem=recv_sems.at[k], device_id=(px, py, pc), device_id_type=MESH).wait_recv()
        for cp in cps:
            cp.wait_send()
        acc = buf[0]
        for k in range(1, 8):
            acc = acc + buf[k]
        o_ref[...] = acc

    vm = pl.BlockSpec(memory_space=pltpu.VMEM)
    return pl.pallas_call(
        body, in_specs=[vm], out_specs=vm, out_shape=jax.ShapeDtypeStruct((r, w), F32),
        scratch_shapes=[pltpu.VMEM((8, r, w), F32), pltpu.SemaphoreType.DMA((7,)), pltpu.SemaphoreType.DMA((7,))],
        name="all_reduce_small")(packet)


def _adamw_math(w, g, m, v):
    m = ADAM_B1 * m + (1.0 - ADAM_B1) * g
    v = ADAM_B2 * v + (1.0 - ADAM_B2) * (g * g)
    m_hat = m / (1.0 - ADAM_B1 ** ADAM_STEP)
    v_hat = v / (1.0 - ADAM_B2 ** ADAM_STEP)
    return -ADAM_LR * (m_hat / (jnp.sqrt(v_hat) + ADAM_EPS) + ADAM_WD * w), m, v


def _adamw(w, g_lower, g_upper, m, v, name):
    nl, r, c = w.shape
    tr = 128
    per_half = r // (2 * tr)

    def body(w_ref, lo_ref, up_ref, m_ref, v_ref, g_ref, d_ref, mo_ref, vo_ref):
        g = jnp.where(pl.program_id(1) == 0, lo_ref[...], up_ref[...])
        g_ref[...] = g
        d_ref[...], mo_ref[...], vo_ref[...] = _adamw_math(w_ref[...], g, m_ref[...], v_ref[...])

    blk = pl.BlockSpec((None, tr, c), lambda l, h, i: (l, h * per_half + i, 0))
    half = pl.BlockSpec((None, tr, c), lambda l, h, i: (l, i, 0))
    out = jax.ShapeDtypeStruct(w.shape, F32)
    return pl.pallas_call(
        body, grid=(nl, 2, per_half), in_specs=[blk, half, half, blk, blk], out_specs=[blk] * 4, out_shape=[out] * 4,
        compiler_params=_cparams(("parallel", "parallel", "parallel")), name=name)(w, g_lower, g_upper, m, v)


def _small_update(gsum, lower_bounds, wpack, mpack, vpack):
    r, w = gsum.shape
    lb_rows = DEPTH * HGRN_W // 128

    def body(g_ref, a_ref, w_ref, m_ref, v_ref, go_ref, d_ref, mo_ref, vo_ref):
        a = a_ref[...]
        a0, a1 = a[0:1], a[1:2]
        mx = jnp.maximum(a0, a1)
        e0, e1 = jnp.exp(a0 - mx), jnp.exp(a1 - mx)
        p0, p1 = e0 / (e0 + e1), e1 / (e0 + e1)
        g = g_ref[...]
        half = lb_rows // 2
        dl0 = jnp.concatenate([g[k:k + 1] for k in range(half)], axis=1)
        dl1 = jnp.concatenate([g[half + k:half + k + 1] for k in range(half)], axis=1)
        dp0 = (dl0 + dl1) - (dl0 + dl1)
        dp1 = dl1
        inner = p0 * dp0 + p1 * dp1
        da0, da1 = p0 * (dp0 - inner), p1 * (dp1 - inner)
        rows = [da0[:, 128 * k:128 * (k + 1)] for k in range(half)] + [da1[:, 128 * k:128 * (k + 1)] for k in range(half)]
        gfull = jnp.concatenate(rows + [g[lb_rows:]], axis=0)
        go_ref[...] = gfull
        d_ref[...], mo_ref[...], vo_ref[...] = _adamw_math(w_ref[...], gfull, m_ref[...], v_ref[...])

    vm = pl.BlockSpec(memory_space=pltpu.VMEM)
    out = jax.ShapeDtypeStruct((r, w), F32)
    return pl.pallas_call(body, in_specs=[vm] * 5, out_specs=[vm] * 4, out_shape=[out] * 4, name="small_update")(
        gsum, lower_bounds, wpack, mpack, vpack)


_SMALL = ("lower_bounds", "pre_norm_g", "hgrn_norm_g", "fox_f_bias", "pool_w", "pool_scale", "post_norm_g")


def _pack(parts):
    rows = []
    for k in _SMALL:
        f = parts[k].reshape(-1)
        pad = (-f.shape[0]) % (8 * 128)
        rows.append(jnp.pad(f, (0, pad)).reshape(-1, 128))
    rows.append(jnp.zeros((8, 128), F32))
    return jnp.concatenate(rows, axis=0)


def _unpack(pack, like):
    out, r = {}, 0
    for k in _SMALL:
        size = int(np.prod(like[k].shape))
        nr = -(-size // (8 * 128)) * 8
        out[k] = pack[r:r + nr].reshape(-1)[:size].reshape(like[k].shape)
        r += nr
    return out, r


def kernel(x, lower_bounds, pre_norm_g, w_in, hgrn_norm_g, fox_f_bias, pool_w, pool_scale, w_out, post_norm_g, loss_target, m_lower_bounds, m_pre_norm_g, m_w_in, m_hgrn_norm_g, m_fox_f_bias, m_pool_w, m_pool_scale, m_w_out, m_post_norm_g, v_lower_bounds, v_pre_norm_g, v_w_in, v_hgrn_norm_g, v_fox_f_bias, v_pool_w, v_pool_scale, v_w_out, v_post_norm_g):
    cx, cy, cc = _place()
    chip = 2 * cx + cy

    halves = lambda w, l: w[l].reshape(2, w.shape[1] // 2, w.shape[2]).astype(BF16)
    needed_first = _gather_weights([halves(w_in, 0)])
    needed_first, later = lax.optimization_barrier((needed_first, [halves(w_out, 0), halves(w_in, 1), halves(w_out, 1)]))
    later = _gather_weights_beside(later)
    w_in_int = [_internal_from_shards([a[q].reshape(D_MODEL, SHARD_W) for q in range(N_CHIPS)]) for a in (needed_first[0], later[1])]
    w_out_full = [a.reshape(D_MODEL, D_MODEL) for a in (later[0], later[2])]

    def on_weight_grads(l, d_w_in, d_w_out):
        pin = _shards_from_internal(d_w_in).reshape(N_CHIPS, 2, D_MODEL // 2, SHARD_W)
        pout = d_w_out.reshape(N_CHIPS, 2, D_MODEL // (2 * N_CHIPS), D_MODEL)
        own = [lax.dynamic_index_in_dim(lax.dynamic_index_in_dim(p, chip, 0, False), cc, 0, False) for p in (pin, pout)]
        return own, _grad_exchange_beside(pin.astype(BF16), pout.astype(BF16), f"grad_exchange{l}", 2 + l)

    sq, grad_x, g, handed = _local_step(x, loss_target, lower_bounds, pre_norm_g, w_in_int, hgrn_norm_g, fox_f_bias,
                                        pool_w, pool_scale, w_out_full, post_norm_g, on_weight_grads)
    first = cc == 0

    def finish(l, own, landed):
        mine = [_add_n([o] + [t[s] for s in range(N_PEERS)], f"grad_sum{l}_{j}") for j, (o, t) in enumerate(zip(own, landed))]
        theirs = _swap_with_sibling(mine, f"grad_swap{l}")
        return [(jnp.where(first, h, o), jnp.where(first, o, h)) for h, o in zip(mine, theirs)]

    grad_x, last = lax.optimization_barrier((grad_x, handed[1]))
    done = [None, finish(1, *last)]

    small = {"lower_bounds": g["lbs"], "pre_norm_g": g["pre"], "hgrn_norm_g": g["hgn"], "fox_f_bias": g["bias"],
             "pool_w": g["pool_w"], "pool_scale": g["pool_scale"], "post_norm_g": g["post"]}
    packet = _pack(small)
    nrows = packet.shape[0]
    packet = packet.at[nrows - 1].set(sq[0])
    gsum = _all_reduce_small(packet)
    loss = gsum[nrows - 1, 0] * (0.5 / D_MODEL)

    weights = {"lower_bounds": lower_bounds, "pre_norm_g": pre_norm_g, "hgrn_norm_g": hgrn_norm_g,
               "fox_f_bias": fox_f_bias, "pool_w": pool_w, "pool_scale": pool_scale, "post_norm_g": post_norm_g}
    moments_m = {"lower_bounds": m_lower_bounds, "pre_norm_g": m_pre_norm_g, "hgrn_norm_g": m_hgrn_norm_g,
                 "fox_f_bias": m_fox_f_bias, "pool_w": m_pool_w, "pool_scale": m_pool_scale, "post_norm_g": m_post_norm_g}
    moments_v = {"lower_bounds": v_lower_bounds, "pre_norm_g": v_pre_norm_g, "hgrn_norm_g": v_hgrn_norm_g,
                 "fox_f_bias": v_fox_f_bias, "pool_w": v_pool_w, "pool_scale": v_pool_scale, "post_norm_g": v_post_norm_g}
    gp, dp, mp, vp = _small_update(gsum, lower_bounds, _pack(weights), _pack(moments_m), _pack(moments_v))
    gs, _ = _unpack(gp, weights)
    ds, _ = _unpack(dp, weights)
    ms, _ = _unpack(mp, weights)
    vs, _ = _unpack(vp, weights)

    first_layer, _ = lax.optimization_barrier((handed[0], (done[1], gp, dp, mp, vp)))
    done[0] = finish(0, *first_layer)
    halves_of = lambda j, side: jnp.stack([done[l][j][side] for l in range(DEPTH)])
    grad_w_in, d_in, m_in, v_in = _adamw(w_in, halves_of(0, 0), halves_of(0, 1), m_w_in, v_w_in, "adamw_w_in")
    grad_w_out, d_out, m_out, v_out = _adamw(w_out, halves_of(1, 0), halves_of(1, 1), m_w_out, v_w_out, "adamw_w_out")

    def ordered(s, big_in, big_out):
        return (s["lower_bounds"], s["pre_norm_g"], big_in, s["hgrn_norm_g"], s["fox_f_bias"], s["pool_w"],
                s["pool_scale"], big_out, s["post_norm_g"])

    return (loss, grad_x, *ordered(gs, grad_w_in, grad_w_out), *ordered(ds, d_in, d_out),
            *ordered(ms, m_in, m_out), *ordered(vs, v_in, v_out))
```

```python
import functools

import numpy as np
import jax
import jax.numpy as jnp
from jax import lax
from jax.experimental import pallas as pl
from jax.experimental.pallas import tpu as pltpu
from jax.experimental.pallas import tpu_sc as plsc

F32 = jnp.float32
BF16 = jnp.bfloat16
HI = lax.Precision.HIGHEST
MESH = pl.DeviceIdType.MESH

NORM_EPS = 1e-6
MASK_VALUE = -1e30
TINY = 1e-30
ADAM_LR, ADAM_B1, ADAM_B2, ADAM_EPS, ADAM_WD, ADAM_STEP = 0.001, 0.9, 0.999, 1e-08, 0.01, 10

D_MODEL = 1024
DEPTH = 2
N_CHIPS = 4
CHUNK = 64
LANES = 128
HGRN_W, POOL_W, FOX_W, FOX_HEADS = 256, 256, 512, 8
POOL_WINDOWS = (2, 4, 8, 16)
POOL_HALO = 16
IN_WIDTH = 3592
SHARD_W = IN_WIDTH // N_CHIPS
A_W, B_W, C_W, F_W = 1024, 512, 2048, 128
E_INT = A_W + B_W + C_W + F_W
B_BLK = A_W // 512
C_BLK0 = (A_W + B_W) // 512
F_BLK = (A_W + B_W + C_W) // 128


def _segments():
    segs = []
    for hp in range(2):
        for part in range(4):
            segs.append((part * 256 + hp * 128, 128))
    segs.append((1024, 256))
    segs.append((1280, 256))
    for hp in range(4):
        for part in range(4):
            segs.append((1536 + part * 512 + hp * 128, 128))
    segs.append((3584, 8))
    return segs


_SEGS = _segments()


def _to_internal(w):
    parts = [w[..., s:s + n] for s, n in _SEGS]
    parts.append(jnp.zeros(w.shape[:-1] + (E_INT - IN_WIDTH,), w.dtype))
    return jnp.concatenate(parts, axis=-1)


def _to_original(w):
    offs, o = [], 0
    for s, n in _SEGS:
        offs.append((s, o, n))
        o += n
    parts = [w[..., o:o + n] for s, o, n in sorted(offs)]
    return jnp.concatenate(parts, axis=-1)


def _internal_from_shards(shards):
    parts = []
    for s, n in _SEGS:
        while n > 0:
            q, r = divmod(s, SHARD_W)
            take = min(n, SHARD_W - r)
            parts.append(shards[q][..., r:r + take])
            s, n = s + take, n - take
    parts.append(jnp.zeros(shards[0].shape[:-1] + (E_INT - IN_WIDTH,), shards[0].dtype))
    return jnp.concatenate(parts, axis=-1)


def _shards_from_internal(w):
    offs, o = [], 0
    for s, n in _SEGS:
        offs.append((s, o, n))
        o += n
    blocks = []
    for q in range(N_CHIPS):
        lo, hi = SHARD_W * q, SHARD_W * (q + 1)
        parts = [w[..., o + max(lo, s) - s:o + min(hi, s + n) - s] for s, o, n in sorted(offs) if s < hi and s + n > lo]
        blocks.append(jnp.concatenate(parts, axis=-1))
    return jnp.stack(blocks)


def _cparams(sem=None, vmem_mb=48):
    kw = dict(vmem_limit_bytes=vmem_mb * 1024 * 1024)
    if sem is not None:
        kw["dimension_semantics"] = sem
    return pltpu.CompilerParams(**kw)


def _sig(x):
    return 1.0 / (1.0 + jnp.exp(-x))


def _silu(x):
    return x * _sig(x)


def _dsilu(x):
    s = _sig(x)
    return s * (1.0 + x * (1.0 - s))


def _rstd(x):
    return lax.rsqrt(jnp.mean(x * x, axis=-1, keepdims=True) + NORM_EPS)


def _dot(a, b, dims, **kw):
    return lax.dot_general(a, b, (dims, ((), ())), preferred_element_type=F32, **kw)


NN = ((1,), (0,))
NT = ((1,), (1,))
TN = ((0,), (0,))


def _iota(shape, dim):
    return lax.broadcasted_iota(jnp.int32, shape, dim)


def _lbs_fwd(lower_bounds):
    def body(a_ref, o_ref):
        a = a_ref[...]
        a0, a1 = a[0:1], a[1:2]
        m = jnp.maximum(a0, a1)
        e0, e1 = jnp.exp(a0 - m), jnp.exp(a1 - m)
        p0, p1 = e0 / (e0 + e1), e1 / (e0 + e1)
        o_ref[...] = jnp.concatenate([p0 - p0, (p0 + p1) - p0], axis=0)

    return pl.pallas_call(body, out_shape=jax.ShapeDtypeStruct(lower_bounds.shape, F32), name="lbs_fwd")(lower_bounds)


def _inproj_fwd(x2, g_row, w_int, name):
    n, d = x2.shape
    e = w_int.shape[1]
    tm = min(512, n)

    def body(x_ref, g_ref, w_ref, o_ref):
        x = x_ref[...]
        h = (x * _rstd(x) * g_ref[...]).astype(BF16)
        o_ref[...] = jnp.dot(h, w_ref[...], preferred_element_type=F32)

    return pl.pallas_call(
        body, grid=(n // tm,),
        in_specs=[pl.BlockSpec((tm, d), lambda i: (i, 0)), pl.BlockSpec((1, d), lambda i: (0, 0)),
                  pl.BlockSpec((d, e), lambda i: (0, 0))],
        out_specs=pl.BlockSpec((tm, e), lambda i: (i, 0)),
        out_shape=jax.ShapeDtypeStruct((n, e), F32),
        compiler_params=_cparams(("parallel",)), name=name)(x2, g_row, w_int)


def _chunk_cumsum_matrix():
    i, j = _iota((LANES, LANES), 0), _iota((LANES, LANES), 1)
    return ((i <= j) & ((i // CHUNK) == (j // CHUNK))).astype(F32)


def _hgrn_gates(a, lb):
    qa, z = a[:, 0:128], a[:, 128:256]
    sg, sgn = _sig(z), _sig(-z)
    fg = lb + (1.0 - lb) * sg
    lf = jnp.log(jnp.maximum(fg, TINY))
    kk = (1.0 - lb) * sgn
    return qa * _sig(qa), kk, lf, sg, sgn, fg


def _hgrn_fwd(proj3, lbs_row, gn_col, name):
    bsz, t, _ = proj3.shape
    nt = t // LANES

    def body(a_ref, lb_ref, gn_ref, og_ref, or_ref):
        lb = lb_ref[...]
        gn = gn_ref[...]
        umat = _chunk_cumsum_matrix()
        lane64 = _iota((1, LANES), 1) % CHUNK

        def tile(i, carry):
            r0 = pl.multiple_of(i * LANES, LANES)
            a = a_ref[pl.ds(r0, LANES), :]
            qq, kk, lf, _, _, _ = _hgrn_gates(a, lb)
            va, ga = a[:, 256:384], a[:, 384:512]
            q_t, k_t, v_t = qq.T, kk.T, va.T
            b_t = jnp.dot(lf.T, umat, precision=HI, preferred_element_type=F32)
            new_s, o_heads = [], []
            for h in range(2):
                s_h = carry[h]
                rs = slice(CHUNK * h, CHUNK * (h + 1))
                qh, kh, vh, bh = q_t[rs], k_t[rs], v_t[rs], b_t[rs]
                inter = []
                for c in range(2):
                    cs = slice(CHUNK * c, CHUNK * (c + 1))
                    b_ = bh[:, cs]
                    qt = (qh[:, cs] * jnp.exp(b_)).astype(BF16)
                    inter.append(_dot(s_h.astype(BF16), qt, TN))
                    bl = b_[:, CHUNK - 1:CHUNK]
                    kt = (kh[:, cs] * jnp.exp(bl - b_)).astype(BF16)
                    s_h = jnp.exp(bl) * s_h + _dot(kt, vh[:, cs].astype(BF16), NT)
                new_s.append(s_h)

                acc = jnp.concatenate(inter, axis=1) + jnp.sum(qh * kh, axis=0, keepdims=True) * vh
                for dlt in range(1, CHUNK):
                    kr, br, vr = pltpu.roll(kh, dlt, 1), pltpu.roll(bh, dlt, 1), pltpu.roll(vh, dlt, 1)
                    e = jnp.exp(jnp.minimum(bh - br, 0.0))
                    att = jnp.sum(qh * kr * e, axis=0, keepdims=True)
                    acc = acc + jnp.where(lane64 >= dlt, att, 0.0) * vr
                o_heads.append(acc)
            normed = []
            for h in range(2):
                o_h = o_heads[h]
                ms = jnp.mean(o_h * o_h, axis=0, keepdims=True)
                normed.append(o_h * lax.rsqrt(ms + NORM_EPS) * gn[CHUNK * h:CHUNK * (h + 1)])
            or_ref[pl.ds(r0, LANES), :] = jnp.concatenate(o_heads, axis=0).T
            og_ref[pl.ds(r0, LANES), :] = jnp.concatenate(normed, axis=0).T * _silu(ga)
            return tuple(new_s)

        zero = jnp.zeros((CHUNK, CHUNK), F32)
        lax.fori_loop(0, nt, tile, (zero, zero))

    out = jax.ShapeDtypeStruct((bsz, t, HGRN_W), F32)
    return pl.pallas_call(
        body, grid=(bsz, 2),
        in_specs=[pl.BlockSpec((None, t, 512), lambda b, p: (b, 0, p)),
                  pl.BlockSpec((1, 128), lambda b, p: (0, p)),
                  pl.BlockSpec((128, 1), lambda b, p: (p, 0))],
        out_specs=[pl.BlockSpec((None, t, 128), lambda b, p: (b, 0, p)),
                   pl.BlockSpec((None, t, 128), lambda b, p: (b, 0, p))],
        out_shape=[out, out],
        compiler_params=_cparams(("parallel", "parallel")), name=name)(proj3, lbs_row, gn_col)


def _hgrn_bwd(proj3, o_raw, dmixed, lbs_row, gn_row, name):
    bsz, t, _ = proj3.shape
    nt = t // LANES
    nchunk = t // CHUNK

    def body(a_ref, or_ref, do_ref, lb_ref, gn_ref, da_ref, dgn_ref, dlb_ref, s_sc):
        lb = lb_ref[...]
        gn = gn_ref[...]
        umat = _chunk_cumsum_matrix()
        lane = _iota((1, LANES), 1)
        lane64 = lane % CHUNK
        half = lane < CHUNK

        def t_layout(a):
            qq, kk, lf, sg, sgn, fg = _hgrn_gates(a, lb)
            b_t = jnp.dot(lf.T, umat, precision=HI, preferred_element_type=F32)
            return qq.T, kk.T, a[:, 256:384].T, b_t, (sg, sgn, fg)

        def fwd_tile(i, carry):
            r0 = pl.multiple_of(i * LANES, LANES)
            q_t, k_t, v_t, b_t, _ = t_layout(a_ref[pl.ds(r0, LANES), :])
            new_s = []
            for h in range(2):
                s_h = carry[h]
                rs = slice(CHUNK * h, CHUNK * (h + 1))
                for c in range(2):
                    cs = slice(CHUNK * c, CHUNK * (c + 1))
                    s_sc[h, 2 * i + c] = s_h
                    b_ = b_t[rs, cs]
                    bl = b_[:, CHUNK - 1:CHUNK]
                    kt = (k_t[rs, cs] * jnp.exp(bl - b_)).astype(BF16)
                    s_h = jnp.exp(bl) * s_h + _dot(kt, v_t[rs, cs].astype(BF16), NT)
                new_s.append(s_h)
            return tuple(new_s)

        zero = jnp.zeros((CHUNK, CHUNK), F32)
        lax.fori_loop(0, nt, fwd_tile, (zero, zero))

        def half_mean(v):
            m0 = jnp.sum(jnp.where(half, v, 0.0), axis=1, keepdims=True) * (1.0 / CHUNK)
            m1 = jnp.sum(jnp.where(half, 0.0, v), axis=1, keepdims=True) * (1.0 / CHUNK)
            return jnp.where(half, m0, m1)

        def bwd_tile(k, carry):
            ds0, ds1, dgn_acc, dlb_acc = carry
            i = nt - 1 - k
            r0 = pl.multiple_of(i * LANES, LANES)
            a = a_ref[pl.ds(r0, LANES), :]
            qa, z, ga = a[:, 0:128], a[:, 128:256], a[:, 384:512]
            q_t, k_t, v_t, b_t, (sg, sgn, fg) = t_layout(a)
            oraw = or_ref[pl.ds(r0, LANES), :]
            dout = do_ref[pl.ds(r0, LANES), :]
            r = lax.rsqrt(half_mean(oraw * oraw) + NORM_EPS)
            xn = oraw * r
            dga = dout * (xn * gn) * _dsilu(ga)
            don = dout * _silu(ga)
            dgn_acc = dgn_acc + jnp.sum(don * xn, axis=0, keepdims=True)
            dxn = don * gn
            do_t = (r * (dxn - xn * half_mean(dxn * xn))).T
            new_ds, dq_h, dk_h, dv_h, db_h = [], [], [], [], []
            for h in range(2):
                ds_h = (ds0, ds1)[h]
                rs = slice(CHUNK * h, CHUNK * (h + 1))
                qh, kh, vh, bh, doh = q_t[rs], k_t[rs], v_t[rs], b_t[rs], do_t[rs]
                dq_c, dk_c, dv_c, dbl_c = [None, None], [None, None], [None, None], [None, None]
                for c in (1, 0):
                    cs = slice(CHUNK * c, CHUNK * (c + 1))
                    s_n = s_sc[h, 2 * i + c]
                    b_ = bh[:, cs]
                    eb = jnp.exp(b_)
                    bl = b_[:, CHUNK - 1:CHUNK]
                    ek = jnp.exp(bl - b_)
                    ebl = jnp.exp(bl)
                    qt, kt = qh[:, cs] * eb, kh[:, cs] * ek
                    do_c = doh[:, cs].astype(BF16)
                    dsb = ds_h.astype(BF16)
                    dv_c[c] = _dot(dsb, kt.astype(BF16), TN)
                    dkt = _dot(dsb, vh[:, cs].astype(BF16), NN)
                    dqt = _dot(s_n.astype(BF16), do_c, NN)
                    dbl_c[c] = jnp.sum(ds_h * s_n, axis=1, keepdims=True) * ebl + jnp.sum(dkt * kt, axis=1, keepdims=True)
                    dq_c[c], dk_c[c] = dqt * eb, dkt * ek
                    ds_h = ebl * ds_h + _dot(qt.astype(BF16), do_c, NT)
                new_ds.append(ds_h)

                att0 = jnp.sum(qh * kh, axis=0, keepdims=True)
                datt0 = jnp.sum(doh * vh, axis=0, keepdims=True)
                dqh = jnp.concatenate(dq_c, axis=1) + datt0 * kh
                dkh = jnp.concatenate(dk_c, axis=1) + datt0 * qh
                dvh = jnp.concatenate(dv_c, axis=1) + att0 * doh
                for dlt in range(1, CHUNK):
                    kr, br, vr = pltpu.roll(kh, dlt, 1), pltpu.roll(bh, dlt, 1), pltpu.roll(vh, dlt, 1)
                    e = jnp.where(lane64 >= dlt, jnp.exp(jnp.minimum(bh - br, 0.0)), 0.0)
                    qe = qh * e
                    att = jnp.sum(qe * kr, axis=0, keepdims=True)
                    datt = jnp.sum(doh * vr, axis=0, keepdims=True)
                    dqh = dqh + datt * (kr * e)
                    dkh = dkh + pltpu.roll(datt * qe, LANES - dlt, 1)
                    dvh = dvh + pltpu.roll(att * doh, LANES - dlt, 1)
                dbl = jnp.where(half, dbl_c[0], dbl_c[1])
                db_h.append(qh * dqh - kh * dkh + jnp.where(lane64 == CHUNK - 1, dbl, 0.0))
                dq_h.append(dqh)
                dk_h.append(dkh)
                dv_h.append(dvh)
            dqq = jnp.concatenate(dq_h, axis=0).T
            dkk = jnp.concatenate(dk_h, axis=0).T
            dvv = jnp.concatenate(dv_h, axis=0).T
            dlf = _dot(jnp.concatenate(db_h, axis=0), umat, NT, precision=HI).T
            dqa = dqq * _dsilu(qa)
            dfg = jnp.where(fg > TINY, dlf / fg, 0.0)
            dz = (dfg - dkk) * (1.0 - lb) * sg * sgn
            dlb_acc = dlb_acc + jnp.sum(dfg * (1.0 - sg) - dkk * sgn, axis=0, keepdims=True)
            da_ref[pl.ds(r0, LANES), :] = jnp.concatenate([dqa, dz, dvv, dga], axis=1)
            return new_ds[0], new_ds[1], dgn_acc, dlb_acc

        zrow = jnp.zeros((1, LANES), F32)
        _, _, dgn_acc, dlb_acc = lax.fori_loop(0, nt, bwd_tile, (zero, zero, zrow, zrow))
        dgn_ref[...] = jnp.broadcast_to(dgn_acc, (8, LANES))
        dlb_ref[...] = jnp.broadcast_to(dlb_acc, (8, LANES))

    rows = jax.ShapeDtypeStruct((bsz, 8, HGRN_W), F32)
    return pl.pallas_call(
        body, grid=(bsz, 2),
        in_specs=[pl.BlockSpec((None, t, 512), lambda b, p: (b, 0, p)),
                  pl.BlockSpec((None, t, 128), lambda b, p: (b, 0, p)),
                  pl.BlockSpec((None, t, 128), lambda b, p: (b, 0, p)),
                  pl.BlockSpec((1, 128), lambda b, p: (0, p)),
                  pl.BlockSpec((1, 128), lambda b, p: (0, p))],
        out_specs=[pl.BlockSpec((None, t, 512), lambda b, p: (b, 0, p)),
                   pl.BlockSpec((None, 8, 128), lambda b, p: (b, 0, p)),
                   pl.BlockSpec((None, 8, 128), lambda b, p: (b, 0, p))],
        out_shape=[jax.ShapeDtypeStruct((bsz, t, A_W), F32), rows, rows],
        scratch_shapes=[pltpu.VMEM((2, nchunk, CHUNK, CHUNK), F32)],
        compiler_params=_cparams(("parallel", "parallel")), name=name)(proj3, o_raw, dmixed, lbs_row, gn_row)


N_LEVELS = 6


def _hgrn_tables():
    t = np.arange(LANES)
    j = np.arange(LANES)[None, :]
    same_chunk = (t[:, None] // CHUNK) == (j // CHUNK)
    w = np.zeros((2 + N_LEVELS, LANES, LANES), np.float32)
    w[0] = same_chunk & (j <= t[:, None])
    w[1] = same_chunk & (j > t[:, None])
    maskf = np.zeros((N_LEVELS, LANES, LANES), np.float32)
    rightf = np.zeros((N_LEVELS, LANES, LANES), np.float32)
    for li in range(N_LEVELS):
        m = (CHUNK // 2) >> li
        start = t - (t % (2 * m))
        right = (t % (2 * m)) >= m
        first = np.where(right, start + m, t + 1)
        last = np.where(right, t, start + m - 1)
        w[2 + li] = (j >= first[:, None]) & (j <= last[:, None])
        maskf[li] = (t[:, None] // (2 * m)) == (j // (2 * m))
        rightf[li] = right[:, None]
    return jnp.asarray(w.reshape(-1, LANES), BF16), jnp.asarray(np.tile(maskf, (1, 2, 1))), jnp.asarray(rightf)


def _split(x, n):
    parts = []
    for _ in range(n - 1):
        p = x.astype(BF16)
        parts.append(p)
        x = x - p.astype(F32)
    parts.append(x.astype(BF16))
    return parts


def _exact_dot(w, parts):
    acc = jnp.dot(w, parts[0], preferred_element_type=F32)
    for p in parts[1:]:
        acc = acc + jnp.dot(w, p, preferred_element_type=F32)
    return acc


def _head_sums(v, ones_blk, n=2):
    parts = _split(v, n)
    acc = jnp.dot(parts[0], ones_blk, preferred_element_type=F32)
    for p in parts[1:]:
        acc = acc + jnp.dot(p, ones_blk, preferred_element_type=F32)
    return acc


def _hgrn_consts():
    r, c = _iota((LANES, LANES), 0), _iota((LANES, LANES), 1)
    ones_blk = ((r // CHUNK) == (c // CHUNK)).astype(BF16)
    eye2 = (_iota((2 * LANES, LANES), 0) % LANES) == _iota((2 * LANES, LANES), 1)
    first = _iota((1, LANES), 1) < CHUNK
    return eye2, ones_blk, jnp.ones((LANES, LANES), BF16), first


def _stack_heads(v, first):
    return jnp.concatenate([jnp.where(first, v, 0.0), jnp.where(first, 0.0, v)], axis=0)


def _pick_heads(v2, first):
    return jnp.where(first, v2[:LANES], v2[LANES:])


def _hgrn_levels(qq, kk, zall, mk_ref, rt_ref, first, d_att=None):
    att = jnp.zeros((2 * LANES, LANES), F32)
    dq = dk = db = jnp.zeros((LANES, LANES), F32)
    for li in range(N_LEVELS):
        e = jnp.exp(zall[(2 + li) * LANES:(3 + li) * LANES])
        rt = rt_ref[li]
        mk = mk_ref[li]
        qef, kef = e * rt, e * (1.0 - rt)
        qe, ke = (qq * qef).astype(BF16), (kk * kef).astype(BF16)
        qe2 = _stack_heads(qe, first)
        att = att + _dot(qe2, ke, NT) * mk
        if d_att is not None:
            dam = (d_att * mk).astype(BF16)
            dqe = _pick_heads(jnp.dot(dam, ke, preferred_element_type=F32), first)
            dke = _dot(dam, qe2, TN)
            dq = dq + dqe * qef
            dk = dk + dke * kef
            db = db + (dqe * qe.astype(F32) - dke * ke.astype(F32))
    return att, dq, dk, db


def _hgrn_fwd(proj3, lbs_row, gn_row, name):
    bsz, t, _ = proj3.shape
    nt = t // LANES
    w_all, maskf, rightf = _hgrn_tables()

    def body(a_ref, lb_ref, gn_ref, w_ref, mk_ref, rt_ref, og_ref, or_ref, st_ref):
        lb = lb_ref[...]
        gn = gn_ref[...]
        eye2, ones_blk, ones_all, first = _hgrn_consts()

        def tile(i, carry):
            r0 = pl.multiple_of(i * LANES, LANES)
            a = a_ref[pl.ds(r0, LANES), :]
            qq, kk, lf, _, _, _ = _hgrn_gates(a, lb)
            va, ga = a[:, 256:384], a[:, 384:512]
            parts = _split(lf, 3)
            zall = _exact_dot(w_ref[...], parts)
            eb, ee = jnp.exp(zall[0:LANES]), jnp.exp(zall[LANES:2 * LANES])
            vb = va.astype(BF16)
            att, _, _, _ = _hgrn_levels(qq, kk, zall, mk_ref, rt_ref, first)
            diag = _head_sums(_stack_heads(qq * kk, first), ones_all)
            a2 = (att + jnp.where(eye2, diag, 0.0)).astype(BF16)
            o_in = _pick_heads(jnp.dot(a2, vb, preferred_element_type=F32), first)
            qeb, keb = (qq * eb).astype(BF16), (kk * ee).astype(BF16)
            new_s, o_heads = [], []
            for h in range(2):
                hs = slice(CHUNK * h, CHUNK * (h + 1))
                o_h = o_in[:, hs]
                st = carry[h]
                chunks = []
                for c in range(2):
                    rc = slice(CHUNK * c, CHUNK * (c + 1))
                    st_ref[h, 2 * i + c] = st
                    chunks.append(o_h[rc] + _dot(qeb[rc, hs], st.astype(BF16), NT))
                    ebl = eb[CHUNK * (c + 1) - 1:CHUNK * (c + 1), hs]
                    st = st * ebl + _dot(vb[rc, hs], keb[rc, hs], TN)
                new_s.append(st)
                o_heads.append(jnp.concatenate(chunks, axis=0))
            o = jnp.concatenate(o_heads, axis=1)
            ms = _head_sums(o * o, ones_blk) * (1.0 / CHUNK)
            or_ref[pl.ds(r0, LANES), :] = o
            og_ref[pl.ds(r0, LANES), :] = o * lax.rsqrt(ms + NORM_EPS) * gn * _silu(ga)
            return tuple(new_s)

        zero = jnp.zeros((CHUNK, CHUNK), F32)
        per_step = 4 if nt % 4 == 0 else 2

        def step(i, carry):
            for k in range(per_step):
                carry = tile(per_step * i + k, carry)
            return carry

        lax.fori_loop(0, nt // per_step, step, (zero, zero))

    out = jax.ShapeDtypeStruct((bsz, t, HGRN_W), F32)
    row = pl.BlockSpec((1, 128), lambda b, p: (0, p))
    return pl.pallas_call(
        body, grid=(bsz, 2),
        in_specs=[pl.BlockSpec((None, t, 512), lambda b, p: (b, 0, p)), row, row,
                  pl.BlockSpec(w_all.shape, lambda b, p: (0, 0)),
                  pl.BlockSpec(maskf.shape, lambda b, p: (0, 0, 0)),
                  pl.BlockSpec(rightf.shape, lambda b, p: (0, 0, 0))],
        out_specs=[pl.BlockSpec((None, t, 128), lambda b, p: (b, 0, p)),
                   pl.BlockSpec((None, t, 128), lambda b, p: (b, 0, p)),
                   pl.BlockSpec((None, 2, t // CHUNK, CHUNK, CHUNK), lambda b, p: (b, p, 0, 0, 0))],
        out_shape=[out, out, jax.ShapeDtypeStruct((bsz, 4, t // CHUNK, CHUNK, CHUNK), F32)],
        compiler_params=_cparams(("parallel", "parallel")), name=name)(proj3, lbs_row, gn_row, w_all, maskf, rightf)


def _exact_dot_r(parts, hs, ones_h):
    acc = jnp.dot(parts[0][:, hs], ones_h, preferred_element_type=F32)
    for p in parts[1:]:
        acc = acc + jnp.dot(p[:, hs], ones_h, preferred_element_type=F32)
    return acc


def _hgrn_bwd(proj3, o_raw, dmixed, states, lbs_row, gn_row, name):
    bsz, t, _ = proj3.shape
    nt = t // LANES
    nchunk = t // CHUNK
    w_all, maskf, rightf = _hgrn_tables()

    def body(a_ref, or_ref, do_ref, s_sc, lb_ref, gn_ref, w_ref, mk_ref, rt_ref, da_ref, dgn_ref, dlb_ref):
        lb = lb_ref[...]
        gn = gn_ref[...]
        eye2, ones_blk, ones_all, first = _hgrn_consts()
        r_i, c_i = _iota((LANES, LANES), 0), _iota((LANES, LANES), 1)
        suffix = ((c_i >= r_i) & ((r_i // CHUNK) == (c_i // CHUNK))).astype(BF16)
        row64 = _iota((LANES, CHUNK), 0)
        zero = jnp.zeros((CHUNK, CHUNK), F32)

        def bwd_tile(k, carry):
            dst0, dst1, dgn_acc, dlb_acc = carry
            i = nt - 1 - k
            r0 = pl.multiple_of(i * LANES, LANES)
            a = a_ref[pl.ds(r0, LANES), :]
            qa, ga = a[:, 0:128], a[:, 384:512]
            qq, kk, lf, sg, sgn, fg = _hgrn_gates(a, lb)
            parts = _split(lf, 3)
            zall = _exact_dot(w_ref[...], parts)
            eb, ee = jnp.exp(zall[0:LANES]), jnp.exp(zall[LANES:2 * LANES])
            vb = a[:, 256:384].astype(BF16)
            oraw = or_ref[pl.ds(r0, LANES), :]
            dout = do_ref[pl.ds(r0, LANES), :]
            r = lax.rsqrt(_head_sums(oraw * oraw, ones_blk) * (1.0 / CHUNK) + NORM_EPS)
            xn = oraw * r
            dga = dout * (xn * gn) * _dsilu(ga)
            don = dout * _silu(ga)
            dgn_acc = dgn_acc + jnp.sum(don * xn, axis=0, keepdims=True)
            dxn = don * gn
            do = r * (dxn - xn * (_head_sums(dxn * xn, ones_blk) * (1.0 / CHUNK)))
            dob = do.astype(BF16)
            do2 = _stack_heads(dob, first)
            d_att = _dot(do2, vb, NT)
            att, dq, dk, db_lv = _hgrn_levels(qq, kk, zall, mk_ref, rt_ref, first, d_att)
            a2 = att + jnp.where(eye2, _head_sums(_stack_heads(qq * kk, first), ones_all), 0.0)
            dv_in = _dot(a2.astype(BF16), do2, TN)
            ddiag = _pick_heads(_head_sums(jnp.where(eye2, d_att, 0.0), ones_all), first)
            dq_in, dk_in = dq + ddiag * kk, dk + ddiag * qq
            qe_f, ke_f = qq * eb, kk * ee
            qeb, keb = qe_f.astype(BF16), ke_f.astype(BF16)
            new_ds, dq_h, dk_h, dv_h, dbl_h = [], [], [], [], []
            for h in range(2):
                hs = slice(CHUNK * h, CHUNK * (h + 1))
                dv, dq_i, dk_i = dv_in[:, hs], dq_in[:, hs], dk_in[:, hs]
                dst = (dst0, dst1)[h]
                dq_c, dk_c, dv_c, dbl_c = [None, None], [None, None], [None, None], [None, None]
                for c in (1, 0):
                    rc = slice(CHUNK * c, CHUNK * (c + 1))
                    st_n = s_sc[h, 2 * i + c]
                    ebl = eb[CHUNK * (c + 1) - 1:CHUNK * (c + 1), hs]
                    dstb = dst.astype(BF16)
                    dv_c[c] = _dot(keb[rc, hs], dstb, NT)
                    dke = jnp.dot(vb[rc, hs], dstb, preferred_element_type=F32)
                    dqe = jnp.dot(dob[rc, hs], st_n.astype(BF16), preferred_element_type=F32)
                    dbl_c[c] = (jnp.sum(dst * st_n, axis=0, keepdims=True) * ebl
                                + jnp.sum(dke * ke_f[rc, hs], axis=0, keepdims=True))
                    dq_c[c], dk_c[c] = dqe * eb[rc, hs], dke * ee[rc, hs]
                    dst = dst * ebl + _dot(dob[rc, hs], qeb[rc, hs], TN)
                new_ds.append(dst)
                dq_x, dk_x = jnp.concatenate(dq_c, axis=0), jnp.concatenate(dk_c, axis=0)
                dq_h.append(dq_i + dq_x)
                dk_h.append(dk_i + dk_x)
                dv_h.append(dv + jnp.concatenate(dv_c, axis=0))
                dbl_h.append(qq[:, hs] * dq_x - kk[:, hs] * dk_x
                             + jnp.where(row64 == CHUNK - 1, dbl_c[0], 0.0) + jnp.where(row64 == LANES - 1, dbl_c[1], 0.0))
            dqq = jnp.concatenate(dq_h, axis=1)
            dkk = jnp.concatenate(dk_h, axis=1)
            dvv = jnp.concatenate(dv_h, axis=1)
            db = db_lv + jnp.concatenate(dbl_h, axis=1)
            dlf = _exact_dot(suffix, _split(db, 3))
            dqa = dqq * _dsilu(qa)
            dfg = jnp.where(fg > TINY, dlf / fg, 0.0)
            dz = (dfg - dkk) * (1.0 - lb) * sg * sgn
            dlb_acc = dlb_acc + jnp.sum(dfg * (1.0 - sg) - dkk * sgn, axis=0, keepdims=True)
            da_ref[pl.ds(r0, LANES), :] = jnp.concatenate([dqa, dz, dvv, dga], axis=1)
            return new_ds[0], new_ds[1], dgn_acc, dlb_acc

        zrow = jnp.zeros((1, LANES), F32)
        per_step = 4 if nt % 4 == 0 else 2

        def step(k, carry):
            for r in range(per_step):
                carry = bwd_tile(per_step * k + r, carry)
            return carry

        _, _, dgn_acc, dlb_acc = lax.fori_loop(0, nt // per_step, step, (zero, zero, zrow, zrow))
        dgn_ref[...] = jnp.broadcast_to(dgn_acc, (8, LANES))
        dlb_ref[...] = jnp.broadcast_to(dlb_acc, (8, LANES))

    rows = jax.ShapeDtypeStruct((bsz, 8, HGRN_W), F32)
    row = pl.BlockSpec((1, 128), lambda b, p: (0, p))
    blk = pl.BlockSpec((None, t, 128), lambda b, p: (b, 0, p))
    return pl.pallas_call(
        body, grid=(bsz, 2),
        in_specs=[pl.BlockSpec((None, t, 512), lambda b, p: (b, 0, p)), blk, blk,
                  pl.BlockSpec((None, 2, nchunk, CHUNK, CHUNK), lambda b, p: (b, p, 0, 0, 0)), row, row,
                  pl.BlockSpec(w_all.shape, lambda b, p: (0, 0)),
                  pl.BlockSpec(maskf.shape, lambda b, p: (0, 0, 0)),
                  pl.BlockSpec(rightf.shape, lambda b, p: (0, 0, 0))],
        out_specs=[pl.BlockSpec((None, t, 512), lambda b, p: (b, 0, p)),
                   pl.BlockSpec((None, 8, 128), lambda b, p: (b, 0, p)),
                   pl.BlockSpec((None, 8, 128), lambda b, p: (b, 0, p))],
        out_shape=[jax.ShapeDtypeStruct((bsz, t, A_W), F32), rows, rows],
        compiler_params=_cparams(("parallel", "parallel")), name=name)(
            proj3, o_raw, dmixed, states, lbs_row, gn_row, w_all, maskf, rightf)


def _pool_tt(t):
    return min(256, t)


def _window_select(s2, s4, s8, s16, lane):
    return jnp.where(lane < 64, s2, jnp.where(lane < 128, s4, jnp.where(lane < 192, s8, s16)))


def _pool_counts(t0, tt):
    lane = _iota((tt, POOL_W), 1)
    tpos = (_iota((tt, POOL_W), 0) + t0 + 1).astype(F32)
    win = jnp.where(lane < 64, 2.0, jnp.where(lane < 128, 4.0, jnp.where(lane < 192, 8.0, 16.0)))
    return 1.0 / jnp.minimum(tpos, win), lane


def _pooled_tile(upad_ref, i, tt):
    r0 = pl.multiple_of(i * tt, 8)
    cat = upad_ref[pl.ds(r0, tt + POOL_HALO), :]
    s2 = cat + pltpu.roll(cat, 1, 0)
    s4 = s2 + pltpu.roll(s2, 2, 0)
    s8 = s4 + pltpu.roll(s4, 4, 0)
    s16 = s8 + pltpu.roll(s8, 8, 0)
    inv, lane = _pool_counts(i * tt, tt)
    sel = _window_select(s2[POOL_HALO:], s4[POOL_HALO:], s8[POOL_HALO:], s16[POOL_HALO:], lane)
    return sel * inv - cat[POOL_HALO:], inv, lane


def _pool_fwd(proj3, wbd, scale_row, name):
    bsz, t, _ = proj3.shape
    tt = _pool_tt(t)

    def body(p_ref, w_ref, sc_ref, o_ref, upad):
        upad[0:POOL_HALO, :] = jnp.zeros((POOL_HALO, POOL_W), F32)
        upad[POOL_HALO:, :] = p_ref[:, 0:POOL_W]
        w = w_ref[...]
        sc = sc_ref[...]

        def tile(i, c):
            pooled, _, _ = _pooled_tile(upad, i, tt)
            r0 = pl.multiple_of(i * tt, 8)
            g = p_ref[pl.ds(r0, tt), POOL_W:2 * POOL_W]
            pre = jnp.dot(pooled.astype(BF16), w, preferred_element_type=F32)
            o_ref[pl.ds(r0, tt), :] = pre * sc * _silu(g)
            return c

        lax.fori_loop(0, t // tt, tile, 0)

    return pl.pallas_call(
        body, grid=(bsz,),
        in_specs=[pl.BlockSpec((None, t, 512), lambda b: (b, 0, B_BLK)),
                  pl.BlockSpec((POOL_W, POOL_W), lambda b: (0, 0)),
                  pl.BlockSpec((1, POOL_W), lambda b: (0, 0))],
        out_specs=pl.BlockSpec((None, t, POOL_W), lambda b: (b, 0, 0)),
        out_shape=jax.ShapeDtypeStruct((bsz, t, POOL_W), F32),
        scratch_shapes=[pltpu.VMEM((t + POOL_HALO, POOL_W), F32)],
        compiler_params=_cparams(("parallel",)), name=name)(proj3, wbd, scale_row)


def _pool_bwd(proj3, dmixed, wbd, scale_row, name):
    bsz, t, _ = proj3.shape
    tt = _pool_tt(t)

    def body(p_ref, do_ref, w_ref, sc_ref, db_ref, dsc_ref, dw_ref, upad, epad):
        upad[0:POOL_HALO, :] = jnp.zeros((POOL_HALO, POOL_W), F32)
        upad[POOL_HALO:, :] = p_ref[:, 0:POOL_W]
        epad[t:, :] = jnp.zeros((POOL_HALO, POOL_W), F32)
        w = w_ref[...]
        sc = sc_ref[...]

        def tile(i, carry):
            dsc_acc, dw_acc = carry
            pooled, inv, _ = _pooled_tile(upad, i, tt)
            r0 = pl.multiple_of(i * tt, 8)
            g = p_ref[pl.ds(r0, tt), POOL_W:2 * POOL_W]
            dout = do_ref[pl.ds(r0, tt), :]
            pb = pooled.astype(BF16)
            pre = jnp.dot(pb, w, preferred_element_type=F32)
            t1 = dout * _silu(g)
            dsc_acc = dsc_acc + jnp.sum(t1 * pre, axis=0, keepdims=True)
            dpre = (t1 * sc).astype(BF16)
            db_ref[pl.ds(r0, tt), POOL_W:2 * POOL_W] = dout * pre * sc * _dsilu(g)
            dw_acc = dw_acc + _dot(pb, dpre, TN)
            dpooled = _dot(dpre, w, NT)
            epad[pl.ds(r0, tt), :] = dpooled * inv
            return dsc_acc, dw_acc

        dsc_acc, dw_acc = lax.fori_loop(0, t // tt, tile, (jnp.zeros((1, POOL_W), F32), jnp.zeros((POOL_W, POOL_W), F32)))
        dsc_ref[...] = jnp.broadcast_to(dsc_acc, (8, POOL_W))
        dw_ref[...] = dw_acc

        def tile2(i, c):
            r0 = pl.multiple_of(i * tt, 8)
            n = tt + POOL_HALO
            cat = epad[pl.ds(r0, n), :]
            s2 = cat + pltpu.roll(cat, n - 1, 0)
            s4 = s2 + pltpu.roll(s2, n - 2, 0)
            s8 = s4 + pltpu.roll(s4, n - 4, 0)
            s16 = s8 + pltpu.roll(s8, n - 8, 0)
            inv, lane = _pool_counts(i * tt, tt)
            sel = _window_select(s2[:tt], s4[:tt], s8[:tt], s16[:tt], lane)
            db_ref[pl.ds(r0, tt), 0:POOL_W] = sel - cat[:tt] / inv
            return c

        lax.fori_loop(0, t // tt, tile2, 0)

    return pl.pallas_call(
        body, grid=(bsz,),
        in_specs=[pl.BlockSpec((None, t, 512), lambda b: (b, 0, B_BLK)),
                  pl.BlockSpec((None, t, POOL_W), lambda b: (b, 0, 1)),
                  pl.BlockSpec((POOL_W, POOL_W), lambda b: (0, 0)),
                  pl.BlockSpec((1, POOL_W), lambda b: (0, 0))],
        out_specs=[pl.BlockSpec((None, t, 512), lambda b: (b, 0, 0)),
                   pl.BlockSpec((None, 8, POOL_W), lambda b: (b, 0, 0)),
                   pl.BlockSpec((None, POOL_W, POOL_W), lambda b: (b, 0, 0))],
        out_shape=[jax.ShapeDtypeStruct((bsz, t, B_W), F32), jax.ShapeDtypeStruct((bsz, 8, POOL_W), F32),
                   jax.ShapeDtypeStruct((bsz, POOL_W, POOL_W), F32)],
        scratch_shapes=[pltpu.VMEM((t + POOL_HALO, POOL_W), F32), pltpu.VMEM((t + POOL_HALO, POOL_W), F32)],
        compiler_params=_cparams(("parallel",)), name=name)(proj3, dmixed, wbd, scale_row)


def _head_select_rows(hp):
    r, c = _iota((8, LANES), 0), _iota((8, LANES), 1)
    return ((r < 2) & (c == 2 * hp + r)).astype(F32)


def _foxgate_fwd(proj3, bias_row, name):
    bsz, t, _ = proj3.shape
    nt = t // LANES

    def body(f_ref, b_ref, cn_ref, ct_ref):
        bias = b_ref[...]
        i, j = _iota((LANES, LANES), 0), _iota((LANES, LANES), 1)
        lower = (j <= i).astype(BF16)
        spread = (_iota((LANES, FOX_W), 0) == _iota((LANES, FOX_W), 1) // 64).astype(BF16)
        select = [_head_select_rows(hp).astype(BF16) for hp in range(4)]
        offset = jnp.zeros((1, LANES), F32)
        for k in range(nt):
            rows = slice(k * LANES, (k + 1) * LANES)
            xg = f_ref[rows, :] + bias
            lf = jnp.minimum(xg, 0.0) - jnp.log(1.0 + jnp.exp(-jnp.abs(xg)))
            c = _exact_dot(lower, _split(lf, 3)) + offset
            offset = c[LANES - 1:LANES, :]
            parts = _split(c, 3)
            cn_ref[rows, :] = _head_sums(c, spread, 3)
            for hp in range(4):
                acc = _dot(select[hp], parts[0], NT)
                for p in parts[1:]:
                    acc = acc + _dot(select[hp], p, NT)
                ct_ref[hp, :, rows] = acc

    return pl.pallas_call(
        body, grid=(bsz,),
        in_specs=[pl.BlockSpec((None, t, 128), lambda b: (b, 0, F_BLK)), pl.BlockSpec((1, 128), lambda b: (0, 0))],
        out_specs=[pl.BlockSpec((None, t, FOX_W), lambda b: (b, 0, 0)),
                   pl.BlockSpec((None, 4, 8, t), lambda b: (b, 0, 0, 0))],
        out_shape=[jax.ShapeDtypeStruct((bsz, t, FOX_W), F32), jax.ShapeDtypeStruct((bsz, 4, 8, t), F32)],
        compiler_params=_cparams(("parallel",)), name=name)(proj3, bias_row)


def _foxgate_bwd(proj3, dc_nat, bias_row, name):
    bsz, t, _ = proj3.shape
    nt = t // LANES

    def body(f_ref, dc_ref, b_ref, df_ref, dbias_ref, run_sc):
        bias = b_ref[...]
        i, j = _iota((LANES, LANES), 0), _iota((LANES, LANES), 1)
        upper = (j >= i).astype(F32)
        valid = _iota((1, LANES), 1) < FOX_HEADS
        run_sc[...] = jnp.zeros((8, LANES), F32)
        dbias_ref[...] = jnp.zeros((8, LANES), F32)

        def tile(k, c):
            r0 = pl.multiple_of((nt - 1 - k) * LANES, LANES)
            dc = dc_ref[pl.ds(r0, LANES), :] + jnp.where(i == LANES - 1, run_sc[0:1, :], 0.0)
            dlf = jnp.dot(upper, dc, precision=HI, preferred_element_type=F32)
            xg = f_ref[pl.ds(r0, LANES), :] + bias
            df = jnp.where(valid, dlf * _sig(-xg), 0.0)
            df_ref[pl.ds(r0, LANES), :] = df
            run_sc[...] = dlf[0:8, :]
            dbias_ref[...] += jnp.sum(df, axis=0, keepdims=True)
            return c

        lax.fori_loop(0, nt, tile, 0)

    blk = pl.BlockSpec((None, t, 128), lambda b: (b, 0, 0))
    return pl.pallas_call(
        body, grid=(bsz,),
        in_specs=[pl.BlockSpec((None, t, 128), lambda b: (b, 0, F_BLK)), blk, pl.BlockSpec((1, 128), lambda b: (0, 0))],
        out_specs=[blk, pl.BlockSpec((None, 8, 128), lambda b: (b, 0, 0))],
        out_shape=[jax.ShapeDtypeStruct((bsz, t, F_W), F32), jax.ShapeDtypeStruct((bsz, 8, 128), F32)],
        scratch_shapes=[pltpu.VMEM((8, LANES), F32)],
        compiler_params=_cparams(("parallel",)), name=name)(proj3, dc_nat, bias_row)


def _fox_tile(t):
    return min(256, t)


def _fox_fwd(proj3, c_nat, c_t, name):
    bsz, t, _ = proj3.shape
    tq = tk = min(2 * _fox_tile(t), t)
    nq = t // tq

    def body(q_ref, kv_ref, cn_ref, ct_ref, og_ref, or_ref, lse_ref):
        i = pl.program_id(2)
        qblk = q_ref[...]
        first = _iota((1, 128), 1) < 64
        qv = qblk[:, 0:128] * 0.125
        qm = [jnp.where(first, qv, 0.0).astype(BF16), jnp.where(first, 0.0, qv).astype(BF16)]
        cqs = [cn_ref[:, 0:1], cn_ref[:, 64:65]]
        rows = _iota((tq, tk), 0) + i * tq

        def scores(j):
            c0 = pl.multiple_of(j * tk, tk)
            kb = kv_ref[pl.ds(c0, tk), 128:256].astype(BF16)
            return tuple(_dot(qm[h], kb, NT) + (cqs[h] - ct_ref[h:h + 1, pl.ds(c0, tk)]) for h in range(2))

        def absorb(j, state, s01, masked):
            c0 = pl.multiple_of(j * tk, tk)
            vblk = kv_ref[pl.ds(c0, tk), 256:384]
            vx = [jnp.where(first, vblk, 1.0).astype(BF16), jnp.where(first, 1.0, vblk).astype(BF16)]
            new = []
            for h in range(2):
                m, acc, s = state[2 * h], state[2 * h + 1], s01[h]
                if masked:
                    s = jnp.where(rows >= _iota((tq, tk), 1) + j * tk, s, MASK_VALUE)
                m_new = jnp.maximum(m, jnp.max(s, axis=1, keepdims=True))
                p = jnp.exp(s - m_new).astype(BF16)
                new += [m_new, jnp.exp(m - m_new) * acc + jnp.dot(p, vx[h], preferred_element_type=F32)]
            return tuple(new)

        init = (jnp.full((tq, 1), MASK_VALUE, F32), jnp.zeros((tq, 128), F32)) * 2
        n_full = (i * tq) // tk
        state = lax.fori_loop(0, n_full, lambda j, state: absorb(j, state, scores(j), False), init)
        m0, acc0, m1, acc1 = absorb(n_full, state, scores(n_full), True)
        l0, l1 = pltpu.roll(acc0, 64, 1), pltpu.roll(acc1, 64, 1)
        o = jnp.where(first, acc0 / l0, acc1 / l1)
        or_ref[...] = o
        og_ref[...] = o * _silu(qblk[:, 384:512])
        lse_ref[...] = jnp.where(first, m0 + jnp.log(l0), m1 + jnp.log(l1))

    out = jax.ShapeDtypeStruct((bsz, t, FOX_W), F32)
    blk = pl.BlockSpec((None, tq, 128), lambda b, p, i: (b, i, p))
    return pl.pallas_call(
        body, grid=(bsz, 4, nq),
        in_specs=[pl.BlockSpec((None, tq, 512), lambda b, p, i: (b, i, C_BLK0 + p)),
                  pl.BlockSpec((None, t, 512), lambda b, p, i: (b, 0, C_BLK0 + p)),
                  blk,
                  pl.BlockSpec((None, None, 8, t), lambda b, p, i: (b, p, 0, 0))],
        out_specs=[blk, blk, blk],
        out_shape=[out, out, out],
        compiler_params=_cparams(("parallel", "parallel", "arbitrary")), name=name)(proj3, proj3, c_nat, c_t)


def _fox_bwd(proj3, o_raw, dmixed, lse, c_nat, c_t, name):
    bsz, t, _ = proj3.shape
    tq = tk = min(2 * _fox_tile(t), t)
    nq = t // tq
    ratio = tk // tq

    def body(a_ref, or_ref, do_ref, lse_ref, cn_ref, ct_ref, dc_out, dct_out, drow_out, dq_sc, do_sc, dl_sc):
        def prep(i, c):
            r0 = pl.multiple_of(i * tq, tq)
            g = a_ref[pl.ds(r0, tq), 384:512]
            dout = do_ref[pl.ds(r0, tq), :]
            o = or_ref[pl.ds(r0, tq), :]
            dc_out[pl.ds(r0, tq), 384:512] = dout * o * _dsilu(g)
            do = dout * _silu(g)
            do_sc[pl.ds(r0, tq), :] = do
            prod = do * o
            d0 = jnp.sum(prod[:, 0:64], axis=1, keepdims=True)
            d1 = jnp.sum(prod[:, 64:128], axis=1, keepdims=True)
            dl_sc[pl.ds(r0, tq), :] = jnp.concatenate([jnp.broadcast_to(d0, (tq, 64)), jnp.broadcast_to(d1, (tq, 64))], axis=1)
            dq_sc[pl.ds(r0, tq), :] = jnp.zeros((tq, 128), F32)
            drow_out[pl.ds(r0, tq), :] = jnp.zeros((tq, 128), F32)
            return c

        lax.fori_loop(0, nq, prep, 0)
        dct_out[...] = jnp.zeros((8, t), F32)

        first = _iota((1, 128), 1) < 64

        def heads(v):
            return [jnp.where(first, v, 0.0).astype(BF16), jnp.where(first, 0.0, v).astype(BF16)]

        def kv_tile(j, c):
            c0 = pl.multiple_of(j * tk, tk)
            kb = a_ref[pl.ds(c0, tk), 128:256].astype(BF16)
            vb = a_ref[pl.ds(c0, tk), 256:384].astype(BF16)
            cks = [ct_ref[h:h + 1, pl.ds(c0, tk)] for h in range(2)]

            def q_step(i, carry, diagonal):
                dk, dv, dcol0, dcol1 = carry
                r0 = pl.multiple_of(i * tq, tq)
                causal = _iota((tq, tk), 0) + i * tq >= _iota((tq, tk), 1) + j * tk
                qv = a_ref[pl.ds(r0, tq), 0:128] * 0.125
                do = do_sc[pl.ds(r0, tq), :]
                qb, dob = qv.astype(BF16), do.astype(BF16)
                qm, dom = heads(qv), heads(do)
                full, dcols, rsums = [], [], []
                for h in range(2):
                    lse_h = lse_ref[pl.ds(r0, tq), 64 * h:64 * h + 1]
                    dl_h = dl_sc[pl.ds(r0, tq), 64 * h:64 * h + 1]
                    cq = cn_ref[pl.ds(r0, tq), 64 * h:64 * h + 1]
                    p = jnp.exp(_dot(qm[h], kb, NT) + (cq - cks[h]) - lse_h)
                    if diagonal:
                        p = jnp.where(causal, p, 0.0)
                    ds = p * (_dot(dom[h], vb, NT) - dl_h)
                    dsb = ds.astype(BF16)
                    full.append((_dot(p.astype(BF16), dob, TN), _dot(dsb, qb, TN),
                                 jnp.dot(dsb, kb, preferred_element_type=F32)))
                    dcols.append(jnp.sum(ds, axis=0, keepdims=True))
                    rsums.append(jnp.broadcast_to(jnp.sum(ds, axis=1, keepdims=True), (tq, 128)))
                dq_sc[pl.ds(r0, tq), :] += jnp.where(first, full[0][2], full[1][2]) * 0.125
                drow_out[pl.ds(r0, tq), :] += jnp.where(first, rsums[0], rsums[1])
                return (dk + jnp.where(first, full[0][1], full[1][1]), dv + jnp.where(first, full[0][0], full[1][0]),
                        dcol0 - dcols[0], dcol1 - dcols[1])

            carry = (jnp.zeros((tk, 128), F32), jnp.zeros((tk, 128), F32), jnp.zeros((1, tk), F32), jnp.zeros((1, tk), F32))
            for r in range(ratio):
                carry = q_step(ratio * j + r, carry, True)
            dk, dv, dcol0, dcol1 = lax.fori_loop(ratio * (j + 1), nq, functools.partial(q_step, diagonal=False), carry)
            dct_out[0:1, pl.ds(c0, tk)] = dcol0
            dct_out[1:2, pl.ds(c0, tk)] = dcol1
            dc_out[pl.ds(c0, tk), 128:256] = dk
            dc_out[pl.ds(c0, tk), 256:384] = dv
            return c

        lax.fori_loop(0, t // tk, kv_tile, 0)
        dc_out[:, 0:128] = dq_sc[...]

    blk = pl.BlockSpec((None, t, 128), lambda b, p: (b, 0, p))
    return pl.pallas_call(
        body, grid=(bsz, 4),
        in_specs=[pl.BlockSpec((None, t, 512), lambda b, p: (b, 0, C_BLK0 + p)),
                  blk,
                  pl.BlockSpec((None, t, 128), lambda b, p: (b, 0, 4 + p)),
                  blk, blk,
                  pl.BlockSpec((None, None, 8, t), lambda b, p: (b, p, 0, 0))],
        out_specs=[pl.BlockSpec((None, t, 512), lambda b, p: (b, 0, p)),
                   pl.BlockSpec((None, None, 8, t), lambda b, p: (b, p, 0, 0)), blk],
        out_shape=[jax.ShapeDtypeStruct((bsz, t, C_W), F32), jax.ShapeDtypeStruct((bsz, 4, 8, t), F32),
                   jax.ShapeDtypeStruct((bsz, t, FOX_W), F32)],
        scratch_shapes=[pltpu.VMEM((t, 128), F32), pltpu.VMEM((t, 128), F32), pltpu.VMEM((t, 128), F32)],
        compiler_params=_cparams(("parallel", "parallel")), name=name)(proj3, o_raw, dmixed, lse, c_nat, c_t)


def _mix_tm(n):
    return min(512, n)


def _outproj_fwd(x2, oa, ob, oc, wo, g_row, name):
    n, d = x2.shape
    tm = _mix_tm(n)

    def body(x_ref, oa_ref, ob_ref, oc_ref, w_ref, g_ref, y_ref, xo_ref):
        y = (jnp.dot(oa_ref[...].astype(BF16), w_ref[0:256, :], preferred_element_type=F32)
             + jnp.dot(ob_ref[...].astype(BF16), w_ref[256:512, :], preferred_element_type=F32)
             + jnp.dot(oc_ref[...].astype(BF16), w_ref[512:1024, :], preferred_element_type=F32))
        y_ref[...] = y
        xo_ref[...] = x_ref[...] + y * _rstd(y) * g_ref[...]

    row = lambda w: pl.BlockSpec((tm, w), lambda i: (i, 0))
    out = jax.ShapeDtypeStruct((n, d), F32)
    return pl.pallas_call(
        body, grid=(n // tm,),
        in_specs=[row(d), row(256), row(256), row(512), pl.BlockSpec((d, d), lambda i: (0, 0)),
                  pl.BlockSpec((1, d), lambda i: (0, 0))],
        out_specs=[row(d), row(d)], out_shape=[out, out],
        compiler_params=_cparams(("parallel",)), name=name)(x2, oa, ob, oc, wo, g_row)


def _outproj_fwd_loss(x2, oa, ob, oc, wo, g_row, target2, name):
    n, d = x2.shape
    tm = _mix_tm(n)

    def body(x_ref, oa_ref, ob_ref, oc_ref, w_ref, g_ref, t_ref, y_ref, dx_ref, l_ref):
        y = (jnp.dot(oa_ref[...].astype(BF16), w_ref[0:256, :], preferred_element_type=F32)
             + jnp.dot(ob_ref[...].astype(BF16), w_ref[256:512, :], preferred_element_type=F32)
             + jnp.dot(oc_ref[...].astype(BF16), w_ref[512:1024, :], preferred_element_type=F32))
        y_ref[...] = y
        err = (x_ref[...] + y * _rstd(y) * g_ref[...]) - t_ref[...]
        dx_ref[...] = err * (1.0 / d)

        @pl.when(pl.program_id(0) == 0)
        def _():
            l_ref[...] = jnp.zeros((8, 128), F32)

        l_ref[...] += jnp.sum(err * err)

    row = lambda w: pl.BlockSpec((tm, w), lambda i: (i, 0))
    out = jax.ShapeDtypeStruct((n, d), F32)
    return pl.pallas_call(
        body, grid=(n // tm,),
        in_specs=[row(d), row(256), row(256), row(512), pl.BlockSpec((d, d), lambda i: (0, 0)),
                  pl.BlockSpec((1, d), lambda i: (0, 0)), row(d)],
        out_specs=[row(d), row(d), pl.BlockSpec((8, 128), lambda i: (0, 0))],
        out_shape=[out, out, jax.ShapeDtypeStruct((8, 128), F32)],
        compiler_params=_cparams(("arbitrary",)), name=name)(x2, oa, ob, oc, wo, g_row, target2)


def _outproj_bwd(dxo, y, oa, ob, oc, wo, g_row, name):
    n, d = dxo.shape
    tm = _mix_tm(n)

    def body(dx_ref, y_ref, oa_ref, ob_ref, oc_ref, w_ref, g_ref, dm_ref, dw_ref, dg_ref):
        @pl.when(pl.program_id(0) == 0)
        def _():
            dw_ref[...] = jnp.zeros((d, d), F32)
            dg_ref[...] = jnp.zeros((8, d), F32)

        yv, dx = y_ref[...], dx_ref[...]
        r = _rstd(yv)
        yn = yv * r
        dg_ref[...] += jnp.sum(dx * yn, axis=0, keepdims=True)
        dyn = dx * g_ref[...]
        dy = (r * (dyn - yn * jnp.mean(dyn * yn, axis=-1, keepdims=True))).astype(BF16)
        dm_ref[...] = _dot(dy, w_ref[...], NT)
        dw_ref[0:256, :] += _dot(oa_ref[...].astype(BF16), dy, TN)
        dw_ref[256:512, :] += _dot(ob_ref[...].astype(BF16), dy, TN)
        dw_ref[512:1024, :] += _dot(oc_ref[...].astype(BF16), dy, TN)

    row = lambda w: pl.BlockSpec((tm, w), lambda i: (i, 0))
    fixed = lambda r, c: pl.BlockSpec((r, c), lambda i: (0, 0))
    return pl.pallas_call(
        body, grid=(n // tm,),
        in_specs=[row(d), row(d), row(256), row(256), row(512), fixed(d, d), fixed(1, d)],
        out_specs=[row(d), fixed(d, d), fixed(8, d)],
        out_shape=[jax.ShapeDtypeStruct((n, d), F32), jax.ShapeDtypeStruct((d, d), F32), jax.ShapeDtypeStruct((8, d), F32)],
        compiler_params=_cparams(("arbitrary",)), name=name)(dxo, y, oa, ob, oc, wo, g_row)


_PIECES = ((0, A_W), (A_W, B_W), (A_W + B_W, C_W), (A_W + B_W + C_W, F_W))


def _inproj_bwd_x(x2, dxo, g_row, w_int, pieces, name):
    n, d = x2.shape
    tm = min(256, n)

    def body(x_ref, dxo_ref, g_ref, w_ref, da_ref, db_ref, dc_ref, df_ref, dx_ref, dg_ref):
        @pl.when(pl.program_id(0) == 0)
        def _():
            dg_ref[...] = jnp.zeros((8, d), F32)

        dh = jnp.zeros((tm, d), F32)
        for ref, (o, w) in zip((da_ref, db_ref, dc_ref, df_ref), _PIECES):
            dh = dh + _dot(ref[...].astype(BF16), w_ref[:, o:o + w], NT)
        x = x_ref[...]
        r = _rstd(x)
        xn = x * r
        dg_ref[...] += jnp.sum(dh * xn, axis=0, keepdims=True)
        dxn = dh * g_ref[...]
        dx_ref[...] = dxo_ref[...] + r * (dxn - xn * jnp.mean(dxn * xn, axis=-1, keepdims=True))

    row = lambda w: pl.BlockSpec((tm, w), lambda i: (i, 0))
    fixed = lambda r, c: pl.BlockSpec((r, c), lambda i: (0, 0))
    return pl.pallas_call(
        body, grid=(n // tm,),
        in_specs=[row(d), row(d), fixed(1, d), fixed(d, E_INT)] + [row(w) for _, w in _PIECES],
        out_specs=[row(d), fixed(8, d)],
        out_shape=[jax.ShapeDtypeStruct((n, d), F32), jax.ShapeDtypeStruct((8, d), F32)],
        compiler_params=_cparams(("arbitrary",)), name=name)(x2, dxo, g_row, w_int, *pieces)


def _inproj_bwd_w(x2, g_row, pieces, name):
    n, d = x2.shape
    tm = min(256, n)

    def body(x_ref, g_ref, da_ref, db_ref, dc_ref, df_ref, dw_ref):
        @pl.when(pl.program_id(0) == 0)
        def _():
            dw_ref[...] = jnp.zeros((d, E_INT), F32)

        x = x_ref[...]
        h = (x * _rstd(x) * g_ref[...]).astype(BF16)
        for ref, (o, w) in zip((da_ref, db_ref, dc_ref, df_ref), _PIECES):
            dw_ref[:, o:o + w] += _dot(h, ref[...].astype(BF16), TN)

    row = lambda w: pl.BlockSpec((tm, w), lambda i: (i, 0))
    return pl.pallas_call(
        body, grid=(n // tm,),
        in_specs=[row(d), pl.BlockSpec((1, d), lambda i: (0, 0))] + [row(w) for _, w in _PIECES],
        out_specs=pl.BlockSpec((d, E_INT), lambda i: (0, 0)),
        out_shape=jax.ShapeDtypeStruct((d, E_INT), F32),
        compiler_params=_cparams(("arbitrary",), vmem_mb=56), name=name)(x2, g_row, *pieces)


def _block_diag(pool_w_l):
    z = jnp.zeros((64, 64), pool_w_l.dtype)
    return jnp.concatenate(
        [jnp.concatenate([pool_w_l[g] if c == g else z for c in range(4)], axis=1) for g in range(4)], axis=0)


def _pad_lanes(v, width=128):
    return jnp.pad(v, ((0, 0),) * (v.ndim - 1) + ((0, width - v.shape[-1]),))


def _local_step(x, target, lower_bounds, pre_norm_g, w_in_int, hgrn_norm_g, fox_f_bias, pool_w, pool_scale,
                w_out_bf, post_norm_g, on_weight_grads):
    bsz, t, d = x.shape
    n = bsz * t
    lbs = _lbs_fwd(lower_bounds)
    saved = []
    xc = x.reshape(n, d)
    for l in range(DEPTH):
        proj = _inproj_fwd(xc, pre_norm_g[l:l + 1], w_in_int[l], f"inproj_fwd{l}").reshape(bsz, t, E_INT)
        wbd = _block_diag(pool_w[l]).astype(BF16)
        bias_row = _pad_lanes(fox_f_bias[l:l + 1])
        oa, oa_raw, states = _hgrn_fwd(proj, lbs[l:l + 1], hgrn_norm_g[l:l + 1], f"hgrn_fwd{l}")
        ob = _pool_fwd(proj, wbd, pool_scale[l:l + 1], f"pool_fwd{l}")
        c_nat, c_t = _foxgate_fwd(proj, bias_row, f"foxgate_fwd{l}")
        oc, oc_raw, lse = _fox_fwd(proj, c_nat, c_t, f"fox_fwd{l}")
        mixed = (oa.reshape(n, -1), ob.reshape(n, -1), oc.reshape(n, -1))
        if l < DEPTH - 1:
            y, xn = _outproj_fwd(xc, *mixed, w_out_bf[l], post_norm_g[l:l + 1], f"outproj_fwd{l}")
        else:
            y, dx, sq = _outproj_fwd_loss(xc, *mixed, w_out_bf[l], post_norm_g[l:l + 1], target.reshape(n, d),
                                          f"outproj_fwd{l}")
        saved.append((xc, proj, wbd, bias_row, oa, oa_raw, states, ob, oc, oc_raw, lse, c_nat, c_t, y))
        xc = xn
    g = {k: [None] * DEPTH for k in ("pre", "hgn", "bias", "pool_w", "pool_scale", "post", "lbs")}
    handed = [None] * DEPTH
    for l in reversed(range(DEPTH)):
        xin, proj, wbd, bias_row, oa, oa_raw, states, ob, oc, oc_raw, lse, c_nat, c_t, y = saved[l]
        dmix, d_w_out, dpost = _outproj_bwd(dx, y, oa.reshape(n, -1), ob.reshape(n, -1), oc.reshape(n, -1),
                                            w_out_bf[l], post_norm_g[l:l + 1], f"outproj_bwd{l}")
        g["post"][l] = dpost[0]
        dmix3 = dmix.reshape(bsz, t, d)
        d_c, dct, drow = _fox_bwd(proj, oc_raw, dmix3, lse, c_nat, c_t, f"fox_bwd{l}")
        dc_nat = _pad_lanes(dct[:, :, 0:2, :].reshape(bsz, FOX_HEADS, t).transpose(0, 2, 1)
                            + drow.reshape(bsz, t, FOX_HEADS, 64)[..., 0])
        d_f, dbias = _foxgate_bwd(proj, dc_nat, bias_row, f"foxgate_bwd{l}")
        g["bias"][l] = jnp.sum(dbias[:, 0, :FOX_HEADS], axis=0)
        d_b, dscale, dwbd = _pool_bwd(proj, dmix3, wbd, pool_scale[l:l + 1], f"pool_bwd{l}")
        g["pool_scale"][l] = jnp.sum(dscale[:, 0], axis=0)
        dwbd = jnp.sum(dwbd, axis=0)
        g["pool_w"][l] = jnp.stack([dwbd[64 * k:64 * (k + 1), 64 * k:64 * (k + 1)] for k in range(4)])
        d_a, dgn, dlb = _hgrn_bwd(proj, oa_raw, dmix3, states, lbs[l:l + 1], hgrn_norm_g[l:l + 1], f"hgrn_bwd{l}")
        g["hgn"][l] = jnp.sum(dgn[:, 0], axis=0)
        g["lbs"][l] = jnp.sum(dlb[:, 0], axis=0)
        pieces = [p.reshape(n, -1) for p in (d_a, d_b, d_c, d_f)]
        handed[l] = on_weight_grads(l, _inproj_bwd_w(xin, pre_norm_g[l:l + 1], pieces, f"inproj_bwd_w{l}"), d_w_out)
        dx, dpre = _inproj_bwd_x(xin, dx, pre_norm_g[l:l + 1], w_in_int[l], pieces, f"inproj_bwd_x{l}")
        g["pre"][l] = dpre[0]
    grads = {k: jnp.stack(v) for k, v in g.items()}
    return sq, dx.reshape(bsz, t, d), grads, handed


def _place():
    return lax.axis_index("x"), lax.axis_index("y"), lax.axis_index("c")


def _other_chips(x, y):
    return [(1 - x, y), (x, 1 - y), (1 - x, 1 - y)]


_ANY = pl.BlockSpec(memory_space=pl.ANY)


def _gather_body(handshake, n_arrays):
    def body(*refs):
        srcs, dsts = refs[:n_arrays], refs[n_arrays:2 * n_arrays]
        ici_send, ici_recv, d2d_send, d2d_recv, local_sems = refs[2 * n_arrays:]
        x, y, c = _place()
        if handshake:
            barrier = pltpu.get_barrier_semaphore()
            for peer in [(px, py, c) for px, py in _other_chips(x, y)] + [(x, y, 1 - c)]:
                pl.semaphore_signal(barrier, inc=1, device_id=peer, device_id_type=MESH)
            pl.semaphore_wait(barrier, 4)
        me = 2 * x + y
        pairs = list(zip(srcs, dsts))
        order = [(k, j) for k in range(3) for j in range(n_arrays)]
        mine = [pltpu.make_async_copy(src, dst.at[me], local_sems.at[j]) for j, (src, dst) in enumerate(pairs)]
        for cp in mine:
            cp.start()
        chips = _other_chips(x, y)
        sends = [pltpu.make_async_remote_copy(
            src_ref=pairs[j][0].at[c], dst_ref=pairs[j][1].at[me, c], send_sem=ici_send.at[n], recv_sem=ici_recv.at[n],
            device_id=(chips[k][0], chips[k][1], c), device_id_type=MESH) for n, (k, j) in enumerate(order)]
        for cp in sends:
            cp.start()
        passed = [pltpu.make_async_remote_copy(
            src_ref=pairs[j][1].at[2 * chips[k][0] + chips[k][1], c], dst_ref=pairs[j][1].at[2 * chips[k][0] + chips[k][1], c],
            send_sem=d2d_send.at[n], recv_sem=d2d_recv.at[n], device_id=(x, y, 1 - c), device_id_type=MESH)
            for n, (k, j) in enumerate(order)]
        for n, (k, j) in enumerate(order):
            px, py = chips[k]
            src, dst = pairs[j]
            pltpu.make_async_remote_copy(
                src_ref=src.at[c], dst_ref=dst.at[2 * px + py, c], send_sem=ici_send.at[n], recv_sem=ici_recv.at[n],
                device_id=(px, py, c), device_id_type=MESH).wait_recv()
            passed[n].start()
        for n, (k, j) in enumerate(order):
            px, py = chips[k]
            src, dst = pairs[j]
            pltpu.make_async_remote_copy(
                src_ref=dst.at[2 * px + py, 1 - c], dst_ref=dst.at[2 * px + py, 1 - c], send_sem=d2d_send.at[n],
                recv_sem=d2d_recv.at[n], device_id=(x, y, 1 - c), device_id_type=MESH).wait_recv()
        for cp in sends + passed:
            cp.wait_send()
        for cp in mine:
            cp.wait()

    return body


def _gather_sems(n_arrays):
    return [pltpu.SemaphoreType.DMA((3 * n_arrays,))] * 4 + [pltpu.SemaphoreType.DMA((n_arrays,))]


def _gathered(a):
    return jax.ShapeDtypeStruct((N_CHIPS,) + a.shape, a.dtype)


def _gather_weights(arrays):
    n = len(arrays)
    return pl.pallas_call(
        _gather_body(False, n), in_specs=[_ANY] * n, out_specs=[_ANY] * n, out_shape=[_gathered(a) for a in arrays],
        scratch_shapes=_gather_sems(n), name="gather_weights")(*arrays)


def _gather_weights_beside(arrays):
    hbm = pltpu.MemorySpace.HBM
    n = len(arrays)
    srcs = [jax.new_ref(a, memory_space=hbm) for a in arrays]
    dsts = [jax.empty_ref(_gathered(a), memory_space=hbm) for a in arrays]
    body = _gather_body(True, n)

    @pl.kernel(mesh=plsc.ScalarSubcoreMesh(axis_name="sequencer", num_cores=1), name="gather_weights_beside",
               scratch_types=_gather_sems(n), compiler_params=pltpu.CompilerParams(collective_id=1))
    def launch(*sems):
        body(*srcs, *dsts, *sems)

    launch()
    return [d[...] for d in dsts]


def _swap_with_sibling(parts, name):
    k = len(parts)

    def body(*refs):
        src, dst = refs[:k], refs[k:2 * k]
        send_sems, recv_sems = refs[2 * k:]
        x, y, c = _place()
        cps = [pltpu.make_async_remote_copy(src_ref=src[j], dst_ref=dst[j], send_sem=send_sems.at[j], recv_sem=recv_sems.at[j],
                                            device_id=(x, y, 1 - c), device_id_type=MESH) for j in range(k)]
        for cp in cps:
            cp.start()
        for cp in cps:
            cp.wait()

    return pl.pallas_call(
        body, in_specs=[_ANY] * k, out_specs=[_ANY] * k,
        out_shape=[jax.ShapeDtypeStruct(p.shape, p.dtype) for p in parts],
        scratch_shapes=[pltpu.SemaphoreType.DMA((k,)), pltpu.SemaphoreType.DMA((k,))], name=name)(*parts)


N_PEERS = 7


def _grad_exchange_body():
    def body(pin_ref, pout_ref, lin_ref, lout_ref, send_sems, recv_sems):
        x, y, c = _place()
        barrier = pltpu.get_barrier_semaphore()
        for k in range(1, N_PEERS + 1):
            peer = (x ^ ((k >> 2) & 1), y ^ ((k >> 1) & 1), c ^ (k & 1))
            pl.semaphore_signal(barrier, inc=1, device_id=peer, device_id_type=MESH)
        pl.semaphore_wait(barrier, N_PEERS)
        me = 2 * x + y
        pairs = ((pin_ref, lin_ref), (pout_ref, lout_ref))
        cps = []
        for k, (px, py) in enumerate(_other_chips(x, y)):
            for r in range(2):
                for j, (src, dst) in enumerate(pairs):
                    cps.append(pltpu.make_async_remote_copy(
                        src_ref=src.at[2 * px + py, r], dst_ref=dst.at[2 * k + c], send_sem=send_sems.at[2 * (2 * k + r) + j],
                        recv_sem=recv_sems.at[2 * (2 * k + c) + j], device_id=(px, py, r), device_id_type=MESH))
        for j, (src, dst) in enumerate(pairs):
            cps.append(pltpu.make_async_remote_copy(
                src_ref=src.at[me, 1 - c], dst_ref=dst.at[N_PEERS - 1], send_sem=send_sems.at[2 * (N_PEERS - 1) + j],
                recv_sem=recv_sems.at[2 * (N_PEERS - 1) + j], device_id=(x, y, 1 - c), device_id_type=MESH))
        for cp in cps:
            cp.start()
        for s in range(N_PEERS):
            for j, (src, dst) in enumerate(pairs):
                pltpu.make_async_remote_copy(
                    src_ref=src.at[0, 0], dst_ref=dst.at[s], send_sem=send_sems.at[2 * s + j], recv_sem=recv_sems.at[2 * s + j],
                    device_id=(x, y, 1 - c), device_id_type=MESH).wait_recv()
        for cp in cps:
            cp.wait_send()

    return body


_EXCHANGE_SEMS = [pltpu.SemaphoreType.DMA((2 * N_PEERS,))] * 2


def _landing(p):
    return jax.ShapeDtypeStruct((N_PEERS,) + p.shape[2:], p.dtype)


def _grad_exchange_beside(pin, pout, name, collective_id):
    hbm = pltpu.MemorySpace.HBM
    pin_ref, pout_ref = jax.new_ref(pin, memory_space=hbm), jax.new_ref(pout, memory_space=hbm)
    lin_ref, lout_ref = jax.empty_ref(_landing(pin), memory_space=hbm), jax.empty_ref(_landing(pout), memory_space=hbm)
    body = _grad_exchange_body()

    @pl.kernel(mesh=plsc.ScalarSubcoreMesh(axis_name="sequencer", num_cores=1), name=name,
               scratch_types=_EXCHANGE_SEMS, compiler_params=pltpu.CompilerParams(collective_id=collective_id))
    def launch(send_sems, recv_sems):
        body(pin_ref, pout_ref, lin_ref, lout_ref, send_sems, recv_sems)

    launch()
    return lin_ref[...], lout_ref[...]


def _add_n(parts, name, with_bf16=False):
    r, c = parts[0].shape
    tr = 256 if r % 256 == 0 else r
    n = len(parts)

    def body(*refs):
        acc = refs[0][...].astype(F32)
        for ref in refs[1:n]:
            acc = acc + ref[...].astype(F32)
        refs[n][...] = acc
        if with_bf16:
            refs[n + 1][...] = acc.astype(BF16)

    blk = pl.BlockSpec((tr, c), lambda i: (i, 0))
    outs = [jax.ShapeDtypeStruct((r, c), F32)] + ([jax.ShapeDtypeStruct((r, c), BF16)] if with_bf16 else [])
    res = pl.pallas_call(
        body, grid=(r // tr,), in_specs=[blk] * n, out_specs=[blk] * len(outs),
        out_shape=outs, compiler_params=_cparams(("parallel",)), name=name)(*parts)
    return res if with_bf16 else res[0]


def _all_reduce_small(packet):
    r, w = packet.shape

    def body(p_ref, o_ref, buf, send_sems, recv_sems):
        x, y, c = _place()
        me = 4 * x + 2 * y + c
        buf[me] = p_ref[...]
        peers = []
        for k in range(1, 8):
            fx, fy, fc = (k >> 2) & 1, (k >> 1) & 1, k & 1
            peers.append((x ^ fx, y ^ fy, c ^ fc))
        cps = [pltpu.make_async_remote_copy(src_ref=p_ref, dst_ref=buf.at[me], send_sem=send_sems.at[k], recv_sem=recv_sems.at[k],
                                            device_id=peer, device_id_type=MESH) for k, peer in enumerate(peers)]
        for cp in cps:
            cp.start()
        for k, (px, py, pc) in enumerate(peers):
            pltpu.make_async_remote_copy(src_ref=p_ref, dst_ref=buf.at[4 * px + 2 * py + pc], send_sem=send_sems.at[k],
                                         recv_sem=recv_sems.at[k], device_id=(px, py, pc), device_id_type=MESH).wait_recv()
        for cp in cps:
            cp.wait_send()
        acc = buf[0]
        for k in range(1, 8):
            acc = acc + buf[k]
        o_ref[...] = acc

    vm = pl.BlockSpec(memory_space=pltpu.VMEM)
    return pl.pallas_call(
        body, in_specs=[vm], out_specs=vm, out_shape=jax.ShapeDtypeStruct((r, w), F32),
        scratch_shapes=[pltpu.VMEM((8, r, w), F32), pltpu.SemaphoreType.DMA((7,)), pltpu.SemaphoreType.DMA((7,))],
        name="all_reduce_small")(packet)


def _adamw_math(w, g, m, v):
    m = ADAM_B1 * m + (1.0 - ADAM_B1) * g
    v = ADAM_B2 * v + (1.0 - ADAM_B2) * (g * g)
    m_hat = m / (1.0 - ADAM_B1 ** ADAM_STEP)
    v_hat = v / (1.0 - ADAM_B2 ** ADAM_STEP)
    return -ADAM_LR * (m_hat / (jnp.sqrt(v_hat) + ADAM_EPS) + ADAM_WD * w), m, v


def _adamw(w, g_lower, g_upper, m, v, name):
    nl, r, c = w.shape
    tr = 128
    per_half = r // (2 * tr)

    def body(w_ref, lo_ref, up_ref, m_ref, v_ref, g_ref, d_ref, mo_ref, vo_ref):
        g = jnp.where(pl.program_id(1) == 0, lo_ref[...], up_ref[...])
        g_ref[...] = g
        d_ref[...], mo_ref[...], vo_ref[...] = _adamw_math(w_ref[...], g, m_ref[...], v_ref[...])

    blk = pl.BlockSpec((None, tr, c), lambda l, h, i: (l, h * per_half + i, 0))
    half = pl.BlockSpec((None, tr, c), lambda l, h, i: (l, i, 0))
    out = jax.ShapeDtypeStruct(w.shape, F32)
    return pl.pallas_call(
        body, grid=(nl, 2, per_half), in_specs=[blk, half, half, blk, blk], out_specs=[blk] * 4, out_shape=[out] * 4,
        compiler_params=_cparams(("parallel", "parallel", "parallel")), name=name)(w, g_lower, g_upper, m, v)


def _small_update(gsum, lower_bounds, wpack, mpack, vpack):
    r, w = gsum.shape
    lb_rows = DEPTH * HGRN_W // 128

    def body(g_ref, a_ref, w_ref, m_ref, v_ref, go_ref, d_ref, mo_ref, vo_ref):
        a = a_ref[...]
        a0, a1 = a[0:1], a[1:2]
        mx = jnp.maximum(a0, a1)
        e0, e1 = jnp.exp(a0 - mx), jnp.exp(a1 - mx)
        p0, p1 = e0 / (e0 + e1), e1 / (e0 + e1)
        g = g_ref[...]
        half = lb_rows // 2
        dl0 = jnp.concatenate([g[k:k + 1] for k in range(half)], axis=1)
        dl1 = jnp.concatenate([g[half + k:half + k + 1] for k in range(half)], axis=1)
        dp0 = (dl0 + dl1) - (dl0 + dl1)
        dp1 = dl1
        inner = p0 * dp0 + p1 * dp1
        da0, da1 = p0 * (dp0 - inner), p1 * (dp1 - inner)
        rows = [da0[:, 128 * k:128 * (k + 1)] for k in range(half)] + [da1[:, 128 * k:128 * (k + 1)] for k in range(half)]
        gfull = jnp.concatenate(rows + [g[lb_rows:]], axis=0)
        go_ref[...] = gfull
        d_ref[...], mo_ref[...], vo_ref[...] = _adamw_math(w_ref[...], gfull, m_ref[...], v_ref[...])

    vm = pl.BlockSpec(memory_space=pltpu.VMEM)
    out = jax.ShapeDtypeStruct((r, w), F32)
    return pl.pallas_call(body, in_specs=[vm] * 5, out_specs=[vm] * 4, out_shape=[out] * 4, name="small_update")(
        gsum, lower_bounds, wpack, mpack, vpack)


_SMALL = ("lower_bounds", "pre_norm_g", "hgrn_norm_g", "fox_f_bias", "pool_w", "pool_scale", "post_norm_g")


def _pack(parts):
    rows = []
    for k in _SMALL:
        f = parts[k].reshape(-1)
        pad = (-f.shape[0]) % (8 * 128)
        rows.append(jnp.pad(f, (0, pad)).reshape(-1, 128))
    rows.append(jnp.zeros((8, 128), F32))
    return jnp.concatenate(rows, axis=0)


def _unpack(pack, like):
    out, r = {}, 0
    for k in _SMALL:
        size = int(np.prod(like[k].shape))
        nr = -(-size // (8 * 128)) * 8
        out[k] = pack[r:r + nr].reshape(-1)[:size].reshape(like[k].shape)
        r += nr
    return out, r


def kernel(x, lower_bounds, pre_norm_g, w_in, hgrn_norm_g, fox_f_bias, pool_w, pool_scale, w_out, post_norm_g, loss_target, m_lower_bounds, m_pre_norm_g, m_w_in, m_hgrn_norm_g, m_fox_f_bias, m_pool_w, m_pool_scale, m_w_out, m_post_norm_g, v_lower_bounds, v_pre_norm_g, v_w_in, v_hgrn_norm_g, v_fox_f_bias, v_pool_w, v_pool_scale, v_w_out, v_post_norm_g):
    cx, cy, cc = _place()
    chip = 2 * cx + cy

    halves = lambda w, l: w[l].reshape(2, w.shape[1] // 2, w.shape[2]).astype(BF16)
    needed_first = _gather_weights([halves(w_in, 0)])
    needed_first, later = lax.optimization_barrier((needed_first, [halves(w_out, 0), halves(w_in, 1), halves(w_out, 1)]))
    later = _gather_weights_beside(later)
    w_in_int = [_internal_from_shards([a[q].reshape(D_MODEL, SHARD_W) for q in range(N_CHIPS)]) for a in (needed_first[0], later[1])]
    w_out_full = [a.reshape(D_MODEL, D_MODEL) for a in (later[0], later[2])]

    def on_weight_grads(l, d_w_in, d_w_out):
        pin = _shards_from_internal(d_w_in).reshape(N_CHIPS, 2, D_MODEL // 2, SHARD_W)
        pout = d_w_out.reshape(N_CHIPS, 2, D_MODEL // (2 * N_CHIPS), D_MODEL)
        own = [lax.dynamic_index_in_dim(lax.dynamic_index_in_dim(p, chip, 0, False), cc, 0, False) for p in (pin, pout)]
        return own, _grad_exchange_beside(pin.astype(BF16), pout.astype(BF16), f"grad_exchange{l}", 2 + l)

    sq, grad_x, g, handed = _local_step(x, loss_target, lower_bounds, pre_norm_g, w_in_int, hgrn_norm_g, fox_f_bias,
                                        pool_w, pool_scale, w_out_full, post_norm_g, on_weight_grads)
    first = cc == 0

    def finish(l, own, landed):
        mine = [_add_n([o] + [t[s] for s in range(N_PEERS)], f"grad_sum{l}_{j}") for j, (o, t) in enumerate(zip(own, landed))]
        theirs = _swap_with_sibling(mine, f"grad_swap{l}")
        return [(jnp.where(first, h, o), jnp.where(first, o, h)) for h, o in zip(mine, theirs)]

    grad_x, last = lax.optimization_barrier((grad_x, handed[1]))
    done = [None, finish(1, *last)]

    small = {"lower_bounds": g["lbs"], "pre_norm_g": g["pre"], "hgrn_norm_g": g["hgn"], "fox_f_bias": g["bias"],
             "pool_w": g["pool_w"], "pool_scale": g["pool_scale"], "post_norm_g": g["post"]}
    packet = _pack(small)
    nrows = packet.shape[0]
    packet = packet.at[nrows - 1].set(sq[0])
    gsum = _all_reduce_small(packet)
    loss = gsum[nrows - 1, 0] * (0.5 / D_MODEL)

    weights = {"lower_bounds": lower_bounds, "pre_norm_g": pre_norm_g, "hgrn_norm_g": hgrn_norm_g,
               "fox_f_bias": fox_f_bias, "pool_w": pool_w, "pool_scale": pool_scale, "post_norm_g": post_norm_g}
    moments_m = {"lower_bounds": m_lower_bounds, "pre_norm_g": m_pre_norm_g, "hgrn_norm_g": m_hgrn_norm_g,
                 "fox_f_bias": m_fox_f_bias, "pool_w": m_pool_w, "pool_scale": m_pool_scale, "post_norm_g": m_post_norm_g}
    moments_v = {"lower_bounds": v_lower_bounds, "pre_norm_g": v_pre_norm_g, "hgrn_norm_g": v_hgrn_norm_g,
                 "fox_f_bias": v_fox_f_bias, "pool_w": v_pool_w, "pool_scale": v_pool_scale, "post_norm_g": v_post_norm_g}
    gp, dp, mp, vp = _small_update(gsum, lower_bounds, _pack(weights), _pack(moments_m), _pack(moments_v))
    gs, _ = _unpack(gp, weights)
    ds, _ = _unpack(dp, weights)
    ms, _ = _unpack(mp, weights)
    vs, _ = _unpack(vp, weights)

    first_layer, _ = lax.optimization_barrier((handed[0], (done[1], gp, dp, mp, vp)))
    done[0] = finish(0, *first_layer)
    halves_of = lambda j, side: jnp.stack([done[l][j][side] for l in range(DEPTH)])
    grad_w_in, d_in, m_in, v_in = _adamw(w_in, halves_of(0, 0), halves_of(0, 1), m_w_in, v_w_in, "adamw_w_in")
    grad_w_out, d_out, m_out, v_out = _adamw(w_out, halves_of(1, 0), halves_of(1, 1), m_w_out, v_w_out, "adamw_w_out")

    def ordered(s, big_in, big_out):
        return (s["lower_bounds"], s["pre_norm_g"], big_in, s["hgrn_norm_g"], s["fox_f_bias"], s["pool_w"],
                s["pool_scale"], big_out, s["post_norm_g"])

    return (loss, grad_x, *ordered(gs, grad_w_in, grad_w_out), *ordered(ds, d_in, d_out),
            *ordered(ms, m_in, m_out), *ordered(vs, v_in, v_out))
```

```python
import functools

import numpy as np
import jax
import jax.numpy as jnp
from jax import lax
from jax.experimental import pallas as pl
from jax.experimental.pallas import tpu as pltpu
from jax.experimental.pallas import tpu_sc as plsc

F32 = jnp.float32
BF16 = jnp.bfloat16
HI = lax.Precision.HIGHEST
MESH = pl.DeviceIdType.MESH

NORM_EPS = 1e-6
MASK_VALUE = -1e30
TINY = 1e-30
ADAM_LR, ADAM_B1, ADAM_B2, ADAM_EPS, ADAM_WD, ADAM_STEP = 0.001, 0.9, 0.999, 1e-08, 0.01, 10

D_MODEL = 1024
DEPTH = 2
N_CHIPS = 4
CHUNK = 64
LANES = 128
HGRN_W, POOL_W, FOX_W, FOX_HEADS = 256, 256, 512, 8
POOL_WINDOWS = (2, 4, 8, 16)
POOL_HALO = 16
IN_WIDTH = 3592
SHARD_W = IN_WIDTH // N_CHIPS
A_W, B_W, C_W, F_W = 1024, 512, 2048, 128
E_INT = A_W + B_W + C_W + F_W
B_BLK = A_W // 512
C_BLK0 = (A_W + B_W) // 512
F_BLK = (A_W + B_W + C_W) // 128


def _segments():
    segs = []
    for hp in range(2):
        for part in range(4):
            segs.append((part * 256 + hp * 128, 128))
    segs.append((1024, 256))
    segs.append((1280, 256))
    for hp in range(4):
        for part in range(4):
            segs.append((1536 + part * 512 + hp * 128, 128))
    segs.append((3584, 8))
    return segs


_SEGS = _segments()


def _to_internal(w):
    parts = [w[..., s:s + n] for s, n in _SEGS]
    parts.append(jnp.zeros(w.shape[:-1] + (E_INT - IN_WIDTH,), w.dtype))
    return jnp.concatenate(parts, axis=-1)


def _to_original(w):
    offs, o = [], 0
    for s, n in _SEGS:
        offs.append((s, o, n))
        o += n
    parts = [w[..., o:o + n] for s, o, n in sorted(offs)]
    return jnp.concatenate(parts, axis=-1)


def _internal_from_shards(shards):
    parts = []
    for s, n in _SEGS:
        while n > 0:
            q, r = divmod(s, SHARD_W)
            take = min(n, SHARD_W - r)
            parts.append(shards[q][..., r:r + take])
            s, n = s + take, n - take
    parts.append(jnp.zeros(shards[0].shape[:-1] + (E_INT - IN_WIDTH,), shards[0].dtype))
    return jnp.concatenate(parts, axis=-1)


def _shards_from_internal(w):
    offs, o = [], 0
    for s, n in _SEGS:
        offs.append((s, o, n))
        o += n
    blocks = []
    for q in range(N_CHIPS):
        lo, hi = SHARD_W * q, SHARD_W * (q + 1)
        parts = [w[..., o + max(lo, s) - s:o + min(hi, s + n) - s] for s, o, n in sorted(offs) if s < hi and s + n > lo]
        blocks.append(jnp.concatenate(parts, axis=-1))
    return jnp.stack(blocks)


def _cparams(sem=None, vmem_mb=48):
    kw = dict(vmem_limit_bytes=vmem_mb * 1024 * 1024)
    if sem is not None:
        kw["dimension_semantics"] = sem
    return pltpu.CompilerParams(**kw)


def _sig(x):
    return 1.0 / (1.0 + jnp.exp(-x))


def _silu(x):
    return x * _sig(x)


def _dsilu(x):
    s = _sig(x)
    return s * (1.0 + x * (1.0 - s))


def _rstd(x):
    return lax.rsqrt(jnp.mean(x * x, axis=-1, keepdims=True) + NORM_EPS)


def _dot(a, b, dims, **kw):
    return lax.dot_general(a, b, (dims, ((), ())), preferred_element_type=F32, **kw)


NN = ((1,), (0,))
NT = ((1,), (1,))
TN = ((0,), (0,))


def _iota(shape, dim):
    return lax.broadcasted_iota(jnp.int32, shape, dim)


def _lbs_fwd(lower_bounds):
    def body(a_ref, o_ref):
        a = a_ref[...]
        a0, a1 = a[0:1], a[1:2]
        m = jnp.maximum(a0, a1)
        e0, e1 = jnp.exp(a0 - m), jnp.exp(a1 - m)
        p0, p1 = e0 / (e0 + e1), e1 / (e0 + e1)
        o_ref[...] = jnp.concatenate([p0 - p0, (p0 + p1) - p0], axis=0)

    return pl.pallas_call(body, out_shape=jax.ShapeDtypeStruct(lower_bounds.shape, F32), name="lbs_fwd")(lower_bounds)


def _inproj_fwd(x2, g_row, w_int, name):
    n, d = x2.shape
    e = w_int.shape[1]
    tm = min(512, n)

    def body(x_ref, g_ref, w_ref, o_ref):
        x = x_ref[...]
        h = (x * _rstd(x) * g_ref[...]).astype(BF16)
        o_ref[...] = jnp.dot(h, w_ref[...], preferred_element_type=F32)

    return pl.pallas_call(
        body, grid=(n // tm,),
        in_specs=[pl.BlockSpec((tm, d), lambda i: (i, 0)), pl.BlockSpec((1, d), lambda i: (0, 0)),
                  pl.BlockSpec((d, e), lambda i: (0, 0))],
        out_specs=pl.BlockSpec((tm, e), lambda i: (i, 0)),
        out_shape=jax.ShapeDtypeStruct((n, e), F32),
        compiler_params=_cparams(("parallel",)), name=name)(x2, g_row, w_int)


def _chunk_cumsum_matrix():
    i, j = _iota((LANES, LANES), 0), _iota((LANES, LANES), 1)
    return ((i <= j) & ((i // CHUNK) == (j // CHUNK))).astype(F32)


def _hgrn_gates(a, lb):
    qa, z = a[:, 0:128], a[:, 128:256]
    sg, sgn = _sig(z), _sig(-z)
    fg = lb + (1.0 - lb) * sg
    lf = jnp.log(jnp.maximum(fg, TINY))
    kk = (1.0 - lb) * sgn
    return qa * _sig(qa), kk, lf, sg, sgn, fg


def _hgrn_fwd(proj3, lbs_row, gn_col, name):
    bsz, t, _ = proj3.shape
    nt = t // LANES

    def body(a_ref, lb_ref, gn_ref, og_ref, or_ref):
        lb = lb_ref[...]
        gn = gn_ref[...]
        umat = _chunk_cumsum_matrix()
        lane64 = _iota((1, LANES), 1) % CHUNK

        def tile(i, carry):
            r0 = pl.multiple_of(i * LANES, LANES)
            a = a_ref[pl.ds(r0, LANES), :]
            qq, kk, lf, _, _, _ = _hgrn_gates(a, lb)
            va, ga = a[:, 256:384], a[:, 384:512]
            q_t, k_t, v_t = qq.T, kk.T, va.T
            b_t = jnp.dot(lf.T, umat, precision=HI, preferred_element_type=F32)
            new_s, o_heads = [], []
            for h in range(2):
                s_h = carry[h]
                rs = slice(CHUNK * h, CHUNK * (h + 1))
                qh, kh, vh, bh = q_t[rs], k_t[rs], v_t[rs], b_t[rs]
                inter = []
                for c in range(2):
                    cs = slice(CHUNK * c, CHUNK * (c + 1))
                    b_ = bh[:, cs]
                    qt = (qh[:, cs] * jnp.exp(b_)).astype(BF16)
                    inter.append(_dot(s_h.astype(BF16), qt, TN))
                    bl = b_[:, CHUNK - 1:CHUNK]
                    kt = (kh[:, cs] * jnp.exp(bl - b_)).astype(BF16)
                    s_h = jnp.exp(bl) * s_h + _dot(kt, vh[:, cs].astype(BF16), NT)
                new_s.append(s_h)

                acc = jnp.concatenate(inter, axis=1) + jnp.sum(qh * kh, axis=0, keepdims=True) * vh
                for dlt in range(1, CHUNK):
                    kr, br, vr = pltpu.roll(kh, dlt, 1), pltpu.roll(bh, dlt, 1), pltpu.roll(vh, dlt, 1)
                    e = jnp.exp(jnp.minimum(bh - br, 0.0))
                    att = jnp.sum(qh * kr * e, axis=0, keepdims=True)
                    acc = acc + jnp.where(lane64 >= dlt, att, 0.0) * vr
                o_heads.append(acc)
            normed = []
            for h in range(2):
                o_h = o_heads[h]
                ms = jnp.mean(o_h * o_h, axis=0, keepdims=True)
                normed.append(o_h * lax.rsqrt(ms + NORM_EPS) * gn[CHUNK * h:CHUNK * (h + 1)])
            or_ref[pl.ds(r0, LANES), :] = jnp.concatenate(o_heads, axis=0).T
            og_ref[pl.ds(r0, LANES), :] = jnp.concatenate(normed, axis=0).T * _silu(ga)
            return tuple(new_s)

        zero = jnp.zeros((CHUNK, CHUNK), F32)
        lax.fori_loop(0, nt, tile, (zero, zero))

    out = jax.ShapeDtypeStruct((bsz, t, HGRN_W), F32)
    return pl.pallas_call(
        body, grid=(bsz, 2),
        in_specs=[pl.BlockSpec((None, t, 512), lambda b, p: (b, 0, p)),
                  pl.BlockSpec((1, 128), lambda b, p: (0, p)),
                  pl.BlockSpec((128, 1), lambda b, p: (p, 0))],
        out_specs=[pl.BlockSpec((None, t, 128), lambda b, p: (b, 0, p)),
                   pl.BlockSpec((None, t, 128), lambda b, p: (b, 0, p))],
        out_shape=[out, out],
        compiler_params=_cparams(("parallel", "parallel")), name=name)(proj3, lbs_row, gn_col)


def _hgrn_bwd(proj3, o_raw, dmixed, lbs_row, gn_row, name):
    bsz, t, _ = proj3.shape
    nt = t // LANES
    nchunk = t // CHUNK

    def body(a_ref, or_ref, do_ref, lb_ref, gn_ref, da_ref, dgn_ref, dlb_ref, s_sc):
        lb = lb_ref[...]
        gn = gn_ref[...]
        umat = _chunk_cumsum_matrix()
        lane = _iota((1, LANES), 1)
        lane64 = lane % CHUNK
        half = lane < CHUNK

        def t_layout(a):
            qq, kk, lf, sg, sgn, fg = _hgrn_gates(a, lb)
            b_t = jnp.dot(lf.T, umat, precision=HI, preferred_element_type=F32)
            return qq.T, kk.T, a[:, 256:384].T, b_t, (sg, sgn, fg)

        def fwd_tile(i, carry):
            r0 = pl.multiple_of(i * LANES, LANES)
            q_t, k_t, v_t, b_t, _ = t_layout(a_ref[pl.ds(r0, LANES), :])
            new_s = []
            for h in range(2):
                s_h = carry[h]
                rs = slice(CHUNK * h, CHUNK * (h + 1))
                for c in range(2):
                    cs = slice(CHUNK * c, CHUNK * (c + 1))
                    s_sc[h, 2 * i + c] = s_h
                    b_ = b_t[rs, cs]
                    bl = b_[:, CHUNK - 1:CHUNK]
                    kt = (k_t[rs, cs] * jnp.exp(bl - b_)).astype(BF16)
                    s_h = jnp.exp(bl) * s_h + _dot(kt, v_t[rs, cs].astype(BF16), NT)
                new_s.append(s_h)
            return tuple(new_s)

        zero = jnp.zeros((CHUNK, CHUNK), F32)
        lax.fori_loop(0, nt, fwd_tile, (zero, zero))

        def half_mean(v):
            m0 = jnp.sum(jnp.where(half, v, 0.0), axis=1, keepdims=True) * (1.0 / CHUNK)
            m1 = jnp.sum(jnp.where(half, 0.0, v), axis=1, keepdims=True) * (1.0 / CHUNK)
            return jnp.where(half, m0, m1)

        def bwd_tile(k, carry):
            ds0, ds1, dgn_acc, dlb_acc = carry
            i = nt - 1 - k
            r0 = pl.multiple_of(i * LANES, LANES)
            a = a_ref[pl.ds(r0, LANES), :]
            qa, z, ga = a[:, 0:128], a[:, 128:256], a[:, 384:512]
            q_t, k_t, v_t, b_t, (sg, sgn, fg) = t_layout(a)
            oraw = or_ref[pl.ds(r0, LANES), :]
            dout = do_ref[pl.ds(r0, LANES), :]
            r = lax.rsqrt(half_mean(oraw * oraw) + NORM_EPS)
            xn = oraw * r
            dga = dout * (xn * gn) * _dsilu(ga)
            don = dout * _silu(ga)
            dgn_acc = dgn_acc + jnp.sum(don * xn, axis=0, keepdims=True)
            dxn = don * gn
            do_t = (r * (dxn - xn * half_mean(dxn * xn))).T
            new_ds, dq_h, dk_h, dv_h, db_h = [], [], [], [], []
            for h in range(2):
                ds_h = (ds0, ds1)[h]
                rs = slice(CHUNK * h, CHUNK * (h + 1))
                qh, kh, vh, bh, doh = q_t[rs], k_t[rs], v_t[rs], b_t[rs], do_t[rs]
                dq_c, dk_c, dv_c, dbl_c = [None, None], [None, None], [None, None], [None, None]
                for c in (1, 0):
                    cs = slice(CHUNK * c, CHUNK * (c + 1))
                    s_n = s_sc[h, 2 * i + c]
                    b_ = bh[:, cs]
                    eb = jnp.exp(b_)
                    bl = b_[:, CHUNK - 1:CHUNK]
                    ek = jnp.exp(bl - b_)
                    ebl = jnp.exp(bl)
                    qt, kt = qh[:, cs] * eb, kh[:, cs] * ek
                    do_c = doh[:, cs].astype(BF16)
                    dsb = ds_h.astype(BF16)
                    dv_c[c] = _dot(dsb, kt.astype(BF16), TN)
                    dkt = _dot(dsb, vh[:, cs].astype(BF16), NN)
                    dqt = _dot(s_n.astype(BF16), do_c, NN)
                    dbl_c[c] = jnp.sum(ds_h * s_n, axis=1, keepdims=True) * ebl + jnp.sum(dkt * kt, axis=1, keepdims=True)
                    dq_c[c], dk_c[c] = dqt * eb, dkt * ek
                    ds_h = ebl * ds_h + _dot(qt.astype(BF16), do_c, NT)
                new_ds.append(ds_h)

                att0 = jnp.sum(qh * kh, axis=0, keepdims=True)
                datt0 = jnp.sum(doh * vh, axis=0, keepdims=True)
                dqh = jnp.concatenate(dq_c, axis=1) + datt0 * kh
                dkh = jnp.concatenate(dk_c, axis=1) + datt0 * qh
                dvh = jnp.concatenate(dv_c, axis=1) + att0 * doh
                for dlt in range(1, CHUNK):
                    kr, br, vr = pltpu.roll(kh, dlt, 1), pltpu.roll(bh, dlt, 1), pltpu.roll(vh, dlt, 1)
                    e = jnp.where(lane64 >= dlt, jnp.exp(jnp.minimum(bh - br, 0.0)), 0.0)
                    qe = qh * e
                    att = jnp.sum(qe * kr, axis=0, keepdims=True)
                    datt = jnp.sum(doh * vr, axis=0, keepdims=True)
                    dqh = dqh + datt * (kr * e)
                    dkh = dkh + pltpu.roll(datt * qe, LANES - dlt, 1)
                    dvh = dvh + pltpu.roll(att * doh, LANES - dlt, 1)
                dbl = jnp.where(half, dbl_c[0], dbl_c[1])
                db_h.append(qh * dqh - kh * dkh + jnp.where(lane64 == CHUNK - 1, dbl, 0.0))
                dq_h.append(dqh)
                dk_h.append(dkh)
                dv_h.append(dvh)
            dqq = jnp.concatenate(dq_h, axis=0).T
            dkk = jnp.concatenate(dk_h, axis=0).T
            dvv = jnp.concatenate(dv_h, axis=0).T
            dlf = _dot(jnp.concatenate(db_h, axis=0), umat, NT, precision=HI).T
            dqa = dqq * _dsilu(qa)
            dfg = jnp.where(fg > TINY, dlf / fg, 0.0)
            dz = (dfg - dkk) * (1.0 - lb) * sg * sgn
            dlb_acc = dlb_acc + jnp.sum(dfg * (1.0 - sg) - dkk * sgn, axis=0, keepdims=True)
            da_ref[pl.ds(r0, LANES), :] = jnp.concatenate([dqa, dz, dvv, dga], axis=1)
            return new_ds[0], new_ds[1], dgn_acc, dlb_acc

        zrow = jnp.zeros((1, LANES), F32)
        _, _, dgn_acc, dlb_acc = lax.fori_loop(0, nt, bwd_tile, (zero, zero, zrow, zrow))
        dgn_ref[...] = jnp.broadcast_to(dgn_acc, (8, LANES))
        dlb_ref[...] = jnp.broadcast_to(dlb_acc, (8, LANES))

    rows = jax.ShapeDtypeStruct((bsz, 8, HGRN_W), F32)
    return pl.pallas_call(
        body, grid=(bsz, 2),
        in_specs=[pl.BlockSpec((None, t, 512), lambda b, p: (b, 0, p)),
                  pl.BlockSpec((None, t, 128), lambda b, p: (b, 0, p)),
                  pl.BlockSpec((None, t, 128), lambda b, p: (b, 0, p)),
                  pl.BlockSpec((1, 128), lambda b, p: (0, p)),
                  pl.BlockSpec((1, 128), lambda b, p: (0, p))],
        out_specs=[pl.BlockSpec((None, t, 512), lambda b, p: (b, 0, p)),
                   pl.BlockSpec((None, 8, 128), lambda b, p: (b, 0, p)),
                   pl.BlockSpec((None, 8, 128), lambda b, p: (b, 0, p))],
        out_shape=[jax.ShapeDtypeStruct((bsz, t, A_W), F32), rows, rows],
        scratch_shapes=[pltpu.VMEM((2, nchunk, CHUNK, CHUNK), F32)],
        compiler_params=_cparams(("parallel", "parallel")), name=name)(proj3, o_raw, dmixed, lbs_row, gn_row)


N_LEVELS = 6


def _hgrn_tables():
    t = np.arange(LANES)
    j = np.arange(LANES)[None, :]
    same_chunk = (t[:, None] // CHUNK) == (j // CHUNK)
    w = np.zeros((2 + N_LEVELS, LANES, LANES), np.float32)
    w[0] = same_chunk & (j <= t[:, None])
    w[1] = same_chunk & (j > t[:, None])
    maskf = np.zeros((N_LEVELS, LANES, LANES), np.float32)
    rightf = np.zeros((N_LEVELS, LANES, LANES), np.float32)
    for li in range(N_LEVELS):
        m = (CHUNK // 2) >> li
        start = t - (t % (2 * m))
        right = (t % (2 * m)) >= m
        first = np.where(right, start + m, t + 1)
        last = np.where(right, t, start + m - 1)
        w[2 + li] = (j >= first[:, None]) & (j <= last[:, None])
        maskf[li] = (t[:, None] // (2 * m)) == (j // (2 * m))
        rightf[li] = right[:, None]
    w = w[:-1]
    return jnp.asarray(w.reshape(-1, LANES), BF16), jnp.asarray(np.tile(maskf, (1, 2, 1))), jnp.asarray(rightf)


def _split(x, n):
    parts = []
    for _ in range(n - 1):
        p = x.astype(BF16)
        parts.append(p)
        x = x - p.astype(F32)
    parts.append(x.astype(BF16))
    return parts


def _exact_dot(w, parts):
    acc = jnp.dot(w, parts[0], preferred_element_type=F32)
    for p in parts[1:]:
        acc = acc + jnp.dot(w, p, preferred_element_type=F32)
    return acc


def _head_sums(v, ones_blk, n=2):
    parts = _split(v, n)
    acc = jnp.dot(parts[0], ones_blk, preferred_element_type=F32)
    for p in parts[1:]:
        acc = acc + jnp.dot(p, ones_blk, preferred_element_type=F32)
    return acc


def _hgrn_consts():
    r, c = _iota((LANES, LANES), 0), _iota((LANES, LANES), 1)
    ones_blk = ((r // CHUNK) == (c // CHUNK)).astype(BF16)
    eye2 = (_iota((2 * LANES, LANES), 0) % LANES) == _iota((2 * LANES, LANES), 1)
    first = _iota((1, LANES), 1) < CHUNK
    return eye2, ones_blk, jnp.ones((LANES, LANES), BF16), first


def _stack_heads(v, first):
    return jnp.concatenate([jnp.where(first, v, 0.0), jnp.where(first, 0.0, v)], axis=0)


def _pick_heads(v2, first):
    return jnp.where(first, v2[:LANES], v2[LANES:])


def _hgrn_levels(qq, kk, lf, zall, mk_ref, rt_ref, first, d_att=None):
    att = jnp.zeros((2 * LANES, LANES), F32)
    dq = dk = db = jnp.zeros((LANES, LANES), F32)
    for li in range(N_LEVELS):
        rt = rt_ref[li]
        e = jnp.exp(zall[(2 + li) * LANES:(3 + li) * LANES] if li < N_LEVELS - 1 else lf * rt)
        mk = mk_ref[li]
        qef, kef = e * rt, e * (1.0 - rt)
        qe, ke = (qq * qef).astype(BF16), (kk * kef).astype(BF16)
        qe2 = _stack_heads(qe, first)
        att = att + _dot(qe2, ke, NT) * mk
        if d_att is not None:
            dam = (d_att * mk).astype(BF16)
            dqe = _pick_heads(jnp.dot(dam, ke, preferred_element_type=F32), first)
            dke = _dot(dam, qe2, TN)
            dq = dq + dqe * qef
            dk = dk + dke * kef
            db = db + (dqe * qe.astype(F32) - dke * ke.astype(F32))
    return att, dq, dk, db


def _hgrn_fwd(proj3, lbs_row, gn_row, name):
    bsz, t, _ = proj3.shape
    nt = t // LANES
    w_all, maskf, rightf = _hgrn_tables()

    def body(a_ref, lb_ref, gn_ref, w_ref, mk_ref, rt_ref, og_ref, or_ref, st_ref):
        lb = lb_ref[...]
        gn = gn_ref[...]
        eye2, ones_blk, ones_all, first = _hgrn_consts()

        def tile(i, carry):
            r0 = pl.multiple_of(i * LANES, LANES)
            a = a_ref[pl.ds(r0, LANES), :]
            qq, kk, lf, _, _, _ = _hgrn_gates(a, lb)
            va, ga = a[:, 256:384], a[:, 384:512]
            parts = _split(lf, 3)
            zall = _exact_dot(w_ref[...], parts)
            eb, ee = jnp.exp(zall[0:LANES]), jnp.exp(zall[LANES:2 * LANES])
            vb = va.astype(BF16)
            att, _, _, _ = _hgrn_levels(qq, kk, lf, zall, mk_ref, rt_ref, first)
            diag = _head_sums(_stack_heads(qq * kk, first), ones_all)
            a2 = (att + jnp.where(eye2, diag, 0.0)).astype(BF16)
            o_in = _pick_heads(jnp.dot(a2, vb, preferred_element_type=F32), first)
            qeb, keb = (qq * eb).astype(BF16), (kk * ee).astype(BF16)
            new_s, o_heads = [], []
            for h in range(2):
                hs = slice(CHUNK * h, CHUNK * (h + 1))
                o_h = o_in[:, hs]
                st = carry[h]
                chunks = []
                for c in range(2):
                    rc = slice(CHUNK * c, CHUNK * (c + 1))
                    st_ref[h, 2 * i + c] = st
                    chunks.append(o_h[rc] + _dot(qeb[rc, hs], st.astype(BF16), NT))
                    ebl = eb[CHUNK * (c + 1) - 1:CHUNK * (c + 1), hs]
                    st = st * ebl + _dot(vb[rc, hs], keb[rc, hs], TN)
                new_s.append(st)
                o_heads.append(jnp.concatenate(chunks, axis=0))
            o = jnp.concatenate(o_heads, axis=1)
            ms = _head_sums(o * o, ones_blk) * (1.0 / CHUNK)
            or_ref[pl.ds(r0, LANES), :] = o
            og_ref[pl.ds(r0, LANES), :] = o * lax.rsqrt(ms + NORM_EPS) * gn * _silu(ga)
            return tuple(new_s)

        zero = jnp.zeros((CHUNK, CHUNK), F32)
        per_step = 4 if nt % 4 == 0 else 2

        def step(i, carry):
            for k in range(per_step):
                carry = tile(per_step * i + k, carry)
            return carry

        lax.fori_loop(0, nt // per_step, step, (zero, zero))

    out = jax.ShapeDtypeStruct((bsz, t, HGRN_W), F32)
    row = pl.BlockSpec((1, 128), lambda b, p: (0, p))
    return pl.pallas_call(
        body, grid=(bsz, 2),
        in_specs=[pl.BlockSpec((None, t, 512), lambda b, p: (b, 0, p)), row, row,
                  pl.BlockSpec(w_all.shape, lambda b, p: (0, 0)),
                  pl.BlockSpec(maskf.shape, lambda b, p: (0, 0, 0)),
                  pl.BlockSpec(rightf.shape, lambda b, p: (0, 0, 0))],
        out_specs=[pl.BlockSpec((None, t, 128), lambda b, p: (b, 0, p)),
                   pl.BlockSpec((None, t, 128), lambda b, p: (b, 0, p)),
                   pl.BlockSpec((None, 2, t // CHUNK, CHUNK, CHUNK), lambda b, p: (b, p, 0, 0, 0))],
        out_shape=[out, out, jax.ShapeDtypeStruct((bsz, 4, t // CHUNK, CHUNK, CHUNK), F32)],
        compiler_params=_cparams(("parallel", "parallel")), name=name)(proj3, lbs_row, gn_row, w_all, maskf, rightf)


def _exact_dot_r(parts, hs, ones_h):
    acc = jnp.dot(parts[0][:, hs], ones_h, preferred_element_type=F32)
    for p in parts[1:]:
        acc = acc + jnp.dot(p[:, hs], ones_h, preferred_element_type=F32)
    return acc


def _hgrn_bwd(proj3, o_raw, dmixed, states, lbs_row, gn_row, name):
    bsz, t, _ = proj3.shape
    nt = t // LANES
    nchunk = t // CHUNK
    w_all, maskf, rightf = _hgrn_tables()

    def body(a_ref, or_ref, do_ref, s_sc, lb_ref, gn_ref, w_ref, mk_ref, rt_ref, da_ref, dgn_ref, dlb_ref):
        lb = lb_ref[...]
        gn = gn_ref[...]
        eye2, ones_blk, ones_all, first = _hgrn_consts()
        r_i, c_i = _iota((LANES, LANES), 0), _iota((LANES, LANES), 1)
        suffix = ((c_i >= r_i) & ((r_i // CHUNK) == (c_i // CHUNK))).astype(BF16)
        row64 = _iota((LANES, CHUNK), 0)
        zero = jnp.zeros((CHUNK, CHUNK), F32)

        def bwd_tile(k, carry):
            dst0, dst1, dgn_acc, dlb_acc = carry
            i = nt - 1 - k
            r0 = pl.multiple_of(i * LANES, LANES)
            a = a_ref[pl.ds(r0, LANES), :]
            qa, ga = a[:, 0:128], a[:, 384:512]
            qq, kk, lf, sg, sgn, fg = _hgrn_gates(a, lb)
            parts = _split(lf, 3)
            zall = _exact_dot(w_ref[...], parts)
            eb, ee = jnp.exp(zall[0:LANES]), jnp.exp(zall[LANES:2 * LANES])
            vb = a[:, 256:384].astype(BF16)
            oraw = or_ref[pl.ds(r0, LANES), :]
            dout = do_ref[pl.ds(r0, LANES), :]
            r = lax.rsqrt(_head_sums(oraw * oraw, ones_blk) * (1.0 / CHUNK) + NORM_EPS)
            xn = oraw * r
            dga = dout * (xn * gn) * _dsilu(ga)
            don = dout * _silu(ga)
            dgn_acc = dgn_acc + jnp.sum(don * xn, axis=0, keepdims=True)
            dxn = don * gn
            do = r * (dxn - xn * (_head_sums(dxn * xn, ones_blk) * (1.0 / CHUNK)))
            dob = do.astype(BF16)
            do2 = _stack_heads(dob, first)
            d_att = _dot(do2, vb, NT)
            att, dq, dk, db_lv = _hgrn_levels(qq, kk, lf, zall, mk_ref, rt_ref, first, d_att)
            a2 = att + jnp.where(eye2, _head_sums(_stack_heads(qq * kk, first), ones_all), 0.0)
            dv_in = _dot(a2.astype(BF16), do2, TN)
            ddiag = _pick_heads(_head_sums(jnp.where(eye2, d_att, 0.0), ones_all), first)
            dq_in, dk_in = dq + ddiag * kk, dk + ddiag * qq
            qe_f, ke_f = qq * eb, kk * ee
            qeb, keb = qe_f.astype(BF16), ke_f.astype(BF16)
            new_ds, dq_h, dk_h, dv_h, dbl_h = [], [], [], [], []
            for h in range(2):
                hs = slice(CHUNK * h, CHUNK * (h + 1))
                dv, dq_i, dk_i = dv_in[:, hs], dq_in[:, hs], dk_in[:, hs]
                dst = (dst0, dst1)[h]
                dq_c, dk_c, dv_c, dbl_c = [None, None], [None, None], [None, None], [None, None]
                for c in (1, 0):
                    rc = slice(CHUNK * c, CHUNK * (c + 1))
                    st_n = s_sc[h, 2 * i + c]
                    ebl = eb[CHUNK * (c + 1) - 1:CHUNK * (c + 1), hs]
                    dstb = dst.astype(BF16)
                    dv_c[c] = _dot(keb[rc, hs], dstb, NT)
                    dke = jnp.dot(vb[rc, hs], dstb, preferred_element_type=F32)
                    dqe = jnp.dot(dob[rc, hs], st_n.astype(BF16), preferred_element_type=F32)
                    dbl_c[c] = (jnp.sum(dst * st_n, axis=0, keepdims=True) * ebl
                                + jnp.sum(dke * ke_f[rc, hs], axis=0, keepdims=True))
                    dq_c[c], dk_c[c] = dqe * eb[rc, hs], dke * ee[rc, hs]
                    dst = dst * ebl + _dot(dob[rc, hs], qeb[rc, hs], TN)
                new_ds.append(dst)
                dq_x, dk_x = jnp.concatenate(dq_c, axis=0), jnp.concatenate(dk_c, axis=0)
                dq_h.append(dq_i + dq_x)
                dk_h.append(dk_i + dk_x)
                dv_h.append(dv + jnp.concatenate(dv_c, axis=0))
                dbl_h.append(qq[:, hs] * dq_x - kk[:, hs] * dk_x
                             + jnp.where(row64 == CHUNK - 1, dbl_c[0], 0.0) + jnp.where(row64 == LANES - 1, dbl_c[1], 0.0))
            dqq = jnp.concatenate(dq_h, axis=1)
            dkk = jnp.concatenate(dk_h, axis=1)
            dvv = jnp.concatenate(dv_h, axis=1)
            db = db_lv + jnp.concatenate(dbl_h, axis=1)
            dlf = _exact_dot(suffix, _split(db, 3))
            dqa = dqq * _dsilu(qa)
            dfg = jnp.where(fg > TINY, dlf / fg, 0.0)
            dz = (dfg - dkk) * (1.0 - lb) * sg * sgn
            dlb_acc = dlb_acc + jnp.sum(dfg * (1.0 - sg) - dkk * sgn, axis=0, keepdims=True)
            da_ref[pl.ds(r0, LANES), :] = jnp.concatenate([dqa, dz, dvv, dga], axis=1)
            return new_ds[0], new_ds[1], dgn_acc, dlb_acc

        zrow = jnp.zeros((1, LANES), F32)
        per_step = 4 if nt % 4 == 0 else 2

        def step(k, carry):
            for r in range(per_step):
                carry = bwd_tile(per_step * k + r, carry)
            return carry

        _, _, dgn_acc, dlb_acc = lax.fori_loop(0, nt // per_step, step, (zero, zero, zrow, zrow))
        dgn_ref[...] = jnp.broadcast_to(dgn_acc, (8, LANES))
        dlb_ref[...] = jnp.broadcast_to(dlb_acc, (8, LANES))

    rows = jax.ShapeDtypeStruct((bsz, 8, HGRN_W), F32)
    row = pl.BlockSpec((1, 128), lambda b, p: (0, p))
    blk = pl.BlockSpec((None, t, 128), lambda b, p: (b, 0, p))
    return pl.pallas_call(
        body, grid=(bsz, 2),
        in_specs=[pl.BlockSpec((None, t, 512), lambda b, p: (b, 0, p)), blk, blk,
                  pl.BlockSpec((None, 2, nchunk, CHUNK, CHUNK), lambda b, p: (b, p, 0, 0, 0)), row, row,
                  pl.BlockSpec(w_all.shape, lambda b, p: (0, 0)),
                  pl.BlockSpec(maskf.shape, lambda b, p: (0, 0, 0)),
                  pl.BlockSpec(rightf.shape, lambda b, p: (0, 0, 0))],
        out_specs=[pl.BlockSpec((None, t, 512), lambda b, p: (b, 0, p)),
                   pl.BlockSpec((None, 8, 128), lambda b, p: (b, 0, p)),
                   pl.BlockSpec((None, 8, 128), lambda b, p: (b, 0, p))],
        out_shape=[jax.ShapeDtypeStruct((bsz, t, A_W), F32), rows, rows],
        compiler_params=_cparams(("parallel", "parallel")), name=name)(
            proj3, o_raw, dmixed, states, lbs_row, gn_row, w_all, maskf, rightf)


def _pool_tt(t):
    return min(256, t)


def _window_select(s2, s4, s8, s16, lane):
    return jnp.where(lane < 64, s2, jnp.where(lane < 128, s4, jnp.where(lane < 192, s8, s16)))


def _pool_counts(t0, tt):
    lane = _iota((tt, POOL_W), 1)
    tpos = (_iota((tt, POOL_W), 0) + t0 + 1).astype(F32)
    win = jnp.where(lane < 64, 2.0, jnp.where(lane < 128, 4.0, jnp.where(lane < 192, 8.0, 16.0)))
    return 1.0 / jnp.minimum(tpos, win), lane


def _pooled_tile(upad_ref, i, tt):
    r0 = pl.multiple_of(i * tt, 8)
    cat = upad_ref[pl.ds(r0, tt + POOL_HALO), :]
    s2 = cat + pltpu.roll(cat, 1, 0)
    s4 = s2 + pltpu.roll(s2, 2, 0)
    s8 = s4 + pltpu.roll(s4, 4, 0)
    s16 = s8 + pltpu.roll(s8, 8, 0)
    inv, lane = _pool_counts(i * tt, tt)
    sel = _window_select(s2[POOL_HALO:], s4[POOL_HALO:], s8[POOL_HALO:], s16[POOL_HALO:], lane)
    return sel * inv - cat[POOL_HALO:], inv, lane


def _pool_fwd(proj3, wbd, scale_row, name):
    bsz, t, _ = proj3.shape
    tt = _pool_tt(t)

    def body(p_ref, w_ref, sc_ref, o_ref, upad):
        upad[0:POOL_HALO, :] = jnp.zeros((POOL_HALO, POOL_W), F32)
        upad[POOL_HALO:, :] = p_ref[:, 0:POOL_W]
        w = w_ref[...]
        sc = sc_ref[...]

        def tile(i, c):
            pooled, _, _ = _pooled_tile(upad, i, tt)
            r0 = pl.multiple_of(i * tt, 8)
            g = p_ref[pl.ds(r0, tt), POOL_W:2 * POOL_W]
            pre = jnp.dot(pooled.astype(BF16), w, preferred_element_type=F32)
            o_ref[pl.ds(r0, tt), :] = pre * sc * _silu(g)
            return c

        lax.fori_loop(0, t // tt, tile, 0)

    return pl.pallas_call(
        body, grid=(bsz,),
        in_specs=[pl.BlockSpec((None, t, 512), lambda b: (b, 0, B_BLK)),
                  pl.BlockSpec((POOL_W, POOL_W), lambda b: (0, 0)),
                  pl.BlockSpec((1, POOL_W), lambda b: (0, 0))],
        out_specs=pl.BlockSpec((None, t, POOL_W), lambda b: (b, 0, 0)),
        out_shape=jax.ShapeDtypeStruct((bsz, t, POOL_W), F32),
        scratch_shapes=[pltpu.VMEM((t + POOL_HALO, POOL_W), F32)],
        compiler_params=_cparams(("parallel",)), name=name)(proj3, wbd, scale_row)


def _pool_bwd(proj3, dmixed, wbd, scale_row, name):
    bsz, t, _ = proj3.shape
    tt = _pool_tt(t)

    def body(p_ref, do_ref, w_ref, sc_ref, db_ref, dsc_ref, dw_ref, upad, epad):
        upad[0:POOL_HALO, :] = jnp.zeros((POOL_HALO, POOL_W), F32)
        upad[POOL_HALO:, :] = p_ref[:, 0:POOL_W]
        epad[t:, :] = jnp.zeros((POOL_HALO, POOL_W), F32)
        w = w_ref[...]
        sc = sc_ref[...]

        def tile(i, carry):
            dsc_acc, dw_acc = carry
            pooled, inv, _ = _pooled_tile(upad, i, tt)
            r0 = pl.multiple_of(i * tt, 8)
            g = p_ref[pl.ds(r0, tt), POOL_W:2 * POOL_W]
            dout = do_ref[pl.ds(r0, tt), :]
            pb = pooled.astype(BF16)
            pre = jnp.dot(pb, w, preferred_element_type=F32)
            t1 = dout * _silu(g)
            dsc_acc = dsc_acc + jnp.sum(t1 * pre, axis=0, keepdims=True)
            dpre = (t1 * sc).astype(BF16)
            db_ref[pl.ds(r0, tt), POOL_W:2 * POOL_W] = dout * pre * sc * _dsilu(g)
            dw_acc = dw_acc + _dot(pb, dpre, TN)
            dpooled = _dot(dpre, w, NT)
            epad[pl.ds(r0, tt), :] = dpooled * inv
            return dsc_acc, dw_acc

        dsc_acc, dw_acc = lax.fori_loop(0, t // tt, tile, (jnp.zeros((1, POOL_W), F32), jnp.zeros((POOL_W, POOL_W), F32)))
        dsc_ref[...] = jnp.broadcast_to(dsc_acc, (8, POOL_W))
        dw_ref[...] = dw_acc

        def tile2(i, c):
            r0 = pl.multiple_of(i * tt, 8)
            n = tt + POOL_HALO
            cat = epad[pl.ds(r0, n), :]
            s2 = cat + pltpu.roll(cat, n - 1, 0)
            s4 = s2 + pltpu.roll(s2, n - 2, 0)
            s8 = s4 + pltpu.roll(s4, n - 4, 0)
            s16 = s8 + pltpu.roll(s8, n - 8, 0)
            inv, lane = _pool_counts(i * tt, tt)
            sel = _window_select(s2[:tt], s4[:tt], s8[:tt], s16[:tt], lane)
            db_ref[pl.ds(r0, tt), 0:POOL_W] = sel - cat[:tt] / inv
            return c

        lax.fori_loop(0, t // tt, tile2, 0)

    return pl.pallas_call(
        body, grid=(bsz,),
        in_specs=[pl.BlockSpec((None, t, 512), lambda b: (b, 0, B_BLK)),
                  pl.BlockSpec((None, t, POOL_W), lambda b: (b, 0, 1)),
                  pl.BlockSpec((POOL_W, POOL_W), lambda b: (0, 0)),
                  pl.BlockSpec((1, POOL_W), lambda b: (0, 0))],
        out_specs=[pl.BlockSpec((None, t, 512), lambda b: (b, 0, 0)),
                   pl.BlockSpec((None, 8, POOL_W), lambda b: (b, 0, 0)),
                   pl.BlockSpec((None, POOL_W, POOL_W), lambda b: (b, 0, 0))],
        out_shape=[jax.ShapeDtypeStruct((bsz, t, B_W), F32), jax.ShapeDtypeStruct((bsz, 8, POOL_W), F32),
                   jax.ShapeDtypeStruct((bsz, POOL_W, POOL_W), F32)],
        scratch_shapes=[pltpu.VMEM((t + POOL_HALO, POOL_W), F32), pltpu.VMEM((t + POOL_HALO, POOL_W), F32)],
        compiler_params=_cparams(("parallel",)), name=name)(proj3, dmixed, wbd, scale_row)


def _head_select_rows(hp):
    r, c = _iota((8, LANES), 0), _iota((8, LANES), 1)
    return ((r < 2) & (c == 2 * hp + r)).astype(F32)


def _foxgate_fwd(proj3, bias_row, name):
    bsz, t, _ = proj3.shape
    nt = t // LANES

    def body(f_ref, b_ref, cn_ref, ct_ref):
        bias = b_ref[...]
        i, j = _iota((LANES, LANES), 0), _iota((LANES, LANES), 1)
        lower = (j <= i).astype(BF16)
        spread = (_iota((LANES, FOX_W), 0) == _iota((LANES, FOX_W), 1) // 64).astype(BF16)
        select = [_head_select_rows(hp).astype(BF16) for hp in range(4)]
        offset = jnp.zeros((1, LANES), F32)
        for k in range(nt):
            rows = slice(k * LANES, (k + 1) * LANES)
            xg = f_ref[rows, :] + bias
            lf = jnp.minimum(xg, 0.0) - jnp.log(1.0 + jnp.exp(-jnp.abs(xg)))
            c = _exact_dot(lower, _split(lf, 3)) + offset
            offset = c[LANES - 1:LANES, :]
            parts = _split(c, 3)
            cn_ref[rows, :] = _head_sums(c, spread, 3)
            for hp in range(4):
                acc = _dot(select[hp], parts[0], NT)
                for p in parts[1:]:
                    acc = acc + _dot(select[hp], p, NT)
                ct_ref[hp, :, rows] = acc

    return pl.pallas_call(
        body, grid=(bsz,),
        in_specs=[pl.BlockSpec((None, t, 128), lambda b: (b, 0, F_BLK)), pl.BlockSpec((1, 128), lambda b: (0, 0))],
        out_specs=[pl.BlockSpec((None, t, FOX_W), lambda b: (b, 0, 0)),
                   pl.BlockSpec((None, 4, 8, t), lambda b: (b, 0, 0, 0))],
        out_shape=[jax.ShapeDtypeStruct((bsz, t, FOX_W), F32), jax.ShapeDtypeStruct((bsz, 4, 8, t), F32)],
        compiler_params=_cparams(("parallel",)), name=name)(proj3, bias_row)


def _foxgate_bwd(proj3, dc_nat, bias_row, name):
    bsz, t, _ = proj3.shape
    nt = t // LANES

    def body(f_ref, dc_ref, b_ref, df_ref, dbias_ref, run_sc):
        bias = b_ref[...]
        i, j = _iota((LANES, LANES), 0), _iota((LANES, LANES), 1)
        upper = (j >= i).astype(F32)
        valid = _iota((1, LANES), 1) < FOX_HEADS
        run_sc[...] = jnp.zeros((8, LANES), F32)
        dbias_ref[...] = jnp.zeros((8, LANES), F32)

        def tile(k, c):
            r0 = pl.multiple_of((nt - 1 - k) * LANES, LANES)
            dc = dc_ref[pl.ds(r0, LANES), :] + jnp.where(i == LANES - 1, run_sc[0:1, :], 0.0)
            dlf = jnp.dot(upper, dc, precision=HI, preferred_element_type=F32)
            xg = f_ref[pl.ds(r0, LANES), :] + bias
            df = jnp.where(valid, dlf * _sig(-xg), 0.0)
            df_ref[pl.ds(r0, LANES), :] = df
            run_sc[...] = dlf[0:8, :]
            dbias_ref[...] += jnp.sum(df, axis=0, keepdims=True)
            return c

        lax.fori_loop(0, nt, tile, 0)

    blk = pl.BlockSpec((None, t, 128), lambda b: (b, 0, 0))
    return pl.pallas_call(
        body, grid=(bsz,),
        in_specs=[pl.BlockSpec((None, t, 128), lambda b: (b, 0, F_BLK)), blk, pl.BlockSpec((1, 128), lambda b: (0, 0))],
        out_specs=[blk, pl.BlockSpec((None, 8, 128), lambda b: (b, 0, 0))],
        out_shape=[jax.ShapeDtypeStruct((bsz, t, F_W), F32), jax.ShapeDtypeStruct((bsz, 8, 128), F32)],
        scratch_shapes=[pltpu.VMEM((8, LANES), F32)],
        compiler_params=_cparams(("parallel",)), name=name)(proj3, dc_nat, bias_row)


def _fox_tile(t):
    return min(256, t)


def _fox_fwd(proj3, c_nat, c_t, name):
    bsz, t, _ = proj3.shape
    tq = tk = min(2 * _fox_tile(t), t)
    nq = t // tq

    def body(q_ref, kv_ref, cn_ref, ct_ref, og_ref, or_ref, lse_ref):
        i = pl.program_id(2)
        qblk = q_ref[...]
        first = _iota((1, 128), 1) < 64
        qv = qblk[:, 0:128] * 0.125
        qm = [jnp.where(first, qv, 0.0).astype(BF16), jnp.where(first, 0.0, qv).astype(BF16)]
        cqs = [cn_ref[:, 0:1], cn_ref[:, 64:65]]
        rows = _iota((tq, tk), 0) + i * tq

        def scores(j):
            c0 = pl.multiple_of(j * tk, tk)
            kb = kv_ref[pl.ds(c0, tk), 128:256].astype(BF16)
            return tuple(_dot(qm[h], kb, NT) + (cqs[h] - ct_ref[h:h + 1, pl.ds(c0, tk)]) for h in range(2))

        def absorb(j, state, s01, masked):
            c0 = pl.multiple_of(j * tk, tk)
            vblk = kv_ref[pl.ds(c0, tk), 256:384]
            vx = [jnp.where(first, vblk, 1.0).astype(BF16), jnp.where(first, 1.0, vblk).astype(BF16)]
            new = []
            for h in range(2):
                m, acc, s = state[2 * h], state[2 * h + 1], s01[h]
                if masked:
                    s = jnp.where(rows >= _iota((tq, tk), 1) + j * tk, s, MASK_VALUE)
                m_new = jnp.maximum(m, jnp.max(s, axis=1, keepdims=True))
                p = jnp.exp(s - m_new).astype(BF16)
                new += [m_new, jnp.exp(m - m_new) * acc + jnp.dot(p, vx[h], preferred_element_type=F32)]
            return tuple(new)

        init = (jnp.full((tq, 1), MASK_VALUE, F32), jnp.zeros((tq, 128), F32)) * 2
        n_full = (i * tq) // tk
        state = lax.fori_loop(0, n_full, lambda j, state: absorb(j, state, scores(j), False), init)
        m0, acc0, m1, acc1 = absorb(n_full, state, scores(n_full), True)
        l0, l1 = pltpu.roll(acc0, 64, 1), pltpu.roll(acc1, 64, 1)
        o = jnp.where(first, acc0 / l0, acc1 / l1)
        or_ref[...] = o
        og_ref[...] = o * _silu(qblk[:, 384:512])
        lse_ref[...] = jnp.where(first, m0 + jnp.log(l0), m1 + jnp.log(l1))

    out = jax.ShapeDtypeStruct((bsz, t, FOX_W), F32)
    blk = pl.BlockSpec((None, tq, 128), lambda b, p, i: (b, i, p))
    return pl.pallas_call(
        body, grid=(bsz, 4, nq),
        in_specs=[pl.BlockSpec((None, tq, 512), lambda b, p, i: (b, i, C_BLK0 + p)),
                  pl.BlockSpec((None, t, 512), lambda b, p, i: (b, 0, C_BLK0 + p)),
                  blk,
                  pl.BlockSpec((None, None, 8, t), lambda b, p, i: (b, p, 0, 0))],
        out_specs=[blk, blk, blk],
        out_shape=[out, out, out],
        compiler_params=_cparams(("parallel", "parallel", "arbitrary")), name=name)(proj3, proj3, c_nat, c_t)


def _fox_bwd(proj3, o_raw, dmixed, lse, c_nat, c_t, name):
    bsz, t, _ = proj3.shape
    tq = tk = min(2 * _fox_tile(t), t)
    nq = t // tq
    ratio = tk // tq

    def body(a_ref, or_ref, do_ref, lse_ref, cn_ref, ct_ref, dc_out, dct_out, drow_out, dq_sc, do_sc, dl_sc):
        def prep(i, c):
            r0 = pl.multiple_of(i * tq, tq)
            g = a_ref[pl.ds(r0, tq), 384:512]
            dout = do_ref[pl.ds(r0, tq), :]
            o = or_ref[pl.ds(r0, tq), :]
            dc_out[pl.ds(r0, tq), 384:512] = dout * o * _dsilu(g)
            do = dout * _silu(g)
            do_sc[pl.ds(r0, tq), :] = do
            prod = do * o
            d0 = jnp.sum(prod[:, 0:64], axis=1, keepdims=True)
            d1 = jnp.sum(prod[:, 64:128], axis=1, keepdims=True)
            dl_sc[pl.ds(r0, tq), :] = jnp.concatenate([jnp.broadcast_to(d0, (tq, 64)), jnp.broadcast_to(d1, (tq, 64))], axis=1)
            dq_sc[pl.ds(r0, tq), :] = jnp.zeros((tq, 128), F32)
            drow_out[pl.ds(r0, tq), :] = jnp.zeros((tq, 128), F32)
            return c

        lax.fori_loop(0, nq, prep, 0)
        dct_out[...] = jnp.zeros((8, t), F32)

        first = _iota((1, 128), 1) < 64

        def heads(v):
            return [jnp.where(first, v, 0.0).astype(BF16), jnp.where(first, 0.0, v).astype(BF16)]

        def kv_tile(j, c):
            c0 = pl.multiple_of(j * tk, tk)
            kb = a_ref[pl.ds(c0, tk), 128:256].astype(BF16)
            vb = a_ref[pl.ds(c0, tk), 256:384].astype(BF16)
            cks = [ct_ref[h:h + 1, pl.ds(c0, tk)] for h in range(2)]

            def q_step(i, carry, diagonal):
                dk, dv, dcol0, dcol1 = carry
                r0 = pl.multiple_of(i * tq, tq)
                causal = _iota((tq, tk), 0) + i * tq >= _iota((tq, tk), 1) + j * tk
                qv = a_ref[pl.ds(r0, tq), 0:128] * 0.125
                do = do_sc[pl.ds(r0, tq), :]
                qb, dob = qv.astype(BF16), do.astype(BF16)
                qm, dom = heads(qv), heads(do)
                full, dcols, rsums = [], [], []
                for h in range(2):
                    lse_h = lse_ref[pl.ds(r0, tq), 64 * h:64 * h + 1]
                    dl_h = dl_sc[pl.ds(r0, tq), 64 * h:64 * h + 1]
                    cq = cn_ref[pl.ds(r0, tq), 64 * h:64 * h + 1]
                    p = jnp.exp(_dot(qm[h], kb, NT) + (cq - cks[h]) - lse_h)
                    if diagonal:
                        p = jnp.where(causal, p, 0.0)
                    ds = p * (_dot(dom[h], vb, NT) - dl_h)
                    dsb = ds.astype(BF16)
                    full.append((_dot(p.astype(BF16), dob, TN), _dot(dsb, qb, TN),
                                 jnp.dot(dsb, kb, preferred_element_type=F32)))
                    dcols.append(jnp.sum(ds, axis=0, keepdims=True))
                    rsums.append(jnp.broadcast_to(jnp.sum(ds, axis=1, keepdims=True), (tq, 128)))
                dq_sc[pl.ds(r0, tq), :] += jnp.where(first, full[0][2], full[1][2]) * 0.125
                drow_out[pl.ds(r0, tq), :] += jnp.where(first, rsums[0], rsums[1])
                return (dk + jnp.where(first, full[0][1], full[1][1]), dv + jnp.where(first, full[0][0], full[1][0]),
                        dcol0 - dcols[0], dcol1 - dcols[1])

            carry = (jnp.zeros((tk, 128), F32), jnp.zeros((tk, 128), F32), jnp.zeros((1, tk), F32), jnp.zeros((1, tk), F32))
            for r in range(ratio):
                carry = q_step(ratio * j + r, carry, True)
            dk, dv, dcol0, dcol1 = lax.fori_loop(ratio * (j + 1), nq, functools.partial(q_step, diagonal=False), carry)
            dct_out[0:1, pl.ds(c0, tk)] = dcol0
            dct_out[1:2, pl.ds(c0, tk)] = dcol1
            dc_out[pl.ds(c0, tk), 128:256] = dk
            dc_out[pl.ds(c0, tk), 256:384] = dv
            return c

        lax.fori_loop(0, t // tk, kv_tile, 0)
        dc_out[:, 0:128] = dq_sc[...]

    blk = pl.BlockSpec((None, t, 128), lambda b, p: (b, 0, p))
    return pl.pallas_call(
        body, grid=(bsz, 4),
        in_specs=[pl.BlockSpec((None, t, 512), lambda b, p: (b, 0, C_BLK0 + p)),
                  blk,
                  pl.BlockSpec((None, t, 128), lambda b, p: (b, 0, 4 + p)),
                  blk, blk,
                  pl.BlockSpec((None, None, 8, t), lambda b, p: (b, p, 0, 0))],
        out_specs=[pl.BlockSpec((None, t, 512), lambda b, p: (b, 0, p)),
                   pl.BlockSpec((None, None, 8, t), lambda b, p: (b, p, 0, 0)), blk],
        out_shape=[jax.ShapeDtypeStruct((bsz, t, C_W), F32), jax.ShapeDtypeStruct((bsz, 4, 8, t), F32),
                   jax.ShapeDtypeStruct((bsz, t, FOX_W), F32)],
        scratch_shapes=[pltpu.VMEM((t, 128), F32), pltpu.VMEM((t, 128), F32), pltpu.VMEM((t, 128), F32)],
        compiler_params=_cparams(("parallel", "parallel")), name=name)(proj3, o_raw, dmixed, lse, c_nat, c_t)


def _mix_tm(n):
    return min(512, n)


def _outproj_fwd(x2, oa, ob, oc, wo, g_row, name):
    n, d = x2.shape
    tm = _mix_tm(n)

    def body(x_ref, oa_ref, ob_ref, oc_ref, w_ref, g_ref, y_ref, xo_ref):
        y = (jnp.dot(oa_ref[...].astype(BF16), w_ref[0:256, :], preferred_element_type=F32)
             + jnp.dot(ob_ref[...].astype(BF16), w_ref[256:512, :], preferred_element_type=F32)
             + jnp.dot(oc_ref[...].astype(BF16), w_ref[512:1024, :], preferred_element_type=F32))
        y_ref[...] = y
        xo_ref[...] = x_ref[...] + y * _rstd(y) * g_ref[...]

    row = lambda w: pl.BlockSpec((tm, w), lambda i: (i, 0))
    out = jax.ShapeDtypeStruct((n, d), F32)
    return pl.pallas_call(
        body, grid=(n // tm,),
        in_specs=[row(d), row(256), row(256), row(512), pl.BlockSpec((d, d), lambda i: (0, 0)),
                  pl.BlockSpec((1, d), lambda i: (0, 0))],
        out_specs=[row(d), row(d)], out_shape=[out, out],
        compiler_params=_cparams(("parallel",)), name=name)(x2, oa, ob, oc, wo, g_row)


def _outproj_fwd_loss(x2, oa, ob, oc, wo, g_row, target2, name):
    n, d = x2.shape
    tm = _mix_tm(n)

    def body(x_ref, oa_ref, ob_ref, oc_ref, w_ref, g_ref, t_ref, y_ref, dx_ref, l_ref):
        y = (jnp.dot(oa_ref[...].astype(BF16), w_ref[0:256, :], preferred_element_type=F32)
             + jnp.dot(ob_ref[...].astype(BF16), w_ref[256:512, :], preferred_element_type=F32)
             + jnp.dot(oc_ref[...].astype(BF16), w_ref[512:1024, :], preferred_element_type=F32))
        y_ref[...] = y
        err = (x_ref[...] + y * _rstd(y) * g_ref[...]) - t_ref[...]
        dx_ref[...] = err * (1.0 / d)

        @pl.when(pl.program_id(0) == 0)
        def _():
            l_ref[...] = jnp.zeros((8, 128), F32)

        l_ref[...] += jnp.sum(err * err)

    row = lambda w: pl.BlockSpec((tm, w), lambda i: (i, 0))
    out = jax.ShapeDtypeStruct((n, d), F32)
    return pl.pallas_call(
        body, grid=(n // tm,),
        in_specs=[row(d), row(256), row(256), row(512), pl.BlockSpec((d, d), lambda i: (0, 0)),
                  pl.BlockSpec((1, d), lambda i: (0, 0)), row(d)],
        out_specs=[row(d), row(d), pl.BlockSpec((8, 128), lambda i: (0, 0))],
        out_shape=[out, out, jax.ShapeDtypeStruct((8, 128), F32)],
        compiler_params=_cparams(("arbitrary",)), name=name)(x2, oa, ob, oc, wo, g_row, target2)


def _outproj_bwd(dxo, y, oa, ob, oc, wo, g_row, name):
    n, d = dxo.shape
    tm = _mix_tm(n)

    def body(dx_ref, y_ref, oa_ref, ob_ref, oc_ref, w_ref, g_ref, dm_ref, dw_ref, dg_ref):
        @pl.when(pl.program_id(0) == 0)
        def _():
            dw_ref[...] = jnp.zeros((d, d), F32)
            dg_ref[...] = jnp.zeros((8, d), F32)

        yv, dx = y_ref[...], dx_ref[...]
        r = _rstd(yv)
        yn = yv * r
        dg_ref[...] += jnp.sum(dx * yn, axis=0, keepdims=True)
        dyn = dx * g_ref[...]
        dy = (r * (dyn - yn * jnp.mean(dyn * yn, axis=-1, keepdims=True))).astype(BF16)
        dm_ref[...] = _dot(dy, w_ref[...], NT)
        dw_ref[0:256, :] += _dot(oa_ref[...].astype(BF16), dy, TN)
        dw_ref[256:512, :] += _dot(ob_ref[...].astype(BF16), dy, TN)
        dw_ref[512:1024, :] += _dot(oc_ref[...].astype(BF16), dy, TN)

    row = lambda w: pl.BlockSpec((tm, w), lambda i: (i, 0))
    fixed = lambda r, c: pl.BlockSpec((r, c), lambda i: (0, 0))
    return pl.pallas_call(
        body, grid=(n // tm,),
        in_specs=[row(d), row(d), row(256), row(256), row(512), fixed(d, d), fixed(1, d)],
        out_specs=[row(d), fixed(d, d), fixed(8, d)],
        out_shape=[jax.ShapeDtypeStruct((n, d), F32), jax.ShapeDtypeStruct((d, d), F32), jax.ShapeDtypeStruct((8, d), F32)],
        compiler_params=_cparams(("arbitrary",)), name=name)(dxo, y, oa, ob, oc, wo, g_row)


_PIECES = ((0, A_W), (A_W, B_W), (A_W + B_W, C_W), (A_W + B_W + C_W, F_W))


def _inproj_bwd_x(x2, dxo, g_row, w_int, pieces, name):
    n, d = x2.shape
    tm = min(512, n)

    def body(x_ref, dxo_ref, g_ref, w_ref, da_ref, db_ref, dc_ref, df_ref, dx_ref, dg_ref):
        @pl.when(pl.program_id(0) == 0)
        def _():
            dg_ref[...] = jnp.zeros((8, d), F32)

        dh = jnp.zeros((tm, d), F32)
        for ref, (o, w) in zip((da_ref, db_ref, dc_ref, df_ref), _PIECES):
            dh = dh + _dot(ref[...].astype(BF16), w_ref[:, o:o + w], NT)
        x = x_ref[...]
        r = _rstd(x)
        xn = x * r
        dg_ref[...] += jnp.sum(dh * xn, axis=0, keepdims=True)
        dxn = dh * g_ref[...]
        dx_ref[...] = dxo_ref[...] + r * (dxn - xn * jnp.mean(dxn * xn, axis=-1, keepdims=True))

    row = lambda w: pl.BlockSpec((tm, w), lambda i: (i, 0))
    fixed = lambda r, c: pl.BlockSpec((r, c), lambda i: (0, 0))
    return pl.pallas_call(
        body, grid=(n // tm,),
        in_specs=[row(d), row(d), fixed(1, d), fixed(d, E_INT)] + [row(w) for _, w in _PIECES],
        out_specs=[row(d), fixed(8, d)],
        out_shape=[jax.ShapeDtypeStruct((n, d), F32), jax.ShapeDtypeStruct((8, d), F32)],
        compiler_params=_cparams(("arbitrary",), vmem_mb=56), name=name)(x2, dxo, g_row, w_int, *pieces)


def _inproj_bwd_w(x2, g_row, pieces, name):
    n, d = x2.shape
    tm = min(256, n)

    def body(x_ref, g_ref, da_ref, db_ref, dc_ref, df_ref, dw_ref):
        @pl.when(pl.program_id(0) == 0)
        def _():
            dw_ref[...] = jnp.zeros((d, E_INT), F32)

        x = x_ref[...]
        h = (x * _rstd(x) * g_ref[...]).astype(BF16)
        for ref, (o, w) in zip((da_ref, db_ref, dc_ref, df_ref), _PIECES):
            dw_ref[:, o:o + w] += _dot(h, ref[...].astype(BF16), TN)

    row = lambda w: pl.BlockSpec((tm, w), lambda i: (i, 0))
    return pl.pallas_call(
        body, grid=(n // tm,),
        in_specs=[row(d), pl.BlockSpec((1, d), lambda i: (0, 0))] + [row(w) for _, w in _PIECES],
        out_specs=pl.BlockSpec((d, E_INT), lambda i: (0, 0)),
        out_shape=jax.ShapeDtypeStruct((d, E_INT), F32),
        compiler_params=_cparams(("arbitrary",), vmem_mb=56), name=name)(x2, g_row, *pieces)


def _block_diag(pool_w_l):
    z = jnp.zeros((64, 64), pool_w_l.dtype)
    return jnp.concatenate(
        [jnp.concatenate([pool_w_l[g] if c == g else z for c in range(4)], axis=1) for g in range(4)], axis=0)


def _pad_lanes(v, width=128):
    return jnp.pad(v, ((0, 0),) * (v.ndim - 1) + ((0, width - v.shape[-1]),))


def _local_step(x, target, lower_bounds, pre_norm_g, w_in_int, hgrn_norm_g, fox_f_bias, pool_w, pool_scale,
                w_out_bf, post_norm_g, on_weight_grads):
    bsz, t, d = x.shape
    n = bsz * t
    lbs = _lbs_fwd(lower_bounds)
    saved = []
    xc = x.reshape(n, d)
    for l in range(DEPTH):
        proj = _inproj_fwd(xc, pre_norm_g[l:l + 1], w_in_int[l], f"inproj_fwd{l}").reshape(bsz, t, E_INT)
        wbd = _block_diag(pool_w[l]).astype(BF16)
        bias_row = _pad_lanes(fox_f_bias[l:l + 1])
        oa, oa_raw, states = _hgrn_fwd(proj, lbs[l:l + 1], hgrn_norm_g[l:l + 1], f"hgrn_fwd{l}")
        ob = _pool_fwd(proj, wbd, pool_scale[l:l + 1], f"pool_fwd{l}")
        c_nat, c_t = _foxgate_fwd(proj, bias_row, f"foxgate_fwd{l}")
        oc, oc_raw, lse = _fox_fwd(proj, c_nat, c_t, f"fox_fwd{l}")
        mixed = (oa.reshape(n, -1), ob.reshape(n, -1), oc.reshape(n, -1))
        if l < DEPTH - 1:
            y, xn = _outproj_fwd(xc, *mixed, w_out_bf[l], post_norm_g[l:l + 1], f"outproj_fwd{l}")
        else:
            y, dx, sq = _outproj_fwd_loss(xc, *mixed, w_out_bf[l], post_norm_g[l:l + 1], target.reshape(n, d),
                                          f"outproj_fwd{l}")
        saved.append((xc, proj, wbd, bias_row, oa, oa_raw, states, ob, oc, oc_raw, lse, c_nat, c_t, y))
        xc = xn
    g = {k: [None] * DEPTH for k in ("pre", "hgn", "bias", "pool_w", "pool_scale", "post", "lbs")}
    handed = [None] * DEPTH
    for l in reversed(range(DEPTH)):
        xin, proj, wbd, bias_row, oa, oa_raw, states, ob, oc, oc_raw, lse, c_nat, c_t, y = saved[l]
        dmix, d_w_out, dpost = _outproj_bwd(dx, y, oa.reshape(n, -1), ob.reshape(n, -1), oc.reshape(n, -1),
                                            w_out_bf[l], post_norm_g[l:l + 1], f"outproj_bwd{l}")
        g["post"][l] = dpost[0]
        dmix3 = dmix.reshape(bsz, t, d)
        d_c, dct, drow = _fox_bwd(proj, oc_raw, dmix3, lse, c_nat, c_t, f"fox_bwd{l}")
        dc_nat = _pad_lanes(dct[:, :, 0:2, :].reshape(bsz, FOX_HEADS, t).transpose(0, 2, 1)
                            + drow.reshape(bsz, t, FOX_HEADS, 64)[..., 0])
        d_f, dbias = _foxgate_bwd(proj, dc_nat, bias_row, f"foxgate_bwd{l}")
        g["bias"][l] = jnp.sum(dbias[:, 0, :FOX_HEADS], axis=0)
        d_b, dscale, dwbd = _pool_bwd(proj, dmix3, wbd, pool_scale[l:l + 1], f"pool_bwd{l}")
        g["pool_scale"][l] = jnp.sum(dscale[:, 0], axis=0)
        dwbd = jnp.sum(dwbd, axis=0)
        g["pool_w"][l] = jnp.stack([dwbd[64 * k:64 * (k + 1), 64 * k:64 * (k + 1)] for k in range(4)])
        d_a, dgn, dlb = _hgrn_bwd(proj, oa_raw, dmix3, states, lbs[l:l + 1], hgrn_norm_g[l:l + 1], f"hgrn_bwd{l}")
        g["hgn"][l] = jnp.sum(dgn[:, 0], axis=0)
        g["lbs"][l] = jnp.sum(dlb[:, 0], axis=0)
        pieces = [p.reshape(n, -1) for p in (d_a, d_b, d_c, d_f)]
        handed[l] = on_weight_grads(l, _inproj_bwd_w(xin, pre_norm_g[l:l + 1], pieces, f"inproj_bwd_w{l}"), d_w_out)
        dx, dpre = _inproj_bwd_x(xin, dx, pre_norm_g[l:l + 1], w_in_int[l], pieces, f"inproj_bwd_x{l}")
        g["pre"][l] = dpre[0]
    grads = {k: jnp.stack(v) for k, v in g.items()}
    return sq, dx.reshape(bsz, t, d), grads, handed


def _place():
    return lax.axis_index("x"), lax.axis_index("y"), lax.axis_index("c")


def _other_chips(x, y):
    return [(1 - x, y), (x, 1 - y), (1 - x, 1 - y)]


_ANY = pl.BlockSpec(memory_space=pl.ANY)


def _gather_body(handshake, n_arrays):
    def body(*refs):
        srcs, dsts = refs[:n_arrays], refs[n_arrays:2 * n_arrays]
        ici_send, ici_recv, d2d_send, d2d_recv, local_sems = refs[2 * n_arrays:]
        x, y, c = _place()
        if handshake:
            barrier = pltpu.get_barrier_semaphore()
            for peer in [(px, py, c) for px, py in _other_chips(x, y)] + [(x, y, 1 - c)]:
                pl.semaphore_signal(barrier, inc=1, device_id=peer, device_id_type=MESH)
            pl.semaphore_wait(barrier, 4)
        me = 2 * x + y
        pairs = list(zip(srcs, dsts))
        order = [(k, j) for k in range(3) for j in range(n_arrays)]
        mine = [pltpu.make_async_copy(src, dst.at[me], local_sems.at[j]) for j, (src, dst) in enumerate(pairs)]
        for cp in mine:
            cp.start()
        chips = _other_chips(x, y)
        sends = [pltpu.make_async_remote_copy(
            src_ref=pairs[j][0].at[c], dst_ref=pairs[j][1].at[me, c], send_sem=ici_send.at[n], recv_sem=ici_recv.at[n],
            device_id=(chips[k][0], chips[k][1], c), device_id_type=MESH) for n, (k, j) in enumerate(order)]
        for cp in sends:
            cp.start()
        passed = [pltpu.make_async_remote_copy(
            src_ref=pairs[j][1].at[2 * chips[k][0] + chips[k][1], c], dst_ref=pairs[j][1].at[2 * chips[k][0] + chips[k][1], c],
            send_sem=d2d_send.at[n], recv_sem=d2d_recv.at[n], device_id=(x, y, 1 - c), device_id_type=MESH)
            for n, (k, j) in enumerate(order)]
        for n, (k, j) in enumerate(order):
            px, py = chips[k]
            src, dst = pairs[j]
            pltpu.make_async_remote_copy(
                src_ref=src.at[c], dst_ref=dst.at[2 * px + py, c], send_sem=ici_send.at[n], recv_sem=ici_recv.at[n],
                device_id=(px, py, c), device_id_type=MESH).wait_recv()
            passed[n].start()
        for n, (k, j) in enumerate(order):
            px, py = chips[k]
            src, dst = pairs[j]
            pltpu.make_async_remote_copy(
                src_ref=dst.at[2 * px + py, 1 - c], dst_ref=dst.at[2 * px + py, 1 - c], send_sem=d2d_send.at[n],
                recv_sem=d2d_recv.at[n], device_id=(x, y, 1 - c), device_id_type=MESH).wait_recv()
        for cp in sends + passed:
            cp.wait_send()
        for cp in mine:
            cp.wait()

    return body


def _gather_sems(n_arrays):
    return [pltpu.SemaphoreType.DMA((3 * n_arrays,))] * 4 + [pltpu.SemaphoreType.DMA((n_arrays,))]


def _gathered(a):
    return jax.ShapeDtypeStruct((N_CHIPS,) + a.shape, a.dtype)


def _gather_weights(arrays):
    n = len(arrays)
    return pl.pallas_call(
        _gather_body(False, n), in_specs=[_ANY] * n, out_specs=[_ANY] * n, out_shape=[_gathered(a) for a in arrays],
        scratch_shapes=_gather_sems(n), name="gather_weights")(*arrays)


def _gather_weights_beside(arrays):
    hbm = pltpu.MemorySpace.HBM
    n = len(arrays)
    srcs = [jax.new_ref(a, memory_space=hbm) for a in arrays]
    dsts = [jax.empty_ref(_gathered(a), memory_space=hbm) for a in arrays]
    body = _gather_body(True, n)

    @pl.kernel(mesh=plsc.ScalarSubcoreMesh(axis_name="sequencer", num_cores=1), name="gather_weights_beside",
               scratch_types=_gather_sems(n), compiler_params=pltpu.CompilerParams(collective_id=1))
    def launch(*sems):
        body(*srcs, *dsts, *sems)

    launch()
    return [d[...] for d in dsts]


def _swap_with_sibling(parts, name):
    k = len(parts)

    def body(*refs):
        src, dst = refs[:k], refs[k:2 * k]
        send_sems, recv_sems = refs[2 * k:]
        x, y, c = _place()
        cps = [pltpu.make_async_remote_copy(src_ref=src[j], dst_ref=dst[j], send_sem=send_sems.at[j], recv_sem=recv_sems.at[j],
                                            device_id=(x, y, 1 - c), device_id_type=MESH) for j in range(k)]
        for cp in cps:
            cp.start()
        for cp in cps:
            cp.wait()

    return pl.pallas_call(
        body, in_specs=[_ANY] * k, out_specs=[_ANY] * k,
        out_shape=[jax.ShapeDtypeStruct(p.shape, p.dtype) for p in parts],
        scratch_shapes=[pltpu.SemaphoreType.DMA((k,)), pltpu.SemaphoreType.DMA((k,))], name=name)(*parts)


N_PEERS = 7


def _grad_exchange_body():
    def body(pin_ref, pout_ref, lin_ref, lout_ref, send_sems, recv_sems):
        x, y, c = _place()
        barrier = pltpu.get_barrier_semaphore()
        for k in range(1, N_PEERS + 1):
            peer = (x ^ ((k >> 2) & 1), y ^ ((k >> 1) & 1), c ^ (k & 1))
            pl.semaphore_signal(barrier, inc=1, device_id=peer, device_id_type=MESH)
        pl.semaphore_wait(barrier, N_PEERS)
        me = 2 * x + y
        pairs = ((pin_ref, lin_ref), (pout_ref, lout_ref))
        cps = []
        for k, (px, py) in enumerate(_other_chips(x, y)):
            for r in range(2):
                for j, (src, dst) in enumerate(pairs):
                    cps.append(pltpu.make_async_remote_copy(
                        src_ref=src.at[2 * px + py, r], dst_ref=dst.at[2 * k + c], send_sem=send_sems.at[2 * (2 * k + r) + j],
                        recv_sem=recv_sems.at[2 * (2 * k + c) + j], device_id=(px, py, r), device_id_type=MESH))
        for j, (src, dst) in enumerate(pairs):
            cps.append(pltpu.make_async_remote_copy(
                src_ref=src.at[me, 1 - c], dst_ref=dst.at[N_PEERS - 1], send_sem=send_sems.at[2 * (N_PEERS - 1) + j],
                recv_sem=recv_sems.at[2 * (N_PEERS - 1) + j], device_id=(x, y, 1 - c), device_id_type=MESH))
        for cp in cps:
            cp.start()
        for s in range(N_PEERS):
            for j, (src, dst) in enumerate(pairs):
                pltpu.make_async_remote_copy(
                    src_ref=src.at[0, 0], dst_ref=dst.at[s], send_sem=send_sems.at[2 * s + j], recv_sem=recv_sems.at[2 * s + j],
                    device_id=(x, y, 1 - c), device_id_type=MESH).wait_recv()
        for cp in cps:
            cp.wait_send()

    return body


_EXCHANGE_SEMS = [pltpu.SemaphoreType.DMA((2 * N_PEERS,))] * 2


def _landing(p):
    return jax.ShapeDtypeStruct((N_PEERS,) + p.shape[2:], p.dtype)


def _grad_exchange_beside(pin, pout, name, collective_id):
    hbm = pltpu.MemorySpace.HBM
    pin_ref, pout_ref = jax.new_ref(pin, memory_space=hbm), jax.new_ref(pout, memory_space=hbm)
    lin_ref, lout_ref = jax.empty_ref(_landing(pin), memory_space=hbm), jax.empty_ref(_landing(pout), memory_space=hbm)
    body = _grad_exchange_body()

    @pl.kernel(mesh=plsc.ScalarSubcoreMesh(axis_name="sequencer", num_cores=1), name=name,
               scratch_types=_EXCHANGE_SEMS, compiler_params=pltpu.CompilerParams(collective_id=collective_id))
    def launch(send_sems, recv_sems):
        body(pin_ref, pout_ref, lin_ref, lout_ref, send_sems, recv_sems)

    launch()
    return lin_ref[...], lout_ref[...]


def _add_n(parts, name, with_bf16=False):
    r, c = parts[0].shape
    tr = 256 if r % 256 == 0 else r
    n = len(parts)

    def body(*refs):
        acc = refs[0][...].astype(F32)
        for ref in refs[1:n]:
            acc = acc + ref[...].astype(F32)
        refs[n][...] = acc
        if with_bf16:
            refs[n + 1][...] = acc.astype(BF16)

    blk = pl.BlockSpec((tr, c), lambda i: (i, 0))
    outs = [jax.ShapeDtypeStruct((r, c), F32)] + ([jax.ShapeDtypeStruct((r, c), BF16)] if with_bf16 else [])
    res = pl.pallas_call(
        body, grid=(r // tr,), in_specs=[blk] * n, out_specs=[blk] * len(outs),
        out_shape=outs, compiler_params=_cparams(("parallel",)), name=name)(*parts)
    return res if with_bf16 else res[0]


def _all_reduce_small(packet):
    r, w = packet.shape

    def body(p_ref, o_ref, buf, send_sems, recv_sems):
        x, y, c = _place()
        me = 4 * x + 2 * y + c
        buf[me] = p_ref[...]
        peers = []
        for k in range(1, 8):
            fx, fy, fc = (k >> 2) & 1, (k >> 1) & 1, k & 1
            peers.append((x ^ fx, y ^ fy, c ^ fc))
        cps = [pltpu.make_async_remote_copy(src_ref=p_ref, dst_ref=buf.at[me], send_sem=send_sems.at[k], recv_sem=recv_sems.at[k],
                                            device_id=peer, device_id_type=MESH) for k, peer in enumerate(peers)]
        for cp in cps:
            cp.start()
        for k, (px, py, pc) in enumerate(peers):
            pltpu.make_async_remote_copy(src_ref=p_ref, dst_ref=buf.at[4 * px + 2 * py + pc], send_sem=send_sems.at[k],
                                         recv_sem=recv_sems.at[k], device_id=(px, py, pc), device_id_type=MESH).wait_recv()
        for cp in cps:
            cp.wait_send()
        acc = buf[0]
        for k in range(1, 8):
            acc = acc + buf[k]
        o_ref[...] = acc

    vm = pl.BlockSpec(memory_space=pltpu.VMEM)
    return pl.pallas_call(
        body, in_specs=[vm], out_specs=vm, out_shape=jax.ShapeDtypeStruct((r, w), F32),
        scratch_shapes=[pltpu.VMEM((8, r, w), F32), pltpu.SemaphoreType.DMA((7,)), pltpu.SemaphoreType.DMA((7,))],
        name="all_reduce_small")(packet)


def _adamw_math(w, g, m, v):
    m = ADAM_B1 * m + (1.0 - ADAM_B1) * g
    v = ADAM_B2 * v + (1.0 - ADAM_B2) * (g * g)
    m_hat = m / (1.0 - ADAM_B1 ** ADAM_STEP)
    v_hat = v / (1.0 - ADAM_B2 ** ADAM_STEP)
    return -ADAM_LR * (m_hat / (jnp.sqrt(v_hat) + ADAM_EPS) + ADAM_WD * w), m, v


def _adamw(w, g_lower, g_upper, m, v, name):
    nl, r, c = w.shape
    tr = 128
    per_half = r // (2 * tr)

    def body(w_ref, lo_ref, up_ref, m_ref, v_ref, g_ref, d_ref, mo_ref, vo_ref):
        g = jnp.where(pl.program_id(1) == 0, lo_ref[...], up_ref[...])
        g_ref[...] = g
        d_ref[...], mo_ref[...], vo_ref[...] = _adamw_math(w_ref[...], g, m_ref[...], v_ref[...])

    blk = pl.BlockSpec((None, tr, c), lambda l, h, i: (l, h * per_half + i, 0))
    half = pl.BlockSpec((None, tr, c), lambda l, h, i: (l, i, 0))
    out = jax.ShapeDtypeStruct(w.shape, F32)
    return pl.pallas_call(
        body, grid=(nl, 2, per_half), in_specs=[blk, half, half, blk, blk], out_specs=[blk] * 4, out_shape=[out] * 4,
        compiler_params=_cparams(("parallel", "parallel", "parallel")), name=name)(w, g_lower, g_upper, m, v)


def _small_update(gsum, lower_bounds, wpack, mpack, vpack):
    r, w = gsum.shape
    lb_rows = DEPTH * HGRN_W // 128

    def body(g_ref, a_ref, w_ref, m_ref, v_ref, go_ref, d_ref, mo_ref, vo_ref):
        a = a_ref[...]
        a0, a1 = a[0:1], a[1:2]
        mx = jnp.maximum(a0, a1)
        e0, e1 = jnp.exp(a0 - mx), jnp.exp(a1 - mx)
        p0, p1 = e0 / (e0 + e1), e1 / (e0 + e1)
        g = g_ref[...]
        half = lb_rows // 2
        dl0 = jnp.concatenate([g[k:k + 1] for k in range(half)], axis=1)
        dl1 = jnp.concatenate([g[half + k:half + k + 1] for k in range(half)], axis=1)
        dp0 = (dl0 + dl1) - (dl0 + dl1)
        dp1 = dl1
        inner = p0 * dp0 + p1 * dp1
        da0, da1 = p0 * (dp0 - inner), p1 * (dp1 - inner)
        rows = [da0[:, 128 * k:128 * (k + 1)] for k in range(half)] + [da1[:, 128 * k:128 * (k + 1)] for k in range(half)]
        gfull = jnp.concatenate(rows + [g[lb_rows:]], axis=0)
        go_ref[...] = gfull
        d_ref[...], mo_ref[...], vo_ref[...] = _adamw_math(w_ref[...], gfull, m_ref[...], v_ref[...])

    vm = pl.BlockSpec(memory_space=pltpu.VMEM)
    out = jax.ShapeDtypeStruct((r, w), F32)
    return pl.pallas_call(body, in_specs=[vm] * 5, out_specs=[vm] * 4, out_shape=[out] * 4, name="small_update")(
        gsum, lower_bounds, wpack, mpack, vpack)


_SMALL = ("lower_bounds", "pre_norm_g", "hgrn_norm_g", "fox_f_bias", "pool_w", "pool_scale", "post_norm_g")


def _pack(parts):
    rows = []
    for k in _SMALL:
        f = parts[k].reshape(-1)
        pad = (-f.shape[0]) % (8 * 128)
        rows.append(jnp.pad(f, (0, pad)).reshape(-1, 128))
    rows.append(jnp.zeros((8, 128), F32))
    return jnp.concatenate(rows, axis=0)


def _unpack(pack, like):
    out, r = {}, 0
    for k in _SMALL:
        size = int(np.prod(like[k].shape))
        nr = -(-size // (8 * 128)) * 8
        out[k] = pack[r:r + nr].reshape(-1)[:size].reshape(like[k].shape)
        r += nr
    return out, r


def kernel(x, lower_bounds, pre_norm_g, w_in, hgrn_norm_g, fox_f_bias, pool_w, pool_scale, w_out, post_norm_g, loss_target, m_lower_bounds, m_pre_norm_g, m_w_in, m_hgrn_norm_g, m_fox_f_bias, m_pool_w, m_pool_scale, m_w_out, m_post_norm_g, v_lower_bounds, v_pre_norm_g, v_w_in, v_hgrn_norm_g, v_fox_f_bias, v_pool_w, v_pool_scale, v_w_out, v_post_norm_g):
    cx, cy, cc = _place()
    chip = 2 * cx + cy

    halves = lambda w, l: w[l].reshape(2, w.shape[1] // 2, w.shape[2]).astype(BF16)
    needed_first = _gather_weights([halves(w_in, 0)])
    needed_first, later = lax.optimization_barrier((needed_first, [halves(w_out, 0), halves(w_in, 1), halves(w_out, 1)]))
    later = _gather_weights_beside(later)
    w_in_int = [_internal_from_shards([a[q].reshape(D_MODEL, SHARD_W) for q in range(N_CHIPS)]) for a in (needed_first[0], later[1])]
    w_out_full = [a.reshape(D_MODEL, D_MODEL) for a in (later[0], later[2])]

    def on_weight_grads(l, d_w_in, d_w_out):
        pin = _shards_from_internal(d_w_in).reshape(N_CHIPS, 2, D_MODEL // 2, SHARD_W)
        pout = d_w_out.reshape(N_CHIPS, 2, D_MODEL // (2 * N_CHIPS), D_MODEL)
        own = [lax.dynamic_index_in_dim(lax.dynamic_index_in_dim(p, chip, 0, False), cc, 0, False) for p in (pin, pout)]
        return own, _grad_exchange_beside(pin.astype(BF16), pout.astype(BF16), f"grad_exchange{l}", 2 + l)

    sq, grad_x, g, handed = _local_step(x, loss_target, lower_bounds, pre_norm_g, w_in_int, hgrn_norm_g, fox_f_bias,
                                        pool_w, pool_scale, w_out_full, post_norm_g, on_weight_grads)
    first = cc == 0

    def finish(l, own, landed):
        mine = [_add_n([o] + [t[s] for s in range(N_PEERS)], f"grad_sum{l}_{j}") for j, (o, t) in enumerate(zip(own, landed))]
        theirs = _swap_with_sibling(mine, f"grad_swap{l}")
        return [(jnp.where(first, h, o), jnp.where(first, o, h)) for h, o in zip(mine, theirs)]

    grad_x, last = lax.optimization_barrier((grad_x, handed[1]))
    done = [None, finish(1, *last)]

    small = {"lower_bounds": g["lbs"], "pre_norm_g": g["pre"], "hgrn_norm_g": g["hgn"], "fox_f_bias": g["bias"],
             "pool_w": g["pool_w"], "pool_scale": g["pool_scale"], "post_norm_g": g["post"]}
    packet = _pack(small)
    nrows = packet.shape[0]
    packet = packet.at[nrows - 1].set(sq[0])
    gsum = _all_reduce_small(packet)
    loss = gsum[nrows - 1, 0] * (0.5 / D_MODEL)

    weights = {"lower_bounds": lower_bounds, "pre_norm_g": pre_norm_g, "hgrn_norm_g": hgrn_norm_g,
               "fox_f_bias": fox_f_bias, "pool_w": pool_w, "pool_scale": pool_scale, "post_norm_g": post_norm_g}
    moments_m = {"lower_bounds": m_lower_bounds, "pre_norm_g": m_pre_norm_g, "hgrn_norm_g": m_hgrn_norm_g,
                 "fox_f_bias": m_fox_f_bias, "pool_w": m_pool_w, "pool_scale": m_pool_scale, "post_norm_g": m_post_norm_g}
    moments_v = {"lower_bounds": v_lower_bounds, "pre_norm_g": v_pre_norm_g, "hgrn_norm_g": v_hgrn_norm_g,
                 "fox_f_bias": v_fox_f_bias, "pool_w": v_pool_w, "pool_scale": v_pool_scale, "post_norm_g": v_post_norm_g}
    gp, dp, mp, vp = _small_update(gsum, lower_bounds, _pack(weights), _pack(moments_m), _pack(moments_v))
    gs, _ = _unpack(gp, weights)
    ds, _ = _unpack(dp, weights)
    ms, _ = _unpack(mp, weights)
    vs, _ = _unpack(vp, weights)

    first_layer, _ = lax.optimization_barrier((handed[0], (done[1], gp, dp, mp, vp)))
    done[0] = finish(0, *first_layer)
    halves_of = lambda j, side: jnp.stack([done[l][j][side] for l in range(DEPTH)])
    grad_w_in, d_in, m_in, v_in = _adamw(w_in, halves_of(0, 0), halves_of(0, 1), m_w_in, v_w_in, "adamw_w_in")
    grad_w_out, d_out, m_out, v_out = _adamw(w_out, halves_of(1, 0), halves_of(1, 1), m_w_out, v_w_out, "adamw_w_out")

    def ordered(s, big_in, big_out):
        return (s["lower_bounds"], s["pre_norm_g"], big_in, s["hgrn_norm_g"], s["fox_f_bias"], s["pool_w"],
                s["pool_scale"], big_out, s["post_norm_g"])

    return (loss, grad_x, *ordered(gs, grad_w_in, grad_w_out), *ordered(ds, d_in, d_out),
            *ordered(ms, m_in, m_out), *ordered(vs, v_in, v_out))
```

```python
import functools

import numpy as np
import jax
import jax.numpy as jnp
from jax import lax
from jax.experimental import pallas as pl
from jax.experimental.pallas import tpu as pltpu
from jax.experimental.pallas import tpu_sc as plsc

F32 = jnp.float32
BF16 = jnp.bfloat16
HI = lax.Precision.HIGHEST
MESH = pl.DeviceIdType.MESH

NORM_EPS = 1e-6
MASK_VALUE = -1e30
TINY = 1e-30
ADAM_LR, ADAM_B1, ADAM_B2, ADAM_EPS, ADAM_WD, ADAM_STEP = 0.001, 0.9, 0.999, 1e-08, 0.01, 10

D_MODEL = 1024
DEPTH = 2
N_CHIPS = 4
CHUNK = 64
LANES = 128
HGRN_W, POOL_W, FOX_W, FOX_HEADS = 256, 256, 512, 8
POOL_WINDOWS = (2, 4, 8, 16)
POOL_HALO = 16
IN_WIDTH = 3592
SHARD_W = IN_WIDTH // N_CHIPS
A_W, B_W, C_W, F_W = 1024, 512, 2048, 128
E_INT = A_W + B_W + C_W + F_W
B_BLK = A_W // 512
C_BLK0 = (A_W + B_W) // 512
F_BLK = (A_W + B_W + C_W) // 128


def _segments():
    segs = []
    for hp in range(2):
        for part in range(4):
            segs.append((part * 256 + hp * 128, 128))
    segs.append((1024, 256))
    segs.append((1280, 256))
    for hp in range(4):
        for part in range(4):
            segs.append((1536 + part * 512 + hp * 128, 128))
    segs.append((3584, 8))
    return segs


_SEGS = _segments()


def _to_internal(w):
    parts = [w[..., s:s + n] for s, n in _SEGS]
    parts.append(jnp.zeros(w.shape[:-1] + (E_INT - IN_WIDTH,), w.dtype))
    return jnp.concatenate(parts, axis=-1)


def _to_original(w):
    offs, o = [], 0
    for s, n in _SEGS:
        offs.append((s, o, n))
        o += n
    parts = [w[..., o:o + n] for s, o, n in sorted(offs)]
    return jnp.concatenate(parts, axis=-1)


def _internal_from_shards(shards):
    parts = []
    for s, n in _SEGS:
        while n > 0:
            q, r = divmod(s, SHARD_W)
            take = min(n, SHARD_W - r)
            parts.append(shards[q][..., r:r + take])
            s, n = s + take, n - take
    parts.append(jnp.zeros(shards[0].shape[:-1] + (E_INT - IN_WIDTH,), shards[0].dtype))
    return jnp.concatenate(parts, axis=-1)


def _shards_from_internal(w):
    offs, o = [], 0
    for s, n in _SEGS:
        offs.append((s, o, n))
        o += n
    blocks = []
    for q in range(N_CHIPS):
        lo, hi = SHARD_W * q, SHARD_W * (q + 1)
        parts = [w[..., o + max(lo, s) - s:o + min(hi, s + n) - s] for s, o, n in sorted(offs) if s < hi and s + n > lo]
        blocks.append(jnp.concatenate(parts, axis=-1))
    return jnp.stack(blocks)


def _cparams(sem=None, vmem_mb=48):
    kw = dict(vmem_limit_bytes=vmem_mb * 1024 * 1024)
    if sem is not None:
        kw["dimension_semantics"] = sem
    return pltpu.CompilerParams(**kw)


def _sig(x):
    return 1.0 / (1.0 + jnp.exp(-x))


def _silu(x):
    return x * _sig(x)


def _dsilu(x):
    s = _sig(x)
    return s * (1.0 + x * (1.0 - s))


def _rstd(x):
    return lax.rsqrt(jnp.mean(x * x, axis=-1, keepdims=True) + NORM_EPS)


def _dot(a, b, dims, **kw):
    return lax.dot_general(a, b, (dims, ((), ())), preferred_element_type=F32, **kw)


NN = ((1,), (0,))
NT = ((1,), (1,))
TN = ((0,), (0,))


def _iota(shape, dim):
    return lax.broadcasted_iota(jnp.int32, shape, dim)


def _lbs_fwd(lower_bounds):
    def body(a_ref, o_ref):
        a = a_ref[...]
        a0, a1 = a[0:1], a[1:2]
        m = jnp.maximum(a0, a1)
        e0, e1 = jnp.exp(a0 - m), jnp.exp(a1 - m)
        p0, p1 = e0 / (e0 + e1), e1 / (e0 + e1)
        o_ref[...] = jnp.concatenate([p0 - p0, (p0 + p1) - p0], axis=0)

    return pl.pallas_call(body, out_shape=jax.ShapeDtypeStruct(lower_bounds.shape, F32), name="lbs_fwd")(lower_bounds)


def _inproj_fwd(x2, g_row, w_int, name):
    n, d = x2.shape
    e = w_int.shape[1]
    tm = min(512, n)

    def body(x_ref, g_ref, w_ref, o_ref):
        x = x_ref[...]
        h = (x * _rstd(x) * g_ref[...]).astype(BF16)
        o_ref[...] = jnp.dot(h, w_ref[...], preferred_element_type=F32)

    return pl.pallas_call(
        body, grid=(n // tm,),
        in_specs=[pl.BlockSpec((tm, d), lambda i: (i, 0)), pl.BlockSpec((1, d), lambda i: (0, 0)),
                  pl.BlockSpec((d, e), lambda i: (0, 0))],
        out_specs=pl.BlockSpec((tm, e), lambda i: (i, 0)),
        out_shape=jax.ShapeDtypeStruct((n, e), F32),
        compiler_params=_cparams(("parallel",)), name=name)(x2, g_row, w_int)


def _chunk_cumsum_matrix():
    i, j = _iota((LANES, LANES), 0), _iota((LANES, LANES), 1)
    return ((i <= j) & ((i // CHUNK) == (j // CHUNK))).astype(F32)


def _hgrn_gates(a, lb):
    qa, z = a[:, 0:128], a[:, 128:256]
    sg, sgn = _sig(z), _sig(-z)
    fg = lb + (1.0 - lb) * sg
    lf = jnp.log(jnp.maximum(fg, TINY))
    kk = (1.0 - lb) * sgn
    return qa * _sig(qa), kk, lf, sg, sgn, fg


def _hgrn_fwd(proj3, lbs_row, gn_col, name):
    bsz, t, _ = proj3.shape
    nt = t // LANES

    def body(a_ref, lb_ref, gn_ref, og_ref, or_ref):
        lb = lb_ref[...]
        gn = gn_ref[...]
        umat = _chunk_cumsum_matrix()
        lane64 = _iota((1, LANES), 1) % CHUNK

        def tile(i, carry):
            r0 = pl.multiple_of(i * LANES, LANES)
            a = a_ref[pl.ds(r0, LANES), :]
            qq, kk, lf, _, _, _ = _hgrn_gates(a, lb)
            va, ga = a[:, 256:384], a[:, 384:512]
            q_t, k_t, v_t = qq.T, kk.T, va.T
            b_t = jnp.dot(lf.T, umat, precision=HI, preferred_element_type=F32)
            new_s, o_heads = [], []
            for h in range(2):
                s_h = carry[h]
                rs = slice(CHUNK * h, CHUNK * (h + 1))
                qh, kh, vh, bh = q_t[rs], k_t[rs], v_t[rs], b_t[rs]
                inter = []
                for c in range(2):
                    cs = slice(CHUNK * c, CHUNK * (c + 1))
                    b_ = bh[:, cs]
                    qt = (qh[:, cs] * jnp.exp(b_)).astype(BF16)
                    inter.append(_dot(s_h.astype(BF16), qt, TN))
                    bl = b_[:, CHUNK - 1:CHUNK]
                    kt = (kh[:, cs] * jnp.exp(bl - b_)).astype(BF16)
                    s_h = jnp.exp(bl) * s_h + _dot(kt, vh[:, cs].astype(BF16), NT)
                new_s.append(s_h)

                acc = jnp.concatenate(inter, axis=1) + jnp.sum(qh * kh, axis=0, keepdims=True) * vh
                for dlt in range(1, CHUNK):
                    kr, br, vr = pltpu.roll(kh, dlt, 1), pltpu.roll(bh, dlt, 1), pltpu.roll(vh, dlt, 1)
                    e = jnp.exp(jnp.minimum(bh - br, 0.0))
                    att = jnp.sum(qh * kr * e, axis=0, keepdims=True)
                    acc = acc + jnp.where(lane64 >= dlt, att, 0.0) * vr
                o_heads.append(acc)
            normed = []
            for h in range(2):
                o_h = o_heads[h]
                ms = jnp.mean(o_h * o_h, axis=0, keepdims=True)
                normed.append(o_h * lax.rsqrt(ms + NORM_EPS) * gn[CHUNK * h:CHUNK * (h + 1)])
            or_ref[pl.ds(r0, LANES), :] = jnp.concatenate(o_heads, axis=0).T
            og_ref[pl.ds(r0, LANES), :] = jnp.concatenate(normed, axis=0).T * _silu(ga)
            return tuple(new_s)

        zero = jnp.zeros((CHUNK, CHUNK), F32)
        lax.fori_loop(0, nt, tile, (zero, zero))

    out = jax.ShapeDtypeStruct((bsz, t, HGRN_W), F32)
    return pl.pallas_call(
        body, grid=(bsz, 2),
        in_specs=[pl.BlockSpec((None, t, 512), lambda b, p: (b, 0, p)),
                  pl.BlockSpec((1, 128), lambda b, p: (0, p)),
                  pl.BlockSpec((128, 1), lambda b, p: (p, 0))],
        out_specs=[pl.BlockSpec((None, t, 128), lambda b, p: (b, 0, p)),
                   pl.BlockSpec((None, t, 128), lambda b, p: (b, 0, p))],
        out_shape=[out, out],
        compiler_params=_cparams(("parallel", "parallel")), name=name)(proj3, lbs_row, gn_col)


def _hgrn_bwd(proj3, o_raw, dmixed, lbs_row, gn_row, name):
    bsz, t, _ = proj3.shape
    nt = t // LANES
    nchunk = t // CHUNK

    def body(a_ref, or_ref, do_ref, lb_ref, gn_ref, da_ref, dgn_ref, dlb_ref, s_sc):
        lb = lb_ref[...]
        gn = gn_ref[...]
        umat = _chunk_cumsum_matrix()
        lane = _iota((1, LANES), 1)
        lane64 = lane % CHUNK
        half = lane < CHUNK

        def t_layout(a):
            qq, kk, lf, sg, sgn, fg = _hgrn_gates(a, lb)
            b_t = jnp.dot(lf.T, umat, precision=HI, preferred_element_type=F32)
            return qq.T, kk.T, a[:, 256:384].T, b_t, (sg, sgn, fg)

        def fwd_tile(i, carry):
            r0 = pl.multiple_of(i * LANES, LANES)
            q_t, k_t, v_t, b_t, _ = t_layout(a_ref[pl.ds(r0, LANES), :])
            new_s = []
            for h in range(2):
                s_h = carry[h]
                rs = slice(CHUNK * h, CHUNK * (h + 1))
                for c in range(2):
                    cs = slice(CHUNK * c, CHUNK * (c + 1))
                    s_sc[h, 2 * i + c] = s_h
                    b_ = b_t[rs, cs]
                    bl = b_[:, CHUNK - 1:CHUNK]
                    kt = (k_t[rs, cs] * jnp.exp(bl - b_)).astype(BF16)
                    s_h = jnp.exp(bl) * s_h + _dot(kt, v_t[rs, cs].astype(BF16), NT)
                new_s.append(s_h)
            return tuple(new_s)

        zero = jnp.zeros((CHUNK, CHUNK), F32)
        lax.fori_loop(0, nt, fwd_tile, (zero, zero))

        def half_mean(v):
            m0 = jnp.sum(jnp.where(half, v, 0.0), axis=1, keepdims=True) * (1.0 / CHUNK)
            m1 = jnp.sum(jnp.where(half, 0.0, v), axis=1, keepdims=True) * (1.0 / CHUNK)
            return jnp.where(half, m0, m1)

        def bwd_tile(k, carry):
            ds0, ds1, dgn_acc, dlb_acc = carry
            i = nt - 1 - k
            r0 = pl.multiple_of(i * LANES, LANES)
            a = a_ref[pl.ds(r0, LANES), :]
            qa, z, ga = a[:, 0:128], a[:, 128:256], a[:, 384:512]
            q_t, k_t, v_t, b_t, (sg, sgn, fg) = t_layout(a)
            oraw = or_ref[pl.ds(r0, LANES), :]
            dout = do_ref[pl.ds(r0, LANES), :]
            r = lax.rsqrt(half_mean(oraw * oraw) + NORM_EPS)
            xn = oraw * r
            dga = dout * (xn * gn) * _dsilu(ga)
            don = dout * _silu(ga)
            dgn_acc = dgn_acc + jnp.sum(don * xn, axis=0, keepdims=True)
            dxn = don * gn
            do_t = (r * (dxn - xn * half_mean(dxn * xn))).T
            new_ds, dq_h, dk_h, dv_h, db_h = [], [], [], [], []
            for h in range(2):
                ds_h = (ds0, ds1)[h]
                rs = slice(CHUNK * h, CHUNK * (h + 1))
                qh, kh, vh, bh, doh = q_t[rs], k_t[rs], v_t[rs], b_t[rs], do_t[rs]
                dq_c, dk_c, dv_c, dbl_c = [None, None], [None, None], [None, None], [None, None]
                for c in (1, 0):
                    cs = slice(CHUNK * c, CHUNK * (c + 1))
                    s_n = s_sc[h, 2 * i + c]
                    b_ = bh[:, cs]
                    eb = jnp.exp(b_)
                    bl = b_[:, CHUNK - 1:CHUNK]
                    ek = jnp.exp(bl - b_)
                    ebl = jnp.exp(bl)
                    qt, kt = qh[:, cs] * eb, kh[:, cs] * ek
                    do_c = doh[:, cs].astype(BF16)
                    dsb = ds_h.astype(BF16)
                    dv_c[c] = _dot(dsb, kt.astype(BF16), TN)
                    dkt = _dot(dsb, vh[:, cs].astype(BF16), NN)
                    dqt = _dot(s_n.astype(BF16), do_c, NN)
                    dbl_c[c] = jnp.sum(ds_h * s_n, axis=1, keepdims=True) * ebl + jnp.sum(dkt * kt, axis=1, keepdims=True)
                    dq_c[c], dk_c[c] = dqt * eb, dkt * ek
                    ds_h = ebl * ds_h + _dot(qt.astype(BF16), do_c, NT)
                new_ds.append(ds_h)

                att0 = jnp.sum(qh * kh, axis=0, keepdims=True)
                datt0 = jnp.sum(doh * vh, axis=0, keepdims=True)
                dqh = jnp.concatenate(dq_c, axis=1) + datt0 * kh
                dkh = jnp.concatenate(dk_c, axis=1) + datt0 * qh
                dvh = jnp.concatenate(dv_c, axis=1) + att0 * doh
                for dlt in range(1, CHUNK):
                    kr, br, vr = pltpu.roll(kh, dlt, 1), pltpu.roll(bh, dlt, 1), pltpu.roll(vh, dlt, 1)
                    e = jnp.where(lane64 >= dlt, jnp.exp(jnp.minimum(bh - br, 0.0)), 0.0)
                    qe = qh * e
                    att = jnp.sum(qe * kr, axis=0, keepdims=True)
                    datt = jnp.sum(doh * vr, axis=0, keepdims=True)
                    dqh = dqh + datt * (kr * e)
                    dkh = dkh + pltpu.roll(datt * qe, LANES - dlt, 1)
                    dvh = dvh + pltpu.roll(att * doh, LANES - dlt, 1)
                dbl = jnp.where(half, dbl_c[0], dbl_c[1])
                db_h.append(qh * dqh - kh * dkh + jnp.where(lane64 == CHUNK - 1, dbl, 0.0))
                dq_h.append(dqh)
                dk_h.append(dkh)
                dv_h.append(dvh)
            dqq = jnp.concatenate(dq_h, axis=0).T
            dkk = jnp.concatenate(dk_h, axis=0).T
            dvv = jnp.concatenate(dv_h, axis=0).T
            dlf = _dot(jnp.concatenate(db_h, axis=0), umat, NT, precision=HI).T
            dqa = dqq * _dsilu(qa)
            dfg = jnp.where(fg > TINY, dlf / fg, 0.0)
            dz = (dfg - dkk) * (1.0 - lb) * sg * sgn
            dlb_acc = dlb_acc + jnp.sum(dfg * (1.0 - sg) - dkk * sgn, axis=0, keepdims=True)
            da_ref[pl.ds(r0, LANES), :] = jnp.concatenate([dqa, dz, dvv, dga], axis=1)
            return new_ds[0], new_ds[1], dgn_acc, dlb_acc

        zrow = jnp.zeros((1, LANES), F32)
        _, _, dgn_acc, dlb_acc = lax.fori_loop(0, nt, bwd_tile, (zero, zero, zrow, zrow))
        dgn_ref[...] = jnp.broadcast_to(dgn_acc, (8, LANES))
        dlb_ref[...] = jnp.broadcast_to(dlb_acc, (8, LANES))

    rows = jax.ShapeDtypeStruct((bsz, 8, HGRN_W), F32)
    return pl.pallas_call(
        body, grid=(bsz, 2),
        in_specs=[pl.BlockSpec((None, t, 512), lambda b, p: (b, 0, p)),
                  pl.BlockSpec((None, t, 128), lambda b, p: (b, 0, p)),
                  pl.BlockSpec((None, t, 128), lambda b, p: (b, 0, p)),
                  pl.BlockSpec((1, 128), lambda b, p: (0, p)),
                  pl.BlockSpec((1, 128), lambda b, p: (0, p))],
        out_specs=[pl.BlockSpec((None, t, 512), lambda b, p: (b, 0, p)),
                   pl.BlockSpec((None, 8, 128), lambda b, p: (b, 0, p)),
                   pl.BlockSpec((None, 8, 128), lambda b, p: (b, 0, p))],
        out_shape=[jax.ShapeDtypeStruct((bsz, t, A_W), F32), rows, rows],
        scratch_shapes=[pltpu.VMEM((2, nchunk, CHUNK, CHUNK), F32)],
        compiler_params=_cparams(("parallel", "parallel")), name=name)(proj3, o_raw, dmixed, lbs_row, gn_row)


N_LEVELS = 6


def _hgrn_tables():
    t = np.arange(LANES)
    j = np.arange(LANES)[None, :]
    same_chunk = (t[:, None] // CHUNK) == (j // CHUNK)
    w = np.zeros((2 + N_LEVELS, LANES, LANES), np.float32)
    w[0] = same_chunk & (j <= t[:, None])
    w[1] = same_chunk & (j > t[:, None])
    maskf = np.zeros((N_LEVELS, LANES, LANES), np.float32)
    rightf = np.zeros((N_LEVELS, LANES, LANES), np.float32)
    for li in range(N_LEVELS):
        m = (CHUNK // 2) >> li
        start = t - (t % (2 * m))
        right = (t % (2 * m)) >= m
        first = np.where(right, start + m, t + 1)
        last = np.where(right, t, start + m - 1)
        w[2 + li] = (j >= first[:, None]) & (j <= last[:, None])
        maskf[li] = (t[:, None] // (2 * m)) == (j // (2 * m))
        rightf[li] = right[:, None]
    w = w[:-1]
    return jnp.asarray(w.reshape(-1, LANES), BF16), jnp.asarray(np.tile(maskf, (1, 2, 1))), jnp.asarray(rightf)


def _split(x, n):
    parts = []
    for _ in range(n - 1):
        p = x.astype(BF16)
        parts.append(p)
        x = x - p.astype(F32)
    parts.append(x.astype(BF16))
    return parts


def _exact_dot(w, parts):
    acc = jnp.dot(w, parts[0], preferred_element_type=F32)
    for p in parts[1:]:
        acc = acc + jnp.dot(w, p, preferred_element_type=F32)
    return acc


def _head_sums(v, ones_blk, n=2):
    parts = _split(v, n)
    acc = jnp.dot(parts[0], ones_blk, preferred_element_type=F32)
    for p in parts[1:]:
        acc = acc + jnp.dot(p, ones_blk, preferred_element_type=F32)
    return acc


def _hgrn_consts():
    r, c = _iota((LANES, LANES), 0), _iota((LANES, LANES), 1)
    ones_blk = ((r // CHUNK) == (c // CHUNK)).astype(BF16)
    eye2 = (_iota((2 * LANES, LANES), 0) % LANES) == _iota((2 * LANES, LANES), 1)
    first = _iota((1, LANES), 1) < CHUNK
    return eye2, ones_blk, jnp.ones((LANES, LANES), BF16), first


def _stack_heads(v, first):
    return jnp.concatenate([jnp.where(first, v, 0.0), jnp.where(first, 0.0, v)], axis=0)


def _pick_heads(v2, first):
    return jnp.where(first, v2[:LANES], v2[LANES:])


def _hgrn_levels(qq, kk, lf, zall, mk_ref, rt_ref, first, d_att=None):
    att = jnp.zeros((2 * LANES, LANES), F32)
    dq = dk = db = jnp.zeros((LANES, LANES), F32)
    for li in range(N_LEVELS):
        rt = rt_ref[li]
        e = jnp.exp(zall[(2 + li) * LANES:(3 + li) * LANES] if li < N_LEVELS - 1 else lf * rt)
        mk = mk_ref[li]
        qef, kef = e * rt, e * (1.0 - rt)
        qe, ke = (qq * qef).astype(BF16), (kk * kef).astype(BF16)
        qe2 = _stack_heads(qe, first)
        att = att + _dot(qe2, ke, NT) * mk
        if d_att is not None:
            dam = (d_att * mk).astype(BF16)
            dqe = _pick_heads(jnp.dot(dam, ke, preferred_element_type=F32), first)
            dke = _dot(dam, qe2, TN)
            dq = dq + dqe * qef
            dk = dk + dke * kef
            db = db + (dqe * qe.astype(F32) - dke * ke.astype(F32))
    return att, dq, dk, db


def _hgrn_fwd(proj3, lbs_row, gn_row, name):
    bsz, t, _ = proj3.shape
    nt = t // LANES
    w_all, maskf, rightf = _hgrn_tables()

    def body(a_ref, lb_ref, gn_ref, w_ref, mk_ref, rt_ref, og_ref, or_ref, st_ref):
        lb = lb_ref[...]
        gn = gn_ref[...]
        eye2, ones_blk, ones_all, first = _hgrn_consts()

        def tile(i, carry):
            r0 = pl.multiple_of(i * LANES, LANES)
            a = a_ref[pl.ds(r0, LANES), :]
            qq, kk, lf, _, _, _ = _hgrn_gates(a, lb)
            va, ga = a[:, 256:384], a[:, 384:512]
            parts = _split(lf, 3)
            zall = _exact_dot(w_ref[...], parts)
            eb, ee = jnp.exp(zall[0:LANES]), jnp.exp(zall[LANES:2 * LANES])
            vb = va.astype(BF16)
            att, _, _, _ = _hgrn_levels(qq, kk, lf, zall, mk_ref, rt_ref, first)
            diag = _head_sums(_stack_heads(qq * kk, first), ones_all)
            a2 = (att + jnp.where(eye2, diag, 0.0)).astype(BF16)
            o_in = _pick_heads(jnp.dot(a2, vb, preferred_element_type=F32), first)
            qeb, keb = (qq * eb).astype(BF16), (kk * ee).astype(BF16)
            new_s, o_heads = [], []
            for h in range(2):
                hs = slice(CHUNK * h, CHUNK * (h + 1))
                o_h = o_in[:, hs]
                st = carry[h]
                chunks = []
                for c in range(2):
                    rc = slice(CHUNK * c, CHUNK * (c + 1))
                    st_ref[h, 2 * i + c] = st
                    chunks.append(o_h[rc] + _dot(qeb[rc, hs], st.astype(BF16), NT))
                    ebl = eb[CHUNK * (c + 1) - 1:CHUNK * (c + 1), hs]
                    st = st * ebl + _dot(vb[rc, hs], keb[rc, hs], TN)
                new_s.append(st)
                o_heads.append(jnp.concatenate(chunks, axis=0))
            o = jnp.concatenate(o_heads, axis=1)
            ms = _head_sums(o * o, ones_blk) * (1.0 / CHUNK)
            or_ref[pl.ds(r0, LANES), :] = o
            og_ref[pl.ds(r0, LANES), :] = o * lax.rsqrt(ms + NORM_EPS) * gn * _silu(ga)
            return tuple(new_s)

        zero = jnp.zeros((CHUNK, CHUNK), F32)
        per_step = 4 if nt % 4 == 0 else 2

        def step(i, carry):
            for k in range(per_step):
                carry = tile(per_step * i + k, carry)
            return carry

        lax.fori_loop(0, nt // per_step, step, (zero, zero))

    out = jax.ShapeDtypeStruct((bsz, t, HGRN_W), F32)
    row = pl.BlockSpec((1, 128), lambda b, p: (0, p))
    return pl.pallas_call(
        body, grid=(bsz, 2),
        in_specs=[pl.BlockSpec((None, t, 512), lambda b, p: (b, 0, p)), row, row,
                  pl.BlockSpec(w_all.shape, lambda b, p: (0, 0)),
                  pl.BlockSpec(maskf.shape, lambda b, p: (0, 0, 0)),
                  pl.BlockSpec(rightf.shape, lambda b, p: (0, 0, 0))],
        out_specs=[pl.BlockSpec((None, t, 128), lambda b, p: (b, 0, p)),
                   pl.BlockSpec((None, t, 128), lambda b, p: (b, 0, p)),
                   pl.BlockSpec((None, 2, t // CHUNK, CHUNK, CHUNK), lambda b, p: (b, p, 0, 0, 0))],
        out_shape=[out, out, jax.ShapeDtypeStruct((bsz, 4, t // CHUNK, CHUNK, CHUNK), F32)],
        compiler_params=_cparams(("parallel", "parallel")), name=name)(proj3, lbs_row, gn_row, w_all, maskf, rightf)


def _exact_dot_r(parts, hs, ones_h):
    acc = jnp.dot(parts[0][:, hs], ones_h, preferred_element_type=F32)
    for p in parts[1:]:
        acc = acc + jnp.dot(p[:, hs], ones_h, preferred_element_type=F32)
    return acc


def _hgrn_bwd(proj3, o_raw, dmixed, states, lbs_row, gn_row, name):
    bsz, t, _ = proj3.shape
    nt = t // LANES
    nchunk = t // CHUNK
    w_all, maskf, rightf = _hgrn_tables()

    def body(a_ref, or_ref, do_ref, s_sc, lb_ref, gn_ref, w_ref, mk_ref, rt_ref, da_ref, dgn_ref, dlb_ref):
        lb = lb_ref[...]
        gn = gn_ref[...]
        eye2, ones_blk, ones_all, first = _hgrn_consts()
        r_i, c_i = _iota((LANES, LANES), 0), _iota((LANES, LANES), 1)
        suffix = ((c_i >= r_i) & ((r_i // CHUNK) == (c_i // CHUNK))).astype(BF16)
        row64 = _iota((LANES, CHUNK), 0)
        zero = jnp.zeros((CHUNK, CHUNK), F32)

        def bwd_tile(k, carry):
            dst0, dst1, dgn_acc, dlb_acc = carry
            i = nt - 1 - k
            r0 = pl.multiple_of(i * LANES, LANES)
            a = a_ref[pl.ds(r0, LANES), :]
            qa, ga = a[:, 0:128], a[:, 384:512]
            qq, kk, lf, sg, sgn, fg = _hgrn_gates(a, lb)
            parts = _split(lf, 3)
            zall = _exact_dot(w_ref[...], parts)
            eb, ee = jnp.exp(zall[0:LANES]), jnp.exp(zall[LANES:2 * LANES])
            vb = a[:, 256:384].astype(BF16)
            oraw = or_ref[pl.ds(r0, LANES), :]
            dout = do_ref[pl.ds(r0, LANES), :]
            r = lax.rsqrt(_head_sums(oraw * oraw, ones_blk) * (1.0 / CHUNK) + NORM_EPS)
            xn = oraw * r
            dga = dout * (xn * gn) * _dsilu(ga)
            don = dout * _silu(ga)
            dgn_acc = dgn_acc + jnp.sum(don * xn, axis=0, keepdims=True)
            dxn = don * gn
            do = r * (dxn - xn * (_head_sums(dxn * xn, ones_blk) * (1.0 / CHUNK)))
            dob = do.astype(BF16)
            do2 = _stack_heads(dob, first)
            d_att = _dot(do2, vb, NT)
            att, dq, dk, db_lv = _hgrn_levels(qq, kk, lf, zall, mk_ref, rt_ref, first, d_att)
            a2 = att + jnp.where(eye2, _head_sums(_stack_heads(qq * kk, first), ones_all), 0.0)
            dv_in = _dot(a2.astype(BF16), do2, TN)
            ddiag = _pick_heads(_head_sums(jnp.where(eye2, d_att, 0.0), ones_all), first)
            dq_in, dk_in = dq + ddiag * kk, dk + ddiag * qq
            qe_f, ke_f = qq * eb, kk * ee
            qeb, keb = qe_f.astype(BF16), ke_f.astype(BF16)
            new_ds, dq_h, dk_h, dv_h, dbl_h = [], [], [], [], []
            for h in range(2):
                hs = slice(CHUNK * h, CHUNK * (h + 1))
                dv, dq_i, dk_i = dv_in[:, hs], dq_in[:, hs], dk_in[:, hs]
                dst = (dst0, dst1)[h]
                dq_c, dk_c, dv_c, dbl_c = [None, None], [None, None], [None, None], [None, None]
                for c in (1, 0):
                    rc = slice(CHUNK * c, CHUNK * (c + 1))
                    st_n = s_sc[h, 2 * i + c]
                    ebl = eb[CHUNK * (c + 1) - 1:CHUNK * (c + 1), hs]
                    dstb = dst.astype(BF16)
                    dv_c[c] = _dot(keb[rc, hs], dstb, NT)
                    dke = jnp.dot(vb[rc, hs], dstb, preferred_element_type=F32)
                    dqe = jnp.dot(dob[rc, hs], st_n.astype(BF16), preferred_element_type=F32)
                    dbl_c[c] = (jnp.sum(dst * st_n, axis=0, keepdims=True) * ebl
                                + jnp.sum(dke * ke_f[rc, hs], axis=0, keepdims=True))
                    dq_c[c], dk_c[c] = dqe * eb[rc, hs], dke * ee[rc, hs]
                    dst = dst * ebl + _dot(dob[rc, hs], qeb[rc, hs], TN)
                new_ds.append(dst)
                dq_x, dk_x = jnp.concatenate(dq_c, axis=0), jnp.concatenate(dk_c, axis=0)
                dq_h.append(dq_i + dq_x)
                dk_h.append(dk_i + dk_x)
                dv_h.append(dv + jnp.concatenate(dv_c, axis=0))
                dbl_h.append(qq[:, hs] * dq_x - kk[:, hs] * dk_x
                             + jnp.where(row64 == CHUNK - 1, dbl_c[0], 0.0) + jnp.where(row64 == LANES - 1, dbl_c[1], 0.0))
            dqq = jnp.concatenate(dq_h, axis=1)
            dkk = jnp.concatenate(dk_h, axis=1)
            dvv = jnp.concatenate(dv_h, axis=1)
            db = db_lv + jnp.concatenate(dbl_h, axis=1)
            dlf = _exact_dot(suffix, _split(db, 3))
            dqa = dqq * _dsilu(qa)
            dfg = jnp.where(fg > TINY, dlf / fg, 0.0)
            dz = (dfg - dkk) * (1.0 - lb) * sg * sgn
            dlb_acc = dlb_acc + jnp.sum(dfg * (1.0 - sg) - dkk * sgn, axis=0, keepdims=True)
            da_ref[pl.ds(r0, LANES), :] = jnp.concatenate([dqa, dz, dvv, dga], axis=1).astype(BF16)
            return new_ds[0], new_ds[1], dgn_acc, dlb_acc

        zrow = jnp.zeros((1, LANES), F32)
        per_step = 4 if nt % 4 == 0 else 2

        def step(k, carry):
            for r in range(per_step):
                carry = bwd_tile(per_step * k + r, carry)
            return carry

        _, _, dgn_acc, dlb_acc = lax.fori_loop(0, nt // per_step, step, (zero, zero, zrow, zrow))
        dgn_ref[...] = jnp.broadcast_to(dgn_acc, (8, LANES))
        dlb_ref[...] = jnp.broadcast_to(dlb_acc, (8, LANES))

    rows = jax.ShapeDtypeStruct((bsz, 8, HGRN_W), F32)
    row = pl.BlockSpec((1, 128), lambda b, p: (0, p))
    blk = pl.BlockSpec((None, t, 128), lambda b, p: (b, 0, p))
    return pl.pallas_call(
        body, grid=(bsz, 2),
        in_specs=[pl.BlockSpec((None, t, 512), lambda b, p: (b, 0, p)), blk, blk,
                  pl.BlockSpec((None, 2, nchunk, CHUNK, CHUNK), lambda b, p: (b, p, 0, 0, 0)), row, row,
                  pl.BlockSpec(w_all.shape, lambda b, p: (0, 0)),
                  pl.BlockSpec(maskf.shape, lambda b, p: (0, 0, 0)),
                  pl.BlockSpec(rightf.shape, lambda b, p: (0, 0, 0))],
        out_specs=[pl.BlockSpec((None, t, 512), lambda b, p: (b, 0, p)),
                   pl.BlockSpec((None, 8, 128), lambda b, p: (b, 0, p)),
                   pl.BlockSpec((None, 8, 128), lambda b, p: (b, 0, p))],
        out_shape=[jax.ShapeDtypeStruct((bsz, t, A_W), BF16), rows, rows],
        compiler_params=_cparams(("parallel", "parallel")), name=name)(
            proj3, o_raw, dmixed, states, lbs_row, gn_row, w_all, maskf, rightf)


def _pool_tt(t):
    return min(256, t)


def _window_select(s2, s4, s8, s16, lane):
    return jnp.where(lane < 64, s2, jnp.where(lane < 128, s4, jnp.where(lane < 192, s8, s16)))


def _pool_counts(t0, tt):
    lane = _iota((tt, POOL_W), 1)
    tpos = (_iota((tt, POOL_W), 0) + t0 + 1).astype(F32)
    win = jnp.where(lane < 64, 2.0, jnp.where(lane < 128, 4.0, jnp.where(lane < 192, 8.0, 16.0)))
    return 1.0 / jnp.minimum(tpos, win), lane


def _pooled_tile(upad_ref, i, tt):
    r0 = pl.multiple_of(i * tt, 8)
    cat = upad_ref[pl.ds(r0, tt + POOL_HALO), :]
    s2 = cat + pltpu.roll(cat, 1, 0)
    s4 = s2 + pltpu.roll(s2, 2, 0)
    s8 = s4 + pltpu.roll(s4, 4, 0)
    s16 = s8 + pltpu.roll(s8, 8, 0)
    inv, lane = _pool_counts(i * tt, tt)
    sel = _window_select(s2[POOL_HALO:], s4[POOL_HALO:], s8[POOL_HALO:], s16[POOL_HALO:], lane)
    return sel * inv - cat[POOL_HALO:], inv, lane


def _pool_fwd(proj3, wbd, scale_row, name):
    bsz, t, _ = proj3.shape
    tt = _pool_tt(t)

    def body(p_ref, w_ref, sc_ref, o_ref, upad):
        upad[0:POOL_HALO, :] = jnp.zeros((POOL_HALO, POOL_W), F32)
        upad[POOL_HALO:, :] = p_ref[:, 0:POOL_W]
        w = w_ref[...]
        sc = sc_ref[...]

        def tile(i, c):
            pooled, _, _ = _pooled_tile(upad, i, tt)
            r0 = pl.multiple_of(i * tt, 8)
            g = p_ref[pl.ds(r0, tt), POOL_W:2 * POOL_W]
            pre = jnp.dot(pooled.astype(BF16), w, preferred_element_type=F32)
            o_ref[pl.ds(r0, tt), :] = pre * sc * _silu(g)
            return c

        lax.fori_loop(0, t // tt, tile, 0)

    return pl.pallas_call(
        body, grid=(bsz,),
        in_specs=[pl.BlockSpec((None, t, 512), lambda b: (b, 0, B_BLK)),
                  pl.BlockSpec((POOL_W, POOL_W), lambda b: (0, 0)),
                  pl.BlockSpec((1, POOL_W), lambda b: (0, 0))],
        out_specs=pl.BlockSpec((None, t, POOL_W), lambda b: (b, 0, 0)),
        out_shape=jax.ShapeDtypeStruct((bsz, t, POOL_W), F32),
        scratch_shapes=[pltpu.VMEM((t + POOL_HALO, POOL_W), F32)],
        compiler_params=_cparams(("parallel",)), name=name)(proj3, wbd, scale_row)


def _pool_bwd(proj3, dmixed, wbd, scale_row, name):
    bsz, t, _ = proj3.shape
    tt = _pool_tt(t)

    def body(p_ref, do_ref, w_ref, sc_ref, db_ref, dsc_ref, dw_ref, upad, epad):
        upad[0:POOL_HALO, :] = jnp.zeros((POOL_HALO, POOL_W), F32)
        upad[POOL_HALO:, :] = p_ref[:, 0:POOL_W]
        epad[t:, :] = jnp.zeros((POOL_HALO, POOL_W), F32)
        w = w_ref[...]
        sc = sc_ref[...]

        def tile(i, carry):
            dsc_acc, dw_acc = carry
            pooled, inv, _ = _pooled_tile(upad, i, tt)
            r0 = pl.multiple_of(i * tt, 8)
            g = p_ref[pl.ds(r0, tt), POOL_W:2 * POOL_W]
            dout = do_ref[pl.ds(r0, tt), :]
            pb = pooled.astype(BF16)
            pre = jnp.dot(pb, w, preferred_element_type=F32)
            t1 = dout * _silu(g)
            dsc_acc = dsc_acc + jnp.sum(t1 * pre, axis=0, keepdims=True)
            dpre = (t1 * sc).astype(BF16)
            db_ref[pl.ds(r0, tt), POOL_W:2 * POOL_W] = (dout * pre * sc * _dsilu(g)).astype(BF16)
            dw_acc = dw_acc + _dot(pb, dpre, TN)
            dpooled = _dot(dpre, w, NT)
            epad[pl.ds(r0, tt), :] = dpooled * inv
            return dsc_acc, dw_acc

        dsc_acc, dw_acc = lax.fori_loop(0, t // tt, tile, (jnp.zeros((1, POOL_W), F32), jnp.zeros((POOL_W, POOL_W), F32)))
        dsc_ref[...] = jnp.broadcast_to(dsc_acc, (8, POOL_W))
        dw_ref[...] = dw_acc

        def tile2(i, c):
            r0 = pl.multiple_of(i * tt, 8)
            n = tt + POOL_HALO
            cat = epad[pl.ds(r0, n), :]
            s2 = cat + pltpu.roll(cat, n - 1, 0)
            s4 = s2 + pltpu.roll(s2, n - 2, 0)
            s8 = s4 + pltpu.roll(s4, n - 4, 0)
            s16 = s8 + pltpu.roll(s8, n - 8, 0)
            inv, lane = _pool_counts(i * tt, tt)
            sel = _window_select(s2[:tt], s4[:tt], s8[:tt], s16[:tt], lane)
            db_ref[pl.ds(r0, tt), 0:POOL_W] = (sel - cat[:tt] / inv).astype(BF16)
            return c

        lax.fori_loop(0, t // tt, tile2, 0)

    return pl.pallas_call(
        body, grid=(bsz,),
        in_specs=[pl.BlockSpec((None, t, 512), lambda b: (b, 0, B_BLK)),
                  pl.BlockSpec((None, t, POOL_W), lambda b: (b, 0, 1)),
                  pl.BlockSpec((POOL_W, POOL_W), lambda b: (0, 0)),
                  pl.BlockSpec((1, POOL_W), lambda b: (0, 0))],
        out_specs=[pl.BlockSpec((None, t, 512), lambda b: (b, 0, 0)),
                   pl.BlockSpec((None, 8, POOL_W), lambda b: (b, 0, 0)),
                   pl.BlockSpec((None, POOL_W, POOL_W), lambda b: (b, 0, 0))],
        out_shape=[jax.ShapeDtypeStruct((bsz, t, B_W), BF16), jax.ShapeDtypeStruct((bsz, 8, POOL_W), F32),
                   jax.ShapeDtypeStruct((bsz, POOL_W, POOL_W), F32)],
        scratch_shapes=[pltpu.VMEM((t + POOL_HALO, POOL_W), F32), pltpu.VMEM((t + POOL_HALO, POOL_W), F32)],
        compiler_params=_cparams(("parallel",)), name=name)(proj3, dmixed, wbd, scale_row)


def _head_select_rows(hp):
    r, c = _iota((8, LANES), 0), _iota((8, LANES), 1)
    return ((r < 2) & (c == 2 * hp + r)).astype(F32)


def _foxgate_fwd(proj3, bias_row, name):
    bsz, t, _ = proj3.shape
    nt = t // LANES

    def body(f_ref, b_ref, cn_ref, ct_ref):
        bias = b_ref[...]
        i, j = _iota((LANES, LANES), 0), _iota((LANES, LANES), 1)
        lower = (j <= i).astype(BF16)
        spread = (_iota((LANES, FOX_W), 0) == _iota((LANES, FOX_W), 1) // 64).astype(BF16)
        select = [_head_select_rows(hp).astype(BF16) for hp in range(4)]
        offset = jnp.zeros((1, LANES), F32)
        for k in range(nt):
            rows = slice(k * LANES, (k + 1) * LANES)
            xg = f_ref[rows, :] + bias
            lf = jnp.minimum(xg, 0.0) - jnp.log(1.0 + jnp.exp(-jnp.abs(xg)))
            c = _exact_dot(lower, _split(lf, 3)) + offset
            offset = c[LANES - 1:LANES, :]
            parts = _split(c, 3)
            cn_ref[rows, :] = _head_sums(c, spread, 3)
            for hp in range(4):
                acc = _dot(select[hp], parts[0], NT)
                for p in parts[1:]:
                    acc = acc + _dot(select[hp], p, NT)
                ct_ref[hp, :, rows] = acc

    return pl.pallas_call(
        body, grid=(bsz,),
        in_specs=[pl.BlockSpec((None, t, 128), lambda b: (b, 0, F_BLK)), pl.BlockSpec((1, 128), lambda b: (0, 0))],
        out_specs=[pl.BlockSpec((None, t, FOX_W), lambda b: (b, 0, 0)),
                   pl.BlockSpec((None, 4, 8, t), lambda b: (b, 0, 0, 0))],
        out_shape=[jax.ShapeDtypeStruct((bsz, t, FOX_W), F32), jax.ShapeDtypeStruct((bsz, 4, 8, t), F32)],
        compiler_params=_cparams(("parallel",)), name=name)(proj3, bias_row)


def _foxgate_bwd(proj3, dc_nat, bias_row, name):
    bsz, t, _ = proj3.shape
    nt = t // LANES

    def body(f_ref, dc_ref, b_ref, df_ref, dbias_ref, run_sc):
        bias = b_ref[...]
        i, j = _iota((LANES, LANES), 0), _iota((LANES, LANES), 1)
        upper = (j >= i).astype(F32)
        valid = _iota((1, LANES), 1) < FOX_HEADS
        run_sc[...] = jnp.zeros((8, LANES), F32)
        dbias_ref[...] = jnp.zeros((8, LANES), F32)

        def tile(k, c):
            r0 = pl.multiple_of((nt - 1 - k) * LANES, LANES)
            dc = dc_ref[pl.ds(r0, LANES), :] + jnp.where(i == LANES - 1, run_sc[0:1, :], 0.0)
            dlf = jnp.dot(upper, dc, precision=HI, preferred_element_type=F32)
            xg = f_ref[pl.ds(r0, LANES), :] + bias
            df = jnp.where(valid, dlf * _sig(-xg), 0.0)
            df_ref[pl.ds(r0, LANES), :] = df.astype(BF16)
            run_sc[...] = dlf[0:8, :]
            dbias_ref[...] += jnp.sum(df, axis=0, keepdims=True)
            return c

        lax.fori_loop(0, nt, tile, 0)

    blk = pl.BlockSpec((None, t, 128), lambda b: (b, 0, 0))
    return pl.pallas_call(
        body, grid=(bsz,),
        in_specs=[pl.BlockSpec((None, t, 128), lambda b: (b, 0, F_BLK)), blk, pl.BlockSpec((1, 128), lambda b: (0, 0))],
        out_specs=[blk, pl.BlockSpec((None, 8, 128), lambda b: (b, 0, 0))],
        out_shape=[jax.ShapeDtypeStruct((bsz, t, F_W), BF16), jax.ShapeDtypeStruct((bsz, 8, 128), F32)],
        scratch_shapes=[pltpu.VMEM((8, LANES), F32)],
        compiler_params=_cparams(("parallel",)), name=name)(proj3, dc_nat, bias_row)


def _fox_tile(t):
    return min(256, t)


def _fox_fwd(proj3, c_nat, c_t, name):
    bsz, t, _ = proj3.shape
    tq = tk = min(2 * _fox_tile(t), t)
    nq = t // tq

    def body(q_ref, kv_ref, cn_ref, ct_ref, og_ref, or_ref, lse_ref):
        i = pl.program_id(2)
        qblk = q_ref[...]
        first = _iota((1, 128), 1) < 64
        qv = qblk[:, 0:128] * 0.125
        qm = [jnp.where(first, qv, 0.0).astype(BF16), jnp.where(first, 0.0, qv).astype(BF16)]
        cqs = [cn_ref[:, 0:1], cn_ref[:, 64:65]]
        rows = _iota((tq, tk), 0) + i * tq

        def scores(j):
            c0 = pl.multiple_of(j * tk, tk)
            kb = kv_ref[pl.ds(c0, tk), 128:256].astype(BF16)
            return tuple(_dot(qm[h], kb, NT) + (cqs[h] - ct_ref[h:h + 1, pl.ds(c0, tk)]) for h in range(2))

        def absorb(j, state, s01, masked):
            c0 = pl.multiple_of(j * tk, tk)
            vblk = kv_ref[pl.ds(c0, tk), 256:384]
            vx = [jnp.where(first, vblk, 1.0).astype(BF16), jnp.where(first, 1.0, vblk).astype(BF16)]
            new = []
            for h in range(2):
                m, acc, s = state[2 * h], state[2 * h + 1], s01[h]
                if masked:
                    s = jnp.where(rows >= _iota((tq, tk), 1) + j * tk, s, MASK_VALUE)
                m_new = jnp.maximum(m, jnp.max(s, axis=1, keepdims=True))
                p = jnp.exp(s - m_new).astype(BF16)
                new += [m_new, jnp.exp(m - m_new) * acc + jnp.dot(p, vx[h], preferred_element_type=F32)]
            return tuple(new)

        init = (jnp.full((tq, 1), MASK_VALUE, F32), jnp.zeros((tq, 128), F32)) * 2
        n_full = (i * tq) // tk
        state = lax.fori_loop(0, n_full, lambda j, state: absorb(j, state, scores(j), False), init)
        m0, acc0, m1, acc1 = absorb(n_full, state, scores(n_full), True)
        l0, l1 = pltpu.roll(acc0, 64, 1), pltpu.roll(acc1, 64, 1)
        o = jnp.where(first, acc0 / l0, acc1 / l1)
        or_ref[...] = o
        og_ref[...] = o * _silu(qblk[:, 384:512])
        lse_ref[...] = jnp.where(first, m0 + jnp.log(l0), m1 + jnp.log(l1))

    out = jax.ShapeDtypeStruct((bsz, t, FOX_W), F32)
    blk = pl.BlockSpec((None, tq, 128), lambda b, p, i: (b, i, p))
    return pl.pallas_call(
        body, grid=(bsz, 4, nq),
        in_specs=[pl.BlockSpec((None, tq, 512), lambda b, p, i: (b, i, C_BLK0 + p)),
                  pl.BlockSpec((None, t, 512), lambda b, p, i: (b, 0, C_BLK0 + p)),
                  blk,
                  pl.BlockSpec((None, None, 8, t), lambda b, p, i: (b, p, 0, 0))],
        out_specs=[blk, blk, blk],
        out_shape=[out, out, out],
        compiler_params=_cparams(("parallel", "parallel", "arbitrary")), name=name)(proj3, proj3, c_nat, c_t)


def _fox_bwd(proj3, o_raw, dmixed, lse, c_nat, c_t, name):
    bsz, t, _ = proj3.shape
    tq = tk = min(2 * _fox_tile(t), t)
    nq = t // tq
    ratio = tk // tq

    def body(a_ref, or_ref, do_ref, lse_ref, cn_ref, ct_ref, dc_out, dct_out, drow_out, dq_sc, do_sc, dl_sc):
        def prep(i, c):
            r0 = pl.multiple_of(i * tq, tq)
            g = a_ref[pl.ds(r0, tq), 384:512]
            dout = do_ref[pl.ds(r0, tq), :]
            o = or_ref[pl.ds(r0, tq), :]
            dc_out[pl.ds(r0, tq), 384:512] = (dout * o * _dsilu(g)).astype(BF16)
            do = dout * _silu(g)
            do_sc[pl.ds(r0, tq), :] = do
            prod = do * o
            d0 = jnp.sum(prod[:, 0:64], axis=1, keepdims=True)
            d1 = jnp.sum(prod[:, 64:128], axis=1, keepdims=True)
            dl_sc[pl.ds(r0, tq), :] = jnp.concatenate([jnp.broadcast_to(d0, (tq, 64)), jnp.broadcast_to(d1, (tq, 64))], axis=1)
            dq_sc[pl.ds(r0, tq), :] = jnp.zeros((tq, 128), F32)
            drow_out[pl.ds(r0, tq), :] = jnp.zeros((tq, 128), F32)
            return c

        lax.fori_loop(0, nq, prep, 0)
        dct_out[...] = jnp.zeros((8, t), F32)

        first = _iota((1, 128), 1) < 64

        def heads(v):
            return [jnp.where(first, v, 0.0).astype(BF16), jnp.where(first, 0.0, v).astype(BF16)]

        def kv_tile(j, c):
            c0 = pl.multiple_of(j * tk, tk)
            kb = a_ref[pl.ds(c0, tk), 128:256].astype(BF16)
            vb = a_ref[pl.ds(c0, tk), 256:384].astype(BF16)
            cks = [ct_ref[h:h + 1, pl.ds(c0, tk)] for h in range(2)]

            def q_step(i, carry, diagonal):
                dk, dv, dcol0, dcol1 = carry
                r0 = pl.multiple_of(i * tq, tq)
                causal = _iota((tq, tk), 0) + i * tq >= _iota((tq, tk), 1) + j * tk
                qv = a_ref[pl.ds(r0, tq), 0:128] * 0.125
                do = do_sc[pl.ds(r0, tq), :]
                qb, dob = qv.astype(BF16), do.astype(BF16)
                qm, dom = heads(qv), heads(do)
                full, dcols, rsums = [], [], []
                for h in range(2):
                    lse_h = lse_ref[pl.ds(r0, tq), 64 * h:64 * h + 1]
                    dl_h = dl_sc[pl.ds(r0, tq), 64 * h:64 * h + 1]
                    cq = cn_ref[pl.ds(r0, tq), 64 * h:64 * h + 1]
                    p = jnp.exp(_dot(qm[h], kb, NT) + (cq - cks[h]) - lse_h)
                    if diagonal:
                        p = jnp.where(causal, p, 0.0)
                    ds = p * (_dot(dom[h], vb, NT) - dl_h)
                    dsb = ds.astype(BF16)
                    full.append((_dot(p.astype(BF16), dob, TN), _dot(dsb, qb, TN),
                                 jnp.dot(dsb, kb, preferred_element_type=F32)))
                    dcols.append(jnp.sum(ds, axis=0, keepdims=True))
                    rsums.append(jnp.broadcast_to(jnp.sum(ds, axis=1, keepdims=True), (tq, 128)))
                dq_sc[pl.ds(r0, tq), :] += jnp.where(first, full[0][2], full[1][2]) * 0.125
                drow_out[pl.ds(r0, tq), :] += jnp.where(first, rsums[0], rsums[1])
                return (dk + jnp.where(first, full[0][1], full[1][1]), dv + jnp.where(first, full[0][0], full[1][0]),
                        dcol0 - dcols[0], dcol1 - dcols[1])

            carry = (jnp.zeros((tk, 128), F32), jnp.zeros((tk, 128), F32), jnp.zeros((1, tk), F32), jnp.zeros((1, tk), F32))
            for r in range(ratio):
                carry = q_step(ratio * j + r, carry, True)
            dk, dv, dcol0, dcol1 = lax.fori_loop(ratio * (j + 1), nq, functools.partial(q_step, diagonal=False), carry)
            dct_out[0:1, pl.ds(c0, tk)] = dcol0
            dct_out[1:2, pl.ds(c0, tk)] = dcol1
            dc_out[pl.ds(c0, tk), 128:256] = dk.astype(BF16)
            dc_out[pl.ds(c0, tk), 256:384] = dv.astype(BF16)
            return c

        lax.fori_loop(0, t // tk, kv_tile, 0)
        dc_out[:, 0:128] = dq_sc[...].astype(BF16)

    blk = pl.BlockSpec((None, t, 128), lambda b, p: (b, 0, p))
    return pl.pallas_call(
        body, grid=(bsz, 4),
        in_specs=[pl.BlockSpec((None, t, 512), lambda b, p: (b, 0, C_BLK0 + p)),
                  blk,
                  pl.BlockSpec((None, t, 128), lambda b, p: (b, 0, 4 + p)),
                  blk, blk,
                  pl.BlockSpec((None, None, 8, t), lambda b, p: (b, p, 0, 0))],
        out_specs=[pl.BlockSpec((None, t, 512), lambda b, p: (b, 0, p)),
                   pl.BlockSpec((None, None, 8, t), lambda b, p: (b, p, 0, 0)), blk],
        out_shape=[jax.ShapeDtypeStruct((bsz, t, C_W), BF16), jax.ShapeDtypeStruct((bsz, 4, 8, t), F32),
                   jax.ShapeDtypeStruct((bsz, t, FOX_W), F32)],
        scratch_shapes=[pltpu.VMEM((t, 128), F32), pltpu.VMEM((t, 128), F32), pltpu.VMEM((t, 128), F32)],
        compiler_params=_cparams(("parallel", "parallel")), name=name)(proj3, o_raw, dmixed, lse, c_nat, c_t)


def _mix_tm(n):
    return min(512, n)


def _outproj_fwd(x2, oa, ob, oc, wo, g_row, name):
    n, d = x2.shape
    tm = _mix_tm(n)

    def body(x_ref, oa_ref, ob_ref, oc_ref, w_ref, g_ref, y_ref, xo_ref):
        y = (jnp.dot(oa_ref[...].astype(BF16), w_ref[0:256, :], preferred_element_type=F32)
             + jnp.dot(ob_ref[...].astype(BF16), w_ref[256:512, :], preferred_element_type=F32)
             + jnp.dot(oc_ref[...].astype(BF16), w_ref[512:1024, :], preferred_element_type=F32))
        y_ref[...] = y
        xo_ref[...] = x_ref[...] + y * _rstd(y) * g_ref[...]

    row = lambda w: pl.BlockSpec((tm, w), lambda i: (i, 0))
    out = jax.ShapeDtypeStruct((n, d), F32)
    return pl.pallas_call(
        body, grid=(n // tm,),
        in_specs=[row(d), row(256), row(256), row(512), pl.BlockSpec((d, d), lambda i: (0, 0)),
                  pl.BlockSpec((1, d), lambda i: (0, 0))],
        out_specs=[row(d), row(d)], out_shape=[out, out],
        compiler_params=_cparams(("parallel",)), name=name)(x2, oa, ob, oc, wo, g_row)


def _outproj_fwd_loss(x2, oa, ob, oc, wo, g_row, target2, name):
    n, d = x2.shape
    tm = _mix_tm(n)

    def body(x_ref, oa_ref, ob_ref, oc_ref, w_ref, g_ref, t_ref, y_ref, dx_ref, l_ref):
        y = (jnp.dot(oa_ref[...].astype(BF16), w_ref[0:256, :], preferred_element_type=F32)
             + jnp.dot(ob_ref[...].astype(BF16), w_ref[256:512, :], preferred_element_type=F32)
             + jnp.dot(oc_ref[...].astype(BF16), w_ref[512:1024, :], preferred_element_type=F32))
        y_ref[...] = y
        err = (x_ref[...] + y * _rstd(y) * g_ref[...]) - t_ref[...]
        dx_ref[...] = err * (1.0 / d)

        @pl.when(pl.program_id(0) == 0)
        def _():
            l_ref[...] = jnp.zeros((8, 128), F32)

        l_ref[...] += jnp.sum(err * err)

    row = lambda w: pl.BlockSpec((tm, w), lambda i: (i, 0))
    out = jax.ShapeDtypeStruct((n, d), F32)
    return pl.pallas_call(
        body, grid=(n // tm,),
        in_specs=[row(d), row(256), row(256), row(512), pl.BlockSpec((d, d), lambda i: (0, 0)),
                  pl.BlockSpec((1, d), lambda i: (0, 0)), row(d)],
        out_specs=[row(d), row(d), pl.BlockSpec((8, 128), lambda i: (0, 0))],
        out_shape=[out, out, jax.ShapeDtypeStruct((8, 128), F32)],
        compiler_params=_cparams(("arbitrary",)), name=name)(x2, oa, ob, oc, wo, g_row, target2)


def _outproj_bwd(dxo, y, oa, ob, oc, wo, g_row, name):
    n, d = dxo.shape
    tm = _mix_tm(n)

    def body(dx_ref, y_ref, oa_ref, ob_ref, oc_ref, w_ref, g_ref, dm_ref, dw_ref, dg_ref):
        @pl.when(pl.program_id(0) == 0)
        def _():
            dw_ref[...] = jnp.zeros((d, d), F32)
            dg_ref[...] = jnp.zeros((8, d), F32)

        yv, dx = y_ref[...], dx_ref[...]
        r = _rstd(yv)
        yn = yv * r
        dg_ref[...] += jnp.sum(dx * yn, axis=0, keepdims=True)
        dyn = dx * g_ref[...]
        dy = (r * (dyn - yn * jnp.mean(dyn * yn, axis=-1, keepdims=True))).astype(BF16)
        dm_ref[...] = _dot(dy, w_ref[...], NT)
        dw_ref[0:256, :] += _dot(oa_ref[...].astype(BF16), dy, TN)
        dw_ref[256:512, :] += _dot(ob_ref[...].astype(BF16), dy, TN)
        dw_ref[512:1024, :] += _dot(oc_ref[...].astype(BF16), dy, TN)

    row = lambda w: pl.BlockSpec((tm, w), lambda i: (i, 0))
    fixed = lambda r, c: pl.BlockSpec((r, c), lambda i: (0, 0))
    return pl.pallas_call(
        body, grid=(n // tm,),
        in_specs=[row(d), row(d), row(256), row(256), row(512), fixed(d, d), fixed(1, d)],
        out_specs=[row(d), fixed(d, d), fixed(8, d)],
        out_shape=[jax.ShapeDtypeStruct((n, d), F32), jax.ShapeDtypeStruct((d, d), F32), jax.ShapeDtypeStruct((8, d), F32)],
        compiler_params=_cparams(("arbitrary",)), name=name)(dxo, y, oa, ob, oc, wo, g_row)


_PIECES = ((0, A_W), (A_W, B_W), (A_W + B_W, C_W), (A_W + B_W + C_W, F_W))


def _inproj_bwd_x(x2, dxo, g_row, w_int, pieces, name):
    n, d = x2.shape
    tm = min(512, n)

    def body(x_ref, dxo_ref, g_ref, w_ref, da_ref, db_ref, dc_ref, df_ref, dx_ref, dg_ref):
        @pl.when(pl.program_id(0) == 0)
        def _():
            dg_ref[...] = jnp.zeros((8, d), F32)

        dh = jnp.zeros((tm, d), F32)
        for ref, (o, w) in zip((da_ref, db_ref, dc_ref, df_ref), _PIECES):
            dh = dh + _dot(ref[...].astype(BF16), w_ref[:, o:o + w], NT)
        x = x_ref[...]
        r = _rstd(x)
        xn = x * r
        dg_ref[...] += jnp.sum(dh * xn, axis=0, keepdims=True)
        dxn = dh * g_ref[...]
        dx_ref[...] = dxo_ref[...] + r * (dxn - xn * jnp.mean(dxn * xn, axis=-1, keepdims=True))

    row = lambda w: pl.BlockSpec((tm, w), lambda i: (i, 0))
    fixed = lambda r, c: pl.BlockSpec((r, c), lambda i: (0, 0))
    return pl.pallas_call(
        body, grid=(n // tm,),
        in_specs=[row(d), row(d), fixed(1, d), fixed(d, E_INT)] + [row(w) for _, w in _PIECES],
        out_specs=[row(d), fixed(8, d)],
        out_shape=[jax.ShapeDtypeStruct((n, d), F32), jax.ShapeDtypeStruct((8, d), F32)],
        compiler_params=_cparams(("arbitrary",), vmem_mb=56), name=name)(x2, dxo, g_row, w_int, *pieces)


def _inproj_bwd_w(x2, g_row, pieces, name):
    n, d = x2.shape
    tm = min(512, n)

    def body(x_ref, g_ref, da_ref, db_ref, dc_ref, df_ref, dw_ref):
        @pl.when(pl.program_id(0) == 0)
        def _():
            dw_ref[...] = jnp.zeros((d, E_INT), F32)

        x = x_ref[...]
        h = (x * _rstd(x) * g_ref[...]).astype(BF16)
        for ref, (o, w) in zip((da_ref, db_ref, dc_ref, df_ref), _PIECES):
            dw_ref[:, o:o + w] += _dot(h, ref[...].astype(BF16), TN)

    row = lambda w: pl.BlockSpec((tm, w), lambda i: (i, 0))
    return pl.pallas_call(
        body, grid=(n // tm,),
        in_specs=[row(d), pl.BlockSpec((1, d), lambda i: (0, 0))] + [row(w) for _, w in _PIECES],
        out_specs=pl.BlockSpec((d, E_INT), lambda i: (0, 0)),
        out_shape=jax.ShapeDtypeStruct((d, E_INT), F32),
        compiler_params=_cparams(("arbitrary",), vmem_mb=56), name=name)(x2, g_row, *pieces)


def _block_diag(pool_w_l):
    z = jnp.zeros((64, 64), pool_w_l.dtype)
    return jnp.concatenate(
        [jnp.concatenate([pool_w_l[g] if c == g else z for c in range(4)], axis=1) for g in range(4)], axis=0)


def _pad_lanes(v, width=128):
    return jnp.pad(v, ((0, 0),) * (v.ndim - 1) + ((0, width - v.shape[-1]),))


def _local_step(x, target, lower_bounds, pre_norm_g, w_in_int, hgrn_norm_g, fox_f_bias, pool_w, pool_scale,
                w_out_bf, post_norm_g, on_weight_grads):
    bsz, t, d = x.shape
    n = bsz * t
    lbs = _lbs_fwd(lower_bounds)
    saved = []
    xc = x.reshape(n, d)
    for l in range(DEPTH):
        proj = _inproj_fwd(xc, pre_norm_g[l:l + 1], w_in_int[l], f"inproj_fwd{l}").reshape(bsz, t, E_INT)
        wbd = _block_diag(pool_w[l]).astype(BF16)
        bias_row = _pad_lanes(fox_f_bias[l:l + 1])
        oa, oa_raw, states = _hgrn_fwd(proj, lbs[l:l + 1], hgrn_norm_g[l:l + 1], f"hgrn_fwd{l}")
        ob = _pool_fwd(proj, wbd, pool_scale[l:l + 1], f"pool_fwd{l}")
        c_nat, c_t = _foxgate_fwd(proj, bias_row, f"foxgate_fwd{l}")
        oc, oc_raw, lse = _fox_fwd(proj, c_nat, c_t, f"fox_fwd{l}")
        mixed = (oa.reshape(n, -1), ob.reshape(n, -1), oc.reshape(n, -1))
        if l < DEPTH - 1:
            y, xn = _outproj_fwd(xc, *mixed, w_out_bf[l], post_norm_g[l:l + 1], f"outproj_fwd{l}")
        else:
            y, dx, sq = _outproj_fwd_loss(xc, *mixed, w_out_bf[l], post_norm_g[l:l + 1], target.reshape(n, d),
                                          f"outproj_fwd{l}")
        saved.append((xc, proj, wbd, bias_row, oa, oa_raw, states, ob, oc, oc_raw, lse, c_nat, c_t, y))
        xc = xn
    g = {k: [None] * DEPTH for k in ("pre", "hgn", "bias", "pool_w", "pool_scale", "post", "lbs")}
    handed = [None] * DEPTH
    for l in reversed(range(DEPTH)):
        xin, proj, wbd, bias_row, oa, oa_raw, states, ob, oc, oc_raw, lse, c_nat, c_t, y = saved[l]
        dmix, d_w_out, dpost = _outproj_bwd(dx, y, oa.reshape(n, -1), ob.reshape(n, -1), oc.reshape(n, -1),
                                            w_out_bf[l], post_norm_g[l:l + 1], f"outproj_bwd{l}")
        g["post"][l] = dpost[0]
        dmix3 = dmix.reshape(bsz, t, d)
        d_c, dct, drow = _fox_bwd(proj, oc_raw, dmix3, lse, c_nat, c_t, f"fox_bwd{l}")
        dc_nat = _pad_lanes(dct[:, :, 0:2, :].reshape(bsz, FOX_HEADS, t).transpose(0, 2, 1)
                            + drow.reshape(bsz, t, FOX_HEADS, 64)[..., 0])
        d_f, dbias = _foxgate_bwd(proj, dc_nat, bias_row, f"foxgate_bwd{l}")
        g["bias"][l] = jnp.sum(dbias[:, 0, :FOX_HEADS], axis=0)
        d_b, dscale, dwbd = _pool_bwd(proj, dmix3, wbd, pool_scale[l:l + 1], f"pool_bwd{l}")
        g["pool_scale"][l] = jnp.sum(dscale[:, 0], axis=0)
        dwbd = jnp.sum(dwbd, axis=0)
        g["pool_w"][l] = jnp.stack([dwbd[64 * k:64 * (k + 1), 64 * k:64 * (k + 1)] for k in range(4)])
        d_a, dgn, dlb = _hgrn_bwd(proj, oa_raw, dmix3, states, lbs[l:l + 1], hgrn_norm_g[l:l + 1], f"hgrn_bwd{l}")
        g["hgn"][l] = jnp.sum(dgn[:, 0], axis=0)
        g["lbs"][l] = jnp.sum(dlb[:, 0], axis=0)
        pieces = [p.reshape(n, -1) for p in (d_a, d_b, d_c, d_f)]
        handed[l] = on_weight_grads(l, _inproj_bwd_w(xin, pre_norm_g[l:l + 1], pieces, f"inproj_bwd_w{l}"), d_w_out)
        dx, dpre = _inproj_bwd_x(xin, dx, pre_norm_g[l:l + 1], w_in_int[l], pieces, f"inproj_bwd_x{l}")
        g["pre"][l] = dpre[0]
    grads = {k: jnp.stack(v) for k, v in g.items()}
    return sq, dx.reshape(bsz, t, d), grads, handed


def _place():
    return lax.axis_index("x"), lax.axis_index("y"), lax.axis_index("c")


def _other_chips(x, y):
    return [(1 - x, y), (x, 1 - y), (1 - x, 1 - y)]


_ANY = pl.BlockSpec(memory_space=pl.ANY)


def _gather_body(handshake, n_arrays):
    def body(*refs):
        srcs, dsts = refs[:n_arrays], refs[n_arrays:2 * n_arrays]
        ici_send, ici_recv, d2d_send, d2d_recv, local_sems = refs[2 * n_arrays:]
        x, y, c = _place()
        if handshake:
            barrier = pltpu.get_barrier_semaphore()
            for peer in [(px, py, c) for px, py in _other_chips(x, y)] + [(x, y, 1 - c)]:
                pl.semaphore_signal(barrier, inc=1, device_id=peer, device_id_type=MESH)
            pl.semaphore_wait(barrier, 4)
        me = 2 * x + y
        pairs = list(zip(srcs, dsts))
        order = [(k, j) for k in range(3) for j in range(n_arrays)]
        mine = [pltpu.make_async_copy(src, dst.at[me], local_sems.at[j]) for j, (src, dst) in enumerate(pairs)]
        for cp in mine:
            cp.start()
        chips = _other_chips(x, y)
        sends = [pltpu.make_async_remote_copy(
            src_ref=pairs[j][0].at[c], dst_ref=pairs[j][1].at[me, c], send_sem=ici_send.at[n], recv_sem=ici_recv.at[n],
            device_id=(chips[k][0], chips[k][1], c), device_id_type=MESH) for n, (k, j) in enumerate(order)]
        for cp in sends:
            cp.start()
        passed = [pltpu.make_async_remote_copy(
            src_ref=pairs[j][1].at[2 * chips[k][0] + chips[k][1], c], dst_ref=pairs[j][1].at[2 * chips[k][0] + chips[k][1], c],
            send_sem=d2d_send.at[n], recv_sem=d2d_recv.at[n], device_id=(x, y, 1 - c), device_id_type=MESH)
            for n, (k, j) in enumerate(order)]
        for n, (k, j) in enumerate(order):
            px, py = chips[k]
            src, dst = pairs[j]
            pltpu.make_async_remote_copy(
                src_ref=src.at[c], dst_ref=dst.at[2 * px + py, c], send_sem=ici_send.at[n], recv_sem=ici_recv.at[n],
                device_id=(px, py, c), device_id_type=MESH).wait_recv()
            passed[n].start()
        for n, (k, j) in enumerate(order):
            px, py = chips[k]
            src, dst = pairs[j]
            pltpu.make_async_remote_copy(
                src_ref=dst.at[2 * px + py, 1 - c], dst_ref=dst.at[2 * px + py, 1 - c], send_sem=d2d_send.at[n],
                recv_sem=d2d_recv.at[n], device_id=(x, y, 1 - c), device_id_type=MESH).wait_recv()
        for cp in sends + passed:
            cp.wait_send()
        for cp in mine:
            cp.wait()

    return body


def _gather_sems(n_arrays):
    return [pltpu.SemaphoreType.DMA((3 * n_arrays,))] * 4 + [pltpu.SemaphoreType.DMA((n_arrays,))]


def _gathered(a):
    return jax.ShapeDtypeStruct((N_CHIPS,) + a.shape, a.dtype)


def _gather_weights(arrays):
    n = len(arrays)
    return pl.pallas_call(
        _gather_body(False, n), in_specs=[_ANY] * n, out_specs=[_ANY] * n, out_shape=[_gathered(a) for a in arrays],
        scratch_shapes=_gather_sems(n), name="gather_weights")(*arrays)


def _gather_weights_beside(arrays):
    hbm = pltpu.MemorySpace.HBM
    n = len(arrays)
    srcs = [jax.new_ref(a, memory_space=hbm) for a in arrays]
    dsts = [jax.empty_ref(_gathered(a), memory_space=hbm) for a in arrays]
    body = _gather_body(True, n)

    @pl.kernel(mesh=plsc.ScalarSubcoreMesh(axis_name="sequencer", num_cores=1), name="gather_weights_beside",
               scratch_types=_gather_sems(n), compiler_params=pltpu.CompilerParams(collective_id=1))
    def launch(*sems):
        body(*srcs, *dsts, *sems)

    launch()
    return [d[...] for d in dsts]


def _swap_with_sibling(parts, name):
    k = len(parts)

    def body(*refs):
        src, dst = refs[:k], refs[k:2 * k]
        send_sems, recv_sems = refs[2 * k:]
        x, y, c = _place()
        cps = [pltpu.make_async_remote_copy(src_ref=src[j], dst_ref=dst[j], send_sem=send_sems.at[j], recv_sem=recv_sems.at[j],
                                            device_id=(x, y, 1 - c), device_id_type=MESH) for j in range(k)]
        for cp in cps:
            cp.start()
        for cp in cps:
            cp.wait()

    return pl.pallas_call(
        body, in_specs=[_ANY] * k, out_specs=[_ANY] * k,
        out_shape=[jax.ShapeDtypeStruct(p.shape, p.dtype) for p in parts],
        scratch_shapes=[pltpu.SemaphoreType.DMA((k,)), pltpu.SemaphoreType.DMA((k,))], name=name)(*parts)


N_PEERS = 7


def _grad_exchange_body():
    def body(pin_ref, pout_ref, lin_ref, lout_ref, send_sems, recv_sems):
        x, y, c = _place()
        barrier = pltpu.get_barrier_semaphore()
        for k in range(1, N_PEERS + 1):
            peer = (x ^ ((k >> 2) & 1), y ^ ((k >> 1) & 1), c ^ (k & 1))
            pl.semaphore_signal(barrier, inc=1, device_id=peer, device_id_type=MESH)
        pl.semaphore_wait(barrier, N_PEERS)
        me = 2 * x + y
        pairs = ((pin_ref, lin_ref), (pout_ref, lout_ref))
        cps = []
        for k, (px, py) in enumerate(_other_chips(x, y)):
            for r in range(2):
                for j, (src, dst) in enumerate(pairs):
                    cps.append(pltpu.make_async_remote_copy(
                        src_ref=src.at[2 * px + py, r], dst_ref=dst.at[2 * k + c], send_sem=send_sems.at[2 * (2 * k + r) + j],
                        recv_sem=recv_sems.at[2 * (2 * k + c) + j], device_id=(px, py, r), device_id_type=MESH))
        for j, (src, dst) in enumerate(pairs):
            cps.append(pltpu.make_async_remote_copy(
                src_ref=src.at[me, 1 - c], dst_ref=dst.at[N_PEERS - 1], send_sem=send_sems.at[2 * (N_PEERS - 1) + j],
                recv_sem=recv_sems.at[2 * (N_PEERS - 1) + j], device_id=(x, y, 1 - c), device_id_type=MESH))
        for cp in cps:
            cp.start()
        for s in range(N_PEERS):
            for j, (src, dst) in enumerate(pairs):
                pltpu.make_async_remote_copy(
                    src_ref=src.at[0, 0], dst_ref=dst.at[s], send_sem=send_sems.at[2 * s + j], recv_sem=recv_sems.at[2 * s + j],
                    device_id=(x, y, 1 - c), device_id_type=MESH).wait_recv()
        for cp in cps:
            cp.wait_send()

    return body


_EXCHANGE_SEMS = [pltpu.SemaphoreType.DMA((2 * N_PEERS,))] * 2


def _landing(p):
    return jax.ShapeDtypeStruct((N_PEERS,) + p.shape[2:], p.dtype)


def _grad_exchange_beside(pin, pout, name, collective_id):
    hbm = pltpu.MemorySpace.HBM
    pin_ref, pout_ref = jax.new_ref(pin, memory_space=hbm), jax.new_ref(pout, memory_space=hbm)
    lin_ref, lout_ref = jax.empty_ref(_landing(pin), memory_space=hbm), jax.empty_ref(_landing(pout), memory_space=hbm)
    body = _grad_exchange_body()

    @pl.kernel(mesh=plsc.ScalarSubcoreMesh(axis_name="sequencer", num_cores=1), name=name,
               scratch_types=_EXCHANGE_SEMS, compiler_params=pltpu.CompilerParams(collective_id=collective_id))
    def launch(send_sems, recv_sems):
        body(pin_ref, pout_ref, lin_ref, lout_ref, send_sems, recv_sems)

    launch()
    return lin_ref[...], lout_ref[...]


def _add_n(parts, name, with_bf16=False):
    r, c = parts[0].shape
    tr = 256 if r % 256 == 0 else r
    n = len(parts)

    def body(*refs):
        acc = refs[0][...].astype(F32)
        for ref in refs[1:n]:
            acc = acc + ref[...].astype(F32)
        refs[n][...] = acc
        if with_bf16:
            refs[n + 1][...] = acc.astype(BF16)

    blk = pl.BlockSpec((tr, c), lambda i: (i, 0))
    outs = [jax.ShapeDtypeStruct((r, c), F32)] + ([jax.ShapeDtypeStruct((r, c), BF16)] if with_bf16 else [])
    res = pl.pallas_call(
        body, grid=(r // tr,), in_specs=[blk] * n, out_specs=[blk] * len(outs),
        out_shape=outs, compiler_params=_cparams(("parallel",)), name=name)(*parts)
    return res if with_bf16 else res[0]


def _all_reduce_small(packet):
    r, w = packet.shape

    def body(p_ref, o_ref, buf, send_sems, recv_sems):
        x, y, c = _place()
        me = 4 * x + 2 * y + c
        buf[me] = p_ref[...]
        peers = []
        for k in range(1, 8):
            fx, fy, fc = (k >> 2) & 1, (k >> 1) & 1, k & 1
            peers.append((x ^ fx, y ^ fy, c ^ fc))
        cps = [pltpu.make_async_remote_copy(src_ref=p_ref, dst_ref=buf.at[me], send_sem=send_sems.at[k], recv_sem=recv_sems.at[k],
                                            device_id=peer, device_id_type=MESH) for k, peer in enumerate(peers)]
        for cp in cps:
            cp.start()
        for k, (px, py, pc) in enumerate(peers):
            pltpu.make_async_remote_copy(src_ref=p_ref, dst_ref=buf.at[4 * px + 2 * py + pc], send_sem=send_sems.at[k],
                                         recv_sem=recv_sems.at[k], device_id=(px, py, pc), device_id_type=MESH).wait_recv()
        for cp in cps:
            cp.wait_send()
        acc = buf[0]
        for k in range(1, 8):
            acc = acc + buf[k]
        o_ref[...] = acc

    vm = pl.BlockSpec(memory_space=pltpu.VMEM)
    return pl.pallas_call(
        body, in_specs=[vm], out_specs=vm, out_shape=jax.ShapeDtypeStruct((r, w), F32),
        scratch_shapes=[pltpu.VMEM((8, r, w), F32), pltpu.SemaphoreType.DMA((7,)), pltpu.SemaphoreType.DMA((7,))],
        name="all_reduce_small")(packet)


def _adamw_math(w, g, m, v):
    m = ADAM_B1 * m + (1.0 - ADAM_B1) * g
    v = ADAM_B2 * v + (1.0 - ADAM_B2) * (g * g)
    m_hat = m / (1.0 - ADAM_B1 ** ADAM_STEP)
    v_hat = v / (1.0 - ADAM_B2 ** ADAM_STEP)
    return -ADAM_LR * (m_hat / (jnp.sqrt(v_hat) + ADAM_EPS) + ADAM_WD * w), m, v


def _adamw(w, g_lower, g_upper, m, v, name):
    nl, r, c = w.shape
    tr = 128
    per_half = r // (2 * tr)

    def body(w_ref, lo_ref, up_ref, m_ref, v_ref, g_ref, d_ref, mo_ref, vo_ref):
        g = jnp.where(pl.program_id(1) == 0, lo_ref[...], up_ref[...])
        g_ref[...] = g
        d_ref[...], mo_ref[...], vo_ref[...] = _adamw_math(w_ref[...], g, m_ref[...], v_ref[...])

    blk = pl.BlockSpec((None, tr, c), lambda l, h, i: (l, h * per_half + i, 0))
    half = pl.BlockSpec((None, tr, c), lambda l, h, i: (l, i, 0))
    out = jax.ShapeDtypeStruct(w.shape, F32)
    return pl.pallas_call(
        body, grid=(nl, 2, per_half), in_specs=[blk, half, half, blk, blk], out_specs=[blk] * 4, out_shape=[out] * 4,
        compiler_params=_cparams(("parallel", "parallel", "parallel")), name=name)(w, g_lower, g_upper, m, v)


def _small_update(gsum, lower_bounds, wpack, mpack, vpack):
    r, w = gsum.shape
    lb_rows = DEPTH * HGRN_W // 128

    def body(g_ref, a_ref, w_ref, m_ref, v_ref, go_ref, d_ref, mo_ref, vo_ref):
        a = a_ref[...]
        a0, a1 = a[0:1], a[1:2]
        mx = jnp.maximum(a0, a1)
        e0, e1 = jnp.exp(a0 - mx), jnp.exp(a1 - mx)
        p0, p1 = e0 / (e0 + e1), e1 / (e0 + e1)
        g = g_ref[...]
        half = lb_rows // 2
        dl0 = jnp.concatenate([g[k:k + 1] for k in range(half)], axis=1)
        dl1 = jnp.concatenate([g[half + k:half + k + 1] for k in range(half)], axis=1)
        dp0 = (dl0 + dl1) - (dl0 + dl1)
        dp1 = dl1
        inner = p0 * dp0 + p1 * dp1
        da0, da1 = p0 * (dp0 - inner), p1 * (dp1 - inner)
        rows = [da0[:, 128 * k:128 * (k + 1)] for k in range(half)] + [da1[:, 128 * k:128 * (k + 1)] for k in range(half)]
        gfull = jnp.concatenate(rows + [g[lb_rows:]], axis=0)
        go_ref[...] = gfull
        d_ref[...], mo_ref[...], vo_ref[...] = _adamw_math(w_ref[...], gfull, m_ref[...], v_ref[...])

    vm = pl.BlockSpec(memory_space=pltpu.VMEM)
    out = jax.ShapeDtypeStruct((r, w), F32)
    return pl.pallas_call(body, in_specs=[vm] * 5, out_specs=[vm] * 4, out_shape=[out] * 4, name="small_update")(
        gsum, lower_bounds, wpack, mpack, vpack)


_SMALL = ("lower_bounds", "pre_norm_g", "hgrn_norm_g", "fox_f_bias", "pool_w", "pool_scale", "post_norm_g")


def _pack(parts):
    rows = []
    for k in _SMALL:
        f = parts[k].reshape(-1)
        pad = (-f.shape[0]) % (8 * 128)
        rows.append(jnp.pad(f, (0, pad)).reshape(-1, 128))
    rows.append(jnp.zeros((8, 128), F32))
    return jnp.concatenate(rows, axis=0)


def _unpack(pack, like):
    out, r = {}, 0
    for k in _SMALL:
        size = int(np.prod(like[k].shape))
        nr = -(-size // (8 * 128)) * 8
        out[k] = pack[r:r + nr].reshape(-1)[:size].reshape(like[k].shape)
        r += nr
    return out, r


def kernel(x, lower_bounds, pre_norm_g, w_in, hgrn_norm_g, fox_f_bias, pool_w, pool_scale, w_out, post_norm_g, loss_target, m_lower_bounds, m_pre_norm_g, m_w_in, m_hgrn_norm_g, m_fox_f_bias, m_pool_w, m_pool_scale, m_w_out, m_post_norm_g, v_lower_bounds, v_pre_norm_g, v_w_in, v_hgrn_norm_g, v_fox_f_bias, v_pool_w, v_pool_scale, v_w_out, v_post_norm_g):
    cx, cy, cc = _place()
    chip = 2 * cx + cy

    halves = lambda w, l: w[l].reshape(2, w.shape[1] // 2, w.shape[2]).astype(BF16)
    needed_first = _gather_weights([halves(w_in, 0)])
    needed_first, later = lax.optimization_barrier((needed_first, [halves(w_out, 0), halves(w_in, 1), halves(w_out, 1)]))
    later = _gather_weights_beside(later)
    w_in_int = [_internal_from_shards([a[q].reshape(D_MODEL, SHARD_W) for q in range(N_CHIPS)]) for a in (needed_first[0], later[1])]
    w_out_full = [a.reshape(D_MODEL, D_MODEL) for a in (later[0], later[2])]

    def on_weight_grads(l, d_w_in, d_w_out):
        pin = _shards_from_internal(d_w_in).reshape(N_CHIPS, 2, D_MODEL // 2, SHARD_W)
        pout = d_w_out.reshape(N_CHIPS, 2, D_MODEL // (2 * N_CHIPS), D_MODEL)
        own = [lax.dynamic_index_in_dim(lax.dynamic_index_in_dim(p, chip, 0, False), cc, 0, False) for p in (pin, pout)]
        return own, _grad_exchange_beside(pin.astype(BF16), pout.astype(BF16), f"grad_exchange{l}", 2 + l)

    sq, grad_x, g, handed = _local_step(x, loss_target, lower_bounds, pre_norm_g, w_in_int, hgrn_norm_g, fox_f_bias,
                                        pool_w, pool_scale, w_out_full, post_norm_g, on_weight_grads)
    first = cc == 0

    def finish(l, own, landed):
        mine = [_add_n([o] + [t[s] for s in range(N_PEERS)], f"grad_sum{l}_{j}") for j, (o, t) in enumerate(zip(own, landed))]
        theirs = _swap_with_sibling(mine, f"grad_swap{l}")
        return [(jnp.where(first, h, o), jnp.where(first, o, h)) for h, o in zip(mine, theirs)]

    grad_x, last = lax.optimization_barrier((grad_x, handed[1]))
    done = [None, finish(1, *last)]

    small = {"lower_bounds": g["lbs"], "pre_norm_g": g["pre"], "hgrn_norm_g": g["hgn"], "fox_f_bias": g["bias"],
             "pool_w": g["pool_w"], "pool_scale": g["pool_scale"], "post_norm_g": g["post"]}
    packet = _pack(small)
    nrows = packet.shape[0]
    packet = packet.at[nrows - 1].set(sq[0])
    gsum = _all_reduce_small(packet)
    loss = gsum[nrows - 1, 0] * (0.5 / D_MODEL)

    weights = {"lower_bounds": lower_bounds, "pre_norm_g": pre_norm_g, "hgrn_norm_g": hgrn_norm_g,
               "fox_f_bias": fox_f_bias, "pool_w": pool_w, "pool_scale": pool_scale, "post_norm_g": post_norm_g}
    moments_m = {"lower_bounds": m_lower_bounds, "pre_norm_g": m_pre_norm_g, "hgrn_norm_g": m_hgrn_norm_g,
                 "fox_f_bias": m_fox_f_bias, "pool_w": m_pool_w, "pool_scale": m_pool_scale, "post_norm_g": m_post_norm_g}
    moments_v = {"lower_bounds": v_lower_bounds, "pre_norm_g": v_pre_norm_g, "hgrn_norm_g": v_hgrn_norm_g,
                 "fox_f_bias": v_fox_f_bias, "pool_w": v_pool_w, "pool_scale": v_pool_scale, "post_norm_g": v_post_norm_g}
    gp, dp, mp, vp = _small_update(gsum, lower_bounds, _pack(weights), _pack(moments_m), _pack(moments_v))
    gs, _ = _unpack(gp, weights)
    ds, _ = _unpack(dp, weights)
    ms, _ = _unpack(mp, weights)
    vs, _ = _unpack(vp, weights)

    first_layer, _ = lax.optimization_barrier((handed[0], (done[1], gp, dp, mp, vp)))
    done[0] = finish(0, *first_layer)
    halves_of = lambda j, side: jnp.stack([done[l][j][side] for l in range(DEPTH)])
    grad_w_in, d_in, m_in, v_in = _adamw(w_in, halves_of(0, 0), halves_of(0, 1), m_w_in, v_w_in, "adamw_w_in")
    grad_w_out, d_out, m_out, v_out = _adamw(w_out, halves_of(1, 0), halves_of(1, 1), m_w_out, v_w_out, "adamw_w_out")

    def ordered(s, big_in, big_out):
        return (s["lower_bounds"], s["pre_norm_g"], big_in, s["hgrn_norm_g"], s["fox_f_bias"], s["pool_w"],
                s["pool_scale"], big_out, s["post_norm_g"])

    return (loss, grad_x, *ordered(gs, grad_w_in, grad_w_out), *ordered(ds, d_in, d_out),
            *ordered(ms, m_in, m_out), *ordered(vs, v_in, v_out))
```

```python
import functools

import numpy as np
import jax
import jax.numpy as jnp
from jax import lax
from jax.experimental import pallas as pl
from jax.experimental.pallas import tpu as pltpu
from jax.experimental.pallas import tpu_sc as plsc

F32 = jnp.float32
BF16 = jnp.bfloat16
HI = lax.Precision.HIGHEST
MESH = pl.DeviceIdType.MESH

NORM_EPS = 1e-6
MASK_VALUE = -1e30
TINY = 1e-30
ADAM_LR, ADAM_B1, ADAM_B2, ADAM_EPS, ADAM_WD, ADAM_STEP = 0.001, 0.9, 0.999, 1e-08, 0.01, 10

D_MODEL = 1024
DEPTH = 2
N_CHIPS = 4
CHUNK = 64
LANES = 128
HGRN_W, POOL_W, FOX_W, FOX_HEADS = 256, 256, 512, 8
POOL_WINDOWS = (2, 4, 8, 16)
POOL_HALO = 16
IN_WIDTH = 3592
SHARD_W = IN_WIDTH // N_CHIPS
A_W, B_W, C_W, F_W = 1024, 512, 2048, 128
E_INT = A_W + B_W + C_W + F_W
B_BLK = A_W // 512
C_BLK0 = (A_W + B_W) // 512
F_BLK = (A_W + B_W + C_W) // 128


def _segments():
    segs = []
    for hp in range(2):
        for part in range(4):
            segs.append((part * 256 + hp * 128, 128))
    segs.append((1024, 256))
    segs.append((1280, 256))
    for hp in range(4):
        for part in range(4):
            segs.append((1536 + part * 512 + hp * 128, 128))
    segs.append((3584, 8))
    return segs


_SEGS = _segments()


def _internal_from_shards(shards):
    parts = []
    for s, n in _SEGS:
        while n > 0:
            q, r = divmod(s, SHARD_W)
            take = min(n, SHARD_W - r)
            parts.append(shards[q][..., r:r + take])
            s, n = s + take, n - take
    parts.append(jnp.zeros(shards[0].shape[:-1] + (E_INT - IN_WIDTH,), shards[0].dtype))
    return jnp.concatenate(parts, axis=-1)


def _shards_from_internal(w):
    offs, o = [], 0
    for s, n in _SEGS:
        offs.append((s, o, n))
        o += n
    blocks = []
    for q in range(N_CHIPS):
        lo, hi = SHARD_W * q, SHARD_W * (q + 1)
        parts = [w[..., o + max(lo, s) - s:o + min(hi, s + n) - s] for s, o, n in sorted(offs) if s < hi and s + n > lo]
        blocks.append(jnp.concatenate(parts, axis=-1))
    return jnp.stack(blocks)


def _cparams(sem=None, vmem_mb=48):
    kw = dict(vmem_limit_bytes=vmem_mb * 1024 * 1024)
    if sem is not None:
        kw["dimension_semantics"] = sem
    return pltpu.CompilerParams(**kw)


def _sig(x):
    return 1.0 / (1.0 + jnp.exp(-x))


def _silu(x):
    return x * _sig(x)


def _dsilu(x):
    s = _sig(x)
    return s * (1.0 + x * (1.0 - s))


def _rstd(x):
    return lax.rsqrt(jnp.mean(x * x, axis=-1, keepdims=True) + NORM_EPS)


def _dot(a, b, dims, **kw):
    return lax.dot_general(a, b, (dims, ((), ())), preferred_element_type=F32, **kw)


NN = ((1,), (0,))
NT = ((1,), (1,))
TN = ((0,), (0,))


def _iota(shape, dim):
    return lax.broadcasted_iota(jnp.int32, shape, dim)


def _lbs_fwd(lower_bounds):
    def body(a_ref, o_ref):
        a = a_ref[...]
        a0, a1 = a[0:1], a[1:2]
        m = jnp.maximum(a0, a1)
        e0, e1 = jnp.exp(a0 - m), jnp.exp(a1 - m)
        p0, p1 = e0 / (e0 + e1), e1 / (e0 + e1)
        o_ref[...] = jnp.concatenate([p0 - p0, (p0 + p1) - p0], axis=0)

    return pl.pallas_call(body, out_shape=jax.ShapeDtypeStruct(lower_bounds.shape, F32), name="lbs_fwd")(lower_bounds)


def _inproj_fwd(x2, g_row, w_int, name):
    n, d = x2.shape
    e = w_int.shape[1]
    tm = min(512, n)

    def body(x_ref, g_ref, w_ref, o_ref):
        x = x_ref[...]
        h = (x * _rstd(x) * g_ref[...]).astype(BF16)
        o_ref[...] = jnp.dot(h, w_ref[...], preferred_element_type=F32)

    return pl.pallas_call(
        body, grid=(n // tm,),
        in_specs=[pl.BlockSpec((tm, d), lambda i: (i, 0)), pl.BlockSpec((1, d), lambda i: (0, 0)),
                  pl.BlockSpec((d, e), lambda i: (0, 0))],
        out_specs=pl.BlockSpec((tm, e), lambda i: (i, 0)),
        out_shape=jax.ShapeDtypeStruct((n, e), F32),
        compiler_params=_cparams(("parallel",)), name=name)(x2, g_row, w_int)


def _hgrn_gates(a, lb):
    qa, z = a[:, 0:128], a[:, 128:256]
    sg, sgn = _sig(z), _sig(-z)
    fg = lb + (1.0 - lb) * sg
    lf = jnp.log(jnp.maximum(fg, TINY))
    kk = (1.0 - lb) * sgn
    return qa * _sig(qa), kk, lf, sg, sgn, fg


N_LEVELS = 6


def _hgrn_tables():
    t = np.arange(LANES)
    j = np.arange(LANES)[None, :]
    same_chunk = (t[:, None] // CHUNK) == (j // CHUNK)
    w = np.zeros((2 + N_LEVELS, LANES, LANES), np.float32)
    w[0] = same_chunk & (j <= t[:, None])
    w[1] = same_chunk & (j > t[:, None])
    maskf = np.zeros((N_LEVELS, LANES, LANES), np.float32)
    rightf = np.zeros((N_LEVELS, LANES, LANES), np.float32)
    for li in range(N_LEVELS):
        m = (CHUNK // 2) >> li
        start = t - (t % (2 * m))
        right = (t % (2 * m)) >= m
        first = np.where(right, start + m, t + 1)
        last = np.where(right, t, start + m - 1)
        w[2 + li] = (j >= first[:, None]) & (j <= last[:, None])
        maskf[li] = (t[:, None] // (2 * m)) == (j // (2 * m))
        rightf[li] = right[:, None]
    w = w[:-1]
    return jnp.asarray(w.reshape(-1, LANES), BF16), jnp.asarray(np.tile(maskf, (1, 2, 1))), jnp.asarray(rightf)


def _split(x, n):
    parts = []
    for _ in range(n - 1):
        p = x.astype(BF16)
        parts.append(p)
        x = x - p.astype(F32)
    parts.append(x.astype(BF16))
    return parts


def _exact_dot(w, parts):
    acc = jnp.dot(w, parts[0], preferred_element_type=F32)
    for p in parts[1:]:
        acc = acc + jnp.dot(w, p, preferred_element_type=F32)
    return acc


def _head_sums(v, ones_blk, n=2):
    parts = _split(v, n)
    acc = jnp.dot(parts[0], ones_blk, preferred_element_type=F32)
    for p in parts[1:]:
        acc = acc + jnp.dot(p, ones_blk, preferred_element_type=F32)
    return acc


def _hgrn_consts():
    r, c = _iota((LANES, LANES), 0), _iota((LANES, LANES), 1)
    ones_blk = ((r // CHUNK) == (c // CHUNK)).astype(BF16)
    eye2 = (_iota((2 * LANES, LANES), 0) % LANES) == _iota((2 * LANES, LANES), 1)
    first = _iota((1, LANES), 1) < CHUNK
    return eye2, ones_blk, jnp.ones((LANES, LANES), BF16), first


def _stack_heads(v, first):
    return jnp.concatenate([jnp.where(first, v, 0.0), jnp.where(first, 0.0, v)], axis=0)


def _pick_heads(v2, first):
    return jnp.where(first, v2[:LANES], v2[LANES:])


def _hgrn_levels(qq, kk, lf, zall, mk_ref, rt_ref, first, d_att=None):
    att = jnp.zeros((2 * LANES, LANES), F32)
    dq = dk = db = jnp.zeros((LANES, LANES), F32)
    for li in range(N_LEVELS):
        rt = rt_ref[li]
        e = jnp.exp(zall[(2 + li) * LANES:(3 + li) * LANES] if li < N_LEVELS - 1 else lf * rt)
        mk = mk_ref[li]
        qef, kef = e * rt, e * (1.0 - rt)
        qe, ke = (qq * qef).astype(BF16), (kk * kef).astype(BF16)
        qe2 = _stack_heads(qe, first)
        att = att + _dot(qe2, ke, NT) * mk
        if d_att is not None:
            dam = (d_att * mk).astype(BF16)
            dqe = _pick_heads(jnp.dot(dam, ke, preferred_element_type=F32), first)
            dke = _dot(dam, qe2, TN)
            dq = dq + dqe * qef
            dk = dk + dke * kef
            db = db + (dqe * qe.astype(F32) - dke * ke.astype(F32))
    return att, dq, dk, db


def _hgrn_fwd(proj3, lbs_row, gn_row, name):
    bsz, t, _ = proj3.shape
    nt = t // LANES
    w_all, maskf, rightf = _hgrn_tables()

    def body(a_ref, lb_ref, gn_ref, w_ref, mk_ref, rt_ref, og_ref, or_ref, st_ref):
        lb = lb_ref[...]
        gn = gn_ref[...]
        eye2, ones_blk, ones_all, first = _hgrn_consts()

        def tile(i, carry):
            r0 = pl.multiple_of(i * LANES, LANES)
            a = a_ref[pl.ds(r0, LANES), :]
            qq, kk, lf, _, _, _ = _hgrn_gates(a, lb)
            va, ga = a[:, 256:384], a[:, 384:512]
            parts = _split(lf, 3)
            zall = _exact_dot(w_ref[...], parts)
            eb, ee = jnp.exp(zall[0:LANES]), jnp.exp(zall[LANES:2 * LANES])
            vb = va.astype(BF16)
            att, _, _, _ = _hgrn_levels(qq, kk, lf, zall, mk_ref, rt_ref, first)
            diag = _head_sums(_stack_heads(qq * kk, first), ones_all)
            a2 = (att + jnp.where(eye2, diag, 0.0)).astype(BF16)
            o_in = _pick_heads(jnp.dot(a2, vb, preferred_element_type=F32), first)
            qeb, keb = (qq * eb).astype(BF16), (kk * ee).astype(BF16)
            new_s, o_heads = [], []
            for h in range(2):
                hs = slice(CHUNK * h, CHUNK * (h + 1))
                o_h = o_in[:, hs]
                st = carry[h]
                chunks = []
                for c in range(2):
                    rc = slice(CHUNK * c, CHUNK * (c + 1))
                    st_ref[h, 2 * i + c] = st
                    chunks.append(o_h[rc] + _dot(qeb[rc, hs], st.astype(BF16), NT))
                    ebl = eb[CHUNK * (c + 1) - 1:CHUNK * (c + 1), hs]
                    st = st * ebl + _dot(vb[rc, hs], keb[rc, hs], TN)
                new_s.append(st)
                o_heads.append(jnp.concatenate(chunks, axis=0))
            o = jnp.concatenate(o_heads, axis=1)
            ms = _head_sums(o * o, ones_blk) * (1.0 / CHUNK)
            or_ref[pl.ds(r0, LANES), :] = o
            og_ref[pl.ds(r0, LANES), :] = (o * lax.rsqrt(ms + NORM_EPS) * gn * _silu(ga)).astype(BF16)
            return tuple(new_s)

        zero = jnp.zeros((CHUNK, CHUNK), F32)
        per_step = 4 if nt % 4 == 0 else 2

        def step(i, carry):
            for k in range(per_step):
                carry = tile(per_step * i + k, carry)
            return carry

        lax.fori_loop(0, nt // per_step, step, (zero, zero))

    out = jax.ShapeDtypeStruct((bsz, t, HGRN_W), F32)
    row = pl.BlockSpec((1, 128), lambda b, p: (0, p))
    return pl.pallas_call(
        body, grid=(bsz, 2),
        in_specs=[pl.BlockSpec((None, t, 512), lambda b, p: (b, 0, p)), row, row,
                  pl.BlockSpec(w_all.shape, lambda b, p: (0, 0)),
                  pl.BlockSpec(maskf.shape, lambda b, p: (0, 0, 0)),
                  pl.BlockSpec(rightf.shape, lambda b, p: (0, 0, 0))],
        out_specs=[pl.BlockSpec((None, t, 128), lambda b, p: (b, 0, p)),
                   pl.BlockSpec((None, t, 128), lambda b, p: (b, 0, p)),
                   pl.BlockSpec((None, 2, t // CHUNK, CHUNK, CHUNK), lambda b, p: (b, p, 0, 0, 0))],
        out_shape=[jax.ShapeDtypeStruct((bsz, t, HGRN_W), BF16), out,
                   jax.ShapeDtypeStruct((bsz, 4, t // CHUNK, CHUNK, CHUNK), F32)],
        compiler_params=_cparams(("parallel", "parallel")), name=name)(proj3, lbs_row, gn_row, w_all, maskf, rightf)


def _hgrn_bwd(proj3, o_raw, dmixed, states, lbs_row, gn_row, name):
    bsz, t, _ = proj3.shape
    nt = t // LANES
    nchunk = t // CHUNK
    w_all, maskf, rightf = _hgrn_tables()

    def body(a_ref, or_ref, do_ref, s_sc, lb_ref, gn_ref, w_ref, mk_ref, rt_ref, da_ref, dgn_ref, dlb_ref):
        lb = lb_ref[...]
        gn = gn_ref[...]
        eye2, ones_blk, ones_all, first = _hgrn_consts()
        r_i, c_i = _iota((LANES, LANES), 0), _iota((LANES, LANES), 1)
        suffix = ((c_i >= r_i) & ((r_i // CHUNK) == (c_i // CHUNK))).astype(BF16)
        row64 = _iota((LANES, CHUNK), 0)
        zero = jnp.zeros((CHUNK, CHUNK), F32)

        def bwd_tile(k, carry):
            dst0, dst1, dgn_acc, dlb_acc = carry
            i = nt - 1 - k
            r0 = pl.multiple_of(i * LANES, LANES)
            a = a_ref[pl.ds(r0, LANES), :]
            qa, ga = a[:, 0:128], a[:, 384:512]
            qq, kk, lf, sg, sgn, fg = _hgrn_gates(a, lb)
            parts = _split(lf, 3)
            zall = _exact_dot(w_ref[...], parts)
            eb, ee = jnp.exp(zall[0:LANES]), jnp.exp(zall[LANES:2 * LANES])
            vb = a[:, 256:384].astype(BF16)
            oraw = or_ref[pl.ds(r0, LANES), :]
            dout = do_ref[pl.ds(r0, LANES), :]
            r = lax.rsqrt(_head_sums(oraw * oraw, ones_blk) * (1.0 / CHUNK) + NORM_EPS)
            xn = oraw * r
            dga = dout * (xn * gn) * _dsilu(ga)
            don = dout * _silu(ga)
            dgn_acc = dgn_acc + jnp.sum(don * xn, axis=0, keepdims=True)
            dxn = don * gn
            do = r * (dxn - xn * (_head_sums(dxn * xn, ones_blk) * (1.0 / CHUNK)))
            dob = do.astype(BF16)
            do2 = _stack_heads(dob, first)
            d_att = _dot(do2, vb, NT)
            att, dq, dk, db_lv = _hgrn_levels(qq, kk, lf, zall, mk_ref, rt_ref, first, d_att)
            a2 = att + jnp.where(eye2, _head_sums(_stack_heads(qq * kk, first), ones_all), 0.0)
            dv_in = _dot(a2.astype(BF16), do2, TN)
            ddiag = _pick_heads(_head_sums(jnp.where(eye2, d_att, 0.0), ones_all), first)
            dq_in, dk_in = dq + ddiag * kk, dk + ddiag * qq
            qe_f, ke_f = qq * eb, kk * ee
            qeb, keb = qe_f.astype(BF16), ke_f.astype(BF16)
            new_ds, dq_h, dk_h, dv_h, dbl_h = [], [], [], [], []
            for h in range(2):
                hs = slice(CHUNK * h, CHUNK * (h + 1))
                dv, dq_i, dk_i = dv_in[:, hs], dq_in[:, hs], dk_in[:, hs]
                dst = (dst0, dst1)[h]
                dq_c, dk_c, dv_c, dbl_c = [None, None], [None, None], [None, None], [None, None]
                for c in (1, 0):
                    rc = slice(CHUNK * c, CHUNK * (c + 1))
                    st_n = s_sc[h, 2 * i + c]
                    ebl = eb[CHUNK * (c + 1) - 1:CHUNK * (c + 1), hs]
                    dstb = dst.astype(BF16)
                    dv_c[c] = _dot(keb[rc, hs], dstb, NT)
                    dke = jnp.dot(vb[rc, hs], dstb, preferred_element_type=F32)
                    dqe = jnp.dot(dob[rc, hs], st_n.astype(BF16), preferred_element_type=F32)
                    dbl_c[c] = (jnp.sum(dst * st_n, axis=0, keepdims=True) * ebl
                                + jnp.sum(dke * ke_f[rc, hs], axis=0, keepdims=True))
                    dq_c[c], dk_c[c] = dqe * eb[rc, hs], dke * ee[rc, hs]
                    dst = dst * ebl + _dot(dob[rc, hs], qeb[rc, hs], TN)
                new_ds.append(dst)
                dq_x, dk_x = jnp.concatenate(dq_c, axis=0), jnp.concatenate(dk_c, axis=0)
                dq_h.append(dq_i + dq_x)
                dk_h.append(dk_i + dk_x)
                dv_h.append(dv + jnp.concatenate(dv_c, axis=0))
                dbl_h.append(qq[:, hs] * dq_x - kk[:, hs] * dk_x
                             + jnp.where(row64 == CHUNK - 1, dbl_c[0], 0.0) + jnp.where(row64 == LANES - 1, dbl_c[1], 0.0))
            dqq = jnp.concatenate(dq_h, axis=1)
            dkk = jnp.concatenate(dk_h, axis=1)
            dvv = jnp.concatenate(dv_h, axis=1)
            db = db_lv + jnp.concatenate(dbl_h, axis=1)
            dlf = _exact_dot(suffix, _split(db, 3))
            dqa = dqq * _dsilu(qa)
            dfg = jnp.where(fg > TINY, dlf / fg, 0.0)
            dz = (dfg - dkk) * (1.0 - lb) * sg * sgn
            dlb_acc = dlb_acc + jnp.sum(dfg * (1.0 - sg) - dkk * sgn, axis=0, keepdims=True)
            da_ref[pl.ds(r0, LANES), :] = jnp.concatenate([dqa, dz, dvv, dga], axis=1).astype(BF16)
            return new_ds[0], new_ds[1], dgn_acc, dlb_acc

        zrow = jnp.zeros((1, LANES), F32)
        per_step = 4 if nt % 4 == 0 else 2

        def step(k, carry):
            for r in range(per_step):
                carry = bwd_tile(per_step * k + r, carry)
            return carry

        _, _, dgn_acc, dlb_acc = lax.fori_loop(0, nt // per_step, step, (zero, zero, zrow, zrow))
        dgn_ref[...] = jnp.broadcast_to(dgn_acc, (8, LANES))
        dlb_ref[...] = jnp.broadcast_to(dlb_acc, (8, LANES))

    rows = jax.ShapeDtypeStruct((bsz, 8, HGRN_W), F32)
    row = pl.BlockSpec((1, 128), lambda b, p: (0, p))
    blk = pl.BlockSpec((None, t, 128), lambda b, p: (b, 0, p))
    return pl.pallas_call(
        body, grid=(bsz, 2),
        in_specs=[pl.BlockSpec((None, t, 512), lambda b, p: (b, 0, p)), blk, blk,
                  pl.BlockSpec((None, 2, nchunk, CHUNK, CHUNK), lambda b, p: (b, p, 0, 0, 0)), row, row,
                  pl.BlockSpec(w_all.shape, lambda b, p: (0, 0)),
                  pl.BlockSpec(maskf.shape, lambda b, p: (0, 0, 0)),
                  pl.BlockSpec(rightf.shape, lambda b, p: (0, 0, 0))],
        out_specs=[pl.BlockSpec((None, t, 512), lambda b, p: (b, 0, p)),
                   pl.BlockSpec((None, 8, 128), lambda b, p: (b, 0, p)),
                   pl.BlockSpec((None, 8, 128), lambda b, p: (b, 0, p))],
        out_shape=[jax.ShapeDtypeStruct((bsz, t, A_W), BF16), rows, rows],
        compiler_params=_cparams(("parallel", "parallel")), name=name)(
            proj3, o_raw, dmixed, states, lbs_row, gn_row, w_all, maskf, rightf)


def _pool_tt(t):
    return min(256, t)


def _window_select(s2, s4, s8, s16, lane):
    return jnp.where(lane < 64, s2, jnp.where(lane < 128, s4, jnp.where(lane < 192, s8, s16)))


def _pool_counts(t0, tt):
    lane = _iota((tt, POOL_W), 1)
    tpos = (_iota((tt, POOL_W), 0) + t0 + 1).astype(F32)
    win = jnp.where(lane < 64, 2.0, jnp.where(lane < 128, 4.0, jnp.where(lane < 192, 8.0, 16.0)))
    return 1.0 / jnp.minimum(tpos, win), lane


def _pooled_tile(upad_ref, i, tt):
    r0 = pl.multiple_of(i * tt, 8)
    cat = upad_ref[pl.ds(r0, tt + POOL_HALO), :]
    s2 = cat + pltpu.roll(cat, 1, 0)
    s4 = s2 + pltpu.roll(s2, 2, 0)
    s8 = s4 + pltpu.roll(s4, 4, 0)
    s16 = s8 + pltpu.roll(s8, 8, 0)
    inv, lane = _pool_counts(i * tt, tt)
    sel = _window_select(s2[POOL_HALO:], s4[POOL_HALO:], s8[POOL_HALO:], s16[POOL_HALO:], lane)
    return sel * inv - cat[POOL_HALO:], inv, lane


def _pool_fwd(proj3, wbd, scale_row, name):
    bsz, t, _ = proj3.shape
    tt = _pool_tt(t)

    def body(p_ref, w_ref, sc_ref, o_ref, upad):
        upad[0:POOL_HALO, :] = jnp.zeros((POOL_HALO, POOL_W), F32)
        upad[POOL_HALO:, :] = p_ref[:, 0:POOL_W]
        w = w_ref[...]
        sc = sc_ref[...]

        def tile(i, c):
            pooled, _, _ = _pooled_tile(upad, i, tt)
            r0 = pl.multiple_of(i * tt, 8)
            g = p_ref[pl.ds(r0, tt), POOL_W:2 * POOL_W]
            pre = jnp.dot(pooled.astype(BF16), w, preferred_element_type=F32)
            o_ref[pl.ds(r0, tt), :] = (pre * sc * _silu(g)).astype(BF16)
            return c

        lax.fori_loop(0, t // tt, tile, 0)

    return pl.pallas_call(
        body, grid=(bsz,),
        in_specs=[pl.BlockSpec((None, t, 512), lambda b: (b, 0, B_BLK)),
                  pl.BlockSpec((POOL_W, POOL_W), lambda b: (0, 0)),
                  pl.BlockSpec((1, POOL_W), lambda b: (0, 0))],
        out_specs=pl.BlockSpec((None, t, POOL_W), lambda b: (b, 0, 0)),
        out_shape=jax.ShapeDtypeStruct((bsz, t, POOL_W), BF16),
        scratch_shapes=[pltpu.VMEM((t + POOL_HALO, POOL_W), F32)],
        compiler_params=_cparams(("parallel",)), name=name)(proj3, wbd, scale_row)


def _pool_bwd(proj3, dmixed, wbd, scale_row, name):
    bsz, t, _ = proj3.shape
    tt = _pool_tt(t)

    def body(p_ref, do_ref, w_ref, sc_ref, db_ref, dsc_ref, dw_ref, upad, epad):
        upad[0:POOL_HALO, :] = jnp.zeros((POOL_HALO, POOL_W), F32)
        upad[POOL_HALO:, :] = p_ref[:, 0:POOL_W]
        epad[t:, :] = jnp.zeros((POOL_HALO, POOL_W), F32)
        w = w_ref[...]
        sc = sc_ref[...]

        def tile(i, carry):
            dsc_acc, dw_acc = carry
            pooled, inv, _ = _pooled_tile(upad, i, tt)
            r0 = pl.multiple_of(i * tt, 8)
            g = p_ref[pl.ds(r0, tt), POOL_W:2 * POOL_W]
            dout = do_ref[pl.ds(r0, tt), :]
            pb = pooled.astype(BF16)
            pre = jnp.dot(pb, w, preferred_element_type=F32)
            t1 = dout * _silu(g)
            dsc_acc = dsc_acc + jnp.sum(t1 * pre, axis=0, keepdims=True)
            dpre = (t1 * sc).astype(BF16)
            db_ref[pl.ds(r0, tt), POOL_W:2 * POOL_W] = (dout * pre * sc * _dsilu(g)).astype(BF16)
            dw_acc = dw_acc + _dot(pb, dpre, TN)
            dpooled = _dot(dpre, w, NT)
            epad[pl.ds(r0, tt), :] = dpooled * inv
            return dsc_acc, dw_acc

        dsc_acc, dw_acc = lax.fori_loop(0, t // tt, tile, (jnp.zeros((1, POOL_W), F32), jnp.zeros((POOL_W, POOL_W), F32)))
        dsc_ref[...] = jnp.broadcast_to(dsc_acc, (8, POOL_W))
        dw_ref[...] = dw_acc

        def tile2(i, c):
            r0 = pl.multiple_of(i * tt, 8)
            n = tt + POOL_HALO
            cat = epad[pl.ds(r0, n), :]
            s2 = cat + pltpu.roll(cat, n - 1, 0)
            s4 = s2 + pltpu.roll(s2, n - 2, 0)
            s8 = s4 + pltpu.roll(s4, n - 4, 0)
            s16 = s8 + pltpu.roll(s8, n - 8, 0)
            inv, lane = _pool_counts(i * tt, tt)
            sel = _window_select(s2[:tt], s4[:tt], s8[:tt], s16[:tt], lane)
            db_ref[pl.ds(r0, tt), 0:POOL_W] = (sel - cat[:tt] / inv).astype(BF16)
            return c

        lax.fori_loop(0, t // tt, tile2, 0)

    return pl.pallas_call(
        body, grid=(bsz,),
        in_specs=[pl.BlockSpec((None, t, 512), lambda b: (b, 0, B_BLK)),
                  pl.BlockSpec((None, t, POOL_W), lambda b: (b, 0, 1)),
                  pl.BlockSpec((POOL_W, POOL_W), lambda b: (0, 0)),
                  pl.BlockSpec((1, POOL_W), lambda b: (0, 0))],
        out_specs=[pl.BlockSpec((None, t, 512), lambda b: (b, 0, 0)),
                   pl.BlockSpec((None, 8, POOL_W), lambda b: (b, 0, 0)),
                   pl.BlockSpec((None, POOL_W, POOL_W), lambda b: (b, 0, 0))],
        out_shape=[jax.ShapeDtypeStruct((bsz, t, B_W), BF16), jax.ShapeDtypeStruct((bsz, 8, POOL_W), F32),
                   jax.ShapeDtypeStruct((bsz, POOL_W, POOL_W), F32)],
        scratch_shapes=[pltpu.VMEM((t + POOL_HALO, POOL_W), F32), pltpu.VMEM((t + POOL_HALO, POOL_W), F32)],
        compiler_params=_cparams(("parallel",)), name=name)(proj3, dmixed, wbd, scale_row)


def _head_select_rows(hp):
    r, c = _iota((8, LANES), 0), _iota((8, LANES), 1)
    return ((r < 2) & (c == 2 * hp + r)).astype(F32)


def _foxgate_fwd(proj3, bias_row, name):
    bsz, t, _ = proj3.shape
    nt = t // LANES

    def body(f_ref, b_ref, cn_ref, ct_ref):
        bias = b_ref[...]
        i, j = _iota((LANES, LANES), 0), _iota((LANES, LANES), 1)
        lower = (j <= i).astype(BF16)
        spread = (_iota((LANES, FOX_W), 0) == _iota((LANES, FOX_W), 1) // 64).astype(BF16)
        select = [_head_select_rows(hp).astype(BF16) for hp in range(4)]
        offset = jnp.zeros((1, LANES), F32)
        for k in range(nt):
            rows = slice(k * LANES, (k + 1) * LANES)
            xg = f_ref[rows, :] + bias
            lf = jnp.minimum(xg, 0.0) - jnp.log(1.0 + jnp.exp(-jnp.abs(xg)))
            c = _exact_dot(lower, _split(lf, 3)) + offset
            offset = c[LANES - 1:LANES, :]
            parts = _split(c, 3)
            cn_ref[rows, :] = _head_sums(c, spread, 3)
            for hp in range(4):
                acc = _dot(select[hp], parts[0], NT)
                for p in parts[1:]:
                    acc = acc + _dot(select[hp], p, NT)
                ct_ref[hp, :, rows] = acc

    return pl.pallas_call(
        body, grid=(bsz,),
        in_specs=[pl.BlockSpec((None, t, 128), lambda b: (b, 0, F_BLK)), pl.BlockSpec((1, 128), lambda b: (0, 0))],
        out_specs=[pl.BlockSpec((None, t, FOX_W), lambda b: (b, 0, 0)),
                   pl.BlockSpec((None, 4, 8, t), lambda b: (b, 0, 0, 0))],
        out_shape=[jax.ShapeDtypeStruct((bsz, t, FOX_W), F32), jax.ShapeDtypeStruct((bsz, 4, 8, t), F32)],
        compiler_params=_cparams(("parallel",)), name=name)(proj3, bias_row)


def _foxgate_bwd(proj3, dc_nat, bias_row, name):
    bsz, t, _ = proj3.shape
    nt = t // LANES

    def body(f_ref, dc_ref, b_ref, df_ref, dbias_ref, run_sc):
        bias = b_ref[...]
        i, j = _iota((LANES, LANES), 0), _iota((LANES, LANES), 1)
        upper = (j >= i).astype(F32)
        valid = _iota((1, LANES), 1) < FOX_HEADS
        run_sc[...] = jnp.zeros((8, LANES), F32)
        dbias_ref[...] = jnp.zeros((8, LANES), F32)

        def tile(k, c):
            r0 = pl.multiple_of((nt - 1 - k) * LANES, LANES)
            dc = dc_ref[pl.ds(r0, LANES), :] + jnp.where(i == LANES - 1, run_sc[0:1, :], 0.0)
            dlf = jnp.dot(upper, dc, precision=HI, preferred_element_type=F32)
            xg = f_ref[pl.ds(r0, LANES), :] + bias
            df = jnp.where(valid, dlf * _sig(-xg), 0.0)
            df_ref[pl.ds(r0, LANES), :] = df.astype(BF16)
            run_sc[...] = dlf[0:8, :]
            dbias_ref[...] += jnp.sum(df, axis=0, keepdims=True)
            return c

        lax.fori_loop(0, nt, tile, 0)

    blk = pl.BlockSpec((None, t, 128), lambda b: (b, 0, 0))
    return pl.pallas_call(
        body, grid=(bsz,),
        in_specs=[pl.BlockSpec((None, t, 128), lambda b: (b, 0, F_BLK)), blk, pl.BlockSpec((1, 128), lambda b: (0, 0))],
        out_specs=[blk, pl.BlockSpec((None, 8, 128), lambda b: (b, 0, 0))],
        out_shape=[jax.ShapeDtypeStruct((bsz, t, F_W), BF16), jax.ShapeDtypeStruct((bsz, 8, 128), F32)],
        scratch_shapes=[pltpu.VMEM((8, LANES), F32)],
        compiler_params=_cparams(("parallel",)), name=name)(proj3, dc_nat, bias_row)


def _fox_tile(t):
    return min(256, t)


def _fox_fwd(proj3, c_nat, c_t, name):
    bsz, t, _ = proj3.shape
    tq = tk = min(2 * _fox_tile(t), t)
    nq = t // tq

    def body(q_ref, kv_ref, cn_ref, ct_ref, og_ref, or_ref, lse_ref):
        i = pl.program_id(2)
        qblk = q_ref[...]
        first = _iota((1, 128), 1) < 64
        qv = qblk[:, 0:128] * 0.125
        qm = [jnp.where(first, qv, 0.0).astype(BF16), jnp.where(first, 0.0, qv).astype(BF16)]
        cqs = [cn_ref[:, 0:1], cn_ref[:, 64:65]]
        rows = _iota((tq, tk), 0) + i * tq

        def scores(j):
            c0 = pl.multiple_of(j * tk, tk)
            kb = kv_ref[pl.ds(c0, tk), 128:256].astype(BF16)
            return tuple(_dot(qm[h], kb, NT) + (cqs[h] - ct_ref[h:h + 1, pl.ds(c0, tk)]) for h in range(2))

        def absorb(j, state, s01, masked):
            c0 = pl.multiple_of(j * tk, tk)
            vblk = kv_ref[pl.ds(c0, tk), 256:384]
            vx = [jnp.where(first, vblk, 1.0).astype(BF16), jnp.where(first, 1.0, vblk).astype(BF16)]
            new = []
            for h in range(2):
                m, acc, s = state[2 * h], state[2 * h + 1], s01[h]
                if masked:
                    s = jnp.where(rows >= _iota((tq, tk), 1) + j * tk, s, MASK_VALUE)
                m_new = jnp.maximum(m, jnp.max(s, axis=1, keepdims=True))
                p = jnp.exp(s - m_new).astype(BF16)
                new += [m_new, jnp.exp(m - m_new) * acc + jnp.dot(p, vx[h], preferred_element_type=F32)]
            return tuple(new)

        init = (jnp.full((tq, 1), MASK_VALUE, F32), jnp.zeros((tq, 128), F32)) * 2
        n_full = (i * tq) // tk
        state = lax.fori_loop(0, n_full, lambda j, state: absorb(j, state, scores(j), False), init)
        m0, acc0, m1, acc1 = absorb(n_full, state, scores(n_full), True)
        l0, l1 = pltpu.roll(acc0, 64, 1), pltpu.roll(acc1, 64, 1)
        o = jnp.where(first, acc0 / l0, acc1 / l1)
        or_ref[...] = o
        og_ref[...] = (o * _silu(qblk[:, 384:512])).astype(BF16)
        lse_ref[...] = jnp.where(first, m0 + jnp.log(l0), m1 + jnp.log(l1))

    out = jax.ShapeDtypeStruct((bsz, t, FOX_W), F32)
    blk = pl.BlockSpec((None, tq, 128), lambda b, p, i: (b, i, p))
    return pl.pallas_call(
        body, grid=(bsz, 4, nq),
        in_specs=[pl.BlockSpec((None, tq, 512), lambda b, p, i: (b, i, C_BLK0 + p)),
                  pl.BlockSpec((None, t, 512), lambda b, p, i: (b, 0, C_BLK0 + p)),
                  blk,
                  pl.BlockSpec((None, None, 8, t), lambda b, p, i: (b, p, 0, 0))],
        out_specs=[blk, blk, blk],
        out_shape=[jax.ShapeDtypeStruct((bsz, t, FOX_W), BF16), out, out],
        compiler_params=_cparams(("parallel", "parallel", "arbitrary")), name=name)(proj3, proj3, c_nat, c_t)


def _fox_bwd(proj3, o_raw, dmixed, lse, c_nat, c_t, name):
    bsz, t, _ = proj3.shape
    tq = tk = min(2 * _fox_tile(t), t)
    nq = t // tq
    ratio = tk // tq

    def body(a_ref, or_ref, do_ref, lse_ref, cn_ref, ct_ref, dc_out, dct_out, drow_out, dq_sc, do_sc, dl_sc):
        def prep(i, c):
            r0 = pl.multiple_of(i * tq, tq)
            g = a_ref[pl.ds(r0, tq), 384:512]
            dout = do_ref[pl.ds(r0, tq), :]
            o = or_ref[pl.ds(r0, tq), :]
            dc_out[pl.ds(r0, tq), 384:512] = (dout * o * _dsilu(g)).astype(BF16)
            do = dout * _silu(g)
            do_sc[pl.ds(r0, tq), :] = do
            prod = do * o
            d0 = jnp.sum(prod[:, 0:64], axis=1, keepdims=True)
            d1 = jnp.sum(prod[:, 64:128], axis=1, keepdims=True)
            dl_sc[pl.ds(r0, tq), :] = jnp.concatenate([jnp.broadcast_to(d0, (tq, 64)), jnp.broadcast_to(d1, (tq, 64))], axis=1)
            dq_sc[pl.ds(r0, tq), :] = jnp.zeros((tq, 128), F32)
            drow_out[pl.ds(r0, tq), :] = jnp.zeros((tq, 128), F32)
            return c

        lax.fori_loop(0, nq, prep, 0)
        dct_out[...] = jnp.zeros((8, t), F32)

        first = _iota((1, 128), 1) < 64

        def heads(v):
            return [jnp.where(first, v, 0.0).astype(BF16), jnp.where(first, 0.0, v).astype(BF16)]

        def kv_tile(j, c):
            c0 = pl.multiple_of(j * tk, tk)
            kb = a_ref[pl.ds(c0, tk), 128:256].astype(BF16)
            vb = a_ref[pl.ds(c0, tk), 256:384].astype(BF16)
            cks = [ct_ref[h:h + 1, pl.ds(c0, tk)] for h in range(2)]

            def q_step(i, carry, diagonal):
                dk, dv, dcol0, dcol1 = carry
                r0 = pl.multiple_of(i * tq, tq)
                causal = _iota((tq, tk), 0) + i * tq >= _iota((tq, tk), 1) + j * tk
                qv = a_ref[pl.ds(r0, tq), 0:128] * 0.125
                do = do_sc[pl.ds(r0, tq), :]
                qb, dob = qv.astype(BF16), do.astype(BF16)
                qm, dom = heads(qv), heads(do)
                full, dcols, rsums = [], [], []
                for h in range(2):
                    lse_h = lse_ref[pl.ds(r0, tq), 64 * h:64 * h + 1]
                    dl_h = dl_sc[pl.ds(r0, tq), 64 * h:64 * h + 1]
                    cq = cn_ref[pl.ds(r0, tq), 64 * h:64 * h + 1]
                    p = jnp.exp(_dot(qm[h], kb, NT) + (cq - cks[h]) - lse_h)
                    if diagonal:
                        p = jnp.where(causal, p, 0.0)
                    ds = p * (_dot(dom[h], vb, NT) - dl_h)
                    dsb = ds.astype(BF16)
                    full.append((_dot(p.astype(BF16), dob, TN), _dot(dsb, qb, TN),
                                 jnp.dot(dsb, kb, preferred_element_type=F32)))
                    dcols.append(jnp.sum(ds, axis=0, keepdims=True))
                    rsums.append(jnp.broadcast_to(jnp.sum(ds, axis=1, keepdims=True), (tq, 128)))
                dq_sc[pl.ds(r0, tq), :] += jnp.where(first, full[0][2], full[1][2]) * 0.125
                drow_out[pl.ds(r0, tq), :] += jnp.where(first, rsums[0], rsums[1])
                return (dk + jnp.where(first, full[0][1], full[1][1]), dv + jnp.where(first, full[0][0], full[1][0]),
                        dcol0 - dcols[0], dcol1 - dcols[1])

            carry = (jnp.zeros((tk, 128), F32), jnp.zeros((tk, 128), F32), jnp.zeros((1, tk), F32), jnp.zeros((1, tk), F32))
            for r in range(ratio):
                carry = q_step(ratio * j + r, carry, True)
            dk, dv, dcol0, dcol1 = lax.fori_loop(ratio * (j + 1), nq, functools.partial(q_step, diagonal=False), carry)
            dct_out[0:1, pl.ds(c0, tk)] = dcol0
            dct_out[1:2, pl.ds(c0, tk)] = dcol1
            dc_out[pl.ds(c0, tk), 128:256] = dk.astype(BF16)
            dc_out[pl.ds(c0, tk), 256:384] = dv.astype(BF16)
            return c

        lax.fori_loop(0, t // tk, kv_tile, 0)
        dc_out[:, 0:128] = dq_sc[...].astype(BF16)

    blk = pl.BlockSpec((None, t, 128), lambda b, p: (b, 0, p))
    return pl.pallas_call(
        body, grid=(bsz, 4),
        in_specs=[pl.BlockSpec((None, t, 512), lambda b, p: (b, 0, C_BLK0 + p)),
                  blk,
                  pl.BlockSpec((None, t, 128), lambda b, p: (b, 0, 4 + p)),
                  blk, blk,
                  pl.BlockSpec((None, None, 8, t), lambda b, p: (b, p, 0, 0))],
        out_specs=[pl.BlockSpec((None, t, 512), lambda b, p: (b, 0, p)),
                   pl.BlockSpec((None, None, 8, t), lambda b, p: (b, p, 0, 0)), blk],
        out_shape=[jax.ShapeDtypeStruct((bsz, t, C_W), BF16), jax.ShapeDtypeStruct((bsz, 4, 8, t), F32),
                   jax.ShapeDtypeStruct((bsz, t, FOX_W), F32)],
        scratch_shapes=[pltpu.VMEM((t, 128), F32), pltpu.VMEM((t, 128), F32), pltpu.VMEM((t, 128), F32)],
        compiler_params=_cparams(("parallel", "parallel")), name=name)(proj3, o_raw, dmixed, lse, c_nat, c_t)


def _mix_tm(n):
    return min(512, n)


def _outproj_fwd(x2, oa, ob, oc, wo, g_row, name):
    n, d = x2.shape
    tm = _mix_tm(n)

    def body(x_ref, oa_ref, ob_ref, oc_ref, w_ref, g_ref, y_ref, xo_ref):
        y = (jnp.dot(oa_ref[...].astype(BF16), w_ref[0:256, :], preferred_element_type=F32)
             + jnp.dot(ob_ref[...].astype(BF16), w_ref[256:512, :], preferred_element_type=F32)
             + jnp.dot(oc_ref[...].astype(BF16), w_ref[512:1024, :], preferred_element_type=F32))
        y_ref[...] = y
        xo_ref[...] = x_ref[...] + y * _rstd(y) * g_ref[...]

    row = lambda w: pl.BlockSpec((tm, w), lambda i: (i, 0))
    out = jax.ShapeDtypeStruct((n, d), F32)
    return pl.pallas_call(
        body, grid=(n // tm,),
        in_specs=[row(d), row(256), row(256), row(512), pl.BlockSpec((d, d), lambda i: (0, 0)),
                  pl.BlockSpec((1, d), lambda i: (0, 0))],
        out_specs=[row(d), row(d)], out_shape=[out, out],
        compiler_params=_cparams(("parallel",)), name=name)(x2, oa, ob, oc, wo, g_row)


def _outproj_fwd_loss(x2, oa, ob, oc, wo, g_row, target2, name):
    n, d = x2.shape
    tm = _mix_tm(n)

    def body(x_ref, oa_ref, ob_ref, oc_ref, w_ref, g_ref, t_ref, y_ref, dx_ref, l_ref):
        y = (jnp.dot(oa_ref[...].astype(BF16), w_ref[0:256, :], preferred_element_type=F32)
             + jnp.dot(ob_ref[...].astype(BF16), w_ref[256:512, :], preferred_element_type=F32)
             + jnp.dot(oc_ref[...].astype(BF16), w_ref[512:1024, :], preferred_element_type=F32))
        y_ref[...] = y
        err = (x_ref[...] + y * _rstd(y) * g_ref[...]) - t_ref[...]
        dx_ref[...] = err * (1.0 / d)

        @pl.when(pl.program_id(0) == 0)
        def _():
            l_ref[...] = jnp.zeros((8, 128), F32)

        l_ref[...] += jnp.sum(err * err)

    row = lambda w: pl.BlockSpec((tm, w), lambda i: (i, 0))
    out = jax.ShapeDtypeStruct((n, d), F32)
    return pl.pallas_call(
        body, grid=(n // tm,),
        in_specs=[row(d), row(256), row(256), row(512), pl.BlockSpec((d, d), lambda i: (0, 0)),
                  pl.BlockSpec((1, d), lambda i: (0, 0)), row(d)],
        out_specs=[row(d), row(d), pl.BlockSpec((8, 128), lambda i: (0, 0))],
        out_shape=[out, out, jax.ShapeDtypeStruct((8, 128), F32)],
        compiler_params=_cparams(("arbitrary",)), name=name)(x2, oa, ob, oc, wo, g_row, target2)


def _outproj_bwd(dxo, y, oa, ob, oc, wo, g_row, name):
    n, d = dxo.shape
    tm = _mix_tm(n)

    def body(dx_ref, y_ref, oa_ref, ob_ref, oc_ref, w_ref, g_ref, dm_ref, dw_ref, dg_ref):
        @pl.when(pl.program_id(0) == 0)
        def _():
            dw_ref[...] = jnp.zeros((d, d), F32)
            dg_ref[...] = jnp.zeros((8, d), F32)

        yv, dx = y_ref[...], dx_ref[...]
        r = _rstd(yv)
        yn = yv * r
        dg_ref[...] += jnp.sum(dx * yn, axis=0, keepdims=True)
        dyn = dx * g_ref[...]
        dy = (r * (dyn - yn * jnp.mean(dyn * yn, axis=-1, keepdims=True))).astype(BF16)
        dm_ref[...] = _dot(dy, w_ref[...], NT)
        dw_ref[0:256, :] += _dot(oa_ref[...].astype(BF16), dy, TN)
        dw_ref[256:512, :] += _dot(ob_ref[...].astype(BF16), dy, TN)
        dw_ref[512:1024, :] += _dot(oc_ref[...].astype(BF16), dy, TN)

    row = lambda w: pl.BlockSpec((tm, w), lambda i: (i, 0))
    fixed = lambda r, c: pl.BlockSpec((r, c), lambda i: (0, 0))
    return pl.pallas_call(
        body, grid=(n // tm,),
        in_specs=[row(d), row(d), row(256), row(256), row(512), fixed(d, d), fixed(1, d)],
        out_specs=[row(d), fixed(d, d), fixed(8, d)],
        out_shape=[jax.ShapeDtypeStruct((n, d), F32), jax.ShapeDtypeStruct((d, d), F32), jax.ShapeDtypeStruct((8, d), F32)],
        compiler_params=_cparams(("arbitrary",)), name=name)(dxo, y, oa, ob, oc, wo, g_row)


_PIECES = ((0, A_W), (A_W, B_W), (A_W + B_W, C_W), (A_W + B_W + C_W, F_W))


def _inproj_bwd_x(x2, dxo, g_row, w_int, pieces, name):
    n, d = x2.shape
    tm = min(512, n)

    def body(x_ref, dxo_ref, g_ref, w_ref, da_ref, db_ref, dc_ref, df_ref, dx_ref, dg_ref):
        @pl.when(pl.program_id(0) == 0)
        def _():
            dg_ref[...] = jnp.zeros((8, d), F32)

        dh = jnp.zeros((tm, d), F32)
        for ref, (o, w) in zip((da_ref, db_ref, dc_ref, df_ref), _PIECES):
            dh = dh + _dot(ref[...].astype(BF16), w_ref[:, o:o + w], NT)
        x = x_ref[...]
        r = _rstd(x)
        xn = x * r
        dg_ref[...] += jnp.sum(dh * xn, axis=0, keepdims=True)
        dxn = dh * g_ref[...]
        dx_ref[...] = dxo_ref[...] + r * (dxn - xn * jnp.mean(dxn * xn, axis=-1, keepdims=True))

    row = lambda w: pl.BlockSpec((tm, w), lambda i: (i, 0))
    fixed = lambda r, c: pl.BlockSpec((r, c), lambda i: (0, 0))
    return pl.pallas_call(
        body, grid=(n // tm,),
        in_specs=[row(d), row(d), fixed(1, d), fixed(d, E_INT)] + [row(w) for _, w in _PIECES],
        out_specs=[row(d), fixed(8, d)],
        out_shape=[jax.ShapeDtypeStruct((n, d), F32), jax.ShapeDtypeStruct((8, d), F32)],
        compiler_params=_cparams(("arbitrary",), vmem_mb=56), name=name)(x2, dxo, g_row, w_int, *pieces)


def _inproj_bwd_w(x2, g_row, pieces, name):
    n, d = x2.shape
    tm = min(512, n)

    def body(x_ref, g_ref, da_ref, db_ref, dc_ref, df_ref, dw_ref):
        @pl.when(pl.program_id(0) == 0)
        def _():
            dw_ref[...] = jnp.zeros((d, E_INT), F32)

        x = x_ref[...]
        h = (x * _rstd(x) * g_ref[...]).astype(BF16)
        for ref, (o, w) in zip((da_ref, db_ref, dc_ref, df_ref), _PIECES):
            dw_ref[:, o:o + w] += _dot(h, ref[...].astype(BF16), TN)

    row = lambda w: pl.BlockSpec((tm, w), lambda i: (i, 0))
    return pl.pallas_call(
        body, grid=(n // tm,),
        in_specs=[row(d), pl.BlockSpec((1, d), lambda i: (0, 0))] + [row(w) for _, w in _PIECES],
        out_specs=pl.BlockSpec((d, E_INT), lambda i: (0, 0)),
        out_shape=jax.ShapeDtypeStruct((d, E_INT), F32),
        compiler_params=_cparams(("arbitrary",), vmem_mb=56), name=name)(x2, g_row, *pieces)


def _block_diag(pool_w_l):
    z = jnp.zeros((64, 64), pool_w_l.dtype)
    return jnp.concatenate(
        [jnp.concatenate([pool_w_l[g] if c == g else z for c in range(4)], axis=1) for g in range(4)], axis=0)


def _pad_lanes(v, width=128):
    return jnp.pad(v, ((0, 0),) * (v.ndim - 1) + ((0, width - v.shape[-1]),))


def _local_step(x, target, lower_bounds, pre_norm_g, w_in_int, hgrn_norm_g, fox_f_bias, pool_w, pool_scale,
                w_out_bf, post_norm_g, on_weight_grads):
    bsz, t, d = x.shape
    n = bsz * t
    lbs = _lbs_fwd(lower_bounds)
    saved = []
    xc = x.reshape(n, d)
    for l in range(DEPTH):
        proj = _inproj_fwd(xc, pre_norm_g[l:l + 1], w_in_int[l], f"inproj_fwd{l}").reshape(bsz, t, E_INT)
        wbd = _block_diag(pool_w[l]).astype(BF16)
        bias_row = _pad_lanes(fox_f_bias[l:l + 1])
        oa, oa_raw, states = _hgrn_fwd(proj, lbs[l:l + 1], hgrn_norm_g[l:l + 1], f"hgrn_fwd{l}")
        ob = _pool_fwd(proj, wbd, pool_scale[l:l + 1], f"pool_fwd{l}")
        c_nat, c_t = _foxgate_fwd(proj, bias_row, f"foxgate_fwd{l}")
        oc, oc_raw, lse = _fox_fwd(proj, c_nat, c_t, f"fox_fwd{l}")
        mixed = (oa.reshape(n, -1), ob.reshape(n, -1), oc.reshape(n, -1))
        if l < DEPTH - 1:
            y, xn = _outproj_fwd(xc, *mixed, w_out_bf[l], post_norm_g[l:l + 1], f"outproj_fwd{l}")
        else:
            y, dx, sq = _outproj_fwd_loss(xc, *mixed, w_out_bf[l], post_norm_g[l:l + 1], target.reshape(n, d),
                                          f"outproj_fwd{l}")
        saved.append((xc, proj, wbd, bias_row, oa, oa_raw, states, ob, oc, oc_raw, lse, c_nat, c_t, y))
        xc = xn
    g = {k: [None] * DEPTH for k in ("pre", "hgn", "bias", "pool_w", "pool_scale", "post", "lbs")}
    handed = [None] * DEPTH
    for l in reversed(range(DEPTH)):
        xin, proj, wbd, bias_row, oa, oa_raw, states, ob, oc, oc_raw, lse, c_nat, c_t, y = saved[l]
        dmix, d_w_out, dpost = _outproj_bwd(dx, y, oa.reshape(n, -1), ob.reshape(n, -1), oc.reshape(n, -1),
                                            w_out_bf[l], post_norm_g[l:l + 1], f"outproj_bwd{l}")
        g["post"][l] = dpost[0]
        dmix3 = dmix.reshape(bsz, t, d)
        d_c, dct, drow = _fox_bwd(proj, oc_raw, dmix3, lse, c_nat, c_t, f"fox_bwd{l}")
        dc_nat = _pad_lanes(dct[:, :, 0:2, :].reshape(bsz, FOX_HEADS, t).transpose(0, 2, 1)
                            + drow.reshape(bsz, t, FOX_HEADS, 64)[..., 0])
        d_f, dbias = _foxgate_bwd(proj, dc_nat, bias_row, f"foxgate_bwd{l}")
        g["bias"][l] = jnp.sum(dbias[:, 0, :FOX_HEADS], axis=0)
        d_b, dscale, dwbd = _pool_bwd(proj, dmix3, wbd, pool_scale[l:l + 1], f"pool_bwd{l}")
        g["pool_scale"][l] = jnp.sum(dscale[:, 0], axis=0)
        dwbd = jnp.sum(dwbd, axis=0)
        g["pool_w"][l] = jnp.stack([dwbd[64 * k:64 * (k + 1), 64 * k:64 * (k + 1)] for k in range(4)])
        d_a, dgn, dlb = _hgrn_bwd(proj, oa_raw, dmix3, states, lbs[l:l + 1], hgrn_norm_g[l:l + 1], f"hgrn_bwd{l}")
        g["hgn"][l] = jnp.sum(dgn[:, 0], axis=0)
        g["lbs"][l] = jnp.sum(dlb[:, 0], axis=0)
        pieces = [p.reshape(n, -1) for p in (d_a, d_b, d_c, d_f)]
        handed[l] = on_weight_grads(l, _inproj_bwd_w(xin, pre_norm_g[l:l + 1], pieces, f"inproj_bwd_w{l}"), d_w_out)
        dx, dpre = _inproj_bwd_x(xin, dx, pre_norm_g[l:l + 1], w_in_int[l], pieces, f"inproj_bwd_x{l}")
        g["pre"][l] = dpre[0]
    grads = {k: jnp.stack(v) for k, v in g.items()}
    return sq, dx.reshape(bsz, t, d), grads, handed


def _place():
    return lax.axis_index("x"), lax.axis_index("y"), lax.axis_index("c")


def _other_chips(x, y):
    return [(1 - x, y), (x, 1 - y), (1 - x, 1 - y)]


_ANY = pl.BlockSpec(memory_space=pl.ANY)


def _gather_body(handshake, n_arrays):
    def body(*refs):
        srcs, dsts = refs[:n_arrays], refs[n_arrays:2 * n_arrays]
        ici_send, ici_recv, d2d_send, d2d_recv, local_sems = refs[2 * n_arrays:]
        x, y, c = _place()
        if handshake:
            barrier = pltpu.get_barrier_semaphore()
            for peer in [(px, py, c) for px, py in _other_chips(x, y)] + [(x, y, 1 - c)]:
                pl.semaphore_signal(barrier, inc=1, device_id=peer, device_id_type=MESH)
            pl.semaphore_wait(barrier, 4)
        me = 2 * x + y
        pairs = list(zip(srcs, dsts))
        order = [(k, j) for k in range(3) for j in range(n_arrays)]
        mine = [pltpu.make_async_copy(src, dst.at[me], local_sems.at[j]) for j, (src, dst) in enumerate(pairs)]
        for cp in mine:
            cp.start()
        chips = _other_chips(x, y)
        sends = [pltpu.make_async_remote_copy(
            src_ref=pairs[j][0].at[c], dst_ref=pairs[j][1].at[me, c], send_sem=ici_send.at[n], recv_sem=ici_recv.at[n],
            device_id=(chips[k][0], chips[k][1], c), device_id_type=MESH) for n, (k, j) in enumerate(order)]
        for cp in sends:
            cp.start()
        passed = [pltpu.make_async_remote_copy(
            src_ref=pairs[j][1].at[2 * chips[k][0] + chips[k][1], c], dst_ref=pairs[j][1].at[2 * chips[k][0] + chips[k][1], c],
            send_sem=d2d_send.at[n], recv_sem=d2d_recv.at[n], device_id=(x, y, 1 - c), device_id_type=MESH)
            for n, (k, j) in enumerate(order)]
        for n, (k, j) in enumerate(order):
            px, py = chips[k]
            src, dst = pairs[j]
            pltpu.make_async_remote_copy(
                src_ref=src.at[c], dst_ref=dst.at[2 * px + py, c], send_sem=ici_send.at[n], recv_sem=ici_recv.at[n],
                device_id=(px, py, c), device_id_type=MESH).wait_recv()
            passed[n].start()
        for n, (k, j) in enumerate(order):
            px, py = chips[k]
            src, dst = pairs[j]
            pltpu.make_async_remote_copy(
                src_ref=dst.at[2 * px + py, 1 - c], dst_ref=dst.at[2 * px + py, 1 - c], send_sem=d2d_send.at[n],
                recv_sem=d2d_recv.at[n], device_id=(x, y, 1 - c), device_id_type=MESH).wait_recv()
        for cp in sends + passed:
            cp.wait_send()
        for cp in mine:
            cp.wait()

    return body


def _gather_sems(n_arrays):
    return [pltpu.SemaphoreType.DMA((3 * n_arrays,))] * 4 + [pltpu.SemaphoreType.DMA((n_arrays,))]


def _gathered(a):
    return jax.ShapeDtypeStruct((N_CHIPS,) + a.shape, a.dtype)


def _gather_weights(arrays):
    n = len(arrays)
    return pl.pallas_call(
        _gather_body(False, n), in_specs=[_ANY] * n, out_specs=[_ANY] * n, out_shape=[_gathered(a) for a in arrays],
        scratch_shapes=_gather_sems(n), name="gather_weights")(*arrays)


def _gather_weights_beside(arrays):
    hbm = pltpu.MemorySpace.HBM
    n = len(arrays)
    srcs = [jax.new_ref(a, memory_space=hbm) for a in arrays]
    dsts = [jax.empty_ref(_gathered(a), memory_space=hbm) for a in arrays]
    body = _gather_body(True, n)

    @pl.kernel(mesh=plsc.ScalarSubcoreMesh(axis_name="sequencer", num_cores=1), name="gather_weights_beside",
               scratch_types=_gather_sems(n), compiler_params=pltpu.CompilerParams(collective_id=1))
    def launch(*sems):
        body(*srcs, *dsts, *sems)

    launch()
    return [d[...] for d in dsts]


def _swap_with_sibling(parts, name):
    k = len(parts)

    def body(*refs):
        src, dst = refs[:k], refs[k:2 * k]
        send_sems, recv_sems = refs[2 * k:]
        x, y, c = _place()
        cps = [pltpu.make_async_remote_copy(src_ref=src[j], dst_ref=dst[j], send_sem=send_sems.at[j], recv_sem=recv_sems.at[j],
                                            device_id=(x, y, 1 - c), device_id_type=MESH) for j in range(k)]
        for cp in cps:
            cp.start()
        for cp in cps:
            cp.wait()

    return pl.pallas_call(
        body, in_specs=[_ANY] * k, out_specs=[_ANY] * k,
        out_shape=[jax.ShapeDtypeStruct(p.shape, p.dtype) for p in parts],
        scratch_shapes=[pltpu.SemaphoreType.DMA((k,)), pltpu.SemaphoreType.DMA((k,))], name=name)(*parts)


N_PEERS = 7


def _grad_exchange_body():
    def body(pin_ref, pout_ref, lin_ref, lout_ref, send_sems, recv_sems):
        x, y, c = _place()
        barrier = pltpu.get_barrier_semaphore()
        for k in range(1, N_PEERS + 1):
            peer = (x ^ ((k >> 2) & 1), y ^ ((k >> 1) & 1), c ^ (k & 1))
            pl.semaphore_signal(barrier, inc=1, device_id=peer, device_id_type=MESH)
        pl.semaphore_wait(barrier, N_PEERS)
        me = 2 * x + y
        pairs = ((pin_ref, lin_ref), (pout_ref, lout_ref))
        cps = []
        for k, (px, py) in enumerate(_other_chips(x, y)):
            for r in range(2):
                for j, (src, dst) in enumerate(pairs):
                    cps.append(pltpu.make_async_remote_copy(
                        src_ref=src.at[2 * px + py, r], dst_ref=dst.at[2 * k + c], send_sem=send_sems.at[2 * (2 * k + r) + j],
                        recv_sem=recv_sems.at[2 * (2 * k + c) + j], device_id=(px, py, r), device_id_type=MESH))
        for j, (src, dst) in enumerate(pairs):
            cps.append(pltpu.make_async_remote_copy(
                src_ref=src.at[me, 1 - c], dst_ref=dst.at[N_PEERS - 1], send_sem=send_sems.at[2 * (N_PEERS - 1) + j],
                recv_sem=recv_sems.at[2 * (N_PEERS - 1) + j], device_id=(x, y, 1 - c), device_id_type=MESH))
        for cp in cps:
            cp.start()
        for s in range(N_PEERS):
            for j, (src, dst) in enumerate(pairs):
                pltpu.make_async_remote_copy(
                    src_ref=src.at[0, 0], dst_ref=dst.at[s], send_sem=send_sems.at[2 * s + j], recv_sem=recv_sems.at[2 * s + j],
                    device_id=(x, y, 1 - c), device_id_type=MESH).wait_recv()
        for cp in cps:
            cp.wait_send()

    return body


_EXCHANGE_SEMS = [pltpu.SemaphoreType.DMA((2 * N_PEERS,))] * 2


def _landing(p):
    return jax.ShapeDtypeStruct((N_PEERS,) + p.shape[2:], p.dtype)


def _grad_exchange_beside(pin, pout, name, collective_id):
    hbm = pltpu.MemorySpace.HBM
    pin_ref, pout_ref = jax.new_ref(pin, memory_space=hbm), jax.new_ref(pout, memory_space=hbm)
    lin_ref, lout_ref = jax.empty_ref(_landing(pin), memory_space=hbm), jax.empty_ref(_landing(pout), memory_space=hbm)
    body = _grad_exchange_body()

    @pl.kernel(mesh=plsc.ScalarSubcoreMesh(axis_name="sequencer", num_cores=1), name=name,
               scratch_types=_EXCHANGE_SEMS, compiler_params=pltpu.CompilerParams(collective_id=collective_id))
    def launch(send_sems, recv_sems):
        body(pin_ref, pout_ref, lin_ref, lout_ref, send_sems, recv_sems)

    launch()
    return lin_ref[...], lout_ref[...]


def _add_n(parts, name):
    r, c = parts[0].shape
    tr = 256 if r % 256 == 0 else r
    n = len(parts)

    def body(*refs):
        acc = refs[0][...].astype(F32)
        for ref in refs[1:n]:
            acc = acc + ref[...].astype(F32)
        refs[n][...] = acc

    blk = pl.BlockSpec((tr, c), lambda i: (i, 0))
    return pl.pallas_call(
        body, grid=(r // tr,), in_specs=[blk] * n, out_specs=blk, out_shape=jax.ShapeDtypeStruct((r, c), F32),
        compiler_params=_cparams(("parallel",)), name=name)(*parts)


def _all_reduce_small(packet):
    r, w = packet.shape

    def body(p_ref, o_ref, buf, send_sems, recv_sems):
        x, y, c = _place()
        me = 4 * x + 2 * y + c
        buf[me] = p_ref[...]
        peers = []
        for k in range(1, 8):
            fx, fy, fc = (k >> 2) & 1, (k >> 1) & 1, k & 1
            peers.append((x ^ fx, y ^ fy, c ^ fc))
        cps = [pltpu.make_async_remote_copy(src_ref=p_ref, dst_ref=buf.at[me], send_sem=send_sems.at[k], recv_sem=recv_sems.at[k],
                                            device_id=peer, device_id_type=MESH) for k, peer in enumerate(peers)]
        for cp in cps:
            cp.start()
        for k, (px, py, pc) in enumerate(peers):
            pltpu.make_async_remote_copy(src_ref=p_ref, dst_ref=buf.at[4 * px + 2 * py + pc], send_sem=send_sems.at[k],
                                         recv_sem=recv_sems.at[k], device_id=(px, py, pc), device_id_type=MESH).wait_recv()
        for cp in cps:
            cp.wait_send()
        acc = buf[0]
        for k in range(1, 8):
            acc = acc + buf[k]
        o_ref[...] = acc

    vm = pl.BlockSpec(memory_space=pltpu.VMEM)
    return pl.pallas_call(
        body, in_specs=[vm], out_specs=vm, out_shape=jax.ShapeDtypeStruct((r, w), F32),
        scratch_shapes=[pltpu.VMEM((8, r, w), F32), pltpu.SemaphoreType.DMA((7,)), pltpu.SemaphoreType.DMA((7,))],
        name="all_reduce_small")(packet)


def _adamw_math(w, g, m, v):
    m = ADAM_B1 * m + (1.0 - ADAM_B1) * g
    v = ADAM_B2 * v + (1.0 - ADAM_B2) * (g * g)
    m_hat = m / (1.0 - ADAM_B1 ** ADAM_STEP)
    v_hat = v / (1.0 - ADAM_B2 ** ADAM_STEP)
    return -ADAM_LR * (m_hat / (jnp.sqrt(v_hat) + ADAM_EPS) + ADAM_WD * w), m, v


def _adamw(w, g_lower, g_upper, m, v, name):
    nl, r, c = w.shape
    tr = 128
    per_half = r // (2 * tr)

    def body(w_ref, lo_ref, up_ref, m_ref, v_ref, g_ref, d_ref, mo_ref, vo_ref):
        g = jnp.where(pl.program_id(1) == 0, lo_ref[...], up_ref[...])
        g_ref[...] = g
        d_ref[...], mo_ref[...], vo_ref[...] = _adamw_math(w_ref[...], g, m_ref[...], v_ref[...])

    blk = pl.BlockSpec((None, tr, c), lambda l, h, i: (l, h * per_half + i, 0))
    half = pl.BlockSpec((None, tr, c), lambda l, h, i: (l, i, 0))
    out = jax.ShapeDtypeStruct(w.shape, F32)
    return pl.pallas_call(
        body, grid=(nl, 2, per_half), in_specs=[blk, half, half, blk, blk], out_specs=[blk] * 4, out_shape=[out] * 4,
        compiler_params=_cparams(("parallel", "parallel", "parallel")), name=name)(w, g_lower, g_upper, m, v)


def _small_update(gsum, lower_bounds, wpack, mpack, vpack):
    r, w = gsum.shape
    lb_rows = DEPTH * HGRN_W // 128

    def body(g_ref, a_ref, w_ref, m_ref, v_ref, go_ref, d_ref, mo_ref, vo_ref):
        a = a_ref[...]
        a0, a1 = a[0:1], a[1:2]
        mx = jnp.maximum(a0, a1)
        e0, e1 = jnp.exp(a0 - mx), jnp.exp(a1 - mx)
        p0, p1 = e0 / (e0 + e1), e1 / (e0 + e1)
        g = g_ref[...]
        half = lb_rows // 2
        dl0 = jnp.concatenate([g[k:k + 1] for k in range(half)], axis=1)
        dl1 = jnp.concatenate([g[half + k:half + k + 1] for k in range(half)], axis=1)
        dp0 = (dl0 + dl1) - (dl0 + dl1)
        dp1 = dl1
        inner = p0 * dp0 + p1 * dp1
        da0, da1 = p0 * (dp0 - inner), p1 * (dp1 - inner)
        rows = [da0[:, 128 * k:128 * (k + 1)] for k in range(half)] + [da1[:, 128 * k:128 * (k + 1)] for k in range(half)]
        gfull = jnp.concatenate(rows + [g[lb_rows:]], axis=0)
        go_ref[...] = gfull
        d_ref[...], mo_ref[...], vo_ref[...] = _adamw_math(w_ref[...], gfull, m_ref[...], v_ref[...])

    vm = pl.BlockSpec(memory_space=pltpu.VMEM)
    out = jax.ShapeDtypeStruct((r, w), F32)
    return pl.pallas_call(body, in_specs=[vm] * 5, out_specs=[vm] * 4, out_shape=[out] * 4, name="small_update")(
        gsum, lower_bounds, wpack, mpack, vpack)


_SMALL = ("lower_bounds", "pre_norm_g", "hgrn_norm_g", "fox_f_bias", "pool_w", "pool_scale", "post_norm_g")


def _pack(parts):
    rows = []
    for k in _SMALL:
        f = parts[k].reshape(-1)
        pad = (-f.shape[0]) % (8 * 128)
        rows.append(jnp.pad(f, (0, pad)).reshape(-1, 128))
    rows.append(jnp.zeros((8, 128), F32))
    return jnp.concatenate(rows, axis=0)


def _unpack(pack, like):
    out, r = {}, 0
    for k in _SMALL:
        size = int(np.prod(like[k].shape))
        nr = -(-size // (8 * 128)) * 8
        out[k] = pack[r:r + nr].reshape(-1)[:size].reshape(like[k].shape)
        r += nr
    return out, r


def kernel(x, lower_bounds, pre_norm_g, w_in, hgrn_norm_g, fox_f_bias, pool_w, pool_scale, w_out, post_norm_g, loss_target, m_lower_bounds, m_pre_norm_g, m_w_in, m_hgrn_norm_g, m_fox_f_bias, m_pool_w, m_pool_scale, m_w_out, m_post_norm_g, v_lower_bounds, v_pre_norm_g, v_w_in, v_hgrn_norm_g, v_fox_f_bias, v_pool_w, v_pool_scale, v_w_out, v_post_norm_g):
    cx, cy, cc = _place()
    chip = 2 * cx + cy

    halves = lambda w, l: w[l].reshape(2, w.shape[1] // 2, w.shape[2]).astype(BF16)
    needed_first = _gather_weights([halves(w_in, 0)])
    needed_first, later = lax.optimization_barrier((needed_first, [halves(w_out, 0), halves(w_in, 1), halves(w_out, 1)]))
    later = _gather_weights_beside(later)
    w_in_int = [_internal_from_shards([a[q].reshape(D_MODEL, SHARD_W) for q in range(N_CHIPS)]) for a in (needed_first[0], later[1])]
    w_out_full = [a.reshape(D_MODEL, D_MODEL) for a in (later[0], later[2])]

    def on_weight_grads(l, d_w_in, d_w_out):
        pin = _shards_from_internal(d_w_in).reshape(N_CHIPS, 2, D_MODEL // 2, SHARD_W)
        pout = d_w_out.reshape(N_CHIPS, 2, D_MODEL // (2 * N_CHIPS), D_MODEL)
        own = [lax.dynamic_index_in_dim(lax.dynamic_index_in_dim(p, chip, 0, False), cc, 0, False) for p in (pin, pout)]
        return own, _grad_exchange_beside(pin.astype(BF16), pout.astype(BF16), f"grad_exchange{l}", 2 + l)

    sq, grad_x, g, handed = _local_step(x, loss_target, lower_bounds, pre_norm_g, w_in_int, hgrn_norm_g, fox_f_bias,
                                        pool_w, pool_scale, w_out_full, post_norm_g, on_weight_grads)
    first = cc == 0

    def finish(l, own, landed):
        mine = [_add_n([o] + [t[s] for s in range(N_PEERS)], f"grad_sum{l}_{j}") for j, (o, t) in enumerate(zip(own, landed))]
        theirs = _swap_with_sibling(mine, f"grad_swap{l}")
        return [(jnp.where(first, h, o), jnp.where(first, o, h)) for h, o in zip(mine, theirs)]

    grad_x, last = lax.optimization_barrier((grad_x, handed[1]))
    done = [None, finish(1, *last)]

    small = {"lower_bounds": g["lbs"], "pre_norm_g": g["pre"], "hgrn_norm_g": g["hgn"], "fox_f_bias": g["bias"],
             "pool_w": g["pool_w"], "pool_scale": g["pool_scale"], "post_norm_g": g["post"]}
    packet = _pack(small)
    nrows = packet.shape[0]
    packet = packet.at[nrows - 1].set(sq[0])
    gsum = _all_reduce_small(packet)
    loss = gsum[nrows - 1, 0] * (0.5 / D_MODEL)

    weights = {"lower_bounds": lower_bounds, "pre_norm_g": pre_norm_g, "hgrn_norm_g": hgrn_norm_g,
               "fox_f_bias": fox_f_bias, "pool_w": pool_w, "pool_scale": pool_scale, "post_norm_g": post_norm_g}
    moments_m = {"lower_bounds": m_lower_bounds, "pre_norm_g": m_pre_norm_g, "hgrn_norm_g": m_hgrn_norm_g,
                 "fox_f_bias": m_fox_f_bias, "pool_w": m_pool_w, "pool_scale": m_pool_scale, "post_norm_g": m_post_norm_g}
    moments_v = {"lower_bounds": v_lower_bounds, "pre_norm_g": v_pre_norm_g, "hgrn_norm_g": v_hgrn_norm_g,
                 "fox_f_bias": v_fox_f_bias, "pool_w": v_pool_w, "pool_scale": v_pool_scale, "post_norm_g": v_post_norm_g}
    gp, dp, mp, vp = _small_update(gsum, lower_bounds, _pack(weights), _pack(moments_m), _pack(moments_v))
    gs, _ = _unpack(gp, weights)
    ds, _ = _unpack(dp, weights)
    ms, _ = _unpack(mp, weights)
    vs, _ = _unpack(vp, weights)

    first_layer, _ = lax.optimization_barrier((handed[0], (done[1], gp, dp, mp, vp)))
    done[0] = finish(0, *first_layer)
    halves_of = lambda j, side: jnp.stack([done[l][j][side] for l in range(DEPTH)])
    grad_w_in, d_in, m_in, v_in = _adamw(w_in, halves_of(0, 0), halves_of(0, 1), m_w_in, v_w_in, "adamw_w_in")
    grad_w_out, d_out, m_out, v_out = _adamw(w_out, halves_of(1, 0), halves_of(1, 1), m_w_out, v_w_out, "adamw_w_out")

    def ordered(s, big_in, big_out):
        return (s["lower_bounds"], s["pre_norm_g"], big_in, s["hgrn_norm_g"], s["fox_f_bias"], s["pool_w"],
                s["pool_scale"], big_out, s["post_norm_g"])

    return (loss, grad_x, *ordered(gs, grad_w_in, grad_w_out), *ordered(ds, d_in, d_out),
            *ordered(ms, m_in, m_out), *ordered(vs, v_in, v_out))
```

```python
import numpy as np
import jax
import jax.numpy as jnp
from jax import lax
from jax.experimental import pallas as pl
from jax.experimental.pallas import tpu as pltpu
from jax.experimental.pallas import tpu_sc as plsc

F32 = jnp.float32
BF16 = jnp.bfloat16
HI = lax.Precision.HIGHEST
MESH = pl.DeviceIdType.MESH

NORM_EPS = 1e-6
MASK_VALUE = -1e30
TINY = 1e-30
ADAM_LR, ADAM_B1, ADAM_B2, ADAM_EPS, ADAM_WD, ADAM_STEP = 0.001, 0.9, 0.999, 1e-08, 0.01, 10

D_MODEL = 1024
DEPTH = 2
N_CHIPS = 4
CHUNK = 64
LANES = 128
HGRN_W, POOL_W, FOX_W, FOX_HEADS = 256, 256, 512, 8
POOL_WINDOWS = (2, 4, 8, 16)
POOL_HALO = 16
IN_WIDTH = 3592
SHARD_W = IN_WIDTH // N_CHIPS
A_W, B_W, C_W, F_W = 1024, 512, 2048, 128
E_INT = A_W + B_W + C_W + F_W
B_BLK = A_W // 512
C_BLK0 = (A_W + B_W) // 512
F_BLK = (A_W + B_W + C_W) // 128


def _segments():
    segs = []
    for hp in range(2):
        for part in range(4):
            segs.append((part * 256 + hp * 128, 128))
    segs.append((1024, 256))
    segs.append((1280, 256))
    for hp in range(4):
        for part in range(4):
            segs.append((1536 + part * 512 + hp * 128, 128))
    segs.append((3584, 8))
    return segs


_SEGS = _segments()


def _internal_from_shards(shards):
    parts = []
    for s, n in _SEGS:
        while n > 0:
            q, r = divmod(s, SHARD_W)
            take = min(n, SHARD_W - r)
            parts.append(shards[q][..., r:r + take])
            s, n = s + take, n - take
    parts.append(jnp.zeros(shards[0].shape[:-1] + (E_INT - IN_WIDTH,), shards[0].dtype))
    return jnp.concatenate(parts, axis=-1)


def _shards_from_internal(w):
    offs, o = [], 0
    for s, n in _SEGS:
        offs.append((s, o, n))
        o += n
    blocks = []
    for q in range(N_CHIPS):
        lo, hi = SHARD_W * q, SHARD_W * (q + 1)
        parts = [w[..., o + max(lo, s) - s:o + min(hi, s + n) - s] for s, o, n in sorted(offs) if s < hi and s + n > lo]
        blocks.append(jnp.concatenate(parts, axis=-1))
    return jnp.stack(blocks)


def _cparams(sem=None, vmem_mb=48):
    kw = dict(vmem_limit_bytes=vmem_mb * 1024 * 1024)
    if sem is not None:
        kw["dimension_semantics"] = sem
    return pltpu.CompilerParams(**kw)


def _sig(x):
    return 1.0 / (1.0 + jnp.exp(-x))


def _silu(x):
    return x * _sig(x)


def _dsilu(x):
    s = _sig(x)
    return s * (1.0 + x * (1.0 - s))


def _rstd(x):
    return lax.rsqrt(jnp.mean(x * x, axis=-1, keepdims=True) + NORM_EPS)


def _dot(a, b, dims, **kw):
    return lax.dot_general(a, b, (dims, ((), ())), preferred_element_type=F32, **kw)


NN = ((1,), (0,))
NT = ((1,), (1,))
TN = ((0,), (0,))


def _iota(shape, dim):
    return lax.broadcasted_iota(jnp.int32, shape, dim)


def _lbs_fwd(lower_bounds):
    def body(a_ref, o_ref):
        a = a_ref[...]
        a0, a1 = a[0:1], a[1:2]
        m = jnp.maximum(a0, a1)
        e0, e1 = jnp.exp(a0 - m), jnp.exp(a1 - m)
        p0, p1 = e0 / (e0 + e1), e1 / (e0 + e1)
        o_ref[...] = jnp.concatenate([p0 - p0, (p0 + p1) - p0], axis=0)

    return pl.pallas_call(body, out_shape=jax.ShapeDtypeStruct(lower_bounds.shape, F32), name="lbs_fwd")(lower_bounds)


def _inproj_fwd(x2, g_row, w_int, name):
    n, d = x2.shape
    e = w_int.shape[1]
    tm = min(512, n)

    def body(x_ref, g_ref, w_ref, o_ref):
        x = x_ref[...]
        h = (x * _rstd(x) * g_ref[...]).astype(BF16)
        o_ref[...] = jnp.dot(h, w_ref[...], preferred_element_type=F32)

    return pl.pallas_call(
        body, grid=(n // tm,),
        in_specs=[pl.BlockSpec((tm, d), lambda i: (i, 0)), pl.BlockSpec((1, d), lambda i: (0, 0)),
                  pl.BlockSpec((d, e), lambda i: (0, 0))],
        out_specs=pl.BlockSpec((tm, e), lambda i: (i, 0)),
        out_shape=jax.ShapeDtypeStruct((n, e), F32),
        compiler_params=_cparams(("parallel",)), name=name)(x2, g_row, w_int)


def _hgrn_gates(a, lb):
    qa, z = a[:, 0:128], a[:, 128:256]
    sg, sgn = _sig(z), _sig(-z)
    fg = lb + (1.0 - lb) * sg
    lf = jnp.log(jnp.maximum(fg, TINY))
    kk = (1.0 - lb) * sgn
    return qa * _sig(qa), kk, lf, sg, sgn, fg


N_LEVELS = 6


def _hgrn_tables():
    t = np.arange(LANES)
    j = np.arange(LANES)[None, :]
    same_chunk = (t[:, None] // CHUNK) == (j // CHUNK)
    w = np.zeros((2 + N_LEVELS, LANES, LANES), np.float32)
    w[0] = same_chunk & (j <= t[:, None])
    w[1] = same_chunk & (j > t[:, None])
    maskf = np.zeros((N_LEVELS, LANES, LANES), np.float32)
    rightf = np.zeros((N_LEVELS, LANES, LANES), np.float32)
    for li in range(N_LEVELS):
        m = (CHUNK // 2) >> li
        start = t - (t % (2 * m))
        right = (t % (2 * m)) >= m
        first = np.where(right, start + m, t + 1)
        last = np.where(right, t, start + m - 1)
        w[2 + li] = (j >= first[:, None]) & (j <= last[:, None])
        maskf[li] = (t[:, None] // (2 * m)) == (j // (2 * m))
        rightf[li] = right[:, None]
    w = w[:-1]
    return jnp.asarray(w.reshape(-1, LANES), BF16), jnp.asarray(np.tile(maskf, (1, 2, 1))), jnp.asarray(rightf)


def _split(x, n):
    parts = []
    for _ in range(n - 1):
        p = x.astype(BF16)
        parts.append(p)
        x = x - p.astype(F32)
    parts.append(x.astype(BF16))
    return parts


def _exact_dot(w, parts):
    acc = jnp.dot(w, parts[0], preferred_element_type=F32)
    for p in parts[1:]:
        acc = acc + jnp.dot(w, p, preferred_element_type=F32)
    return acc


def _head_sums(v, ones_blk, n=2):
    parts = _split(v, n)
    acc = jnp.dot(parts[0], ones_blk, preferred_element_type=F32)
    for p in parts[1:]:
        acc = acc + jnp.dot(p, ones_blk, preferred_element_type=F32)
    return acc


def _hgrn_consts():
    r, c = _iota((LANES, LANES), 0), _iota((LANES, LANES), 1)
    ones_blk = ((r // CHUNK) == (c // CHUNK)).astype(BF16)
    eye2 = (_iota((2 * LANES, LANES), 0) % LANES) == _iota((2 * LANES, LANES), 1)
    first = _iota((1, LANES), 1) < CHUNK
    return eye2, ones_blk, jnp.ones((LANES, LANES), BF16), first


def _stack_heads(v, first):
    return jnp.concatenate([jnp.where(first, v, 0.0), jnp.where(first, 0.0, v)], axis=0)


def _pick_heads(v2, first):
    return jnp.where(first, v2[:LANES], v2[LANES:])


def _hgrn_levels(qq, kk, lf, zall, mk_ref, rt_ref, first, d_att=None):
    att = jnp.zeros((2 * LANES, LANES), F32)
    dq = dk = db = jnp.zeros((LANES, LANES), F32)
    for li in range(N_LEVELS):
        rt = rt_ref[li]
        e = jnp.exp(zall[(2 + li) * LANES:(3 + li) * LANES] if li < N_LEVELS - 1 else lf * rt)
        mk = mk_ref[li]
        qef, kef = e * rt, e * (1.0 - rt)
        qe, ke = (qq * qef).astype(BF16), (kk * kef).astype(BF16)
        qe2 = _stack_heads(qe, first)
        att = att + _dot(qe2, ke, NT) * mk
        if d_att is not None:
            dam = (d_att * mk).astype(BF16)
            dqe = _pick_heads(jnp.dot(dam, ke, preferred_element_type=F32), first)
            dke = _dot(dam, qe2, TN)
            dq = dq + dqe * qef
            dk = dk + dke * kef
            db = db + (dqe * qe.astype(F32) - dke * ke.astype(F32))
    return att, dq, dk, db


def _hgrn_fwd(proj3, lbs_row, gn_row, name):
    bsz, t, _ = proj3.shape
    nt = t // LANES
    w_all, maskf, rightf = _hgrn_tables()

    def body(a_ref, lb_ref, gn_ref, w_ref, mk_ref, rt_ref, og_ref, or_ref, st_ref):
        lb = lb_ref[...]
        gn = gn_ref[...]
        eye2, ones_blk, ones_all, first = _hgrn_consts()

        def tile(i, carry):
            r0 = pl.multiple_of(i * LANES, LANES)
            a = a_ref[pl.ds(r0, LANES), :]
            qq, kk, lf, _, _, _ = _hgrn_gates(a, lb)
            va, ga = a[:, 256:384], a[:, 384:512]
            parts = _split(lf, 3)
            zall = _exact_dot(w_ref[...], parts)
            eb, ee = jnp.exp(zall[0:LANES]), jnp.exp(zall[LANES:2 * LANES])
            vb = va.astype(BF16)
            att, _, _, _ = _hgrn_levels(qq, kk, lf, zall, mk_ref, rt_ref, first)
            diag = _head_sums(_stack_heads(qq * kk, first), ones_all)
            a2 = (att + jnp.where(eye2, diag, 0.0)).astype(BF16)
            o_in = _pick_heads(jnp.dot(a2, vb, preferred_element_type=F32), first)
            qeb, keb = (qq * eb).astype(BF16), (kk * ee).astype(BF16)
            new_s, o_heads = [], []
            for h in range(2):
                hs = slice(CHUNK * h, CHUNK * (h + 1))
                o_h = o_in[:, hs]
                st = carry[h]
                chunks = []
                for c in range(2):
                    rc = slice(CHUNK * c, CHUNK * (c + 1))
                    st_ref[h, 2 * i + c] = st
                    chunks.append(o_h[rc] + _dot(qeb[rc, hs], st.astype(BF16), NT))
                    ebl = eb[CHUNK * (c + 1) - 1:CHUNK * (c + 1), hs]
                    st = st * ebl + _dot(vb[rc, hs], keb[rc, hs], TN)
                new_s.append(st)
                o_heads.append(jnp.concatenate(chunks, axis=0))
            o = jnp.concatenate(o_heads, axis=1)
            ms = _head_sums(o * o, ones_blk) * (1.0 / CHUNK)
            or_ref[pl.ds(r0, LANES), :] = o
            og_ref[pl.ds(r0, LANES), :] = (o * lax.rsqrt(ms + NORM_EPS) * gn * _silu(ga)).astype(BF16)
            return tuple(new_s)

        zero = jnp.zeros((CHUNK, CHUNK), F32)
        per_step = 4 if nt % 4 == 0 else 2

        def step(i, carry):
            for k in range(per_step):
                carry = tile(per_step * i + k, carry)
            return carry

        lax.fori_loop(0, nt // per_step, step, (zero, zero))

    out = jax.ShapeDtypeStruct((bsz, t, HGRN_W), F32)
    row = pl.BlockSpec((1, 128), lambda b, p: (0, p))
    return pl.pallas_call(
        body, grid=(bsz, 2),
        in_specs=[pl.BlockSpec((None, t, 512), lambda b, p: (b, 0, p)), row, row,
                  pl.BlockSpec(w_all.shape, lambda b, p: (0, 0)),
                  pl.BlockSpec(maskf.shape, lambda b, p: (0, 0, 0)),
                  pl.BlockSpec(rightf.shape, lambda b, p: (0, 0, 0))],
        out_specs=[pl.BlockSpec((None, t, 128), lambda b, p: (b, 0, p)),
                   pl.BlockSpec((None, t, 128), lambda b, p: (b, 0, p)),
                   pl.BlockSpec((None, 2, t // CHUNK, CHUNK, CHUNK), lambda b, p: (b, p, 0, 0, 0))],
        out_shape=[jax.ShapeDtypeStruct((bsz, t, HGRN_W), BF16), out,
                   jax.ShapeDtypeStruct((bsz, 4, t // CHUNK, CHUNK, CHUNK), F32)],
        compiler_params=_cparams(("parallel", "parallel")), name=name)(proj3, lbs_row, gn_row, w_all, maskf, rightf)


def _hgrn_bwd(proj3, o_raw, dmixed, states, lbs_row, gn_row, name):
    bsz, t, _ = proj3.shape
    nt = t // LANES
    nchunk = t // CHUNK
    w_all, maskf, rightf = _hgrn_tables()

    def body(a_ref, or_ref, do_ref, s_sc, lb_ref, gn_ref, w_ref, mk_ref, rt_ref, da_ref, dgn_ref, dlb_ref):
        lb = lb_ref[...]
        gn = gn_ref[...]
        eye2, ones_blk, ones_all, first = _hgrn_consts()
        r_i, c_i = _iota((LANES, LANES), 0), _iota((LANES, LANES), 1)
        suffix = ((c_i >= r_i) & ((r_i // CHUNK) == (c_i // CHUNK))).astype(BF16)
        row64 = _iota((LANES, CHUNK), 0)
        zero = jnp.zeros((CHUNK, CHUNK), F32)

        def bwd_tile(k, carry):
            dst0, dst1, dgn_acc, dlb_acc = carry
            i = nt - 1 - k
            r0 = pl.multiple_of(i * LANES, LANES)
            a = a_ref[pl.ds(r0, LANES), :]
            qa, ga = a[:, 0:128], a[:, 384:512]
            qq, kk, lf, sg, sgn, fg = _hgrn_gates(a, lb)
            parts = _split(lf, 3)
            zall = _exact_dot(w_ref[...], parts)
            eb, ee = jnp.exp(zall[0:LANES]), jnp.exp(zall[LANES:2 * LANES])
            vb = a[:, 256:384].astype(BF16)
            oraw = or_ref[pl.ds(r0, LANES), :]
            dout = do_ref[pl.ds(r0, LANES), :]
            r = lax.rsqrt(_head_sums(oraw * oraw, ones_blk) * (1.0 / CHUNK) + NORM_EPS)
            xn = oraw * r
            dga = dout * (xn * gn) * _dsilu(ga)
            don = dout * _silu(ga)
            dgn_acc = dgn_acc + jnp.sum(don * xn, axis=0, keepdims=True)
            dxn = don * gn
            do = r * (dxn - xn * (_head_sums(dxn * xn, ones_blk) * (1.0 / CHUNK)))
            dob = do.astype(BF16)
            do2 = _stack_heads(dob, first)
            d_att = _dot(do2, vb, NT)
            att, dq, dk, db_lv = _hgrn_levels(qq, kk, lf, zall, mk_ref, rt_ref, first, d_att)
            a2 = att + jnp.where(eye2, _head_sums(_stack_heads(qq * kk, first), ones_all), 0.0)
            dv_in = _dot(a2.astype(BF16), do2, TN)
            ddiag = _pick_heads(_head_sums(jnp.where(eye2, d_att, 0.0), ones_all), first)
            dq_in, dk_in = dq + ddiag * kk, dk + ddiag * qq
            qe_f, ke_f = qq * eb, kk * ee
            qeb, keb = qe_f.astype(BF16), ke_f.astype(BF16)
            new_ds, dq_h, dk_h, dv_h, dbl_h = [], [], [], [], []
            for h in range(2):
                hs = slice(CHUNK * h, CHUNK * (h + 1))
                dv, dq_i, dk_i = dv_in[:, hs], dq_in[:, hs], dk_in[:, hs]
                dst = (dst0, dst1)[h]
                dq_c, dk_c, dv_c, dbl_c = [None, None], [None, None], [None, None], [None, None]
                for c in (1, 0):
                    rc = slice(CHUNK * c, CHUNK * (c + 1))
                    st_n = s_sc[h, 2 * i + c]
                    ebl = eb[CHUNK * (c + 1) - 1:CHUNK * (c + 1), hs]
                    dstb = dst.astype(BF16)
                    dv_c[c] = _dot(keb[rc, hs], dstb, NT)
                    dke = jnp.dot(vb[rc, hs], dstb, preferred_element_type=F32)
                    dqe = jnp.dot(dob[rc, hs], st_n.astype(BF16), preferred_element_type=F32)
                    dbl_c[c] = (jnp.sum(dst * st_n, axis=0, keepdims=True) * ebl
                                + jnp.sum(dke * ke_f[rc, hs], axis=0, keepdims=True))
                    dq_c[c], dk_c[c] = dqe * eb[rc, hs], dke * ee[rc, hs]
                    dst = dst * ebl + _dot(dob[rc, hs], qeb[rc, hs], TN)
                new_ds.append(dst)
                dq_x, dk_x = jnp.concatenate(dq_c, axis=0), jnp.concatenate(dk_c, axis=0)
                dq_h.append(dq_i + dq_x)
                dk_h.append(dk_i + dk_x)
                dv_h.append(dv + jnp.concatenate(dv_c, axis=0))
                dbl_h.append(qq[:, hs] * dq_x - kk[:, hs] * dk_x
                             + jnp.where(row64 == CHUNK - 1, dbl_c[0], 0.0) + jnp.where(row64 == LANES - 1, dbl_c[1], 0.0))
            dqq = jnp.concatenate(dq_h, axis=1)
            dkk = jnp.concatenate(dk_h, axis=1)
            dvv = jnp.concatenate(dv_h, axis=1)
            db = db_lv + jnp.concatenate(dbl_h, axis=1)
            dlf = _exact_dot(suffix, _split(db, 3))
            dqa = dqq * _dsilu(qa)
            dfg = jnp.where(fg > TINY, dlf / fg, 0.0)
            dz = (dfg - dkk) * (1.0 - lb) * sg * sgn
            dlb_acc = dlb_acc + jnp.sum(dfg * (1.0 - sg) - dkk * sgn, axis=0, keepdims=True)
            da_ref[pl.ds(r0, LANES), :] = jnp.concatenate([dqa, dz, dvv, dga], axis=1).astype(BF16)
            return new_ds[0], new_ds[1], dgn_acc, dlb_acc

        zrow = jnp.zeros((1, LANES), F32)
        per_step = 4 if nt % 4 == 0 else 2

        def step(k, carry):
            for r in range(per_step):
                carry = bwd_tile(per_step * k + r, carry)
            return carry

        _, _, dgn_acc, dlb_acc = lax.fori_loop(0, nt // per_step, step, (zero, zero, zrow, zrow))
        dgn_ref[...] = jnp.broadcast_to(dgn_acc, (8, LANES))
        dlb_ref[...] = jnp.broadcast_to(dlb_acc, (8, LANES))

    rows = jax.ShapeDtypeStruct((bsz, 8, HGRN_W), F32)
    row = pl.BlockSpec((1, 128), lambda b, p: (0, p))
    blk = pl.BlockSpec((None, t, 128), lambda b, p: (b, 0, p))
    return pl.pallas_call(
        body, grid=(bsz, 2),
        in_specs=[pl.BlockSpec((None, t, 512), lambda b, p: (b, 0, p)), blk, blk,
                  pl.BlockSpec((None, 2, nchunk, CHUNK, CHUNK), lambda b, p: (b, p, 0, 0, 0)), row, row,
                  pl.BlockSpec(w_all.shape, lambda b, p: (0, 0)),
                  pl.BlockSpec(maskf.shape, lambda b, p: (0, 0, 0)),
                  pl.BlockSpec(rightf.shape, lambda b, p: (0, 0, 0))],
        out_specs=[pl.BlockSpec((None, t, 512), lambda b, p: (b, 0, p)),
                   pl.BlockSpec((None, 8, 128), lambda b, p: (b, 0, p)),
                   pl.BlockSpec((None, 8, 128), lambda b, p: (b, 0, p))],
        out_shape=[jax.ShapeDtypeStruct((bsz, t, A_W), BF16), rows, rows],
        compiler_params=_cparams(("parallel", "parallel")), name=name)(
            proj3, o_raw, dmixed, states, lbs_row, gn_row, w_all, maskf, rightf)


def _pool_tt(t):
    return min(256, t)


def _window_select(s2, s4, s8, s16, lane):
    return jnp.where(lane < 64, s2, jnp.where(lane < 128, s4, jnp.where(lane < 192, s8, s16)))


def _pool_counts(t0, tt):
    lane = _iota((tt, POOL_W), 1)
    tpos = (_iota((tt, POOL_W), 0) + t0 + 1).astype(F32)
    win = jnp.where(lane < 64, 2.0, jnp.where(lane < 128, 4.0, jnp.where(lane < 192, 8.0, 16.0)))
    return 1.0 / jnp.minimum(tpos, win), lane


def _pooled_tile(upad_ref, i, tt):
    r0 = pl.multiple_of(i * tt, 8)
    cat = upad_ref[pl.ds(r0, tt + POOL_HALO), :]
    s2 = cat + pltpu.roll(cat, 1, 0)
    s4 = s2 + pltpu.roll(s2, 2, 0)
    s8 = s4 + pltpu.roll(s4, 4, 0)
    s16 = s8 + pltpu.roll(s8, 8, 0)
    inv, lane = _pool_counts(i * tt, tt)
    sel = _window_select(s2[POOL_HALO:], s4[POOL_HALO:], s8[POOL_HALO:], s16[POOL_HALO:], lane)
    return sel * inv - cat[POOL_HALO:], inv, lane


def _pool_fwd(proj3, wbd, scale_row, name):
    bsz, t, _ = proj3.shape
    tt = _pool_tt(t)

    def body(p_ref, w_ref, sc_ref, o_ref, upad):
        upad[0:POOL_HALO, :] = jnp.zeros((POOL_HALO, POOL_W), F32)
        upad[POOL_HALO:, :] = p_ref[:, 0:POOL_W]
        w = w_ref[...]
        sc = sc_ref[...]

        def tile(i, c):
            pooled, _, _ = _pooled_tile(upad, i, tt)
            r0 = pl.multiple_of(i * tt, 8)
            g = p_ref[pl.ds(r0, tt), POOL_W:2 * POOL_W]
            pre = jnp.dot(pooled.astype(BF16), w, preferred_element_type=F32)
            o_ref[pl.ds(r0, tt), :] = (pre * sc * _silu(g)).astype(BF16)
            return c

        lax.fori_loop(0, t // tt, tile, 0)

    return pl.pallas_call(
        body, grid=(bsz,),
        in_specs=[pl.BlockSpec((None, t, 512), lambda b: (b, 0, B_BLK)),
                  pl.BlockSpec((POOL_W, POOL_W), lambda b: (0, 0)),
                  pl.BlockSpec((1, POOL_W), lambda b: (0, 0))],
        out_specs=pl.BlockSpec((None, t, POOL_W), lambda b: (b, 0, 0)),
        out_shape=jax.ShapeDtypeStruct((bsz, t, POOL_W), BF16),
        scratch_shapes=[pltpu.VMEM((t + POOL_HALO, POOL_W), F32)],
        compiler_params=_cparams(("parallel",)), name=name)(proj3, wbd, scale_row)


def _pool_bwd(proj3, dmixed, wbd, scale_row, name):
    bsz, t, _ = proj3.shape
    tt = _pool_tt(t)

    def body(p_ref, do_ref, w_ref, sc_ref, db_ref, dsc_ref, dw_ref, upad, epad):
        upad[0:POOL_HALO, :] = jnp.zeros((POOL_HALO, POOL_W), F32)
        upad[POOL_HALO:, :] = p_ref[:, 0:POOL_W]
        epad[t:, :] = jnp.zeros((POOL_HALO, POOL_W), F32)
        w = w_ref[...]
        sc = sc_ref[...]

        def tile(i, carry):
            dsc_acc, dw_acc = carry
            pooled, inv, _ = _pooled_tile(upad, i, tt)
            r0 = pl.multiple_of(i * tt, 8)
            g = p_ref[pl.ds(r0, tt), POOL_W:2 * POOL_W]
            dout = do_ref[pl.ds(r0, tt), :]
            pb = pooled.astype(BF16)
            pre = jnp.dot(pb, w, preferred_element_type=F32)
            t1 = dout * _silu(g)
            dsc_acc = dsc_acc + jnp.sum(t1 * pre, axis=0, keepdims=True)
            dpre = (t1 * sc).astype(BF16)
            db_ref[pl.ds(r0, tt), POOL_W:2 * POOL_W] = (dout * pre * sc * _dsilu(g)).astype(BF16)
            dw_acc = dw_acc + _dot(pb, dpre, TN)
            dpooled = _dot(dpre, w, NT)
            epad[pl.ds(r0, tt), :] = dpooled * inv
            return dsc_acc, dw_acc

        dsc_acc, dw_acc = lax.fori_loop(0, t // tt, tile, (jnp.zeros((1, POOL_W), F32), jnp.zeros((POOL_W, POOL_W), F32)))
        dsc_ref[...] = jnp.broadcast_to(dsc_acc, (8, POOL_W))
        dw_ref[...] = dw_acc

        def tile2(i, c):
            r0 = pl.multiple_of(i * tt, 8)
            n = tt + POOL_HALO
            cat = epad[pl.ds(r0, n), :]
            s2 = cat + pltpu.roll(cat, n - 1, 0)
            s4 = s2 + pltpu.roll(s2, n - 2, 0)
            s8 = s4 + pltpu.roll(s4, n - 4, 0)
            s16 = s8 + pltpu.roll(s8, n - 8, 0)
            inv, lane = _pool_counts(i * tt, tt)
            sel = _window_select(s2[:tt], s4[:tt], s8[:tt], s16[:tt], lane)
            db_ref[pl.ds(r0, tt), 0:POOL_W] = (sel - cat[:tt] / inv).astype(BF16)
            return c

        lax.fori_loop(0, t // tt, tile2, 0)

    return pl.pallas_call(
        body, grid=(bsz,),
        in_specs=[pl.BlockSpec((None, t, 512), lambda b: (b, 0, B_BLK)),
                  pl.BlockSpec((None, t, POOL_W), lambda b: (b, 0, 1)),
                  pl.BlockSpec((POOL_W, POOL_W), lambda b: (0, 0)),
                  pl.BlockSpec((1, POOL_W), lambda b: (0, 0))],
        out_specs=[pl.BlockSpec((None, t, 512), lambda b: (b, 0, 0)),
                   pl.BlockSpec((None, 8, POOL_W), lambda b: (b, 0, 0)),
                   pl.BlockSpec((None, POOL_W, POOL_W), lambda b: (b, 0, 0))],
        out_shape=[jax.ShapeDtypeStruct((bsz, t, B_W), BF16), jax.ShapeDtypeStruct((bsz, 8, POOL_W), F32),
                   jax.ShapeDtypeStruct((bsz, POOL_W, POOL_W), F32)],
        scratch_shapes=[pltpu.VMEM((t + POOL_HALO, POOL_W), F32), pltpu.VMEM((t + POOL_HALO, POOL_W), F32)],
        compiler_params=_cparams(("parallel",)), name=name)(proj3, dmixed, wbd, scale_row)


def _head_select_rows(hp):
    r, c = _iota((8, LANES), 0), _iota((8, LANES), 1)
    return ((r < 2) & (c == 2 * hp + r)).astype(F32)


def _foxgate_fwd(proj3, bias_row, name):
    bsz, t, _ = proj3.shape
    nt = t // LANES

    def body(f_ref, b_ref, cn_ref, ct_ref):
        bias = b_ref[...]
        i, j = _iota((LANES, LANES), 0), _iota((LANES, LANES), 1)
        lower = (j <= i).astype(BF16)
        spread = (_iota((LANES, FOX_W), 0) == _iota((LANES, FOX_W), 1) // 64).astype(BF16)
        select = [_head_select_rows(hp).astype(BF16) for hp in range(4)]
        offset = jnp.zeros((1, LANES), F32)
        for k in range(nt):
            rows = slice(k * LANES, (k + 1) * LANES)
            xg = f_ref[rows, :] + bias
            lf = jnp.minimum(xg, 0.0) - jnp.log(1.0 + jnp.exp(-jnp.abs(xg)))
            c = _exact_dot(lower, _split(lf, 3)) + offset
            offset = c[LANES - 1:LANES, :]
            parts = _split(c, 3)
            cn_ref[rows, :] = _head_sums(c, spread, 3)
            for hp in range(4):
                acc = _dot(select[hp], parts[0], NT)
                for p in parts[1:]:
                    acc = acc + _dot(select[hp], p, NT)
                ct_ref[hp, :, rows] = acc

    return pl.pallas_call(
        body, grid=(bsz,),
        in_specs=[pl.BlockSpec((None, t, 128), lambda b: (b, 0, F_BLK)), pl.BlockSpec((1, 128), lambda b: (0, 0))],
        out_specs=[pl.BlockSpec((None, t, FOX_W), lambda b: (b, 0, 0)),
                   pl.BlockSpec((None, 4, 8, t), lambda b: (b, 0, 0, 0))],
        out_shape=[jax.ShapeDtypeStruct((bsz, t, FOX_W), F32), jax.ShapeDtypeStruct((bsz, 4, 8, t), F32)],
        compiler_params=_cparams(("parallel",)), name=name)(proj3, bias_row)


def _foxgate_bwd(proj3, dc_nat, bias_row, name):
    bsz, t, _ = proj3.shape
    nt = t // LANES

    def body(f_ref, dc_ref, b_ref, df_ref, dbias_ref, run_sc):
        bias = b_ref[...]
        i, j = _iota((LANES, LANES), 0), _iota((LANES, LANES), 1)
        upper = (j >= i).astype(F32)
        valid = _iota((1, LANES), 1) < FOX_HEADS
        run_sc[...] = jnp.zeros((8, LANES), F32)
        dbias_ref[...] = jnp.zeros((8, LANES), F32)

        def tile(k, c):
            r0 = pl.multiple_of((nt - 1 - k) * LANES, LANES)
            dc = dc_ref[pl.ds(r0, LANES), :] + jnp.where(i == LANES - 1, run_sc[0:1, :], 0.0)
            dlf = jnp.dot(upper, dc, precision=HI, preferred_element_type=F32)
            xg = f_ref[pl.ds(r0, LANES), :] + bias
            df = jnp.where(valid, dlf * _sig(-xg), 0.0)
            df_ref[pl.ds(r0, LANES), :] = df.astype(BF16)
            run_sc[...] = dlf[0:8, :]
            dbias_ref[...] += jnp.sum(df, axis=0, keepdims=True)
            return c

        lax.fori_loop(0, nt, tile, 0)

    blk = pl.BlockSpec((None, t, 128), lambda b: (b, 0, 0))
    return pl.pallas_call(
        body, grid=(bsz,),
        in_specs=[pl.BlockSpec((None, t, 128), lambda b: (b, 0, F_BLK)), blk, pl.BlockSpec((1, 128), lambda b: (0, 0))],
        out_specs=[blk, pl.BlockSpec((None, 8, 128), lambda b: (b, 0, 0))],
        out_shape=[jax.ShapeDtypeStruct((bsz, t, F_W), BF16), jax.ShapeDtypeStruct((bsz, 8, 128), F32)],
        scratch_shapes=[pltpu.VMEM((8, LANES), F32)],
        compiler_params=_cparams(("parallel",)), name=name)(proj3, dc_nat, bias_row)


def _fox_tile(t):
    return min(256, t)


def _fox_fwd(proj3, c_nat, c_t, name):
    bsz, t, _ = proj3.shape
    tq = tk = min(2 * _fox_tile(t), t)
    nq = t // tq

    def body(q_ref, kv_ref, cn_ref, ct_ref, og_ref, or_ref, lse_ref):
        i = pl.program_id(2)
        qblk = q_ref[...]
        first = _iota((1, 128), 1) < 64
        qv = qblk[:, 0:128] * 0.125
        qm = [jnp.where(first, qv, 0.0).astype(BF16), jnp.where(first, 0.0, qv).astype(BF16)]
        cqs = [cn_ref[:, 0:1], cn_ref[:, 64:65]]

        def absorb(c0, nc, r_lo, state, masked):
            nr = tq - r_lo
            kb = kv_ref[pl.ds(c0, nc), 128:256].astype(BF16)
            vblk = kv_ref[pl.ds(c0, nc), 256:384]
            vx = [jnp.where(first, vblk, 1.0).astype(BF16), jnp.where(first, 1.0, vblk).astype(BF16)]
            new = []
            for h in range(2):
                m_all, acc_all = state[2 * h], state[2 * h + 1]
                m, acc = m_all[r_lo:], acc_all[r_lo:]
                s = _dot(qm[h][r_lo:], kb, NT) + (cqs[h][r_lo:] - ct_ref[h:h + 1, pl.ds(c0, nc)])
                if masked:
                    s = jnp.where(_iota((nr, nc), 0) + (i * tq + r_lo) >= _iota((nr, nc), 1) + c0, s, MASK_VALUE)
                m_new = jnp.maximum(m, jnp.max(s, axis=1, keepdims=True))
                p = jnp.exp(s - m_new).astype(BF16)
                acc_new = jnp.exp(m - m_new) * acc + jnp.dot(p, vx[h], preferred_element_type=F32)
                if r_lo:
                    m_new = jnp.concatenate([m_all[:r_lo], m_new], axis=0)
                    acc_new = jnp.concatenate([acc_all[:r_lo], acc_new], axis=0)
                new += [m_new, acc_new]
            return tuple(new)

        init = (jnp.full((tq, 1), MASK_VALUE, F32), jnp.zeros((tq, 128), F32)) * 2
        state = lax.fori_loop(0, i, lambda j, state: absorb(pl.multiple_of(j * tk, tk), tk, 0, state, False), init)
        half = tk // 2
        diag = pl.multiple_of(i * tk, tk)
        state = absorb(diag, half, 0, state, True)
        m0, acc0, m1, acc1 = absorb(pl.multiple_of(diag + half, half), half, half, state, True)
        l0, l1 = pltpu.roll(acc0, 64, 1), pltpu.roll(acc1, 64, 1)
        o = jnp.where(first, acc0 / l0, acc1 / l1)
        or_ref[...] = o
        og_ref[...] = (o * _silu(qblk[:, 384:512])).astype(BF16)
        lse_ref[...] = jnp.where(first, m0 + jnp.log(l0), m1 + jnp.log(l1))

    out = jax.ShapeDtypeStruct((bsz, t, FOX_W), F32)
    blk = pl.BlockSpec((None, tq, 128), lambda b, p, i: (b, i, p))
    return pl.pallas_call(
        body, grid=(bsz, 4, nq),
        in_specs=[pl.BlockSpec((None, tq, 512), lambda b, p, i: (b, i, C_BLK0 + p)),
                  pl.BlockSpec((None, t, 512), lambda b, p, i: (b, 0, C_BLK0 + p)),
                  blk,
                  pl.BlockSpec((None, None, 8, t), lambda b, p, i: (b, p, 0, 0))],
        out_specs=[blk, blk, blk],
        out_shape=[jax.ShapeDtypeStruct((bsz, t, FOX_W), BF16), out, out],
        compiler_params=_cparams(("parallel", "parallel", "arbitrary")), name=name)(proj3, proj3, c_nat, c_t)


def _fox_bwd(proj3, o_raw, dmixed, lse, c_nat, c_t, name):
    bsz, t, _ = proj3.shape
    tq = tk = min(2 * _fox_tile(t), t)
    nq = t // tq

    def body(a_ref, or_ref, do_ref, lse_ref, cn_ref, ct_ref, dc_out, dct_out, drow_out, dq_sc, do_sc, dl_sc):
        def prep(i, c):
            r0 = pl.multiple_of(i * tq, tq)
            g = a_ref[pl.ds(r0, tq), 384:512]
            dout = do_ref[pl.ds(r0, tq), :]
            o = or_ref[pl.ds(r0, tq), :]
            dc_out[pl.ds(r0, tq), 384:512] = (dout * o * _dsilu(g)).astype(BF16)
            do = dout * _silu(g)
            do_sc[pl.ds(r0, tq), :] = do
            prod = do * o
            d0 = jnp.sum(prod[:, 0:64], axis=1, keepdims=True)
            d1 = jnp.sum(prod[:, 64:128], axis=1, keepdims=True)
            dl_sc[pl.ds(r0, tq), :] = jnp.concatenate([jnp.broadcast_to(d0, (tq, 64)), jnp.broadcast_to(d1, (tq, 64))], axis=1)
            dq_sc[pl.ds(r0, tq), :] = jnp.zeros((tq, 128), F32)
            drow_out[pl.ds(r0, tq), :] = jnp.zeros((tq, 128), F32)
            return c

        lax.fori_loop(0, nq, prep, 0)
        dct_out[...] = jnp.zeros((8, t), F32)

        first = _iota((1, 128), 1) < 64

        def heads(v):
            return [jnp.where(first, v, 0.0).astype(BF16), jnp.where(first, 0.0, v).astype(BF16)]

        def kv_tile(j, c):
            c0 = pl.multiple_of(j * tk, tk)
            kb = a_ref[pl.ds(c0, tk), 128:256].astype(BF16)
            vb = a_ref[pl.ds(c0, tk), 256:384].astype(BF16)
            cks = [ct_ref[h:h + 1, pl.ds(c0, tk)] for h in range(2)]

            def pair(r0, nr, nc, carry, masked):
                dk, dv, dcol0, dcol1 = carry
                kbs, vbs = kb[0:nc], vb[0:nc]
                qv = a_ref[pl.ds(r0, nr), 0:128] * 0.125
                do = do_sc[pl.ds(r0, nr), :]
                qb, dob = qv.astype(BF16), do.astype(BF16)
                qm, dom = heads(qv), heads(do)
                full, dcols, rsums = [], [], []
                for h in range(2):
                    lse_h = lse_ref[pl.ds(r0, nr), 64 * h:64 * h + 1]
                    dl_h = dl_sc[pl.ds(r0, nr), 64 * h:64 * h + 1]
                    cq = cn_ref[pl.ds(r0, nr), 64 * h:64 * h + 1]
                    p = jnp.exp(_dot(qm[h], kbs, NT) + (cq - cks[h][:, 0:nc]) - lse_h)
                    if masked:
                        p = jnp.where(_iota((nr, nc), 0) + r0 >= _iota((nr, nc), 1) + c0, p, 0.0)
                    ds = p * (_dot(dom[h], vbs, NT) - dl_h)
                    dsb = ds.astype(BF16)
                    full.append((_dot(p.astype(BF16), dob, TN), _dot(dsb, qb, TN),
                                 jnp.dot(dsb, kbs, preferred_element_type=F32)))
                    dcols.append(jnp.sum(ds, axis=0, keepdims=True))
                    rsums.append(jnp.broadcast_to(jnp.sum(ds, axis=1, keepdims=True), (nr, 128)))
                dq_sc[pl.ds(r0, nr), :] += jnp.where(first, full[0][2], full[1][2]) * 0.125
                drow_out[pl.ds(r0, nr), :] += jnp.where(first, rsums[0], rsums[1])
                dk_new, dv_new = jnp.where(first, full[0][1], full[1][1]), jnp.where(first, full[0][0], full[1][0])
                if nc == tk:
                    return dk + dk_new, dv + dv_new, dcol0 - dcols[0], dcol1 - dcols[1]
                rows = lambda acc, new: jnp.concatenate([acc[0:nc] + new, acc[nc:]], axis=0)
                lanes = lambda acc, new: jnp.concatenate([acc[:, 0:nc] - new, acc[:, nc:]], axis=1)
                return rows(dk, dk_new), rows(dv, dv_new), lanes(dcol0, dcols[0]), lanes(dcol1, dcols[1])

            carry = (jnp.zeros((tk, 128), F32), jnp.zeros((tk, 128), F32), jnp.zeros((1, tk), F32), jnp.zeros((1, tk), F32))
            half = tq // 2
            carry = pair(c0, half, half, carry, True)
            carry = pair(pl.multiple_of(c0 + half, half), half, tk, carry, True)
            dk, dv, dcol0, dcol1 = lax.fori_loop(
                j + 1, nq, lambda i, carry: pair(pl.multiple_of(i * tq, tq), tq, tk, carry, False), carry)
            dct_out[0:1, pl.ds(c0, tk)] = dcol0
            dct_out[1:2, pl.ds(c0, tk)] = dcol1
            dc_out[pl.ds(c0, tk), 128:256] = dk.astype(BF16)
            dc_out[pl.ds(c0, tk), 256:384] = dv.astype(BF16)
            return c

        lax.fori_loop(0, t // tk, kv_tile, 0)
        dc_out[:, 0:128] = dq_sc[...].astype(BF16)

    blk = pl.BlockSpec((None, t, 128), lambda b, p: (b, 0, p))
    return pl.pallas_call(
        body, grid=(bsz, 4),
        in_specs=[pl.BlockSpec((None, t, 512), lambda b, p: (b, 0, C_BLK0 + p)),
                  blk,
                  pl.BlockSpec((None, t, 128), lambda b, p: (b, 0, 4 + p)),
                  blk, blk,
                  pl.BlockSpec((None, None, 8, t), lambda b, p: (b, p, 0, 0))],
        out_specs=[pl.BlockSpec((None, t, 512), lambda b, p: (b, 0, p)),
                   pl.BlockSpec((None, None, 8, t), lambda b, p: (b, p, 0, 0)), blk],
        out_shape=[jax.ShapeDtypeStruct((bsz, t, C_W), BF16), jax.ShapeDtypeStruct((bsz, 4, 8, t), F32),
                   jax.ShapeDtypeStruct((bsz, t, FOX_W), F32)],
        scratch_shapes=[pltpu.VMEM((t, 128), F32), pltpu.VMEM((t, 128), F32), pltpu.VMEM((t, 128), F32)],
        compiler_params=_cparams(("parallel", "parallel")), name=name)(proj3, o_raw, dmixed, lse, c_nat, c_t)


def _mix_tm(n):
    return min(512, n)


def _outproj_fwd(x2, oa, ob, oc, wo, g_row, name):
    n, d = x2.shape
    tm = _mix_tm(n)

    def body(x_ref, oa_ref, ob_ref, oc_ref, w_ref, g_ref, y_ref, xo_ref):
        y = (jnp.dot(oa_ref[...].astype(BF16), w_ref[0:256, :], preferred_element_type=F32)
             + jnp.dot(ob_ref[...].astype(BF16), w_ref[256:512, :], preferred_element_type=F32)
             + jnp.dot(oc_ref[...].astype(BF16), w_ref[512:1024, :], preferred_element_type=F32))
        y_ref[...] = y
        xo_ref[...] = x_ref[...] + y * _rstd(y) * g_ref[...]

    row = lambda w: pl.BlockSpec((tm, w), lambda i: (i, 0))
    out = jax.ShapeDtypeStruct((n, d), F32)
    return pl.pallas_call(
        body, grid=(n // tm,),
        in_specs=[row(d), row(256), row(256), row(512), pl.BlockSpec((d, d), lambda i: (0, 0)),
                  pl.BlockSpec((1, d), lambda i: (0, 0))],
        out_specs=[row(d), row(d)], out_shape=[out, out],
        compiler_params=_cparams(("parallel",)), name=name)(x2, oa, ob, oc, wo, g_row)


def _outproj_fwd_loss(x2, oa, ob, oc, wo, g_row, target2, name):
    n, d = x2.shape
    tm = _mix_tm(n)

    def body(x_ref, oa_ref, ob_ref, oc_ref, w_ref, g_ref, t_ref, y_ref, dx_ref, l_ref):
        y = (jnp.dot(oa_ref[...].astype(BF16), w_ref[0:256, :], preferred_element_type=F32)
             + jnp.dot(ob_ref[...].astype(BF16), w_ref[256:512, :], preferred_element_type=F32)
             + jnp.dot(oc_ref[...].astype(BF16), w_ref[512:1024, :], preferred_element_type=F32))
        y_ref[...] = y
        err = (x_ref[...] + y * _rstd(y) * g_ref[...]) - t_ref[...]
        dx_ref[...] = err * (1.0 / d)

        @pl.when(pl.program_id(0) == 0)
        def _():
            l_ref[...] = jnp.zeros((8, 128), F32)

        l_ref[...] += jnp.sum(err * err)

    row = lambda w: pl.BlockSpec((tm, w), lambda i: (i, 0))
    out = jax.ShapeDtypeStruct((n, d), F32)
    return pl.pallas_call(
        body, grid=(n // tm,),
        in_specs=[row(d), row(256), row(256), row(512), pl.BlockSpec((d, d), lambda i: (0, 0)),
                  pl.BlockSpec((1, d), lambda i: (0, 0)), row(d)],
        out_specs=[row(d), row(d), pl.BlockSpec((8, 128), lambda i: (0, 0))],
        out_shape=[out, out, jax.ShapeDtypeStruct((8, 128), F32)],
        compiler_params=_cparams(("arbitrary",)), name=name)(x2, oa, ob, oc, wo, g_row, target2)


def _outproj_bwd(dxo, y, oa, ob, oc, wo, g_row, name):
    n, d = dxo.shape
    tm = _mix_tm(n)

    def body(dx_ref, y_ref, oa_ref, ob_ref, oc_ref, w_ref, g_ref, dm_ref, dw_ref, dg_ref):
        @pl.when(pl.program_id(0) == 0)
        def _():
            dw_ref[...] = jnp.zeros((d, d), F32)
            dg_ref[...] = jnp.zeros((8, d), F32)

        yv, dx = y_ref[...], dx_ref[...]
        r = _rstd(yv)
        yn = yv * r
        dg_ref[...] += jnp.sum(dx * yn, axis=0, keepdims=True)
        dyn = dx * g_ref[...]
        dy = (r * (dyn - yn * jnp.mean(dyn * yn, axis=-1, keepdims=True))).astype(BF16)
        dm_ref[...] = _dot(dy, w_ref[...], NT)
        dw_ref[0:256, :] += _dot(oa_ref[...].astype(BF16), dy, TN)
        dw_ref[256:512, :] += _dot(ob_ref[...].astype(BF16), dy, TN)
        dw_ref[512:1024, :] += _dot(oc_ref[...].astype(BF16), dy, TN)

    row = lambda w: pl.BlockSpec((tm, w), lambda i: (i, 0))
    fixed = lambda r, c: pl.BlockSpec((r, c), lambda i: (0, 0))
    return pl.pallas_call(
        body, grid=(n // tm,),
        in_specs=[row(d), row(d), row(256), row(256), row(512), fixed(d, d), fixed(1, d)],
        out_specs=[row(d), fixed(d, d), fixed(8, d)],
        out_shape=[jax.ShapeDtypeStruct((n, d), F32), jax.ShapeDtypeStruct((d, d), F32), jax.ShapeDtypeStruct((8, d), F32)],
        compiler_params=_cparams(("arbitrary",)), name=name)(dxo, y, oa, ob, oc, wo, g_row)


_PIECES = ((0, A_W), (A_W, B_W), (A_W + B_W, C_W), (A_W + B_W + C_W, F_W))


def _inproj_bwd_x(x2, dxo, g_row, w_int, pieces, name):
    n, d = x2.shape
    tm = min(512, n)

    def body(x_ref, dxo_ref, g_ref, w_ref, da_ref, db_ref, dc_ref, df_ref, dx_ref, dg_ref):
        @pl.when(pl.program_id(0) == 0)
        def _():
            dg_ref[...] = jnp.zeros((8, d), F32)

        dh = jnp.zeros((tm, d), F32)
        for ref, (o, w) in zip((da_ref, db_ref, dc_ref, df_ref), _PIECES):
            dh = dh + _dot(ref[...].astype(BF16), w_ref[:, o:o + w], NT)
        x = x_ref[...]
        r = _rstd(x)
        xn = x * r
        dg_ref[...] += jnp.sum(dh * xn, axis=0, keepdims=True)
        dxn = dh * g_ref[...]
        dx_ref[...] = dxo_ref[...] + r * (dxn - xn * jnp.mean(dxn * xn, axis=-1, keepdims=True))

    row = lambda w: pl.BlockSpec((tm, w), lambda i: (i, 0))
    fixed = lambda r, c: pl.BlockSpec((r, c), lambda i: (0, 0))
    return pl.pallas_call(
        body, grid=(n // tm,),
        in_specs=[row(d), row(d), fixed(1, d), fixed(d, E_INT)] + [row(w) for _, w in _PIECES],
        out_specs=[row(d), fixed(8, d)],
        out_shape=[jax.ShapeDtypeStruct((n, d), F32), jax.ShapeDtypeStruct((8, d), F32)],
        compiler_params=_cparams(("arbitrary",), vmem_mb=56), name=name)(x2, dxo, g_row, w_int, *pieces)


def _inproj_bwd_w(x2, g_row, pieces, name):
    n, d = x2.shape
    tm = min(512, n)

    def body(x_ref, g_ref, da_ref, db_ref, dc_ref, df_ref, dw_ref):
        @pl.when(pl.program_id(0) == 0)
        def _():
            dw_ref[...] = jnp.zeros((d, E_INT), F32)

        x = x_ref[...]
        h = (x * _rstd(x) * g_ref[...]).astype(BF16)
        for ref, (o, w) in zip((da_ref, db_ref, dc_ref, df_ref), _PIECES):
            dw_ref[:, o:o + w] += _dot(h, ref[...].astype(BF16), TN)

    row = lambda w: pl.BlockSpec((tm, w), lambda i: (i, 0))
    return pl.pallas_call(
        body, grid=(n // tm,),
        in_specs=[row(d), pl.BlockSpec((1, d), lambda i: (0, 0))] + [row(w) for _, w in _PIECES],
        out_specs=pl.BlockSpec((d, E_INT), lambda i: (0, 0)),
        out_shape=jax.ShapeDtypeStruct((d, E_INT), F32),
        compiler_params=_cparams(("arbitrary",), vmem_mb=56), name=name)(x2, g_row, *pieces)


def _block_diag(pool_w_l):
    z = jnp.zeros((64, 64), pool_w_l.dtype)
    return jnp.concatenate(
        [jnp.concatenate([pool_w_l[g] if c == g else z for c in range(4)], axis=1) for g in range(4)], axis=0)


def _pad_lanes(v, width=128):
    return jnp.pad(v, ((0, 0),) * (v.ndim - 1) + ((0, width - v.shape[-1]),))


def _local_step(x, target, lower_bounds, pre_norm_g, w_in_int, hgrn_norm_g, fox_f_bias, pool_w, pool_scale,
                w_out_bf, post_norm_g, on_weight_grads):
    bsz, t, d = x.shape
    n = bsz * t
    lbs = _lbs_fwd(lower_bounds)
    saved = []
    xc = x.reshape(n, d)
    for l in range(DEPTH):
        proj = _inproj_fwd(xc, pre_norm_g[l:l + 1], w_in_int[l], f"inproj_fwd{l}").reshape(bsz, t, E_INT)
        wbd = _block_diag(pool_w[l]).astype(BF16)
        bias_row = _pad_lanes(fox_f_bias[l:l + 1])
        oa, oa_raw, states = _hgrn_fwd(proj, lbs[l:l + 1], hgrn_norm_g[l:l + 1], f"hgrn_fwd{l}")
        ob = _pool_fwd(proj, wbd, pool_scale[l:l + 1], f"pool_fwd{l}")
        c_nat, c_t = _foxgate_fwd(proj, bias_row, f"foxgate_fwd{l}")
        oc, oc_raw, lse = _fox_fwd(proj, c_nat, c_t, f"fox_fwd{l}")
        mixed = (oa.reshape(n, -1), ob.reshape(n, -1), oc.reshape(n, -1))
        if l < DEPTH - 1:
            y, xn = _outproj_fwd(xc, *mixed, w_out_bf[l], post_norm_g[l:l + 1], f"outproj_fwd{l}")
        else:
            y, dx, sq = _outproj_fwd_loss(xc, *mixed, w_out_bf[l], post_norm_g[l:l + 1], target.reshape(n, d),
                                          f"outproj_fwd{l}")
        saved.append((xc, proj, wbd, bias_row, oa, oa_raw, states, ob, oc, oc_raw, lse, c_nat, c_t, y))
        xc = xn
    g = {k: [None] * DEPTH for k in ("pre", "hgn", "bias", "pool_w", "pool_scale", "post", "lbs")}
    handed = [None] * DEPTH
    for l in reversed(range(DEPTH)):
        xin, proj, wbd, bias_row, oa, oa_raw, states, ob, oc, oc_raw, lse, c_nat, c_t, y = saved[l]
        dmix, d_w_out, dpost = _outproj_bwd(dx, y, oa.reshape(n, -1), ob.reshape(n, -1), oc.reshape(n, -1),
                                            w_out_bf[l], post_norm_g[l:l + 1], f"outproj_bwd{l}")
        g["post"][l] = dpost[0]
        dmix3 = dmix.reshape(bsz, t, d)
        d_c, dct, drow = _fox_bwd(proj, oc_raw, dmix3, lse, c_nat, c_t, f"fox_bwd{l}")
        dc_nat = _pad_lanes(dct[:, :, 0:2, :].reshape(bsz, FOX_HEADS, t).transpose(0, 2, 1)
                            + drow.reshape(bsz, t, FOX_HEADS, 64)[..., 0])
        d_f, dbias = _foxgate_bwd(proj, dc_nat, bias_row, f"foxgate_bwd{l}")
        g["bias"][l] = jnp.sum(dbias[:, 0, :FOX_HEADS], axis=0)
        d_b, dscale, dwbd = _pool_bwd(proj, dmix3, wbd, pool_scale[l:l + 1], f"pool_bwd{l}")
        g["pool_scale"][l] = jnp.sum(dscale[:, 0], axis=0)
        dwbd = jnp.sum(dwbd, axis=0)
        g["pool_w"][l] = jnp.stack([dwbd[64 * k:64 * (k + 1), 64 * k:64 * (k + 1)] for k in range(4)])
        d_a, dgn, dlb = _hgrn_bwd(proj, oa_raw, dmix3, states, lbs[l:l + 1], hgrn_norm_g[l:l + 1], f"hgrn_bwd{l}")
        g["hgn"][l] = jnp.sum(dgn[:, 0], axis=0)
        g["lbs"][l] = jnp.sum(dlb[:, 0], axis=0)
        pieces = [p.reshape(n, -1) for p in (d_a, d_b, d_c, d_f)]
        handed[l] = on_weight_grads(l, _inproj_bwd_w(xin, pre_norm_g[l:l + 1], pieces, f"inproj_bwd_w{l}"), d_w_out)
        dx, dpre = _inproj_bwd_x(xin, dx, pre_norm_g[l:l + 1], w_in_int[l], pieces, f"inproj_bwd_x{l}")
        g["pre"][l] = dpre[0]
    grads = {k: jnp.stack(v) for k, v in g.items()}
    return sq, dx.reshape(bsz, t, d), grads, handed


def _place():
    return lax.axis_index("x"), lax.axis_index("y"), lax.axis_index("c")


def _other_chips(x, y):
    return [(1 - x, y), (x, 1 - y), (1 - x, 1 - y)]


_ANY = pl.BlockSpec(memory_space=pl.ANY)


def _gather_body(handshake, n_arrays):
    def body(*refs):
        srcs, dsts = refs[:n_arrays], refs[n_arrays:2 * n_arrays]
        ici_send, ici_recv, d2d_send, d2d_recv, local_sems = refs[2 * n_arrays:]
        x, y, c = _place()
        if handshake:
            barrier = pltpu.get_barrier_semaphore()
            for peer in [(px, py, c) for px, py in _other_chips(x, y)] + [(x, y, 1 - c)]:
                pl.semaphore_signal(barrier, inc=1, device_id=peer, device_id_type=MESH)
            pl.semaphore_wait(barrier, 4)
        me = 2 * x + y
        pairs = list(zip(srcs, dsts))
        order = [(k, j) for k in range(3) for j in range(n_arrays)]
        mine = [pltpu.make_async_copy(src, dst.at[me], local_sems.at[j]) for j, (src, dst) in enumerate(pairs)]
        for cp in mine:
            cp.start()
        chips = _other_chips(x, y)
        sends = [pltpu.make_async_remote_copy(
            src_ref=pairs[j][0].at[c], dst_ref=pairs[j][1].at[me, c], send_sem=ici_send.at[n], recv_sem=ici_recv.at[n],
            device_id=(chips[k][0], chips[k][1], c), device_id_type=MESH) for n, (k, j) in enumerate(order)]
        for cp in sends:
            cp.start()
        passed = [pltpu.make_async_remote_copy(
            src_ref=pairs[j][1].at[2 * chips[k][0] + chips[k][1], c], dst_ref=pairs[j][1].at[2 * chips[k][0] + chips[k][1], c],
            send_sem=d2d_send.at[n], recv_sem=d2d_recv.at[n], device_id=(x, y, 1 - c), device_id_type=MESH)
            for n, (k, j) in enumerate(order)]
        for n, (k, j) in enumerate(order):
            px, py = chips[k]
            src, dst = pairs[j]
            pltpu.make_async_remote_copy(
                src_ref=src.at[c], dst_ref=dst.at[2 * px + py, c], send_sem=ici_send.at[n], recv_sem=ici_recv.at[n],
                device_id=(px, py, c), device_id_type=MESH).wait_recv()
            passed[n].start()
        for n, (k, j) in enumerate(order):
            px, py = chips[k]
            src, dst = pairs[j]
            pltpu.make_async_remote_copy(
                src_ref=dst.at[2 * px + py, 1 - c], dst_ref=dst.at[2 * px + py, 1 - c], send_sem=d2d_send.at[n],
                recv_sem=d2d_recv.at[n], device_id=(x, y, 1 - c), device_id_type=MESH).wait_recv()
        for cp in sends + passed:
            cp.wait_send()
        for cp in mine:
            cp.wait()

    return body


def _gather_sems(n_arrays):
    return [pltpu.SemaphoreType.DMA((3 * n_arrays,))] * 4 + [pltpu.SemaphoreType.DMA((n_arrays,))]


def _gathered(a):
    return jax.ShapeDtypeStruct((N_CHIPS,) + a.shape, a.dtype)


def _gather_weights(arrays):
    n = len(arrays)
    return pl.pallas_call(
        _gather_body(False, n), in_specs=[_ANY] * n, out_specs=[_ANY] * n, out_shape=[_gathered(a) for a in arrays],
        scratch_shapes=_gather_sems(n), name="gather_weights")(*arrays)


def _gather_weights_beside(arrays):
    hbm = pltpu.MemorySpace.HBM
    n = len(arrays)
    srcs = [jax.new_ref(a, memory_space=hbm) for a in arrays]
    dsts = [jax.empty_ref(_gathered(a), memory_space=hbm) for a in arrays]
    body = _gather_body(True, n)

    @pl.kernel(mesh=plsc.ScalarSubcoreMesh(axis_name="sequencer", num_cores=1), name="gather_weights_beside",
               scratch_types=_gather_sems(n), compiler_params=pltpu.CompilerParams(collective_id=1))
    def launch(*sems):
        body(*srcs, *dsts, *sems)

    launch()
    return [d[...] for d in dsts]


def _swap_with_sibling(parts, name):
    k = len(parts)

    def body(*refs):
        src, dst = refs[:k], refs[k:2 * k]
        send_sems, recv_sems = refs[2 * k:]
        x, y, c = _place()
        cps = [pltpu.make_async_remote_copy(src_ref=src[j], dst_ref=dst[j], send_sem=send_sems.at[j], recv_sem=recv_sems.at[j],
                                            device_id=(x, y, 1 - c), device_id_type=MESH) for j in range(k)]
        for cp in cps:
            cp.start()
        for cp in cps:
            cp.wait()

    return pl.pallas_call(
        body, in_specs=[_ANY] * k, out_specs=[_ANY] * k,
        out_shape=[jax.ShapeDtypeStruct(p.shape, p.dtype) for p in parts],
        scratch_shapes=[pltpu.SemaphoreType.DMA((k,)), pltpu.SemaphoreType.DMA((k,))], name=name)(*parts)


N_PEERS = 7


def _grad_exchange_body():
    def body(pin_ref, pout_ref, lin_ref, lout_ref, send_sems, recv_sems):
        x, y, c = _place()
        barrier = pltpu.get_barrier_semaphore()
        for k in range(1, N_PEERS + 1):
            peer = (x ^ ((k >> 2) & 1), y ^ ((k >> 1) & 1), c ^ (k & 1))
            pl.semaphore_signal(barrier, inc=1, device_id=peer, device_id_type=MESH)
        pl.semaphore_wait(barrier, N_PEERS)
        me = 2 * x + y
        pairs = ((pin_ref, lin_ref), (pout_ref, lout_ref))
        cps = []
        for k, (px, py) in enumerate(_other_chips(x, y)):
            for r in range(2):
                for j, (src, dst) in enumerate(pairs):
                    cps.append(pltpu.make_async_remote_copy(
                        src_ref=src.at[2 * px + py, r], dst_ref=dst.at[2 * k + c], send_sem=send_sems.at[2 * (2 * k + r) + j],
                        recv_sem=recv_sems.at[2 * (2 * k + c) + j], device_id=(px, py, r), device_id_type=MESH))
        for j, (src, dst) in enumerate(pairs):
            cps.append(pltpu.make_async_remote_copy(
                src_ref=src.at[me, 1 - c], dst_ref=dst.at[N_PEERS - 1], send_sem=send_sems.at[2 * (N_PEERS - 1) + j],
                recv_sem=recv_sems.at[2 * (N_PEERS - 1) + j], device_id=(x, y, 1 - c), device_id_type=MESH))
        for cp in cps:
            cp.start()
        for s in range(N_PEERS):
            for j, (src, dst) in enumerate(pairs):
                pltpu.make_async_remote_copy(
                    src_ref=src.at[0, 0], dst_ref=dst.at[s], send_sem=send_sems.at[2 * s + j], recv_sem=recv_sems.at[2 * s + j],
                    device_id=(x, y, 1 - c), device_id_type=MESH).wait_recv()
        for cp in cps:
            cp.wait_send()

    return body


_EXCHANGE_SEMS = [pltpu.SemaphoreType.DMA((2 * N_PEERS,))] * 2


def _landing(p):
    return jax.ShapeDtypeStruct((N_PEERS,) + p.shape[2:], p.dtype)


def _grad_exchange_beside(pin, pout, name, collective_id):
    hbm = pltpu.MemorySpace.HBM
    pin_ref, pout_ref = jax.new_ref(pin, memory_space=hbm), jax.new_ref(pout, memory_space=hbm)
    lin_ref, lout_ref = jax.empty_ref(_landing(pin), memory_space=hbm), jax.empty_ref(_landing(pout), memory_space=hbm)
    body = _grad_exchange_body()

    @pl.kernel(mesh=plsc.ScalarSubcoreMesh(axis_name="sequencer", num_cores=1), name=name,
               scratch_types=_EXCHANGE_SEMS, compiler_params=pltpu.CompilerParams(collective_id=collective_id))
    def launch(send_sems, recv_sems):
        body(pin_ref, pout_ref, lin_ref, lout_ref, send_sems, recv_sems)

    launch()
    return lin_ref[...], lout_ref[...]


def _add_n(parts, name):
    r, c = parts[0].shape
    tr = 256 if r % 256 == 0 else r
    n = len(parts)

    def body(*refs):
        acc = refs[0][...].astype(F32)
        for ref in refs[1:n]:
            acc = acc + ref[...].astype(F32)
        refs[n][...] = acc

    blk = pl.BlockSpec((tr, c), lambda i: (i, 0))
    return pl.pallas_call(
        body, grid=(r // tr,), in_specs=[blk] * n, out_specs=blk, out_shape=jax.ShapeDtypeStruct((r, c), F32),
        compiler_params=_cparams(("parallel",)), name=name)(*parts)


def _all_reduce_small(packet):
    r, w = packet.shape

    def body(p_ref, o_ref, buf, send_sems, recv_sems):
        x, y, c = _place()
        me = 4 * x + 2 * y + c
        buf[me] = p_ref[...]
        peers = []
        for k in range(1, 8):
            fx, fy, fc = (k >> 2) & 1, (k >> 1) & 1, k & 1
            peers.append((x ^ fx, y ^ fy, c ^ fc))
        cps = [pltpu.make_async_remote_copy(src_ref=p_ref, dst_ref=buf.at[me], send_sem=send_sems.at[k], recv_sem=recv_sems.at[k],
                                            device_id=peer, device_id_type=MESH) for k, peer in enumerate(peers)]
        for cp in cps:
            cp.start()
        for k, (px, py, pc) in enumerate(peers):
            pltpu.make_async_remote_copy(src_ref=p_ref, dst_ref=buf.at[4 * px + 2 * py + pc], send_sem=send_sems.at[k],
                                         recv_sem=recv_sems.at[k], device_id=(px, py, pc), device_id_type=MESH).wait_recv()
        for cp in cps:
            cp.wait_send()
        acc = buf[0]
        for k in range(1, 8):
            acc = acc + buf[k]
        o_ref[...] = acc

    vm = pl.BlockSpec(memory_space=pltpu.VMEM)
    return pl.pallas_call(
        body, in_specs=[vm], out_specs=vm, out_shape=jax.ShapeDtypeStruct((r, w), F32),
        scratch_shapes=[pltpu.VMEM((8, r, w), F32), pltpu.SemaphoreType.DMA((7,)), pltpu.SemaphoreType.DMA((7,))],
        name="all_reduce_small")(packet)


def _adamw_math(w, g, m, v):
    m = ADAM_B1 * m + (1.0 - ADAM_B1) * g
    v = ADAM_B2 * v + (1.0 - ADAM_B2) * (g * g)
    m_hat = m / (1.0 - ADAM_B1 ** ADAM_STEP)
    v_hat = v / (1.0 - ADAM_B2 ** ADAM_STEP)
    return -ADAM_LR * (m_hat / (jnp.sqrt(v_hat) + ADAM_EPS) + ADAM_WD * w), m, v


def _adamw(w, g_lower, g_upper, m, v, name):
    nl, r, c = w.shape
    tr = 128
    per_half = r // (2 * tr)

    def body(w_ref, lo_ref, up_ref, m_ref, v_ref, g_ref, d_ref, mo_ref, vo_ref):
        g = jnp.where(pl.program_id(1) == 0, lo_ref[...], up_ref[...])
        g_ref[...] = g
        d_ref[...], mo_ref[...], vo_ref[...] = _adamw_math(w_ref[...], g, m_ref[...], v_ref[...])

    blk = pl.BlockSpec((None, tr, c), lambda l, h, i: (l, h * per_half + i, 0))
    half = pl.BlockSpec((None, tr, c), lambda l, h, i: (l, i, 0))
    out = jax.ShapeDtypeStruct(w.shape, F32)
    return pl.pallas_call(
        body, grid=(nl, 2, per_half), in_specs=[blk, half, half, blk, blk], out_specs=[blk] * 4, out_shape=[out] * 4,
        compiler_params=_cparams(("parallel", "parallel", "parallel")), name=name)(w, g_lower, g_upper, m, v)


def _small_update(gsum, lower_bounds, wpack, mpack, vpack):
    r, w = gsum.shape
    lb_rows = DEPTH * HGRN_W // 128

    def body(g_ref, a_ref, w_ref, m_ref, v_ref, go_ref, d_ref, mo_ref, vo_ref):
        a = a_ref[...]
        a0, a1 = a[0:1], a[1:2]
        mx = jnp.maximum(a0, a1)
        e0, e1 = jnp.exp(a0 - mx), jnp.exp(a1 - mx)
        p0, p1 = e0 / (e0 + e1), e1 / (e0 + e1)
        g = g_ref[...]
        half = lb_rows // 2
        dl0 = jnp.concatenate([g[k:k + 1] for k in range(half)], axis=1)
        dl1 = jnp.concatenate([g[half + k:half + k + 1] for k in range(half)], axis=1)
        dp0 = (dl0 + dl1) - (dl0 + dl1)
        dp1 = dl1
        inner = p0 * dp0 + p1 * dp1
        da0, da1 = p0 * (dp0 - inner), p1 * (dp1 - inner)
        rows = [da0[:, 128 * k:128 * (k + 1)] for k in range(half)] + [da1[:, 128 * k:128 * (k + 1)] for k in range(half)]
        gfull = jnp.concatenate(rows + [g[lb_rows:]], axis=0)
        go_ref[...] = gfull
        d_ref[...], mo_ref[...], vo_ref[...] = _adamw_math(w_ref[...], gfull, m_ref[...], v_ref[...])

    vm = pl.BlockSpec(memory_space=pltpu.VMEM)
    out = jax.ShapeDtypeStruct((r, w), F32)
    return pl.pallas_call(body, in_specs=[vm] * 5, out_specs=[vm] * 4, out_shape=[out] * 4, name="small_update")(
        gsum, lower_bounds, wpack, mpack, vpack)


_SMALL = ("lower_bounds", "pre_norm_g", "hgrn_norm_g", "fox_f_bias", "pool_w", "pool_scale", "post_norm_g")


def _pack(parts):
    rows = []
    for k in _SMALL:
        f = parts[k].reshape(-1)
        pad = (-f.shape[0]) % (8 * 128)
        rows.append(jnp.pad(f, (0, pad)).reshape(-1, 128))
    rows.append(jnp.zeros((8, 128), F32))
    return jnp.concatenate(rows, axis=0)


def _unpack(pack, like):
    out, r = {}, 0
    for k in _SMALL:
        size = int(np.prod(like[k].shape))
        nr = -(-size // (8 * 128)) * 8
        out[k] = pack[r:r + nr].reshape(-1)[:size].reshape(like[k].shape)
        r += nr
    return out, r


def kernel(x, lower_bounds, pre_norm_g, w_in, hgrn_norm_g, fox_f_bias, pool_w, pool_scale, w_out, post_norm_g, loss_target, m_lower_bounds, m_pre_norm_g, m_w_in, m_hgrn_norm_g, m_fox_f_bias, m_pool_w, m_pool_scale, m_w_out, m_post_norm_g, v_lower_bounds, v_pre_norm_g, v_w_in, v_hgrn_norm_g, v_fox_f_bias, v_pool_w, v_pool_scale, v_w_out, v_post_norm_g):
    cx, cy, cc = _place()
    chip = 2 * cx + cy

    halves = lambda w, l: w[l].reshape(2, w.shape[1] // 2, w.shape[2]).astype(BF16)
    needed_first = _gather_weights([halves(w_in, 0)])
    needed_first, later = lax.optimization_barrier((needed_first, [halves(w_out, 0), halves(w_in, 1), halves(w_out, 1)]))
    later = _gather_weights_beside(later)
    w_in_int = [_internal_from_shards([a[q].reshape(D_MODEL, SHARD_W) for q in range(N_CHIPS)]) for a in (needed_first[0], later[1])]
    w_out_full = [a.reshape(D_MODEL, D_MODEL) for a in (later[0], later[2])]

    def on_weight_grads(l, d_w_in, d_w_out):
        pin = _shards_from_internal(d_w_in).reshape(N_CHIPS, 2, D_MODEL // 2, SHARD_W)
        pout = d_w_out.reshape(N_CHIPS, 2, D_MODEL // (2 * N_CHIPS), D_MODEL)
        own = [lax.dynamic_index_in_dim(lax.dynamic_index_in_dim(p, chip, 0, False), cc, 0, False) for p in (pin, pout)]
        return own, _grad_exchange_beside(pin.astype(BF16), pout.astype(BF16), f"grad_exchange{l}", 2 + l)

    sq, grad_x, g, handed = _local_step(x, loss_target, lower_bounds, pre_norm_g, w_in_int, hgrn_norm_g, fox_f_bias,
                                        pool_w, pool_scale, w_out_full, post_norm_g, on_weight_grads)
    first = cc == 0

    def finish(l, own, landed):
        mine = [_add_n([o] + [t[s] for s in range(N_PEERS)], f"grad_sum{l}_{j}") for j, (o, t) in enumerate(zip(own, landed))]
        theirs = _swap_with_sibling(mine, f"grad_swap{l}")
        return [(jnp.where(first, h, o), jnp.where(first, o, h)) for h, o in zip(mine, theirs)]

    grad_x, last = lax.optimization_barrier((grad_x, handed[1]))
    done = [None, finish(1, *last)]

    small = {"lower_bounds": g["lbs"], "pre_norm_g": g["pre"], "hgrn_norm_g": g["hgn"], "fox_f_bias": g["bias"],
             "pool_w": g["pool_w"], "pool_scale": g["pool_scale"], "post_norm_g": g["post"]}
    packet = _pack(small)
    nrows = packet.shape[0]
    packet = packet.at[nrows - 1].set(sq[0])
    gsum = _all_reduce_small(packet)
    loss = gsum[nrows - 1, 0] * (0.5 / D_MODEL)

    weights = {"lower_bounds": lower_bounds, "pre_norm_g": pre_norm_g, "hgrn_norm_g": hgrn_norm_g,
               "fox_f_bias": fox_f_bias, "pool_w": pool_w, "pool_scale": pool_scale, "post_norm_g": post_norm_g}
    moments_m = {"lower_bounds": m_lower_bounds, "pre_norm_g": m_pre_norm_g, "hgrn_norm_g": m_hgrn_norm_g,
                 "fox_f_bias": m_fox_f_bias, "pool_w": m_pool_w, "pool_scale": m_pool_scale, "post_norm_g": m_post_norm_g}
    moments_v = {"lower_bounds": v_lower_bounds, "pre_norm_g": v_pre_norm_g, "hgrn_norm_g": v_hgrn_norm_g,
                 "fox_f_bias": v_fox_f_bias, "pool_w": v_pool_w, "pool_scale": v_pool_scale, "post_norm_g": v_post_norm_g}
    gp, dp, mp, vp = _small_update(gsum, lower_bounds, _pack(weights), _pack(moments_m), _pack(moments_v))
    gs, _ = _unpack(gp, weights)
    ds, _ = _unpack(dp, weights)
    ms, _ = _unpack(mp, weights)
    vs, _ = _unpack(vp, weights)

    first_layer, _ = lax.optimization_barrier((handed[0], (done[1], gp, dp, mp, vp)))
    done[0] = finish(0, *first_layer)
    halves_of = lambda j, side: jnp.stack([done[l][j][side] for l in range(DEPTH)])
    grad_w_in, d_in, m_in, v_in = _adamw(w_in, halves_of(0, 0), halves_of(0, 1), m_w_in, v_w_in, "adamw_w_in")
    grad_w_out, d_out, m_out, v_out = _adamw(w_out, halves_of(1, 0), halves_of(1, 1), m_w_out, v_w_out, "adamw_w_out")

    def ordered(s, big_in, big_out):
        return (s["lower_bounds"], s["pre_norm_g"], big_in, s["hgrn_norm_g"], s["fox_f_bias"], s["pool_w"],
                s["pool_scale"], big_out, s["post_norm_g"])

    return (loss, grad_x, *ordered(gs, grad_w_in, grad_w_out), *ordered(ds, d_in, d_out),
            *ordered(ms, m_in, m_out), *ordered(vs, v_in, v_out))
```

```python
import numpy as np
import jax
import jax.numpy as jnp
from jax import lax
from jax.experimental import pallas as pl
from jax.experimental.pallas import tpu as pltpu
from jax.experimental.pallas import tpu_sc as plsc

F32 = jnp.float32
BF16 = jnp.bfloat16
HI = lax.Precision.HIGHEST
MESH = pl.DeviceIdType.MESH

NORM_EPS = 1e-6
MASK_VALUE = -1e30
TINY = 1e-30
ADAM_LR, ADAM_B1, ADAM_B2, ADAM_EPS, ADAM_WD, ADAM_STEP = 0.001, 0.9, 0.999, 1e-08, 0.01, 10

D_MODEL = 1024
DEPTH = 2
N_CHIPS = 4
CHUNK = 64
LANES = 128
HGRN_W, POOL_W, FOX_W, FOX_HEADS = 256, 256, 512, 8
POOL_WINDOWS = (2, 4, 8, 16)
POOL_HALO = 16
IN_WIDTH = 3592
SHARD_W = IN_WIDTH // N_CHIPS
A_W, B_W, C_W, F_W = 1024, 512, 2048, 128
E_INT = A_W + B_W + C_W + F_W
B_BLK = A_W // 512
C_BLK0 = (A_W + B_W) // 512
F_BLK = (A_W + B_W + C_W) // 128


def _segments():
    segs = []
    for hp in range(2):
        for part in range(4):
            segs.append((part * 256 + hp * 128, 128))
    segs.append((1024, 256))
    segs.append((1280, 256))
    for hp in range(4):
        for part in range(4):
            segs.append((1536 + part * 512 + hp * 128, 128))
    segs.append((3584, 8))
    return segs


_SEGS = _segments()


def _internal_from_shards(shards):
    parts = []
    for s, n in _SEGS:
        while n > 0:
            q, r = divmod(s, SHARD_W)
            take = min(n, SHARD_W - r)
            parts.append(shards[q][..., r:r + take])
            s, n = s + take, n - take
    parts.append(jnp.zeros(shards[0].shape[:-1] + (E_INT - IN_WIDTH,), shards[0].dtype))
    return jnp.concatenate(parts, axis=-1)


def _shards_from_internal(w):
    offs, o = [], 0
    for s, n in _SEGS:
        offs.append((s, o, n))
        o += n
    blocks = []
    for q in range(N_CHIPS):
        lo, hi = SHARD_W * q, SHARD_W * (q + 1)
        parts = [w[..., o + max(lo, s) - s:o + min(hi, s + n) - s] for s, o, n in sorted(offs) if s < hi and s + n > lo]
        blocks.append(jnp.concatenate(parts, axis=-1))
    return jnp.stack(blocks)


def _cparams(sem=None, vmem_mb=48):
    kw = dict(vmem_limit_bytes=vmem_mb * 1024 * 1024)
    if sem is not None:
        kw["dimension_semantics"] = sem
    return pltpu.CompilerParams(**kw)


def _sig(x):
    return 1.0 / (1.0 + jnp.exp(-x))


def _silu(x):
    return x * _sig(x)


def _dsilu(x):
    s = _sig(x)
    return s * (1.0 + x * (1.0 - s))


def _rstd(x):
    return lax.rsqrt(jnp.mean(x * x, axis=-1, keepdims=True) + NORM_EPS)


def _dot(a, b, dims, **kw):
    return lax.dot_general(a, b, (dims, ((), ())), preferred_element_type=F32, **kw)


NN = ((1,), (0,))
NT = ((1,), (1,))
TN = ((0,), (0,))


def _iota(shape, dim):
    return lax.broadcasted_iota(jnp.int32, shape, dim)


def _lbs_fwd(lower_bounds):
    def body(a_ref, o_ref):
        a = a_ref[...]
        a0, a1 = a[0:1], a[1:2]
        m = jnp.maximum(a0, a1)
        e0, e1 = jnp.exp(a0 - m), jnp.exp(a1 - m)
        p0, p1 = e0 / (e0 + e1), e1 / (e0 + e1)
        o_ref[...] = jnp.concatenate([p0 - p0, (p0 + p1) - p0], axis=0)

    return pl.pallas_call(body, out_shape=jax.ShapeDtypeStruct(lower_bounds.shape, F32), name="lbs_fwd")(lower_bounds)


def _inproj_fwd(x2, g_row, w_int, name):
    n, d = x2.shape
    e = w_int.shape[1]
    tm = min(512, n)

    def body(x_ref, g_ref, w_ref, o_ref):
        x = x_ref[...]
        h = (x * _rstd(x) * g_ref[...]).astype(BF16)
        o_ref[...] = jnp.dot(h, w_ref[...], preferred_element_type=F32)

    return pl.pallas_call(
        body, grid=(n // tm,),
        in_specs=[pl.BlockSpec((tm, d), lambda i: (i, 0)), pl.BlockSpec((1, d), lambda i: (0, 0)),
                  pl.BlockSpec((d, e), lambda i: (0, 0))],
        out_specs=pl.BlockSpec((tm, e), lambda i: (i, 0)),
        out_shape=jax.ShapeDtypeStruct((n, e), F32),
        compiler_params=_cparams(("parallel",)), name=name)(x2, g_row, w_int)


def _hgrn_gates(a, lb):
    qa, z = a[:, 0:128], a[:, 128:256]
    sg, sgn = _sig(z), _sig(-z)
    fg = lb + (1.0 - lb) * sg
    lf = jnp.log(jnp.maximum(fg, TINY))
    kk = (1.0 - lb) * sgn
    return qa * _sig(qa), kk, lf, sg, sgn, fg


N_LEVELS = 6


def _hgrn_tables():
    t = np.arange(LANES)
    j = np.arange(LANES)[None, :]
    same_chunk = (t[:, None] // CHUNK) == (j // CHUNK)
    w = np.zeros((2 + N_LEVELS, LANES, LANES), np.float32)
    w[0] = same_chunk & (j <= t[:, None])
    w[1] = same_chunk & (j > t[:, None])
    maskf = np.zeros((N_LEVELS, LANES, LANES), np.float32)
    rightf = np.zeros((N_LEVELS, LANES, LANES), np.float32)
    for li in range(N_LEVELS):
        m = (CHUNK // 2) >> li
        start = t - (t % (2 * m))
        right = (t % (2 * m)) >= m
        first = np.where(right, start + m, t + 1)
        last = np.where(right, t, start + m - 1)
        w[2 + li] = (j >= first[:, None]) & (j <= last[:, None])
        maskf[li] = (t[:, None] // (2 * m)) == (j // (2 * m))
        rightf[li] = right[:, None]
    w = w[:-1]
    return jnp.asarray(w.reshape(-1, LANES), BF16), jnp.asarray(np.tile(maskf, (1, 2, 1))), jnp.asarray(rightf)


def _split(x, n):
    parts = []
    for _ in range(n - 1):
        p = x.astype(BF16)
        parts.append(p)
        x = x - p.astype(F32)
    parts.append(x.astype(BF16))
    return parts


def _exact_dot(w, parts):
    acc = jnp.dot(w, parts[0], preferred_element_type=F32)
    for p in parts[1:]:
        acc = acc + jnp.dot(w, p, preferred_element_type=F32)
    return acc


def _head_sums(v, ones_blk, n=2):
    parts = _split(v, n)
    acc = jnp.dot(parts[0], ones_blk, preferred_element_type=F32)
    for p in parts[1:]:
        acc = acc + jnp.dot(p, ones_blk, preferred_element_type=F32)
    return acc


def _hgrn_consts():
    r, c = _iota((LANES, LANES), 0), _iota((LANES, LANES), 1)
    ones_blk = ((r // CHUNK) == (c // CHUNK)).astype(BF16)
    eye2 = (_iota((2 * LANES, LANES), 0) % LANES) == _iota((2 * LANES, LANES), 1)
    first = _iota((1, LANES), 1) < CHUNK
    return eye2, ones_blk, jnp.ones((LANES, LANES), BF16), first


def _stack_heads(v, first):
    return jnp.concatenate([jnp.where(first, v, 0.0), jnp.where(first, 0.0, v)], axis=0)


def _pick_heads(v2, first):
    return jnp.where(first, v2[:LANES], v2[LANES:])


def _hgrn_levels(qq, kk, lf, zall, mk_ref, rt_ref, first, d_att=None):
    att = jnp.zeros((2 * LANES, LANES), F32)
    dq = dk = db = jnp.zeros((LANES, LANES), F32)
    for li in range(N_LEVELS):
        rt = rt_ref[li]
        e = jnp.exp(zall[(2 + li) * LANES:(3 + li) * LANES] if li < N_LEVELS - 1 else lf * rt)
        mk = mk_ref[li]
        qef, kef = e * rt, e * (1.0 - rt)
        qe, ke = (qq * qef).astype(BF16), (kk * kef).astype(BF16)
        qe2 = _stack_heads(qe, first)
        att = att + _dot(qe2, ke, NT) * mk
        if d_att is not None:
            dam = (d_att * mk).astype(BF16)
            dqe = _pick_heads(jnp.dot(dam, ke, preferred_element_type=F32), first)
            dke = _dot(dam, qe2, TN)
            dq = dq + dqe * qef
            dk = dk + dke * kef
            db = db + (dqe * qe.astype(F32) - dke * ke.astype(F32))
    return att, dq, dk, db


def _hgrn_fwd(proj3, lbs_row, gn_row, name):
    bsz, t, _ = proj3.shape
    nt = t // LANES
    w_all, maskf, rightf = _hgrn_tables()

    def body(a_ref, lb_ref, gn_ref, w_ref, mk_ref, rt_ref, og_ref, or_ref, st_ref):
        lb = lb_ref[...]
        gn = gn_ref[...]
        eye2, ones_blk, ones_all, first = _hgrn_consts()

        def tile(i, carry):
            r0 = pl.multiple_of(i * LANES, LANES)
            a = a_ref[pl.ds(r0, LANES), :]
            qq, kk, lf, _, _, _ = _hgrn_gates(a, lb)
            va, ga = a[:, 256:384], a[:, 384:512]
            parts = _split(lf, 3)
            zall = _exact_dot(w_ref[...], parts)
            eb, ee = jnp.exp(zall[0:LANES]), jnp.exp(zall[LANES:2 * LANES])
            vb = va.astype(BF16)
            att, _, _, _ = _hgrn_levels(qq, kk, lf, zall, mk_ref, rt_ref, first)
            diag = _head_sums(_stack_heads(qq * kk, first), ones_all)
            a2 = (att + jnp.where(eye2, diag, 0.0)).astype(BF16)
            o_in = _pick_heads(jnp.dot(a2, vb, preferred_element_type=F32), first)
            qeb, keb = (qq * eb).astype(BF16), (kk * ee).astype(BF16)
            new_s, o_heads = [], []
            for h in range(2):
                hs = slice(CHUNK * h, CHUNK * (h + 1))
                o_h = o_in[:, hs]
                st = carry[h]
                chunks = []
                for c in range(2):
                    rc = slice(CHUNK * c, CHUNK * (c + 1))
                    st_ref[h, 2 * i + c] = st
                    chunks.append(o_h[rc] + _dot(qeb[rc, hs], st.astype(BF16), NT))
                    ebl = eb[CHUNK * (c + 1) - 1:CHUNK * (c + 1), hs]
                    st = st * ebl + _dot(vb[rc, hs], keb[rc, hs], TN)
                new_s.append(st)
                o_heads.append(jnp.concatenate(chunks, axis=0))
            o = jnp.concatenate(o_heads, axis=1)
            ms = _head_sums(o * o, ones_blk) * (1.0 / CHUNK)
            or_ref[pl.ds(r0, LANES), :] = o
            og_ref[pl.ds(r0, LANES), :] = (o * lax.rsqrt(ms + NORM_EPS) * gn * _silu(ga)).astype(BF16)
            return tuple(new_s)

        zero = jnp.zeros((CHUNK, CHUNK), F32)
        per_step = 4 if nt % 4 == 0 else 2

        def step(i, carry):
            for k in range(per_step):
                carry = tile(per_step * i + k, carry)
            return carry

        lax.fori_loop(0, nt // per_step, step, (zero, zero))

    out = jax.ShapeDtypeStruct((bsz, t, HGRN_W), F32)
    row = pl.BlockSpec((1, 128), lambda b, p: (0, p))
    return pl.pallas_call(
        body, grid=(bsz, 2),
        in_specs=[pl.BlockSpec((None, t, 512), lambda b, p: (b, 0, p)), row, row,
                  pl.BlockSpec(w_all.shape, lambda b, p: (0, 0)),
                  pl.BlockSpec(maskf.shape, lambda b, p: (0, 0, 0)),
                  pl.BlockSpec(rightf.shape, lambda b, p: (0, 0, 0))],
        out_specs=[pl.BlockSpec((None, t, 128), lambda b, p: (b, 0, p)),
                   pl.BlockSpec((None, t, 128), lambda b, p: (b, 0, p)),
                   pl.BlockSpec((None, 2, t // CHUNK, CHUNK, CHUNK), lambda b, p: (b, p, 0, 0, 0))],
        out_shape=[jax.ShapeDtypeStruct((bsz, t, HGRN_W), BF16), out,
                   jax.ShapeDtypeStruct((bsz, 4, t // CHUNK, CHUNK, CHUNK), F32)],
        compiler_params=_cparams(("parallel", "parallel")), name=name)(proj3, lbs_row, gn_row, w_all, maskf, rightf)


def _hgrn_bwd(proj3, o_raw, dmixed, states, lbs_row, gn_row, name):
    bsz, t, _ = proj3.shape
    nt = t // LANES
    nchunk = t // CHUNK
    w_all, maskf, rightf = _hgrn_tables()

    def body(a_ref, or_ref, do_ref, s_sc, lb_ref, gn_ref, w_ref, mk_ref, rt_ref, da_ref, dgn_ref, dlb_ref):
        lb = lb_ref[...]
        gn = gn_ref[...]
        eye2, ones_blk, ones_all, first = _hgrn_consts()
        r_i, c_i = _iota((LANES, LANES), 0), _iota((LANES, LANES), 1)
        suffix = ((c_i >= r_i) & ((r_i // CHUNK) == (c_i // CHUNK))).astype(BF16)
        row64 = _iota((LANES, CHUNK), 0)
        zero = jnp.zeros((CHUNK, CHUNK), F32)

        def bwd_tile(k, carry):
            dst0, dst1, dgn_acc, dlb_acc = carry
            i = nt - 1 - k
            r0 = pl.multiple_of(i * LANES, LANES)
            a = a_ref[pl.ds(r0, LANES), :]
            qa, ga = a[:, 0:128], a[:, 384:512]
            qq, kk, lf, sg, sgn, fg = _hgrn_gates(a, lb)
            parts = _split(lf, 3)
            zall = _exact_dot(w_ref[...], parts)
            eb, ee = jnp.exp(zall[0:LANES]), jnp.exp(zall[LANES:2 * LANES])
            vb = a[:, 256:384].astype(BF16)
            oraw = or_ref[pl.ds(r0, LANES), :]
            dout = do_ref[pl.ds(r0, LANES), :]
            r = lax.rsqrt(_head_sums(oraw * oraw, ones_blk) * (1.0 / CHUNK) + NORM_EPS)
            xn = oraw * r
            dga = dout * (xn * gn) * _dsilu(ga)
            don = dout * _silu(ga)
            dgn_acc = dgn_acc + jnp.sum(don * xn, axis=0, keepdims=True)
            dxn = don * gn
            do = r * (dxn - xn * (_head_sums(dxn * xn, ones_blk) * (1.0 / CHUNK)))
            dob = do.astype(BF16)
            do2 = _stack_heads(dob, first)
            d_att = _dot(do2, vb, NT)
            att, dq, dk, db_lv = _hgrn_levels(qq, kk, lf, zall, mk_ref, rt_ref, first, d_att)
            a2 = att + jnp.where(eye2, _head_sums(_stack_heads(qq * kk, first), ones_all), 0.0)
            dv_in = _dot(a2.astype(BF16), do2, TN)
            ddiag = _pick_heads(_head_sums(jnp.where(eye2, d_att, 0.0), ones_all), first)
            dq_in, dk_in = dq + ddiag * kk, dk + ddiag * qq
            qe_f, ke_f = qq * eb, kk * ee
            qeb, keb = qe_f.astype(BF16), ke_f.astype(BF16)
            new_ds, dq_h, dk_h, dv_h, dbl_h = [], [], [], [], []
            for h in range(2):
                hs = slice(CHUNK * h, CHUNK * (h + 1))
                dv, dq_i, dk_i = dv_in[:, hs], dq_in[:, hs], dk_in[:, hs]
                dst = (dst0, dst1)[h]
                dq_c, dk_c, dv_c, dbl_c = [None, None], [None, None], [None, None], [None, None]
                for c in (1, 0):
                    rc = slice(CHUNK * c, CHUNK * (c + 1))
                    st_n = s_sc[h, 2 * i + c]
                    ebl = eb[CHUNK * (c + 1) - 1:CHUNK * (c + 1), hs]
                    dstb = dst.astype(BF16)
                    dv_c[c] = _dot(keb[rc, hs], dstb, NT)
                    dke = jnp.dot(vb[rc, hs], dstb, preferred_element_type=F32)
                    dqe = jnp.dot(dob[rc, hs], st_n.astype(BF16), preferred_element_type=F32)
                    dbl_c[c] = (jnp.sum(dst * st_n, axis=0, keepdims=True) * ebl
                                + jnp.sum(dke * ke_f[rc, hs], axis=0, keepdims=True))
                    dq_c[c], dk_c[c] = dqe * eb[rc, hs], dke * ee[rc, hs]
                    dst = dst * ebl + _dot(dob[rc, hs], qeb[rc, hs], TN)
                new_ds.append(dst)
                dq_x, dk_x = jnp.concatenate(dq_c, axis=0), jnp.concatenate(dk_c, axis=0)
                dq_h.append(dq_i + dq_x)
                dk_h.append(dk_i + dk_x)
                dv_h.append(dv + jnp.concatenate(dv_c, axis=0))
                dbl_h.append(qq[:, hs] * dq_x - kk[:, hs] * dk_x
                             + jnp.where(row64 == CHUNK - 1, dbl_c[0], 0.0) + jnp.where(row64 == LANES - 1, dbl_c[1], 0.0))
            dqq = jnp.concatenate(dq_h, axis=1)
            dkk = jnp.concatenate(dk_h, axis=1)
            dvv = jnp.concatenate(dv_h, axis=1)
            db = db_lv + jnp.concatenate(dbl_h, axis=1)
            dlf = _exact_dot(suffix, _split(db, 3))
            dqa = dqq * _dsilu(qa)
            dfg = jnp.where(fg > TINY, dlf / fg, 0.0)
            dz = (dfg - dkk) * (1.0 - lb) * sg * sgn
            dlb_acc = dlb_acc + jnp.sum(dfg * (1.0 - sg) - dkk * sgn, axis=0, keepdims=True)
            da_ref[pl.ds(r0, LANES), :] = jnp.concatenate([dqa, dz, dvv, dga], axis=1).astype(BF16)
            return new_ds[0], new_ds[1], dgn_acc, dlb_acc

        zrow = jnp.zeros((1, LANES), F32)
        per_step = 4 if nt % 4 == 0 else 2

        def step(k, carry):
            for r in range(per_step):
                carry = bwd_tile(per_step * k + r, carry)
            return carry

        _, _, dgn_acc, dlb_acc = lax.fori_loop(0, nt // per_step, step, (zero, zero, zrow, zrow))
        dgn_ref[...] = jnp.broadcast_to(dgn_acc, (8, LANES))
        dlb_ref[...] = jnp.broadcast_to(dlb_acc, (8, LANES))

    rows = jax.ShapeDtypeStruct((bsz, 8, HGRN_W), F32)
    row = pl.BlockSpec((1, 128), lambda b, p: (0, p))
    blk = pl.BlockSpec((None, t, 128), lambda b, p: (b, 0, p))
    return pl.pallas_call(
        body, grid=(bsz, 2),
        in_specs=[pl.BlockSpec((None, t, 512), lambda b, p: (b, 0, p)), blk, blk,
                  pl.BlockSpec((None, 2, nchunk, CHUNK, CHUNK), lambda b, p: (b, p, 0, 0, 0)), row, row,
                  pl.BlockSpec(w_all.shape, lambda b, p: (0, 0)),
                  pl.BlockSpec(maskf.shape, lambda b, p: (0, 0, 0)),
                  pl.BlockSpec(rightf.shape, lambda b, p: (0, 0, 0))],
        out_specs=[pl.BlockSpec((None, t, 512), lambda b, p: (b, 0, p)),
                   pl.BlockSpec((None, 8, 128), lambda b, p: (b, 0, p)),
                   pl.BlockSpec((None, 8, 128), lambda b, p: (b, 0, p))],
        out_shape=[jax.ShapeDtypeStruct((bsz, t, A_W), BF16), rows, rows],
        compiler_params=_cparams(("parallel", "parallel")), name=name)(
            proj3, o_raw, dmixed, states, lbs_row, gn_row, w_all, maskf, rightf)


def _pool_tt(t):
    return min(256, t)


def _window_select(s2, s4, s8, s16, lane):
    return jnp.where(lane < 64, s2, jnp.where(lane < 128, s4, jnp.where(lane < 192, s8, s16)))


def _pool_counts(t0, tt):
    lane = _iota((tt, POOL_W), 1)
    tpos = (_iota((tt, POOL_W), 0) + t0 + 1).astype(F32)
    win = jnp.where(lane < 64, 2.0, jnp.where(lane < 128, 4.0, jnp.where(lane < 192, 8.0, 16.0)))
    return 1.0 / jnp.minimum(tpos, win), lane


def _pooled_tile(upad_ref, i, tt):
    r0 = pl.multiple_of(i * tt, 8)
    cat = upad_ref[pl.ds(r0, tt + POOL_HALO), :]
    s2 = cat + pltpu.roll(cat, 1, 0)
    s4 = s2 + pltpu.roll(s2, 2, 0)
    s8 = s4 + pltpu.roll(s4, 4, 0)
    s16 = s8 + pltpu.roll(s8, 8, 0)
    inv, lane = _pool_counts(i * tt, tt)
    sel = _window_select(s2[POOL_HALO:], s4[POOL_HALO:], s8[POOL_HALO:], s16[POOL_HALO:], lane)
    return sel * inv - cat[POOL_HALO:], inv, lane


def _pool_fwd(proj3, wbd, scale_row, name):
    bsz, t, _ = proj3.shape
    tt = _pool_tt(t)

    def body(p_ref, w_ref, sc_ref, o_ref, upad):
        upad[0:POOL_HALO, :] = jnp.zeros((POOL_HALO, POOL_W), F32)
        upad[POOL_HALO:, :] = p_ref[:, 0:POOL_W]
        w = w_ref[...]
        sc = sc_ref[...]

        def tile(i, c):
            pooled, _, _ = _pooled_tile(upad, i, tt)
            r0 = pl.multiple_of(i * tt, 8)
            g = p_ref[pl.ds(r0, tt), POOL_W:2 * POOL_W]
            pre = jnp.dot(pooled.astype(BF16), w, preferred_element_type=F32)
            o_ref[pl.ds(r0, tt), :] = (pre * sc * _silu(g)).astype(BF16)
            return c

        lax.fori_loop(0, t // tt, tile, 0)

    return pl.pallas_call(
        body, grid=(bsz,),
        in_specs=[pl.BlockSpec((None, t, 512), lambda b: (b, 0, B_BLK)),
                  pl.BlockSpec((POOL_W, POOL_W), lambda b: (0, 0)),
                  pl.BlockSpec((1, POOL_W), lambda b: (0, 0))],
        out_specs=pl.BlockSpec((None, t, POOL_W), lambda b: (b, 0, 0)),
        out_shape=jax.ShapeDtypeStruct((bsz, t, POOL_W), BF16),
        scratch_shapes=[pltpu.VMEM((t + POOL_HALO, POOL_W), F32)],
        compiler_params=_cparams(("parallel",)), name=name)(proj3, wbd, scale_row)


def _pool_bwd(proj3, dmixed, wbd, scale_row, name):
    bsz, t, _ = proj3.shape
    tt = _pool_tt(t)

    def body(p_ref, do_ref, w_ref, sc_ref, db_ref, dsc_ref, dw_ref, upad, epad):
        upad[0:POOL_HALO, :] = jnp.zeros((POOL_HALO, POOL_W), F32)
        upad[POOL_HALO:, :] = p_ref[:, 0:POOL_W]
        epad[t:, :] = jnp.zeros((POOL_HALO, POOL_W), F32)
        w = w_ref[...]
        sc = sc_ref[...]

        def tile(i, carry):
            dsc_acc, dw_acc = carry
            pooled, inv, _ = _pooled_tile(upad, i, tt)
            r0 = pl.multiple_of(i * tt, 8)
            g = p_ref[pl.ds(r0, tt), POOL_W:2 * POOL_W]
            dout = do_ref[pl.ds(r0, tt), :]
            pb = pooled.astype(BF16)
            pre = jnp.dot(pb, w, preferred_element_type=F32)
            t1 = dout * _silu(g)
            dsc_acc = dsc_acc + jnp.sum(t1 * pre, axis=0, keepdims=True)
            dpre = (t1 * sc).astype(BF16)
            db_ref[pl.ds(r0, tt), POOL_W:2 * POOL_W] = (dout * pre * sc * _dsilu(g)).astype(BF16)
            dw_acc = dw_acc + _dot(pb, dpre, TN)
            dpooled = _dot(dpre, w, NT)
            epad[pl.ds(r0, tt), :] = dpooled * inv
            return dsc_acc, dw_acc

        dsc_acc, dw_acc = lax.fori_loop(0, t // tt, tile, (jnp.zeros((1, POOL_W), F32), jnp.zeros((POOL_W, POOL_W), F32)))
        dsc_ref[...] = jnp.broadcast_to(dsc_acc, (8, POOL_W))
        dw_ref[...] = dw_acc

        def tile2(i, c):
            r0 = pl.multiple_of(i * tt, 8)
            n = tt + POOL_HALO
            cat = epad[pl.ds(r0, n), :]
            s2 = cat + pltpu.roll(cat, n - 1, 0)
            s4 = s2 + pltpu.roll(s2, n - 2, 0)
            s8 = s4 + pltpu.roll(s4, n - 4, 0)
            s16 = s8 + pltpu.roll(s8, n - 8, 0)
            inv, lane = _pool_counts(i * tt, tt)
            sel = _window_select(s2[:tt], s4[:tt], s8[:tt], s16[:tt], lane)
            db_ref[pl.ds(r0, tt), 0:POOL_W] = (sel - cat[:tt] / inv).astype(BF16)
            return c

        lax.fori_loop(0, t // tt, tile2, 0)

    return pl.pallas_call(
        body, grid=(bsz,),
        in_specs=[pl.BlockSpec((None, t, 512), lambda b: (b, 0, B_BLK)),
                  pl.BlockSpec((None, t, POOL_W), lambda b: (b, 0, 1)),
                  pl.BlockSpec((POOL_W, POOL_W), lambda b: (0, 0)),
                  pl.BlockSpec((1, POOL_W), lambda b: (0, 0))],
        out_specs=[pl.BlockSpec((None, t, 512), lambda b: (b, 0, 0)),
                   pl.BlockSpec((None, 8, POOL_W), lambda b: (b, 0, 0)),
                   pl.BlockSpec((None, POOL_W, POOL_W), lambda b: (b, 0, 0))],
        out_shape=[jax.ShapeDtypeStruct((bsz, t, B_W), BF16), jax.ShapeDtypeStruct((bsz, 8, POOL_W), F32),
                   jax.ShapeDtypeStruct((bsz, POOL_W, POOL_W), F32)],
        scratch_shapes=[pltpu.VMEM((t + POOL_HALO, POOL_W), F32), pltpu.VMEM((t + POOL_HALO, POOL_W), F32)],
        compiler_params=_cparams(("parallel",)), name=name)(proj3, dmixed, wbd, scale_row)


def _head_select_rows(hp):
    r, c = _iota((8, LANES), 0), _iota((8, LANES), 1)
    return ((r < 2) & (c == 2 * hp + r)).astype(F32)


def _foxgate_fwd(proj3, bias_row, name):
    bsz, t, _ = proj3.shape
    nt = t // LANES

    def body(f_ref, b_ref, cn_ref, ct_ref):
        bias = b_ref[...]
        i, j = _iota((LANES, LANES), 0), _iota((LANES, LANES), 1)
        lower = (j <= i).astype(BF16)
        spread = (_iota((LANES, FOX_W), 0) == _iota((LANES, FOX_W), 1) // 64).astype(BF16)
        select = [_head_select_rows(hp).astype(BF16) for hp in range(4)]
        offset = jnp.zeros((1, LANES), F32)
        for k in range(nt):
            rows = slice(k * LANES, (k + 1) * LANES)
            xg = f_ref[rows, :] + bias
            lf = jnp.minimum(xg, 0.0) - jnp.log(1.0 + jnp.exp(-jnp.abs(xg)))
            c = _exact_dot(lower, _split(lf, 3)) + offset
            offset = c[LANES - 1:LANES, :]
            parts = _split(c, 3)
            cn_ref[rows, :] = _head_sums(c, spread, 3)
            for hp in range(4):
                acc = _dot(select[hp], parts[0], NT)
                for p in parts[1:]:
                    acc = acc + _dot(select[hp], p, NT)
                ct_ref[hp, :, rows] = acc

    return pl.pallas_call(
        body, grid=(bsz,),
        in_specs=[pl.BlockSpec((None, t, 128), lambda b: (b, 0, F_BLK)), pl.BlockSpec((1, 128), lambda b: (0, 0))],
        out_specs=[pl.BlockSpec((None, t, FOX_W), lambda b: (b, 0, 0)),
                   pl.BlockSpec((None, 4, 8, t), lambda b: (b, 0, 0, 0))],
        out_shape=[jax.ShapeDtypeStruct((bsz, t, FOX_W), F32), jax.ShapeDtypeStruct((bsz, 4, 8, t), F32)],
        compiler_params=_cparams(("parallel",)), name=name)(proj3, bias_row)


def _foxgate_bwd(proj3, dc_nat, bias_row, name):
    bsz, t, _ = proj3.shape
    nt = t // LANES

    def body(f_ref, dc_ref, b_ref, df_ref, dbias_ref, run_sc):
        bias = b_ref[...]
        i, j = _iota((LANES, LANES), 0), _iota((LANES, LANES), 1)
        upper = (j >= i).astype(F32)
        valid = _iota((1, LANES), 1) < FOX_HEADS
        run_sc[...] = jnp.zeros((8, LANES), F32)
        dbias_ref[...] = jnp.zeros((8, LANES), F32)

        def tile(k, c):
            r0 = pl.multiple_of((nt - 1 - k) * LANES, LANES)
            dc = dc_ref[pl.ds(r0, LANES), :] + jnp.where(i == LANES - 1, run_sc[0:1, :], 0.0)
            dlf = jnp.dot(upper, dc, precision=HI, preferred_element_type=F32)
            xg = f_ref[pl.ds(r0, LANES), :] + bias
            df = jnp.where(valid, dlf * _sig(-xg), 0.0)
            df_ref[pl.ds(r0, LANES), :] = df.astype(BF16)
            run_sc[...] = dlf[0:8, :]
            dbias_ref[...] += jnp.sum(df, axis=0, keepdims=True)
            return c

        lax.fori_loop(0, nt, tile, 0)

    blk = pl.BlockSpec((None, t, 128), lambda b: (b, 0, 0))
    return pl.pallas_call(
        body, grid=(bsz,),
        in_specs=[pl.BlockSpec((None, t, 128), lambda b: (b, 0, F_BLK)), blk, pl.BlockSpec((1, 128), lambda b: (0, 0))],
        out_specs=[blk, pl.BlockSpec((None, 8, 128), lambda b: (b, 0, 0))],
        out_shape=[jax.ShapeDtypeStruct((bsz, t, F_W), BF16), jax.ShapeDtypeStruct((bsz, 8, 128), F32)],
        scratch_shapes=[pltpu.VMEM((8, LANES), F32)],
        compiler_params=_cparams(("parallel",)), name=name)(proj3, dc_nat, bias_row)


def _fox_tile(t):
    return min(256, t)


def _fox_fwd(proj3, c_nat, c_t, name):
    bsz, t, _ = proj3.shape
    tq = tk = min(2 * _fox_tile(t), t)
    nq = t // tq

    def body(q_ref, kv_ref, cn_ref, ct_ref, og_ref, or_ref, lse_ref):
        i = pl.program_id(2)
        qblk = q_ref[...]
        first = _iota((1, 128), 1) < 64
        qv = qblk[:, 0:128] * 0.125
        qm = [jnp.where(first, qv, 0.0).astype(BF16), jnp.where(first, 0.0, qv).astype(BF16)]
        cqs = [cn_ref[:, 0:1], cn_ref[:, 64:65]]
        rows = _iota((tq, tk), 0) + i * tq

        def scores(j):
            c0 = pl.multiple_of(j * tk, tk)
            kb = kv_ref[pl.ds(c0, tk), 128:256].astype(BF16)
            return tuple(_dot(qm[h], kb, NT) + (cqs[h] - ct_ref[h:h + 1, pl.ds(c0, tk)]) for h in range(2))

        def absorb(j, state, s01, masked):
            c0 = pl.multiple_of(j * tk, tk)
            vblk = kv_ref[pl.ds(c0, tk), 256:384]
            vx = [jnp.where(first, vblk, 1.0).astype(BF16), jnp.where(first, 1.0, vblk).astype(BF16)]
            new = []
            for h in range(2):
                m, acc, s = state[2 * h], state[2 * h + 1], s01[h]
                if masked:
                    s = jnp.where(rows >= _iota((tq, tk), 1) + j * tk, s, MASK_VALUE)
                m_new = jnp.maximum(m, jnp.max(s, axis=1, keepdims=True))
                p = jnp.exp(s - m_new).astype(BF16)
                new += [m_new, jnp.exp(m - m_new) * acc + jnp.dot(p, vx[h], preferred_element_type=F32)]
            return tuple(new)

        init = (jnp.full((tq, 1), MASK_VALUE, F32), jnp.zeros((tq, 128), F32)) * 2
        n_full = (i * tq) // tk
        state = lax.fori_loop(0, n_full, lambda j, state: absorb(j, state, scores(j), False), init)
        m0, acc0, m1, acc1 = absorb(n_full, state, scores(n_full), True)
        l0, l1 = pltpu.roll(acc0, 64, 1), pltpu.roll(acc1, 64, 1)
        o = jnp.where(first, acc0 / l0, acc1 / l1)
        or_ref[...] = o
        og_ref[...] = (o * _silu(qblk[:, 384:512])).astype(BF16)
        lse_ref[...] = jnp.where(first, m0 + jnp.log(l0), m1 + jnp.log(l1))

    out = jax.ShapeDtypeStruct((bsz, t, FOX_W), F32)
    blk = pl.BlockSpec((None, tq, 128), lambda b, p, i: (b, i, p))
    return pl.pallas_call(
        body, grid=(bsz, 4, nq),
        in_specs=[pl.BlockSpec((None, tq, 512), lambda b, p, i: (b, i, C_BLK0 + p)),
                  pl.BlockSpec((None, t, 512), lambda b, p, i: (b, 0, C_BLK0 + p)),
                  blk,
                  pl.BlockSpec((None, None, 8, t), lambda b, p, i: (b, p, 0, 0))],
        out_specs=[blk, blk, blk],
        out_shape=[jax.ShapeDtypeStruct((bsz, t, FOX_W), BF16), out, out],
        compiler_params=_cparams(("parallel", "parallel", "arbitrary")), name=name)(proj3, proj3, c_nat, c_t)


def _fox_bwd(proj3, o_raw, dmixed, lse, c_nat, c_t, name):
    bsz, t, _ = proj3.shape
    tq = tk = min(2 * _fox_tile(t), t)
    nq = t // tq

    def body(a_ref, or_ref, do_ref, lse_ref, cn_ref, ct_ref, dc_out, dct_out, drow_out, dq_sc, do_sc, dl_sc):
        def prep(i, c):
            r0 = pl.multiple_of(i * tq, tq)
            g = a_ref[pl.ds(r0, tq), 384:512]
            dout = do_ref[pl.ds(r0, tq), :]
            o = or_ref[pl.ds(r0, tq), :]
            dc_out[pl.ds(r0, tq), 384:512] = (dout * o * _dsilu(g)).astype(BF16)
            do = dout * _silu(g)
            do_sc[pl.ds(r0, tq), :] = do
            prod = do * o
            d0 = jnp.sum(prod[:, 0:64], axis=1, keepdims=True)
            d1 = jnp.sum(prod[:, 64:128], axis=1, keepdims=True)
            dl_sc[pl.ds(r0, tq), :] = jnp.concatenate([jnp.broadcast_to(d0, (tq, 64)), jnp.broadcast_to(d1, (tq, 64))], axis=1)
            dq_sc[pl.ds(r0, tq), :] = jnp.zeros((tq, 128), F32)
            drow_out[pl.ds(r0, tq), :] = jnp.zeros((tq, 128), F32)
            return c

        lax.fori_loop(0, nq, prep, 0)
        dct_out[...] = jnp.zeros((8, t), F32)

        first = _iota((1, 128), 1) < 64

        def heads(v):
            return [jnp.where(first, v, 0.0).astype(BF16), jnp.where(first, 0.0, v).astype(BF16)]

        def kv_tile(j, c):
            c0 = pl.multiple_of(j * tk, tk)
            kb = a_ref[pl.ds(c0, tk), 128:256].astype(BF16)
            vb = a_ref[pl.ds(c0, tk), 256:384].astype(BF16)
            cks = [ct_ref[h:h + 1, pl.ds(c0, tk)] for h in range(2)]

            def q_step(i, carry, diagonal):
                dk, dv, dcol0, dcol1 = carry
                r0 = pl.multiple_of(i * tq, tq)
                causal = _iota((tq, tk), 0) + i * tq >= _iota((tq, tk), 1) + j * tk
                qv = a_ref[pl.ds(r0, tq), 0:128] * 0.125
                do = do_sc[pl.ds(r0, tq), :]
                qb, dob = qv.astype(BF16), do.astype(BF16)
                qm, dom = heads(qv), heads(do)
                full, dcols, rsums = [], [], []
                for h in range(2):
                    lse_h = lse_ref[pl.ds(r0, tq), 64 * h:64 * h + 1]
                    dl_h = dl_sc[pl.ds(r0, tq), 64 * h:64 * h + 1]
                    cq = cn_ref[pl.ds(r0, tq), 64 * h:64 * h + 1]
                    p = jnp.exp(_dot(qm[h], kb, NT) + (cq - cks[h]) - lse_h)
                    if diagonal:
                        p = jnp.where(causal, p, 0.0)
                    ds = p * (_dot(dom[h], vb, NT) - dl_h)
                    dsb = ds.astype(BF16)
                    full.append((_dot(p.astype(BF16), dob, TN), _dot(dsb, qb, TN),
                                 jnp.dot(dsb, kb, preferred_element_type=F32)))
                    dcols.append(jnp.sum(ds, axis=0, keepdims=True))
                    rsums.append(jnp.broadcast_to(jnp.sum(ds, axis=1, keepdims=True), (tq, 128)))
                dq_sc[pl.ds(r0, tq), :] += jnp.where(first, full[0][2], full[1][2]) * 0.125
                drow_out[pl.ds(r0, tq), :] += jnp.where(first, rsums[0], rsums[1])
                return (dk + jnp.where(first, full[0][1], full[1][1]), dv + jnp.where(first, full[0][0], full[1][0]),
                        dcol0 - dcols[0], dcol1 - dcols[1])

            carry = (jnp.zeros((tk, 128), F32), jnp.zeros((tk, 128), F32), jnp.zeros((1, tk), F32), jnp.zeros((1, tk), F32))
            carry = q_step(j, carry, True)
            dk, dv, dcol0, dcol1 = lax.fori_loop(j + 1, nq, lambda i, carry: q_step(i, carry, False), carry)
            dct_out[0:1, pl.ds(c0, tk)] = dcol0
            dct_out[1:2, pl.ds(c0, tk)] = dcol1
            dc_out[pl.ds(c0, tk), 128:256] = dk.astype(BF16)
            dc_out[pl.ds(c0, tk), 256:384] = dv.astype(BF16)
            return c

        lax.fori_loop(0, t // tk, kv_tile, 0)
        dc_out[:, 0:128] = dq_sc[...].astype(BF16)

    blk = pl.BlockSpec((None, t, 128), lambda b, p: (b, 0, p))
    return pl.pallas_call(
        body, grid=(bsz, 4),
        in_specs=[pl.BlockSpec((None, t, 512), lambda b, p: (b, 0, C_BLK0 + p)),
                  blk,
                  pl.BlockSpec((None, t, 128), lambda b, p: (b, 0, 4 + p)),
                  blk, blk,
                  pl.BlockSpec((None, None, 8, t), lambda b, p: (b, p, 0, 0))],
        out_specs=[pl.BlockSpec((None, t, 512), lambda b, p: (b, 0, p)),
                   pl.BlockSpec((None, None, 8, t), lambda b, p: (b, p, 0, 0)), blk],
        out_shape=[jax.ShapeDtypeStruct((bsz, t, C_W), BF16), jax.ShapeDtypeStruct((bsz, 4, 8, t), F32),
                   jax.ShapeDtypeStruct((bsz, t, FOX_W), F32)],
        scratch_shapes=[pltpu.VMEM((t, 128), F32), pltpu.VMEM((t, 128), F32), pltpu.VMEM((t, 128), F32)],
        compiler_params=_cparams(("parallel", "parallel")), name=name)(proj3, o_raw, dmixed, lse, c_nat, c_t)


def _mix_tm(n):
    return min(512, n)


def _outproj_fwd(x2, oa, ob, oc, wo, g_row, name):
    n, d = x2.shape
    tm = _mix_tm(n)

    def body(x_ref, oa_ref, ob_ref, oc_ref, w_ref, g_ref, y_ref, xo_ref):
        y = (jnp.dot(oa_ref[...].astype(BF16), w_ref[0:256, :], preferred_element_type=F32)
             + jnp.dot(ob_ref[...].astype(BF16), w_ref[256:512, :], preferred_element_type=F32)
             + jnp.dot(oc_ref[...].astype(BF16), w_ref[512:1024, :], preferred_element_type=F32))
        y_ref[...] = y
        xo_ref[...] = x_ref[...] + y * _rstd(y) * g_ref[...]

    row = lambda w: pl.BlockSpec((tm, w), lambda i: (i, 0))
    out = jax.ShapeDtypeStruct((n, d), F32)
    return pl.pallas_call(
        body, grid=(n // tm,),
        in_specs=[row(d), row(256), row(256), row(512), pl.BlockSpec((d, d), lambda i: (0, 0)),
                  pl.BlockSpec((1, d), lambda i: (0, 0))],
        out_specs=[row(d), row(d)], out_shape=[out, out],
        compiler_params=_cparams(("parallel",)), name=name)(x2, oa, ob, oc, wo, g_row)


def _outproj_fwd_loss(x2, oa, ob, oc, wo, g_row, target2, name):
    n, d = x2.shape
    tm = _mix_tm(n)

    def body(x_ref, oa_ref, ob_ref, oc_ref, w_ref, g_ref, t_ref, y_ref, dx_ref, l_ref):
        y = (jnp.dot(oa_ref[...].astype(BF16), w_ref[0:256, :], preferred_element_type=F32)
             + jnp.dot(ob_ref[...].astype(BF16), w_ref[256:512, :], preferred_element_type=F32)
             + jnp.dot(oc_ref[...].astype(BF16), w_ref[512:1024, :], preferred_element_type=F32))
        y_ref[...] = y
        err = (x_ref[...] + y * _rstd(y) * g_ref[...]) - t_ref[...]
        dx_ref[...] = err * (1.0 / d)

        @pl.when(pl.program_id(0) == 0)
        def _():
            l_ref[...] = jnp.zeros((8, 128), F32)

        l_ref[...] += jnp.sum(err * err)

    row = lambda w: pl.BlockSpec((tm, w), lambda i: (i, 0))
    out = jax.ShapeDtypeStruct((n, d), F32)
    return pl.pallas_call(
        body, grid=(n // tm,),
        in_specs=[row(d), row(256), row(256), row(512), pl.BlockSpec((d, d), lambda i: (0, 0)),
                  pl.BlockSpec((1, d), lambda i: (0, 0)), row(d)],
        out_specs=[row(d), row(d), pl.BlockSpec((8, 128), lambda i: (0, 0))],
        out_shape=[out, out, jax.ShapeDtypeStruct((8, 128), F32)],
        compiler_params=_cparams(("arbitrary",)), name=name)(x2, oa, ob, oc, wo, g_row, target2)


def _outproj_bwd(dxo, y, oa, ob, oc, wo, g_row, name):
    n, d = dxo.shape
    tm = _mix_tm(n)

    def body(dx_ref, y_ref, oa_ref, ob_ref, oc_ref, w_ref, g_ref, dm_ref, dw_ref, dg_ref):
        @pl.when(pl.program_id(0) == 0)
        def _():
            dw_ref[...] = jnp.zeros((d, d), F32)
            dg_ref[...] = jnp.zeros((8, d), F32)

        yv, dx = y_ref[...], dx_ref[...]
        r = _rstd(yv)
        yn = yv * r
        dg_ref[...] += jnp.sum(dx * yn, axis=0, keepdims=True)
        dyn = dx * g_ref[...]
        dy = (r * (dyn - yn * jnp.mean(dyn * yn, axis=-1, keepdims=True))).astype(BF16)
        dm_ref[...] = _dot(dy, w_ref[...], NT)
        dw_ref[0:256, :] += _dot(oa_ref[...].astype(BF16), dy, TN)
        dw_ref[256:512, :] += _dot(ob_ref[...].astype(BF16), dy, TN)
        dw_ref[512:1024, :] += _dot(oc_ref[...].astype(BF16), dy, TN)

    row = lambda w: pl.BlockSpec((tm, w), lambda i: (i, 0))
    fixed = lambda r, c: pl.BlockSpec((r, c), lambda i: (0, 0))
    return pl.pallas_call(
        body, grid=(n // tm,),
        in_specs=[row(d), row(d), row(256), row(256), row(512), fixed(d, d), fixed(1, d)],
        out_specs=[row(d), fixed(d, d), fixed(8, d)],
        out_shape=[jax.ShapeDtypeStruct((n, d), F32), jax.ShapeDtypeStruct((d, d), F32), jax.ShapeDtypeStruct((8, d), F32)],
        compiler_params=_cparams(("arbitrary",)), name=name)(dxo, y, oa, ob, oc, wo, g_row)


_PIECES = ((0, A_W), (A_W, B_W), (A_W + B_W, C_W), (A_W + B_W + C_W, F_W))


def _inproj_bwd_x(x2, dxo, g_row, w_int, pieces, name):
    n, d = x2.shape
    tm = min(512, n)

    def body(x_ref, dxo_ref, g_ref, w_ref, da_ref, db_ref, dc_ref, df_ref, dx_ref, dg_ref):
        @pl.when(pl.program_id(0) == 0)
        def _():
            dg_ref[...] = jnp.zeros((8, d), F32)

        dh = jnp.zeros((tm, d), F32)
        for ref, (o, w) in zip((da_ref, db_ref, dc_ref, df_ref), _PIECES):
            dh = dh + _dot(ref[...].astype(BF16), w_ref[:, o:o + w], NT)
        x = x_ref[...]
        r = _rstd(x)
        xn = x * r
        dg_ref[...] += jnp.sum(dh * xn, axis=0, keepdims=True)
        dxn = dh * g_ref[...]
        dx_ref[...] = dxo_ref[...] + r * (dxn - xn * jnp.mean(dxn * xn, axis=-1, keepdims=True))

    row = lambda w: pl.BlockSpec((tm, w), lambda i: (i, 0))
    fixed = lambda r, c: pl.BlockSpec((r, c), lambda i: (0, 0))
    return pl.pallas_call(
        body, grid=(n // tm,),
        in_specs=[row(d), row(d), fixed(1, d), fixed(d, E_INT)] + [row(w) for _, w in _PIECES],
        out_specs=[row(d), fixed(8, d)],
        out_shape=[jax.ShapeDtypeStruct((n, d), F32), jax.ShapeDtypeStruct((8, d), F32)],
        compiler_params=_cparams(("arbitrary",), vmem_mb=56), name=name)(x2, dxo, g_row, w_int, *pieces)


def _inproj_bwd_w(x2, g_row, pieces, name):
    n, d = x2.shape
    tm = min(512, n)

    def body(x_ref, g_ref, da_ref, db_ref, dc_ref, df_ref, dw_ref):
        @pl.when(pl.program_id(0) == 0)
        def _():
            dw_ref[...] = jnp.zeros((d, E_INT), F32)

        x = x_ref[...]
        h = (x * _rstd(x) * g_ref[...]).astype(BF16)
        for ref, (o, w) in zip((da_ref, db_ref, dc_ref, df_ref), _PIECES):
            dw_ref[:, o:o + w] += _dot(h, ref[...].astype(BF16), TN)

    row = lambda w: pl.BlockSpec((tm, w), lambda i: (i, 0))
    return pl.pallas_call(
        body, grid=(n // tm,),
        in_specs=[row(d), pl.BlockSpec((1, d), lambda i: (0, 0))] + [row(w) for _, w in _PIECES],
        out_specs=pl.BlockSpec((d, E_INT), lambda i: (0, 0)),
        out_shape=jax.ShapeDtypeStruct((d, E_INT), F32),
        compiler_params=_cparams(("arbitrary",), vmem_mb=56), name=name)(x2, g_row, *pieces)


def _block_diag(pool_w_l):
    z = jnp.zeros((64, 64), pool_w_l.dtype)
    return jnp.concatenate(
        [jnp.concatenate([pool_w_l[g] if c == g else z for c in range(4)], axis=1) for g in range(4)], axis=0)


def _pad_lanes(v, width=128):
    return jnp.pad(v, ((0, 0),) * (v.ndim - 1) + ((0, width - v.shape[-1]),))


def _local_step(x, target, lower_bounds, pre_norm_g, w_in_int, hgrn_norm_g, fox_f_bias, pool_w, pool_scale,
                w_out_bf, post_norm_g, on_weight_grads):
    bsz, t, d = x.shape
    n = bsz * t
    lbs = _lbs_fwd(lower_bounds)
    saved = []
    xc = x.reshape(n, d)
    for l in range(DEPTH):
        proj = _inproj_fwd(xc, pre_norm_g[l:l + 1], w_in_int[l], f"inproj_fwd{l}").reshape(bsz, t, E_INT)
        wbd = _block_diag(pool_w[l]).astype(BF16)
        bias_row = _pad_lanes(fox_f_bias[l:l + 1])
        oa, oa_raw, states = _hgrn_fwd(proj, lbs[l:l + 1], hgrn_norm_g[l:l + 1], f"hgrn_fwd{l}")
        ob = _pool_fwd(proj, wbd, pool_scale[l:l + 1], f"pool_fwd{l}")
        c_nat, c_t = _foxgate_fwd(proj, bias_row, f"foxgate_fwd{l}")
        oc, oc_raw, lse = _fox_fwd(proj, c_nat, c_t, f"fox_fwd{l}")
        mixed = (oa.reshape(n, -1), ob.reshape(n, -1), oc.reshape(n, -1))
        if l < DEPTH - 1:
            y, xn = _outproj_fwd(xc, *mixed, w_out_bf[l], post_norm_g[l:l + 1], f"outproj_fwd{l}")
        else:
            y, dx, sq = _outproj_fwd_loss(xc, *mixed, w_out_bf[l], post_norm_g[l:l + 1], target.reshape(n, d),
                                          f"outproj_fwd{l}")
        saved.append((xc, proj, wbd, bias_row, oa, oa_raw, states, ob, oc, oc_raw, lse, c_nat, c_t, y))
        xc = xn
    g = {k: [None] * DEPTH for k in ("pre", "hgn", "bias", "pool_w", "pool_scale", "post", "lbs")}
    handed = [None] * DEPTH
    for l in reversed(range(DEPTH)):
        xin, proj, wbd, bias_row, oa, oa_raw, states, ob, oc, oc_raw, lse, c_nat, c_t, y = saved[l]
        dmix, d_w_out, dpost = _outproj_bwd(dx, y, oa.reshape(n, -1), ob.reshape(n, -1), oc.reshape(n, -1),
                                            w_out_bf[l], post_norm_g[l:l + 1], f"outproj_bwd{l}")
        g["post"][l] = dpost[0]
        dmix3 = dmix.reshape(bsz, t, d)
        d_c, dct, drow = _fox_bwd(proj, oc_raw, dmix3, lse, c_nat, c_t, f"fox_bwd{l}")
        dc_nat = _pad_lanes(dct[:, :, 0:2, :].reshape(bsz, FOX_HEADS, t).transpose(0, 2, 1)
                            + drow.reshape(bsz, t, FOX_HEADS, 64)[..., 0])
        d_f, dbias = _foxgate_bwd(proj, dc_nat, bias_row, f"foxgate_bwd{l}")
        g["bias"][l] = jnp.sum(dbias[:, 0, :FOX_HEADS], axis=0)
        d_b, dscale, dwbd = _pool_bwd(proj, dmix3, wbd, pool_scale[l:l + 1], f"pool_bwd{l}")
        g["pool_scale"][l] = jnp.sum(dscale[:, 0], axis=0)
        dwbd = jnp.sum(dwbd, axis=0)
        g["pool_w"][l] = jnp.stack([dwbd[64 * k:64 * (k + 1), 64 * k:64 * (k + 1)] for k in range(4)])
        d_a, dgn, dlb = _hgrn_bwd(proj, oa_raw, dmix3, states, lbs[l:l + 1], hgrn_norm_g[l:l + 1], f"hgrn_bwd{l}")
        g["hgn"][l] = jnp.sum(dgn[:, 0], axis=0)
        g["lbs"][l] = jnp.sum(dlb[:, 0], axis=0)
        pieces = [p.reshape(n, -1) for p in (d_a, d_b, d_c, d_f)]
        handed[l] = on_weight_grads(l, _inproj_bwd_w(xin, pre_norm_g[l:l + 1], pieces, f"inproj_bwd_w{l}"), d_w_out)
        dx, dpre = _inproj_bwd_x(xin, dx, pre_norm_g[l:l + 1], w_in_int[l], pieces, f"inproj_bwd_x{l}")
        g["pre"][l] = dpre[0]
    grads = {k: jnp.stack(v) for k, v in g.items()}
    return sq, dx.reshape(bsz, t, d), grads, handed


def _place():
    return lax.axis_index("x"), lax.axis_index("y"), lax.axis_index("c")


def _other_chips(x, y):
    return [(1 - x, y), (x, 1 - y), (1 - x, 1 - y)]


_ANY = pl.BlockSpec(memory_space=pl.ANY)


def _gather_body(handshake, n_arrays):
    def body(*refs):
        srcs, dsts = refs[:n_arrays], refs[n_arrays:2 * n_arrays]
        ici_send, ici_recv, d2d_send, d2d_recv, local_sems = refs[2 * n_arrays:]
        x, y, c = _place()
        if handshake:
            barrier = pltpu.get_barrier_semaphore()
            for peer in [(px, py, c) for px, py in _other_chips(x, y)] + [(x, y, 1 - c)]:
                pl.semaphore_signal(barrier, inc=1, device_id=peer, device_id_type=MESH)
            pl.semaphore_wait(barrier, 4)
        me = 2 * x + y
        pairs = list(zip(srcs, dsts))
        order = [(k, j) for k in range(3) for j in range(n_arrays)]
        mine = [pltpu.make_async_copy(src, dst.at[me], local_sems.at[j]) for j, (src, dst) in enumerate(pairs)]
        for cp in mine:
            cp.start()
        chips = _other_chips(x, y)
        sends = [pltpu.make_async_remote_copy(
            src_ref=pairs[j][0].at[c], dst_ref=pairs[j][1].at[me, c], send_sem=ici_send.at[n], recv_sem=ici_recv.at[n],
            device_id=(chips[k][0], chips[k][1], c), device_id_type=MESH) for n, (k, j) in enumerate(order)]
        for cp in sends:
            cp.start()
        passed = [pltpu.make_async_remote_copy(
            src_ref=pairs[j][1].at[2 * chips[k][0] + chips[k][1], c], dst_ref=pairs[j][1].at[2 * chips[k][0] + chips[k][1], c],
            send_sem=d2d_send.at[n], recv_sem=d2d_recv.at[n], device_id=(x, y, 1 - c), device_id_type=MESH)
            for n, (k, j) in enumerate(order)]
        for n, (k, j) in enumerate(order):
            px, py = chips[k]
            src, dst = pairs[j]
            pltpu.make_async_remote_copy(
                src_ref=src.at[c], dst_ref=dst.at[2 * px + py, c], send_sem=ici_send.at[n], recv_sem=ici_recv.at[n],
                device_id=(px, py, c), device_id_type=MESH).wait_recv()
            passed[n].start()
        for n, (k, j) in enumerate(order):
            px, py = chips[k]
            src, dst = pairs[j]
            pltpu.make_async_remote_copy(
                src_ref=dst.at[2 * px + py, 1 - c], dst_ref=dst.at[2 * px + py, 1 - c], send_sem=d2d_send.at[n],
                recv_sem=d2d_recv.at[n], device_id=(x, y, 1 - c), device_id_type=MESH).wait_recv()
        for cp in sends + passed:
            cp.wait_send()
        for cp in mine:
            cp.wait()

    return body


def _gather_sems(n_arrays):
    return [pltpu.SemaphoreType.DMA((3 * n_arrays,))] * 4 + [pltpu.SemaphoreType.DMA((n_arrays,))]


def _gathered(a):
    return jax.ShapeDtypeStruct((N_CHIPS,) + a.shape, a.dtype)


def _gather_weights(arrays):
    n = len(arrays)
    return pl.pallas_call(
        _gather_body(False, n), in_specs=[_ANY] * n, out_specs=[_ANY] * n, out_shape=[_gathered(a) for a in arrays],
        scratch_shapes=_gather_sems(n), name="gather_weights")(*arrays)


def _gather_weights_beside(arrays):
    hbm = pltpu.MemorySpace.HBM
    n = len(arrays)
    srcs = [jax.new_ref(a, memory_space=hbm) for a in arrays]
    dsts = [jax.empty_ref(_gathered(a), memory_space=hbm) for a in arrays]
    body = _gather_body(True, n)

    @pl.kernel(mesh=plsc.ScalarSubcoreMesh(axis_name="sequencer", num_cores=1), name="gather_weights_beside",
               scratch_types=_gather_sems(n), compiler_params=pltpu.CompilerParams(collective_id=1))
    def launch(*sems):
        body(*srcs, *dsts, *sems)

    launch()
    return [d[...] for d in dsts]


def _swap_with_sibling(parts, name):
    k = len(parts)

    def body(*refs):
        src, dst = refs[:k], refs[k:2 * k]
        send_sems, recv_sems = refs[2 * k:]
        x, y, c = _place()
        cps = [pltpu.make_async_remote_copy(src_ref=src[j], dst_ref=dst[j], send_sem=send_sems.at[j], recv_sem=recv_sems.at[j],
                                            device_id=(x, y, 1 - c), device_id_type=MESH) for j in range(k)]
        for cp in cps:
            cp.start()
        for cp in cps:
            cp.wait()

    return pl.pallas_call(
        body, in_specs=[_ANY] * k, out_specs=[_ANY] * k,
        out_shape=[jax.ShapeDtypeStruct(p.shape, p.dtype) for p in parts],
        scratch_shapes=[pltpu.SemaphoreType.DMA((k,)), pltpu.SemaphoreType.DMA((k,))], name=name)(*parts)


N_PEERS = 7


def _grad_exchange_body():
    def body(pin_ref, pout_ref, lin_ref, lout_ref, send_sems, recv_sems):
        x, y, c = _place()
        barrier = pltpu.get_barrier_semaphore()
        for k in range(1, N_PEERS + 1):
            peer = (x ^ ((k >> 2) & 1), y ^ ((k >> 1) & 1), c ^ (k & 1))
            pl.semaphore_signal(barrier, inc=1, device_id=peer, device_id_type=MESH)
        pl.semaphore_wait(barrier, N_PEERS)
        me = 2 * x + y
        pairs = ((pin_ref, lin_ref), (pout_ref, lout_ref))
        cps = []
        for k, (px, py) in enumerate(_other_chips(x, y)):
            for r in range(2):
                for j, (src, dst) in enumerate(pairs):
                    cps.append(pltpu.make_async_remote_copy(
                        src_ref=src.at[2 * px + py, r], dst_ref=dst.at[2 * k + c], send_sem=send_sems.at[2 * (2 * k + r) + j],
                        recv_sem=recv_sems.at[2 * (2 * k + c) + j], device_id=(px, py, r), device_id_type=MESH))
        for j, (src, dst) in enumerate(pairs):
            cps.append(pltpu.make_async_remote_copy(
                src_ref=src.at[me, 1 - c], dst_ref=dst.at[N_PEERS - 1], send_sem=send_sems.at[2 * (N_PEERS - 1) + j],
                recv_sem=recv_sems.at[2 * (N_PEERS - 1) + j], device_id=(x, y, 1 - c), device_id_type=MESH))
        for cp in cps:
            cp.start()
        for s in range(N_PEERS):
            for j, (src, dst) in enumerate(pairs):
                pltpu.make_async_remote_copy(
                    src_ref=src.at[0, 0], dst_ref=dst.at[s], send_sem=send_sems.at[2 * s + j], recv_sem=recv_sems.at[2 * s + j],
                    device_id=(x, y, 1 - c), device_id_type=MESH).wait_recv()
        for cp in cps:
            cp.wait_send()

    return body


_EXCHANGE_SEMS = [pltpu.SemaphoreType.DMA((2 * N_PEERS,))] * 2


def _landing(p):
    return jax.ShapeDtypeStruct((N_PEERS,) + p.shape[2:], p.dtype)


def _grad_exchange_beside(pin, pout, name, collective_id):
    hbm = pltpu.MemorySpace.HBM
    pin_ref, pout_ref = jax.new_ref(pin, memory_space=hbm), jax.new_ref(pout, memory_space=hbm)
    lin_ref, lout_ref = jax.empty_ref(_landing(pin), memory_space=hbm), jax.empty_ref(_landing(pout), memory_space=hbm)
    body = _grad_exchange_body()

    @pl.kernel(mesh=plsc.ScalarSubcoreMesh(axis_name="sequencer", num_cores=1), name=name,
               scratch_types=_EXCHANGE_SEMS, compiler_params=pltpu.CompilerParams(collective_id=collective_id))
    def launch(send_sems, recv_sems):
        body(pin_ref, pout_ref, lin_ref, lout_ref, send_sems, recv_sems)

    launch()
    return lin_ref[...], lout_ref[...]


def _add_n(parts, name):
    r, c = parts[0].shape
    tr = 256 if r % 256 == 0 else r
    n = len(parts)

    def body(*refs):
        acc = refs[0][...].astype(F32)
        for ref in refs[1:n]:
            acc = acc + ref[...].astype(F32)
        refs[n][...] = acc

    blk = pl.BlockSpec((tr, c), lambda i: (i, 0))
    return pl.pallas_call(
        body, grid=(r // tr,), in_specs=[blk] * n, out_specs=blk, out_shape=jax.ShapeDtypeStruct((r, c), F32),
        compiler_params=_cparams(("parallel",)), name=name)(*parts)


def _all_reduce_small(packet):
    r, w = packet.shape

    def body(p_ref, o_ref, buf, send_sems, recv_sems):
        x, y, c = _place()
        me = 4 * x + 2 * y + c
        buf[me] = p_ref[...]
        peers = []
        for k in range(1, 8):
            fx, fy, fc = (k >> 2) & 1, (k >> 1) & 1, k & 1
            peers.append((x ^ fx, y ^ fy, c ^ fc))
        cps = [pltpu.make_async_remote_copy(src_ref=p_ref, dst_ref=buf.at[me], send_sem=send_sems.at[k], recv_sem=recv_sems.at[k],
                                            device_id=peer, device_id_type=MESH) for k, peer in enumerate(peers)]
        for cp in cps:
            cp.start()
        for k, (px, py, pc) in enumerate(peers):
            pltpu.make_async_remote_copy(src_ref=p_ref, dst_ref=buf.at[4 * px + 2 * py + pc], send_sem=send_sems.at[k],
                                         recv_sem=recv_sems.at[k], device_id=(px, py, pc), device_id_type=MESH).wait_recv()
        for cp in cps:
            cp.wait_send()
        acc = buf[0]
        for k in range(1, 8):
            acc = acc + buf[k]
        o_ref[...] = acc

    vm = pl.BlockSpec(memory_space=pltpu.VMEM)
    return pl.pallas_call(
        body, in_specs=[vm], out_specs=vm, out_shape=jax.ShapeDtypeStruct((r, w), F32),
        scratch_shapes=[pltpu.VMEM((8, r, w), F32), pltpu.SemaphoreType.DMA((7,)), pltpu.SemaphoreType.DMA((7,))],
        name="all_reduce_small")(packet)


def _adamw_math(w, g, m, v):
    m = ADAM_B1 * m + (1.0 - ADAM_B1) * g
    v = ADAM_B2 * v + (1.0 - ADAM_B2) * (g * g)
    m_hat = m / (1.0 - ADAM_B1 ** ADAM_STEP)
    v_hat = v / (1.0 - ADAM_B2 ** ADAM_STEP)
    return -ADAM_LR * (m_hat / (jnp.sqrt(v_hat) + ADAM_EPS) + ADAM_WD * w), m, v


def _adamw(w, g_lower, g_upper, m, v, name):
    nl, r, c = w.shape
    tr = 128
    per_half = r // (2 * tr)

    def body(w_ref, lo_ref, up_ref, m_ref, v_ref, g_ref, d_ref, mo_ref, vo_ref):
        g = jnp.where(pl.program_id(1) == 0, lo_ref[...], up_ref[...])
        g_ref[...] = g
        d_ref[...], mo_ref[...], vo_ref[...] = _adamw_math(w_ref[...], g, m_ref[...], v_ref[...])

    blk = pl.BlockSpec((None, tr, c), lambda l, h, i: (l, h * per_half + i, 0))
    half = pl.BlockSpec((None, tr, c), lambda l, h, i: (l, i, 0))
    out = jax.ShapeDtypeStruct(w.shape, F32)
    return pl.pallas_call(
        body, grid=(nl, 2, per_half), in_specs=[blk, half, half, blk, blk], out_specs=[blk] * 4, out_shape=[out] * 4,
        compiler_params=_cparams(("parallel", "parallel", "parallel")), name=name)(w, g_lower, g_upper, m, v)


def _small_update(gsum, lower_bounds, wpack, mpack, vpack):
    r, w = gsum.shape
    lb_rows = DEPTH * HGRN_W // 128

    def body(g_ref, a_ref, w_ref, m_ref, v_ref, go_ref, d_ref, mo_ref, vo_ref):
        a = a_ref[...]
        a0, a1 = a[0:1], a[1:2]
        mx = jnp.maximum(a0, a1)
        e0, e1 = jnp.exp(a0 - mx), jnp.exp(a1 - mx)
        p0, p1 = e0 / (e0 + e1), e1 / (e0 + e1)
        g = g_ref[...]
        half = lb_rows // 2
        dl0 = jnp.concatenate([g[k:k + 1] for k in range(half)], axis=1)
        dl1 = jnp.concatenate([g[half + k:half + k + 1] for k in range(half)], axis=1)
        dp0 = (dl0 + dl1) - (dl0 + dl1)
        dp1 = dl1
        inner = p0 * dp0 + p1 * dp1
        da0, da1 = p0 * (dp0 - inner), p1 * (dp1 - inner)
        rows = [da0[:, 128 * k:128 * (k + 1)] for k in range(half)] + [da1[:, 128 * k:128 * (k + 1)] for k in range(half)]
        gfull = jnp.concatenate(rows + [g[lb_rows:]], axis=0)
        go_ref[...] = gfull
        d_ref[...], mo_ref[...], vo_ref[...] = _adamw_math(w_ref[...], gfull, m_ref[...], v_ref[...])

    vm = pl.BlockSpec(memory_space=pltpu.VMEM)
    out = jax.ShapeDtypeStruct((r, w), F32)
    return pl.pallas_call(body, in_specs=[vm] * 5, out_specs=[vm] * 4, out_shape=[out] * 4, name="small_update")(
        gsum, lower_bounds, wpack, mpack, vpack)


_SMALL = ("lower_bounds", "pre_norm_g", "hgrn_norm_g", "fox_f_bias", "pool_w", "pool_scale", "post_norm_g")


def _pack(parts):
    rows = []
    for k in _SMALL:
        f = parts[k].reshape(-1)
        pad = (-f.shape[0]) % (8 * 128)
        rows.append(jnp.pad(f, (0, pad)).reshape(-1, 128))
    rows.append(jnp.zeros((8, 128), F32))
    return jnp.concatenate(rows, axis=0)


def _unpack(pack, like):
    out, r = {}, 0
    for k in _SMALL:
        size = int(np.prod(like[k].shape))
        nr = -(-size // (8 * 128)) * 8
        out[k] = pack[r:r + nr].reshape(-1)[:size].reshape(like[k].shape)
        r += nr
    return out, r


def kernel(x, lower_bounds, pre_norm_g, w_in, hgrn_norm_g, fox_f_bias, pool_w, pool_scale, w_out, post_norm_g, loss_target, m_lower_bounds, m_pre_norm_g, m_w_in, m_hgrn_norm_g, m_fox_f_bias, m_pool_w, m_pool_scale, m_w_out, m_post_norm_g, v_lower_bounds, v_pre_norm_g, v_w_in, v_hgrn_norm_g, v_fox_f_bias, v_pool_w, v_pool_scale, v_w_out, v_post_norm_g):
    cx, cy, cc = _place()
    chip = 2 * cx + cy

    halves = lambda w, l: w[l].reshape(2, w.shape[1] // 2, w.shape[2]).astype(BF16)
    needed_first = _gather_weights([halves(w_in, 0)])
    needed_first, later = lax.optimization_barrier((needed_first, [halves(w_out, 0), halves(w_in, 1), halves(w_out, 1)]))
    later = _gather_weights_beside(later)
    w_in_int = [_internal_from_shards([a[q].reshape(D_MODEL, SHARD_W) for q in range(N_CHIPS)]) for a in (needed_first[0], later[1])]
    w_out_full = [a.reshape(D_MODEL, D_MODEL) for a in (later[0], later[2])]

    def on_weight_grads(l, d_w_in, d_w_out):
        pin = _shards_from_internal(d_w_in).reshape(N_CHIPS, 2, D_MODEL // 2, SHARD_W)
        pout = d_w_out.reshape(N_CHIPS, 2, D_MODEL // (2 * N_CHIPS), D_MODEL)
        own = [lax.dynamic_index_in_dim(lax.dynamic_index_in_dim(p, chip, 0, False), cc, 0, False) for p in (pin, pout)]
        return own, _grad_exchange_beside(pin.astype(BF16), pout.astype(BF16), f"grad_exchange{l}", 2 + l)

    sq, grad_x, g, handed = _local_step(x, loss_target, lower_bounds, pre_norm_g, w_in_int, hgrn_norm_g, fox_f_bias,
                                        pool_w, pool_scale, w_out_full, post_norm_g, on_weight_grads)
    first = cc == 0

    def finish(l, own, landed):
        mine = [_add_n([o] + [t[s] for s in range(N_PEERS)], f"grad_sum{l}_{j}") for j, (o, t) in enumerate(zip(own, landed))]
        theirs = _swap_with_sibling(mine, f"grad_swap{l}")
        return [(jnp.where(first, h, o), jnp.where(first, o, h)) for h, o in zip(mine, theirs)]

    grad_x, last = lax.optimization_barrier((grad_x, handed[1]))
    done = [None, finish(1, *last)]

    small = {"lower_bounds": g["lbs"], "pre_norm_g": g["pre"], "hgrn_norm_g": g["hgn"], "fox_f_bias": g["bias"],
             "pool_w": g["pool_w"], "pool_scale": g["pool_scale"], "post_norm_g": g["post"]}
    packet = _pack(small)
    nrows = packet.shape[0]
    packet = packet.at[nrows - 1].set(sq[0])
    gsum = _all_reduce_small(packet)
    loss = gsum[nrows - 1, 0] * (0.5 / D_MODEL)

    weights = {"lower_bounds": lower_bounds, "pre_norm_g": pre_norm_g, "hgrn_norm_g": hgrn_norm_g,
               "fox_f_bias": fox_f_bias, "pool_w": pool_w, "pool_scale": pool_scale, "post_norm_g": post_norm_g}
    moments_m = {"lower_bounds": m_lower_bounds, "pre_norm_g": m_pre_norm_g, "hgrn_norm_g": m_hgrn_norm_g,
                 "fox_f_bias": m_fox_f_bias, "pool_w": m_pool_w, "pool_scale": m_pool_scale, "post_norm_g": m_post_norm_g}
    moments_v = {"lower_bounds": v_lower_bounds, "pre_norm_g": v_pre_norm_g, "hgrn_norm_g": v_hgrn_norm_g,
                 "fox_f_bias": v_fox_f_bias, "pool_w": v_pool_w, "pool_scale": v_pool_scale, "post_norm_g": v_post_norm_g}
    gp, dp, mp, vp = _small_update(gsum, lower_bounds, _pack(weights), _pack(moments_m), _pack(moments_v))
    gs, _ = _unpack(gp, weights)
    ds, _ = _unpack(dp, weights)
    ms, _ = _unpack(mp, weights)
    vs, _ = _unpack(vp, weights)

    first_layer, _ = lax.optimization_barrier((handed[0], (done[1], gp, dp, mp, vp)))
    done[0] = finish(0, *first_layer)
    halves_of = lambda j, side: jnp.stack([done[l][j][side] for l in range(DEPTH)])
    grad_w_in, d_in, m_in, v_in = _adamw(w_in, halves_of(0, 0), halves_of(0, 1), m_w_in, v_w_in, "adamw_w_in")
    grad_w_out, d_out, m_out, v_out = _adamw(w_out, halves_of(1, 0), halves_of(1, 1), m_w_out, v_w_out, "adamw_w_out")

    def ordered(s, big_in, big_out):
        return (s["lower_bounds"], s["pre_norm_g"], big_in, s["hgrn_norm_g"], s["fox_f_bias"], s["pool_w"],
                s["pool_scale"], big_out, s["post_norm_g"])

    return (loss, grad_x, *ordered(gs, grad_w_in, grad_w_out), *ordered(ds, d_in, d_out),
            *ordered(ms, m_in, m_out), *ordered(vs, v_in, v_out))
```

```python
import numpy as np
import jax
import jax.numpy as jnp
from jax import lax
from jax.experimental import pallas as pl
from jax.experimental.pallas import tpu as pltpu
from jax.experimental.pallas import tpu_sc as plsc

F32 = jnp.float32
BF16 = jnp.bfloat16
HI = lax.Precision.HIGHEST
MESH = pl.DeviceIdType.MESH

NORM_EPS = 1e-6
MASK_VALUE = -1e30
TINY = 1e-30
ADAM_LR, ADAM_B1, ADAM_B2, ADAM_EPS, ADAM_WD, ADAM_STEP = 0.001, 0.9, 0.999, 1e-08, 0.01, 10

D_MODEL = 1024
DEPTH = 2
N_CHIPS = 4
CHUNK = 64
LANES = 128
HGRN_W, POOL_W, FOX_W, FOX_HEADS = 256, 256, 512, 8
POOL_WINDOWS = (2, 4, 8, 16)
POOL_HALO = 16
IN_WIDTH = 3592
SHARD_W = IN_WIDTH // N_CHIPS
A_W, B_W, C_W, F_W = 1024, 512, 2048, 128
E_INT = A_W + B_W + C_W + F_W
B_BLK = A_W // 512
C_BLK0 = (A_W + B_W) // 512
F_BLK = (A_W + B_W + C_W) // 128


def _segments():
    segs = []
    for hp in range(2):
        for part in range(4):
            segs.append((part * 256 + hp * 128, 128))
    segs.append((1024, 256))
    segs.append((1280, 256))
    for hp in range(4):
        for part in range(4):
            segs.append((1536 + part * 512 + hp * 128, 128))
    segs.append((3584, 8))
    return segs


_SEGS = _segments()


def _internal_from_shards(shards):
    parts = []
    for s, n in _SEGS:
        while n > 0:
            q, r = divmod(s, SHARD_W)
            take = min(n, SHARD_W - r)
            parts.append(shards[q][..., r:r + take])
            s, n = s + take, n - take
    parts.append(jnp.zeros(shards[0].shape[:-1] + (E_INT - IN_WIDTH,), shards[0].dtype))
    return jnp.concatenate(parts, axis=-1)


def _shards_from_internal(w):
    offs, o = [], 0
    for s, n in _SEGS:
        offs.append((s, o, n))
        o += n
    blocks = []
    for q in range(N_CHIPS):
        lo, hi = SHARD_W * q, SHARD_W * (q + 1)
        parts = [w[..., o + max(lo, s) - s:o + min(hi, s + n) - s] for s, o, n in sorted(offs) if s < hi and s + n > lo]
        blocks.append(jnp.concatenate(parts, axis=-1))
    return jnp.stack(blocks)


def _cparams(sem=None, vmem_mb=48):
    kw = dict(vmem_limit_bytes=vmem_mb * 1024 * 1024)
    if sem is not None:
        kw["dimension_semantics"] = sem
    return pltpu.CompilerParams(**kw)


def _sig(x):
    return 1.0 / (1.0 + jnp.exp(-x))


def _silu(x):
    return x * _sig(x)


def _dsilu(x):
    s = _sig(x)
    return s * (1.0 + x * (1.0 - s))


def _rstd(x):
    return lax.rsqrt(jnp.mean(x * x, axis=-1, keepdims=True) + NORM_EPS)


def _dot(a, b, dims, **kw):
    return lax.dot_general(a, b, (dims, ((), ())), preferred_element_type=F32, **kw)


NN = ((1,), (0,))
NT = ((1,), (1,))
TN = ((0,), (0,))


def _iota(shape, dim):
    return lax.broadcasted_iota(jnp.int32, shape, dim)


def _lbs_fwd(lower_bounds):
    def body(a_ref, o_ref):
        a = a_ref[...]
        a0, a1 = a[0:1], a[1:2]
        m = jnp.maximum(a0, a1)
        e0, e1 = jnp.exp(a0 - m), jnp.exp(a1 - m)
        p0, p1 = e0 / (e0 + e1), e1 / (e0 + e1)
        o_ref[...] = jnp.concatenate([p0 - p0, (p0 + p1) - p0], axis=0)

    return pl.pallas_call(body, out_shape=jax.ShapeDtypeStruct(lower_bounds.shape, F32), name="lbs_fwd")(lower_bounds)


def _inproj_fwd(x2, g_row, w_int, name):
    n, d = x2.shape
    e = w_int.shape[1]
    tm = min(512, n)

    def body(x_ref, g_ref, w_ref, o_ref):
        x = x_ref[...]
        h = (x * _rstd(x) * g_ref[...]).astype(BF16)
        o_ref[...] = jnp.dot(h, w_ref[...], preferred_element_type=F32)

    return pl.pallas_call(
        body, grid=(n // tm,),
        in_specs=[pl.BlockSpec((tm, d), lambda i: (i, 0)), pl.BlockSpec((1, d), lambda i: (0, 0)),
                  pl.BlockSpec((d, e), lambda i: (0, 0))],
        out_specs=pl.BlockSpec((tm, e), lambda i: (i, 0)),
        out_shape=jax.ShapeDtypeStruct((n, e), F32),
        compiler_params=_cparams(("parallel",)), name=name)(x2, g_row, w_int)


def _hgrn_gates(a, lb):
    qa, z = a[:, 0:128], a[:, 128:256]
    sg, sgn = _sig(z), _sig(-z)
    fg = lb + (1.0 - lb) * sg
    lf = jnp.log(jnp.maximum(fg, TINY))
    kk = (1.0 - lb) * sgn
    return qa * _sig(qa), kk, lf, sg, sgn, fg


N_LEVELS = 6


def _hgrn_tables():
    t = np.arange(LANES)
    j = np.arange(LANES)[None, :]
    same_chunk = (t[:, None] // CHUNK) == (j // CHUNK)
    w = np.zeros((2 + N_LEVELS, LANES, LANES), np.float32)
    w[0] = same_chunk & (j <= t[:, None])
    w[1] = same_chunk & (j > t[:, None])
    maskf = np.zeros((N_LEVELS, LANES, LANES), np.float32)
    rightf = np.zeros((N_LEVELS, LANES, LANES), np.float32)
    for li in range(N_LEVELS):
        m = (CHUNK // 2) >> li
        start = t - (t % (2 * m))
        right = (t % (2 * m)) >= m
        first = np.where(right, start + m, t + 1)
        last = np.where(right, t, start + m - 1)
        w[2 + li] = (j >= first[:, None]) & (j <= last[:, None])
        maskf[li] = (t[:, None] // (2 * m)) == (j // (2 * m))
        rightf[li] = right[:, None]
    w = w[:-1]
    return jnp.asarray(w.reshape(-1, LANES), BF16), jnp.asarray(np.tile(maskf, (1, 2, 1))), jnp.asarray(rightf)


def _split(x, n):
    parts = []
    for _ in range(n - 1):
        p = x.astype(BF16)
        parts.append(p)
        x = x - p.astype(F32)
    parts.append(x.astype(BF16))
    return parts


def _exact_dot(w, parts):
    acc = jnp.dot(w, parts[0], preferred_element_type=F32)
    for p in parts[1:]:
        acc = acc + jnp.dot(w, p, preferred_element_type=F32)
    return acc


def _head_sums(v, ones_blk, n=2):
    parts = _split(v, n)
    acc = jnp.dot(parts[0], ones_blk, preferred_element_type=F32)
    for p in parts[1:]:
        acc = acc + jnp.dot(p, ones_blk, preferred_element_type=F32)
    return acc


def _hgrn_consts():
    r, c = _iota((LANES, LANES), 0), _iota((LANES, LANES), 1)
    ones_blk = ((r // CHUNK) == (c // CHUNK)).astype(BF16)
    eye2 = (_iota((2 * LANES, LANES), 0) % LANES) == _iota((2 * LANES, LANES), 1)
    first = _iota((1, LANES), 1) < CHUNK
    return eye2, ones_blk, jnp.ones((LANES, LANES), BF16), first


def _stack_heads(v, first):
    return jnp.concatenate([jnp.where(first, v, 0.0), jnp.where(first, 0.0, v)], axis=0)


def _pick_heads(v2, first):
    return jnp.where(first, v2[:LANES], v2[LANES:])


def _hgrn_levels(qq, kk, lf, zall, mk_ref, rt_ref, first, d_att=None):
    att = jnp.zeros((2 * LANES, LANES), F32)
    dq = dk = db = jnp.zeros((LANES, LANES), F32)
    for li in range(N_LEVELS):
        rt = rt_ref[li]
        e = jnp.exp(zall[(2 + li) * LANES:(3 + li) * LANES] if li < N_LEVELS - 1 else lf * rt)
        mk = mk_ref[li]
        qef, kef = e * rt, e * (1.0 - rt)
        qe, ke = (qq * qef).astype(BF16), (kk * kef).astype(BF16)
        qe2 = _stack_heads(qe, first)
        att = att + _dot(qe2, ke, NT) * mk
        if d_att is not None:
            dam = (d_att * mk).astype(BF16)
            dqe = _pick_heads(jnp.dot(dam, ke, preferred_element_type=F32), first)
            dke = _dot(dam, qe2, TN)
            dq = dq + dqe * qef
            dk = dk + dke * kef
            db = db + (dqe * qe.astype(F32) - dke * ke.astype(F32))
    return att, dq, dk, db


def _hgrn_fwd(proj3, lbs_row, gn_row, name):
    bsz, t, _ = proj3.shape
    nt = t // LANES
    w_all, maskf, rightf = _hgrn_tables()

    def body(a_ref, lb_ref, gn_ref, w_ref, mk_ref, rt_ref, og_ref, or_ref, st_ref):
        lb = lb_ref[...]
        gn = gn_ref[...]
        eye2, ones_blk, ones_all, first = _hgrn_consts()

        def tile(i, carry):
            r0 = pl.multiple_of(i * LANES, LANES)
            a = a_ref[pl.ds(r0, LANES), :]
            qq, kk, lf, _, _, _ = _hgrn_gates(a, lb)
            va, ga = a[:, 256:384], a[:, 384:512]
            parts = _split(lf, 2)
            zall = _exact_dot(w_ref[...], parts)
            eb, ee = jnp.exp(zall[0:LANES]), jnp.exp(zall[LANES:2 * LANES])
            vb = va.astype(BF16)
            att, _, _, _ = _hgrn_levels(qq, kk, lf, zall, mk_ref, rt_ref, first)
            diag = _head_sums(_stack_heads(qq * kk, first), ones_all)
            a2 = (att + jnp.where(eye2, diag, 0.0)).astype(BF16)
            o_in = _pick_heads(jnp.dot(a2, vb, preferred_element_type=F32), first)
            qeb, keb = (qq * eb).astype(BF16), (kk * ee).astype(BF16)
            new_s, o_heads = [], []
            for h in range(2):
                hs = slice(CHUNK * h, CHUNK * (h + 1))
                o_h = o_in[:, hs]
                st = carry[h]
                chunks = []
                for c in range(2):
                    rc = slice(CHUNK * c, CHUNK * (c + 1))
                    st_ref[h, 2 * i + c] = st
                    chunks.append(o_h[rc] + _dot(qeb[rc, hs], st.astype(BF16), NT))
                    ebl = eb[CHUNK * (c + 1) - 1:CHUNK * (c + 1), hs]
                    st = st * ebl + _dot(vb[rc, hs], keb[rc, hs], TN)
                new_s.append(st)
                o_heads.append(jnp.concatenate(chunks, axis=0))
            o = jnp.concatenate(o_heads, axis=1)
            ms = _head_sums(o * o, ones_blk) * (1.0 / CHUNK)
            or_ref[pl.ds(r0, LANES), :] = o
            og_ref[pl.ds(r0, LANES), :] = (o * lax.rsqrt(ms + NORM_EPS) * gn * _silu(ga)).astype(BF16)
            return tuple(new_s)

        zero = jnp.zeros((CHUNK, CHUNK), F32)
        per_step = 8 if nt % 8 == 0 else 2

        def step(i, carry):
            for k in range(per_step):
                carry = tile(per_step * i + k, carry)
            return carry

        lax.fori_loop(0, nt // per_step, step, (zero, zero))

    out = jax.ShapeDtypeStruct((bsz, t, HGRN_W), F32)
    row = pl.BlockSpec((1, 128), lambda b, p: (0, p))
    return pl.pallas_call(
        body, grid=(bsz, 2),
        in_specs=[pl.BlockSpec((None, t, 512), lambda b, p: (b, 0, p)), row, row,
                  pl.BlockSpec(w_all.shape, lambda b, p: (0, 0)),
                  pl.BlockSpec(maskf.shape, lambda b, p: (0, 0, 0)),
                  pl.BlockSpec(rightf.shape, lambda b, p: (0, 0, 0))],
        out_specs=[pl.BlockSpec((None, t, 128), lambda b, p: (b, 0, p)),
                   pl.BlockSpec((None, t, 128), lambda b, p: (b, 0, p)),
                   pl.BlockSpec((None, 2, t // CHUNK, CHUNK, CHUNK), lambda b, p: (b, p, 0, 0, 0))],
        out_shape=[jax.ShapeDtypeStruct((bsz, t, HGRN_W), BF16), out,
                   jax.ShapeDtypeStruct((bsz, 4, t // CHUNK, CHUNK, CHUNK), F32)],
        compiler_params=_cparams(("parallel", "parallel")), name=name)(proj3, lbs_row, gn_row, w_all, maskf, rightf)


def _hgrn_bwd(proj3, o_raw, dmixed, states, lbs_row, gn_row, name):
    bsz, t, _ = proj3.shape
    nt = t // LANES
    nchunk = t // CHUNK
    w_all, maskf, rightf = _hgrn_tables()

    def body(a_ref, or_ref, do_ref, s_sc, lb_ref, gn_ref, w_ref, mk_ref, rt_ref, da_ref, dgn_ref, dlb_ref):
        lb = lb_ref[...]
        gn = gn_ref[...]
        eye2, ones_blk, ones_all, first = _hgrn_consts()
        r_i, c_i = _iota((LANES, LANES), 0), _iota((LANES, LANES), 1)
        suffix = ((c_i >= r_i) & ((r_i // CHUNK) == (c_i // CHUNK))).astype(BF16)
        row64 = _iota((LANES, CHUNK), 0)
        zero = jnp.zeros((CHUNK, CHUNK), F32)

        def bwd_tile(k, carry):
            dst0, dst1, dgn_acc, dlb_acc = carry
            i = nt - 1 - k
            r0 = pl.multiple_of(i * LANES, LANES)
            a = a_ref[pl.ds(r0, LANES), :]
            qa, ga = a[:, 0:128], a[:, 384:512]
            qq, kk, lf, sg, sgn, fg = _hgrn_gates(a, lb)
            parts = _split(lf, 2)
            zall = _exact_dot(w_ref[...], parts)
            eb, ee = jnp.exp(zall[0:LANES]), jnp.exp(zall[LANES:2 * LANES])
            vb = a[:, 256:384].astype(BF16)
            oraw = or_ref[pl.ds(r0, LANES), :]
            dout = do_ref[pl.ds(r0, LANES), :]
            r = lax.rsqrt(_head_sums(oraw * oraw, ones_blk) * (1.0 / CHUNK) + NORM_EPS)
            xn = oraw * r
            dga = dout * (xn * gn) * _dsilu(ga)
            don = dout * _silu(ga)
            dgn_acc = dgn_acc + jnp.sum(don * xn, axis=0, keepdims=True)
            dxn = don * gn
            do = r * (dxn - xn * (_head_sums(dxn * xn, ones_blk) * (1.0 / CHUNK)))
            dob = do.astype(BF16)
            do2 = _stack_heads(dob, first)
            d_att = _dot(do2, vb, NT)
            att, dq, dk, db_lv = _hgrn_levels(qq, kk, lf, zall, mk_ref, rt_ref, first, d_att)
            a2 = att + jnp.where(eye2, _head_sums(_stack_heads(qq * kk, first), ones_all), 0.0)
            dv_in = _dot(a2.astype(BF16), do2, TN)
            ddiag = _pick_heads(_head_sums(jnp.where(eye2, d_att, 0.0), ones_all), first)
            dq_in, dk_in = dq + ddiag * kk, dk + ddiag * qq
            qe_f, ke_f = qq * eb, kk * ee
            qeb, keb = qe_f.astype(BF16), ke_f.astype(BF16)
            new_ds, dq_h, dk_h, dv_h, dbl_h = [], [], [], [], []
            for h in range(2):
                hs = slice(CHUNK * h, CHUNK * (h + 1))
                dv, dq_i, dk_i = dv_in[:, hs], dq_in[:, hs], dk_in[:, hs]
                dst = (dst0, dst1)[h]
                dq_c, dk_c, dv_c, dbl_c = [None, None], [None, None], [None, None], [None, None]
                for c in (1, 0):
                    rc = slice(CHUNK * c, CHUNK * (c + 1))
                    st_n = s_sc[h, 2 * i + c]
                    ebl = eb[CHUNK * (c + 1) - 1:CHUNK * (c + 1), hs]
                    dstb = dst.astype(BF16)
                    dv_c[c] = _dot(keb[rc, hs], dstb, NT)
                    dke = jnp.dot(vb[rc, hs], dstb, preferred_element_type=F32)
                    dqe = jnp.dot(dob[rc, hs], st_n.astype(BF16), preferred_element_type=F32)
                    dbl_c[c] = (jnp.sum(dst * st_n, axis=0, keepdims=True) * ebl
                                + jnp.sum(dke * ke_f[rc, hs], axis=0, keepdims=True))
                    dq_c[c], dk_c[c] = dqe * eb[rc, hs], dke * ee[rc, hs]
                    dst = dst * ebl + _dot(dob[rc, hs], qeb[rc, hs], TN)
                new_ds.append(dst)
                dq_x, dk_x = jnp.concatenate(dq_c, axis=0), jnp.concatenate(dk_c, axis=0)
                dq_h.append(dq_i + dq_x)
                dk_h.append(dk_i + dk_x)
                dv_h.append(dv + jnp.concatenate(dv_c, axis=0))
                dbl_h.append(qq[:, hs] * dq_x - kk[:, hs] * dk_x
                             + jnp.where(row64 == CHUNK - 1, dbl_c[0], 0.0) + jnp.where(row64 == LANES - 1, dbl_c[1], 0.0))
            dqq = jnp.concatenate(dq_h, axis=1)
            dkk = jnp.concatenate(dk_h, axis=1)
            dvv = jnp.concatenate(dv_h, axis=1)
            db = db_lv + jnp.concatenate(dbl_h, axis=1)
            dlf = _exact_dot(suffix, _split(db, 3))
            dqa = dqq * _dsilu(qa)
            dfg = jnp.where(fg > TINY, dlf / fg, 0.0)
            dz = (dfg - dkk) * (1.0 - lb) * sg * sgn
            dlb_acc = dlb_acc + jnp.sum(dfg * (1.0 - sg) - dkk * sgn, axis=0, keepdims=True)
            da_ref[pl.ds(r0, LANES), :] = jnp.concatenate([dqa, dz, dvv, dga], axis=1).astype(BF16)
            return new_ds[0], new_ds[1], dgn_acc, dlb_acc

        zrow = jnp.zeros((1, LANES), F32)
        per_step = 8 if nt % 8 == 0 else 2

        def step(k, carry):
            for r in range(per_step):
                carry = bwd_tile(per_step * k + r, carry)
            return carry

        _, _, dgn_acc, dlb_acc = lax.fori_loop(0, nt // per_step, step, (zero, zero, zrow, zrow))
        dgn_ref[...] = jnp.broadcast_to(dgn_acc, (8, LANES))
        dlb_ref[...] = jnp.broadcast_to(dlb_acc, (8, LANES))

    rows = jax.ShapeDtypeStruct((bsz, 8, HGRN_W), F32)
    row = pl.BlockSpec((1, 128), lambda b, p: (0, p))
    blk = pl.BlockSpec((None, t, 128), lambda b, p: (b, 0, p))
    return pl.pallas_call(
        body, grid=(bsz, 2),
        in_specs=[pl.BlockSpec((None, t, 512), lambda b, p: (b, 0, p)), blk, blk,
                  pl.BlockSpec((None, 2, nchunk, CHUNK, CHUNK), lambda b, p: (b, p, 0, 0, 0)), row, row,
                  pl.BlockSpec(w_all.shape, lambda b, p: (0, 0)),
                  pl.BlockSpec(maskf.shape, lambda b, p: (0, 0, 0)),
                  pl.BlockSpec(rightf.shape, lambda b, p: (0, 0, 0))],
        out_specs=[pl.BlockSpec((None, t, 512), lambda b, p: (b, 0, p)),
                   pl.BlockSpec((None, 8, 128), lambda b, p: (b, 0, p)),
                   pl.BlockSpec((None, 8, 128), lambda b, p: (b, 0, p))],
        out_shape=[jax.ShapeDtypeStruct((bsz, t, A_W), BF16), rows, rows],
        compiler_params=_cparams(("parallel", "parallel")), name=name)(
            proj3, o_raw, dmixed, states, lbs_row, gn_row, w_all, maskf, rightf)


def _pool_tt(t):
    return min(256, t)


def _window_select(s2, s4, s8, s16, lane):
    return jnp.where(lane < 64, s2, jnp.where(lane < 128, s4, jnp.where(lane < 192, s8, s16)))


def _pool_counts(t0, tt):
    lane = _iota((tt, POOL_W), 1)
    tpos = (_iota((tt, POOL_W), 0) + t0 + 1).astype(F32)
    win = jnp.where(lane < 64, 2.0, jnp.where(lane < 128, 4.0, jnp.where(lane < 192, 8.0, 16.0)))
    return 1.0 / jnp.minimum(tpos, win), lane


def _pooled_tile(upad_ref, i, tt):
    r0 = pl.multiple_of(i * tt, 8)
    cat = upad_ref[pl.ds(r0, tt + POOL_HALO), :]
    s2 = cat + pltpu.roll(cat, 1, 0)
    s4 = s2 + pltpu.roll(s2, 2, 0)
    s8 = s4 + pltpu.roll(s4, 4, 0)
    s16 = s8 + pltpu.roll(s8, 8, 0)
    inv, lane = _pool_counts(i * tt, tt)
    sel = _window_select(s2[POOL_HALO:], s4[POOL_HALO:], s8[POOL_HALO:], s16[POOL_HALO:], lane)
    return sel * inv - cat[POOL_HALO:], inv, lane


def _pool_fwd(proj3, wbd, scale_row, name):
    bsz, t, _ = proj3.shape
    tt = _pool_tt(t)

    def body(p_ref, w_ref, sc_ref, o_ref, upad):
        upad[0:POOL_HALO, :] = jnp.zeros((POOL_HALO, POOL_W), F32)
        upad[POOL_HALO:, :] = p_ref[:, 0:POOL_W]
        w = w_ref[...]
        sc = sc_ref[...]

        def tile(i, c):
            pooled, _, _ = _pooled_tile(upad, i, tt)
            r0 = pl.multiple_of(i * tt, 8)
            g = p_ref[pl.ds(r0, tt), POOL_W:2 * POOL_W]
            pre = jnp.dot(pooled.astype(BF16), w, preferred_element_type=F32)
            o_ref[pl.ds(r0, tt), :] = (pre * sc * _silu(g)).astype(BF16)
            return c

        lax.fori_loop(0, t // tt, tile, 0)

    return pl.pallas_call(
        body, grid=(bsz,),
        in_specs=[pl.BlockSpec((None, t, 512), lambda b: (b, 0, B_BLK)),
                  pl.BlockSpec((POOL_W, POOL_W), lambda b: (0, 0)),
                  pl.BlockSpec((1, POOL_W), lambda b: (0, 0))],
        out_specs=pl.BlockSpec((None, t, POOL_W), lambda b: (b, 0, 0)),
        out_shape=jax.ShapeDtypeStruct((bsz, t, POOL_W), BF16),
        scratch_shapes=[pltpu.VMEM((t + POOL_HALO, POOL_W), F32)],
        compiler_params=_cparams(("parallel",)), name=name)(proj3, wbd, scale_row)


def _pool_bwd(proj3, dmixed, wbd, scale_row, name):
    bsz, t, _ = proj3.shape
    tt = _pool_tt(t)

    def body(p_ref, do_ref, w_ref, sc_ref, db_ref, dsc_ref, dw_ref, upad, epad):
        upad[0:POOL_HALO, :] = jnp.zeros((POOL_HALO, POOL_W), F32)
        upad[POOL_HALO:, :] = p_ref[:, 0:POOL_W]
        epad[t:, :] = jnp.zeros((POOL_HALO, POOL_W), F32)
        w = w_ref[...]
        sc = sc_ref[...]

        def tile(i, carry):
            dsc_acc, dw_acc = carry
            pooled, inv, _ = _pooled_tile(upad, i, tt)
            r0 = pl.multiple_of(i * tt, 8)
            g = p_ref[pl.ds(r0, tt), POOL_W:2 * POOL_W]
            dout = do_ref[pl.ds(r0, tt), :]
            pb = pooled.astype(BF16)
            pre = jnp.dot(pb, w, preferred_element_type=F32)
            t1 = dout * _silu(g)
            dsc_acc = dsc_acc + jnp.sum(t1 * pre, axis=0, keepdims=True)
            dpre = (t1 * sc).astype(BF16)
            db_ref[pl.ds(r0, tt), POOL_W:2 * POOL_W] = (dout * pre * sc * _dsilu(g)).astype(BF16)
            dw_acc = dw_acc + _dot(pb, dpre, TN)
            dpooled = _dot(dpre, w, NT)
            epad[pl.ds(r0, tt), :] = dpooled * inv
            return dsc_acc, dw_acc

        dsc_acc, dw_acc = lax.fori_loop(0, t // tt, tile, (jnp.zeros((1, POOL_W), F32), jnp.zeros((POOL_W, POOL_W), F32)))
        dsc_ref[...] = jnp.broadcast_to(dsc_acc, (8, POOL_W))
        dw_ref[...] = dw_acc

        def tile2(i, c):
            r0 = pl.multiple_of(i * tt, 8)
            n = tt + POOL_HALO
            cat = epad[pl.ds(r0, n), :]
            s2 = cat + pltpu.roll(cat, n - 1, 0)
            s4 = s2 + pltpu.roll(s2, n - 2, 0)
            s8 = s4 + pltpu.roll(s4, n - 4, 0)
            s16 = s8 + pltpu.roll(s8, n - 8, 0)
            inv, lane = _pool_counts(i * tt, tt)
            sel = _window_select(s2[:tt], s4[:tt], s8[:tt], s16[:tt], lane)
            db_ref[pl.ds(r0, tt), 0:POOL_W] = (sel - cat[:tt] / inv).astype(BF16)
            return c

        lax.fori_loop(0, t // tt, tile2, 0)

    return pl.pallas_call(
        body, grid=(bsz,),
        in_specs=[pl.BlockSpec((None, t, 512), lambda b: (b, 0, B_BLK)),
                  pl.BlockSpec((None, t, POOL_W), lambda b: (b, 0, 1)),
                  pl.BlockSpec((POOL_W, POOL_W), lambda b: (0, 0)),
                  pl.BlockSpec((1, POOL_W), lambda b: (0, 0))],
        out_specs=[pl.BlockSpec((None, t, 512), lambda b: (b, 0, 0)),
                   pl.BlockSpec((None, 8, POOL_W), lambda b: (b, 0, 0)),
                   pl.BlockSpec((None, POOL_W, POOL_W), lambda b: (b, 0, 0))],
        out_shape=[jax.ShapeDtypeStruct((bsz, t, B_W), BF16), jax.ShapeDtypeStruct((bsz, 8, POOL_W), F32),
                   jax.ShapeDtypeStruct((bsz, POOL_W, POOL_W), F32)],
        scratch_shapes=[pltpu.VMEM((t + POOL_HALO, POOL_W), F32), pltpu.VMEM((t + POOL_HALO, POOL_W), F32)],
        compiler_params=_cparams(("parallel",)), name=name)(proj3, dmixed, wbd, scale_row)


def _head_select_rows(hp):
    r, c = _iota((8, LANES), 0), _iota((8, LANES), 1)
    return ((r < 2) & (c == 2 * hp + r)).astype(F32)


def _foxgate_fwd(proj3, bias_row, name):
    bsz, t, _ = proj3.shape
    nt = t // LANES

    def body(f_ref, b_ref, cn_ref, ct_ref):
        bias = b_ref[...]
        i, j = _iota((LANES, LANES), 0), _iota((LANES, LANES), 1)
        lower = (j <= i).astype(BF16)
        spread = (_iota((LANES, FOX_W), 0) == _iota((LANES, FOX_W), 1) // 64).astype(BF16)
        select = [_head_select_rows(hp).astype(BF16) for hp in range(4)]
        offset = jnp.zeros((1, LANES), F32)
        for k in range(nt):
            rows = slice(k * LANES, (k + 1) * LANES)
            xg = f_ref[rows, :] + bias
            lf = jnp.minimum(xg, 0.0) - jnp.log(1.0 + jnp.exp(-jnp.abs(xg)))
            c = _exact_dot(lower, _split(lf, 3)) + offset
            offset = c[LANES - 1:LANES, :]
            parts = _split(c, 3)
            cn_ref[rows, :] = _head_sums(c, spread, 3)
            for hp in range(4):
                acc = _dot(select[hp], parts[0], NT)
                for p in parts[1:]:
                    acc = acc + _dot(select[hp], p, NT)
                ct_ref[hp, :, rows] = acc

    return pl.pallas_call(
        body, grid=(bsz,),
        in_specs=[pl.BlockSpec((None, t, 128), lambda b: (b, 0, F_BLK)), pl.BlockSpec((1, 128), lambda b: (0, 0))],
        out_specs=[pl.BlockSpec((None, t, FOX_W), lambda b: (b, 0, 0)),
                   pl.BlockSpec((None, 4, 8, t), lambda b: (b, 0, 0, 0))],
        out_shape=[jax.ShapeDtypeStruct((bsz, t, FOX_W), F32), jax.ShapeDtypeStruct((bsz, 4, 8, t), F32)],
        compiler_params=_cparams(("parallel",)), name=name)(proj3, bias_row)


def _foxgate_bwd(proj3, dc_nat, bias_row, name):
    bsz, t, _ = proj3.shape
    nt = t // LANES

    def body(f_ref, dc_ref, b_ref, df_ref, dbias_ref, run_sc):
        bias = b_ref[...]
        i, j = _iota((LANES, LANES), 0), _iota((LANES, LANES), 1)
        upper = (j >= i).astype(F32)
        valid = _iota((1, LANES), 1) < FOX_HEADS
        run_sc[...] = jnp.zeros((8, LANES), F32)
        dbias_ref[...] = jnp.zeros((8, LANES), F32)

        def tile(k, c):
            r0 = pl.multiple_of((nt - 1 - k) * LANES, LANES)
            dc = dc_ref[pl.ds(r0, LANES), :] + jnp.where(i == LANES - 1, run_sc[0:1, :], 0.0)
            dlf = jnp.dot(upper, dc, precision=HI, preferred_element_type=F32)
            xg = f_ref[pl.ds(r0, LANES), :] + bias
            df = jnp.where(valid, dlf * _sig(-xg), 0.0)
            df_ref[pl.ds(r0, LANES), :] = df.astype(BF16)
            run_sc[...] = dlf[0:8, :]
            dbias_ref[...] += jnp.sum(df, axis=0, keepdims=True)
            return c

        lax.fori_loop(0, nt, tile, 0)

    blk = pl.BlockSpec((None, t, 128), lambda b: (b, 0, 0))
    return pl.pallas_call(
        body, grid=(bsz,),
        in_specs=[pl.BlockSpec((None, t, 128), lambda b: (b, 0, F_BLK)), blk, pl.BlockSpec((1, 128), lambda b: (0, 0))],
        out_specs=[blk, pl.BlockSpec((None, 8, 128), lambda b: (b, 0, 0))],
        out_shape=[jax.ShapeDtypeStruct((bsz, t, F_W), BF16), jax.ShapeDtypeStruct((bsz, 8, 128), F32)],
        scratch_shapes=[pltpu.VMEM((8, LANES), F32)],
        compiler_params=_cparams(("parallel",)), name=name)(proj3, dc_nat, bias_row)


def _fox_tile(t):
    return min(256, t)


def _fox_fwd(proj3, c_nat, c_t, name):
    bsz, t, _ = proj3.shape
    tq = tk = min(2 * _fox_tile(t), t)
    nq = t // tq

    def body(q_ref, kv_ref, cn_ref, ct_ref, og_ref, or_ref, lse_ref):
        i = pl.program_id(2)
        qblk = q_ref[...]
        first = _iota((1, 128), 1) < 64
        qv = qblk[:, 0:128] * 0.125
        qm = [jnp.where(first, qv, 0.0).astype(BF16), jnp.where(first, 0.0, qv).astype(BF16)]
        cqs = [cn_ref[:, 0:1], cn_ref[:, 64:65]]
        rows = _iota((tq, tk), 0) + i * tq

        def scores(j):
            c0 = pl.multiple_of(j * tk, tk)
            kb = kv_ref[pl.ds(c0, tk), 128:256].astype(BF16)
            return tuple(_dot(qm[h], kb, NT) + (cqs[h] - ct_ref[h:h + 1, pl.ds(c0, tk)]) for h in range(2))

        def absorb(j, state, s01, masked):
            c0 = pl.multiple_of(j * tk, tk)
            vblk = kv_ref[pl.ds(c0, tk), 256:384]
            vx = [jnp.where(first, vblk, 1.0).astype(BF16), jnp.where(first, 1.0, vblk).astype(BF16)]
            new = []
            for h in range(2):
                m, acc, s = state[2 * h], state[2 * h + 1], s01[h]
                if masked:
                    s = jnp.where(rows >= _iota((tq, tk), 1) + j * tk, s, MASK_VALUE)
                m_new = jnp.maximum(m, jnp.max(s, axis=1, keepdims=True))
                p = jnp.exp(s - m_new).astype(BF16)
                new += [m_new, jnp.exp(m - m_new) * acc + jnp.dot(p, vx[h], preferred_element_type=F32)]
            return tuple(new)

        init = (jnp.full((tq, 1), MASK_VALUE, F32), jnp.zeros((tq, 128), F32)) * 2
        n_full = (i * tq) // tk
        state = lax.fori_loop(0, n_full, lambda j, state: absorb(j, state, scores(j), False), init)
        m0, acc0, m1, acc1 = absorb(n_full, state, scores(n_full), True)
        l0, l1 = pltpu.roll(acc0, 64, 1), pltpu.roll(acc1, 64, 1)
        o = jnp.where(first, acc0 / l0, acc1 / l1)
        or_ref[...] = o
        og_ref[...] = (o * _silu(qblk[:, 384:512])).astype(BF16)
        lse_ref[...] = jnp.where(first, m0 + jnp.log(l0), m1 + jnp.log(l1))

    out = jax.ShapeDtypeStruct((bsz, t, FOX_W), F32)
    blk = pl.BlockSpec((None, tq, 128), lambda b, p, i: (b, i, p))
    return pl.pallas_call(
        body, grid=(bsz, 4, nq),
        in_specs=[pl.BlockSpec((None, tq, 512), lambda b, p, i: (b, i, C_BLK0 + p)),
                  pl.BlockSpec((None, t, 512), lambda b, p, i: (b, 0, C_BLK0 + p)),
                  blk,
                  pl.BlockSpec((None, None, 8, t), lambda b, p, i: (b, p, 0, 0))],
        out_specs=[blk, blk, blk],
        out_shape=[jax.ShapeDtypeStruct((bsz, t, FOX_W), BF16), out, out],
        compiler_params=_cparams(("parallel", "parallel", "arbitrary")), name=name)(proj3, proj3, c_nat, c_t)


def _fox_bwd(proj3, o_raw, dmixed, lse, c_nat, c_t, name):
    bsz, t, _ = proj3.shape
    tq = tk = min(2 * _fox_tile(t), t)
    nq = t // tq

    def body(a_ref, or_ref, do_ref, lse_ref, cn_ref, ct_ref, dc_out, dct_out, drow_out, dq_sc, do_sc, dl_sc):
        def prep(i, c):
            r0 = pl.multiple_of(i * tq, tq)
            g = a_ref[pl.ds(r0, tq), 384:512]
            dout = do_ref[pl.ds(r0, tq), :]
            o = or_ref[pl.ds(r0, tq), :]
            dc_out[pl.ds(r0, tq), 384:512] = (dout * o * _dsilu(g)).astype(BF16)
            do = dout * _silu(g)
            do_sc[pl.ds(r0, tq), :] = do
            prod = do * o
            d0 = jnp.sum(prod[:, 0:64], axis=1, keepdims=True)
            d1 = jnp.sum(prod[:, 64:128], axis=1, keepdims=True)
            dl_sc[pl.ds(r0, tq), :] = jnp.concatenate([jnp.broadcast_to(d0, (tq, 64)), jnp.broadcast_to(d1, (tq, 64))], axis=1)
            dq_sc[pl.ds(r0, tq), :] = jnp.zeros((tq, 128), F32)
            drow_out[pl.ds(r0, tq), :] = jnp.zeros((tq, 128), F32)
            return c

        lax.fori_loop(0, nq, prep, 0)
        dct_out[...] = jnp.zeros((8, t), F32)

        first = _iota((1, 128), 1) < 64

        def heads(v):
            return [jnp.where(first, v, 0.0).astype(BF16), jnp.where(first, 0.0, v).astype(BF16)]

        def kv_tile(j, c):
            c0 = pl.multiple_of(j * tk, tk)
            kb = a_ref[pl.ds(c0, tk), 128:256].astype(BF16)
            vb = a_ref[pl.ds(c0, tk), 256:384].astype(BF16)
            cks = [ct_ref[h:h + 1, pl.ds(c0, tk)] for h in range(2)]

            def q_step(i, carry, diagonal):
                dk, dv, dcol0, dcol1 = carry
                r0 = pl.multiple_of(i * tq, tq)
                causal = _iota((tq, tk), 0) + i * tq >= _iota((tq, tk), 1) + j * tk
                qv = a_ref[pl.ds(r0, tq), 0:128] * 0.125
                do = do_sc[pl.ds(r0, tq), :]
                qb, dob = qv.astype(BF16), do.astype(BF16)
                qm, dom = heads(qv), heads(do)
                full, dcols, rsums = [], [], []
                for h in range(2):
                    lse_h = lse_ref[pl.ds(r0, tq), 64 * h:64 * h + 1]
                    dl_h = dl_sc[pl.ds(r0, tq), 64 * h:64 * h + 1]
                    cq = cn_ref[pl.ds(r0, tq), 64 * h:64 * h + 1]
                    p = jnp.exp(_dot(qm[h], kb, NT) + (cq - cks[h]) - lse_h)
                    if diagonal:
                        p = jnp.where(causal, p, 0.0)
                    ds = p * (_dot(dom[h], vb, NT) - dl_h)
                    dsb = ds.astype(BF16)
                    full.append((_dot(p.astype(BF16), dob, TN), _dot(dsb, qb, TN),
                                 jnp.dot(dsb, kb, preferred_element_type=F32)))
                    dcols.append(jnp.sum(ds, axis=0, keepdims=True))
                    rsums.append(jnp.broadcast_to(jnp.sum(ds, axis=1, keepdims=True), (tq, 128)))
                dq_sc[pl.ds(r0, tq), :] += jnp.where(first, full[0][2], full[1][2]) * 0.125
                drow_out[pl.ds(r0, tq), :] += jnp.where(first, rsums[0], rsums[1])
                return (dk + jnp.where(first, full[0][1], full[1][1]), dv + jnp.where(first, full[0][0], full[1][0]),
                        dcol0 - dcols[0], dcol1 - dcols[1])

            carry = (jnp.zeros((tk, 128), F32), jnp.zeros((tk, 128), F32), jnp.zeros((1, tk), F32), jnp.zeros((1, tk), F32))
            carry = q_step(j, carry, True)
            dk, dv, dcol0, dcol1 = lax.fori_loop(j + 1, nq, lambda i, carry: q_step(i, carry, False), carry)
            dct_out[0:1, pl.ds(c0, tk)] = dcol0
            dct_out[1:2, pl.ds(c0, tk)] = dcol1
            dc_out[pl.ds(c0, tk), 128:256] = dk.astype(BF16)
            dc_out[pl.ds(c0, tk), 256:384] = dv.astype(BF16)
            return c

        lax.fori_loop(0, t // tk, kv_tile, 0)
        dc_out[:, 0:128] = dq_sc[...].astype(BF16)

    blk = pl.BlockSpec((None, t, 128), lambda b, p: (b, 0, p))
    return pl.pallas_call(
        body, grid=(bsz, 4),
        in_specs=[pl.BlockSpec((None, t, 512), lambda b, p: (b, 0, C_BLK0 + p)),
                  blk,
                  pl.BlockSpec((None, t, 128), lambda b, p: (b, 0, 4 + p)),
                  blk, blk,
                  pl.BlockSpec((None, None, 8, t), lambda b, p: (b, p, 0, 0))],
        out_specs=[pl.BlockSpec((None, t, 512), lambda b, p: (b, 0, p)),
                   pl.BlockSpec((None, None, 8, t), lambda b, p: (b, p, 0, 0)), blk],
        out_shape=[jax.ShapeDtypeStruct((bsz, t, C_W), BF16), jax.ShapeDtypeStruct((bsz, 4, 8, t), F32),
                   jax.ShapeDtypeStruct((bsz, t, FOX_W), F32)],
        scratch_shapes=[pltpu.VMEM((t, 128), F32), pltpu.VMEM((t, 128), F32), pltpu.VMEM((t, 128), F32)],
        compiler_params=_cparams(("parallel", "parallel")), name=name)(proj3, o_raw, dmixed, lse, c_nat, c_t)


def _mix_tm(n):
    return min(512, n)


def _outproj_fwd(x2, oa, ob, oc, wo, g_row, name):
    n, d = x2.shape
    tm = _mix_tm(n)

    def body(x_ref, oa_ref, ob_ref, oc_ref, w_ref, g_ref, y_ref, xo_ref):
        y = (jnp.dot(oa_ref[...].astype(BF16), w_ref[0:256, :], preferred_element_type=F32)
             + jnp.dot(ob_ref[...].astype(BF16), w_ref[256:512, :], preferred_element_type=F32)
             + jnp.dot(oc_ref[...].astype(BF16), w_ref[512:1024, :], preferred_element_type=F32))
        y_ref[...] = y
        xo_ref[...] = x_ref[...] + y * _rstd(y) * g_ref[...]

    row = lambda w: pl.BlockSpec((tm, w), lambda i: (i, 0))
    out = jax.ShapeDtypeStruct((n, d), F32)
    return pl.pallas_call(
        body, grid=(n // tm,),
        in_specs=[row(d), row(256), row(256), row(512), pl.BlockSpec((d, d), lambda i: (0, 0)),
                  pl.BlockSpec((1, d), lambda i: (0, 0))],
        out_specs=[row(d), row(d)], out_shape=[out, out],
        compiler_params=_cparams(("parallel",)), name=name)(x2, oa, ob, oc, wo, g_row)


def _outproj_fwd_loss(x2, oa, ob, oc, wo, g_row, target2, name):
    n, d = x2.shape
    tm = _mix_tm(n)

    def body(x_ref, oa_ref, ob_ref, oc_ref, w_ref, g_ref, t_ref, y_ref, dx_ref, l_ref):
        y = (jnp.dot(oa_ref[...].astype(BF16), w_ref[0:256, :], preferred_element_type=F32)
             + jnp.dot(ob_ref[...].astype(BF16), w_ref[256:512, :], preferred_element_type=F32)
             + jnp.dot(oc_ref[...].astype(BF16), w_ref[512:1024, :], preferred_element_type=F32))
        y_ref[...] = y
        err = (x_ref[...] + y * _rstd(y) * g_ref[...]) - t_ref[...]
        dx_ref[...] = err * (1.0 / d)

        @pl.when(pl.program_id(0) == 0)
        def _():
            l_ref[...] = jnp.zeros((8, 128), F32)

        l_ref[...] += jnp.sum(err * err)

    row = lambda w: pl.BlockSpec((tm, w), lambda i: (i, 0))
    out = jax.ShapeDtypeStruct((n, d), F32)
    return pl.pallas_call(
        body, grid=(n // tm,),
        in_specs=[row(d), row(256), row(256), row(512), pl.BlockSpec((d, d), lambda i: (0, 0)),
                  pl.BlockSpec((1, d), lambda i: (0, 0)), row(d)],
        out_specs=[row(d), row(d), pl.BlockSpec((8, 128), lambda i: (0, 0))],
        out_shape=[out, out, jax.ShapeDtypeStruct((8, 128), F32)],
        compiler_params=_cparams(("arbitrary",)), name=name)(x2, oa, ob, oc, wo, g_row, target2)


def _outproj_bwd(dxo, y, oa, ob, oc, wo, g_row, name):
    n, d = dxo.shape
    tm = _mix_tm(n)

    def body(dx_ref, y_ref, oa_ref, ob_ref, oc_ref, w_ref, g_ref, dm_ref, dw_ref, dg_ref):
        @pl.when(pl.program_id(0) == 0)
        def _():
            dw_ref[...] = jnp.zeros((d, d), F32)
            dg_ref[...] = jnp.zeros((8, d), F32)

        yv, dx = y_ref[...], dx_ref[...]
        r = _rstd(yv)
        yn = yv * r
        dg_ref[...] += jnp.sum(dx * yn, axis=0, keepdims=True)
        dyn = dx * g_ref[...]
        dy = (r * (dyn - yn * jnp.mean(dyn * yn, axis=-1, keepdims=True))).astype(BF16)
        dm_ref[...] = _dot(dy, w_ref[...], NT)
        dw_ref[0:256, :] += _dot(oa_ref[...].astype(BF16), dy, TN)
        dw_ref[256:512, :] += _dot(ob_ref[...].astype(BF16), dy, TN)
        dw_ref[512:1024, :] += _dot(oc_ref[...].astype(BF16), dy, TN)

    row = lambda w: pl.BlockSpec((tm, w), lambda i: (i, 0))
    fixed = lambda r, c: pl.BlockSpec((r, c), lambda i: (0, 0))
    return pl.pallas_call(
        body, grid=(n // tm,),
        in_specs=[row(d), row(d), row(256), row(256), row(512), fixed(d, d), fixed(1, d)],
        out_specs=[row(d), fixed(d, d), fixed(8, d)],
        out_shape=[jax.ShapeDtypeStruct((n, d), F32), jax.ShapeDtypeStruct((d, d), F32), jax.ShapeDtypeStruct((8, d), F32)],
        compiler_params=_cparams(("arbitrary",)), name=name)(dxo, y, oa, ob, oc, wo, g_row)


_PIECES = ((0, A_W), (A_W, B_W), (A_W + B_W, C_W), (A_W + B_W + C_W, F_W))


def _inproj_bwd_x(x2, dxo, g_row, w_int, pieces, name):
    n, d = x2.shape
    tm = min(512, n)

    def body(x_ref, dxo_ref, g_ref, w_ref, da_ref, db_ref, dc_ref, df_ref, dx_ref, dg_ref):
        @pl.when(pl.program_id(0) == 0)
        def _():
            dg_ref[...] = jnp.zeros((8, d), F32)

        dh = jnp.zeros((tm, d), F32)
        for ref, (o, w) in zip((da_ref, db_ref, dc_ref, df_ref), _PIECES):
            dh = dh + _dot(ref[...].astype(BF16), w_ref[:, o:o + w], NT)
        x = x_ref[...]
        r = _rstd(x)
        xn = x * r
        dg_ref[...] += jnp.sum(dh * xn, axis=0, keepdims=True)
        dxn = dh * g_ref[...]
        dx_ref[...] = dxo_ref[...] + r * (dxn - xn * jnp.mean(dxn * xn, axis=-1, keepdims=True))

    row = lambda w: pl.BlockSpec((tm, w), lambda i: (i, 0))
    fixed = lambda r, c: pl.BlockSpec((r, c), lambda i: (0, 0))
    return pl.pallas_call(
        body, grid=(n // tm,),
        in_specs=[row(d), row(d), fixed(1, d), fixed(d, E_INT)] + [row(w) for _, w in _PIECES],
        out_specs=[row(d), fixed(8, d)],
        out_shape=[jax.ShapeDtypeStruct((n, d), F32), jax.ShapeDtypeStruct((8, d), F32)],
        compiler_params=_cparams(("arbitrary",), vmem_mb=56), name=name)(x2, dxo, g_row, w_int, *pieces)


def _inproj_bwd_w(x2, g_row, pieces, name):
    n, d = x2.shape
    tm = min(512, n)

    def body(x_ref, g_ref, da_ref, db_ref, dc_ref, df_ref, dw_ref):
        @pl.when(pl.program_id(0) == 0)
        def _():
            dw_ref[...] = jnp.zeros((d, E_INT), F32)

        x = x_ref[...]
        h = (x * _rstd(x) * g_ref[...]).astype(BF16)
        for ref, (o, w) in zip((da_ref, db_ref, dc_ref, df_ref), _PIECES):
            dw_ref[:, o:o + w] += _dot(h, ref[...].astype(BF16), TN)

    row = lambda w: pl.BlockSpec((tm, w), lambda i: (i, 0))
    return pl.pallas_call(
        body, grid=(n // tm,),
        in_specs=[row(d), pl.BlockSpec((1, d), lambda i: (0, 0))] + [row(w) for _, w in _PIECES],
        out_specs=pl.BlockSpec((d, E_INT), lambda i: (0, 0)),
        out_shape=jax.ShapeDtypeStruct((d, E_INT), F32),
        compiler_params=_cparams(("arbitrary",), vmem_mb=56), name=name)(x2, g_row, *pieces)


def _block_diag(pool_w_l):
    z = jnp.zeros((64, 64), pool_w_l.dtype)
    return jnp.concatenate(
        [jnp.concatenate([pool_w_l[g] if c == g else z for c in range(4)], axis=1) for g in range(4)], axis=0)


def _pad_lanes(v, width=128):
    return jnp.pad(v, ((0, 0),) * (v.ndim - 1) + ((0, width - v.shape[-1]),))


def _local_step(x, target, lower_bounds, pre_norm_g, w_in_int, hgrn_norm_g, fox_f_bias, pool_w, pool_scale,
                w_out_bf, post_norm_g, on_weight_grads):
    bsz, t, d = x.shape
    n = bsz * t
    lbs = _lbs_fwd(lower_bounds)
    saved = []
    xc = x.reshape(n, d)
    for l in range(DEPTH):
        proj = _inproj_fwd(xc, pre_norm_g[l:l + 1], w_in_int[l], f"inproj_fwd{l}").reshape(bsz, t, E_INT)
        wbd = _block_diag(pool_w[l]).astype(BF16)
        bias_row = _pad_lanes(fox_f_bias[l:l + 1])
        oa, oa_raw, states = _hgrn_fwd(proj, lbs[l:l + 1], hgrn_norm_g[l:l + 1], f"hgrn_fwd{l}")
        ob = _pool_fwd(proj, wbd, pool_scale[l:l + 1], f"pool_fwd{l}")
        c_nat, c_t = _foxgate_fwd(proj, bias_row, f"foxgate_fwd{l}")
        oc, oc_raw, lse = _fox_fwd(proj, c_nat, c_t, f"fox_fwd{l}")
        mixed = (oa.reshape(n, -1), ob.reshape(n, -1), oc.reshape(n, -1))
        if l < DEPTH - 1:
            y, xn = _outproj_fwd(xc, *mixed, w_out_bf[l], post_norm_g[l:l + 1], f"outproj_fwd{l}")
        else:
            y, dx, sq = _outproj_fwd_loss(xc, *mixed, w_out_bf[l], post_norm_g[l:l + 1], target.reshape(n, d),
                                          f"outproj_fwd{l}")
        saved.append((xc, proj, wbd, bias_row, oa, oa_raw, states, ob, oc, oc_raw, lse, c_nat, c_t, y))
        xc = xn
    g = {k: [None] * DEPTH for k in ("pre", "hgn", "bias", "pool_w", "pool_scale", "post", "lbs")}
    handed = [None] * DEPTH
    for l in reversed(range(DEPTH)):
        xin, proj, wbd, bias_row, oa, oa_raw, states, ob, oc, oc_raw, lse, c_nat, c_t, y = saved[l]
        dmix, d_w_out, dpost = _outproj_bwd(dx, y, oa.reshape(n, -1), ob.reshape(n, -1), oc.reshape(n, -1),
                                            w_out_bf[l], post_norm_g[l:l + 1], f"outproj_bwd{l}")
        g["post"][l] = dpost[0]
        dmix3 = dmix.reshape(bsz, t, d)
        d_c, dct, drow = _fox_bwd(proj, oc_raw, dmix3, lse, c_nat, c_t, f"fox_bwd{l}")
        dc_nat = _pad_lanes(dct[:, :, 0:2, :].reshape(bsz, FOX_HEADS, t).transpose(0, 2, 1)
                            + drow.reshape(bsz, t, FOX_HEADS, 64)[..., 0])
        d_f, dbias = _foxgate_bwd(proj, dc_nat, bias_row, f"foxgate_bwd{l}")
        g["bias"][l] = jnp.sum(dbias[:, 0, :FOX_HEADS], axis=0)
        d_b, dscale, dwbd = _pool_bwd(proj, dmix3, wbd, pool_scale[l:l + 1], f"pool_bwd{l}")
        g["pool_scale"][l] = jnp.sum(dscale[:, 0], axis=0)
        dwbd = jnp.sum(dwbd, axis=0)
        g["pool_w"][l] = jnp.stack([dwbd[64 * k:64 * (k + 1), 64 * k:64 * (k + 1)] for k in range(4)])
        d_a, dgn, dlb = _hgrn_bwd(proj, oa_raw, dmix3, states, lbs[l:l + 1], hgrn_norm_g[l:l + 1], f"hgrn_bwd{l}")
        g["hgn"][l] = jnp.sum(dgn[:, 0], axis=0)
        g["lbs"][l] = jnp.sum(dlb[:, 0], axis=0)
        pieces = [p.reshape(n, -1) for p in (d_a, d_b, d_c, d_f)]
        handed[l] = on_weight_grads(l, _inproj_bwd_w(xin, pre_norm_g[l:l + 1], pieces, f"inproj_bwd_w{l}"), d_w_out)
        dx, dpre = _inproj_bwd_x(xin, dx, pre_norm_g[l:l + 1], w_in_int[l], pieces, f"inproj_bwd_x{l}")
        g["pre"][l] = dpre[0]
    grads = {k: jnp.stack(v) for k, v in g.items()}
    return sq, dx.reshape(bsz, t, d), grads, handed


def _place():
    return lax.axis_index("x"), lax.axis_index("y"), lax.axis_index("c")


def _other_chips(x, y):
    return [(1 - x, y), (x, 1 - y), (1 - x, 1 - y)]


_ANY = pl.BlockSpec(memory_space=pl.ANY)


def _gather_body(handshake, n_arrays):
    def body(*refs):
        srcs, dsts = refs[:n_arrays], refs[n_arrays:2 * n_arrays]
        ici_send, ici_recv, d2d_send, d2d_recv, local_sems = refs[2 * n_arrays:]
        x, y, c = _place()
        if handshake:
            barrier = pltpu.get_barrier_semaphore()
            for peer in [(px, py, c) for px, py in _other_chips(x, y)] + [(x, y, 1 - c)]:
                pl.semaphore_signal(barrier, inc=1, device_id=peer, device_id_type=MESH)
            pl.semaphore_wait(barrier, 4)
        me = 2 * x + y
        pairs = list(zip(srcs, dsts))
        order = [(k, j) for k in range(3) for j in range(n_arrays)]
        mine = [pltpu.make_async_copy(src, dst.at[me], local_sems.at[j]) for j, (src, dst) in enumerate(pairs)]
        for cp in mine:
            cp.start()
        chips = _other_chips(x, y)
        sends = [pltpu.make_async_remote_copy(
            src_ref=pairs[j][0].at[c], dst_ref=pairs[j][1].at[me, c], send_sem=ici_send.at[n], recv_sem=ici_recv.at[n],
            device_id=(chips[k][0], chips[k][1], c), device_id_type=MESH) for n, (k, j) in enumerate(order)]
        for cp in sends:
            cp.start()
        passed = [pltpu.make_async_remote_copy(
            src_ref=pairs[j][1].at[2 * chips[k][0] + chips[k][1], c], dst_ref=pairs[j][1].at[2 * chips[k][0] + chips[k][1], c],
            send_sem=d2d_send.at[n], recv_sem=d2d_recv.at[n], device_id=(x, y, 1 - c), device_id_type=MESH)
            for n, (k, j) in enumerate(order)]
        for n, (k, j) in enumerate(order):
            px, py = chips[k]
            src, dst = pairs[j]
            pltpu.make_async_remote_copy(
                src_ref=src.at[c], dst_ref=dst.at[2 * px + py, c], send_sem=ici_send.at[n], recv_sem=ici_recv.at[n],
                device_id=(px, py, c), device_id_type=MESH).wait_recv()
            passed[n].start()
        for n, (k, j) in enumerate(order):
            px, py = chips[k]
            src, dst = pairs[j]
            pltpu.make_async_remote_copy(
                src_ref=dst.at[2 * px + py, 1 - c], dst_ref=dst.at[2 * px + py, 1 - c], send_sem=d2d_send.at[n],
                recv_sem=d2d_recv.at[n], device_id=(x, y, 1 - c), device_id_type=MESH).wait_recv()
        for cp in sends + passed:
            cp.wait_send()
        for cp in mine:
            cp.wait()

    return body


def _gather_sems(n_arrays):
    return [pltpu.SemaphoreType.DMA((3 * n_arrays,))] * 4 + [pltpu.SemaphoreType.DMA((n_arrays,))]


def _gathered(a):
    return jax.ShapeDtypeStruct((N_CHIPS,) + a.shape, a.dtype)


def _gather_weights(arrays):
    n = len(arrays)
    return pl.pallas_call(
        _gather_body(False, n), in_specs=[_ANY] * n, out_specs=[_ANY] * n, out_shape=[_gathered(a) for a in arrays],
        scratch_shapes=_gather_sems(n), name="gather_weights")(*arrays)


def _gather_weights_beside(arrays):
    hbm = pltpu.MemorySpace.HBM
    n = len(arrays)
    srcs = [jax.new_ref(a, memory_space=hbm) for a in arrays]
    dsts = [jax.empty_ref(_gathered(a), memory_space=hbm) for a in arrays]
    body = _gather_body(True, n)

    @pl.kernel(mesh=plsc.ScalarSubcoreMesh(axis_name="sequencer", num_cores=1), name="gather_weights_beside",
               scratch_types=_gather_sems(n), compiler_params=pltpu.CompilerParams(collective_id=1))
    def launch(*sems):
        body(*srcs, *dsts, *sems)

    launch()
    return [d[...] for d in dsts]


def _swap_with_sibling(parts, name):
    k = len(parts)

    def body(*refs):
        src, dst = refs[:k], refs[k:2 * k]
        send_sems, recv_sems = refs[2 * k:]
        x, y, c = _place()
        cps = [pltpu.make_async_remote_copy(src_ref=src[j], dst_ref=dst[j], send_sem=send_sems.at[j], recv_sem=recv_sems.at[j],
                                            device_id=(x, y, 1 - c), device_id_type=MESH) for j in range(k)]
        for cp in cps:
            cp.start()
        for cp in cps:
            cp.wait()

    return pl.pallas_call(
        body, in_specs=[_ANY] * k, out_specs=[_ANY] * k,
        out_shape=[jax.ShapeDtypeStruct(p.shape, p.dtype) for p in parts],
        scratch_shapes=[pltpu.SemaphoreType.DMA((k,)), pltpu.SemaphoreType.DMA((k,))], name=name)(*parts)


N_PEERS = 7


def _grad_exchange_body():
    def body(pin_ref, pout_ref, lin_ref, lout_ref, send_sems, recv_sems):
        x, y, c = _place()
        barrier = pltpu.get_barrier_semaphore()
        for k in range(1, N_PEERS + 1):
            peer = (x ^ ((k >> 2) & 1), y ^ ((k >> 1) & 1), c ^ (k & 1))
            pl.semaphore_signal(barrier, inc=1, device_id=peer, device_id_type=MESH)
        pl.semaphore_wait(barrier, N_PEERS)
        me = 2 * x + y
        pairs = ((pin_ref, lin_ref), (pout_ref, lout_ref))
        cps = []
        for k, (px, py) in enumerate(_other_chips(x, y)):
            for r in range(2):
                for j, (src, dst) in enumerate(pairs):
                    cps.append(pltpu.make_async_remote_copy(
                        src_ref=src.at[2 * px + py, r], dst_ref=dst.at[2 * k + c], send_sem=send_sems.at[2 * (2 * k + r) + j],
                        recv_sem=recv_sems.at[2 * (2 * k + c) + j], device_id=(px, py, r), device_id_type=MESH))
        for j, (src, dst) in enumerate(pairs):
            cps.append(pltpu.make_async_remote_copy(
                src_ref=src.at[me, 1 - c], dst_ref=dst.at[N_PEERS - 1], send_sem=send_sems.at[2 * (N_PEERS - 1) + j],
                recv_sem=recv_sems.at[2 * (N_PEERS - 1) + j], device_id=(x, y, 1 - c), device_id_type=MESH))
        for cp in cps:
            cp.start()
        for s in range(N_PEERS):
            for j, (src, dst) in enumerate(pairs):
                pltpu.make_async_remote_copy(
                    src_ref=src.at[0, 0], dst_ref=dst.at[s], send_sem=send_sems.at[2 * s + j], recv_sem=recv_sems.at[2 * s + j],
                    device_id=(x, y, 1 - c), device_id_type=MESH).wait_recv()
        for cp in cps:
            cp.wait_send()

    return body


_EXCHANGE_SEMS = [pltpu.SemaphoreType.DMA((2 * N_PEERS,))] * 2


def _landing(p):
    return jax.ShapeDtypeStruct((N_PEERS,) + p.shape[2:], p.dtype)


def _grad_exchange_beside(pin, pout, name, collective_id):
    hbm = pltpu.MemorySpace.HBM
    pin_ref, pout_ref = jax.new_ref(pin, memory_space=hbm), jax.new_ref(pout, memory_space=hbm)
    lin_ref, lout_ref = jax.empty_ref(_landing(pin), memory_space=hbm), jax.empty_ref(_landing(pout), memory_space=hbm)
    body = _grad_exchange_body()

    @pl.kernel(mesh=plsc.ScalarSubcoreMesh(axis_name="sequencer", num_cores=1), name=name,
               scratch_types=_EXCHANGE_SEMS, compiler_params=pltpu.CompilerParams(collective_id=collective_id))
    def launch(send_sems, recv_sems):
        body(pin_ref, pout_ref, lin_ref, lout_ref, send_sems, recv_sems)

    launch()
    return lin_ref[...], lout_ref[...]


def _add_n(parts, name):
    r, c = parts[0].shape
    tr = 256 if r % 256 == 0 else r
    n = len(parts)

    def body(*refs):
        acc = refs[0][...].astype(F32)
        for ref in refs[1:n]:
            acc = acc + ref[...].astype(F32)
        refs[n][...] = acc

    blk = pl.BlockSpec((tr, c), lambda i: (i, 0))
    return pl.pallas_call(
        body, grid=(r // tr,), in_specs=[blk] * n, out_specs=blk, out_shape=jax.ShapeDtypeStruct((r, c), F32),
        compiler_params=_cparams(("parallel",)), name=name)(*parts)


def _all_reduce_small(packet):
    r, w = packet.shape

    def body(p_ref, o_ref, buf, send_sems, recv_sems):
        x, y, c = _place()
        me = 4 * x + 2 * y + c
        buf[me] = p_ref[...]
        peers = []
        for k in range(1, 8):
            fx, fy, fc = (k >> 2) & 1, (k >> 1) & 1, k & 1
            peers.append((x ^ fx, y ^ fy, c ^ fc))
        cps = [pltpu.make_async_remote_copy(src_ref=p_ref, dst_ref=buf.at[me], send_sem=send_sems.at[k], recv_sem=recv_sems.at[k],
                                            device_id=peer, device_id_type=MESH) for k, peer in enumerate(peers)]
        for cp in cps:
            cp.start()
        for k, (px, py, pc) in enumerate(peers):
            pltpu.make_async_remote_copy(src_ref=p_ref, dst_ref=buf.at[4 * px + 2 * py + pc], send_sem=send_sems.at[k],
                                         recv_sem=recv_sems.at[k], device_id=(px, py, pc), device_id_type=MESH).wait_recv()
        for cp in cps:
            cp.wait_send()
        acc = buf[0]
        for k in range(1, 8):
            acc = acc + buf[k]
        o_ref[...] = acc

    vm = pl.BlockSpec(memory_space=pltpu.VMEM)
    return pl.pallas_call(
        body, in_specs=[vm], out_specs=vm, out_shape=jax.ShapeDtypeStruct((r, w), F32),
        scratch_shapes=[pltpu.VMEM((8, r, w), F32), pltpu.SemaphoreType.DMA((7,)), pltpu.SemaphoreType.DMA((7,))],
        name="all_reduce_small")(packet)


def _adamw_math(w, g, m, v):
    m = ADAM_B1 * m + (1.0 - ADAM_B1) * g
    v = ADAM_B2 * v + (1.0 - ADAM_B2) * (g * g)
    m_hat = m / (1.0 - ADAM_B1 ** ADAM_STEP)
    v_hat = v / (1.0 - ADAM_B2 ** ADAM_STEP)
    return -ADAM_LR * (m_hat / (jnp.sqrt(v_hat) + ADAM_EPS) + ADAM_WD * w), m, v


def _adamw(w, g_lower, g_upper, m, v, name):
    nl, r, c = w.shape
    tr = 128
    per_half = r // (2 * tr)

    def body(w_ref, lo_ref, up_ref, m_ref, v_ref, g_ref, d_ref, mo_ref, vo_ref):
        g = jnp.where(pl.program_id(1) == 0, lo_ref[...], up_ref[...])
        g_ref[...] = g
        d_ref[...], mo_ref[...], vo_ref[...] = _adamw_math(w_ref[...], g, m_ref[...], v_ref[...])

    blk = pl.BlockSpec((None, tr, c), lambda l, h, i: (l, h * per_half + i, 0))
    half = pl.BlockSpec((None, tr, c), lambda l, h, i: (l, i, 0))
    out = jax.ShapeDtypeStruct(w.shape, F32)
    return pl.pallas_call(
        body, grid=(nl, 2, per_half), in_specs=[blk, half, half, blk, blk], out_specs=[blk] * 4, out_shape=[out] * 4,
        compiler_params=_cparams(("parallel", "parallel", "parallel")), name=name)(w, g_lower, g_upper, m, v)


def _small_update(gsum, lower_bounds, wpack, mpack, vpack):
    r, w = gsum.shape
    lb_rows = DEPTH * HGRN_W // 128

    def body(g_ref, a_ref, w_ref, m_ref, v_ref, go_ref, d_ref, mo_ref, vo_ref):
        a = a_ref[...]
        a0, a1 = a[0:1], a[1:2]
        mx = jnp.maximum(a0, a1)
        e0, e1 = jnp.exp(a0 - mx), jnp.exp(a1 - mx)
        p0, p1 = e0 / (e0 + e1), e1 / (e0 + e1)
        g = g_ref[...]
        half = lb_rows // 2
        dl0 = jnp.concatenate([g[k:k + 1] for k in range(half)], axis=1)
        dl1 = jnp.concatenate([g[half + k:half + k + 1] for k in range(half)], axis=1)
        dp0 = (dl0 + dl1) - (dl0 + dl1)
        dp1 = dl1
        inner = p0 * dp0 + p1 * dp1
        da0, da1 = p0 * (dp0 - inner), p1 * (dp1 - inner)
        rows = [da0[:, 128 * k:128 * (k + 1)] for k in range(half)] + [da1[:, 128 * k:128 * (k + 1)] for k in range(half)]
        gfull = jnp.concatenate(rows + [g[lb_rows:]], axis=0)
        go_ref[...] = gfull
        d_ref[...], mo_ref[...], vo_ref[...] = _adamw_math(w_ref[...], gfull, m_ref[...], v_ref[...])

    vm = pl.BlockSpec(memory_space=pltpu.VMEM)
    out = jax.ShapeDtypeStruct((r, w), F32)
    return pl.pallas_call(body, in_specs=[vm] * 5, out_specs=[vm] * 4, out_shape=[out] * 4, name="small_update")(
        gsum, lower_bounds, wpack, mpack, vpack)


_SMALL = ("lower_bounds", "pre_norm_g", "hgrn_norm_g", "fox_f_bias", "pool_w", "pool_scale", "post_norm_g")


def _pack(parts):
    rows = []
    for k in _SMALL:
        f = parts[k].reshape(-1)
        pad = (-f.shape[0]) % (8 * 128)
        rows.append(jnp.pad(f, (0, pad)).reshape(-1, 128))
    rows.append(jnp.zeros((8, 128), F32))
    return jnp.concatenate(rows, axis=0)


def _unpack(pack, like):
    out, r = {}, 0
    for k in _SMALL:
        size = int(np.prod(like[k].shape))
        nr = -(-size // (8 * 128)) * 8
        out[k] = pack[r:r + nr].reshape(-1)[:size].reshape(like[k].shape)
        r += nr
    return out, r


def kernel(x, lower_bounds, pre_norm_g, w_in, hgrn_norm_g, fox_f_bias, pool_w, pool_scale, w_out, post_norm_g, loss_target, m_lower_bounds, m_pre_norm_g, m_w_in, m_hgrn_norm_g, m_fox_f_bias, m_pool_w, m_pool_scale, m_w_out, m_post_norm_g, v_lower_bounds, v_pre_norm_g, v_w_in, v_hgrn_norm_g, v_fox_f_bias, v_pool_w, v_pool_scale, v_w_out, v_post_norm_g):
    cx, cy, cc = _place()
    chip = 2 * cx + cy

    halves = lambda w, l: w[l].reshape(2, w.shape[1] // 2, w.shape[2]).astype(BF16)
    needed_first = _gather_weights([halves(w_in, 0)])
    needed_first, later = lax.optimization_barrier((needed_first, [halves(w_out, 0), halves(w_in, 1), halves(w_out, 1)]))
    later = _gather_weights_beside(later)
    w_in_int = [_internal_from_shards([a[q].reshape(D_MODEL, SHARD_W) for q in range(N_CHIPS)]) for a in (needed_first[0], later[1])]
    w_out_full = [a.reshape(D_MODEL, D_MODEL) for a in (later[0], later[2])]

    def on_weight_grads(l, d_w_in, d_w_out):
        pin = _shards_from_internal(d_w_in).reshape(N_CHIPS, 2, D_MODEL // 2, SHARD_W)
        pout = d_w_out.reshape(N_CHIPS, 2, D_MODEL // (2 * N_CHIPS), D_MODEL)
        own = [lax.dynamic_index_in_dim(lax.dynamic_index_in_dim(p, chip, 0, False), cc, 0, False) for p in (pin, pout)]
        return own, _grad_exchange_beside(pin.astype(BF16), pout.astype(BF16), f"grad_exchange{l}", 2 + l)

    sq, grad_x, g, handed = _local_step(x, loss_target, lower_bounds, pre_norm_g, w_in_int, hgrn_norm_g, fox_f_bias,
                                        pool_w, pool_scale, w_out_full, post_norm_g, on_weight_grads)
    first = cc == 0

    def finish(l, own, landed):
        mine = [_add_n([o] + [t[s] for s in range(N_PEERS)], f"grad_sum{l}_{j}") for j, (o, t) in enumerate(zip(own, landed))]
        theirs = _swap_with_sibling(mine, f"grad_swap{l}")
        return [(jnp.where(first, h, o), jnp.where(first, o, h)) for h, o in zip(mine, theirs)]

    grad_x, last = lax.optimization_barrier((grad_x, handed[1]))
    done = [None, finish(1, *last)]

    small = {"lower_bounds": g["lbs"], "pre_norm_g": g["pre"], "hgrn_norm_g": g["hgn"], "fox_f_bias": g["bias"],
             "pool_w": g["pool_w"], "pool_scale": g["pool_scale"], "post_norm_g": g["post"]}
    packet = _pack(small)
    nrows = packet.shape[0]
    packet = packet.at[nrows - 1].set(sq[0])
    gsum = _all_reduce_small(packet)
    loss = gsum[nrows - 1, 0] * (0.5 / D_MODEL)

    weights = {"lower_bounds": lower_bounds, "pre_norm_g": pre_norm_g, "hgrn_norm_g": hgrn_norm_g,
               "fox_f_bias": fox_f_bias, "pool_w": pool_w, "pool_scale": pool_scale, "post_norm_g": post_norm_g}
    moments_m = {"lower_bounds": m_lower_bounds, "pre_norm_g": m_pre_norm_g, "hgrn_norm_g": m_hgrn_norm_g,
                 "fox_f_bias": m_fox_f_bias, "pool_w": m_pool_w, "pool_scale": m_pool_scale, "post_norm_g": m_post_norm_g}
    moments_v = {"lower_bounds": v_lower_bounds, "pre_norm_g": v_pre_norm_g, "hgrn_norm_g": v_hgrn_norm_g,
                 "fox_f_bias": v_fox_f_bias, "pool_w": v_pool_w, "pool_scale": v_pool_scale, "post_norm_g": v_post_norm_g}
    gp, dp, mp, vp = _small_update(gsum, lower_bounds, _pack(weights), _pack(moments_m), _pack(moments_v))
    gs, _ = _unpack(gp, weights)
    ds, _ = _unpack(dp, weights)
    ms, _ = _unpack(mp, weights)
    vs, _ = _unpack(vp, weights)

    first_layer, _ = lax.optimization_barrier((handed[0], (done[1], gp, dp, mp, vp)))
    done[0] = finish(0, *first_layer)
    halves_of = lambda j, side: jnp.stack([done[l][j][side] for l in range(DEPTH)])
    grad_w_in, d_in, m_in, v_in = _adamw(w_in, halves_of(0, 0), halves_of(0, 1), m_w_in, v_w_in, "adamw_w_in")
    grad_w_out, d_out, m_out, v_out = _adamw(w_out, halves_of(1, 0), halves_of(1, 1), m_w_out, v_w_out, "adamw_w_out")

    def ordered(s, big_in, big_out):
        return (s["lower_bounds"], s["pre_norm_g"], big_in, s["hgrn_norm_g"], s["fox_f_bias"], s["pool_w"],
                s["pool_scale"], big_out, s["post_norm_g"])

    return (loss, grad_x, *ordered(gs, grad_w_in, grad_w_out), *ordered(ds, d_in, d_out),
            *ordered(ms, m_in, m_out), *ordered(vs, v_in, v_out))
```

```python
import numpy as np
import jax
import jax.numpy as jnp
from jax import lax
from jax.experimental import pallas as pl
from jax.experimental.pallas import tpu as pltpu
from jax.experimental.pallas import tpu_sc as plsc

F32 = jnp.float32
BF16 = jnp.bfloat16
HI = lax.Precision.HIGHEST
MESH = pl.DeviceIdType.MESH

NORM_EPS = 1e-6
MASK_VALUE = -1e30
TINY = 1e-30
ADAM_LR, ADAM_B1, ADAM_B2, ADAM_EPS, ADAM_WD, ADAM_STEP = 0.001, 0.9, 0.999, 1e-08, 0.01, 10

D_MODEL = 1024
DEPTH = 2
N_CHIPS = 4
CHUNK = 64
LANES = 128
HGRN_W, POOL_W, FOX_W, FOX_HEADS = 256, 256, 512, 8
POOL_WINDOWS = (2, 4, 8, 16)
POOL_HALO = 16
IN_WIDTH = 3592
SHARD_W = IN_WIDTH // N_CHIPS
A_W, B_W, C_W, F_W = 1024, 512, 2048, 128
E_INT = A_W + B_W + C_W + F_W
B_BLK = A_W // 512
C_BLK0 = (A_W + B_W) // 512
F_BLK = (A_W + B_W + C_W) // 128


def _segments():
    segs = []
    for hp in range(2):
        for part in range(4):
            segs.append((part * 256 + hp * 128, 128))
    segs.append((1024, 256))
    segs.append((1280, 256))
    for hp in range(4):
        for part in range(4):
            segs.append((1536 + part * 512 + hp * 128, 128))
    segs.append((3584, 8))
    return segs


_SEGS = _segments()


def _internal_from_shards(shards):
    parts = []
    for s, n in _SEGS:
        while n > 0:
            q, r = divmod(s, SHARD_W)
            take = min(n, SHARD_W - r)
            parts.append(shards[q][..., r:r + take])
            s, n = s + take, n - take
    parts.append(jnp.zeros(shards[0].shape[:-1] + (E_INT - IN_WIDTH,), shards[0].dtype))
    return jnp.concatenate(parts, axis=-1)


def _shards_from_internal(w):
    offs, o = [], 0
    for s, n in _SEGS:
        offs.append((s, o, n))
        o += n
    blocks = []
    for q in range(N_CHIPS):
        lo, hi = SHARD_W * q, SHARD_W * (q + 1)
        parts = [w[..., o + max(lo, s) - s:o + min(hi, s + n) - s] for s, o, n in sorted(offs) if s < hi and s + n > lo]
        blocks.append(jnp.concatenate(parts, axis=-1))
    return jnp.stack(blocks)


def _cparams(sem=None, vmem_mb=48):
    kw = dict(vmem_limit_bytes=vmem_mb * 1024 * 1024)
    if sem is not None:
        kw["dimension_semantics"] = sem
    return pltpu.CompilerParams(**kw)


def _sig(x):
    return 1.0 / (1.0 + jnp.exp(-x))


def _silu(x):
    return x * _sig(x)


def _dsilu(x):
    s = _sig(x)
    return s * (1.0 + x * (1.0 - s))


def _rstd(x):
    return lax.rsqrt(jnp.mean(x * x, axis=-1, keepdims=True) + NORM_EPS)


def _dot(a, b, dims, **kw):
    return lax.dot_general(a, b, (dims, ((), ())), preferred_element_type=F32, **kw)


NN = ((1,), (0,))
NT = ((1,), (1,))
TN = ((0,), (0,))


def _iota(shape, dim):
    return lax.broadcasted_iota(jnp.int32, shape, dim)


def _lbs_fwd(lower_bounds):
    def body(a_ref, o_ref):
        a = a_ref[...]
        a0, a1 = a[0:1], a[1:2]
        m = jnp.maximum(a0, a1)
        e0, e1 = jnp.exp(a0 - m), jnp.exp(a1 - m)
        p0, p1 = e0 / (e0 + e1), e1 / (e0 + e1)
        o_ref[...] = jnp.concatenate([p0 - p0, (p0 + p1) - p0], axis=0)

    return pl.pallas_call(body, out_shape=jax.ShapeDtypeStruct(lower_bounds.shape, F32), name="lbs_fwd")(lower_bounds)


def _inproj_fwd(x2, g_row, w_int, name):
    n, d = x2.shape
    e = w_int.shape[1]
    tm = min(512, n)

    def body(x_ref, g_ref, w_ref, o_ref):
        x = x_ref[...]
        h = (x * _rstd(x) * g_ref[...]).astype(BF16)
        o_ref[...] = jnp.dot(h, w_ref[...], preferred_element_type=F32)

    return pl.pallas_call(
        body, grid=(n // tm,),
        in_specs=[pl.BlockSpec((tm, d), lambda i: (i, 0)), pl.BlockSpec((1, d), lambda i: (0, 0)),
                  pl.BlockSpec((d, e), lambda i: (0, 0))],
        out_specs=pl.BlockSpec((tm, e), lambda i: (i, 0)),
        out_shape=jax.ShapeDtypeStruct((n, e), F32),
        compiler_params=_cparams(("parallel",)), name=name)(x2, g_row, w_int)


def _hgrn_gates(a, lb):
    qa, z = a[:, 0:128], a[:, 128:256]
    sg, sgn = _sig(z), _sig(-z)
    fg = lb + (1.0 - lb) * sg
    lf = jnp.log(jnp.maximum(fg, TINY))
    kk = (1.0 - lb) * sgn
    return qa * _sig(qa), kk, lf, sg, sgn, fg


N_LEVELS = 6


def _hgrn_tables():
    t = np.arange(LANES)
    j = np.arange(LANES)[None, :]
    same_chunk = (t[:, None] // CHUNK) == (j // CHUNK)
    w = np.zeros((2 + N_LEVELS, LANES, LANES), np.float32)
    w[0] = same_chunk & (j <= t[:, None])
    w[1] = same_chunk & (j > t[:, None])
    maskf = np.zeros((N_LEVELS, LANES, LANES), np.float32)
    rightf = np.zeros((N_LEVELS, LANES, LANES), np.float32)
    for li in range(N_LEVELS):
        m = (CHUNK // 2) >> li
        start = t - (t % (2 * m))
        right = (t % (2 * m)) >= m
        first = np.where(right, start + m, t + 1)
        last = np.where(right, t, start + m - 1)
        w[2 + li] = (j >= first[:, None]) & (j <= last[:, None])
        maskf[li] = (t[:, None] // (2 * m)) == (j // (2 * m))
        rightf[li] = right[:, None]
    w = w[:-1]
    return jnp.asarray(w.reshape(-1, LANES), BF16), jnp.asarray(np.tile(maskf, (1, 2, 1))), jnp.asarray(rightf)


def _split(x, n):
    parts = []
    for _ in range(n - 1):
        p = x.astype(BF16)
        parts.append(p)
        x = x - p.astype(F32)
    parts.append(x.astype(BF16))
    return parts


def _exact_dot(w, parts):
    acc = jnp.dot(w, parts[0], preferred_element_type=F32)
    for p in parts[1:]:
        acc = acc + jnp.dot(w, p, preferred_element_type=F32)
    return acc


def _head_sums(v, ones_blk, n=2):
    parts = _split(v, n)
    acc = jnp.dot(parts[0], ones_blk, preferred_element_type=F32)
    for p in parts[1:]:
        acc = acc + jnp.dot(p, ones_blk, preferred_element_type=F32)
    return acc


def _hgrn_consts():
    r, c = _iota((LANES, LANES), 0), _iota((LANES, LANES), 1)
    ones_blk = ((r // CHUNK) == (c // CHUNK)).astype(BF16)
    eye2 = (_iota((2 * LANES, LANES), 0) % LANES) == _iota((2 * LANES, LANES), 1)
    first = _iota((1, LANES), 1) < CHUNK
    return eye2, ones_blk, jnp.ones((LANES, LANES), BF16), first


def _stack_heads(v, first):
    return jnp.concatenate([jnp.where(first, v, 0.0), jnp.where(first, 0.0, v)], axis=0)


def _pick_heads(v2, first):
    return jnp.where(first, v2[:LANES], v2[LANES:])


def _hgrn_levels(qq, kk, lf, zall, mk_ref, rt_ref, first, d_att=None):
    att = jnp.zeros((2 * LANES, LANES), F32)
    dq = dk = db = jnp.zeros((LANES, LANES), F32)
    for li in range(N_LEVELS):
        rt = rt_ref[li]
        e = jnp.exp(zall[(2 + li) * LANES:(3 + li) * LANES] if li < N_LEVELS - 1 else lf * rt)
        mk = mk_ref[li]
        qef, kef = e * rt, e * (1.0 - rt)
        qe, ke = (qq * qef).astype(BF16), (kk * kef).astype(BF16)
        qe2 = _stack_heads(qe, first)
        att = att + _dot(qe2, ke, NT) * mk
        if d_att is not None:
            dam = (d_att * mk).astype(BF16)
            dqe = _pick_heads(jnp.dot(dam, ke, preferred_element_type=F32), first)
            dke = _dot(dam, qe2, TN)
            dq = dq + dqe * qef
            dk = dk + dke * kef
            db = db + (dqe * qe.astype(F32) - dke * ke.astype(F32))
    return att, dq, dk, db


def _hgrn_fwd(proj3, lbs_row, gn_row, name):
    bsz, t, _ = proj3.shape
    nt = t // LANES
    w_all, maskf, rightf = _hgrn_tables()

    def body(a_ref, lb_ref, gn_ref, w_ref, mk_ref, rt_ref, og_ref, or_ref, st_ref):
        lb = lb_ref[...]
        gn = gn_ref[...]
        eye2, ones_blk, ones_all, first = _hgrn_consts()

        def tile(i, carry):
            r0 = pl.multiple_of(i * LANES, LANES)
            a = a_ref[pl.ds(r0, LANES), :]
            qq, kk, lf, _, _, _ = _hgrn_gates(a, lb)
            va, ga = a[:, 256:384], a[:, 384:512]
            parts = _split(lf, 3)
            zall = _exact_dot(w_ref[...], parts)
            eb, ee = jnp.exp(zall[0:LANES]), jnp.exp(zall[LANES:2 * LANES])
            vb = va.astype(BF16)
            att, _, _, _ = _hgrn_levels(qq, kk, lf, zall, mk_ref, rt_ref, first)
            diag = _head_sums(_stack_heads(qq * kk, first), ones_all)
            a2 = (att + jnp.where(eye2, diag, 0.0)).astype(BF16)
            o_in = _pick_heads(jnp.dot(a2, vb, preferred_element_type=F32), first)
            qeb, keb = (qq * eb).astype(BF16), (kk * ee).astype(BF16)
            new_s, o_heads = [], []
            for h in range(2):
                hs = slice(CHUNK * h, CHUNK * (h + 1))
                o_h = o_in[:, hs]
                st = carry[h]
                chunks = []
                for c in range(2):
                    rc = slice(CHUNK * c, CHUNK * (c + 1))
                    st_ref[h, 2 * i + c] = st
                    chunks.append(o_h[rc] + _dot(qeb[rc, hs], st.astype(BF16), NT))
                    ebl = eb[CHUNK * (c + 1) - 1:CHUNK * (c + 1), hs]
                    st = st * ebl + _dot(vb[rc, hs], keb[rc, hs], TN)
                new_s.append(st)
                o_heads.append(jnp.concatenate(chunks, axis=0))
            o = jnp.concatenate(o_heads, axis=1)
            ms = _head_sums(o * o, ones_blk) * (1.0 / CHUNK)
            or_ref[pl.ds(r0, LANES), :] = o
            og_ref[pl.ds(r0, LANES), :] = (o * lax.rsqrt(ms + NORM_EPS) * gn * _silu(ga)).astype(BF16)
            return tuple(new_s)

        zero = jnp.zeros((CHUNK, CHUNK), F32)
        per_step = 4 if nt % 4 == 0 else 2

        def step(i, carry):
            for k in range(per_step):
                carry = tile(per_step * i + k, carry)
            return carry

        lax.fori_loop(0, nt // per_step, step, (zero, zero))

    out = jax.ShapeDtypeStruct((bsz, t, HGRN_W), F32)
    row = pl.BlockSpec((1, 128), lambda b, p: (0, p))
    return pl.pallas_call(
        body, grid=(bsz, 2),
        in_specs=[pl.BlockSpec((None, t, 512), lambda b, p: (b, 0, p)), row, row,
                  pl.BlockSpec(w_all.shape, lambda b, p: (0, 0)),
                  pl.BlockSpec(maskf.shape, lambda b, p: (0, 0, 0)),
                  pl.BlockSpec(rightf.shape, lambda b, p: (0, 0, 0))],
        out_specs=[pl.BlockSpec((None, t, 128), lambda b, p: (b, 0, p)),
                   pl.BlockSpec((None, t, 128), lambda b, p: (b, 0, p)),
                   pl.BlockSpec((None, 2, t // CHUNK, CHUNK, CHUNK), lambda b, p: (b, p, 0, 0, 0))],
        out_shape=[jax.ShapeDtypeStruct((bsz, t, HGRN_W), BF16), out,
                   jax.ShapeDtypeStruct((bsz, 4, t // CHUNK, CHUNK, CHUNK), F32)],
        compiler_params=_cparams(("parallel", "parallel")), name=name)(proj3, lbs_row, gn_row, w_all, maskf, rightf)


def _hgrn_bwd(proj3, o_raw, dmixed, states, lbs_row, gn_row, name):
    bsz, t, _ = proj3.shape
    nt = t // LANES
    nchunk = t // CHUNK
    w_all, maskf, rightf = _hgrn_tables()

    def body(a_ref, or_ref, do_ref, s_sc, lb_ref, gn_ref, w_ref, mk_ref, rt_ref, da_ref, dgn_ref, dlb_ref):
        lb = lb_ref[...]
        gn = gn_ref[...]
        eye2, ones_blk, ones_all, first = _hgrn_consts()
        r_i, c_i = _iota((LANES, LANES), 0), _iota((LANES, LANES), 1)
        suffix = ((c_i >= r_i) & ((r_i // CHUNK) == (c_i // CHUNK))).astype(BF16)
        row64 = _iota((LANES, CHUNK), 0)
        zero = jnp.zeros((CHUNK, CHUNK), F32)

        def bwd_tile(k, carry):
            dst0, dst1, dgn_acc, dlb_acc = carry
            i = nt - 1 - k
            r0 = pl.multiple_of(i * LANES, LANES)
            a = a_ref[pl.ds(r0, LANES), :]
            qa, ga = a[:, 0:128], a[:, 384:512]
            qq, kk, lf, sg, sgn, fg = _hgrn_gates(a, lb)
            parts = _split(lf, 3)
            zall = _exact_dot(w_ref[...], parts)
            eb, ee = jnp.exp(zall[0:LANES]), jnp.exp(zall[LANES:2 * LANES])
            vb = a[:, 256:384].astype(BF16)
            oraw = or_ref[pl.ds(r0, LANES), :]
            dout = do_ref[pl.ds(r0, LANES), :]
            r = lax.rsqrt(_head_sums(oraw * oraw, ones_blk) * (1.0 / CHUNK) + NORM_EPS)
            xn = oraw * r
            dga = dout * (xn * gn) * _dsilu(ga)
            don = dout * _silu(ga)
            dgn_acc = dgn_acc + jnp.sum(don * xn, axis=0, keepdims=True)
            dxn = don * gn
            do = r * (dxn - xn * (_head_sums(dxn * xn, ones_blk) * (1.0 / CHUNK)))
            dob = do.astype(BF16)
            do2 = _stack_heads(dob, first)
            d_att = _dot(do2, vb, NT)
            att, dq, dk, db_lv = _hgrn_levels(qq, kk, lf, zall, mk_ref, rt_ref, first, d_att)
            a2 = att + jnp.where(eye2, _head_sums(_stack_heads(qq * kk, first), ones_all), 0.0)
            dv_in = _dot(a2.astype(BF16), do2, TN)
            ddiag = _pick_heads(_head_sums(jnp.where(eye2, d_att, 0.0), ones_all), first)
            dq_in, dk_in = dq + ddiag * kk, dk + ddiag * qq
            qe_f, ke_f = qq * eb, kk * ee
            qeb, keb = qe_f.astype(BF16), ke_f.astype(BF16)
            new_ds, dq_h, dk_h, dv_h, dbl_h = [], [], [], [], []
            for h in range(2):
                hs = slice(CHUNK * h, CHUNK * (h + 1))
                dv, dq_i, dk_i = dv_in[:, hs], dq_in[:, hs], dk_in[:, hs]
                dst = (dst0, dst1)[h]
                dq_c, dk_c, dv_c, dbl_c = [None, None], [None, None], [None, None], [None, None]
                for c in (1, 0):
                    rc = slice(CHUNK * c, CHUNK * (c + 1))
                    st_n = s_sc[h, 2 * i + c]
                    ebl = eb[CHUNK * (c + 1) - 1:CHUNK * (c + 1), hs]
                    dstb = dst.astype(BF16)
                    dv_c[c] = _dot(keb[rc, hs], dstb, NT)
                    dke = jnp.dot(vb[rc, hs], dstb, preferred_element_type=F32)
                    dqe = jnp.dot(dob[rc, hs], st_n.astype(BF16), preferred_element_type=F32)
                    dbl_c[c] = (jnp.sum(dst * st_n, axis=0, keepdims=True) * ebl
                                + jnp.sum(dke * ke_f[rc, hs], axis=0, keepdims=True))
                    dq_c[c], dk_c[c] = dqe * eb[rc, hs], dke * ee[rc, hs]
                    dst = dst * ebl + _dot(dob[rc, hs], qeb[rc, hs], TN)
                new_ds.append(dst)
                dq_x, dk_x = jnp.concatenate(dq_c, axis=0), jnp.concatenate(dk_c, axis=0)
                dq_h.append(dq_i + dq_x)
                dk_h.append(dk_i + dk_x)
                dv_h.append(dv + jnp.concatenate(dv_c, axis=0))
                dbl_h.append(qq[:, hs] * dq_x - kk[:, hs] * dk_x
                             + jnp.where(row64 == CHUNK - 1, dbl_c[0], 0.0) + jnp.where(row64 == LANES - 1, dbl_c[1], 0.0))
            dqq = jnp.concatenate(dq_h, axis=1)
            dkk = jnp.concatenate(dk_h, axis=1)
            dvv = jnp.concatenate(dv_h, axis=1)
            db = db_lv + jnp.concatenate(dbl_h, axis=1)
            dlf = _exact_dot(suffix, _split(db, 3))
            dqa = dqq * _dsilu(qa)
            dfg = jnp.where(fg > TINY, dlf / fg, 0.0)
            dz = (dfg - dkk) * (1.0 - lb) * sg * sgn
            dlb_acc = dlb_acc + jnp.sum(dfg * (1.0 - sg) - dkk * sgn, axis=0, keepdims=True)
            da_ref[pl.ds(r0, LANES), :] = jnp.concatenate([dqa, dz, dvv, dga], axis=1).astype(BF16)
            return new_ds[0], new_ds[1], dgn_acc, dlb_acc

        zrow = jnp.zeros((1, LANES), F32)
        per_step = 4 if nt % 4 == 0 else 2

        def step(k, carry):
            for r in range(per_step):
                carry = bwd_tile(per_step * k + r, carry)
            return carry

        _, _, dgn_acc, dlb_acc = lax.fori_loop(0, nt // per_step, step, (zero, zero, zrow, zrow))
        dgn_ref[...] = jnp.broadcast_to(dgn_acc, (8, LANES))
        dlb_ref[...] = jnp.broadcast_to(dlb_acc, (8, LANES))

    rows = jax.ShapeDtypeStruct((bsz, 8, HGRN_W), F32)
    row = pl.BlockSpec((1, 128), lambda b, p: (0, p))
    blk = pl.BlockSpec((None, t, 128), lambda b, p: (b, 0, p))
    return pl.pallas_call(
        body, grid=(bsz, 2),
        in_specs=[pl.BlockSpec((None, t, 512), lambda b, p: (b, 0, p)), blk, blk,
                  pl.BlockSpec((None, 2, nchunk, CHUNK, CHUNK), lambda b, p: (b, p, 0, 0, 0)), row, row,
                  pl.BlockSpec(w_all.shape, lambda b, p: (0, 0)),
                  pl.BlockSpec(maskf.shape, lambda b, p: (0, 0, 0)),
                  pl.BlockSpec(rightf.shape, lambda b, p: (0, 0, 0))],
        out_specs=[pl.BlockSpec((None, t, 512), lambda b, p: (b, 0, p)),
                   pl.BlockSpec((None, 8, 128), lambda b, p: (b, 0, p)),
                   pl.BlockSpec((None, 8, 128), lambda b, p: (b, 0, p))],
        out_shape=[jax.ShapeDtypeStruct((bsz, t, A_W), BF16), rows, rows],
        compiler_params=_cparams(("parallel", "parallel")), name=name)(
            proj3, o_raw, dmixed, states, lbs_row, gn_row, w_all, maskf, rightf)


def _pool_tt(t):
    return min(256, t)


def _window_select(s2, s4, s8, s16, lane):
    return jnp.where(lane < 64, s2, jnp.where(lane < 128, s4, jnp.where(lane < 192, s8, s16)))


def _pool_counts(t0, tt):
    lane = _iota((tt, POOL_W), 1)
    tpos = (_iota((tt, POOL_W), 0) + t0 + 1).astype(F32)
    win = jnp.where(lane < 64, 2.0, jnp.where(lane < 128, 4.0, jnp.where(lane < 192, 8.0, 16.0)))
    return 1.0 / jnp.minimum(tpos, win), lane


def _pooled_tile(upad_ref, i, tt):
    r0 = pl.multiple_of(i * tt, 8)
    cat = upad_ref[pl.ds(r0, tt + POOL_HALO), :]
    s2 = cat + pltpu.roll(cat, 1, 0)
    s4 = s2 + pltpu.roll(s2, 2, 0)
    s8 = s4 + pltpu.roll(s4, 4, 0)
    s16 = s8 + pltpu.roll(s8, 8, 0)
    inv, lane = _pool_counts(i * tt, tt)
    sel = _window_select(s2[POOL_HALO:], s4[POOL_HALO:], s8[POOL_HALO:], s16[POOL_HALO:], lane)
    return sel * inv - cat[POOL_HALO:], inv, lane


def _pool_fwd(proj3, wbd, scale_row, name):
    bsz, t, _ = proj3.shape
    tt = _pool_tt(t)

    def body(p_ref, w_ref, sc_ref, o_ref, upad):
        upad[0:POOL_HALO, :] = jnp.zeros((POOL_HALO, POOL_W), F32)
        upad[POOL_HALO:, :] = p_ref[:, 0:POOL_W]
        w = w_ref[...]
        sc = sc_ref[...]

        def tile(i, c):
            pooled, _, _ = _pooled_tile(upad, i, tt)
            r0 = pl.multiple_of(i * tt, 8)
            g = p_ref[pl.ds(r0, tt), POOL_W:2 * POOL_W]
            pre = jnp.dot(pooled.astype(BF16), w, preferred_element_type=F32)
            o_ref[pl.ds(r0, tt), :] = (pre * sc * _silu(g)).astype(BF16)
            return c

        lax.fori_loop(0, t // tt, tile, 0)

    return pl.pallas_call(
        body, grid=(bsz,),
        in_specs=[pl.BlockSpec((None, t, 512), lambda b: (b, 0, B_BLK)),
                  pl.BlockSpec((POOL_W, POOL_W), lambda b: (0, 0)),
                  pl.BlockSpec((1, POOL_W), lambda b: (0, 0))],
        out_specs=pl.BlockSpec((None, t, POOL_W), lambda b: (b, 0, 0)),
        out_shape=jax.ShapeDtypeStruct((bsz, t, POOL_W), BF16),
        scratch_shapes=[pltpu.VMEM((t + POOL_HALO, POOL_W), F32)],
        compiler_params=_cparams(("parallel",)), name=name)(proj3, wbd, scale_row)


def _pool_bwd(proj3, dmixed, wbd, scale_row, name):
    bsz, t, _ = proj3.shape
    tt = _pool_tt(t)

    def body(p_ref, do_ref, w_ref, sc_ref, db_ref, dsc_ref, dw_ref, upad, epad):
        upad[0:POOL_HALO, :] = jnp.zeros((POOL_HALO, POOL_W), F32)
        upad[POOL_HALO:, :] = p_ref[:, 0:POOL_W]
        epad[t:, :] = jnp.zeros((POOL_HALO, POOL_W), F32)
        w = w_ref[...]
        sc = sc_ref[...]

        def tile(i, carry):
            dsc_acc, dw_acc = carry
            pooled, inv, _ = _pooled_tile(upad, i, tt)
            r0 = pl.multiple_of(i * tt, 8)
            g = p_ref[pl.ds(r0, tt), POOL_W:2 * POOL_W]
            dout = do_ref[pl.ds(r0, tt), :]
            pb = pooled.astype(BF16)
            pre = jnp.dot(pb, w, preferred_element_type=F32)
            t1 = dout * _silu(g)
            dsc_acc = dsc_acc + jnp.sum(t1 * pre, axis=0, keepdims=True)
            dpre = (t1 * sc).astype(BF16)
            db_ref[pl.ds(r0, tt), POOL_W:2 * POOL_W] = (dout * pre * sc * _dsilu(g)).astype(BF16)
            dw_acc = dw_acc + _dot(pb, dpre, TN)
            dpooled = _dot(dpre, w, NT)
            epad[pl.ds(r0, tt), :] = dpooled * inv
            return dsc_acc, dw_acc

        dsc_acc, dw_acc = lax.fori_loop(0, t // tt, tile, (jnp.zeros((1, POOL_W), F32), jnp.zeros((POOL_W, POOL_W), F32)))
        dsc_ref[...] = jnp.broadcast_to(dsc_acc, (8, POOL_W))
        dw_ref[...] = dw_acc

        def tile2(i, c):
            r0 = pl.multiple_of(i * tt, 8)
            n = tt + POOL_HALO
            cat = epad[pl.ds(r0, n), :]
            s2 = cat + pltpu.roll(cat, n - 1, 0)
            s4 = s2 + pltpu.roll(s2, n - 2, 0)
            s8 = s4 + pltpu.roll(s4, n - 4, 0)
            s16 = s8 + pltpu.roll(s8, n - 8, 0)
            inv, lane = _pool_counts(i * tt, tt)
            sel = _window_select(s2[:tt], s4[:tt], s8[:tt], s16[:tt], lane)
            db_ref[pl.ds(r0, tt), 0:POOL_W] = (sel - cat[:tt] / inv).astype(BF16)
            return c

        lax.fori_loop(0, t // tt, tile2, 0)

    return pl.pallas_call(
        body, grid=(bsz,),
        in_specs=[pl.BlockSpec((None, t, 512), lambda b: (b, 0, B_BLK)),
                  pl.BlockSpec((None, t, POOL_W), lambda b: (b, 0, 1)),
                  pl.BlockSpec((POOL_W, POOL_W), lambda b: (0, 0)),
                  pl.BlockSpec((1, POOL_W), lambda b: (0, 0))],
        out_specs=[pl.BlockSpec((None, t, 512), lambda b: (b, 0, 0)),
                   pl.BlockSpec((None, 8, POOL_W), lambda b: (b, 0, 0)),
                   pl.BlockSpec((None, POOL_W, POOL_W), lambda b: (b, 0, 0))],
        out_shape=[jax.ShapeDtypeStruct((bsz, t, B_W), BF16), jax.ShapeDtypeStruct((bsz, 8, POOL_W), F32),
                   jax.ShapeDtypeStruct((bsz, POOL_W, POOL_W), F32)],
        scratch_shapes=[pltpu.VMEM((t + POOL_HALO, POOL_W), F32), pltpu.VMEM((t + POOL_HALO, POOL_W), F32)],
        compiler_params=_cparams(("parallel",)), name=name)(proj3, dmixed, wbd, scale_row)


def _head_select_rows(hp):
    r, c = _iota((8, LANES), 0), _iota((8, LANES), 1)
    return ((r < 2) & (c == 2 * hp + r)).astype(F32)


def _foxgate_fwd(proj3, bias_row, name):
    bsz, t, _ = proj3.shape
    nt = t // LANES

    def body(f_ref, b_ref, cn_ref, ct_ref):
        bias = b_ref[...]
        i, j = _iota((LANES, LANES), 0), _iota((LANES, LANES), 1)
        lower = (j <= i).astype(BF16)
        spread = (_iota((LANES, FOX_W), 0) == _iota((LANES, FOX_W), 1) // 64).astype(BF16)
        select = [_head_select_rows(hp).astype(BF16) for hp in range(4)]
        offset = jnp.zeros((1, LANES), F32)
        for k in range(nt):
            rows = slice(k * LANES, (k + 1) * LANES)
            xg = f_ref[rows, :] + bias
            lf = jnp.minimum(xg, 0.0) - jnp.log(1.0 + jnp.exp(-jnp.abs(xg)))
            c = _exact_dot(lower, _split(lf, 3)) + offset
            offset = c[LANES - 1:LANES, :]
            parts = _split(c, 3)
            cn_ref[rows, :] = _head_sums(c, spread, 3)
            for hp in range(4):
                acc = _dot(select[hp], parts[0], NT)
                for p in parts[1:]:
                    acc = acc + _dot(select[hp], p, NT)
                ct_ref[hp, :, rows] = acc

    return pl.pallas_call(
        body, grid=(bsz,),
        in_specs=[pl.BlockSpec((None, t, 128), lambda b: (b, 0, F_BLK)), pl.BlockSpec((1, 128), lambda b: (0, 0))],
        out_specs=[pl.BlockSpec((None, t, FOX_W), lambda b: (b, 0, 0)),
                   pl.BlockSpec((None, 4, 8, t), lambda b: (b, 0, 0, 0))],
        out_shape=[jax.ShapeDtypeStruct((bsz, t, FOX_W), F32), jax.ShapeDtypeStruct((bsz, 4, 8, t), F32)],
        compiler_params=_cparams(("parallel",)), name=name)(proj3, bias_row)


def _foxgate_bwd(proj3, dc_nat, bias_row, name):
    bsz, t, _ = proj3.shape
    nt = t // LANES

    def body(f_ref, dc_ref, b_ref, df_ref, dbias_ref, run_sc):
        bias = b_ref[...]
        i, j = _iota((LANES, LANES), 0), _iota((LANES, LANES), 1)
        upper = (j >= i).astype(F32)
        valid = _iota((1, LANES), 1) < FOX_HEADS
        run_sc[...] = jnp.zeros((8, LANES), F32)
        dbias_ref[...] = jnp.zeros((8, LANES), F32)

        def tile(k, c):
            r0 = pl.multiple_of((nt - 1 - k) * LANES, LANES)
            dc = dc_ref[pl.ds(r0, LANES), :] + jnp.where(i == LANES - 1, run_sc[0:1, :], 0.0)
            dlf = jnp.dot(upper, dc, precision=HI, preferred_element_type=F32)
            xg = f_ref[pl.ds(r0, LANES), :] + bias
            df = jnp.where(valid, dlf * _sig(-xg), 0.0)
            df_ref[pl.ds(r0, LANES), :] = df.astype(BF16)
            run_sc[...] = dlf[0:8, :]
            dbias_ref[...] += jnp.sum(df, axis=0, keepdims=True)
            return c

        lax.fori_loop(0, nt, tile, 0)

    blk = pl.BlockSpec((None, t, 128), lambda b: (b, 0, 0))
    return pl.pallas_call(
        body, grid=(bsz,),
        in_specs=[pl.BlockSpec((None, t, 128), lambda b: (b, 0, F_BLK)), blk, pl.BlockSpec((1, 128), lambda b: (0, 0))],
        out_specs=[blk, pl.BlockSpec((None, 8, 128), lambda b: (b, 0, 0))],
        out_shape=[jax.ShapeDtypeStruct((bsz, t, F_W), BF16), jax.ShapeDtypeStruct((bsz, 8, 128), F32)],
        scratch_shapes=[pltpu.VMEM((8, LANES), F32)],
        compiler_params=_cparams(("parallel",)), name=name)(proj3, dc_nat, bias_row)


def _fox_tile(t):
    return min(256, t)


def _fox_fwd(proj3, c_nat, c_t, name):
    bsz, t, _ = proj3.shape
    tq = tk = min(4 * _fox_tile(t), t)
    nq = t // tq

    def body(q_ref, kv_ref, cn_ref, ct_ref, og_ref, or_ref, lse_ref):
        i = pl.program_id(2)
        qblk = q_ref[...]
        first = _iota((1, 128), 1) < 64
        qv = qblk[:, 0:128] * 0.125
        qm = [jnp.where(first, qv, 0.0).astype(BF16), jnp.where(first, 0.0, qv).astype(BF16)]
        cqs = [cn_ref[:, 0:1], cn_ref[:, 64:65]]
        rows = _iota((tq, tk), 0) + i * tq

        def scores(j):
            c0 = pl.multiple_of(j * tk, tk)
            kb = kv_ref[pl.ds(c0, tk), 128:256].astype(BF16)
            return tuple(_dot(qm[h], kb, NT) + (cqs[h] - ct_ref[h:h + 1, pl.ds(c0, tk)]) for h in range(2))

        def absorb(j, state, s01, masked):
            c0 = pl.multiple_of(j * tk, tk)
            vblk = kv_ref[pl.ds(c0, tk), 256:384]
            vx = [jnp.where(first, vblk, 1.0).astype(BF16), jnp.where(first, 1.0, vblk).astype(BF16)]
            new = []
            for h in range(2):
                m, acc, s = state[2 * h], state[2 * h + 1], s01[h]
                if masked:
                    s = jnp.where(rows >= _iota((tq, tk), 1) + j * tk, s, MASK_VALUE)
                m_new = jnp.maximum(m, jnp.max(s, axis=1, keepdims=True))
                p = jnp.exp(s - m_new).astype(BF16)
                new += [m_new, jnp.exp(m - m_new) * acc + jnp.dot(p, vx[h], preferred_element_type=F32)]
            return tuple(new)

        init = (jnp.full((tq, 1), MASK_VALUE, F32), jnp.zeros((tq, 128), F32)) * 2
        n_full = (i * tq) // tk
        state = lax.fori_loop(0, n_full, lambda j, state: absorb(j, state, scores(j), False), init)
        m0, acc0, m1, acc1 = absorb(n_full, state, scores(n_full), True)
        l0, l1 = pltpu.roll(acc0, 64, 1), pltpu.roll(acc1, 64, 1)
        o = jnp.where(first, acc0 / l0, acc1 / l1)
        or_ref[...] = o
        og_ref[...] = (o * _silu(qblk[:, 384:512])).astype(BF16)
        lse_ref[...] = jnp.where(first, m0 + jnp.log(l0), m1 + jnp.log(l1))

    out = jax.ShapeDtypeStruct((bsz, t, FOX_W), F32)
    blk = pl.BlockSpec((None, tq, 128), lambda b, p, i: (b, i, p))
    return pl.pallas_call(
        body, grid=(bsz, 4, nq),
        in_specs=[pl.BlockSpec((None, tq, 512), lambda b, p, i: (b, i, C_BLK0 + p)),
                  pl.BlockSpec((None, t, 512), lambda b, p, i: (b, 0, C_BLK0 + p)),
                  blk,
                  pl.BlockSpec((None, None, 8, t), lambda b, p, i: (b, p, 0, 0))],
        out_specs=[blk, blk, blk],
        out_shape=[jax.ShapeDtypeStruct((bsz, t, FOX_W), BF16), out, out],
        compiler_params=_cparams(("parallel", "parallel", "arbitrary")), name=name)(proj3, proj3, c_nat, c_t)


def _fox_bwd(proj3, o_raw, dmixed, lse, c_nat, c_t, name):
    bsz, t, _ = proj3.shape
    tq = tk = min(2 * _fox_tile(t), t)
    nq = t // tq

    def body(a_ref, or_ref, do_ref, lse_ref, cn_ref, ct_ref, dc_out, dct_out, drow_out, dq_sc, do_sc, dl_sc):
        def prep(i, c):
            r0 = pl.multiple_of(i * tq, tq)
            g = a_ref[pl.ds(r0, tq), 384:512]
            dout = do_ref[pl.ds(r0, tq), :]
            o = or_ref[pl.ds(r0, tq), :]
            dc_out[pl.ds(r0, tq), 384:512] = (dout * o * _dsilu(g)).astype(BF16)
            do = dout * _silu(g)
            do_sc[pl.ds(r0, tq), :] = do
            prod = do * o
            d0 = jnp.sum(prod[:, 0:64], axis=1, keepdims=True)
            d1 = jnp.sum(prod[:, 64:128], axis=1, keepdims=True)
            dl_sc[pl.ds(r0, tq), :] = jnp.concatenate([jnp.broadcast_to(d0, (tq, 64)), jnp.broadcast_to(d1, (tq, 64))], axis=1)
            dq_sc[pl.ds(r0, tq), :] = jnp.zeros((tq, 128), F32)
            drow_out[pl.ds(r0, tq), :] = jnp.zeros((tq, 128), F32)
            return c

        lax.fori_loop(0, nq, prep, 0)
        dct_out[...] = jnp.zeros((8, t), F32)

        first = _iota((1, 128), 1) < 64

        def heads(v):
            return [jnp.where(first, v, 0.0).astype(BF16), jnp.where(first, 0.0, v).astype(BF16)]

        def kv_tile(j, c):
            c0 = pl.multiple_of(j * tk, tk)
            kb = a_ref[pl.ds(c0, tk), 128:256].astype(BF16)
            vb = a_ref[pl.ds(c0, tk), 256:384].astype(BF16)
            cks = [ct_ref[h:h + 1, pl.ds(c0, tk)] for h in range(2)]

            def q_step(i, carry, diagonal):
                dk, dv, dcol0, dcol1 = carry
                r0 = pl.multiple_of(i * tq, tq)
                causal = _iota((tq, tk), 0) + i * tq >= _iota((tq, tk), 1) + j * tk
                qv = a_ref[pl.ds(r0, tq), 0:128] * 0.125
                do = do_sc[pl.ds(r0, tq), :]
                qb, dob = qv.astype(BF16), do.astype(BF16)
                qm, dom = heads(qv), heads(do)
                full, dcols, rsums = [], [], []
                for h in range(2):
                    lse_h = lse_ref[pl.ds(r0, tq), 64 * h:64 * h + 1]
                    dl_h = dl_sc[pl.ds(r0, tq), 64 * h:64 * h + 1]
                    cq = cn_ref[pl.ds(r0, tq), 64 * h:64 * h + 1]
                    p = jnp.exp(_dot(qm[h], kb, NT) + (cq - cks[h]) - lse_h)
                    if diagonal:
                        p = jnp.where(causal, p, 0.0)
                    ds = p * (_dot(dom[h], vb, NT) - dl_h)
                    dsb = ds.astype(BF16)
                    full.append((_dot(p.astype(BF16), dob, TN), _dot(dsb, qb, TN),
                                 jnp.dot(dsb, kb, preferred_element_type=F32)))
                    dcols.append(jnp.sum(ds, axis=0, keepdims=True))
                    rsums.append(jnp.broadcast_to(jnp.sum(ds, axis=1, keepdims=True), (tq, 128)))
                dq_sc[pl.ds(r0, tq), :] += jnp.where(first, full[0][2], full[1][2]) * 0.125
                drow_out[pl.ds(r0, tq), :] += jnp.where(first, rsums[0], rsums[1])
                return (dk + jnp.where(first, full[0][1], full[1][1]), dv + jnp.where(first, full[0][0], full[1][0]),
                        dcol0 - dcols[0], dcol1 - dcols[1])

            carry = (jnp.zeros((tk, 128), F32), jnp.zeros((tk, 128), F32), jnp.zeros((1, tk), F32), jnp.zeros((1, tk), F32))
            carry = q_step(j, carry, True)
            dk, dv, dcol0, dcol1 = lax.fori_loop(j + 1, nq, lambda i, carry: q_step(i, carry, False), carry)
            dct_out[0:1, pl.ds(c0, tk)] = dcol0
            dct_out[1:2, pl.ds(c0, tk)] = dcol1
            dc_out[pl.ds(c0, tk), 128:256] = dk.astype(BF16)
            dc_out[pl.ds(c0, tk), 256:384] = dv.astype(BF16)
            return c

        lax.fori_loop(0, t // tk, kv_tile, 0)
        dc_out[:, 0:128] = dq_sc[...].astype(BF16)

    blk = pl.BlockSpec((None, t, 128), lambda b, p: (b, 0, p))
    return pl.pallas_call(
        body, grid=(bsz, 4),
        in_specs=[pl.BlockSpec((None, t, 512), lambda b, p: (b, 0, C_BLK0 + p)),
                  blk,
                  pl.BlockSpec((None, t, 128), lambda b, p: (b, 0, 4 + p)),
                  blk, blk,
                  pl.BlockSpec((None, None, 8, t), lambda b, p: (b, p, 0, 0))],
        out_specs=[pl.BlockSpec((None, t, 512), lambda b, p: (b, 0, p)),
                   pl.BlockSpec((None, None, 8, t), lambda b, p: (b, p, 0, 0)), blk],
        out_shape=[jax.ShapeDtypeStruct((bsz, t, C_W), BF16), jax.ShapeDtypeStruct((bsz, 4, 8, t), F32),
                   jax.ShapeDtypeStruct((bsz, t, FOX_W), F32)],
        scratch_shapes=[pltpu.VMEM((t, 128), F32), pltpu.VMEM((t, 128), F32), pltpu.VMEM((t, 128), F32)],
        compiler_params=_cparams(("parallel", "parallel")), name=name)(proj3, o_raw, dmixed, lse, c_nat, c_t)


def _mix_tm(n):
    return min(512, n)


def _outproj_fwd(x2, oa, ob, oc, wo, g_row, name):
    n, d = x2.shape
    tm = _mix_tm(n)

    def body(x_ref, oa_ref, ob_ref, oc_ref, w_ref, g_ref, y_ref, xo_ref):
        y = (jnp.dot(oa_ref[...].astype(BF16), w_ref[0:256, :], preferred_element_type=F32)
             + jnp.dot(ob_ref[...].astype(BF16), w_ref[256:512, :], preferred_element_type=F32)
             + jnp.dot(oc_ref[...].astype(BF16), w_ref[512:1024, :], preferred_element_type=F32))
        y_ref[...] = y
        xo_ref[...] = x_ref[...] + y * _rstd(y) * g_ref[...]

    row = lambda w: pl.BlockSpec((tm, w), lambda i: (i, 0))
    out = jax.ShapeDtypeStruct((n, d), F32)
    return pl.pallas_call(
        body, grid=(n // tm,),
        in_specs=[row(d), row(256), row(256), row(512), pl.BlockSpec((d, d), lambda i: (0, 0)),
                  pl.BlockSpec((1, d), lambda i: (0, 0))],
        out_specs=[row(d), row(d)], out_shape=[out, out],
        compiler_params=_cparams(("parallel",)), name=name)(x2, oa, ob, oc, wo, g_row)


def _outproj_fwd_loss(x2, oa, ob, oc, wo, g_row, target2, name):
    n, d = x2.shape
    tm = _mix_tm(n)

    def body(x_ref, oa_ref, ob_ref, oc_ref, w_ref, g_ref, t_ref, y_ref, dx_ref, l_ref):
        y = (jnp.dot(oa_ref[...].astype(BF16), w_ref[0:256, :], preferred_element_type=F32)
             + jnp.dot(ob_ref[...].astype(BF16), w_ref[256:512, :], preferred_element_type=F32)
             + jnp.dot(oc_ref[...].astype(BF16), w_ref[512:1024, :], preferred_element_type=F32))
        y_ref[...] = y
        err = (x_ref[...] + y * _rstd(y) * g_ref[...]) - t_ref[...]
        dx_ref[...] = err * (1.0 / d)

        @pl.when(pl.program_id(0) == 0)
        def _():
            l_ref[...] = jnp.zeros((8, 128), F32)

        l_ref[...] += jnp.sum(err * err)

    row = lambda w: pl.BlockSpec((tm, w), lambda i: (i, 0))
    out = jax.ShapeDtypeStruct((n, d), F32)
    return pl.pallas_call(
        body, grid=(n // tm,),
        in_specs=[row(d), row(256), row(256), row(512), pl.BlockSpec((d, d), lambda i: (0, 0)),
                  pl.BlockSpec((1, d), lambda i: (0, 0)), row(d)],
        out_specs=[row(d), row(d), pl.BlockSpec((8, 128), lambda i: (0, 0))],
        out_shape=[out, out, jax.ShapeDtypeStruct((8, 128), F32)],
        compiler_params=_cparams(("arbitrary",)), name=name)(x2, oa, ob, oc, wo, g_row, target2)


def _outproj_bwd(dxo, y, oa, ob, oc, wo, g_row, name):
    n, d = dxo.shape
    tm = _mix_tm(n)

    def body(dx_ref, y_ref, oa_ref, ob_ref, oc_ref, w_ref, g_ref, dm_ref, dw_ref, dg_ref):
        @pl.when(pl.program_id(0) == 0)
        def _():
            dw_ref[...] = jnp.zeros((d, d), F32)
            dg_ref[...] = jnp.zeros((8, d), F32)

        yv, dx = y_ref[...], dx_ref[...]
        r = _rstd(yv)
        yn = yv * r
        dg_ref[...] += jnp.sum(dx * yn, axis=0, keepdims=True)
        dyn = dx * g_ref[...]
        dy = (r * (dyn - yn * jnp.mean(dyn * yn, axis=-1, keepdims=True))).astype(BF16)
        dm_ref[...] = _dot(dy, w_ref[...], NT)
        dw_ref[0:256, :] += _dot(oa_ref[...].astype(BF16), dy, TN)
        dw_ref[256:512, :] += _dot(ob_ref[...].astype(BF16), dy, TN)
        dw_ref[512:1024, :] += _dot(oc_ref[...].astype(BF16), dy, TN)

    row = lambda w: pl.BlockSpec((tm, w), lambda i: (i, 0))
    fixed = lambda r, c: pl.BlockSpec((r, c), lambda i: (0, 0))
    return pl.pallas_call(
        body, grid=(n // tm,),
        in_specs=[row(d), row(d), row(256), row(256), row(512), fixed(d, d), fixed(1, d)],
        out_specs=[row(d), fixed(d, d), fixed(8, d)],
        out_shape=[jax.ShapeDtypeStruct((n, d), F32), jax.ShapeDtypeStruct((d, d), F32), jax.ShapeDtypeStruct((8, d), F32)],
        compiler_params=_cparams(("arbitrary",)), name=name)(dxo, y, oa, ob, oc, wo, g_row)


_PIECES = ((0, A_W), (A_W, B_W), (A_W + B_W, C_W), (A_W + B_W + C_W, F_W))


def _inproj_bwd_x(x2, dxo, g_row, w_int, pieces, name):
    n, d = x2.shape
    tm = min(512, n)

    def body(x_ref, dxo_ref, g_ref, w_ref, da_ref, db_ref, dc_ref, df_ref, dx_ref, dg_ref):
        @pl.when(pl.program_id(0) == 0)
        def _():
            dg_ref[...] = jnp.zeros((8, d), F32)

        dh = jnp.zeros((tm, d), F32)
        for ref, (o, w) in zip((da_ref, db_ref, dc_ref, df_ref), _PIECES):
            dh = dh + _dot(ref[...].astype(BF16), w_ref[:, o:o + w], NT)
        x = x_ref[...]
        r = _rstd(x)
        xn = x * r
        dg_ref[...] += jnp.sum(dh * xn, axis=0, keepdims=True)
        dxn = dh * g_ref[...]
        dx_ref[...] = dxo_ref[...] + r * (dxn - xn * jnp.mean(dxn * xn, axis=-1, keepdims=True))

    row = lambda w: pl.BlockSpec((tm, w), lambda i: (i, 0))
    fixed = lambda r, c: pl.BlockSpec((r, c), lambda i: (0, 0))
    return pl.pallas_call(
        body, grid=(n // tm,),
        in_specs=[row(d), row(d), fixed(1, d), fixed(d, E_INT)] + [row(w) for _, w in _PIECES],
        out_specs=[row(d), fixed(8, d)],
        out_shape=[jax.ShapeDtypeStruct((n, d), F32), jax.ShapeDtypeStruct((8, d), F32)],
        compiler_params=_cparams(("arbitrary",), vmem_mb=56), name=name)(x2, dxo, g_row, w_int, *pieces)


def _inproj_bwd_w(x2, g_row, pieces, name):
    n, d = x2.shape
    tm = min(512, n)

    def body(x_ref, g_ref, da_ref, db_ref, dc_ref, df_ref, dw_ref):
        @pl.when(pl.program_id(0) == 0)
        def _():
            dw_ref[...] = jnp.zeros((d, E_INT), F32)

        x = x_ref[...]
        h = (x * _rstd(x) * g_ref[...]).astype(BF16)
        for ref, (o, w) in zip((da_ref, db_ref, dc_ref, df_ref), _PIECES):
            dw_ref[:, o:o + w] += _dot(h, ref[...].astype(BF16), TN)

    row = lambda w: pl.BlockSpec((tm, w), lambda i: (i, 0))
    return pl.pallas_call(
        body, grid=(n // tm,),
        in_specs=[row(d), pl.BlockSpec((1, d), lambda i: (0, 0))] + [row(w) for _, w in _PIECES],
        out_specs=pl.BlockSpec((d, E_INT), lambda i: (0, 0)),
        out_shape=jax.ShapeDtypeStruct((d, E_INT), F32),
        compiler_params=_cparams(("arbitrary",), vmem_mb=56), name=name)(x2, g_row, *pieces)


def _block_diag(pool_w_l):
    z = jnp.zeros((64, 64), pool_w_l.dtype)
    return jnp.concatenate(
        [jnp.concatenate([pool_w_l[g] if c == g else z for c in range(4)], axis=1) for g in range(4)], axis=0)


def _pad_lanes(v, width=128):
    return jnp.pad(v, ((0, 0),) * (v.ndim - 1) + ((0, width - v.shape[-1]),))


def _local_step(x, target, lower_bounds, pre_norm_g, w_in_int, hgrn_norm_g, fox_f_bias, pool_w, pool_scale,
                w_out_bf, post_norm_g, on_weight_grads):
    bsz, t, d = x.shape
    n = bsz * t
    lbs = _lbs_fwd(lower_bounds)
    saved = []
    xc = x.reshape(n, d)
    for l in range(DEPTH):
        proj = _inproj_fwd(xc, pre_norm_g[l:l + 1], w_in_int[l], f"inproj_fwd{l}").reshape(bsz, t, E_INT)
        wbd = _block_diag(pool_w[l]).astype(BF16)
        bias_row = _pad_lanes(fox_f_bias[l:l + 1])
        oa, oa_raw, states = _hgrn_fwd(proj, lbs[l:l + 1], hgrn_norm_g[l:l + 1], f"hgrn_fwd{l}")
        ob = _pool_fwd(proj, wbd, pool_scale[l:l + 1], f"pool_fwd{l}")
        c_nat, c_t = _foxgate_fwd(proj, bias_row, f"foxgate_fwd{l}")
        oc, oc_raw, lse = _fox_fwd(proj, c_nat, c_t, f"fox_fwd{l}")
        mixed = (oa.reshape(n, -1), ob.reshape(n, -1), oc.reshape(n, -1))
        if l < DEPTH - 1:
            y, xn = _outproj_fwd(xc, *mixed, w_out_bf[l], post_norm_g[l:l + 1], f"outproj_fwd{l}")
        else:
            y, dx, sq = _outproj_fwd_loss(xc, *mixed, w_out_bf[l], post_norm_g[l:l + 1], target.reshape(n, d),
                                          f"outproj_fwd{l}")
        saved.append((xc, proj, wbd, bias_row, oa, oa_raw, states, ob, oc, oc_raw, lse, c_nat, c_t, y))
        xc = xn
    g = {k: [None] * DEPTH for k in ("pre", "hgn", "bias", "pool_w", "pool_scale", "post", "lbs")}
    handed = [None] * DEPTH
    for l in reversed(range(DEPTH)):
        xin, proj, wbd, bias_row, oa, oa_raw, states, ob, oc, oc_raw, lse, c_nat, c_t, y = saved[l]
        dmix, d_w_out, dpost = _outproj_bwd(dx, y, oa.reshape(n, -1), ob.reshape(n, -1), oc.reshape(n, -1),
                                            w_out_bf[l], post_norm_g[l:l + 1], f"outproj_bwd{l}")
        g["post"][l] = dpost[0]
        dmix3 = dmix.reshape(bsz, t, d)
        d_c, dct, drow = _fox_bwd(proj, oc_raw, dmix3, lse, c_nat, c_t, f"fox_bwd{l}")
        dc_nat = _pad_lanes(dct[:, :, 0:2, :].reshape(bsz, FOX_HEADS, t).transpose(0, 2, 1)
                            + drow.reshape(bsz, t, FOX_HEADS, 64)[..., 0])
        d_f, dbias = _foxgate_bwd(proj, dc_nat, bias_row, f"foxgate_bwd{l}")
        g["bias"][l] = jnp.sum(dbias[:, 0, :FOX_HEADS], axis=0)
        d_b, dscale, dwbd = _pool_bwd(proj, dmix3, wbd, pool_scale[l:l + 1], f"pool_bwd{l}")
        g["pool_scale"][l] = jnp.sum(dscale[:, 0], axis=0)
        dwbd = jnp.sum(dwbd, axis=0)
        g["pool_w"][l] = jnp.stack([dwbd[64 * k:64 * (k + 1), 64 * k:64 * (k + 1)] for k in range(4)])
        d_a, dgn, dlb = _hgrn_bwd(proj, oa_raw, dmix3, states, lbs[l:l + 1], hgrn_norm_g[l:l + 1], f"hgrn_bwd{l}")
        g["hgn"][l] = jnp.sum(dgn[:, 0], axis=0)
        g["lbs"][l] = jnp.sum(dlb[:, 0], axis=0)
        pieces = [p.reshape(n, -1) for p in (d_a, d_b, d_c, d_f)]
        handed[l] = on_weight_grads(l, _inproj_bwd_w(xin, pre_norm_g[l:l + 1], pieces, f"inproj_bwd_w{l}"), d_w_out)
        dx, dpre = _inproj_bwd_x(xin, dx, pre_norm_g[l:l + 1], w_in_int[l], pieces, f"inproj_bwd_x{l}")
        g["pre"][l] = dpre[0]
    grads = {k: jnp.stack(v) for k, v in g.items()}
    return sq, dx.reshape(bsz, t, d), grads, handed


def _place():
    return lax.axis_index("x"), lax.axis_index("y"), lax.axis_index("c")


def _other_chips(x, y):
    return [(1 - x, y), (x, 1 - y), (1 - x, 1 - y)]


_ANY = pl.BlockSpec(memory_space=pl.ANY)


def _gather_body(handshake, n_arrays):
    def body(*refs):
        srcs, dsts = refs[:n_arrays], refs[n_arrays:2 * n_arrays]
        ici_send, ici_recv, d2d_send, d2d_recv, local_sems = refs[2 * n_arrays:]
        x, y, c = _place()
        if handshake:
            barrier = pltpu.get_barrier_semaphore()
            for peer in [(px, py, c) for px, py in _other_chips(x, y)] + [(x, y, 1 - c)]:
                pl.semaphore_signal(barrier, inc=1, device_id=peer, device_id_type=MESH)
            pl.semaphore_wait(barrier, 4)
        me = 2 * x + y
        pairs = list(zip(srcs, dsts))
        order = [(k, j) for k in range(3) for j in range(n_arrays)]
        mine = [pltpu.make_async_copy(src, dst.at[me], local_sems.at[j]) for j, (src, dst) in enumerate(pairs)]
        for cp in mine:
            cp.start()
        chips = _other_chips(x, y)
        sends = [pltpu.make_async_remote_copy(
            src_ref=pairs[j][0].at[c], dst_ref=pairs[j][1].at[me, c], send_sem=ici_send.at[n], recv_sem=ici_recv.at[n],
            device_id=(chips[k][0], chips[k][1], c), device_id_type=MESH) for n, (k, j) in enumerate(order)]
        for cp in sends:
            cp.start()
        passed = [pltpu.make_async_remote_copy(
            src_ref=pairs[j][1].at[2 * chips[k][0] + chips[k][1], c], dst_ref=pairs[j][1].at[2 * chips[k][0] + chips[k][1], c],
            send_sem=d2d_send.at[n], recv_sem=d2d_recv.at[n], device_id=(x, y, 1 - c), device_id_type=MESH)
            for n, (k, j) in enumerate(order)]
        for n, (k, j) in enumerate(order):
            px, py = chips[k]
            src, dst = pairs[j]
            pltpu.make_async_remote_copy(
                src_ref=src.at[c], dst_ref=dst.at[2 * px + py, c], send_sem=ici_send.at[n], recv_sem=ici_recv.at[n],
                device_id=(px, py, c), device_id_type=MESH).wait_recv()
            passed[n].start()
        for n, (k, j) in enumerate(order):
            px, py = chips[k]
            src, dst = pairs[j]
            pltpu.make_async_remote_copy(
                src_ref=dst.at[2 * px + py, 1 - c], dst_ref=dst.at[2 * px + py, 1 - c], send_sem=d2d_send.at[n],
                recv_sem=d2d_recv.at[n], device_id=(x, y, 1 - c), device_id_type=MESH).wait_recv()
        for cp in sends + passed:
            cp.wait_send()
        for cp in mine:
            cp.wait()

    return body


def _gather_sems(n_arrays):
    return [pltpu.SemaphoreType.DMA((3 * n_arrays,))] * 4 + [pltpu.SemaphoreType.DMA((n_arrays,))]


def _gathered(a):
    return jax.ShapeDtypeStruct((N_CHIPS,) + a.shape, a.dtype)


def _gather_weights(arrays):
    n = len(arrays)
    return pl.pallas_call(
        _gather_body(False, n), in_specs=[_ANY] * n, out_specs=[_ANY] * n, out_shape=[_gathered(a) for a in arrays],
        scratch_shapes=_gather_sems(n), name="gather_weights")(*arrays)


def _gather_weights_beside(arrays):
    hbm = pltpu.MemorySpace.HBM
    n = len(arrays)
    srcs = [jax.new_ref(a, memory_space=hbm) for a in arrays]
    dsts = [jax.empty_ref(_gathered(a), memory_space=hbm) for a in arrays]
    body = _gather_body(True, n)

    @pl.kernel(mesh=plsc.ScalarSubcoreMesh(axis_name="sequencer", num_cores=1), name="gather_weights_beside",
               scratch_types=_gather_sems(n), compiler_params=pltpu.CompilerParams(collective_id=1))
    def launch(*sems):
        body(*srcs, *dsts, *sems)

    launch()
    return [d[...] for d in dsts]


def _swap_with_sibling(parts, name):
    k = len(parts)

    def body(*refs):
        src, dst = refs[:k], refs[k:2 * k]
        send_sems, recv_sems = refs[2 * k:]
        x, y, c = _place()
        cps = [pltpu.make_async_remote_copy(src_ref=src[j], dst_ref=dst[j], send_sem=send_sems.at[j], recv_sem=recv_sems.at[j],
                                            device_id=(x, y, 1 - c), device_id_type=MESH) for j in range(k)]
        for cp in cps:
            cp.start()
        for cp in cps:
            cp.wait()

    return pl.pallas_call(
        body, in_specs=[_ANY] * k, out_specs=[_ANY] * k,
        out_shape=[jax.ShapeDtypeStruct(p.shape, p.dtype) for p in parts],
        scratch_shapes=[pltpu.SemaphoreType.DMA((k,)), pltpu.SemaphoreType.DMA((k,))], name=name)(*parts)


N_PEERS = 7


def _grad_exchange_body():
    def body(pin_ref, pout_ref, lin_ref, lout_ref, send_sems, recv_sems):
        x, y, c = _place()
        barrier = pltpu.get_barrier_semaphore()
        for k in range(1, N_PEERS + 1):
            peer = (x ^ ((k >> 2) & 1), y ^ ((k >> 1) & 1), c ^ (k & 1))
            pl.semaphore_signal(barrier, inc=1, device_id=peer, device_id_type=MESH)
        pl.semaphore_wait(barrier, N_PEERS)
        me = 2 * x + y
        pairs = ((pin_ref, lin_ref), (pout_ref, lout_ref))
        cps = []
        for k, (px, py) in enumerate(_other_chips(x, y)):
            for r in range(2):
                for j, (src, dst) in enumerate(pairs):
                    cps.append(pltpu.make_async_remote_copy(
                        src_ref=src.at[2 * px + py, r], dst_ref=dst.at[2 * k + c], send_sem=send_sems.at[2 * (2 * k + r) + j],
                        recv_sem=recv_sems.at[2 * (2 * k + c) + j], device_id=(px, py, r), device_id_type=MESH))
        for j, (src, dst) in enumerate(pairs):
            cps.append(pltpu.make_async_remote_copy(
                src_ref=src.at[me, 1 - c], dst_ref=dst.at[N_PEERS - 1], send_sem=send_sems.at[2 * (N_PEERS - 1) + j],
                recv_sem=recv_sems.at[2 * (N_PEERS - 1) + j], device_id=(x, y, 1 - c), device_id_type=MESH))
        for cp in cps:
            cp.start()
        for s in range(N_PEERS):
            for j, (src, dst) in enumerate(pairs):
                pltpu.make_async_remote_copy(
                    src_ref=src.at[0, 0], dst_ref=dst.at[s], send_sem=send_sems.at[2 * s + j], recv_sem=recv_sems.at[2 * s + j],
                    device_id=(x, y, 1 - c), device_id_type=MESH).wait_recv()
        for cp in cps:
            cp.wait_send()

    return body


_EXCHANGE_SEMS = [pltpu.SemaphoreType.DMA((2 * N_PEERS,))] * 2


def _landing(p):
    return jax.ShapeDtypeStruct((N_PEERS,) + p.shape[2:], p.dtype)


def _grad_exchange_beside(pin, pout, name, collective_id):
    hbm = pltpu.MemorySpace.HBM
    pin_ref, pout_ref = jax.new_ref(pin, memory_space=hbm), jax.new_ref(pout, memory_space=hbm)
    lin_ref, lout_ref = jax.empty_ref(_landing(pin), memory_space=hbm), jax.empty_ref(_landing(pout), memory_space=hbm)
    body = _grad_exchange_body()

    @pl.kernel(mesh=plsc.ScalarSubcoreMesh(axis_name="sequencer", num_cores=1), name=name,
               scratch_types=_EXCHANGE_SEMS, compiler_params=pltpu.CompilerParams(collective_id=collective_id))
    def launch(send_sems, recv_sems):
        body(pin_ref, pout_ref, lin_ref, lout_ref, send_sems, recv_sems)

    launch()
    return lin_ref[...], lout_ref[...]


def _add_n(parts, name):
    r, c = parts[0].shape
    tr = 256 if r % 256 == 0 else r
    n = len(parts)

    def body(*refs):
        acc = refs[0][...].astype(F32)
        for ref in refs[1:n]:
            acc = acc + ref[...].astype(F32)
        refs[n][...] = acc

    blk = pl.BlockSpec((tr, c), lambda i: (i, 0))
    return pl.pallas_call(
        body, grid=(r // tr,), in_specs=[blk] * n, out_specs=blk, out_shape=jax.ShapeDtypeStruct((r, c), F32),
        compiler_params=_cparams(("parallel",)), name=name)(*parts)


def _all_reduce_small(packet):
    r, w = packet.shape

    def body(p_ref, o_ref, buf, send_sems, recv_sems):
        x, y, c = _place()
        me = 4 * x + 2 * y + c
        buf[me] = p_ref[...]
        peers = []
        for k in range(1, 8):
            fx, fy, fc = (k >> 2) & 1, (k >> 1) & 1, k & 1
            peers.append((x ^ fx, y ^ fy, c ^ fc))
        cps = [pltpu.make_async_remote_copy(src_ref=p_ref, dst_ref=buf.at[me], send_sem=send_sems.at[k], recv_sem=recv_sems.at[k],
                                            device_id=peer, device_id_type=MESH) for k, peer in enumerate(peers)]
        for cp in cps:
            cp.start()
        for k, (px, py, pc) in enumerate(peers):
            pltpu.make_async_remote_copy(src_ref=p_ref, dst_ref=buf.at[4 * px + 2 * py + pc], send_sem=send_sems.at[k],
                                         recv_sem=recv_sems.at[k], device_id=(px, py, pc), device_id_type=MESH).wait_recv()
        for cp in cps:
            cp.wait_send()
        acc = buf[0]
        for k in range(1, 8):
            acc = acc + buf[k]
        o_ref[...] = acc

    vm = pl.BlockSpec(memory_space=pltpu.VMEM)
    return pl.pallas_call(
        body, in_specs=[vm], out_specs=vm, out_shape=jax.ShapeDtypeStruct((r, w), F32),
        scratch_shapes=[pltpu.VMEM((8, r, w), F32), pltpu.SemaphoreType.DMA((7,)), pltpu.SemaphoreType.DMA((7,))],
        name="all_reduce_small")(packet)


def _adamw_math(w, g, m, v):
    m = ADAM_B1 * m + (1.0 - ADAM_B1) * g
    v = ADAM_B2 * v + (1.0 - ADAM_B2) * (g * g)
    m_hat = m / (1.0 - ADAM_B1 ** ADAM_STEP)
    v_hat = v / (1.0 - ADAM_B2 ** ADAM_STEP)
    return -ADAM_LR * (m_hat / (jnp.sqrt(v_hat) + ADAM_EPS) + ADAM_WD * w), m, v


def _adamw(w, g_lower, g_upper, m, v, name):
    nl, r, c = w.shape
    tr = 128
    per_half = r // (2 * tr)

    def body(w_ref, lo_ref, up_ref, m_ref, v_ref, g_ref, d_ref, mo_ref, vo_ref):
        g = jnp.where(pl.program_id(1) == 0, lo_ref[...], up_ref[...])
        g_ref[...] = g
        d_ref[...], mo_ref[...], vo_ref[...] = _adamw_math(w_ref[...], g, m_ref[...], v_ref[...])

    blk = pl.BlockSpec((None, tr, c), lambda l, h, i: (l, h * per_half + i, 0))
    half = pl.BlockSpec((None, tr, c), lambda l, h, i: (l, i, 0))
    out = jax.ShapeDtypeStruct(w.shape, F32)
    return pl.pallas_call(
        body, grid=(nl, 2, per_half), in_specs=[blk, half, half, blk, blk], out_specs=[blk] * 4, out_shape=[out] * 4,
        compiler_params=_cparams(("parallel", "parallel", "parallel")), name=name)(w, g_lower, g_upper, m, v)


def _small_update(gsum, lower_bounds, wpack, mpack, vpack):
    r, w = gsum.shape
    lb_rows = DEPTH * HGRN_W // 128

    def body(g_ref, a_ref, w_ref, m_ref, v_ref, go_ref, d_ref, mo_ref, vo_ref):
        a = a_ref[...]
        a0, a1 = a[0:1], a[1:2]
        mx = jnp.maximum(a0, a1)
        e0, e1 = jnp.exp(a0 - mx), jnp.exp(a1 - mx)
        p0, p1 = e0 / (e0 + e1), e1 / (e0 + e1)
        g = g_ref[...]
        half = lb_rows // 2
        dl0 = jnp.concatenate([g[k:k + 1] for k in range(half)], axis=1)
        dl1 = jnp.concatenate([g[half + k:half + k + 1] for k in range(half)], axis=1)
        dp0 = (dl0 + dl1) - (dl0 + dl1)
        dp1 = dl1
        inner = p0 * dp0 + p1 * dp1
        da0, da1 = p0 * (dp0 - inner), p1 * (dp1 - inner)
        rows = [da0[:, 128 * k:128 * (k + 1)] for k in range(half)] + [da1[:, 128 * k:128 * (k + 1)] for k in range(half)]
        gfull = jnp.concatenate(rows + [g[lb_rows:]], axis=0)
        go_ref[...] = gfull
        d_ref[...], mo_ref[...], vo_ref[...] = _adamw_math(w_ref[...], gfull, m_ref[...], v_ref[...])

    vm = pl.BlockSpec(memory_space=pltpu.VMEM)
    out = jax.ShapeDtypeStruct((r, w), F32)
    return pl.pallas_call(body, in_specs=[vm] * 5, out_specs=[vm] * 4, out_shape=[out] * 4, name="small_update")(
        gsum, lower_bounds, wpack, mpack, vpack)


_SMALL = ("lower_bounds", "pre_norm_g", "hgrn_norm_g", "fox_f_bias", "pool_w", "pool_scale", "post_norm_g")


def _pack(parts):
    rows = []
    for k in _SMALL:
        f = parts[k].reshape(-1)
        pad = (-f.shape[0]) % (8 * 128)
        rows.append(jnp.pad(f, (0, pad)).reshape(-1, 128))
    rows.append(jnp.zeros((8, 128), F32))
    return jnp.concatenate(rows, axis=0)


def _unpack(pack, like):
    out, r = {}, 0
    for k in _SMALL:
        size = int(np.prod(like[k].shape))
        nr = -(-size // (8 * 128)) * 8
        out[k] = pack[r:r + nr].reshape(-1)[:size].reshape(like[k].shape)
        r += nr
    return out, r


def kernel(x, lower_bounds, pre_norm_g, w_in, hgrn_norm_g, fox_f_bias, pool_w, pool_scale, w_out, post_norm_g, loss_target, m_lower_bounds, m_pre_norm_g, m_w_in, m_hgrn_norm_g, m_fox_f_bias, m_pool_w, m_pool_scale, m_w_out, m_post_norm_g, v_lower_bounds, v_pre_norm_g, v_w_in, v_hgrn_norm_g, v_fox_f_bias, v_pool_w, v_pool_scale, v_w_out, v_post_norm_g):
    cx, cy, cc = _place()
    chip = 2 * cx + cy

    halves = lambda w, l: w[l].reshape(2, w.shape[1] // 2, w.shape[2]).astype(BF16)
    needed_first = _gather_weights([halves(w_in, 0)])
    needed_first, later = lax.optimization_barrier((needed_first, [halves(w_out, 0), halves(w_in, 1), halves(w_out, 1)]))
    later = _gather_weights_beside(later)
    w_in_int = [_internal_from_shards([a[q].reshape(D_MODEL, SHARD_W) for q in range(N_CHIPS)]) for a in (needed_first[0], later[1])]
    w_out_full = [a.reshape(D_MODEL, D_MODEL) for a in (later[0], later[2])]

    def on_weight_grads(l, d_w_in, d_w_out):
        pin = _shards_from_internal(d_w_in).reshape(N_CHIPS, 2, D_MODEL // 2, SHARD_W)
        pout = d_w_out.reshape(N_CHIPS, 2, D_MODEL // (2 * N_CHIPS), D_MODEL)
        own = [lax.dynamic_index_in_dim(lax.dynamic_index_in_dim(p, chip, 0, False), cc, 0, False) for p in (pin, pout)]
        return own, _grad_exchange_beside(pin.astype(BF16), pout.astype(BF16), f"grad_exchange{l}", 2 + l)

    sq, grad_x, g, handed = _local_step(x, loss_target, lower_bounds, pre_norm_g, w_in_int, hgrn_norm_g, fox_f_bias,
                                        pool_w, pool_scale, w_out_full, post_norm_g, on_weight_grads)
    first = cc == 0

    def finish(l, own, landed):
        mine = [_add_n([o] + [t[s] for s in range(N_PEERS)], f"grad_sum{l}_{j}") for j, (o, t) in enumerate(zip(own, landed))]
        theirs = _swap_with_sibling(mine, f"grad_swap{l}")
        return [(jnp.where(first, h, o), jnp.where(first, o, h)) for h, o in zip(mine, theirs)]

    grad_x, last = lax.optimization_barrier((grad_x, handed[1]))
    done = [None, finish(1, *last)]

    small = {"lower_bounds": g["lbs"], "pre_norm_g": g["pre"], "hgrn_norm_g": g["hgn"], "fox_f_bias": g["bias"],
             "pool_w": g["pool_w"], "pool_scale": g["pool_scale"], "post_norm_g": g["post"]}
    packet = _pack(small)
    nrows = packet.shape[0]
    packet = packet.at[nrows - 1].set(sq[0])
    gsum = _all_reduce_small(packet)
    loss = gsum[nrows - 1, 0] * (0.5 / D_MODEL)

    weights = {"lower_bounds": lower_bounds, "pre_norm_g": pre_norm_g, "hgrn_norm_g": hgrn_norm_g,
               "fox_f_bias": fox_f_bias, "pool_w": pool_w, "pool_scale": pool_scale, "post_norm_g": post_norm_g}
    moments_m = {"lower_bounds": m_lower_bounds, "pre_norm_g": m_pre_norm_g, "hgrn_norm_g": m_hgrn_norm_g,
                 "fox_f_bias": m_fox_f_bias, "pool_w": m_pool_w, "pool_scale": m_pool_scale, "post_norm_g": m_post_norm_g}
    moments_v = {"lower_bounds": v_lower_bounds, "pre_norm_g": v_pre_norm_g, "hgrn_norm_g": v_hgrn_norm_g,
                 "fox_f_bias": v_fox_f_bias, "pool_w": v_pool_w, "pool_scale": v_pool_scale, "post_norm_g": v_post_norm_g}
    gp, dp, mp, vp = _small_update(gsum, lower_bounds, _pack(weights), _pack(moments_m), _pack(moments_v))
    gs, _ = _unpack(gp, weights)
    ds, _ = _unpack(dp, weights)
    ms, _ = _unpack(mp, weights)
    vs, _ = _unpack(vp, weights)

    first_layer, _ = lax.optimization_barrier((handed[0], (done[1], gp, dp, mp, vp)))
    done[0] = finish(0, *first_layer)
    halves_of = lambda j, side: jnp.stack([done[l][j][side] for l in range(DEPTH)])
    grad_w_in, d_in, m_in, v_in = _adamw(w_in, halves_of(0, 0), halves_of(0, 1), m_w_in, v_w_in, "adamw_w_in")
    grad_w_out, d_out, m_out, v_out = _adamw(w_out, halves_of(1, 0), halves_of(1, 1), m_w_out, v_w_out, "adamw_w_out")

    def ordered(s, big_in, big_out):
        return (s["lower_bounds"], s["pre_norm_g"], big_in, s["hgrn_norm_g"], s["fox_f_bias"], s["pool_w"],
                s["pool_scale"], big_out, s["post_norm_g"])

    return (loss, grad_x, *ordered(gs, grad_w_in, grad_w_out), *ordered(ds, d_in, d_out),
            *ordered(ms, m_in, m_out), *ordered(vs, v_in, v_out))
```

```python
import numpy as np
import jax
import jax.numpy as jnp
from jax import lax
from jax.experimental import pallas as pl
from jax.experimental.pallas import tpu as pltpu
from jax.experimental.pallas import tpu_sc as plsc

F32 = jnp.float32
BF16 = jnp.bfloat16
HI = lax.Precision.HIGHEST
MESH = pl.DeviceIdType.MESH

NORM_EPS = 1e-6
MASK_VALUE = -1e30
TINY = 1e-30
ADAM_LR, ADAM_B1, ADAM_B2, ADAM_EPS, ADAM_WD, ADAM_STEP = 0.001, 0.9, 0.999, 1e-08, 0.01, 10

D_MODEL = 1024
DEPTH = 2
N_CHIPS = 4
CHUNK = 64
LANES = 128
HGRN_W, POOL_W, FOX_W, FOX_HEADS = 256, 256, 512, 8
POOL_WINDOWS = (2, 4, 8, 16)
POOL_HALO = 16
IN_WIDTH = 3592
SHARD_W = IN_WIDTH // N_CHIPS
A_W, B_W, C_W, F_W = 1024, 512, 2048, 128
E_INT = A_W + B_W + C_W + F_W
B_BLK = A_W // 512
C_BLK0 = (A_W + B_W) // 512
F_BLK = (A_W + B_W + C_W) // 128


def _segments():
    segs = []
    for hp in range(2):
        for part in range(4):
            segs.append((part * 256 + hp * 128, 128))
    segs.append((1024, 256))
    segs.append((1280, 256))
    for hp in range(4):
        for part in range(4):
            segs.append((1536 + part * 512 + hp * 128, 128))
    segs.append((3584, 8))
    return segs


_SEGS = _segments()


def _internal_from_shards(shards):
    parts = []
    for s, n in _SEGS:
        while n > 0:
            q, r = divmod(s, SHARD_W)
            take = min(n, SHARD_W - r)
            parts.append(shards[q][..., r:r + take])
            s, n = s + take, n - take
    parts.append(jnp.zeros(shards[0].shape[:-1] + (E_INT - IN_WIDTH,), shards[0].dtype))
    return jnp.concatenate(parts, axis=-1)


def _shards_from_internal(w):
    offs, o = [], 0
    for s, n in _SEGS:
        offs.append((s, o, n))
        o += n
    blocks = []
    for q in range(N_CHIPS):
        lo, hi = SHARD_W * q, SHARD_W * (q + 1)
        parts = [w[..., o + max(lo, s) - s:o + min(hi, s + n) - s] for s, o, n in sorted(offs) if s < hi and s + n > lo]
        blocks.append(jnp.concatenate(parts, axis=-1))
    return jnp.stack(blocks)


def _cparams(sem=None, vmem_mb=48):
    kw = dict(vmem_limit_bytes=vmem_mb * 1024 * 1024)
    if sem is not None:
        kw["dimension_semantics"] = sem
    return pltpu.CompilerParams(**kw)


def _sig(x):
    return 1.0 / (1.0 + jnp.exp(-x))


def _silu(x):
    return x * _sig(x)


def _dsilu(x):
    s = _sig(x)
    return s * (1.0 + x * (1.0 - s))


def _rstd(x):
    return lax.rsqrt(jnp.mean(x * x, axis=-1, keepdims=True) + NORM_EPS)


def _dot(a, b, dims, **kw):
    return lax.dot_general(a, b, (dims, ((), ())), preferred_element_type=F32, **kw)


NN = ((1,), (0,))
NT = ((1,), (1,))
TN = ((0,), (0,))


def _iota(shape, dim):
    return lax.broadcasted_iota(jnp.int32, shape, dim)


def _lbs_fwd(lower_bounds):
    def body(a_ref, o_ref):
        a = a_ref[...]
        a0, a1 = a[0:1], a[1:2]
        m = jnp.maximum(a0, a1)
        e0, e1 = jnp.exp(a0 - m), jnp.exp(a1 - m)
        p0, p1 = e0 / (e0 + e1), e1 / (e0 + e1)
        o_ref[...] = jnp.concatenate([p0 - p0, (p0 + p1) - p0], axis=0)

    return pl.pallas_call(body, out_shape=jax.ShapeDtypeStruct(lower_bounds.shape, F32), name="lbs_fwd")(lower_bounds)


def _inproj_fwd(x2, g_row, w_int, name):
    n, d = x2.shape
    e = w_int.shape[1]
    tm = min(512, n)

    def body(x_ref, g_ref, w_ref, o_ref):
        x = x_ref[...]
        h = (x * _rstd(x) * g_ref[...]).astype(BF16)
        o_ref[...] = jnp.dot(h, w_ref[...], preferred_element_type=F32)

    return pl.pallas_call(
        body, grid=(n // tm,),
        in_specs=[pl.BlockSpec((tm, d), lambda i: (i, 0)), pl.BlockSpec((1, d), lambda i: (0, 0)),
                  pl.BlockSpec((d, e), lambda i: (0, 0))],
        out_specs=pl.BlockSpec((tm, e), lambda i: (i, 0)),
        out_shape=jax.ShapeDtypeStruct((n, e), F32),
        compiler_params=_cparams(("parallel",)), name=name)(x2, g_row, w_int)


def _hgrn_gates(a, lb):
    qa, z = a[:, 0:128], a[:, 128:256]
    sg, sgn = _sig(z), _sig(-z)
    fg = lb + (1.0 - lb) * sg
    lf = jnp.log(jnp.maximum(fg, TINY))
    kk = (1.0 - lb) * sgn
    return qa * _sig(qa), kk, lf, sg, sgn, fg


N_LEVELS = 6


def _hgrn_tables():
    t = np.arange(LANES)
    j = np.arange(LANES)[None, :]
    same_chunk = (t[:, None] // CHUNK) == (j // CHUNK)
    w = np.zeros((2 + N_LEVELS, LANES, LANES), np.float32)
    w[0] = same_chunk & (j <= t[:, None])
    w[1] = same_chunk & (j > t[:, None])
    maskf = np.zeros((N_LEVELS, LANES, LANES), np.float32)
    rightf = np.zeros((N_LEVELS, LANES, LANES), np.float32)
    for li in range(N_LEVELS):
        m = (CHUNK // 2) >> li
        start = t - (t % (2 * m))
        right = (t % (2 * m)) >= m
        first = np.where(right, start + m, t + 1)
        last = np.where(right, t, start + m - 1)
        w[2 + li] = (j >= first[:, None]) & (j <= last[:, None])
        maskf[li] = (t[:, None] // (2 * m)) == (j // (2 * m))
        rightf[li] = right[:, None]
    w = w[:-1]
    return jnp.asarray(w.reshape(-1, LANES), BF16), jnp.asarray(np.tile(maskf, (1, 2, 1))), jnp.asarray(rightf)


def _split(x, n):
    parts = []
    for _ in range(n - 1):
        p = x.astype(BF16)
        parts.append(p)
        x = x - p.astype(F32)
    parts.append(x.astype(BF16))
    return parts


def _exact_dot(w, parts):
    acc = jnp.dot(w, parts[0], preferred_element_type=F32)
    for p in parts[1:]:
        acc = acc + jnp.dot(w, p, preferred_element_type=F32)
    return acc


def _head_sums(v, ones_blk, n=2):
    parts = _split(v, n)
    acc = jnp.dot(parts[0], ones_blk, preferred_element_type=F32)
    for p in parts[1:]:
        acc = acc + jnp.dot(p, ones_blk, preferred_element_type=F32)
    return acc


def _hgrn_consts():
    r, c = _iota((LANES, LANES), 0), _iota((LANES, LANES), 1)
    ones_blk = ((r // CHUNK) == (c // CHUNK)).astype(BF16)
    eye2 = (_iota((2 * LANES, LANES), 0) % LANES) == _iota((2 * LANES, LANES), 1)
    first = _iota((1, LANES), 1) < CHUNK
    return eye2, ones_blk, jnp.ones((LANES, LANES), BF16), first


def _stack_heads(v, first):
    return jnp.concatenate([jnp.where(first, v, 0.0), jnp.where(first, 0.0, v)], axis=0)


def _pick_heads(v2, first):
    return jnp.where(first, v2[:LANES], v2[LANES:])


def _hgrn_levels(qq, kk, lf, zall, mk_ref, rt_ref, first, d_att=None):
    att = jnp.zeros((2 * LANES, LANES), F32)
    dq = dk = db = jnp.zeros((LANES, LANES), F32)
    for li in range(N_LEVELS):
        rt = rt_ref[li]
        e = jnp.exp(zall[(2 + li) * LANES:(3 + li) * LANES] if li < N_LEVELS - 1 else lf * rt)
        mk = mk_ref[li]
        qef, kef = e * rt, e * (1.0 - rt)
        qe, ke = (qq * qef).astype(BF16), (kk * kef).astype(BF16)
        qe2 = _stack_heads(qe, first)
        att = att + _dot(qe2, ke, NT) * mk
        if d_att is not None:
            dam = (d_att * mk).astype(BF16)
            dqe = _pick_heads(jnp.dot(dam, ke, preferred_element_type=F32), first)
            dke = _dot(dam, qe2, TN)
            dq = dq + dqe * qef
            dk = dk + dke * kef
            db = db + (dqe * qe.astype(F32) - dke * ke.astype(F32))
    return att, dq, dk, db


def _hgrn_fwd(proj3, lbs_row, gn_row, name):
    bsz, t, _ = proj3.shape
    nt = t // LANES
    w_all, maskf, rightf = _hgrn_tables()

    def body(a_ref, lb_ref, gn_ref, w_ref, mk_ref, rt_ref, og_ref, or_ref, st_ref):
        lb = lb_ref[...]
        gn = gn_ref[...]
        eye2, ones_blk, ones_all, first = _hgrn_consts()

        def tile(i, carry):
            r0 = pl.multiple_of(i * LANES, LANES)
            a = a_ref[pl.ds(r0, LANES), :]
            qq, kk, lf, _, _, _ = _hgrn_gates(a, lb)
            va, ga = a[:, 256:384], a[:, 384:512]
            parts = _split(lf, 3)
            zall = _exact_dot(w_ref[...], parts)
            eb, ee = jnp.exp(zall[0:LANES]), jnp.exp(zall[LANES:2 * LANES])
            vb = va.astype(BF16)
            att, _, _, _ = _hgrn_levels(qq, kk, lf, zall, mk_ref, rt_ref, first)
            diag = _head_sums(_stack_heads(qq * kk, first), ones_all)
            a2 = (att + jnp.where(eye2, diag, 0.0)).astype(BF16)
            o_in = _pick_heads(jnp.dot(a2, vb, preferred_element_type=F32), first)
            qeb, keb = (qq * eb).astype(BF16), (kk * ee).astype(BF16)
            new_s, o_heads = [], []
            for h in range(2):
                hs = slice(CHUNK * h, CHUNK * (h + 1))
                o_h = o_in[:, hs]
                st = carry[h]
                chunks = []
                for c in range(2):
                    rc = slice(CHUNK * c, CHUNK * (c + 1))
                    st_ref[h, 2 * i + c] = st
                    chunks.append(o_h[rc] + _dot(qeb[rc, hs], st.astype(BF16), NT))
                    ebl = eb[CHUNK * (c + 1) - 1:CHUNK * (c + 1), hs]
                    st = st * ebl + _dot(vb[rc, hs], keb[rc, hs], TN)
                new_s.append(st)
                o_heads.append(jnp.concatenate(chunks, axis=0))
            o = jnp.concatenate(o_heads, axis=1)
            ms = _head_sums(o * o, ones_blk) * (1.0 / CHUNK)
            or_ref[pl.ds(r0, LANES), :] = o
            og_ref[pl.ds(r0, LANES), :] = (o * lax.rsqrt(ms + NORM_EPS) * gn * _silu(ga)).astype(BF16)
            return tuple(new_s)

        zero = jnp.zeros((CHUNK, CHUNK), F32)
        per_step = 4 if nt % 4 == 0 else 2

        def step(i, carry):
            for k in range(per_step):
                carry = tile(per_step * i + k, carry)
            return carry

        lax.fori_loop(0, nt // per_step, step, (zero, zero))

    out = jax.ShapeDtypeStruct((bsz, t, HGRN_W), F32)
    row = pl.BlockSpec((1, 128), lambda b, p: (0, p))
    return pl.pallas_call(
        body, grid=(bsz, 2),
        in_specs=[pl.BlockSpec((None, t, 512), lambda b, p: (b, 0, p)), row, row,
                  pl.BlockSpec(w_all.shape, lambda b, p: (0, 0)),
                  pl.BlockSpec(maskf.shape, lambda b, p: (0, 0, 0)),
                  pl.BlockSpec(rightf.shape, lambda b, p: (0, 0, 0))],
        out_specs=[pl.BlockSpec((None, t, 128), lambda b, p: (b, 0, p)),
                   pl.BlockSpec((None, t, 128), lambda b, p: (b, 0, p)),
                   pl.BlockSpec((None, 2, t // CHUNK, CHUNK, CHUNK), lambda b, p: (b, p, 0, 0, 0))],
        out_shape=[jax.ShapeDtypeStruct((bsz, t, HGRN_W), BF16), out,
                   jax.ShapeDtypeStruct((bsz, 4, t // CHUNK, CHUNK, CHUNK), F32)],
        compiler_params=_cparams(("parallel", "parallel")), name=name)(proj3, lbs_row, gn_row, w_all, maskf, rightf)


def _hgrn_bwd(proj3, o_raw, dmixed, states, lbs_row, gn_row, name):
    bsz, t, _ = proj3.shape
    nt = t // LANES
    nchunk = t // CHUNK
    w_all, maskf, rightf = _hgrn_tables()

    def body(a_ref, or_ref, do_ref, s_sc, lb_ref, gn_ref, w_ref, mk_ref, rt_ref, da_ref, dgn_ref, dlb_ref):
        lb = lb_ref[...]
        gn = gn_ref[...]
        eye2, ones_blk, ones_all, first = _hgrn_consts()
        r_i, c_i = _iota((LANES, LANES), 0), _iota((LANES, LANES), 1)
        suffix = ((c_i >= r_i) & ((r_i // CHUNK) == (c_i // CHUNK))).astype(BF16)
        row64 = _iota((LANES, CHUNK), 0)
        zero = jnp.zeros((CHUNK, CHUNK), F32)

        def bwd_tile(k, carry):
            dst0, dst1, dgn_acc, dlb_acc = carry
            i = nt - 1 - k
            r0 = pl.multiple_of(i * LANES, LANES)
            a = a_ref[pl.ds(r0, LANES), :]
            qa, ga = a[:, 0:128], a[:, 384:512]
            qq, kk, lf, sg, sgn, fg = _hgrn_gates(a, lb)
            parts = _split(lf, 3)
            zall = _exact_dot(w_ref[...], parts)
            eb, ee = jnp.exp(zall[0:LANES]), jnp.exp(zall[LANES:2 * LANES])
            vb = a[:, 256:384].astype(BF16)
            oraw = or_ref[pl.ds(r0, LANES), :]
            dout = do_ref[pl.ds(r0, LANES), :]
            r = lax.rsqrt(_head_sums(oraw * oraw, ones_blk) * (1.0 / CHUNK) + NORM_EPS)
            xn = oraw * r
            dga = dout * (xn * gn) * _dsilu(ga)
            don = dout * _silu(ga)
            dgn_acc = dgn_acc + jnp.sum(don * xn, axis=0, keepdims=True)
            dxn = don * gn
            do = r * (dxn - xn * (_head_sums(dxn * xn, ones_blk) * (1.0 / CHUNK)))
            dob = do.astype(BF16)
            do2 = _stack_heads(dob, first)
            d_att = _dot(do2, vb, NT)
            att, dq, dk, db_lv = _hgrn_levels(qq, kk, lf, zall, mk_ref, rt_ref, first, d_att)
            a2 = att + jnp.where(eye2, _head_sums(_stack_heads(qq * kk, first), ones_all), 0.0)
            dv_in = _dot(a2.astype(BF16), do2, TN)
            ddiag = _pick_heads(_head_sums(jnp.where(eye2, d_att, 0.0), ones_all), first)
            dq_in, dk_in = dq + ddiag * kk, dk + ddiag * qq
            qe_f, ke_f = qq * eb, kk * ee
            qeb, keb = qe_f.astype(BF16), ke_f.astype(BF16)
            new_ds, dq_h, dk_h, dv_h, dbl_h = [], [], [], [], []
            for h in range(2):
                hs = slice(CHUNK * h, CHUNK * (h + 1))
                dv, dq_i, dk_i = dv_in[:, hs], dq_in[:, hs], dk_in[:, hs]
                dst = (dst0, dst1)[h]
                dq_c, dk_c, dv_c, dbl_c = [None, None], [None, None], [None, None], [None, None]
                for c in (1, 0):
                    rc = slice(CHUNK * c, CHUNK * (c + 1))
                    st_n = s_sc[h, 2 * i + c]
                    ebl = eb[CHUNK * (c + 1) - 1:CHUNK * (c + 1), hs]
                    dstb = dst.astype(BF16)
                    dv_c[c] = _dot(keb[rc, hs], dstb, NT)
                    dke = jnp.dot(vb[rc, hs], dstb, preferred_element_type=F32)
                    dqe = jnp.dot(dob[rc, hs], st_n.astype(BF16), preferred_element_type=F32)
                    dbl_c[c] = (jnp.sum(dst * st_n, axis=0, keepdims=True) * ebl
                                + jnp.sum(dke * ke_f[rc, hs], axis=0, keepdims=True))
                    dq_c[c], dk_c[c] = dqe * eb[rc, hs], dke * ee[rc, hs]
                    dst = dst * ebl + _dot(dob[rc, hs], qeb[rc, hs], TN)
                new_ds.append(dst)
                dq_x, dk_x = jnp.concatenate(dq_c, axis=0), jnp.concatenate(dk_c, axis=0)
                dq_h.append(dq_i + dq_x)
                dk_h.append(dk_i + dk_x)
                dv_h.append(dv + jnp.concatenate(dv_c, axis=0))
                dbl_h.append(qq[:, hs] * dq_x - kk[:, hs] * dk_x
                             + jnp.where(row64 == CHUNK - 1, dbl_c[0], 0.0) + jnp.where(row64 == LANES - 1, dbl_c[1], 0.0))
            dqq = jnp.concatenate(dq_h, axis=1)
            dkk = jnp.concatenate(dk_h, axis=1)
            dvv = jnp.concatenate(dv_h, axis=1)
            db = db_lv + jnp.concatenate(dbl_h, axis=1)
            dlf = _exact_dot(suffix, _split(db, 3))
            dqa = dqq * _dsilu(qa)
            dfg = jnp.where(fg > TINY, dlf / fg, 0.0)
            dz = (dfg - dkk) * (1.0 - lb) * sg * sgn
            dlb_acc = dlb_acc + jnp.sum(dfg * (1.0 - sg) - dkk * sgn, axis=0, keepdims=True)
            da_ref[pl.ds(r0, LANES), :] = jnp.concatenate([dqa, dz, dvv, dga], axis=1).astype(BF16)
            return new_ds[0], new_ds[1], dgn_acc, dlb_acc

        zrow = jnp.zeros((1, LANES), F32)
        per_step = 4 if nt % 4 == 0 else 2

        def step(k, carry):
            for r in range(per_step):
                carry = bwd_tile(per_step * k + r, carry)
            return carry

        _, _, dgn_acc, dlb_acc = lax.fori_loop(0, nt // per_step, step, (zero, zero, zrow, zrow))
        dgn_ref[...] = jnp.broadcast_to(dgn_acc, (8, LANES))
        dlb_ref[...] = jnp.broadcast_to(dlb_acc, (8, LANES))

    rows = jax.ShapeDtypeStruct((bsz, 8, HGRN_W), F32)
    row = pl.BlockSpec((1, 128), lambda b, p: (0, p))
    blk = pl.BlockSpec((None, t, 128), lambda b, p: (b, 0, p))
    return pl.pallas_call(
        body, grid=(bsz, 2),
        in_specs=[pl.BlockSpec((None, t, 512), lambda b, p: (b, 0, p)), blk, blk,
                  pl.BlockSpec((None, 2, nchunk, CHUNK, CHUNK), lambda b, p: (b, p, 0, 0, 0)), row, row,
                  pl.BlockSpec(w_all.shape, lambda b, p: (0, 0)),
                  pl.BlockSpec(maskf.shape, lambda b, p: (0, 0, 0)),
                  pl.BlockSpec(rightf.shape, lambda b, p: (0, 0, 0))],
        out_specs=[pl.BlockSpec((None, t, 512), lambda b, p: (b, 0, p)),
                   pl.BlockSpec((None, 8, 128), lambda b, p: (b, 0, p)),
                   pl.BlockSpec((None, 8, 128), lambda b, p: (b, 0, p))],
        out_shape=[jax.ShapeDtypeStruct((bsz, t, A_W), BF16), rows, rows],
        compiler_params=_cparams(("parallel", "parallel")), name=name)(
            proj3, o_raw, dmixed, states, lbs_row, gn_row, w_all, maskf, rightf)


def _pool_tt(t):
    return min(256, t)


def _window_select(s2, s4, s8, s16, lane):
    return jnp.where(lane < 64, s2, jnp.where(lane < 128, s4, jnp.where(lane < 192, s8, s16)))


def _pool_counts(t0, tt):
    lane = _iota((tt, POOL_W), 1)
    tpos = (_iota((tt, POOL_W), 0) + t0 + 1).astype(F32)
    win = jnp.where(lane < 64, 2.0, jnp.where(lane < 128, 4.0, jnp.where(lane < 192, 8.0, 16.0)))
    return 1.0 / jnp.minimum(tpos, win), lane


def _pooled_tile(upad_ref, i, tt):
    r0 = pl.multiple_of(i * tt, 8)
    cat = upad_ref[pl.ds(r0, tt + POOL_HALO), :]
    s2 = cat + pltpu.roll(cat, 1, 0)
    s4 = s2 + pltpu.roll(s2, 2, 0)
    s8 = s4 + pltpu.roll(s4, 4, 0)
    s16 = s8 + pltpu.roll(s8, 8, 0)
    inv, lane = _pool_counts(i * tt, tt)
    sel = _window_select(s2[POOL_HALO:], s4[POOL_HALO:], s8[POOL_HALO:], s16[POOL_HALO:], lane)
    return sel * inv - cat[POOL_HALO:], inv, lane


def _pool_fwd(proj3, wbd, scale_row, name):
    bsz, t, _ = proj3.shape
    tt = _pool_tt(t)

    def body(p_ref, w_ref, sc_ref, o_ref, upad):
        upad[0:POOL_HALO, :] = jnp.zeros((POOL_HALO, POOL_W), F32)
        upad[POOL_HALO:, :] = p_ref[:, 0:POOL_W]
        w = w_ref[...]
        sc = sc_ref[...]

        def tile(i, c):
            pooled, _, _ = _pooled_tile(upad, i, tt)
            r0 = pl.multiple_of(i * tt, 8)
            g = p_ref[pl.ds(r0, tt), POOL_W:2 * POOL_W]
            pre = jnp.dot(pooled.astype(BF16), w, preferred_element_type=F32)
            o_ref[pl.ds(r0, tt), :] = (pre * sc * _silu(g)).astype(BF16)
            return c

        lax.fori_loop(0, t // tt, tile, 0)

    return pl.pallas_call(
        body, grid=(bsz,),
        in_specs=[pl.BlockSpec((None, t, 512), lambda b: (b, 0, B_BLK)),
                  pl.BlockSpec((POOL_W, POOL_W), lambda b: (0, 0)),
                  pl.BlockSpec((1, POOL_W), lambda b: (0, 0))],
        out_specs=pl.BlockSpec((None, t, POOL_W), lambda b: (b, 0, 0)),
        out_shape=jax.ShapeDtypeStruct((bsz, t, POOL_W), BF16),
        scratch_shapes=[pltpu.VMEM((t + POOL_HALO, POOL_W), F32)],
        compiler_params=_cparams(("parallel",)), name=name)(proj3, wbd, scale_row)


def _pool_bwd(proj3, dmixed, wbd, scale_row, name):
    bsz, t, _ = proj3.shape
    tt = _pool_tt(t)

    def body(p_ref, do_ref, w_ref, sc_ref, db_ref, dsc_ref, dw_ref, upad, epad):
        upad[0:POOL_HALO, :] = jnp.zeros((POOL_HALO, POOL_W), F32)
        upad[POOL_HALO:, :] = p_ref[:, 0:POOL_W]
        epad[t:, :] = jnp.zeros((POOL_HALO, POOL_W), F32)
        w = w_ref[...]
        sc = sc_ref[...]

        def tile(i, carry):
            dsc_acc, dw_acc = carry
            pooled, inv, _ = _pooled_tile(upad, i, tt)
            r0 = pl.multiple_of(i * tt, 8)
            g = p_ref[pl.ds(r0, tt), POOL_W:2 * POOL_W]
            dout = do_ref[pl.ds(r0, tt), :]
            pb = pooled.astype(BF16)
            pre = jnp.dot(pb, w, preferred_element_type=F32)
            t1 = dout * _silu(g)
            dsc_acc = dsc_acc + jnp.sum(t1 * pre, axis=0, keepdims=True)
            dpre = (t1 * sc).astype(BF16)
            db_ref[pl.ds(r0, tt), POOL_W:2 * POOL_W] = (dout * pre * sc * _dsilu(g)).astype(BF16)
            dw_acc = dw_acc + _dot(pb, dpre, TN)
            dpooled = _dot(dpre, w, NT)
            epad[pl.ds(r0, tt), :] = dpooled * inv
            return dsc_acc, dw_acc

        dsc_acc, dw_acc = lax.fori_loop(0, t // tt, tile, (jnp.zeros((1, POOL_W), F32), jnp.zeros((POOL_W, POOL_W), F32)))
        dsc_ref[...] = jnp.broadcast_to(dsc_acc, (8, POOL_W))
        dw_ref[...] = dw_acc

        def tile2(i, c):
            r0 = pl.multiple_of(i * tt, 8)
            n = tt + POOL_HALO
            cat = epad[pl.ds(r0, n), :]
            s2 = cat + pltpu.roll(cat, n - 1, 0)
            s4 = s2 + pltpu.roll(s2, n - 2, 0)
            s8 = s4 + pltpu.roll(s4, n - 4, 0)
            s16 = s8 + pltpu.roll(s8, n - 8, 0)
            inv, lane = _pool_counts(i * tt, tt)
            sel = _window_select(s2[:tt], s4[:tt], s8[:tt], s16[:tt], lane)
            db_ref[pl.ds(r0, tt), 0:POOL_W] = (sel - cat[:tt] / inv).astype(BF16)
            return c

        lax.fori_loop(0, t // tt, tile2, 0)

    return pl.pallas_call(
        body, grid=(bsz,),
        in_specs=[pl.BlockSpec((None, t, 512), lambda b: (b, 0, B_BLK)),
                  pl.BlockSpec((None, t, POOL_W), lambda b: (b, 0, 1)),
                  pl.BlockSpec((POOL_W, POOL_W), lambda b: (0, 0)),
                  pl.BlockSpec((1, POOL_W), lambda b: (0, 0))],
        out_specs=[pl.BlockSpec((None, t, 512), lambda b: (b, 0, 0)),
                   pl.BlockSpec((None, 8, POOL_W), lambda b: (b, 0, 0)),
                   pl.BlockSpec((None, POOL_W, POOL_W), lambda b: (b, 0, 0))],
        out_shape=[jax.ShapeDtypeStruct((bsz, t, B_W), BF16), jax.ShapeDtypeStruct((bsz, 8, POOL_W), F32),
                   jax.ShapeDtypeStruct((bsz, POOL_W, POOL_W), F32)],
        scratch_shapes=[pltpu.VMEM((t + POOL_HALO, POOL_W), F32), pltpu.VMEM((t + POOL_HALO, POOL_W), F32)],
        compiler_params=_cparams(("parallel",)), name=name)(proj3, dmixed, wbd, scale_row)


def _head_select_rows(hp):
    r, c = _iota((8, LANES), 0), _iota((8, LANES), 1)
    return ((r < 2) & (c == 2 * hp + r)).astype(F32)


def _foxgate_fwd(proj3, bias_row, name):
    bsz, t, _ = proj3.shape
    nt = t // LANES

    def body(f_ref, b_ref, cn_ref, ct_ref):
        bias = b_ref[...]
        i, j = _iota((LANES, LANES), 0), _iota((LANES, LANES), 1)
        lower = (j <= i).astype(BF16)
        spread = (_iota((LANES, FOX_W), 0) == _iota((LANES, FOX_W), 1) // 64).astype(BF16)
        select = [_head_select_rows(hp).astype(BF16) for hp in range(4)]
        offset = jnp.zeros((1, LANES), F32)
        for k in range(nt):
            rows = slice(k * LANES, (k + 1) * LANES)
            xg = f_ref[rows, :] + bias
            lf = jnp.minimum(xg, 0.0) - jnp.log(1.0 + jnp.exp(-jnp.abs(xg)))
            c = _exact_dot(lower, _split(lf, 3)) + offset
            offset = c[LANES - 1:LANES, :]
            parts = _split(c, 3)
            cn_ref[rows, :] = _head_sums(c, spread, 3)
            for hp in range(4):
                acc = _dot(select[hp], parts[0], NT)
                for p in parts[1:]:
                    acc = acc + _dot(select[hp], p, NT)
                ct_ref[hp, :, rows] = acc

    return pl.pallas_call(
        body, grid=(bsz,),
        in_specs=[pl.BlockSpec((None, t, 128), lambda b: (b, 0, F_BLK)), pl.BlockSpec((1, 128), lambda b: (0, 0))],
        out_specs=[pl.BlockSpec((None, t, FOX_W), lambda b: (b, 0, 0)),
                   pl.BlockSpec((None, 4, 8, t), lambda b: (b, 0, 0, 0))],
        out_shape=[jax.ShapeDtypeStruct((bsz, t, FOX_W), F32), jax.ShapeDtypeStruct((bsz, 4, 8, t), F32)],
        compiler_params=_cparams(("parallel",)), name=name)(proj3, bias_row)


def _foxgate_bwd(proj3, dc_nat, bias_row, name):
    bsz, t, _ = proj3.shape
    nt = t // LANES

    def body(f_ref, dc_ref, b_ref, df_ref, dbias_ref, run_sc):
        bias = b_ref[...]
        i, j = _iota((LANES, LANES), 0), _iota((LANES, LANES), 1)
        upper = (j >= i).astype(F32)
        valid = _iota((1, LANES), 1) < FOX_HEADS
        run_sc[...] = jnp.zeros((8, LANES), F32)
        dbias_ref[...] = jnp.zeros((8, LANES), F32)

        def tile(k, c):
            r0 = pl.multiple_of((nt - 1 - k) * LANES, LANES)
            dc = dc_ref[pl.ds(r0, LANES), :] + jnp.where(i == LANES - 1, run_sc[0:1, :], 0.0)
            dlf = jnp.dot(upper, dc, precision=HI, preferred_element_type=F32)
            xg = f_ref[pl.ds(r0, LANES), :] + bias
            df = jnp.where(valid, dlf * _sig(-xg), 0.0)
            df_ref[pl.ds(r0, LANES), :] = df.astype(BF16)
            run_sc[...] = dlf[0:8, :]
            dbias_ref[...] += jnp.sum(df, axis=0, keepdims=True)
            return c

        lax.fori_loop(0, nt, tile, 0)

    blk = pl.BlockSpec((None, t, 128), lambda b: (b, 0, 0))
    return pl.pallas_call(
        body, grid=(bsz,),
        in_specs=[pl.BlockSpec((None, t, 128), lambda b: (b, 0, F_BLK)), blk, pl.BlockSpec((1, 128), lambda b: (0, 0))],
        out_specs=[blk, pl.BlockSpec((None, 8, 128), lambda b: (b, 0, 0))],
        out_shape=[jax.ShapeDtypeStruct((bsz, t, F_W), BF16), jax.ShapeDtypeStruct((bsz, 8, 128), F32)],
        scratch_shapes=[pltpu.VMEM((8, LANES), F32)],
        compiler_params=_cparams(("parallel",)), name=name)(proj3, dc_nat, bias_row)


def _fox_tile(t):
    return min(256, t)


def _fox_fwd(proj3, c_nat, c_t, name):
    bsz, t, _ = proj3.shape
    tq = tk = min(4 * _fox_tile(t), t)
    nq = t // tq

    def body(q_ref, kv_ref, cn_ref, ct_ref, og_ref, or_ref, lse_ref):
        i = pl.program_id(2)
        qblk = q_ref[...]
        first = _iota((1, 128), 1) < 64
        qv = qblk[:, 0:128] * 0.125
        qm = [jnp.where(first, qv, 0.0).astype(BF16), jnp.where(first, 0.0, qv).astype(BF16)]
        cqs = [cn_ref[:, 0:1], cn_ref[:, 64:65]]
        rows = _iota((tq, tk), 0) + i * tq

        def scores(j):
            c0 = pl.multiple_of(j * tk, tk)
            kb = kv_ref[pl.ds(c0, tk), 128:256].astype(BF16)
            return tuple(_dot(qm[h], kb, NT) + (cqs[h] - ct_ref[h:h + 1, pl.ds(c0, tk)]) for h in range(2))

        def absorb(j, state, s01, masked):
            c0 = pl.multiple_of(j * tk, tk)
            vblk = kv_ref[pl.ds(c0, tk), 256:384]
            vx = [jnp.where(first, vblk, 1.0).astype(BF16), jnp.where(first, 1.0, vblk).astype(BF16)]
            new = []
            for h in range(2):
                m, acc, s = state[2 * h], state[2 * h + 1], s01[h]
                if masked:
                    s = jnp.where(rows >= _iota((tq, tk), 1) + j * tk, s, MASK_VALUE)
                m_new = jnp.maximum(m, jnp.max(s, axis=1, keepdims=True))
                p = jnp.exp(s - m_new).astype(BF16)
                new += [m_new, jnp.exp(m - m_new) * acc + jnp.dot(p, vx[h], preferred_element_type=F32)]
            return tuple(new)

        init = (jnp.full((tq, 1), MASK_VALUE, F32), jnp.zeros((tq, 128), F32)) * 2
        n_full = (i * tq) // tk
        state = lax.fori_loop(0, n_full, lambda j, state: absorb(j, state, scores(j), False), init)
        m0, acc0, m1, acc1 = absorb(n_full, state, scores(n_full), True)
        l0, l1 = pltpu.roll(acc0, 64, 1), pltpu.roll(acc1, 64, 1)
        o = jnp.where(first, acc0 / l0, acc1 / l1)
        or_ref[...] = o
        og_ref[...] = (o * _silu(qblk[:, 384:512])).astype(BF16)
        lse_ref[...] = jnp.where(first, m0 + jnp.log(l0), m1 + jnp.log(l1))

    out = jax.ShapeDtypeStruct((bsz, t, FOX_W), F32)
    blk = pl.BlockSpec((None, tq, 128), lambda b, p, i: (b, i, p))
    return pl.pallas_call(
        body, grid=(bsz, 4, nq),
        in_specs=[pl.BlockSpec((None, tq, 512), lambda b, p, i: (b, i, C_BLK0 + p)),
                  pl.BlockSpec((None, t, 512), lambda b, p, i: (b, 0, C_BLK0 + p)),
                  blk,
                  pl.BlockSpec((None, None, 8, t), lambda b, p, i: (b, p, 0, 0))],
        out_specs=[blk, blk, blk],
        out_shape=[jax.ShapeDtypeStruct((bsz, t, FOX_W), BF16), out, out],
        compiler_params=_cparams(("parallel", "parallel", "arbitrary")), name=name)(proj3, proj3, c_nat, c_t)


def _fox_bwd(proj3, o_raw, dmixed, lse, c_nat, c_t, name):
    bsz, t, _ = proj3.shape
    tq = tk = min(2 * _fox_tile(t), t)
    nq = t // tq

    def body(a_ref, or_ref, do_ref, lse_ref, cn_ref, ct_ref, dc_out, dct_out, dq_sc, do_sc, dl_sc):
        def prep(i, c):
            r0 = pl.multiple_of(i * tq, tq)
            g = a_ref[pl.ds(r0, tq), 384:512]
            dout = do_ref[pl.ds(r0, tq), :]
            o = or_ref[pl.ds(r0, tq), :]
            dc_out[pl.ds(r0, tq), 384:512] = (dout * o * _dsilu(g)).astype(BF16)
            do = dout * _silu(g)
            do_sc[pl.ds(r0, tq), :] = do
            prod = do * o
            d0 = jnp.sum(prod[:, 0:64], axis=1, keepdims=True)
            d1 = jnp.sum(prod[:, 64:128], axis=1, keepdims=True)
            dl_sc[pl.ds(r0, tq), :] = jnp.concatenate([jnp.broadcast_to(d0, (tq, 64)), jnp.broadcast_to(d1, (tq, 64))], axis=1)
            dq_sc[pl.ds(r0, tq), :] = jnp.zeros((tq, 128), F32)
            return c

        lax.fori_loop(0, nq, prep, 0)
        dct_out[...] = jnp.zeros((8, t), F32)

        first = _iota((1, 128), 1) < 64
        ones8 = jnp.ones((8, tk), BF16)

        def heads(v):
            return [jnp.where(first, v, 0.0).astype(BF16), jnp.where(first, 0.0, v).astype(BF16)]

        def kv_tile(j, c):
            c0 = pl.multiple_of(j * tk, tk)
            kb = a_ref[pl.ds(c0, tk), 128:256].astype(BF16)
            vb = a_ref[pl.ds(c0, tk), 256:384].astype(BF16)
            cks = [ct_ref[h:h + 1, pl.ds(c0, tk)] for h in range(2)]

            def q_step(i, carry, diagonal):
                dk, dv, dcol0, dcol1 = carry
                r0 = pl.multiple_of(i * tq, tq)
                causal = _iota((tq, tk), 0) + i * tq >= _iota((tq, tk), 1) + j * tk
                qv = a_ref[pl.ds(r0, tq), 0:128] * 0.125
                do = do_sc[pl.ds(r0, tq), :]
                qb, dob = qv.astype(BF16), do.astype(BF16)
                qm, dom = heads(qv), heads(do)
                full, dcols = [], []
                for h in range(2):
                    lse_h = lse_ref[pl.ds(r0, tq), 64 * h:64 * h + 1]
                    dl_h = dl_sc[pl.ds(r0, tq), 64 * h:64 * h + 1]
                    cq = cn_ref[pl.ds(r0, tq), 64 * h:64 * h + 1]
                    p = jnp.exp(_dot(qm[h], kb, NT) + (cq - cks[h]) - lse_h)
                    if diagonal:
                        p = jnp.where(causal, p, 0.0)
                    ds = p * (_dot(dom[h], vb, NT) - dl_h)
                    dsb = ds.astype(BF16)
                    full.append((_dot(p.astype(BF16), dob, TN), _dot(dsb, qb, TN),
                                 jnp.dot(dsb, kb, preferred_element_type=F32)))
                    dcols.append(jnp.sum(ds, axis=0, keepdims=True))
                    hi, lo = _split(ds, 2)
                    dct_out[h:h + 1, pl.ds(r0, tq)] += (_dot(ones8, hi, NT) + _dot(ones8, lo, NT))[0:1]
                dq_sc[pl.ds(r0, tq), :] += jnp.where(first, full[0][2], full[1][2]) * 0.125
                return (dk + jnp.where(first, full[0][1], full[1][1]), dv + jnp.where(first, full[0][0], full[1][0]),
                        dcol0 - dcols[0], dcol1 - dcols[1])

            carry = (jnp.zeros((tk, 128), F32), jnp.zeros((tk, 128), F32), jnp.zeros((1, tk), F32), jnp.zeros((1, tk), F32))
            carry = q_step(j, carry, True)
            dk, dv, dcol0, dcol1 = lax.fori_loop(j + 1, nq, lambda i, carry: q_step(i, carry, False), carry)
            dct_out[0:1, pl.ds(c0, tk)] += dcol0
            dct_out[1:2, pl.ds(c0, tk)] += dcol1
            dc_out[pl.ds(c0, tk), 128:256] = dk.astype(BF16)
            dc_out[pl.ds(c0, tk), 256:384] = dv.astype(BF16)
            return c

        lax.fori_loop(0, t // tk, kv_tile, 0)
        dc_out[:, 0:128] = dq_sc[...].astype(BF16)

    blk = pl.BlockSpec((None, t, 128), lambda b, p: (b, 0, p))
    return pl.pallas_call(
        body, grid=(bsz, 4),
        in_specs=[pl.BlockSpec((None, t, 512), lambda b, p: (b, 0, C_BLK0 + p)),
                  blk,
                  pl.BlockSpec((None, t, 128), lambda b, p: (b, 0, 4 + p)),
                  blk, blk,
                  pl.BlockSpec((None, None, 8, t), lambda b, p: (b, p, 0, 0))],
        out_specs=[pl.BlockSpec((None, t, 512), lambda b, p: (b, 0, p)),
                   pl.BlockSpec((None, None, 8, t), lambda b, p: (b, p, 0, 0))],
        out_shape=[jax.ShapeDtypeStruct((bsz, t, C_W), BF16), jax.ShapeDtypeStruct((bsz, 4, 8, t), F32)],
        scratch_shapes=[pltpu.VMEM((t, 128), F32), pltpu.VMEM((t, 128), F32), pltpu.VMEM((t, 128), F32)],
        compiler_params=_cparams(("parallel", "parallel")), name=name)(proj3, o_raw, dmixed, lse, c_nat, c_t)


def _mix_tm(n):
    return min(512, n)


def _outproj_fwd(x2, oa, ob, oc, wo, g_row, name):
    n, d = x2.shape
    tm = _mix_tm(n)

    def body(x_ref, oa_ref, ob_ref, oc_ref, w_ref, g_ref, y_ref, xo_ref):
        y = (jnp.dot(oa_ref[...].astype(BF16), w_ref[0:256, :], preferred_element_type=F32)
             + jnp.dot(ob_ref[...].astype(BF16), w_ref[256:512, :], preferred_element_type=F32)
             + jnp.dot(oc_ref[...].astype(BF16), w_ref[512:1024, :], preferred_element_type=F32))
        y_ref[...] = y
        xo_ref[...] = x_ref[...] + y * _rstd(y) * g_ref[...]

    row = lambda w: pl.BlockSpec((tm, w), lambda i: (i, 0))
    out = jax.ShapeDtypeStruct((n, d), F32)
    return pl.pallas_call(
        body, grid=(n // tm,),
        in_specs=[row(d), row(256), row(256), row(512), pl.BlockSpec((d, d), lambda i: (0, 0)),
                  pl.BlockSpec((1, d), lambda i: (0, 0))],
        out_specs=[row(d), row(d)], out_shape=[out, out],
        compiler_params=_cparams(("parallel",)), name=name)(x2, oa, ob, oc, wo, g_row)


def _outproj_fwd_loss(x2, oa, ob, oc, wo, g_row, target2, name):
    n, d = x2.shape
    tm = _mix_tm(n)

    def body(x_ref, oa_ref, ob_ref, oc_ref, w_ref, g_ref, t_ref, y_ref, dx_ref, l_ref):
        y = (jnp.dot(oa_ref[...].astype(BF16), w_ref[0:256, :], preferred_element_type=F32)
             + jnp.dot(ob_ref[...].astype(BF16), w_ref[256:512, :], preferred_element_type=F32)
             + jnp.dot(oc_ref[...].astype(BF16), w_ref[512:1024, :], preferred_element_type=F32))
        y_ref[...] = y
        err = (x_ref[...] + y * _rstd(y) * g_ref[...]) - t_ref[...]
        dx_ref[...] = err * (1.0 / d)

        @pl.when(pl.program_id(0) == 0)
        def _():
            l_ref[...] = jnp.zeros((8, 128), F32)

        l_ref[...] += jnp.sum(err * err)

    row = lambda w: pl.BlockSpec((tm, w), lambda i: (i, 0))
    out = jax.ShapeDtypeStruct((n, d), F32)
    return pl.pallas_call(
        body, grid=(n // tm,),
        in_specs=[row(d), row(256), row(256), row(512), pl.BlockSpec((d, d), lambda i: (0, 0)),
                  pl.BlockSpec((1, d), lambda i: (0, 0)), row(d)],
        out_specs=[row(d), row(d), pl.BlockSpec((8, 128), lambda i: (0, 0))],
        out_shape=[out, out, jax.ShapeDtypeStruct((8, 128), F32)],
        compiler_params=_cparams(("arbitrary",)), name=name)(x2, oa, ob, oc, wo, g_row, target2)


def _outproj_bwd(dxo, y, oa, ob, oc, wo, g_row, name):
    n, d = dxo.shape
    tm = _mix_tm(n)

    def body(dx_ref, y_ref, oa_ref, ob_ref, oc_ref, w_ref, g_ref, dm_ref, dw_ref, dg_ref):
        @pl.when(pl.program_id(0) == 0)
        def _():
            dw_ref[...] = jnp.zeros((d, d), F32)
            dg_ref[...] = jnp.zeros((8, d), F32)

        yv, dx = y_ref[...], dx_ref[...]
        r = _rstd(yv)
        yn = yv * r
        dg_ref[...] += jnp.sum(dx * yn, axis=0, keepdims=True)
        dyn = dx * g_ref[...]
        dy = (r * (dyn - yn * jnp.mean(dyn * yn, axis=-1, keepdims=True))).astype(BF16)
        dm_ref[...] = _dot(dy, w_ref[...], NT)
        dw_ref[0:256, :] += _dot(oa_ref[...].astype(BF16), dy, TN)
        dw_ref[256:512, :] += _dot(ob_ref[...].astype(BF16), dy, TN)
        dw_ref[512:1024, :] += _dot(oc_ref[...].astype(BF16), dy, TN)

    row = lambda w: pl.BlockSpec((tm, w), lambda i: (i, 0))
    fixed = lambda r, c: pl.BlockSpec((r, c), lambda i: (0, 0))
    return pl.pallas_call(
        body, grid=(n // tm,),
        in_specs=[row(d), row(d), row(256), row(256), row(512), fixed(d, d), fixed(1, d)],
        out_specs=[row(d), fixed(d, d), fixed(8, d)],
        out_shape=[jax.ShapeDtypeStruct((n, d), F32), jax.ShapeDtypeStruct((d, d), F32), jax.ShapeDtypeStruct((8, d), F32)],
        compiler_params=_cparams(("arbitrary",)), name=name)(dxo, y, oa, ob, oc, wo, g_row)


_PIECES = ((0, A_W), (A_W, B_W), (A_W + B_W, C_W), (A_W + B_W + C_W, F_W))


def _inproj_bwd_x(x2, dxo, g_row, w_int, pieces, name):
    n, d = x2.shape
    tm = min(512, n)

    def body(x_ref, dxo_ref, g_ref, w_ref, da_ref, db_ref, dc_ref, df_ref, dx_ref, dg_ref):
        @pl.when(pl.program_id(0) == 0)
        def _():
            dg_ref[...] = jnp.zeros((8, d), F32)

        dh = jnp.zeros((tm, d), F32)
        for ref, (o, w) in zip((da_ref, db_ref, dc_ref, df_ref), _PIECES):
            dh = dh + _dot(ref[...].astype(BF16), w_ref[:, o:o + w], NT)
        x = x_ref[...]
        r = _rstd(x)
        xn = x * r
        dg_ref[...] += jnp.sum(dh * xn, axis=0, keepdims=True)
        dxn = dh * g_ref[...]
        dx_ref[...] = dxo_ref[...] + r * (dxn - xn * jnp.mean(dxn * xn, axis=-1, keepdims=True))

    row = lambda w: pl.BlockSpec((tm, w), lambda i: (i, 0))
    fixed = lambda r, c: pl.BlockSpec((r, c), lambda i: (0, 0))
    return pl.pallas_call(
        body, grid=(n // tm,),
        in_specs=[row(d), row(d), fixed(1, d), fixed(d, E_INT)] + [row(w) for _, w in _PIECES],
        out_specs=[row(d), fixed(8, d)],
        out_shape=[jax.ShapeDtypeStruct((n, d), F32), jax.ShapeDtypeStruct((8, d), F32)],
        compiler_params=_cparams(("arbitrary",), vmem_mb=56), name=name)(x2, dxo, g_row, w_int, *pieces)


def _inproj_bwd_w(x2, g_row, pieces, name):
    n, d = x2.shape
    tm = min(512, n)

    def body(x_ref, g_ref, da_ref, db_ref, dc_ref, df_ref, dw_ref):
        @pl.when(pl.program_id(0) == 0)
        def _():
            dw_ref[...] = jnp.zeros((d, E_INT), F32)

        x = x_ref[...]
        h = (x * _rstd(x) * g_ref[...]).astype(BF16)
        for ref, (o, w) in zip((da_ref, db_ref, dc_ref, df_ref), _PIECES):
            dw_ref[:, o:o + w] += _dot(h, ref[...].astype(BF16), TN)

    row = lambda w: pl.BlockSpec((tm, w), lambda i: (i, 0))
    return pl.pallas_call(
        body, grid=(n // tm,),
        in_specs=[row(d), pl.BlockSpec((1, d), lambda i: (0, 0))] + [row(w) for _, w in _PIECES],
        out_specs=pl.BlockSpec((d, E_INT), lambda i: (0, 0)),
        out_shape=jax.ShapeDtypeStruct((d, E_INT), F32),
        compiler_params=_cparams(("arbitrary",), vmem_mb=56), name=name)(x2, g_row, *pieces)


def _block_diag(pool_w_l):
    z = jnp.zeros((64, 64), pool_w_l.dtype)
    return jnp.concatenate(
        [jnp.concatenate([pool_w_l[g] if c == g else z for c in range(4)], axis=1) for g in range(4)], axis=0)


def _pad_lanes(v, width=128):
    return jnp.pad(v, ((0, 0),) * (v.ndim - 1) + ((0, width - v.shape[-1]),))


def _local_step(x, target, lower_bounds, pre_norm_g, w_in_int, hgrn_norm_g, fox_f_bias, pool_w, pool_scale,
                w_out_bf, post_norm_g, on_weight_grads):
    bsz, t, d = x.shape
    n = bsz * t
    lbs = _lbs_fwd(lower_bounds)
    saved = []
    xc = x.reshape(n, d)
    for l in range(DEPTH):
        proj = _inproj_fwd(xc, pre_norm_g[l:l + 1], w_in_int[l], f"inproj_fwd{l}").reshape(bsz, t, E_INT)
        wbd = _block_diag(pool_w[l]).astype(BF16)
        bias_row = _pad_lanes(fox_f_bias[l:l + 1])
        oa, oa_raw, states = _hgrn_fwd(proj, lbs[l:l + 1], hgrn_norm_g[l:l + 1], f"hgrn_fwd{l}")
        ob = _pool_fwd(proj, wbd, pool_scale[l:l + 1], f"pool_fwd{l}")
        c_nat, c_t = _foxgate_fwd(proj, bias_row, f"foxgate_fwd{l}")
        oc, oc_raw, lse = _fox_fwd(proj, c_nat, c_t, f"fox_fwd{l}")
        mixed = (oa.reshape(n, -1), ob.reshape(n, -1), oc.reshape(n, -1))
        if l < DEPTH - 1:
            y, xn = _outproj_fwd(xc, *mixed, w_out_bf[l], post_norm_g[l:l + 1], f"outproj_fwd{l}")
        else:
            y, dx, sq = _outproj_fwd_loss(xc, *mixed, w_out_bf[l], post_norm_g[l:l + 1], target.reshape(n, d),
                                          f"outproj_fwd{l}")
        saved.append((xc, proj, wbd, bias_row, oa, oa_raw, states, ob, oc, oc_raw, lse, c_nat, c_t, y))
        xc = xn
    g = {k: [None] * DEPTH for k in ("pre", "hgn", "bias", "pool_w", "pool_scale", "post", "lbs")}
    handed = [None] * DEPTH
    for l in reversed(range(DEPTH)):
        xin, proj, wbd, bias_row, oa, oa_raw, states, ob, oc, oc_raw, lse, c_nat, c_t, y = saved[l]
        dmix, d_w_out, dpost = _outproj_bwd(dx, y, oa.reshape(n, -1), ob.reshape(n, -1), oc.reshape(n, -1),
                                            w_out_bf[l], post_norm_g[l:l + 1], f"outproj_bwd{l}")
        g["post"][l] = dpost[0]
        dmix3 = dmix.reshape(bsz, t, d)
        d_c, dct = _fox_bwd(proj, oc_raw, dmix3, lse, c_nat, c_t, f"fox_bwd{l}")
        dc_nat = _pad_lanes(dct[:, :, 0:2, :].reshape(bsz, FOX_HEADS, t).transpose(0, 2, 1))
        d_f, dbias = _foxgate_bwd(proj, dc_nat, bias_row, f"foxgate_bwd{l}")
        g["bias"][l] = jnp.sum(dbias[:, 0, :FOX_HEADS], axis=0)
        d_b, dscale, dwbd = _pool_bwd(proj, dmix3, wbd, pool_scale[l:l + 1], f"pool_bwd{l}")
        g["pool_scale"][l] = jnp.sum(dscale[:, 0], axis=0)
        dwbd = jnp.sum(dwbd, axis=0)
        g["pool_w"][l] = jnp.stack([dwbd[64 * k:64 * (k + 1), 64 * k:64 * (k + 1)] for k in range(4)])
        d_a, dgn, dlb = _hgrn_bwd(proj, oa_raw, dmix3, states, lbs[l:l + 1], hgrn_norm_g[l:l + 1], f"hgrn_bwd{l}")
        g["hgn"][l] = jnp.sum(dgn[:, 0], axis=0)
        g["lbs"][l] = jnp.sum(dlb[:, 0], axis=0)
        pieces = [p.reshape(n, -1) for p in (d_a, d_b, d_c, d_f)]
        handed[l] = on_weight_grads(l, _inproj_bwd_w(xin, pre_norm_g[l:l + 1], pieces, f"inproj_bwd_w{l}"), d_w_out)
        dx, dpre = _inproj_bwd_x(xin, dx, pre_norm_g[l:l + 1], w_in_int[l], pieces, f"inproj_bwd_x{l}")
        g["pre"][l] = dpre[0]
    grads = {k: jnp.stack(v) for k, v in g.items()}
    return sq, dx.reshape(bsz, t, d), grads, handed


def _place():
    return lax.axis_index("x"), lax.axis_index("y"), lax.axis_index("c")


def _other_chips(x, y):
    return [(1 - x, y), (x, 1 - y), (1 - x, 1 - y)]


_ANY = pl.BlockSpec(memory_space=pl.ANY)


def _gather_body(handshake, n_arrays):
    def body(*refs):
        srcs, dsts = refs[:n_arrays], refs[n_arrays:2 * n_arrays]
        ici_send, ici_recv, d2d_send, d2d_recv, local_sems = refs[2 * n_arrays:]
        x, y, c = _place()
        if handshake:
            barrier = pltpu.get_barrier_semaphore()
            for peer in [(px, py, c) for px, py in _other_chips(x, y)] + [(x, y, 1 - c)]:
                pl.semaphore_signal(barrier, inc=1, device_id=peer, device_id_type=MESH)
            pl.semaphore_wait(barrier, 4)
        me = 2 * x + y
        pairs = list(zip(srcs, dsts))
        order = [(k, j) for k in range(3) for j in range(n_arrays)]
        mine = [pltpu.make_async_copy(src, dst.at[me], local_sems.at[j]) for j, (src, dst) in enumerate(pairs)]
        for cp in mine:
            cp.start()
        chips = _other_chips(x, y)
        sends = [pltpu.make_async_remote_copy(
            src_ref=pairs[j][0].at[c], dst_ref=pairs[j][1].at[me, c], send_sem=ici_send.at[n], recv_sem=ici_recv.at[n],
            device_id=(chips[k][0], chips[k][1], c), device_id_type=MESH) for n, (k, j) in enumerate(order)]
        for cp in sends:
            cp.start()
        passed = [pltpu.make_async_remote_copy(
            src_ref=pairs[j][1].at[2 * chips[k][0] + chips[k][1], c], dst_ref=pairs[j][1].at[2 * chips[k][0] + chips[k][1], c],
            send_sem=d2d_send.at[n], recv_sem=d2d_recv.at[n], device_id=(x, y, 1 - c), device_id_type=MESH)
            for n, (k, j) in enumerate(order)]
        for n, (k, j) in enumerate(order):
            px, py = chips[k]
            src, dst = pairs[j]
            pltpu.make_async_remote_copy(
                src_ref=src.at[c], dst_ref=dst.at[2 * px + py, c], send_sem=ici_send.at[n], recv_sem=ici_recv.at[n],
                device_id=(px, py, c), device_id_type=MESH).wait_recv()
            passed[n].start()
        for n, (k, j) in enumerate(order):
            px, py = chips[k]
            src, dst = pairs[j]
            pltpu.make_async_remote_copy(
                src_ref=dst.at[2 * px + py, 1 - c], dst_ref=dst.at[2 * px + py, 1 - c], send_sem=d2d_send.at[n],
                recv_sem=d2d_recv.at[n], device_id=(x, y, 1 - c), device_id_type=MESH).wait_recv()
        for cp in sends + passed:
            cp.wait_send()
        for cp in mine:
            cp.wait()

    return body


def _gather_sems(n_arrays):
    return [pltpu.SemaphoreType.DMA((3 * n_arrays,))] * 4 + [pltpu.SemaphoreType.DMA((n_arrays,))]


def _gathered(a):
    return jax.ShapeDtypeStruct((N_CHIPS,) + a.shape, a.dtype)


def _gather_weights(arrays):
    n = len(arrays)
    return pl.pallas_call(
        _gather_body(False, n), in_specs=[_ANY] * n, out_specs=[_ANY] * n, out_shape=[_gathered(a) for a in arrays],
        scratch_shapes=_gather_sems(n), name="gather_weights")(*arrays)


def _gather_weights_beside(arrays):
    hbm = pltpu.MemorySpace.HBM
    n = len(arrays)
    srcs = [jax.new_ref(a, memory_space=hbm) for a in arrays]
    dsts = [jax.empty_ref(_gathered(a), memory_space=hbm) for a in arrays]
    body = _gather_body(True, n)

    @pl.kernel(mesh=plsc.ScalarSubcoreMesh(axis_name="sequencer", num_cores=1), name="gather_weights_beside",
               scratch_types=_gather_sems(n), compiler_params=pltpu.CompilerParams(collective_id=1))
    def launch(*sems):
        body(*srcs, *dsts, *sems)

    launch()
    return [d[...] for d in dsts]


def _swap_with_sibling(parts, name):
    k = len(parts)

    def body(*refs):
        src, dst = refs[:k], refs[k:2 * k]
        send_sems, recv_sems = refs[2 * k:]
        x, y, c = _place()
        cps = [pltpu.make_async_remote_copy(src_ref=src[j], dst_ref=dst[j], send_sem=send_sems.at[j], recv_sem=recv_sems.at[j],
                                            device_id=(x, y, 1 - c), device_id_type=MESH) for j in range(k)]
        for cp in cps:
            cp.start()
        for cp in cps:
            cp.wait()

    return pl.pallas_call(
        body, in_specs=[_ANY] * k, out_specs=[_ANY] * k,
        out_shape=[jax.ShapeDtypeStruct(p.shape, p.dtype) for p in parts],
        scratch_shapes=[pltpu.SemaphoreType.DMA((k,)), pltpu.SemaphoreType.DMA((k,))], name=name)(*parts)


N_PEERS = 7


def _grad_exchange_body():
    def body(pin_ref, pout_ref, lin_ref, lout_ref, send_sems, recv_sems):
        x, y, c = _place()
        barrier = pltpu.get_barrier_semaphore()
        for k in range(1, N_PEERS + 1):
            peer = (x ^ ((k >> 2) & 1), y ^ ((k >> 1) & 1), c ^ (k & 1))
            pl.semaphore_signal(barrier, inc=1, device_id=peer, device_id_type=MESH)
        pl.semaphore_wait(barrier, N_PEERS)
        me = 2 * x + y
        pairs = ((pin_ref, lin_ref), (pout_ref, lout_ref))
        cps = []
        for k, (px, py) in enumerate(_other_chips(x, y)):
            for r in range(2):
                for j, (src, dst) in enumerate(pairs):
                    cps.append(pltpu.make_async_remote_copy(
                        src_ref=src.at[2 * px + py, r], dst_ref=dst.at[2 * k + c], send_sem=send_sems.at[2 * (2 * k + r) + j],
                        recv_sem=recv_sems.at[2 * (2 * k + c) + j], device_id=(px, py, r), device_id_type=MESH))
        for j, (src, dst) in enumerate(pairs):
            cps.append(pltpu.make_async_remote_copy(
                src_ref=src.at[me, 1 - c], dst_ref=dst.at[N_PEERS - 1], send_sem=send_sems.at[2 * (N_PEERS - 1) + j],
                recv_sem=recv_sems.at[2 * (N_PEERS - 1) + j], device_id=(x, y, 1 - c), device_id_type=MESH))
        for cp in cps:
            cp.start()
        for s in range(N_PEERS):
            for j, (src, dst) in enumerate(pairs):
                pltpu.make_async_remote_copy(
                    src_ref=src.at[0, 0], dst_ref=dst.at[s], send_sem=send_sems.at[2 * s + j], recv_sem=recv_sems.at[2 * s + j],
                    device_id=(x, y, 1 - c), device_id_type=MESH).wait_recv()
        for cp in cps:
            cp.wait_send()

    return body


_EXCHANGE_SEMS = [pltpu.SemaphoreType.DMA((2 * N_PEERS,))] * 2


def _landing(p):
    return jax.ShapeDtypeStruct((N_PEERS,) + p.shape[2:], p.dtype)


def _grad_exchange_beside(pin, pout, name, collective_id):
    hbm = pltpu.MemorySpace.HBM
    pin_ref, pout_ref = jax.new_ref(pin, memory_space=hbm), jax.new_ref(pout, memory_space=hbm)
    lin_ref, lout_ref = jax.empty_ref(_landing(pin), memory_space=hbm), jax.empty_ref(_landing(pout), memory_space=hbm)
    body = _grad_exchange_body()

    @pl.kernel(mesh=plsc.ScalarSubcoreMesh(axis_name="sequencer", num_cores=1), name=name,
               scratch_types=_EXCHANGE_SEMS, compiler_params=pltpu.CompilerParams(collective_id=collective_id))
    def launch(send_sems, recv_sems):
        body(pin_ref, pout_ref, lin_ref, lout_ref, send_sems, recv_sems)

    launch()
    return lin_ref[...], lout_ref[...]


def _add_n(parts, name):
    r, c = parts[0].shape
    tr = 256 if r % 256 == 0 else r
    n = len(parts)

    def body(*refs):
        acc = refs[0][...].astype(F32)
        for ref in refs[1:n]:
            acc = acc + ref[...].astype(F32)
        refs[n][...] = acc

    blk = pl.BlockSpec((tr, c), lambda i: (i, 0))
    return pl.pallas_call(
        body, grid=(r // tr,), in_specs=[blk] * n, out_specs=blk, out_shape=jax.ShapeDtypeStruct((r, c), F32),
        compiler_params=_cparams(("parallel",)), name=name)(*parts)


def _all_reduce_small(packet):
    r, w = packet.shape

    def body(p_ref, o_ref, buf, send_sems, recv_sems):
        x, y, c = _place()
        me = 4 * x + 2 * y + c
        buf[me] = p_ref[...]
        peers = []
        for k in range(1, 8):
            fx, fy, fc = (k >> 2) & 1, (k >> 1) & 1, k & 1
            peers.append((x ^ fx, y ^ fy, c ^ fc))
        cps = [pltpu.make_async_remote_copy(src_ref=p_ref, dst_ref=buf.at[me], send_sem=send_sems.at[k], recv_sem=recv_sems.at[k],
                                            device_id=peer, device_id_type=MESH) for k, peer in enumerate(peers)]
        for cp in cps:
            cp.start()
        for k, (px, py, pc) in enumerate(peers):
            pltpu.make_async_remote_copy(src_ref=p_ref, dst_ref=buf.at[4 * px + 2 * py + pc], send_sem=send_sems.at[k],
                                         recv_sem=recv_sems.at[k], device_id=(px, py, pc), device_id_type=MESH).wait_recv()
        for cp in cps:
            cp.wait_send()
        acc = buf[0]
        for k in range(1, 8):
            acc = acc + buf[k]
        o_ref[...] = acc

    vm = pl.BlockSpec(memory_space=pltpu.VMEM)
    return pl.pallas_call(
        body, in_specs=[vm], out_specs=vm, out_shape=jax.ShapeDtypeStruct((r, w), F32),
        scratch_shapes=[pltpu.VMEM((8, r, w), F32), pltpu.SemaphoreType.DMA((7,)), pltpu.SemaphoreType.DMA((7,))],
        name="all_reduce_small")(packet)


def _adamw_math(w, g, m, v):
    m = ADAM_B1 * m + (1.0 - ADAM_B1) * g
    v = ADAM_B2 * v + (1.0 - ADAM_B2) * (g * g)
    m_hat = m / (1.0 - ADAM_B1 ** ADAM_STEP)
    v_hat = v / (1.0 - ADAM_B2 ** ADAM_STEP)
    return -ADAM_LR * (m_hat / (jnp.sqrt(v_hat) + ADAM_EPS) + ADAM_WD * w), m, v


def _adamw(w, g_lower, g_upper, m, v, name):
    nl, r, c = w.shape
    tr = 128
    per_half = r // (2 * tr)

    def body(w_ref, lo_ref, up_ref, m_ref, v_ref, g_ref, d_ref, mo_ref, vo_ref):
        g = jnp.where(pl.program_id(1) == 0, lo_ref[...], up_ref[...])
        g_ref[...] = g
        d_ref[...], mo_ref[...], vo_ref[...] = _adamw_math(w_ref[...], g, m_ref[...], v_ref[...])

    blk = pl.BlockSpec((None, tr, c), lambda l, h, i: (l, h * per_half + i, 0))
    half = pl.BlockSpec((None, tr, c), lambda l, h, i: (l, i, 0))
    out = jax.ShapeDtypeStruct(w.shape, F32)
    return pl.pallas_call(
        body, grid=(nl, 2, per_half), in_specs=[blk, half, half, blk, blk], out_specs=[blk] * 4, out_shape=[out] * 4,
        compiler_params=_cparams(("parallel", "parallel", "parallel")), name=name)(w, g_lower, g_upper, m, v)


def _small_update(gsum, lower_bounds, wpack, mpack, vpack):
    r, w = gsum.shape
    lb_rows = DEPTH * HGRN_W // 128

    def body(g_ref, a_ref, w_ref, m_ref, v_ref, go_ref, d_ref, mo_ref, vo_ref):
        a = a_ref[...]
        a0, a1 = a[0:1], a[1:2]
        mx = jnp.maximum(a0, a1)
        e0, e1 = jnp.exp(a0 - mx), jnp.exp(a1 - mx)
        p0, p1 = e0 / (e0 + e1), e1 / (e0 + e1)
        g = g_ref[...]
        half = lb_rows // 2
        dl0 = jnp.concatenate([g[k:k + 1] for k in range(half)], axis=1)
        dl1 = jnp.concatenate([g[half + k:half + k + 1] for k in range(half)], axis=1)
        dp0 = (dl0 + dl1) - (dl0 + dl1)
        dp1 = dl1
        inner = p0 * dp0 + p1 * dp1
        da0, da1 = p0 * (dp0 - inner), p1 * (dp1 - inner)
        rows = [da0[:, 128 * k:128 * (k + 1)] for k in range(half)] + [da1[:, 128 * k:128 * (k + 1)] for k in range(half)]
        gfull = jnp.concatenate(rows + [g[lb_rows:]], axis=0)
        go_ref[...] = gfull
        d_ref[...], mo_ref[...], vo_ref[...] = _adamw_math(w_ref[...], gfull, m_ref[...], v_ref[...])

    vm = pl.BlockSpec(memory_space=pltpu.VMEM)
    out = jax.ShapeDtypeStruct((r, w), F32)
    return pl.pallas_call(body, in_specs=[vm] * 5, out_specs=[vm] * 4, out_shape=[out] * 4, name="small_update")(
        gsum, lower_bounds, wpack, mpack, vpack)


_SMALL = ("lower_bounds", "pre_norm_g", "hgrn_norm_g", "fox_f_bias", "pool_w", "pool_scale", "post_norm_g")


def _pack(parts):
    rows = []
    for k in _SMALL:
        f = parts[k].reshape(-1)
        pad = (-f.shape[0]) % (8 * 128)
        rows.append(jnp.pad(f, (0, pad)).reshape(-1, 128))
    rows.append(jnp.zeros((8, 128), F32))
    return jnp.concatenate(rows, axis=0)


def _unpack(pack, like):
    out, r = {}, 0
    for k in _SMALL:
        size = int(np.prod(like[k].shape))
        nr = -(-size // (8 * 128)) * 8
        out[k] = pack[r:r + nr].reshape(-1)[:size].reshape(like[k].shape)
        r += nr
    return out, r


def kernel(x, lower_bounds, pre_norm_g, w_in, hgrn_norm_g, fox_f_bias, pool_w, pool_scale, w_out, post_norm_g, loss_target, m_lower_bounds, m_pre_norm_g, m_w_in, m_hgrn_norm_g, m_fox_f_bias, m_pool_w, m_pool_scale, m_w_out, m_post_norm_g, v_lower_bounds, v_pre_norm_g, v_w_in, v_hgrn_norm_g, v_fox_f_bias, v_pool_w, v_pool_scale, v_w_out, v_post_norm_g):
    cx, cy, cc = _place()
    chip = 2 * cx + cy

    halves = lambda w, l: w[l].reshape(2, w.shape[1] // 2, w.shape[2]).astype(BF16)
    needed_first = _gather_weights([halves(w_in, 0)])
    needed_first, later = lax.optimization_barrier((needed_first, [halves(w_out, 0), halves(w_in, 1), halves(w_out, 1)]))
    later = _gather_weights_beside(later)
    w_in_int = [_internal_from_shards([a[q].reshape(D_MODEL, SHARD_W) for q in range(N_CHIPS)]) for a in (needed_first[0], later[1])]
    w_out_full = [a.reshape(D_MODEL, D_MODEL) for a in (later[0], later[2])]

    def on_weight_grads(l, d_w_in, d_w_out):
        pin = _shards_from_internal(d_w_in).reshape(N_CHIPS, 2, D_MODEL // 2, SHARD_W)
        pout = d_w_out.reshape(N_CHIPS, 2, D_MODEL // (2 * N_CHIPS), D_MODEL)
        own = [lax.dynamic_index_in_dim(lax.dynamic_index_in_dim(p, chip, 0, False), cc, 0, False) for p in (pin, pout)]
        return own, _grad_exchange_beside(pin.astype(BF16), pout.astype(BF16), f"grad_exchange{l}", 2 + l)

    sq, grad_x, g, handed = _local_step(x, loss_target, lower_bounds, pre_norm_g, w_in_int, hgrn_norm_g, fox_f_bias,
                                        pool_w, pool_scale, w_out_full, post_norm_g, on_weight_grads)
    first = cc == 0

    def finish(l, own, landed):
        mine = [_add_n([o] + [t[s] for s in range(N_PEERS)], f"grad_sum{l}_{j}") for j, (o, t) in enumerate(zip(own, landed))]
        theirs = _swap_with_sibling(mine, f"grad_swap{l}")
        return [(jnp.where(first, h, o), jnp.where(first, o, h)) for h, o in zip(mine, theirs)]

    grad_x, last = lax.optimization_barrier((grad_x, handed[1]))
    done = [None, finish(1, *last)]

    small = {"lower_bounds": g["lbs"], "pre_norm_g": g["pre"], "hgrn_norm_g": g["hgn"], "fox_f_bias": g["bias"],
             "pool_w": g["pool_w"], "pool_scale": g["pool_scale"], "post_norm_g": g["post"]}
    packet = _pack(small)
    nrows = packet.shape[0]
    packet = packet.at[nrows - 1].set(sq[0])
    gsum = _all_reduce_small(packet)
    loss = gsum[nrows - 1, 0] * (0.5 / D_MODEL)

    weights = {"lower_bounds": lower_bounds, "pre_norm_g": pre_norm_g, "hgrn_norm_g": hgrn_norm_g,
               "fox_f_bias": fox_f_bias, "pool_w": pool_w, "pool_scale": pool_scale, "post_norm_g": post_norm_g}
    moments_m = {"lower_bounds": m_lower_bounds, "pre_norm_g": m_pre_norm_g, "hgrn_norm_g": m_hgrn_norm_g,
                 "fox_f_bias": m_fox_f_bias, "pool_w": m_pool_w, "pool_scale": m_pool_scale, "post_norm_g": m_post_norm_g}
    moments_v = {"lower_bounds": v_lower_bounds, "pre_norm_g": v_pre_norm_g, "hgrn_norm_g": v_hgrn_norm_g,
                 "fox_f_bias": v_fox_f_bias, "pool_w": v_pool_w, "pool_scale": v_pool_scale, "post_norm_g": v_post_norm_g}
    gp, dp, mp, vp = _small_update(gsum, lower_bounds, _pack(weights), _pack(moments_m), _pack(moments_v))
    gs, _ = _unpack(gp, weights)
    ds, _ = _unpack(dp, weights)
    ms, _ = _unpack(mp, weights)
    vs, _ = _unpack(vp, weights)

    first_layer, _ = lax.optimization_barrier((handed[0], (done[1], gp, dp, mp, vp)))
    done[0] = finish(0, *first_layer)
    halves_of = lambda j, side: jnp.stack([done[l][j][side] for l in range(DEPTH)])
    grad_w_in, d_in, m_in, v_in = _adamw(w_in, halves_of(0, 0), halves_of(0, 1), m_w_in, v_w_in, "adamw_w_in")
    grad_w_out, d_out, m_out, v_out = _adamw(w_out, halves_of(1, 0), halves_of(1, 1), m_w_out, v_w_out, "adamw_w_out")

    def ordered(s, big_in, big_out):
        return (s["lower_bounds"], s["pre_norm_g"], big_in, s["hgrn_norm_g"], s["fox_f_bias"], s["pool_w"],
                s["pool_scale"], big_out, s["post_norm_g"])

    return (loss, grad_x, *ordered(gs, grad_w_in, grad_w_out), *ordered(ds, d_in, d_out),
            *ordered(ms, m_in, m_out), *ordered(vs, v_in, v_out))
```

```python
import numpy as np
import jax
import jax.numpy as jnp
from jax import lax
from jax.experimental import pallas as pl
from jax.experimental.pallas import tpu as pltpu
from jax.experimental.pallas import tpu_sc as plsc

F32 = jnp.float32
BF16 = jnp.bfloat16
HI = lax.Precision.HIGHEST
MESH = pl.DeviceIdType.MESH

NORM_EPS = 1e-6
MASK_VALUE = -1e30
TINY = 1e-30
ADAM_LR, ADAM_B1, ADAM_B2, ADAM_EPS, ADAM_WD, ADAM_STEP = 0.001, 0.9, 0.999, 1e-08, 0.01, 10

D_MODEL = 1024
DEPTH = 2
N_CHIPS = 4
CHUNK = 64
LANES = 128
HGRN_W, POOL_W, FOX_W, FOX_HEADS = 256, 256, 512, 8
POOL_WINDOWS = (2, 4, 8, 16)
POOL_HALO = 16
IN_WIDTH = 3592
SHARD_W = IN_WIDTH // N_CHIPS
A_W, B_W, C_W, F_W = 1024, 512, 2048, 128
E_INT = A_W + B_W + C_W + F_W
B_BLK = A_W // 512
C_BLK0 = (A_W + B_W) // 512
F_BLK = (A_W + B_W + C_W) // 128


def _segments():
    segs = []
    for hp in range(2):
        for part in range(4):
            segs.append((part * 256 + hp * 128, 128))
    segs.append((1024, 256))
    segs.append((1280, 256))
    for hp in range(4):
        for part in range(4):
            segs.append((1536 + part * 512 + hp * 128, 128))
    segs.append((3584, 8))
    return segs


_SEGS = _segments()


def _internal_from_shards(shards):
    parts = []
    for s, n in _SEGS:
        while n > 0:
            q, r = divmod(s, SHARD_W)
            take = min(n, SHARD_W - r)
            parts.append(shards[q][..., r:r + take])
            s, n = s + take, n - take
    parts.append(jnp.zeros(shards[0].shape[:-1] + (E_INT - IN_WIDTH,), shards[0].dtype))
    return jnp.concatenate(parts, axis=-1)


def _shards_from_internal(w):
    offs, o = [], 0
    for s, n in _SEGS:
        offs.append((s, o, n))
        o += n
    blocks = []
    for q in range(N_CHIPS):
        lo, hi = SHARD_W * q, SHARD_W * (q + 1)
        parts = [w[..., o + max(lo, s) - s:o + min(hi, s + n) - s] for s, o, n in sorted(offs) if s < hi and s + n > lo]
        blocks.append(jnp.concatenate(parts, axis=-1))
    return jnp.stack(blocks)


def _cparams(sem=None, vmem_mb=48):
    kw = dict(vmem_limit_bytes=vmem_mb * 1024 * 1024)
    if sem is not None:
        kw["dimension_semantics"] = sem
    return pltpu.CompilerParams(**kw)


def _sig(x):
    return 1.0 / (1.0 + jnp.exp(-x))


def _silu(x):
    return x * _sig(x)


def _dsilu(x):
    s = _sig(x)
    return s * (1.0 + x * (1.0 - s))


def _rstd(x):
    return lax.rsqrt(jnp.mean(x * x, axis=-1, keepdims=True) + NORM_EPS)


def _dot(a, b, dims, **kw):
    return lax.dot_general(a, b, (dims, ((), ())), preferred_element_type=F32, **kw)


NN = ((1,), (0,))
NT = ((1,), (1,))
TN = ((0,), (0,))


def _iota(shape, dim):
    return lax.broadcasted_iota(jnp.int32, shape, dim)


def _lbs_fwd(lower_bounds):
    def body(a_ref, o_ref):
        a = a_ref[...]
        a0, a1 = a[0:1], a[1:2]
        m = jnp.maximum(a0, a1)
        e0, e1 = jnp.exp(a0 - m), jnp.exp(a1 - m)
        p0, p1 = e0 / (e0 + e1), e1 / (e0 + e1)
        o_ref[...] = jnp.concatenate([p0 - p0, (p0 + p1) - p0], axis=0)

    return pl.pallas_call(body, out_shape=jax.ShapeDtypeStruct(lower_bounds.shape, F32), name="lbs_fwd")(lower_bounds)


def _inproj_fwd(x2, g_row, w_int, name):
    n, d = x2.shape
    e = w_int.shape[1]
    tm = min(512, n)

    def body(x_ref, g_ref, w_ref, o_ref):
        x = x_ref[...]
        h = (x * _rstd(x) * g_ref[...]).astype(BF16)
        o_ref[...] = jnp.dot(h, w_ref[...], preferred_element_type=F32)

    return pl.pallas_call(
        body, grid=(n // tm,),
        in_specs=[pl.BlockSpec((tm, d), lambda i: (i, 0)), pl.BlockSpec((1, d), lambda i: (0, 0)),
                  pl.BlockSpec((d, e), lambda i: (0, 0))],
        out_specs=pl.BlockSpec((tm, e), lambda i: (i, 0)),
        out_shape=jax.ShapeDtypeStruct((n, e), F32),
        compiler_params=_cparams(("parallel",)), name=name)(x2, g_row, w_int)


def _hgrn_gates(a, lb):
    qa, z = a[:, 0:128], a[:, 128:256]
    sg, sgn = _sig(z), _sig(-z)
    fg = lb + (1.0 - lb) * sg
    lf = jnp.log(jnp.maximum(fg, TINY))
    kk = (1.0 - lb) * sgn
    return qa * _sig(qa), kk, lf, sg, sgn, fg


N_LEVELS = 6


def _hgrn_tables():
    t = np.arange(LANES)
    j = np.arange(LANES)[None, :]
    same_chunk = (t[:, None] // CHUNK) == (j // CHUNK)
    w = np.zeros((2 + N_LEVELS, LANES, LANES), np.float32)
    w[0] = same_chunk & (j <= t[:, None])
    w[1] = same_chunk & (j > t[:, None])
    maskf = np.zeros((N_LEVELS, LANES, LANES), np.float32)
    rightf = np.zeros((N_LEVELS, LANES, LANES), np.float32)
    for li in range(N_LEVELS):
        m = (CHUNK // 2) >> li
        start = t - (t % (2 * m))
        right = (t % (2 * m)) >= m
        first = np.where(right, start + m, t + 1)
        last = np.where(right, t, start + m - 1)
        w[2 + li] = (j >= first[:, None]) & (j <= last[:, None])
        maskf[li] = (t[:, None] // (2 * m)) == (j // (2 * m))
        rightf[li] = right[:, None]
    w = w[:-1]
    return jnp.asarray(w.reshape(-1, LANES), BF16), jnp.asarray(np.tile(maskf, (1, 2, 1))), jnp.asarray(rightf)


def _split(x, n):
    parts = []
    for _ in range(n - 1):
        p = x.astype(BF16)
        parts.append(p)
        x = x - p.astype(F32)
    parts.append(x.astype(BF16))
    return parts


def _exact_dot(w, parts):
    acc = jnp.dot(w, parts[0], preferred_element_type=F32)
    for p in parts[1:]:
        acc = acc + jnp.dot(w, p, preferred_element_type=F32)
    return acc


def _head_sums(v, ones_blk, n=2):
    parts = _split(v, n)
    acc = jnp.dot(parts[0], ones_blk, preferred_element_type=F32)
    for p in parts[1:]:
        acc = acc + jnp.dot(p, ones_blk, preferred_element_type=F32)
    return acc


def _hgrn_consts():
    r, c = _iota((LANES, LANES), 0), _iota((LANES, LANES), 1)
    ones_blk = ((r // CHUNK) == (c // CHUNK)).astype(BF16)
    eye2 = (_iota((2 * LANES, LANES), 0) % LANES) == _iota((2 * LANES, LANES), 1)
    first = _iota((1, LANES), 1) < CHUNK
    return eye2, ones_blk, jnp.ones((LANES, LANES), BF16), first


def _stack_heads(v, first):
    return jnp.concatenate([jnp.where(first, v, 0.0), jnp.where(first, 0.0, v)], axis=0)


def _pick_heads(v2, first):
    return jnp.where(first, v2[:LANES], v2[LANES:])


def _hgrn_levels(qq, kk, lf, zall, mk_ref, rt_ref, first, d_att=None):
    att = jnp.zeros((2 * LANES, LANES), F32)
    dq = dk = db = jnp.zeros((LANES, LANES), F32)
    for li in range(N_LEVELS):
        rt = rt_ref[li]
        e = jnp.exp(zall[(2 + li) * LANES:(3 + li) * LANES] if li < N_LEVELS - 1 else lf * rt)
        mk = mk_ref[li]
        qef, kef = e * rt, e * (1.0 - rt)
        qe, ke = (qq * qef).astype(BF16), (kk * kef).astype(BF16)
        qe2 = _stack_heads(qe, first)
        att = att + _dot(qe2, ke, NT) * mk
        if d_att is not None:
            dam = (d_att * mk).astype(BF16)
            dqe = _pick_heads(jnp.dot(dam, ke, preferred_element_type=F32), first)
            dke = _dot(dam, qe2, TN)
            dq = dq + dqe * qef
            dk = dk + dke * kef
            db = db + (dqe * qe.astype(F32) - dke * ke.astype(F32))
    return att, dq, dk, db


def _hgrn_fwd(proj3, lbs_row, gn_row, name):
    bsz, t, _ = proj3.shape
    nt = t // LANES
    w_all, maskf, rightf = _hgrn_tables()

    def body(a_ref, lb_ref, gn_ref, w_ref, mk_ref, rt_ref, og_ref, or_ref, st_ref):
        lb = lb_ref[...]
        gn = gn_ref[...]
        eye2, ones_blk, ones_all, first = _hgrn_consts()

        def tile(i, carry):
            r0 = pl.multiple_of(i * LANES, LANES)
            a = a_ref[pl.ds(r0, LANES), :]
            qq, kk, lf, _, _, _ = _hgrn_gates(a, lb)
            va, ga = a[:, 256:384], a[:, 384:512]
            parts = _split(lf, 3)
            zall = _exact_dot(w_ref[...], parts)
            eb, ee = jnp.exp(zall[0:LANES]), jnp.exp(zall[LANES:2 * LANES])
            vb = va.astype(BF16)
            att, _, _, _ = _hgrn_levels(qq, kk, lf, zall, mk_ref, rt_ref, first)
            diag = _head_sums(_stack_heads(qq * kk, first), ones_all)
            a2 = (att + jnp.where(eye2, diag, 0.0)).astype(BF16)
            o_in = _pick_heads(jnp.dot(a2, vb, preferred_element_type=F32), first)
            qeb, keb = (qq * eb).astype(BF16), (kk * ee).astype(BF16)
            new_s, o_heads = [], []
            for h in range(2):
                hs = slice(CHUNK * h, CHUNK * (h + 1))
                o_h = o_in[:, hs]
                st = carry[h]
                chunks = []
                for c in range(2):
                    rc = slice(CHUNK * c, CHUNK * (c + 1))
                    st_ref[h, 2 * i + c] = st
                    chunks.append(o_h[rc] + _dot(qeb[rc, hs], st.astype(BF16), NT))
                    ebl = eb[CHUNK * (c + 1) - 1:CHUNK * (c + 1), hs]
                    st = st * ebl + _dot(vb[rc, hs], keb[rc, hs], TN)
                new_s.append(st)
                o_heads.append(jnp.concatenate(chunks, axis=0))
            o = jnp.concatenate(o_heads, axis=1)
            ms = _head_sums(o * o, ones_blk) * (1.0 / CHUNK)
            or_ref[pl.ds(r0, LANES), :] = o
            og_ref[pl.ds(r0, LANES), :] = (o * lax.rsqrt(ms + NORM_EPS) * gn * _silu(ga)).astype(BF16)
            return tuple(new_s)

        zero = jnp.zeros((CHUNK, CHUNK), F32)
        per_step = 4 if nt % 4 == 0 else 2

        def step(i, carry):
            for k in range(per_step):
                carry = tile(per_step * i + k, carry)
            return carry

        lax.fori_loop(0, nt // per_step, step, (zero, zero))

    out = jax.ShapeDtypeStruct((bsz, t, HGRN_W), F32)
    row = pl.BlockSpec((1, 128), lambda b, p: (0, p))
    return pl.pallas_call(
        body, grid=(bsz, 2),
        in_specs=[pl.BlockSpec((None, t, 512), lambda b, p: (b, 0, p)), row, row,
                  pl.BlockSpec(w_all.shape, lambda b, p: (0, 0)),
                  pl.BlockSpec(maskf.shape, lambda b, p: (0, 0, 0)),
                  pl.BlockSpec(rightf.shape, lambda b, p: (0, 0, 0))],
        out_specs=[pl.BlockSpec((None, t, 128), lambda b, p: (b, 0, p)),
                   pl.BlockSpec((None, t, 128), lambda b, p: (b, 0, p)),
                   pl.BlockSpec((None, 2, t // CHUNK, CHUNK, CHUNK), lambda b, p: (b, p, 0, 0, 0))],
        out_shape=[jax.ShapeDtypeStruct((bsz, t, HGRN_W), BF16), out,
                   jax.ShapeDtypeStruct((bsz, 4, t // CHUNK, CHUNK, CHUNK), F32)],
        compiler_params=_cparams(("parallel", "parallel")), name=name)(proj3, lbs_row, gn_row, w_all, maskf, rightf)


def _hgrn_bwd(proj3, o_raw, dmixed, states, lbs_row, gn_row, name):
    bsz, t, _ = proj3.shape
    nt = t // LANES
    nchunk = t // CHUNK
    w_all, maskf, rightf = _hgrn_tables()

    def body(a_ref, or_ref, do_ref, s_sc, lb_ref, gn_ref, w_ref, mk_ref, rt_ref, da_ref, dgn_ref, dlb_ref):
        lb = lb_ref[...]
        gn = gn_ref[...]
        eye2, ones_blk, ones_all, first = _hgrn_consts()
        r_i, c_i = _iota((LANES, LANES), 0), _iota((LANES, LANES), 1)
        suffix = ((c_i >= r_i) & ((r_i // CHUNK) == (c_i // CHUNK))).astype(BF16)
        row64 = _iota((LANES, CHUNK), 0)
        zero = jnp.zeros((CHUNK, CHUNK), F32)

        def bwd_tile(k, carry):
            dst0, dst1, dgn_acc, dlb_acc = carry
            i = nt - 1 - k
            r0 = pl.multiple_of(i * LANES, LANES)
            a = a_ref[pl.ds(r0, LANES), :]
            qa, ga = a[:, 0:128], a[:, 384:512]
            qq, kk, lf, sg, sgn, fg = _hgrn_gates(a, lb)
            parts = _split(lf, 3)
            zall = _exact_dot(w_ref[...], parts)
            eb, ee = jnp.exp(zall[0:LANES]), jnp.exp(zall[LANES:2 * LANES])
            vb = a[:, 256:384].astype(BF16)
            oraw = or_ref[pl.ds(r0, LANES), :]
            dout = do_ref[pl.ds(r0, LANES), :]
            r = lax.rsqrt(_head_sums(oraw * oraw, ones_blk) * (1.0 / CHUNK) + NORM_EPS)
            xn = oraw * r
            dga = dout * (xn * gn) * _dsilu(ga)
            don = dout * _silu(ga)
            dgn_acc = dgn_acc + jnp.sum(don * xn, axis=0, keepdims=True)
            dxn = don * gn
            do = r * (dxn - xn * (_head_sums(dxn * xn, ones_blk) * (1.0 / CHUNK)))
            dob = do.astype(BF16)
            do2 = _stack_heads(dob, first)
            d_att = _dot(do2, vb, NT)
            att, dq, dk, db_lv = _hgrn_levels(qq, kk, lf, zall, mk_ref, rt_ref, first, d_att)
            a2 = att + jnp.where(eye2, _head_sums(_stack_heads(qq * kk, first), ones_all), 0.0)
            dv_in = _dot(a2.astype(BF16), do2, TN)
            ddiag = _pick_heads(_head_sums(jnp.where(eye2, d_att, 0.0), ones_all), first)
            dq_in, dk_in = dq + ddiag * kk, dk + ddiag * qq
            qe_f, ke_f = qq * eb, kk * ee
            qeb, keb = qe_f.astype(BF16), ke_f.astype(BF16)
            new_ds, dq_h, dk_h, dv_h, dbl_h = [], [], [], [], []
            for h in range(2):
                hs = slice(CHUNK * h, CHUNK * (h + 1))
                dv, dq_i, dk_i = dv_in[:, hs], dq_in[:, hs], dk_in[:, hs]
                dst = (dst0, dst1)[h]
                dq_c, dk_c, dv_c, dbl_c = [None, None], [None, None], [None, None], [None, None]
                for c in (1, 0):
                    rc = slice(CHUNK * c, CHUNK * (c + 1))
                    st_n = s_sc[h, 2 * i + c]
                    ebl = eb[CHUNK * (c + 1) - 1:CHUNK * (c + 1), hs]
                    dstb = dst.astype(BF16)
                    dv_c[c] = _dot(keb[rc, hs], dstb, NT)
                    dke = jnp.dot(vb[rc, hs], dstb, preferred_element_type=F32)
                    dqe = jnp.dot(dob[rc, hs], st_n.astype(BF16), preferred_element_type=F32)
                    dbl_c[c] = (jnp.sum(dst * st_n, axis=0, keepdims=True) * ebl
                                + jnp.sum(dke * ke_f[rc, hs], axis=0, keepdims=True))
                    dq_c[c], dk_c[c] = dqe * eb[rc, hs], dke * ee[rc, hs]
                    dst = dst * ebl + _dot(dob[rc, hs], qeb[rc, hs], TN)
                new_ds.append(dst)
                dq_x, dk_x = jnp.concatenate(dq_c, axis=0), jnp.concatenate(dk_c, axis=0)
                dq_h.append(dq_i + dq_x)
                dk_h.append(dk_i + dk_x)
                dv_h.append(dv + jnp.concatenate(dv_c, axis=0))
                dbl_h.append(qq[:, hs] * dq_x - kk[:, hs] * dk_x
                             + jnp.where(row64 == CHUNK - 1, dbl_c[0], 0.0) + jnp.where(row64 == LANES - 1, dbl_c[1], 0.0))
            dqq = jnp.concatenate(dq_h, axis=1)
            dkk = jnp.concatenate(dk_h, axis=1)
            dvv = jnp.concatenate(dv_h, axis=1)
            db = db_lv + jnp.concatenate(dbl_h, axis=1)
            dlf = _exact_dot(suffix, _split(db, 3))
            dqa = dqq * _dsilu(qa)
            dfg = jnp.where(fg > TINY, dlf / fg, 0.0)
            dz = (dfg - dkk) * (1.0 - lb) * sg * sgn
            dlb_acc = dlb_acc + jnp.sum(dfg * (1.0 - sg) - dkk * sgn, axis=0, keepdims=True)
            da_ref[pl.ds(r0, LANES), :] = jnp.concatenate([dqa, dz, dvv, dga], axis=1).astype(BF16)
            return new_ds[0], new_ds[1], dgn_acc, dlb_acc

        zrow = jnp.zeros((1, LANES), F32)
        per_step = 4 if nt % 4 == 0 else 2

        def step(k, carry):
            for r in range(per_step):
                carry = bwd_tile(per_step * k + r, carry)
            return carry

        _, _, dgn_acc, dlb_acc = lax.fori_loop(0, nt // per_step, step, (zero, zero, zrow, zrow))
        dgn_ref[...] = jnp.broadcast_to(dgn_acc, (8, LANES))
        dlb_ref[...] = jnp.broadcast_to(dlb_acc, (8, LANES))

    rows = jax.ShapeDtypeStruct((bsz, 8, HGRN_W), F32)
    row = pl.BlockSpec((1, 128), lambda b, p: (0, p))
    blk = pl.BlockSpec((None, t, 128), lambda b, p: (b, 0, p))
    return pl.pallas_call(
        body, grid=(bsz, 2),
        in_specs=[pl.BlockSpec((None, t, 512), lambda b, p: (b, 0, p)), blk, blk,
                  pl.BlockSpec((None, 2, nchunk, CHUNK, CHUNK), lambda b, p: (b, p, 0, 0, 0)), row, row,
                  pl.BlockSpec(w_all.shape, lambda b, p: (0, 0)),
                  pl.BlockSpec(maskf.shape, lambda b, p: (0, 0, 0)),
                  pl.BlockSpec(rightf.shape, lambda b, p: (0, 0, 0))],
        out_specs=[pl.BlockSpec((None, t, 512), lambda b, p: (b, 0, p)),
                   pl.BlockSpec((None, 8, 128), lambda b, p: (b, 0, p)),
                   pl.BlockSpec((None, 8, 128), lambda b, p: (b, 0, p))],
        out_shape=[jax.ShapeDtypeStruct((bsz, t, A_W), BF16), rows, rows],
        compiler_params=_cparams(("parallel", "parallel")), name=name)(
            proj3, o_raw, dmixed, states, lbs_row, gn_row, w_all, maskf, rightf)


def _pool_tt(t):
    return min(256, t)


def _window_select(s2, s4, s8, s16, lane):
    return jnp.where(lane < 64, s2, jnp.where(lane < 128, s4, jnp.where(lane < 192, s8, s16)))


def _pool_counts(t0, tt):
    lane = _iota((tt, POOL_W), 1)
    tpos = (_iota((tt, POOL_W), 0) + t0 + 1).astype(F32)
    win = jnp.where(lane < 64, 2.0, jnp.where(lane < 128, 4.0, jnp.where(lane < 192, 8.0, 16.0)))
    return 1.0 / jnp.minimum(tpos, win), lane


def _pooled_tile(upad_ref, i, tt):
    r0 = pl.multiple_of(i * tt, 8)
    cat = upad_ref[pl.ds(r0, tt + POOL_HALO), :]
    s2 = cat + pltpu.roll(cat, 1, 0)
    s4 = s2 + pltpu.roll(s2, 2, 0)
    s8 = s4 + pltpu.roll(s4, 4, 0)
    s16 = s8 + pltpu.roll(s8, 8, 0)
    inv, lane = _pool_counts(i * tt, tt)
    sel = _window_select(s2[POOL_HALO:], s4[POOL_HALO:], s8[POOL_HALO:], s16[POOL_HALO:], lane)
    return sel * inv - cat[POOL_HALO:], inv, lane


def _pool_fwd(proj3, wbd, scale_row, name):
    bsz, t, _ = proj3.shape
    tt = _pool_tt(t)

    def body(p_ref, w_ref, sc_ref, o_ref, upad):
        upad[0:POOL_HALO, :] = jnp.zeros((POOL_HALO, POOL_W), F32)
        upad[POOL_HALO:, :] = p_ref[:, 0:POOL_W]
        w = w_ref[...]
        sc = sc_ref[...]

        def tile(i, c):
            pooled, _, _ = _pooled_tile(upad, i, tt)
            r0 = pl.multiple_of(i * tt, 8)
            g = p_ref[pl.ds(r0, tt), POOL_W:2 * POOL_W]
            pre = jnp.dot(pooled.astype(BF16), w, preferred_element_type=F32)
            o_ref[pl.ds(r0, tt), :] = (pre * sc * _silu(g)).astype(BF16)
            return c

        lax.fori_loop(0, t // tt, tile, 0)

    return pl.pallas_call(
        body, grid=(bsz,),
        in_specs=[pl.BlockSpec((None, t, 512), lambda b: (b, 0, B_BLK)),
                  pl.BlockSpec((POOL_W, POOL_W), lambda b: (0, 0)),
                  pl.BlockSpec((1, POOL_W), lambda b: (0, 0))],
        out_specs=pl.BlockSpec((None, t, POOL_W), lambda b: (b, 0, 0)),
        out_shape=jax.ShapeDtypeStruct((bsz, t, POOL_W), BF16),
        scratch_shapes=[pltpu.VMEM((t + POOL_HALO, POOL_W), F32)],
        compiler_params=_cparams(("parallel",)), name=name)(proj3, wbd, scale_row)


def _pool_bwd(proj3, dmixed, wbd, scale_row, name):
    bsz, t, _ = proj3.shape
    tt = _pool_tt(t)

    def body(p_ref, do_ref, w_ref, sc_ref, db_ref, dsc_ref, dw_ref, upad, epad):
        upad[0:POOL_HALO, :] = jnp.zeros((POOL_HALO, POOL_W), F32)
        upad[POOL_HALO:, :] = p_ref[:, 0:POOL_W]
        epad[t:, :] = jnp.zeros((POOL_HALO, POOL_W), F32)
        w = w_ref[...]
        sc = sc_ref[...]

        def tile(i, carry):
            dsc_acc, dw_acc = carry
            pooled, inv, _ = _pooled_tile(upad, i, tt)
            r0 = pl.multiple_of(i * tt, 8)
            g = p_ref[pl.ds(r0, tt), POOL_W:2 * POOL_W]
            dout = do_ref[pl.ds(r0, tt), :]
            pb = pooled.astype(BF16)
            pre = jnp.dot(pb, w, preferred_element_type=F32)
            t1 = dout * _silu(g)
            dsc_acc = dsc_acc + jnp.sum(t1 * pre, axis=0, keepdims=True)
            dpre = (t1 * sc).astype(BF16)
            db_ref[pl.ds(r0, tt), POOL_W:2 * POOL_W] = (dout * pre * sc * _dsilu(g)).astype(BF16)
            dw_acc = dw_acc + _dot(pb, dpre, TN)
            dpooled = _dot(dpre, w, NT)
            epad[pl.ds(r0, tt), :] = dpooled * inv
            return dsc_acc, dw_acc

        dsc_acc, dw_acc = lax.fori_loop(0, t // tt, tile, (jnp.zeros((1, POOL_W), F32), jnp.zeros((POOL_W, POOL_W), F32)))
        dsc_ref[...] = jnp.broadcast_to(dsc_acc, (8, POOL_W))
        dw_ref[...] = dw_acc

        def tile2(i, c):
            r0 = pl.multiple_of(i * tt, 8)
            n = tt + POOL_HALO
            cat = epad[pl.ds(r0, n), :]
            s2 = cat + pltpu.roll(cat, n - 1, 0)
            s4 = s2 + pltpu.roll(s2, n - 2, 0)
            s8 = s4 + pltpu.roll(s4, n - 4, 0)
            s16 = s8 + pltpu.roll(s8, n - 8, 0)
            inv, lane = _pool_counts(i * tt, tt)
            sel = _window_select(s2[:tt], s4[:tt], s8[:tt], s16[:tt], lane)
            db_ref[pl.ds(r0, tt), 0:POOL_W] = (sel - cat[:tt] / inv).astype(BF16)
            return c

        lax.fori_loop(0, t // tt, tile2, 0)

    return pl.pallas_call(
        body, grid=(bsz,),
        in_specs=[pl.BlockSpec((None, t, 512), lambda b: (b, 0, B_BLK)),
                  pl.BlockSpec((None, t, POOL_W), lambda b: (b, 0, 1)),
                  pl.BlockSpec((POOL_W, POOL_W), lambda b: (0, 0)),
                  pl.BlockSpec((1, POOL_W), lambda b: (0, 0))],
        out_specs=[pl.BlockSpec((None, t, 512), lambda b: (b, 0, 0)),
                   pl.BlockSpec((None, 8, POOL_W), lambda b: (b, 0, 0)),
                   pl.BlockSpec((None, POOL_W, POOL_W), lambda b: (b, 0, 0))],
        out_shape=[jax.ShapeDtypeStruct((bsz, t, B_W), BF16), jax.ShapeDtypeStruct((bsz, 8, POOL_W), F32),
                   jax.ShapeDtypeStruct((bsz, POOL_W, POOL_W), F32)],
        scratch_shapes=[pltpu.VMEM((t + POOL_HALO, POOL_W), F32), pltpu.VMEM((t + POOL_HALO, POOL_W), F32)],
        compiler_params=_cparams(("parallel",)), name=name)(proj3, dmixed, wbd, scale_row)


def _head_select_rows(hp):
    r, c = _iota((8, LANES), 0), _iota((8, LANES), 1)
    return ((r < 2) & (c == 2 * hp + r)).astype(F32)


def _foxgate_fwd(proj3, bias_row, name):
    bsz, t, _ = proj3.shape
    nt = t // LANES

    def body(f_ref, b_ref, cn_ref, ct_ref):
        bias = b_ref[...]
        i, j = _iota((LANES, LANES), 0), _iota((LANES, LANES), 1)
        lower = (j <= i).astype(BF16)
        spread = (_iota((LANES, FOX_W), 0) == _iota((LANES, FOX_W), 1) // 64).astype(BF16)
        select = [_head_select_rows(hp).astype(BF16) for hp in range(4)]
        offset = jnp.zeros((1, LANES), F32)
        for k in range(nt):
            rows = slice(k * LANES, (k + 1) * LANES)
            xg = f_ref[rows, :] + bias
            lf = jnp.minimum(xg, 0.0) - jnp.log(1.0 + jnp.exp(-jnp.abs(xg)))
            c = _exact_dot(lower, _split(lf, 3)) + offset
            offset = c[LANES - 1:LANES, :]
            parts = _split(c, 3)
            cn_ref[rows, :] = _head_sums(c, spread, 3)
            for hp in range(4):
                acc = _dot(select[hp], parts[0], NT)
                for p in parts[1:]:
                    acc = acc + _dot(select[hp], p, NT)
                ct_ref[hp, :, rows] = acc

    return pl.pallas_call(
        body, grid=(bsz,),
        in_specs=[pl.BlockSpec((None, t, 128), lambda b: (b, 0, F_BLK)), pl.BlockSpec((1, 128), lambda b: (0, 0))],
        out_specs=[pl.BlockSpec((None, t, FOX_W), lambda b: (b, 0, 0)),
                   pl.BlockSpec((None, 4, 8, t), lambda b: (b, 0, 0, 0))],
        out_shape=[jax.ShapeDtypeStruct((bsz, t, FOX_W), F32), jax.ShapeDtypeStruct((bsz, 4, 8, t), F32)],
        compiler_params=_cparams(("parallel",)), name=name)(proj3, bias_row)


def _foxgate_bwd(proj3, dc_nat, bias_row, name):
    bsz, t, _ = proj3.shape
    nt = t // LANES

    def body(f_ref, dc_ref, b_ref, df_ref, dbias_ref, run_sc):
        bias = b_ref[...]
        i, j = _iota((LANES, LANES), 0), _iota((LANES, LANES), 1)
        upper = (j >= i).astype(F32)
        valid = _iota((1, LANES), 1) < FOX_HEADS
        run_sc[...] = jnp.zeros((8, LANES), F32)
        dbias_ref[...] = jnp.zeros((8, LANES), F32)

        def tile(k, c):
            r0 = pl.multiple_of((nt - 1 - k) * LANES, LANES)
            dc = dc_ref[pl.ds(r0, LANES), :] + jnp.where(i == LANES - 1, run_sc[0:1, :], 0.0)
            dlf = jnp.dot(upper, dc, precision=HI, preferred_element_type=F32)
            xg = f_ref[pl.ds(r0, LANES), :] + bias
            df = jnp.where(valid, dlf * _sig(-xg), 0.0)
            df_ref[pl.ds(r0, LANES), :] = df.astype(BF16)
            run_sc[...] = dlf[0:8, :]
            dbias_ref[...] += jnp.sum(df, axis=0, keepdims=True)
            return c

        lax.fori_loop(0, nt, tile, 0)

    blk = pl.BlockSpec((None, t, 128), lambda b: (b, 0, 0))
    return pl.pallas_call(
        body, grid=(bsz,),
        in_specs=[pl.BlockSpec((None, t, 128), lambda b: (b, 0, F_BLK)), blk, pl.BlockSpec((1, 128), lambda b: (0, 0))],
        out_specs=[blk, pl.BlockSpec((None, 8, 128), lambda b: (b, 0, 0))],
        out_shape=[jax.ShapeDtypeStruct((bsz, t, F_W), BF16), jax.ShapeDtypeStruct((bsz, 8, 128), F32)],
        scratch_shapes=[pltpu.VMEM((8, LANES), F32)],
        compiler_params=_cparams(("parallel",)), name=name)(proj3, dc_nat, bias_row)


def _fox_tile(t):
    return min(256, t)


def _fox_fwd(proj3, c_nat, c_t, name):
    bsz, t, _ = proj3.shape
    tq = tk = min(4 * _fox_tile(t), t)
    nq = t // tq

    def body(q_ref, kv_ref, cn_ref, ct_ref, og_ref, or_ref, lse_ref):
        i = pl.program_id(2)
        qblk = q_ref[...]
        first = _iota((1, 128), 1) < 64
        qv = qblk[:, 0:128] * 0.125
        qm = [jnp.where(first, qv, 0.0).astype(BF16), jnp.where(first, 0.0, qv).astype(BF16)]
        cqs = [cn_ref[:, 0:1], cn_ref[:, 64:65]]
        rows = _iota((tq, tk), 0) + i * tq

        def scores(j):
            c0 = pl.multiple_of(j * tk, tk)
            kb = kv_ref[pl.ds(c0, tk), 128:256].astype(BF16)
            return tuple(_dot(qm[h], kb, NT) + (cqs[h] - ct_ref[h:h + 1, pl.ds(c0, tk)]) for h in range(2))

        def absorb(j, state, s01, masked):
            c0 = pl.multiple_of(j * tk, tk)
            vblk = kv_ref[pl.ds(c0, tk), 256:384]
            vx = [jnp.where(first, vblk, 1.0).astype(BF16), jnp.where(first, 1.0, vblk).astype(BF16)]
            new = []
            for h in range(2):
                m, acc, s = state[2 * h], state[2 * h + 1], s01[h]
                if masked:
                    s = jnp.where(rows >= _iota((tq, tk), 1) + j * tk, s, MASK_VALUE)
                m_new = jnp.maximum(m, jnp.max(s, axis=1, keepdims=True))
                p = jnp.exp(s - m_new).astype(BF16)
                new += [m_new, jnp.exp(m - m_new) * acc + jnp.dot(p, vx[h], preferred_element_type=F32)]
            return tuple(new)

        init = (jnp.full((tq, 1), MASK_VALUE, F32), jnp.zeros((tq, 128), F32)) * 2
        n_full = (i * tq) // tk
        state = lax.fori_loop(0, n_full, lambda j, state: absorb(j, state, scores(j), False), init)
        m0, acc0, m1, acc1 = absorb(n_full, state, scores(n_full), True)
        l0, l1 = pltpu.roll(acc0, 64, 1), pltpu.roll(acc1, 64, 1)
        o = jnp.where(first, acc0 / l0, acc1 / l1)
        or_ref[...] = o
        og_ref[...] = (o * _silu(qblk[:, 384:512])).astype(BF16)
        lse_ref[...] = jnp.where(first, m0 + jnp.log(l0), m1 + jnp.log(l1))

    out = jax.ShapeDtypeStruct((bsz, t, FOX_W), F32)
    blk = pl.BlockSpec((None, tq, 128), lambda b, p, i: (b, i, p))
    return pl.pallas_call(
        body, grid=(bsz, 4, nq),
        in_specs=[pl.BlockSpec((None, tq, 512), lambda b, p, i: (b, i, C_BLK0 + p)),
                  pl.BlockSpec((None, t, 512), lambda b, p, i: (b, 0, C_BLK0 + p)),
                  blk,
                  pl.BlockSpec((None, None, 8, t), lambda b, p, i: (b, p, 0, 0))],
        out_specs=[blk, blk, blk],
        out_shape=[jax.ShapeDtypeStruct((bsz, t, FOX_W), BF16), out, out],
        compiler_params=_cparams(("parallel", "parallel", "arbitrary")), name=name)(proj3, proj3, c_nat, c_t)


def _fox_bwd(proj3, o_raw, dmixed, lse, c_nat, c_t, name):
    bsz, t, _ = proj3.shape
    tq = tk = min(2 * _fox_tile(t), t)
    nq = t // tq

    def body(a_ref, or_ref, do_ref, lse_ref, cn_ref, ct_ref, dc_out, dct_out, drow_out, dq_sc, do_sc, dl_sc):
        def prep(i, c):
            r0 = pl.multiple_of(i * tq, tq)
            g = a_ref[pl.ds(r0, tq), 384:512]
            dout = do_ref[pl.ds(r0, tq), :]
            o = or_ref[pl.ds(r0, tq), :]
            dc_out[pl.ds(r0, tq), 384:512] = (dout * o * _dsilu(g)).astype(BF16)
            do = dout * _silu(g)
            do_sc[pl.ds(r0, tq), :] = do
            prod = do * o
            d0 = jnp.sum(prod[:, 0:64], axis=1, keepdims=True)
            d1 = jnp.sum(prod[:, 64:128], axis=1, keepdims=True)
            dl_sc[pl.ds(r0, tq), :] = jnp.concatenate([jnp.broadcast_to(d0, (tq, 64)), jnp.broadcast_to(d1, (tq, 64))], axis=1)
            dq_sc[pl.ds(r0, tq), :] = jnp.zeros((tq, 128), F32)
            drow_out[pl.ds(r0, tq), :] = jnp.zeros((tq, 128), F32)
            return c

        lax.fori_loop(0, nq, prep, 0)
        dct_out[...] = jnp.zeros((8, t), F32)

        first = _iota((1, 128), 1) < 64

        def heads(v):
            return [jnp.where(first, v, 0.0).astype(BF16), jnp.where(first, 0.0, v).astype(BF16)]

        def kv_tile(j, c):
            c0 = pl.multiple_of(j * tk, tk)
            kb = a_ref[pl.ds(c0, tk), 128:256].astype(BF16)
            vb = a_ref[pl.ds(c0, tk), 256:384].astype(BF16)
            cks = [ct_ref[h:h + 1, pl.ds(c0, tk)] for h in range(2)]

            def q_step(i, carry, diagonal):
                dk, dv, dcol0, dcol1 = carry
                r0 = pl.multiple_of(i * tq, tq)
                causal = _iota((tq, tk), 0) + i * tq >= _iota((tq, tk), 1) + j * tk
                qv = a_ref[pl.ds(r0, tq), 0:128] * 0.125
                do = do_sc[pl.ds(r0, tq), :]
                qb, dob = qv.astype(BF16), do.astype(BF16)
                qm, dom = heads(qv), heads(do)
                full, dcols, rsums = [], [], []
                for h in range(2):
                    lse_h = lse_ref[pl.ds(r0, tq), 64 * h:64 * h + 1]
                    dl_h = dl_sc[pl.ds(r0, tq), 64 * h:64 * h + 1]
                    cq = cn_ref[pl.ds(r0, tq), 64 * h:64 * h + 1]
                    p = jnp.exp(_dot(qm[h], kb, NT) + (cq - cks[h]) - lse_h)
                    if diagonal:
                        p = jnp.where(causal, p, 0.0)
                    ds = p * (_dot(dom[h], vb, NT) - dl_h)
                    dsb = ds.astype(BF16)
                    full.append((_dot(p.astype(BF16), dob, TN), _dot(dsb, qb, TN),
                                 jnp.dot(dsb, kb, preferred_element_type=F32)))
                    dcols.append(jnp.sum(ds, axis=0, keepdims=True))
                    rsums.append(jnp.broadcast_to(jnp.sum(ds, axis=1, keepdims=True), (tq, 128)))
                dq_sc[pl.ds(r0, tq), :] += jnp.where(first, full[0][2], full[1][2]) * 0.125
                drow_out[pl.ds(r0, tq), :] += jnp.where(first, rsums[0], rsums[1])
                return (dk + jnp.where(first, full[0][1], full[1][1]), dv + jnp.where(first, full[0][0], full[1][0]),
                        dcol0 - dcols[0], dcol1 - dcols[1])

            carry = (jnp.zeros((tk, 128), F32), jnp.zeros((tk, 128), F32), jnp.zeros((1, tk), F32), jnp.zeros((1, tk), F32))
            carry = q_step(j, carry, True)
            dk, dv, dcol0, dcol1 = lax.fori_loop(j + 1, nq, lambda i, carry: q_step(i, carry, False), carry)
            dct_out[0:1, pl.ds(c0, tk)] = dcol0
            dct_out[1:2, pl.ds(c0, tk)] = dcol1
            dc_out[pl.ds(c0, tk), 128:256] = dk.astype(BF16)
            dc_out[pl.ds(c0, tk), 256:384] = dv.astype(BF16)
            return c

        lax.fori_loop(0, t // tk, kv_tile, 0)
        dc_out[:, 0:128] = dq_sc[...].astype(BF16)

    blk = pl.BlockSpec((None, t, 128), lambda b, p: (b, 0, p))
    return pl.pallas_call(
        body, grid=(bsz, 4),
        in_specs=[pl.BlockSpec((None, t, 512), lambda b, p: (b, 0, C_BLK0 + p)),
                  blk,
                  pl.BlockSpec((None, t, 128), lambda b, p: (b, 0, 4 + p)),
                  blk, blk,
                  pl.BlockSpec((None, None, 8, t), lambda b, p: (b, p, 0, 0))],
        out_specs=[pl.BlockSpec((None, t, 512), lambda b, p: (b, 0, p)),
                   pl.BlockSpec((None, None, 8, t), lambda b, p: (b, p, 0, 0)), blk],
        out_shape=[jax.ShapeDtypeStruct((bsz, t, C_W), BF16), jax.ShapeDtypeStruct((bsz, 4, 8, t), F32),
                   jax.ShapeDtypeStruct((bsz, t, FOX_W), F32)],
        scratch_shapes=[pltpu.VMEM((t, 128), F32), pltpu.VMEM((t, 128), F32), pltpu.VMEM((t, 128), F32)],
        compiler_params=_cparams(("parallel", "parallel")), name=name)(proj3, o_raw, dmixed, lse, c_nat, c_t)


def _mix_tm(n):
    return min(512, n)


def _outproj_fwd(x2, oa, ob, oc, wo, g_row, name):
    n, d = x2.shape
    tm = _mix_tm(n)

    def body(x_ref, oa_ref, ob_ref, oc_ref, w_ref, g_ref, y_ref, xo_ref):
        y = (jnp.dot(oa_ref[...].astype(BF16), w_ref[0:256, :], preferred_element_type=F32)
             + jnp.dot(ob_ref[...].astype(BF16), w_ref[256:512, :], preferred_element_type=F32)
             + jnp.dot(oc_ref[...].astype(BF16), w_ref[512:1024, :], preferred_element_type=F32))
        y_ref[...] = y
        xo_ref[...] = x_ref[...] + y * _rstd(y) * g_ref[...]

    row = lambda w: pl.BlockSpec((tm, w), lambda i: (i, 0))
    out = jax.ShapeDtypeStruct((n, d), F32)
    return pl.pallas_call(
        body, grid=(n // tm,),
        in_specs=[row(d), row(256), row(256), row(512), pl.BlockSpec((d, d), lambda i: (0, 0)),
                  pl.BlockSpec((1, d), lambda i: (0, 0))],
        out_specs=[row(d), row(d)], out_shape=[out, out],
        compiler_params=_cparams(("parallel",)), name=name)(x2, oa, ob, oc, wo, g_row)


def _outproj_fwd_loss(x2, oa, ob, oc, wo, g_row, target2, name):
    n, d = x2.shape
    tm = _mix_tm(n)

    def body(x_ref, oa_ref, ob_ref, oc_ref, w_ref, g_ref, t_ref, y_ref, dx_ref, l_ref):
        y = (jnp.dot(oa_ref[...].astype(BF16), w_ref[0:256, :], preferred_element_type=F32)
             + jnp.dot(ob_ref[...].astype(BF16), w_ref[256:512, :], preferred_element_type=F32)
             + jnp.dot(oc_ref[...].astype(BF16), w_ref[512:1024, :], preferred_element_type=F32))
        y_ref[...] = y
        err = (x_ref[...] + y * _rstd(y) * g_ref[...]) - t_ref[...]
        dx_ref[...] = err * (1.0 / d)

        @pl.when(pl.program_id(0) == 0)
        def _():
            l_ref[...] = jnp.zeros((8, 128), F32)

        l_ref[...] += jnp.sum(err * err)

    row = lambda w: pl.BlockSpec((tm, w), lambda i: (i, 0))
    out = jax.ShapeDtypeStruct((n, d), F32)
    return pl.pallas_call(
        body, grid=(n // tm,),
        in_specs=[row(d), row(256), row(256), row(512), pl.BlockSpec((d, d), lambda i: (0, 0)),
                  pl.BlockSpec((1, d), lambda i: (0, 0)), row(d)],
        out_specs=[row(d), row(d), pl.BlockSpec((8, 128), lambda i: (0, 0))],
        out_shape=[out, out, jax.ShapeDtypeStruct((8, 128), F32)],
        compiler_params=_cparams(("arbitrary",)), name=name)(x2, oa, ob, oc, wo, g_row, target2)


def _outproj_bwd(dxo, y, oa, ob, oc, wo, g_row, name):
    n, d = dxo.shape
    tm = _mix_tm(n)

    def body(dx_ref, y_ref, oa_ref, ob_ref, oc_ref, w_ref, g_ref, dm_ref, dw_ref, dg_ref):
        @pl.when(pl.program_id(0) == 0)
        def _():
            dw_ref[...] = jnp.zeros((d, d), F32)
            dg_ref[...] = jnp.zeros((8, d), F32)

        yv, dx = y_ref[...], dx_ref[...]
        r = _rstd(yv)
        yn = yv * r
        dg_ref[...] += jnp.sum(dx * yn, axis=0, keepdims=True)
        dyn = dx * g_ref[...]
        dy = (r * (dyn - yn * jnp.mean(dyn * yn, axis=-1, keepdims=True))).astype(BF16)
        dm_ref[...] = _dot(dy, w_ref[...], NT)
        dw_ref[0:256, :] += _dot(oa_ref[...].astype(BF16), dy, TN)
        dw_ref[256:512, :] += _dot(ob_ref[...].astype(BF16), dy, TN)
        dw_ref[512:1024, :] += _dot(oc_ref[...].astype(BF16), dy, TN)

    row = lambda w: pl.BlockSpec((tm, w), lambda i: (i, 0))
    fixed = lambda r, c: pl.BlockSpec((r, c), lambda i: (0, 0))
    return pl.pallas_call(
        body, grid=(n // tm,),
        in_specs=[row(d), row(d), row(256), row(256), row(512), fixed(d, d), fixed(1, d)],
        out_specs=[row(d), fixed(d, d), fixed(8, d)],
        out_shape=[jax.ShapeDtypeStruct((n, d), F32), jax.ShapeDtypeStruct((d, d), F32), jax.ShapeDtypeStruct((8, d), F32)],
        compiler_params=_cparams(("arbitrary",)), name=name)(dxo, y, oa, ob, oc, wo, g_row)


_PIECES = ((0, A_W), (A_W, B_W), (A_W + B_W, C_W), (A_W + B_W + C_W, F_W))


def _inproj_bwd_x(x2, dxo, g_row, w_int, pieces, name):
    n, d = x2.shape
    tm = min(512, n)

    def body(x_ref, dxo_ref, g_ref, w_ref, da_ref, db_ref, dc_ref, df_ref, dx_ref, dg_ref):
        @pl.when(pl.program_id(0) == 0)
        def _():
            dg_ref[...] = jnp.zeros((8, d), F32)

        dh = jnp.zeros((tm, d), F32)
        for ref, (o, w) in zip((da_ref, db_ref, dc_ref, df_ref), _PIECES):
            dh = dh + _dot(ref[...].astype(BF16), w_ref[:, o:o + w], NT)
        x = x_ref[...]
        r = _rstd(x)
        xn = x * r
        dg_ref[...] += jnp.sum(dh * xn, axis=0, keepdims=True)
        dxn = dh * g_ref[...]
        dx_ref[...] = dxo_ref[...] + r * (dxn - xn * jnp.mean(dxn * xn, axis=-1, keepdims=True))

    row = lambda w: pl.BlockSpec((tm, w), lambda i: (i, 0))
    fixed = lambda r, c: pl.BlockSpec((r, c), lambda i: (0, 0))
    return pl.pallas_call(
        body, grid=(n // tm,),
        in_specs=[row(d), row(d), fixed(1, d), fixed(d, E_INT)] + [row(w) for _, w in _PIECES],
        out_specs=[row(d), fixed(8, d)],
        out_shape=[jax.ShapeDtypeStruct((n, d), F32), jax.ShapeDtypeStruct((8, d), F32)],
        compiler_params=_cparams(("arbitrary",), vmem_mb=56), name=name)(x2, dxo, g_row, w_int, *pieces)


def _inproj_bwd_w(x2, g_row, pieces, name):
    n, d = x2.shape
    tm = min(512, n)

    def body(x_ref, g_ref, da_ref, db_ref, dc_ref, df_ref, dw_ref):
        @pl.when(pl.program_id(0) == 0)
        def _():
            dw_ref[...] = jnp.zeros((d, E_INT), F32)

        x = x_ref[...]
        h = (x * _rstd(x) * g_ref[...]).astype(BF16)
        for ref, (o, w) in zip((da_ref, db_ref, dc_ref, df_ref), _PIECES):
            dw_ref[:, o:o + w] += _dot(h, ref[...].astype(BF16), TN)

    row = lambda w: pl.BlockSpec((tm, w), lambda i: (i, 0))
    return pl.pallas_call(
        body, grid=(n // tm,),
        in_specs=[row(d), pl.BlockSpec((1, d), lambda i: (0, 0))] + [row(w) for _, w in _PIECES],
        out_specs=pl.BlockSpec((d, E_INT), lambda i: (0, 0)),
        out_shape=jax.ShapeDtypeStruct((d, E_INT), F32),
        compiler_params=_cparams(("arbitrary",), vmem_mb=56), name=name)(x2, g_row, *pieces)


def _block_diag(pool_w_l):
    z = jnp.zeros((64, 64), pool_w_l.dtype)
    return jnp.concatenate(
        [jnp.concatenate([pool_w_l[g] if c == g else z for c in range(4)], axis=1) for g in range(4)], axis=0)


def _pad_lanes(v, width=128):
    return jnp.pad(v, ((0, 0),) * (v.ndim - 1) + ((0, width - v.shape[-1]),))


def _local_step(x, target, lower_bounds, pre_norm_g, w_in_int, hgrn_norm_g, fox_f_bias, pool_w, pool_scale,
                w_out_bf, post_norm_g, on_weight_grads):
    bsz, t, d = x.shape
    n = bsz * t
    lbs = _lbs_fwd(lower_bounds)
    saved = []
    xc = x.reshape(n, d)
    for l in range(DEPTH):
        proj = _inproj_fwd(xc, pre_norm_g[l:l + 1], w_in_int[l], f"inproj_fwd{l}").reshape(bsz, t, E_INT)
        wbd = _block_diag(pool_w[l]).astype(BF16)
        bias_row = _pad_lanes(fox_f_bias[l:l + 1])
        oa, oa_raw, states = _hgrn_fwd(proj, lbs[l:l + 1], hgrn_norm_g[l:l + 1], f"hgrn_fwd{l}")
        ob = _pool_fwd(proj, wbd, pool_scale[l:l + 1], f"pool_fwd{l}")
        c_nat, c_t = _foxgate_fwd(proj, bias_row, f"foxgate_fwd{l}")
        oc, oc_raw, lse = _fox_fwd(proj, c_nat, c_t, f"fox_fwd{l}")
        mixed = (oa.reshape(n, -1), ob.reshape(n, -1), oc.reshape(n, -1))
        if l < DEPTH - 1:
            y, xn = _outproj_fwd(xc, *mixed, w_out_bf[l], post_norm_g[l:l + 1], f"outproj_fwd{l}")
        else:
            y, dx, sq = _outproj_fwd_loss(xc, *mixed, w_out_bf[l], post_norm_g[l:l + 1], target.reshape(n, d),
                                          f"outproj_fwd{l}")
        saved.append((xc, proj, wbd, bias_row, oa, oa_raw, states, ob, oc, oc_raw, lse, c_nat, c_t, y))
        xc = xn
    g = {k: [None] * DEPTH for k in ("pre", "hgn", "bias", "pool_w", "pool_scale", "post", "lbs")}
    handed = [None] * DEPTH
    for l in reversed(range(DEPTH)):
        xin, proj, wbd, bias_row, oa, oa_raw, states, ob, oc, oc_raw, lse, c_nat, c_t, y = saved[l]
        dmix, d_w_out, dpost = _outproj_bwd(dx, y, oa.reshape(n, -1), ob.reshape(n, -1), oc.reshape(n, -1),
                                            w_out_bf[l], post_norm_g[l:l + 1], f"outproj_bwd{l}")
        g["post"][l] = dpost[0]
        dmix3 = dmix.reshape(bsz, t, d)
        d_c, dct, drow = _fox_bwd(proj, oc_raw, dmix3, lse, c_nat, c_t, f"fox_bwd{l}")
        dc_nat = _pad_lanes(dct[:, :, 0:2, :].reshape(bsz, FOX_HEADS, t).transpose(0, 2, 1)
                            + drow.reshape(bsz, t, FOX_HEADS, 64)[..., 0])
        d_f, dbias = _foxgate_bwd(proj, dc_nat, bias_row, f"foxgate_bwd{l}")
        g["bias"][l] = jnp.sum(dbias[:, 0, :FOX_HEADS], axis=0)
        d_b, dscale, dwbd = _pool_bwd(proj, dmix3, wbd, pool_scale[l:l + 1], f"pool_bwd{l}")
        g["pool_scale"][l] = jnp.sum(dscale[:, 0], axis=0)
        dwbd = jnp.sum(dwbd, axis=0)
        g["pool_w"][l] = jnp.stack([dwbd[64 * k:64 * (k + 1), 64 * k:64 * (k + 1)] for k in range(4)])
        d_a, dgn, dlb = _hgrn_bwd(proj, oa_raw, dmix3, states, lbs[l:l + 1], hgrn_norm_g[l:l + 1], f"hgrn_bwd{l}")
        g["hgn"][l] = jnp.sum(dgn[:, 0], axis=0)
        g["lbs"][l] = jnp.sum(dlb[:, 0], axis=0)
        pieces = [p.reshape(n, -1) for p in (d_a, d_b, d_c, d_f)]
        handed[l] = on_weight_grads(l, _inproj_bwd_w(xin, pre_norm_g[l:l + 1], pieces, f"inproj_bwd_w{l}"), d_w_out)
        dx, dpre = _inproj_bwd_x(xin, dx, pre_norm_g[l:l + 1], w_in_int[l], pieces, f"inproj_bwd_x{l}")
        g["pre"][l] = dpre[0]
    grads = {k: jnp.stack(v) for k, v in g.items()}
    return sq, dx.reshape(bsz, t, d), grads, handed


def _place():
    return lax.axis_index("x"), lax.axis_index("y"), lax.axis_index("c")


def _other_chips(x, y):
    return [(1 - x, y), (x, 1 - y), (1 - x, 1 - y)]


_ANY = pl.BlockSpec(memory_space=pl.ANY)


def _gather_body(handshake, n_arrays):
    def body(*refs):
        srcs, dsts = refs[:n_arrays], refs[n_arrays:2 * n_arrays]
        ici_send, ici_recv, d2d_send, d2d_recv, local_sems = refs[2 * n_arrays:]
        x, y, c = _place()
        if handshake:
            barrier = pltpu.get_barrier_semaphore()
            for peer in [(px, py, c) for px, py in _other_chips(x, y)] + [(x, y, 1 - c)]:
                pl.semaphore_signal(barrier, inc=1, device_id=peer, device_id_type=MESH)
            pl.semaphore_wait(barrier, 4)
        me = 2 * x + y
        pairs = list(zip(srcs, dsts))
        order = [(k, j) for k in range(3) for j in range(n_arrays)]
        mine = [pltpu.make_async_copy(src, dst.at[me], local_sems.at[j]) for j, (src, dst) in enumerate(pairs)]
        for cp in mine:
            cp.start()
        chips = _other_chips(x, y)
        sends = [pltpu.make_async_remote_copy(
            src_ref=pairs[j][0].at[c], dst_ref=pairs[j][1].at[me, c], send_sem=ici_send.at[n], recv_sem=ici_recv.at[n],
            device_id=(chips[k][0], chips[k][1], c), device_id_type=MESH) for n, (k, j) in enumerate(order)]
        for cp in sends:
            cp.start()
        passed = [pltpu.make_async_remote_copy(
            src_ref=pairs[j][1].at[2 * chips[k][0] + chips[k][1], c], dst_ref=pairs[j][1].at[2 * chips[k][0] + chips[k][1], c],
            send_sem=d2d_send.at[n], recv_sem=d2d_recv.at[n], device_id=(x, y, 1 - c), device_id_type=MESH)
            for n, (k, j) in enumerate(order)]
        for n, (k, j) in enumerate(order):
            px, py = chips[k]
            src, dst = pairs[j]
            pltpu.make_async_remote_copy(
                src_ref=src.at[c], dst_ref=dst.at[2 * px + py, c], send_sem=ici_send.at[n], recv_sem=ici_recv.at[n],
                device_id=(px, py, c), device_id_type=MESH).wait_recv()
            passed[n].start()
        for n, (k, j) in enumerate(order):
            px, py = chips[k]
            src, dst = pairs[j]
            pltpu.make_async_remote_copy(
                src_ref=dst.at[2 * px + py, 1 - c], dst_ref=dst.at[2 * px + py, 1 - c], send_sem=d2d_send.at[n],
                recv_sem=d2d_recv.at[n], device_id=(x, y, 1 - c), device_id_type=MESH).wait_recv()
        for cp in sends + passed:
            cp.wait_send()
        for cp in mine:
            cp.wait()

    return body


def _gather_sems(n_arrays):
    return [pltpu.SemaphoreType.DMA((3 * n_arrays,))] * 4 + [pltpu.SemaphoreType.DMA((n_arrays,))]


def _gathered(a):
    return jax.ShapeDtypeStruct((N_CHIPS,) + a.shape, a.dtype)


def _gather_weights(arrays):
    n = len(arrays)
    return pl.pallas_call(
        _gather_body(False, n), in_specs=[_ANY] * n, out_specs=[_ANY] * n, out_shape=[_gathered(a) for a in arrays],
        scratch_shapes=_gather_sems(n), name="gather_weights")(*arrays)


def _gather_weights_beside(arrays, name, collective_id):
    hbm = pltpu.MemorySpace.HBM
    n = len(arrays)
    srcs = [jax.new_ref(a, memory_space=hbm) for a in arrays]
    dsts = [jax.empty_ref(_gathered(a), memory_space=hbm) for a in arrays]
    body = _gather_body(True, n)

    @pl.kernel(mesh=plsc.ScalarSubcoreMesh(axis_name="sequencer", num_cores=1), name=name,
               scratch_types=_gather_sems(n), compiler_params=pltpu.CompilerParams(collective_id=collective_id))
    def launch(*sems):
        body(*srcs, *dsts, *sems)

    launch()
    return [d[...] for d in dsts]


def _swap_with_sibling(parts, name):
    k = len(parts)

    def body(*refs):
        src, dst = refs[:k], refs[k:2 * k]
        send_sems, recv_sems = refs[2 * k:]
        x, y, c = _place()
        cps = [pltpu.make_async_remote_copy(src_ref=src[j], dst_ref=dst[j], send_sem=send_sems.at[j], recv_sem=recv_sems.at[j],
                                            device_id=(x, y, 1 - c), device_id_type=MESH) for j in range(k)]
        for cp in cps:
            cp.start()
        for cp in cps:
            cp.wait()

    return pl.pallas_call(
        body, in_specs=[_ANY] * k, out_specs=[_ANY] * k,
        out_shape=[jax.ShapeDtypeStruct(p.shape, p.dtype) for p in parts],
        scratch_shapes=[pltpu.SemaphoreType.DMA((k,)), pltpu.SemaphoreType.DMA((k,))], name=name)(*parts)


N_PEERS = 7


def _grad_exchange_body():
    def body(pin_ref, pout_ref, lin_ref, lout_ref, send_sems, recv_sems):
        x, y, c = _place()
        barrier = pltpu.get_barrier_semaphore()
        for k in range(1, N_PEERS + 1):
            peer = (x ^ ((k >> 2) & 1), y ^ ((k >> 1) & 1), c ^ (k & 1))
            pl.semaphore_signal(barrier, inc=1, device_id=peer, device_id_type=MESH)
        pl.semaphore_wait(barrier, N_PEERS)
        me = 2 * x + y
        pairs = ((pin_ref, lin_ref), (pout_ref, lout_ref))
        cps = []
        for k, (px, py) in enumerate(_other_chips(x, y)):
            for r in range(2):
                for j, (src, dst) in enumerate(pairs):
                    cps.append(pltpu.make_async_remote_copy(
                        src_ref=src.at[2 * px + py, r], dst_ref=dst.at[2 * k + c], send_sem=send_sems.at[2 * (2 * k + r) + j],
                        recv_sem=recv_sems.at[2 * (2 * k + c) + j], device_id=(px, py, r), device_id_type=MESH))
        for j, (src, dst) in enumerate(pairs):
            cps.append(pltpu.make_async_remote_copy(
                src_ref=src.at[me, 1 - c], dst_ref=dst.at[N_PEERS - 1], send_sem=send_sems.at[2 * (N_PEERS - 1) + j],
                recv_sem=recv_sems.at[2 * (N_PEERS - 1) + j], device_id=(x, y, 1 - c), device_id_type=MESH))
        for cp in cps:
            cp.start()
        for s in range(N_PEERS):
            for j, (src, dst) in enumerate(pairs):
                pltpu.make_async_remote_copy(
                    src_ref=src.at[0, 0], dst_ref=dst.at[s], send_sem=send_sems.at[2 * s + j], recv_sem=recv_sems.at[2 * s + j],
                    device_id=(x, y, 1 - c), device_id_type=MESH).wait_recv()
        for cp in cps:
            cp.wait_send()

    return body


_EXCHANGE_SEMS = [pltpu.SemaphoreType.DMA((2 * N_PEERS,))] * 2


def _landing(p):
    return jax.ShapeDtypeStruct((N_PEERS,) + p.shape[2:], p.dtype)


def _grad_exchange_beside(pin, pout, name, collective_id):
    hbm = pltpu.MemorySpace.HBM
    pin_ref, pout_ref = jax.new_ref(pin, memory_space=hbm), jax.new_ref(pout, memory_space=hbm)
    lin_ref, lout_ref = jax.empty_ref(_landing(pin), memory_space=hbm), jax.empty_ref(_landing(pout), memory_space=hbm)
    body = _grad_exchange_body()

    @pl.kernel(mesh=plsc.ScalarSubcoreMesh(axis_name="sequencer", num_cores=1), name=name,
               scratch_types=_EXCHANGE_SEMS, compiler_params=pltpu.CompilerParams(collective_id=collective_id))
    def launch(send_sems, recv_sems):
        body(pin_ref, pout_ref, lin_ref, lout_ref, send_sems, recv_sems)

    launch()
    return lin_ref[...], lout_ref[...]


def _add_n(parts, name):
    r, c = parts[0].shape
    tr = 256 if r % 256 == 0 else r
    n = len(parts)

    def body(*refs):
        acc = refs[0][...].astype(F32)
        for ref in refs[1:n]:
            acc = acc + ref[...].astype(F32)
        refs[n][...] = acc

    blk = pl.BlockSpec((tr, c), lambda i: (i, 0))
    return pl.pallas_call(
        body, grid=(r // tr,), in_specs=[blk] * n, out_specs=blk, out_shape=jax.ShapeDtypeStruct((r, c), F32),
        compiler_params=_cparams(("parallel",)), name=name)(*parts)


def _all_reduce_small(packet):
    r, w = packet.shape

    def body(p_ref, o_ref, buf, send_sems, recv_sems):
        x, y, c = _place()
        me = 4 * x + 2 * y + c
        buf[me] = p_ref[...]
        peers = []
        for k in range(1, 8):
            fx, fy, fc = (k >> 2) & 1, (k >> 1) & 1, k & 1
            peers.append((x ^ fx, y ^ fy, c ^ fc))
        cps = [pltpu.make_async_remote_copy(src_ref=p_ref, dst_ref=buf.at[me], send_sem=send_sems.at[k], recv_sem=recv_sems.at[k],
                                            device_id=peer, device_id_type=MESH) for k, peer in enumerate(peers)]
        for cp in cps:
            cp.start()
        for k, (px, py, pc) in enumerate(peers):
            pltpu.make_async_remote_copy(src_ref=p_ref, dst_ref=buf.at[4 * px + 2 * py + pc], send_sem=send_sems.at[k],
                                         recv_sem=recv_sems.at[k], device_id=(px, py, pc), device_id_type=MESH).wait_recv()
        for cp in cps:
            cp.wait_send()
        acc = buf[0]
        for k in range(1, 8):
            acc = acc + buf[k]
        o_ref[...] = acc

    vm = pl.BlockSpec(memory_space=pltpu.VMEM)
    return pl.pallas_call(
        body, in_specs=[vm], out_specs=vm, out_shape=jax.ShapeDtypeStruct((r, w), F32),
        scratch_shapes=[pltpu.VMEM((8, r, w), F32), pltpu.SemaphoreType.DMA((7,)), pltpu.SemaphoreType.DMA((7,))],
        name="all_reduce_small")(packet)


def _adamw_math(w, g, m, v):
    m = ADAM_B1 * m + (1.0 - ADAM_B1) * g
    v = ADAM_B2 * v + (1.0 - ADAM_B2) * (g * g)
    m_hat = m / (1.0 - ADAM_B1 ** ADAM_STEP)
    v_hat = v / (1.0 - ADAM_B2 ** ADAM_STEP)
    return -ADAM_LR * (m_hat / (jnp.sqrt(v_hat) + ADAM_EPS) + ADAM_WD * w), m, v


def _adamw_start(w, m, v, name):
    nl, r, c = w.shape
    tr = 128

    def body(w_ref, m_ref, v_ref, wd_ref, mo_ref, vo_ref):
        wd_ref[...], mo_ref[...], vo_ref[...] = ADAM_WD * w_ref[...], ADAM_B1 * m_ref[...], ADAM_B2 * v_ref[...]

    blk = pl.BlockSpec((None, tr, c), lambda l, i: (l, i, 0))
    out = jax.ShapeDtypeStruct(w.shape, F32)
    return pl.pallas_call(
        body, grid=(nl, r // tr), in_specs=[blk] * 3, out_specs=[blk] * 3, out_shape=[out] * 3,
        compiler_params=_cparams(("parallel", "parallel")), name=name)(w, m, v)


def _adamw(started, g_lower, g_upper, name):
    nl, r, c = started[0].shape
    tr = 128
    per_half = r // (2 * tr)

    def body(wd_ref, m_ref, v_ref, lo_ref, up_ref, g_ref, d_ref, mo_ref, vo_ref):
        g = jnp.where(pl.program_id(1) == 0, lo_ref[...], up_ref[...])
        g_ref[...] = g
        m = m_ref[...] + (1.0 - ADAM_B1) * g
        v = v_ref[...] + (1.0 - ADAM_B2) * (g * g)
        m_hat = m / (1.0 - ADAM_B1 ** ADAM_STEP)
        v_hat = v / (1.0 - ADAM_B2 ** ADAM_STEP)
        d_ref[...] = -ADAM_LR * (m_hat / (jnp.sqrt(v_hat) + ADAM_EPS) + wd_ref[...])
        mo_ref[...], vo_ref[...] = m, v

    blk = pl.BlockSpec((None, tr, c), lambda l, h, i: (l, h * per_half + i, 0))
    half = pl.BlockSpec((None, tr, c), lambda l, h, i: (l, i, 0))
    out = jax.ShapeDtypeStruct(started[0].shape, F32)
    return pl.pallas_call(
        body, grid=(nl, 2, per_half), in_specs=[blk, blk, blk, half, half], out_specs=[blk] * 4, out_shape=[out] * 4,
        compiler_params=_cparams(("parallel", "parallel", "parallel")), name=name)(*started, g_lower, g_upper)


def _small_update(gsum, lower_bounds, wpack, mpack, vpack):
    r, w = gsum.shape
    lb_rows = DEPTH * HGRN_W // 128

    def body(g_ref, a_ref, w_ref, m_ref, v_ref, go_ref, d_ref, mo_ref, vo_ref):
        a = a_ref[...]
        a0, a1 = a[0:1], a[1:2]
        mx = jnp.maximum(a0, a1)
        e0, e1 = jnp.exp(a0 - mx), jnp.exp(a1 - mx)
        p0, p1 = e0 / (e0 + e1), e1 / (e0 + e1)
        g = g_ref[...]
        half = lb_rows // 2
        dl0 = jnp.concatenate([g[k:k + 1] for k in range(half)], axis=1)
        dl1 = jnp.concatenate([g[half + k:half + k + 1] for k in range(half)], axis=1)
        dp0 = (dl0 + dl1) - (dl0 + dl1)
        dp1 = dl1
        inner = p0 * dp0 + p1 * dp1
        da0, da1 = p0 * (dp0 - inner), p1 * (dp1 - inner)
        rows = [da0[:, 128 * k:128 * (k + 1)] for k in range(half)] + [da1[:, 128 * k:128 * (k + 1)] for k in range(half)]
        gfull = jnp.concatenate(rows + [g[lb_rows:]], axis=0)
        go_ref[...] = gfull
        d_ref[...], mo_ref[...], vo_ref[...] = _adamw_math(w_ref[...], gfull, m_ref[...], v_ref[...])

    vm = pl.BlockSpec(memory_space=pltpu.VMEM)
    out = jax.ShapeDtypeStruct((r, w), F32)
    return pl.pallas_call(body, in_specs=[vm] * 5, out_specs=[vm] * 4, out_shape=[out] * 4, name="small_update")(
        gsum, lower_bounds, wpack, mpack, vpack)


_SMALL = ("lower_bounds", "pre_norm_g", "hgrn_norm_g", "fox_f_bias", "pool_w", "pool_scale", "post_norm_g")


def _pack(parts):
    rows = []
    for k in _SMALL:
        f = parts[k].reshape(-1)
        pad = (-f.shape[0]) % (8 * 128)
        rows.append(jnp.pad(f, (0, pad)).reshape(-1, 128))
    rows.append(jnp.zeros((8, 128), F32))
    return jnp.concatenate(rows, axis=0)


def _unpack(pack, like):
    out, r = {}, 0
    for k in _SMALL:
        size = int(np.prod(like[k].shape))
        nr = -(-size // (8 * 128)) * 8
        out[k] = pack[r:r + nr].reshape(-1)[:size].reshape(like[k].shape)
        r += nr
    return out, r


def kernel(x, lower_bounds, pre_norm_g, w_in, hgrn_norm_g, fox_f_bias, pool_w, pool_scale, w_out, post_norm_g, loss_target, m_lower_bounds, m_pre_norm_g, m_w_in, m_hgrn_norm_g, m_fox_f_bias, m_pool_w, m_pool_scale, m_w_out, m_post_norm_g, v_lower_bounds, v_pre_norm_g, v_w_in, v_hgrn_norm_g, v_fox_f_bias, v_pool_w, v_pool_scale, v_w_out, v_post_norm_g):
    cx, cy, cc = _place()
    chip = 2 * cx + cy

    halves = lambda w, l: w[l].reshape(2, w.shape[1] // 2, w.shape[2]).astype(BF16)
    needed_first = _gather_weights_beside([halves(w_in, 0)], "gather_weights_first", 4)
    started = (_adamw_start(w_in, m_w_in, v_w_in, "adamw_start_w_in"), _adamw_start(w_out, m_w_out, v_w_out, "adamw_start_w_out"))
    needed_first, started = lax.optimization_barrier((needed_first, started))
    needed_first, later = lax.optimization_barrier((needed_first, [halves(w_out, 0), halves(w_in, 1), halves(w_out, 1)]))
    later = _gather_weights_beside(later, "gather_weights_beside", 1)
    w_in_int = [_internal_from_shards([a[q].reshape(D_MODEL, SHARD_W) for q in range(N_CHIPS)]) for a in (needed_first[0], later[1])]
    w_out_full = [a.reshape(D_MODEL, D_MODEL) for a in (later[0], later[2])]

    def on_weight_grads(l, d_w_in, d_w_out):
        pin = _shards_from_internal(d_w_in).reshape(N_CHIPS, 2, D_MODEL // 2, SHARD_W)
        pout = d_w_out.reshape(N_CHIPS, 2, D_MODEL // (2 * N_CHIPS), D_MODEL)
        own = [lax.dynamic_index_in_dim(lax.dynamic_index_in_dim(p, chip, 0, False), cc, 0, False) for p in (pin, pout)]
        return own, _grad_exchange_beside(pin.astype(BF16), pout.astype(BF16), f"grad_exchange{l}", 2 + l)

    sq, grad_x, g, handed = _local_step(x, loss_target, lower_bounds, pre_norm_g, w_in_int, hgrn_norm_g, fox_f_bias,
                                        pool_w, pool_scale, w_out_full, post_norm_g, on_weight_grads)
    first = cc == 0

    def finish(l, own, landed):
        mine = [_add_n([o] + [t[s] for s in range(N_PEERS)], f"grad_sum{l}_{j}") for j, (o, t) in enumerate(zip(own, landed))]
        theirs = _swap_with_sibling(mine, f"grad_swap{l}")
        return [(jnp.where(first, h, o), jnp.where(first, o, h)) for h, o in zip(mine, theirs)]

    grad_x, last = lax.optimization_barrier((grad_x, handed[1]))
    done = [None, finish(1, *last)]

    small = {"lower_bounds": g["lbs"], "pre_norm_g": g["pre"], "hgrn_norm_g": g["hgn"], "fox_f_bias": g["bias"],
             "pool_w": g["pool_w"], "pool_scale": g["pool_scale"], "post_norm_g": g["post"]}
    packet = _pack(small)
    nrows = packet.shape[0]
    packet = packet.at[nrows - 1].set(sq[0])
    gsum = _all_reduce_small(packet)
    loss = gsum[nrows - 1, 0] * (0.5 / D_MODEL)

    weights = {"lower_bounds": lower_bounds, "pre_norm_g": pre_norm_g, "hgrn_norm_g": hgrn_norm_g,
               "fox_f_bias": fox_f_bias, "pool_w": pool_w, "pool_scale": pool_scale, "post_norm_g": post_norm_g}
    moments_m = {"lower_bounds": m_lower_bounds, "pre_norm_g": m_pre_norm_g, "hgrn_norm_g": m_hgrn_norm_g,
                 "fox_f_bias": m_fox_f_bias, "pool_w": m_pool_w, "pool_scale": m_pool_scale, "post_norm_g": m_post_norm_g}
    moments_v = {"lower_bounds": v_lower_bounds, "pre_norm_g": v_pre_norm_g, "hgrn_norm_g": v_hgrn_norm_g,
                 "fox_f_bias": v_fox_f_bias, "pool_w": v_pool_w, "pool_scale": v_pool_scale, "post_norm_g": v_post_norm_g}
    gp, dp, mp, vp = _small_update(gsum, lower_bounds, _pack(weights), _pack(moments_m), _pack(moments_v))
    gs, _ = _unpack(gp, weights)
    ds, _ = _unpack(dp, weights)
    ms, _ = _unpack(mp, weights)
    vs, _ = _unpack(vp, weights)

    first_layer, _ = lax.optimization_barrier((handed[0], (done[1], gp, dp, mp, vp)))
    done[0] = finish(0, *first_layer)
    halves_of = lambda j, side: jnp.stack([done[l][j][side] for l in range(DEPTH)])
    grad_w_in, d_in, m_in, v_in = _adamw(started[0], halves_of(0, 0), halves_of(0, 1), "adamw_w_in")
    grad_w_out, d_out, m_out, v_out = _adamw(started[1], halves_of(1, 0), halves_of(1, 1), "adamw_w_out")

    def ordered(s, big_in, big_out):
        return (s["lower_bounds"], s["pre_norm_g"], big_in, s["hgrn_norm_g"], s["fox_f_bias"], s["pool_w"],
                s["pool_scale"], big_out, s["post_norm_g"])

    return (loss, grad_x, *ordered(gs, grad_w_in, grad_w_out), *ordered(ds, d_in, d_out),
            *ordered(ms, m_in, m_out), *ordered(vs, v_in, v_out))
```

```python
import numpy as np
import jax
import jax.numpy as jnp
from jax import lax
from jax.experimental import pallas as pl
from jax.experimental.pallas import tpu as pltpu
from jax.experimental.pallas import tpu_sc as plsc

F32 = jnp.float32
BF16 = jnp.bfloat16
HI = lax.Precision.HIGHEST
MESH = pl.DeviceIdType.MESH

NORM_EPS = 1e-6
MASK_VALUE = -1e30
TINY = 1e-30
ADAM_LR, ADAM_B1, ADAM_B2, ADAM_EPS, ADAM_WD, ADAM_STEP = 0.001, 0.9, 0.999, 1e-08, 0.01, 10

D_MODEL = 1024
DEPTH = 2
N_CHIPS = 4
CHUNK = 64
LANES = 128
HGRN_W, POOL_W, FOX_W, FOX_HEADS = 256, 256, 512, 8
POOL_WINDOWS = (2, 4, 8, 16)
POOL_HALO = 16
IN_WIDTH = 3592
SHARD_W = IN_WIDTH // N_CHIPS
A_W, B_W, C_W, F_W = 1024, 512, 2048, 128
E_INT = A_W + B_W + C_W + F_W
B_BLK = A_W // 512
C_BLK0 = (A_W + B_W) // 512
F_BLK = (A_W + B_W + C_W) // 128


def _segments():
    segs = []
    for hp in range(2):
        for part in range(4):
            segs.append((part * 256 + hp * 128, 128))
    segs.append((1024, 256))
    segs.append((1280, 256))
    for hp in range(4):
        for part in range(4):
            segs.append((1536 + part * 512 + hp * 128, 128))
    segs.append((3584, 8))
    return segs


_SEGS = _segments()


def _internal_from_shards(shards):
    parts = []
    for s, n in _SEGS:
        while n > 0:
            q, r = divmod(s, SHARD_W)
            take = min(n, SHARD_W - r)
            parts.append(shards[q][..., r:r + take])
            s, n = s + take, n - take
    parts.append(jnp.zeros(shards[0].shape[:-1] + (E_INT - IN_WIDTH,), shards[0].dtype))
    return jnp.concatenate(parts, axis=-1)


def _shards_from_internal(w):
    offs, o = [], 0
    for s, n in _SEGS:
        offs.append((s, o, n))
        o += n
    blocks = []
    for q in range(N_CHIPS):
        lo, hi = SHARD_W * q, SHARD_W * (q + 1)
        parts = [w[..., o + max(lo, s) - s:o + min(hi, s + n) - s] for s, o, n in sorted(offs) if s < hi and s + n > lo]
        blocks.append(jnp.concatenate(parts, axis=-1))
    return jnp.stack(blocks)


def _cparams(sem=None, vmem_mb=48):
    kw = dict(vmem_limit_bytes=vmem_mb * 1024 * 1024)
    if sem is not None:
        kw["dimension_semantics"] = sem
    return pltpu.CompilerParams(**kw)


def _sig(x):
    return 1.0 / (1.0 + jnp.exp(-x))


def _silu(x):
    return x * _sig(x)


def _dsilu(x):
    s = _sig(x)
    return s * (1.0 + x * (1.0 - s))


def _rstd(x):
    return lax.rsqrt(jnp.mean(x * x, axis=-1, keepdims=True) + NORM_EPS)


def _dot(a, b, dims, **kw):
    return lax.dot_general(a, b, (dims, ((), ())), preferred_element_type=F32, **kw)


NN = ((1,), (0,))
NT = ((1,), (1,))
TN = ((0,), (0,))


def _iota(shape, dim):
    return lax.broadcasted_iota(jnp.int32, shape, dim)


def _lbs_fwd(lower_bounds):
    def body(a_ref, o_ref):
        a = a_ref[...]
        a0, a1 = a[0:1], a[1:2]
        m = jnp.maximum(a0, a1)
        e0, e1 = jnp.exp(a0 - m), jnp.exp(a1 - m)
        p0, p1 = e0 / (e0 + e1), e1 / (e0 + e1)
        o_ref[...] = jnp.concatenate([p0 - p0, (p0 + p1) - p0], axis=0)

    return pl.pallas_call(body, out_shape=jax.ShapeDtypeStruct(lower_bounds.shape, F32), name="lbs_fwd")(lower_bounds)


def _inproj_fwd(x2, g_row, w_int, name):
    n, d = x2.shape
    e = w_int.shape[1]
    tm = min(512, n)

    def body(x_ref, g_ref, w_ref, o_ref):
        x = x_ref[...]
        h = (x * _rstd(x) * g_ref[...]).astype(BF16)
        o_ref[...] = jnp.dot(h, w_ref[...], preferred_element_type=F32)

    return pl.pallas_call(
        body, grid=(n // tm,),
        in_specs=[pl.BlockSpec((tm, d), lambda i: (i, 0)), pl.BlockSpec((1, d), lambda i: (0, 0)),
                  pl.BlockSpec((d, e), lambda i: (0, 0))],
        out_specs=pl.BlockSpec((tm, e), lambda i: (i, 0)),
        out_shape=jax.ShapeDtypeStruct((n, e), F32),
        compiler_params=_cparams(("parallel",)), name=name)(x2, g_row, w_int)


def _hgrn_gates(a, lb):
    qa, z = a[:, 0:128], a[:, 128:256]
    sg, sgn = _sig(z), _sig(-z)
    fg = lb + (1.0 - lb) * sg
    lf = jnp.log(jnp.maximum(fg, TINY))
    kk = (1.0 - lb) * sgn
    return qa * _sig(qa), kk, lf, sg, sgn, fg


N_LEVELS = 6


def _hgrn_tables():
    t = np.arange(LANES)
    j = np.arange(LANES)[None, :]
    same_chunk = (t[:, None] // CHUNK) == (j // CHUNK)
    w = np.zeros((2 + N_LEVELS, LANES, LANES), np.float32)
    w[0] = same_chunk & (j <= t[:, None])
    w[1] = same_chunk & (j > t[:, None])
    maskf = np.zeros((N_LEVELS, LANES, LANES), np.float32)
    rightf = np.zeros((N_LEVELS, LANES, LANES), np.float32)
    for li in range(N_LEVELS):
        m = (CHUNK // 2) >> li
        start = t - (t % (2 * m))
        right = (t % (2 * m)) >= m
        first = np.where(right, start + m, t + 1)
        last = np.where(right, t, start + m - 1)
        w[2 + li] = (j >= first[:, None]) & (j <= last[:, None])
        maskf[li] = (t[:, None] // (2 * m)) == (j // (2 * m))
        rightf[li] = right[:, None]
    w = w[:-1]
    return jnp.asarray(w.reshape(-1, LANES), BF16), jnp.asarray(np.tile(maskf, (1, 2, 1))), jnp.asarray(rightf)


def _split(x, n):
    parts = []
    for _ in range(n - 1):
        p = x.astype(BF16)
        parts.append(p)
        x = x - p.astype(F32)
    parts.append(x.astype(BF16))
    return parts


def _exact_dot(w, parts):
    acc = jnp.dot(w, parts[0], preferred_element_type=F32)
    for p in parts[1:]:
        acc = acc + jnp.dot(w, p, preferred_element_type=F32)
    return acc


def _head_sums(v, ones_blk, n=2):
    parts = _split(v, n)
    acc = jnp.dot(parts[0], ones_blk, preferred_element_type=F32)
    for p in parts[1:]:
        acc = acc + jnp.dot(p, ones_blk, preferred_element_type=F32)
    return acc


def _hgrn_consts():
    r, c = _iota((LANES, LANES), 0), _iota((LANES, LANES), 1)
    ones_blk = ((r // CHUNK) == (c // CHUNK)).astype(BF16)
    eye2 = (_iota((2 * LANES, LANES), 0) % LANES) == _iota((2 * LANES, LANES), 1)
    first = _iota((1, LANES), 1) < CHUNK
    return eye2, ones_blk, jnp.ones((LANES, LANES), BF16), first


def _stack_heads(v, first):
    return jnp.concatenate([jnp.where(first, v, 0.0), jnp.where(first, 0.0, v)], axis=0)


def _pick_heads(v2, first):
    return jnp.where(first, v2[:LANES], v2[LANES:])


def _hgrn_levels(qq, kk, lf, zall, mk_ref, rt_ref, first, d_att=None):
    att = jnp.zeros((2 * LANES, LANES), F32)
    dq = dk = db = jnp.zeros((LANES, LANES), F32)
    for li in range(N_LEVELS):
        rt = rt_ref[li]
        e = jnp.exp(zall[(2 + li) * LANES:(3 + li) * LANES] if li < N_LEVELS - 1 else lf * rt)
        mk = mk_ref[li]
        qef, kef = e * rt, e * (1.0 - rt)
        qe, ke = (qq * qef).astype(BF16), (kk * kef).astype(BF16)
        qe2 = _stack_heads(qe, first)
        att = att + _dot(qe2, ke, NT) * mk
        if d_att is not None:
            dam = (d_att * mk).astype(BF16)
            dqe = _pick_heads(jnp.dot(dam, ke, preferred_element_type=F32), first)
            dke = _dot(dam, qe2, TN)
            dq = dq + dqe * qef
            dk = dk + dke * kef
            db = db + (dqe * qe.astype(F32) - dke * ke.astype(F32))
    return att, dq, dk, db


def _hgrn_fwd(proj3, lbs_row, gn_row, name):
    bsz, t, _ = proj3.shape
    nt = t // LANES
    w_all, maskf, rightf = _hgrn_tables()

    def body(a_ref, lb_ref, gn_ref, w_ref, mk_ref, rt_ref, og_ref, or_ref, st_ref):
        lb = lb_ref[...]
        gn = gn_ref[...]
        eye2, ones_blk, ones_all, first = _hgrn_consts()

        def tile(i, carry):
            r0 = pl.multiple_of(i * LANES, LANES)
            a = a_ref[pl.ds(r0, LANES), :]
            qq, kk, lf, _, _, _ = _hgrn_gates(a, lb)
            va, ga = a[:, 256:384], a[:, 384:512]
            parts = _split(lf, 3)
            zall = _exact_dot(w_ref[...], parts)
            eb, ee = jnp.exp(zall[0:LANES]), jnp.exp(zall[LANES:2 * LANES])
            vb = va.astype(BF16)
            att, _, _, _ = _hgrn_levels(qq, kk, lf, zall, mk_ref, rt_ref, first)
            diag = _head_sums(_stack_heads(qq * kk, first), ones_all)
            a2 = (att + jnp.where(eye2, diag, 0.0)).astype(BF16)
            o_in = _pick_heads(jnp.dot(a2, vb, preferred_element_type=F32), first)
            qeb, keb = (qq * eb).astype(BF16), (kk * ee).astype(BF16)
            new_s, o_heads = [], []
            for h in range(2):
                hs = slice(CHUNK * h, CHUNK * (h + 1))
                o_h = o_in[:, hs]
                st = carry[h]
                chunks = []
                for c in range(2):
                    rc = slice(CHUNK * c, CHUNK * (c + 1))
                    st_ref[h, 2 * i + c] = st
                    chunks.append(o_h[rc] + _dot(qeb[rc, hs], st.astype(BF16), NT))
                    ebl = eb[CHUNK * (c + 1) - 1:CHUNK * (c + 1), hs]
                    st = st * ebl + _dot(vb[rc, hs], keb[rc, hs], TN)
                new_s.append(st)
                o_heads.append(jnp.concatenate(chunks, axis=0))
            o = jnp.concatenate(o_heads, axis=1)
            ms = _head_sums(o * o, ones_blk) * (1.0 / CHUNK)
            or_ref[pl.ds(r0, LANES), :] = o
            og_ref[pl.ds(r0, LANES), :] = (o * lax.rsqrt(ms + NORM_EPS) * gn * _silu(ga)).astype(BF16)
            return tuple(new_s)

        zero = jnp.zeros((CHUNK, CHUNK), F32)
        per_step = 4 if nt % 4 == 0 else 2

        def step(i, carry):
            for k in range(per_step):
                carry = tile(per_step * i + k, carry)
            return carry

        lax.fori_loop(0, nt // per_step, step, (zero, zero))

    out = jax.ShapeDtypeStruct((bsz, t, HGRN_W), F32)
    row = pl.BlockSpec((1, 128), lambda b, p: (0, p))
    return pl.pallas_call(
        body, grid=(bsz, 2),
        in_specs=[pl.BlockSpec((None, t, 512), lambda b, p: (b, 0, p)), row, row,
                  pl.BlockSpec(w_all.shape, lambda b, p: (0, 0)),
                  pl.BlockSpec(maskf.shape, lambda b, p: (0, 0, 0)),
                  pl.BlockSpec(rightf.shape, lambda b, p: (0, 0, 0))],
        out_specs=[pl.BlockSpec((None, t, 128), lambda b, p: (b, 0, p)),
                   pl.BlockSpec((None, t, 128), lambda b, p: (b, 0, p)),
                   pl.BlockSpec((None, 2, t // CHUNK, CHUNK, CHUNK), lambda b, p: (b, p, 0, 0, 0))],
        out_shape=[jax.ShapeDtypeStruct((bsz, t, HGRN_W), BF16), out,
                   jax.ShapeDtypeStruct((bsz, 4, t // CHUNK, CHUNK, CHUNK), F32)],
        compiler_params=_cparams(("parallel", "parallel")), name=name)(proj3, lbs_row, gn_row, w_all, maskf, rightf)


def _hgrn_bwd(proj3, o_raw, dmixed, states, lbs_row, gn_row, name):
    bsz, t, _ = proj3.shape
    nt = t // LANES
    nchunk = t // CHUNK
    w_all, maskf, rightf = _hgrn_tables()

    def body(a_ref, or_ref, do_ref, s_sc, lb_ref, gn_ref, w_ref, mk_ref, rt_ref, da_ref, dgn_ref, dlb_ref):
        lb = lb_ref[...]
        gn = gn_ref[...]
        eye2, ones_blk, ones_all, first = _hgrn_consts()
        r_i, c_i = _iota((LANES, LANES), 0), _iota((LANES, LANES), 1)
        suffix = ((c_i >= r_i) & ((r_i // CHUNK) == (c_i // CHUNK))).astype(BF16)
        row64 = _iota((LANES, CHUNK), 0)
        zero = jnp.zeros((CHUNK, CHUNK), F32)

        def bwd_tile(k, carry):
            dst0, dst1, dgn_acc, dlb_acc = carry
            i = nt - 1 - k
            r0 = pl.multiple_of(i * LANES, LANES)
            a = a_ref[pl.ds(r0, LANES), :]
            qa, ga = a[:, 0:128], a[:, 384:512]
            qq, kk, lf, sg, sgn, fg = _hgrn_gates(a, lb)
            parts = _split(lf, 3)
            zall = _exact_dot(w_ref[...], parts)
            eb, ee = jnp.exp(zall[0:LANES]), jnp.exp(zall[LANES:2 * LANES])
            vb = a[:, 256:384].astype(BF16)
            oraw = or_ref[pl.ds(r0, LANES), :]
            dout = do_ref[pl.ds(r0, LANES), :]
            r = lax.rsqrt(_head_sums(oraw * oraw, ones_blk) * (1.0 / CHUNK) + NORM_EPS)
            xn = oraw * r
            dga = dout * (xn * gn) * _dsilu(ga)
            don = dout * _silu(ga)
            dgn_acc = dgn_acc + jnp.sum(don * xn, axis=0, keepdims=True)
            dxn = don * gn
            do = r * (dxn - xn * (_head_sums(dxn * xn, ones_blk) * (1.0 / CHUNK)))
            dob = do.astype(BF16)
            do2 = _stack_heads(dob, first)
            d_att = _dot(do2, vb, NT)
            att, dq, dk, db_lv = _hgrn_levels(qq, kk, lf, zall, mk_ref, rt_ref, first, d_att)
            a2 = att + jnp.where(eye2, _head_sums(_stack_heads(qq * kk, first), ones_all), 0.0)
            dv_in = _dot(a2.astype(BF16), do2, TN)
            ddiag = _pick_heads(_head_sums(jnp.where(eye2, d_att, 0.0), ones_all), first)
            dq_in, dk_in = dq + ddiag * kk, dk + ddiag * qq
            qe_f, ke_f = qq * eb, kk * ee
            qeb, keb = qe_f.astype(BF16), ke_f.astype(BF16)
            new_ds, dq_h, dk_h, dv_h, dbl_h = [], [], [], [], []
            for h in range(2):
                hs = slice(CHUNK * h, CHUNK * (h + 1))
                dv, dq_i, dk_i = dv_in[:, hs], dq_in[:, hs], dk_in[:, hs]
                dst = (dst0, dst1)[h]
                dq_c, dk_c, dv_c, dbl_c = [None, None], [None, None], [None, None], [None, None]
                for c in (1, 0):
                    rc = slice(CHUNK * c, CHUNK * (c + 1))
                    st_n = s_sc[h, 2 * i + c]
                    ebl = eb[CHUNK * (c + 1) - 1:CHUNK * (c + 1), hs]
                    dstb = dst.astype(BF16)
                    dv_c[c] = _dot(keb[rc, hs], dstb, NT)
                    dke = jnp.dot(vb[rc, hs], dstb, preferred_element_type=F32)
                    dqe = jnp.dot(dob[rc, hs], st_n.astype(BF16), preferred_element_type=F32)
                    dbl_c[c] = (jnp.sum(dst * st_n, axis=0, keepdims=True) * ebl
                                + jnp.sum(dke * ke_f[rc, hs], axis=0, keepdims=True))
                    dq_c[c], dk_c[c] = dqe * eb[rc, hs], dke * ee[rc, hs]
                    dst = dst * ebl + _dot(dob[rc, hs], qeb[rc, hs], TN)
                new_ds.append(dst)
                dq_x, dk_x = jnp.concatenate(dq_c, axis=0), jnp.concatenate(dk_c, axis=0)
                dq_h.append(dq_i + dq_x)
                dk_h.append(dk_i + dk_x)
                dv_h.append(dv + jnp.concatenate(dv_c, axis=0))
                dbl_h.append(qq[:, hs] * dq_x - kk[:, hs] * dk_x
                             + jnp.where(row64 == CHUNK - 1, dbl_c[0], 0.0) + jnp.where(row64 == LANES - 1, dbl_c[1], 0.0))
            dqq = jnp.concatenate(dq_h, axis=1)
            dkk = jnp.concatenate(dk_h, axis=1)
            dvv = jnp.concatenate(dv_h, axis=1)
            db = db_lv + jnp.concatenate(dbl_h, axis=1)
            dlf = _exact_dot(suffix, _split(db, 3))
            dqa = dqq * _dsilu(qa)
            dfg = jnp.where(fg > TINY, dlf / fg, 0.0)
            dz = (dfg - dkk) * (1.0 - lb) * sg * sgn
            dlb_acc = dlb_acc + jnp.sum(dfg * (1.0 - sg) - dkk * sgn, axis=0, keepdims=True)
            da_ref[pl.ds(r0, LANES), :] = jnp.concatenate([dqa, dz, dvv, dga], axis=1).astype(BF16)
            return new_ds[0], new_ds[1], dgn_acc, dlb_acc

        zrow = jnp.zeros((1, LANES), F32)
        per_step = 4 if nt % 4 == 0 else 2

        def step(k, carry):
            for r in range(per_step):
                carry = bwd_tile(per_step * k + r, carry)
            return carry

        _, _, dgn_acc, dlb_acc = lax.fori_loop(0, nt // per_step, step, (zero, zero, zrow, zrow))
        dgn_ref[...] = jnp.broadcast_to(dgn_acc, (8, LANES))
        dlb_ref[...] = jnp.broadcast_to(dlb_acc, (8, LANES))

    rows = jax.ShapeDtypeStruct((bsz, 8, HGRN_W), F32)
    row = pl.BlockSpec((1, 128), lambda b, p: (0, p))
    blk = pl.BlockSpec((None, t, 128), lambda b, p: (b, 0, p))
    return pl.pallas_call(
        body, grid=(bsz, 2),
        in_specs=[pl.BlockSpec((None, t, 512), lambda b, p: (b, 0, p)), blk, blk,
                  pl.BlockSpec((None, 2, nchunk, CHUNK, CHUNK), lambda b, p: (b, p, 0, 0, 0)), row, row,
                  pl.BlockSpec(w_all.shape, lambda b, p: (0, 0)),
                  pl.BlockSpec(maskf.shape, lambda b, p: (0, 0, 0)),
                  pl.BlockSpec(rightf.shape, lambda b, p: (0, 0, 0))],
        out_specs=[pl.BlockSpec((None, t, 512), lambda b, p: (b, 0, p)),
                   pl.BlockSpec((None, 8, 128), lambda b, p: (b, 0, p)),
                   pl.BlockSpec((None, 8, 128), lambda b, p: (b, 0, p))],
        out_shape=[jax.ShapeDtypeStruct((bsz, t, A_W), BF16), rows, rows],
        compiler_params=_cparams(("parallel", "parallel")), name=name)(
            proj3, o_raw, dmixed, states, lbs_row, gn_row, w_all, maskf, rightf)


def _pool_tt(t):
    return min(256, t)


def _window_select(s2, s4, s8, s16, lane):
    return jnp.where(lane < 64, s2, jnp.where(lane < 128, s4, jnp.where(lane < 192, s8, s16)))


def _pool_counts(t0, tt):
    lane = _iota((tt, POOL_W), 1)
    tpos = (_iota((tt, POOL_W), 0) + t0 + 1).astype(F32)
    win = jnp.where(lane < 64, 2.0, jnp.where(lane < 128, 4.0, jnp.where(lane < 192, 8.0, 16.0)))
    return 1.0 / jnp.minimum(tpos, win), lane


def _pooled_tile(upad_ref, i, tt):
    r0 = pl.multiple_of(i * tt, 8)
    cat = upad_ref[pl.ds(r0, tt + POOL_HALO), :]
    s2 = cat + pltpu.roll(cat, 1, 0)
    s4 = s2 + pltpu.roll(s2, 2, 0)
    s8 = s4 + pltpu.roll(s4, 4, 0)
    s16 = s8 + pltpu.roll(s8, 8, 0)
    inv, lane = _pool_counts(i * tt, tt)
    sel = _window_select(s2[POOL_HALO:], s4[POOL_HALO:], s8[POOL_HALO:], s16[POOL_HALO:], lane)
    return sel * inv - cat[POOL_HALO:], inv, lane


def _pool_fwd(proj3, wbd, scale_row, name):
    bsz, t, _ = proj3.shape
    tt = _pool_tt(t)

    def body(p_ref, w_ref, sc_ref, o_ref, upad):
        upad[0:POOL_HALO, :] = jnp.zeros((POOL_HALO, POOL_W), F32)
        upad[POOL_HALO:, :] = p_ref[:, 0:POOL_W]
        w = w_ref[...]
        sc = sc_ref[...]

        def tile(i, c):
            pooled, _, _ = _pooled_tile(upad, i, tt)
            r0 = pl.multiple_of(i * tt, 8)
            g = p_ref[pl.ds(r0, tt), POOL_W:2 * POOL_W]
            pre = jnp.dot(pooled.astype(BF16), w, preferred_element_type=F32)
            o_ref[pl.ds(r0, tt), :] = (pre * sc * _silu(g)).astype(BF16)
            return c

        lax.fori_loop(0, t // tt, tile, 0)

    return pl.pallas_call(
        body, grid=(bsz,),
        in_specs=[pl.BlockSpec((None, t, 512), lambda b: (b, 0, B_BLK)),
                  pl.BlockSpec((POOL_W, POOL_W), lambda b: (0, 0)),
                  pl.BlockSpec((1, POOL_W), lambda b: (0, 0))],
        out_specs=pl.BlockSpec((None, t, POOL_W), lambda b: (b, 0, 0)),
        out_shape=jax.ShapeDtypeStruct((bsz, t, POOL_W), BF16),
        scratch_shapes=[pltpu.VMEM((t + POOL_HALO, POOL_W), F32)],
        compiler_params=_cparams(("parallel",)), name=name)(proj3, wbd, scale_row)


def _pool_bwd(proj3, dmixed, wbd, scale_row, name):
    bsz, t, _ = proj3.shape
    tt = _pool_tt(t)

    def body(p_ref, do_ref, w_ref, sc_ref, db_ref, dsc_ref, dw_ref, upad, epad):
        upad[0:POOL_HALO, :] = jnp.zeros((POOL_HALO, POOL_W), F32)
        upad[POOL_HALO:, :] = p_ref[:, 0:POOL_W]
        epad[t:, :] = jnp.zeros((POOL_HALO, POOL_W), F32)
        w = w_ref[...]
        sc = sc_ref[...]

        def tile(i, carry):
            dsc_acc, dw_acc = carry
            pooled, inv, _ = _pooled_tile(upad, i, tt)
            r0 = pl.multiple_of(i * tt, 8)
            g = p_ref[pl.ds(r0, tt), POOL_W:2 * POOL_W]
            dout = do_ref[pl.ds(r0, tt), :]
            pb = pooled.astype(BF16)
            pre = jnp.dot(pb, w, preferred_element_type=F32)
            t1 = dout * _silu(g)
            dsc_acc = dsc_acc + jnp.sum(t1 * pre, axis=0, keepdims=True)
            dpre = (t1 * sc).astype(BF16)
            db_ref[pl.ds(r0, tt), POOL_W:2 * POOL_W] = (dout * pre * sc * _dsilu(g)).astype(BF16)
            dw_acc = dw_acc + _dot(pb, dpre, TN)
            dpooled = _dot(dpre, w, NT)
            epad[pl.ds(r0, tt), :] = dpooled * inv
            return dsc_acc, dw_acc

        dsc_acc, dw_acc = lax.fori_loop(0, t // tt, tile, (jnp.zeros((1, POOL_W), F32), jnp.zeros((POOL_W, POOL_W), F32)))
        dsc_ref[...] = jnp.broadcast_to(dsc_acc, (8, POOL_W))
        dw_ref[...] = dw_acc

        def tile2(i, c):
            r0 = pl.multiple_of(i * tt, 8)
            n = tt + POOL_HALO
            cat = epad[pl.ds(r0, n), :]
            s2 = cat + pltpu.roll(cat, n - 1, 0)
            s4 = s2 + pltpu.roll(s2, n - 2, 0)
            s8 = s4 + pltpu.roll(s4, n - 4, 0)
            s16 = s8 + pltpu.roll(s8, n - 8, 0)
            inv, lane = _pool_counts(i * tt, tt)
            sel = _window_select(s2[:tt], s4[:tt], s8[:tt], s16[:tt], lane)
            db_ref[pl.ds(r0, tt), 0:POOL_W] = (sel - cat[:tt] / inv).astype(BF16)
            return c

        lax.fori_loop(0, t // tt, tile2, 0)

    return pl.pallas_call(
        body, grid=(bsz,),
        in_specs=[pl.BlockSpec((None, t, 512), lambda b: (b, 0, B_BLK)),
                  pl.BlockSpec((None, t, POOL_W), lambda b: (b, 0, 1)),
                  pl.BlockSpec((POOL_W, POOL_W), lambda b: (0, 0)),
                  pl.BlockSpec((1, POOL_W), lambda b: (0, 0))],
        out_specs=[pl.BlockSpec((None, t, 512), lambda b: (b, 0, 0)),
                   pl.BlockSpec((None, 8, POOL_W), lambda b: (b, 0, 0)),
                   pl.BlockSpec((None, POOL_W, POOL_W), lambda b: (b, 0, 0))],
        out_shape=[jax.ShapeDtypeStruct((bsz, t, B_W), BF16), jax.ShapeDtypeStruct((bsz, 8, POOL_W), F32),
                   jax.ShapeDtypeStruct((bsz, POOL_W, POOL_W), F32)],
        scratch_shapes=[pltpu.VMEM((t + POOL_HALO, POOL_W), F32), pltpu.VMEM((t + POOL_HALO, POOL_W), F32)],
        compiler_params=_cparams(("parallel",)), name=name)(proj3, dmixed, wbd, scale_row)


def _head_select_rows(hp):
    r, c = _iota((8, LANES), 0), _iota((8, LANES), 1)
    return ((r < 2) & (c == 2 * hp + r)).astype(F32)


def _foxgate_fwd(proj3, bias_row, name):
    bsz, t, _ = proj3.shape
    nt = t // LANES

    def body(f_ref, b_ref, cn_ref, ct_ref):
        bias = b_ref[...]
        i, j = _iota((LANES, LANES), 0), _iota((LANES, LANES), 1)
        lower = (j <= i).astype(BF16)
        spread = (_iota((LANES, FOX_W), 0) == _iota((LANES, FOX_W), 1) // 64).astype(BF16)
        select = [_head_select_rows(hp).astype(BF16) for hp in range(4)]
        offset = jnp.zeros((1, LANES), F32)
        for k in range(nt):
            rows = slice(k * LANES, (k + 1) * LANES)
            xg = f_ref[rows, :] + bias
            lf = jnp.minimum(xg, 0.0) - jnp.log(1.0 + jnp.exp(-jnp.abs(xg)))
            c = _exact_dot(lower, _split(lf, 3)) + offset
            offset = c[LANES - 1:LANES, :]
            parts = _split(c, 3)
            cn_ref[rows, :] = _head_sums(c, spread, 3)
            for hp in range(4):
                acc = _dot(select[hp], parts[0], NT)
                for p in parts[1:]:
                    acc = acc + _dot(select[hp], p, NT)
                ct_ref[hp, :, rows] = acc

    return pl.pallas_call(
        body, grid=(bsz,),
        in_specs=[pl.BlockSpec((None, t, 128), lambda b: (b, 0, F_BLK)), pl.BlockSpec((1, 128), lambda b: (0, 0))],
        out_specs=[pl.BlockSpec((None, t, FOX_W), lambda b: (b, 0, 0)),
                   pl.BlockSpec((None, 4, 8, t), lambda b: (b, 0, 0, 0))],
        out_shape=[jax.ShapeDtypeStruct((bsz, t, FOX_W), F32), jax.ShapeDtypeStruct((bsz, 4, 8, t), F32)],
        compiler_params=_cparams(("parallel",)), name=name)(proj3, bias_row)


def _foxgate_bwd(proj3, dc_nat, bias_row, name):
    bsz, t, _ = proj3.shape
    nt = t // LANES

    def body(f_ref, dc_ref, b_ref, df_ref, dbias_ref, run_sc):
        bias = b_ref[...]
        i, j = _iota((LANES, LANES), 0), _iota((LANES, LANES), 1)
        upper = (j >= i).astype(F32)
        valid = _iota((1, LANES), 1) < FOX_HEADS
        run_sc[...] = jnp.zeros((8, LANES), F32)
        dbias_ref[...] = jnp.zeros((8, LANES), F32)

        def tile(k, c):
            r0 = pl.multiple_of((nt - 1 - k) * LANES, LANES)
            dc = dc_ref[pl.ds(r0, LANES), :] + jnp.where(i == LANES - 1, run_sc[0:1, :], 0.0)
            dlf = jnp.dot(upper, dc, precision=HI, preferred_element_type=F32)
            xg = f_ref[pl.ds(r0, LANES), :] + bias
            df = jnp.where(valid, dlf * _sig(-xg), 0.0)
            df_ref[pl.ds(r0, LANES), :] = df.astype(BF16)
            run_sc[...] = dlf[0:8, :]
            dbias_ref[...] += jnp.sum(df, axis=0, keepdims=True)
            return c

        lax.fori_loop(0, nt, tile, 0)

    blk = pl.BlockSpec((None, t, 128), lambda b: (b, 0, 0))
    return pl.pallas_call(
        body, grid=(bsz,),
        in_specs=[pl.BlockSpec((None, t, 128), lambda b: (b, 0, F_BLK)), blk, pl.BlockSpec((1, 128), lambda b: (0, 0))],
        out_specs=[blk, pl.BlockSpec((None, 8, 128), lambda b: (b, 0, 0))],
        out_shape=[jax.ShapeDtypeStruct((bsz, t, F_W), BF16), jax.ShapeDtypeStruct((bsz, 8, 128), F32)],
        scratch_shapes=[pltpu.VMEM((8, LANES), F32)],
        compiler_params=_cparams(("parallel",)), name=name)(proj3, dc_nat, bias_row)


def _fox_tile(t):
    return min(256, t)


def _fox_fwd(proj3, c_nat, c_t, name):
    bsz, t, _ = proj3.shape
    tq = tk = min(4 * _fox_tile(t), t)
    nq = t // tq

    def body(q_ref, kv_ref, cn_ref, ct_ref, og_ref, or_ref, lse_ref):
        i = pl.program_id(2)
        qblk = q_ref[...]
        first = _iota((1, 128), 1) < 64
        qv = qblk[:, 0:128] * 0.125
        qm = [jnp.where(first, qv, 0.0).astype(BF16), jnp.where(first, 0.0, qv).astype(BF16)]
        cqs = [cn_ref[:, 0:1], cn_ref[:, 64:65]]
        rows = _iota((tq, tk), 0) + i * tq

        def absorb(j, state, h, masked):
            m, acc = state
            c0 = pl.multiple_of(j * tk, tk)
            kb = kv_ref[pl.ds(c0, tk), 128:256].astype(BF16)
            vblk = kv_ref[pl.ds(c0, tk), 256:384]
            vx = (jnp.where(first, vblk, 1.0) if h == 0 else jnp.where(first, 1.0, vblk)).astype(BF16)
            s = _dot(qm[h], kb, NT) + (cqs[h] - ct_ref[h:h + 1, pl.ds(c0, tk)])
            if masked:
                s = jnp.where(rows >= _iota((tq, tk), 1) + j * tk, s, MASK_VALUE)
            m_new = jnp.maximum(m, jnp.max(s, axis=1, keepdims=True))
            p = jnp.exp(s - m_new).astype(BF16)
            return m_new, jnp.exp(m - m_new) * acc + jnp.dot(p, vx, preferred_element_type=F32)

        init = (jnp.full((tq, 1), MASK_VALUE, F32), jnp.zeros((tq, 128), F32))
        n_full = (i * tq) // tk
        done = []
        for h in range(2):
            state = lax.fori_loop(0, n_full, lambda j, state, h=h: absorb(j, state, h, False), init)
            done += list(absorb(n_full, state, h, True))
        m0, acc0, m1, acc1 = done
        l0, l1 = pltpu.roll(acc0, 64, 1), pltpu.roll(acc1, 64, 1)
        o = jnp.where(first, acc0 / l0, acc1 / l1)
        or_ref[...] = o
        og_ref[...] = (o * _silu(qblk[:, 384:512])).astype(BF16)
        lse_ref[...] = jnp.where(first, m0 + jnp.log(l0), m1 + jnp.log(l1))

    out = jax.ShapeDtypeStruct((bsz, t, FOX_W), F32)
    blk = pl.BlockSpec((None, tq, 128), lambda b, p, i: (b, i, p))
    return pl.pallas_call(
        body, grid=(bsz, 4, nq),
        in_specs=[pl.BlockSpec((None, tq, 512), lambda b, p, i: (b, i, C_BLK0 + p)),
                  pl.BlockSpec((None, t, 512), lambda b, p, i: (b, 0, C_BLK0 + p)),
                  blk,
                  pl.BlockSpec((None, None, 8, t), lambda b, p, i: (b, p, 0, 0))],
        out_specs=[blk, blk, blk],
        out_shape=[jax.ShapeDtypeStruct((bsz, t, FOX_W), BF16), out, out],
        compiler_params=_cparams(("parallel", "parallel", "arbitrary")), name=name)(proj3, proj3, c_nat, c_t)


def _fox_bwd(proj3, o_raw, dmixed, lse, c_nat, c_t, name):
    bsz, t, _ = proj3.shape
    tq = tk = min(2 * _fox_tile(t), t)
    nq = t // tq

    def body(a_ref, or_ref, do_ref, lse_ref, cn_ref, ct_ref, dc_out, dct_out, drow_out, dq_sc, do_sc, dl_sc):
        def prep(i, c):
            r0 = pl.multiple_of(i * tq, tq)
            g = a_ref[pl.ds(r0, tq), 384:512]
            dout = do_ref[pl.ds(r0, tq), :]
            o = or_ref[pl.ds(r0, tq), :]
            dc_out[pl.ds(r0, tq), 384:512] = (dout * o * _dsilu(g)).astype(BF16)
            do = dout * _silu(g)
            do_sc[pl.ds(r0, tq), :] = do
            prod = do * o
            d0 = jnp.sum(prod[:, 0:64], axis=1, keepdims=True)
            d1 = jnp.sum(prod[:, 64:128], axis=1, keepdims=True)
            dl_sc[pl.ds(r0, tq), :] = jnp.concatenate([jnp.broadcast_to(d0, (tq, 64)), jnp.broadcast_to(d1, (tq, 64))], axis=1)
            dq_sc[pl.ds(r0, tq), :] = jnp.zeros((tq, 128), F32)
            drow_out[pl.ds(r0, tq), :] = jnp.zeros((tq, 128), F32)
            return c

        lax.fori_loop(0, nq, prep, 0)
        dct_out[...] = jnp.zeros((8, t), F32)

        first = _iota((1, 128), 1) < 64

        def heads(v):
            return [jnp.where(first, v, 0.0).astype(BF16), jnp.where(first, 0.0, v).astype(BF16)]

        def kv_tile(j, c):
            c0 = pl.multiple_of(j * tk, tk)
            kb = a_ref[pl.ds(c0, tk), 128:256].astype(BF16)
            vb = a_ref[pl.ds(c0, tk), 256:384].astype(BF16)
            cks = [ct_ref[h:h + 1, pl.ds(c0, tk)] for h in range(2)]

            def q_step(i, carry, diagonal):
                dk, dv, dcol0, dcol1 = carry
                r0 = pl.multiple_of(i * tq, tq)
                causal = _iota((tq, tk), 0) + i * tq >= _iota((tq, tk), 1) + j * tk
                qv = a_ref[pl.ds(r0, tq), 0:128] * 0.125
                do = do_sc[pl.ds(r0, tq), :]
                qb, dob = qv.astype(BF16), do.astype(BF16)
                qm, dom = heads(qv), heads(do)
                full, dcols, rsums = [], [], []
                for h in range(2):
                    lse_h = lse_ref[pl.ds(r0, tq), 64 * h:64 * h + 1]
                    dl_h = dl_sc[pl.ds(r0, tq), 64 * h:64 * h + 1]
                    cq = cn_ref[pl.ds(r0, tq), 64 * h:64 * h + 1]
                    p = jnp.exp(_dot(qm[h], kb, NT) + (cq - cks[h]) - lse_h)
                    if diagonal:
                        p = jnp.where(causal, p, 0.0)
                    ds = p * (_dot(dom[h], vb, NT) - dl_h)
                    dsb = ds.astype(BF16)
                    full.append((_dot(p.astype(BF16), dob, TN), _dot(dsb, qb, TN),
                                 jnp.dot(dsb, kb, preferred_element_type=F32)))
                    dcols.append(jnp.sum(ds, axis=0, keepdims=True))
                    rsums.append(jnp.broadcast_to(jnp.sum(ds, axis=1, keepdims=True), (tq, 128)))
                dq_sc[pl.ds(r0, tq), :] += jnp.where(first, full[0][2], full[1][2]) * 0.125
                drow_out[pl.ds(r0, tq), :] += jnp.where(first, rsums[0], rsums[1])
                return (dk + jnp.where(first, full[0][1], full[1][1]), dv + jnp.where(first, full[0][0], full[1][0]),
                        dcol0 - dcols[0], dcol1 - dcols[1])

            carry = (jnp.zeros((tk, 128), F32), jnp.zeros((tk, 128), F32), jnp.zeros((1, tk), F32), jnp.zeros((1, tk), F32))
            carry = q_step(j, carry, True)
            dk, dv, dcol0, dcol1 = lax.fori_loop(j + 1, nq, lambda i, carry: q_step(i, carry, False), carry)
            dct_out[0:1, pl.ds(c0, tk)] = dcol0
            dct_out[1:2, pl.ds(c0, tk)] = dcol1
            dc_out[pl.ds(c0, tk), 128:256] = dk.astype(BF16)
            dc_out[pl.ds(c0, tk), 256:384] = dv.astype(BF16)
            return c

        lax.fori_loop(0, t // tk, kv_tile, 0)
        dc_out[:, 0:128] = dq_sc[...].astype(BF16)

    blk = pl.BlockSpec((None, t, 128), lambda b, p: (b, 0, p))
    return pl.pallas_call(
        body, grid=(bsz, 4),
        in_specs=[pl.BlockSpec((None, t, 512), lambda b, p: (b, 0, C_BLK0 + p)),
                  blk,
                  pl.BlockSpec((None, t, 128), lambda b, p: (b, 0, 4 + p)),
                  blk, blk,
                  pl.BlockSpec((None, None, 8, t), lambda b, p: (b, p, 0, 0))],
        out_specs=[pl.BlockSpec((None, t, 512), lambda b, p: (b, 0, p)),
                   pl.BlockSpec((None, None, 8, t), lambda b, p: (b, p, 0, 0)), blk],
        out_shape=[jax.ShapeDtypeStruct((bsz, t, C_W), BF16), jax.ShapeDtypeStruct((bsz, 4, 8, t), F32),
                   jax.ShapeDtypeStruct((bsz, t, FOX_W), F32)],
        scratch_shapes=[pltpu.VMEM((t, 128), F32), pltpu.VMEM((t, 128), F32), pltpu.VMEM((t, 128), F32)],
        compiler_params=_cparams(("parallel", "parallel")), name=name)(proj3, o_raw, dmixed, lse, c_nat, c_t)


def _mix_tm(n):
    return min(512, n)


def _outproj_fwd(x2, oa, ob, oc, wo, g_row, name):
    n, d = x2.shape
    tm = _mix_tm(n)

    def body(x_ref, oa_ref, ob_ref, oc_ref, w_ref, g_ref, y_ref, xo_ref):
        y = (jnp.dot(oa_ref[...].astype(BF16), w_ref[0:256, :], preferred_element_type=F32)
             + jnp.dot(ob_ref[...].astype(BF16), w_ref[256:512, :], preferred_element_type=F32)
             + jnp.dot(oc_ref[...].astype(BF16), w_ref[512:1024, :], preferred_element_type=F32))
        y_ref[...] = y
        xo_ref[...] = x_ref[...] + y * _rstd(y) * g_ref[...]

    row = lambda w: pl.BlockSpec((tm, w), lambda i: (i, 0))
    out = jax.ShapeDtypeStruct((n, d), F32)
    return pl.pallas_call(
        body, grid=(n // tm,),
        in_specs=[row(d), row(256), row(256), row(512), pl.BlockSpec((d, d), lambda i: (0, 0)),
                  pl.BlockSpec((1, d), lambda i: (0, 0))],
        out_specs=[row(d), row(d)], out_shape=[out, out],
        compiler_params=_cparams(("parallel",)), name=name)(x2, oa, ob, oc, wo, g_row)


def _outproj_fwd_loss(x2, oa, ob, oc, wo, g_row, target2, name):
    n, d = x2.shape
    tm = _mix_tm(n)

    def body(x_ref, oa_ref, ob_ref, oc_ref, w_ref, g_ref, t_ref, y_ref, dx_ref, l_ref):
        y = (jnp.dot(oa_ref[...].astype(BF16), w_ref[0:256, :], preferred_element_type=F32)
             + jnp.dot(ob_ref[...].astype(BF16), w_ref[256:512, :], preferred_element_type=F32)
             + jnp.dot(oc_ref[...].astype(BF16), w_ref[512:1024, :], preferred_element_type=F32))
        y_ref[...] = y
        err = (x_ref[...] + y * _rstd(y) * g_ref[...]) - t_ref[...]
        dx_ref[...] = err * (1.0 / d)

        @pl.when(pl.program_id(0) == 0)
        def _():
            l_ref[...] = jnp.zeros((8, 128), F32)

        l_ref[...] += jnp.sum(err * err)

    row = lambda w: pl.BlockSpec((tm, w), lambda i: (i, 0))
    out = jax.ShapeDtypeStruct((n, d), F32)
    return pl.pallas_call(
        body, grid=(n // tm,),
        in_specs=[row(d), row(256), row(256), row(512), pl.BlockSpec((d, d), lambda i: (0, 0)),
                  pl.BlockSpec((1, d), lambda i: (0, 0)), row(d)],
        out_specs=[row(d), row(d), pl.BlockSpec((8, 128), lambda i: (0, 0))],
        out_shape=[out, out, jax.ShapeDtypeStruct((8, 128), F32)],
        compiler_params=_cparams(("arbitrary",)), name=name)(x2, oa, ob, oc, wo, g_row, target2)


def _outproj_bwd(dxo, y, oa, ob, oc, wo, g_row, name):
    n, d = dxo.shape
    tm = _mix_tm(n)

    def body(dx_ref, y_ref, oa_ref, ob_ref, oc_ref, w_ref, g_ref, dm_ref, dw_ref, dg_ref):
        @pl.when(pl.program_id(0) == 0)
        def _():
            dw_ref[...] = jnp.zeros((d, d), F32)
            dg_ref[...] = jnp.zeros((8, d), F32)

        yv, dx = y_ref[...], dx_ref[...]
        r = _rstd(yv)
        yn = yv * r
        dg_ref[...] += jnp.sum(dx * yn, axis=0, keepdims=True)
        dyn = dx * g_ref[...]
        dy = (r * (dyn - yn * jnp.mean(dyn * yn, axis=-1, keepdims=True))).astype(BF16)
        dm_ref[...] = _dot(dy, w_ref[...], NT)
        dw_ref[0:256, :] += _dot(oa_ref[...].astype(BF16), dy, TN)
        dw_ref[256:512, :] += _dot(ob_ref[...].astype(BF16), dy, TN)
        dw_ref[512:1024, :] += _dot(oc_ref[...].astype(BF16), dy, TN)

    row = lambda w: pl.BlockSpec((tm, w), lambda i: (i, 0))
    fixed = lambda r, c: pl.BlockSpec((r, c), lambda i: (0, 0))
    return pl.pallas_call(
        body, grid=(n // tm,),
        in_specs=[row(d), row(d), row(256), row(256), row(512), fixed(d, d), fixed(1, d)],
        out_specs=[row(d), fixed(d, d), fixed(8, d)],
        out_shape=[jax.ShapeDtypeStruct((n, d), F32), jax.ShapeDtypeStruct((d, d), F32), jax.ShapeDtypeStruct((8, d), F32)],
        compiler_params=_cparams(("arbitrary",)), name=name)(dxo, y, oa, ob, oc, wo, g_row)


_PIECES = ((0, A_W), (A_W, B_W), (A_W + B_W, C_W), (A_W + B_W + C_W, F_W))


def _inproj_bwd_x(x2, dxo, g_row, w_int, pieces, name):
    n, d = x2.shape
    tm = min(512, n)

    def body(x_ref, dxo_ref, g_ref, w_ref, da_ref, db_ref, dc_ref, df_ref, dx_ref, dg_ref):
        @pl.when(pl.program_id(0) == 0)
        def _():
            dg_ref[...] = jnp.zeros((8, d), F32)

        dh = jnp.zeros((tm, d), F32)
        for ref, (o, w) in zip((da_ref, db_ref, dc_ref, df_ref), _PIECES):
            dh = dh + _dot(ref[...].astype(BF16), w_ref[:, o:o + w], NT)
        x = x_ref[...]
        r = _rstd(x)
        xn = x * r
        dg_ref[...] += jnp.sum(dh * xn, axis=0, keepdims=True)
        dxn = dh * g_ref[...]
        dx_ref[...] = dxo_ref[...] + r * (dxn - xn * jnp.mean(dxn * xn, axis=-1, keepdims=True))

    row = lambda w: pl.BlockSpec((tm, w), lambda i: (i, 0))
    fixed = lambda r, c: pl.BlockSpec((r, c), lambda i: (0, 0))
    return pl.pallas_call(
        body, grid=(n // tm,),
        in_specs=[row(d), row(d), fixed(1, d), fixed(d, E_INT)] + [row(w) for _, w in _PIECES],
        out_specs=[row(d), fixed(8, d)],
        out_shape=[jax.ShapeDtypeStruct((n, d), F32), jax.ShapeDtypeStruct((8, d), F32)],
        compiler_params=_cparams(("arbitrary",), vmem_mb=56), name=name)(x2, dxo, g_row, w_int, *pieces)


def _inproj_bwd_w(x2, g_row, pieces, name):
    n, d = x2.shape
    tm = min(512, n)

    def body(x_ref, g_ref, da_ref, db_ref, dc_ref, df_ref, dw_ref):
        @pl.when(pl.program_id(0) == 0)
        def _():
            dw_ref[...] = jnp.zeros((d, E_INT), F32)

        x = x_ref[...]
        h = (x * _rstd(x) * g_ref[...]).astype(BF16)
        for ref, (o, w) in zip((da_ref, db_ref, dc_ref, df_ref), _PIECES):
            dw_ref[:, o:o + w] += _dot(h, ref[...].astype(BF16), TN)

    row = lambda w: pl.BlockSpec((tm, w), lambda i: (i, 0))
    return pl.pallas_call(
        body, grid=(n // tm,),
        in_specs=[row(d), pl.BlockSpec((1, d), lambda i: (0, 0))] + [row(w) for _, w in _PIECES],
        out_specs=pl.BlockSpec((d, E_INT), lambda i: (0, 0)),
        out_shape=jax.ShapeDtypeStruct((d, E_INT), F32),
        compiler_params=_cparams(("arbitrary",), vmem_mb=56), name=name)(x2, g_row, *pieces)


def _block_diag(pool_w_l):
    z = jnp.zeros((64, 64), pool_w_l.dtype)
    return jnp.concatenate(
        [jnp.concatenate([pool_w_l[g] if c == g else z for c in range(4)], axis=1) for g in range(4)], axis=0)


def _pad_lanes(v, width=128):
    return jnp.pad(v, ((0, 0),) * (v.ndim - 1) + ((0, width - v.shape[-1]),))


def _local_step(x, target, lower_bounds, pre_norm_g, w_in_int, hgrn_norm_g, fox_f_bias, pool_w, pool_scale,
                w_out_bf, post_norm_g, on_weight_grads):
    bsz, t, d = x.shape
    n = bsz * t
    lbs = _lbs_fwd(lower_bounds)
    saved = []
    xc = x.reshape(n, d)
    for l in range(DEPTH):
        proj = _inproj_fwd(xc, pre_norm_g[l:l + 1], w_in_int[l], f"inproj_fwd{l}").reshape(bsz, t, E_INT)
        wbd = _block_diag(pool_w[l]).astype(BF16)
        bias_row = _pad_lanes(fox_f_bias[l:l + 1])
        oa, oa_raw, states = _hgrn_fwd(proj, lbs[l:l + 1], hgrn_norm_g[l:l + 1], f"hgrn_fwd{l}")
        ob = _pool_fwd(proj, wbd, pool_scale[l:l + 1], f"pool_fwd{l}")
        c_nat, c_t = _foxgate_fwd(proj, bias_row, f"foxgate_fwd{l}")
        oc, oc_raw, lse = _fox_fwd(proj, c_nat, c_t, f"fox_fwd{l}")
        mixed = (oa.reshape(n, -1), ob.reshape(n, -1), oc.reshape(n, -1))
        if l < DEPTH - 1:
            y, xn = _outproj_fwd(xc, *mixed, w_out_bf[l], post_norm_g[l:l + 1], f"outproj_fwd{l}")
        else:
            y, dx, sq = _outproj_fwd_loss(xc, *mixed, w_out_bf[l], post_norm_g[l:l + 1], target.reshape(n, d),
                                          f"outproj_fwd{l}")
        saved.append((xc, proj, wbd, bias_row, oa, oa_raw, states, ob, oc, oc_raw, lse, c_nat, c_t, y))
        xc = xn
    g = {k: [None] * DEPTH for k in ("pre", "hgn", "bias", "pool_w", "pool_scale", "post", "lbs")}
    handed = [None] * DEPTH
    for l in reversed(range(DEPTH)):
        xin, proj, wbd, bias_row, oa, oa_raw, states, ob, oc, oc_raw, lse, c_nat, c_t, y = saved[l]
        dmix, d_w_out, dpost = _outproj_bwd(dx, y, oa.reshape(n, -1), ob.reshape(n, -1), oc.reshape(n, -1),
                                            w_out_bf[l], post_norm_g[l:l + 1], f"outproj_bwd{l}")
        g["post"][l] = dpost[0]
        dmix3 = dmix.reshape(bsz, t, d)
        d_c, dct, drow = _fox_bwd(proj, oc_raw, dmix3, lse, c_nat, c_t, f"fox_bwd{l}")
        dc_nat = _pad_lanes(dct[:, :, 0:2, :].reshape(bsz, FOX_HEADS, t).transpose(0, 2, 1)
                            + drow.reshape(bsz, t, FOX_HEADS, 64)[..., 0])
        d_f, dbias = _foxgate_bwd(proj, dc_nat, bias_row, f"foxgate_bwd{l}")
        g["bias"][l] = jnp.sum(dbias[:, 0, :FOX_HEADS], axis=0)
        d_b, dscale, dwbd = _pool_bwd(proj, dmix3, wbd, pool_scale[l:l + 1], f"pool_bwd{l}")
        g["pool_scale"][l] = jnp.sum(dscale[:, 0], axis=0)
        dwbd = jnp.sum(dwbd, axis=0)
        g["pool_w"][l] = jnp.stack([dwbd[64 * k:64 * (k + 1), 64 * k:64 * (k + 1)] for k in range(4)])
        d_a, dgn, dlb = _hgrn_bwd(proj, oa_raw, dmix3, states, lbs[l:l + 1], hgrn_norm_g[l:l + 1], f"hgrn_bwd{l}")
        g["hgn"][l] = jnp.sum(dgn[:, 0], axis=0)
        g["lbs"][l] = jnp.sum(dlb[:, 0], axis=0)
        pieces = [p.reshape(n, -1) for p in (d_a, d_b, d_c, d_f)]
        handed[l] = on_weight_grads(l, _inproj_bwd_w(xin, pre_norm_g[l:l + 1], pieces, f"inproj_bwd_w{l}"), d_w_out)
        dx, dpre = _inproj_bwd_x(xin, dx, pre_norm_g[l:l + 1], w_in_int[l], pieces, f"inproj_bwd_x{l}")
        g["pre"][l] = dpre[0]
    grads = {k: jnp.stack(v) for k, v in g.items()}
    return sq, dx.reshape(bsz, t, d), grads, handed


def _place():
    return lax.axis_index("x"), lax.axis_index("y"), lax.axis_index("c")


def _other_chips(x, y):
    return [(1 - x, y), (x, 1 - y), (1 - x, 1 - y)]


_ANY = pl.BlockSpec(memory_space=pl.ANY)


def _gather_body(handshake, n_arrays):
    def body(*refs):
        srcs, dsts = refs[:n_arrays], refs[n_arrays:2 * n_arrays]
        ici_send, ici_recv, d2d_send, d2d_recv, local_sems = refs[2 * n_arrays:]
        x, y, c = _place()
        if handshake:
            barrier = pltpu.get_barrier_semaphore()
            for peer in [(px, py, c) for px, py in _other_chips(x, y)] + [(x, y, 1 - c)]:
                pl.semaphore_signal(barrier, inc=1, device_id=peer, device_id_type=MESH)
            pl.semaphore_wait(barrier, 4)
        me = 2 * x + y
        pairs = list(zip(srcs, dsts))
        order = [(k, j) for k in range(3) for j in range(n_arrays)]
        mine = [pltpu.make_async_copy(src, dst.at[me], local_sems.at[j]) for j, (src, dst) in enumerate(pairs)]
        for cp in mine:
            cp.start()
        chips = _other_chips(x, y)
        sends = [pltpu.make_async_remote_copy(
            src_ref=pairs[j][0].at[c], dst_ref=pairs[j][1].at[me, c], send_sem=ici_send.at[n], recv_sem=ici_recv.at[n],
            device_id=(chips[k][0], chips[k][1], c), device_id_type=MESH) for n, (k, j) in enumerate(order)]
        for cp in sends:
            cp.start()
        passed = [pltpu.make_async_remote_copy(
            src_ref=pairs[j][1].at[2 * chips[k][0] + chips[k][1], c], dst_ref=pairs[j][1].at[2 * chips[k][0] + chips[k][1], c],
            send_sem=d2d_send.at[n], recv_sem=d2d_recv.at[n], device_id=(x, y, 1 - c), device_id_type=MESH)
            for n, (k, j) in enumerate(order)]
        for n, (k, j) in enumerate(order):
            px, py = chips[k]
            src, dst = pairs[j]
            pltpu.make_async_remote_copy(
                src_ref=src.at[c], dst_ref=dst.at[2 * px + py, c], send_sem=ici_send.at[n], recv_sem=ici_recv.at[n],
                device_id=(px, py, c), device_id_type=MESH).wait_recv()
            passed[n].start()
        for n, (k, j) in enumerate(order):
            px, py = chips[k]
            src, dst = pairs[j]
            pltpu.make_async_remote_copy(
                src_ref=dst.at[2 * px + py, 1 - c], dst_ref=dst.at[2 * px + py, 1 - c], send_sem=d2d_send.at[n],
                recv_sem=d2d_recv.at[n], device_id=(x, y, 1 - c), device_id_type=MESH).wait_recv()
        for cp in sends + passed:
            cp.wait_send()
        for cp in mine:
            cp.wait()

    return body


def _gather_sems(n_arrays):
    return [pltpu.SemaphoreType.DMA((3 * n_arrays,))] * 4 + [pltpu.SemaphoreType.DMA((n_arrays,))]


def _gathered(a):
    return jax.ShapeDtypeStruct((N_CHIPS,) + a.shape, a.dtype)


def _gather_weights(arrays):
    n = len(arrays)
    return pl.pallas_call(
        _gather_body(False, n), in_specs=[_ANY] * n, out_specs=[_ANY] * n, out_shape=[_gathered(a) for a in arrays],
        scratch_shapes=_gather_sems(n), name="gather_weights")(*arrays)


def _gather_weights_beside(arrays):
    hbm = pltpu.MemorySpace.HBM
    n = len(arrays)
    srcs = [jax.new_ref(a, memory_space=hbm) for a in arrays]
    dsts = [jax.empty_ref(_gathered(a), memory_space=hbm) for a in arrays]
    body = _gather_body(True, n)

    @pl.kernel(mesh=plsc.ScalarSubcoreMesh(axis_name="sequencer", num_cores=1), name="gather_weights_beside",
               scratch_types=_gather_sems(n), compiler_params=pltpu.CompilerParams(collective_id=1))
    def launch(*sems):
        body(*srcs, *dsts, *sems)

    launch()
    return [d[...] for d in dsts]


def _swap_with_sibling(parts, name):
    k = len(parts)

    def body(*refs):
        src, dst = refs[:k], refs[k:2 * k]
        send_sems, recv_sems = refs[2 * k:]
        x, y, c = _place()
        cps = [pltpu.make_async_remote_copy(src_ref=src[j], dst_ref=dst[j], send_sem=send_sems.at[j], recv_sem=recv_sems.at[j],
                                            device_id=(x, y, 1 - c), device_id_type=MESH) for j in range(k)]
        for cp in cps:
            cp.start()
        for cp in cps:
            cp.wait()

    return pl.pallas_call(
        body, in_specs=[_ANY] * k, out_specs=[_ANY] * k,
        out_shape=[jax.ShapeDtypeStruct(p.shape, p.dtype) for p in parts],
        scratch_shapes=[pltpu.SemaphoreType.DMA((k,)), pltpu.SemaphoreType.DMA((k,))], name=name)(*parts)


N_PEERS = 7


def _grad_exchange_body():
    def body(pin_ref, pout_ref, lin_ref, lout_ref, send_sems, recv_sems):
        x, y, c = _place()
        barrier = pltpu.get_barrier_semaphore()
        for k in range(1, N_PEERS + 1):
            peer = (x ^ ((k >> 2) & 1), y ^ ((k >> 1) & 1), c ^ (k & 1))
            pl.semaphore_signal(barrier, inc=1, device_id=peer, device_id_type=MESH)
        pl.semaphore_wait(barrier, N_PEERS)
        me = 2 * x + y
        pairs = ((pin_ref, lin_ref), (pout_ref, lout_ref))
        cps = []
        for k, (px, py) in enumerate(_other_chips(x, y)):
            for r in range(2):
                for j, (src, dst) in enumerate(pairs):
                    cps.append(pltpu.make_async_remote_copy(
                        src_ref=src.at[2 * px + py, r], dst_ref=dst.at[2 * k + c], send_sem=send_sems.at[2 * (2 * k + r) + j],
                        recv_sem=recv_sems.at[2 * (2 * k + c) + j], device_id=(px, py, r), device_id_type=MESH))
        for j, (src, dst) in enumerate(pairs):
            cps.append(pltpu.make_async_remote_copy(
                src_ref=src.at[me, 1 - c], dst_ref=dst.at[N_PEERS - 1], send_sem=send_sems.at[2 * (N_PEERS - 1) + j],
                recv_sem=recv_sems.at[2 * (N_PEERS - 1) + j], device_id=(x, y, 1 - c), device_id_type=MESH))
        for cp in cps:
            cp.start()
        for s in range(N_PEERS):
            for j, (src, dst) in enumerate(pairs):
                pltpu.make_async_remote_copy(
                    src_ref=src.at[0, 0], dst_ref=dst.at[s], send_sem=send_sems.at[2 * s + j], recv_sem=recv_sems.at[2 * s + j],
                    device_id=(x, y, 1 - c), device_id_type=MESH).wait_recv()
        for cp in cps:
            cp.wait_send()

    return body


_EXCHANGE_SEMS = [pltpu.SemaphoreType.DMA((2 * N_PEERS,))] * 2


def _landing(p):
    return jax.ShapeDtypeStruct((N_PEERS,) + p.shape[2:], p.dtype)


def _grad_exchange_beside(pin, pout, name, collective_id):
    hbm = pltpu.MemorySpace.HBM
    pin_ref, pout_ref = jax.new_ref(pin, memory_space=hbm), jax.new_ref(pout, memory_space=hbm)
    lin_ref, lout_ref = jax.empty_ref(_landing(pin), memory_space=hbm), jax.empty_ref(_landing(pout), memory_space=hbm)
    body = _grad_exchange_body()

    @pl.kernel(mesh=plsc.ScalarSubcoreMesh(axis_name="sequencer", num_cores=1), name=name,
               scratch_types=_EXCHANGE_SEMS, compiler_params=pltpu.CompilerParams(collective_id=collective_id))
    def launch(send_sems, recv_sems):
        body(pin_ref, pout_ref, lin_ref, lout_ref, send_sems, recv_sems)

    launch()
    return lin_ref[...], lout_ref[...]


def _add_n(parts, name):
    r, c = parts[0].shape
    tr = 256 if r % 256 == 0 else r
    n = len(parts)

    def body(*refs):
        acc = refs[0][...].astype(F32)
        for ref in refs[1:n]:
            acc = acc + ref[...].astype(F32)
        refs[n][...] = acc

    blk = pl.BlockSpec((tr, c), lambda i: (i, 0))
    return pl.pallas_call(
        body, grid=(r // tr,), in_specs=[blk] * n, out_specs=blk, out_shape=jax.ShapeDtypeStruct((r, c), F32),
        compiler_params=_cparams(("parallel",)), name=name)(*parts)


def _all_reduce_small(packet):
    r, w = packet.shape

    def body(p_ref, o_ref, buf, send_sems, recv_sems):
        x, y, c = _place()
        me = 4 * x + 2 * y + c
        buf[me] = p_ref[...]
        peers = []
        for k in range(1, 8):
            fx, fy, fc = (k >> 2) & 1, (k >> 1) & 1, k & 1
            peers.append((x ^ fx, y ^ fy, c ^ fc))
        cps = [pltpu.make_async_remote_copy(src_ref=p_ref, dst_ref=buf.at[me], send_sem=send_sems.at[k], recv_sem=recv_sems.at[k],
                                            device_id=peer, device_id_type=MESH) for k, peer in enumerate(peers)]
        for cp in cps:
            cp.start()
        for k, (px, py, pc) in enumerate(peers):
            pltpu.make_async_remote_copy(src_ref=p_ref, dst_ref=buf.at[4 * px + 2 * py + pc], send_sem=send_sems.at[k],
                                         recv_sem=recv_sems.at[k], device_id=(px, py, pc), device_id_type=MESH).wait_recv()
        for cp in cps:
            cp.wait_send()
        acc = buf[0]
        for k in range(1, 8):
            acc = acc + buf[k]
        o_ref[...] = acc

    vm = pl.BlockSpec(memory_space=pltpu.VMEM)
    return pl.pallas_call(
        body, in_specs=[vm], out_specs=vm, out_shape=jax.ShapeDtypeStruct((r, w), F32),
        scratch_shapes=[pltpu.VMEM((8, r, w), F32), pltpu.SemaphoreType.DMA((7,)), pltpu.SemaphoreType.DMA((7,))],
        name="all_reduce_small")(packet)


def _adamw_math(w, g, m, v):
    m = ADAM_B1 * m + (1.0 - ADAM_B1) * g
    v = ADAM_B2 * v + (1.0 - ADAM_B2) * (g * g)
    m_hat = m / (1.0 - ADAM_B1 ** ADAM_STEP)
    v_hat = v / (1.0 - ADAM_B2 ** ADAM_STEP)
    return -ADAM_LR * (m_hat / (jnp.sqrt(v_hat) + ADAM_EPS) + ADAM_WD * w), m, v


def _adamw(w, g_lower, g_upper, m, v, name):
    nl, r, c = w.shape
    tr = 128
    per_half = r // (2 * tr)

    def body(w_ref, lo_ref, up_ref, m_ref, v_ref, g_ref, d_ref, mo_ref, vo_ref):
        g = jnp.where(pl.program_id(1) == 0, lo_ref[...], up_ref[...])
        g_ref[...] = g
        d_ref[...], mo_ref[...], vo_ref[...] = _adamw_math(w_ref[...], g, m_ref[...], v_ref[...])

    blk = pl.BlockSpec((None, tr, c), lambda l, h, i: (l, h * per_half + i, 0))
    half = pl.BlockSpec((None, tr, c), lambda l, h, i: (l, i, 0))
    out = jax.ShapeDtypeStruct(w.shape, F32)
    return pl.pallas_call(
        body, grid=(nl, 2, per_half), in_specs=[blk, half, half, blk, blk], out_specs=[blk] * 4, out_shape=[out] * 4,
        compiler_params=_cparams(("parallel", "parallel", "parallel")), name=name)(w, g_lower, g_upper, m, v)


def _small_update(gsum, lower_bounds, wpack, mpack, vpack):
    r, w = gsum.shape
    lb_rows = DEPTH * HGRN_W // 128

    def body(g_ref, a_ref, w_ref, m_ref, v_ref, go_ref, d_ref, mo_ref, vo_ref):
        a = a_ref[...]
        a0, a1 = a[0:1], a[1:2]
        mx = jnp.maximum(a0, a1)
        e0, e1 = jnp.exp(a0 - mx), jnp.exp(a1 - mx)
        p0, p1 = e0 / (e0 + e1), e1 / (e0 + e1)
        g = g_ref[...]
        half = lb_rows // 2
        dl0 = jnp.concatenate([g[k:k + 1] for k in range(half)], axis=1)
        dl1 = jnp.concatenate([g[half + k:half + k + 1] for k in range(half)], axis=1)
        dp0 = (dl0 + dl1) - (dl0 + dl1)
        dp1 = dl1
        inner = p0 * dp0 + p1 * dp1
        da0, da1 = p0 * (dp0 - inner), p1 * (dp1 - inner)
        rows = [da0[:, 128 * k:128 * (k + 1)] for k in range(half)] + [da1[:, 128 * k:128 * (k + 1)] for k in range(half)]
        gfull = jnp.concatenate(rows + [g[lb_rows:]], axis=0)
        go_ref[...] = gfull
        d_ref[...], mo_ref[...], vo_ref[...] = _adamw_math(w_ref[...], gfull, m_ref[...], v_ref[...])

    vm = pl.BlockSpec(memory_space=pltpu.VMEM)
    out = jax.ShapeDtypeStruct((r, w), F32)
    return pl.pallas_call(body, in_specs=[vm] * 5, out_specs=[vm] * 4, out_shape=[out] * 4, name="small_update")(
        gsum, lower_bounds, wpack, mpack, vpack)


_SMALL = ("lower_bounds", "pre_norm_g", "hgrn_norm_g", "fox_f_bias", "pool_w", "pool_scale", "post_norm_g")


def _pack(parts):
    rows = []
    for k in _SMALL:
        f = parts[k].reshape(-1)
        pad = (-f.shape[0]) % (8 * 128)
        rows.append(jnp.pad(f, (0, pad)).reshape(-1, 128))
    rows.append(jnp.zeros((8, 128), F32))
    return jnp.concatenate(rows, axis=0)


def _unpack(pack, like):
    out, r = {}, 0
    for k in _SMALL:
        size = int(np.prod(like[k].shape))
        nr = -(-size // (8 * 128)) * 8
        out[k] = pack[r:r + nr].reshape(-1)[:size].reshape(like[k].shape)
        r += nr
    return out, r


def kernel(x, lower_bounds, pre_norm_g, w_in, hgrn_norm_g, fox_f_bias, pool_w, pool_scale, w_out, post_norm_g, loss_target, m_lower_bounds, m_pre_norm_g, m_w_in, m_hgrn_norm_g, m_fox_f_bias, m_pool_w, m_pool_scale, m_w_out, m_post_norm_g, v_lower_bounds, v_pre_norm_g, v_w_in, v_hgrn_norm_g, v_fox_f_bias, v_pool_w, v_pool_scale, v_w_out, v_post_norm_g):
    cx, cy, cc = _place()
    chip = 2 * cx + cy

    halves = lambda w, l: w[l].reshape(2, w.shape[1] // 2, w.shape[2]).astype(BF16)
    needed_first = _gather_weights([halves(w_in, 0)])
    needed_first, later = lax.optimization_barrier((needed_first, [halves(w_out, 0), halves(w_in, 1), halves(w_out, 1)]))
    later = _gather_weights_beside(later)
    w_in_int = [_internal_from_shards([a[q].reshape(D_MODEL, SHARD_W) for q in range(N_CHIPS)]) for a in (needed_first[0], later[1])]
    w_out_full = [a.reshape(D_MODEL, D_MODEL) for a in (later[0], later[2])]

    def on_weight_grads(l, d_w_in, d_w_out):
        pin = _shards_from_internal(d_w_in).reshape(N_CHIPS, 2, D_MODEL // 2, SHARD_W)
        pout = d_w_out.reshape(N_CHIPS, 2, D_MODEL // (2 * N_CHIPS), D_MODEL)
        own = [lax.dynamic_index_in_dim(lax.dynamic_index_in_dim(p, chip, 0, False), cc, 0, False) for p in (pin, pout)]
        return own, _grad_exchange_beside(pin.astype(BF16), pout.astype(BF16), f"grad_exchange{l}", 2 + l)

    sq, grad_x, g, handed = _local_step(x, loss_target, lower_bounds, pre_norm_g, w_in_int, hgrn_norm_g, fox_f_bias,
                                        pool_w, pool_scale, w_out_full, post_norm_g, on_weight_grads)
    first = cc == 0

    def finish(l, own, landed):
        mine = [_add_n([o] + [t[s] for s in range(N_PEERS)], f"grad_sum{l}_{j}") for j, (o, t) in enumerate(zip(own, landed))]
        theirs = _swap_with_sibling(mine, f"grad_swap{l}")
        return [(jnp.where(first, h, o), jnp.where(first, o, h)) for h, o in zip(mine, theirs)]

    grad_x, last = lax.optimization_barrier((grad_x, handed[1]))
    done = [None, finish(1, *last)]

    small = {"lower_bounds": g["lbs"], "pre_norm_g": g["pre"], "hgrn_norm_g": g["hgn"], "fox_f_bias": g["bias"],
             "pool_w": g["pool_w"], "pool_scale": g["pool_scale"], "post_norm_g": g["post"]}
    packet = _pack(small)
    nrows = packet.shape[0]
    packet = packet.at[nrows - 1].set(sq[0])
    gsum = _all_reduce_small(packet)
    loss = gsum[nrows - 1, 0] * (0.5 / D_MODEL)

    weights = {"lower_bounds": lower_bounds, "pre_norm_g": pre_norm_g, "hgrn_norm_g": hgrn_norm_g,
               "fox_f_bias": fox_f_bias, "pool_w": pool_w, "pool_scale": pool_scale, "post_norm_g": post_norm_g}
    moments_m = {"lower_bounds": m_lower_bounds, "pre_norm_g": m_pre_norm_g, "hgrn_norm_g": m_hgrn_norm_g,
                 "fox_f_bias": m_fox_f_bias, "pool_w": m_pool_w, "pool_scale": m_pool_scale, "post_norm_g": m_post_norm_g}
    moments_v = {"lower_bounds": v_lower_bounds, "pre_norm_g": v_pre_norm_g, "hgrn_norm_g": v_hgrn_norm_g,
                 "fox_f_bias": v_fox_f_bias, "pool_w": v_pool_w, "pool_scale": v_pool_scale, "post_norm_g": v_post_norm_g}
    gp, dp, mp, vp = _small_update(gsum, lower_bounds, _pack(weights), _pack(moments_m), _pack(moments_v))
    gs, _ = _unpack(gp, weights)
    ds, _ = _unpack(dp, weights)
    ms, _ = _unpack(mp, weights)
    vs, _ = _unpack(vp, weights)

    first_layer, _ = lax.optimization_barrier((handed[0], (done[1], gp, dp, mp, vp)))
    done[0] = finish(0, *first_layer)
    halves_of = lambda j, side: jnp.stack([done[l][j][side] for l in range(DEPTH)])
    grad_w_in, d_in, m_in, v_in = _adamw(w_in, halves_of(0, 0), halves_of(0, 1), m_w_in, v_w_in, "adamw_w_in")
    grad_w_out, d_out, m_out, v_out = _adamw(w_out, halves_of(1, 0), halves_of(1, 1), m_w_out, v_w_out, "adamw_w_out")

    def ordered(s, big_in, big_out):
        return (s["lower_bounds"], s["pre_norm_g"], big_in, s["hgrn_norm_g"], s["fox_f_bias"], s["pool_w"],
                s["pool_scale"], big_out, s["post_norm_g"])

    return (loss, grad_x, *ordered(gs, grad_w_in, grad_w_out), *ordered(ds, d_in, d_out),
            *ordered(ms, m_in, m_out), *ordered(vs, v_in, v_out))
```
